```python
import jax, jax.numpy as jnp
from jax import lax
import numpy as np

D_MODEL = 1024
BATCH = 8
SEQ = 4096
DEPTH = 1

MIX_WIDTH = D_MODEL
ATTN_WIDTH = MIX_WIDTH // 2
SGU_WIDTH = MIX_WIDTH - ATTN_WIDTH
HEAD_DIM = 64
N_Q_HEADS = ATTN_WIDTH // HEAD_DIM
N_KV_HEADS = 2
Q_PER_KV = N_Q_HEADS // N_KV_HEADS
KV_WIDTH = N_KV_HEADS * HEAD_DIM
WINDOW = 128
BLOCK = 128
ROPE_THETA = 500000.0
ROT_DIM = HEAD_DIM // 4
SGU_CHUNK = 128
N_SGU_GROUPS = 4
SGU_GROUP_DIM = SGU_WIDTH // N_SGU_GROUPS
D_FF = -(-8 * D_MODEL // (3 * 256)) * 256
IN_WIDTH = ATTN_WIDTH + 2 * KV_WIDTH + 2 * SGU_WIDTH
LN_EPS = 1e-5
ALPHA = (2.0 * DEPTH) ** 0.25
BETA = (8.0 * DEPTH) ** -0.25

kernel_name = "hybrid_swa_sink_gmlp_deepnorm_block"


def layer_norm(x, g, b):
    x32 = x.astype(jnp.float32)
    mu = jnp.mean(x32, axis=-1, keepdims=True)
    var = jnp.mean(jnp.square(x32 - mu), axis=-1, keepdims=True)
    y = (x32 - mu) * lax.rsqrt(var + LN_EPS) * g.astype(jnp.float32) + b.astype(jnp.float32)
    return y.astype(x.dtype)


def rope_tables(positions):
    inv_freq = ROPE_THETA ** (-jnp.arange(0, ROT_DIM, 2, dtype=jnp.float32) / ROT_DIM)
    ang = positions.astype(jnp.float32)[..., None] * inv_freq
    return jnp.cos(ang)[:, :, None, :], jnp.sin(ang)[:, :, None, :]


def apply_partial_rope(t, cos, sin):
    half = ROT_DIM // 2
    t1 = t[..., :half].astype(jnp.float32)
    t2 = t[..., half:ROT_DIM].astype(jnp.float32)
    rot = jnp.concatenate([t1 * cos - t2 * sin, t2 * cos + t1 * sin], axis=-1).astype(t.dtype)
    return jnp.concatenate([rot, t[..., ROT_DIM:]], axis=-1)


def sliding_window_attention(q, k, v, sinks):
    b, s = q.shape[0], q.shape[1]
    nb = s // BLOCK
    qb = q.reshape(b, nb, BLOCK, N_KV_HEADS, Q_PER_KV, HEAD_DIM)

    def band(t):
        cur = t.reshape(b, nb, BLOCK, N_KV_HEADS, HEAD_DIM)
        prev = jnp.pad(cur, ((0, 0), (1, 0), (0, 0), (0, 0), (0, 0)))[:, :-1]
        return jnp.concatenate([prev, cur], axis=2)

    kb, vb = band(k), band(v)
    scores = jnp.einsum('bnqkgd,bnskd->bnkgqs', qb, kb).astype(jnp.float32) * (HEAD_DIM ** -0.5)
    qi = jnp.arange(BLOCK)[:, None]
    kj = jnp.arange(2 * BLOCK)[None, :]
    rel = qi + BLOCK - kj
    kpos = jnp.arange(nb)[:, None, None] * BLOCK + kj - BLOCK
    allowed = (rel >= 0) & (rel < WINDOW) & (kpos >= 0)
    scores = jnp.where(allowed[None, :, None, None, :, :], scores, -1e30)
    sink = jnp.broadcast_to(
        sinks.astype(jnp.float32).reshape(N_KV_HEADS, Q_PER_KV)[None, None, :, :, None, None],
        scores.shape[:-1] + (1,))
    probs = jax.nn.softmax(jnp.concatenate([scores, sink], axis=-1), axis=-1)[..., :-1]
    out = jnp.einsum('bnkgqs,bnskd->bnqkgd', probs.astype(v.dtype), vb)
    return out.reshape(b, s, ATTN_WIDTH)


def spatial_gating(su, sv, ln_g, ln_b, w_s, b_s):
    b, s = su.shape[0], su.shape[1]
    nc = s // SGU_CHUNK
    u = jax.nn.gelu(su, approximate=False)
    vv = layer_norm(jax.nn.gelu(sv, approximate=False), ln_g, ln_b)
    vv = vv.reshape(b, nc, SGU_CHUNK, N_SGU_GROUPS, SGU_GROUP_DIM)
    causal = jnp.tril(jnp.ones((SGU_CHUNK, SGU_CHUNK), dtype=w_s.dtype))
    mixed = jnp.einsum('hts,bcshd->bcthd', w_s * causal, vv) + b_s.T[:, :, None]
    return u * mixed.reshape(b, s, SGU_WIDTH)


def swiglu(h, w_gate, w_up, w_down):
    return (jax.nn.silu(h @ w_gate) * (h @ w_up)) @ w_down


def _fwd_setup_inputs(seed: int = 0) -> dict:
    key = jax.random.key(seed)
    ks = jax.random.split(key, 24)
    f32 = jnp.float32
    L = DEPTH

    def nrm(k, shape, scale):
        return jax.random.normal(k, shape, f32) * scale

    x = jax.random.normal(ks[0], (BATCH, SEQ, D_MODEL), f32)
    positions = jnp.broadcast_to(jnp.arange(SEQ, dtype=jnp.int32)[None, :], (BATCH, SEQ))
    col_scale = jnp.concatenate([
        jnp.ones((ATTN_WIDTH + KV_WIDTH,), f32),
        jnp.full((KV_WIDTH,), BETA, f32),
        jnp.full((2 * SGU_WIDTH,), BETA, f32)])
    w_in = nrm(ks[1], (L, D_MODEL, IN_WIDTH), D_MODEL ** -0.5) * col_scale
    return {
        "x": x,
        "positions": positions,
        "ln_in_g": 1.0 + nrm(ks[2], (D_MODEL,), 0.02),
        "ln_in_b": nrm(ks[3], (D_MODEL,), 0.02),
        "w_in": w_in,
        "b_in": nrm(ks[4], (L, IN_WIDTH), 0.02),
        "attn_sinks": nrm(ks[5], (L, N_Q_HEADS), 0.5),
        "sgu_ln_g": 1.0 + nrm(ks[6], (L, SGU_WIDTH), 0.02),
        "sgu_ln_b": nrm(ks[7], (L, SGU_WIDTH), 0.02),
        "sgu_w": nrm(ks[8], (L, N_SGU_GROUPS, SGU_CHUNK, SGU_CHUNK), 0.5 * SGU_CHUNK ** -0.5),
        "sgu_b": 1.0 + nrm(ks[9], (L, N_SGU_GROUPS, SGU_CHUNK), 0.1),
        "w_out": nrm(ks[10], (L, MIX_WIDTH, D_MODEL), BETA * MIX_WIDTH ** -0.5),
        "b_out": nrm(ks[11], (L, D_MODEL), 0.02),
        "ln_mix_g": 1.0 + nrm(ks[12], (L, D_MODEL), 0.02),
        "ln_mix_b": nrm(ks[13], (L, D_MODEL), 0.02),
        "w_gate": nrm(ks[14], (L, D_MODEL, D_FF), BETA * D_MODEL ** -0.5),
        "w_up": nrm(ks[15], (L, D_MODEL, D_FF), BETA * D_MODEL ** -0.5),
        "w_down": nrm(ks[16], (L, D_FF, D_MODEL), BETA * D_FF ** -0.5),
        "ln_ffn_g": 1.0 + nrm(ks[17], (L, D_MODEL), 0.02),
        "ln_ffn_b": nrm(ks[18], (L, D_MODEL), 0.02),
    }


def _fwd_reference(x, positions, ln_in_g, ln_in_b, w_in, b_in, attn_sinks, sgu_ln_g, sgu_ln_b,
              sgu_w, sgu_b, w_out, b_out, ln_mix_g, ln_mix_b, w_gate, w_up, w_down,
              ln_ffn_g, ln_ffn_b):
    b, s, _ = x.shape
    cos, sin = rope_tables(positions)
    h = layer_norm(x, ln_in_g, ln_in_b)
    splits = [ATTN_WIDTH, ATTN_WIDTH + KV_WIDTH, ATTN_WIDTH + 2 * KV_WIDTH,
              ATTN_WIDTH + 2 * KV_WIDTH + SGU_WIDTH]
    for l in range(DEPTH):
        proj = h @ w_in[l] + b_in[l]
        q, k, v, su, sv = jnp.split(proj, splits, axis=-1)
        q = apply_partial_rope(q.reshape(b, s, N_Q_HEADS, HEAD_DIM), cos, sin)
        k = apply_partial_rope(k.reshape(b, s, N_KV_HEADS, HEAD_DIM), cos, sin)
        v = v.reshape(b, s, N_KV_HEADS, HEAD_DIM)
        attn = sliding_window_attention(q, k, v, attn_sinks[l])
        sgu = spatial_gating(su, sv, sgu_ln_g[l], sgu_ln_b[l], sgu_w[l], sgu_b[l])
        mix = jnp.concatenate([attn, sgu], axis=-1) @ w_out[l] + b_out[l]
        h = layer_norm(ALPHA * h + mix, ln_mix_g[l], ln_mix_b[l])
        ffn = swiglu(h, w_gate[l], w_up[l], w_down[l])
        h = layer_norm(ALPHA * h + ffn, ln_ffn_g[l], ln_ffn_b[l])
    return h


import jax as _jax
import jax.numpy as _jnp

TWIN_FORMAT = 'train_step'
FWD_PARAMS = ['x', 'positions', 'ln_in_g', 'ln_in_b', 'w_in', 'b_in', 'attn_sinks', 'sgu_ln_g', 'sgu_ln_b', 'sgu_w', 'sgu_b', 'w_out', 'b_out', 'ln_mix_g', 'ln_mix_b', 'w_gate', 'w_up', 'w_down', 'ln_ffn_g', 'ln_ffn_b']
TWIN_WEIGHTS = ['ln_in_g', 'ln_in_b', 'w_in', 'b_in', 'attn_sinks', 'sgu_ln_g', 'sgu_ln_b', 'sgu_w', 'sgu_b', 'w_out', 'b_out', 'ln_mix_g', 'ln_mix_b', 'w_gate', 'w_up', 'w_down', 'ln_ffn_g', 'ln_ffn_b']
TWIN_DIFF_INPUT = 'x'
TWIN_INPUTS = ['x', 'positions', 'ln_in_g', 'ln_in_b', 'w_in', 'b_in', 'attn_sinks', 'sgu_ln_g', 'sgu_ln_b', 'sgu_w', 'sgu_b', 'w_out', 'b_out', 'ln_mix_g', 'ln_mix_b', 'w_gate', 'w_up', 'w_down', 'ln_ffn_g', 'ln_ffn_b', 'loss_target', 'm_ln_in_g', 'm_ln_in_b', 'm_w_in', 'm_b_in', 'm_attn_sinks', 'm_sgu_ln_g', 'm_sgu_ln_b', 'm_sgu_w', 'm_sgu_b', 'm_w_out', 'm_b_out', 'm_ln_mix_g', 'm_ln_mix_b', 'm_w_gate', 'm_w_up', 'm_w_down', 'm_ln_ffn_g', 'm_ln_ffn_b', 'v_ln_in_g', 'v_ln_in_b', 'v_w_in', 'v_b_in', 'v_attn_sinks', 'v_sgu_ln_g', 'v_sgu_ln_b', 'v_sgu_w', 'v_sgu_b', 'v_w_out', 'v_b_out', 'v_ln_mix_g', 'v_ln_mix_b', 'v_w_gate', 'v_w_up', 'v_w_down', 'v_ln_ffn_g', 'v_ln_ffn_b']
TWIN_OUTPUTS = ['loss', 'grad_x', 'grad_ln_in_g', 'grad_ln_in_b', 'grad_w_in', 'grad_b_in', 'grad_attn_sinks', 'grad_sgu_ln_g', 'grad_sgu_ln_b', 'grad_sgu_w', 'grad_sgu_b', 'grad_w_out', 'grad_b_out', 'grad_ln_mix_g', 'grad_ln_mix_b', 'grad_w_gate', 'grad_w_up', 'grad_w_down', 'grad_ln_ffn_g', 'grad_ln_ffn_b', 'delta_ln_in_g', 'delta_ln_in_b', 'delta_w_in', 'delta_b_in', 'delta_attn_sinks', 'delta_sgu_ln_g', 'delta_sgu_ln_b', 'delta_sgu_w', 'delta_sgu_b', 'delta_w_out', 'delta_b_out', 'delta_ln_mix_g', 'delta_ln_mix_b', 'delta_w_gate', 'delta_w_up', 'delta_w_down', 'delta_ln_ffn_g', 'delta_ln_ffn_b', 'new_m_ln_in_g', 'new_m_ln_in_b', 'new_m_w_in', 'new_m_b_in', 'new_m_attn_sinks', 'new_m_sgu_ln_g', 'new_m_sgu_ln_b', 'new_m_sgu_w', 'new_m_sgu_b', 'new_m_w_out', 'new_m_b_out', 'new_m_ln_mix_g', 'new_m_ln_mix_b', 'new_m_w_gate', 'new_m_w_up', 'new_m_w_down', 'new_m_ln_ffn_g', 'new_m_ln_ffn_b', 'new_v_ln_in_g', 'new_v_ln_in_b', 'new_v_w_in', 'new_v_b_in', 'new_v_attn_sinks', 'new_v_sgu_ln_g', 'new_v_sgu_ln_b', 'new_v_sgu_w', 'new_v_sgu_b', 'new_v_w_out', 'new_v_b_out', 'new_v_ln_mix_g', 'new_v_ln_mix_b', 'new_v_w_gate', 'new_v_w_up', 'new_v_w_down', 'new_v_ln_ffn_g', 'new_v_ln_ffn_b']
TWIN_LEAF_KINDS = {'loss': 'loss', 'grad_x': 'grad_x', 'grad_ln_in_g': 'grad_w', 'grad_ln_in_b': 'grad_w', 'grad_w_in': 'grad_w', 'grad_b_in': 'grad_w', 'grad_attn_sinks': 'grad_w', 'grad_sgu_ln_g': 'grad_w', 'grad_sgu_ln_b': 'grad_w', 'grad_sgu_w': 'grad_w', 'grad_sgu_b': 'grad_w', 'grad_w_out': 'grad_w', 'grad_b_out': 'grad_w', 'grad_ln_mix_g': 'grad_w', 'grad_ln_mix_b': 'grad_w', 'grad_w_gate': 'grad_w', 'grad_w_up': 'grad_w', 'grad_w_down': 'grad_w', 'grad_ln_ffn_g': 'grad_w', 'grad_ln_ffn_b': 'grad_w', 'delta_ln_in_g': 'delta_w', 'delta_ln_in_b': 'delta_w', 'delta_w_in': 'delta_w', 'delta_b_in': 'delta_w', 'delta_attn_sinks': 'delta_w', 'delta_sgu_ln_g': 'delta_w', 'delta_sgu_ln_b': 'delta_w', 'delta_sgu_w': 'delta_w', 'delta_sgu_b': 'delta_w', 'delta_w_out': 'delta_w', 'delta_b_out': 'delta_w', 'delta_ln_mix_g': 'delta_w', 'delta_ln_mix_b': 'delta_w', 'delta_w_gate': 'delta_w', 'delta_w_up': 'delta_w', 'delta_w_down': 'delta_w', 'delta_ln_ffn_g': 'delta_w', 'delta_ln_ffn_b': 'delta_w', 'new_m_ln_in_g': 'new_m', 'new_m_ln_in_b': 'new_m', 'new_m_w_in': 'new_m', 'new_m_b_in': 'new_m', 'new_m_attn_sinks': 'new_m', 'new_m_sgu_ln_g': 'new_m', 'new_m_sgu_ln_b': 'new_m', 'new_m_sgu_w': 'new_m', 'new_m_sgu_b': 'new_m', 'new_m_w_out': 'new_m', 'new_m_b_out': 'new_m', 'new_m_ln_mix_g': 'new_m', 'new_m_ln_mix_b': 'new_m', 'new_m_w_gate': 'new_m', 'new_m_w_up': 'new_m', 'new_m_w_down': 'new_m', 'new_m_ln_ffn_g': 'new_m', 'new_m_ln_ffn_b': 'new_m', 'new_v_ln_in_g': 'new_v', 'new_v_ln_in_b': 'new_v', 'new_v_w_in': 'new_v', 'new_v_b_in': 'new_v', 'new_v_attn_sinks': 'new_v', 'new_v_sgu_ln_g': 'new_v', 'new_v_sgu_ln_b': 'new_v', 'new_v_sgu_w': 'new_v', 'new_v_sgu_b': 'new_v', 'new_v_w_out': 'new_v', 'new_v_b_out': 'new_v', 'new_v_ln_mix_g': 'new_v', 'new_v_ln_mix_b': 'new_v', 'new_v_w_gate': 'new_v', 'new_v_w_up': 'new_v', 'new_v_w_down': 'new_v', 'new_v_ln_ffn_g': 'new_v', 'new_v_ln_ffn_b': 'new_v'}


def _forward(args):
    return _fwd_reference(*[args[k] for k in FWD_PARAMS])


def _output_shape():
    out = _jax.eval_shape(lambda: _forward(_fwd_setup_inputs(0)))
    return out.shape, out.dtype

N_MICROBATCH = 1
ADAM_LR = 0.001
ADAM_B1 = 0.9
ADAM_B2 = 0.999
ADAM_EPS = 1e-08
ADAM_WD = 0.01
ADAM_STEP = 10
PER_EXAMPLE_BATCH_AXIS = {'x': 0, 'positions': 0, 'loss_target': 0}
SHARED_INPUTS = []
_WEIGHT_DTYPES = {'ln_in_g': _jnp.float32, 'ln_in_b': _jnp.float32, 'w_in': _jnp.float32, 'b_in': _jnp.float32, 'attn_sinks': _jnp.float32, 'sgu_ln_g': _jnp.float32, 'sgu_ln_b': _jnp.float32, 'sgu_w': _jnp.float32, 'sgu_b': _jnp.float32, 'w_out': _jnp.float32, 'b_out': _jnp.float32, 'ln_mix_g': _jnp.float32, 'ln_mix_b': _jnp.float32, 'w_gate': _jnp.float32, 'w_up': _jnp.float32, 'w_down': _jnp.float32, 'ln_ffn_g': _jnp.float32, 'ln_ffn_b': _jnp.float32}
MOMENT_SCALE = {'ln_in_g': 1.002464e+00, 'ln_in_b': 4.505807e-01, 'w_in': 3.497551e-02, 'b_in': 1.237516e-01, 'attn_sinks': 7.098277e-03, 'sgu_ln_g': 1.184411e-02, 'sgu_ln_b': 1.118576e-02, 'sgu_w': 2.229595e-02, 'sgu_b': 3.479577e-02, 'w_out': 4.674118e-02, 'b_out': 3.789091e-01, 'ln_mix_g': 1.023053e+00, 'ln_mix_b': 4.543835e-01, 'w_gate': 1.784796e-02, 'w_up': 1.743101e-02, 'w_down': 2.897112e-02, 'ln_ffn_g': 3.204657e+01, 'ln_ffn_b': 1.472476e+00}


def _to_microbatches(a, axis):
    t = _jnp.moveaxis(a, axis, 0)
    t = t.reshape((N_MICROBATCH, t.shape[0] // N_MICROBATCH) + t.shape[1:])
    return _jnp.moveaxis(t, 1, axis + 1)


def setup_inputs(seed: int = 0) -> dict:
    inp = _fwd_setup_inputs(seed)
    key = _jax.random.fold_in(_jax.random.key(seed), 7919)
    shape, _ = _output_shape()
    out = dict(inp)
    out["loss_target"] = _jax.random.normal(_jax.random.fold_in(key, 0), shape, _jnp.float32)
    for i, name in enumerate(TWIN_WEIGHTS):
        w = inp[name].astype(_jnp.float32)
        if MOMENT_SCALE is None:
            s = _jnp.sqrt(_jnp.mean(_jnp.square(w)) + 1e-30)
        else:
            s = MOMENT_SCALE[name]
        km, kv = _jax.random.split(_jax.random.fold_in(key, i + 1))
        out[name] = w
        out["m_" + name] = s * _jax.random.normal(km, w.shape, _jnp.float32)
        out["v_" + name] = (s * s) * _jax.random.uniform(kv, w.shape, _jnp.float32, 0.5, 1.5)
    if N_MICROBATCH > 1:
        for name, axis in PER_EXAMPLE_BATCH_AXIS.items():
            out[name] = _to_microbatches(out[name], axis)
    return {'x': out['x'], 'positions': out['positions'], 'ln_in_g': out['ln_in_g'], 'ln_in_b': out['ln_in_b'], 'w_in': out['w_in'], 'b_in': out['b_in'], 'attn_sinks': out['attn_sinks'], 'sgu_ln_g': out['sgu_ln_g'], 'sgu_ln_b': out['sgu_ln_b'], 'sgu_w': out['sgu_w'], 'sgu_b': out['sgu_b'], 'w_out': out['w_out'], 'b_out': out['b_out'], 'ln_mix_g': out['ln_mix_g'], 'ln_mix_b': out['ln_mix_b'], 'w_gate': out['w_gate'], 'w_up': out['w_up'], 'w_down': out['w_down'], 'ln_ffn_g': out['ln_ffn_g'], 'ln_ffn_b': out['ln_ffn_b'], 'loss_target': out['loss_target'], 'm_ln_in_g': out['m_ln_in_g'], 'm_ln_in_b': out['m_ln_in_b'], 'm_w_in': out['m_w_in'], 'm_b_in': out['m_b_in'], 'm_attn_sinks': out['m_attn_sinks'], 'm_sgu_ln_g': out['m_sgu_ln_g'], 'm_sgu_ln_b': out['m_sgu_ln_b'], 'm_sgu_w': out['m_sgu_w'], 'm_sgu_b': out['m_sgu_b'], 'm_w_out': out['m_w_out'], 'm_b_out': out['m_b_out'], 'm_ln_mix_g': out['m_ln_mix_g'], 'm_ln_mix_b': out['m_ln_mix_b'], 'm_w_gate': out['m_w_gate'], 'm_w_up': out['m_w_up'], 'm_w_down': out['m_w_down'], 'm_ln_ffn_g': out['m_ln_ffn_g'], 'm_ln_ffn_b': out['m_ln_ffn_b'], 'v_ln_in_g': out['v_ln_in_g'], 'v_ln_in_b': out['v_ln_in_b'], 'v_w_in': out['v_w_in'], 'v_b_in': out['v_b_in'], 'v_attn_sinks': out['v_attn_sinks'], 'v_sgu_ln_g': out['v_sgu_ln_g'], 'v_sgu_ln_b': out['v_sgu_ln_b'], 'v_sgu_w': out['v_sgu_w'], 'v_sgu_b': out['v_sgu_b'], 'v_w_out': out['v_w_out'], 'v_b_out': out['v_b_out'], 'v_ln_mix_g': out['v_ln_mix_g'], 'v_ln_mix_b': out['v_ln_mix_b'], 'v_w_gate': out['v_w_gate'], 'v_w_up': out['v_w_up'], 'v_w_down': out['v_w_down'], 'v_ln_ffn_g': out['v_ln_ffn_g'], 'v_ln_ffn_b': out['v_ln_ffn_b']}


def _loss(weights, diff, rest, loss_target):
    with _jax.named_scope("forward"):
        args = {**rest, TWIN_DIFF_INPUT: diff, **{k: w.astype(_WEIGHT_DTYPES[k]) for k, w in weights.items()}}
        y = _forward(args)
    with _jax.named_scope("loss_head"):
        err = _jnp.square(y.astype(_jnp.float32) - loss_target)
        return 0.5 * _jnp.sum(_jnp.mean(err, axis=-1)) if err.ndim else 0.5 * err


def _adamw(w, g, m, v):
    m = ADAM_B1 * m + (1.0 - ADAM_B1) * g
    v = ADAM_B2 * v + (1.0 - ADAM_B2) * _jnp.square(g)
    m_hat = m / (1.0 - ADAM_B1 ** ADAM_STEP)
    v_hat = v / (1.0 - ADAM_B2 ** ADAM_STEP)
    delta = -ADAM_LR * (m_hat / (_jnp.sqrt(v_hat) + ADAM_EPS) + ADAM_WD * w)
    return delta, m, v


def reference(x, positions, ln_in_g, ln_in_b, w_in, b_in, attn_sinks, sgu_ln_g, sgu_ln_b, sgu_w, sgu_b, w_out, b_out, ln_mix_g, ln_mix_b, w_gate, w_up, w_down, ln_ffn_g, ln_ffn_b, loss_target, m_ln_in_g, m_ln_in_b, m_w_in, m_b_in, m_attn_sinks, m_sgu_ln_g, m_sgu_ln_b, m_sgu_w, m_sgu_b, m_w_out, m_b_out, m_ln_mix_g, m_ln_mix_b, m_w_gate, m_w_up, m_w_down, m_ln_ffn_g, m_ln_ffn_b, v_ln_in_g, v_ln_in_b, v_w_in, v_b_in, v_attn_sinks, v_sgu_ln_g, v_sgu_ln_b, v_sgu_w, v_sgu_b, v_w_out, v_b_out, v_ln_mix_g, v_ln_mix_b, v_w_gate, v_w_up, v_w_down, v_ln_ffn_g, v_ln_ffn_b):
    given = dict(x=x, positions=positions, ln_in_g=ln_in_g, ln_in_b=ln_in_b, w_in=w_in, b_in=b_in, attn_sinks=attn_sinks, sgu_ln_g=sgu_ln_g, sgu_ln_b=sgu_ln_b, sgu_w=sgu_w, sgu_b=sgu_b, w_out=w_out, b_out=b_out, ln_mix_g=ln_mix_g, ln_mix_b=ln_mix_b, w_gate=w_gate, w_up=w_up, w_down=w_down, ln_ffn_g=ln_ffn_g, ln_ffn_b=ln_ffn_b, loss_target=loss_target, m_ln_in_g=m_ln_in_g, m_ln_in_b=m_ln_in_b, m_w_in=m_w_in, m_b_in=m_b_in, m_attn_sinks=m_attn_sinks, m_sgu_ln_g=m_sgu_ln_g, m_sgu_ln_b=m_sgu_ln_b, m_sgu_w=m_sgu_w, m_sgu_b=m_sgu_b, m_w_out=m_w_out, m_b_out=m_b_out, m_ln_mix_g=m_ln_mix_g, m_ln_mix_b=m_ln_mix_b, m_w_gate=m_w_gate, m_w_up=m_w_up, m_w_down=m_w_down, m_ln_ffn_g=m_ln_ffn_g, m_ln_ffn_b=m_ln_ffn_b, v_ln_in_g=v_ln_in_g, v_ln_in_b=v_ln_in_b, v_w_in=v_w_in, v_b_in=v_b_in, v_attn_sinks=v_attn_sinks, v_sgu_ln_g=v_sgu_ln_g, v_sgu_ln_b=v_sgu_ln_b, v_sgu_w=v_sgu_w, v_sgu_b=v_sgu_b, v_w_out=v_w_out, v_b_out=v_b_out, v_ln_mix_g=v_ln_mix_g, v_ln_mix_b=v_ln_mix_b, v_w_gate=v_w_gate, v_w_up=v_w_up, v_w_down=v_w_down, v_ln_ffn_g=v_ln_ffn_g, v_ln_ffn_b=v_ln_ffn_b)
    weights = {n: given[n] for n in TWIN_WEIGHTS}
    shared = {n: given[n] for n in SHARED_INPUTS}
    per_example = {n: given[n] for n in ['x', 'positions']}
    grad_fn = _jax.value_and_grad(_loss, argnums=(0, 1))

    def one_microbatch(ex, loss_target):
        ex = dict(ex)
        diff = ex.pop(TWIN_DIFF_INPUT)
        return grad_fn(weights, diff, {**shared, **ex}, loss_target)

    if N_MICROBATCH == 1:
        loss, (grad_w, grad_x) = one_microbatch(per_example, given["loss_target"])
    else:
        def body(carry, xs):
            loss_sum, grad_sum = carry
            l_k, (gw_k, gx_k) = one_microbatch(xs[0], xs[1])
            with _jax.named_scope("update"):
                return (loss_sum + l_k, _jax.tree.map(_jnp.add, grad_sum, gw_k)), gx_k

        init = (_jnp.zeros((), _jnp.float32), _jax.tree.map(_jnp.zeros_like, weights))
        (loss, grad_w), grad_x = _jax.lax.scan(body, init, (per_example, given["loss_target"]))
    with _jax.named_scope("update"):
        delta_w, new_m, new_v = {}, {}, {}
        for n in TWIN_WEIGHTS:
            delta_w[n], new_m[n], new_v[n] = _adamw(weights[n], grad_w[n], given["m_" + n], given["v_" + n])
    return (loss, grad_x, *[grad_w[n] for n in TWIN_WEIGHTS], *[delta_w[n] for n in TWIN_WEIGHTS],
            *[new_m[n] for n in TWIN_WEIGHTS], *[new_v[n] for n in TWIN_WEIGHTS])
```

```python
import functools
import math

import jax
import jax.numpy as jnp
from jax import lax
from jax.experimental import pallas as pl
from jax.experimental.pallas import tpu as pltpu

F32 = jnp.float32
_MXU = jnp.bfloat16
_WIRE = jnp.bfloat16

D_MODEL = 1024
ATTN_W = 512
SGU_W = 512
HEAD_DIM = 64
N_Q = 8
N_KV = 2
Q_PER_KV = 4
KV_W = 128
BLK = 128
ROT_DIM = 16
ROPE_THETA = 500000.0
N_GRP = 4
GRP_DIM = 128
D_FF = 2816
IN_W = 1792
LN_EPS = 1e-5
ALPHA = 2.0 ** 0.25
N_CHIP = 4
FF_SH = D_FF // N_CHIP
IN_SH = IN_W // N_CHIP
OUT_SH = D_MODEL // N_CHIP

ADAM_LR = 0.001
ADAM_B1 = 0.9
ADAM_B2 = 0.999
ADAM_EPS = 1e-08
ADAM_WD = 0.01
ADAM_STEP = 10

SQRT_HALF = 0.7071067811865476
INV_SQRT_2PI = 0.3989422804014327
MESH_AXES = ("x", "y", "c")
MESH = pl.DeviceIdType.MESH
MIB = 2 ** 20


def _vmem():
    return pl.BlockSpec(memory_space=pltpu.VMEM)


def _smem():
    return pl.BlockSpec(memory_space=pltpu.SMEM)


def _hbm():
    return pl.BlockSpec(memory_space=pl.ANY)


def _params(vmem_mib=48):
    return pltpu.CompilerParams(dimension_semantics=("arbitrary",), vmem_limit_bytes=vmem_mib * MIB)


def _tile(n, cap):
    if n <= cap:
        return n
    for t in range(cap - cap % 16, 0, -16):
        if n % t == 0:
            return t
    raise ValueError((n, cap))


def _rows(tm, width):
    return pl.BlockSpec((tm, width), lambda i: (i, 0))


def _const2(shape):
    return pl.BlockSpec(shape, lambda i: (0,) * len(shape))


def _ln(x, g, b):
    mu = jnp.mean(x, axis=-1, keepdims=True)
    xc = x - mu
    var = jnp.mean(xc * xc, axis=-1, keepdims=True)
    rstd = lax.rsqrt(var + LN_EPS)
    xhat = xc * rstd
    return xhat * g + b, xhat, rstd


def _ln_bwd(dy, xhat, rstd, g):
    gdy = dy * g
    m1 = jnp.mean(gdy, axis=-1, keepdims=True)
    m2 = jnp.mean(gdy * xhat, axis=-1, keepdims=True)
    return rstd * (gdy - m1 - xhat * m2)


def _colsum(a):
    return jnp.sum(a, axis=0, keepdims=True)


def _gelu(x):
    return 0.5 * x * (1.0 + lax.erf(x * SQRT_HALF))


def _gelu_grad(x):
    return 0.5 * (1.0 + lax.erf(x * SQRT_HALF)) + x * jnp.exp(-0.5 * x * x) * INV_SQRT_2PI


def _dot(a, b):
    return jnp.dot(a, b, preferred_element_type=F32)


def _dot_nt(a, b):
    return lax.dot_general(a, b, (((1,), (1,)), ((), ())), preferred_element_type=F32)


def _dot_tn(a, b):
    return lax.dot_general(a, b, (((0,), (0,)), ((), ())), preferred_element_type=F32)


def _rope(t, tc, t1, t2):
    n = t.shape[1]
    rep = n // 128
    if rep > 1:
        tc, t1, t2 = (jnp.tile(a, (1, rep)) for a in (tc, t1, t2))
    return t * tc + pltpu.roll(t, n - 8, 1) * t1 + pltpu.roll(t, 8, 1) * t2


def _rope_bwd(d, tc, t1, t2):
    n = d.shape[1]
    rep = n // 128
    if rep > 1:
        tc, t1, t2 = (jnp.tile(a, (1, rep)) for a in (tc, t1, t2))
    return d * tc + pltpu.roll(d * t1, 8, 1) + pltpu.roll(d * t2, n - 8, 1)


def _band_mask(first_block):
    qi = lax.broadcasted_iota(jnp.int32, (BLK, 2 * BLK), 0)
    kj = lax.broadcasted_iota(jnp.int32, (BLK, 2 * BLK), 1)
    shut = jnp.where(first_block, 2 * BLK, 0)
    prev_ok = jnp.logical_and(kj < BLK, kj > qi + shut)
    cur_ok = jnp.logical_and(kj >= BLK, (kj - BLK) <= qi)
    return jnp.logical_or(prev_ok, cur_ok)


def _causal_w(w_ref, h):
    t = lax.broadcasted_iota(jnp.int32, (BLK, BLK), 0)
    s = lax.broadcasted_iota(jnp.int32, (BLK, BLK), 1)
    return jnp.where(s <= t, w_ref[h], 0.0)


def _lane_put(vals, width):
    rows = vals[0].shape[0]
    lane = lax.broadcasted_iota(jnp.int32, (rows, width), 1)
    out = jnp.zeros((rows, width), F32)
    for k, v in enumerate(vals):
        out = out + jnp.where(lane == k, v, 0.0)
    return out


def _rope_consts():
    lane = jnp.arange(128) % HEAD_DIM
    inv_freq = ROPE_THETA ** (-jnp.arange(0, ROT_DIM, 2, dtype=F32) / ROT_DIM)
    rot = lane < ROT_DIM
    freq = jnp.where(rot, inv_freq[lane % (ROT_DIM // 2)], 0.0)
    rows = [freq, rot.astype(F32), 1.0 - rot.astype(F32), (lane < ROT_DIM // 2).astype(F32),
            jnp.logical_and(lane >= ROT_DIM // 2, rot).astype(F32)]
    rows += [jnp.zeros((128,), F32)] * 3
    return jnp.stack(rows).astype(F32)


def _ln_inproj(x, pos_col, g0, b0, w_in, b_in):
    s_len = x.shape[0]
    tm = _tile(s_len, 512)

    def body(x_ref, pos_ref, g_ref, b_ref, w_ref, bi_ref, rc_ref,
             q_ref, k_ref, v_ref, su_ref, sv_ref, tc_ref, t1_ref, t2_ref):
        h0, _, _ = _ln(x_ref[...], g_ref[...], b_ref[...])
        proj = _dot(h0.astype(_MXU), w_ref[...]) + bi_ref[...]
        ang = pos_ref[...].astype(F32) * rc_ref[0:1, :]
        cs = jnp.cos(ang)
        sn = jnp.sin(ang)
        tc = cs * rc_ref[1:2, :] + rc_ref[2:3, :]
        t1 = -sn * rc_ref[3:4, :]
        t2 = sn * rc_ref[4:5, :]
        tc_ref[...] = tc
        t1_ref[...] = t1
        t2_ref[...] = t2
        q = _rope(proj[:, 0:ATTN_W], tc, t1, t2) * (HEAD_DIM ** -0.5)
        q_ref[...] = q.astype(_MXU)
        k_ref[...] = _rope(proj[:, ATTN_W:ATTN_W + KV_W], tc, t1, t2).astype(_MXU)
        v_ref[...] = proj[:, ATTN_W + KV_W:ATTN_W + 2 * KV_W].astype(_MXU)
        su_ref[...] = proj[:, ATTN_W + 2 * KV_W:ATTN_W + 2 * KV_W + SGU_W]
        sv_ref[...] = proj[:, ATTN_W + 2 * KV_W + SGU_W:IN_W]

    sd = jax.ShapeDtypeStruct
    return pl.pallas_call(
        body, name="ln_inproj", grid=(s_len // tm,),
        in_specs=[_rows(tm, D_MODEL), _rows(tm, 1), _const2((1, D_MODEL)), _const2((1, D_MODEL)), _vmem(),
                  _const2((1, IN_W)), _const2((8, 128))],
        out_specs=[_rows(tm, ATTN_W), _rows(tm, KV_W), _rows(tm, KV_W), _rows(tm, SGU_W), _rows(tm, SGU_W),
                   _rows(tm, 128), _rows(tm, 128), _rows(tm, 128)],
        out_shape=[sd((s_len, ATTN_W), _MXU), sd((s_len, KV_W), _MXU), sd((s_len, KV_W), _MXU),
                   sd((s_len, SGU_W), F32), sd((s_len, SGU_W), F32),
                   sd((s_len, 128), F32), sd((s_len, 128), F32), sd((s_len, 128), F32)],
        compiler_params=_params(48),
    )(x, pos_col, g0, b0, w_in, b_in, _rope_consts())


def _attn_probs(qh, kh, sink, allowed):
    s = jnp.where(allowed, _dot_nt(qh, kh), -1e30)
    m = jnp.maximum(jnp.max(s, axis=-1, keepdims=True), sink)
    p = jnp.exp(s - m)
    ps = jnp.exp(sink - m)
    inv = 1.0 / (jnp.sum(p, axis=-1, keepdims=True) + ps)
    return p * inv, ps * inv


def _sgu_fwd(su, sv, lg, lb, w_ref, bt_ref):
    u = _gelu(su)
    vv, vhat, rstd = _ln(_gelu(sv), lg, lb)
    vvb = vv.astype(_MXU)
    wcs, mixed = [], []
    for h in range(N_GRP):
        wc = _causal_w(w_ref, h).astype(_MXU)
        wcs.append(wc)
        mixed.append(_dot(wc, vvb[:, h * GRP_DIM:(h + 1) * GRP_DIM]) + bt_ref[:, h:h + 1])
    return u, vhat, rstd, vvb, wcs, jnp.concatenate(mixed, axis=1)


def _prev_map(i):
    return (jnp.maximum(i - 1, 0), 0)


def _mixer_fwd(q, k, v, su, sv, sinks, sg, sb, sgu_w, sgu_bt):
    s_len = q.shape[0]
    nb = s_len // BLK

    def body(q_ref, kc_ref, kp_ref, vc_ref, vp_ref, su_ref, sv_ref, sink_ref, lg_ref, lb_ref, w_ref, bt_ref, mc_ref):
        i = pl.program_id(0)
        allowed = _band_mask(i == 0)
        kb = jnp.concatenate([kp_ref[...], kc_ref[...]], axis=0)
        vb = jnp.concatenate([vp_ref[...], vc_ref[...]], axis=0)
        qv = q_ref[...]
        outs = []
        for h in range(N_Q):
            g = h // Q_PER_KV
            kh = kb[:, g * HEAD_DIM:(g + 1) * HEAD_DIM]
            vh = vb[:, g * HEAD_DIM:(g + 1) * HEAD_DIM]
            probs, _ = _attn_probs(qv[:, h * HEAD_DIM:(h + 1) * HEAD_DIM], kh, sink_ref[h], allowed)
            outs.append(_dot(probs.astype(_MXU), vh))
        u, _, _, _, _, mixed = _sgu_fwd(su_ref[...], sv_ref[...], lg_ref[...], lb_ref[...], w_ref, bt_ref)
        mc_ref[...] = jnp.concatenate(outs + [u * mixed], axis=1).astype(_MXU)

    cur = lambda w: pl.BlockSpec((BLK, w), lambda i: (i, 0))
    prev = lambda w: pl.BlockSpec((BLK, w), _prev_map)
    return pl.pallas_call(
        body, name="mixer_fwd", grid=(nb,),
        in_specs=[cur(ATTN_W), cur(KV_W), prev(KV_W), cur(KV_W), prev(KV_W), cur(SGU_W), cur(SGU_W), _smem(),
                  _const2((1, SGU_W)), _const2((1, SGU_W)), _const2((N_GRP, BLK, BLK)), _const2((BLK, N_GRP))],
        out_specs=cur(D_MODEL),
        out_shape=jax.ShapeDtypeStruct((s_len, D_MODEL), _MXU),
        compiler_params=_params(32),
    )(q, k, k, v, v, su, sv, sinks, sg, sb, sgu_w, sgu_bt)


def _outproj(mc, w_out, b_out, x, g0, b0):
    s_len = x.shape[0]
    tm = _tile(s_len, 512)

    def body(mc_ref, w_ref, bo_ref, x_ref, g_ref, b_ref, r1_ref):
        h0, _, _ = _ln(x_ref[...], g_ref[...], b_ref[...])
        r1_ref[...] = ALPHA * h0 + (_dot(mc_ref[...], w_ref[...]) + bo_ref[...])

    return pl.pallas_call(
        body, name="outproj", grid=(s_len // tm,),
        in_specs=[_rows(tm, D_MODEL), _vmem(), _const2((1, D_MODEL)), _rows(tm, D_MODEL),
                  _const2((1, D_MODEL)), _const2((1, D_MODEL))],
        out_specs=_rows(tm, D_MODEL),
        out_shape=jax.ShapeDtypeStruct((s_len, D_MODEL), F32),
        compiler_params=_params(32),
    )(mc, w_out, b_out, x, g0, b0)


def _ffn_spec(tm):
    return pl.BlockSpec((N_CHIP, tm, FF_SH), lambda i: (0, i, 0))


def _ffn_up(r1, g1, b1, wg, wu):
    s_len = r1.shape[0]
    tm = _tile(s_len, 256)

    def body(r1_ref, g_ref, b_ref, wg_ref, wu_ref, go_ref, uo_ref):
        h1, _, _ = _ln(r1_ref[...], g_ref[...], b_ref[...])
        h1b = h1.astype(_MXU)
        for j in range(N_CHIP):
            go_ref[j] = _dot(h1b, wg_ref[j])
            uo_ref[j] = _dot(h1b, wu_ref[j])

    sd = jax.ShapeDtypeStruct((N_CHIP, s_len, FF_SH), F32)
    return pl.pallas_call(
        body, name="ffn_up", grid=(s_len // tm,),
        in_specs=[_rows(tm, D_MODEL), _const2((1, D_MODEL)), _const2((1, D_MODEL)), _vmem(), _vmem()],
        out_specs=[_ffn_spec(tm), _ffn_spec(tm)],
        out_shape=[sd, sd],
        compiler_params=_params(56),
    )(r1, g1, b1, wg, wu)


def _silu_parts(g):
    sg = 1.0 / (1.0 + jnp.exp(-g))
    return g * sg, sg


def _ffn_down_loss(gact, uact, wd, r1, g1, b1, g2, b2, target):
    s_len = r1.shape[0]
    tm = _tile(s_len, 256)

    def body(g_ref, u_ref, wd_ref, r1_ref, g1_ref, b1_ref, g2_ref, b2_ref, t_ref,
             dr2_ref, loss_ref, dg2_ref, db2_ref):
        i = pl.program_id(0)
        f = jnp.zeros((tm, D_MODEL), F32)
        for j in range(N_CHIP):
            silu, _ = _silu_parts(g_ref[j])
            f = f + _dot((silu * u_ref[j]).astype(_MXU), wd_ref[j])
        h1, _, _ = _ln(r1_ref[...], g1_ref[...], b1_ref[...])
        h2, r2hat, rstd2 = _ln(ALPHA * h1 + f, g2_ref[...], b2_ref[...])
        diff = h2 - t_ref[...]
        dh2 = diff * (1.0 / D_MODEL)

        @pl.when(i == 0)
        def _():
            loss_ref[...] = jnp.zeros_like(loss_ref)
            dg2_ref[...] = jnp.zeros_like(dg2_ref)
            db2_ref[...] = jnp.zeros_like(db2_ref)

        loss_ref[...] += _colsum(diff * diff)
        dg2_ref[...] += _colsum(dh2 * r2hat)
        db2_ref[...] += _colsum(dh2)
        dr2_ref[...] = _ln_bwd(dh2, r2hat, rstd2, g2_ref[...])

    vec = jax.ShapeDtypeStruct((1, D_MODEL), F32)
    c = _const2((1, D_MODEL))
    return pl.pallas_call(
        body, name="ffn_down_loss", grid=(s_len // tm,),
        in_specs=[_ffn_spec(tm), _ffn_spec(tm), _vmem(), _rows(tm, D_MODEL), c, c, c, c, _rows(tm, D_MODEL)],
        out_specs=[_rows(tm, D_MODEL), c, c, c],
        out_shape=[jax.ShapeDtypeStruct((s_len, D_MODEL), F32), vec, vec, vec],
        compiler_params=_params(48),
    )(gact, uact, wd, r1, g1, b1, g2, b2, target)


def _ffn_bwd_a(dr2, gact, uact, wd):
    s_len = dr2.shape[0]
    tm = _tile(s_len, 256)

    def body(dr2_ref, g_ref, u_ref, wd_ref, dg_ref, du_ref, dwd_ref):
        i = pl.program_id(0)

        @pl.when(i == 0)
        def _():
            dwd_ref[...] = jnp.zeros_like(dwd_ref)

        dfb = dr2_ref[...].astype(_MXU)
        for j in range(N_CHIP):
            g = g_ref[j]
            u = u_ref[j]
            silu, sg = _silu_parts(g)
            da = _dot_nt(dfb, wd_ref[j])
            dg_ref[j] = (da * u * (sg * (1.0 + g * (1.0 - sg)))).astype(_MXU)
            du_ref[j] = (da * silu).astype(_MXU)
            dwd_ref[j] += _dot_tn((silu * u).astype(_MXU), dfb)

    sd = jax.ShapeDtypeStruct((N_CHIP, s_len, FF_SH), _MXU)
    return pl.pallas_call(
        body, name="ffn_bwd_a", grid=(s_len // tm,),
        in_specs=[_rows(tm, D_MODEL), _ffn_spec(tm), _ffn_spec(tm), _vmem()],
        out_specs=[_ffn_spec(tm), _ffn_spec(tm), _vmem()],
        out_shape=[sd, sd, jax.ShapeDtypeStruct((N_CHIP, FF_SH, D_MODEL), F32)],
        compiler_params=_params(56),
    )(dr2, gact, uact, wd)


def _ffn_bwd_b(dr2, dg, du, r1, g1, b1, wg, wu):
    s_len = dr2.shape[0]
    tm = _tile(s_len, 256)

    def body(dr2_ref, dg_ref, du_ref, r1_ref, g1_ref, b1_ref, wg_ref, wu_ref,
             dr1_ref, dwg_ref, dwu_ref, dg1_ref, db1_ref):
        i = pl.program_id(0)

        @pl.when(i == 0)
        def _():
            dwg_ref[...] = jnp.zeros_like(dwg_ref)
            dwu_ref[...] = jnp.zeros_like(dwu_ref)
            dg1_ref[...] = jnp.zeros_like(dg1_ref)
            db1_ref[...] = jnp.zeros_like(db1_ref)

        h1, r1hat, rstd1 = _ln(r1_ref[...], g1_ref[...], b1_ref[...])
        h1b = h1.astype(_MXU)
        dh1 = ALPHA * dr2_ref[...]
        for j in range(N_CHIP):
            dgj = dg_ref[j]
            duj = du_ref[j]
            dh1 = dh1 + _dot_nt(dgj, wg_ref[j]) + _dot_nt(duj, wu_ref[j])
            dwg_ref[j] += _dot_tn(h1b, dgj)
            dwu_ref[j] += _dot_tn(h1b, duj)
        dg1_ref[...] += _colsum(dh1 * r1hat)
        db1_ref[...] += _colsum(dh1)
        dr1_ref[...] = _ln_bwd(dh1, r1hat, rstd1, g1_ref[...])

    vec = jax.ShapeDtypeStruct((1, D_MODEL), F32)
    c = _const2((1, D_MODEL))
    wsd = jax.ShapeDtypeStruct((N_CHIP, D_MODEL, FF_SH), F32)
    return pl.pallas_call(
        body, name="ffn_bwd_b", grid=(s_len // tm,),
        in_specs=[_rows(tm, D_MODEL), _ffn_spec(tm), _ffn_spec(tm), _rows(tm, D_MODEL), c, c, _vmem(), _vmem()],
        out_specs=[_rows(tm, D_MODEL), _vmem(), _vmem(), c, c],
        out_shape=[jax.ShapeDtypeStruct((s_len, D_MODEL), F32), wsd, wsd, vec, vec],
        compiler_params=_params(60),
    )(dr2, dg, du, r1, g1, b1, wg, wu)


def _outproj_bwd(dr1, mc, w_out):
    s_len = dr1.shape[0]
    tm = _tile(s_len, 512)

    def body(dr1_ref, mc_ref, w_ref, dmc_ref, dw_ref, db_ref):
        i = pl.program_id(0)

        @pl.when(i == 0)
        def _():
            dw_ref[...] = jnp.zeros_like(dw_ref)
            db_ref[...] = jnp.zeros_like(db_ref)

        d = dr1_ref[...]
        db_ref[...] += _colsum(d)
        db16 = d.astype(_MXU)
        dmc_ref[...] = _dot_nt(db16, w_ref[...])
        dw_ref[...] += _dot_tn(mc_ref[...], db16)

    return pl.pallas_call(
        body, name="outproj_bwd", grid=(s_len // tm,),
        in_specs=[_rows(tm, D_MODEL), _rows(tm, D_MODEL), _vmem()],
        out_specs=[_rows(tm, D_MODEL), _vmem(), _const2((1, D_MODEL))],
        out_shape=[jax.ShapeDtypeStruct((s_len, D_MODEL), F32), jax.ShapeDtypeStruct((D_MODEL, D_MODEL), F32),
                   jax.ShapeDtypeStruct((1, D_MODEL), F32)],
        compiler_params=_params(40),
    )(dr1, mc, w_out)


def _mixer_bwd(q, k, v, su, sv, dmc, tc, t1, t2, sinks, sg, sb, sgu_w, sgu_bt):
    s_len = q.shape[0]
    nb = s_len // BLK

    def body(q_ref, kc_ref, kp_ref, vc_ref, vp_ref, su_ref, sv_ref, dmc_ref,
             tc_ref, t1_ref, t2_ref, tcp_ref, t1p_ref, t2p_ref,
             sink_ref, lg_ref, lb_ref, w_ref, bt_ref,
             dq_ref, dkv_ref, dsuv_ref, dbq_ref, dbkv_ref, dbsuv_ref,
             dsink_ref, dlg_ref, dlb_ref, dw_ref, dbt_ref, carry_ref):
        i = pl.program_id(0)

        @pl.when(i == 0)
        def _():
            for r in (dbq_ref, dbkv_ref, dbsuv_ref, dsink_ref, dlg_ref, dlb_ref, dw_ref, dbt_ref):
                r[...] = jnp.zeros_like(r)

        def emit_kv(fin):
            dk = _rope_bwd(fin[:, 0:KV_W], tcp_ref[...], t1p_ref[...], t2p_ref[...])
            out = jnp.concatenate([dk, fin[:, KV_W:2 * KV_W]], axis=1)
            dkv_ref[...] = out.astype(_MXU)
            dbkv_ref[...] += _colsum(out)

        @pl.when(i < nb)
        def _():
            allowed = _band_mask(i == 0)
            kb = jnp.concatenate([kp_ref[...], kc_ref[...]], axis=0)
            vb = jnp.concatenate([vp_ref[...], vc_ref[...]], axis=0)
            qv = q_ref[...]
            dmc = dmc_ref[...]
            dqs, dks, dvs, dsinks = [], [], [], []
            for g in range(N_KV):
                kh = kb[:, g * HEAD_DIM:(g + 1) * HEAD_DIM]
                vh = vb[:, g * HEAD_DIM:(g + 1) * HEAD_DIM]
                dk_g = jnp.zeros((2 * BLK, HEAD_DIM), F32)
                dv_g = jnp.zeros((2 * BLK, HEAD_DIM), F32)
                for hh in range(Q_PER_KV):
                    h = g * Q_PER_KV + hh
                    qh = qv[:, h * HEAD_DIM:(h + 1) * HEAD_DIM]
                    probs, psink = _attn_probs(qh, kh, sink_ref[h], allowed)
                    pb = probs.astype(_MXU)
                    dob = dmc[:, h * HEAD_DIM:(h + 1) * HEAD_DIM].astype(_MXU)
                    dv_g = dv_g + _dot_tn(pb, dob)
                    dp = _dot_nt(dob, vh)
                    rd = jnp.sum(probs * dp, axis=-1, keepdims=True)
                    dsb = (probs * (dp - rd)).astype(_MXU)
                    dsinks.append(-jnp.sum(psink * rd, axis=0, keepdims=True))
                    dqs.append(_dot(dsb, kh))
                    dk_g = dk_g + _dot_tn(dsb, qh)
                dks.append(dk_g)
                dvs.append(dv_g)
            dq = _rope_bwd(jnp.concatenate(dqs, axis=1) * (HEAD_DIM ** -0.5), tc_ref[...], t1_ref[...], t2_ref[...])
            dq_ref[...] = dq.astype(_MXU)
            dbq_ref[...] += _colsum(dq)
            dsink_ref[...] += _lane_put(dsinks, 128)
            contrib = jnp.concatenate(dks + dvs, axis=1)

            @pl.when(i > 0)
            def _():
                emit_kv(carry_ref[...] + contrib[0:BLK, :])

            carry_ref[...] = contrib[BLK:2 * BLK, :]

            su = su_ref[...]
            sv = sv_ref[...]
            lg = lg_ref[...]
            u, vhat, rstd, vvb, wcs, mixed = _sgu_fwd(su, sv, lg, lb_ref[...], w_ref, bt_ref)
            dsgu = dmc[:, ATTN_W:D_MODEL]
            dsu = dsgu * mixed * _gelu_grad(su)
            dmixed = dsgu * u
            tri_t = lax.broadcasted_iota(jnp.int32, (BLK, BLK), 0)
            tri_s = lax.broadcasted_iota(jnp.int32, (BLK, BLK), 1)
            dvv, dbs = [], []
            for h in range(N_GRP):
                dm = dmixed[:, h * GRP_DIM:(h + 1) * GRP_DIM]
                dmb = dm.astype(_MXU)
                dbs.append(jnp.sum(dm, axis=1, keepdims=True))
                dw_ref[h] += jnp.where(tri_s <= tri_t, _dot_nt(dmb, vvb[:, h * GRP_DIM:(h + 1) * GRP_DIM]), 0.0)
                dvv.append(_dot_tn(wcs[h], dmb))
            dvv = jnp.concatenate(dvv, axis=1)
            dbt_ref[...] += _lane_put(dbs, 128)
            dlg_ref[...] += _colsum(dvv * vhat)
            dlb_ref[...] += _colsum(dvv)
            dsv = _ln_bwd(dvv, vhat, rstd, lg) * _gelu_grad(sv)
            dsuv = jnp.concatenate([dsu, dsv], axis=1)
            dsuv_ref[...] = dsuv.astype(_MXU)
            dbsuv_ref[...] += _colsum(dsuv)

        @pl.when(i == nb)
        def _():
            emit_kv(carry_ref[...])

    last = nb - 1
    cur = lambda w: pl.BlockSpec((BLK, w), lambda i: (jnp.minimum(i, last), 0))
    prev = lambda w: pl.BlockSpec((BLK, w), lambda i: (jnp.clip(i - 1, 0, last), 0))
    sd = jax.ShapeDtypeStruct
    return pl.pallas_call(
        body, name="mixer_bwd", grid=(nb + 1,),
        in_specs=[cur(ATTN_W), cur(KV_W), prev(KV_W), cur(KV_W), prev(KV_W), cur(SGU_W), cur(SGU_W), cur(D_MODEL),
                  cur(128), cur(128), cur(128), prev(128), prev(128), prev(128),
                  _smem(), _const2((1, SGU_W)), _const2((1, SGU_W)), _const2((N_GRP, BLK, BLK)), _const2((BLK, N_GRP))],
        out_specs=[cur(ATTN_W), prev(2 * KV_W), cur(2 * SGU_W),
                   _const2((1, ATTN_W)), _const2((1, 2 * KV_W)), _const2((1, 2 * SGU_W)),
                   _const2((1, 128)), _const2((1, SGU_W)), _const2((1, SGU_W)),
                   _const2((N_GRP, BLK, BLK)), _const2((BLK, 128))],
        out_shape=[sd((s_len, ATTN_W), _MXU), sd((s_len, 2 * KV_W), _MXU), sd((s_len, 2 * SGU_W), _MXU),
                   sd((1, ATTN_W), F32), sd((1, 2 * KV_W), F32), sd((1, 2 * SGU_W), F32),
                   sd((1, 128), F32), sd((1, SGU_W), F32), sd((1, SGU_W), F32),
                   sd((N_GRP, BLK, BLK), F32), sd((BLK, 128), F32)],
        scratch_shapes=[pltpu.VMEM((BLK, 2 * KV_W), F32)],
        compiler_params=_params(32),
    )(q, k, k, v, v, su, sv, dmc, tc, t1, t2, tc, t1, t2, sinks, sg, sb, sgu_w, sgu_bt)


def _inproj_bwd(dq, dkv, dsuv, dr1, x, g0, b0, w_in):
    s_len = x.shape[0]
    tm = _tile(s_len, 512)
    cuts = ((0, ATTN_W), (ATTN_W, ATTN_W + 2 * KV_W), (ATTN_W + 2 * KV_W, IN_W))

    def body(dq_ref, dkv_ref, dsuv_ref, dr1_ref, x_ref, g_ref, b_ref, w_ref, dx_ref, dw_ref, dg_ref, db_ref):
        i = pl.program_id(0)

        @pl.when(i == 0)
        def _():
            dw_ref[...] = jnp.zeros_like(dw_ref)
            dg_ref[...] = jnp.zeros_like(dg_ref)
            db_ref[...] = jnp.zeros_like(db_ref)

        h0, xhat, rstd = _ln(x_ref[...], g_ref[...], b_ref[...])
        h0b = h0.astype(_MXU)
        dh0 = ALPHA * dr1_ref[...]
        for (lo, hi), d_ref in zip(cuts, (dq_ref, dkv_ref, dsuv_ref)):
            d = d_ref[...]
            dh0 = dh0 + _dot_nt(d, w_ref[:, lo:hi])
            dw_ref[:, lo:hi] += _dot_tn(h0b, d)
        dg_ref[...] += _colsum(dh0 * xhat)
        db_ref[...] += _colsum(dh0)
        dx_ref[...] = _ln_bwd(dh0, xhat, rstd, g_ref[...])

    vec = jax.ShapeDtypeStruct((1, D_MODEL), F32)
    c = _const2((1, D_MODEL))
    return pl.pallas_call(
        body, name="inproj_bwd", grid=(s_len // tm,),
        in_specs=[_rows(tm, ATTN_W), _rows(tm, 2 * KV_W), _rows(tm, 2 * SGU_W), _rows(tm, D_MODEL), _rows(tm, D_MODEL),
                  c, c, _vmem()],
        out_specs=[_rows(tm, D_MODEL), _vmem(), c, c],
        out_shape=[jax.ShapeDtypeStruct((s_len, D_MODEL), F32), jax.ShapeDtypeStruct((D_MODEL, IN_W), F32), vec, vec],
        compiler_params=_params(48),
    )(dq, dkv, dsuv, dr1, x, g0, b0, w_in)


def _place():
    x, y, c = (lax.axis_index(a) for a in MESH_AXES)
    chips = [(1 - x, y), (x, 1 - y), (1 - x, 1 - y)]
    return x, y, c, chips


def _gather_weights(shards):
    n = len(shards)
    halves = [s.shape[0] // 2 for s in shards]

    def body(*refs):
        ins, outs = refs[:n], refs[n:2 * n]
        send_sems, recv_sems = refs[2 * n], refs[2 * n + 1]
        x, y, c, chips = _place()
        me = 2 * x + y

        def piece(t, slot, half):
            return outs[t].at[slot, pl.ds(pl.multiple_of(half * halves[t], 16), halves[t]), :]

        def copy(k, t, slot, half, to):
            return pltpu.make_async_remote_copy(
                src_ref=piece(t, slot, half), dst_ref=piece(t, slot, half),
                send_sem=send_sems.at[k], recv_sem=recv_sems.at[k], device_id=to, device_id_type=MESH)

        for t in range(n):
            outs[t][me] = ins[t][...].astype(_WIRE)
        started = []
        for t in range(n):
            for d, chip in enumerate(chips):
                cp = copy(3 * t + d, t, me, c, (chip[0], chip[1], c))
                cp.start()
                started.append(cp)
        for t in range(n):
            for d, chip in enumerate(chips):
                slot = 2 * chip[0] + chip[1]
                copy(3 * t + d, t, slot, c, (chip[0], chip[1], c)).wait_recv()
                fwd = copy(3 * n + 3 * t + d, t, slot, c, (x, y, 1 - c))
                fwd.start()
                started.append(fwd)
        for t in range(n):
            for d, chip in enumerate(chips):
                slot = 2 * chip[0] + chip[1]
                copy(3 * n + 3 * t + d, t, slot, 1 - c, (x, y, 1 - c)).wait_recv()
        for cp in started:
            cp.wait_send()

    return pl.pallas_call(
        body, name="gather_weights",
        in_specs=[_vmem()] * n, out_specs=[_vmem()] * n,
        out_shape=[jax.ShapeDtypeStruct((N_CHIP,) + s.shape, _WIRE) for s in shards],
        scratch_shapes=[pltpu.SemaphoreType.DMA((6 * n,)), pltpu.SemaphoreType.DMA((6 * n,))],
        compiler_params=pltpu.CompilerParams(vmem_limit_bytes=56 * MIB),
    )(*shards)


def _pair_exchange(grads):
    n = len(grads)
    halves = [g.shape[1] // 2 for g in grads]

    def body(*refs):
        ins, mine, theirs = refs[:n], refs[n:2 * n], refs[2 * n:3 * n]
        send_sems, recv_sems, local_sems = refs[3 * n:3 * n + 3]
        x, y, c, _ = _place()
        copies = []
        for t in range(n):
            keep = ins[t].at[:, pl.ds(pl.multiple_of(c * halves[t], 8), halves[t]), :]
            give = ins[t].at[:, pl.ds(pl.multiple_of((1 - c) * halves[t], 8), halves[t]), :]
            rc = pltpu.make_async_remote_copy(src_ref=give, dst_ref=theirs[t], send_sem=send_sems.at[t],
                                              recv_sem=recv_sems.at[t], device_id=(x, y, 1 - c), device_id_type=MESH)
            lc = pltpu.make_async_copy(keep, mine[t], local_sems.at[t])
            rc.start()
            lc.start()
            copies += [rc, lc]
        for cp in copies:
            cp.wait()

    half_shapes = [jax.ShapeDtypeStruct((N_CHIP, h, g.shape[2]), F32) for g, h in zip(grads, halves)]
    return pl.pallas_call(
        body, name="pair_exchange",
        in_specs=[_hbm()] * n, out_specs=[_hbm()] * (2 * n), out_shape=half_shapes + half_shapes,
        scratch_shapes=[pltpu.SemaphoreType.DMA((n,)), pltpu.SemaphoreType.DMA((n,)), pltpu.SemaphoreType.DMA((n,))],
    )(*grads)


def _chip_exchange(sums_wire, sums_f32):
    n = len(sums_wire)

    def body(*refs):
        wires, fulls, got, own = refs[:n], refs[n:2 * n], refs[2 * n:3 * n], refs[3 * n:4 * n]
        send_sems, recv_sems, local_sems = refs[4 * n:4 * n + 3]
        x, y, c, chips = _place()
        me = 2 * x + y
        copies = []
        for t in range(n):
            lc = pltpu.make_async_copy(fulls[t].at[me], own[t], local_sems.at[t])
            lc.start()
            copies.append(lc)
            for d, chip in enumerate(chips):
                rc = pltpu.make_async_remote_copy(
                    src_ref=wires[t].at[2 * chip[0] + chip[1]], dst_ref=got[t].at[d],
                    send_sem=send_sems.at[3 * t + d], recv_sem=recv_sems.at[3 * t + d],
                    device_id=(chip[0], chip[1], c), device_id_type=MESH)
                rc.start()
                copies.append(rc)
        for cp in copies:
            cp.wait()

    got_shapes = [jax.ShapeDtypeStruct((3,) + s.shape[1:], s.dtype) for s in sums_wire]
    own_shapes = [jax.ShapeDtypeStruct(s.shape[1:], F32) for s in sums_f32]
    return pl.pallas_call(
        body, name="chip_exchange",
        in_specs=[_hbm()] * (2 * n), out_specs=[_hbm()] * (2 * n), out_shape=got_shapes + own_shapes,
        scratch_shapes=[pltpu.SemaphoreType.DMA((3 * n,)), pltpu.SemaphoreType.DMA((3 * n,)),
                        pltpu.SemaphoreType.DMA((n,))],
    )(*sums_wire, *sums_f32)


def _half_swap(halves_in):
    n = len(halves_in)

    def body(*refs):
        ins, outs = refs[:n], refs[n:2 * n]
        send_sems, recv_sems, local_sems = refs[2 * n:2 * n + 3]
        x, y, c, _ = _place()
        copies = []
        for t in range(n):
            rc = pltpu.make_async_remote_copy(src_ref=ins[t], dst_ref=outs[t].at[c], send_sem=send_sems.at[t],
                                              recv_sem=recv_sems.at[t], device_id=(x, y, 1 - c), device_id_type=MESH)
            lc = pltpu.make_async_copy(ins[t], outs[t].at[c], local_sems.at[t])
            rc.start()
            lc.start()
            copies += [rc, lc]
        for cp in copies:
            cp.wait()

    return pl.pallas_call(
        body, name="half_swap",
        in_specs=[_hbm()] * n, out_specs=[_hbm()] * n,
        out_shape=[jax.ShapeDtypeStruct((2,) + h.shape, F32) for h in halves_in],
        scratch_shapes=[pltpu.SemaphoreType.DMA((n,)), pltpu.SemaphoreType.DMA((n,)), pltpu.SemaphoreType.DMA((n,))],
    )(*halves_in)


def _allreduce_small(packed):
    rows = packed.shape[0]

    def body(p_ref, out_ref, buf_ref, send_sems, recv_sems):
        x, y, c, _ = _place()
        me = 4 * x + 2 * y + c
        buf_ref[me] = p_ref[...]
        copies = []
        for r in range(1, 8):
            rx, ry, rc_ = (r >> 2) & 1, (r >> 1) & 1, r & 1
            peer = (x ^ rx, y ^ ry, c ^ rc_)
            cp = pltpu.make_async_remote_copy(src_ref=buf_ref.at[me], dst_ref=buf_ref.at[me],
                                              send_sem=send_sems.at[r - 1], recv_sem=recv_sems.at[r - 1],
                                              device_id=peer, device_id_type=MESH)
            cp.start()
            copies.append(cp)
        for cp in copies:
            cp.wait()
        acc = buf_ref[0]
        for d in range(1, 8):
            acc = acc + buf_ref[d]
        out_ref[...] = acc

    return pl.pallas_call(
        body, name="allreduce_small",
        in_specs=[_vmem()], out_specs=_vmem(),
        out_shape=jax.ShapeDtypeStruct((rows, 128), F32),
        scratch_shapes=[pltpu.VMEM((8, rows, 128), F32), pltpu.SemaphoreType.DMA((7,)), pltpu.SemaphoreType.DMA((7,))],
        compiler_params=pltpu.CompilerParams(vmem_limit_bytes=32 * MIB),
    )(packed)


def _elementwise(name, fn, ins, out_dtypes, tile_rows=256):
    shape = ins[0].shape
    lead = shape[:-2]
    rows, cols = shape[-2:]
    tr = _tile(rows, tile_rows)
    n_lead = math.prod(lead)
    nr = rows // tr
    flat = [a.reshape((n_lead, rows, cols)) for a in ins]

    def body(*refs):
        outs = fn(*[r[0] for r in refs[:len(ins)]])
        for o_ref, o in zip(refs[len(ins):], outs):
            o_ref[0] = o.astype(o_ref.dtype)

    spec = pl.BlockSpec((1, tr, cols), lambda i: (i // nr, i % nr, 0))
    res = pl.pallas_call(
        body, name=name, grid=(n_lead * nr,),
        in_specs=[spec] * len(ins), out_specs=[spec] * len(out_dtypes),
        out_shape=[jax.ShapeDtypeStruct((n_lead, rows, cols), dt) for dt in out_dtypes],
        compiler_params=_params(32),
    )(*flat)
    return [r.reshape(shape) for r in res]


def _adamw_math(w, g, m, v):
    m = ADAM_B1 * m + (1.0 - ADAM_B1) * g
    v = ADAM_B2 * v + (1.0 - ADAM_B2) * (g * g)
    m_hat = m / (1.0 - ADAM_B1 ** ADAM_STEP)
    v_hat = v / (1.0 - ADAM_B2 ** ADAM_STEP)
    delta = -ADAM_LR * (m_hat / (jnp.sqrt(v_hat) + ADAM_EPS) + ADAM_WD * w)
    return delta, m, v


def _adamw(name, w, g, m, v, tile_rows=256):
    return _elementwise(name, _adamw_math, [w, g, m, v], [F32, F32, F32], tile_rows)


_SMALL = ("ln_in_g", "ln_in_b", "b_in", "attn_sinks", "sgu_ln_g", "sgu_ln_b", "sgu_w", "sgu_b", "b_out",
          "ln_mix_g", "ln_mix_b", "ln_ffn_g", "ln_ffn_b")


def _pack(arrs):
    parts = []
    for a in arrs:
        flat = a.reshape(-1)
        pad = (-flat.shape[0]) % 1024
        parts.append(jnp.pad(flat, (0, pad)) if pad else flat)
    return jnp.concatenate(parts).reshape(-1, 128)


def _unpack(packed, like):
    flat = packed.reshape(-1)
    out, off = [], 0
    for a in like:
        n = math.prod(a.shape)
        out.append(flat[off:off + n].reshape(a.shape))
        off += n + ((-n) % 1024)
    return out


def kernel(x, positions, ln_in_g, ln_in_b, w_in, b_in, attn_sinks, sgu_ln_g, sgu_ln_b, sgu_w, sgu_b, w_out, b_out, ln_mix_g, ln_mix_b, w_gate, w_up, w_down, ln_ffn_g, ln_ffn_b, loss_target, m_ln_in_g, m_ln_in_b, m_w_in, m_b_in, m_attn_sinks, m_sgu_ln_g, m_sgu_ln_b, m_sgu_w, m_sgu_b, m_w_out, m_b_out, m_ln_mix_g, m_ln_mix_b, m_w_gate, m_w_up, m_w_down, m_ln_ffn_g, m_ln_ffn_b, v_ln_in_g, v_ln_in_b, v_w_in, v_b_in, v_attn_sinks, v_sgu_ln_g, v_sgu_ln_b, v_sgu_w, v_sgu_b, v_w_out, v_b_out, v_ln_mix_g, v_ln_mix_b, v_w_gate, v_w_up, v_w_down, v_ln_ffn_g, v_ln_ffn_b):
    weights = dict(ln_in_g=ln_in_g, ln_in_b=ln_in_b, w_in=w_in, b_in=b_in, attn_sinks=attn_sinks, sgu_ln_g=sgu_ln_g,
                   sgu_ln_b=sgu_ln_b, sgu_w=sgu_w, sgu_b=sgu_b, w_out=w_out, b_out=b_out, ln_mix_g=ln_mix_g,
                   ln_mix_b=ln_mix_b, w_gate=w_gate, w_up=w_up, w_down=w_down, ln_ffn_g=ln_ffn_g, ln_ffn_b=ln_ffn_b)
    mom_m = dict(ln_in_g=m_ln_in_g, ln_in_b=m_ln_in_b, w_in=m_w_in, b_in=m_b_in, attn_sinks=m_attn_sinks,
                 sgu_ln_g=m_sgu_ln_g, sgu_ln_b=m_sgu_ln_b, sgu_w=m_sgu_w, sgu_b=m_sgu_b, w_out=m_w_out, b_out=m_b_out,
                 ln_mix_g=m_ln_mix_g, ln_mix_b=m_ln_mix_b, w_gate=m_w_gate, w_up=m_w_up, w_down=m_w_down,
                 ln_ffn_g=m_ln_ffn_g, ln_ffn_b=m_ln_ffn_b)
    mom_v = dict(ln_in_g=v_ln_in_g, ln_in_b=v_ln_in_b, w_in=v_w_in, b_in=v_b_in, attn_sinks=v_attn_sinks,
                 sgu_ln_g=v_sgu_ln_g, sgu_ln_b=v_sgu_ln_b, sgu_w=v_sgu_w, sgu_b=v_sgu_b, w_out=v_w_out, b_out=v_b_out,
                 ln_mix_g=v_ln_mix_g, ln_mix_b=v_ln_mix_b, w_gate=v_w_gate, w_up=v_w_up, w_down=v_w_down,
                 ln_ffn_g=v_ln_ffn_g, ln_ffn_b=v_ln_ffn_b)
    order = list(weights)
    big = ("w_in", "w_out", "w_gate", "w_up", "w_down")

    s_len = x.shape[1]
    xs = x.reshape(s_len, D_MODEL)
    tgt = loss_target.reshape(s_len, D_MODEL)
    pos_col = positions.reshape(s_len, 1)
    g0, b0 = ln_in_g.reshape(1, D_MODEL), ln_in_b.reshape(1, D_MODEL)
    sinks = attn_sinks.reshape(N_Q)
    sgu_w3 = sgu_w.reshape(N_GRP, BLK, BLK)
    sgu_bt = sgu_b.reshape(N_GRP, BLK).T

    shards = [weights[n][0] for n in big]
    gw_in, gw_out, gw_gate, gw_up, gw_down = _gather_weights(shards)
    w_in_full = jnp.concatenate([gw_in[j] for j in range(N_CHIP)], axis=1)
    w_out_full = gw_out.reshape(D_MODEL, D_MODEL)

    q, k, v, su, sv, tc, t1, t2 = _ln_inproj(xs, pos_col, g0, b0, w_in_full, b_in)
    mc = _mixer_fwd(q, k, v, su, sv, sinks, sgu_ln_g, sgu_ln_b, sgu_w3, sgu_bt)
    r1 = _outproj(mc, w_out_full, b_out, xs, g0, b0)
    gact, uact = _ffn_up(r1, ln_mix_g, ln_mix_b, gw_gate, gw_up)
    dr2, loss_cols, d_ln_ffn_g, d_ln_ffn_b = _ffn_down_loss(gact, uact, gw_down, r1, ln_mix_g, ln_mix_b,
                                                            ln_ffn_g, ln_ffn_b, tgt)
    loss = lax.psum(jnp.sum(loss_cols) * (0.5 / D_MODEL), MESH_AXES)

    dg, du, d_w_down = _ffn_bwd_a(dr2, gact, uact, gw_down)
    dr1, d_w_gate, d_w_up, d_ln_mix_g, d_ln_mix_b = _ffn_bwd_b(dr2, dg, du, r1, ln_mix_g, ln_mix_b, gw_gate, gw_up)
    dmc, d_w_out, d_b_out = _outproj_bwd(dr1, mc, w_out_full)
    (dq, dkv, dsuv, dbq, dbkv, dbsuv, d_sink, d_sgu_ln_g, d_sgu_ln_b, d_sgu_w, d_sgu_bt) = _mixer_bwd(
        q, k, v, su, sv, dmc, tc, t1, t2, sinks, sgu_ln_g, sgu_ln_b, sgu_w3, sgu_bt)
    grad_x, d_w_in, d_ln_in_g, d_ln_in_b = _inproj_bwd(dq, dkv, dsuv, dr1, xs, g0, b0, w_in_full)

    d_w_in_sh = jnp.stack([d_w_in[:, j * IN_SH:(j + 1) * IN_SH] for j in range(N_CHIP)])
    partial = [d_w_in_sh, d_w_out.reshape(N_CHIP, OUT_SH, D_MODEL), d_w_gate, d_w_up, d_w_down]
    pe = _pair_exchange(partial)
    mine, theirs = pe[:5], pe[5:]
    sums_wire, sums_f32 = [], []
    for t, name in enumerate(big):
        sw, sf = _elementwise("pair_sum_" + name, lambda a, b: (a + b, a + b), [mine[t], theirs[t]], [_WIRE, F32])
        sums_wire.append(sw)
        sums_f32.append(sf)
    ce = _chip_exchange(sums_wire, sums_f32)
    got, own = ce[:5], ce[5:]
    reduced_half = []
    for t, name in enumerate(big):
        (rh,) = _elementwise(
            "chip_sum_" + name,
            lambda o, a, b, c_: (((o + a.astype(F32)) + b.astype(F32)) + c_.astype(F32),),
            [own[t], got[t][0], got[t][1], got[t][2]], [F32])
        reduced_half.append(rh)
    swapped = _half_swap(reduced_half)
    grads = {name: swapped[t].reshape(weights[name].shape) for t, name in enumerate(big)}

    small_local = dict(
        ln_in_g=d_ln_in_g.reshape(ln_in_g.shape), ln_in_b=d_ln_in_b.reshape(ln_in_b.shape),
        b_in=jnp.concatenate([dbq, dbkv, dbsuv], axis=1), attn_sinks=d_sink[:, :N_Q],
        sgu_ln_g=d_sgu_ln_g, sgu_ln_b=d_sgu_ln_b, sgu_w=d_sgu_w.reshape(sgu_w.shape),
        sgu_b=d_sgu_bt[:, :N_GRP].T.reshape(sgu_b.shape), b_out=d_b_out,
        ln_mix_g=d_ln_mix_g, ln_mix_b=d_ln_mix_b, ln_ffn_g=d_ln_ffn_g, ln_ffn_b=d_ln_ffn_b)
    small_sum = _allreduce_small(_pack([small_local[n] for n in _SMALL]))
    for n, g in zip(_SMALL, _unpack(small_sum, [weights[n] for n in _SMALL])):
        grads[n] = g

    delta, new_m, new_v = {}, {}, {}
    for name in big:
        d_, m_, v_ = _adamw("adamw_" + name, weights[name], grads[name], mom_m[name], mom_v[name])
        delta[name], new_m[name], new_v[name] = d_, m_, v_
    packs = [_pack([src[n] for n in _SMALL]) for src in (weights, mom_m, mom_v)]
    d_, m_, v_ = _adamw("adamw_small", packs[0], small_sum, packs[1], packs[2], tile_rows=packs[0].shape[0])
    like = [weights[n] for n in _SMALL]
    for n, a, b, c_ in zip(_SMALL, _unpack(d_, like), _unpack(m_, like), _unpack(v_, like)):
        delta[n], new_m[n], new_v[n] = a, b, c_

    return (loss, grad_x.reshape(x.shape), *[grads[n] for n in order], *[delta[n] for n in order],
            *[new_m[n] for n in order], *[new_v[n] for n in order])
```

```python
import functools
import math

import jax
import jax.numpy as jnp
from jax import lax
from jax.experimental import pallas as pl
from jax.experimental.pallas import tpu as pltpu

F32 = jnp.float32
_MXU = jnp.bfloat16
_WIRE = jnp.bfloat16

D_MODEL = 1024
ATTN_W = 512
SGU_W = 512
HEAD_DIM = 64
N_Q = 8
N_KV = 2
Q_PER_KV = 4
KV_W = 128
BLK = 128
ROT_DIM = 16
ROPE_THETA = 500000.0
N_GRP = 4
GRP_DIM = 128
D_FF = 2816
IN_W = 1792
LN_EPS = 1e-5
ALPHA = 2.0 ** 0.25
N_CHIP = 4
FF_SH = D_FF // N_CHIP
IN_SH = IN_W // N_CHIP
OUT_SH = D_MODEL // N_CHIP
ROW_CHUNK = 32

ADAM_LR = 0.001
ADAM_B1 = 0.9
ADAM_B2 = 0.999
ADAM_EPS = 1e-08
ADAM_WD = 0.01
ADAM_STEP = 10

SQRT_HALF = 0.7071067811865476
INV_SQRT_2PI = 0.3989422804014327
MESH_AXES = ("x", "y", "c")
MESH = pl.DeviceIdType.MESH
MIB = 2 ** 20


def _vmem():
    return pl.BlockSpec(memory_space=pltpu.VMEM)


def _smem():
    return pl.BlockSpec(memory_space=pltpu.SMEM)


def _hbm():
    return pl.BlockSpec(memory_space=pl.ANY)


def _params(vmem_mib=48):
    return pltpu.CompilerParams(dimension_semantics=("arbitrary",), vmem_limit_bytes=vmem_mib * MIB)


def _tile(n, cap):
    if n <= cap:
        return n
    for t in range(cap - cap % 16, 0, -16):
        if n % t == 0:
            return t
    raise ValueError((n, cap))


def _rows(tm, width):
    return pl.BlockSpec((tm, width), lambda i: (i, 0))


def _const2(shape):
    return pl.BlockSpec(shape, lambda i: (0,) * len(shape))


def _ln(x, g, b):
    mu = jnp.mean(x, axis=-1, keepdims=True)
    xc = x - mu
    var = jnp.mean(xc * xc, axis=-1, keepdims=True)
    rstd = lax.rsqrt(var + LN_EPS)
    xhat = xc * rstd
    return xhat * g + b, xhat, rstd


def _ln_bwd(dy, xhat, rstd, g):
    gdy = dy * g
    m1 = jnp.mean(gdy, axis=-1, keepdims=True)
    m2 = jnp.mean(gdy * xhat, axis=-1, keepdims=True)
    return rstd * (gdy - m1 - xhat * m2)


def _colsum(a):
    return jnp.sum(a, axis=0, keepdims=True)


def _gelu(x):
    return 0.5 * x * (1.0 + lax.erf(x * SQRT_HALF))


def _gelu_grad(x):
    return 0.5 * (1.0 + lax.erf(x * SQRT_HALF)) + x * jnp.exp(-0.5 * x * x) * INV_SQRT_2PI


def _dot(a, b):
    return jnp.dot(a, b, preferred_element_type=F32)


def _dot_nt(a, b):
    return lax.dot_general(a, b, (((1,), (1,)), ((), ())), preferred_element_type=F32)


def _dot_tn(a, b):
    return lax.dot_general(a, b, (((0,), (0,)), ((), ())), preferred_element_type=F32)


def _rope(t, tc, t1, t2):
    n = t.shape[1]
    rep = n // 128
    if rep > 1:
        tc, t1, t2 = (jnp.tile(a, (1, rep)) for a in (tc, t1, t2))
    return t * tc + pltpu.roll(t, n - 8, 1) * t1 + pltpu.roll(t, 8, 1) * t2


def _rope_bwd(d, tc, t1, t2):
    n = d.shape[1]
    rep = n // 128
    if rep > 1:
        tc, t1, t2 = (jnp.tile(a, (1, rep)) for a in (tc, t1, t2))
    return d * tc + pltpu.roll(d * t1, 8, 1) + pltpu.roll(d * t2, n - 8, 1)


def _band_mask(first_block):
    qi = lax.broadcasted_iota(jnp.int32, (BLK, 2 * BLK), 0)
    kj = lax.broadcasted_iota(jnp.int32, (BLK, 2 * BLK), 1)
    shut = jnp.where(first_block, 2 * BLK, 0)
    prev_ok = jnp.logical_and(kj < BLK, kj > qi + shut)
    cur_ok = jnp.logical_and(kj >= BLK, (kj - BLK) <= qi)
    return jnp.logical_or(prev_ok, cur_ok)


def _causal_w(w_ref, h):
    t = lax.broadcasted_iota(jnp.int32, (BLK, BLK), 0)
    s = lax.broadcasted_iota(jnp.int32, (BLK, BLK), 1)
    return jnp.where(s <= t, w_ref[h], 0.0)


def _lane_put(vals, width):
    rows = vals[0].shape[0]
    lane = lax.broadcasted_iota(jnp.int32, (rows, width), 1)
    out = jnp.zeros((rows, width), F32)
    for k, v in enumerate(vals):
        out = out + jnp.where(lane == k, v, 0.0)
    return out


def _rope_consts():
    lane = jnp.arange(128) % HEAD_DIM
    inv_freq = ROPE_THETA ** (-jnp.arange(0, ROT_DIM, 2, dtype=F32) / ROT_DIM)
    rot = lane < ROT_DIM
    freq = jnp.where(rot, inv_freq[lane % (ROT_DIM // 2)], 0.0)
    rows = [freq, rot.astype(F32), 1.0 - rot.astype(F32), (lane < ROT_DIM // 2).astype(F32),
            jnp.logical_and(lane >= ROT_DIM // 2, rot).astype(F32)]
    rows += [jnp.zeros((128,), F32)] * 3
    return jnp.stack(rows).astype(F32)


def _ln_inproj(x, pos_col, g0, b0, w_in, b_in):
    s_len = x.shape[0]
    tm = _tile(s_len, 512)

    def body(x_ref, pos_ref, g_ref, b_ref, w_ref, bi_ref, rc_ref,
             q_ref, k_ref, v_ref, su_ref, sv_ref, tc_ref, t1_ref, t2_ref):
        h0, _, _ = _ln(x_ref[...], g_ref[...], b_ref[...])
        proj = _dot(h0.astype(_MXU), w_ref[...]) + bi_ref[...]
        ang = pos_ref[...].astype(F32) * rc_ref[0:1, :]
        cs = jnp.cos(ang)
        sn = jnp.sin(ang)
        tc = cs * rc_ref[1:2, :] + rc_ref[2:3, :]
        t1 = -sn * rc_ref[3:4, :]
        t2 = sn * rc_ref[4:5, :]
        tc_ref[...] = tc
        t1_ref[...] = t1
        t2_ref[...] = t2
        q = _rope(proj[:, 0:ATTN_W], tc, t1, t2) * (HEAD_DIM ** -0.5)
        q_ref[...] = q.astype(_MXU)
        k_ref[...] = _rope(proj[:, ATTN_W:ATTN_W + KV_W], tc, t1, t2).astype(_MXU)
        v_ref[...] = proj[:, ATTN_W + KV_W:ATTN_W + 2 * KV_W].astype(_MXU)
        su_ref[...] = proj[:, ATTN_W + 2 * KV_W:ATTN_W + 2 * KV_W + SGU_W]
        sv_ref[...] = proj[:, ATTN_W + 2 * KV_W + SGU_W:IN_W]

    sd = jax.ShapeDtypeStruct
    return pl.pallas_call(
        body, name="ln_inproj", grid=(s_len // tm,),
        in_specs=[_rows(tm, D_MODEL), _rows(tm, 1), _const2((1, D_MODEL)), _const2((1, D_MODEL)), _vmem(),
                  _const2((1, IN_W)), _const2((8, 128))],
        out_specs=[_rows(tm, ATTN_W), _rows(tm, KV_W), _rows(tm, KV_W), _rows(tm, SGU_W), _rows(tm, SGU_W),
                   _rows(tm, 128), _rows(tm, 128), _rows(tm, 128)],
        out_shape=[sd((s_len, ATTN_W), _MXU), sd((s_len, KV_W), _MXU), sd((s_len, KV_W), _MXU),
                   sd((s_len, SGU_W), F32), sd((s_len, SGU_W), F32),
                   sd((s_len, 128), F32), sd((s_len, 128), F32), sd((s_len, 128), F32)],
        compiler_params=_params(48),
    )(x, pos_col, g0, b0, w_in, b_in, _rope_consts())


def _attn_probs(qh, kh, sink, allowed):
    s = jnp.where(allowed, _dot_nt(qh, kh), -1e30)
    m = jnp.maximum(jnp.max(s, axis=-1, keepdims=True), sink)
    p = jnp.exp(s - m)
    ps = jnp.exp(sink - m)
    inv = 1.0 / (jnp.sum(p, axis=-1, keepdims=True) + ps)
    return p * inv, ps * inv


def _sgu_fwd(su, sv, lg, lb, w_ref, bt_ref):
    u = _gelu(su)
    vv, vhat, rstd = _ln(_gelu(sv), lg, lb)
    vvb = vv.astype(_MXU)
    wcs, mixed = [], []
    for h in range(N_GRP):
        wc = _causal_w(w_ref, h).astype(_MXU)
        wcs.append(wc)
        mixed.append(_dot(wc, vvb[:, h * GRP_DIM:(h + 1) * GRP_DIM]) + bt_ref[:, h:h + 1])
    return u, vhat, rstd, vvb, wcs, jnp.concatenate(mixed, axis=1)


def _prev_map(i):
    return (jnp.maximum(i - 1, 0), 0)


def _mixer_fwd(q, k, v, su, sv, sinks, sg, sb, sgu_w, sgu_bt):
    s_len = q.shape[0]
    nb = s_len // BLK

    def body(q_ref, kc_ref, kp_ref, vc_ref, vp_ref, su_ref, sv_ref, sink_ref, lg_ref, lb_ref, w_ref, bt_ref, mc_ref):
        i = pl.program_id(0)
        allowed = _band_mask(i == 0)
        kb = jnp.concatenate([kp_ref[...], kc_ref[...]], axis=0)
        vb = jnp.concatenate([vp_ref[...], vc_ref[...]], axis=0)
        qv = q_ref[...]
        outs = []
        for h in range(N_Q):
            g = h // Q_PER_KV
            kh = kb[:, g * HEAD_DIM:(g + 1) * HEAD_DIM]
            vh = vb[:, g * HEAD_DIM:(g + 1) * HEAD_DIM]
            probs, _ = _attn_probs(qv[:, h * HEAD_DIM:(h + 1) * HEAD_DIM], kh, sink_ref[h], allowed)
            outs.append(_dot(probs.astype(_MXU), vh))
        u, _, _, _, _, mixed = _sgu_fwd(su_ref[...], sv_ref[...], lg_ref[...], lb_ref[...], w_ref, bt_ref)
        mc_ref[...] = jnp.concatenate(outs + [u * mixed], axis=1).astype(_MXU)

    cur = lambda w: pl.BlockSpec((BLK, w), lambda i: (i, 0))
    prev = lambda w: pl.BlockSpec((BLK, w), _prev_map)
    return pl.pallas_call(
        body, name="mixer_fwd", grid=(nb,),
        in_specs=[cur(ATTN_W), cur(KV_W), prev(KV_W), cur(KV_W), prev(KV_W), cur(SGU_W), cur(SGU_W), _smem(),
                  _const2((1, SGU_W)), _const2((1, SGU_W)), _const2((N_GRP, BLK, BLK)), _const2((BLK, N_GRP))],
        out_specs=cur(D_MODEL),
        out_shape=jax.ShapeDtypeStruct((s_len, D_MODEL), _MXU),
        compiler_params=_params(32),
    )(q, k, k, v, v, su, sv, sinks, sg, sb, sgu_w, sgu_bt)


def _outproj(mc, w_out, b_out, x, g0, b0):
    s_len = x.shape[0]
    tm = _tile(s_len, 512)

    def body(mc_ref, w_ref, bo_ref, x_ref, g_ref, b_ref, r1_ref):
        h0, _, _ = _ln(x_ref[...], g_ref[...], b_ref[...])
        r1_ref[...] = ALPHA * h0 + (_dot(mc_ref[...], w_ref[...]) + bo_ref[...])

    return pl.pallas_call(
        body, name="outproj", grid=(s_len // tm,),
        in_specs=[_rows(tm, D_MODEL), _vmem(), _const2((1, D_MODEL)), _rows(tm, D_MODEL),
                  _const2((1, D_MODEL)), _const2((1, D_MODEL))],
        out_specs=_rows(tm, D_MODEL),
        out_shape=jax.ShapeDtypeStruct((s_len, D_MODEL), F32),
        compiler_params=_params(32),
    )(mc, w_out, b_out, x, g0, b0)


def _ffn_spec(tm):
    return pl.BlockSpec((N_CHIP, tm, FF_SH), lambda i: (0, i, 0))


def _ffn_up(r1, g1, b1, wg, wu):
    s_len = r1.shape[0]
    tm = _tile(s_len, 256)

    def body(r1_ref, g_ref, b_ref, wg_ref, wu_ref, go_ref, uo_ref):
        h1, _, _ = _ln(r1_ref[...], g_ref[...], b_ref[...])
        h1b = h1.astype(_MXU)
        for j in range(N_CHIP):
            go_ref[j] = _dot(h1b, wg_ref[j])
            uo_ref[j] = _dot(h1b, wu_ref[j])

    sd = jax.ShapeDtypeStruct((N_CHIP, s_len, FF_SH), F32)
    return pl.pallas_call(
        body, name="ffn_up", grid=(s_len // tm,),
        in_specs=[_rows(tm, D_MODEL), _const2((1, D_MODEL)), _const2((1, D_MODEL)), _vmem(), _vmem()],
        out_specs=[_ffn_spec(tm), _ffn_spec(tm)],
        out_shape=[sd, sd],
        compiler_params=_params(56),
    )(r1, g1, b1, wg, wu)


def _silu_parts(g):
    sg = 1.0 / (1.0 + jnp.exp(-g))
    return g * sg, sg


def _ffn_down_loss(gact, uact, wd, r1, g1, b1, g2, b2, target):
    s_len = r1.shape[0]
    tm = _tile(s_len, 256)

    def body(g_ref, u_ref, wd_ref, r1_ref, g1_ref, b1_ref, g2_ref, b2_ref, t_ref,
             dr2_ref, loss_ref, dg2_ref, db2_ref):
        i = pl.program_id(0)
        f = jnp.zeros((tm, D_MODEL), F32)
        for j in range(N_CHIP):
            silu, _ = _silu_parts(g_ref[j])
            f = f + _dot((silu * u_ref[j]).astype(_MXU), wd_ref[j])
        h1, _, _ = _ln(r1_ref[...], g1_ref[...], b1_ref[...])
        h2, r2hat, rstd2 = _ln(ALPHA * h1 + f, g2_ref[...], b2_ref[...])
        diff = h2 - t_ref[...]
        dh2 = diff * (1.0 / D_MODEL)

        @pl.when(i == 0)
        def _():
            loss_ref[...] = jnp.zeros_like(loss_ref)
            dg2_ref[...] = jnp.zeros_like(dg2_ref)
            db2_ref[...] = jnp.zeros_like(db2_ref)

        loss_ref[...] += _colsum(diff * diff)
        dg2_ref[...] += _colsum(dh2 * r2hat)
        db2_ref[...] += _colsum(dh2)
        dr2_ref[...] = _ln_bwd(dh2, r2hat, rstd2, g2_ref[...])

    vec = jax.ShapeDtypeStruct((1, D_MODEL), F32)
    c = _const2((1, D_MODEL))
    return pl.pallas_call(
        body, name="ffn_down_loss", grid=(s_len // tm,),
        in_specs=[_ffn_spec(tm), _ffn_spec(tm), _vmem(), _rows(tm, D_MODEL), c, c, c, c, _rows(tm, D_MODEL)],
        out_specs=[_rows(tm, D_MODEL), c, c, c],
        out_shape=[jax.ShapeDtypeStruct((s_len, D_MODEL), F32), vec, vec, vec],
        compiler_params=_params(48),
    )(gact, uact, wd, r1, g1, b1, g2, b2, target)


def _ffn_bwd_a(dr2, gact, uact, wd):
    s_len = dr2.shape[0]
    tm = _tile(s_len, 256)

    def body(dr2_ref, g_ref, u_ref, wd_ref, dg_ref, du_ref, dwd_ref, wire_ref, land_ref, send_sem, recv_sem):
        i = pl.program_id(0)

        @pl.when(i == 0)
        def _():
            dwd_ref[...] = jnp.zeros_like(dwd_ref)

        dfb = dr2_ref[...].astype(_MXU)
        for j in range(N_CHIP):
            g = g_ref[j]
            u = u_ref[j]
            silu, sg = _silu_parts(g)
            da = _dot_nt(dfb, wd_ref[j])
            dg_ref[j] = (da * u * (sg * (1.0 + g * (1.0 - sg)))).astype(_MXU)
            du_ref[j] = (da * silu).astype(_MXU)
            dwd_ref[j] += _dot_tn((silu * u).astype(_MXU), dfb)

        @pl.when(i == pl.num_programs(0) - 1)
        def _():
            _pair_reduce(dwd_ref, land_ref, wire_ref, send_sem, recv_sem)

    sd = jax.ShapeDtypeStruct((N_CHIP, s_len, FF_SH), _MXU)
    return pl.pallas_call(
        body, name="ffn_bwd_a", grid=(s_len // tm,),
        in_specs=[_rows(tm, D_MODEL), _ffn_spec(tm), _ffn_spec(tm), _vmem()],
        out_specs=[_ffn_spec(tm), _ffn_spec(tm), _vmem(), _vmem()],
        out_shape=[sd, sd, jax.ShapeDtypeStruct((N_CHIP, FF_SH, D_MODEL), F32),
                   jax.ShapeDtypeStruct((N_CHIP, FF_SH // 2, D_MODEL), _WIRE)],
        scratch_shapes=_pair_scratch((N_CHIP, FF_SH // 2, D_MODEL)),
        compiler_params=_params(58),
    )(dr2, gact, uact, wd)


def _ffn_bwd_g(dr2, dg, r1, g1, b1, wg):
    s_len = dr2.shape[0]
    tm = _tile(s_len, 256)

    def body(dr2_ref, dg_ref, r1_ref, g1_ref, b1_ref, wg_ref, dh1_ref, dwg_ref, wire_ref, land_ref, send_sem, recv_sem):
        i = pl.program_id(0)

        @pl.when(i == 0)
        def _():
            dwg_ref[...] = jnp.zeros_like(dwg_ref)

        h1, _, _ = _ln(r1_ref[...], g1_ref[...], b1_ref[...])
        h1b = h1.astype(_MXU)
        dh1 = ALPHA * dr2_ref[...]
        for j in range(N_CHIP):
            dgj = dg_ref[j]
            dh1 = dh1 + _dot_nt(dgj, wg_ref[j])
            dwg_ref[j] += _dot_tn(h1b, dgj)
        dh1_ref[...] = dh1

        @pl.when(i == pl.num_programs(0) - 1)
        def _():
            _pair_reduce(dwg_ref, land_ref, wire_ref, send_sem, recv_sem)

    c = _const2((1, D_MODEL))
    return pl.pallas_call(
        body, name="ffn_bwd_g", grid=(s_len // tm,),
        in_specs=[_rows(tm, D_MODEL), _ffn_spec(tm), _rows(tm, D_MODEL), c, c, _vmem()],
        out_specs=[_rows(tm, D_MODEL), _vmem(), _vmem()],
        out_shape=[jax.ShapeDtypeStruct((s_len, D_MODEL), F32), jax.ShapeDtypeStruct((N_CHIP, D_MODEL, FF_SH), F32),
                   jax.ShapeDtypeStruct((N_CHIP, D_MODEL // 2, FF_SH), _WIRE)],
        scratch_shapes=_pair_scratch((N_CHIP, D_MODEL // 2, FF_SH)),
        compiler_params=_params(56),
    )(dr2, dg, r1, g1, b1, wg)


def _ffn_bwd_u(dh1a, du, r1, g1, b1, wu):
    s_len = dh1a.shape[0]
    tm = _tile(s_len, 256)

    def body(dh1_ref, du_ref, r1_ref, g1_ref, b1_ref, wu_ref,
             dr1_ref, dwu_ref, wire_ref, dg1_ref, db1_ref, land_ref, send_sem, recv_sem):
        i = pl.program_id(0)

        @pl.when(i == 0)
        def _():
            dwu_ref[...] = jnp.zeros_like(dwu_ref)
            dg1_ref[...] = jnp.zeros_like(dg1_ref)
            db1_ref[...] = jnp.zeros_like(db1_ref)

        h1, r1hat, rstd1 = _ln(r1_ref[...], g1_ref[...], b1_ref[...])
        h1b = h1.astype(_MXU)
        dh1 = dh1_ref[...]
        for j in range(N_CHIP):
            duj = du_ref[j]
            dh1 = dh1 + _dot_nt(duj, wu_ref[j])
            dwu_ref[j] += _dot_tn(h1b, duj)
        dg1_ref[...] += _colsum(dh1 * r1hat)
        db1_ref[...] += _colsum(dh1)
        dr1_ref[...] = _ln_bwd(dh1, r1hat, rstd1, g1_ref[...])

        @pl.when(i == pl.num_programs(0) - 1)
        def _():
            _pair_reduce(dwu_ref, land_ref, wire_ref, send_sem, recv_sem)

    vec = jax.ShapeDtypeStruct((1, D_MODEL), F32)
    c = _const2((1, D_MODEL))
    return pl.pallas_call(
        body, name="ffn_bwd_u", grid=(s_len // tm,),
        in_specs=[_rows(tm, D_MODEL), _ffn_spec(tm), _rows(tm, D_MODEL), c, c, _vmem()],
        out_specs=[_rows(tm, D_MODEL), _vmem(), _vmem(), c, c],
        out_shape=[jax.ShapeDtypeStruct((s_len, D_MODEL), F32), jax.ShapeDtypeStruct((N_CHIP, D_MODEL, FF_SH), F32),
                   jax.ShapeDtypeStruct((N_CHIP, D_MODEL // 2, FF_SH), _WIRE), vec, vec],
        scratch_shapes=_pair_scratch((N_CHIP, D_MODEL // 2, FF_SH)),
        compiler_params=_params(56),
    )(dh1a, du, r1, g1, b1, wu)


def _outproj_bwd(dr1, mc, w_out):
    s_len = dr1.shape[0]
    tm = _tile(s_len, 512)

    def body(dr1_ref, mc_ref, w_ref, dmc_ref, dw_ref, wire_ref, db_ref, land_ref, send_sem, recv_sem):
        i = pl.program_id(0)

        @pl.when(i == 0)
        def _():
            dw_ref[...] = jnp.zeros_like(dw_ref)
            db_ref[...] = jnp.zeros_like(db_ref)

        d = dr1_ref[...]
        db_ref[...] += _colsum(d)
        db16 = d.astype(_MXU)
        dmc_ref[...] = _dot_nt(db16, w_ref[...])
        mc = mc_ref[...]
        for j in range(N_CHIP):
            dw_ref[j] += _dot_tn(mc[:, j * OUT_SH:(j + 1) * OUT_SH], db16)

        @pl.when(i == pl.num_programs(0) - 1)
        def _():
            _pair_reduce(dw_ref, land_ref, wire_ref, send_sem, recv_sem)

    return pl.pallas_call(
        body, name="outproj_bwd", grid=(s_len // tm,),
        in_specs=[_rows(tm, D_MODEL), _rows(tm, D_MODEL), _vmem()],
        out_specs=[_rows(tm, D_MODEL), _vmem(), _vmem(), _const2((1, D_MODEL))],
        out_shape=[jax.ShapeDtypeStruct((s_len, D_MODEL), F32), jax.ShapeDtypeStruct((N_CHIP, OUT_SH, D_MODEL), F32),
                   jax.ShapeDtypeStruct((N_CHIP, OUT_SH // 2, D_MODEL), _WIRE), jax.ShapeDtypeStruct((1, D_MODEL), F32)],
        scratch_shapes=_pair_scratch((N_CHIP, OUT_SH // 2, D_MODEL)),
        compiler_params=_params(40),
    )(dr1, mc, w_out)


def _mixer_bwd(q, k, v, su, sv, dmc, tc, t1, t2, sinks, sg, sb, sgu_w, sgu_bt):
    s_len = q.shape[0]
    nb = s_len // BLK

    def body(q_ref, kc_ref, kp_ref, vc_ref, vp_ref, su_ref, sv_ref, dmc_ref,
             tc_ref, t1_ref, t2_ref, tcp_ref, t1p_ref, t2p_ref,
             sink_ref, lg_ref, lb_ref, w_ref, bt_ref,
             dq_ref, dkv_ref, dsuv_ref, dbq_ref, dbkv_ref, dbsuv_ref,
             dsink_ref, dlg_ref, dlb_ref, dw_ref, dbt_ref, carry_ref):
        i = pl.program_id(0)

        @pl.when(i == 0)
        def _():
            for r in (dbq_ref, dbkv_ref, dbsuv_ref, dsink_ref, dlg_ref, dlb_ref, dw_ref, dbt_ref):
                r[...] = jnp.zeros_like(r)

        def emit_kv(fin):
            dk = _rope_bwd(fin[:, 0:KV_W], tcp_ref[...], t1p_ref[...], t2p_ref[...])
            out = jnp.concatenate([dk, fin[:, KV_W:2 * KV_W]], axis=1)
            dkv_ref[...] = out.astype(_MXU)
            dbkv_ref[...] += _colsum(out)

        @pl.when(i < nb)
        def _():
            allowed = _band_mask(i == 0)
            kb = jnp.concatenate([kp_ref[...], kc_ref[...]], axis=0)
            vb = jnp.concatenate([vp_ref[...], vc_ref[...]], axis=0)
            qv = q_ref[...]
            dmc = dmc_ref[...]
            dqs, dks, dvs, dsinks = [], [], [], []
            for g in range(N_KV):
                kh = kb[:, g * HEAD_DIM:(g + 1) * HEAD_DIM]
                vh = vb[:, g * HEAD_DIM:(g + 1) * HEAD_DIM]
                dk_g = jnp.zeros((2 * BLK, HEAD_DIM), F32)
                dv_g = jnp.zeros((2 * BLK, HEAD_DIM), F32)
                for hh in range(Q_PER_KV):
                    h = g * Q_PER_KV + hh
                    qh = qv[:, h * HEAD_DIM:(h + 1) * HEAD_DIM]
                    probs, psink = _attn_probs(qh, kh, sink_ref[h], allowed)
                    pb = probs.astype(_MXU)
                    dob = dmc[:, h * HEAD_DIM:(h + 1) * HEAD_DIM].astype(_MXU)
                    dv_g = dv_g + _dot_tn(pb, dob)
                    dp = _dot_nt(dob, vh)
                    rd = jnp.sum(probs * dp, axis=-1, keepdims=True)
                    dsb = (probs * (dp - rd)).astype(_MXU)
                    dsinks.append(-jnp.sum(psink * rd, axis=0, keepdims=True))
                    dqs.append(_dot(dsb, kh))
                    dk_g = dk_g + _dot_tn(dsb, qh)
                dks.append(dk_g)
                dvs.append(dv_g)
            dq = _rope_bwd(jnp.concatenate(dqs, axis=1) * (HEAD_DIM ** -0.5), tc_ref[...], t1_ref[...], t2_ref[...])
            dq_ref[...] = dq.astype(_MXU)
            dbq_ref[...] += _colsum(dq)
            dsink_ref[...] += _lane_put(dsinks, 128)
            contrib = jnp.concatenate(dks + dvs, axis=1)

            @pl.when(i > 0)
            def _():
                emit_kv(carry_ref[...] + contrib[0:BLK, :])

            carry_ref[...] = contrib[BLK:2 * BLK, :]

            su = su_ref[...]
            sv = sv_ref[...]
            lg = lg_ref[...]
            u, vhat, rstd, vvb, wcs, mixed = _sgu_fwd(su, sv, lg, lb_ref[...], w_ref, bt_ref)
            dsgu = dmc[:, ATTN_W:D_MODEL]
            dsu = dsgu * mixed * _gelu_grad(su)
            dmixed = dsgu * u
            tri_t = lax.broadcasted_iota(jnp.int32, (BLK, BLK), 0)
            tri_s = lax.broadcasted_iota(jnp.int32, (BLK, BLK), 1)
            dvv, dbs = [], []
            for h in range(N_GRP):
                dm = dmixed[:, h * GRP_DIM:(h + 1) * GRP_DIM]
                dmb = dm.astype(_MXU)
                dbs.append(jnp.sum(dm, axis=1, keepdims=True))
                dw_ref[h] += jnp.where(tri_s <= tri_t, _dot_nt(dmb, vvb[:, h * GRP_DIM:(h + 1) * GRP_DIM]), 0.0)
                dvv.append(_dot_tn(wcs[h], dmb))
            dvv = jnp.concatenate(dvv, axis=1)
            dbt_ref[...] += _lane_put(dbs, 128)
            dlg_ref[...] += _colsum(dvv * vhat)
            dlb_ref[...] += _colsum(dvv)
            dsv = _ln_bwd(dvv, vhat, rstd, lg) * _gelu_grad(sv)
            dsuv = jnp.concatenate([dsu, dsv], axis=1)
            dsuv_ref[...] = dsuv.astype(_MXU)
            dbsuv_ref[...] += _colsum(dsuv)

        @pl.when(i == nb)
        def _():
            emit_kv(carry_ref[...])

    last = nb - 1
    cur = lambda w: pl.BlockSpec((BLK, w), lambda i: (jnp.minimum(i, last), 0))
    prev = lambda w: pl.BlockSpec((BLK, w), lambda i: (jnp.clip(i - 1, 0, last), 0))
    sd = jax.ShapeDtypeStruct
    return pl.pallas_call(
        body, name="mixer_bwd", grid=(nb + 1,),
        in_specs=[cur(ATTN_W), cur(KV_W), prev(KV_W), cur(KV_W), prev(KV_W), cur(SGU_W), cur(SGU_W), cur(D_MODEL),
                  cur(128), cur(128), cur(128), prev(128), prev(128), prev(128),
                  _smem(), _const2((1, SGU_W)), _const2((1, SGU_W)), _const2((N_GRP, BLK, BLK)), _const2((BLK, N_GRP))],
        out_specs=[cur(ATTN_W), prev(2 * KV_W), cur(2 * SGU_W),
                   _const2((1, ATTN_W)), _const2((1, 2 * KV_W)), _const2((1, 2 * SGU_W)),
                   _const2((1, 128)), _const2((1, SGU_W)), _const2((1, SGU_W)),
                   _const2((N_GRP, BLK, BLK)), _const2((BLK, 128))],
        out_shape=[sd((s_len, ATTN_W), _MXU), sd((s_len, 2 * KV_W), _MXU), sd((s_len, 2 * SGU_W), _MXU),
                   sd((1, ATTN_W), F32), sd((1, 2 * KV_W), F32), sd((1, 2 * SGU_W), F32),
                   sd((1, 128), F32), sd((1, SGU_W), F32), sd((1, SGU_W), F32),
                   sd((N_GRP, BLK, BLK), F32), sd((BLK, 128), F32)],
        scratch_shapes=[pltpu.VMEM((BLK, 2 * KV_W), F32)],
        compiler_params=_params(32),
    )(q, k, k, v, v, su, sv, dmc, tc, t1, t2, tc, t1, t2, sinks, sg, sb, sgu_w, sgu_bt)


def _inproj_bwd(dq, dkv, dsuv, dr1, x, g0, b0, w_in):
    s_len = x.shape[0]
    tm = _tile(s_len, 512)
    cuts = ((0, ATTN_W), (ATTN_W, ATTN_W + 2 * KV_W), (ATTN_W + 2 * KV_W, IN_W))

    def body(dq_ref, dkv_ref, dsuv_ref, dr1_ref, x_ref, g_ref, b_ref, w_ref,
             dx_ref, sh_ref, wire_ref, dg_ref, db_ref, dw_ref, land_ref, send_sem, recv_sem):
        i = pl.program_id(0)

        @pl.when(i == 0)
        def _():
            dw_ref[...] = jnp.zeros_like(dw_ref)
            dg_ref[...] = jnp.zeros_like(dg_ref)
            db_ref[...] = jnp.zeros_like(db_ref)

        h0, xhat, rstd = _ln(x_ref[...], g_ref[...], b_ref[...])
        h0b = h0.astype(_MXU)
        dh0 = ALPHA * dr1_ref[...]
        for (lo, hi), d_ref in zip(cuts, (dq_ref, dkv_ref, dsuv_ref)):
            d = d_ref[...]
            dh0 = dh0 + _dot_nt(d, w_ref[:, lo:hi])
            dw_ref[:, lo:hi] += _dot_tn(h0b, d)
        dg_ref[...] += _colsum(dh0 * xhat)
        db_ref[...] += _colsum(dh0)
        dx_ref[...] = _ln_bwd(dh0, xhat, rstd, g_ref[...])

        @pl.when(i == pl.num_programs(0) - 1)
        def _():
            def regroup(r, carry):
                rows = pl.ds(pl.multiple_of(r * ROW_CHUNK, ROW_CHUNK), ROW_CHUNK)
                for j in range(N_CHIP):
                    sh_ref[j, rows, :] = dw_ref[rows, j * IN_SH:(j + 1) * IN_SH]
                return carry

            lax.fori_loop(0, D_MODEL // ROW_CHUNK, regroup, 0)
            _pair_reduce(sh_ref, land_ref, wire_ref, send_sem, recv_sem)

    vec = jax.ShapeDtypeStruct((1, D_MODEL), F32)
    c = _const2((1, D_MODEL))
    return pl.pallas_call(
        body, name="inproj_bwd", grid=(s_len // tm,),
        in_specs=[_rows(tm, ATTN_W), _rows(tm, 2 * KV_W), _rows(tm, 2 * SGU_W), _rows(tm, D_MODEL), _rows(tm, D_MODEL),
                  c, c, _vmem()],
        out_specs=[_rows(tm, D_MODEL), _vmem(), _vmem(), c, c],
        out_shape=[jax.ShapeDtypeStruct((s_len, D_MODEL), F32), jax.ShapeDtypeStruct((N_CHIP, D_MODEL, IN_SH), F32),
                   jax.ShapeDtypeStruct((N_CHIP, D_MODEL // 2, IN_SH), _WIRE), vec, vec],
        scratch_shapes=[pltpu.VMEM((D_MODEL, IN_W), F32)] + _pair_scratch((N_CHIP, D_MODEL // 2, IN_SH)),
        compiler_params=_params(56),
    )(dq, dkv, dsuv, dr1, x, g0, b0, w_in)


def _place():
    x, y, c = (lax.axis_index(a) for a in MESH_AXES)
    chips = [(1 - x, y), (x, 1 - y), (1 - x, 1 - y)]
    return x, y, c, chips


def _gather_weights(shards):
    n = len(shards)
    halves = [s.shape[0] // 2 for s in shards]

    def body(*refs):
        ins, outs = refs[:n], refs[n:2 * n]
        send_sems, recv_sems = refs[2 * n], refs[2 * n + 1]
        x, y, c, chips = _place()
        me = 2 * x + y

        def piece(t, slot, half):
            return outs[t].at[slot, pl.ds(pl.multiple_of(half * halves[t], 16), halves[t]), :]

        def copy(k, t, slot, half, to):
            return pltpu.make_async_remote_copy(
                src_ref=piece(t, slot, half), dst_ref=piece(t, slot, half),
                send_sem=send_sems.at[k], recv_sem=recv_sems.at[k], device_id=to, device_id_type=MESH)

        for t in range(n):
            outs[t][me] = ins[t][...].astype(_WIRE)
        started = []
        for t in range(n):
            for d, chip in enumerate(chips):
                cp = copy(3 * t + d, t, me, c, (chip[0], chip[1], c))
                cp.start()
                started.append(cp)
        for t in range(n):
            for d, chip in enumerate(chips):
                slot = 2 * chip[0] + chip[1]
                copy(3 * t + d, t, slot, c, (chip[0], chip[1], c)).wait_recv()
                fwd = copy(3 * n + 3 * t + d, t, slot, c, (x, y, 1 - c))
                fwd.start()
                started.append(fwd)
        for t in range(n):
            for d, chip in enumerate(chips):
                slot = 2 * chip[0] + chip[1]
                copy(3 * n + 3 * t + d, t, slot, 1 - c, (x, y, 1 - c)).wait_recv()
        for cp in started:
            cp.wait_send()

    return pl.pallas_call(
        body, name="gather_weights",
        in_specs=[_vmem()] * n, out_specs=[_vmem()] * n,
        out_shape=[jax.ShapeDtypeStruct((N_CHIP,) + s.shape, _WIRE) for s in shards],
        scratch_shapes=[pltpu.SemaphoreType.DMA((6 * n,)), pltpu.SemaphoreType.DMA((6 * n,))],
        compiler_params=pltpu.CompilerParams(vmem_limit_bytes=56 * MIB),
    )(*shards)


def _pair_scratch(half_shape):
    return [pltpu.VMEM(half_shape, F32), pltpu.SemaphoreType.DMA, pltpu.SemaphoreType.DMA]


def _pair_reduce(acc_ref, land_ref, wire_ref, send_sem, recv_sem):
    rh = land_ref.shape[1]
    x, y, c, _ = _place()
    give = acc_ref.at[:, pl.ds(pl.multiple_of((1 - c) * rh, 8), rh), :]
    cp = pltpu.make_async_remote_copy(src_ref=give, dst_ref=land_ref, send_sem=send_sem, recv_sem=recv_sem,
                                      device_id=(x, y, 1 - c), device_id_type=MESH)
    cp.start()
    cp.wait()

    def chunk(r, carry):
        theirs = pl.ds(pl.multiple_of(r * ROW_CHUNK, ROW_CHUNK), ROW_CHUNK)
        mine = pl.ds(pl.multiple_of(c * rh + r * ROW_CHUNK, 8), ROW_CHUNK)
        for j in range(N_CHIP):
            s = acc_ref[j, mine, :] + land_ref[j, theirs, :]
            acc_ref[j, mine, :] = s
            wire_ref[j, theirs, :] = s.astype(_WIRE)
        return carry

    lax.fori_loop(0, rh // ROW_CHUNK, chunk, 0)


def _grad_exchange(wires, accs):
    n = len(wires)
    halves = [w.shape[1] for w in wires]

    def body(*refs):
        wire, acc, g = refs[:n], refs[n:2 * n], refs[2 * n:3 * n]
        land, own = refs[3 * n:4 * n], refs[4 * n:5 * n]
        ici_send, ici_recv, pair_send, pair_recv, local_sems = refs[5 * n:5 * n + 5]
        x, y, c, chips = _place()
        me = 2 * x + y

        def half_rows(t, half):
            return pl.ds(pl.multiple_of(half * halves[t], 8), halves[t])

        def own_copy(t):
            return pltpu.make_async_copy(acc[t].at[me, half_rows(t, c), :], own[t], local_sems.at[t])

        def ici_copy(t, d):
            chip = chips[d]
            return pltpu.make_async_remote_copy(
                src_ref=wire[t].at[2 * chip[0] + chip[1]], dst_ref=land[t].at[d],
                send_sem=ici_send.at[3 * t + d], recv_sem=ici_recv.at[3 * t + d],
                device_id=(chip[0], chip[1], c), device_id_type=MESH)

        def pair_copy(t, half):
            rows = g[t].at[half_rows(t, half), :]
            return pltpu.make_async_remote_copy(src_ref=rows, dst_ref=rows, send_sem=pair_send.at[t],
                                                recv_sem=pair_recv.at[t], device_id=(x, y, 1 - c), device_id_type=MESH)

        started = []
        for t in range(n):
            own_copy(t).start()
            for d in range(3):
                cp = ici_copy(t, d)
                cp.start()
                started.append(cp)
        for t in range(n):
            own_copy(t).wait()
            for d in range(3):
                ici_copy(t, d).wait_recv()

            def chunk(r, carry, t=t):
                src = pl.ds(pl.multiple_of(r * ROW_CHUNK, ROW_CHUNK), ROW_CHUNK)
                dst = pl.ds(pl.multiple_of(c * halves[t] + r * ROW_CHUNK, 8), ROW_CHUNK)
                s = own[t][src, :]
                for d in range(3):
                    s = s + land[t][d, src, :].astype(F32)
                g[t][dst, :] = s
                return carry

            lax.fori_loop(0, halves[t] // ROW_CHUNK, chunk, 0)
            cp = pair_copy(t, c)
            cp.start()
            started.append(cp)
        for t in range(n):
            pair_copy(t, 1 - c).wait_recv()
        for cp in started:
            cp.wait_send()

    return pl.pallas_call(
        body, name="grad_exchange",
        in_specs=[_vmem()] * n + [_hbm()] * n, out_specs=[_vmem()] * n,
        out_shape=[jax.ShapeDtypeStruct(a.shape[1:], F32) for a in accs],
        scratch_shapes=[pltpu.VMEM((3,) + w.shape[1:], w.dtype) for w in wires]
        + [pltpu.VMEM(w.shape[1:], F32) for w in wires]
        + [pltpu.SemaphoreType.DMA((3 * n,)), pltpu.SemaphoreType.DMA((3 * n,)), pltpu.SemaphoreType.DMA((n,)),
           pltpu.SemaphoreType.DMA((n,)), pltpu.SemaphoreType.DMA((n,))],
        compiler_params=pltpu.CompilerParams(vmem_limit_bytes=56 * MIB),
    )(*wires, *accs)


def _allreduce_small(packed):
    rows = packed.shape[0]

    def body(p_ref, out_ref, buf_ref, send_sems, recv_sems):
        x, y, c, _ = _place()
        me = 4 * x + 2 * y + c
        buf_ref[me] = p_ref[...]
        copies = []
        for r in range(1, 8):
            rx, ry, rc_ = (r >> 2) & 1, (r >> 1) & 1, r & 1
            peer = (x ^ rx, y ^ ry, c ^ rc_)
            cp = pltpu.make_async_remote_copy(src_ref=buf_ref.at[me], dst_ref=buf_ref.at[me],
                                              send_sem=send_sems.at[r - 1], recv_sem=recv_sems.at[r - 1],
                                              device_id=peer, device_id_type=MESH)
            cp.start()
            copies.append(cp)
        for cp in copies:
            cp.wait()
        acc = buf_ref[0]
        for d in range(1, 8):
            acc = acc + buf_ref[d]
        out_ref[...] = acc

    return pl.pallas_call(
        body, name="allreduce_small",
        in_specs=[_vmem()], out_specs=_vmem(),
        out_shape=jax.ShapeDtypeStruct((rows, 128), F32),
        scratch_shapes=[pltpu.VMEM((8, rows, 128), F32), pltpu.SemaphoreType.DMA((7,)), pltpu.SemaphoreType.DMA((7,))],
        compiler_params=pltpu.CompilerParams(vmem_limit_bytes=32 * MIB),
    )(packed)


def _elementwise(name, fn, ins, out_dtypes, tile_rows=256):
    shape = ins[0].shape
    lead = shape[:-2]
    rows, cols = shape[-2:]
    tr = _tile(rows, tile_rows)
    n_lead = math.prod(lead)
    nr = rows // tr
    flat = [a.reshape((n_lead, rows, cols)) for a in ins]

    def body(*refs):
        outs = fn(*[r[0] for r in refs[:len(ins)]])
        for o_ref, o in zip(refs[len(ins):], outs):
            o_ref[0] = o.astype(o_ref.dtype)

    spec = pl.BlockSpec((1, tr, cols), lambda i: (i // nr, i % nr, 0))
    res = pl.pallas_call(
        body, name=name, grid=(n_lead * nr,),
        in_specs=[spec] * len(ins), out_specs=[spec] * len(out_dtypes),
        out_shape=[jax.ShapeDtypeStruct((n_lead, rows, cols), dt) for dt in out_dtypes],
        compiler_params=_params(32),
    )(*flat)
    return [r.reshape(shape) for r in res]


def _adamw_math(w, g, m, v):
    m = ADAM_B1 * m + (1.0 - ADAM_B1) * g
    v = ADAM_B2 * v + (1.0 - ADAM_B2) * (g * g)
    m_hat = m / (1.0 - ADAM_B1 ** ADAM_STEP)
    v_hat = v / (1.0 - ADAM_B2 ** ADAM_STEP)
    delta = -ADAM_LR * (m_hat / (jnp.sqrt(v_hat) + ADAM_EPS) + ADAM_WD * w)
    return delta, m, v


def _adamw(name, w, g, m, v, tile_rows=256):
    return _elementwise(name, _adamw_math, [w, g, m, v], [F32, F32, F32], tile_rows)


_SMALL = ("ln_in_g", "ln_in_b", "b_in", "attn_sinks", "sgu_ln_g", "sgu_ln_b", "sgu_w", "sgu_b", "b_out",
          "ln_mix_g", "ln_mix_b", "ln_ffn_g", "ln_ffn_b")


def _pack(arrs):
    parts = []
    for a in arrs:
        flat = a.reshape(-1)
        pad = (-flat.shape[0]) % 1024
        parts.append(jnp.pad(flat, (0, pad)) if pad else flat)
    return jnp.concatenate(parts).reshape(-1, 128)


def _unpack(packed, like):
    flat = packed.reshape(-1)
    out, off = [], 0
    for a in like:
        n = math.prod(a.shape)
        out.append(flat[off:off + n].reshape(a.shape))
        off += n + ((-n) % 1024)
    return out


def kernel(x, positions, ln_in_g, ln_in_b, w_in, b_in, attn_sinks, sgu_ln_g, sgu_ln_b, sgu_w, sgu_b, w_out, b_out, ln_mix_g, ln_mix_b, w_gate, w_up, w_down, ln_ffn_g, ln_ffn_b, loss_target, m_ln_in_g, m_ln_in_b, m_w_in, m_b_in, m_attn_sinks, m_sgu_ln_g, m_sgu_ln_b, m_sgu_w, m_sgu_b, m_w_out, m_b_out, m_ln_mix_g, m_ln_mix_b, m_w_gate, m_w_up, m_w_down, m_ln_ffn_g, m_ln_ffn_b, v_ln_in_g, v_ln_in_b, v_w_in, v_b_in, v_attn_sinks, v_sgu_ln_g, v_sgu_ln_b, v_sgu_w, v_sgu_b, v_w_out, v_b_out, v_ln_mix_g, v_ln_mix_b, v_w_gate, v_w_up, v_w_down, v_ln_ffn_g, v_ln_ffn_b):
    weights = dict(ln_in_g=ln_in_g, ln_in_b=ln_in_b, w_in=w_in, b_in=b_in, attn_sinks=attn_sinks, sgu_ln_g=sgu_ln_g,
                   sgu_ln_b=sgu_ln_b, sgu_w=sgu_w, sgu_b=sgu_b, w_out=w_out, b_out=b_out, ln_mix_g=ln_mix_g,
                   ln_mix_b=ln_mix_b, w_gate=w_gate, w_up=w_up, w_down=w_down, ln_ffn_g=ln_ffn_g, ln_ffn_b=ln_ffn_b)
    mom_m = dict(ln_in_g=m_ln_in_g, ln_in_b=m_ln_in_b, w_in=m_w_in, b_in=m_b_in, attn_sinks=m_attn_sinks,
                 sgu_ln_g=m_sgu_ln_g, sgu_ln_b=m_sgu_ln_b, sgu_w=m_sgu_w, sgu_b=m_sgu_b, w_out=m_w_out, b_out=m_b_out,
                 ln_mix_g=m_ln_mix_g, ln_mix_b=m_ln_mix_b, w_gate=m_w_gate, w_up=m_w_up, w_down=m_w_down,
                 ln_ffn_g=m_ln_ffn_g, ln_ffn_b=m_ln_ffn_b)
    mom_v = dict(ln_in_g=v_ln_in_g, ln_in_b=v_ln_in_b, w_in=v_w_in, b_in=v_b_in, attn_sinks=v_attn_sinks,
                 sgu_ln_g=v_sgu_ln_g, sgu_ln_b=v_sgu_ln_b, sgu_w=v_sgu_w, sgu_b=v_sgu_b, w_out=v_w_out, b_out=v_b_out,
                 ln_mix_g=v_ln_mix_g, ln_mix_b=v_ln_mix_b, w_gate=v_w_gate, w_up=v_w_up, w_down=v_w_down,
                 ln_ffn_g=v_ln_ffn_g, ln_ffn_b=v_ln_ffn_b)
    order = list(weights)
    big = ("w_in", "w_out", "w_gate", "w_up", "w_down")

    s_len = x.shape[1]
    xs = x.reshape(s_len, D_MODEL)
    tgt = loss_target.reshape(s_len, D_MODEL)
    pos_col = positions.reshape(s_len, 1)
    g0, b0 = ln_in_g.reshape(1, D_MODEL), ln_in_b.reshape(1, D_MODEL)
    sinks = attn_sinks.reshape(N_Q)
    sgu_w3 = sgu_w.reshape(N_GRP, BLK, BLK)
    sgu_bt = sgu_b.reshape(N_GRP, BLK).T

    shards = [weights[n][0] for n in big]
    gw_in, gw_out, gw_gate, gw_up, gw_down = _gather_weights(shards)
    w_in_full = jnp.concatenate([gw_in[j] for j in range(N_CHIP)], axis=1)
    w_out_full = gw_out.reshape(D_MODEL, D_MODEL)

    q, k, v, su, sv, tc, t1, t2 = _ln_inproj(xs, pos_col, g0, b0, w_in_full, b_in)
    mc = _mixer_fwd(q, k, v, su, sv, sinks, sgu_ln_g, sgu_ln_b, sgu_w3, sgu_bt)
    r1 = _outproj(mc, w_out_full, b_out, xs, g0, b0)
    gact, uact = _ffn_up(r1, ln_mix_g, ln_mix_b, gw_gate, gw_up)
    dr2, loss_cols, d_ln_ffn_g, d_ln_ffn_b = _ffn_down_loss(gact, uact, gw_down, r1, ln_mix_g, ln_mix_b,
                                                            ln_ffn_g, ln_ffn_b, tgt)
    loss = lax.psum(jnp.sum(loss_cols) * (0.5 / D_MODEL), MESH_AXES)

    dg, du, acc_down, wire_down = _ffn_bwd_a(dr2, gact, uact, gw_down)
    dh1a, acc_gate, wire_gate = _ffn_bwd_g(dr2, dg, r1, ln_mix_g, ln_mix_b, gw_gate)
    dr1, acc_up, wire_up, d_ln_mix_g, d_ln_mix_b = _ffn_bwd_u(dh1a, du, r1, ln_mix_g, ln_mix_b, gw_up)
    dmc, acc_out, wire_out, d_b_out = _outproj_bwd(dr1, mc, w_out_full)
    (dq, dkv, dsuv, dbq, dbkv, dbsuv, d_sink, d_sgu_ln_g, d_sgu_ln_b, d_sgu_w, d_sgu_bt) = _mixer_bwd(
        q, k, v, su, sv, dmc, tc, t1, t2, sinks, sgu_ln_g, sgu_ln_b, sgu_w3, sgu_bt)
    grad_x, acc_in, wire_in, d_ln_in_g, d_ln_in_b = _inproj_bwd(dq, dkv, dsuv, dr1, xs, g0, b0, w_in_full)

    reduced = _grad_exchange([wire_in, wire_out, wire_gate, wire_up, wire_down],
                             [acc_in, acc_out, acc_gate, acc_up, acc_down])
    grads = {name: reduced[t].reshape(weights[name].shape) for t, name in enumerate(big)}

    small_local = dict(
        ln_in_g=d_ln_in_g.reshape(ln_in_g.shape), ln_in_b=d_ln_in_b.reshape(ln_in_b.shape),
        b_in=jnp.concatenate([dbq, dbkv, dbsuv], axis=1), attn_sinks=d_sink[:, :N_Q],
        sgu_ln_g=d_sgu_ln_g, sgu_ln_b=d_sgu_ln_b, sgu_w=d_sgu_w.reshape(sgu_w.shape),
        sgu_b=d_sgu_bt[:, :N_GRP].T.reshape(sgu_b.shape), b_out=d_b_out,
        ln_mix_g=d_ln_mix_g, ln_mix_b=d_ln_mix_b, ln_ffn_g=d_ln_ffn_g, ln_ffn_b=d_ln_ffn_b)
    small_sum = _allreduce_small(_pack([small_local[n] for n in _SMALL]))
    for n, g in zip(_SMALL, _unpack(small_sum, [weights[n] for n in _SMALL])):
        grads[n] = g

    delta, new_m, new_v = {}, {}, {}
    for name in big:
        d_, m_, v_ = _adamw("adamw_" + name, weights[name], grads[name], mom_m[name], mom_v[name])
        delta[name], new_m[name], new_v[name] = d_, m_, v_
    packs = [_pack([src[n] for n in _SMALL]) for src in (weights, mom_m, mom_v)]
    d_, m_, v_ = _adamw("adamw_small", packs[0], small_sum, packs[1], packs[2], tile_rows=packs[0].shape[0])
    like = [weights[n] for n in _SMALL]
    for n, a, b, c_ in zip(_SMALL, _unpack(d_, like), _unpack(m_, like), _unpack(v_, like)):
        delta[n], new_m[n], new_v[n] = a, b, c_

    return (loss, grad_x.reshape(x.shape), *[grads[n] for n in order], *[delta[n] for n in order],
            *[new_m[n] for n in order], *[new_v[n] for n in order])
```

```python
import functools
import math

import jax
import jax.numpy as jnp
from jax import lax
from jax.experimental import pallas as pl
from jax.experimental.pallas import tpu as pltpu

F32 = jnp.float32
_MXU = jnp.bfloat16
_WIRE = jnp.bfloat16

D_MODEL = 1024
ATTN_W = 512
SGU_W = 512
HEAD_DIM = 64
N_Q = 8
N_KV = 2
Q_PER_KV = 4
KV_W = 128
BLK = 128
ROT_DIM = 16
ROPE_THETA = 500000.0
N_GRP = 4
GRP_DIM = 128
D_FF = 2816
IN_W = 1792
LN_EPS = 1e-5
ALPHA = 2.0 ** 0.25
N_CHIP = 4
FF_SH = D_FF // N_CHIP
IN_SH = IN_W // N_CHIP
OUT_SH = D_MODEL // N_CHIP
ROW_CHUNK = 32

ADAM_LR = 0.001
ADAM_B1 = 0.9
ADAM_B2 = 0.999
ADAM_EPS = 1e-08
ADAM_WD = 0.01
ADAM_STEP = 10

SQRT_HALF = 0.7071067811865476
INV_SQRT_2PI = 0.3989422804014327
MESH_AXES = ("x", "y", "c")
MESH = pl.DeviceIdType.MESH
MIB = 2 ** 20


def _vmem():
    return pl.BlockSpec(memory_space=pltpu.VMEM)


def _smem():
    return pl.BlockSpec(memory_space=pltpu.SMEM)


def _hbm():
    return pl.BlockSpec(memory_space=pl.ANY)


def _params(vmem_mib=48):
    return pltpu.CompilerParams(dimension_semantics=("arbitrary",), vmem_limit_bytes=vmem_mib * MIB)


def _tile(n, cap):
    if n <= cap:
        return n
    for t in range(cap - cap % 16, 0, -16):
        if n % t == 0:
            return t
    raise ValueError((n, cap))


def _rows(tm, width):
    return pl.BlockSpec((tm, width), lambda i: (i, 0))


def _const2(shape):
    return pl.BlockSpec(shape, lambda i: (0,) * len(shape))


def _ln(x, g, b):
    mu = jnp.mean(x, axis=-1, keepdims=True)
    xc = x - mu
    var = jnp.mean(xc * xc, axis=-1, keepdims=True)
    rstd = lax.rsqrt(var + LN_EPS)
    xhat = xc * rstd
    return xhat * g + b, xhat, rstd


def _ln_bwd(dy, xhat, rstd, g):
    gdy = dy * g
    m1 = jnp.mean(gdy, axis=-1, keepdims=True)
    m2 = jnp.mean(gdy * xhat, axis=-1, keepdims=True)
    return rstd * (gdy - m1 - xhat * m2)


def _colsum(a):
    return jnp.sum(a, axis=0, keepdims=True)


def _gelu(x):
    return 0.5 * x * (1.0 + lax.erf(x * SQRT_HALF))


def _gelu_grad(x):
    return 0.5 * (1.0 + lax.erf(x * SQRT_HALF)) + x * jnp.exp(-0.5 * x * x) * INV_SQRT_2PI


def _dot(a, b):
    return jnp.dot(a, b, preferred_element_type=F32)


def _dot_nt(a, b):
    return lax.dot_general(a, b, (((1,), (1,)), ((), ())), preferred_element_type=F32)


def _dot_tn(a, b):
    return lax.dot_general(a, b, (((0,), (0,)), ((), ())), preferred_element_type=F32)


def _rope(t, tc, t1, t2):
    n = t.shape[1]
    rep = n // 128
    if rep > 1:
        tc, t1, t2 = (jnp.tile(a, (1, rep)) for a in (tc, t1, t2))
    return t * tc + pltpu.roll(t, n - 8, 1) * t1 + pltpu.roll(t, 8, 1) * t2


def _rope_bwd(d, tc, t1, t2):
    n = d.shape[1]
    rep = n // 128
    if rep > 1:
        tc, t1, t2 = (jnp.tile(a, (1, rep)) for a in (tc, t1, t2))
    return d * tc + pltpu.roll(d * t1, 8, 1) + pltpu.roll(d * t2, n - 8, 1)


def _band_mask(first_block):
    qi = lax.broadcasted_iota(jnp.int32, (BLK, 2 * BLK), 0)
    kj = lax.broadcasted_iota(jnp.int32, (BLK, 2 * BLK), 1)
    shut = jnp.where(first_block, 2 * BLK, 0)
    prev_ok = jnp.logical_and(kj < BLK, kj > qi + shut)
    cur_ok = jnp.logical_and(kj >= BLK, (kj - BLK) <= qi)
    return jnp.logical_or(prev_ok, cur_ok)


def _causal_w(w_ref, h):
    t = lax.broadcasted_iota(jnp.int32, (BLK, BLK), 0)
    s = lax.broadcasted_iota(jnp.int32, (BLK, BLK), 1)
    return jnp.where(s <= t, w_ref[h], 0.0)


def _lane_put(vals, width):
    rows = vals[0].shape[0]
    lane = lax.broadcasted_iota(jnp.int32, (rows, width), 1)
    out = jnp.zeros((rows, width), F32)
    for k, v in enumerate(vals):
        out = out + jnp.where(lane == k, v, 0.0)
    return out


def _rope_consts():
    lane = jnp.arange(128) % HEAD_DIM
    inv_freq = ROPE_THETA ** (-jnp.arange(0, ROT_DIM, 2, dtype=F32) / ROT_DIM)
    rot = lane < ROT_DIM
    freq = jnp.where(rot, inv_freq[lane % (ROT_DIM // 2)], 0.0)
    rows = [freq, rot.astype(F32), 1.0 - rot.astype(F32), (lane < ROT_DIM // 2).astype(F32),
            jnp.logical_and(lane >= ROT_DIM // 2, rot).astype(F32)]
    rows += [jnp.zeros((128,), F32)] * 3
    return jnp.stack(rows).astype(F32)


def _ln_inproj(x, pos_col, g0, b0, w_in, b_in):
    s_len = x.shape[0]
    tm = _tile(s_len, 512)

    def body(x_ref, pos_ref, g_ref, b_ref, w_ref, bi_ref, rc_ref,
             q_ref, k_ref, v_ref, su_ref, sv_ref, tc_ref, t1_ref, t2_ref):
        h0, _, _ = _ln(x_ref[...], g_ref[...], b_ref[...])
        proj = _dot_nt(h0.astype(_MXU), w_ref[...]) + bi_ref[...]
        ang = pos_ref[...].astype(F32) * rc_ref[0:1, :]
        cs = jnp.cos(ang)
        sn = jnp.sin(ang)
        tc = cs * rc_ref[1:2, :] + rc_ref[2:3, :]
        t1 = -sn * rc_ref[3:4, :]
        t2 = sn * rc_ref[4:5, :]
        tc_ref[...] = tc
        t1_ref[...] = t1
        t2_ref[...] = t2
        q = _rope(proj[:, 0:ATTN_W], tc, t1, t2) * (HEAD_DIM ** -0.5)
        q_ref[...] = q.astype(_MXU)
        k_ref[...] = _rope(proj[:, ATTN_W:ATTN_W + KV_W], tc, t1, t2).astype(_MXU)
        v_ref[...] = proj[:, ATTN_W + KV_W:ATTN_W + 2 * KV_W].astype(_MXU)
        su_ref[...] = proj[:, ATTN_W + 2 * KV_W:ATTN_W + 2 * KV_W + SGU_W]
        sv_ref[...] = proj[:, ATTN_W + 2 * KV_W + SGU_W:IN_W]

    sd = jax.ShapeDtypeStruct
    return pl.pallas_call(
        body, name="ln_inproj", grid=(s_len // tm,),
        in_specs=[_rows(tm, D_MODEL), _rows(tm, 1), _const2((1, D_MODEL)), _const2((1, D_MODEL)), _vmem(),
                  _const2((1, IN_W)), _const2((8, 128))],
        out_specs=[_rows(tm, ATTN_W), _rows(tm, KV_W), _rows(tm, KV_W), _rows(tm, SGU_W), _rows(tm, SGU_W),
                   _rows(tm, 128), _rows(tm, 128), _rows(tm, 128)],
        out_shape=[sd((s_len, ATTN_W), _MXU), sd((s_len, KV_W), _MXU), sd((s_len, KV_W), _MXU),
                   sd((s_len, SGU_W), F32), sd((s_len, SGU_W), F32),
                   sd((s_len, 128), F32), sd((s_len, 128), F32), sd((s_len, 128), F32)],
        compiler_params=_params(48),
    )(x, pos_col, g0, b0, w_in, b_in, _rope_consts())


def _attn_probs(qh, kh, sink, allowed):
    s = jnp.where(allowed, _dot_nt(qh, kh), -1e30)
    m = jnp.maximum(jnp.max(s, axis=-1, keepdims=True), sink)
    p = jnp.exp(s - m)
    ps = jnp.exp(sink - m)
    inv = 1.0 / (jnp.sum(p, axis=-1, keepdims=True) + ps)
    return p * inv, ps * inv


def _sgu_fwd(su, sv, lg, lb, w_ref, bt_ref):
    u = _gelu(su)
    vv, vhat, rstd = _ln(_gelu(sv), lg, lb)
    vvb = vv.astype(_MXU)
    wcs, mixed = [], []
    for h in range(N_GRP):
        wc = _causal_w(w_ref, h).astype(_MXU)
        wcs.append(wc)
        mixed.append(_dot(wc, vvb[:, h * GRP_DIM:(h + 1) * GRP_DIM]) + bt_ref[:, h:h + 1])
    return u, vhat, rstd, vvb, wcs, jnp.concatenate(mixed, axis=1)


def _prev_map(i):
    return (jnp.maximum(i - 1, 0), 0)


def _mixer_fwd(q, k, v, su, sv, sinks, sg, sb, sgu_w, sgu_bt):
    s_len = q.shape[0]
    nb = s_len // BLK

    def body(q_ref, kc_ref, kp_ref, vc_ref, vp_ref, su_ref, sv_ref, sink_ref, lg_ref, lb_ref, w_ref, bt_ref, mc_ref):
        i = pl.program_id(0)
        allowed = _band_mask(i == 0)
        kb = jnp.concatenate([kp_ref[...], kc_ref[...]], axis=0)
        vb = jnp.concatenate([vp_ref[...], vc_ref[...]], axis=0)
        qv = q_ref[...]
        outs = []
        for h in range(N_Q):
            g = h // Q_PER_KV
            kh = kb[:, g * HEAD_DIM:(g + 1) * HEAD_DIM]
            vh = vb[:, g * HEAD_DIM:(g + 1) * HEAD_DIM]
            probs, _ = _attn_probs(qv[:, h * HEAD_DIM:(h + 1) * HEAD_DIM], kh, sink_ref[h], allowed)
            outs.append(_dot(probs.astype(_MXU), vh))
        u, _, _, _, _, mixed = _sgu_fwd(su_ref[...], sv_ref[...], lg_ref[...], lb_ref[...], w_ref, bt_ref)
        mc_ref[...] = jnp.concatenate(outs + [u * mixed], axis=1).astype(_MXU)

    cur = lambda w: pl.BlockSpec((BLK, w), lambda i: (i, 0))
    prev = lambda w: pl.BlockSpec((BLK, w), _prev_map)
    return pl.pallas_call(
        body, name="mixer_fwd", grid=(nb,),
        in_specs=[cur(ATTN_W), cur(KV_W), prev(KV_W), cur(KV_W), prev(KV_W), cur(SGU_W), cur(SGU_W), _smem(),
                  _const2((1, SGU_W)), _const2((1, SGU_W)), _const2((N_GRP, BLK, BLK)), _const2((BLK, N_GRP))],
        out_specs=cur(D_MODEL),
        out_shape=jax.ShapeDtypeStruct((s_len, D_MODEL), _MXU),
        compiler_params=_params(32),
    )(q, k, k, v, v, su, sv, sinks, sg, sb, sgu_w, sgu_bt)


def _outproj(mc, w_out, b_out, x, g0, b0):
    s_len = x.shape[0]
    tm = _tile(s_len, 512)

    def body(mc_ref, w_ref, bo_ref, x_ref, g_ref, b_ref, r1_ref):
        h0, _, _ = _ln(x_ref[...], g_ref[...], b_ref[...])
        r1_ref[...] = ALPHA * h0 + (_dot(mc_ref[...], w_ref[...]) + bo_ref[...])

    return pl.pallas_call(
        body, name="outproj", grid=(s_len // tm,),
        in_specs=[_rows(tm, D_MODEL), _vmem(), _const2((1, D_MODEL)), _rows(tm, D_MODEL),
                  _const2((1, D_MODEL)), _const2((1, D_MODEL))],
        out_specs=_rows(tm, D_MODEL),
        out_shape=jax.ShapeDtypeStruct((s_len, D_MODEL), F32),
        compiler_params=_params(32),
    )(mc, w_out, b_out, x, g0, b0)


def _ffn_spec(tm):
    return pl.BlockSpec((N_CHIP, tm, FF_SH), lambda i: (0, i, 0))


def _ffn_up(r1, g1, b1, wg, wu):
    s_len = r1.shape[0]
    tm = _tile(s_len, 256)

    def body(r1_ref, g_ref, b_ref, wg_ref, wu_ref, go_ref, uo_ref):
        h1, _, _ = _ln(r1_ref[...], g_ref[...], b_ref[...])
        h1b = h1.astype(_MXU)
        for j in range(N_CHIP):
            go_ref[j] = _dot_nt(h1b, wg_ref[j])
            uo_ref[j] = _dot_nt(h1b, wu_ref[j])

    sd = jax.ShapeDtypeStruct((N_CHIP, s_len, FF_SH), F32)
    return pl.pallas_call(
        body, name="ffn_up", grid=(s_len // tm,),
        in_specs=[_rows(tm, D_MODEL), _const2((1, D_MODEL)), _const2((1, D_MODEL)), _vmem(), _vmem()],
        out_specs=[_ffn_spec(tm), _ffn_spec(tm)],
        out_shape=[sd, sd],
        compiler_params=_params(56),
    )(r1, g1, b1, wg, wu)


def _silu_parts(g):
    sg = 1.0 / (1.0 + jnp.exp(-g))
    return g * sg, sg


def _ffn_down_loss(gact, uact, wd, r1, g1, b1, g2, b2, target):
    s_len = r1.shape[0]
    tm = _tile(s_len, 256)

    def body(g_ref, u_ref, wd_ref, r1_ref, g1_ref, b1_ref, g2_ref, b2_ref, t_ref,
             dr2_ref, loss_ref, dg2_ref, db2_ref):
        i = pl.program_id(0)
        f = jnp.zeros((tm, D_MODEL), F32)
        for j in range(N_CHIP):
            silu, _ = _silu_parts(g_ref[j])
            f = f + _dot((silu * u_ref[j]).astype(_MXU), wd_ref[j])
        h1, _, _ = _ln(r1_ref[...], g1_ref[...], b1_ref[...])
        h2, r2hat, rstd2 = _ln(ALPHA * h1 + f, g2_ref[...], b2_ref[...])
        diff = h2 - t_ref[...]
        dh2 = diff * (1.0 / D_MODEL)

        @pl.when(i == 0)
        def _():
            loss_ref[...] = jnp.zeros_like(loss_ref)
            dg2_ref[...] = jnp.zeros_like(dg2_ref)
            db2_ref[...] = jnp.zeros_like(db2_ref)

        loss_ref[...] += _colsum(diff * diff)
        dg2_ref[...] += _colsum(dh2 * r2hat)
        db2_ref[...] += _colsum(dh2)
        dr2_ref[...] = _ln_bwd(dh2, r2hat, rstd2, g2_ref[...])

    vec = jax.ShapeDtypeStruct((1, D_MODEL), F32)
    c = _const2((1, D_MODEL))
    return pl.pallas_call(
        body, name="ffn_down_loss", grid=(s_len // tm,),
        in_specs=[_ffn_spec(tm), _ffn_spec(tm), _vmem(), _rows(tm, D_MODEL), c, c, c, c, _rows(tm, D_MODEL)],
        out_specs=[_rows(tm, D_MODEL), c, c, c],
        out_shape=[jax.ShapeDtypeStruct((s_len, D_MODEL), F32), vec, vec, vec],
        compiler_params=_params(48),
    )(gact, uact, wd, r1, g1, b1, g2, b2, target)


def _ffn_bwd_a(dr2, gact, uact, wd):
    s_len = dr2.shape[0]
    tm = _tile(s_len, 256)

    def body(dr2_ref, g_ref, u_ref, wd_ref, dg_ref, du_ref, dwd_ref, wire_ref, land_ref, send_sem, recv_sem):
        i = pl.program_id(0)

        @pl.when(i == 0)
        def _():
            dwd_ref[...] = jnp.zeros_like(dwd_ref)

        dfb = dr2_ref[...].astype(_MXU)
        for j in range(N_CHIP):
            g = g_ref[j]
            u = u_ref[j]
            silu, sg = _silu_parts(g)
            da = _dot_nt(dfb, wd_ref[j])
            dg_ref[j] = (da * u * (sg * (1.0 + g * (1.0 - sg)))).astype(_MXU)
            du_ref[j] = (da * silu).astype(_MXU)
            dwd_ref[j * FF_SH:(j + 1) * FF_SH, :] += _dot_tn((silu * u).astype(_MXU), dfb)

        @pl.when(i == pl.num_programs(0) - 1)
        def _():
            _pair_reduce(dwd_ref, land_ref, wire_ref, send_sem, recv_sem)

    sd = jax.ShapeDtypeStruct((N_CHIP, s_len, FF_SH), _MXU)
    return pl.pallas_call(
        body, name="ffn_bwd_a", grid=(s_len // tm,),
        in_specs=[_rows(tm, D_MODEL), _ffn_spec(tm), _ffn_spec(tm), _vmem()],
        out_specs=[_ffn_spec(tm), _ffn_spec(tm), _vmem(), _vmem()],
        out_shape=[sd, sd, jax.ShapeDtypeStruct((D_FF, D_MODEL), F32),
                   jax.ShapeDtypeStruct((N_CHIP, FF_SH // 2, D_MODEL), _WIRE)],
        scratch_shapes=_pair_scratch((N_CHIP, FF_SH // 2, D_MODEL)),
        compiler_params=_params(58),
    )(dr2, gact, uact, wd)


def _ffn_bwd_g(dr2, dg, r1, g1, b1, wg):
    s_len = dr2.shape[0]
    tm = _tile(s_len, 256)

    def body(dr2_ref, dg_ref, r1_ref, g1_ref, b1_ref, wg_ref, dh1_ref, dwg_ref, wire_ref, land_ref, send_sem, recv_sem):
        i = pl.program_id(0)

        @pl.when(i == 0)
        def _():
            dwg_ref[...] = jnp.zeros_like(dwg_ref)

        h1, _, _ = _ln(r1_ref[...], g1_ref[...], b1_ref[...])
        h1b = h1.astype(_MXU)
        dh1 = ALPHA * dr2_ref[...]
        for j in range(N_CHIP):
            dgj = dg_ref[j]
            dh1 = dh1 + _dot(dgj, wg_ref[j])
            dwg_ref[j * FF_SH:(j + 1) * FF_SH, :] += _dot_tn(dgj, h1b)
        dh1_ref[...] = dh1

        @pl.when(i == pl.num_programs(0) - 1)
        def _():
            _pair_reduce(dwg_ref, land_ref, wire_ref, send_sem, recv_sem)

    c = _const2((1, D_MODEL))
    return pl.pallas_call(
        body, name="ffn_bwd_g", grid=(s_len // tm,),
        in_specs=[_rows(tm, D_MODEL), _ffn_spec(tm), _rows(tm, D_MODEL), c, c, _vmem()],
        out_specs=[_rows(tm, D_MODEL), _vmem(), _vmem()],
        out_shape=[jax.ShapeDtypeStruct((s_len, D_MODEL), F32), jax.ShapeDtypeStruct((D_FF, D_MODEL), F32),
                   jax.ShapeDtypeStruct((N_CHIP, FF_SH // 2, D_MODEL), _WIRE)],
        scratch_shapes=_pair_scratch((N_CHIP, FF_SH // 2, D_MODEL)),
        compiler_params=_params(56),
    )(dr2, dg, r1, g1, b1, wg)


def _ffn_bwd_u(dh1a, du, r1, g1, b1, wu):
    s_len = dh1a.shape[0]
    tm = _tile(s_len, 256)

    def body(dh1_ref, du_ref, r1_ref, g1_ref, b1_ref, wu_ref,
             dr1_ref, dwu_ref, wire_ref, dg1_ref, db1_ref, land_ref, send_sem, recv_sem):
        i = pl.program_id(0)

        @pl.when(i == 0)
        def _():
            dwu_ref[...] = jnp.zeros_like(dwu_ref)
            dg1_ref[...] = jnp.zeros_like(dg1_ref)
            db1_ref[...] = jnp.zeros_like(db1_ref)

        h1, r1hat, rstd1 = _ln(r1_ref[...], g1_ref[...], b1_ref[...])
        h1b = h1.astype(_MXU)
        dh1 = dh1_ref[...]
        for j in range(N_CHIP):
            duj = du_ref[j]
            dh1 = dh1 + _dot(duj, wu_ref[j])
            dwu_ref[j * FF_SH:(j + 1) * FF_SH, :] += _dot_tn(duj, h1b)
        dg1_ref[...] += _colsum(dh1 * r1hat)
        db1_ref[...] += _colsum(dh1)
        dr1_ref[...] = _ln_bwd(dh1, r1hat, rstd1, g1_ref[...])

        @pl.when(i == pl.num_programs(0) - 1)
        def _():
            _pair_reduce(dwu_ref, land_ref, wire_ref, send_sem, recv_sem)

    vec = jax.ShapeDtypeStruct((1, D_MODEL), F32)
    c = _const2((1, D_MODEL))
    return pl.pallas_call(
        body, name="ffn_bwd_u", grid=(s_len // tm,),
        in_specs=[_rows(tm, D_MODEL), _ffn_spec(tm), _rows(tm, D_MODEL), c, c, _vmem()],
        out_specs=[_rows(tm, D_MODEL), _vmem(), _vmem(), c, c],
        out_shape=[jax.ShapeDtypeStruct((s_len, D_MODEL), F32), jax.ShapeDtypeStruct((D_FF, D_MODEL), F32),
                   jax.ShapeDtypeStruct((N_CHIP, FF_SH // 2, D_MODEL), _WIRE), vec, vec],
        scratch_shapes=_pair_scratch((N_CHIP, FF_SH // 2, D_MODEL)),
        compiler_params=_params(56),
    )(dh1a, du, r1, g1, b1, wu)


def _outproj_bwd(dr1, mc, w_out):
    s_len = dr1.shape[0]
    tm = _tile(s_len, 512)

    def body(dr1_ref, mc_ref, w_ref, dmc_ref, dw_ref, wire_ref, db_ref, land_ref, send_sem, recv_sem):
        i = pl.program_id(0)

        @pl.when(i == 0)
        def _():
            dw_ref[...] = jnp.zeros_like(dw_ref)
            db_ref[...] = jnp.zeros_like(db_ref)

        d = dr1_ref[...]
        db_ref[...] += _colsum(d)
        db16 = d.astype(_MXU)
        dmc_ref[...] = _dot_nt(db16, w_ref[...])
        dw_ref[...] += _dot_tn(mc_ref[...], db16)

        @pl.when(i == pl.num_programs(0) - 1)
        def _():
            _pair_reduce(dw_ref, land_ref, wire_ref, send_sem, recv_sem)

    return pl.pallas_call(
        body, name="outproj_bwd", grid=(s_len // tm,),
        in_specs=[_rows(tm, D_MODEL), _rows(tm, D_MODEL), _vmem()],
        out_specs=[_rows(tm, D_MODEL), _vmem(), _vmem(), _const2((1, D_MODEL))],
        out_shape=[jax.ShapeDtypeStruct((s_len, D_MODEL), F32), jax.ShapeDtypeStruct((D_MODEL, D_MODEL), F32),
                   jax.ShapeDtypeStruct((N_CHIP, OUT_SH // 2, D_MODEL), _WIRE), jax.ShapeDtypeStruct((1, D_MODEL), F32)],
        scratch_shapes=_pair_scratch((N_CHIP, OUT_SH // 2, D_MODEL)),
        compiler_params=_params(40),
    )(dr1, mc, w_out)


def _mixer_bwd(q, k, v, su, sv, dmc, tc, t1, t2, sinks, sg, sb, sgu_w, sgu_bt):
    s_len = q.shape[0]
    nb = s_len // BLK

    def body(q_ref, kc_ref, kp_ref, vc_ref, vp_ref, su_ref, sv_ref, dmc_ref,
             tc_ref, t1_ref, t2_ref, tcp_ref, t1p_ref, t2p_ref,
             sink_ref, lg_ref, lb_ref, w_ref, bt_ref,
             dq_ref, dkv_ref, dsuv_ref, dbq_ref, dbkv_ref, dbsuv_ref,
             dsink_ref, dlg_ref, dlb_ref, dw_ref, dbt_ref, carry_ref):
        i = pl.program_id(0)

        @pl.when(i == 0)
        def _():
            for r in (dbq_ref, dbkv_ref, dbsuv_ref, dsink_ref, dlg_ref, dlb_ref, dw_ref, dbt_ref):
                r[...] = jnp.zeros_like(r)

        def emit_kv(fin):
            dk = _rope_bwd(fin[:, 0:KV_W], tcp_ref[...], t1p_ref[...], t2p_ref[...])
            out = jnp.concatenate([dk, fin[:, KV_W:2 * KV_W]], axis=1)
            dkv_ref[...] = out.astype(_MXU)
            dbkv_ref[...] += _colsum(out)

        @pl.when(i < nb)
        def _():
            allowed = _band_mask(i == 0)
            kb = jnp.concatenate([kp_ref[...], kc_ref[...]], axis=0)
            vb = jnp.concatenate([vp_ref[...], vc_ref[...]], axis=0)
            qv = q_ref[...]
            dmc = dmc_ref[...]
            dqs, dks, dvs, dsinks = [], [], [], []
            for g in range(N_KV):
                kh = kb[:, g * HEAD_DIM:(g + 1) * HEAD_DIM]
                vh = vb[:, g * HEAD_DIM:(g + 1) * HEAD_DIM]
                dk_g = jnp.zeros((2 * BLK, HEAD_DIM), F32)
                dv_g = jnp.zeros((2 * BLK, HEAD_DIM), F32)
                for hh in range(Q_PER_KV):
                    h = g * Q_PER_KV + hh
                    qh = qv[:, h * HEAD_DIM:(h + 1) * HEAD_DIM]
                    probs, psink = _attn_probs(qh, kh, sink_ref[h], allowed)
                    pb = probs.astype(_MXU)
                    dob = dmc[:, h * HEAD_DIM:(h + 1) * HEAD_DIM].astype(_MXU)
                    dv_g = dv_g + _dot_tn(pb, dob)
                    dp = _dot_nt(dob, vh)
                    rd = jnp.sum(probs * dp, axis=-1, keepdims=True)
                    dsb = (probs * (dp - rd)).astype(_MXU)
                    dsinks.append(-jnp.sum(psink * rd, axis=0, keepdims=True))
                    dqs.append(_dot(dsb, kh))
                    dk_g = dk_g + _dot_tn(dsb, qh)
                dks.append(dk_g)
                dvs.append(dv_g)
            dq = _rope_bwd(jnp.concatenate(dqs, axis=1) * (HEAD_DIM ** -0.5), tc_ref[...], t1_ref[...], t2_ref[...])
            dq_ref[...] = dq.astype(_MXU)
            dbq_ref[...] += _colsum(dq)
            dsink_ref[...] += _lane_put(dsinks, 128)
            contrib = jnp.concatenate(dks + dvs, axis=1)

            @pl.when(i > 0)
            def _():
                emit_kv(carry_ref[...] + contrib[0:BLK, :])

            carry_ref[...] = contrib[BLK:2 * BLK, :]

            su = su_ref[...]
            sv = sv_ref[...]
            lg = lg_ref[...]
            u, vhat, rstd, vvb, wcs, mixed = _sgu_fwd(su, sv, lg, lb_ref[...], w_ref, bt_ref)
            dsgu = dmc[:, ATTN_W:D_MODEL]
            dsu = dsgu * mixed * _gelu_grad(su)
            dmixed = dsgu * u
            tri_t = lax.broadcasted_iota(jnp.int32, (BLK, BLK), 0)
            tri_s = lax.broadcasted_iota(jnp.int32, (BLK, BLK), 1)
            dvv, dbs = [], []
            for h in range(N_GRP):
                dm = dmixed[:, h * GRP_DIM:(h + 1) * GRP_DIM]
                dmb = dm.astype(_MXU)
                dbs.append(jnp.sum(dm, axis=1, keepdims=True))
                dw_ref[h] += jnp.where(tri_s <= tri_t, _dot_nt(dmb, vvb[:, h * GRP_DIM:(h + 1) * GRP_DIM]), 0.0)
                dvv.append(_dot_tn(wcs[h], dmb))
            dvv = jnp.concatenate(dvv, axis=1)
            dbt_ref[...] += _lane_put(dbs, 128)
            dlg_ref[...] += _colsum(dvv * vhat)
            dlb_ref[...] += _colsum(dvv)
            dsv = _ln_bwd(dvv, vhat, rstd, lg) * _gelu_grad(sv)
            dsuv = jnp.concatenate([dsu, dsv], axis=1)
            dsuv_ref[...] = dsuv.astype(_MXU)
            dbsuv_ref[...] += _colsum(dsuv)

        @pl.when(i == nb)
        def _():
            emit_kv(carry_ref[...])

    last = nb - 1
    cur = lambda w: pl.BlockSpec((BLK, w), lambda i: (jnp.minimum(i, last), 0))
    prev = lambda w: pl.BlockSpec((BLK, w), lambda i: (jnp.clip(i - 1, 0, last), 0))
    sd = jax.ShapeDtypeStruct
    return pl.pallas_call(
        body, name="mixer_bwd", grid=(nb + 1,),
        in_specs=[cur(ATTN_W), cur(KV_W), prev(KV_W), cur(KV_W), prev(KV_W), cur(SGU_W), cur(SGU_W), cur(D_MODEL),
                  cur(128), cur(128), cur(128), prev(128), prev(128), prev(128),
                  _smem(), _const2((1, SGU_W)), _const2((1, SGU_W)), _const2((N_GRP, BLK, BLK)), _const2((BLK, N_GRP))],
        out_specs=[cur(ATTN_W), prev(2 * KV_W), cur(2 * SGU_W),
                   _const2((1, ATTN_W)), _const2((1, 2 * KV_W)), _const2((1, 2 * SGU_W)),
                   _const2((1, 128)), _const2((1, SGU_W)), _const2((1, SGU_W)),
                   _const2((N_GRP, BLK, BLK)), _const2((BLK, 128))],
        out_shape=[sd((s_len, ATTN_W), _MXU), sd((s_len, 2 * KV_W), _MXU), sd((s_len, 2 * SGU_W), _MXU),
                   sd((1, ATTN_W), F32), sd((1, 2 * KV_W), F32), sd((1, 2 * SGU_W), F32),
                   sd((1, 128), F32), sd((1, SGU_W), F32), sd((1, SGU_W), F32),
                   sd((N_GRP, BLK, BLK), F32), sd((BLK, 128), F32)],
        scratch_shapes=[pltpu.VMEM((BLK, 2 * KV_W), F32)],
        compiler_params=_params(32),
    )(q, k, k, v, v, su, sv, dmc, tc, t1, t2, tc, t1, t2, sinks, sg, sb, sgu_w, sgu_bt)


def _inproj_bwd(dq, dkv, dsuv, dr1, x, g0, b0, w_in):
    s_len = x.shape[0]
    tm = _tile(s_len, 512)
    cuts = ((0, ATTN_W), (ATTN_W, ATTN_W + 2 * KV_W), (ATTN_W + 2 * KV_W, IN_W))

    def body(dq_ref, dkv_ref, dsuv_ref, dr1_ref, x_ref, g_ref, b_ref, w_ref,
             dx_ref, dw_ref, wire_ref, dg_ref, db_ref, land_ref, send_sem, recv_sem):
        i = pl.program_id(0)

        @pl.when(i == 0)
        def _():
            dw_ref[...] = jnp.zeros_like(dw_ref)
            dg_ref[...] = jnp.zeros_like(dg_ref)
            db_ref[...] = jnp.zeros_like(db_ref)

        h0, xhat, rstd = _ln(x_ref[...], g_ref[...], b_ref[...])
        h0b = h0.astype(_MXU)
        dh0 = ALPHA * dr1_ref[...]
        for (lo, hi), d_ref in zip(cuts, (dq_ref, dkv_ref, dsuv_ref)):
            d = d_ref[...]
            dh0 = dh0 + _dot(d, w_ref[lo:hi, :])
            dw_ref[lo:hi, :] += _dot_tn(d, h0b)
        dg_ref[...] += _colsum(dh0 * xhat)
        db_ref[...] += _colsum(dh0)
        dx_ref[...] = _ln_bwd(dh0, xhat, rstd, g_ref[...])

        @pl.when(i == pl.num_programs(0) - 1)
        def _():
            _pair_reduce(dw_ref, land_ref, wire_ref, send_sem, recv_sem)

    vec = jax.ShapeDtypeStruct((1, D_MODEL), F32)
    c = _const2((1, D_MODEL))
    return pl.pallas_call(
        body, name="inproj_bwd", grid=(s_len // tm,),
        in_specs=[_rows(tm, ATTN_W), _rows(tm, 2 * KV_W), _rows(tm, 2 * SGU_W), _rows(tm, D_MODEL), _rows(tm, D_MODEL),
                  c, c, _vmem()],
        out_specs=[_rows(tm, D_MODEL), _vmem(), _vmem(), c, c],
        out_shape=[jax.ShapeDtypeStruct((s_len, D_MODEL), F32), jax.ShapeDtypeStruct((IN_W, D_MODEL), F32),
                   jax.ShapeDtypeStruct((N_CHIP, IN_SH // 2, D_MODEL), _WIRE), vec, vec],
        scratch_shapes=_pair_scratch((N_CHIP, IN_SH // 2, D_MODEL)),
        compiler_params=_params(56),
    )(dq, dkv, dsuv, dr1, x, g0, b0, w_in)


def _place():
    x, y, c = (lax.axis_index(a) for a in MESH_AXES)
    chips = [(1 - x, y), (x, 1 - y), (1 - x, 1 - y)]
    return x, y, c, chips


def _gather_weights(shards):
    n = len(shards)
    halves = [s.shape[0] // 2 for s in shards]

    def body(*refs):
        ins, outs = refs[:n], refs[n:2 * n]
        send_sems, recv_sems = refs[2 * n], refs[2 * n + 1]
        x, y, c, chips = _place()
        me = 2 * x + y

        def piece(t, slot, half):
            return outs[t].at[slot, pl.ds(pl.multiple_of(half * halves[t], 16), halves[t]), :]

        def copy(k, t, slot, half, to):
            return pltpu.make_async_remote_copy(
                src_ref=piece(t, slot, half), dst_ref=piece(t, slot, half),
                send_sem=send_sems.at[k], recv_sem=recv_sems.at[k], device_id=to, device_id_type=MESH)

        for t in range(n):
            outs[t][me] = ins[t][...].astype(_WIRE)
        started = []
        for t in range(n):
            for d, chip in enumerate(chips):
                cp = copy(3 * t + d, t, me, c, (chip[0], chip[1], c))
                cp.start()
                started.append(cp)
        for t in range(n):
            for d, chip in enumerate(chips):
                slot = 2 * chip[0] + chip[1]
                copy(3 * t + d, t, slot, c, (chip[0], chip[1], c)).wait_recv()
                fwd = copy(3 * n + 3 * t + d, t, slot, c, (x, y, 1 - c))
                fwd.start()
                started.append(fwd)
        for t in range(n):
            for d, chip in enumerate(chips):
                slot = 2 * chip[0] + chip[1]
                copy(3 * n + 3 * t + d, t, slot, 1 - c, (x, y, 1 - c)).wait_recv()
        for cp in started:
            cp.wait_send()

    return pl.pallas_call(
        body, name="gather_weights",
        in_specs=[_vmem()] * n, out_specs=[_vmem()] * n,
        out_shape=[jax.ShapeDtypeStruct((N_CHIP,) + s.shape, _WIRE) for s in shards],
        scratch_shapes=[pltpu.SemaphoreType.DMA((6 * n,)), pltpu.SemaphoreType.DMA((6 * n,))],
        compiler_params=pltpu.CompilerParams(vmem_limit_bytes=56 * MIB),
    )(*shards)


def _pair_scratch(half_shape):
    return [pltpu.VMEM(half_shape, F32), pltpu.SemaphoreType.DMA((N_CHIP,)), pltpu.SemaphoreType.DMA((N_CHIP,))]


def _pair_reduce(acc_ref, land_ref, wire_ref, send_sems, recv_sems):
    rh = land_ref.shape[1]
    x, y, c, _ = _place()
    copies = []
    for j in range(N_CHIP):
        give = acc_ref.at[pl.ds(pl.multiple_of(j * 2 * rh + (1 - c) * rh, 8), rh), :]
        cp = pltpu.make_async_remote_copy(src_ref=give, dst_ref=land_ref.at[j], send_sem=send_sems.at[j],
                                          recv_sem=recv_sems.at[j], device_id=(x, y, 1 - c), device_id_type=MESH)
        cp.start()
        copies.append(cp)
    for cp in copies:
        cp.wait()

    def chunk(r, carry):
        theirs = pl.ds(pl.multiple_of(r * ROW_CHUNK, ROW_CHUNK), ROW_CHUNK)
        for j in range(N_CHIP):
            mine = pl.ds(pl.multiple_of(j * 2 * rh + c * rh + r * ROW_CHUNK, 8), ROW_CHUNK)
            s = acc_ref[mine, :] + land_ref[j, theirs, :]
            acc_ref[mine, :] = s
            wire_ref[j, theirs, :] = s.astype(_WIRE)
        return carry

    lax.fori_loop(0, rh // ROW_CHUNK, chunk, 0)


def _grad_exchange(wires, accs):
    n = len(wires)
    halves = [w.shape[1] for w in wires]

    def body(*refs):
        wire, acc, g = refs[:n], refs[n:2 * n], refs[2 * n:3 * n]
        land, own = refs[3 * n:4 * n], refs[4 * n:5 * n]
        ici_send, ici_recv, pair_send, pair_recv, local_sems = refs[5 * n:5 * n + 5]
        x, y, c, chips = _place()
        me = 2 * x + y

        def half_rows(t, half):
            return pl.ds(pl.multiple_of(half * halves[t], 8), halves[t])

        def own_copy(t):
            rows = pl.ds(pl.multiple_of((2 * me + c) * halves[t], 8), halves[t])
            return pltpu.make_async_copy(acc[t].at[rows, :], own[t], local_sems.at[t])

        def ici_copy(t, d):
            chip = chips[d]
            return pltpu.make_async_remote_copy(
                src_ref=wire[t].at[2 * chip[0] + chip[1]], dst_ref=land[t].at[d],
                send_sem=ici_send.at[3 * t + d], recv_sem=ici_recv.at[3 * t + d],
                device_id=(chip[0], chip[1], c), device_id_type=MESH)

        def pair_copy(t, half):
            rows = g[t].at[half_rows(t, half), :]
            return pltpu.make_async_remote_copy(src_ref=rows, dst_ref=rows, send_sem=pair_send.at[t],
                                                recv_sem=pair_recv.at[t], device_id=(x, y, 1 - c), device_id_type=MESH)

        started = []
        for t in range(n):
            own_copy(t).start()
            for d in range(3):
                cp = ici_copy(t, d)
                cp.start()
                started.append(cp)
        for t in range(n):
            own_copy(t).wait()
            for d in range(3):
                ici_copy(t, d).wait_recv()

            def chunk(r, carry, t=t):
                src = pl.ds(pl.multiple_of(r * ROW_CHUNK, ROW_CHUNK), ROW_CHUNK)
                dst = pl.ds(pl.multiple_of(c * halves[t] + r * ROW_CHUNK, 8), ROW_CHUNK)
                s = own[t][src, :]
                for d in range(3):
                    s = s + land[t][d, src, :].astype(F32)
                g[t][dst, :] = s
                return carry

            lax.fori_loop(0, halves[t] // ROW_CHUNK, chunk, 0)
            cp = pair_copy(t, c)
            cp.start()
            started.append(cp)
        for t in range(n):
            pair_copy(t, 1 - c).wait_recv()
        for cp in started:
            cp.wait_send()

    return pl.pallas_call(
        body, name="grad_exchange",
        in_specs=[_vmem()] * n + [_hbm()] * n, out_specs=[_vmem()] * n,
        out_shape=[jax.ShapeDtypeStruct((2 * w.shape[1], w.shape[2]), F32) for w in wires],
        scratch_shapes=[pltpu.VMEM((3,) + w.shape[1:], w.dtype) for w in wires]
        + [pltpu.VMEM(w.shape[1:], F32) for w in wires]
        + [pltpu.SemaphoreType.DMA((3 * n,)), pltpu.SemaphoreType.DMA((3 * n,)), pltpu.SemaphoreType.DMA((n,)),
           pltpu.SemaphoreType.DMA((n,)), pltpu.SemaphoreType.DMA((n,))],
        compiler_params=pltpu.CompilerParams(vmem_limit_bytes=56 * MIB),
    )(*wires, *accs)


def _allreduce_small(packed):
    rows = packed.shape[0]

    def body(p_ref, out_ref, buf_ref, send_sems, recv_sems):
        x, y, c, _ = _place()
        me = 4 * x + 2 * y + c
        buf_ref[me] = p_ref[...]
        copies = []
        for r in range(1, 8):
            rx, ry, rc_ = (r >> 2) & 1, (r >> 1) & 1, r & 1
            peer = (x ^ rx, y ^ ry, c ^ rc_)
            cp = pltpu.make_async_remote_copy(src_ref=buf_ref.at[me], dst_ref=buf_ref.at[me],
                                              send_sem=send_sems.at[r - 1], recv_sem=recv_sems.at[r - 1],
                                              device_id=peer, device_id_type=MESH)
            cp.start()
            copies.append(cp)
        for cp in copies:
            cp.wait()
        acc = buf_ref[0]
        for d in range(1, 8):
            acc = acc + buf_ref[d]
        out_ref[...] = acc

    return pl.pallas_call(
        body, name="allreduce_small",
        in_specs=[_vmem()], out_specs=_vmem(),
        out_shape=jax.ShapeDtypeStruct((rows, 128), F32),
        scratch_shapes=[pltpu.VMEM((8, rows, 128), F32), pltpu.SemaphoreType.DMA((7,)), pltpu.SemaphoreType.DMA((7,))],
        compiler_params=pltpu.CompilerParams(vmem_limit_bytes=32 * MIB),
    )(packed)


def _elementwise(name, fn, ins, out_dtypes, tile_rows=256):
    shape = ins[0].shape
    lead = shape[:-2]
    rows, cols = shape[-2:]
    tr = _tile(rows, tile_rows)
    n_lead = math.prod(lead)
    nr = rows // tr
    flat = [a.reshape((n_lead, rows, cols)) for a in ins]

    def body(*refs):
        outs = fn(*[r[0] for r in refs[:len(ins)]])
        for o_ref, o in zip(refs[len(ins):], outs):
            o_ref[0] = o.astype(o_ref.dtype)

    spec = pl.BlockSpec((1, tr, cols), lambda i: (i // nr, i % nr, 0))
    res = pl.pallas_call(
        body, name=name, grid=(n_lead * nr,),
        in_specs=[spec] * len(ins), out_specs=[spec] * len(out_dtypes),
        out_shape=[jax.ShapeDtypeStruct((n_lead, rows, cols), dt) for dt in out_dtypes],
        compiler_params=_params(32),
    )(*flat)
    return [r.reshape(shape) for r in res]


def _adamw_math(w, g, m, v):
    m = ADAM_B1 * m + (1.0 - ADAM_B1) * g
    v = ADAM_B2 * v + (1.0 - ADAM_B2) * (g * g)
    m_hat = m / (1.0 - ADAM_B1 ** ADAM_STEP)
    v_hat = v / (1.0 - ADAM_B2 ** ADAM_STEP)
    delta = -ADAM_LR * (m_hat / (jnp.sqrt(v_hat) + ADAM_EPS) + ADAM_WD * w)
    return delta, m, v


def _adamw(name, w, g, m, v, tile_rows=256):
    return _elementwise(name, _adamw_math, [w, g, m, v], [F32, F32, F32], tile_rows)


_SMALL = ("ln_in_g", "ln_in_b", "b_in", "attn_sinks", "sgu_ln_g", "sgu_ln_b", "sgu_w", "sgu_b", "b_out",
          "ln_mix_g", "ln_mix_b", "ln_ffn_g", "ln_ffn_b")


def _pack(arrs):
    parts = []
    for a in arrs:
        flat = a.reshape(-1)
        pad = (-flat.shape[0]) % 1024
        parts.append(jnp.pad(flat, (0, pad)) if pad else flat)
    return jnp.concatenate(parts).reshape(-1, 128)


def _unpack(packed, like):
    flat = packed.reshape(-1)
    out, off = [], 0
    for a in like:
        n = math.prod(a.shape)
        out.append(flat[off:off + n].reshape(a.shape))
        off += n + ((-n) % 1024)
    return out


def kernel(x, positions, ln_in_g, ln_in_b, w_in, b_in, attn_sinks, sgu_ln_g, sgu_ln_b, sgu_w, sgu_b, w_out, b_out, ln_mix_g, ln_mix_b, w_gate, w_up, w_down, ln_ffn_g, ln_ffn_b, loss_target, m_ln_in_g, m_ln_in_b, m_w_in, m_b_in, m_attn_sinks, m_sgu_ln_g, m_sgu_ln_b, m_sgu_w, m_sgu_b, m_w_out, m_b_out, m_ln_mix_g, m_ln_mix_b, m_w_gate, m_w_up, m_w_down, m_ln_ffn_g, m_ln_ffn_b, v_ln_in_g, v_ln_in_b, v_w_in, v_b_in, v_attn_sinks, v_sgu_ln_g, v_sgu_ln_b, v_sgu_w, v_sgu_b, v_w_out, v_b_out, v_ln_mix_g, v_ln_mix_b, v_w_gate, v_w_up, v_w_down, v_ln_ffn_g, v_ln_ffn_b):
    weights = dict(ln_in_g=ln_in_g, ln_in_b=ln_in_b, w_in=w_in, b_in=b_in, attn_sinks=attn_sinks, sgu_ln_g=sgu_ln_g,
                   sgu_ln_b=sgu_ln_b, sgu_w=sgu_w, sgu_b=sgu_b, w_out=w_out, b_out=b_out, ln_mix_g=ln_mix_g,
                   ln_mix_b=ln_mix_b, w_gate=w_gate, w_up=w_up, w_down=w_down, ln_ffn_g=ln_ffn_g, ln_ffn_b=ln_ffn_b)
    mom_m = dict(ln_in_g=m_ln_in_g, ln_in_b=m_ln_in_b, w_in=m_w_in, b_in=m_b_in, attn_sinks=m_attn_sinks,
                 sgu_ln_g=m_sgu_ln_g, sgu_ln_b=m_sgu_ln_b, sgu_w=m_sgu_w, sgu_b=m_sgu_b, w_out=m_w_out, b_out=m_b_out,
                 ln_mix_g=m_ln_mix_g, ln_mix_b=m_ln_mix_b, w_gate=m_w_gate, w_up=m_w_up, w_down=m_w_down,
                 ln_ffn_g=m_ln_ffn_g, ln_ffn_b=m_ln_ffn_b)
    mom_v = dict(ln_in_g=v_ln_in_g, ln_in_b=v_ln_in_b, w_in=v_w_in, b_in=v_b_in, attn_sinks=v_attn_sinks,
                 sgu_ln_g=v_sgu_ln_g, sgu_ln_b=v_sgu_ln_b, sgu_w=v_sgu_w, sgu_b=v_sgu_b, w_out=v_w_out, b_out=v_b_out,
                 ln_mix_g=v_ln_mix_g, ln_mix_b=v_ln_mix_b, w_gate=v_w_gate, w_up=v_w_up, w_down=v_w_down,
                 ln_ffn_g=v_ln_ffn_g, ln_ffn_b=v_ln_ffn_b)
    order = list(weights)
    big = ("w_in", "w_out", "w_gate", "w_up", "w_down")

    s_len = x.shape[1]
    xs = x.reshape(s_len, D_MODEL)
    tgt = loss_target.reshape(s_len, D_MODEL)
    pos_col = positions.reshape(s_len, 1)
    g0, b0 = ln_in_g.reshape(1, D_MODEL), ln_in_b.reshape(1, D_MODEL)
    sinks = attn_sinks.reshape(N_Q)
    sgu_w3 = sgu_w.reshape(N_GRP, BLK, BLK)
    sgu_bt = sgu_b.reshape(N_GRP, BLK).T

    col_sharded = ("w_in", "w_gate", "w_up")

    def rowmajor(name, a):
        return jnp.swapaxes(a[0], 0, 1) if name in col_sharded else a[0]

    def as_given(name, a):
        return (jnp.swapaxes(a, 0, 1) if name in col_sharded else a)[None]

    shards = [rowmajor(n, weights[n]) for n in big]
    gw_in, gw_out, gw_gate, gw_up, gw_down = _gather_weights(shards)
    w_in_full = gw_in.reshape(IN_W, D_MODEL)
    w_out_full = gw_out.reshape(D_MODEL, D_MODEL)

    q, k, v, su, sv, tc, t1, t2 = _ln_inproj(xs, pos_col, g0, b0, w_in_full, b_in)
    mc = _mixer_fwd(q, k, v, su, sv, sinks, sgu_ln_g, sgu_ln_b, sgu_w3, sgu_bt)
    r1 = _outproj(mc, w_out_full, b_out, xs, g0, b0)
    gact, uact = _ffn_up(r1, ln_mix_g, ln_mix_b, gw_gate, gw_up)
    dr2, loss_cols, d_ln_ffn_g, d_ln_ffn_b = _ffn_down_loss(gact, uact, gw_down, r1, ln_mix_g, ln_mix_b,
                                                            ln_ffn_g, ln_ffn_b, tgt)

    dg, du, acc_down, wire_down = _ffn_bwd_a(dr2, gact, uact, gw_down)
    dh1a, acc_gate, wire_gate = _ffn_bwd_g(dr2, dg, r1, ln_mix_g, ln_mix_b, gw_gate)
    dr1, acc_up, wire_up, d_ln_mix_g, d_ln_mix_b = _ffn_bwd_u(dh1a, du, r1, ln_mix_g, ln_mix_b, gw_up)
    dmc, acc_out, wire_out, d_b_out = _outproj_bwd(dr1, mc, w_out_full)
    (dq, dkv, dsuv, dbq, dbkv, dbsuv, d_sink, d_sgu_ln_g, d_sgu_ln_b, d_sgu_w, d_sgu_bt) = _mixer_bwd(
        q, k, v, su, sv, dmc, tc, t1, t2, sinks, sgu_ln_g, sgu_ln_b, sgu_w3, sgu_bt)
    grad_x, acc_in, wire_in, d_ln_in_g, d_ln_in_b = _inproj_bwd(dq, dkv, dsuv, dr1, xs, g0, b0, w_in_full)

    reduced = _grad_exchange([wire_in, wire_out, wire_gate, wire_up, wire_down],
                             [acc_in, acc_out, acc_gate, acc_up, acc_down])
    small_local = dict(
        ln_in_g=d_ln_in_g.reshape(ln_in_g.shape), ln_in_b=d_ln_in_b.reshape(ln_in_b.shape),
        b_in=jnp.concatenate([dbq, dbkv, dbsuv], axis=1), attn_sinks=d_sink[:, :N_Q],
        sgu_ln_g=d_sgu_ln_g, sgu_ln_b=d_sgu_ln_b, sgu_w=d_sgu_w.reshape(sgu_w.shape),
        sgu_b=d_sgu_bt[:, :N_GRP].T.reshape(sgu_b.shape), b_out=d_b_out,
        ln_mix_g=d_ln_mix_g, ln_mix_b=d_ln_mix_b, ln_ffn_g=d_ln_ffn_g, ln_ffn_b=d_ln_ffn_b)
    small_sum = _allreduce_small(_pack([small_local[n] for n in _SMALL] + [loss_cols]))
    like = [weights[n] for n in _SMALL]
    *small_grads, loss_sum = _unpack(small_sum, like + [loss_cols])
    loss = jnp.sum(loss_sum) * (0.5 / D_MODEL)
    grads = dict(zip(_SMALL, small_grads))

    delta, new_m, new_v = {}, {}, {}
    for t, name in enumerate(big):
        d_, m_, v_ = _adamw("adamw_" + name, shards[t], reduced[t], rowmajor(name, mom_m[name]),
                            rowmajor(name, mom_v[name]))
        grads[name] = as_given(name, reduced[t])
        delta[name], new_m[name], new_v[name] = as_given(name, d_), as_given(name, m_), as_given(name, v_)
    spare = jnp.zeros_like(loss_cols)
    packs = [_pack([src[n] for n in _SMALL] + [spare]) for src in (weights, mom_m, mom_v)]
    d_, m_, v_ = _adamw("adamw_small", packs[0], small_sum, packs[1], packs[2], tile_rows=packs[0].shape[0])
    for n, a, b, c_ in zip(_SMALL, _unpack(d_, like), _unpack(m_, like), _unpack(v_, like)):
        delta[n], new_m[n], new_v[n] = a, b, c_

    return (loss, grad_x.reshape(x.shape), *[grads[n] for n in order], *[delta[n] for n in order],
            *[new_m[n] for n in order], *[new_v[n] for n in order])
```

```python
import functools
import math

import jax
import jax.numpy as jnp
from jax import lax
from jax.experimental import pallas as pl
from jax.experimental.pallas import tpu as pltpu

F32 = jnp.float32
_MXU = jnp.bfloat16
_WIRE = jnp.bfloat16

D_MODEL = 1024
ATTN_W = 512
SGU_W = 512
HEAD_DIM = 64
N_Q = 8
N_KV = 2
Q_PER_KV = 4
KV_W = 128
BLK = 128
ROT_DIM = 16
ROPE_THETA = 500000.0
N_GRP = 4
GRP_DIM = 128
D_FF = 2816
IN_W = 1792
LN_EPS = 1e-5
ALPHA = 2.0 ** 0.25
N_CHIP = 4
FF_SH = D_FF // N_CHIP
IN_SH = IN_W // N_CHIP
OUT_SH = D_MODEL // N_CHIP
ROW_CHUNK = 32

ADAM_LR = 0.001
ADAM_B1 = 0.9
ADAM_B2 = 0.999
ADAM_EPS = 1e-08
ADAM_WD = 0.01
ADAM_STEP = 10

SQRT_HALF = 0.7071067811865476
INV_SQRT_2PI = 0.3989422804014327
MESH_AXES = ("x", "y", "c")
MESH = pl.DeviceIdType.MESH
MIB = 2 ** 20


def _vmem():
    return pl.BlockSpec(memory_space=pltpu.VMEM)


def _smem():
    return pl.BlockSpec(memory_space=pltpu.SMEM)


def _hbm():
    return pl.BlockSpec(memory_space=pl.ANY)


def _params(vmem_mib=48):
    return pltpu.CompilerParams(dimension_semantics=("arbitrary",), vmem_limit_bytes=vmem_mib * MIB)


def _tile(n, cap):
    if n <= cap:
        return n
    for t in range(cap - cap % 16, 0, -16):
        if n % t == 0:
            return t
    raise ValueError((n, cap))


def _rows(tm, width):
    return pl.BlockSpec((tm, width), lambda i: (i, 0))


def _const2(shape):
    return pl.BlockSpec(shape, lambda i: (0,) * len(shape))


def _ln(x, g, b):
    mu = jnp.mean(x, axis=-1, keepdims=True)
    xc = x - mu
    var = jnp.mean(xc * xc, axis=-1, keepdims=True)
    rstd = lax.rsqrt(var + LN_EPS)
    xhat = xc * rstd
    return xhat * g + b, xhat, rstd


def _ln_bwd(dy, xhat, rstd, g):
    gdy = dy * g
    m1 = jnp.mean(gdy, axis=-1, keepdims=True)
    m2 = jnp.mean(gdy * xhat, axis=-1, keepdims=True)
    return rstd * (gdy - m1 - xhat * m2)


def _colsum(a):
    return jnp.sum(a, axis=0, keepdims=True)


def _gelu(x):
    return 0.5 * x * (1.0 + lax.erf(x * SQRT_HALF))


def _gelu_grad(x):
    return 0.5 * (1.0 + lax.erf(x * SQRT_HALF)) + x * jnp.exp(-0.5 * x * x) * INV_SQRT_2PI


def _dot(a, b):
    return jnp.dot(a, b, preferred_element_type=F32)


def _dot_nt(a, b):
    return lax.dot_general(a, b, (((1,), (1,)), ((), ())), preferred_element_type=F32)


def _dot_tn(a, b):
    return lax.dot_general(a, b, (((0,), (0,)), ((), ())), preferred_element_type=F32)


def _rope(t, tc, t1, t2):
    n = t.shape[1]
    rep = n // 128
    if rep > 1:
        tc, t1, t2 = (jnp.tile(a, (1, rep)) for a in (tc, t1, t2))
    return t * tc + pltpu.roll(t, n - 8, 1) * t1 + pltpu.roll(t, 8, 1) * t2


def _rope_bwd(d, tc, t1, t2):
    n = d.shape[1]
    rep = n // 128
    if rep > 1:
        tc, t1, t2 = (jnp.tile(a, (1, rep)) for a in (tc, t1, t2))
    return d * tc + pltpu.roll(d * t1, 8, 1) + pltpu.roll(d * t2, n - 8, 1)


def _band_mask(first_block):
    qi = lax.broadcasted_iota(jnp.int32, (BLK, 2 * BLK), 0)
    kj = lax.broadcasted_iota(jnp.int32, (BLK, 2 * BLK), 1)
    shut = jnp.where(first_block, 2 * BLK, 0)
    prev_ok = jnp.logical_and(kj < BLK, kj > qi + shut)
    cur_ok = jnp.logical_and(kj >= BLK, (kj - BLK) <= qi)
    return jnp.logical_or(prev_ok, cur_ok)


def _causal_w(w_ref, h):
    t = lax.broadcasted_iota(jnp.int32, (BLK, BLK), 0)
    s = lax.broadcasted_iota(jnp.int32, (BLK, BLK), 1)
    return jnp.where(s <= t, w_ref[h], 0.0)


def _lane_put(vals, width):
    rows = vals[0].shape[0]
    lane = lax.broadcasted_iota(jnp.int32, (rows, width), 1)
    out = jnp.zeros((rows, width), F32)
    for k, v in enumerate(vals):
        out = out + jnp.where(lane == k, v, 0.0)
    return out


def _rope_consts():
    lane = jnp.arange(128) % HEAD_DIM
    inv_freq = ROPE_THETA ** (-jnp.arange(0, ROT_DIM, 2, dtype=F32) / ROT_DIM)
    rot = lane < ROT_DIM
    freq = jnp.where(rot, inv_freq[lane % (ROT_DIM // 2)], 0.0)
    rows = [freq, rot.astype(F32), 1.0 - rot.astype(F32), (lane < ROT_DIM // 2).astype(F32),
            jnp.logical_and(lane >= ROT_DIM // 2, rot).astype(F32)]
    rows += [jnp.zeros((128,), F32)] * 3
    return jnp.stack(rows).astype(F32)


def _ln_inproj(x, pos_col, g0, b0, w_in, b_in, shards):
    s_len = x.shape[0]
    tm = _tile(s_len, 512)

    n = len(shards)

    def body(x_ref, pos_ref, g_ref, b_ref, w_ref, bi_ref, rc_ref, *rest):
        q_ref, k_ref, v_ref, su_ref, sv_ref, tc_ref, t1_ref, t2_ref = rest[n:n + 8]
        gathered = rest[n + 8:2 * n + 8]
        gather = _Gather(rest[:n], rest[2 * n + 8:3 * n + 8], rest[3 * n + 8], rest[3 * n + 9])
        flush_sems = rest[3 * n + 10]
        i = pl.program_id(0)

        @pl.when(i == 0)
        def _():
            gather.start()

        h0, _, _ = _ln(x_ref[...], g_ref[...], b_ref[...])
        proj = _dot_nt(h0.astype(_MXU), w_ref[...]) + bi_ref[...]
        ang = pos_ref[...].astype(F32) * rc_ref[0:1, :]
        cs = jnp.cos(ang)
        sn = jnp.sin(ang)
        tc = cs * rc_ref[1:2, :] + rc_ref[2:3, :]
        t1 = -sn * rc_ref[3:4, :]
        t2 = sn * rc_ref[4:5, :]
        tc_ref[...] = tc
        t1_ref[...] = t1
        t2_ref[...] = t2
        q = _rope(proj[:, 0:ATTN_W], tc, t1, t2) * (HEAD_DIM ** -0.5)
        q_ref[...] = q.astype(_MXU)
        k_ref[...] = _rope(proj[:, ATTN_W:ATTN_W + KV_W], tc, t1, t2).astype(_MXU)
        v_ref[...] = proj[:, ATTN_W + KV_W:ATTN_W + 2 * KV_W].astype(_MXU)
        su_ref[...] = proj[:, ATTN_W + 2 * KV_W:ATTN_W + 2 * KV_W + SGU_W]
        sv_ref[...] = proj[:, ATTN_W + 2 * KV_W + SGU_W:IN_W]

        @pl.when(i == pl.num_programs(0) - 1)
        def _():
            gather.finish()
            gather.flush(gathered, flush_sems)

    sd = jax.ShapeDtypeStruct
    return pl.pallas_call(
        body, name="ln_inproj", grid=(s_len // tm,),
        in_specs=[_rows(tm, D_MODEL), _rows(tm, 1), _const2((1, D_MODEL)), _const2((1, D_MODEL)), _vmem(),
                  _const2((1, IN_W)), _const2((8, 128))] + [_vmem()] * n,
        out_specs=[_rows(tm, ATTN_W), _rows(tm, KV_W), _rows(tm, KV_W), _rows(tm, SGU_W), _rows(tm, SGU_W),
                   _rows(tm, 128), _rows(tm, 128), _rows(tm, 128)] + [_hbm()] * n,
        out_shape=[sd((s_len, ATTN_W), _MXU), sd((s_len, KV_W), _MXU), sd((s_len, KV_W), _MXU),
                   sd((s_len, SGU_W), F32), sd((s_len, SGU_W), F32),
                   sd((s_len, 128), F32), sd((s_len, 128), F32), sd((s_len, 128), F32)] + _Gather.out_shapes(shards),
        scratch_shapes=_Gather.scratch(shards),
        compiler_params=_params(56),
    )(x, pos_col, g0, b0, w_in, b_in, _rope_consts(), *shards)


def _attn_probs(qh, kh, sink, allowed):
    s = jnp.where(allowed, _dot_nt(qh, kh), -1e30)
    m = jnp.maximum(jnp.max(s, axis=-1, keepdims=True), sink)
    p = jnp.exp(s - m)
    ps = jnp.exp(sink - m)
    inv = 1.0 / (jnp.sum(p, axis=-1, keepdims=True) + ps)
    return p * inv, ps * inv


def _sgu_fwd(su, sv, lg, lb, w_ref, bt_ref):
    u = _gelu(su)
    vv, vhat, rstd = _ln(_gelu(sv), lg, lb)
    vvb = vv.astype(_MXU)
    wcs, mixed = [], []
    for h in range(N_GRP):
        wc = _causal_w(w_ref, h).astype(_MXU)
        wcs.append(wc)
        mixed.append(_dot(wc, vvb[:, h * GRP_DIM:(h + 1) * GRP_DIM]) + bt_ref[:, h:h + 1])
    return u, vhat, rstd, vvb, wcs, jnp.concatenate(mixed, axis=1)


def _prev_map(i):
    return (jnp.maximum(i - 1, 0), 0)


def _mixer_fwd(q, k, v, su, sv, sinks, sg, sb, sgu_w, sgu_bt, shards):
    s_len = q.shape[0]
    nb = s_len // BLK
    n = len(shards)

    def body(q_ref, kc_ref, kp_ref, vc_ref, vp_ref, su_ref, sv_ref, sink_ref, lg_ref, lb_ref, w_ref, bt_ref, *rest):
        mc_ref = rest[n]
        gathered = rest[n + 1:2 * n + 1]
        gather = _Gather(rest[:n], rest[2 * n + 1:3 * n + 1], rest[3 * n + 1], rest[3 * n + 2])
        flush_sems = rest[3 * n + 3]
        i = pl.program_id(0)

        @pl.when(i == 0)
        def _():
            gather.start()

        @pl.when(i == nb - 1)
        def _():
            gather.finish()
            gather.flush(gathered, flush_sems)

        allowed = _band_mask(i == 0)
        kb = jnp.concatenate([kp_ref[...], kc_ref[...]], axis=0)
        vb = jnp.concatenate([vp_ref[...], vc_ref[...]], axis=0)
        qv = q_ref[...]
        outs = []
        for h in range(N_Q):
            g = h // Q_PER_KV
            kh = kb[:, g * HEAD_DIM:(g + 1) * HEAD_DIM]
            vh = vb[:, g * HEAD_DIM:(g + 1) * HEAD_DIM]
            probs, _ = _attn_probs(qv[:, h * HEAD_DIM:(h + 1) * HEAD_DIM], kh, sink_ref[h], allowed)
            outs.append(_dot(probs.astype(_MXU), vh))
        u, _, _, _, _, mixed = _sgu_fwd(su_ref[...], sv_ref[...], lg_ref[...], lb_ref[...], w_ref, bt_ref)
        mc_ref[...] = jnp.concatenate(outs + [u * mixed], axis=1).astype(_MXU)

    cur = lambda w: pl.BlockSpec((BLK, w), lambda i: (i, 0))
    prev = lambda w: pl.BlockSpec((BLK, w), _prev_map)
    return pl.pallas_call(
        body, name="mixer_fwd", grid=(nb,),
        in_specs=[cur(ATTN_W), cur(KV_W), prev(KV_W), cur(KV_W), prev(KV_W), cur(SGU_W), cur(SGU_W), _smem(),
                  _const2((1, SGU_W)), _const2((1, SGU_W)), _const2((N_GRP, BLK, BLK)), _const2((BLK, N_GRP))]
        + [_vmem()] * n,
        out_specs=[cur(D_MODEL)] + [_hbm()] * n,
        out_shape=[jax.ShapeDtypeStruct((s_len, D_MODEL), _MXU)] + _Gather.out_shapes(shards),
        scratch_shapes=_Gather.scratch(shards),
        compiler_params=_params(48),
    )(q, k, k, v, v, su, sv, sinks, sg, sb, sgu_w, sgu_bt, *shards)


def _outproj(mc, w_out, b_out, x, g0, b0):
    s_len = x.shape[0]
    tm = _tile(s_len, 512)

    def body(mc_ref, w_ref, bo_ref, x_ref, g_ref, b_ref, r1_ref):
        h0, _, _ = _ln(x_ref[...], g_ref[...], b_ref[...])
        r1_ref[...] = ALPHA * h0 + (_dot(mc_ref[...], w_ref[...]) + bo_ref[...])

    return pl.pallas_call(
        body, name="outproj", grid=(s_len // tm,),
        in_specs=[_rows(tm, D_MODEL), _vmem(), _const2((1, D_MODEL)), _rows(tm, D_MODEL),
                  _const2((1, D_MODEL)), _const2((1, D_MODEL))],
        out_specs=_rows(tm, D_MODEL),
        out_shape=jax.ShapeDtypeStruct((s_len, D_MODEL), F32),
        compiler_params=_params(32),
    )(mc, w_out, b_out, x, g0, b0)


def _ffn_spec(tm):
    return pl.BlockSpec((N_CHIP, tm, FF_SH), lambda i: (0, i, 0))


def _ffn_up(r1, g1, b1, wg, wu):
    s_len = r1.shape[0]
    tm = _tile(s_len, 256)

    def body(r1_ref, g_ref, b_ref, wg_ref, wu_ref, go_ref, uo_ref):
        h1, _, _ = _ln(r1_ref[...], g_ref[...], b_ref[...])
        h1b = h1.astype(_MXU)
        for j in range(N_CHIP):
            go_ref[j] = _dot_nt(h1b, wg_ref[j])
            uo_ref[j] = _dot_nt(h1b, wu_ref[j])

    sd = jax.ShapeDtypeStruct((N_CHIP, s_len, FF_SH), F32)
    return pl.pallas_call(
        body, name="ffn_up", grid=(s_len // tm,),
        in_specs=[_rows(tm, D_MODEL), _const2((1, D_MODEL)), _const2((1, D_MODEL)), _vmem(), _vmem()],
        out_specs=[_ffn_spec(tm), _ffn_spec(tm)],
        out_shape=[sd, sd],
        compiler_params=_params(56),
    )(r1, g1, b1, wg, wu)


def _silu_parts(g):
    sg = 1.0 / (1.0 + jnp.exp(-g))
    return g * sg, sg


def _ffn_down_loss(gact, uact, wd, r1, g1, b1, g2, b2, target):
    s_len = r1.shape[0]
    tm = _tile(s_len, 256)

    def body(g_ref, u_ref, wd_ref, r1_ref, g1_ref, b1_ref, g2_ref, b2_ref, t_ref,
             dr2_ref, loss_ref, dg2_ref, db2_ref):
        i = pl.program_id(0)
        f = jnp.zeros((tm, D_MODEL), F32)
        for j in range(N_CHIP):
            silu, _ = _silu_parts(g_ref[j])
            f = f + _dot((silu * u_ref[j]).astype(_MXU), wd_ref[j])
        h1, _, _ = _ln(r1_ref[...], g1_ref[...], b1_ref[...])
        h2, r2hat, rstd2 = _ln(ALPHA * h1 + f, g2_ref[...], b2_ref[...])
        diff = h2 - t_ref[...]
        dh2 = diff * (1.0 / D_MODEL)

        @pl.when(i == 0)
        def _():
            loss_ref[...] = jnp.zeros_like(loss_ref)
            dg2_ref[...] = jnp.zeros_like(dg2_ref)
            db2_ref[...] = jnp.zeros_like(db2_ref)

        loss_ref[...] += _colsum(diff * diff)
        dg2_ref[...] += _colsum(dh2 * r2hat)
        db2_ref[...] += _colsum(dh2)
        dr2_ref[...] = _ln_bwd(dh2, r2hat, rstd2, g2_ref[...])

    vec = jax.ShapeDtypeStruct((1, D_MODEL), F32)
    c = _const2((1, D_MODEL))
    return pl.pallas_call(
        body, name="ffn_down_loss", grid=(s_len // tm,),
        in_specs=[_ffn_spec(tm), _ffn_spec(tm), _vmem(), _rows(tm, D_MODEL), c, c, c, c, _rows(tm, D_MODEL)],
        out_specs=[_rows(tm, D_MODEL), c, c, c],
        out_shape=[jax.ShapeDtypeStruct((s_len, D_MODEL), F32), vec, vec, vec],
        compiler_params=_params(48),
    )(gact, uact, wd, r1, g1, b1, g2, b2, target)


def _ffn_bwd_a(dr2, gact, uact, wd):
    s_len = dr2.shape[0]
    tm = _tile(s_len, 256)

    def body(dr2_ref, g_ref, u_ref, wd_ref, dg_ref, du_ref, dwd_ref, wire_ref, land_ref, send_sem, recv_sem):
        i = pl.program_id(0)

        @pl.when(i == 0)
        def _():
            dwd_ref[...] = jnp.zeros_like(dwd_ref)

        dfb = dr2_ref[...].astype(_MXU)
        for j in range(N_CHIP):
            g = g_ref[j]
            u = u_ref[j]
            silu, sg = _silu_parts(g)
            da = _dot_nt(dfb, wd_ref[j])
            dg_ref[j] = (da * u * (sg * (1.0 + g * (1.0 - sg)))).astype(_MXU)
            du_ref[j] = (da * silu).astype(_MXU)
            dwd_ref[j * FF_SH:(j + 1) * FF_SH, :] += _dot_tn((silu * u).astype(_MXU), dfb)

        @pl.when(i == pl.num_programs(0) - 1)
        def _():
            _pair_reduce(dwd_ref, land_ref, wire_ref, send_sem, recv_sem)

    sd = jax.ShapeDtypeStruct((N_CHIP, s_len, FF_SH), _MXU)
    return pl.pallas_call(
        body, name="ffn_bwd_a", grid=(s_len // tm,),
        in_specs=[_rows(tm, D_MODEL), _ffn_spec(tm), _ffn_spec(tm), _vmem()],
        out_specs=[_ffn_spec(tm), _ffn_spec(tm), _vmem(), _vmem()],
        out_shape=[sd, sd, jax.ShapeDtypeStruct((D_FF, D_MODEL), F32),
                   jax.ShapeDtypeStruct((N_CHIP, FF_SH // 2, D_MODEL), _WIRE)],
        scratch_shapes=_pair_scratch((N_CHIP, FF_SH // 2, D_MODEL)),
        compiler_params=_params(58),
    )(dr2, gact, uact, wd)


def _ffn_bwd_g(dr2, dg, r1, g1, b1, wg, prev_wire):
    s_len = dr2.shape[0]
    tm = _tile(s_len, 256)

    def body(dr2_ref, dg_ref, r1_ref, g1_ref, b1_ref, wg_ref, pw_ref, dh1_ref, dwg_ref, wire_ref, pl_ref,
             land_ref, send_sem, recv_sem, xl_ref, x_send, x_recv, x_flush):
        i = pl.program_id(0)
        exchange = _ChipExchange(pw_ref, xl_ref, x_send, x_recv)

        @pl.when(i == 0)
        def _():
            exchange.start()
            dwg_ref[...] = jnp.zeros_like(dwg_ref)

        h1, _, _ = _ln(r1_ref[...], g1_ref[...], b1_ref[...])
        h1b = h1.astype(_MXU)
        dh1 = ALPHA * dr2_ref[...]
        for j in range(N_CHIP):
            dgj = dg_ref[j]
            dh1 = dh1 + _dot(dgj, wg_ref[j])
            dwg_ref[j * FF_SH:(j + 1) * FF_SH, :] += _dot_tn(dgj, h1b)
        dh1_ref[...] = dh1

        @pl.when(i == pl.num_programs(0) - 1)
        def _():
            _pair_reduce(dwg_ref, land_ref, wire_ref, send_sem, recv_sem)
            exchange.finish_to(pl_ref, x_flush)

    c = _const2((1, D_MODEL))
    return pl.pallas_call(
        body, name="ffn_bwd_g", grid=(s_len // tm,),
        in_specs=[_rows(tm, D_MODEL), _ffn_spec(tm), _rows(tm, D_MODEL), c, c, _vmem(), _vmem()],
        out_specs=[_rows(tm, D_MODEL), _vmem(), _vmem(), _hbm()],
        out_shape=[jax.ShapeDtypeStruct((s_len, D_MODEL), F32), jax.ShapeDtypeStruct((D_FF, D_MODEL), F32),
                   jax.ShapeDtypeStruct((N_CHIP, FF_SH // 2, D_MODEL), _WIRE), _ChipExchange.land_shape(prev_wire)],
        scratch_shapes=_pair_scratch((N_CHIP, FF_SH // 2, D_MODEL)) + _ChipExchange.scratch(prev_wire),
        compiler_params=_params(58),
    )(dr2, dg, r1, g1, b1, wg, prev_wire)


def _ffn_bwd_u(dh1a, du, r1, g1, b1, wu, prev_wire):
    s_len = dh1a.shape[0]
    tm = _tile(s_len, 256)

    def body(dh1_ref, du_ref, r1_ref, g1_ref, b1_ref, wu_ref, pw_ref,
             dr1_ref, dwu_ref, wire_ref, dg1_ref, db1_ref, pl_ref,
             land_ref, send_sem, recv_sem, xl_ref, x_send, x_recv, x_flush):
        i = pl.program_id(0)
        exchange = _ChipExchange(pw_ref, xl_ref, x_send, x_recv)

        @pl.when(i == 0)
        def _():
            exchange.start()
            dwu_ref[...] = jnp.zeros_like(dwu_ref)
            dg1_ref[...] = jnp.zeros_like(dg1_ref)
            db1_ref[...] = jnp.zeros_like(db1_ref)

        h1, r1hat, rstd1 = _ln(r1_ref[...], g1_ref[...], b1_ref[...])
        h1b = h1.astype(_MXU)
        dh1 = dh1_ref[...]
        for j in range(N_CHIP):
            duj = du_ref[j]
            dh1 = dh1 + _dot(duj, wu_ref[j])
            dwu_ref[j * FF_SH:(j + 1) * FF_SH, :] += _dot_tn(duj, h1b)
        dg1_ref[...] += _colsum(dh1 * r1hat)
        db1_ref[...] += _colsum(dh1)
        dr1_ref[...] = _ln_bwd(dh1, r1hat, rstd1, g1_ref[...])

        @pl.when(i == pl.num_programs(0) - 1)
        def _():
            _pair_reduce(dwu_ref, land_ref, wire_ref, send_sem, recv_sem)
            exchange.finish_to(pl_ref, x_flush)

    vec = jax.ShapeDtypeStruct((1, D_MODEL), F32)
    c = _const2((1, D_MODEL))
    return pl.pallas_call(
        body, name="ffn_bwd_u", grid=(s_len // tm,),
        in_specs=[_rows(tm, D_MODEL), _ffn_spec(tm), _rows(tm, D_MODEL), c, c, _vmem(), _vmem()],
        out_specs=[_rows(tm, D_MODEL), _vmem(), _vmem(), c, c, _hbm()],
        out_shape=[jax.ShapeDtypeStruct((s_len, D_MODEL), F32), jax.ShapeDtypeStruct((D_FF, D_MODEL), F32),
                   jax.ShapeDtypeStruct((N_CHIP, FF_SH // 2, D_MODEL), _WIRE), vec, vec,
                   _ChipExchange.land_shape(prev_wire)],
        scratch_shapes=_pair_scratch((N_CHIP, FF_SH // 2, D_MODEL)) + _ChipExchange.scratch(prev_wire),
        compiler_params=_params(58),
    )(dh1a, du, r1, g1, b1, wu, prev_wire)


def _outproj_bwd(dr1, mc, w_out, prev_wire):
    s_len = dr1.shape[0]
    tm = _tile(s_len, 512)

    def body(dr1_ref, mc_ref, w_ref, pw_ref, dmc_ref, dw_ref, wire_ref, db_ref, pl_ref,
             land_ref, send_sem, recv_sem, xl_ref, x_send, x_recv, x_flush):
        i = pl.program_id(0)
        exchange = _ChipExchange(pw_ref, xl_ref, x_send, x_recv)

        @pl.when(i == 0)
        def _():
            exchange.start()
            dw_ref[...] = jnp.zeros_like(dw_ref)
            db_ref[...] = jnp.zeros_like(db_ref)

        d = dr1_ref[...]
        db_ref[...] += _colsum(d)
        db16 = d.astype(_MXU)
        dmc_ref[...] = _dot_nt(db16, w_ref[...])
        dw_ref[...] += _dot_tn(mc_ref[...], db16)

        @pl.when(i == pl.num_programs(0) - 1)
        def _():
            _pair_reduce(dw_ref, land_ref, wire_ref, send_sem, recv_sem)
            exchange.finish_to(pl_ref, x_flush)

    return pl.pallas_call(
        body, name="outproj_bwd", grid=(s_len // tm,),
        in_specs=[_rows(tm, D_MODEL), _rows(tm, D_MODEL), _vmem(), _vmem()],
        out_specs=[_rows(tm, D_MODEL), _vmem(), _vmem(), _const2((1, D_MODEL)), _hbm()],
        out_shape=[jax.ShapeDtypeStruct((s_len, D_MODEL), F32), jax.ShapeDtypeStruct((D_MODEL, D_MODEL), F32),
                   jax.ShapeDtypeStruct((N_CHIP, OUT_SH // 2, D_MODEL), _WIRE), jax.ShapeDtypeStruct((1, D_MODEL), F32),
                   _ChipExchange.land_shape(prev_wire)],
        scratch_shapes=_pair_scratch((N_CHIP, OUT_SH // 2, D_MODEL)) + _ChipExchange.scratch(prev_wire),
        compiler_params=_params(48),
    )(dr1, mc, w_out, prev_wire)


def _mixer_bwd(q, k, v, su, sv, dmc, tc, t1, t2, sinks, sg, sb, sgu_w, sgu_bt, prev_wire):
    s_len = q.shape[0]
    nb = s_len // BLK

    def body(q_ref, kc_ref, kp_ref, vc_ref, vp_ref, su_ref, sv_ref, dmc_ref,
             tc_ref, t1_ref, t2_ref, tcp_ref, t1p_ref, t2p_ref,
             sink_ref, lg_ref, lb_ref, w_ref, bt_ref, pw_ref,
             dq_ref, dkv_ref, dsuv_ref, dbq_ref, dbkv_ref, dbsuv_ref,
             dsink_ref, dlg_ref, dlb_ref, dw_ref, dbt_ref, pl_ref, carry_ref, xl_ref, x_send, x_recv, x_flush):
        i = pl.program_id(0)
        exchange = _ChipExchange(pw_ref, xl_ref, x_send, x_recv)

        @pl.when(i == 0)
        def _():
            exchange.start()

        @pl.when(i == 0)
        def _():
            for r in (dbq_ref, dbkv_ref, dbsuv_ref, dsink_ref, dlg_ref, dlb_ref, dw_ref, dbt_ref):
                r[...] = jnp.zeros_like(r)

        def emit_kv(fin):
            dk = _rope_bwd(fin[:, 0:KV_W], tcp_ref[...], t1p_ref[...], t2p_ref[...])
            out = jnp.concatenate([dk, fin[:, KV_W:2 * KV_W]], axis=1)
            dkv_ref[...] = out.astype(_MXU)
            dbkv_ref[...] += _colsum(out)

        @pl.when(i < nb)
        def _():
            allowed = _band_mask(i == 0)
            kb = jnp.concatenate([kp_ref[...], kc_ref[...]], axis=0)
            vb = jnp.concatenate([vp_ref[...], vc_ref[...]], axis=0)
            qv = q_ref[...]
            dmc = dmc_ref[...]
            dqs, dks, dvs, dsinks = [], [], [], []
            for g in range(N_KV):
                kh = kb[:, g * HEAD_DIM:(g + 1) * HEAD_DIM]
                vh = vb[:, g * HEAD_DIM:(g + 1) * HEAD_DIM]
                dk_g = jnp.zeros((2 * BLK, HEAD_DIM), F32)
                dv_g = jnp.zeros((2 * BLK, HEAD_DIM), F32)
                for hh in range(Q_PER_KV):
                    h = g * Q_PER_KV + hh
                    qh = qv[:, h * HEAD_DIM:(h + 1) * HEAD_DIM]
                    probs, psink = _attn_probs(qh, kh, sink_ref[h], allowed)
                    pb = probs.astype(_MXU)
                    dob = dmc[:, h * HEAD_DIM:(h + 1) * HEAD_DIM].astype(_MXU)
                    dv_g = dv_g + _dot_tn(pb, dob)
                    dp = _dot_nt(dob, vh)
                    rd = jnp.sum(probs * dp, axis=-1, keepdims=True)
                    dsb = (probs * (dp - rd)).astype(_MXU)
                    dsinks.append(-jnp.sum(psink * rd, axis=0, keepdims=True))
                    dqs.append(_dot(dsb, kh))
                    dk_g = dk_g + _dot_tn(dsb, qh)
                dks.append(dk_g)
                dvs.append(dv_g)
            dq = _rope_bwd(jnp.concatenate(dqs, axis=1) * (HEAD_DIM ** -0.5), tc_ref[...], t1_ref[...], t2_ref[...])
            dq_ref[...] = dq.astype(_MXU)
            dbq_ref[...] += _colsum(dq)
            dsink_ref[...] += _lane_put(dsinks, 128)
            contrib = jnp.concatenate(dks + dvs, axis=1)

            @pl.when(i > 0)
            def _():
                emit_kv(carry_ref[...] + contrib[0:BLK, :])

            carry_ref[...] = contrib[BLK:2 * BLK, :]

            su = su_ref[...]
            sv = sv_ref[...]
            lg = lg_ref[...]
            u, vhat, rstd, vvb, wcs, mixed = _sgu_fwd(su, sv, lg, lb_ref[...], w_ref, bt_ref)
            dsgu = dmc[:, ATTN_W:D_MODEL]
            dsu = dsgu * mixed * _gelu_grad(su)
            dmixed = dsgu * u
            tri_t = lax.broadcasted_iota(jnp.int32, (BLK, BLK), 0)
            tri_s = lax.broadcasted_iota(jnp.int32, (BLK, BLK), 1)
            dvv, dbs = [], []
            for h in range(N_GRP):
                dm = dmixed[:, h * GRP_DIM:(h + 1) * GRP_DIM]
                dmb = dm.astype(_MXU)
                dbs.append(jnp.sum(dm, axis=1, keepdims=True))
                dw_ref[h] += jnp.where(tri_s <= tri_t, _dot_nt(dmb, vvb[:, h * GRP_DIM:(h + 1) * GRP_DIM]), 0.0)
                dvv.append(_dot_tn(wcs[h], dmb))
            dvv = jnp.concatenate(dvv, axis=1)
            dbt_ref[...] += _lane_put(dbs, 128)
            dlg_ref[...] += _colsum(dvv * vhat)
            dlb_ref[...] += _colsum(dvv)
            dsv = _ln_bwd(dvv, vhat, rstd, lg) * _gelu_grad(sv)
            dsuv = jnp.concatenate([dsu, dsv], axis=1)
            dsuv_ref[...] = dsuv.astype(_MXU)
            dbsuv_ref[...] += _colsum(dsuv)

        @pl.when(i == nb)
        def _():
            emit_kv(carry_ref[...])
            exchange.finish_to(pl_ref, x_flush)

    last = nb - 1
    cur = lambda w: pl.BlockSpec((BLK, w), lambda i: (jnp.minimum(i, last), 0))
    prev = lambda w: pl.BlockSpec((BLK, w), lambda i: (jnp.clip(i - 1, 0, last), 0))
    sd = jax.ShapeDtypeStruct
    return pl.pallas_call(
        body, name="mixer_bwd", grid=(nb + 1,),
        in_specs=[cur(ATTN_W), cur(KV_W), prev(KV_W), cur(KV_W), prev(KV_W), cur(SGU_W), cur(SGU_W), cur(D_MODEL),
                  cur(128), cur(128), cur(128), prev(128), prev(128), prev(128),
                  _smem(), _const2((1, SGU_W)), _const2((1, SGU_W)), _const2((N_GRP, BLK, BLK)), _const2((BLK, N_GRP)),
                  _vmem()],
        out_specs=[cur(ATTN_W), prev(2 * KV_W), cur(2 * SGU_W),
                   _const2((1, ATTN_W)), _const2((1, 2 * KV_W)), _const2((1, 2 * SGU_W)),
                   _const2((1, 128)), _const2((1, SGU_W)), _const2((1, SGU_W)),
                   _const2((N_GRP, BLK, BLK)), _const2((BLK, 128)), _hbm()],
        out_shape=[sd((s_len, ATTN_W), _MXU), sd((s_len, 2 * KV_W), _MXU), sd((s_len, 2 * SGU_W), _MXU),
                   sd((1, ATTN_W), F32), sd((1, 2 * KV_W), F32), sd((1, 2 * SGU_W), F32),
                   sd((1, 128), F32), sd((1, SGU_W), F32), sd((1, SGU_W), F32),
                   sd((N_GRP, BLK, BLK), F32), sd((BLK, 128), F32), _ChipExchange.land_shape(prev_wire)],
        scratch_shapes=[pltpu.VMEM((BLK, 2 * KV_W), F32)] + _ChipExchange.scratch(prev_wire),
        compiler_params=_params(32),
    )(q, k, k, v, v, su, sv, dmc, tc, t1, t2, tc, t1, t2, sinks, sg, sb, sgu_w, sgu_bt, prev_wire)


def _inproj_bwd(dq, dkv, dsuv, dr1, x, g0, b0, w_in):
    s_len = x.shape[0]
    tm = _tile(s_len, 512)
    cuts = ((0, ATTN_W), (ATTN_W, ATTN_W + 2 * KV_W), (ATTN_W + 2 * KV_W, IN_W))

    def body(dq_ref, dkv_ref, dsuv_ref, dr1_ref, x_ref, g_ref, b_ref, w_ref,
             dx_ref, dw_ref, wire_ref, dg_ref, db_ref, land_ref, send_sem, recv_sem):
        i = pl.program_id(0)

        @pl.when(i == 0)
        def _():
            dw_ref[...] = jnp.zeros_like(dw_ref)
            dg_ref[...] = jnp.zeros_like(dg_ref)
            db_ref[...] = jnp.zeros_like(db_ref)

        h0, xhat, rstd = _ln(x_ref[...], g_ref[...], b_ref[...])
        h0b = h0.astype(_MXU)
        dh0 = ALPHA * dr1_ref[...]
        for (lo, hi), d_ref in zip(cuts, (dq_ref, dkv_ref, dsuv_ref)):
            d = d_ref[...]
            dh0 = dh0 + _dot(d, w_ref[lo:hi, :])
            dw_ref[lo:hi, :] += _dot_tn(d, h0b)
        dg_ref[...] += _colsum(dh0 * xhat)
        db_ref[...] += _colsum(dh0)
        dx_ref[...] = _ln_bwd(dh0, xhat, rstd, g_ref[...])

        @pl.when(i == pl.num_programs(0) - 1)
        def _():
            _pair_reduce(dw_ref, land_ref, wire_ref, send_sem, recv_sem)

    vec = jax.ShapeDtypeStruct((1, D_MODEL), F32)
    c = _const2((1, D_MODEL))
    return pl.pallas_call(
        body, name="inproj_bwd", grid=(s_len // tm,),
        in_specs=[_rows(tm, ATTN_W), _rows(tm, 2 * KV_W), _rows(tm, 2 * SGU_W), _rows(tm, D_MODEL), _rows(tm, D_MODEL),
                  c, c, _vmem()],
        out_specs=[_rows(tm, D_MODEL), _vmem(), _vmem(), c, c],
        out_shape=[jax.ShapeDtypeStruct((s_len, D_MODEL), F32), jax.ShapeDtypeStruct((IN_W, D_MODEL), F32),
                   jax.ShapeDtypeStruct((N_CHIP, IN_SH // 2, D_MODEL), _WIRE), vec, vec],
        scratch_shapes=_pair_scratch((N_CHIP, IN_SH // 2, D_MODEL)),
        compiler_params=_params(56),
    )(dq, dkv, dsuv, dr1, x, g0, b0, w_in)


def _place():
    x, y, c = (lax.axis_index(a) for a in MESH_AXES)
    chips = [(1 - x, y), (x, 1 - y), (1 - x, 1 - y)]
    return x, y, c, chips


class _Gather:
    def __init__(self, ins, outs, send_sems, recv_sems):
        self.ins, self.outs, self.send_sems, self.recv_sems = ins, outs, send_sems, recv_sems
        self.n = len(ins)
        self.halves = [r.shape[0] // 2 for r in ins]

    def _copy(self, k, t, slot, half, to):
        rows = pl.ds(pl.multiple_of(half * self.halves[t], 16), self.halves[t])
        piece = self.outs[t].at[slot, rows, :]
        return pltpu.make_async_remote_copy(src_ref=piece, dst_ref=piece, send_sem=self.send_sems.at[k],
                                            recv_sem=self.recv_sems.at[k], device_id=to, device_id_type=MESH)

    def _chip_copy(self, t, d, slot):
        x, y, c, chips = _place()
        return self._copy(3 * t + d, t, slot, c, (chips[d][0], chips[d][1], c))

    def _pass_copy(self, t, d, half):
        x, y, c, chips = _place()
        return self._copy(3 * self.n + 3 * t + d, t, 2 * chips[d][0] + chips[d][1], half, (x, y, 1 - c))

    def start(self):
        x, y, c, chips = _place()
        me = 2 * x + y
        for t in range(self.n):
            self.outs[t][me] = self.ins[t][...].astype(_WIRE)
        for t in range(self.n):
            for d in range(3):
                self._chip_copy(t, d, me).start()

    def finish(self):
        x, y, c, chips = _place()
        me = 2 * x + y
        for t in range(self.n):
            for d in range(3):
                self._chip_copy(t, d, 2 * chips[d][0] + chips[d][1]).wait_recv()
                self._pass_copy(t, d, c).start()
        for t in range(self.n):
            for d in range(3):
                self._pass_copy(t, d, 1 - c).wait_recv()
        for t in range(self.n):
            for d in range(3):
                self._chip_copy(t, d, me).wait_send()
                self._pass_copy(t, d, c).wait_send()

    def flush(self, hbm_outs, flush_sems):
        _flush(self.outs, hbm_outs, flush_sems)

    @staticmethod
    def out_shapes(shards):
        return [jax.ShapeDtypeStruct((N_CHIP,) + s.shape, _WIRE) for s in shards]

    @staticmethod
    def sems(n):
        return [pltpu.SemaphoreType.DMA((6 * n,)), pltpu.SemaphoreType.DMA((6 * n,))]

    @staticmethod
    def scratch(shards):
        n = len(shards)
        return ([pltpu.VMEM((N_CHIP,) + s.shape, _WIRE) for s in shards] + _Gather.sems(n)
                + [pltpu.SemaphoreType.DMA((n,))])


def _flush(bufs, hbm_outs, sems):
    copies = [pltpu.make_async_copy(b, o, sems.at[k]) for k, (b, o) in enumerate(zip(bufs, hbm_outs))]
    for cp in copies:
        cp.start()
    for cp in copies:
        cp.wait()


def _gather_weights(shards):
    n = len(shards)

    def body(*refs):
        gather = _Gather(refs[:n], refs[n:2 * n], refs[2 * n], refs[2 * n + 1])
        gather.start()
        gather.finish()

    return pl.pallas_call(
        body, name="gather_weights",
        in_specs=[_vmem()] * n, out_specs=[_vmem()] * n,
        out_shape=_Gather.out_shapes(shards), scratch_shapes=_Gather.sems(n),
        compiler_params=pltpu.CompilerParams(vmem_limit_bytes=32 * MIB),
    )(*shards)


class _ChipExchange:
    def __init__(self, wire_ref, land_ref, send_sems, recv_sems):
        self.wire, self.land, self.send_sems, self.recv_sems = wire_ref, land_ref, send_sems, recv_sems

    def _copy(self, d):
        x, y, c, chips = _place()
        return pltpu.make_async_remote_copy(
            src_ref=self.wire.at[2 * chips[d][0] + chips[d][1]], dst_ref=self.land.at[d],
            send_sem=self.send_sems.at[d], recv_sem=self.recv_sems.at[d],
            device_id=(chips[d][0], chips[d][1], c), device_id_type=MESH)

    def start(self):
        for d in range(3):
            self._copy(d).start()

    def wait_recv(self):
        for d in range(3):
            self._copy(d).wait_recv()

    def wait_send(self):
        for d in range(3):
            self._copy(d).wait_send()

    def finish_to(self, hbm_out, flush_sem):
        self.wait_recv()
        _flush([self.land], [hbm_out], flush_sem)
        self.wait_send()

    @staticmethod
    def land_shape(wire):
        return jax.ShapeDtypeStruct((3,) + wire.shape[1:], wire.dtype)

    @staticmethod
    def sems():
        return [pltpu.SemaphoreType.DMA((3,)), pltpu.SemaphoreType.DMA((3,))]

    @staticmethod
    def scratch(wire):
        return ([pltpu.VMEM((3,) + wire.shape[1:], wire.dtype)] + _ChipExchange.sems() + [pltpu.SemaphoreType.DMA((1,))])


def _pair_scratch(half_shape):
    return [pltpu.VMEM(half_shape, F32), pltpu.SemaphoreType.DMA((N_CHIP,)), pltpu.SemaphoreType.DMA((N_CHIP,))]


def _pair_reduce(acc_ref, land_ref, wire_ref, send_sems, recv_sems):
    rh = land_ref.shape[1]
    x, y, c, _ = _place()
    copies = []
    for j in range(N_CHIP):
        give = acc_ref.at[pl.ds(pl.multiple_of(j * 2 * rh + (1 - c) * rh, 8), rh), :]
        cp = pltpu.make_async_remote_copy(src_ref=give, dst_ref=land_ref.at[j], send_sem=send_sems.at[j],
                                          recv_sem=recv_sems.at[j], device_id=(x, y, 1 - c), device_id_type=MESH)
        cp.start()
        copies.append(cp)
    for cp in copies:
        cp.wait()

    def chunk(r, carry):
        theirs = pl.ds(pl.multiple_of(r * ROW_CHUNK, ROW_CHUNK), ROW_CHUNK)
        for j in range(N_CHIP):
            mine = pl.ds(pl.multiple_of(j * 2 * rh + c * rh + r * ROW_CHUNK, 8), ROW_CHUNK)
            s = acc_ref[mine, :] + land_ref[j, theirs, :]
            acc_ref[mine, :] = s
            wire_ref[j, theirs, :] = s.astype(_WIRE)
        return carry

    lax.fori_loop(0, rh // ROW_CHUNK, chunk, 0)


def _grad_finish(last_wire, lands, accs):
    n = len(accs)
    halves = [last_wire.shape[1]] + [w.shape[1] for w in lands]
    widths = [a.shape[1] for a in accs]

    def body(*refs):
        wire0, land, acc, g = refs[0], (None,) + refs[1:n], refs[n:2 * n], refs[2 * n:3 * n]
        land0, own = refs[3 * n], refs[3 * n + 1:4 * n + 1]
        x_send, x_recv, pair_send, pair_recv, local_sems = refs[4 * n + 1:4 * n + 6]
        land = (land0,) + land[1:]
        x, y, c, chips = _place()
        me = 2 * x + y
        exchange = _ChipExchange(wire0, land0, x_send, x_recv)

        def half_rows(t, half):
            return pl.ds(pl.multiple_of(half * halves[t], 8), halves[t])

        def own_copy(t):
            rows = pl.ds(pl.multiple_of((2 * me + c) * halves[t], 8), halves[t])
            return pltpu.make_async_copy(acc[t].at[rows, :], own[t], local_sems.at[t])

        def pair_copy(t, half):
            rows = g[t].at[half_rows(t, half), :]
            return pltpu.make_async_remote_copy(src_ref=rows, dst_ref=rows, send_sem=pair_send.at[t],
                                                recv_sem=pair_recv.at[t], device_id=(x, y, 1 - c), device_id_type=MESH)

        exchange.start()
        for t in range(n):
            own_copy(t).start()
        for t in list(range(1, n)) + [0]:
            own_copy(t).wait()
            if t == 0:
                exchange.wait_recv()

            def chunk(r, carry, t=t):
                src = pl.ds(pl.multiple_of(r * ROW_CHUNK, ROW_CHUNK), ROW_CHUNK)
                dst = pl.ds(pl.multiple_of(c * halves[t] + r * ROW_CHUNK, 8), ROW_CHUNK)
                s = own[t][src, :]
                for d in range(3):
                    s = s + land[t][d, src, :].astype(F32)
                g[t][dst, :] = s
                return carry

            lax.fori_loop(0, halves[t] // ROW_CHUNK, chunk, 0)
            pair_copy(t, c).start()
        for t in range(n):
            pair_copy(t, 1 - c).wait_recv()
        for t in range(n):
            pair_copy(t, c).wait_send()
        exchange.wait_send()

    return pl.pallas_call(
        body, name="grad_finish",
        in_specs=[_vmem()] * n + [_hbm()] * n, out_specs=[_vmem()] * n,
        out_shape=[jax.ShapeDtypeStruct((2 * h, w), F32) for h, w in zip(halves, widths)],
        scratch_shapes=[pltpu.VMEM((3,) + last_wire.shape[1:], last_wire.dtype)]
        + [pltpu.VMEM((h, w), F32) for h, w in zip(halves, widths)]
        + _ChipExchange.sems()
        + [pltpu.SemaphoreType.DMA((n,)), pltpu.SemaphoreType.DMA((n,)), pltpu.SemaphoreType.DMA((n,))],
        compiler_params=pltpu.CompilerParams(vmem_limit_bytes=56 * MIB),
    )(last_wire, *lands, *accs)


def _allreduce_small(packed):
    rows = packed.shape[0]

    def body(p_ref, out_ref, buf_ref, send_sems, recv_sems):
        x, y, c, _ = _place()
        me = 4 * x + 2 * y + c
        buf_ref[me] = p_ref[...]
        copies = []
        for r in range(1, 8):
            rx, ry, rc_ = (r >> 2) & 1, (r >> 1) & 1, r & 1
            peer = (x ^ rx, y ^ ry, c ^ rc_)
            cp = pltpu.make_async_remote_copy(src_ref=buf_ref.at[me], dst_ref=buf_ref.at[me],
                                              send_sem=send_sems.at[r - 1], recv_sem=recv_sems.at[r - 1],
                                              device_id=peer, device_id_type=MESH)
            cp.start()
            copies.append(cp)
        for cp in copies:
            cp.wait()
        acc = buf_ref[0]
        for d in range(1, 8):
            acc = acc + buf_ref[d]
        out_ref[...] = acc

    return pl.pallas_call(
        body, name="allreduce_small",
        in_specs=[_vmem()], out_specs=_vmem(),
        out_shape=jax.ShapeDtypeStruct((rows, 128), F32),
        scratch_shapes=[pltpu.VMEM((8, rows, 128), F32), pltpu.SemaphoreType.DMA((7,)), pltpu.SemaphoreType.DMA((7,))],
        compiler_params=pltpu.CompilerParams(vmem_limit_bytes=32 * MIB),
    )(packed)


def _elementwise(name, fn, ins, out_dtypes, tile_rows=256):
    shape = ins[0].shape
    lead = shape[:-2]
    rows, cols = shape[-2:]
    tr = _tile(rows, tile_rows)
    n_lead = math.prod(lead)
    nr = rows // tr
    flat = [a.reshape((n_lead, rows, cols)) for a in ins]

    def body(*refs):
        outs = fn(*[r[0] for r in refs[:len(ins)]])
        for o_ref, o in zip(refs[len(ins):], outs):
            o_ref[0] = o.astype(o_ref.dtype)

    spec = pl.BlockSpec((1, tr, cols), lambda i: (i // nr, i % nr, 0))
    res = pl.pallas_call(
        body, name=name, grid=(n_lead * nr,),
        in_specs=[spec] * len(ins), out_specs=[spec] * len(out_dtypes),
        out_shape=[jax.ShapeDtypeStruct((n_lead, rows, cols), dt) for dt in out_dtypes],
        compiler_params=_params(32),
    )(*flat)
    return [r.reshape(shape) for r in res]


def _adamw_math(w, g, m, v):
    m = ADAM_B1 * m + (1.0 - ADAM_B1) * g
    v = ADAM_B2 * v + (1.0 - ADAM_B2) * (g * g)
    m_hat = m / (1.0 - ADAM_B1 ** ADAM_STEP)
    v_hat = v / (1.0 - ADAM_B2 ** ADAM_STEP)
    delta = -ADAM_LR * (m_hat / (jnp.sqrt(v_hat) + ADAM_EPS) + ADAM_WD * w)
    return delta, m, v


def _adamw(name, w, g, m, v, tile_rows=256):
    return _elementwise(name, _adamw_math, [w, g, m, v], [F32, F32, F32], tile_rows)


_SMALL = ("ln_in_g", "ln_in_b", "b_in", "attn_sinks", "sgu_ln_g", "sgu_ln_b", "sgu_w", "sgu_b", "b_out",
          "ln_mix_g", "ln_mix_b", "ln_ffn_g", "ln_ffn_b")


def _pack(arrs):
    parts = []
    for a in arrs:
        flat = a.reshape(-1)
        pad = (-flat.shape[0]) % 1024
        parts.append(jnp.pad(flat, (0, pad)) if pad else flat)
    return jnp.concatenate(parts).reshape(-1, 128)


def _unpack(packed, like):
    flat = packed.reshape(-1)
    out, off = [], 0
    for a in like:
        n = math.prod(a.shape)
        out.append(flat[off:off + n].reshape(a.shape))
        off += n + ((-n) % 1024)
    return out


def kernel(x, positions, ln_in_g, ln_in_b, w_in, b_in, attn_sinks, sgu_ln_g, sgu_ln_b, sgu_w, sgu_b, w_out, b_out, ln_mix_g, ln_mix_b, w_gate, w_up, w_down, ln_ffn_g, ln_ffn_b, loss_target, m_ln_in_g, m_ln_in_b, m_w_in, m_b_in, m_attn_sinks, m_sgu_ln_g, m_sgu_ln_b, m_sgu_w, m_sgu_b, m_w_out, m_b_out, m_ln_mix_g, m_ln_mix_b, m_w_gate, m_w_up, m_w_down, m_ln_ffn_g, m_ln_ffn_b, v_ln_in_g, v_ln_in_b, v_w_in, v_b_in, v_attn_sinks, v_sgu_ln_g, v_sgu_ln_b, v_sgu_w, v_sgu_b, v_w_out, v_b_out, v_ln_mix_g, v_ln_mix_b, v_w_gate, v_w_up, v_w_down, v_ln_ffn_g, v_ln_ffn_b):
    weights = dict(ln_in_g=ln_in_g, ln_in_b=ln_in_b, w_in=w_in, b_in=b_in, attn_sinks=attn_sinks, sgu_ln_g=sgu_ln_g,
                   sgu_ln_b=sgu_ln_b, sgu_w=sgu_w, sgu_b=sgu_b, w_out=w_out, b_out=b_out, ln_mix_g=ln_mix_g,
                   ln_mix_b=ln_mix_b, w_gate=w_gate, w_up=w_up, w_down=w_down, ln_ffn_g=ln_ffn_g, ln_ffn_b=ln_ffn_b)
    mom_m = dict(ln_in_g=m_ln_in_g, ln_in_b=m_ln_in_b, w_in=m_w_in, b_in=m_b_in, attn_sinks=m_attn_sinks,
                 sgu_ln_g=m_sgu_ln_g, sgu_ln_b=m_sgu_ln_b, sgu_w=m_sgu_w, sgu_b=m_sgu_b, w_out=m_w_out, b_out=m_b_out,
                 ln_mix_g=m_ln_mix_g, ln_mix_b=m_ln_mix_b, w_gate=m_w_gate, w_up=m_w_up, w_down=m_w_down,
                 ln_ffn_g=m_ln_ffn_g, ln_ffn_b=m_ln_ffn_b)
    mom_v = dict(ln_in_g=v_ln_in_g, ln_in_b=v_ln_in_b, w_in=v_w_in, b_in=v_b_in, attn_sinks=v_attn_sinks,
                 sgu_ln_g=v_sgu_ln_g, sgu_ln_b=v_sgu_ln_b, sgu_w=v_sgu_w, sgu_b=v_sgu_b, w_out=v_w_out, b_out=v_b_out,
                 ln_mix_g=v_ln_mix_g, ln_mix_b=v_ln_mix_b, w_gate=v_w_gate, w_up=v_w_up, w_down=v_w_down,
                 ln_ffn_g=v_ln_ffn_g, ln_ffn_b=v_ln_ffn_b)
    order = list(weights)
    big = ("w_in", "w_out", "w_gate", "w_up", "w_down")

    s_len = x.shape[1]
    xs = x.reshape(s_len, D_MODEL)
    tgt = loss_target.reshape(s_len, D_MODEL)
    pos_col = positions.reshape(s_len, 1)
    g0, b0 = ln_in_g.reshape(1, D_MODEL), ln_in_b.reshape(1, D_MODEL)
    sinks = attn_sinks.reshape(N_Q)
    sgu_w3 = sgu_w.reshape(N_GRP, BLK, BLK)
    sgu_bt = sgu_b.reshape(N_GRP, BLK).T

    col_sharded = ("w_in", "w_gate", "w_up")

    def rowmajor(name, a):
        return jnp.swapaxes(a[0], 0, 1) if name in col_sharded else a[0]

    def as_given(name, a):
        return (jnp.swapaxes(a, 0, 1) if name in col_sharded else a)[None]

    shards = [rowmajor(n, weights[n]) for n in big]
    (gw_in,) = _gather_weights(shards[0:1])
    w_in_full = gw_in.reshape(IN_W, D_MODEL)

    q, k, v, su, sv, tc, t1, t2, gw_out, gw_gate = _ln_inproj(xs, pos_col, g0, b0, w_in_full, b_in, shards[1:3])
    mc, gw_up, gw_down = _mixer_fwd(q, k, v, su, sv, sinks, sgu_ln_g, sgu_ln_b, sgu_w3, sgu_bt, shards[3:5])
    w_out_full = gw_out.reshape(D_MODEL, D_MODEL)
    r1 = _outproj(mc, w_out_full, b_out, xs, g0, b0)
    gact, uact = _ffn_up(r1, ln_mix_g, ln_mix_b, gw_gate, gw_up)
    dr2, loss_cols, d_ln_ffn_g, d_ln_ffn_b = _ffn_down_loss(gact, uact, gw_down, r1, ln_mix_g, ln_mix_b,
                                                            ln_ffn_g, ln_ffn_b, tgt)

    dg, du, acc_down, wire_down = _ffn_bwd_a(dr2, gact, uact, gw_down)
    dh1a, acc_gate, wire_gate, land_down = _ffn_bwd_g(dr2, dg, r1, ln_mix_g, ln_mix_b, gw_gate, wire_down)
    dr1, acc_up, wire_up, d_ln_mix_g, d_ln_mix_b, land_gate = _ffn_bwd_u(dh1a, du, r1, ln_mix_g, ln_mix_b, gw_up,
                                                                         wire_gate)
    dmc, acc_out, wire_out, d_b_out, land_up = _outproj_bwd(dr1, mc, w_out_full, wire_up)
    (dq, dkv, dsuv, dbq, dbkv, dbsuv, d_sink, d_sgu_ln_g, d_sgu_ln_b, d_sgu_w, d_sgu_bt, land_out) = _mixer_bwd(
        q, k, v, su, sv, dmc, tc, t1, t2, sinks, sgu_ln_g, sgu_ln_b, sgu_w3, sgu_bt, wire_out)
    grad_x, acc_in, wire_in, d_ln_in_g, d_ln_in_b = _inproj_bwd(dq, dkv, dsuv, dr1, xs, g0, b0, w_in_full)

    reduced = _grad_finish(wire_in, [land_out, land_gate, land_up, land_down],
                           [acc_in, acc_out, acc_gate, acc_up, acc_down])
    small_local = dict(
        ln_in_g=d_ln_in_g.reshape(ln_in_g.shape), ln_in_b=d_ln_in_b.reshape(ln_in_b.shape),
        b_in=jnp.concatenate([dbq, dbkv, dbsuv], axis=1), attn_sinks=d_sink[:, :N_Q],
        sgu_ln_g=d_sgu_ln_g, sgu_ln_b=d_sgu_ln_b, sgu_w=d_sgu_w.reshape(sgu_w.shape),
        sgu_b=d_sgu_bt[:, :N_GRP].T.reshape(sgu_b.shape), b_out=d_b_out,
        ln_mix_g=d_ln_mix_g, ln_mix_b=d_ln_mix_b, ln_ffn_g=d_ln_ffn_g, ln_ffn_b=d_ln_ffn_b)
    small_sum = _allreduce_small(_pack([small_local[n] for n in _SMALL] + [loss_cols]))
    like = [weights[n] for n in _SMALL]
    *small_grads, loss_sum = _unpack(small_sum, like + [loss_cols])
    loss = jnp.sum(loss_sum) * (0.5 / D_MODEL)
    grads = dict(zip(_SMALL, small_grads))

    delta, new_m, new_v = {}, {}, {}
    for t, name in enumerate(big):
        d_, m_, v_ = _adamw("adamw_" + name, shards[t], reduced[t], rowmajor(name, mom_m[name]),
                            rowmajor(name, mom_v[name]))
        grads[name] = as_given(name, reduced[t])
        delta[name], new_m[name], new_v[name] = as_given(name, d_), as_given(name, m_), as_given(name, v_)
    spare = jnp.zeros_like(loss_cols)
    packs = [_pack([src[n] for n in _SMALL] + [spare]) for src in (weights, mom_m, mom_v)]
    d_, m_, v_ = _adamw("adamw_small", packs[0], small_sum, packs[1], packs[2], tile_rows=packs[0].shape[0])
    for n, a, b, c_ in zip(_SMALL, _unpack(d_, like), _unpack(m_, like), _unpack(v_, like)):
        delta[n], new_m[n], new_v[n] = a, b, c_

    return (loss, grad_x.reshape(x.shape), *[grads[n] for n in order], *[delta[n] for n in order],
            *[new_m[n] for n in order], *[new_v[n] for n in order])
```

```python
import functools
import math

import jax
import jax.numpy as jnp
from jax import lax
from jax.experimental import pallas as pl
from jax.experimental.pallas import tpu as pltpu

F32 = jnp.float32
_MXU = jnp.bfloat16
_WIRE = jnp.bfloat16
_ACT = jnp.bfloat16

D_MODEL = 1024
ATTN_W = 512
SGU_W = 512
HEAD_DIM = 64
N_Q = 8
N_KV = 2
Q_PER_KV = 4
KV_W = 128
BLK = 128
ROT_DIM = 16
ROPE_THETA = 500000.0
N_GRP = 4
GRP_DIM = 128
D_FF = 2816
IN_W = 1792
LN_EPS = 1e-5
ALPHA = 2.0 ** 0.25
N_CHIP = 4
FF_SH = D_FF // N_CHIP
IN_SH = IN_W // N_CHIP
OUT_SH = D_MODEL // N_CHIP
ROW_CHUNK = 32

ADAM_LR = 0.001
ADAM_B1 = 0.9
ADAM_B2 = 0.999
ADAM_EPS = 1e-08
ADAM_WD = 0.01
ADAM_STEP = 10

SQRT_HALF = 0.7071067811865476
INV_SQRT_2PI = 0.3989422804014327
MESH_AXES = ("x", "y", "c")
MESH = pl.DeviceIdType.MESH
MIB = 2 ** 20


def _vmem():
    return pl.BlockSpec(memory_space=pltpu.VMEM)


def _smem():
    return pl.BlockSpec(memory_space=pltpu.SMEM)


def _hbm():
    return pl.BlockSpec(memory_space=pl.ANY)


def _params(vmem_mib=48):
    return pltpu.CompilerParams(dimension_semantics=("arbitrary",), vmem_limit_bytes=vmem_mib * MIB)


def _tile(n, cap):
    if n <= cap:
        return n
    for t in range(cap - cap % 16, 0, -16):
        if n % t == 0:
            return t
    raise ValueError((n, cap))


def _rows(tm, width):
    return pl.BlockSpec((tm, width), lambda i: (i, 0))


def _const2(shape):
    return pl.BlockSpec(shape, lambda i: (0,) * len(shape))


def _ln(x, g, b):
    mu = jnp.mean(x, axis=-1, keepdims=True)
    xc = x - mu
    var = jnp.mean(xc * xc, axis=-1, keepdims=True)
    rstd = lax.rsqrt(var + LN_EPS)
    xhat = xc * rstd
    return xhat * g + b, xhat, rstd


def _ln_bwd(dy, xhat, rstd, g):
    gdy = dy * g
    m1 = jnp.mean(gdy, axis=-1, keepdims=True)
    m2 = jnp.mean(gdy * xhat, axis=-1, keepdims=True)
    return rstd * (gdy - m1 - xhat * m2)


def _colsum(a):
    return jnp.sum(a, axis=0, keepdims=True)


def _gelu(x):
    return 0.5 * x * (1.0 + lax.erf(x * SQRT_HALF))


def _gelu_grad(x):
    return 0.5 * (1.0 + lax.erf(x * SQRT_HALF)) + x * jnp.exp(-0.5 * x * x) * INV_SQRT_2PI


def _dot(a, b):
    return jnp.dot(a, b, preferred_element_type=F32)


def _dot_nt(a, b):
    return lax.dot_general(a, b, (((1,), (1,)), ((), ())), preferred_element_type=F32)


def _dot_tn(a, b):
    return lax.dot_general(a, b, (((0,), (0,)), ((), ())), preferred_element_type=F32)


def _rope(t, tc, t1, t2):
    n = t.shape[1]
    rep = n // 128
    if rep > 1:
        tc, t1, t2 = (jnp.tile(a, (1, rep)) for a in (tc, t1, t2))
    return t * tc + pltpu.roll(t, n - 8, 1) * t1 + pltpu.roll(t, 8, 1) * t2


def _rope_bwd(d, tc, t1, t2):
    n = d.shape[1]
    rep = n // 128
    if rep > 1:
        tc, t1, t2 = (jnp.tile(a, (1, rep)) for a in (tc, t1, t2))
    return d * tc + pltpu.roll(d * t1, 8, 1) + pltpu.roll(d * t2, n - 8, 1)


def _band_mask(first_block):
    qi = lax.broadcasted_iota(jnp.int32, (BLK, 2 * BLK), 0)
    kj = lax.broadcasted_iota(jnp.int32, (BLK, 2 * BLK), 1)
    shut = jnp.where(first_block, 2 * BLK, 0)
    prev_ok = jnp.logical_and(kj < BLK, kj > qi + shut)
    cur_ok = jnp.logical_and(kj >= BLK, (kj - BLK) <= qi)
    return jnp.logical_or(prev_ok, cur_ok)


def _causal_w(w_ref, h):
    t = lax.broadcasted_iota(jnp.int32, (BLK, BLK), 0)
    s = lax.broadcasted_iota(jnp.int32, (BLK, BLK), 1)
    return jnp.where(s <= t, w_ref[h], 0.0)


def _lane_put(vals, width):
    rows = vals[0].shape[0]
    lane = lax.broadcasted_iota(jnp.int32, (rows, width), 1)
    out = jnp.zeros((rows, width), F32)
    for k, v in enumerate(vals):
        out = out + jnp.where(lane == k, v, 0.0)
    return out


def _rope_consts():
    lane = jnp.arange(128) % HEAD_DIM
    inv_freq = ROPE_THETA ** (-jnp.arange(0, ROT_DIM, 2, dtype=F32) / ROT_DIM)
    rot = lane < ROT_DIM
    freq = jnp.where(rot, inv_freq[lane % (ROT_DIM // 2)], 0.0)
    rows = [freq, rot.astype(F32), 1.0 - rot.astype(F32), (lane < ROT_DIM // 2).astype(F32),
            jnp.logical_and(lane >= ROT_DIM // 2, rot).astype(F32)]
    rows += [jnp.zeros((128,), F32)] * 3
    return jnp.stack(rows).astype(F32)


def _ln_inproj(x, pos_col, g0, b0, w_in, b_in, shards):
    s_len = x.shape[0]
    tm = _tile(s_len, 512)

    n = len(shards)

    def body(x_ref, pos_ref, g_ref, b_ref, w_ref, bi_ref, rc_ref, *rest):
        q_ref, k_ref, v_ref, su_ref, sv_ref, tc_ref, t1_ref, t2_ref = rest[n:n + 8]
        gathered = rest[n + 8:2 * n + 8]
        gather = _Gather(rest[:n], rest[2 * n + 8:3 * n + 8], rest[3 * n + 8], rest[3 * n + 9])
        flush_sems = rest[3 * n + 10]
        i = pl.program_id(0)

        @pl.when(i == 0)
        def _():
            gather.start()

        h0, _, _ = _ln(x_ref[...], g_ref[...], b_ref[...])
        proj = _dot_nt(h0.astype(_MXU), w_ref[...]) + bi_ref[...]
        ang = pos_ref[...].astype(F32) * rc_ref[0:1, :]
        cs = jnp.cos(ang)
        sn = jnp.sin(ang)
        tc = cs * rc_ref[1:2, :] + rc_ref[2:3, :]
        t1 = -sn * rc_ref[3:4, :]
        t2 = sn * rc_ref[4:5, :]
        tc_ref[...] = tc
        t1_ref[...] = t1
        t2_ref[...] = t2
        q = _rope(proj[:, 0:ATTN_W], tc, t1, t2) * (HEAD_DIM ** -0.5)
        q_ref[...] = q.astype(_MXU)
        k_ref[...] = _rope(proj[:, ATTN_W:ATTN_W + KV_W], tc, t1, t2).astype(_MXU)
        v_ref[...] = proj[:, ATTN_W + KV_W:ATTN_W + 2 * KV_W].astype(_MXU)
        su_ref[...] = proj[:, ATTN_W + 2 * KV_W:ATTN_W + 2 * KV_W + SGU_W]
        sv_ref[...] = proj[:, ATTN_W + 2 * KV_W + SGU_W:IN_W]

        @pl.when(i == pl.num_programs(0) - 1)
        def _():
            gather.finish()
            gather.flush(gathered, flush_sems)

    sd = jax.ShapeDtypeStruct
    return pl.pallas_call(
        body, name="ln_inproj", grid=(s_len // tm,),
        in_specs=[_rows(tm, D_MODEL), _rows(tm, 1), _const2((1, D_MODEL)), _const2((1, D_MODEL)), _vmem(),
                  _const2((1, IN_W)), _const2((8, 128))] + [_vmem()] * n,
        out_specs=[_rows(tm, ATTN_W), _rows(tm, KV_W), _rows(tm, KV_W), _rows(tm, SGU_W), _rows(tm, SGU_W),
                   _rows(tm, 128), _rows(tm, 128), _rows(tm, 128)] + [_hbm()] * n,
        out_shape=[sd((s_len, ATTN_W), _MXU), sd((s_len, KV_W), _MXU), sd((s_len, KV_W), _MXU),
                   sd((s_len, SGU_W), F32), sd((s_len, SGU_W), F32),
                   sd((s_len, 128), F32), sd((s_len, 128), F32), sd((s_len, 128), F32)] + _Gather.out_shapes(shards),
        scratch_shapes=_Gather.scratch(shards),
        compiler_params=_params(56),
    )(x, pos_col, g0, b0, w_in, b_in, _rope_consts(), *shards)


def _attn_probs(qh, kh, sink, allowed):
    s = jnp.where(allowed, _dot_nt(qh, kh), -1e30)
    m = jnp.maximum(jnp.max(s, axis=-1, keepdims=True), sink)
    p = jnp.exp(s - m)
    ps = jnp.exp(sink - m)
    inv = 1.0 / (jnp.sum(p, axis=-1, keepdims=True) + ps)
    return p * inv, ps * inv


def _sgu_fwd(su, sv, lg, lb, w_ref, bt_ref):
    u = _gelu(su)
    vv, vhat, rstd = _ln(_gelu(sv), lg, lb)
    vvb = vv.astype(_MXU)
    wcs, mixed = [], []
    for h in range(N_GRP):
        wc = _causal_w(w_ref, h).astype(_MXU)
        wcs.append(wc)
        mixed.append(_dot(wc, vvb[:, h * GRP_DIM:(h + 1) * GRP_DIM]) + bt_ref[:, h:h + 1])
    return u, vhat, rstd, vvb, wcs, jnp.concatenate(mixed, axis=1)


def _prev_map(i):
    return (jnp.maximum(i - 1, 0), 0)


def _mixer_fwd(q, k, v, su, sv, sinks, sg, sb, sgu_w, sgu_bt, shards):
    s_len = q.shape[0]
    nb = s_len // BLK
    n = len(shards)

    def body(q_ref, kc_ref, kp_ref, vc_ref, vp_ref, su_ref, sv_ref, sink_ref, lg_ref, lb_ref, w_ref, bt_ref, *rest):
        mc_ref = rest[n]
        gathered = rest[n + 1:2 * n + 1]
        gather = _Gather(rest[:n], rest[2 * n + 1:3 * n + 1], rest[3 * n + 1], rest[3 * n + 2])
        flush_sems = rest[3 * n + 3]
        i = pl.program_id(0)

        @pl.when(i == 0)
        def _():
            gather.start()

        @pl.when(i == nb - 1)
        def _():
            gather.finish()
            gather.flush(gathered, flush_sems)

        allowed = _band_mask(i == 0)
        kb = jnp.concatenate([kp_ref[...], kc_ref[...]], axis=0)
        vb = jnp.concatenate([vp_ref[...], vc_ref[...]], axis=0)
        qv = q_ref[...]
        outs = []
        for h in range(N_Q):
            g = h // Q_PER_KV
            kh = kb[:, g * HEAD_DIM:(g + 1) * HEAD_DIM]
            vh = vb[:, g * HEAD_DIM:(g + 1) * HEAD_DIM]
            probs, _ = _attn_probs(qv[:, h * HEAD_DIM:(h + 1) * HEAD_DIM], kh, sink_ref[h], allowed)
            outs.append(_dot(probs.astype(_MXU), vh))
        u, _, _, _, _, mixed = _sgu_fwd(su_ref[...], sv_ref[...], lg_ref[...], lb_ref[...], w_ref, bt_ref)
        mc_ref[...] = jnp.concatenate(outs + [u * mixed], axis=1).astype(_MXU)

    cur = lambda w: pl.BlockSpec((BLK, w), lambda i: (i, 0))
    prev = lambda w: pl.BlockSpec((BLK, w), _prev_map)
    return pl.pallas_call(
        body, name="mixer_fwd", grid=(nb,),
        in_specs=[cur(ATTN_W), cur(KV_W), prev(KV_W), cur(KV_W), prev(KV_W), cur(SGU_W), cur(SGU_W), _smem(),
                  _const2((1, SGU_W)), _const2((1, SGU_W)), _const2((N_GRP, BLK, BLK)), _const2((BLK, N_GRP))]
        + [_vmem()] * n,
        out_specs=[cur(D_MODEL)] + [_hbm()] * n,
        out_shape=[jax.ShapeDtypeStruct((s_len, D_MODEL), _MXU)] + _Gather.out_shapes(shards),
        scratch_shapes=_Gather.scratch(shards),
        compiler_params=_params(48),
    )(q, k, k, v, v, su, sv, sinks, sg, sb, sgu_w, sgu_bt, *shards)


def _outproj(mc, w_out, b_out, x, g0, b0, shards):
    s_len = x.shape[0]
    tm = _tile(s_len, 512)
    n = len(shards)

    def body(mc_ref, w_ref, bo_ref, x_ref, g_ref, b_ref, *rest):
        r1_ref = rest[n]
        gathered = rest[n + 1:2 * n + 1]
        gather = _Gather(rest[:n], rest[2 * n + 1:3 * n + 1], rest[3 * n + 1], rest[3 * n + 2])
        flush_sems = rest[3 * n + 3]
        i = pl.program_id(0)

        @pl.when(i == 0)
        def _():
            gather.start()

        h0, _, _ = _ln(x_ref[...], g_ref[...], b_ref[...])
        r1_ref[...] = ALPHA * h0 + (_dot(mc_ref[...], w_ref[...]) + bo_ref[...])

        @pl.when(i == pl.num_programs(0) - 1)
        def _():
            gather.finish()
            gather.flush(gathered, flush_sems)

    return pl.pallas_call(
        body, name="outproj", grid=(s_len // tm,),
        in_specs=[_rows(tm, D_MODEL), _vmem(), _const2((1, D_MODEL)), _rows(tm, D_MODEL),
                  _const2((1, D_MODEL)), _const2((1, D_MODEL))] + [_vmem()] * n,
        out_specs=[_rows(tm, D_MODEL)] + [_hbm()] * n,
        out_shape=[jax.ShapeDtypeStruct((s_len, D_MODEL), F32)] + _Gather.out_shapes(shards),
        scratch_shapes=_Gather.scratch(shards),
        compiler_params=_params(40),
    )(mc, w_out, b_out, x, g0, b0, *shards)


def _ffn_spec(tm):
    return pl.BlockSpec((N_CHIP, tm, FF_SH), lambda i: (0, i, 0))


def _ffn_up(r1, g1, b1, wg, wu, shards):
    s_len = r1.shape[0]
    tm = _tile(s_len, 512)
    n = len(shards)

    def body(r1_ref, g_ref, b_ref, wg_ref, wu_ref, *rest):
        go_ref, uo_ref = rest[n:n + 2]
        gathered = rest[n + 2:2 * n + 2]
        gather = _Gather(rest[:n], rest[2 * n + 2:3 * n + 2], rest[3 * n + 2], rest[3 * n + 3])
        flush_sems = rest[3 * n + 4]
        i = pl.program_id(0)

        @pl.when(i == 0)
        def _():
            gather.start()

        h1, _, _ = _ln(r1_ref[...], g_ref[...], b_ref[...])
        h1b = h1.astype(_MXU)
        for j in range(N_CHIP):
            go_ref[j] = _dot_nt(h1b, wg_ref[j]).astype(_ACT)
            uo_ref[j] = _dot_nt(h1b, wu_ref[j]).astype(_ACT)

        @pl.when(i == pl.num_programs(0) - 1)
        def _():
            gather.finish()
            gather.flush(gathered, flush_sems)

    sd = jax.ShapeDtypeStruct((N_CHIP, s_len, FF_SH), _ACT)
    return pl.pallas_call(
        body, name="ffn_up", grid=(s_len // tm,),
        in_specs=[_rows(tm, D_MODEL), _const2((1, D_MODEL)), _const2((1, D_MODEL)), _vmem(), _vmem()] + [_vmem()] * n,
        out_specs=[_ffn_spec(tm), _ffn_spec(tm)] + [_hbm()] * n,
        out_shape=[sd, sd] + _Gather.out_shapes(shards),
        scratch_shapes=_Gather.scratch(shards),
        compiler_params=_params(56),
    )(r1, g1, b1, wg, wu, *shards)


def _silu_parts(g):
    sg = 1.0 / (1.0 + jnp.exp(-g))
    return g * sg, sg


def _ffn_down_loss(gact, uact, wd, r1, g1, b1, g2, b2, target):
    s_len = r1.shape[0]
    tm = _tile(s_len, 512)

    def body(g_ref, u_ref, wd_ref, r1_ref, g1_ref, b1_ref, g2_ref, b2_ref, t_ref,
             dr2_ref, loss_ref, dg2_ref, db2_ref):
        i = pl.program_id(0)
        f = jnp.zeros((tm, D_MODEL), F32)
        for j in range(N_CHIP):
            silu, _ = _silu_parts(g_ref[j].astype(F32))
            f = f + _dot((silu * u_ref[j].astype(F32)).astype(_MXU), wd_ref[j])
        h1, _, _ = _ln(r1_ref[...], g1_ref[...], b1_ref[...])
        h2, r2hat, rstd2 = _ln(ALPHA * h1 + f, g2_ref[...], b2_ref[...])
        diff = h2 - t_ref[...]
        dh2 = diff * (1.0 / D_MODEL)

        @pl.when(i == 0)
        def _():
            loss_ref[...] = jnp.zeros_like(loss_ref)
            dg2_ref[...] = jnp.zeros_like(dg2_ref)
            db2_ref[...] = jnp.zeros_like(db2_ref)

        loss_ref[...] += _colsum(diff * diff)
        dg2_ref[...] += _colsum(dh2 * r2hat)
        db2_ref[...] += _colsum(dh2)
        dr2_ref[...] = _ln_bwd(dh2, r2hat, rstd2, g2_ref[...])

    vec = jax.ShapeDtypeStruct((1, D_MODEL), F32)
    c = _const2((1, D_MODEL))
    return pl.pallas_call(
        body, name="ffn_down_loss", grid=(s_len // tm,),
        in_specs=[_ffn_spec(tm), _ffn_spec(tm), _vmem(), _rows(tm, D_MODEL), c, c, c, c, _rows(tm, D_MODEL)],
        out_specs=[_rows(tm, D_MODEL), c, c, c],
        out_shape=[jax.ShapeDtypeStruct((s_len, D_MODEL), F32), vec, vec, vec],
        compiler_params=_params(48),
    )(gact, uact, wd, r1, g1, b1, g2, b2, target)


def _ffn_bwd_a(dr2, gact, uact, wd):
    s_len = dr2.shape[0]
    tm = _tile(s_len, 512)

    def body(dr2_ref, g_ref, u_ref, wd_ref, dg_ref, du_ref, dwd_ref, wire_ref, land_ref, send_sem, recv_sem):
        i = pl.program_id(0)

        @pl.when(i == 0)
        def _():
            dwd_ref[...] = jnp.zeros_like(dwd_ref)

        dfb = dr2_ref[...].astype(_MXU)
        for j in range(N_CHIP):
            g = g_ref[j].astype(F32)
            u = u_ref[j].astype(F32)
            silu, sg = _silu_parts(g)
            da = _dot_nt(dfb, wd_ref[j])
            dg_ref[j] = (da * u * (sg * (1.0 + g * (1.0 - sg)))).astype(_MXU)
            du_ref[j] = (da * silu).astype(_MXU)
            dwd_ref[j * FF_SH:(j + 1) * FF_SH, :] += _dot_tn((silu * u).astype(_MXU), dfb)

        @pl.when(i == pl.num_programs(0) - 1)
        def _():
            _pair_reduce(dwd_ref, land_ref, wire_ref, send_sem, recv_sem)

    sd = jax.ShapeDtypeStruct((N_CHIP, s_len, FF_SH), _MXU)
    return pl.pallas_call(
        body, name="ffn_bwd_a", grid=(s_len // tm,),
        in_specs=[_rows(tm, D_MODEL), _ffn_spec(tm), _ffn_spec(tm), _vmem()],
        out_specs=[_ffn_spec(tm), _ffn_spec(tm), _vmem(), _vmem()],
        out_shape=[sd, sd, jax.ShapeDtypeStruct((D_FF, D_MODEL), F32),
                   jax.ShapeDtypeStruct((N_CHIP, FF_SH // 2, D_MODEL), _WIRE)],
        scratch_shapes=_pair_scratch((N_CHIP, FF_SH // 2, D_MODEL)),
        compiler_params=_params(58),
    )(dr2, gact, uact, wd)


def _ffn_bwd_g(dr2, dg, r1, g1, b1, wg, prev_wire):
    s_len = dr2.shape[0]
    tm = _tile(s_len, 512)

    def body(dr2_ref, dg_ref, r1_ref, g1_ref, b1_ref, wg_ref, pw_ref, dh1_ref, dwg_ref, wire_ref, pl_ref,
             land_ref, send_sem, recv_sem, xl_ref, x_send, x_recv, x_flush):
        i = pl.program_id(0)
        exchange = _ChipExchange(pw_ref, xl_ref, x_send, x_recv)

        @pl.when(i == 0)
        def _():
            exchange.start()
            dwg_ref[...] = jnp.zeros_like(dwg_ref)

        h1, _, _ = _ln(r1_ref[...], g1_ref[...], b1_ref[...])
        h1b = h1.astype(_MXU)
        dh1 = ALPHA * dr2_ref[...]
        for j in range(N_CHIP):
            dgj = dg_ref[j]
            dh1 = dh1 + _dot(dgj, wg_ref[j])
            dwg_ref[j * FF_SH:(j + 1) * FF_SH, :] += _dot_tn(dgj, h1b)
        dh1_ref[...] = dh1

        @pl.when(i == pl.num_programs(0) - 1)
        def _():
            _pair_reduce(dwg_ref, land_ref, wire_ref, send_sem, recv_sem)
            exchange.finish_to(pl_ref, x_flush)

    c = _const2((1, D_MODEL))
    return pl.pallas_call(
        body, name="ffn_bwd_g", grid=(s_len // tm,),
        in_specs=[_rows(tm, D_MODEL), _ffn_spec(tm), _rows(tm, D_MODEL), c, c, _vmem(), _vmem()],
        out_specs=[_rows(tm, D_MODEL), _vmem(), _vmem(), _hbm()],
        out_shape=[jax.ShapeDtypeStruct((s_len, D_MODEL), F32), jax.ShapeDtypeStruct((D_FF, D_MODEL), F32),
                   jax.ShapeDtypeStruct((N_CHIP, FF_SH // 2, D_MODEL), _WIRE), _ChipExchange.land_shape(prev_wire)],
        scratch_shapes=_pair_scratch((N_CHIP, FF_SH // 2, D_MODEL)) + _ChipExchange.scratch(prev_wire),
        compiler_params=_params(58),
    )(dr2, dg, r1, g1, b1, wg, prev_wire)


def _ffn_bwd_u(dh1a, du, r1, g1, b1, wu, prev_wire):
    s_len = dh1a.shape[0]
    tm = _tile(s_len, 512)

    def body(dh1_ref, du_ref, r1_ref, g1_ref, b1_ref, wu_ref, pw_ref,
             dr1_ref, dwu_ref, wire_ref, dg1_ref, db1_ref, pl_ref,
             land_ref, send_sem, recv_sem, xl_ref, x_send, x_recv, x_flush):
        i = pl.program_id(0)
        exchange = _ChipExchange(pw_ref, xl_ref, x_send, x_recv)

        @pl.when(i == 0)
        def _():
            exchange.start()
            dwu_ref[...] = jnp.zeros_like(dwu_ref)
            dg1_ref[...] = jnp.zeros_like(dg1_ref)
            db1_ref[...] = jnp.zeros_like(db1_ref)

        h1, r1hat, rstd1 = _ln(r1_ref[...], g1_ref[...], b1_ref[...])
        h1b = h1.astype(_MXU)
        dh1 = dh1_ref[...]
        for j in range(N_CHIP):
            duj = du_ref[j]
            dh1 = dh1 + _dot(duj, wu_ref[j])
            dwu_ref[j * FF_SH:(j + 1) * FF_SH, :] += _dot_tn(duj, h1b)
        dg1_ref[...] += _colsum(dh1 * r1hat)
        db1_ref[...] += _colsum(dh1)
        dr1_ref[...] = _ln_bwd(dh1, r1hat, rstd1, g1_ref[...])

        @pl.when(i == pl.num_programs(0) - 1)
        def _():
            _pair_reduce(dwu_ref, land_ref, wire_ref, send_sem, recv_sem)
            exchange.finish_to(pl_ref, x_flush)

    vec = jax.ShapeDtypeStruct((1, D_MODEL), F32)
    c = _const2((1, D_MODEL))
    return pl.pallas_call(
        body, name="ffn_bwd_u", grid=(s_len // tm,),
        in_specs=[_rows(tm, D_MODEL), _ffn_spec(tm), _rows(tm, D_MODEL), c, c, _vmem(), _vmem()],
        out_specs=[_rows(tm, D_MODEL), _vmem(), _vmem(), c, c, _hbm()],
        out_shape=[jax.ShapeDtypeStruct((s_len, D_MODEL), F32), jax.ShapeDtypeStruct((D_FF, D_MODEL), F32),
                   jax.ShapeDtypeStruct((N_CHIP, FF_SH // 2, D_MODEL), _WIRE), vec, vec,
                   _ChipExchange.land_shape(prev_wire)],
        scratch_shapes=_pair_scratch((N_CHIP, FF_SH // 2, D_MODEL)) + _ChipExchange.scratch(prev_wire),
        compiler_params=_params(58),
    )(dh1a, du, r1, g1, b1, wu, prev_wire)


def _outproj_bwd(dr1, mc, w_out, prev_wire):
    s_len = dr1.shape[0]
    tm = _tile(s_len, 512)

    def body(dr1_ref, mc_ref, w_ref, pw_ref, dmc_ref, dw_ref, wire_ref, db_ref, pl_ref,
             land_ref, send_sem, recv_sem, xl_ref, x_send, x_recv, x_flush):
        i = pl.program_id(0)
        exchange = _ChipExchange(pw_ref, xl_ref, x_send, x_recv)

        @pl.when(i == 0)
        def _():
            exchange.start()
            dw_ref[...] = jnp.zeros_like(dw_ref)
            db_ref[...] = jnp.zeros_like(db_ref)

        d = dr1_ref[...]
        db_ref[...] += _colsum(d)
        db16 = d.astype(_MXU)
        dmc_ref[...] = _dot_nt(db16, w_ref[...])
        dw_ref[...] += _dot_tn(mc_ref[...], db16)

        @pl.when(i == pl.num_programs(0) - 1)
        def _():
            _pair_reduce(dw_ref, land_ref, wire_ref, send_sem, recv_sem)
            exchange.finish_to(pl_ref, x_flush)

    return pl.pallas_call(
        body, name="outproj_bwd", grid=(s_len // tm,),
        in_specs=[_rows(tm, D_MODEL), _rows(tm, D_MODEL), _vmem(), _vmem()],
        out_specs=[_rows(tm, D_MODEL), _vmem(), _vmem(), _const2((1, D_MODEL)), _hbm()],
        out_shape=[jax.ShapeDtypeStruct((s_len, D_MODEL), F32), jax.ShapeDtypeStruct((D_MODEL, D_MODEL), F32),
                   jax.ShapeDtypeStruct((N_CHIP, OUT_SH // 2, D_MODEL), _WIRE), jax.ShapeDtypeStruct((1, D_MODEL), F32),
                   _ChipExchange.land_shape(prev_wire)],
        scratch_shapes=_pair_scratch((N_CHIP, OUT_SH // 2, D_MODEL)) + _ChipExchange.scratch(prev_wire),
        compiler_params=_params(48),
    )(dr1, mc, w_out, prev_wire)


def _mixer_bwd(q, k, v, su, sv, dmc, tc, t1, t2, sinks, sg, sb, sgu_w, sgu_bt, prev_wire):
    s_len = q.shape[0]
    nb = s_len // BLK

    def body(q_ref, kc_ref, kp_ref, vc_ref, vp_ref, su_ref, sv_ref, dmc_ref,
             tc_ref, t1_ref, t2_ref, tcp_ref, t1p_ref, t2p_ref,
             sink_ref, lg_ref, lb_ref, w_ref, bt_ref, pw_ref,
             dq_ref, dkv_ref, dsuv_ref, dbq_ref, dbkv_ref, dbsuv_ref,
             dsink_ref, dlg_ref, dlb_ref, dw_ref, dbt_ref, pl_ref, carry_ref, xl_ref, x_send, x_recv, x_flush):
        i = pl.program_id(0)
        exchange = _ChipExchange(pw_ref, xl_ref, x_send, x_recv)

        @pl.when(i == 0)
        def _():
            exchange.start()

        @pl.when(i == 0)
        def _():
            for r in (dbq_ref, dbkv_ref, dbsuv_ref, dsink_ref, dlg_ref, dlb_ref, dw_ref, dbt_ref):
                r[...] = jnp.zeros_like(r)

        def emit_kv(fin):
            dk = _rope_bwd(fin[:, 0:KV_W], tcp_ref[...], t1p_ref[...], t2p_ref[...])
            out = jnp.concatenate([dk, fin[:, KV_W:2 * KV_W]], axis=1)
            dkv_ref[...] = out.astype(_MXU)
            dbkv_ref[...] += _colsum(out)

        @pl.when(i < nb)
        def _():
            allowed = _band_mask(i == 0)
            kb = jnp.concatenate([kp_ref[...], kc_ref[...]], axis=0)
            vb = jnp.concatenate([vp_ref[...], vc_ref[...]], axis=0)
            qv = q_ref[...]
            dmc = dmc_ref[...]
            dqs, dks, dvs, dsinks = [], [], [], []
            for g in range(N_KV):
                kh = kb[:, g * HEAD_DIM:(g + 1) * HEAD_DIM]
                vh = vb[:, g * HEAD_DIM:(g + 1) * HEAD_DIM]
                dk_g = jnp.zeros((2 * BLK, HEAD_DIM), F32)
                dv_g = jnp.zeros((2 * BLK, HEAD_DIM), F32)
                for hh in range(Q_PER_KV):
                    h = g * Q_PER_KV + hh
                    qh = qv[:, h * HEAD_DIM:(h + 1) * HEAD_DIM]
                    probs, psink = _attn_probs(qh, kh, sink_ref[h], allowed)
                    pb = probs.astype(_MXU)
                    dob = dmc[:, h * HEAD_DIM:(h + 1) * HEAD_DIM].astype(_MXU)
                    dv_g = dv_g + _dot_tn(pb, dob)
                    dp = _dot_nt(dob, vh)
                    rd = jnp.sum(probs * dp, axis=-1, keepdims=True)
                    dsb = (probs * (dp - rd)).astype(_MXU)
                    dsinks.append(-jnp.sum(psink * rd, axis=0, keepdims=True))
                    dqs.append(_dot(dsb, kh))
                    dk_g = dk_g + _dot_tn(dsb, qh)
                dks.append(dk_g)
                dvs.append(dv_g)
            dq = _rope_bwd(jnp.concatenate(dqs, axis=1) * (HEAD_DIM ** -0.5), tc_ref[...], t1_ref[...], t2_ref[...])
            dq_ref[...] = dq.astype(_MXU)
            dbq_ref[...] += _colsum(dq)
            dsink_ref[...] += _lane_put(dsinks, 128)
            contrib = jnp.concatenate(dks + dvs, axis=1)

            @pl.when(i > 0)
            def _():
                emit_kv(carry_ref[...] + contrib[0:BLK, :])

            carry_ref[...] = contrib[BLK:2 * BLK, :]

            su = su_ref[...]
            sv = sv_ref[...]
            lg = lg_ref[...]
            u, vhat, rstd, vvb, wcs, mixed = _sgu_fwd(su, sv, lg, lb_ref[...], w_ref, bt_ref)
            dsgu = dmc[:, ATTN_W:D_MODEL]
            dsu = dsgu * mixed * _gelu_grad(su)
            dmixed = dsgu * u
            tri_t = lax.broadcasted_iota(jnp.int32, (BLK, BLK), 0)
            tri_s = lax.broadcasted_iota(jnp.int32, (BLK, BLK), 1)
            dvv, dbs = [], []
            for h in range(N_GRP):
                dm = dmixed[:, h * GRP_DIM:(h + 1) * GRP_DIM]
                dmb = dm.astype(_MXU)
                dbs.append(jnp.sum(dm, axis=1, keepdims=True))
                dw_ref[h] += jnp.where(tri_s <= tri_t, _dot_nt(dmb, vvb[:, h * GRP_DIM:(h + 1) * GRP_DIM]), 0.0)
                dvv.append(_dot_tn(wcs[h], dmb))
            dvv = jnp.concatenate(dvv, axis=1)
            dbt_ref[...] += _lane_put(dbs, 128)
            dlg_ref[...] += _colsum(dvv * vhat)
            dlb_ref[...] += _colsum(dvv)
            dsv = _ln_bwd(dvv, vhat, rstd, lg) * _gelu_grad(sv)
            dsuv = jnp.concatenate([dsu, dsv], axis=1)
            dsuv_ref[...] = dsuv.astype(_MXU)
            dbsuv_ref[...] += _colsum(dsuv)

        @pl.when(i == nb)
        def _():
            emit_kv(carry_ref[...])
            exchange.finish_to(pl_ref, x_flush)

    last = nb - 1
    cur = lambda w: pl.BlockSpec((BLK, w), lambda i: (jnp.minimum(i, last), 0))
    prev = lambda w: pl.BlockSpec((BLK, w), lambda i: (jnp.clip(i - 1, 0, last), 0))
    sd = jax.ShapeDtypeStruct
    return pl.pallas_call(
        body, name="mixer_bwd", grid=(nb + 1,),
        in_specs=[cur(ATTN_W), cur(KV_W), prev(KV_W), cur(KV_W), prev(KV_W), cur(SGU_W), cur(SGU_W), cur(D_MODEL),
                  cur(128), cur(128), cur(128), prev(128), prev(128), prev(128),
                  _smem(), _const2((1, SGU_W)), _const2((1, SGU_W)), _const2((N_GRP, BLK, BLK)), _const2((BLK, N_GRP)),
                  _vmem()],
        out_specs=[cur(ATTN_W), prev(2 * KV_W), cur(2 * SGU_W),
                   _const2((1, ATTN_W)), _const2((1, 2 * KV_W)), _const2((1, 2 * SGU_W)),
                   _const2((1, 128)), _const2((1, SGU_W)), _const2((1, SGU_W)),
                   _const2((N_GRP, BLK, BLK)), _const2((BLK, 128)), _hbm()],
        out_shape=[sd((s_len, ATTN_W), _MXU), sd((s_len, 2 * KV_W), _MXU), sd((s_len, 2 * SGU_W), _MXU),
                   sd((1, ATTN_W), F32), sd((1, 2 * KV_W), F32), sd((1, 2 * SGU_W), F32),
                   sd((1, 128), F32), sd((1, SGU_W), F32), sd((1, SGU_W), F32),
                   sd((N_GRP, BLK, BLK), F32), sd((BLK, 128), F32), _ChipExchange.land_shape(prev_wire)],
        scratch_shapes=[pltpu.VMEM((BLK, 2 * KV_W), F32)] + _ChipExchange.scratch(prev_wire),
        compiler_params=_params(32),
    )(q, k, k, v, v, su, sv, dmc, tc, t1, t2, tc, t1, t2, sinks, sg, sb, sgu_w, sgu_bt, prev_wire)


def _inproj_bwd(dq, dkv, dsuv, dr1, x, g0, b0, w_in):
    s_len = x.shape[0]
    tm = _tile(s_len, 512)
    cuts = ((0, ATTN_W), (ATTN_W, ATTN_W + 2 * KV_W), (ATTN_W + 2 * KV_W, IN_W))

    def body(dq_ref, dkv_ref, dsuv_ref, dr1_ref, x_ref, g_ref, b_ref, w_ref,
             dx_ref, dw_ref, wire_ref, dg_ref, db_ref, land_ref, send_sem, recv_sem):
        i = pl.program_id(0)

        @pl.when(i == 0)
        def _():
            dw_ref[...] = jnp.zeros_like(dw_ref)
            dg_ref[...] = jnp.zeros_like(dg_ref)
            db_ref[...] = jnp.zeros_like(db_ref)

        h0, xhat, rstd = _ln(x_ref[...], g_ref[...], b_ref[...])
        h0b = h0.astype(_MXU)
        dh0 = ALPHA * dr1_ref[...]
        for (lo, hi), d_ref in zip(cuts, (dq_ref, dkv_ref, dsuv_ref)):
            d = d_ref[...]
            dh0 = dh0 + _dot(d, w_ref[lo:hi, :])
            dw_ref[lo:hi, :] += _dot_tn(d, h0b)
        dg_ref[...] += _colsum(dh0 * xhat)
        db_ref[...] += _colsum(dh0)
        dx_ref[...] = _ln_bwd(dh0, xhat, rstd, g_ref[...])

        @pl.when(i == pl.num_programs(0) - 1)
        def _():
            _pair_reduce(dw_ref, land_ref, wire_ref, send_sem, recv_sem)

    vec = jax.ShapeDtypeStruct((1, D_MODEL), F32)
    c = _const2((1, D_MODEL))
    return pl.pallas_call(
        body, name="inproj_bwd", grid=(s_len // tm,),
        in_specs=[_rows(tm, ATTN_W), _rows(tm, 2 * KV_W), _rows(tm, 2 * SGU_W), _rows(tm, D_MODEL), _rows(tm, D_MODEL),
                  c, c, _vmem()],
        out_specs=[_rows(tm, D_MODEL), _vmem(), _vmem(), c, c],
        out_shape=[jax.ShapeDtypeStruct((s_len, D_MODEL), F32), jax.ShapeDtypeStruct((IN_W, D_MODEL), F32),
                   jax.ShapeDtypeStruct((N_CHIP, IN_SH // 2, D_MODEL), _WIRE), vec, vec],
        scratch_shapes=_pair_scratch((N_CHIP, IN_SH // 2, D_MODEL)),
        compiler_params=_params(56),
    )(dq, dkv, dsuv, dr1, x, g0, b0, w_in)


def _place():
    x, y, c = (lax.axis_index(a) for a in MESH_AXES)
    chips = [(1 - x, y), (x, 1 - y), (1 - x, 1 - y)]
    return x, y, c, chips


class _Gather:
    def __init__(self, ins, outs, send_sems, recv_sems):
        self.ins, self.outs, self.send_sems, self.recv_sems = ins, outs, send_sems, recv_sems
        self.n = len(ins)
        self.halves = [r.shape[0] // 2 for r in ins]

    def _copy(self, k, t, slot, half, to):
        rows = pl.ds(pl.multiple_of(half * self.halves[t], 16), self.halves[t])
        piece = self.outs[t].at[slot, rows, :]
        return pltpu.make_async_remote_copy(src_ref=piece, dst_ref=piece, send_sem=self.send_sems.at[k],
                                            recv_sem=self.recv_sems.at[k], device_id=to, device_id_type=MESH)

    def _chip_copy(self, t, d, slot):
        x, y, c, chips = _place()
        return self._copy(3 * t + d, t, slot, c, (chips[d][0], chips[d][1], c))

    def _pass_copy(self, t, d, half):
        x, y, c, chips = _place()
        return self._copy(3 * self.n + 3 * t + d, t, 2 * chips[d][0] + chips[d][1], half, (x, y, 1 - c))

    def start(self):
        x, y, c, chips = _place()
        me = 2 * x + y
        for t in range(self.n):
            self.outs[t][me] = self.ins[t][...].astype(_WIRE)
        for t in range(self.n):
            for d in range(3):
                self._chip_copy(t, d, me).start()

    def finish(self):
        x, y, c, chips = _place()
        me = 2 * x + y
        for t in range(self.n):
            for d in range(3):
                self._chip_copy(t, d, 2 * chips[d][0] + chips[d][1]).wait_recv()
                self._pass_copy(t, d, c).start()
        for t in range(self.n):
            for d in range(3):
                self._pass_copy(t, d, 1 - c).wait_recv()
        for t in range(self.n):
            for d in range(3):
                self._chip_copy(t, d, me).wait_send()
                self._pass_copy(t, d, c).wait_send()

    def flush(self, hbm_outs, flush_sems):
        _flush(self.outs, hbm_outs, flush_sems)

    @staticmethod
    def out_shapes(shards):
        return [jax.ShapeDtypeStruct((N_CHIP,) + s.shape, _WIRE) for s in shards]

    @staticmethod
    def sems(n):
        return [pltpu.SemaphoreType.DMA((6 * n,)), pltpu.SemaphoreType.DMA((6 * n,))]

    @staticmethod
    def scratch(shards):
        n = len(shards)
        return ([pltpu.VMEM((N_CHIP,) + s.shape, _WIRE) for s in shards] + _Gather.sems(n)
                + [pltpu.SemaphoreType.DMA((n,))])


def _flush(bufs, hbm_outs, sems):
    copies = [pltpu.make_async_copy(b, o, sems.at[k]) for k, (b, o) in enumerate(zip(bufs, hbm_outs))]
    for cp in copies:
        cp.start()
    for cp in copies:
        cp.wait()


def _gather_weights(shards):
    n = len(shards)

    def body(*refs):
        gather = _Gather(refs[:n], refs[n:2 * n], refs[2 * n], refs[2 * n + 1])
        gather.start()
        gather.finish()

    return pl.pallas_call(
        body, name="gather_weights",
        in_specs=[_vmem()] * n, out_specs=[_vmem()] * n,
        out_shape=_Gather.out_shapes(shards), scratch_shapes=_Gather.sems(n),
        compiler_params=pltpu.CompilerParams(vmem_limit_bytes=32 * MIB),
    )(*shards)


class _ChipExchange:
    def __init__(self, wire_ref, land_ref, send_sems, recv_sems):
        self.wire, self.land, self.send_sems, self.recv_sems = wire_ref, land_ref, send_sems, recv_sems

    def _copy(self, d):
        x, y, c, chips = _place()
        return pltpu.make_async_remote_copy(
            src_ref=self.wire.at[2 * chips[d][0] + chips[d][1]], dst_ref=self.land.at[d],
            send_sem=self.send_sems.at[d], recv_sem=self.recv_sems.at[d],
            device_id=(chips[d][0], chips[d][1], c), device_id_type=MESH)

    def start(self):
        for d in range(3):
            self._copy(d).start()

    def wait_recv(self):
        for d in range(3):
            self._copy(d).wait_recv()

    def wait_send(self):
        for d in range(3):
            self._copy(d).wait_send()

    def finish_to(self, hbm_out, flush_sem):
        self.wait_recv()
        _flush([self.land], [hbm_out], flush_sem)
        self.wait_send()

    @staticmethod
    def land_shape(wire):
        return jax.ShapeDtypeStruct((3,) + wire.shape[1:], wire.dtype)

    @staticmethod
    def sems():
        return [pltpu.SemaphoreType.DMA((3,)), pltpu.SemaphoreType.DMA((3,))]

    @staticmethod
    def scratch(wire):
        return ([pltpu.VMEM((3,) + wire.shape[1:], wire.dtype)] + _ChipExchange.sems() + [pltpu.SemaphoreType.DMA((1,))])


def _pair_scratch(half_shape):
    return [pltpu.VMEM(half_shape, F32), pltpu.SemaphoreType.DMA((N_CHIP,)), pltpu.SemaphoreType.DMA((N_CHIP,))]


def _pair_reduce(acc_ref, land_ref, wire_ref, send_sems, recv_sems):
    rh = land_ref.shape[1]
    x, y, c, _ = _place()
    copies = []
    for j in range(N_CHIP):
        give = acc_ref.at[pl.ds(pl.multiple_of(j * 2 * rh + (1 - c) * rh, 8), rh), :]
        cp = pltpu.make_async_remote_copy(src_ref=give, dst_ref=land_ref.at[j], send_sem=send_sems.at[j],
                                          recv_sem=recv_sems.at[j], device_id=(x, y, 1 - c), device_id_type=MESH)
        cp.start()
        copies.append(cp)
    for cp in copies:
        cp.wait()

    def chunk(r, carry):
        theirs = pl.ds(pl.multiple_of(r * ROW_CHUNK, ROW_CHUNK), ROW_CHUNK)
        for j in range(N_CHIP):
            mine = pl.ds(pl.multiple_of(j * 2 * rh + c * rh + r * ROW_CHUNK, 8), ROW_CHUNK)
            s = acc_ref[mine, :] + land_ref[j, theirs, :]
            acc_ref[mine, :] = s
            wire_ref[j, theirs, :] = s.astype(_WIRE)
        return carry

    lax.fori_loop(0, rh // ROW_CHUNK, chunk, 0)


def _grad_finish(last_wire, lands, accs):
    n = len(accs)
    halves = [last_wire.shape[1]] + [w.shape[1] for w in lands]
    widths = [a.shape[1] for a in accs]

    def body(*refs):
        wire0, land, acc, g = refs[0], (None,) + refs[1:n], refs[n:2 * n], refs[2 * n:3 * n]
        land0, own = refs[3 * n], refs[3 * n + 1:4 * n + 1]
        x_send, x_recv, pair_send, pair_recv, local_sems = refs[4 * n + 1:4 * n + 6]
        land = (land0,) + land[1:]
        x, y, c, chips = _place()
        me = 2 * x + y
        exchange = _ChipExchange(wire0, land0, x_send, x_recv)

        def half_rows(t, half):
            return pl.ds(pl.multiple_of(half * halves[t], 8), halves[t])

        def own_copy(t):
            rows = pl.ds(pl.multiple_of((2 * me + c) * halves[t], 8), halves[t])
            return pltpu.make_async_copy(acc[t].at[rows, :], own[t], local_sems.at[t])

        def pair_copy(t, half):
            rows = g[t].at[half_rows(t, half), :]
            return pltpu.make_async_remote_copy(src_ref=rows, dst_ref=rows, send_sem=pair_send.at[t],
                                                recv_sem=pair_recv.at[t], device_id=(x, y, 1 - c), device_id_type=MESH)

        exchange.start()
        for t in range(n):
            own_copy(t).start()
        for t in list(range(1, n)) + [0]:
            own_copy(t).wait()
            if t == 0:
                exchange.wait_recv()

            def chunk(r, carry, t=t):
                src = pl.ds(pl.multiple_of(r * ROW_CHUNK, ROW_CHUNK), ROW_CHUNK)
                dst = pl.ds(pl.multiple_of(c * halves[t] + r * ROW_CHUNK, 8), ROW_CHUNK)
                s = own[t][src, :]
                for d in range(3):
                    s = s + land[t][d, src, :].astype(F32)
                g[t][dst, :] = s
                return carry

            lax.fori_loop(0, halves[t] // ROW_CHUNK, chunk, 0)
            pair_copy(t, c).start()
        for t in range(n):
            pair_copy(t, 1 - c).wait_recv()
        for t in range(n):
            pair_copy(t, c).wait_send()
        exchange.wait_send()

    return pl.pallas_call(
        body, name="grad_finish",
        in_specs=[_vmem()] * n + [_hbm()] * n, out_specs=[_vmem()] * n,
        out_shape=[jax.ShapeDtypeStruct((2 * h, w), F32) for h, w in zip(halves, widths)],
        scratch_shapes=[pltpu.VMEM((3,) + last_wire.shape[1:], last_wire.dtype)]
        + [pltpu.VMEM((h, w), F32) for h, w in zip(halves, widths)]
        + _ChipExchange.sems()
        + [pltpu.SemaphoreType.DMA((n,)), pltpu.SemaphoreType.DMA((n,)), pltpu.SemaphoreType.DMA((n,))],
        compiler_params=pltpu.CompilerParams(vmem_limit_bytes=56 * MIB),
    )(last_wire, *lands, *accs)


def _allreduce_small(packed):
    rows = packed.shape[0]

    def body(p_ref, out_ref, buf_ref, send_sems, recv_sems):
        x, y, c, _ = _place()
        me = 4 * x + 2 * y + c
        buf_ref[me] = p_ref[...]
        copies = []
        for r in range(1, 8):
            rx, ry, rc_ = (r >> 2) & 1, (r >> 1) & 1, r & 1
            peer = (x ^ rx, y ^ ry, c ^ rc_)
            cp = pltpu.make_async_remote_copy(src_ref=buf_ref.at[me], dst_ref=buf_ref.at[me],
                                              send_sem=send_sems.at[r - 1], recv_sem=recv_sems.at[r - 1],
                                              device_id=peer, device_id_type=MESH)
            cp.start()
            copies.append(cp)
        for cp in copies:
            cp.wait()
        acc = buf_ref[0]
        for d in range(1, 8):
            acc = acc + buf_ref[d]
        out_ref[...] = acc

    return pl.pallas_call(
        body, name="allreduce_small",
        in_specs=[_vmem()], out_specs=_vmem(),
        out_shape=jax.ShapeDtypeStruct((rows, 128), F32),
        scratch_shapes=[pltpu.VMEM((8, rows, 128), F32), pltpu.SemaphoreType.DMA((7,)), pltpu.SemaphoreType.DMA((7,))],
        compiler_params=pltpu.CompilerParams(vmem_limit_bytes=32 * MIB),
    )(packed)


def _elementwise(name, fn, ins, out_dtypes, tile_rows=256):
    shape = ins[0].shape
    lead = shape[:-2]
    rows, cols = shape[-2:]
    tr = _tile(rows, tile_rows)
    n_lead = math.prod(lead)
    nr = rows // tr
    flat = [a.reshape((n_lead, rows, cols)) for a in ins]

    def body(*refs):
        outs = fn(*[r[0] for r in refs[:len(ins)]])
        for o_ref, o in zip(refs[len(ins):], outs):
            o_ref[0] = o.astype(o_ref.dtype)

    spec = pl.BlockSpec((1, tr, cols), lambda i: (i // nr, i % nr, 0))
    res = pl.pallas_call(
        body, name=name, grid=(n_lead * nr,),
        in_specs=[spec] * len(ins), out_specs=[spec] * len(out_dtypes),
        out_shape=[jax.ShapeDtypeStruct((n_lead, rows, cols), dt) for dt in out_dtypes],
        compiler_params=_params(32),
    )(*flat)
    return [r.reshape(shape) for r in res]


def _adamw_math(w, g, m, v):
    m = ADAM_B1 * m + (1.0 - ADAM_B1) * g
    v = ADAM_B2 * v + (1.0 - ADAM_B2) * (g * g)
    m_hat = m / (1.0 - ADAM_B1 ** ADAM_STEP)
    v_hat = v / (1.0 - ADAM_B2 ** ADAM_STEP)
    delta = -ADAM_LR * (m_hat / (jnp.sqrt(v_hat) + ADAM_EPS) + ADAM_WD * w)
    return delta, m, v


def _adamw(name, w, g, m, v, tile_rows=256):
    return _elementwise(name, _adamw_math, [w, g, m, v], [F32, F32, F32], tile_rows)


_SMALL = ("ln_in_g", "ln_in_b", "b_in", "attn_sinks", "sgu_ln_g", "sgu_ln_b", "sgu_w", "sgu_b", "b_out",
          "ln_mix_g", "ln_mix_b", "ln_ffn_g", "ln_ffn_b")


def _pack(arrs):
    parts = []
    for a in arrs:
        flat = a.reshape(-1)
        pad = (-flat.shape[0]) % 1024
        parts.append(jnp.pad(flat, (0, pad)) if pad else flat)
    return jnp.concatenate(parts).reshape(-1, 128)


def _unpack(packed, like):
    flat = packed.reshape(-1)
    out, off = [], 0
    for a in like:
        n = math.prod(a.shape)
        out.append(flat[off:off + n].reshape(a.shape))
        off += n + ((-n) % 1024)
    return out


def kernel(x, positions, ln_in_g, ln_in_b, w_in, b_in, attn_sinks, sgu_ln_g, sgu_ln_b, sgu_w, sgu_b, w_out, b_out, ln_mix_g, ln_mix_b, w_gate, w_up, w_down, ln_ffn_g, ln_ffn_b, loss_target, m_ln_in_g, m_ln_in_b, m_w_in, m_b_in, m_attn_sinks, m_sgu_ln_g, m_sgu_ln_b, m_sgu_w, m_sgu_b, m_w_out, m_b_out, m_ln_mix_g, m_ln_mix_b, m_w_gate, m_w_up, m_w_down, m_ln_ffn_g, m_ln_ffn_b, v_ln_in_g, v_ln_in_b, v_w_in, v_b_in, v_attn_sinks, v_sgu_ln_g, v_sgu_ln_b, v_sgu_w, v_sgu_b, v_w_out, v_b_out, v_ln_mix_g, v_ln_mix_b, v_w_gate, v_w_up, v_w_down, v_ln_ffn_g, v_ln_ffn_b):
    weights = dict(ln_in_g=ln_in_g, ln_in_b=ln_in_b, w_in=w_in, b_in=b_in, attn_sinks=attn_sinks, sgu_ln_g=sgu_ln_g,
                   sgu_ln_b=sgu_ln_b, sgu_w=sgu_w, sgu_b=sgu_b, w_out=w_out, b_out=b_out, ln_mix_g=ln_mix_g,
                   ln_mix_b=ln_mix_b, w_gate=w_gate, w_up=w_up, w_down=w_down, ln_ffn_g=ln_ffn_g, ln_ffn_b=ln_ffn_b)
    mom_m = dict(ln_in_g=m_ln_in_g, ln_in_b=m_ln_in_b, w_in=m_w_in, b_in=m_b_in, attn_sinks=m_attn_sinks,
                 sgu_ln_g=m_sgu_ln_g, sgu_ln_b=m_sgu_ln_b, sgu_w=m_sgu_w, sgu_b=m_sgu_b, w_out=m_w_out, b_out=m_b_out,
                 ln_mix_g=m_ln_mix_g, ln_mix_b=m_ln_mix_b, w_gate=m_w_gate, w_up=m_w_up, w_down=m_w_down,
                 ln_ffn_g=m_ln_ffn_g, ln_ffn_b=m_ln_ffn_b)
    mom_v = dict(ln_in_g=v_ln_in_g, ln_in_b=v_ln_in_b, w_in=v_w_in, b_in=v_b_in, attn_sinks=v_attn_sinks,
                 sgu_ln_g=v_sgu_ln_g, sgu_ln_b=v_sgu_ln_b, sgu_w=v_sgu_w, sgu_b=v_sgu_b, w_out=v_w_out, b_out=v_b_out,
                 ln_mix_g=v_ln_mix_g, ln_mix_b=v_ln_mix_b, w_gate=v_w_gate, w_up=v_w_up, w_down=v_w_down,
                 ln_ffn_g=v_ln_ffn_g, ln_ffn_b=v_ln_ffn_b)
    order = list(weights)
    big = ("w_in", "w_out", "w_gate", "w_up", "w_down")

    s_len = x.shape[1]
    xs = x.reshape(s_len, D_MODEL)
    tgt = loss_target.reshape(s_len, D_MODEL)
    pos_col = positions.reshape(s_len, 1)
    g0, b0 = ln_in_g.reshape(1, D_MODEL), ln_in_b.reshape(1, D_MODEL)
    sinks = attn_sinks.reshape(N_Q)
    sgu_w3 = sgu_w.reshape(N_GRP, BLK, BLK)
    sgu_bt = sgu_b.reshape(N_GRP, BLK).T

    col_sharded = ("w_in", "w_gate", "w_up")

    def rowmajor(name, a):
        return jnp.swapaxes(a[0], 0, 1) if name in col_sharded else a[0]

    def as_given(name, a):
        return (jnp.swapaxes(a, 0, 1) if name in col_sharded else a)[None]

    shards = [rowmajor(n, weights[n]) for n in big]
    (gw_in,) = _gather_weights(shards[0:1])
    w_in_full = gw_in.reshape(IN_W, D_MODEL)

    q, k, v, su, sv, tc, t1, t2, gw_out = _ln_inproj(xs, pos_col, g0, b0, w_in_full, b_in, shards[1:2])
    mc, gw_gate = _mixer_fwd(q, k, v, su, sv, sinks, sgu_ln_g, sgu_ln_b, sgu_w3, sgu_bt, shards[2:3])
    w_out_full = gw_out.reshape(D_MODEL, D_MODEL)
    r1, gw_up = _outproj(mc, w_out_full, b_out, xs, g0, b0, shards[3:4])
    gact, uact, gw_down = _ffn_up(r1, ln_mix_g, ln_mix_b, gw_gate, gw_up, shards[4:5])
    dr2, loss_cols, d_ln_ffn_g, d_ln_ffn_b = _ffn_down_loss(gact, uact, gw_down, r1, ln_mix_g, ln_mix_b,
                                                            ln_ffn_g, ln_ffn_b, tgt)

    dg, du, acc_down, wire_down = _ffn_bwd_a(dr2, gact, uact, gw_down)
    dh1a, acc_gate, wire_gate, land_down = _ffn_bwd_g(dr2, dg, r1, ln_mix_g, ln_mix_b, gw_gate, wire_down)
    dr1, acc_up, wire_up, d_ln_mix_g, d_ln_mix_b, land_gate = _ffn_bwd_u(dh1a, du, r1, ln_mix_g, ln_mix_b, gw_up,
                                                                         wire_gate)
    dmc, acc_out, wire_out, d_b_out, land_up = _outproj_bwd(dr1, mc, w_out_full, wire_up)
    (dq, dkv, dsuv, dbq, dbkv, dbsuv, d_sink, d_sgu_ln_g, d_sgu_ln_b, d_sgu_w, d_sgu_bt, land_out) = _mixer_bwd(
        q, k, v, su, sv, dmc, tc, t1, t2, sinks, sgu_ln_g, sgu_ln_b, sgu_w3, sgu_bt, wire_out)
    grad_x, acc_in, wire_in, d_ln_in_g, d_ln_in_b = _inproj_bwd(dq, dkv, dsuv, dr1, xs, g0, b0, w_in_full)

    reduced = _grad_finish(wire_in, [land_out, land_gate, land_up, land_down],
                           [acc_in, acc_out, acc_gate, acc_up, acc_down])
    small_local = dict(
        ln_in_g=d_ln_in_g.reshape(ln_in_g.shape), ln_in_b=d_ln_in_b.reshape(ln_in_b.shape),
        b_in=jnp.concatenate([dbq, dbkv, dbsuv], axis=1), attn_sinks=d_sink[:, :N_Q],
        sgu_ln_g=d_sgu_ln_g, sgu_ln_b=d_sgu_ln_b, sgu_w=d_sgu_w.reshape(sgu_w.shape),
        sgu_b=d_sgu_bt[:, :N_GRP].T.reshape(sgu_b.shape), b_out=d_b_out,
        ln_mix_g=d_ln_mix_g, ln_mix_b=d_ln_mix_b, ln_ffn_g=d_ln_ffn_g, ln_ffn_b=d_ln_ffn_b)
    small_sum = _allreduce_small(_pack([small_local[n] for n in _SMALL] + [loss_cols]))
    like = [weights[n] for n in _SMALL]
    *small_grads, loss_sum = _unpack(small_sum, like + [loss_cols])
    loss = jnp.sum(loss_sum) * (0.5 / D_MODEL)
    grads = dict(zip(_SMALL, small_grads))

    delta, new_m, new_v = {}, {}, {}
    for t, name in enumerate(big):
        d_, m_, v_ = _adamw("adamw_" + name, shards[t], reduced[t], rowmajor(name, mom_m[name]),
                            rowmajor(name, mom_v[name]))
        grads[name] = as_given(name, reduced[t])
        delta[name], new_m[name], new_v[name] = as_given(name, d_), as_given(name, m_), as_given(name, v_)
    spare = jnp.zeros_like(loss_cols)
    packs = [_pack([src[n] for n in _SMALL] + [spare]) for src in (weights, mom_m, mom_v)]
    d_, m_, v_ = _adamw("adamw_small", packs[0], small_sum, packs[1], packs[2], tile_rows=packs[0].shape[0])
    for n, a, b, c_ in zip(_SMALL, _unpack(d_, like), _unpack(m_, like), _unpack(v_, like)):
        delta[n], new_m[n], new_v[n] = a, b, c_

    return (loss, grad_x.reshape(x.shape), *[grads[n] for n in order], *[delta[n] for n in order],
            *[new_m[n] for n in order], *[new_v[n] for n in order])
```

```python
import functools
import math

import jax
import jax.numpy as jnp
from jax import lax
from jax.experimental import pallas as pl
from jax.experimental.pallas import tpu as pltpu

F32 = jnp.float32
_MXU = jnp.bfloat16
_WIRE = jnp.bfloat16
_ACT = jnp.bfloat16

D_MODEL = 1024
ATTN_W = 512
SGU_W = 512
HEAD_DIM = 64
N_Q = 8
N_KV = 2
Q_PER_KV = 4
KV_W = 128
BLK = 128
ROT_DIM = 16
ROPE_THETA = 500000.0
N_GRP = 4
GRP_DIM = 128
D_FF = 2816
IN_W = 1792
LN_EPS = 1e-5
ALPHA = 2.0 ** 0.25
N_CHIP = 4
FF_SH = D_FF // N_CHIP
IN_SH = IN_W // N_CHIP
OUT_SH = D_MODEL // N_CHIP
ROW_CHUNK = 32

ADAM_LR = 0.001
ADAM_B1 = 0.9
ADAM_B2 = 0.999
ADAM_EPS = 1e-08
ADAM_WD = 0.01
ADAM_STEP = 10

SQRT_HALF = 0.7071067811865476
INV_SQRT_2PI = 0.3989422804014327
MESH_AXES = ("x", "y", "c")
MESH = pl.DeviceIdType.MESH
MIB = 2 ** 20


def _vmem():
    return pl.BlockSpec(memory_space=pltpu.VMEM)


def _smem():
    return pl.BlockSpec(memory_space=pltpu.SMEM)


def _hbm():
    return pl.BlockSpec(memory_space=pl.ANY)


def _params(vmem_mib=48):
    return pltpu.CompilerParams(dimension_semantics=("arbitrary",), vmem_limit_bytes=vmem_mib * MIB)


def _tile(n, cap):
    if n <= cap:
        return n
    for t in range(cap - cap % 16, 0, -16):
        if n % t == 0:
            return t
    raise ValueError((n, cap))


def _rows(tm, width):
    return pl.BlockSpec((tm, width), lambda i: (i, 0))


def _const2(shape):
    return pl.BlockSpec(shape, lambda i: (0,) * len(shape))


def _ln(x, g, b):
    mu = jnp.mean(x, axis=-1, keepdims=True)
    xc = x - mu
    var = jnp.mean(xc * xc, axis=-1, keepdims=True)
    rstd = lax.rsqrt(var + LN_EPS)
    xhat = xc * rstd
    return xhat * g + b, xhat, rstd


def _ln_bwd(dy, xhat, rstd, g):
    gdy = dy * g
    m1 = jnp.mean(gdy, axis=-1, keepdims=True)
    m2 = jnp.mean(gdy * xhat, axis=-1, keepdims=True)
    return rstd * (gdy - m1 - xhat * m2)


def _colsum(a):
    return jnp.sum(a, axis=0, keepdims=True)


def _gelu(x):
    return 0.5 * x * (1.0 + lax.erf(x * SQRT_HALF))


def _gelu_grad(x):
    return 0.5 * (1.0 + lax.erf(x * SQRT_HALF)) + x * jnp.exp(-0.5 * x * x) * INV_SQRT_2PI


def _dot(a, b):
    return jnp.dot(a, b, preferred_element_type=F32)


def _dot_nt(a, b):
    return lax.dot_general(a, b, (((1,), (1,)), ((), ())), preferred_element_type=F32)


def _dot_tn(a, b):
    return lax.dot_general(a, b, (((0,), (0,)), ((), ())), preferred_element_type=F32)


def _rope(t, tc, t1, t2):
    n = t.shape[1]
    rep = n // 128
    if rep > 1:
        tc, t1, t2 = (jnp.tile(a, (1, rep)) for a in (tc, t1, t2))
    return t * tc + pltpu.roll(t, n - 8, 1) * t1 + pltpu.roll(t, 8, 1) * t2


def _rope_bwd(d, tc, t1, t2):
    n = d.shape[1]
    rep = n // 128
    if rep > 1:
        tc, t1, t2 = (jnp.tile(a, (1, rep)) for a in (tc, t1, t2))
    return d * tc + pltpu.roll(d * t1, 8, 1) + pltpu.roll(d * t2, n - 8, 1)


def _band_mask(first_block):
    qi = lax.broadcasted_iota(jnp.int32, (BLK, 2 * BLK), 0)
    kj = lax.broadcasted_iota(jnp.int32, (BLK, 2 * BLK), 1)
    shut = jnp.where(first_block, 2 * BLK, 0)
    prev_ok = jnp.logical_and(kj < BLK, kj > qi + shut)
    cur_ok = jnp.logical_and(kj >= BLK, (kj - BLK) <= qi)
    return jnp.logical_or(prev_ok, cur_ok)


def _causal_w(w_ref, h):
    t = lax.broadcasted_iota(jnp.int32, (BLK, BLK), 0)
    s = lax.broadcasted_iota(jnp.int32, (BLK, BLK), 1)
    return jnp.where(s <= t, w_ref[h], 0.0)


def _lane_put(vals, width):
    rows = vals[0].shape[0]
    lane = lax.broadcasted_iota(jnp.int32, (rows, width), 1)
    out = jnp.zeros((rows, width), F32)
    for k, v in enumerate(vals):
        out = out + jnp.where(lane == k, v, 0.0)
    return out


def _rope_consts():
    lane = jnp.arange(128) % HEAD_DIM
    inv_freq = ROPE_THETA ** (-jnp.arange(0, ROT_DIM, 2, dtype=F32) / ROT_DIM)
    rot = lane < ROT_DIM
    freq = jnp.where(rot, inv_freq[lane % (ROT_DIM // 2)], 0.0)
    rows = [freq, rot.astype(F32), 1.0 - rot.astype(F32), (lane < ROT_DIM // 2).astype(F32),
            jnp.logical_and(lane >= ROT_DIM // 2, rot).astype(F32)]
    rows += [jnp.zeros((128,), F32)] * 3
    return jnp.stack(rows).astype(F32)


def _ln_inproj(x, pos_col, g0, b0, w_in, b_in, shards):
    s_len = x.shape[0]
    tm = _tile(s_len, 512)

    n = len(shards)

    def body(x_ref, pos_ref, g_ref, b_ref, w_ref, bi_ref, rc_ref, *rest):
        q_ref, k_ref, v_ref, su_ref, sv_ref, tc_ref, t1_ref, t2_ref = rest[n:n + 8]
        gathered = rest[n + 8:2 * n + 8]
        gather = _Gather(rest[:n], rest[2 * n + 8:3 * n + 8], rest[3 * n + 8], rest[3 * n + 9])
        flush_sems = rest[3 * n + 10]
        i = pl.program_id(0)

        @pl.when(i == 0)
        def _():
            gather.start()

        h0, _, _ = _ln(x_ref[...], g_ref[...], b_ref[...])
        proj = _dot_nt(h0.astype(_MXU), w_ref[...]) + bi_ref[...]
        ang = pos_ref[...].astype(F32) * rc_ref[0:1, :]
        cs = jnp.cos(ang)
        sn = jnp.sin(ang)
        tc = cs * rc_ref[1:2, :] + rc_ref[2:3, :]
        t1 = -sn * rc_ref[3:4, :]
        t2 = sn * rc_ref[4:5, :]
        tc_ref[...] = tc
        t1_ref[...] = t1
        t2_ref[...] = t2
        q = _rope(proj[:, 0:ATTN_W], tc, t1, t2) * (HEAD_DIM ** -0.5)
        q_ref[...] = q.astype(_MXU)
        k_ref[...] = _rope(proj[:, ATTN_W:ATTN_W + KV_W], tc, t1, t2).astype(_MXU)
        v_ref[...] = proj[:, ATTN_W + KV_W:ATTN_W + 2 * KV_W].astype(_MXU)
        su_ref[...] = proj[:, ATTN_W + 2 * KV_W:ATTN_W + 2 * KV_W + SGU_W]
        sv_ref[...] = proj[:, ATTN_W + 2 * KV_W + SGU_W:IN_W]

        @pl.when(i == pl.num_programs(0) - 1)
        def _():
            gather.finish()
            gather.flush(gathered, flush_sems)

    sd = jax.ShapeDtypeStruct
    return pl.pallas_call(
        body, name="ln_inproj", grid=(s_len // tm,),
        in_specs=[_rows(tm, D_MODEL), _rows(tm, 1), _const2((1, D_MODEL)), _const2((1, D_MODEL)), _vmem(),
                  _const2((1, IN_W)), _const2((8, 128))] + [_vmem()] * n,
        out_specs=[_rows(tm, ATTN_W), _rows(tm, KV_W), _rows(tm, KV_W), _rows(tm, SGU_W), _rows(tm, SGU_W),
                   _rows(tm, 128), _rows(tm, 128), _rows(tm, 128)] + [_hbm()] * n,
        out_shape=[sd((s_len, ATTN_W), _MXU), sd((s_len, KV_W), _MXU), sd((s_len, KV_W), _MXU),
                   sd((s_len, SGU_W), F32), sd((s_len, SGU_W), F32),
                   sd((s_len, 128), F32), sd((s_len, 128), F32), sd((s_len, 128), F32)] + _Gather.out_shapes(shards),
        scratch_shapes=_Gather.scratch(shards),
        compiler_params=_params(56),
    )(x, pos_col, g0, b0, w_in, b_in, _rope_consts(), *shards)


def _attn_probs(qh, kh, sink, allowed):
    s = jnp.where(allowed, _dot_nt(qh, kh), -1e30)
    m = jnp.maximum(jnp.max(s, axis=-1, keepdims=True), sink)
    p = jnp.exp(s - m)
    ps = jnp.exp(sink - m)
    inv = 1.0 / (jnp.sum(p, axis=-1, keepdims=True) + ps)
    return p * inv, ps * inv


def _sgu_fwd(su, sv, lg, lb, w_ref, bt_ref):
    u = _gelu(su)
    vv, vhat, rstd = _ln(_gelu(sv), lg, lb)
    vvb = vv.astype(_MXU)
    wcs, mixed = [], []
    for h in range(N_GRP):
        wc = _causal_w(w_ref, h).astype(_MXU)
        wcs.append(wc)
        mixed.append(_dot(wc, vvb[:, h * GRP_DIM:(h + 1) * GRP_DIM]) + bt_ref[:, h:h + 1])
    return u, vhat, rstd, vvb, wcs, jnp.concatenate(mixed, axis=1)


def _prev_map(i):
    return (jnp.maximum(i - 1, 0), 0)


def _mixer_fwd(q, k, v, su, sv, sinks, sg, sb, sgu_w, sgu_bt, shards):
    s_len = q.shape[0]
    nb = s_len // BLK
    n = len(shards)

    def body(q_ref, kc_ref, kp_ref, vc_ref, vp_ref, su_ref, sv_ref, sink_ref, lg_ref, lb_ref, w_ref, bt_ref, *rest):
        mc_ref = rest[n]
        gathered = rest[n + 1:2 * n + 1]
        gather = _Gather(rest[:n], rest[2 * n + 1:3 * n + 1], rest[3 * n + 1], rest[3 * n + 2])
        flush_sems = rest[3 * n + 3]
        i = pl.program_id(0)

        @pl.when(i == 0)
        def _():
            gather.start()

        @pl.when(i == nb - 1)
        def _():
            gather.finish()
            gather.flush(gathered, flush_sems)

        allowed = _band_mask(i == 0)
        kb = jnp.concatenate([kp_ref[...], kc_ref[...]], axis=0)
        vb = jnp.concatenate([vp_ref[...], vc_ref[...]], axis=0)
        qv = q_ref[...]
        outs = []
        for h in range(N_Q):
            g = h // Q_PER_KV
            kh = kb[:, g * HEAD_DIM:(g + 1) * HEAD_DIM]
            vh = vb[:, g * HEAD_DIM:(g + 1) * HEAD_DIM]
            probs, _ = _attn_probs(qv[:, h * HEAD_DIM:(h + 1) * HEAD_DIM], kh, sink_ref[h], allowed)
            outs.append(_dot(probs.astype(_MXU), vh))
        u, _, _, _, _, mixed = _sgu_fwd(su_ref[...], sv_ref[...], lg_ref[...], lb_ref[...], w_ref, bt_ref)
        mc_ref[...] = jnp.concatenate(outs + [u * mixed], axis=1).astype(_MXU)

    cur = lambda w: pl.BlockSpec((BLK, w), lambda i: (i, 0))
    prev = lambda w: pl.BlockSpec((BLK, w), _prev_map)
    return pl.pallas_call(
        body, name="mixer_fwd", grid=(nb,),
        in_specs=[cur(ATTN_W), cur(KV_W), prev(KV_W), cur(KV_W), prev(KV_W), cur(SGU_W), cur(SGU_W), _smem(),
                  _const2((1, SGU_W)), _const2((1, SGU_W)), _const2((N_GRP, BLK, BLK)), _const2((BLK, N_GRP))]
        + [_vmem()] * n,
        out_specs=[cur(D_MODEL)] + [_hbm()] * n,
        out_shape=[jax.ShapeDtypeStruct((s_len, D_MODEL), _MXU)] + _Gather.out_shapes(shards),
        scratch_shapes=_Gather.scratch(shards),
        compiler_params=_params(48),
    )(q, k, k, v, v, su, sv, sinks, sg, sb, sgu_w, sgu_bt, *shards)


def _outproj(mc, w_out, b_out, x, g0, b0, shards):
    s_len = x.shape[0]
    tm = _tile(s_len, 512)
    n = len(shards)

    def body(mc_ref, w_ref, bo_ref, x_ref, g_ref, b_ref, *rest):
        r1_ref = rest[n]
        gathered = rest[n + 1:2 * n + 1]
        gather = _Gather(rest[:n], rest[2 * n + 1:3 * n + 1], rest[3 * n + 1], rest[3 * n + 2])
        flush_sems = rest[3 * n + 3]
        i = pl.program_id(0)

        @pl.when(i == 0)
        def _():
            gather.start()

        h0, _, _ = _ln(x_ref[...], g_ref[...], b_ref[...])
        r1_ref[...] = ALPHA * h0 + (_dot(mc_ref[...], w_ref[...]) + bo_ref[...])

        @pl.when(i == pl.num_programs(0) - 1)
        def _():
            gather.finish()
            gather.flush(gathered, flush_sems)

    return pl.pallas_call(
        body, name="outproj", grid=(s_len // tm,),
        in_specs=[_rows(tm, D_MODEL), _vmem(), _const2((1, D_MODEL)), _rows(tm, D_MODEL),
                  _const2((1, D_MODEL)), _const2((1, D_MODEL))] + [_vmem()] * n,
        out_specs=[_rows(tm, D_MODEL)] + [_hbm()] * n,
        out_shape=[jax.ShapeDtypeStruct((s_len, D_MODEL), F32)] + _Gather.out_shapes(shards),
        scratch_shapes=_Gather.scratch(shards),
        compiler_params=_params(40),
    )(mc, w_out, b_out, x, g0, b0, *shards)


def _ffn_spec(tm):
    return pl.BlockSpec((N_CHIP, tm, FF_SH), lambda i: (0, i, 0))


def _ffn_up(r1, g1, b1, wg, wu, shards):
    s_len = r1.shape[0]
    tm = _tile(s_len, 512)
    n = len(shards)

    def body(r1_ref, g_ref, b_ref, wg_ref, wu_ref, *rest):
        go_ref, uo_ref = rest[n:n + 2]
        gathered = rest[n + 2:2 * n + 2]
        gather = _Gather(rest[:n], rest[2 * n + 2:3 * n + 2], rest[3 * n + 2], rest[3 * n + 3])
        flush_sems = rest[3 * n + 4]
        i = pl.program_id(0)

        @pl.when(i == 0)
        def _():
            gather.start()

        h1, _, _ = _ln(r1_ref[...], g_ref[...], b_ref[...])
        h1b = h1.astype(_MXU)
        for j in range(N_CHIP):
            go_ref[j] = _dot_nt(h1b, wg_ref[j]).astype(_ACT)
            uo_ref[j] = _dot_nt(h1b, wu_ref[j]).astype(_ACT)

        @pl.when(i == pl.num_programs(0) - 1)
        def _():
            gather.finish()
            gather.flush(gathered, flush_sems)

    sd = jax.ShapeDtypeStruct((N_CHIP, s_len, FF_SH), _ACT)
    return pl.pallas_call(
        body, name="ffn_up", grid=(s_len // tm,),
        in_specs=[_rows(tm, D_MODEL), _const2((1, D_MODEL)), _const2((1, D_MODEL)), _vmem(), _vmem()] + [_vmem()] * n,
        out_specs=[_ffn_spec(tm), _ffn_spec(tm)] + [_hbm()] * n,
        out_shape=[sd, sd] + _Gather.out_shapes(shards),
        scratch_shapes=_Gather.scratch(shards),
        compiler_params=_params(56),
    )(r1, g1, b1, wg, wu, *shards)


def _silu_parts(g):
    sg = 1.0 / (1.0 + jnp.exp(-g))
    return g * sg, sg


def _ffn_down_loss(gact, uact, wd, r1, g1, b1, g2, b2, target):
    s_len = r1.shape[0]
    tm = _tile(s_len, 512)

    def body(g_ref, u_ref, wd_ref, r1_ref, g1_ref, b1_ref, g2_ref, b2_ref, t_ref,
             dr2_ref, loss_ref, dg2_ref, db2_ref):
        i = pl.program_id(0)
        f = jnp.zeros((tm, D_MODEL), F32)
        for j in range(N_CHIP):
            silu, _ = _silu_parts(g_ref[j].astype(F32))
            f = f + _dot((silu * u_ref[j].astype(F32)).astype(_MXU), wd_ref[j])
        h1, _, _ = _ln(r1_ref[...], g1_ref[...], b1_ref[...])
        h2, r2hat, rstd2 = _ln(ALPHA * h1 + f, g2_ref[...], b2_ref[...])
        diff = h2 - t_ref[...]
        dh2 = diff * (1.0 / D_MODEL)

        @pl.when(i == 0)
        def _():
            loss_ref[...] = jnp.zeros_like(loss_ref)
            dg2_ref[...] = jnp.zeros_like(dg2_ref)
            db2_ref[...] = jnp.zeros_like(db2_ref)

        loss_ref[...] += _colsum(diff * diff)
        dg2_ref[...] += _colsum(dh2 * r2hat)
        db2_ref[...] += _colsum(dh2)
        dr2_ref[...] = _ln_bwd(dh2, r2hat, rstd2, g2_ref[...])

    vec = jax.ShapeDtypeStruct((1, D_MODEL), F32)
    c = _const2((1, D_MODEL))
    return pl.pallas_call(
        body, name="ffn_down_loss", grid=(s_len // tm,),
        in_specs=[_ffn_spec(tm), _ffn_spec(tm), _vmem(), _rows(tm, D_MODEL), c, c, c, c, _rows(tm, D_MODEL)],
        out_specs=[_rows(tm, D_MODEL), c, c, c],
        out_shape=[jax.ShapeDtypeStruct((s_len, D_MODEL), F32), vec, vec, vec],
        compiler_params=_params(48),
    )(gact, uact, wd, r1, g1, b1, g2, b2, target)


def _ffn_bwd_a(dr2, gact, uact, wd):
    s_len = dr2.shape[0]
    tm = _tile(s_len, 512)

    def body(dr2_ref, g_ref, u_ref, wd_ref, dg_ref, du_ref, dwd_ref, wire_ref, land_ref, send_sem, recv_sem):
        i = pl.program_id(0)

        @pl.when(i == 0)
        def _():
            dwd_ref[...] = jnp.zeros_like(dwd_ref)

        dfb = dr2_ref[...].astype(_MXU)
        for j in range(N_CHIP):
            g = g_ref[j].astype(F32)
            u = u_ref[j].astype(F32)
            silu, sg = _silu_parts(g)
            da = _dot_nt(dfb, wd_ref[j])
            dg_ref[j] = (da * u * (sg * (1.0 + g * (1.0 - sg)))).astype(_MXU)
            du_ref[j] = (da * silu).astype(_MXU)
            dwd_ref[j * FF_SH:(j + 1) * FF_SH, :] += _dot_tn((silu * u).astype(_MXU), dfb)

        @pl.when(i == pl.num_programs(0) - 1)
        def _():
            _pair_reduce(dwd_ref, land_ref, wire_ref, send_sem, recv_sem)

    sd = jax.ShapeDtypeStruct((N_CHIP, s_len, FF_SH), _MXU)
    return pl.pallas_call(
        body, name="ffn_bwd_a", grid=(s_len // tm,),
        in_specs=[_rows(tm, D_MODEL), _ffn_spec(tm), _ffn_spec(tm), _vmem()],
        out_specs=[_ffn_spec(tm), _ffn_spec(tm), _vmem(), _vmem()],
        out_shape=[sd, sd, jax.ShapeDtypeStruct((D_FF, D_MODEL), F32),
                   jax.ShapeDtypeStruct((N_CHIP, FF_SH // 2, D_MODEL), _WIRE)],
        scratch_shapes=_pair_scratch((N_CHIP, FF_SH // 2, D_MODEL)),
        compiler_params=_params(58),
    )(dr2, gact, uact, wd)


def _ffn_bwd_g(dr2, dg, r1, g1, b1, wg, prev_wire):
    s_len = dr2.shape[0]
    tm = _tile(s_len, 512)

    def body(dr2_ref, dg_ref, r1_ref, g1_ref, b1_ref, wg_ref, pw_ref, dh1_ref, dwg_ref, wire_ref, pl_ref,
             land_ref, send_sem, recv_sem, xl_ref, x_send, x_recv, x_flush):
        i = pl.program_id(0)
        exchange = _ChipExchange(pw_ref, xl_ref, x_send, x_recv)

        @pl.when(i == 0)
        def _():
            exchange.start()
            dwg_ref[...] = jnp.zeros_like(dwg_ref)

        h1, _, _ = _ln(r1_ref[...], g1_ref[...], b1_ref[...])
        h1b = h1.astype(_MXU)
        dh1 = ALPHA * dr2_ref[...]
        for j in range(N_CHIP):
            dgj = dg_ref[j]
            dh1 = dh1 + _dot(dgj, wg_ref[j])
            dwg_ref[j * FF_SH:(j + 1) * FF_SH, :] += _dot_tn(dgj, h1b)
        dh1_ref[...] = dh1

        @pl.when(i == pl.num_programs(0) - 1)
        def _():
            _pair_reduce(dwg_ref, land_ref, wire_ref, send_sem, recv_sem)
            exchange.finish_to(pl_ref, x_flush)

    c = _const2((1, D_MODEL))
    return pl.pallas_call(
        body, name="ffn_bwd_g", grid=(s_len // tm,),
        in_specs=[_rows(tm, D_MODEL), _ffn_spec(tm), _rows(tm, D_MODEL), c, c, _vmem(), _vmem()],
        out_specs=[_rows(tm, D_MODEL), _vmem(), _vmem(), _hbm()],
        out_shape=[jax.ShapeDtypeStruct((s_len, D_MODEL), F32), jax.ShapeDtypeStruct((D_FF, D_MODEL), F32),
                   jax.ShapeDtypeStruct((N_CHIP, FF_SH // 2, D_MODEL), _WIRE), _ChipExchange.land_shape(prev_wire)],
        scratch_shapes=_pair_scratch((N_CHIP, FF_SH // 2, D_MODEL)) + _ChipExchange.scratch(prev_wire),
        compiler_params=_params(58),
    )(dr2, dg, r1, g1, b1, wg, prev_wire)


def _ffn_bwd_u(dh1a, du, r1, g1, b1, wu, prev_wire):
    s_len = dh1a.shape[0]
    tm = _tile(s_len, 512)

    def body(dh1_ref, du_ref, r1_ref, g1_ref, b1_ref, wu_ref, pw_ref,
             dr1_ref, dwu_ref, wire_ref, dg1_ref, db1_ref, pl_ref,
             land_ref, send_sem, recv_sem, xl_ref, x_send, x_recv, x_flush):
        i = pl.program_id(0)
        exchange = _ChipExchange(pw_ref, xl_ref, x_send, x_recv)

        @pl.when(i == 0)
        def _():
            exchange.start()
            dwu_ref[...] = jnp.zeros_like(dwu_ref)
            dg1_ref[...] = jnp.zeros_like(dg1_ref)
            db1_ref[...] = jnp.zeros_like(db1_ref)

        h1, r1hat, rstd1 = _ln(r1_ref[...], g1_ref[...], b1_ref[...])
        h1b = h1.astype(_MXU)
        dh1 = dh1_ref[...]
        for j in range(N_CHIP):
            duj = du_ref[j]
            dh1 = dh1 + _dot(duj, wu_ref[j])
            dwu_ref[j * FF_SH:(j + 1) * FF_SH, :] += _dot_tn(duj, h1b)
        dg1_ref[...] += _colsum(dh1 * r1hat)
        db1_ref[...] += _colsum(dh1)
        dr1_ref[...] = _ln_bwd(dh1, r1hat, rstd1, g1_ref[...])

        @pl.when(i == pl.num_programs(0) - 1)
        def _():
            _pair_reduce(dwu_ref, land_ref, wire_ref, send_sem, recv_sem)
            exchange.finish_to(pl_ref, x_flush)

    vec = jax.ShapeDtypeStruct((1, D_MODEL), F32)
    c = _const2((1, D_MODEL))
    return pl.pallas_call(
        body, name="ffn_bwd_u", grid=(s_len // tm,),
        in_specs=[_rows(tm, D_MODEL), _ffn_spec(tm), _rows(tm, D_MODEL), c, c, _vmem(), _vmem()],
        out_specs=[_rows(tm, D_MODEL), _vmem(), _vmem(), c, c, _hbm()],
        out_shape=[jax.ShapeDtypeStruct((s_len, D_MODEL), F32), jax.ShapeDtypeStruct((D_FF, D_MODEL), F32),
                   jax.ShapeDtypeStruct((N_CHIP, FF_SH // 2, D_MODEL), _WIRE), vec, vec,
                   _ChipExchange.land_shape(prev_wire)],
        scratch_shapes=_pair_scratch((N_CHIP, FF_SH // 2, D_MODEL)) + _ChipExchange.scratch(prev_wire),
        compiler_params=_params(58),
    )(dh1a, du, r1, g1, b1, wu, prev_wire)


def _outproj_bwd(dr1, mc, w_out, prev_wire):
    s_len = dr1.shape[0]
    tm = _tile(s_len, 512)

    def body(dr1_ref, mc_ref, w_ref, pw_ref, dmc_ref, dw_ref, wire_ref, db_ref, pl_ref,
             land_ref, send_sem, recv_sem, xl_ref, x_send, x_recv, x_flush):
        i = pl.program_id(0)
        exchange = _ChipExchange(pw_ref, xl_ref, x_send, x_recv)

        @pl.when(i == 0)
        def _():
            exchange.start()
            dw_ref[...] = jnp.zeros_like(dw_ref)
            db_ref[...] = jnp.zeros_like(db_ref)

        d = dr1_ref[...]
        db_ref[...] += _colsum(d)
        db16 = d.astype(_MXU)
        dmc_ref[...] = _dot_nt(db16, w_ref[...])
        dw_ref[...] += _dot_tn(mc_ref[...], db16)

        @pl.when(i == pl.num_programs(0) - 1)
        def _():
            _pair_reduce(dw_ref, land_ref, wire_ref, send_sem, recv_sem)
            exchange.finish_to(pl_ref, x_flush)

    return pl.pallas_call(
        body, name="outproj_bwd", grid=(s_len // tm,),
        in_specs=[_rows(tm, D_MODEL), _rows(tm, D_MODEL), _vmem(), _vmem()],
        out_specs=[_rows(tm, D_MODEL), _vmem(), _vmem(), _const2((1, D_MODEL)), _hbm()],
        out_shape=[jax.ShapeDtypeStruct((s_len, D_MODEL), F32), jax.ShapeDtypeStruct((D_MODEL, D_MODEL), F32),
                   jax.ShapeDtypeStruct((N_CHIP, OUT_SH // 2, D_MODEL), _WIRE), jax.ShapeDtypeStruct((1, D_MODEL), F32),
                   _ChipExchange.land_shape(prev_wire)],
        scratch_shapes=_pair_scratch((N_CHIP, OUT_SH // 2, D_MODEL)) + _ChipExchange.scratch(prev_wire),
        compiler_params=_params(48),
    )(dr1, mc, w_out, prev_wire)


def _mixer_bwd(q, k, v, su, sv, dmc, tc, t1, t2, sinks, sg, sb, sgu_w, sgu_bt, prev_wire):
    s_len = q.shape[0]
    nb = s_len // BLK

    def body(q_ref, kc_ref, kp_ref, vc_ref, vp_ref, su_ref, sv_ref, dmc_ref,
             tc_ref, t1_ref, t2_ref, tcp_ref, t1p_ref, t2p_ref,
             sink_ref, lg_ref, lb_ref, w_ref, bt_ref, pw_ref,
             dq_ref, dkv_ref, dsuv_ref, dbq_ref, dbkv_ref, dbsuv_ref,
             dsink_ref, dlg_ref, dlb_ref, dw_ref, dbt_ref, pl_ref, carry_ref, xl_ref, x_send, x_recv, x_flush):
        i = pl.program_id(0)
        exchange = _ChipExchange(pw_ref, xl_ref, x_send, x_recv)

        @pl.when(i == 0)
        def _():
            exchange.start()

        @pl.when(i == 0)
        def _():
            for r in (dbq_ref, dbkv_ref, dbsuv_ref, dsink_ref, dlg_ref, dlb_ref, dw_ref, dbt_ref):
                r[...] = jnp.zeros_like(r)

        def emit_kv(fin):
            dk = _rope_bwd(fin[:, 0:KV_W], tcp_ref[...], t1p_ref[...], t2p_ref[...])
            out = jnp.concatenate([dk, fin[:, KV_W:2 * KV_W]], axis=1)
            dkv_ref[...] = out.astype(_MXU)
            dbkv_ref[...] += _colsum(out)

        @pl.when(i < nb)
        def _():
            allowed = _band_mask(i == 0)
            kb = jnp.concatenate([kp_ref[...], kc_ref[...]], axis=0)
            vb = jnp.concatenate([vp_ref[...], vc_ref[...]], axis=0)
            qv = q_ref[...]
            dmc = dmc_ref[...]
            dqs, dks, dvs, dsinks = [], [], [], []
            for g in range(N_KV):
                kh = kb[:, g * HEAD_DIM:(g + 1) * HEAD_DIM]
                vh = vb[:, g * HEAD_DIM:(g + 1) * HEAD_DIM]
                dk_g = jnp.zeros((2 * BLK, HEAD_DIM), F32)
                dv_g = jnp.zeros((2 * BLK, HEAD_DIM), F32)
                for hh in range(Q_PER_KV):
                    h = g * Q_PER_KV + hh
                    qh = qv[:, h * HEAD_DIM:(h + 1) * HEAD_DIM]
                    probs, psink = _attn_probs(qh, kh, sink_ref[h], allowed)
                    pb = probs.astype(_MXU)
                    dob = dmc[:, h * HEAD_DIM:(h + 1) * HEAD_DIM].astype(_MXU)
                    dv_g = dv_g + _dot_tn(pb, dob)
                    dp = _dot_nt(dob, vh)
                    rd = jnp.sum(probs * dp, axis=-1, keepdims=True)
                    dsb = (probs * (dp - rd)).astype(_MXU)
                    dsinks.append(-jnp.sum(psink * rd, axis=0, keepdims=True))
                    dqs.append(_dot(dsb, kh))
                    dk_g = dk_g + _dot_tn(dsb, qh)
                dks.append(dk_g)
                dvs.append(dv_g)
            dq = _rope_bwd(jnp.concatenate(dqs, axis=1) * (HEAD_DIM ** -0.5), tc_ref[...], t1_ref[...], t2_ref[...])
            dq_ref[...] = dq.astype(_MXU)
            dbq_ref[...] += _colsum(dq)
            dsink_ref[...] += _lane_put(dsinks, 128)
            contrib = jnp.concatenate(dks + dvs, axis=1)

            @pl.when(i > 0)
            def _():
                emit_kv(carry_ref[...] + contrib[0:BLK, :])

            carry_ref[...] = contrib[BLK:2 * BLK, :]

            su = su_ref[...]
            sv = sv_ref[...]
            lg = lg_ref[...]
            u, vhat, rstd, vvb, wcs, mixed = _sgu_fwd(su, sv, lg, lb_ref[...], w_ref, bt_ref)
            dsgu = dmc[:, ATTN_W:D_MODEL]
            dsu = dsgu * mixed * _gelu_grad(su)
            dmixed = dsgu * u
            tri_t = lax.broadcasted_iota(jnp.int32, (BLK, BLK), 0)
            tri_s = lax.broadcasted_iota(jnp.int32, (BLK, BLK), 1)
            dvv, dbs = [], []
            for h in range(N_GRP):
                dm = dmixed[:, h * GRP_DIM:(h + 1) * GRP_DIM]
                dmb = dm.astype(_MXU)
                dbs.append(jnp.sum(dm, axis=1, keepdims=True))
                dw_ref[h] += jnp.where(tri_s <= tri_t, _dot_nt(dmb, vvb[:, h * GRP_DIM:(h + 1) * GRP_DIM]), 0.0)
                dvv.append(_dot_tn(wcs[h], dmb))
            dvv = jnp.concatenate(dvv, axis=1)
            dbt_ref[...] += _lane_put(dbs, 128)
            dlg_ref[...] += _colsum(dvv * vhat)
            dlb_ref[...] += _colsum(dvv)
            dsv = _ln_bwd(dvv, vhat, rstd, lg) * _gelu_grad(sv)
            dsuv = jnp.concatenate([dsu, dsv], axis=1)
            dsuv_ref[...] = dsuv.astype(_MXU)
            dbsuv_ref[...] += _colsum(dsuv)

        @pl.when(i == nb)
        def _():
            emit_kv(carry_ref[...])
            exchange.finish_to(pl_ref, x_flush)

    last = nb - 1
    cur = lambda w: pl.BlockSpec((BLK, w), lambda i: (jnp.minimum(i, last), 0))
    prev = lambda w: pl.BlockSpec((BLK, w), lambda i: (jnp.clip(i - 1, 0, last), 0))
    sd = jax.ShapeDtypeStruct
    return pl.pallas_call(
        body, name="mixer_bwd", grid=(nb + 1,),
        in_specs=[cur(ATTN_W), cur(KV_W), prev(KV_W), cur(KV_W), prev(KV_W), cur(SGU_W), cur(SGU_W), cur(D_MODEL),
                  cur(128), cur(128), cur(128), prev(128), prev(128), prev(128),
                  _smem(), _const2((1, SGU_W)), _const2((1, SGU_W)), _const2((N_GRP, BLK, BLK)), _const2((BLK, N_GRP)),
                  _vmem()],
        out_specs=[cur(ATTN_W), prev(2 * KV_W), cur(2 * SGU_W),
                   _const2((1, ATTN_W)), _const2((1, 2 * KV_W)), _const2((1, 2 * SGU_W)),
                   _const2((1, 128)), _const2((1, SGU_W)), _const2((1, SGU_W)),
                   _const2((N_GRP, BLK, BLK)), _const2((BLK, 128)), _hbm()],
        out_shape=[sd((s_len, ATTN_W), _MXU), sd((s_len, 2 * KV_W), _MXU), sd((s_len, 2 * SGU_W), _MXU),
                   sd((1, ATTN_W), F32), sd((1, 2 * KV_W), F32), sd((1, 2 * SGU_W), F32),
                   sd((1, 128), F32), sd((1, SGU_W), F32), sd((1, SGU_W), F32),
                   sd((N_GRP, BLK, BLK), F32), sd((BLK, 128), F32), _ChipExchange.land_shape(prev_wire)],
        scratch_shapes=[pltpu.VMEM((BLK, 2 * KV_W), F32)] + _ChipExchange.scratch(prev_wire),
        compiler_params=_params(32),
    )(q, k, k, v, v, su, sv, dmc, tc, t1, t2, tc, t1, t2, sinks, sg, sb, sgu_w, sgu_bt, prev_wire)


def _inproj_bwd(dq, dkv, dsuv, dr1, x, g0, b0, w_in):
    s_len = x.shape[0]
    tm = _tile(s_len, 512)
    cuts = ((0, ATTN_W), (ATTN_W, ATTN_W + 2 * KV_W), (ATTN_W + 2 * KV_W, IN_W))

    def body(dq_ref, dkv_ref, dsuv_ref, dr1_ref, x_ref, g_ref, b_ref, w_ref,
             dx_ref, dw_ref, wire_ref, dg_ref, db_ref, land_ref, send_sem, recv_sem):
        i = pl.program_id(0)

        @pl.when(i == 0)
        def _():
            dw_ref[...] = jnp.zeros_like(dw_ref)
            dg_ref[...] = jnp.zeros_like(dg_ref)
            db_ref[...] = jnp.zeros_like(db_ref)

        h0, xhat, rstd = _ln(x_ref[...], g_ref[...], b_ref[...])
        h0b = h0.astype(_MXU)
        dh0 = ALPHA * dr1_ref[...]
        for (lo, hi), d_ref in zip(cuts, (dq_ref, dkv_ref, dsuv_ref)):
            d = d_ref[...]
            dh0 = dh0 + _dot(d, w_ref[lo:hi, :])
            dw_ref[lo:hi, :] += _dot_tn(d, h0b)
        dg_ref[...] += _colsum(dh0 * xhat)
        db_ref[...] += _colsum(dh0)
        dx_ref[...] = _ln_bwd(dh0, xhat, rstd, g_ref[...])

        @pl.when(i == pl.num_programs(0) - 1)
        def _():
            _pair_reduce(dw_ref, land_ref, wire_ref, send_sem, recv_sem)

    vec = jax.ShapeDtypeStruct((1, D_MODEL), F32)
    c = _const2((1, D_MODEL))
    return pl.pallas_call(
        body, name="inproj_bwd", grid=(s_len // tm,),
        in_specs=[_rows(tm, ATTN_W), _rows(tm, 2 * KV_W), _rows(tm, 2 * SGU_W), _rows(tm, D_MODEL), _rows(tm, D_MODEL),
                  c, c, _vmem()],
        out_specs=[_rows(tm, D_MODEL), _vmem(), _vmem(), c, c],
        out_shape=[jax.ShapeDtypeStruct((s_len, D_MODEL), F32), jax.ShapeDtypeStruct((IN_W, D_MODEL), F32),
                   jax.ShapeDtypeStruct((N_CHIP, IN_SH // 2, D_MODEL), _WIRE), vec, vec],
        scratch_shapes=_pair_scratch((N_CHIP, IN_SH // 2, D_MODEL)),
        compiler_params=_params(56),
    )(dq, dkv, dsuv, dr1, x, g0, b0, w_in)


def _place():
    x, y, c = (lax.axis_index(a) for a in MESH_AXES)
    chips = [(1 - x, y), (x, 1 - y), (1 - x, 1 - y)]
    return x, y, c, chips


class _Gather:
    def __init__(self, ins, outs, send_sems, recv_sems):
        self.ins, self.outs, self.send_sems, self.recv_sems = ins, outs, send_sems, recv_sems
        self.n = len(ins)
        self.halves = [r.shape[0] // 2 for r in ins]

    def _copy(self, k, t, slot, half, to):
        rows = pl.ds(pl.multiple_of(half * self.halves[t], 16), self.halves[t])
        piece = self.outs[t].at[slot, rows, :]
        return pltpu.make_async_remote_copy(src_ref=piece, dst_ref=piece, send_sem=self.send_sems.at[k],
                                            recv_sem=self.recv_sems.at[k], device_id=to, device_id_type=MESH)

    def _chip_copy(self, t, d, slot):
        x, y, c, chips = _place()
        return self._copy(3 * t + d, t, slot, c, (chips[d][0], chips[d][1], c))

    def _pass_copy(self, t, d, half):
        x, y, c, chips = _place()
        return self._copy(3 * self.n + 3 * t + d, t, 2 * chips[d][0] + chips[d][1], half, (x, y, 1 - c))

    def start(self):
        x, y, c, chips = _place()
        me = 2 * x + y
        for t in range(self.n):
            self.outs[t][me] = self.ins[t][...].astype(_WIRE)
        for t in range(self.n):
            for d in range(3):
                self._chip_copy(t, d, me).start()

    def finish(self):
        x, y, c, chips = _place()
        me = 2 * x + y
        for t in range(self.n):
            for d in range(3):
                self._chip_copy(t, d, 2 * chips[d][0] + chips[d][1]).wait_recv()
                self._pass_copy(t, d, c).start()
        for t in range(self.n):
            for d in range(3):
                self._pass_copy(t, d, 1 - c).wait_recv()
        for t in range(self.n):
            for d in range(3):
                self._chip_copy(t, d, me).wait_send()
                self._pass_copy(t, d, c).wait_send()

    def flush(self, hbm_outs, flush_sems):
        _flush(self.outs, hbm_outs, flush_sems)

    @staticmethod
    def out_shapes(shards):
        return [jax.ShapeDtypeStruct((N_CHIP,) + s.shape, _WIRE) for s in shards]

    @staticmethod
    def sems(n):
        return [pltpu.SemaphoreType.DMA((6 * n,)), pltpu.SemaphoreType.DMA((6 * n,))]

    @staticmethod
    def scratch(shards):
        n = len(shards)
        return ([pltpu.VMEM((N_CHIP,) + s.shape, _WIRE) for s in shards] + _Gather.sems(n)
                + [pltpu.SemaphoreType.DMA((n,))])


def _flush(bufs, hbm_outs, sems):
    copies = [pltpu.make_async_copy(b, o, sems.at[k]) for k, (b, o) in enumerate(zip(bufs, hbm_outs))]
    for cp in copies:
        cp.start()
    for cp in copies:
        cp.wait()


def _gather_weights(shards):
    n = len(shards)

    def body(*refs):
        gather = _Gather(refs[:n], refs[n:2 * n], refs[2 * n], refs[2 * n + 1])
        gather.start()
        gather.finish()

    return pl.pallas_call(
        body, name="gather_weights",
        in_specs=[_vmem()] * n, out_specs=[_vmem()] * n,
        out_shape=_Gather.out_shapes(shards), scratch_shapes=_Gather.sems(n),
        compiler_params=pltpu.CompilerParams(vmem_limit_bytes=32 * MIB),
    )(*shards)


class _ChipExchange:
    def __init__(self, wire_ref, land_ref, send_sems, recv_sems):
        self.wire, self.land, self.send_sems, self.recv_sems = wire_ref, land_ref, send_sems, recv_sems

    def _copy(self, d):
        x, y, c, chips = _place()
        return pltpu.make_async_remote_copy(
            src_ref=self.wire.at[2 * chips[d][0] + chips[d][1]], dst_ref=self.land.at[d],
            send_sem=self.send_sems.at[d], recv_sem=self.recv_sems.at[d],
            device_id=(chips[d][0], chips[d][1], c), device_id_type=MESH)

    def start(self):
        for d in range(3):
            self._copy(d).start()

    def wait_recv(self):
        for d in range(3):
            self._copy(d).wait_recv()

    def wait_send(self):
        for d in range(3):
            self._copy(d).wait_send()

    def finish_to(self, hbm_out, flush_sem):
        self.wait_recv()
        _flush([self.land], [hbm_out], flush_sem)
        self.wait_send()

    @staticmethod
    def land_shape(wire):
        return jax.ShapeDtypeStruct((3,) + wire.shape[1:], wire.dtype)

    @staticmethod
    def sems():
        return [pltpu.SemaphoreType.DMA((3,)), pltpu.SemaphoreType.DMA((3,))]

    @staticmethod
    def scratch(wire):
        return ([pltpu.VMEM((3,) + wire.shape[1:], wire.dtype)] + _ChipExchange.sems() + [pltpu.SemaphoreType.DMA((1,))])


def _pair_scratch(half_shape):
    return [pltpu.VMEM(half_shape, F32), pltpu.SemaphoreType.DMA((N_CHIP,)), pltpu.SemaphoreType.DMA((N_CHIP,))]


def _pair_reduce(acc_ref, land_ref, wire_ref, send_sems, recv_sems):
    rh = land_ref.shape[1]
    x, y, c, _ = _place()
    copies = []
    for j in range(N_CHIP):
        give = acc_ref.at[pl.ds(pl.multiple_of(j * 2 * rh + (1 - c) * rh, 8), rh), :]
        cp = pltpu.make_async_remote_copy(src_ref=give, dst_ref=land_ref.at[j], send_sem=send_sems.at[j],
                                          recv_sem=recv_sems.at[j], device_id=(x, y, 1 - c), device_id_type=MESH)
        cp.start()
        copies.append(cp)
    for cp in copies:
        cp.wait()

    def chunk(r, carry):
        theirs = pl.ds(pl.multiple_of(r * ROW_CHUNK, ROW_CHUNK), ROW_CHUNK)
        for j in range(N_CHIP):
            mine = pl.ds(pl.multiple_of(j * 2 * rh + c * rh + r * ROW_CHUNK, 8), ROW_CHUNK)
            s = acc_ref[mine, :] + land_ref[j, theirs, :]
            acc_ref[mine, :] = s
            wire_ref[j, theirs, :] = s.astype(_WIRE)
        return carry

    lax.fori_loop(0, rh // ROW_CHUNK, chunk, 0)


def _grad_finish(last_wire, lands, accs):
    n = len(accs)
    halves = [last_wire.shape[1]] + [w.shape[1] for w in lands]
    widths = [a.shape[1] for a in accs]

    def body(*refs):
        wire0, land, acc, g = refs[0], (None,) + refs[1:n], refs[n:2 * n], refs[2 * n:3 * n]
        land0, own = refs[3 * n], refs[3 * n + 1:4 * n + 1]
        x_send, x_recv, pair_send, pair_recv, local_sems = refs[4 * n + 1:4 * n + 6]
        land = (land0,) + land[1:]
        x, y, c, chips = _place()
        me = 2 * x + y
        exchange = _ChipExchange(wire0, land0, x_send, x_recv)

        def half_rows(t, half):
            return pl.ds(pl.multiple_of(half * halves[t], 8), halves[t])

        def own_copy(t):
            rows = pl.ds(pl.multiple_of((2 * me + c) * halves[t], 8), halves[t])
            return pltpu.make_async_copy(acc[t].at[rows, :], own[t], local_sems.at[t])

        def pair_copy(t, half):
            rows = g[t].at[half_rows(t, half), :]
            return pltpu.make_async_remote_copy(src_ref=rows, dst_ref=rows, send_sem=pair_send.at[t],
                                                recv_sem=pair_recv.at[t], device_id=(x, y, 1 - c), device_id_type=MESH)

        exchange.start()
        for t in range(n):
            own_copy(t).start()
        for t in list(range(1, n)) + [0]:
            own_copy(t).wait()
            if t == 0:
                exchange.wait_recv()

            def chunk(r, carry, t=t):
                src = pl.ds(pl.multiple_of(r * ROW_CHUNK, ROW_CHUNK), ROW_CHUNK)
                dst = pl.ds(pl.multiple_of(c * halves[t] + r * ROW_CHUNK, 8), ROW_CHUNK)
                s = own[t][src, :]
                for d in range(3):
                    s = s + land[t][d, src, :].astype(F32)
                g[t][dst, :] = s
                return carry

            lax.fori_loop(0, halves[t] // ROW_CHUNK, chunk, 0)
            pair_copy(t, c).start()
        for t in range(n):
            pair_copy(t, 1 - c).wait_recv()
        for t in range(n):
            pair_copy(t, c).wait_send()
        exchange.wait_send()

    return pl.pallas_call(
        body, name="grad_finish",
        in_specs=[_vmem()] * n + [_hbm()] * n, out_specs=[_vmem()] * n,
        out_shape=[jax.ShapeDtypeStruct((2 * h, w), F32) for h, w in zip(halves, widths)],
        scratch_shapes=[pltpu.VMEM((3,) + last_wire.shape[1:], last_wire.dtype)]
        + [pltpu.VMEM((h, w), F32) for h, w in zip(halves, widths)]
        + _ChipExchange.sems()
        + [pltpu.SemaphoreType.DMA((n,)), pltpu.SemaphoreType.DMA((n,)), pltpu.SemaphoreType.DMA((n,))],
        compiler_params=pltpu.CompilerParams(vmem_limit_bytes=56 * MIB),
    )(last_wire, *lands, *accs)


_SMALL = ("ln_in_g", "ln_in_b", "b_in", "attn_sinks", "sgu_ln_g", "sgu_ln_b", "sgu_w", "sgu_b", "b_out",
          "ln_mix_g", "ln_mix_b", "ln_ffn_g", "ln_ffn_b")
_VEC_ROW = dict(ln_in_g=0, ln_in_b=1, b_in=2, attn_sinks=4, sgu_ln_g=5, sgu_ln_b=6, b_out=7, ln_mix_g=8, ln_mix_b=9,
                ln_ffn_g=10, ln_ffn_b=11)
_LOSS_ROW = 12
_VEC_ROWS = 16
_MAT_ROWS = N_GRP * BLK + BLK


def _small_update(local, params):
    n_in = 16
    shapes = [params[nm][0].shape for nm in _SMALL]

    def body(*refs):
        (g_ln_in_g, g_ln_in_b, g_bq, g_bkv, g_bsuv, g_sink, g_sln_g, g_sln_b, g_sw, g_sbt, g_bout,
         g_lmg, g_lmb, g_lfg, g_lfb, g_loss) = refs[:n_in]
        prm = refs[n_in:n_in + 3 * len(_SMALL)]
        outs = refs[n_in + 3 * len(_SMALL):n_in + 7 * len(_SMALL) + 1]
        (buf_a, buf_b, pair_a, pair_b, stage_a, stage_b, tot_a, tot_b,
         p1_send, p1_recv, x_send, x_recv, p2_send, p2_recv) = refs[n_in + 7 * len(_SMALL) + 1:]
        x, y, c, chips = _place()
        me = 2 * x + y
        sibling = (x, y, 1 - c)
        half_a, half_b = _VEC_ROWS // 2, _MAT_ROWS // 2

        buf_a[...] = jnp.zeros_like(buf_a)
        for row, ref in ((0, g_ln_in_g), (1, g_ln_in_b), (7, g_bout), (8, g_lmg), (9, g_lmb), (10, g_lfg), (11, g_lfb),
                         (_LOSS_ROW, g_loss)):
            buf_a[row:row + 1, :] = ref[...]
        buf_a[2:3, 0:ATTN_W] = g_bq[...]
        buf_a[2:3, ATTN_W:ATTN_W + 2 * KV_W] = g_bkv[...]
        buf_a[2:3, ATTN_W + 2 * KV_W:D_MODEL] = g_bsuv[:, 0:2 * KV_W]
        buf_a[3:4, 0:2 * SGU_W - 2 * KV_W] = g_bsuv[:, 2 * KV_W:2 * SGU_W]
        buf_a[4:5, 0:128] = g_sink[...]
        buf_a[5:6, 0:SGU_W] = g_sln_g[...]
        buf_a[6:7, 0:SGU_W] = g_sln_b[...]
        for h in range(N_GRP):
            buf_b[h * BLK:(h + 1) * BLK, :] = g_sw[h]
        buf_b[N_GRP * BLK:_MAT_ROWS, :] = g_sbt[...]

        def remote(src, dst, send_sem, recv_sem, to):
            return pltpu.make_async_remote_copy(src_ref=src, dst_ref=dst, send_sem=send_sem, recv_sem=recv_sem,
                                                device_id=to, device_id_type=MESH)

        first = [remote(buf_a, pair_a, p1_send.at[0], p1_recv.at[0], sibling),
                 remote(buf_b, pair_b, p1_send.at[1], p1_recv.at[1], sibling)]
        for cp in first:
            cp.start()
        for cp in first:
            cp.wait()
        rows_a = pl.ds(pl.multiple_of(c * half_a, 8), half_a)
        rows_b = pl.ds(pl.multiple_of(c * half_b, 8), half_b)
        stage_a[me] = buf_a[rows_a, :] + pair_a[rows_a, :]
        stage_b[me] = buf_b[rows_b, :] + pair_b[rows_b, :]

        def chip_copies(d):
            to = (chips[d][0], chips[d][1], c)
            return [remote(stage_a.at[me], stage_a.at[me], x_send.at[2 * d], x_recv.at[2 * d], to),
                    remote(stage_b.at[me], stage_b.at[me], x_send.at[2 * d + 1], x_recv.at[2 * d + 1], to)]

        def chip_arrivals(d):
            slot = 2 * chips[d][0] + chips[d][1]
            to = (chips[d][0], chips[d][1], c)
            return [remote(stage_a.at[slot], stage_a.at[slot], x_send.at[2 * d], x_recv.at[2 * d], to),
                    remote(stage_b.at[slot], stage_b.at[slot], x_send.at[2 * d + 1], x_recv.at[2 * d + 1], to)]

        for d in range(3):
            for cp in chip_copies(d):
                cp.start()
        for d in range(3):
            for cp in chip_arrivals(d):
                cp.wait_recv()
        tot_a[rows_a, :] = ((stage_a[0] + stage_a[1]) + stage_a[2]) + stage_a[3]
        tot_b[rows_b, :] = ((stage_b[0] + stage_b[1]) + stage_b[2]) + stage_b[3]

        second = [remote(tot_a.at[rows_a, :], tot_a.at[rows_a, :], p2_send.at[0], p2_recv.at[0], sibling),
                  remote(tot_b.at[rows_b, :], tot_b.at[rows_b, :], p2_send.at[1], p2_recv.at[1], sibling)]
        for cp in second:
            cp.start()
        other_a = pl.ds(pl.multiple_of((1 - c) * half_a, 8), half_a)
        other_b = pl.ds(pl.multiple_of((1 - c) * half_b, 8), half_b)
        remote(tot_a.at[other_a, :], tot_a.at[other_a, :], p2_send.at[0], p2_recv.at[0], sibling).wait_recv()
        remote(tot_b.at[other_b, :], tot_b.at[other_b, :], p2_send.at[1], p2_recv.at[1], sibling).wait_recv()
        for cp in second:
            cp.wait_send()
        for d in range(3):
            for cp in chip_copies(d):
                cp.wait_send()

        def grad_of(k, name):
            if name == "sgu_w":
                return [tot_b[h * BLK:(h + 1) * BLK, :] for h in range(N_GRP)]
            if name == "sgu_b":
                return jnp.transpose(tot_b[N_GRP * BLK:_MAT_ROWS, :])[0:N_GRP, :]
            row = _VEC_ROW[name]
            if name == "b_in":
                return jnp.concatenate([tot_a[row:row + 1, :], tot_a[row + 1:row + 2, 0:IN_W - D_MODEL]], axis=1)
            return tot_a[row:row + 1, 0:shapes[k][-1]]

        for k, name in enumerate(_SMALL):
            w_ref, m_ref, v_ref = prm[3 * k:3 * k + 3]
            g_out, d_out, m_out, v_out = outs[4 * k:4 * k + 4]
            g = grad_of(k, name)
            if name == "sgu_w":
                for h in range(N_GRP):
                    d_, m_, v_ = _adamw_math(w_ref[h], g[h], m_ref[h], v_ref[h])
                    g_out[h], d_out[h], m_out[h], v_out[h] = g[h], d_, m_, v_
            else:
                d_, m_, v_ = _adamw_math(w_ref[...], g, m_ref[...], v_ref[...])
                g_out[...], d_out[...], m_out[...], v_out[...] = g, d_, m_, v_
        outs[-1][...] = tot_a[_LOSS_ROW:_LOSS_ROW + 1, :]

    ins = [local[k] for k in ("ln_in_g", "ln_in_b", "bq", "bkv", "bsuv", "sink", "sgu_ln_g", "sgu_ln_b", "sgu_w",
                              "sgu_bt", "b_out", "ln_mix_g", "ln_mix_b", "ln_ffn_g", "ln_ffn_b", "loss")]
    ins += [a for nm in _SMALL for a in params[nm]]
    out_shape = [jax.ShapeDtypeStruct(s, F32) for s in shapes for _ in range(4)] + [jax.ShapeDtypeStruct((1, D_MODEL), F32)]
    vec = pltpu.VMEM((_VEC_ROWS, D_MODEL), F32)
    mat = pltpu.VMEM((_MAT_ROWS, 128), F32)
    res = pl.pallas_call(
        body, name="small_update",
        in_specs=[_vmem()] * len(ins), out_specs=[_vmem()] * len(out_shape), out_shape=out_shape,
        scratch_shapes=[vec, mat, vec, mat, pltpu.VMEM((N_CHIP, _VEC_ROWS // 2, D_MODEL), F32),
                        pltpu.VMEM((N_CHIP, _MAT_ROWS // 2, 128), F32), vec, mat,
                        pltpu.SemaphoreType.DMA((2,)), pltpu.SemaphoreType.DMA((2,)), pltpu.SemaphoreType.DMA((6,)),
                        pltpu.SemaphoreType.DMA((6,)), pltpu.SemaphoreType.DMA((2,)), pltpu.SemaphoreType.DMA((2,))],
        compiler_params=pltpu.CompilerParams(vmem_limit_bytes=32 * MIB),
    )(*ins)
    return {nm: tuple(res[4 * k:4 * k + 4]) for k, nm in enumerate(_SMALL)}, res[-1]


def _elementwise(name, fn, ins, out_dtypes, tile_rows=256):
    shape = ins[0].shape
    lead = shape[:-2]
    rows, cols = shape[-2:]
    tr = _tile(rows, tile_rows)
    n_lead = math.prod(lead)
    nr = rows // tr
    flat = [a.reshape((n_lead, rows, cols)) for a in ins]

    def body(*refs):
        outs = fn(*[r[0] for r in refs[:len(ins)]])
        for o_ref, o in zip(refs[len(ins):], outs):
            o_ref[0] = o.astype(o_ref.dtype)

    spec = pl.BlockSpec((1, tr, cols), lambda i: (i // nr, i % nr, 0))
    res = pl.pallas_call(
        body, name=name, grid=(n_lead * nr,),
        in_specs=[spec] * len(ins), out_specs=[spec] * len(out_dtypes),
        out_shape=[jax.ShapeDtypeStruct((n_lead, rows, cols), dt) for dt in out_dtypes],
        compiler_params=_params(32),
    )(*flat)
    return [r.reshape(shape) for r in res]


def _adamw_math(w, g, m, v):
    m = ADAM_B1 * m + (1.0 - ADAM_B1) * g
    v = ADAM_B2 * v + (1.0 - ADAM_B2) * (g * g)
    m_hat = m / (1.0 - ADAM_B1 ** ADAM_STEP)
    v_hat = v / (1.0 - ADAM_B2 ** ADAM_STEP)
    delta = -ADAM_LR * (m_hat / (jnp.sqrt(v_hat) + ADAM_EPS) + ADAM_WD * w)
    return delta, m, v


def _adamw(name, w, g, m, v, tile_rows=256):
    return _elementwise(name, _adamw_math, [w, g, m, v], [F32, F32, F32], tile_rows)


def kernel(x, positions, ln_in_g, ln_in_b, w_in, b_in, attn_sinks, sgu_ln_g, sgu_ln_b, sgu_w, sgu_b, w_out, b_out, ln_mix_g, ln_mix_b, w_gate, w_up, w_down, ln_ffn_g, ln_ffn_b, loss_target, m_ln_in_g, m_ln_in_b, m_w_in, m_b_in, m_attn_sinks, m_sgu_ln_g, m_sgu_ln_b, m_sgu_w, m_sgu_b, m_w_out, m_b_out, m_ln_mix_g, m_ln_mix_b, m_w_gate, m_w_up, m_w_down, m_ln_ffn_g, m_ln_ffn_b, v_ln_in_g, v_ln_in_b, v_w_in, v_b_in, v_attn_sinks, v_sgu_ln_g, v_sgu_ln_b, v_sgu_w, v_sgu_b, v_w_out, v_b_out, v_ln_mix_g, v_ln_mix_b, v_w_gate, v_w_up, v_w_down, v_ln_ffn_g, v_ln_ffn_b):
    weights = dict(ln_in_g=ln_in_g, ln_in_b=ln_in_b, w_in=w_in, b_in=b_in, attn_sinks=attn_sinks, sgu_ln_g=sgu_ln_g,
                   sgu_ln_b=sgu_ln_b, sgu_w=sgu_w, sgu_b=sgu_b, w_out=w_out, b_out=b_out, ln_mix_g=ln_mix_g,
                   ln_mix_b=ln_mix_b, w_gate=w_gate, w_up=w_up, w_down=w_down, ln_ffn_g=ln_ffn_g, ln_ffn_b=ln_ffn_b)
    mom_m = dict(ln_in_g=m_ln_in_g, ln_in_b=m_ln_in_b, w_in=m_w_in, b_in=m_b_in, attn_sinks=m_attn_sinks,
                 sgu_ln_g=m_sgu_ln_g, sgu_ln_b=m_sgu_ln_b, sgu_w=m_sgu_w, sgu_b=m_sgu_b, w_out=m_w_out, b_out=m_b_out,
                 ln_mix_g=m_ln_mix_g, ln_mix_b=m_ln_mix_b, w_gate=m_w_gate, w_up=m_w_up, w_down=m_w_down,
                 ln_ffn_g=m_ln_ffn_g, ln_ffn_b=m_ln_ffn_b)
    mom_v = dict(ln_in_g=v_ln_in_g, ln_in_b=v_ln_in_b, w_in=v_w_in, b_in=v_b_in, attn_sinks=v_attn_sinks,
                 sgu_ln_g=v_sgu_ln_g, sgu_ln_b=v_sgu_ln_b, sgu_w=v_sgu_w, sgu_b=v_sgu_b, w_out=v_w_out, b_out=v_b_out,
                 ln_mix_g=v_ln_mix_g, ln_mix_b=v_ln_mix_b, w_gate=v_w_gate, w_up=v_w_up, w_down=v_w_down,
                 ln_ffn_g=v_ln_ffn_g, ln_ffn_b=v_ln_ffn_b)
    order = list(weights)
    big = ("w_in", "w_out", "w_gate", "w_up", "w_down")

    s_len = x.shape[1]
    xs = x.reshape(s_len, D_MODEL)
    tgt = loss_target.reshape(s_len, D_MODEL)
    pos_col = positions.reshape(s_len, 1)
    g0, b0 = ln_in_g.reshape(1, D_MODEL), ln_in_b.reshape(1, D_MODEL)
    sinks = attn_sinks.reshape(N_Q)
    sgu_w3 = sgu_w.reshape(N_GRP, BLK, BLK)
    sgu_bt = sgu_b.reshape(N_GRP, BLK).T

    col_sharded = ("w_in", "w_gate", "w_up")

    def rowmajor(name, a):
        return jnp.swapaxes(a[0], 0, 1) if name in col_sharded else a[0]

    def as_given(name, a):
        return (jnp.swapaxes(a, 0, 1) if name in col_sharded else a)[None]

    shards = [rowmajor(n, weights[n]) for n in big]
    (gw_in,) = _gather_weights(shards[0:1])
    w_in_full = gw_in.reshape(IN_W, D_MODEL)

    q, k, v, su, sv, tc, t1, t2, gw_out = _ln_inproj(xs, pos_col, g0, b0, w_in_full, b_in, shards[1:2])
    mc, gw_gate = _mixer_fwd(q, k, v, su, sv, sinks, sgu_ln_g, sgu_ln_b, sgu_w3, sgu_bt, shards[2:3])
    w_out_full = gw_out.reshape(D_MODEL, D_MODEL)
    r1, gw_up = _outproj(mc, w_out_full, b_out, xs, g0, b0, shards[3:4])
    gact, uact, gw_down = _ffn_up(r1, ln_mix_g, ln_mix_b, gw_gate, gw_up, shards[4:5])
    dr2, loss_cols, d_ln_ffn_g, d_ln_ffn_b = _ffn_down_loss(gact, uact, gw_down, r1, ln_mix_g, ln_mix_b,
                                                            ln_ffn_g, ln_ffn_b, tgt)

    dg, du, acc_down, wire_down = _ffn_bwd_a(dr2, gact, uact, gw_down)
    dh1a, acc_gate, wire_gate, land_down = _ffn_bwd_g(dr2, dg, r1, ln_mix_g, ln_mix_b, gw_gate, wire_down)
    dr1, acc_up, wire_up, d_ln_mix_g, d_ln_mix_b, land_gate = _ffn_bwd_u(dh1a, du, r1, ln_mix_g, ln_mix_b, gw_up,
                                                                         wire_gate)
    dmc, acc_out, wire_out, d_b_out, land_up = _outproj_bwd(dr1, mc, w_out_full, wire_up)
    (dq, dkv, dsuv, dbq, dbkv, dbsuv, d_sink, d_sgu_ln_g, d_sgu_ln_b, d_sgu_w, d_sgu_bt, land_out) = _mixer_bwd(
        q, k, v, su, sv, dmc, tc, t1, t2, sinks, sgu_ln_g, sgu_ln_b, sgu_w3, sgu_bt, wire_out)
    grad_x, acc_in, wire_in, d_ln_in_g, d_ln_in_b = _inproj_bwd(dq, dkv, dsuv, dr1, xs, g0, b0, w_in_full)

    reduced = _grad_finish(wire_in, [land_out, land_gate, land_up, land_down],
                           [acc_in, acc_out, acc_gate, acc_up, acc_down])
    small_shape = dict(ln_in_g=(1, D_MODEL), ln_in_b=(1, D_MODEL), sgu_w=(N_GRP, BLK, BLK), sgu_b=(N_GRP, BLK))
    small_local = dict(
        ln_in_g=d_ln_in_g, ln_in_b=d_ln_in_b, bq=dbq, bkv=dbkv, bsuv=dbsuv, sink=d_sink, sgu_ln_g=d_sgu_ln_g,
        sgu_ln_b=d_sgu_ln_b, sgu_w=d_sgu_w, sgu_bt=d_sgu_bt, b_out=d_b_out, ln_mix_g=d_ln_mix_g, ln_mix_b=d_ln_mix_b,
        ln_ffn_g=d_ln_ffn_g, ln_ffn_b=d_ln_ffn_b, loss=loss_cols)
    small_params = {nm: tuple(src[nm].reshape(small_shape.get(nm, src[nm].shape)) for src in (weights, mom_m, mom_v))
                    for nm in _SMALL}
    small_out, loss_sum = _small_update(small_local, small_params)
    loss = jnp.sum(loss_sum) * (0.5 / D_MODEL)
    grads, delta, new_m, new_v = {}, {}, {}, {}
    for nm in _SMALL:
        grads[nm], delta[nm], new_m[nm], new_v[nm] = (a.reshape(weights[nm].shape) for a in small_out[nm])

    for t, name in enumerate(big):
        d_, m_, v_ = _adamw("adamw_" + name, shards[t], reduced[t], rowmajor(name, mom_m[name]),
                            rowmajor(name, mom_v[name]))
        grads[name] = as_given(name, reduced[t])
        delta[name], new_m[name], new_v[name] = as_given(name, d_), as_given(name, m_), as_given(name, v_)

    return (loss, grad_x.reshape(x.shape), *[grads[n] for n in order], *[delta[n] for n in order],
            *[new_m[n] for n in order], *[new_v[n] for n in order])
```

```python
import functools
import math

import jax
import jax.numpy as jnp
from jax import lax
from jax.experimental import pallas as pl
from jax.experimental.pallas import tpu as pltpu

F32 = jnp.float32
_MXU = jnp.bfloat16
_WIRE = jnp.bfloat16
_ACT = jnp.bfloat16

D_MODEL = 1024
ATTN_W = 512
SGU_W = 512
HEAD_DIM = 64
N_Q = 8
N_KV = 2
Q_PER_KV = 4
KV_W = 128
BLK = 128
ROT_DIM = 16
ROPE_THETA = 500000.0
N_GRP = 4
GRP_DIM = 128
D_FF = 2816
IN_W = 1792
LN_EPS = 1e-5
ALPHA = 2.0 ** 0.25
N_CHIP = 4
FF_SH = D_FF // N_CHIP
IN_SH = IN_W // N_CHIP
OUT_SH = D_MODEL // N_CHIP
ROW_CHUNK = 32

ADAM_LR = 0.001
ADAM_B1 = 0.9
ADAM_B2 = 0.999
ADAM_EPS = 1e-08
ADAM_WD = 0.01
ADAM_STEP = 10

SQRT_HALF = 0.7071067811865476
INV_SQRT_2PI = 0.3989422804014327
MESH_AXES = ("x", "y", "c")
MESH = pl.DeviceIdType.MESH
MIB = 2 ** 20


def _vmem():
    return pl.BlockSpec(memory_space=pltpu.VMEM)


def _smem():
    return pl.BlockSpec(memory_space=pltpu.SMEM)


def _hbm():
    return pl.BlockSpec(memory_space=pl.ANY)


def _params(vmem_mib=48):
    return pltpu.CompilerParams(dimension_semantics=("arbitrary",), vmem_limit_bytes=vmem_mib * MIB)


def _tile(n, cap):
    if n <= cap:
        return n
    for t in range(cap - cap % 16, 0, -16):
        if n % t == 0:
            return t
    raise ValueError((n, cap))


def _rows(tm, width):
    return pl.BlockSpec((tm, width), lambda i: (i, 0))


def _const2(shape):
    return pl.BlockSpec(shape, lambda i: (0,) * len(shape))


def _ln(x, g, b):
    mu = jnp.mean(x, axis=-1, keepdims=True)
    xc = x - mu
    var = jnp.mean(xc * xc, axis=-1, keepdims=True)
    rstd = lax.rsqrt(var + LN_EPS)
    xhat = xc * rstd
    return xhat * g + b, xhat, rstd


def _ln_bwd(dy, xhat, rstd, g):
    gdy = dy * g
    m1 = jnp.mean(gdy, axis=-1, keepdims=True)
    m2 = jnp.mean(gdy * xhat, axis=-1, keepdims=True)
    return rstd * (gdy - m1 - xhat * m2)


def _colsum(a):
    return jnp.sum(a, axis=0, keepdims=True)


def _gelu(x):
    return 0.5 * x * (1.0 + lax.erf(x * SQRT_HALF))


def _gelu_grad(x):
    return 0.5 * (1.0 + lax.erf(x * SQRT_HALF)) + x * jnp.exp(-0.5 * x * x) * INV_SQRT_2PI


def _dot(a, b):
    return jnp.dot(a, b, preferred_element_type=F32)


def _dot_nt(a, b):
    return lax.dot_general(a, b, (((1,), (1,)), ((), ())), preferred_element_type=F32)


def _dot_tn(a, b):
    return lax.dot_general(a, b, (((0,), (0,)), ((), ())), preferred_element_type=F32)


def _rope(t, tc, t1, t2):
    n = t.shape[1]
    rep = n // 128
    if rep > 1:
        tc, t1, t2 = (jnp.tile(a, (1, rep)) for a in (tc, t1, t2))
    return t * tc + pltpu.roll(t, n - 8, 1) * t1 + pltpu.roll(t, 8, 1) * t2


def _rope_bwd(d, tc, t1, t2):
    n = d.shape[1]
    rep = n // 128
    if rep > 1:
        tc, t1, t2 = (jnp.tile(a, (1, rep)) for a in (tc, t1, t2))
    return d * tc + pltpu.roll(d * t1, 8, 1) + pltpu.roll(d * t2, n - 8, 1)


def _band_mask(first_block):
    qi = lax.broadcasted_iota(jnp.int32, (BLK, 2 * BLK), 0)
    kj = lax.broadcasted_iota(jnp.int32, (BLK, 2 * BLK), 1)
    shut = jnp.where(first_block, 2 * BLK, 0)
    prev_ok = jnp.logical_and(kj < BLK, kj > qi + shut)
    cur_ok = jnp.logical_and(kj >= BLK, (kj - BLK) <= qi)
    return jnp.logical_or(prev_ok, cur_ok)


def _causal_w(w_ref, h):
    t = lax.broadcasted_iota(jnp.int32, (BLK, BLK), 0)
    s = lax.broadcasted_iota(jnp.int32, (BLK, BLK), 1)
    return jnp.where(s <= t, w_ref[h], 0.0)


def _lane_put(vals, width):
    rows = vals[0].shape[0]
    lane = lax.broadcasted_iota(jnp.int32, (rows, width), 1)
    out = jnp.zeros((rows, width), F32)
    for k, v in enumerate(vals):
        out = out + jnp.where(lane == k, v, 0.0)
    return out


def _rope_consts():
    lane = jnp.arange(128) % HEAD_DIM
    inv_freq = ROPE_THETA ** (-jnp.arange(0, ROT_DIM, 2, dtype=F32) / ROT_DIM)
    rot = lane < ROT_DIM
    freq = jnp.where(rot, inv_freq[lane % (ROT_DIM // 2)], 0.0)
    rows = [freq, rot.astype(F32), 1.0 - rot.astype(F32), (lane < ROT_DIM // 2).astype(F32),
            jnp.logical_and(lane >= ROT_DIM // 2, rot).astype(F32)]
    rows += [jnp.zeros((128,), F32)] * 3
    return jnp.stack(rows).astype(F32)


def _ln_inproj(x, pos_col, g0, b0, w_in, b_in, shards):
    s_len = x.shape[0]
    tm = _tile(s_len, 512)

    n = len(shards)

    def body(x_ref, pos_ref, g_ref, b_ref, w_ref, bi_ref, rc_ref, *rest):
        q_ref, k_ref, v_ref, su_ref, sv_ref, tc_ref, t1_ref, t2_ref = rest[n:n + 8]
        gathered = rest[n + 8:2 * n + 8]
        gather = _Gather(rest[:n], rest[2 * n + 8:3 * n + 8], rest[3 * n + 8], rest[3 * n + 9])
        flush_sems = rest[3 * n + 10]
        i = pl.program_id(0)

        @pl.when(i == 0)
        def _():
            gather.start()

        h0, _, _ = _ln(x_ref[...], g_ref[...], b_ref[...])
        proj = _dot_nt(h0.astype(_MXU), w_ref[...]) + bi_ref[...]
        ang = pos_ref[...].astype(F32) * rc_ref[0:1, :]
        cs = jnp.cos(ang)
        sn = jnp.sin(ang)
        tc = cs * rc_ref[1:2, :] + rc_ref[2:3, :]
        t1 = -sn * rc_ref[3:4, :]
        t2 = sn * rc_ref[4:5, :]
        tc_ref[...] = tc
        t1_ref[...] = t1
        t2_ref[...] = t2
        q = _rope(proj[:, 0:ATTN_W], tc, t1, t2) * (HEAD_DIM ** -0.5)
        q_ref[...] = q.astype(_MXU)
        k_ref[...] = _rope(proj[:, ATTN_W:ATTN_W + KV_W], tc, t1, t2).astype(_MXU)
        v_ref[...] = proj[:, ATTN_W + KV_W:ATTN_W + 2 * KV_W].astype(_MXU)
        su_ref[...] = proj[:, ATTN_W + 2 * KV_W:ATTN_W + 2 * KV_W + SGU_W]
        sv_ref[...] = proj[:, ATTN_W + 2 * KV_W + SGU_W:IN_W]

        @pl.when(i == pl.num_programs(0) - 1)
        def _():
            gather.finish()
            gather.flush(gathered, flush_sems)

    sd = jax.ShapeDtypeStruct
    return pl.pallas_call(
        body, name="ln_inproj", grid=(s_len // tm,),
        in_specs=[_rows(tm, D_MODEL), _rows(tm, 1), _const2((1, D_MODEL)), _const2((1, D_MODEL)), _vmem(),
                  _const2((1, IN_W)), _const2((8, 128))] + [_vmem()] * n,
        out_specs=[_rows(tm, ATTN_W), _rows(tm, KV_W), _rows(tm, KV_W), _rows(tm, SGU_W), _rows(tm, SGU_W),
                   _rows(tm, 128), _rows(tm, 128), _rows(tm, 128)] + [_hbm()] * n,
        out_shape=[sd((s_len, ATTN_W), _MXU), sd((s_len, KV_W), _MXU), sd((s_len, KV_W), _MXU),
                   sd((s_len, SGU_W), F32), sd((s_len, SGU_W), F32),
                   sd((s_len, 128), F32), sd((s_len, 128), F32), sd((s_len, 128), F32)] + _Gather.out_shapes(shards),
        scratch_shapes=_Gather.scratch(shards),
        compiler_params=_params(56),
    )(x, pos_col, g0, b0, w_in, b_in, _rope_consts(), *shards)


def _attn_probs(qh, kh, sink, allowed):
    s = jnp.where(allowed, _dot_nt(qh, kh), -1e30)
    m = jnp.maximum(jnp.max(s, axis=-1, keepdims=True), sink)
    p = jnp.exp(s - m)
    ps = jnp.exp(sink - m)
    inv = 1.0 / (jnp.sum(p, axis=-1, keepdims=True) + ps)
    return p * inv, ps * inv


def _sgu_fwd(su, sv, lg, lb, w_ref, bt_ref):
    u = _gelu(su)
    vv, vhat, rstd = _ln(_gelu(sv), lg, lb)
    vvb = vv.astype(_MXU)
    wcs, mixed = [], []
    for h in range(N_GRP):
        wc = _causal_w(w_ref, h).astype(_MXU)
        wcs.append(wc)
        mixed.append(_dot(wc, vvb[:, h * GRP_DIM:(h + 1) * GRP_DIM]) + bt_ref[:, h:h + 1])
    return u, vhat, rstd, vvb, wcs, jnp.concatenate(mixed, axis=1)


def _prev_map(i):
    return (jnp.maximum(i - 1, 0), 0)


def _mixer_fwd(q, k, v, su, sv, sinks, sg, sb, sgu_w, sgu_bt, shards):
    s_len = q.shape[0]
    nb = s_len // BLK
    n = len(shards)

    def body(q_ref, kc_ref, kp_ref, vc_ref, vp_ref, su_ref, sv_ref, sink_ref, lg_ref, lb_ref, w_ref, bt_ref, *rest):
        mc_ref = rest[n]
        gathered = rest[n + 1:2 * n + 1]
        gather = _Gather(rest[:n], rest[2 * n + 1:3 * n + 1], rest[3 * n + 1], rest[3 * n + 2])
        flush_sems = rest[3 * n + 3]
        i = pl.program_id(0)

        @pl.when(i == 0)
        def _():
            gather.start()

        @pl.when(i == nb - 1)
        def _():
            gather.finish()
            gather.flush(gathered, flush_sems)

        allowed = _band_mask(i == 0)
        kb = jnp.concatenate([kp_ref[...], kc_ref[...]], axis=0)
        vb = jnp.concatenate([vp_ref[...], vc_ref[...]], axis=0)
        qv = q_ref[...]
        outs = []
        for h in range(N_Q):
            g = h // Q_PER_KV
            kh = kb[:, g * HEAD_DIM:(g + 1) * HEAD_DIM]
            vh = vb[:, g * HEAD_DIM:(g + 1) * HEAD_DIM]
            probs, _ = _attn_probs(qv[:, h * HEAD_DIM:(h + 1) * HEAD_DIM], kh, sink_ref[h], allowed)
            outs.append(_dot(probs.astype(_MXU), vh))
        u, _, _, _, _, mixed = _sgu_fwd(su_ref[...], sv_ref[...], lg_ref[...], lb_ref[...], w_ref, bt_ref)
        mc_ref[...] = jnp.concatenate(outs + [u * mixed], axis=1).astype(_MXU)

    cur = lambda w: pl.BlockSpec((BLK, w), lambda i: (i, 0))
    prev = lambda w: pl.BlockSpec((BLK, w), _prev_map)
    return pl.pallas_call(
        body, name="mixer_fwd", grid=(nb,),
        in_specs=[cur(ATTN_W), cur(KV_W), prev(KV_W), cur(KV_W), prev(KV_W), cur(SGU_W), cur(SGU_W), _smem(),
                  _const2((1, SGU_W)), _const2((1, SGU_W)), _const2((N_GRP, BLK, BLK)), _const2((BLK, N_GRP))]
        + [_vmem()] * n,
        out_specs=[cur(D_MODEL)] + [_hbm()] * n,
        out_shape=[jax.ShapeDtypeStruct((s_len, D_MODEL), _MXU)] + _Gather.out_shapes(shards),
        scratch_shapes=_Gather.scratch(shards),
        compiler_params=_params(48),
    )(q, k, k, v, v, su, sv, sinks, sg, sb, sgu_w, sgu_bt, *shards)


def _outproj(mc, w_out, b_out, x, g0, b0, shards):
    s_len = x.shape[0]
    tm = _tile(s_len, 512)
    n = len(shards)

    def body(mc_ref, w_ref, bo_ref, x_ref, g_ref, b_ref, *rest):
        r1_ref = rest[n]
        gathered = rest[n + 1:2 * n + 1]
        gather = _Gather(rest[:n], rest[2 * n + 1:3 * n + 1], rest[3 * n + 1], rest[3 * n + 2])
        flush_sems = rest[3 * n + 3]
        i = pl.program_id(0)

        @pl.when(i == 0)
        def _():
            gather.start()

        h0, _, _ = _ln(x_ref[...], g_ref[...], b_ref[...])
        r1_ref[...] = ALPHA * h0 + (_dot(mc_ref[...], w_ref[...]) + bo_ref[...])

        @pl.when(i == pl.num_programs(0) - 1)
        def _():
            gather.finish()
            gather.flush(gathered, flush_sems)

    return pl.pallas_call(
        body, name="outproj", grid=(s_len // tm,),
        in_specs=[_rows(tm, D_MODEL), _vmem(), _const2((1, D_MODEL)), _rows(tm, D_MODEL),
                  _const2((1, D_MODEL)), _const2((1, D_MODEL))] + [_vmem()] * n,
        out_specs=[_rows(tm, D_MODEL)] + [_hbm()] * n,
        out_shape=[jax.ShapeDtypeStruct((s_len, D_MODEL), F32)] + _Gather.out_shapes(shards),
        scratch_shapes=_Gather.scratch(shards),
        compiler_params=_params(40),
    )(mc, w_out, b_out, x, g0, b0, *shards)


def _ffn_spec(tm):
    return pl.BlockSpec((N_CHIP, tm, FF_SH), lambda i: (0, i, 0))


def _ffn_up(r1, g1, b1, wg, wu, shards):
    s_len = r1.shape[0]
    tm = _tile(s_len, 512)
    n = len(shards)

    def body(r1_ref, g_ref, b_ref, wg_ref, wu_ref, *rest):
        go_ref, uo_ref = rest[n:n + 2]
        gathered = rest[n + 2:2 * n + 2]
        gather = _Gather(rest[:n], rest[2 * n + 2:3 * n + 2], rest[3 * n + 2], rest[3 * n + 3])
        flush_sems = rest[3 * n + 4]
        i = pl.program_id(0)

        @pl.when(i == 0)
        def _():
            gather.start()

        h1, _, _ = _ln(r1_ref[...], g_ref[...], b_ref[...])
        h1b = h1.astype(_MXU)
        for j in range(N_CHIP):
            go_ref[j] = _dot_nt(h1b, wg_ref[j]).astype(_ACT)
            uo_ref[j] = _dot_nt(h1b, wu_ref[j]).astype(_ACT)

        @pl.when(i == pl.num_programs(0) - 1)
        def _():
            gather.finish()
            gather.flush(gathered, flush_sems)

    sd = jax.ShapeDtypeStruct((N_CHIP, s_len, FF_SH), _ACT)
    return pl.pallas_call(
        body, name="ffn_up", grid=(s_len // tm,),
        in_specs=[_rows(tm, D_MODEL), _const2((1, D_MODEL)), _const2((1, D_MODEL)), _vmem(), _vmem()] + [_vmem()] * n,
        out_specs=[_ffn_spec(tm), _ffn_spec(tm)] + [_hbm()] * n,
        out_shape=[sd, sd] + _Gather.out_shapes(shards),
        scratch_shapes=_Gather.scratch(shards),
        compiler_params=_params(56),
    )(r1, g1, b1, wg, wu, *shards)


def _silu_parts(g):
    sg = 1.0 / (1.0 + jnp.exp(-g))
    return g * sg, sg


def _ffn_down_loss(gact, uact, wd, r1, g1, b1, g2, b2, target):
    s_len = r1.shape[0]
    tm = _tile(s_len, 512)

    def body(g_ref, u_ref, wd_ref, r1_ref, g1_ref, b1_ref, g2_ref, b2_ref, t_ref,
             dr2_ref, loss_ref, dg2_ref, db2_ref):
        i = pl.program_id(0)
        f = jnp.zeros((tm, D_MODEL), F32)
        for j in range(N_CHIP):
            silu, _ = _silu_parts(g_ref[j].astype(F32))
            f = f + _dot((silu * u_ref[j].astype(F32)).astype(_MXU), wd_ref[j])
        h1, _, _ = _ln(r1_ref[...], g1_ref[...], b1_ref[...])
        h2, r2hat, rstd2 = _ln(ALPHA * h1 + f, g2_ref[...], b2_ref[...])
        diff = h2 - t_ref[...]
        dh2 = diff * (1.0 / D_MODEL)

        @pl.when(i == 0)
        def _():
            loss_ref[...] = jnp.zeros_like(loss_ref)
            dg2_ref[...] = jnp.zeros_like(dg2_ref)
            db2_ref[...] = jnp.zeros_like(db2_ref)

        loss_ref[...] += _colsum(diff * diff)
        dg2_ref[...] += _colsum(dh2 * r2hat)
        db2_ref[...] += _colsum(dh2)
        dr2_ref[...] = _ln_bwd(dh2, r2hat, rstd2, g2_ref[...])

    vec = jax.ShapeDtypeStruct((1, D_MODEL), F32)
    c = _const2((1, D_MODEL))
    return pl.pallas_call(
        body, name="ffn_down_loss", grid=(s_len // tm,),
        in_specs=[_ffn_spec(tm), _ffn_spec(tm), _vmem(), _rows(tm, D_MODEL), c, c, c, c, _rows(tm, D_MODEL)],
        out_specs=[_rows(tm, D_MODEL), c, c, c],
        out_shape=[jax.ShapeDtypeStruct((s_len, D_MODEL), F32), vec, vec, vec],
        compiler_params=_params(48),
    )(gact, uact, wd, r1, g1, b1, g2, b2, target)


def _ffn_bwd_a(dr2, gact, uact, wd):
    s_len = dr2.shape[0]
    tm = _tile(s_len, 512)

    def body(dr2_ref, g_ref, u_ref, wd_ref, dg_ref, du_ref, dwd_ref, wire_ref, land_ref, send_sem, recv_sem):
        i = pl.program_id(0)

        @pl.when(i == 0)
        def _():
            dwd_ref[...] = jnp.zeros_like(dwd_ref)

        dfb = dr2_ref[...].astype(_MXU)
        for j in range(N_CHIP):
            g = g_ref[j].astype(F32)
            u = u_ref[j].astype(F32)
            silu, sg = _silu_parts(g)
            da = _dot_nt(dfb, wd_ref[j])
            dg_ref[j] = (da * u * (sg * (1.0 + g * (1.0 - sg)))).astype(_MXU)
            du_ref[j] = (da * silu).astype(_MXU)
            dwd_ref[j * FF_SH:(j + 1) * FF_SH, :] += _dot_tn((silu * u).astype(_MXU), dfb)

        @pl.when(i == pl.num_programs(0) - 1)
        def _():
            _pair_reduce(dwd_ref, land_ref, wire_ref, send_sem, recv_sem)

    sd = jax.ShapeDtypeStruct((N_CHIP, s_len, FF_SH), _MXU)
    return pl.pallas_call(
        body, name="ffn_bwd_a", grid=(s_len // tm,),
        in_specs=[_rows(tm, D_MODEL), _ffn_spec(tm), _ffn_spec(tm), _vmem()],
        out_specs=[_ffn_spec(tm), _ffn_spec(tm), _vmem(), _vmem()],
        out_shape=[sd, sd, jax.ShapeDtypeStruct((D_FF, D_MODEL), F32),
                   jax.ShapeDtypeStruct((N_CHIP, FF_SH // 2, D_MODEL), _WIRE)],
        scratch_shapes=_pair_scratch((N_CHIP, FF_SH // 2, D_MODEL)),
        compiler_params=_params(58),
    )(dr2, gact, uact, wd)


def _ffn_bwd_g(dr2, dg, r1, g1, b1, wg, prev_wire):
    s_len = dr2.shape[0]
    tm = _tile(s_len, 512)

    def body(dr2_ref, dg_ref, r1_ref, g1_ref, b1_ref, wg_ref, pw_ref, dh1_ref, dwg_ref, wire_ref, pl_ref,
             land_ref, send_sem, recv_sem, xl_ref, x_send, x_recv, x_flush):
        i = pl.program_id(0)
        exchange = _ChipExchange(pw_ref, xl_ref, x_send, x_recv)

        @pl.when(i == 0)
        def _():
            exchange.start()
            dwg_ref[...] = jnp.zeros_like(dwg_ref)

        h1, _, _ = _ln(r1_ref[...], g1_ref[...], b1_ref[...])
        h1b = h1.astype(_MXU)
        dh1 = ALPHA * dr2_ref[...]
        for j in range(N_CHIP):
            dgj = dg_ref[j]
            dh1 = dh1 + _dot(dgj, wg_ref[j])
            dwg_ref[j * FF_SH:(j + 1) * FF_SH, :] += _dot_tn(dgj, h1b)
        dh1_ref[...] = dh1

        @pl.when(i == pl.num_programs(0) - 1)
        def _():
            _pair_reduce(dwg_ref, land_ref, wire_ref, send_sem, recv_sem)
            exchange.finish_to(pl_ref, x_flush)

    c = _const2((1, D_MODEL))
    return pl.pallas_call(
        body, name="ffn_bwd_g", grid=(s_len // tm,),
        in_specs=[_rows(tm, D_MODEL), _ffn_spec(tm), _rows(tm, D_MODEL), c, c, _vmem(), _vmem()],
        out_specs=[_rows(tm, D_MODEL), _vmem(), _vmem(), _hbm()],
        out_shape=[jax.ShapeDtypeStruct((s_len, D_MODEL), F32), jax.ShapeDtypeStruct((D_FF, D_MODEL), F32),
                   jax.ShapeDtypeStruct((N_CHIP, FF_SH // 2, D_MODEL), _WIRE), _ChipExchange.land_shape(prev_wire)],
        scratch_shapes=_pair_scratch((N_CHIP, FF_SH // 2, D_MODEL)) + _ChipExchange.scratch(prev_wire),
        compiler_params=_params(58),
    )(dr2, dg, r1, g1, b1, wg, prev_wire)


def _ffn_bwd_u(dh1a, du, r1, g1, b1, wu, prev_wire):
    s_len = dh1a.shape[0]
    tm = _tile(s_len, 512)

    def body(dh1_ref, du_ref, r1_ref, g1_ref, b1_ref, wu_ref, pw_ref,
             dr1_ref, dwu_ref, wire_ref, dg1_ref, db1_ref, pl_ref,
             land_ref, send_sem, recv_sem, xl_ref, x_send, x_recv, x_flush):
        i = pl.program_id(0)
        exchange = _ChipExchange(pw_ref, xl_ref, x_send, x_recv)

        @pl.when(i == 0)
        def _():
            exchange.start()
            dwu_ref[...] = jnp.zeros_like(dwu_ref)
            dg1_ref[...] = jnp.zeros_like(dg1_ref)
            db1_ref[...] = jnp.zeros_like(db1_ref)

        h1, r1hat, rstd1 = _ln(r1_ref[...], g1_ref[...], b1_ref[...])
        h1b = h1.astype(_MXU)
        dh1 = dh1_ref[...]
        for j in range(N_CHIP):
            duj = du_ref[j]
            dh1 = dh1 + _dot(duj, wu_ref[j])
            dwu_ref[j * FF_SH:(j + 1) * FF_SH, :] += _dot_tn(duj, h1b)
        dg1_ref[...] += _colsum(dh1 * r1hat)
        db1_ref[...] += _colsum(dh1)
        dr1_ref[...] = _ln_bwd(dh1, r1hat, rstd1, g1_ref[...])

        @pl.when(i == pl.num_programs(0) - 1)
        def _():
            _pair_reduce(dwu_ref, land_ref, wire_ref, send_sem, recv_sem)
            exchange.finish_to(pl_ref, x_flush)

    vec = jax.ShapeDtypeStruct((1, D_MODEL), F32)
    c = _const2((1, D_MODEL))
    return pl.pallas_call(
        body, name="ffn_bwd_u", grid=(s_len // tm,),
        in_specs=[_rows(tm, D_MODEL), _ffn_spec(tm), _rows(tm, D_MODEL), c, c, _vmem(), _vmem()],
        out_specs=[_rows(tm, D_MODEL), _vmem(), _vmem(), c, c, _hbm()],
        out_shape=[jax.ShapeDtypeStruct((s_len, D_MODEL), F32), jax.ShapeDtypeStruct((D_FF, D_MODEL), F32),
                   jax.ShapeDtypeStruct((N_CHIP, FF_SH // 2, D_MODEL), _WIRE), vec, vec,
                   _ChipExchange.land_shape(prev_wire)],
        scratch_shapes=_pair_scratch((N_CHIP, FF_SH // 2, D_MODEL)) + _ChipExchange.scratch(prev_wire),
        compiler_params=_params(58),
    )(dh1a, du, r1, g1, b1, wu, prev_wire)


def _outproj_bwd(dr1, mc, w_out, prev_wire):
    s_len = dr1.shape[0]
    tm = _tile(s_len, 512)

    def body(dr1_ref, mc_ref, w_ref, pw_ref, dmc_ref, dw_ref, wire_ref, db_ref, pl_ref,
             land_ref, send_sem, recv_sem, xl_ref, x_send, x_recv, x_flush):
        i = pl.program_id(0)
        exchange = _ChipExchange(pw_ref, xl_ref, x_send, x_recv)

        @pl.when(i == 0)
        def _():
            exchange.start()
            dw_ref[...] = jnp.zeros_like(dw_ref)
            db_ref[...] = jnp.zeros_like(db_ref)

        d = dr1_ref[...]
        db_ref[...] += _colsum(d)
        db16 = d.astype(_MXU)
        dmc_ref[...] = _dot_nt(db16, w_ref[...])
        dw_ref[...] += _dot_tn(mc_ref[...], db16)

        @pl.when(i == pl.num_programs(0) - 1)
        def _():
            _pair_reduce(dw_ref, land_ref, wire_ref, send_sem, recv_sem)
            exchange.finish_to(pl_ref, x_flush)

    return pl.pallas_call(
        body, name="outproj_bwd", grid=(s_len // tm,),
        in_specs=[_rows(tm, D_MODEL), _rows(tm, D_MODEL), _vmem(), _vmem()],
        out_specs=[_rows(tm, D_MODEL), _vmem(), _vmem(), _const2((1, D_MODEL)), _hbm()],
        out_shape=[jax.ShapeDtypeStruct((s_len, D_MODEL), F32), jax.ShapeDtypeStruct((D_MODEL, D_MODEL), F32),
                   jax.ShapeDtypeStruct((N_CHIP, OUT_SH // 2, D_MODEL), _WIRE), jax.ShapeDtypeStruct((1, D_MODEL), F32),
                   _ChipExchange.land_shape(prev_wire)],
        scratch_shapes=_pair_scratch((N_CHIP, OUT_SH // 2, D_MODEL)) + _ChipExchange.scratch(prev_wire),
        compiler_params=_params(48),
    )(dr1, mc, w_out, prev_wire)


def _mixer_bwd(q, k, v, su, sv, dmc, tc, t1, t2, sinks, sg, sb, sgu_w, sgu_bt, prev_wire):
    s_len = q.shape[0]
    nb = s_len // BLK

    def body(q_ref, kc_ref, kp_ref, vc_ref, vp_ref, su_ref, sv_ref, dmc_ref,
             tc_ref, t1_ref, t2_ref, tcp_ref, t1p_ref, t2p_ref,
             sink_ref, lg_ref, lb_ref, w_ref, bt_ref, pw_ref,
             dq_ref, dkv_ref, dsuv_ref, dbq_ref, dbkv_ref, dbsuv_ref,
             dsink_ref, dlg_ref, dlb_ref, dw_ref, dbt_ref, pl_ref, carry_ref, xl_ref, x_send, x_recv, x_flush):
        i = pl.program_id(0)
        exchange = _ChipExchange(pw_ref, xl_ref, x_send, x_recv)

        @pl.when(i == 0)
        def _():
            exchange.start()

        @pl.when(i == 0)
        def _():
            for r in (dbq_ref, dbkv_ref, dbsuv_ref, dsink_ref, dlg_ref, dlb_ref, dw_ref, dbt_ref):
                r[...] = jnp.zeros_like(r)

        def emit_kv(fin):
            dk = _rope_bwd(fin[:, 0:KV_W], tcp_ref[...], t1p_ref[...], t2p_ref[...])
            out = jnp.concatenate([dk, fin[:, KV_W:2 * KV_W]], axis=1)
            dkv_ref[...] = out.astype(_MXU)
            dbkv_ref[...] += _colsum(out)

        @pl.when(i < nb)
        def _():
            allowed = _band_mask(i == 0)
            kb = jnp.concatenate([kp_ref[...], kc_ref[...]], axis=0)
            vb = jnp.concatenate([vp_ref[...], vc_ref[...]], axis=0)
            qv = q_ref[...]
            dmc = dmc_ref[...]
            dqs, dks, dvs, dsinks = [], [], [], []
            for g in range(N_KV):
                kh = kb[:, g * HEAD_DIM:(g + 1) * HEAD_DIM]
                vh = vb[:, g * HEAD_DIM:(g + 1) * HEAD_DIM]
                dk_g = jnp.zeros((2 * BLK, HEAD_DIM), F32)
                dv_g = jnp.zeros((2 * BLK, HEAD_DIM), F32)
                for hh in range(Q_PER_KV):
                    h = g * Q_PER_KV + hh
                    qh = qv[:, h * HEAD_DIM:(h + 1) * HEAD_DIM]
                    probs, psink = _attn_probs(qh, kh, sink_ref[h], allowed)
                    pb = probs.astype(_MXU)
                    dob = dmc[:, h * HEAD_DIM:(h + 1) * HEAD_DIM].astype(_MXU)
                    dv_g = dv_g + _dot_tn(pb, dob)
                    dp = _dot_nt(dob, vh)
                    rd = jnp.sum(probs * dp, axis=-1, keepdims=True)
                    dsb = (probs * (dp - rd)).astype(_MXU)
                    dsinks.append(-jnp.sum(psink * rd, axis=0, keepdims=True))
                    dqs.append(_dot(dsb, kh))
                    dk_g = dk_g + _dot_tn(dsb, qh)
                dks.append(dk_g)
                dvs.append(dv_g)
            dq = _rope_bwd(jnp.concatenate(dqs, axis=1) * (HEAD_DIM ** -0.5), tc_ref[...], t1_ref[...], t2_ref[...])
            dq_ref[...] = dq.astype(_MXU)
            dbq_ref[...] += _colsum(dq)
            dsink_ref[...] += _lane_put(dsinks, 128)
            contrib = jnp.concatenate(dks + dvs, axis=1)

            @pl.when(i > 0)
            def _():
                emit_kv(carry_ref[...] + contrib[0:BLK, :])

            carry_ref[...] = contrib[BLK:2 * BLK, :]

            su = su_ref[...]
            sv = sv_ref[...]
            lg = lg_ref[...]
            u, vhat, rstd, vvb, wcs, mixed = _sgu_fwd(su, sv, lg, lb_ref[...], w_ref, bt_ref)
            dsgu = dmc[:, ATTN_W:D_MODEL]
            dsu = dsgu * mixed * _gelu_grad(su)
            dmixed = dsgu * u
            tri_t = lax.broadcasted_iota(jnp.int32, (BLK, BLK), 0)
            tri_s = lax.broadcasted_iota(jnp.int32, (BLK, BLK), 1)
            dvv, dbs = [], []
            for h in range(N_GRP):
                dm = dmixed[:, h * GRP_DIM:(h + 1) * GRP_DIM]
                dmb = dm.astype(_MXU)
                dbs.append(jnp.sum(dm, axis=1, keepdims=True))
                dw_ref[h] += jnp.where(tri_s <= tri_t, _dot_nt(dmb, vvb[:, h * GRP_DIM:(h + 1) * GRP_DIM]), 0.0)
                dvv.append(_dot_tn(wcs[h], dmb))
            dvv = jnp.concatenate(dvv, axis=1)
            dbt_ref[...] += _lane_put(dbs, 128)
            dlg_ref[...] += _colsum(dvv * vhat)
            dlb_ref[...] += _colsum(dvv)
            dsv = _ln_bwd(dvv, vhat, rstd, lg) * _gelu_grad(sv)
            dsuv = jnp.concatenate([dsu, dsv], axis=1)
            dsuv_ref[...] = dsuv.astype(_MXU)
            dbsuv_ref[...] += _colsum(dsuv)

        @pl.when(i == nb)
        def _():
            emit_kv(carry_ref[...])
            exchange.finish_to(pl_ref, x_flush)

    last = nb - 1
    cur = lambda w: pl.BlockSpec((BLK, w), lambda i: (jnp.minimum(i, last), 0))
    prev = lambda w: pl.BlockSpec((BLK, w), lambda i: (jnp.clip(i - 1, 0, last), 0))
    sd = jax.ShapeDtypeStruct
    return pl.pallas_call(
        body, name="mixer_bwd", grid=(nb + 1,),
        in_specs=[cur(ATTN_W), cur(KV_W), prev(KV_W), cur(KV_W), prev(KV_W), cur(SGU_W), cur(SGU_W), cur(D_MODEL),
                  cur(128), cur(128), cur(128), prev(128), prev(128), prev(128),
                  _smem(), _const2((1, SGU_W)), _const2((1, SGU_W)), _const2((N_GRP, BLK, BLK)), _const2((BLK, N_GRP)),
                  _vmem()],
        out_specs=[cur(ATTN_W), prev(2 * KV_W), cur(2 * SGU_W),
                   _const2((1, ATTN_W)), _const2((1, 2 * KV_W)), _const2((1, 2 * SGU_W)),
                   _const2((1, 128)), _const2((1, SGU_W)), _const2((1, SGU_W)),
                   _const2((N_GRP, BLK, BLK)), _const2((BLK, 128)), _hbm()],
        out_shape=[sd((s_len, ATTN_W), _MXU), sd((s_len, 2 * KV_W), _MXU), sd((s_len, 2 * SGU_W), _MXU),
                   sd((1, ATTN_W), F32), sd((1, 2 * KV_W), F32), sd((1, 2 * SGU_W), F32),
                   sd((1, 128), F32), sd((1, SGU_W), F32), sd((1, SGU_W), F32),
                   sd((N_GRP, BLK, BLK), F32), sd((BLK, 128), F32), _ChipExchange.land_shape(prev_wire)],
        scratch_shapes=[pltpu.VMEM((BLK, 2 * KV_W), F32)] + _ChipExchange.scratch(prev_wire),
        compiler_params=_params(32),
    )(q, k, k, v, v, su, sv, dmc, tc, t1, t2, tc, t1, t2, sinks, sg, sb, sgu_w, sgu_bt, prev_wire)


def _inproj_bwd(dq, dkv, dsuv, dr1, x, g0, b0, w_in):
    s_len = x.shape[0]
    tm = _tile(s_len, 512)
    cuts = ((0, ATTN_W), (ATTN_W, ATTN_W + 2 * KV_W), (ATTN_W + 2 * KV_W, IN_W))

    def body(dq_ref, dkv_ref, dsuv_ref, dr1_ref, x_ref, g_ref, b_ref, w_ref,
             dx_ref, dw_ref, wire_ref, dg_ref, db_ref, land_ref, send_sem, recv_sem):
        i = pl.program_id(0)

        @pl.when(i == 0)
        def _():
            dw_ref[...] = jnp.zeros_like(dw_ref)
            dg_ref[...] = jnp.zeros_like(dg_ref)
            db_ref[...] = jnp.zeros_like(db_ref)

        h0, xhat, rstd = _ln(x_ref[...], g_ref[...], b_ref[...])
        h0b = h0.astype(_MXU)
        dh0 = ALPHA * dr1_ref[...]
        for (lo, hi), d_ref in zip(cuts, (dq_ref, dkv_ref, dsuv_ref)):
            d = d_ref[...]
            dh0 = dh0 + _dot(d, w_ref[lo:hi, :])
            dw_ref[lo:hi, :] += _dot_tn(d, h0b)
        dg_ref[...] += _colsum(dh0 * xhat)
        db_ref[...] += _colsum(dh0)
        dx_ref[...] = _ln_bwd(dh0, xhat, rstd, g_ref[...])

        @pl.when(i == pl.num_programs(0) - 1)
        def _():
            _pair_reduce(dw_ref, land_ref, wire_ref, send_sem, recv_sem)

    vec = jax.ShapeDtypeStruct((1, D_MODEL), F32)
    c = _const2((1, D_MODEL))
    return pl.pallas_call(
        body, name="inproj_bwd", grid=(s_len // tm,),
        in_specs=[_rows(tm, ATTN_W), _rows(tm, 2 * KV_W), _rows(tm, 2 * SGU_W), _rows(tm, D_MODEL), _rows(tm, D_MODEL),
                  c, c, _vmem()],
        out_specs=[_rows(tm, D_MODEL), _vmem(), _vmem(), c, c],
        out_shape=[jax.ShapeDtypeStruct((s_len, D_MODEL), F32), jax.ShapeDtypeStruct((IN_W, D_MODEL), F32),
                   jax.ShapeDtypeStruct((N_CHIP, IN_SH // 2, D_MODEL), _WIRE), vec, vec],
        scratch_shapes=_pair_scratch((N_CHIP, IN_SH // 2, D_MODEL)),
        compiler_params=_params(56),
    )(dq, dkv, dsuv, dr1, x, g0, b0, w_in)


def _place():
    x, y, c = (lax.axis_index(a) for a in MESH_AXES)
    chips = [(1 - x, y), (x, 1 - y), (1 - x, 1 - y)]
    return x, y, c, chips


class _Gather:
    def __init__(self, ins, outs, send_sems, recv_sems):
        self.ins, self.outs, self.send_sems, self.recv_sems = ins, outs, send_sems, recv_sems
        self.n = len(ins)
        self.halves = [r.shape[0] // 2 for r in ins]

    def _copy(self, k, t, slot, half, to):
        rows = pl.ds(pl.multiple_of(half * self.halves[t], 16), self.halves[t])
        piece = self.outs[t].at[slot, rows, :]
        return pltpu.make_async_remote_copy(src_ref=piece, dst_ref=piece, send_sem=self.send_sems.at[k],
                                            recv_sem=self.recv_sems.at[k], device_id=to, device_id_type=MESH)

    def _chip_copy(self, t, d, slot):
        x, y, c, chips = _place()
        return self._copy(3 * t + d, t, slot, c, (chips[d][0], chips[d][1], c))

    def _pass_copy(self, t, d, half):
        x, y, c, chips = _place()
        return self._copy(3 * self.n + 3 * t + d, t, 2 * chips[d][0] + chips[d][1], half, (x, y, 1 - c))

    def start(self):
        x, y, c, chips = _place()
        me = 2 * x + y
        for t in range(self.n):
            self.outs[t][me] = self.ins[t][...].astype(_WIRE)
        for t in range(self.n):
            for d in range(3):
                self._chip_copy(t, d, me).start()

    def finish(self):
        x, y, c, chips = _place()
        me = 2 * x + y
        for t in range(self.n):
            for d in range(3):
                self._chip_copy(t, d, 2 * chips[d][0] + chips[d][1]).wait_recv()
                self._pass_copy(t, d, c).start()
        for t in range(self.n):
            for d in range(3):
                self._pass_copy(t, d, 1 - c).wait_recv()
        for t in range(self.n):
            for d in range(3):
                self._chip_copy(t, d, me).wait_send()
                self._pass_copy(t, d, c).wait_send()

    def flush(self, hbm_outs, flush_sems):
        _flush(self.outs, hbm_outs, flush_sems)

    @staticmethod
    def out_shapes(shards):
        return [jax.ShapeDtypeStruct((N_CHIP,) + s.shape, _WIRE) for s in shards]

    @staticmethod
    def sems(n):
        return [pltpu.SemaphoreType.DMA((6 * n,)), pltpu.SemaphoreType.DMA((6 * n,))]

    @staticmethod
    def scratch(shards):
        n = len(shards)
        return ([pltpu.VMEM((N_CHIP,) + s.shape, _WIRE) for s in shards] + _Gather.sems(n)
                + [pltpu.SemaphoreType.DMA((n,))])


def _flush(bufs, hbm_outs, sems):
    copies = [pltpu.make_async_copy(b, o, sems.at[k]) for k, (b, o) in enumerate(zip(bufs, hbm_outs))]
    for cp in copies:
        cp.start()
    for cp in copies:
        cp.wait()


def _gather_weights(shards):
    n = len(shards)

    def body(*refs):
        gather = _Gather(refs[:n], refs[n:2 * n], refs[2 * n], refs[2 * n + 1])
        gather.start()
        gather.finish()

    return pl.pallas_call(
        body, name="gather_weights",
        in_specs=[_vmem()] * n, out_specs=[_vmem()] * n,
        out_shape=_Gather.out_shapes(shards), scratch_shapes=_Gather.sems(n),
        compiler_params=pltpu.CompilerParams(vmem_limit_bytes=32 * MIB),
    )(*shards)


class _ChipExchange:
    def __init__(self, wire_ref, land_ref, send_sems, recv_sems):
        self.wire, self.land, self.send_sems, self.recv_sems = wire_ref, land_ref, send_sems, recv_sems

    def _copy(self, d):
        x, y, c, chips = _place()
        return pltpu.make_async_remote_copy(
            src_ref=self.wire.at[2 * chips[d][0] + chips[d][1]], dst_ref=self.land.at[d],
            send_sem=self.send_sems.at[d], recv_sem=self.recv_sems.at[d],
            device_id=(chips[d][0], chips[d][1], c), device_id_type=MESH)

    def start(self):
        for d in range(3):
            self._copy(d).start()

    def wait_recv(self):
        for d in range(3):
            self._copy(d).wait_recv()

    def wait_send(self):
        for d in range(3):
            self._copy(d).wait_send()

    def finish_to(self, hbm_out, flush_sem):
        self.wait_recv()
        _flush([self.land], [hbm_out], flush_sem)
        self.wait_send()

    @staticmethod
    def land_shape(wire):
        return jax.ShapeDtypeStruct((3,) + wire.shape[1:], wire.dtype)

    @staticmethod
    def sems():
        return [pltpu.SemaphoreType.DMA((3,)), pltpu.SemaphoreType.DMA((3,))]

    @staticmethod
    def scratch(wire):
        return ([pltpu.VMEM((3,) + wire.shape[1:], wire.dtype)] + _ChipExchange.sems() + [pltpu.SemaphoreType.DMA((1,))])


def _pair_scratch(half_shape):
    return [pltpu.VMEM(half_shape, F32), pltpu.SemaphoreType.DMA((N_CHIP,)), pltpu.SemaphoreType.DMA((N_CHIP,))]


def _pair_reduce(acc_ref, land_ref, wire_ref, send_sems, recv_sems):
    rh = land_ref.shape[1]
    x, y, c, _ = _place()
    copies = []
    for j in range(N_CHIP):
        give = acc_ref.at[pl.ds(pl.multiple_of(j * 2 * rh + (1 - c) * rh, 8), rh), :]
        cp = pltpu.make_async_remote_copy(src_ref=give, dst_ref=land_ref.at[j], send_sem=send_sems.at[j],
                                          recv_sem=recv_sems.at[j], device_id=(x, y, 1 - c), device_id_type=MESH)
        cp.start()
        copies.append(cp)
    for cp in copies:
        cp.wait()

    def chunk(r, carry):
        theirs = pl.ds(pl.multiple_of(r * ROW_CHUNK, ROW_CHUNK), ROW_CHUNK)
        for j in range(N_CHIP):
            mine = pl.ds(pl.multiple_of(j * 2 * rh + c * rh + r * ROW_CHUNK, 8), ROW_CHUNK)
            s = acc_ref[mine, :] + land_ref[j, theirs, :]
            acc_ref[mine, :] = s
            wire_ref[j, theirs, :] = s.astype(_WIRE)
        return carry

    lax.fori_loop(0, rh // ROW_CHUNK, chunk, 0)


def _grad_finish(last_wire, lands, accs):
    n = len(accs)
    halves = [last_wire.shape[1]] + [w.shape[1] for w in lands]
    widths = [a.shape[1] for a in accs]

    def body(*refs):
        wire0, land, acc, g = refs[0], (None,) + refs[1:n], refs[n:2 * n], refs[2 * n:3 * n]
        land0, own = refs[3 * n], refs[3 * n + 1:4 * n + 1]
        x_send, x_recv, pair_send, pair_recv, local_sems = refs[4 * n + 1:4 * n + 6]
        land = (land0,) + land[1:]
        x, y, c, chips = _place()
        me = 2 * x + y
        exchange = _ChipExchange(wire0, land0, x_send, x_recv)

        def half_rows(t, half):
            return pl.ds(pl.multiple_of(half * halves[t], 8), halves[t])

        def own_copy(t):
            rows = pl.ds(pl.multiple_of((2 * me + c) * halves[t], 8), halves[t])
            return pltpu.make_async_copy(acc[t].at[rows, :], own[t], local_sems.at[t])

        def pair_copy(t, half):
            rows = g[t].at[half_rows(t, half), :]
            return pltpu.make_async_remote_copy(src_ref=rows, dst_ref=rows, send_sem=pair_send.at[t],
                                                recv_sem=pair_recv.at[t], device_id=(x, y, 1 - c), device_id_type=MESH)

        exchange.start()
        for t in range(n):
            own_copy(t).start()
        for t in list(range(1, n)) + [0]:
            own_copy(t).wait()
            if t == 0:
                exchange.wait_recv()

            def chunk(r, carry, t=t):
                src = pl.ds(pl.multiple_of(r * ROW_CHUNK, ROW_CHUNK), ROW_CHUNK)
                dst = pl.ds(pl.multiple_of(c * halves[t] + r * ROW_CHUNK, 8), ROW_CHUNK)
                s = own[t][src, :]
                for d in range(3):
                    s = s + land[t][d, src, :].astype(F32)
                g[t][dst, :] = s
                return carry

            lax.fori_loop(0, halves[t] // ROW_CHUNK, chunk, 0)
            pair_copy(t, c).start()
        for t in range(n):
            pair_copy(t, 1 - c).wait_recv()
        for t in range(n):
            pair_copy(t, c).wait_send()
        exchange.wait_send()

    return pl.pallas_call(
        body, name="grad_finish",
        in_specs=[_vmem()] * n + [_hbm()] * n, out_specs=[_vmem()] * n,
        out_shape=[jax.ShapeDtypeStruct((2 * h, w), F32) for h, w in zip(halves, widths)],
        scratch_shapes=[pltpu.VMEM((3,) + last_wire.shape[1:], last_wire.dtype)]
        + [pltpu.VMEM((h, w), F32) for h, w in zip(halves, widths)]
        + _ChipExchange.sems()
        + [pltpu.SemaphoreType.DMA((n,)), pltpu.SemaphoreType.DMA((n,)), pltpu.SemaphoreType.DMA((n,))],
        compiler_params=pltpu.CompilerParams(vmem_limit_bytes=56 * MIB),
    )(last_wire, *lands, *accs)


_SMALL = ("ln_in_g", "ln_in_b", "b_in", "attn_sinks", "sgu_ln_g", "sgu_ln_b", "sgu_w", "sgu_b", "b_out",
          "ln_mix_g", "ln_mix_b", "ln_ffn_g", "ln_ffn_b")
_VEC_ROW = dict(ln_in_g=0, ln_in_b=1, b_in=2, attn_sinks=4, sgu_ln_g=5, sgu_ln_b=6, b_out=7, ln_mix_g=8, ln_mix_b=9,
                ln_ffn_g=10, ln_ffn_b=11)
_LOSS_ROW = 12
_VEC_ROWS = 16
_MAT_ROWS = N_GRP * BLK + BLK


def _small_update(local, params):
    n_in = 16
    shapes = [params[nm][0].shape for nm in _SMALL]

    def body(*refs):
        (g_ln_in_g, g_ln_in_b, g_bq, g_bkv, g_bsuv, g_sink, g_sln_g, g_sln_b, g_sw, g_sbt, g_bout,
         g_lmg, g_lmb, g_lfg, g_lfb, g_loss) = refs[:n_in]
        prm = refs[n_in:n_in + 3 * len(_SMALL)]
        outs = refs[n_in + 3 * len(_SMALL):n_in + 7 * len(_SMALL) + 1]
        (buf_a, buf_b, pair_a, pair_b, stage_a, stage_b, tot_a, tot_b,
         p1_send, p1_recv, x_send, x_recv, p2_send, p2_recv) = refs[n_in + 7 * len(_SMALL) + 1:]
        x, y, c, chips = _place()
        me = 2 * x + y
        sibling = (x, y, 1 - c)
        half_a, half_b = _VEC_ROWS // 2, _MAT_ROWS // 2

        buf_a[...] = jnp.zeros_like(buf_a)
        for row, ref in ((0, g_ln_in_g), (1, g_ln_in_b), (7, g_bout), (8, g_lmg), (9, g_lmb), (10, g_lfg), (11, g_lfb),
                         (_LOSS_ROW, g_loss)):
            buf_a[row:row + 1, :] = ref[...]
        buf_a[2:3, 0:ATTN_W] = g_bq[...]
        buf_a[2:3, ATTN_W:ATTN_W + 2 * KV_W] = g_bkv[...]
        buf_a[2:3, ATTN_W + 2 * KV_W:D_MODEL] = g_bsuv[:, 0:2 * KV_W]
        buf_a[3:4, 0:2 * SGU_W - 2 * KV_W] = g_bsuv[:, 2 * KV_W:2 * SGU_W]
        buf_a[4:5, 0:128] = g_sink[...]
        buf_a[5:6, 0:SGU_W] = g_sln_g[...]
        buf_a[6:7, 0:SGU_W] = g_sln_b[...]
        for h in range(N_GRP):
            buf_b[h * BLK:(h + 1) * BLK, :] = g_sw[h]
        buf_b[N_GRP * BLK:_MAT_ROWS, :] = g_sbt[...]

        def remote(src, dst, send_sem, recv_sem, to):
            return pltpu.make_async_remote_copy(src_ref=src, dst_ref=dst, send_sem=send_sem, recv_sem=recv_sem,
                                                device_id=to, device_id_type=MESH)

        first = [remote(buf_a, pair_a, p1_send.at[0], p1_recv.at[0], sibling),
                 remote(buf_b, pair_b, p1_send.at[1], p1_recv.at[1], sibling)]
        for cp in first:
            cp.start()
        for cp in first:
            cp.wait()
        rows_a = pl.ds(pl.multiple_of(c * half_a, 8), half_a)
        rows_b = pl.ds(pl.multiple_of(c * half_b, 8), half_b)
        stage_a[me] = buf_a[rows_a, :] + pair_a[rows_a, :]
        stage_b[me] = buf_b[rows_b, :] + pair_b[rows_b, :]

        def chip_copies(d):
            to = (chips[d][0], chips[d][1], c)
            return [remote(stage_a.at[me], stage_a.at[me], x_send.at[2 * d], x_recv.at[2 * d], to),
                    remote(stage_b.at[me], stage_b.at[me], x_send.at[2 * d + 1], x_recv.at[2 * d + 1], to)]

        def chip_arrivals(d):
            slot = 2 * chips[d][0] + chips[d][1]
            to = (chips[d][0], chips[d][1], c)
            return [remote(stage_a.at[slot], stage_a.at[slot], x_send.at[2 * d], x_recv.at[2 * d], to),
                    remote(stage_b.at[slot], stage_b.at[slot], x_send.at[2 * d + 1], x_recv.at[2 * d + 1], to)]

        for d in range(3):
            for cp in chip_copies(d):
                cp.start()
        for d in range(3):
            for cp in chip_arrivals(d):
                cp.wait_recv()
        tot_a[rows_a, :] = ((stage_a[0] + stage_a[1]) + stage_a[2]) + stage_a[3]
        tot_b[rows_b, :] = ((stage_b[0] + stage_b[1]) + stage_b[2]) + stage_b[3]

        second = [remote(tot_a.at[rows_a, :], tot_a.at[rows_a, :], p2_send.at[0], p2_recv.at[0], sibling),
                  remote(tot_b.at[rows_b, :], tot_b.at[rows_b, :], p2_send.at[1], p2_recv.at[1], sibling)]
        for cp in second:
            cp.start()
        other_a = pl.ds(pl.multiple_of((1 - c) * half_a, 8), half_a)
        other_b = pl.ds(pl.multiple_of((1 - c) * half_b, 8), half_b)
        remote(tot_a.at[other_a, :], tot_a.at[other_a, :], p2_send.at[0], p2_recv.at[0], sibling).wait_recv()
        remote(tot_b.at[other_b, :], tot_b.at[other_b, :], p2_send.at[1], p2_recv.at[1], sibling).wait_recv()
        for cp in second:
            cp.wait_send()
        for d in range(3):
            for cp in chip_copies(d):
                cp.wait_send()

        def grad_of(k, name):
            if name == "sgu_w":
                return [tot_b[h * BLK:(h + 1) * BLK, :] for h in range(N_GRP)]
            if name == "sgu_b":
                return jnp.transpose(tot_b[N_GRP * BLK:_MAT_ROWS, :])[0:N_GRP, :]
            row = _VEC_ROW[name]
            if name == "b_in":
                return jnp.concatenate([tot_a[row:row + 1, :], tot_a[row + 1:row + 2, 0:IN_W - D_MODEL]], axis=1)
            return tot_a[row:row + 1, 0:shapes[k][-1]]

        for k, name in enumerate(_SMALL):
            w_ref, m_ref, v_ref = prm[3 * k:3 * k + 3]
            g_out, d_out, m_out, v_out = outs[4 * k:4 * k + 4]
            g = grad_of(k, name)
            if name == "sgu_w":
                for h in range(N_GRP):
                    d_, m_, v_ = _adamw_math(w_ref[h], g[h], m_ref[h], v_ref[h])
                    g_out[h], d_out[h], m_out[h], v_out[h] = g[h], d_, m_, v_
            else:
                d_, m_, v_ = _adamw_math(w_ref[...], g, m_ref[...], v_ref[...])
                g_out[...], d_out[...], m_out[...], v_out[...] = g, d_, m_, v_
        outs[-1][...] = tot_a[_LOSS_ROW:_LOSS_ROW + 1, :]

    ins = [local[k] for k in ("ln_in_g", "ln_in_b", "bq", "bkv", "bsuv", "sink", "sgu_ln_g", "sgu_ln_b", "sgu_w",
                              "sgu_bt", "b_out", "ln_mix_g", "ln_mix_b", "ln_ffn_g", "ln_ffn_b", "loss")]
    ins += [a for nm in _SMALL for a in params[nm]]
    out_shape = [jax.ShapeDtypeStruct(s, F32) for s in shapes for _ in range(4)] + [jax.ShapeDtypeStruct((1, D_MODEL), F32)]
    vec = pltpu.VMEM((_VEC_ROWS, D_MODEL), F32)
    mat = pltpu.VMEM((_MAT_ROWS, 128), F32)
    res = pl.pallas_call(
        body, name="small_update", grid=(1,),
        in_specs=[_const2(a.shape) for a in ins], out_specs=[_const2(s.shape) for s in out_shape], out_shape=out_shape,
        scratch_shapes=[vec, mat, vec, mat, pltpu.VMEM((N_CHIP, _VEC_ROWS // 2, D_MODEL), F32),
                        pltpu.VMEM((N_CHIP, _MAT_ROWS // 2, 128), F32), vec, mat,
                        pltpu.SemaphoreType.DMA((2,)), pltpu.SemaphoreType.DMA((2,)), pltpu.SemaphoreType.DMA((6,)),
                        pltpu.SemaphoreType.DMA((6,)), pltpu.SemaphoreType.DMA((2,)), pltpu.SemaphoreType.DMA((2,))],
        compiler_params=pltpu.CompilerParams(vmem_limit_bytes=32 * MIB),
    )(*ins)
    return {nm: tuple(res[4 * k:4 * k + 4]) for k, nm in enumerate(_SMALL)}, res[-1]


def _elementwise(name, fn, ins, out_dtypes, tile_rows=256):
    shape = ins[0].shape
    lead = shape[:-2]
    rows, cols = shape[-2:]
    tr = _tile(rows, tile_rows)
    n_lead = math.prod(lead)
    nr = rows // tr
    flat = [a.reshape((n_lead, rows, cols)) for a in ins]

    def body(*refs):
        outs = fn(*[r[0] for r in refs[:len(ins)]])
        for o_ref, o in zip(refs[len(ins):], outs):
            o_ref[0] = o.astype(o_ref.dtype)

    spec = pl.BlockSpec((1, tr, cols), lambda i: (i // nr, i % nr, 0))
    res = pl.pallas_call(
        body, name=name, grid=(n_lead * nr,),
        in_specs=[spec] * len(ins), out_specs=[spec] * len(out_dtypes),
        out_shape=[jax.ShapeDtypeStruct((n_lead, rows, cols), dt) for dt in out_dtypes],
        compiler_params=_params(32),
    )(*flat)
    return [r.reshape(shape) for r in res]


def _adamw_math(w, g, m, v):
    m = ADAM_B1 * m + (1.0 - ADAM_B1) * g
    v = ADAM_B2 * v + (1.0 - ADAM_B2) * (g * g)
    m_hat = m / (1.0 - ADAM_B1 ** ADAM_STEP)
    v_hat = v / (1.0 - ADAM_B2 ** ADAM_STEP)
    delta = -ADAM_LR * (m_hat / (jnp.sqrt(v_hat) + ADAM_EPS) + ADAM_WD * w)
    return delta, m, v


def _adamw(name, w, g, m, v, tile_rows=256):
    return _elementwise(name, _adamw_math, [w, g, m, v], [F32, F32, F32], tile_rows)


def kernel(x, positions, ln_in_g, ln_in_b, w_in, b_in, attn_sinks, sgu_ln_g, sgu_ln_b, sgu_w, sgu_b, w_out, b_out, ln_mix_g, ln_mix_b, w_gate, w_up, w_down, ln_ffn_g, ln_ffn_b, loss_target, m_ln_in_g, m_ln_in_b, m_w_in, m_b_in, m_attn_sinks, m_sgu_ln_g, m_sgu_ln_b, m_sgu_w, m_sgu_b, m_w_out, m_b_out, m_ln_mix_g, m_ln_mix_b, m_w_gate, m_w_up, m_w_down, m_ln_ffn_g, m_ln_ffn_b, v_ln_in_g, v_ln_in_b, v_w_in, v_b_in, v_attn_sinks, v_sgu_ln_g, v_sgu_ln_b, v_sgu_w, v_sgu_b, v_w_out, v_b_out, v_ln_mix_g, v_ln_mix_b, v_w_gate, v_w_up, v_w_down, v_ln_ffn_g, v_ln_ffn_b):
    weights = dict(ln_in_g=ln_in_g, ln_in_b=ln_in_b, w_in=w_in, b_in=b_in, attn_sinks=attn_sinks, sgu_ln_g=sgu_ln_g,
                   sgu_ln_b=sgu_ln_b, sgu_w=sgu_w, sgu_b=sgu_b, w_out=w_out, b_out=b_out, ln_mix_g=ln_mix_g,
                   ln_mix_b=ln_mix_b, w_gate=w_gate, w_up=w_up, w_down=w_down, ln_ffn_g=ln_ffn_g, ln_ffn_b=ln_ffn_b)
    mom_m = dict(ln_in_g=m_ln_in_g, ln_in_b=m_ln_in_b, w_in=m_w_in, b_in=m_b_in, attn_sinks=m_attn_sinks,
                 sgu_ln_g=m_sgu_ln_g, sgu_ln_b=m_sgu_ln_b, sgu_w=m_sgu_w, sgu_b=m_sgu_b, w_out=m_w_out, b_out=m_b_out,
                 ln_mix_g=m_ln_mix_g, ln_mix_b=m_ln_mix_b, w_gate=m_w_gate, w_up=m_w_up, w_down=m_w_down,
                 ln_ffn_g=m_ln_ffn_g, ln_ffn_b=m_ln_ffn_b)
    mom_v = dict(ln_in_g=v_ln_in_g, ln_in_b=v_ln_in_b, w_in=v_w_in, b_in=v_b_in, attn_sinks=v_attn_sinks,
                 sgu_ln_g=v_sgu_ln_g, sgu_ln_b=v_sgu_ln_b, sgu_w=v_sgu_w, sgu_b=v_sgu_b, w_out=v_w_out, b_out=v_b_out,
                 ln_mix_g=v_ln_mix_g, ln_mix_b=v_ln_mix_b, w_gate=v_w_gate, w_up=v_w_up, w_down=v_w_down,
                 ln_ffn_g=v_ln_ffn_g, ln_ffn_b=v_ln_ffn_b)
    order = list(weights)
    big = ("w_in", "w_out", "w_gate", "w_up", "w_down")

    s_len = x.shape[1]
    xs = x.reshape(s_len, D_MODEL)
    tgt = loss_target.reshape(s_len, D_MODEL)
    pos_col = positions.reshape(s_len, 1)
    g0, b0 = ln_in_g.reshape(1, D_MODEL), ln_in_b.reshape(1, D_MODEL)
    sinks = attn_sinks.reshape(N_Q)
    sgu_w3 = sgu_w.reshape(N_GRP, BLK, BLK)
    sgu_bt = sgu_b.reshape(N_GRP, BLK).T

    col_sharded = ("w_in", "w_gate", "w_up")

    def rowmajor(name, a):
        return jnp.swapaxes(a[0], 0, 1) if name in col_sharded else a[0]

    def as_given(name, a):
        return (jnp.swapaxes(a, 0, 1) if name in col_sharded else a)[None]

    shards = [rowmajor(n, weights[n]) for n in big]
    (gw_in,) = _gather_weights(shards[0:1])
    w_in_full = gw_in.reshape(IN_W, D_MODEL)

    q, k, v, su, sv, tc, t1, t2, gw_out = _ln_inproj(xs, pos_col, g0, b0, w_in_full, b_in, shards[1:2])
    mc, gw_gate = _mixer_fwd(q, k, v, su, sv, sinks, sgu_ln_g, sgu_ln_b, sgu_w3, sgu_bt, shards[2:3])
    w_out_full = gw_out.reshape(D_MODEL, D_MODEL)
    r1, gw_up = _outproj(mc, w_out_full, b_out, xs, g0, b0, shards[3:4])
    gact, uact, gw_down = _ffn_up(r1, ln_mix_g, ln_mix_b, gw_gate, gw_up, shards[4:5])
    dr2, loss_cols, d_ln_ffn_g, d_ln_ffn_b = _ffn_down_loss(gact, uact, gw_down, r1, ln_mix_g, ln_mix_b,
                                                            ln_ffn_g, ln_ffn_b, tgt)

    dg, du, acc_down, wire_down = _ffn_bwd_a(dr2, gact, uact, gw_down)
    dh1a, acc_gate, wire_gate, land_down = _ffn_bwd_g(dr2, dg, r1, ln_mix_g, ln_mix_b, gw_gate, wire_down)
    dr1, acc_up, wire_up, d_ln_mix_g, d_ln_mix_b, land_gate = _ffn_bwd_u(dh1a, du, r1, ln_mix_g, ln_mix_b, gw_up,
                                                                         wire_gate)
    dmc, acc_out, wire_out, d_b_out, land_up = _outproj_bwd(dr1, mc, w_out_full, wire_up)
    (dq, dkv, dsuv, dbq, dbkv, dbsuv, d_sink, d_sgu_ln_g, d_sgu_ln_b, d_sgu_w, d_sgu_bt, land_out) = _mixer_bwd(
        q, k, v, su, sv, dmc, tc, t1, t2, sinks, sgu_ln_g, sgu_ln_b, sgu_w3, sgu_bt, wire_out)
    grad_x, acc_in, wire_in, d_ln_in_g, d_ln_in_b = _inproj_bwd(dq, dkv, dsuv, dr1, xs, g0, b0, w_in_full)

    reduced = _grad_finish(wire_in, [land_out, land_gate, land_up, land_down],
                           [acc_in, acc_out, acc_gate, acc_up, acc_down])
    small_shape = dict(ln_in_g=(1, D_MODEL), ln_in_b=(1, D_MODEL), sgu_w=(N_GRP, BLK, BLK), sgu_b=(N_GRP, BLK))
    small_local = dict(
        ln_in_g=d_ln_in_g, ln_in_b=d_ln_in_b, bq=dbq, bkv=dbkv, bsuv=dbsuv, sink=d_sink, sgu_ln_g=d_sgu_ln_g,
        sgu_ln_b=d_sgu_ln_b, sgu_w=d_sgu_w, sgu_bt=d_sgu_bt, b_out=d_b_out, ln_mix_g=d_ln_mix_g, ln_mix_b=d_ln_mix_b,
        ln_ffn_g=d_ln_ffn_g, ln_ffn_b=d_ln_ffn_b, loss=loss_cols)
    small_params = {nm: tuple(src[nm].reshape(small_shape.get(nm, src[nm].shape)) for src in (weights, mom_m, mom_v))
                    for nm in _SMALL}
    small_out, loss_sum = _small_update(small_local, small_params)
    loss = jnp.sum(loss_sum) * (0.5 / D_MODEL)
    grads, delta, new_m, new_v = {}, {}, {}, {}
    for nm in _SMALL:
        grads[nm], delta[nm], new_m[nm], new_v[nm] = (a.reshape(weights[nm].shape) for a in small_out[nm])

    for t, name in enumerate(big):
        d_, m_, v_ = _adamw("adamw_" + name, shards[t], reduced[t], rowmajor(name, mom_m[name]),
                            rowmajor(name, mom_v[name]))
        grads[name] = as_given(name, reduced[t])
        delta[name], new_m[name], new_v[name] = as_given(name, d_), as_given(name, m_), as_given(name, v_)

    return (loss, grad_x.reshape(x.shape), *[grads[n] for n in order], *[delta[n] for n in order],
            *[new_m[n] for n in order], *[new_v[n] for n in order])
```

```python
import functools
import math

import jax
import jax.numpy as jnp
from jax import lax
from jax.experimental import pallas as pl
from jax.experimental.pallas import tpu as pltpu

F32 = jnp.float32
_MXU = jnp.bfloat16
_WIRE = jnp.bfloat16
_ACT = jnp.bfloat16

D_MODEL = 1024
ATTN_W = 512
SGU_W = 512
HEAD_DIM = 64
N_Q = 8
N_KV = 2
Q_PER_KV = 4
KV_W = 128
BLK = 128
ROT_DIM = 16
ROPE_THETA = 500000.0
N_GRP = 4
GRP_DIM = 128
D_FF = 2816
IN_W = 1792
LN_EPS = 1e-5
ALPHA = 2.0 ** 0.25
N_CHIP = 4
FF_SH = D_FF // N_CHIP
IN_SH = IN_W // N_CHIP
OUT_SH = D_MODEL // N_CHIP
ROW_CHUNK = 32

ADAM_LR = 0.001
ADAM_B1 = 0.9
ADAM_B2 = 0.999
ADAM_EPS = 1e-08
ADAM_WD = 0.01
ADAM_STEP = 10

SQRT_HALF = 0.7071067811865476
INV_SQRT_2PI = 0.3989422804014327
MESH_AXES = ("x", "y", "c")
MESH = pl.DeviceIdType.MESH
MIB = 2 ** 20


def _vmem():
    return pl.BlockSpec(memory_space=pltpu.VMEM)


def _smem():
    return pl.BlockSpec(memory_space=pltpu.SMEM)


def _hbm():
    return pl.BlockSpec(memory_space=pl.ANY)


def _hbm_shape(shape, dtype):
    return pltpu.HBM(shape, dtype)


def _in_hbm(a):
    return pltpu.with_memory_space_constraint(a, pltpu.HBM)


def _params(vmem_mib=48):
    return pltpu.CompilerParams(dimension_semantics=("arbitrary",), vmem_limit_bytes=vmem_mib * MIB)


def _tile(n, cap):
    if n <= cap:
        return n
    for t in range(cap - cap % 16, 0, -16):
        if n % t == 0:
            return t
    raise ValueError((n, cap))


def _rows(tm, width):
    return pl.BlockSpec((tm, width), lambda i: (i, 0))


def _const2(shape):
    return pl.BlockSpec(shape, lambda i: (0,) * len(shape))


def _ln(x, g, b):
    mu = jnp.mean(x, axis=-1, keepdims=True)
    xc = x - mu
    var = jnp.mean(xc * xc, axis=-1, keepdims=True)
    rstd = lax.rsqrt(var + LN_EPS)
    xhat = xc * rstd
    return xhat * g + b, xhat, rstd


def _ln_bwd(dy, xhat, rstd, g):
    gdy = dy * g
    m1 = jnp.mean(gdy, axis=-1, keepdims=True)
    m2 = jnp.mean(gdy * xhat, axis=-1, keepdims=True)
    return rstd * (gdy - m1 - xhat * m2)


def _colsum(a):
    return jnp.sum(a, axis=0, keepdims=True)


def _gelu(x):
    return 0.5 * x * (1.0 + lax.erf(x * SQRT_HALF))


def _gelu_grad(x):
    return 0.5 * (1.0 + lax.erf(x * SQRT_HALF)) + x * jnp.exp(-0.5 * x * x) * INV_SQRT_2PI


def _dot(a, b):
    return jnp.dot(a, b, preferred_element_type=F32)


def _dot_nt(a, b):
    return lax.dot_general(a, b, (((1,), (1,)), ((), ())), preferred_element_type=F32)


def _dot_tn(a, b):
    return lax.dot_general(a, b, (((0,), (0,)), ((), ())), preferred_element_type=F32)


def _rope(t, tc, t1, t2):
    n = t.shape[1]
    rep = n // 128
    if rep > 1:
        tc, t1, t2 = (jnp.tile(a, (1, rep)) for a in (tc, t1, t2))
    return t * tc + pltpu.roll(t, n - 8, 1) * t1 + pltpu.roll(t, 8, 1) * t2


def _rope_bwd(d, tc, t1, t2):
    n = d.shape[1]
    rep = n // 128
    if rep > 1:
        tc, t1, t2 = (jnp.tile(a, (1, rep)) for a in (tc, t1, t2))
    return d * tc + pltpu.roll(d * t1, 8, 1) + pltpu.roll(d * t2, n - 8, 1)


def _band_mask(first_block):
    qi = lax.broadcasted_iota(jnp.int32, (BLK, 2 * BLK), 0)
    kj = lax.broadcasted_iota(jnp.int32, (BLK, 2 * BLK), 1)
    shut = jnp.where(first_block, 2 * BLK, 0)
    prev_ok = jnp.logical_and(kj < BLK, kj > qi + shut)
    cur_ok = jnp.logical_and(kj >= BLK, (kj - BLK) <= qi)
    return jnp.logical_or(prev_ok, cur_ok)


def _causal_w(w_ref, h):
    t = lax.broadcasted_iota(jnp.int32, (BLK, BLK), 0)
    s = lax.broadcasted_iota(jnp.int32, (BLK, BLK), 1)
    return jnp.where(s <= t, w_ref[h], 0.0)


def _lane_put(vals, width):
    rows = vals[0].shape[0]
    lane = lax.broadcasted_iota(jnp.int32, (rows, width), 1)
    out = jnp.zeros((rows, width), F32)
    for k, v in enumerate(vals):
        out = out + jnp.where(lane == k, v, 0.0)
    return out


def _rope_consts():
    lane = jnp.arange(128) % HEAD_DIM
    inv_freq = ROPE_THETA ** (-jnp.arange(0, ROT_DIM, 2, dtype=F32) / ROT_DIM)
    rot = lane < ROT_DIM
    freq = jnp.where(rot, inv_freq[lane % (ROT_DIM // 2)], 0.0)
    rows = [freq, rot.astype(F32), 1.0 - rot.astype(F32), (lane < ROT_DIM // 2).astype(F32),
            jnp.logical_and(lane >= ROT_DIM // 2, rot).astype(F32)]
    rows += [jnp.zeros((128,), F32)] * 3
    return jnp.stack(rows).astype(F32)


def _ln_inproj(x, pos_col, g0, b0, w_in, b_in, shards):
    s_len = x.shape[0]
    tm = _tile(s_len, 512)

    n = len(shards)

    def body(x_ref, pos_ref, g_ref, b_ref, w_ref, bi_ref, rc_ref, *rest):
        q_ref, k_ref, v_ref, su_ref, sv_ref, tc_ref, t1_ref, t2_ref = rest[n:n + 8]
        gathered = rest[n + 8:2 * n + 8]
        gather = _Gather(rest[:n], rest[2 * n + 8:3 * n + 8], rest[3 * n + 8], rest[3 * n + 9])
        flush_sems = rest[3 * n + 10]
        i = pl.program_id(0)

        @pl.when(i == 0)
        def _():
            gather.start()

        h0, _, _ = _ln(x_ref[...], g_ref[...], b_ref[...])
        proj = _dot_nt(h0.astype(_MXU), w_ref[...]) + bi_ref[...]
        ang = pos_ref[...].astype(F32) * rc_ref[0:1, :]
        cs = jnp.cos(ang)
        sn = jnp.sin(ang)
        tc = cs * rc_ref[1:2, :] + rc_ref[2:3, :]
        t1 = -sn * rc_ref[3:4, :]
        t2 = sn * rc_ref[4:5, :]
        tc_ref[...] = tc
        t1_ref[...] = t1
        t2_ref[...] = t2
        q = _rope(proj[:, 0:ATTN_W], tc, t1, t2) * (HEAD_DIM ** -0.5)
        q_ref[...] = q.astype(_MXU)
        k_ref[...] = _rope(proj[:, ATTN_W:ATTN_W + KV_W], tc, t1, t2).astype(_MXU)
        v_ref[...] = proj[:, ATTN_W + KV_W:ATTN_W + 2 * KV_W].astype(_MXU)
        su_ref[...] = proj[:, ATTN_W + 2 * KV_W:ATTN_W + 2 * KV_W + SGU_W]
        sv_ref[...] = proj[:, ATTN_W + 2 * KV_W + SGU_W:IN_W]

        @pl.when(i == pl.num_programs(0) - 1)
        def _():
            gather.finish()
            gather.flush(gathered, flush_sems)

    sd = _hbm_shape
    return pl.pallas_call(
        body, name="ln_inproj", grid=(s_len // tm,),
        in_specs=[_rows(tm, D_MODEL), _rows(tm, 1), _const2((1, D_MODEL)), _const2((1, D_MODEL)), _vmem(),
                  _const2((1, IN_W)), _const2((8, 128))] + [_vmem()] * n,
        out_specs=[_rows(tm, ATTN_W), _rows(tm, KV_W), _rows(tm, KV_W), _rows(tm, SGU_W), _rows(tm, SGU_W),
                   _rows(tm, 128), _rows(tm, 128), _rows(tm, 128)] + [_hbm()] * n,
        out_shape=[sd((s_len, ATTN_W), _MXU), sd((s_len, KV_W), _MXU), sd((s_len, KV_W), _MXU),
                   sd((s_len, SGU_W), F32), sd((s_len, SGU_W), F32),
                   sd((s_len, 128), F32), sd((s_len, 128), F32), sd((s_len, 128), F32)]
        + _Gather.out_shapes(shards, _hbm_shape),
        scratch_shapes=_Gather.scratch(shards),
        compiler_params=_params(56),
    )(x, pos_col, g0, b0, w_in, b_in, _rope_consts(), *shards)


def _attn_probs(qh, kh, sink, allowed):
    s = jnp.where(allowed, _dot_nt(qh, kh), -1e30)
    m = jnp.maximum(jnp.max(s, axis=-1, keepdims=True), sink)
    p = jnp.exp(s - m)
    ps = jnp.exp(sink - m)
    inv = 1.0 / (jnp.sum(p, axis=-1, keepdims=True) + ps)
    return p * inv, ps * inv


def _sgu_fwd(su, sv, lg, lb, w_ref, bt_ref):
    u = _gelu(su)
    vv, vhat, rstd = _ln(_gelu(sv), lg, lb)
    vvb = vv.astype(_MXU)
    wcs, mixed = [], []
    for h in range(N_GRP):
        wc = _causal_w(w_ref, h).astype(_MXU)
        wcs.append(wc)
        mixed.append(_dot(wc, vvb[:, h * GRP_DIM:(h + 1) * GRP_DIM]) + bt_ref[:, h:h + 1])
    return u, vhat, rstd, vvb, wcs, jnp.concatenate(mixed, axis=1)


def _prev_map(i):
    return (jnp.maximum(i - 1, 0), 0)


def _mixer_fwd(q, k, v, su, sv, sinks, sg, sb, sgu_w, sgu_bt, shards):
    s_len = q.shape[0]
    nb = s_len // BLK
    n = len(shards)

    def body(q_ref, kc_ref, kp_ref, vc_ref, vp_ref, su_ref, sv_ref, sink_ref, lg_ref, lb_ref, w_ref, bt_ref, *rest):
        mc_ref = rest[n]
        gathered = rest[n + 1:2 * n + 1]
        gather = _Gather(rest[:n], rest[2 * n + 1:3 * n + 1], rest[3 * n + 1], rest[3 * n + 2])
        flush_sems = rest[3 * n + 3]
        i = pl.program_id(0)

        @pl.when(i == 0)
        def _():
            gather.start()

        @pl.when(i == nb - 1)
        def _():
            gather.finish()
            gather.flush(gathered, flush_sems)

        allowed = _band_mask(i == 0)
        kb = jnp.concatenate([kp_ref[...], kc_ref[...]], axis=0)
        vb = jnp.concatenate([vp_ref[...], vc_ref[...]], axis=0)
        qv = q_ref[...]
        outs = []
        for h in range(N_Q):
            g = h // Q_PER_KV
            kh = kb[:, g * HEAD_DIM:(g + 1) * HEAD_DIM]
            vh = vb[:, g * HEAD_DIM:(g + 1) * HEAD_DIM]
            probs, _ = _attn_probs(qv[:, h * HEAD_DIM:(h + 1) * HEAD_DIM], kh, sink_ref[h], allowed)
            outs.append(_dot(probs.astype(_MXU), vh))
        u, _, _, _, _, mixed = _sgu_fwd(su_ref[...], sv_ref[...], lg_ref[...], lb_ref[...], w_ref, bt_ref)
        mc_ref[...] = jnp.concatenate(outs + [u * mixed], axis=1).astype(_MXU)

    cur = lambda w: pl.BlockSpec((BLK, w), lambda i: (i, 0))
    prev = lambda w: pl.BlockSpec((BLK, w), _prev_map)
    return pl.pallas_call(
        body, name="mixer_fwd", grid=(nb,),
        in_specs=[cur(ATTN_W), cur(KV_W), prev(KV_W), cur(KV_W), prev(KV_W), cur(SGU_W), cur(SGU_W), _smem(),
                  _const2((1, SGU_W)), _const2((1, SGU_W)), _const2((N_GRP, BLK, BLK)), _const2((BLK, N_GRP))]
        + [_vmem()] * n,
        out_specs=[cur(D_MODEL)] + [_hbm()] * n,
        out_shape=[_hbm_shape((s_len, D_MODEL), _MXU)] + _Gather.out_shapes(shards, _hbm_shape),
        scratch_shapes=_Gather.scratch(shards),
        compiler_params=_params(48),
    )(q, k, k, v, v, su, sv, sinks, sg, sb, sgu_w, sgu_bt, *shards)


def _outproj(mc, w_out, b_out, x, g0, b0, shards):
    s_len = x.shape[0]
    tm = _tile(s_len, 512)
    n = len(shards)

    def body(mc_ref, w_ref, bo_ref, x_ref, g_ref, b_ref, *rest):
        r1_ref = rest[n]
        gathered = rest[n + 1:2 * n + 1]
        gather = _Gather(rest[:n], rest[2 * n + 1:3 * n + 1], rest[3 * n + 1], rest[3 * n + 2])
        flush_sems = rest[3 * n + 3]
        i = pl.program_id(0)

        @pl.when(i == 0)
        def _():
            gather.start()

        h0, _, _ = _ln(x_ref[...], g_ref[...], b_ref[...])
        r1_ref[...] = ALPHA * h0 + (_dot(mc_ref[...], w_ref[...]) + bo_ref[...])

        @pl.when(i == pl.num_programs(0) - 1)
        def _():
            gather.finish()
            gather.flush(gathered, flush_sems)

    return pl.pallas_call(
        body, name="outproj", grid=(s_len // tm,),
        in_specs=[_rows(tm, D_MODEL), _vmem(), _const2((1, D_MODEL)), _rows(tm, D_MODEL),
                  _const2((1, D_MODEL)), _const2((1, D_MODEL))] + [_vmem()] * n,
        out_specs=[_rows(tm, D_MODEL)] + [_hbm()] * n,
        out_shape=[_hbm_shape((s_len, D_MODEL), F32)] + _Gather.out_shapes(shards, _hbm_shape),
        scratch_shapes=_Gather.scratch(shards),
        compiler_params=_params(40),
    )(mc, w_out, b_out, x, g0, b0, *shards)


def _ffn_spec(tm):
    return pl.BlockSpec((N_CHIP, tm, FF_SH), lambda i: (0, i, 0))


def _ffn_up(r1, g1, b1, wg, wu, shards):
    s_len = r1.shape[0]
    tm = _tile(s_len, 512)
    n = len(shards)

    def body(r1_ref, g_ref, b_ref, wg_ref, wu_ref, *rest):
        go_ref, uo_ref = rest[n:n + 2]
        gathered = rest[n + 2:2 * n + 2]
        gather = _Gather(rest[:n], rest[2 * n + 2:3 * n + 2], rest[3 * n + 2], rest[3 * n + 3])
        flush_sems = rest[3 * n + 4]
        i = pl.program_id(0)

        @pl.when(i == 0)
        def _():
            gather.start()

        h1, _, _ = _ln(r1_ref[...], g_ref[...], b_ref[...])
        h1b = h1.astype(_MXU)
        for j in range(N_CHIP):
            go_ref[j] = _dot_nt(h1b, wg_ref[j]).astype(_ACT)
            uo_ref[j] = _dot_nt(h1b, wu_ref[j]).astype(_ACT)

        @pl.when(i == pl.num_programs(0) - 1)
        def _():
            gather.finish()
            gather.flush(gathered, flush_sems)

    sd = _hbm_shape((N_CHIP, s_len, FF_SH), _ACT)
    return pl.pallas_call(
        body, name="ffn_up", grid=(s_len // tm,),
        in_specs=[_rows(tm, D_MODEL), _const2((1, D_MODEL)), _const2((1, D_MODEL)), _vmem(), _vmem()] + [_vmem()] * n,
        out_specs=[_ffn_spec(tm), _ffn_spec(tm)] + [_hbm()] * n,
        out_shape=[sd, sd] + _Gather.out_shapes(shards, _hbm_shape),
        scratch_shapes=_Gather.scratch(shards),
        compiler_params=_params(56),
    )(r1, g1, b1, wg, wu, *shards)


def _silu_parts(g):
    sg = 1.0 / (1.0 + jnp.exp(-g))
    return g * sg, sg


def _ffn_down_loss(gact, uact, wd, r1, g1, b1, g2, b2, target):
    s_len = r1.shape[0]
    tm = _tile(s_len, 512)

    def body(g_ref, u_ref, wd_ref, r1_ref, g1_ref, b1_ref, g2_ref, b2_ref, t_ref,
             dr2_ref, loss_ref, dg2_ref, db2_ref):
        i = pl.program_id(0)
        f = jnp.zeros((tm, D_MODEL), F32)
        for j in range(N_CHIP):
            silu, _ = _silu_parts(g_ref[j].astype(F32))
            f = f + _dot((silu * u_ref[j].astype(F32)).astype(_MXU), wd_ref[j])
        h1, _, _ = _ln(r1_ref[...], g1_ref[...], b1_ref[...])
        h2, r2hat, rstd2 = _ln(ALPHA * h1 + f, g2_ref[...], b2_ref[...])
        diff = h2 - t_ref[...]
        dh2 = diff * (1.0 / D_MODEL)

        @pl.when(i == 0)
        def _():
            loss_ref[...] = jnp.zeros_like(loss_ref)
            dg2_ref[...] = jnp.zeros_like(dg2_ref)
            db2_ref[...] = jnp.zeros_like(db2_ref)

        loss_ref[...] += _colsum(diff * diff)
        dg2_ref[...] += _colsum(dh2 * r2hat)
        db2_ref[...] += _colsum(dh2)
        dr2_ref[...] = _ln_bwd(dh2, r2hat, rstd2, g2_ref[...])

    vec = _hbm_shape((1, D_MODEL), F32)
    c = _const2((1, D_MODEL))
    return pl.pallas_call(
        body, name="ffn_down_loss", grid=(s_len // tm,),
        in_specs=[_ffn_spec(tm), _ffn_spec(tm), _vmem(), _rows(tm, D_MODEL), c, c, c, c, _rows(tm, D_MODEL)],
        out_specs=[_rows(tm, D_MODEL), c, c, c],
        out_shape=[_hbm_shape((s_len, D_MODEL), F32), vec, vec, vec],
        compiler_params=_params(48),
    )(gact, uact, wd, r1, g1, b1, g2, b2, target)


def _ffn_bwd_a(dr2, gact, uact, wd):
    s_len = dr2.shape[0]
    tm = _tile(s_len, 512)

    def body(dr2_ref, g_ref, u_ref, wd_ref, dg_ref, du_ref, dwd_ref, wire_ref, land_ref, send_sem, recv_sem):
        i = pl.program_id(0)

        @pl.when(i == 0)
        def _():
            dwd_ref[...] = jnp.zeros_like(dwd_ref)

        dfb = dr2_ref[...].astype(_MXU)
        for j in range(N_CHIP):
            g = g_ref[j].astype(F32)
            u = u_ref[j].astype(F32)
            silu, sg = _silu_parts(g)
            da = _dot_nt(dfb, wd_ref[j])
            dg_ref[j] = (da * u * (sg * (1.0 + g * (1.0 - sg)))).astype(_MXU)
            du_ref[j] = (da * silu).astype(_MXU)
            dwd_ref[j * FF_SH:(j + 1) * FF_SH, :] += _dot_tn((silu * u).astype(_MXU), dfb)

        @pl.when(i == pl.num_programs(0) - 1)
        def _():
            _pair_reduce(dwd_ref, land_ref, wire_ref, send_sem, recv_sem)

    sd = _hbm_shape((N_CHIP, s_len, FF_SH), _MXU)
    return pl.pallas_call(
        body, name="ffn_bwd_a", grid=(s_len // tm,),
        in_specs=[_rows(tm, D_MODEL), _ffn_spec(tm), _ffn_spec(tm), _vmem()],
        out_specs=[_ffn_spec(tm), _ffn_spec(tm), _vmem(), _vmem()],
        out_shape=[sd, sd, jax.ShapeDtypeStruct((D_FF, D_MODEL), F32),
                   jax.ShapeDtypeStruct((N_CHIP, FF_SH // 2, D_MODEL), _WIRE)],
        scratch_shapes=_pair_scratch((N_CHIP, FF_SH // 2, D_MODEL)),
        compiler_params=_params(58),
    )(dr2, gact, uact, wd)


def _ffn_bwd_g(dr2, dg, r1, g1, b1, wg, prev_wire):
    s_len = dr2.shape[0]
    tm = _tile(s_len, 512)

    def body(dr2_ref, dg_ref, r1_ref, g1_ref, b1_ref, wg_ref, pw_ref, dh1_ref, dwg_ref, wire_ref, pl_ref,
             land_ref, send_sem, recv_sem, xl_ref, x_send, x_recv, x_flush):
        i = pl.program_id(0)
        exchange = _ChipExchange(pw_ref, xl_ref, x_send, x_recv)

        @pl.when(i == 0)
        def _():
            exchange.start()
            dwg_ref[...] = jnp.zeros_like(dwg_ref)

        h1, _, _ = _ln(r1_ref[...], g1_ref[...], b1_ref[...])
        h1b = h1.astype(_MXU)
        dh1 = ALPHA * dr2_ref[...]
        for j in range(N_CHIP):
            dgj = dg_ref[j]
            dh1 = dh1 + _dot(dgj, wg_ref[j])
            dwg_ref[j * FF_SH:(j + 1) * FF_SH, :] += _dot_tn(dgj, h1b)
        dh1_ref[...] = dh1

        @pl.when(i == pl.num_programs(0) - 1)
        def _():
            _pair_reduce(dwg_ref, land_ref, wire_ref, send_sem, recv_sem)
            exchange.finish_to(pl_ref, x_flush)

    c = _const2((1, D_MODEL))
    return pl.pallas_call(
        body, name="ffn_bwd_g", grid=(s_len // tm,),
        in_specs=[_rows(tm, D_MODEL), _ffn_spec(tm), _rows(tm, D_MODEL), c, c, _vmem(), _vmem()],
        out_specs=[_rows(tm, D_MODEL), _vmem(), _vmem(), _hbm()],
        out_shape=[_hbm_shape((s_len, D_MODEL), F32), jax.ShapeDtypeStruct((D_FF, D_MODEL), F32),
                   jax.ShapeDtypeStruct((N_CHIP, FF_SH // 2, D_MODEL), _WIRE), _ChipExchange.land_shape(prev_wire)],
        scratch_shapes=_pair_scratch((N_CHIP, FF_SH // 2, D_MODEL)) + _ChipExchange.scratch(prev_wire),
        compiler_params=_params(58),
    )(dr2, dg, r1, g1, b1, wg, prev_wire)


def _ffn_bwd_u(dh1a, du, r1, g1, b1, wu, prev_wire):
    s_len = dh1a.shape[0]
    tm = _tile(s_len, 512)

    def body(dh1_ref, du_ref, r1_ref, g1_ref, b1_ref, wu_ref, pw_ref,
             dr1_ref, dwu_ref, wire_ref, dg1_ref, db1_ref, pl_ref,
             land_ref, send_sem, recv_sem, xl_ref, x_send, x_recv, x_flush):
        i = pl.program_id(0)
        exchange = _ChipExchange(pw_ref, xl_ref, x_send, x_recv)

        @pl.when(i == 0)
        def _():
            exchange.start()
            dwu_ref[...] = jnp.zeros_like(dwu_ref)
            dg1_ref[...] = jnp.zeros_like(dg1_ref)
            db1_ref[...] = jnp.zeros_like(db1_ref)

        h1, r1hat, rstd1 = _ln(r1_ref[...], g1_ref[...], b1_ref[...])
        h1b = h1.astype(_MXU)
        dh1 = dh1_ref[...]
        for j in range(N_CHIP):
            duj = du_ref[j]
            dh1 = dh1 + _dot(duj, wu_ref[j])
            dwu_ref[j * FF_SH:(j + 1) * FF_SH, :] += _dot_tn(duj, h1b)
        dg1_ref[...] += _colsum(dh1 * r1hat)
        db1_ref[...] += _colsum(dh1)
        dr1_ref[...] = _ln_bwd(dh1, r1hat, rstd1, g1_ref[...])

        @pl.when(i == pl.num_programs(0) - 1)
        def _():
            _pair_reduce(dwu_ref, land_ref, wire_ref, send_sem, recv_sem)
            exchange.finish_to(pl_ref, x_flush)

    vec = _hbm_shape((1, D_MODEL), F32)
    c = _const2((1, D_MODEL))
    return pl.pallas_call(
        body, name="ffn_bwd_u", grid=(s_len // tm,),
        in_specs=[_rows(tm, D_MODEL), _ffn_spec(tm), _rows(tm, D_MODEL), c, c, _vmem(), _vmem()],
        out_specs=[_rows(tm, D_MODEL), _vmem(), _vmem(), c, c, _hbm()],
        out_shape=[_hbm_shape((s_len, D_MODEL), F32), jax.ShapeDtypeStruct((D_FF, D_MODEL), F32),
                   jax.ShapeDtypeStruct((N_CHIP, FF_SH // 2, D_MODEL), _WIRE), vec, vec,
                   _ChipExchange.land_shape(prev_wire)],
        scratch_shapes=_pair_scratch((N_CHIP, FF_SH // 2, D_MODEL)) + _ChipExchange.scratch(prev_wire),
        compiler_params=_params(58),
    )(dh1a, du, r1, g1, b1, wu, prev_wire)


def _outproj_bwd(dr1, mc, w_out, prev_wire):
    s_len = dr1.shape[0]
    tm = _tile(s_len, 512)

    def body(dr1_ref, mc_ref, w_ref, pw_ref, dmc_ref, dw_ref, wire_ref, db_ref, pl_ref,
             land_ref, send_sem, recv_sem, xl_ref, x_send, x_recv, x_flush):
        i = pl.program_id(0)
        exchange = _ChipExchange(pw_ref, xl_ref, x_send, x_recv)

        @pl.when(i == 0)
        def _():
            exchange.start()
            dw_ref[...] = jnp.zeros_like(dw_ref)
            db_ref[...] = jnp.zeros_like(db_ref)

        d = dr1_ref[...]
        db_ref[...] += _colsum(d)
        db16 = d.astype(_MXU)
        dmc_ref[...] = _dot_nt(db16, w_ref[...])
        dw_ref[...] += _dot_tn(mc_ref[...], db16)

        @pl.when(i == pl.num_programs(0) - 1)
        def _():
            _pair_reduce(dw_ref, land_ref, wire_ref, send_sem, recv_sem)
            exchange.finish_to(pl_ref, x_flush)

    return pl.pallas_call(
        body, name="outproj_bwd", grid=(s_len // tm,),
        in_specs=[_rows(tm, D_MODEL), _rows(tm, D_MODEL), _vmem(), _vmem()],
        out_specs=[_rows(tm, D_MODEL), _vmem(), _vmem(), _const2((1, D_MODEL)), _hbm()],
        out_shape=[_hbm_shape((s_len, D_MODEL), F32), jax.ShapeDtypeStruct((D_MODEL, D_MODEL), F32),
                   jax.ShapeDtypeStruct((N_CHIP, OUT_SH // 2, D_MODEL), _WIRE), _hbm_shape((1, D_MODEL), F32),
                   _ChipExchange.land_shape(prev_wire)],
        scratch_shapes=_pair_scratch((N_CHIP, OUT_SH // 2, D_MODEL)) + _ChipExchange.scratch(prev_wire),
        compiler_params=_params(48),
    )(dr1, mc, w_out, prev_wire)


def _mixer_bwd(q, k, v, su, sv, dmc, tc, t1, t2, sinks, sg, sb, sgu_w, sgu_bt, prev_wire):
    s_len = q.shape[0]
    nb = s_len // BLK

    def body(q_ref, kc_ref, kp_ref, vc_ref, vp_ref, su_ref, sv_ref, dmc_ref,
             tc_ref, t1_ref, t2_ref, tcp_ref, t1p_ref, t2p_ref,
             sink_ref, lg_ref, lb_ref, w_ref, bt_ref, pw_ref,
             dq_ref, dkv_ref, dsuv_ref, dbq_ref, dbkv_ref, dbsuv_ref,
             dsink_ref, dlg_ref, dlb_ref, dw_ref, dbt_ref, pl_ref, carry_ref, xl_ref, x_send, x_recv, x_flush):
        i = pl.program_id(0)
        exchange = _ChipExchange(pw_ref, xl_ref, x_send, x_recv)

        @pl.when(i == 0)
        def _():
            exchange.start()

        @pl.when(i == 0)
        def _():
            for r in (dbq_ref, dbkv_ref, dbsuv_ref, dsink_ref, dlg_ref, dlb_ref, dw_ref, dbt_ref):
                r[...] = jnp.zeros_like(r)

        def emit_kv(fin):
            dk = _rope_bwd(fin[:, 0:KV_W], tcp_ref[...], t1p_ref[...], t2p_ref[...])
            out = jnp.concatenate([dk, fin[:, KV_W:2 * KV_W]], axis=1)
            dkv_ref[...] = out.astype(_MXU)
            dbkv_ref[...] += _colsum(out)

        @pl.when(i < nb)
        def _():
            allowed = _band_mask(i == 0)
            kb = jnp.concatenate([kp_ref[...], kc_ref[...]], axis=0)
            vb = jnp.concatenate([vp_ref[...], vc_ref[...]], axis=0)
            qv = q_ref[...]
            dmc = dmc_ref[...]
            dqs, dks, dvs, dsinks = [], [], [], []
            for g in range(N_KV):
                kh = kb[:, g * HEAD_DIM:(g + 1) * HEAD_DIM]
                vh = vb[:, g * HEAD_DIM:(g + 1) * HEAD_DIM]
                dk_g = jnp.zeros((2 * BLK, HEAD_DIM), F32)
                dv_g = jnp.zeros((2 * BLK, HEAD_DIM), F32)
                for hh in range(Q_PER_KV):
                    h = g * Q_PER_KV + hh
                    qh = qv[:, h * HEAD_DIM:(h + 1) * HEAD_DIM]
                    probs, psink = _attn_probs(qh, kh, sink_ref[h], allowed)
                    pb = probs.astype(_MXU)
                    dob = dmc[:, h * HEAD_DIM:(h + 1) * HEAD_DIM].astype(_MXU)
                    dv_g = dv_g + _dot_tn(pb, dob)
                    dp = _dot_nt(dob, vh)
                    rd = jnp.sum(probs * dp, axis=-1, keepdims=True)
                    dsb = (probs * (dp - rd)).astype(_MXU)
                    dsinks.append(-jnp.sum(psink * rd, axis=0, keepdims=True))
                    dqs.append(_dot(dsb, kh))
                    dk_g = dk_g + _dot_tn(dsb, qh)
                dks.append(dk_g)
                dvs.append(dv_g)
            dq = _rope_bwd(jnp.concatenate(dqs, axis=1) * (HEAD_DIM ** -0.5), tc_ref[...], t1_ref[...], t2_ref[...])
            dq_ref[...] = dq.astype(_MXU)
            dbq_ref[...] += _colsum(dq)
            dsink_ref[...] += _lane_put(dsinks, 128)
            contrib = jnp.concatenate(dks + dvs, axis=1)

            @pl.when(i > 0)
            def _():
                emit_kv(carry_ref[...] + contrib[0:BLK, :])

            carry_ref[...] = contrib[BLK:2 * BLK, :]

            su = su_ref[...]
            sv = sv_ref[...]
            lg = lg_ref[...]
            u, vhat, rstd, vvb, wcs, mixed = _sgu_fwd(su, sv, lg, lb_ref[...], w_ref, bt_ref)
            dsgu = dmc[:, ATTN_W:D_MODEL]
            dsu = dsgu * mixed * _gelu_grad(su)
            dmixed = dsgu * u
            tri_t = lax.broadcasted_iota(jnp.int32, (BLK, BLK), 0)
            tri_s = lax.broadcasted_iota(jnp.int32, (BLK, BLK), 1)
            dvv, dbs = [], []
            for h in range(N_GRP):
                dm = dmixed[:, h * GRP_DIM:(h + 1) * GRP_DIM]
                dmb = dm.astype(_MXU)
                dbs.append(jnp.sum(dm, axis=1, keepdims=True))
                dw_ref[h] += jnp.where(tri_s <= tri_t, _dot_nt(dmb, vvb[:, h * GRP_DIM:(h + 1) * GRP_DIM]), 0.0)
                dvv.append(_dot_tn(wcs[h], dmb))
            dvv = jnp.concatenate(dvv, axis=1)
            dbt_ref[...] += _lane_put(dbs, 128)
            dlg_ref[...] += _colsum(dvv * vhat)
            dlb_ref[...] += _colsum(dvv)
            dsv = _ln_bwd(dvv, vhat, rstd, lg) * _gelu_grad(sv)
            dsuv = jnp.concatenate([dsu, dsv], axis=1)
            dsuv_ref[...] = dsuv.astype(_MXU)
            dbsuv_ref[...] += _colsum(dsuv)

        @pl.when(i == nb)
        def _():
            emit_kv(carry_ref[...])
            exchange.finish_to(pl_ref, x_flush)

    last = nb - 1
    cur = lambda w: pl.BlockSpec((BLK, w), lambda i: (jnp.minimum(i, last), 0))
    prev = lambda w: pl.BlockSpec((BLK, w), lambda i: (jnp.clip(i - 1, 0, last), 0))
    sd = _hbm_shape
    return pl.pallas_call(
        body, name="mixer_bwd", grid=(nb + 1,),
        in_specs=[cur(ATTN_W), cur(KV_W), prev(KV_W), cur(KV_W), prev(KV_W), cur(SGU_W), cur(SGU_W), cur(D_MODEL),
                  cur(128), cur(128), cur(128), prev(128), prev(128), prev(128),
                  _smem(), _const2((1, SGU_W)), _const2((1, SGU_W)), _const2((N_GRP, BLK, BLK)), _const2((BLK, N_GRP)),
                  _vmem()],
        out_specs=[cur(ATTN_W), prev(2 * KV_W), cur(2 * SGU_W),
                   _const2((1, ATTN_W)), _const2((1, 2 * KV_W)), _const2((1, 2 * SGU_W)),
                   _const2((1, 128)), _const2((1, SGU_W)), _const2((1, SGU_W)),
                   _const2((N_GRP, BLK, BLK)), _const2((BLK, 128)), _hbm()],
        out_shape=[sd((s_len, ATTN_W), _MXU), sd((s_len, 2 * KV_W), _MXU), sd((s_len, 2 * SGU_W), _MXU),
                   sd((1, ATTN_W), F32), sd((1, 2 * KV_W), F32), sd((1, 2 * SGU_W), F32),
                   sd((1, 128), F32), sd((1, SGU_W), F32), sd((1, SGU_W), F32),
                   sd((N_GRP, BLK, BLK), F32), sd((BLK, 128), F32), _ChipExchange.land_shape(prev_wire)],
        scratch_shapes=[pltpu.VMEM((BLK, 2 * KV_W), F32)] + _ChipExchange.scratch(prev_wire),
        compiler_params=_params(32),
    )(q, k, k, v, v, su, sv, dmc, tc, t1, t2, tc, t1, t2, sinks, sg, sb, sgu_w, sgu_bt, prev_wire)


def _inproj_bwd(dq, dkv, dsuv, dr1, x, g0, b0, w_in):
    s_len = x.shape[0]
    tm = _tile(s_len, 512)
    cuts = ((0, ATTN_W), (ATTN_W, ATTN_W + 2 * KV_W), (ATTN_W + 2 * KV_W, IN_W))

    def body(dq_ref, dkv_ref, dsuv_ref, dr1_ref, x_ref, g_ref, b_ref, w_ref, dx_ref, dw_ref, dg_ref, db_ref):
        i = pl.program_id(0)

        @pl.when(i == 0)
        def _():
            dw_ref[...] = jnp.zeros_like(dw_ref)
            dg_ref[...] = jnp.zeros_like(dg_ref)
            db_ref[...] = jnp.zeros_like(db_ref)

        h0, xhat, rstd = _ln(x_ref[...], g_ref[...], b_ref[...])
        h0b = h0.astype(_MXU)
        dh0 = ALPHA * dr1_ref[...]
        for (lo, hi), d_ref in zip(cuts, (dq_ref, dkv_ref, dsuv_ref)):
            d = d_ref[...]
            dh0 = dh0 + _dot(d, w_ref[lo:hi, :])
            dw_ref[lo:hi, :] += _dot_tn(d, h0b)
        dg_ref[...] += _colsum(dh0 * xhat)
        db_ref[...] += _colsum(dh0)
        dx_ref[...] = _ln_bwd(dh0, xhat, rstd, g_ref[...])

    vec = _hbm_shape((1, D_MODEL), F32)
    c = _const2((1, D_MODEL))
    return pl.pallas_call(
        body, name="inproj_bwd", grid=(s_len // tm,),
        in_specs=[_rows(tm, ATTN_W), _rows(tm, 2 * KV_W), _rows(tm, 2 * SGU_W), _rows(tm, D_MODEL), _rows(tm, D_MODEL),
                  c, c, _vmem()],
        out_specs=[_rows(tm, D_MODEL), _vmem(), c, c],
        out_shape=[_hbm_shape((s_len, D_MODEL), F32), jax.ShapeDtypeStruct((IN_W, D_MODEL), F32), vec, vec],
        compiler_params=_params(48),
    )(dq, dkv, dsuv, dr1, x, g0, b0, w_in)


def _place():
    x, y, c = (lax.axis_index(a) for a in MESH_AXES)
    chips = [(1 - x, y), (x, 1 - y), (1 - x, 1 - y)]
    return x, y, c, chips


class _Gather:
    def __init__(self, ins, outs, send_sems, recv_sems):
        self.ins, self.outs, self.send_sems, self.recv_sems = ins, outs, send_sems, recv_sems
        self.n = len(ins)
        self.halves = [r.shape[0] // 2 for r in ins]

    def _copy(self, k, t, slot, half, to):
        rows = pl.ds(pl.multiple_of(half * self.halves[t], 16), self.halves[t])
        piece = self.outs[t].at[slot, rows, :]
        return pltpu.make_async_remote_copy(src_ref=piece, dst_ref=piece, send_sem=self.send_sems.at[k],
                                            recv_sem=self.recv_sems.at[k], device_id=to, device_id_type=MESH)

    def _chip_copy(self, t, d, slot):
        x, y, c, chips = _place()
        return self._copy(3 * t + d, t, slot, c, (chips[d][0], chips[d][1], c))

    def _pass_copy(self, t, d, half):
        x, y, c, chips = _place()
        return self._copy(3 * self.n + 3 * t + d, t, 2 * chips[d][0] + chips[d][1], half, (x, y, 1 - c))

    def start(self):
        x, y, c, chips = _place()
        me = 2 * x + y
        for t in range(self.n):
            self.outs[t][me] = self.ins[t][...].astype(_WIRE)
        for t in range(self.n):
            for d in range(3):
                self._chip_copy(t, d, me).start()

    def finish(self):
        x, y, c, chips = _place()
        me = 2 * x + y
        for t in range(self.n):
            for d in range(3):
                self._chip_copy(t, d, 2 * chips[d][0] + chips[d][1]).wait_recv()
                self._pass_copy(t, d, c).start()
        for t in range(self.n):
            for d in range(3):
                self._pass_copy(t, d, 1 - c).wait_recv()
        for t in range(self.n):
            for d in range(3):
                self._chip_copy(t, d, me).wait_send()
                self._pass_copy(t, d, c).wait_send()

    def flush(self, hbm_outs, flush_sems):
        _flush(self.outs, hbm_outs, flush_sems)

    @staticmethod
    def out_shapes(shards, make=jax.ShapeDtypeStruct):
        return [make((N_CHIP,) + s.shape, _WIRE) for s in shards]

    @staticmethod
    def sems(n):
        return [pltpu.SemaphoreType.DMA((6 * n,)), pltpu.SemaphoreType.DMA((6 * n,))]

    @staticmethod
    def scratch(shards):
        n = len(shards)
        return ([pltpu.VMEM((N_CHIP,) + s.shape, _WIRE) for s in shards] + _Gather.sems(n)
                + [pltpu.SemaphoreType.DMA((n,))])


def _flush(bufs, hbm_outs, sems):
    copies = [pltpu.make_async_copy(b, o, sems.at[k]) for k, (b, o) in enumerate(zip(bufs, hbm_outs))]
    for cp in copies:
        cp.start()
    for cp in copies:
        cp.wait()


def _gather_weights(shards):
    n = len(shards)

    def body(*refs):
        gather = _Gather(refs[:n], refs[n:2 * n], refs[2 * n], refs[2 * n + 1])
        gather.start()
        gather.finish()

    return pl.pallas_call(
        body, name="gather_weights",
        in_specs=[_vmem()] * n, out_specs=[_vmem()] * n,
        out_shape=_Gather.out_shapes(shards), scratch_shapes=_Gather.sems(n),
        compiler_params=pltpu.CompilerParams(vmem_limit_bytes=32 * MIB),
    )(*shards)


class _ChipExchange:
    def __init__(self, wire_ref, land_ref, send_sems, recv_sems):
        self.wire, self.land, self.send_sems, self.recv_sems = wire_ref, land_ref, send_sems, recv_sems

    def _copy(self, d):
        x, y, c, chips = _place()
        return pltpu.make_async_remote_copy(
            src_ref=self.wire.at[2 * chips[d][0] + chips[d][1]], dst_ref=self.land.at[d],
            send_sem=self.send_sems.at[d], recv_sem=self.recv_sems.at[d],
            device_id=(chips[d][0], chips[d][1], c), device_id_type=MESH)

    def start(self):
        for d in range(3):
            self._copy(d).start()

    def wait_recv(self):
        for d in range(3):
            self._copy(d).wait_recv()

    def wait_send(self):
        for d in range(3):
            self._copy(d).wait_send()

    def finish_to(self, hbm_out, flush_sem):
        self.wait_recv()
        _flush([self.land], [hbm_out], flush_sem)
        self.wait_send()

    @staticmethod
    def land_shape(wire):
        return _hbm_shape((3,) + wire.shape[1:], wire.dtype)

    @staticmethod
    def sems():
        return [pltpu.SemaphoreType.DMA((3,)), pltpu.SemaphoreType.DMA((3,))]

    @staticmethod
    def scratch(wire):
        return ([pltpu.VMEM((3,) + wire.shape[1:], wire.dtype)] + _ChipExchange.sems() + [pltpu.SemaphoreType.DMA((1,))])


def _pair_scratch(half_shape):
    return [pltpu.VMEM(half_shape, F32), pltpu.SemaphoreType.DMA((N_CHIP,)), pltpu.SemaphoreType.DMA((N_CHIP,))]


def _pair_reduce(acc_ref, land_ref, wire_ref, send_sems, recv_sems):
    rh = land_ref.shape[1]
    x, y, c, _ = _place()
    copies = []
    for j in range(N_CHIP):
        give = acc_ref.at[pl.ds(pl.multiple_of(j * 2 * rh + (1 - c) * rh, 8), rh), :]
        cp = pltpu.make_async_remote_copy(src_ref=give, dst_ref=land_ref.at[j], send_sem=send_sems.at[j],
                                          recv_sem=recv_sems.at[j], device_id=(x, y, 1 - c), device_id_type=MESH)
        cp.start()
        copies.append(cp)
    for cp in copies:
        cp.wait()

    def chunk(r, carry):
        theirs = pl.ds(pl.multiple_of(r * ROW_CHUNK, ROW_CHUNK), ROW_CHUNK)
        for j in range(N_CHIP):
            mine = pl.ds(pl.multiple_of(j * 2 * rh + c * rh + r * ROW_CHUNK, 8), ROW_CHUNK)
            s = acc_ref[mine, :] + land_ref[j, theirs, :]
            acc_ref[mine, :] = s
            wire_ref[j, theirs, :] = s.astype(_WIRE)
        return carry

    lax.fori_loop(0, rh // ROW_CHUNK, chunk, 0)


def _grad_finish(last_acc, lands, accs):
    n = len(accs) + 1
    halves = [last_acc.shape[0] // (2 * N_CHIP)] + [w.shape[1] for w in lands]
    widths = [last_acc.shape[1]] + [a.shape[1] for a in accs]

    def body(*refs):
        acc0, land, acc, g = refs[0], (None,) + refs[1:n], (None,) + refs[n:2 * n - 1], refs[2 * n - 1:3 * n - 1]
        pland0, wire0, land0 = refs[3 * n - 1:3 * n + 2]
        own = refs[3 * n + 2:4 * n + 2]
        p_send, p_recv, x_send, x_recv, pair_send, pair_recv, local_sems = refs[4 * n + 2:4 * n + 9]
        land = (land0,) + land[1:]
        x, y, c, chips = _place()
        me = 2 * x + y
        exchange = _ChipExchange(wire0, land0, x_send, x_recv)

        def half_rows(t, half):
            return pl.ds(pl.multiple_of(half * halves[t], 8), halves[t])

        def own_copy(t):
            rows = pl.ds(pl.multiple_of((2 * me + c) * halves[t], 8), halves[t])
            return pltpu.make_async_copy(acc[t].at[rows, :], own[t], local_sems.at[t])

        def pair_copy(t, half):
            rows = g[t].at[half_rows(t, half), :]
            return pltpu.make_async_remote_copy(src_ref=rows, dst_ref=rows, send_sem=pair_send.at[t],
                                                recv_sem=pair_recv.at[t], device_id=(x, y, 1 - c), device_id_type=MESH)

        for t in range(1, n):
            own_copy(t).start()

        rh = halves[0]
        gives = []
        for j in range(N_CHIP):
            rows = acc0.at[pl.ds(pl.multiple_of((2 * j + 1 - c) * rh, 8), rh), :]
            cp = pltpu.make_async_remote_copy(src_ref=rows, dst_ref=pland0.at[j], send_sem=p_send.at[j],
                                              recv_sem=p_recv.at[j], device_id=(x, y, 1 - c), device_id_type=MESH)
            cp.start()
            gives.append(cp)
        for cp in gives:
            cp.wait()

        def chip_sum(r, carry):
            src = pl.ds(pl.multiple_of(r * ROW_CHUNK, ROW_CHUNK), ROW_CHUNK)
            for j in range(N_CHIP):
                mine = pl.ds(pl.multiple_of((2 * j + c) * rh + r * ROW_CHUNK, 8), ROW_CHUNK)
                wire0[j, src, :] = (acc0[mine, :] + pland0[j, src, :]).astype(_WIRE)
            mine = pl.ds(pl.multiple_of((2 * me + c) * rh + r * ROW_CHUNK, 8), ROW_CHUNK)
            own[0][src, :] = acc0[mine, :] + pland0[me, src, :]
            return carry

        lax.fori_loop(0, rh // ROW_CHUNK, chip_sum, 0)
        exchange.start()

        for t in list(range(1, n)) + [0]:
            if t == 0:
                exchange.wait_recv()
            else:
                own_copy(t).wait()

            def chunk(r, carry, t=t):
                src = pl.ds(pl.multiple_of(r * ROW_CHUNK, ROW_CHUNK), ROW_CHUNK)
                dst = pl.ds(pl.multiple_of(c * halves[t] + r * ROW_CHUNK, 8), ROW_CHUNK)
                s = own[t][src, :]
                for d in range(3):
                    s = s + land[t][d, src, :].astype(F32)
                g[t][dst, :] = s
                return carry

            lax.fori_loop(0, halves[t] // ROW_CHUNK, chunk, 0)
            pair_copy(t, c).start()
        for t in range(n):
            pair_copy(t, 1 - c).wait_recv()
        for t in range(n):
            pair_copy(t, c).wait_send()
        exchange.wait_send()

    half0 = (halves[0], widths[0])
    return pl.pallas_call(
        body, name="grad_finish",
        in_specs=[_vmem()] * n + [_hbm()] * (n - 1), out_specs=[_vmem()] * n,
        out_shape=[jax.ShapeDtypeStruct((2 * h, w), F32) for h, w in zip(halves, widths)],
        scratch_shapes=[pltpu.VMEM((N_CHIP,) + half0, F32), pltpu.VMEM((N_CHIP,) + half0, _WIRE),
                        pltpu.VMEM((3,) + half0, _WIRE)]
        + [pltpu.VMEM((h, w), F32) for h, w in zip(halves, widths)]
        + [pltpu.SemaphoreType.DMA((N_CHIP,)), pltpu.SemaphoreType.DMA((N_CHIP,))]
        + _ChipExchange.sems()
        + [pltpu.SemaphoreType.DMA((n,)), pltpu.SemaphoreType.DMA((n,)), pltpu.SemaphoreType.DMA((n,))],
        compiler_params=pltpu.CompilerParams(vmem_limit_bytes=56 * MIB),
    )(last_acc, *lands, *accs)


_SMALL = ("ln_in_g", "ln_in_b", "b_in", "attn_sinks", "sgu_ln_g", "sgu_ln_b", "sgu_w", "sgu_b", "b_out",
          "ln_mix_g", "ln_mix_b", "ln_ffn_g", "ln_ffn_b")
_VEC_ROW = dict(ln_in_g=0, ln_in_b=1, b_in=2, attn_sinks=4, sgu_ln_g=5, sgu_ln_b=6, b_out=7, ln_mix_g=8, ln_mix_b=9,
                ln_ffn_g=10, ln_ffn_b=11)
_LOSS_ROW = 12
_VEC_ROWS = 16
_MAT_ROWS = N_GRP * BLK + BLK


def _small_allreduce(local):
    n_in = 16

    def body(*refs):
        (g_ln_in_g, g_ln_in_b, g_bq, g_bkv, g_bsuv, g_sink, g_sln_g, g_sln_b, g_sw, g_sbt, g_bout,
         g_lmg, g_lmb, g_lfg, g_lfb, g_loss) = refs[:n_in]
        out_a, out_b = refs[n_in:n_in + 2]
        (buf_a, buf_b, pair_a, pair_b, stage_a, stage_b, tot_a, tot_b,
         p1_send, p1_recv, x_send, x_recv, p2_send, p2_recv) = refs[n_in + 2:]
        x, y, c, chips = _place()
        me = 2 * x + y
        sibling = (x, y, 1 - c)
        half_a, half_b = _VEC_ROWS // 2, _MAT_ROWS // 2

        buf_a[...] = jnp.zeros_like(buf_a)
        for row, ref in ((0, g_ln_in_g), (1, g_ln_in_b), (7, g_bout), (8, g_lmg), (9, g_lmb), (10, g_lfg), (11, g_lfb),
                         (_LOSS_ROW, g_loss)):
            buf_a[row:row + 1, :] = ref[...]
        buf_a[2:3, 0:ATTN_W] = g_bq[...]
        buf_a[2:3, ATTN_W:ATTN_W + 2 * KV_W] = g_bkv[...]
        buf_a[2:3, ATTN_W + 2 * KV_W:D_MODEL] = g_bsuv[:, 0:2 * KV_W]
        buf_a[3:4, 0:2 * SGU_W - 2 * KV_W] = g_bsuv[:, 2 * KV_W:2 * SGU_W]
        buf_a[4:5, 0:128] = g_sink[...]
        buf_a[5:6, 0:SGU_W] = g_sln_g[...]
        buf_a[6:7, 0:SGU_W] = g_sln_b[...]
        for h in range(N_GRP):
            buf_b[h * BLK:(h + 1) * BLK, :] = g_sw[h]
        buf_b[N_GRP * BLK:_MAT_ROWS, :] = g_sbt[...]

        def remote(src, dst, send_sem, recv_sem, to):
            return pltpu.make_async_remote_copy(src_ref=src, dst_ref=dst, send_sem=send_sem, recv_sem=recv_sem,
                                                device_id=to, device_id_type=MESH)

        first = [remote(buf_a, pair_a, p1_send.at[0], p1_recv.at[0], sibling),
                 remote(buf_b, pair_b, p1_send.at[1], p1_recv.at[1], sibling)]
        for cp in first:
            cp.start()
        for cp in first:
            cp.wait()
        rows_a = pl.ds(pl.multiple_of(c * half_a, 8), half_a)
        rows_b = pl.ds(pl.multiple_of(c * half_b, 8), half_b)
        stage_a[me] = buf_a[rows_a, :] + pair_a[rows_a, :]
        stage_b[me] = buf_b[rows_b, :] + pair_b[rows_b, :]

        def chip_copies(d):
            to = (chips[d][0], chips[d][1], c)
            return [remote(stage_a.at[me], stage_a.at[me], x_send.at[2 * d], x_recv.at[2 * d], to),
                    remote(stage_b.at[me], stage_b.at[me], x_send.at[2 * d + 1], x_recv.at[2 * d + 1], to)]

        def chip_arrivals(d):
            slot = 2 * chips[d][0] + chips[d][1]
            to = (chips[d][0], chips[d][1], c)
            return [remote(stage_a.at[slot], stage_a.at[slot], x_send.at[2 * d], x_recv.at[2 * d], to),
                    remote(stage_b.at[slot], stage_b.at[slot], x_send.at[2 * d + 1], x_recv.at[2 * d + 1], to)]

        for d in range(3):
            for cp in chip_copies(d):
                cp.start()
        for d in range(3):
            for cp in chip_arrivals(d):
                cp.wait_recv()
        tot_a[rows_a, :] = ((stage_a[0] + stage_a[1]) + stage_a[2]) + stage_a[3]
        tot_b[rows_b, :] = ((stage_b[0] + stage_b[1]) + stage_b[2]) + stage_b[3]

        second = [remote(tot_a.at[rows_a, :], tot_a.at[rows_a, :], p2_send.at[0], p2_recv.at[0], sibling),
                  remote(tot_b.at[rows_b, :], tot_b.at[rows_b, :], p2_send.at[1], p2_recv.at[1], sibling)]
        for cp in second:
            cp.start()
        other_a = pl.ds(pl.multiple_of((1 - c) * half_a, 8), half_a)
        other_b = pl.ds(pl.multiple_of((1 - c) * half_b, 8), half_b)
        remote(tot_a.at[other_a, :], tot_a.at[other_a, :], p2_send.at[0], p2_recv.at[0], sibling).wait_recv()
        remote(tot_b.at[other_b, :], tot_b.at[other_b, :], p2_send.at[1], p2_recv.at[1], sibling).wait_recv()
        for cp in second:
            cp.wait_send()
        for d in range(3):
            for cp in chip_copies(d):
                cp.wait_send()
        out_a[...] = tot_a[...]
        out_b[...] = tot_b[...]

    ins = [local[k] for k in ("ln_in_g", "ln_in_b", "bq", "bkv", "bsuv", "sink", "sgu_ln_g", "sgu_ln_b", "sgu_w",
                              "sgu_bt", "b_out", "ln_mix_g", "ln_mix_b", "ln_ffn_g", "ln_ffn_b", "loss")]
    out_dims = [(_VEC_ROWS, D_MODEL), (_MAT_ROWS, 128)]
    vec = pltpu.VMEM((_VEC_ROWS, D_MODEL), F32)
    mat = pltpu.VMEM((_MAT_ROWS, 128), F32)
    return pl.pallas_call(
        body, name="small_allreduce", grid=(1,),
        in_specs=[_const2(a.shape) for a in ins], out_specs=[_const2(s) for s in out_dims],
        out_shape=[_hbm_shape(s, F32) for s in out_dims],
        scratch_shapes=[vec, mat, vec, mat, pltpu.VMEM((N_CHIP, _VEC_ROWS // 2, D_MODEL), F32),
                        pltpu.VMEM((N_CHIP, _MAT_ROWS // 2, 128), F32), vec, mat,
                        pltpu.SemaphoreType.DMA((2,)), pltpu.SemaphoreType.DMA((2,)), pltpu.SemaphoreType.DMA((6,)),
                        pltpu.SemaphoreType.DMA((6,)), pltpu.SemaphoreType.DMA((2,)), pltpu.SemaphoreType.DMA((2,))],
        compiler_params=pltpu.CompilerParams(vmem_limit_bytes=32 * MIB),
    )(*ins)


def _small_adamw(tot_a, tot_b, params):
    shapes = [params[nm][0].shape for nm in _SMALL]

    def body(*refs):
        ta, tb = refs[:2]
        prm = refs[2:2 + 3 * len(_SMALL)]
        outs = refs[2 + 3 * len(_SMALL):]

        def grad_of(k, name):
            if name == "sgu_w":
                return [tb[h * BLK:(h + 1) * BLK, :] for h in range(N_GRP)]
            if name == "sgu_b":
                return jnp.transpose(tb[N_GRP * BLK:_MAT_ROWS, :])[0:N_GRP, :]
            row = _VEC_ROW[name]
            if name == "b_in":
                return jnp.concatenate([ta[row:row + 1, :], ta[row + 1:row + 2, 0:IN_W - D_MODEL]], axis=1)
            return ta[row:row + 1, 0:shapes[k][-1]]

        for k, name in enumerate(_SMALL):
            w_ref, m_ref, v_ref = prm[3 * k:3 * k + 3]
            g_out, d_out, m_out, v_out = outs[4 * k:4 * k + 4]
            g = grad_of(k, name)
            if name == "sgu_w":
                for h in range(N_GRP):
                    d_, m_, v_ = _adamw_math(w_ref[h], g[h], m_ref[h], v_ref[h])
                    g_out[h], d_out[h], m_out[h], v_out[h] = g[h], d_, m_, v_
            else:
                d_, m_, v_ = _adamw_math(w_ref[...], g, m_ref[...], v_ref[...])
                g_out[...], d_out[...], m_out[...], v_out[...] = g, d_, m_, v_
        outs[-1][...] = ta[_LOSS_ROW:_LOSS_ROW + 1, :]

    ins = [tot_a, tot_b] + [_in_hbm(a) for nm in _SMALL for a in params[nm]]
    out_dims = [s for s in shapes for _ in range(4)] + [(1, D_MODEL)]
    res = pl.pallas_call(
        body, name="small_adamw", grid=(1,),
        in_specs=[_const2(a.shape) for a in ins], out_specs=[_const2(s) for s in out_dims],
        out_shape=[_hbm_shape(s, F32) for s in out_dims],
        compiler_params=_params(32),
    )(*ins)
    return {nm: tuple(res[4 * k:4 * k + 4]) for k, nm in enumerate(_SMALL)}, res[-1]


def _elementwise(name, fn, ins, out_dtypes, tile_rows=256):
    shape = ins[0].shape
    lead = shape[:-2]
    rows, cols = shape[-2:]
    tr = _tile(rows, tile_rows)
    n_lead = math.prod(lead)
    nr = rows // tr
    flat = [_in_hbm(a.reshape((n_lead, rows, cols))) for a in ins]

    def body(*refs):
        outs = fn(*[r[0] for r in refs[:len(ins)]])
        for o_ref, o in zip(refs[len(ins):], outs):
            o_ref[0] = o.astype(o_ref.dtype)

    spec = pl.BlockSpec((1, tr, cols), lambda i: (i // nr, i % nr, 0))
    res = pl.pallas_call(
        body, name=name, grid=(n_lead * nr,),
        in_specs=[spec] * len(ins), out_specs=[spec] * len(out_dtypes),
        out_shape=[_hbm_shape((n_lead, rows, cols), dt) for dt in out_dtypes],
        compiler_params=_params(32),
    )(*flat)
    return [r.reshape(shape) for r in res]


def _adamw_math(w, g, m, v):
    m = ADAM_B1 * m + (1.0 - ADAM_B1) * g
    v = ADAM_B2 * v + (1.0 - ADAM_B2) * (g * g)
    m_hat = m / (1.0 - ADAM_B1 ** ADAM_STEP)
    v_hat = v / (1.0 - ADAM_B2 ** ADAM_STEP)
    delta = -ADAM_LR * (m_hat / (jnp.sqrt(v_hat) + ADAM_EPS) + ADAM_WD * w)
    return delta, m, v


def _adamw(name, w, g, m, v, tile_rows=256):
    return _elementwise(name, lambda w_, g_, m_, v_: (g_,) + _adamw_math(w_, g_, m_, v_), [w, g, m, v],
                        [F32, F32, F32, F32], tile_rows)


def kernel(x, positions, ln_in_g, ln_in_b, w_in, b_in, attn_sinks, sgu_ln_g, sgu_ln_b, sgu_w, sgu_b, w_out, b_out, ln_mix_g, ln_mix_b, w_gate, w_up, w_down, ln_ffn_g, ln_ffn_b, loss_target, m_ln_in_g, m_ln_in_b, m_w_in, m_b_in, m_attn_sinks, m_sgu_ln_g, m_sgu_ln_b, m_sgu_w, m_sgu_b, m_w_out, m_b_out, m_ln_mix_g, m_ln_mix_b, m_w_gate, m_w_up, m_w_down, m_ln_ffn_g, m_ln_ffn_b, v_ln_in_g, v_ln_in_b, v_w_in, v_b_in, v_attn_sinks, v_sgu_ln_g, v_sgu_ln_b, v_sgu_w, v_sgu_b, v_w_out, v_b_out, v_ln_mix_g, v_ln_mix_b, v_w_gate, v_w_up, v_w_down, v_ln_ffn_g, v_ln_ffn_b):
    weights = dict(ln_in_g=ln_in_g, ln_in_b=ln_in_b, w_in=w_in, b_in=b_in, attn_sinks=attn_sinks, sgu_ln_g=sgu_ln_g,
                   sgu_ln_b=sgu_ln_b, sgu_w=sgu_w, sgu_b=sgu_b, w_out=w_out, b_out=b_out, ln_mix_g=ln_mix_g,
                   ln_mix_b=ln_mix_b, w_gate=w_gate, w_up=w_up, w_down=w_down, ln_ffn_g=ln_ffn_g, ln_ffn_b=ln_ffn_b)
    mom_m = dict(ln_in_g=m_ln_in_g, ln_in_b=m_ln_in_b, w_in=m_w_in, b_in=m_b_in, attn_sinks=m_attn_sinks,
                 sgu_ln_g=m_sgu_ln_g, sgu_ln_b=m_sgu_ln_b, sgu_w=m_sgu_w, sgu_b=m_sgu_b, w_out=m_w_out, b_out=m_b_out,
                 ln_mix_g=m_ln_mix_g, ln_mix_b=m_ln_mix_b, w_gate=m_w_gate, w_up=m_w_up, w_down=m_w_down,
                 ln_ffn_g=m_ln_ffn_g, ln_ffn_b=m_ln_ffn_b)
    mom_v = dict(ln_in_g=v_ln_in_g, ln_in_b=v_ln_in_b, w_in=v_w_in, b_in=v_b_in, attn_sinks=v_attn_sinks,
                 sgu_ln_g=v_sgu_ln_g, sgu_ln_b=v_sgu_ln_b, sgu_w=v_sgu_w, sgu_b=v_sgu_b, w_out=v_w_out, b_out=v_b_out,
                 ln_mix_g=v_ln_mix_g, ln_mix_b=v_ln_mix_b, w_gate=v_w_gate, w_up=v_w_up, w_down=v_w_down,
                 ln_ffn_g=v_ln_ffn_g, ln_ffn_b=v_ln_ffn_b)
    order = list(weights)
    big = ("w_in", "w_out", "w_gate", "w_up", "w_down")

    s_len = x.shape[1]
    xs = _in_hbm(x.reshape(s_len, D_MODEL))
    tgt = _in_hbm(loss_target.reshape(s_len, D_MODEL))
    pos_col = _in_hbm(positions.reshape(s_len, 1))
    g0, b0 = _in_hbm(ln_in_g.reshape(1, D_MODEL)), _in_hbm(ln_in_b.reshape(1, D_MODEL))
    sinks = attn_sinks.reshape(N_Q)
    sgu_w3 = _in_hbm(sgu_w.reshape(N_GRP, BLK, BLK))
    sgu_bt = _in_hbm(sgu_b.reshape(N_GRP, BLK).T)
    b_in, b_out, sgu_ln_g, sgu_ln_b, ln_mix_g, ln_mix_b, ln_ffn_g, ln_ffn_b = (
        _in_hbm(a) for a in (b_in, b_out, sgu_ln_g, sgu_ln_b, ln_mix_g, ln_mix_b, ln_ffn_g, ln_ffn_b))

    col_sharded = ("w_in", "w_gate", "w_up")

    def rowmajor(name, a):
        return jnp.swapaxes(a[0], 0, 1) if name in col_sharded else a[0]

    def as_given(name, a):
        return (jnp.swapaxes(a, 0, 1) if name in col_sharded else a)[None]

    shards = [rowmajor(n, weights[n]) for n in big]
    (gw_in,) = _gather_weights(shards[0:1])
    w_in_full = gw_in.reshape(IN_W, D_MODEL)

    *acts, gw_out = _ln_inproj(xs, pos_col, g0, b0, w_in_full, b_in, shards[1:2])
    q, k, v, su, sv, tc, t1, t2 = (_in_hbm(a) for a in acts)
    mc, gw_gate = _mixer_fwd(q, k, v, su, sv, sinks, sgu_ln_g, sgu_ln_b, sgu_w3, sgu_bt, shards[2:3])
    mc = _in_hbm(mc)
    w_out_full = gw_out.reshape(D_MODEL, D_MODEL)
    r1, gw_up = _outproj(mc, w_out_full, b_out, xs, g0, b0, shards[3:4])
    r1 = _in_hbm(r1)
    gact, uact, gw_down = _ffn_up(r1, ln_mix_g, ln_mix_b, gw_gate, gw_up, shards[4:5])
    gact, uact = _in_hbm(gact), _in_hbm(uact)
    dr2, loss_cols, d_ln_ffn_g, d_ln_ffn_b = _ffn_down_loss(gact, uact, gw_down, r1, ln_mix_g, ln_mix_b,
                                                            ln_ffn_g, ln_ffn_b, tgt)
    dr2 = _in_hbm(dr2)

    dg, du, acc_down, wire_down = _ffn_bwd_a(dr2, gact, uact, gw_down)
    dh1a, acc_gate, wire_gate, land_down = _ffn_bwd_g(dr2, _in_hbm(dg), r1, ln_mix_g, ln_mix_b, gw_gate, wire_down)
    dr1, acc_up, wire_up, d_ln_mix_g, d_ln_mix_b, land_gate = _ffn_bwd_u(_in_hbm(dh1a), _in_hbm(du), r1, ln_mix_g,
                                                                         ln_mix_b, gw_up, wire_gate)
    dr1 = _in_hbm(dr1)
    dmc, acc_out, wire_out, d_b_out, land_up = _outproj_bwd(dr1, mc, w_out_full, wire_up)
    (dq, dkv, dsuv, dbq, dbkv, dbsuv, d_sink, d_sgu_ln_g, d_sgu_ln_b, d_sgu_w, d_sgu_bt, land_out) = _mixer_bwd(
        q, k, v, su, sv, _in_hbm(dmc), tc, t1, t2, sinks, sgu_ln_g, sgu_ln_b, sgu_w3, sgu_bt, wire_out)
    grad_x, acc_in, d_ln_in_g, d_ln_in_b = _inproj_bwd(_in_hbm(dq), _in_hbm(dkv), _in_hbm(dsuv), dr1, xs, g0, b0,
                                                       w_in_full)

    reduced = _grad_finish(acc_in, [land_out, land_gate, land_up, land_down], [acc_out, acc_gate, acc_up, acc_down])
    small_shape = dict(ln_in_g=(1, D_MODEL), ln_in_b=(1, D_MODEL), sgu_w=(N_GRP, BLK, BLK), sgu_b=(N_GRP, BLK))
    small_local = dict(
        ln_in_g=d_ln_in_g, ln_in_b=d_ln_in_b, bq=dbq, bkv=dbkv, bsuv=dbsuv, sink=d_sink, sgu_ln_g=d_sgu_ln_g,
        sgu_ln_b=d_sgu_ln_b, sgu_w=d_sgu_w, sgu_bt=d_sgu_bt, b_out=d_b_out, ln_mix_g=d_ln_mix_g, ln_mix_b=d_ln_mix_b,
        ln_ffn_g=d_ln_ffn_g, ln_ffn_b=d_ln_ffn_b, loss=loss_cols)
    small_params = {nm: tuple(src[nm].reshape(small_shape.get(nm, src[nm].shape)) for src in (weights, mom_m, mom_v))
                    for nm in _SMALL}
    tot_a, tot_b = _small_allreduce({nm: _in_hbm(a) for nm, a in small_local.items()})
    small_out, loss_sum = _small_adamw(_in_hbm(tot_a), _in_hbm(tot_b), small_params)
    loss = jnp.sum(loss_sum) * (0.5 / D_MODEL)
    grads, delta, new_m, new_v = {}, {}, {}, {}
    for nm in _SMALL:
        grads[nm], delta[nm], new_m[nm], new_v[nm] = (a.reshape(weights[nm].shape) for a in small_out[nm])

    for t, name in enumerate(big):
        g_, d_, m_, v_ = _adamw("adamw_" + name, shards[t], reduced[t], rowmajor(name, mom_m[name]),
                                rowmajor(name, mom_v[name]))
        grads[name], delta[name], new_m[name], new_v[name] = (as_given(name, a) for a in (g_, d_, m_, v_))

    return (loss, grad_x.reshape(x.shape), *[grads[n] for n in order], *[delta[n] for n in order],
            *[new_m[n] for n in order], *[new_v[n] for n in order])
```

```python
import functools
import math

import jax
import jax.numpy as jnp
from jax import lax
from jax.experimental import pallas as pl
from jax.experimental.pallas import tpu as pltpu

F32 = jnp.float32
_MXU = jnp.bfloat16
_WIRE = jnp.bfloat16
_ACT = jnp.bfloat16

D_MODEL = 1024
ATTN_W = 512
SGU_W = 512
HEAD_DIM = 64
N_Q = 8
N_KV = 2
Q_PER_KV = 4
KV_W = 128
BLK = 128
ROT_DIM = 16
ROPE_THETA = 500000.0
N_GRP = 4
GRP_DIM = 128
D_FF = 2816
IN_W = 1792
LN_EPS = 1e-5
ALPHA = 2.0 ** 0.25
N_CHIP = 4
FF_SH = D_FF // N_CHIP
IN_SH = IN_W // N_CHIP
OUT_SH = D_MODEL // N_CHIP
ROW_CHUNK = 32

ADAM_LR = 0.001
ADAM_B1 = 0.9
ADAM_B2 = 0.999
ADAM_EPS = 1e-08
ADAM_WD = 0.01
ADAM_STEP = 10

SQRT_HALF = 0.7071067811865476
INV_SQRT_2PI = 0.3989422804014327
MESH_AXES = ("x", "y", "c")
MESH = pl.DeviceIdType.MESH
MIB = 2 ** 20


def _vmem():
    return pl.BlockSpec(memory_space=pltpu.VMEM)


def _smem():
    return pl.BlockSpec(memory_space=pltpu.SMEM)


def _hbm():
    return pl.BlockSpec(memory_space=pl.ANY)


def _hbm_shape(shape, dtype):
    return pltpu.HBM(shape, dtype)


def _in_hbm(a):
    return pltpu.with_memory_space_constraint(a, pltpu.HBM)


def _params(vmem_mib=48):
    return pltpu.CompilerParams(dimension_semantics=("arbitrary",), vmem_limit_bytes=vmem_mib * MIB)


def _tile(n, cap):
    if n <= cap:
        return n
    for t in range(cap - cap % 16, 0, -16):
        if n % t == 0:
            return t
    raise ValueError((n, cap))


def _rows(tm, width):
    return pl.BlockSpec((tm, width), lambda i: (i, 0))


def _const2(shape):
    return pl.BlockSpec(shape, lambda i: (0,) * len(shape))


def _ln(x, g, b):
    mu = jnp.mean(x, axis=-1, keepdims=True)
    xc = x - mu
    var = jnp.mean(xc * xc, axis=-1, keepdims=True)
    rstd = lax.rsqrt(var + LN_EPS)
    xhat = xc * rstd
    return xhat * g + b, xhat, rstd


def _ln_bwd(dy, xhat, rstd, g):
    gdy = dy * g
    m1 = jnp.mean(gdy, axis=-1, keepdims=True)
    m2 = jnp.mean(gdy * xhat, axis=-1, keepdims=True)
    return rstd * (gdy - m1 - xhat * m2)


def _colsum(a):
    return jnp.sum(a, axis=0, keepdims=True)


def _gelu_and_grad(x):
    cdf = 0.5 * (1.0 + lax.erf(x * SQRT_HALF))
    return x * cdf, cdf + x * jnp.exp(-0.5 * x * x) * INV_SQRT_2PI


def _dot(a, b):
    return jnp.dot(a, b, preferred_element_type=F32)


def _dot_nt(a, b):
    return lax.dot_general(a, b, (((1,), (1,)), ((), ())), preferred_element_type=F32)


def _dot_tn(a, b):
    return lax.dot_general(a, b, (((0,), (0,)), ((), ())), preferred_element_type=F32)


def _rope(t, tc, t1, t2):
    n = t.shape[1]
    rep = n // 128
    if rep > 1:
        tc, t1, t2 = (jnp.tile(a, (1, rep)) for a in (tc, t1, t2))
    return t * tc + pltpu.roll(t, n - 8, 1) * t1 + pltpu.roll(t, 8, 1) * t2


def _rope_bwd(d, tc, t1, t2):
    n = d.shape[1]
    rep = n // 128
    if rep > 1:
        tc, t1, t2 = (jnp.tile(a, (1, rep)) for a in (tc, t1, t2))
    return d * tc + pltpu.roll(d * t1, 8, 1) + pltpu.roll(d * t2, n - 8, 1)


def _band_mask(first_block):
    qi = lax.broadcasted_iota(jnp.int32, (BLK, 2 * BLK), 0)
    kj = lax.broadcasted_iota(jnp.int32, (BLK, 2 * BLK), 1)
    shut = jnp.where(first_block, 2 * BLK, 0)
    prev_ok = jnp.logical_and(kj < BLK, kj > qi + shut)
    cur_ok = jnp.logical_and(kj >= BLK, (kj - BLK) <= qi)
    return jnp.logical_or(prev_ok, cur_ok)


def _causal_w(w_ref, h):
    t = lax.broadcasted_iota(jnp.int32, (BLK, BLK), 0)
    s = lax.broadcasted_iota(jnp.int32, (BLK, BLK), 1)
    return jnp.where(s <= t, w_ref[h], 0.0)


def _lane_put(vals, width):
    rows = vals[0].shape[0]
    lane = lax.broadcasted_iota(jnp.int32, (rows, width), 1)
    out = jnp.zeros((rows, width), F32)
    for k, v in enumerate(vals):
        out = out + jnp.where(lane == k, v, 0.0)
    return out


def _rope_consts():
    lane = jnp.arange(128) % HEAD_DIM
    inv_freq = ROPE_THETA ** (-jnp.arange(0, ROT_DIM, 2, dtype=F32) / ROT_DIM)
    rot = lane < ROT_DIM
    freq = jnp.where(rot, inv_freq[lane % (ROT_DIM // 2)], 0.0)
    rows = [freq, rot.astype(F32), 1.0 - rot.astype(F32), (lane < ROT_DIM // 2).astype(F32),
            jnp.logical_and(lane >= ROT_DIM // 2, rot).astype(F32)]
    rows += [jnp.zeros((128,), F32)] * 3
    return jnp.stack(rows).astype(F32)


def _ln_inproj(x, pos_col, g0, b0, w_in, b_in, shards):
    s_len = x.shape[0]
    tm = _tile(s_len, 512)

    n = len(shards)

    def body(x_ref, pos_ref, g_ref, b_ref, w_ref, bi_ref, rc_ref, *rest):
        q_ref, k_ref, v_ref, su_ref, sv_ref, tc_ref, t1_ref, t2_ref = rest[n:n + 8]
        gathered = rest[n + 8:2 * n + 8]
        gather = _Gather(rest[:n], rest[2 * n + 8:3 * n + 8], rest[3 * n + 8], rest[3 * n + 9])
        flush_sems = rest[3 * n + 10]
        i = pl.program_id(0)

        @pl.when(i == 0)
        def _():
            gather.start()

        h0, _, _ = _ln(x_ref[...], g_ref[...], b_ref[...])
        proj = _dot_nt(h0.astype(_MXU), w_ref[...]) + bi_ref[...]
        ang = pos_ref[...].astype(F32) * rc_ref[0:1, :]
        cs = jnp.cos(ang)
        sn = jnp.sin(ang)
        tc = cs * rc_ref[1:2, :] + rc_ref[2:3, :]
        t1 = -sn * rc_ref[3:4, :]
        t2 = sn * rc_ref[4:5, :]
        tc_ref[...] = tc
        t1_ref[...] = t1
        t2_ref[...] = t2
        q = _rope(proj[:, 0:ATTN_W], tc, t1, t2) * (HEAD_DIM ** -0.5)
        q_ref[...] = q.astype(_MXU)
        k_ref[...] = _rope(proj[:, ATTN_W:ATTN_W + KV_W], tc, t1, t2).astype(_MXU)
        v_ref[...] = proj[:, ATTN_W + KV_W:ATTN_W + 2 * KV_W].astype(_MXU)
        su_ref[...] = proj[:, ATTN_W + 2 * KV_W:ATTN_W + 2 * KV_W + SGU_W]
        sv_ref[...] = proj[:, ATTN_W + 2 * KV_W + SGU_W:IN_W]

        @pl.when(i == pl.num_programs(0) - 1)
        def _():
            gather.finish()
            gather.flush(gathered, flush_sems)

    sd = _hbm_shape
    return pl.pallas_call(
        body, name="ln_inproj", grid=(s_len // tm,),
        in_specs=[_rows(tm, D_MODEL), _rows(tm, 1), _const2((1, D_MODEL)), _const2((1, D_MODEL)), _vmem(),
                  _const2((1, IN_W)), _const2((8, 128))] + [_vmem()] * n,
        out_specs=[_rows(tm, ATTN_W), _rows(tm, KV_W), _rows(tm, KV_W), _rows(tm, SGU_W), _rows(tm, SGU_W),
                   _rows(tm, 128), _rows(tm, 128), _rows(tm, 128)] + [_hbm()] * n,
        out_shape=[sd((s_len, ATTN_W), _MXU), sd((s_len, KV_W), _MXU), sd((s_len, KV_W), _MXU),
                   sd((s_len, SGU_W), F32), sd((s_len, SGU_W), F32),
                   sd((s_len, 128), F32), sd((s_len, 128), F32), sd((s_len, 128), F32)]
        + _Gather.out_shapes(shards, _hbm_shape),
        scratch_shapes=_Gather.scratch(shards),
        compiler_params=_params(56),
    )(x, pos_col, g0, b0, w_in, b_in, _rope_consts(), *shards)


def _attn_probs(qh, kh, sink, allowed):
    s = jnp.where(allowed, _dot_nt(qh, kh), -1e30)
    m = jnp.maximum(jnp.max(s, axis=-1, keepdims=True), sink)
    p = jnp.exp(s - m)
    ps = jnp.exp(sink - m)
    inv = 1.0 / (jnp.sum(p, axis=-1, keepdims=True) + ps)
    return p * inv, ps * inv


def _sgu_fwd(su, sv, lg, lb, w_ref, bt_ref):
    u, du_dsu = _gelu_and_grad(su)
    gv, dgv_dsv = _gelu_and_grad(sv)
    vv, vhat, rstd = _ln(gv, lg, lb)
    vvb = vv.astype(_MXU)
    wcs, mixed = [], []
    for h in range(N_GRP):
        wc = _causal_w(w_ref, h).astype(_MXU)
        wcs.append(wc)
        mixed.append(_dot(wc, vvb[:, h * GRP_DIM:(h + 1) * GRP_DIM]) + bt_ref[:, h:h + 1])
    return u, jnp.concatenate(mixed, axis=1), du_dsu, dgv_dsv, vhat, rstd, vvb, wcs


def _prev_map(i):
    return (jnp.maximum(i - 1, 0), 0)


def _mixer_fwd(q, k, v, su, sv, sinks, sg, sb, sgu_w, sgu_bt, shards):
    s_len = q.shape[0]
    nb = s_len // BLK
    n = len(shards)

    def body(q_ref, kc_ref, kp_ref, vc_ref, vp_ref, su_ref, sv_ref, sink_ref, lg_ref, lb_ref, w_ref, bt_ref, *rest):
        mc_ref = rest[n]
        gathered = rest[n + 1:2 * n + 1]
        gather = _Gather(rest[:n], rest[2 * n + 1:3 * n + 1], rest[3 * n + 1], rest[3 * n + 2])
        flush_sems = rest[3 * n + 3]
        i = pl.program_id(0)

        @pl.when(i == 0)
        def _():
            gather.start()

        @pl.when(i == nb - 1)
        def _():
            gather.finish()
            gather.flush(gathered, flush_sems)

        allowed = _band_mask(i == 0)
        kb = jnp.concatenate([kp_ref[...], kc_ref[...]], axis=0)
        vb = jnp.concatenate([vp_ref[...], vc_ref[...]], axis=0)
        qv = q_ref[...]
        outs = []
        for h in range(N_Q):
            g = h // Q_PER_KV
            kh = kb[:, g * HEAD_DIM:(g + 1) * HEAD_DIM]
            vh = vb[:, g * HEAD_DIM:(g + 1) * HEAD_DIM]
            probs, _ = _attn_probs(qv[:, h * HEAD_DIM:(h + 1) * HEAD_DIM], kh, sink_ref[h], allowed)
            outs.append(_dot(probs.astype(_MXU), vh))
        u, mixed = _sgu_fwd(su_ref[...], sv_ref[...], lg_ref[...], lb_ref[...], w_ref, bt_ref)[:2]
        mc_ref[...] = jnp.concatenate(outs + [u * mixed], axis=1).astype(_MXU)

    cur = lambda w: pl.BlockSpec((BLK, w), lambda i: (i, 0))
    prev = lambda w: pl.BlockSpec((BLK, w), _prev_map)
    return pl.pallas_call(
        body, name="mixer_fwd", grid=(nb,),
        in_specs=[cur(ATTN_W), cur(KV_W), prev(KV_W), cur(KV_W), prev(KV_W), cur(SGU_W), cur(SGU_W), _smem(),
                  _const2((1, SGU_W)), _const2((1, SGU_W)), _const2((N_GRP, BLK, BLK)), _const2((BLK, N_GRP))]
        + [_vmem()] * n,
        out_specs=[cur(D_MODEL)] + [_hbm()] * n,
        out_shape=[_hbm_shape((s_len, D_MODEL), _MXU)] + _Gather.out_shapes(shards, _hbm_shape),
        scratch_shapes=_Gather.scratch(shards),
        compiler_params=_params(48),
    )(q, k, k, v, v, su, sv, sinks, sg, sb, sgu_w, sgu_bt, *shards)


def _outproj(mc, w_out, b_out, x, g0, b0, shards):
    s_len = x.shape[0]
    tm = _tile(s_len, 512)
    n = len(shards)

    def body(mc_ref, w_ref, bo_ref, x_ref, g_ref, b_ref, *rest):
        r1_ref = rest[n]
        gathered = rest[n + 1:2 * n + 1]
        gather = _Gather(rest[:n], rest[2 * n + 1:3 * n + 1], rest[3 * n + 1], rest[3 * n + 2])
        flush_sems = rest[3 * n + 3]
        i = pl.program_id(0)

        @pl.when(i == 0)
        def _():
            gather.start()

        h0, _, _ = _ln(x_ref[...], g_ref[...], b_ref[...])
        r1_ref[...] = ALPHA * h0 + (_dot(mc_ref[...], w_ref[...]) + bo_ref[...])

        @pl.when(i == pl.num_programs(0) - 1)
        def _():
            gather.finish()
            gather.flush(gathered, flush_sems)

    return pl.pallas_call(
        body, name="outproj", grid=(s_len // tm,),
        in_specs=[_rows(tm, D_MODEL), _vmem(), _const2((1, D_MODEL)), _rows(tm, D_MODEL),
                  _const2((1, D_MODEL)), _const2((1, D_MODEL))] + [_vmem()] * n,
        out_specs=[_rows(tm, D_MODEL)] + [_hbm()] * n,
        out_shape=[_hbm_shape((s_len, D_MODEL), F32)] + _Gather.out_shapes(shards, _hbm_shape),
        scratch_shapes=_Gather.scratch(shards),
        compiler_params=_params(40),
    )(mc, w_out, b_out, x, g0, b0, *shards)


def _ffn_spec(tm):
    return pl.BlockSpec((N_CHIP, tm, FF_SH), lambda i: (0, i, 0))


def _ffn_up(r1, g1, b1, wg, wu, shards):
    s_len = r1.shape[0]
    tm = _tile(s_len, 512)
    n = len(shards)

    def body(r1_ref, g_ref, b_ref, wg_ref, wu_ref, *rest):
        a_ref, p_ref, q_ref = rest[n:n + 3]
        gathered = rest[n + 3:2 * n + 3]
        gather = _Gather(rest[:n], rest[2 * n + 3:3 * n + 3], rest[3 * n + 3], rest[3 * n + 4])
        flush_sems = rest[3 * n + 5]
        i = pl.program_id(0)

        @pl.when(i == 0)
        def _():
            gather.start()

        h1, _, _ = _ln(r1_ref[...], g_ref[...], b_ref[...])
        h1b = h1.astype(_MXU)
        for j in range(N_CHIP):
            g = _dot_nt(h1b, wg_ref[j])
            u = _dot_nt(h1b, wu_ref[j])
            silu, sg = _silu_parts(g)
            a_ref[j] = (silu * u).astype(_MXU)
            p_ref[j] = silu.astype(_ACT)
            q_ref[j] = (u * (sg * (1.0 + g * (1.0 - sg)))).astype(_ACT)

        @pl.when(i == pl.num_programs(0) - 1)
        def _():
            gather.finish()
            gather.flush(gathered, flush_sems)

    sd = _hbm_shape((N_CHIP, s_len, FF_SH), _ACT)
    return pl.pallas_call(
        body, name="ffn_up", grid=(s_len // tm,),
        in_specs=[_rows(tm, D_MODEL), _const2((1, D_MODEL)), _const2((1, D_MODEL)), _vmem(), _vmem()] + [_vmem()] * n,
        out_specs=[_ffn_spec(tm)] * 3 + [_hbm()] * n,
        out_shape=[_hbm_shape((N_CHIP, s_len, FF_SH), _MXU), sd, sd] + _Gather.out_shapes(shards, _hbm_shape),
        scratch_shapes=_Gather.scratch(shards),
        compiler_params=_params(56),
    )(r1, g1, b1, wg, wu, *shards)


def _silu_parts(g):
    sg = 1.0 / (1.0 + jnp.exp(-g))
    return g * sg, sg


def _ffn_down_loss(act, wd, r1, g1, b1, g2, b2, target):
    s_len = r1.shape[0]
    tm = _tile(s_len, 512)

    parts = 2 if tm % 32 == 0 else 1
    sub = tm // parts

    def body(a_ref, wd_ref, r1_ref, g1_ref, b1_ref, g2_ref, b2_ref, t_ref, dr2_ref, loss_ref, dg2_ref, db2_ref):
        i = pl.program_id(0)

        @pl.when(i == 0)
        def _():
            loss_ref[...] = jnp.zeros_like(loss_ref)
            dg2_ref[...] = jnp.zeros_like(dg2_ref)
            db2_ref[...] = jnp.zeros_like(db2_ref)

        for part in range(parts):
            rows = slice(part * sub, (part + 1) * sub)
            f = jnp.zeros((sub, D_MODEL), F32)
            for j in range(N_CHIP):
                f = f + _dot(a_ref[j, rows, :], wd_ref[j])
            h1, _, _ = _ln(r1_ref[rows, :], g1_ref[...], b1_ref[...])
            h2, r2hat, rstd2 = _ln(ALPHA * h1 + f, g2_ref[...], b2_ref[...])
            diff = h2 - t_ref[rows, :]
            dh2 = diff * (1.0 / D_MODEL)
            loss_ref[...] += _colsum(diff * diff)
            dg2_ref[...] += _colsum(dh2 * r2hat)
            db2_ref[...] += _colsum(dh2)
            dr2_ref[rows, :] = _ln_bwd(dh2, r2hat, rstd2, g2_ref[...])

    vec = _hbm_shape((1, D_MODEL), F32)
    c = _const2((1, D_MODEL))
    return pl.pallas_call(
        body, name="ffn_down_loss", grid=(s_len // tm,),
        in_specs=[_ffn_spec(tm), _vmem(), _rows(tm, D_MODEL), c, c, c, c, _rows(tm, D_MODEL)],
        out_specs=[_rows(tm, D_MODEL), c, c, c],
        out_shape=[_hbm_shape((s_len, D_MODEL), F32), vec, vec, vec],
        compiler_params=_params(48),
    )(act, wd, r1, g1, b1, g2, b2, target)


def _ffn_bwd_a(dr2, act, p_act, q_act, wd):
    s_len = dr2.shape[0]
    tm = _tile(s_len, 512)

    def body(dr2_ref, a_ref, p_ref, q_ref, wd_ref, dg_ref, du_ref, dwd_ref, wire_ref, land_ref, send_sem, recv_sem):
        i = pl.program_id(0)

        @pl.when(i == 0)
        def _():
            dwd_ref[...] = jnp.zeros_like(dwd_ref)

        dfb = dr2_ref[...].astype(_MXU)
        for j in range(N_CHIP):
            da = _dot_nt(dfb, wd_ref[j])
            dg_ref[j] = (da * q_ref[j].astype(F32)).astype(_MXU)
            du_ref[j] = (da * p_ref[j].astype(F32)).astype(_MXU)
            dwd_ref[j * FF_SH:(j + 1) * FF_SH, :] += _dot_tn(a_ref[j], dfb)

        @pl.when(i == pl.num_programs(0) - 1)
        def _():
            _pair_reduce(dwd_ref, land_ref, wire_ref, send_sem, recv_sem)

    sd = _hbm_shape((N_CHIP, s_len, FF_SH), _MXU)
    return pl.pallas_call(
        body, name="ffn_bwd_a", grid=(s_len // tm,),
        in_specs=[_rows(tm, D_MODEL), _ffn_spec(tm), _ffn_spec(tm), _ffn_spec(tm), _vmem()],
        out_specs=[_ffn_spec(tm), _ffn_spec(tm), _vmem(), _vmem()],
        out_shape=[sd, sd, jax.ShapeDtypeStruct((D_FF, D_MODEL), F32),
                   jax.ShapeDtypeStruct((N_CHIP, FF_SH // 2, D_MODEL), _WIRE)],
        scratch_shapes=_pair_scratch((N_CHIP, FF_SH // 2, D_MODEL)),
        compiler_params=_params(61),
    )(dr2, act, p_act, q_act, wd)


def _ffn_bwd_g(dr2, dg, r1, g1, b1, wg, prev_wire):
    s_len = dr2.shape[0]
    tm = _tile(s_len, 512)

    def body(dr2_ref, dg_ref, r1_ref, g1_ref, b1_ref, wg_ref, pw_ref, dh1_ref, dwg_ref, wire_ref, pl_ref,
             land_ref, send_sem, recv_sem, xl_ref, x_send, x_recv, x_flush):
        i = pl.program_id(0)
        exchange = _ChipExchange(pw_ref, xl_ref, x_send, x_recv)

        @pl.when(i == 0)
        def _():
            exchange.start()
            dwg_ref[...] = jnp.zeros_like(dwg_ref)

        h1, _, _ = _ln(r1_ref[...], g1_ref[...], b1_ref[...])
        h1b = h1.astype(_MXU)
        dh1 = ALPHA * dr2_ref[...]
        for j in range(N_CHIP):
            dgj = dg_ref[j]
            dh1 = dh1 + _dot(dgj, wg_ref[j])
            dwg_ref[j * FF_SH:(j + 1) * FF_SH, :] += _dot_tn(dgj, h1b)
        dh1_ref[...] = dh1

        @pl.when(i == pl.num_programs(0) - 1)
        def _():
            _pair_reduce(dwg_ref, land_ref, wire_ref, send_sem, recv_sem)
            exchange.finish_to(pl_ref, x_flush)

    c = _const2((1, D_MODEL))
    return pl.pallas_call(
        body, name="ffn_bwd_g", grid=(s_len // tm,),
        in_specs=[_rows(tm, D_MODEL), _ffn_spec(tm), _rows(tm, D_MODEL), c, c, _vmem(), _vmem()],
        out_specs=[_rows(tm, D_MODEL), _vmem(), _vmem(), _hbm()],
        out_shape=[_hbm_shape((s_len, D_MODEL), F32), jax.ShapeDtypeStruct((D_FF, D_MODEL), F32),
                   jax.ShapeDtypeStruct((N_CHIP, FF_SH // 2, D_MODEL), _WIRE), _ChipExchange.land_shape(prev_wire)],
        scratch_shapes=_pair_scratch((N_CHIP, FF_SH // 2, D_MODEL)) + _ChipExchange.scratch(prev_wire),
        compiler_params=_params(58),
    )(dr2, dg, r1, g1, b1, wg, prev_wire)


def _ffn_bwd_u(dh1a, du, r1, g1, b1, wu, prev_wire):
    s_len = dh1a.shape[0]
    tm = _tile(s_len, 512)

    def body(dh1_ref, du_ref, r1_ref, g1_ref, b1_ref, wu_ref, pw_ref,
             dr1_ref, dwu_ref, wire_ref, dg1_ref, db1_ref, pl_ref,
             land_ref, send_sem, recv_sem, xl_ref, x_send, x_recv, x_flush):
        i = pl.program_id(0)
        exchange = _ChipExchange(pw_ref, xl_ref, x_send, x_recv)

        @pl.when(i == 0)
        def _():
            exchange.start()
            dwu_ref[...] = jnp.zeros_like(dwu_ref)
            dg1_ref[...] = jnp.zeros_like(dg1_ref)
            db1_ref[...] = jnp.zeros_like(db1_ref)

        h1, r1hat, rstd1 = _ln(r1_ref[...], g1_ref[...], b1_ref[...])
        h1b = h1.astype(_MXU)
        dh1 = dh1_ref[...]
        for j in range(N_CHIP):
            duj = du_ref[j]
            dh1 = dh1 + _dot(duj, wu_ref[j])
            dwu_ref[j * FF_SH:(j + 1) * FF_SH, :] += _dot_tn(duj, h1b)
        dg1_ref[...] += _colsum(dh1 * r1hat)
        db1_ref[...] += _colsum(dh1)
        dr1_ref[...] = _ln_bwd(dh1, r1hat, rstd1, g1_ref[...])

        @pl.when(i == pl.num_programs(0) - 1)
        def _():
            _pair_reduce(dwu_ref, land_ref, wire_ref, send_sem, recv_sem)
            exchange.finish_to(pl_ref, x_flush)

    vec = _hbm_shape((1, D_MODEL), F32)
    c = _const2((1, D_MODEL))
    return pl.pallas_call(
        body, name="ffn_bwd_u", grid=(s_len // tm,),
        in_specs=[_rows(tm, D_MODEL), _ffn_spec(tm), _rows(tm, D_MODEL), c, c, _vmem(), _vmem()],
        out_specs=[_rows(tm, D_MODEL), _vmem(), _vmem(), c, c, _hbm()],
        out_shape=[_hbm_shape((s_len, D_MODEL), F32), jax.ShapeDtypeStruct((D_FF, D_MODEL), F32),
                   jax.ShapeDtypeStruct((N_CHIP, FF_SH // 2, D_MODEL), _WIRE), vec, vec,
                   _ChipExchange.land_shape(prev_wire)],
        scratch_shapes=_pair_scratch((N_CHIP, FF_SH // 2, D_MODEL)) + _ChipExchange.scratch(prev_wire),
        compiler_params=_params(58),
    )(dh1a, du, r1, g1, b1, wu, prev_wire)


def _outproj_bwd(dr1, mc, w_out, prev_wire):
    s_len = dr1.shape[0]
    tm = _tile(s_len, 512)

    def body(dr1_ref, mc_ref, w_ref, pw_ref, dmc_ref, dw_ref, wire_ref, db_ref, pl_ref,
             land_ref, send_sem, recv_sem, xl_ref, x_send, x_recv, x_flush):
        i = pl.program_id(0)
        exchange = _ChipExchange(pw_ref, xl_ref, x_send, x_recv)

        @pl.when(i == 0)
        def _():
            exchange.start()
            dw_ref[...] = jnp.zeros_like(dw_ref)
            db_ref[...] = jnp.zeros_like(db_ref)

        d = dr1_ref[...]
        db_ref[...] += _colsum(d)
        db16 = d.astype(_MXU)
        dmc_ref[...] = _dot_nt(db16, w_ref[...])
        dw_ref[...] += _dot_tn(mc_ref[...], db16)

        @pl.when(i == pl.num_programs(0) - 1)
        def _():
            _pair_reduce(dw_ref, land_ref, wire_ref, send_sem, recv_sem)
            exchange.finish_to(pl_ref, x_flush)

    return pl.pallas_call(
        body, name="outproj_bwd", grid=(s_len // tm,),
        in_specs=[_rows(tm, D_MODEL), _rows(tm, D_MODEL), _vmem(), _vmem()],
        out_specs=[_rows(tm, D_MODEL), _vmem(), _vmem(), _const2((1, D_MODEL)), _hbm()],
        out_shape=[_hbm_shape((s_len, D_MODEL), F32), jax.ShapeDtypeStruct((D_MODEL, D_MODEL), F32),
                   jax.ShapeDtypeStruct((N_CHIP, OUT_SH // 2, D_MODEL), _WIRE), _hbm_shape((1, D_MODEL), F32),
                   _ChipExchange.land_shape(prev_wire)],
        scratch_shapes=_pair_scratch((N_CHIP, OUT_SH // 2, D_MODEL)) + _ChipExchange.scratch(prev_wire),
        compiler_params=_params(48),
    )(dr1, mc, w_out, prev_wire)


def _mixer_bwd(q, k, v, su, sv, dmc, tc, t1, t2, sinks, sg, sb, sgu_w, sgu_bt, prev_wire):
    s_len = q.shape[0]
    nb = s_len // BLK

    def body(q_ref, kc_ref, kp_ref, vc_ref, vp_ref, su_ref, sv_ref, dmc_ref,
             tc_ref, t1_ref, t2_ref, tcp_ref, t1p_ref, t2p_ref,
             sink_ref, lg_ref, lb_ref, w_ref, bt_ref, pw_ref,
             dq_ref, dkv_ref, dsuv_ref, dbq_ref, dbkv_ref, dbsuv_ref,
             dsink_ref, dlg_ref, dlb_ref, dw_ref, dbt_ref, pl_ref, carry_ref, xl_ref, x_send, x_recv, x_flush):
        i = pl.program_id(0)
        exchange = _ChipExchange(pw_ref, xl_ref, x_send, x_recv)

        @pl.when(i == 0)
        def _():
            exchange.start()

        @pl.when(i == 0)
        def _():
            for r in (dbq_ref, dbkv_ref, dbsuv_ref, dsink_ref, dlg_ref, dlb_ref, dw_ref, dbt_ref):
                r[...] = jnp.zeros_like(r)

        def emit_kv(fin):
            dk = _rope_bwd(fin[:, 0:KV_W], tcp_ref[...], t1p_ref[...], t2p_ref[...])
            out = jnp.concatenate([dk, fin[:, KV_W:2 * KV_W]], axis=1)
            dkv_ref[...] = out.astype(_MXU)
            dbkv_ref[...] += _colsum(out)

        @pl.when(i < nb)
        def _():
            allowed = _band_mask(i == 0)
            kb = jnp.concatenate([kp_ref[...], kc_ref[...]], axis=0)
            vb = jnp.concatenate([vp_ref[...], vc_ref[...]], axis=0)
            qv = q_ref[...]
            dmc = dmc_ref[...]
            dqs, dks, dvs, dsinks = [], [], [], []
            for g in range(N_KV):
                kh = kb[:, g * HEAD_DIM:(g + 1) * HEAD_DIM]
                vh = vb[:, g * HEAD_DIM:(g + 1) * HEAD_DIM]
                dk_g = jnp.zeros((2 * BLK, HEAD_DIM), F32)
                dv_g = jnp.zeros((2 * BLK, HEAD_DIM), F32)
                for hh in range(Q_PER_KV):
                    h = g * Q_PER_KV + hh
                    qh = qv[:, h * HEAD_DIM:(h + 1) * HEAD_DIM]
                    probs, psink = _attn_probs(qh, kh, sink_ref[h], allowed)
                    pb = probs.astype(_MXU)
                    dob = dmc[:, h * HEAD_DIM:(h + 1) * HEAD_DIM].astype(_MXU)
                    dv_g = dv_g + _dot_tn(pb, dob)
                    dp = _dot_nt(dob, vh)
                    rd = jnp.sum(probs * dp, axis=-1, keepdims=True)
                    dsb = (probs * (dp - rd)).astype(_MXU)
                    dsinks.append(-jnp.sum(psink * rd, axis=0, keepdims=True))
                    dqs.append(_dot(dsb, kh))
                    dk_g = dk_g + _dot_tn(dsb, qh)
                dks.append(dk_g)
                dvs.append(dv_g)
            dq = _rope_bwd(jnp.concatenate(dqs, axis=1) * (HEAD_DIM ** -0.5), tc_ref[...], t1_ref[...], t2_ref[...])
            dq_ref[...] = dq.astype(_MXU)
            dbq_ref[...] += _colsum(dq)
            dsink_ref[...] += _lane_put(dsinks, 128)
            contrib = jnp.concatenate(dks + dvs, axis=1)

            @pl.when(i > 0)
            def _():
                emit_kv(carry_ref[...] + contrib[0:BLK, :])

            carry_ref[...] = contrib[BLK:2 * BLK, :]

            su = su_ref[...]
            sv = sv_ref[...]
            lg = lg_ref[...]
            u, mixed, du_dsu, dgv_dsv, vhat, rstd, vvb, wcs = _sgu_fwd(su, sv, lg, lb_ref[...], w_ref, bt_ref)
            dsgu = dmc[:, ATTN_W:D_MODEL]
            dsu = dsgu * mixed * du_dsu
            dmixed = dsgu * u
            tri_t = lax.broadcasted_iota(jnp.int32, (BLK, BLK), 0)
            tri_s = lax.broadcasted_iota(jnp.int32, (BLK, BLK), 1)
            dvv, dbs = [], []
            for h in range(N_GRP):
                dm = dmixed[:, h * GRP_DIM:(h + 1) * GRP_DIM]
                dmb = dm.astype(_MXU)
                dbs.append(jnp.sum(dm, axis=1, keepdims=True))
                dw_ref[h] += jnp.where(tri_s <= tri_t, _dot_nt(dmb, vvb[:, h * GRP_DIM:(h + 1) * GRP_DIM]), 0.0)
                dvv.append(_dot_tn(wcs[h], dmb))
            dvv = jnp.concatenate(dvv, axis=1)
            dbt_ref[...] += _lane_put(dbs, 128)
            dlg_ref[...] += _colsum(dvv * vhat)
            dlb_ref[...] += _colsum(dvv)
            dsv = _ln_bwd(dvv, vhat, rstd, lg) * dgv_dsv
            dsuv = jnp.concatenate([dsu, dsv], axis=1)
            dsuv_ref[...] = dsuv.astype(_MXU)
            dbsuv_ref[...] += _colsum(dsuv)

        @pl.when(i == nb)
        def _():
            emit_kv(carry_ref[...])
            exchange.finish_to(pl_ref, x_flush)

    last = nb - 1
    cur = lambda w: pl.BlockSpec((BLK, w), lambda i: (jnp.minimum(i, last), 0))
    prev = lambda w: pl.BlockSpec((BLK, w), lambda i: (jnp.clip(i - 1, 0, last), 0))
    sd = _hbm_shape
    return pl.pallas_call(
        body, name="mixer_bwd", grid=(nb + 1,),
        in_specs=[cur(ATTN_W), cur(KV_W), prev(KV_W), cur(KV_W), prev(KV_W), cur(SGU_W), cur(SGU_W), cur(D_MODEL),
                  cur(128), cur(128), cur(128), prev(128), prev(128), prev(128),
                  _smem(), _const2((1, SGU_W)), _const2((1, SGU_W)), _const2((N_GRP, BLK, BLK)), _const2((BLK, N_GRP)),
                  _vmem()],
        out_specs=[cur(ATTN_W), prev(2 * KV_W), cur(2 * SGU_W),
                   _const2((1, ATTN_W)), _const2((1, 2 * KV_W)), _const2((1, 2 * SGU_W)),
                   _const2((1, 128)), _const2((1, SGU_W)), _const2((1, SGU_W)),
                   _const2((N_GRP, BLK, BLK)), _const2((BLK, 128)), _hbm()],
        out_shape=[sd((s_len, ATTN_W), _MXU), sd((s_len, 2 * KV_W), _MXU), sd((s_len, 2 * SGU_W), _MXU),
                   sd((1, ATTN_W), F32), sd((1, 2 * KV_W), F32), sd((1, 2 * SGU_W), F32),
                   sd((1, 128), F32), sd((1, SGU_W), F32), sd((1, SGU_W), F32),
                   sd((N_GRP, BLK, BLK), F32), sd((BLK, 128), F32), _ChipExchange.land_shape(prev_wire)],
        scratch_shapes=[pltpu.VMEM((BLK, 2 * KV_W), F32)] + _ChipExchange.scratch(prev_wire),
        compiler_params=_params(32),
    )(q, k, k, v, v, su, sv, dmc, tc, t1, t2, tc, t1, t2, sinks, sg, sb, sgu_w, sgu_bt, prev_wire)


def _inproj_bwd(dq, dkv, dsuv, dr1, x, g0, b0, w_in):
    s_len = x.shape[0]
    tm = _tile(s_len, 512)
    cuts = ((0, ATTN_W), (ATTN_W, ATTN_W + 2 * KV_W), (ATTN_W + 2 * KV_W, IN_W))

    def body(dq_ref, dkv_ref, dsuv_ref, dr1_ref, x_ref, g_ref, b_ref, w_ref, dx_ref, dw_ref, dg_ref, db_ref):
        i = pl.program_id(0)

        @pl.when(i == 0)
        def _():
            dw_ref[...] = jnp.zeros_like(dw_ref)
            dg_ref[...] = jnp.zeros_like(dg_ref)
            db_ref[...] = jnp.zeros_like(db_ref)

        h0, xhat, rstd = _ln(x_ref[...], g_ref[...], b_ref[...])
        h0b = h0.astype(_MXU)
        dh0 = ALPHA * dr1_ref[...]
        for (lo, hi), d_ref in zip(cuts, (dq_ref, dkv_ref, dsuv_ref)):
            d = d_ref[...]
            dh0 = dh0 + _dot(d, w_ref[lo:hi, :])
            dw_ref[lo:hi, :] += _dot_tn(d, h0b)
        dg_ref[...] += _colsum(dh0 * xhat)
        db_ref[...] += _colsum(dh0)
        dx_ref[...] = _ln_bwd(dh0, xhat, rstd, g_ref[...])

    vec = _hbm_shape((1, D_MODEL), F32)
    c = _const2((1, D_MODEL))
    return pl.pallas_call(
        body, name="inproj_bwd", grid=(s_len // tm,),
        in_specs=[_rows(tm, ATTN_W), _rows(tm, 2 * KV_W), _rows(tm, 2 * SGU_W), _rows(tm, D_MODEL), _rows(tm, D_MODEL),
                  c, c, _vmem()],
        out_specs=[_rows(tm, D_MODEL), _vmem(), c, c],
        out_shape=[_hbm_shape((s_len, D_MODEL), F32), jax.ShapeDtypeStruct((IN_W, D_MODEL), F32), vec, vec],
        compiler_params=_params(48),
    )(dq, dkv, dsuv, dr1, x, g0, b0, w_in)


def _place():
    x, y, c = (lax.axis_index(a) for a in MESH_AXES)
    chips = [(1 - x, y), (x, 1 - y), (1 - x, 1 - y)]
    return x, y, c, chips


class _Gather:
    def __init__(self, ins, outs, send_sems, recv_sems):
        self.ins, self.outs, self.send_sems, self.recv_sems = ins, outs, send_sems, recv_sems
        self.n = len(ins)
        self.halves = [r.shape[0] // 2 for r in ins]

    def _copy(self, k, t, slot, half, to):
        rows = pl.ds(pl.multiple_of(half * self.halves[t], 16), self.halves[t])
        piece = self.outs[t].at[slot, rows, :]
        return pltpu.make_async_remote_copy(src_ref=piece, dst_ref=piece, send_sem=self.send_sems.at[k],
                                            recv_sem=self.recv_sems.at[k], device_id=to, device_id_type=MESH)

    def _chip_copy(self, t, d, slot):
        x, y, c, chips = _place()
        return self._copy(3 * t + d, t, slot, c, (chips[d][0], chips[d][1], c))

    def _pass_copy(self, t, d, half):
        x, y, c, chips = _place()
        return self._copy(3 * self.n + 3 * t + d, t, 2 * chips[d][0] + chips[d][1], half, (x, y, 1 - c))

    def start(self):
        x, y, c, chips = _place()
        me = 2 * x + y
        for t in range(self.n):
            self.outs[t][me] = self.ins[t][...].astype(_WIRE)
        for t in range(self.n):
            for d in range(3):
                self._chip_copy(t, d, me).start()

    def finish(self):
        x, y, c, chips = _place()
        me = 2 * x + y
        for t in range(self.n):
            for d in range(3):
                self._chip_copy(t, d, 2 * chips[d][0] + chips[d][1]).wait_recv()
                self._pass_copy(t, d, c).start()
        for t in range(self.n):
            for d in range(3):
                self._pass_copy(t, d, 1 - c).wait_recv()
        for t in range(self.n):
            for d in range(3):
                self._chip_copy(t, d, me).wait_send()
                self._pass_copy(t, d, c).wait_send()

    def flush(self, hbm_outs, flush_sems):
        _flush(self.outs, hbm_outs, flush_sems)

    @staticmethod
    def out_shapes(shards, make=jax.ShapeDtypeStruct):
        return [make((N_CHIP,) + s.shape, _WIRE) for s in shards]

    @staticmethod
    def sems(n):
        return [pltpu.SemaphoreType.DMA((6 * n,)), pltpu.SemaphoreType.DMA((6 * n,))]

    @staticmethod
    def scratch(shards):
        n = len(shards)
        return ([pltpu.VMEM((N_CHIP,) + s.shape, _WIRE) for s in shards] + _Gather.sems(n)
                + [pltpu.SemaphoreType.DMA((n,))])


def _flush(bufs, hbm_outs, sems):
    copies = [pltpu.make_async_copy(b, o, sems.at[k]) for k, (b, o) in enumerate(zip(bufs, hbm_outs))]
    for cp in copies:
        cp.start()
    for cp in copies:
        cp.wait()


def _gather_weights(shards):
    n = len(shards)

    def body(*refs):
        gather = _Gather(refs[:n], refs[n:2 * n], refs[2 * n], refs[2 * n + 1])
        gather.start()
        gather.finish()

    return pl.pallas_call(
        body, name="gather_weights",
        in_specs=[_vmem()] * n, out_specs=[_vmem()] * n,
        out_shape=_Gather.out_shapes(shards), scratch_shapes=_Gather.sems(n),
        compiler_params=pltpu.CompilerParams(vmem_limit_bytes=32 * MIB),
    )(*shards)


class _ChipExchange:
    def __init__(self, wire_ref, land_ref, send_sems, recv_sems):
        self.wire, self.land, self.send_sems, self.recv_sems = wire_ref, land_ref, send_sems, recv_sems

    def _copy(self, d):
        x, y, c, chips = _place()
        return pltpu.make_async_remote_copy(
            src_ref=self.wire.at[2 * chips[d][0] + chips[d][1]], dst_ref=self.land.at[d],
            send_sem=self.send_sems.at[d], recv_sem=self.recv_sems.at[d],
            device_id=(chips[d][0], chips[d][1], c), device_id_type=MESH)

    def start(self):
        for d in range(3):
            self._copy(d).start()

    def wait_recv(self):
        for d in range(3):
            self._copy(d).wait_recv()

    def wait_send(self):
        for d in range(3):
            self._copy(d).wait_send()

    def finish_to(self, hbm_out, flush_sem):
        self.wait_recv()
        _flush([self.land], [hbm_out], flush_sem)
        self.wait_send()

    @staticmethod
    def land_shape(wire):
        return _hbm_shape((3,) + wire.shape[1:], wire.dtype)

    @staticmethod
    def sems():
        return [pltpu.SemaphoreType.DMA((3,)), pltpu.SemaphoreType.DMA((3,))]

    @staticmethod
    def scratch(wire):
        return ([pltpu.VMEM((3,) + wire.shape[1:], wire.dtype)] + _ChipExchange.sems() + [pltpu.SemaphoreType.DMA((1,))])


def _pair_scratch(half_shape):
    return [pltpu.VMEM(half_shape, F32), pltpu.SemaphoreType.DMA((N_CHIP,)), pltpu.SemaphoreType.DMA((N_CHIP,))]


def _pair_reduce(acc_ref, land_ref, wire_ref, send_sems, recv_sems):
    rh = land_ref.shape[1]
    x, y, c, _ = _place()
    copies = []
    for j in range(N_CHIP):
        give = acc_ref.at[pl.ds(pl.multiple_of(j * 2 * rh + (1 - c) * rh, 8), rh), :]
        cp = pltpu.make_async_remote_copy(src_ref=give, dst_ref=land_ref.at[j], send_sem=send_sems.at[j],
                                          recv_sem=recv_sems.at[j], device_id=(x, y, 1 - c), device_id_type=MESH)
        cp.start()
        copies.append(cp)
    for cp in copies:
        cp.wait()

    def chunk(r, carry):
        theirs = pl.ds(pl.multiple_of(r * ROW_CHUNK, ROW_CHUNK), ROW_CHUNK)
        for j in range(N_CHIP):
            mine = pl.ds(pl.multiple_of(j * 2 * rh + c * rh + r * ROW_CHUNK, 8), ROW_CHUNK)
            s = acc_ref[mine, :] + land_ref[j, theirs, :]
            acc_ref[mine, :] = s
            wire_ref[j, theirs, :] = s.astype(_WIRE)
        return carry

    lax.fori_loop(0, rh // ROW_CHUNK, chunk, 0)


def _grad_finish(last_acc, lands, accs):
    n = len(accs) + 1
    halves = [last_acc.shape[0] // (2 * N_CHIP)] + [w.shape[1] for w in lands]
    widths = [last_acc.shape[1]] + [a.shape[1] for a in accs]

    def body(*refs):
        acc0, land, acc, g = refs[0], (None,) + refs[1:n], (None,) + refs[n:2 * n - 1], refs[2 * n - 1:3 * n - 1]
        pland0, wire0, land0 = refs[3 * n - 1:3 * n + 2]
        own = refs[3 * n + 2:4 * n + 2]
        p_send, p_recv, x_send, x_recv, pair_send, pair_recv, local_sems = refs[4 * n + 2:4 * n + 9]
        land = (land0,) + land[1:]
        x, y, c, chips = _place()
        me = 2 * x + y
        exchange = _ChipExchange(wire0, land0, x_send, x_recv)

        def half_rows(t, half):
            return pl.ds(pl.multiple_of(half * halves[t], 8), halves[t])

        def own_copy(t):
            rows = pl.ds(pl.multiple_of((2 * me + c) * halves[t], 8), halves[t])
            return pltpu.make_async_copy(acc[t].at[rows, :], own[t], local_sems.at[t])

        def pair_copy(t, half):
            rows = g[t].at[half_rows(t, half), :]
            return pltpu.make_async_remote_copy(src_ref=rows, dst_ref=rows, send_sem=pair_send.at[t],
                                                recv_sem=pair_recv.at[t], device_id=(x, y, 1 - c), device_id_type=MESH)

        for t in range(1, n):
            own_copy(t).start()

        rh = halves[0]
        gives = []
        for j in range(N_CHIP):
            rows = acc0.at[pl.ds(pl.multiple_of((2 * j + 1 - c) * rh, 8), rh), :]
            cp = pltpu.make_async_remote_copy(src_ref=rows, dst_ref=pland0.at[j], send_sem=p_send.at[j],
                                              recv_sem=p_recv.at[j], device_id=(x, y, 1 - c), device_id_type=MESH)
            cp.start()
            gives.append(cp)
        for cp in gives:
            cp.wait()

        def chip_sum(r, carry):
            src = pl.ds(pl.multiple_of(r * ROW_CHUNK, ROW_CHUNK), ROW_CHUNK)
            for j in range(N_CHIP):
                mine = pl.ds(pl.multiple_of((2 * j + c) * rh + r * ROW_CHUNK, 8), ROW_CHUNK)
                wire0[j, src, :] = (acc0[mine, :] + pland0[j, src, :]).astype(_WIRE)
            mine = pl.ds(pl.multiple_of((2 * me + c) * rh + r * ROW_CHUNK, 8), ROW_CHUNK)
            own[0][src, :] = acc0[mine, :] + pland0[me, src, :]
            return carry

        lax.fori_loop(0, rh // ROW_CHUNK, chip_sum, 0)
        exchange.start()

        for t in list(range(1, n)) + [0]:
            if t == 0:
                exchange.wait_recv()
            else:
                own_copy(t).wait()

            def chunk(r, carry, t=t):
                src = pl.ds(pl.multiple_of(r * ROW_CHUNK, ROW_CHUNK), ROW_CHUNK)
                dst = pl.ds(pl.multiple_of(c * halves[t] + r * ROW_CHUNK, 8), ROW_CHUNK)
                s = own[t][src, :]
                for d in range(3):
                    s = s + land[t][d, src, :].astype(F32)
                g[t][dst, :] = s
                return carry

            lax.fori_loop(0, halves[t] // ROW_CHUNK, chunk, 0)
            pair_copy(t, c).start()
        for t in range(n):
            pair_copy(t, 1 - c).wait_recv()
        for t in range(n):
            pair_copy(t, c).wait_send()
        exchange.wait_send()

    half0 = (halves[0], widths[0])
    return pl.pallas_call(
        body, name="grad_finish",
        in_specs=[_vmem()] * n + [_hbm()] * (n - 1), out_specs=[_vmem()] * n,
        out_shape=[jax.ShapeDtypeStruct((2 * h, w), F32) for h, w in zip(halves, widths)],
        scratch_shapes=[pltpu.VMEM((N_CHIP,) + half0, F32), pltpu.VMEM((N_CHIP,) + half0, _WIRE),
                        pltpu.VMEM((3,) + half0, _WIRE)]
        + [pltpu.VMEM((h, w), F32) for h, w in zip(halves, widths)]
        + [pltpu.SemaphoreType.DMA((N_CHIP,)), pltpu.SemaphoreType.DMA((N_CHIP,))]
        + _ChipExchange.sems()
        + [pltpu.SemaphoreType.DMA((n,)), pltpu.SemaphoreType.DMA((n,)), pltpu.SemaphoreType.DMA((n,))],
        compiler_params=pltpu.CompilerParams(vmem_limit_bytes=56 * MIB),
    )(last_acc, *lands, *accs)


_SMALL = ("ln_in_g", "ln_in_b", "b_in", "attn_sinks", "sgu_ln_g", "sgu_ln_b", "sgu_w", "sgu_b", "b_out",
          "ln_mix_g", "ln_mix_b", "ln_ffn_g", "ln_ffn_b")
_VEC_ROW = dict(ln_in_g=0, ln_in_b=1, b_in=2, attn_sinks=4, sgu_ln_g=5, sgu_ln_b=6, b_out=7, ln_mix_g=8, ln_mix_b=9,
                ln_ffn_g=10, ln_ffn_b=11)
_LOSS_ROW = 12
_VEC_ROWS = 16
_MAT_ROWS = N_GRP * BLK + BLK


def _small_allreduce(local):
    n_in = 16

    def body(*refs):
        (g_ln_in_g, g_ln_in_b, g_bq, g_bkv, g_bsuv, g_sink, g_sln_g, g_sln_b, g_sw, g_sbt, g_bout,
         g_lmg, g_lmb, g_lfg, g_lfb, g_loss) = refs[:n_in]
        out_a, out_b = refs[n_in:n_in + 2]
        (buf_a, buf_b, pair_a, pair_b, stage_a, stage_b, tot_a, tot_b,
         p1_send, p1_recv, x_send, x_recv, p2_send, p2_recv) = refs[n_in + 2:]
        x, y, c, chips = _place()
        me = 2 * x + y
        sibling = (x, y, 1 - c)
        half_a, half_b = _VEC_ROWS // 2, _MAT_ROWS // 2

        buf_a[...] = jnp.zeros_like(buf_a)
        for row, ref in ((0, g_ln_in_g), (1, g_ln_in_b), (7, g_bout), (8, g_lmg), (9, g_lmb), (10, g_lfg), (11, g_lfb),
                         (_LOSS_ROW, g_loss)):
            buf_a[row:row + 1, :] = ref[...]
        buf_a[2:3, 0:ATTN_W] = g_bq[...]
        buf_a[2:3, ATTN_W:ATTN_W + 2 * KV_W] = g_bkv[...]
        buf_a[2:3, ATTN_W + 2 * KV_W:D_MODEL] = g_bsuv[:, 0:2 * KV_W]
        buf_a[3:4, 0:2 * SGU_W - 2 * KV_W] = g_bsuv[:, 2 * KV_W:2 * SGU_W]
        buf_a[4:5, 0:128] = g_sink[...]
        buf_a[5:6, 0:SGU_W] = g_sln_g[...]
        buf_a[6:7, 0:SGU_W] = g_sln_b[...]
        for h in range(N_GRP):
            buf_b[h * BLK:(h + 1) * BLK, :] = g_sw[h]
        buf_b[N_GRP * BLK:_MAT_ROWS, :] = g_sbt[...]

        def remote(src, dst, send_sem, recv_sem, to):
            return pltpu.make_async_remote_copy(src_ref=src, dst_ref=dst, send_sem=send_sem, recv_sem=recv_sem,
                                                device_id=to, device_id_type=MESH)

        first = [remote(buf_a, pair_a, p1_send.at[0], p1_recv.at[0], sibling),
                 remote(buf_b, pair_b, p1_send.at[1], p1_recv.at[1], sibling)]
        for cp in first:
            cp.start()
        for cp in first:
            cp.wait()
        rows_a = pl.ds(pl.multiple_of(c * half_a, 8), half_a)
        rows_b = pl.ds(pl.multiple_of(c * half_b, 8), half_b)
        stage_a[me] = buf_a[rows_a, :] + pair_a[rows_a, :]
        stage_b[me] = buf_b[rows_b, :] + pair_b[rows_b, :]

        def chip_copies(d):
            to = (chips[d][0], chips[d][1], c)
            return [remote(stage_a.at[me], stage_a.at[me], x_send.at[2 * d], x_recv.at[2 * d], to),
                    remote(stage_b.at[me], stage_b.at[me], x_send.at[2 * d + 1], x_recv.at[2 * d + 1], to)]

        def chip_arrivals(d):
            slot = 2 * chips[d][0] + chips[d][1]
            to = (chips[d][0], chips[d][1], c)
            return [remote(stage_a.at[slot], stage_a.at[slot], x_send.at[2 * d], x_recv.at[2 * d], to),
                    remote(stage_b.at[slot], stage_b.at[slot], x_send.at[2 * d + 1], x_recv.at[2 * d + 1], to)]

        for d in range(3):
            for cp in chip_copies(d):
                cp.start()
        for d in range(3):
            for cp in chip_arrivals(d):
                cp.wait_recv()
        tot_a[rows_a, :] = ((stage_a[0] + stage_a[1]) + stage_a[2]) + stage_a[3]
        tot_b[rows_b, :] = ((stage_b[0] + stage_b[1]) + stage_b[2]) + stage_b[3]

        second = [remote(tot_a.at[rows_a, :], tot_a.at[rows_a, :], p2_send.at[0], p2_recv.at[0], sibling),
                  remote(tot_b.at[rows_b, :], tot_b.at[rows_b, :], p2_send.at[1], p2_recv.at[1], sibling)]
        for cp in second:
            cp.start()
        other_a = pl.ds(pl.multiple_of((1 - c) * half_a, 8), half_a)
        other_b = pl.ds(pl.multiple_of((1 - c) * half_b, 8), half_b)
        remote(tot_a.at[other_a, :], tot_a.at[other_a, :], p2_send.at[0], p2_recv.at[0], sibling).wait_recv()
        remote(tot_b.at[other_b, :], tot_b.at[other_b, :], p2_send.at[1], p2_recv.at[1], sibling).wait_recv()
        for cp in second:
            cp.wait_send()
        for d in range(3):
            for cp in chip_copies(d):
                cp.wait_send()
        out_a[...] = tot_a[...]
        out_b[...] = tot_b[...]

    ins = [local[k] for k in ("ln_in_g", "ln_in_b", "bq", "bkv", "bsuv", "sink", "sgu_ln_g", "sgu_ln_b", "sgu_w",
                              "sgu_bt", "b_out", "ln_mix_g", "ln_mix_b", "ln_ffn_g", "ln_ffn_b", "loss")]
    out_dims = [(_VEC_ROWS, D_MODEL), (_MAT_ROWS, 128)]
    vec = pltpu.VMEM((_VEC_ROWS, D_MODEL), F32)
    mat = pltpu.VMEM((_MAT_ROWS, 128), F32)
    return pl.pallas_call(
        body, name="small_allreduce", grid=(1,),
        in_specs=[_const2(a.shape) for a in ins], out_specs=[_const2(s) for s in out_dims],
        out_shape=[_hbm_shape(s, F32) for s in out_dims],
        scratch_shapes=[vec, mat, vec, mat, pltpu.VMEM((N_CHIP, _VEC_ROWS // 2, D_MODEL), F32),
                        pltpu.VMEM((N_CHIP, _MAT_ROWS // 2, 128), F32), vec, mat,
                        pltpu.SemaphoreType.DMA((2,)), pltpu.SemaphoreType.DMA((2,)), pltpu.SemaphoreType.DMA((6,)),
                        pltpu.SemaphoreType.DMA((6,)), pltpu.SemaphoreType.DMA((2,)), pltpu.SemaphoreType.DMA((2,))],
        compiler_params=pltpu.CompilerParams(vmem_limit_bytes=32 * MIB),
    )(*ins)


def _small_adamw(tot_a, tot_b, params):
    shapes = [params[nm][0].shape for nm in _SMALL]

    def body(*refs):
        ta, tb = refs[:2]
        prm = refs[2:2 + 3 * len(_SMALL)]
        outs = refs[2 + 3 * len(_SMALL):]

        def grad_of(k, name):
            if name == "sgu_w":
                return [tb[h * BLK:(h + 1) * BLK, :] for h in range(N_GRP)]
            if name == "sgu_b":
                return jnp.transpose(tb[N_GRP * BLK:_MAT_ROWS, :])[0:N_GRP, :]
            row = _VEC_ROW[name]
            if name == "b_in":
                return jnp.concatenate([ta[row:row + 1, :], ta[row + 1:row + 2, 0:IN_W - D_MODEL]], axis=1)
            return ta[row:row + 1, 0:shapes[k][-1]]

        for k, name in enumerate(_SMALL):
            w_ref, m_ref, v_ref = prm[3 * k:3 * k + 3]
            g_out, d_out, m_out, v_out = outs[4 * k:4 * k + 4]
            g = grad_of(k, name)
            if name == "sgu_w":
                for h in range(N_GRP):
                    d_, m_, v_ = _adamw_math(w_ref[h], g[h], m_ref[h], v_ref[h])
                    g_out[h], d_out[h], m_out[h], v_out[h] = g[h], d_, m_, v_
            else:
                d_, m_, v_ = _adamw_math(w_ref[...], g, m_ref[...], v_ref[...])
                g_out[...], d_out[...], m_out[...], v_out[...] = g, d_, m_, v_
        outs[-1][...] = ta[_LOSS_ROW:_LOSS_ROW + 1, :]

    ins = [tot_a, tot_b] + [_in_hbm(a) for nm in _SMALL for a in params[nm]]
    out_dims = [s for s in shapes for _ in range(4)] + [(1, D_MODEL)]
    res = pl.pallas_call(
        body, name="small_adamw", grid=(1,),
        in_specs=[_const2(a.shape) for a in ins], out_specs=[_const2(s) for s in out_dims],
        out_shape=[_hbm_shape(s, F32) for s in out_dims],
        compiler_params=_params(32),
    )(*ins)
    return {nm: tuple(res[4 * k:4 * k + 4]) for k, nm in enumerate(_SMALL)}, res[-1]


def _elementwise(name, fn, ins, out_dtypes, tile_rows=256):
    shape = ins[0].shape
    lead = shape[:-2]
    rows, cols = shape[-2:]
    tr = _tile(rows, tile_rows)
    n_lead = math.prod(lead)
    nr = rows // tr
    flat = [_in_hbm(a.reshape((n_lead, rows, cols))) for a in ins]

    def body(*refs):
        outs = fn(*[r[0] for r in refs[:len(ins)]])
        for o_ref, o in zip(refs[len(ins):], outs):
            o_ref[0] = o.astype(o_ref.dtype)

    spec = pl.BlockSpec((1, tr, cols), lambda i: (i // nr, i % nr, 0))
    res = pl.pallas_call(
        body, name=name, grid=(n_lead * nr,),
        in_specs=[spec] * len(ins), out_specs=[spec] * len(out_dtypes),
        out_shape=[_hbm_shape((n_lead, rows, cols), dt) for dt in out_dtypes],
        compiler_params=_params(32),
    )(*flat)
    return [r.reshape(shape) for r in res]


def _adamw_math(w, g, m, v):
    m = ADAM_B1 * m + (1.0 - ADAM_B1) * g
    v = ADAM_B2 * v + (1.0 - ADAM_B2) * (g * g)
    m_hat = m / (1.0 - ADAM_B1 ** ADAM_STEP)
    v_hat = v / (1.0 - ADAM_B2 ** ADAM_STEP)
    delta = -ADAM_LR * (m_hat / (jnp.sqrt(v_hat) + ADAM_EPS) + ADAM_WD * w)
    return delta, m, v


def _adamw(name, w, g, m, v, tile_rows=256):
    return _elementwise(name, lambda w_, g_, m_, v_: (g_,) + _adamw_math(w_, g_, m_, v_), [w, g, m, v],
                        [F32, F32, F32, F32], tile_rows)


def kernel(x, positions, ln_in_g, ln_in_b, w_in, b_in, attn_sinks, sgu_ln_g, sgu_ln_b, sgu_w, sgu_b, w_out, b_out, ln_mix_g, ln_mix_b, w_gate, w_up, w_down, ln_ffn_g, ln_ffn_b, loss_target, m_ln_in_g, m_ln_in_b, m_w_in, m_b_in, m_attn_sinks, m_sgu_ln_g, m_sgu_ln_b, m_sgu_w, m_sgu_b, m_w_out, m_b_out, m_ln_mix_g, m_ln_mix_b, m_w_gate, m_w_up, m_w_down, m_ln_ffn_g, m_ln_ffn_b, v_ln_in_g, v_ln_in_b, v_w_in, v_b_in, v_attn_sinks, v_sgu_ln_g, v_sgu_ln_b, v_sgu_w, v_sgu_b, v_w_out, v_b_out, v_ln_mix_g, v_ln_mix_b, v_w_gate, v_w_up, v_w_down, v_ln_ffn_g, v_ln_ffn_b):
    weights = dict(ln_in_g=ln_in_g, ln_in_b=ln_in_b, w_in=w_in, b_in=b_in, attn_sinks=attn_sinks, sgu_ln_g=sgu_ln_g,
                   sgu_ln_b=sgu_ln_b, sgu_w=sgu_w, sgu_b=sgu_b, w_out=w_out, b_out=b_out, ln_mix_g=ln_mix_g,
                   ln_mix_b=ln_mix_b, w_gate=w_gate, w_up=w_up, w_down=w_down, ln_ffn_g=ln_ffn_g, ln_ffn_b=ln_ffn_b)
    mom_m = dict(ln_in_g=m_ln_in_g, ln_in_b=m_ln_in_b, w_in=m_w_in, b_in=m_b_in, attn_sinks=m_attn_sinks,
                 sgu_ln_g=m_sgu_ln_g, sgu_ln_b=m_sgu_ln_b, sgu_w=m_sgu_w, sgu_b=m_sgu_b, w_out=m_w_out, b_out=m_b_out,
                 ln_mix_g=m_ln_mix_g, ln_mix_b=m_ln_mix_b, w_gate=m_w_gate, w_up=m_w_up, w_down=m_w_down,
                 ln_ffn_g=m_ln_ffn_g, ln_ffn_b=m_ln_ffn_b)
    mom_v = dict(ln_in_g=v_ln_in_g, ln_in_b=v_ln_in_b, w_in=v_w_in, b_in=v_b_in, attn_sinks=v_attn_sinks,
                 sgu_ln_g=v_sgu_ln_g, sgu_ln_b=v_sgu_ln_b, sgu_w=v_sgu_w, sgu_b=v_sgu_b, w_out=v_w_out, b_out=v_b_out,
                 ln_mix_g=v_ln_mix_g, ln_mix_b=v_ln_mix_b, w_gate=v_w_gate, w_up=v_w_up, w_down=v_w_down,
                 ln_ffn_g=v_ln_ffn_g, ln_ffn_b=v_ln_ffn_b)
    order = list(weights)
    big = ("w_in", "w_out", "w_gate", "w_up", "w_down")

    s_len = x.shape[1]
    xs = _in_hbm(x.reshape(s_len, D_MODEL))
    tgt = _in_hbm(loss_target.reshape(s_len, D_MODEL))
    pos_col = _in_hbm(positions.reshape(s_len, 1))
    g0, b0 = _in_hbm(ln_in_g.reshape(1, D_MODEL)), _in_hbm(ln_in_b.reshape(1, D_MODEL))
    sinks = attn_sinks.reshape(N_Q)
    sgu_w3 = _in_hbm(sgu_w.reshape(N_GRP, BLK, BLK))
    sgu_bt = _in_hbm(sgu_b.reshape(N_GRP, BLK).T)
    b_in, b_out, sgu_ln_g, sgu_ln_b, ln_mix_g, ln_mix_b, ln_ffn_g, ln_ffn_b = (
        _in_hbm(a) for a in (b_in, b_out, sgu_ln_g, sgu_ln_b, ln_mix_g, ln_mix_b, ln_ffn_g, ln_ffn_b))

    col_sharded = ("w_in", "w_gate", "w_up")

    def rowmajor(name, a):
        return jnp.swapaxes(a[0], 0, 1) if name in col_sharded else a[0]

    def as_given(name, a):
        return (jnp.swapaxes(a, 0, 1) if name in col_sharded else a)[None]

    shards = [rowmajor(n, weights[n]) for n in big]
    (gw_in,) = _gather_weights(shards[0:1])
    w_in_full = gw_in.reshape(IN_W, D_MODEL)

    *acts, gw_out = _ln_inproj(xs, pos_col, g0, b0, w_in_full, b_in, shards[1:2])
    q, k, v, su, sv, tc, t1, t2 = (_in_hbm(a) for a in acts)
    mc, gw_gate = _mixer_fwd(q, k, v, su, sv, sinks, sgu_ln_g, sgu_ln_b, sgu_w3, sgu_bt, shards[2:3])
    mc = _in_hbm(mc)
    w_out_full = gw_out.reshape(D_MODEL, D_MODEL)
    r1, gw_up = _outproj(mc, w_out_full, b_out, xs, g0, b0, shards[3:4])
    r1 = _in_hbm(r1)
    act, p_act, q_act, gw_down = _ffn_up(r1, ln_mix_g, ln_mix_b, gw_gate, gw_up, shards[4:5])
    act, p_act, q_act = _in_hbm(act), _in_hbm(p_act), _in_hbm(q_act)
    dr2, loss_cols, d_ln_ffn_g, d_ln_ffn_b = _ffn_down_loss(act, gw_down, r1, ln_mix_g, ln_mix_b, ln_ffn_g, ln_ffn_b, tgt)
    dr2 = _in_hbm(dr2)

    dg, du, acc_down, wire_down = _ffn_bwd_a(dr2, act, p_act, q_act, gw_down)
    dh1a, acc_gate, wire_gate, land_down = _ffn_bwd_g(dr2, _in_hbm(dg), r1, ln_mix_g, ln_mix_b, gw_gate, wire_down)
    dr1, acc_up, wire_up, d_ln_mix_g, d_ln_mix_b, land_gate = _ffn_bwd_u(_in_hbm(dh1a), _in_hbm(du), r1, ln_mix_g,
                                                                         ln_mix_b, gw_up, wire_gate)
    dr1 = _in_hbm(dr1)
    dmc, acc_out, wire_out, d_b_out, land_up = _outproj_bwd(dr1, mc, w_out_full, wire_up)
    (dq, dkv, dsuv, dbq, dbkv, dbsuv, d_sink, d_sgu_ln_g, d_sgu_ln_b, d_sgu_w, d_sgu_bt, land_out) = _mixer_bwd(
        q, k, v, su, sv, _in_hbm(dmc), tc, t1, t2, sinks, sgu_ln_g, sgu_ln_b, sgu_w3, sgu_bt, wire_out)
    grad_x, acc_in, d_ln_in_g, d_ln_in_b = _inproj_bwd(_in_hbm(dq), _in_hbm(dkv), _in_hbm(dsuv), dr1, xs, g0, b0,
                                                       w_in_full)

    reduced = _grad_finish(acc_in, [land_out, land_gate, land_up, land_down], [acc_out, acc_gate, acc_up, acc_down])
    small_shape = dict(ln_in_g=(1, D_MODEL), ln_in_b=(1, D_MODEL), sgu_w=(N_GRP, BLK, BLK), sgu_b=(N_GRP, BLK))
    small_local = dict(
        ln_in_g=d_ln_in_g, ln_in_b=d_ln_in_b, bq=dbq, bkv=dbkv, bsuv=dbsuv, sink=d_sink, sgu_ln_g=d_sgu_ln_g,
        sgu_ln_b=d_sgu_ln_b, sgu_w=d_sgu_w, sgu_bt=d_sgu_bt, b_out=d_b_out, ln_mix_g=d_ln_mix_g, ln_mix_b=d_ln_mix_b,
        ln_ffn_g=d_ln_ffn_g, ln_ffn_b=d_ln_ffn_b, loss=loss_cols)
    small_params = {nm: tuple(src[nm].reshape(small_shape.get(nm, src[nm].shape)) for src in (weights, mom_m, mom_v))
                    for nm in _SMALL}
    tot_a, tot_b = _small_allreduce({nm: _in_hbm(a) for nm, a in small_local.items()})
    small_out, loss_sum = _small_adamw(_in_hbm(tot_a), _in_hbm(tot_b), small_params)
    loss = jnp.sum(loss_sum) * (0.5 / D_MODEL)
    grads, delta, new_m, new_v = {}, {}, {}, {}
    for nm in _SMALL:
        grads[nm], delta[nm], new_m[nm], new_v[nm] = (a.reshape(weights[nm].shape) for a in small_out[nm])

    for t, name in enumerate(big):
        g_, d_, m_, v_ = _adamw("adamw_" + name, shards[t], reduced[t], rowmajor(name, mom_m[name]),
                                rowmajor(name, mom_v[name]))
        grads[name], delta[name], new_m[name], new_v[name] = (as_given(name, a) for a in (g_, d_, m_, v_))

    return (loss, grad_x.reshape(x.shape), *[grads[n] for n in order], *[delta[n] for n in order],
            *[new_m[n] for n in order], *[new_v[n] for n in order])
```

```python
import functools
import math

import jax
import jax.numpy as jnp
from jax import lax
from jax.experimental import pallas as pl
from jax.experimental.pallas import tpu as pltpu

F32 = jnp.float32
_MXU = jnp.bfloat16
_WIRE = jnp.bfloat16
_ACT = jnp.bfloat16

D_MODEL = 1024
ATTN_W = 512
SGU_W = 512
HEAD_DIM = 64
N_Q = 8
N_KV = 2
Q_PER_KV = 4
KV_W = 128
BLK = 128
ROT_DIM = 16
ROPE_THETA = 500000.0
N_GRP = 4
GRP_DIM = 128
D_FF = 2816
IN_W = 1792
LN_EPS = 1e-5
ALPHA = 2.0 ** 0.25
N_CHIP = 4
FF_SH = D_FF // N_CHIP
IN_SH = IN_W // N_CHIP
OUT_SH = D_MODEL // N_CHIP
ROW_CHUNK = 32
GATHER_CUT = 224

ADAM_LR = 0.001
ADAM_B1 = 0.9
ADAM_B2 = 0.999
ADAM_EPS = 1e-08
ADAM_WD = 0.01
ADAM_STEP = 10

SQRT_HALF = 0.7071067811865476
INV_SQRT_2PI = 0.3989422804014327
MESH_AXES = ("x", "y", "c")
MESH = pl.DeviceIdType.MESH
MIB = 2 ** 20


def _vmem():
    return pl.BlockSpec(memory_space=pltpu.VMEM)


def _smem():
    return pl.BlockSpec(memory_space=pltpu.SMEM)


def _hbm():
    return pl.BlockSpec(memory_space=pl.ANY)


def _hbm_shape(shape, dtype):
    return pltpu.HBM(shape, dtype)


def _in_hbm(a):
    return pltpu.with_memory_space_constraint(a, pltpu.HBM)


def _params(vmem_mib=48):
    return pltpu.CompilerParams(dimension_semantics=("arbitrary",), vmem_limit_bytes=vmem_mib * MIB)


def _tile(n, cap):
    if n <= cap:
        return n
    for t in range(cap - cap % 16, 0, -16):
        if n % t == 0:
            return t
    raise ValueError((n, cap))


def _rows(tm, width):
    return pl.BlockSpec((tm, width), lambda i: (i, 0))


def _const2(shape):
    return pl.BlockSpec(shape, lambda i: (0,) * len(shape))


def _ln(x, g, b):
    mu = jnp.mean(x, axis=-1, keepdims=True)
    xc = x - mu
    var = jnp.mean(xc * xc, axis=-1, keepdims=True)
    rstd = lax.rsqrt(var + LN_EPS)
    xhat = xc * rstd
    return xhat * g + b, xhat, rstd


def _ln_bwd(dy, xhat, rstd, g):
    gdy = dy * g
    m1 = jnp.mean(gdy, axis=-1, keepdims=True)
    m2 = jnp.mean(gdy * xhat, axis=-1, keepdims=True)
    return rstd * (gdy - m1 - xhat * m2)


def _colsum(a):
    return jnp.sum(a, axis=0, keepdims=True)


def _gelu_and_grad(x):
    cdf = 0.5 * (1.0 + lax.erf(x * SQRT_HALF))
    return x * cdf, cdf + x * jnp.exp(-0.5 * x * x) * INV_SQRT_2PI


def _dot(a, b):
    return jnp.dot(a, b, preferred_element_type=F32)


def _dot_nt(a, b):
    return lax.dot_general(a, b, (((1,), (1,)), ((), ())), preferred_element_type=F32)


def _dot_tn(a, b):
    return lax.dot_general(a, b, (((0,), (0,)), ((), ())), preferred_element_type=F32)


def _rope(t, tc, t1, t2):
    n = t.shape[1]
    rep = n // 128
    if rep > 1:
        tc, t1, t2 = (jnp.tile(a, (1, rep)) for a in (tc, t1, t2))
    return t * tc + pltpu.roll(t, n - 8, 1) * t1 + pltpu.roll(t, 8, 1) * t2


def _rope_bwd(d, tc, t1, t2):
    n = d.shape[1]
    rep = n // 128
    if rep > 1:
        tc, t1, t2 = (jnp.tile(a, (1, rep)) for a in (tc, t1, t2))
    return d * tc + pltpu.roll(d * t1, 8, 1) + pltpu.roll(d * t2, n - 8, 1)


def _band_mask(first_block):
    qi = lax.broadcasted_iota(jnp.int32, (BLK, 2 * BLK), 0)
    kj = lax.broadcasted_iota(jnp.int32, (BLK, 2 * BLK), 1)
    shut = jnp.where(first_block, 2 * BLK, 0)
    prev_ok = jnp.logical_and(kj < BLK, kj > qi + shut)
    cur_ok = jnp.logical_and(kj >= BLK, (kj - BLK) <= qi)
    return jnp.logical_or(prev_ok, cur_ok)


def _causal_w(w_ref, h):
    t = lax.broadcasted_iota(jnp.int32, (BLK, BLK), 0)
    s = lax.broadcasted_iota(jnp.int32, (BLK, BLK), 1)
    return jnp.where(s <= t, w_ref[h], 0.0)


def _lane_put(vals, width):
    rows = vals[0].shape[0]
    lane = lax.broadcasted_iota(jnp.int32, (rows, width), 1)
    out = jnp.zeros((rows, width), F32)
    for k, v in enumerate(vals):
        out = out + jnp.where(lane == k, v, 0.0)
    return out


def _rope_consts():
    lane = jnp.arange(128) % HEAD_DIM
    rot = lane < ROT_DIM
    pair = (2 * (lane % (ROT_DIM // 2))).astype(F32)
    freq = jnp.where(rot, ROPE_THETA ** (-pair / ROT_DIM), 0.0)
    rows = [freq, rot.astype(F32), 1.0 - rot.astype(F32), (lane < ROT_DIM // 2).astype(F32),
            jnp.logical_and(lane >= ROT_DIM // 2, rot).astype(F32)]
    rows += [jnp.zeros((128,), F32)] * 3
    return jnp.stack(rows).astype(F32)


def _ln_inproj(x, pos_col, g0, b0, w_in, b_in, plan):
    s_len = x.shape[0]
    tm = _tile(s_len, 512)
    m, n = len(plan.operands()), plan.n

    def body(x_ref, pos_ref, g_ref, b_ref, w_ref, bi_ref, rc_ref, *rest):
        q_ref, k_ref, v_ref, su_ref, sv_ref, tc_ref, t1_ref, t2_ref = rest[m:m + 8]
        gather = plan.bind(rest[:m], rest[m + 8:m + 8 + n], rest[m + 8 + n:])
        i = pl.program_id(0)

        @pl.when(i == 0)
        def _():
            gather.start()

        h0, _, _ = _ln(x_ref[...], g_ref[...], b_ref[...])
        proj = _dot_nt(h0.astype(_MXU), w_ref[...]) + bi_ref[...]
        ang = pos_ref[...].astype(F32) * rc_ref[0:1, :]
        cs = jnp.cos(ang)
        sn = jnp.sin(ang)
        tc = cs * rc_ref[1:2, :] + rc_ref[2:3, :]
        t1 = -sn * rc_ref[3:4, :]
        t2 = sn * rc_ref[4:5, :]
        tc_ref[...] = tc
        t1_ref[...] = t1
        t2_ref[...] = t2
        q = _rope(proj[:, 0:ATTN_W], tc, t1, t2) * (HEAD_DIM ** -0.5)
        q_ref[...] = q.astype(_MXU)
        k_ref[...] = _rope(proj[:, ATTN_W:ATTN_W + KV_W], tc, t1, t2).astype(_MXU)
        v_ref[...] = proj[:, ATTN_W + KV_W:ATTN_W + 2 * KV_W].astype(_MXU)
        su_ref[...] = proj[:, ATTN_W + 2 * KV_W:ATTN_W + 2 * KV_W + SGU_W]
        sv_ref[...] = proj[:, ATTN_W + 2 * KV_W + SGU_W:IN_W]

        @pl.when(i == pl.num_programs(0) - 1)
        def _():
            gather.finish()

    sd = _hbm_shape
    return pl.pallas_call(
        body, name="ln_inproj", grid=(s_len // tm,),
        in_specs=[_rows(tm, D_MODEL), _rows(tm, 1), _const2((1, D_MODEL)), _const2((1, D_MODEL)), _vmem(),
                  _const2((1, IN_W)), _const2((8, 128))] + plan.in_specs(),
        out_specs=[_rows(tm, ATTN_W), _rows(tm, KV_W), _rows(tm, KV_W), _rows(tm, SGU_W), _rows(tm, SGU_W),
                   _rows(tm, 128), _rows(tm, 128), _rows(tm, 128)] + plan.out_specs(),
        out_shape=[sd((s_len, ATTN_W), _MXU), sd((s_len, KV_W), _MXU), sd((s_len, KV_W), _MXU),
                   sd((s_len, SGU_W), F32), sd((s_len, SGU_W), F32),
                   sd((s_len, 128), F32), sd((s_len, 128), F32), sd((s_len, 128), F32)] + plan.out_shapes(),
        scratch_shapes=plan.scratch(),
        compiler_params=_params(56),
    )(x, pos_col, g0, b0, w_in, b_in, _rope_consts(), *plan.operands())


def _attn_probs(qh, kh, sink, allowed):
    s = jnp.where(allowed, _dot_nt(qh, kh), -1e30)
    m = jnp.maximum(jnp.max(s, axis=-1, keepdims=True), sink)
    p = jnp.exp(s - m)
    ps = jnp.exp(sink - m)
    inv = 1.0 / (jnp.sum(p, axis=-1, keepdims=True) + ps)
    return p * inv, ps * inv


def _sgu_fwd(su, sv, lg, lb, w_ref, bt_ref):
    u, du_dsu = _gelu_and_grad(su)
    gv, dgv_dsv = _gelu_and_grad(sv)
    vv, vhat, rstd = _ln(gv, lg, lb)
    vvb = vv.astype(_MXU)
    wcs, mixed = [], []
    for h in range(N_GRP):
        wc = _causal_w(w_ref, h).astype(_MXU)
        wcs.append(wc)
        mixed.append(_dot(wc, vvb[:, h * GRP_DIM:(h + 1) * GRP_DIM]) + bt_ref[:, h:h + 1])
    return u, jnp.concatenate(mixed, axis=1), du_dsu, dgv_dsv, vhat, rstd, vvb, wcs


def _prev_map(i):
    return (jnp.maximum(i - 1, 0), 0)


def _mixer_fwd(q, k, v, su, sv, sinks, sg, sb, sgu_w, sgu_bt, plan):
    s_len = q.shape[0]
    nb = s_len // BLK
    m, n = len(plan.operands()), plan.n

    def body(q_ref, kc_ref, kp_ref, vc_ref, vp_ref, su_ref, sv_ref, sink_ref, lg_ref, lb_ref, w_ref, bt_ref, *rest):
        mc_ref = rest[m]
        gather = plan.bind(rest[:m], rest[m + 1:m + 1 + n], rest[m + 1 + n:])
        i = pl.program_id(0)

        @pl.when(i == 0)
        def _():
            gather.start()

        @pl.when(i == nb - 1)
        def _():
            gather.finish()

        allowed = _band_mask(i == 0)
        kb = jnp.concatenate([kp_ref[...], kc_ref[...]], axis=0)
        vb = jnp.concatenate([vp_ref[...], vc_ref[...]], axis=0)
        qv = q_ref[...]
        outs = []
        for h in range(N_Q):
            g = h // Q_PER_KV
            kh = kb[:, g * HEAD_DIM:(g + 1) * HEAD_DIM]
            vh = vb[:, g * HEAD_DIM:(g + 1) * HEAD_DIM]
            probs, _ = _attn_probs(qv[:, h * HEAD_DIM:(h + 1) * HEAD_DIM], kh, sink_ref[h], allowed)
            outs.append(_dot(probs.astype(_MXU), vh))
        u, mixed = _sgu_fwd(su_ref[...], sv_ref[...], lg_ref[...], lb_ref[...], w_ref, bt_ref)[:2]
        mc_ref[...] = jnp.concatenate(outs + [u * mixed], axis=1).astype(_MXU)

    cur = lambda w: pl.BlockSpec((BLK, w), lambda i: (i, 0))
    prev = lambda w: pl.BlockSpec((BLK, w), _prev_map)
    return pl.pallas_call(
        body, name="mixer_fwd", grid=(nb,),
        in_specs=[cur(ATTN_W), cur(KV_W), prev(KV_W), cur(KV_W), prev(KV_W), cur(SGU_W), cur(SGU_W), _smem(),
                  _const2((1, SGU_W)), _const2((1, SGU_W)), _const2((N_GRP, BLK, BLK)), _const2((BLK, N_GRP))]
        + plan.in_specs(),
        out_specs=[cur(D_MODEL)] + plan.out_specs(),
        out_shape=[_hbm_shape((s_len, D_MODEL), _MXU)] + plan.out_shapes(),
        scratch_shapes=plan.scratch(),
        compiler_params=_params(56),
    )(q, k, k, v, v, su, sv, sinks, sg, sb, sgu_w, sgu_bt, *plan.operands())


def _outproj(mc, w_out, b_out, x, g0, b0, plan):
    s_len = x.shape[0]
    tm = _tile(s_len, 512)
    m, n = len(plan.operands()), plan.n

    def body(mc_ref, w_ref, bo_ref, x_ref, g_ref, b_ref, *rest):
        r1_ref = rest[m]
        gather = plan.bind(rest[:m], rest[m + 1:m + 1 + n], rest[m + 1 + n:])
        i = pl.program_id(0)

        @pl.when(i == 0)
        def _():
            gather.start()

        h0, _, _ = _ln(x_ref[...], g_ref[...], b_ref[...])
        r1_ref[...] = ALPHA * h0 + (_dot(mc_ref[...], w_ref[...]) + bo_ref[...])

        @pl.when(i == pl.num_programs(0) - 1)
        def _():
            gather.finish()

    return pl.pallas_call(
        body, name="outproj", grid=(s_len // tm,),
        in_specs=[_rows(tm, D_MODEL), _vmem(), _const2((1, D_MODEL)), _rows(tm, D_MODEL),
                  _const2((1, D_MODEL)), _const2((1, D_MODEL))] + plan.in_specs(),
        out_specs=[_rows(tm, D_MODEL)] + plan.out_specs(),
        out_shape=[_hbm_shape((s_len, D_MODEL), F32)] + plan.out_shapes(),
        scratch_shapes=plan.scratch(),
        compiler_params=_params(40),
    )(mc, w_out, b_out, x, g0, b0, *plan.operands())


def _ffn_spec(tm):
    return pl.BlockSpec((N_CHIP, tm, FF_SH), lambda i: (0, i, 0))


def _ffn_up(r1, g1, b1, wg, wu, plan):
    s_len = r1.shape[0]
    tm = _tile(s_len, 512)
    m, n = len(plan.operands()), plan.n

    def body(r1_ref, g_ref, b_ref, wg_ref, wu_ref, *rest):
        a_ref, p_ref, q_ref = rest[m:m + 3]
        gather = plan.bind(rest[:m], rest[m + 3:m + 3 + n], rest[m + 3 + n:])
        i = pl.program_id(0)

        @pl.when(i == 0)
        def _():
            gather.start()

        h1, _, _ = _ln(r1_ref[...], g_ref[...], b_ref[...])
        h1b = h1.astype(_MXU)
        for j in range(N_CHIP):
            g = _dot_nt(h1b, wg_ref[j])
            u = _dot_nt(h1b, wu_ref[j])
            silu, sg = _silu_parts(g)
            a_ref[j] = (silu * u).astype(_MXU)
            p_ref[j] = silu.astype(_ACT)
            q_ref[j] = (u * (sg * (1.0 + g * (1.0 - sg)))).astype(_ACT)

        @pl.when(i == pl.num_programs(0) - 1)
        def _():
            gather.finish()

    sd = _hbm_shape((N_CHIP, s_len, FF_SH), _ACT)
    return pl.pallas_call(
        body, name="ffn_up", grid=(s_len // tm,),
        in_specs=[_rows(tm, D_MODEL), _const2((1, D_MODEL)), _const2((1, D_MODEL)), _vmem(), _vmem()] + plan.in_specs(),
        out_specs=[_ffn_spec(tm)] * 3 + plan.out_specs(),
        out_shape=[_hbm_shape((N_CHIP, s_len, FF_SH), _MXU), sd, sd] + plan.out_shapes(),
        scratch_shapes=plan.scratch(),
        compiler_params=_params(56),
    )(r1, g1, b1, wg, wu, *plan.operands())


def _silu_parts(g):
    sg = 1.0 / (1.0 + jnp.exp(-g))
    return g * sg, sg


def _ffn_down_loss(act, wd, r1, g1, b1, g2, b2, target):
    s_len = r1.shape[0]
    tm = _tile(s_len, 512)

    parts = 2 if tm % 32 == 0 else 1
    sub = tm // parts

    def body(a_ref, wd_ref, r1_ref, g1_ref, b1_ref, g2_ref, b2_ref, t_ref, dr2_ref, loss_ref, dg2_ref, db2_ref):
        i = pl.program_id(0)

        @pl.when(i == 0)
        def _():
            loss_ref[...] = jnp.zeros_like(loss_ref)
            dg2_ref[...] = jnp.zeros_like(dg2_ref)
            db2_ref[...] = jnp.zeros_like(db2_ref)

        for part in range(parts):
            rows = slice(part * sub, (part + 1) * sub)
            f = jnp.zeros((sub, D_MODEL), F32)
            for j in range(N_CHIP):
                f = f + _dot(a_ref[j, rows, :], wd_ref[j])
            h1, _, _ = _ln(r1_ref[rows, :], g1_ref[...], b1_ref[...])
            h2, r2hat, rstd2 = _ln(ALPHA * h1 + f, g2_ref[...], b2_ref[...])
            diff = h2 - t_ref[rows, :]
            dh2 = diff * (1.0 / D_MODEL)
            loss_ref[...] += _colsum(diff * diff)
            dg2_ref[...] += _colsum(dh2 * r2hat)
            db2_ref[...] += _colsum(dh2)
            dr2_ref[rows, :] = _ln_bwd(dh2, r2hat, rstd2, g2_ref[...])

    vec = _hbm_shape((1, D_MODEL), F32)
    c = _const2((1, D_MODEL))
    return pl.pallas_call(
        body, name="ffn_down_loss", grid=(s_len // tm,),
        in_specs=[_ffn_spec(tm), _vmem(), _rows(tm, D_MODEL), c, c, c, c, _rows(tm, D_MODEL)],
        out_specs=[_rows(tm, D_MODEL), c, c, c],
        out_shape=[_hbm_shape((s_len, D_MODEL), F32), vec, vec, vec],
        compiler_params=_params(48),
    )(act, wd, r1, g1, b1, g2, b2, target)


def _ffn_bwd_a(dr2, act, p_act, q_act, wd):
    s_len = dr2.shape[0]
    tm = _tile(s_len, 512)

    def body(dr2_ref, a_ref, p_ref, q_ref, wd_ref, dg_ref, du_ref, dwd_ref, wire_ref, land_ref, send_sem, recv_sem):
        i = pl.program_id(0)

        @pl.when(i == 0)
        def _():
            dwd_ref[...] = jnp.zeros_like(dwd_ref)

        dfb = dr2_ref[...].astype(_MXU)
        for j in range(N_CHIP):
            da = _dot_nt(dfb, wd_ref[j])
            dg_ref[j] = (da * q_ref[j].astype(F32)).astype(_MXU)
            du_ref[j] = (da * p_ref[j].astype(F32)).astype(_MXU)
            dwd_ref[j * FF_SH:(j + 1) * FF_SH, :] += _dot_tn(a_ref[j], dfb)

        @pl.when(i == pl.num_programs(0) - 1)
        def _():
            _pair_reduce(dwd_ref, land_ref, wire_ref, send_sem, recv_sem)

    sd = _hbm_shape((N_CHIP, s_len, FF_SH), _MXU)
    return pl.pallas_call(
        body, name="ffn_bwd_a", grid=(s_len // tm,),
        in_specs=[_rows(tm, D_MODEL), _ffn_spec(tm), _ffn_spec(tm), _ffn_spec(tm), _vmem()],
        out_specs=[_ffn_spec(tm), _ffn_spec(tm), _vmem(), _vmem()],
        out_shape=[sd, sd, jax.ShapeDtypeStruct((D_FF, D_MODEL), F32),
                   jax.ShapeDtypeStruct((N_CHIP, FF_SH // 2, D_MODEL), _WIRE)],
        scratch_shapes=_pair_scratch((N_CHIP, FF_SH // 2, D_MODEL)),
        compiler_params=_params(61),
    )(dr2, act, p_act, q_act, wd)


def _ffn_bwd_g(dr2, dg, r1, g1, b1, wg, prev_wire):
    s_len = dr2.shape[0]
    tm = _tile(s_len, 512)

    def body(dr2_ref, dg_ref, r1_ref, g1_ref, b1_ref, wg_ref, pw_ref, dh1_ref, dwg_ref, wire_ref, pl_ref,
             land_ref, send_sem, recv_sem, xl_ref, x_send, x_recv, x_flush):
        i = pl.program_id(0)
        exchange = _ChipExchange(pw_ref, xl_ref, x_send, x_recv)

        @pl.when(i == 0)
        def _():
            exchange.start()
            dwg_ref[...] = jnp.zeros_like(dwg_ref)

        h1, _, _ = _ln(r1_ref[...], g1_ref[...], b1_ref[...])
        h1b = h1.astype(_MXU)
        dh1 = ALPHA * dr2_ref[...]
        for j in range(N_CHIP):
            dgj = dg_ref[j]
            dh1 = dh1 + _dot(dgj, wg_ref[j])
            dwg_ref[j * FF_SH:(j + 1) * FF_SH, :] += _dot_tn(dgj, h1b)
        dh1_ref[...] = dh1

        @pl.when(i == pl.num_programs(0) - 1)
        def _():
            _pair_reduce(dwg_ref, land_ref, wire_ref, send_sem, recv_sem)
            exchange.finish_to(pl_ref, x_flush)

    c = _const2((1, D_MODEL))
    return pl.pallas_call(
        body, name="ffn_bwd_g", grid=(s_len // tm,),
        in_specs=[_rows(tm, D_MODEL), _ffn_spec(tm), _rows(tm, D_MODEL), c, c, _vmem(), _vmem()],
        out_specs=[_rows(tm, D_MODEL), _vmem(), _vmem(), _hbm()],
        out_shape=[_hbm_shape((s_len, D_MODEL), F32), jax.ShapeDtypeStruct((D_FF, D_MODEL), F32),
                   jax.ShapeDtypeStruct((N_CHIP, FF_SH // 2, D_MODEL), _WIRE), _ChipExchange.land_shape(prev_wire)],
        scratch_shapes=_pair_scratch((N_CHIP, FF_SH // 2, D_MODEL)) + _ChipExchange.scratch(prev_wire),
        compiler_params=_params(58),
    )(dr2, dg, r1, g1, b1, wg, prev_wire)


def _ffn_bwd_u(dh1a, du, r1, g1, b1, wu, prev_wire):
    s_len = dh1a.shape[0]
    tm = _tile(s_len, 512)

    def body(dh1_ref, du_ref, r1_ref, g1_ref, b1_ref, wu_ref, pw_ref,
             dr1_ref, dwu_ref, wire_ref, dg1_ref, db1_ref, pl_ref,
             land_ref, send_sem, recv_sem, xl_ref, x_send, x_recv, x_flush):
        i = pl.program_id(0)
        exchange = _ChipExchange(pw_ref, xl_ref, x_send, x_recv)

        @pl.when(i == 0)
        def _():
            exchange.start()
            dwu_ref[...] = jnp.zeros_like(dwu_ref)
            dg1_ref[...] = jnp.zeros_like(dg1_ref)
            db1_ref[...] = jnp.zeros_like(db1_ref)

        h1, r1hat, rstd1 = _ln(r1_ref[...], g1_ref[...], b1_ref[...])
        h1b = h1.astype(_MXU)
        dh1 = dh1_ref[...]
        for j in range(N_CHIP):
            duj = du_ref[j]
            dh1 = dh1 + _dot(duj, wu_ref[j])
            dwu_ref[j * FF_SH:(j + 1) * FF_SH, :] += _dot_tn(duj, h1b)
        dg1_ref[...] += _colsum(dh1 * r1hat)
        db1_ref[...] += _colsum(dh1)
        dr1_ref[...] = _ln_bwd(dh1, r1hat, rstd1, g1_ref[...])

        @pl.when(i == pl.num_programs(0) - 1)
        def _():
            _pair_reduce(dwu_ref, land_ref, wire_ref, send_sem, recv_sem)
            exchange.finish_to(pl_ref, x_flush)

    vec = _hbm_shape((1, D_MODEL), F32)
    c = _const2((1, D_MODEL))
    return pl.pallas_call(
        body, name="ffn_bwd_u", grid=(s_len // tm,),
        in_specs=[_rows(tm, D_MODEL), _ffn_spec(tm), _rows(tm, D_MODEL), c, c, _vmem(), _vmem()],
        out_specs=[_rows(tm, D_MODEL), _vmem(), _vmem(), c, c, _hbm()],
        out_shape=[_hbm_shape((s_len, D_MODEL), F32), jax.ShapeDtypeStruct((D_FF, D_MODEL), F32),
                   jax.ShapeDtypeStruct((N_CHIP, FF_SH // 2, D_MODEL), _WIRE), vec, vec,
                   _ChipExchange.land_shape(prev_wire)],
        scratch_shapes=_pair_scratch((N_CHIP, FF_SH // 2, D_MODEL)) + _ChipExchange.scratch(prev_wire),
        compiler_params=_params(58),
    )(dh1a, du, r1, g1, b1, wu, prev_wire)


def _outproj_bwd(dr1, mc, w_out, prev_wire):
    s_len = dr1.shape[0]
    tm = _tile(s_len, 512)

    def body(dr1_ref, mc_ref, w_ref, pw_ref, dmc_ref, dw_ref, wire_ref, db_ref, pl_ref,
             land_ref, send_sem, recv_sem, xl_ref, x_send, x_recv, x_flush):
        i = pl.program_id(0)
        exchange = _ChipExchange(pw_ref, xl_ref, x_send, x_recv)

        @pl.when(i == 0)
        def _():
            exchange.start()
            dw_ref[...] = jnp.zeros_like(dw_ref)
            db_ref[...] = jnp.zeros_like(db_ref)

        d = dr1_ref[...]
        db_ref[...] += _colsum(d)
        db16 = d.astype(_MXU)
        dmc_ref[...] = _dot_nt(db16, w_ref[...])
        dw_ref[...] += _dot_tn(mc_ref[...], db16)

        @pl.when(i == pl.num_programs(0) - 1)
        def _():
            _pair_reduce(dw_ref, land_ref, wire_ref, send_sem, recv_sem)
            exchange.finish_to(pl_ref, x_flush)

    return pl.pallas_call(
        body, name="outproj_bwd", grid=(s_len // tm,),
        in_specs=[_rows(tm, D_MODEL), _rows(tm, D_MODEL), _vmem(), _vmem()],
        out_specs=[_rows(tm, D_MODEL), _vmem(), _vmem(), _const2((1, D_MODEL)), _hbm()],
        out_shape=[_hbm_shape((s_len, D_MODEL), F32), jax.ShapeDtypeStruct((D_MODEL, D_MODEL), F32),
                   jax.ShapeDtypeStruct((N_CHIP, OUT_SH // 2, D_MODEL), _WIRE), _hbm_shape((1, D_MODEL), F32),
                   _ChipExchange.land_shape(prev_wire)],
        scratch_shapes=_pair_scratch((N_CHIP, OUT_SH // 2, D_MODEL)) + _ChipExchange.scratch(prev_wire),
        compiler_params=_params(48),
    )(dr1, mc, w_out, prev_wire)


def _mixer_bwd(q, k, v, su, sv, dmc, tc, t1, t2, sinks, sg, sb, sgu_w, sgu_bt, prev_wire):
    s_len = q.shape[0]
    nb = s_len // BLK

    def body(q_ref, kc_ref, kp_ref, vc_ref, vp_ref, su_ref, sv_ref, dmc_ref,
             tc_ref, t1_ref, t2_ref, tcp_ref, t1p_ref, t2p_ref,
             sink_ref, lg_ref, lb_ref, w_ref, bt_ref, pw_ref,
             dq_ref, dkv_ref, dsuv_ref, dbq_ref, dbkv_ref, dbsuv_ref,
             dsink_ref, dlg_ref, dlb_ref, dw_ref, dbt_ref, pl_ref, carry_ref, xl_ref, x_send, x_recv, x_flush):
        i = pl.program_id(0)
        exchange = _ChipExchange(pw_ref, xl_ref, x_send, x_recv)

        @pl.when(i == 0)
        def _():
            exchange.start()

        @pl.when(i == 0)
        def _():
            for r in (dbq_ref, dbkv_ref, dbsuv_ref, dsink_ref, dlg_ref, dlb_ref, dw_ref, dbt_ref):
                r[...] = jnp.zeros_like(r)

        def emit_kv(fin):
            dk = _rope_bwd(fin[:, 0:KV_W], tcp_ref[...], t1p_ref[...], t2p_ref[...])
            out = jnp.concatenate([dk, fin[:, KV_W:2 * KV_W]], axis=1)
            dkv_ref[...] = out.astype(_MXU)
            dbkv_ref[...] += _colsum(out)

        @pl.when(i < nb)
        def _():
            allowed = _band_mask(i == 0)
            kb = jnp.concatenate([kp_ref[...], kc_ref[...]], axis=0)
            vb = jnp.concatenate([vp_ref[...], vc_ref[...]], axis=0)
            qv = q_ref[...]
            dmc = dmc_ref[...]
            dqs, dks, dvs, dsinks = [], [], [], []
            for g in range(N_KV):
                kh = kb[:, g * HEAD_DIM:(g + 1) * HEAD_DIM]
                vh = vb[:, g * HEAD_DIM:(g + 1) * HEAD_DIM]
                dk_g = jnp.zeros((2 * BLK, HEAD_DIM), F32)
                dv_g = jnp.zeros((2 * BLK, HEAD_DIM), F32)
                for hh in range(Q_PER_KV):
                    h = g * Q_PER_KV + hh
                    qh = qv[:, h * HEAD_DIM:(h + 1) * HEAD_DIM]
                    probs, psink = _attn_probs(qh, kh, sink_ref[h], allowed)
                    pb = probs.astype(_MXU)
                    dob = dmc[:, h * HEAD_DIM:(h + 1) * HEAD_DIM].astype(_MXU)
                    dv_g = dv_g + _dot_tn(pb, dob)
                    dp = _dot_nt(dob, vh)
                    rd = jnp.sum(probs * dp, axis=-1, keepdims=True)
                    dsb = (probs * (dp - rd)).astype(_MXU)
                    dsinks.append(-jnp.sum(psink * rd, axis=0, keepdims=True))
                    dqs.append(_dot(dsb, kh))
                    dk_g = dk_g + _dot_tn(dsb, qh)
                dks.append(dk_g)
                dvs.append(dv_g)
            dq = _rope_bwd(jnp.concatenate(dqs, axis=1) * (HEAD_DIM ** -0.5), tc_ref[...], t1_ref[...], t2_ref[...])
            dq_ref[...] = dq.astype(_MXU)
            dbq_ref[...] += _colsum(dq)
            dsink_ref[...] += _lane_put(dsinks, 128)
            contrib = jnp.concatenate(dks + dvs, axis=1)

            @pl.when(i > 0)
            def _():
                emit_kv(carry_ref[...] + contrib[0:BLK, :])

            carry_ref[...] = contrib[BLK:2 * BLK, :]

            su = su_ref[...]
            sv = sv_ref[...]
            lg = lg_ref[...]
            u, mixed, du_dsu, dgv_dsv, vhat, rstd, vvb, wcs = _sgu_fwd(su, sv, lg, lb_ref[...], w_ref, bt_ref)
            dsgu = dmc[:, ATTN_W:D_MODEL]
            dsu = dsgu * mixed * du_dsu
            dmixed = dsgu * u
            tri_t = lax.broadcasted_iota(jnp.int32, (BLK, BLK), 0)
            tri_s = lax.broadcasted_iota(jnp.int32, (BLK, BLK), 1)
            dvv, dbs = [], []
            for h in range(N_GRP):
                dm = dmixed[:, h * GRP_DIM:(h + 1) * GRP_DIM]
                dmb = dm.astype(_MXU)
                dbs.append(jnp.sum(dm, axis=1, keepdims=True))
                dw_ref[h] += jnp.where(tri_s <= tri_t, _dot_nt(dmb, vvb[:, h * GRP_DIM:(h + 1) * GRP_DIM]), 0.0)
                dvv.append(_dot_tn(wcs[h], dmb))
            dvv = jnp.concatenate(dvv, axis=1)
            dbt_ref[...] += _lane_put(dbs, 128)
            dlg_ref[...] += _colsum(dvv * vhat)
            dlb_ref[...] += _colsum(dvv)
            dsv = _ln_bwd(dvv, vhat, rstd, lg) * dgv_dsv
            dsuv = jnp.concatenate([dsu, dsv], axis=1)
            dsuv_ref[...] = dsuv.astype(_MXU)
            dbsuv_ref[...] += _colsum(dsuv)

        @pl.when(i == nb)
        def _():
            emit_kv(carry_ref[...])
            exchange.finish_to(pl_ref, x_flush)

    last = nb - 1
    cur = lambda w: pl.BlockSpec((BLK, w), lambda i: (jnp.minimum(i, last), 0))
    prev = lambda w: pl.BlockSpec((BLK, w), lambda i: (jnp.clip(i - 1, 0, last), 0))
    sd = _hbm_shape
    return pl.pallas_call(
        body, name="mixer_bwd", grid=(nb + 1,),
        in_specs=[cur(ATTN_W), cur(KV_W), prev(KV_W), cur(KV_W), prev(KV_W), cur(SGU_W), cur(SGU_W), cur(D_MODEL),
                  cur(128), cur(128), cur(128), prev(128), prev(128), prev(128),
                  _smem(), _const2((1, SGU_W)), _const2((1, SGU_W)), _const2((N_GRP, BLK, BLK)), _const2((BLK, N_GRP)),
                  _vmem()],
        out_specs=[cur(ATTN_W), prev(2 * KV_W), cur(2 * SGU_W),
                   _const2((1, ATTN_W)), _const2((1, 2 * KV_W)), _const2((1, 2 * SGU_W)),
                   _const2((1, 128)), _const2((1, SGU_W)), _const2((1, SGU_W)),
                   _const2((N_GRP, BLK, BLK)), _const2((BLK, 128)), _hbm()],
        out_shape=[sd((s_len, ATTN_W), _MXU), sd((s_len, 2 * KV_W), _MXU), sd((s_len, 2 * SGU_W), _MXU),
                   sd((1, ATTN_W), F32), sd((1, 2 * KV_W), F32), sd((1, 2 * SGU_W), F32),
                   sd((1, 128), F32), sd((1, SGU_W), F32), sd((1, SGU_W), F32),
                   sd((N_GRP, BLK, BLK), F32), sd((BLK, 128), F32), _ChipExchange.land_shape(prev_wire)],
        scratch_shapes=[pltpu.VMEM((BLK, 2 * KV_W), F32)] + _ChipExchange.scratch(prev_wire),
        compiler_params=_params(32),
    )(q, k, k, v, v, su, sv, dmc, tc, t1, t2, tc, t1, t2, sinks, sg, sb, sgu_w, sgu_bt, prev_wire)


def _inproj_bwd(dq, dkv, dsuv, dr1, x, g0, b0, w_in):
    s_len = x.shape[0]
    tm = _tile(s_len, 512)
    cuts = ((0, ATTN_W), (ATTN_W, ATTN_W + 2 * KV_W), (ATTN_W + 2 * KV_W, IN_W))

    def body(dq_ref, dkv_ref, dsuv_ref, dr1_ref, x_ref, g_ref, b_ref, w_ref, dx_ref, dw_ref, dg_ref, db_ref):
        i = pl.program_id(0)

        @pl.when(i == 0)
        def _():
            dw_ref[...] = jnp.zeros_like(dw_ref)
            dg_ref[...] = jnp.zeros_like(dg_ref)
            db_ref[...] = jnp.zeros_like(db_ref)

        h0, xhat, rstd = _ln(x_ref[...], g_ref[...], b_ref[...])
        h0b = h0.astype(_MXU)
        dh0 = ALPHA * dr1_ref[...]
        for (lo, hi), d_ref in zip(cuts, (dq_ref, dkv_ref, dsuv_ref)):
            d = d_ref[...]
            dh0 = dh0 + _dot(d, w_ref[lo:hi, :])
            dw_ref[lo:hi, :] += _dot_tn(d, h0b)
        dg_ref[...] += _colsum(dh0 * xhat)
        db_ref[...] += _colsum(dh0)
        dx_ref[...] = _ln_bwd(dh0, xhat, rstd, g_ref[...])

    vec = _hbm_shape((1, D_MODEL), F32)
    c = _const2((1, D_MODEL))
    return pl.pallas_call(
        body, name="inproj_bwd", grid=(s_len // tm,),
        in_specs=[_rows(tm, ATTN_W), _rows(tm, 2 * KV_W), _rows(tm, 2 * SGU_W), _rows(tm, D_MODEL), _rows(tm, D_MODEL),
                  c, c, _vmem()],
        out_specs=[_rows(tm, D_MODEL), _vmem(), c, c],
        out_shape=[_hbm_shape((s_len, D_MODEL), F32), jax.ShapeDtypeStruct((IN_W, D_MODEL), F32), vec, vec],
        compiler_params=_params(48),
    )(dq, dkv, dsuv, dr1, x, g0, b0, w_in)


def _place():
    x, y, c = (lax.axis_index(a) for a in MESH_AXES)
    chips = [(1 - x, y), (x, 1 - y), (1 - x, 1 - y)]
    return x, y, c, chips


class _Gather:
    def __init__(self, ins, outs, send_sems, recv_sems, spans=None):
        self.ins, self.outs, self.send_sems, self.recv_sems = ins, outs, send_sems, recv_sems
        self.n = len(ins)
        self.spans = spans or [(0, r.shape[0]) for r in ins]
        self.halves = [(hi - lo) // 2 for lo, hi in self.spans]

    def _copy(self, k, t, slot, half, to):
        rows = pl.ds(pl.multiple_of(self.spans[t][0] + half * self.halves[t], 16), self.halves[t])
        piece = self.outs[t].at[slot, rows, :]
        return pltpu.make_async_remote_copy(src_ref=piece, dst_ref=piece, send_sem=self.send_sems.at[k],
                                            recv_sem=self.recv_sems.at[k], device_id=to, device_id_type=MESH)

    def _chip_copy(self, t, d, slot):
        x, y, c, chips = _place()
        return self._copy(3 * t + d, t, slot, c, (chips[d][0], chips[d][1], c))

    def _pass_copy(self, t, d, half):
        x, y, c, chips = _place()
        return self._copy(3 * self.n + 3 * t + d, t, 2 * chips[d][0] + chips[d][1], half, (x, y, 1 - c))

    def start(self):
        x, y, c, chips = _place()
        me = 2 * x + y
        for t in range(self.n):
            lo, hi = self.spans[t]
            self.outs[t][me, lo:hi, :] = self.ins[t][lo:hi, :].astype(_WIRE)
        for t in range(self.n):
            for d in range(3):
                self._chip_copy(t, d, me).start()

    def finish(self):
        x, y, c, chips = _place()
        me = 2 * x + y
        for t in range(self.n):
            for d in range(3):
                self._chip_copy(t, d, 2 * chips[d][0] + chips[d][1]).wait_recv()
                self._pass_copy(t, d, c).start()
        for t in range(self.n):
            for d in range(3):
                self._pass_copy(t, d, 1 - c).wait_recv()
        for t in range(self.n):
            for d in range(3):
                self._chip_copy(t, d, me).wait_send()
                self._pass_copy(t, d, c).wait_send()

    @staticmethod
    def out_shapes(shards, make=jax.ShapeDtypeStruct):
        return [make((N_CHIP,) + s.shape, _WIRE) for s in shards]

    @staticmethod
    def sems(n):
        return [pltpu.SemaphoreType.DMA((6 * n,)), pltpu.SemaphoreType.DMA((6 * n,))]


class _GatherPlan:
    def __init__(self, pieces):
        self.shards = [p[0] for p in pieces]
        self.spans = [p[1] for p in pieces]
        self.earlier = [p[2] for p in pieces]
        self.n = len(pieces)
        self.carried = [t for t in range(self.n) if self.earlier[t] is not None]

    def operands(self):
        return self.shards + [self.earlier[t] for t in self.carried]

    def in_specs(self):
        return [_vmem()] * self.n + [_hbm()] * len(self.carried)

    def out_specs(self):
        return [_hbm()] * self.n

    def out_shapes(self):
        return _Gather.out_shapes(self.shards, _hbm_shape)

    def scratch(self):
        return ([pltpu.VMEM((N_CHIP,) + s.shape, _WIRE) for s in self.shards] + _Gather.sems(self.n)
                + [pltpu.SemaphoreType.DMA((self.n,)), pltpu.SemaphoreType.DMA((max(len(self.carried), 1),))])

    def bind(self, in_refs, out_refs, scratch_refs):
        plan = self
        shard_refs, earlier_refs = in_refs[:self.n], in_refs[self.n:]
        bufs = scratch_refs[:self.n]
        send_sems, recv_sems, flush_sems, carry_sems = scratch_refs[self.n:self.n + 4]
        gather = _Gather(shard_refs, bufs, send_sems, recv_sems, self.spans)

        def carry_copy(k):
            t = plan.carried[k]
            lo = plan.spans[t][0]
            return pltpu.make_async_copy(earlier_refs[k].at[:, 0:lo, :], bufs[t].at[:, 0:lo, :], carry_sems.at[k])

        class Bound:
            @staticmethod
            def start():
                for k in range(len(plan.carried)):
                    carry_copy(k).start()
                gather.start()

            @staticmethod
            def finish():
                gather.finish()
                for k in range(len(plan.carried)):
                    carry_copy(k).wait()
                _flush([bufs[t].at[:, 0:plan.spans[t][1], :] for t in range(plan.n)],
                       [out_refs[t].at[:, 0:plan.spans[t][1], :] for t in range(plan.n)], flush_sems)

        return Bound


def _flush(bufs, hbm_outs, sems):
    copies = [pltpu.make_async_copy(b, o, sems.at[k]) for k, (b, o) in enumerate(zip(bufs, hbm_outs))]
    for cp in copies:
        cp.start()
    for cp in copies:
        cp.wait()


def _gather_weights(shards):
    n = len(shards)

    def body(*refs):
        gather = _Gather(refs[:n], refs[n:2 * n], refs[2 * n], refs[2 * n + 1])
        gather.start()
        gather.finish()

    return pl.pallas_call(
        body, name="gather_weights",
        in_specs=[_vmem()] * n, out_specs=[_vmem()] * n,
        out_shape=_Gather.out_shapes(shards), scratch_shapes=_Gather.sems(n),
        compiler_params=pltpu.CompilerParams(vmem_limit_bytes=32 * MIB),
    )(*shards)


class _ChipExchange:
    def __init__(self, wire_ref, land_ref, send_sems, recv_sems):
        self.wire, self.land, self.send_sems, self.recv_sems = wire_ref, land_ref, send_sems, recv_sems

    def _copy(self, d):
        x, y, c, chips = _place()
        return pltpu.make_async_remote_copy(
            src_ref=self.wire.at[2 * chips[d][0] + chips[d][1]], dst_ref=self.land.at[d],
            send_sem=self.send_sems.at[d], recv_sem=self.recv_sems.at[d],
            device_id=(chips[d][0], chips[d][1], c), device_id_type=MESH)

    def start(self):
        for d in range(3):
            self._copy(d).start()

    def wait_recv(self):
        for d in range(3):
            self._copy(d).wait_recv()

    def wait_send(self):
        for d in range(3):
            self._copy(d).wait_send()

    def finish_to(self, hbm_out, flush_sem):
        self.wait_recv()
        _flush([self.land], [hbm_out], flush_sem)
        self.wait_send()

    @staticmethod
    def land_shape(wire):
        return _hbm_shape((3,) + wire.shape[1:], wire.dtype)

    @staticmethod
    def sems():
        return [pltpu.SemaphoreType.DMA((3,)), pltpu.SemaphoreType.DMA((3,))]

    @staticmethod
    def scratch(wire):
        return ([pltpu.VMEM((3,) + wire.shape[1:], wire.dtype)] + _ChipExchange.sems() + [pltpu.SemaphoreType.DMA((1,))])


def _pair_scratch(half_shape):
    return [pltpu.VMEM(half_shape, F32), pltpu.SemaphoreType.DMA((N_CHIP,)), pltpu.SemaphoreType.DMA((N_CHIP,))]


def _pair_reduce(acc_ref, land_ref, wire_ref, send_sems, recv_sems):
    rh = land_ref.shape[1]
    x, y, c, _ = _place()
    copies = []
    for j in range(N_CHIP):
        give = acc_ref.at[pl.ds(pl.multiple_of(j * 2 * rh + (1 - c) * rh, 8), rh), :]
        cp = pltpu.make_async_remote_copy(src_ref=give, dst_ref=land_ref.at[j], send_sem=send_sems.at[j],
                                          recv_sem=recv_sems.at[j], device_id=(x, y, 1 - c), device_id_type=MESH)
        cp.start()
        copies.append(cp)
    for j in range(N_CHIP):
        copies[j].wait_recv()

        def chunk(r, carry, j=j):
            theirs = pl.ds(pl.multiple_of(r * ROW_CHUNK, ROW_CHUNK), ROW_CHUNK)
            mine = pl.ds(pl.multiple_of(j * 2 * rh + c * rh + r * ROW_CHUNK, 8), ROW_CHUNK)
            s = acc_ref[mine, :] + land_ref[j, theirs, :]
            acc_ref[mine, :] = s
            wire_ref[j, theirs, :] = s.astype(_WIRE)
            return carry

        lax.fori_loop(0, rh // ROW_CHUNK, chunk, 0)
    for cp in copies:
        cp.wait_send()


def _grad_finish(last_acc, lands, accs):
    n = len(accs) + 1
    halves = [last_acc.shape[0] // (2 * N_CHIP)] + [w.shape[1] for w in lands]
    widths = [last_acc.shape[1]] + [a.shape[1] for a in accs]

    def body(*refs):
        acc0, land, acc, g = refs[0], (None,) + refs[1:n], (None,) + refs[n:2 * n - 1], refs[2 * n - 1:3 * n - 1]
        pland0, wire0, land0 = refs[3 * n - 1:3 * n + 2]
        own = refs[3 * n + 2:4 * n + 2]
        p_send, p_recv, x_send, x_recv, pair_send, pair_recv, local_sems = refs[4 * n + 2:4 * n + 9]
        land = (land0,) + land[1:]
        x, y, c, chips = _place()
        me = 2 * x + y
        exchange = _ChipExchange(wire0, land0, x_send, x_recv)

        def half_rows(t, half):
            return pl.ds(pl.multiple_of(half * halves[t], 8), halves[t])

        def own_copy(t):
            rows = pl.ds(pl.multiple_of((2 * me + c) * halves[t], 8), halves[t])
            return pltpu.make_async_copy(acc[t].at[rows, :], own[t], local_sems.at[t])

        def pair_copy(t, half):
            rows = g[t].at[half_rows(t, half), :]
            return pltpu.make_async_remote_copy(src_ref=rows, dst_ref=rows, send_sem=pair_send.at[t],
                                                recv_sem=pair_recv.at[t], device_id=(x, y, 1 - c), device_id_type=MESH)

        for t in range(1, n):
            own_copy(t).start()

        rh = halves[0]
        gives = []
        for j in range(N_CHIP):
            rows = acc0.at[pl.ds(pl.multiple_of((2 * j + 1 - c) * rh, 8), rh), :]
            cp = pltpu.make_async_remote_copy(src_ref=rows, dst_ref=pland0.at[j], send_sem=p_send.at[j],
                                              recv_sem=p_recv.at[j], device_id=(x, y, 1 - c), device_id_type=MESH)
            cp.start()
            gives.append(cp)
        for cp in gives:
            cp.wait()

        def chip_sum(r, carry):
            src = pl.ds(pl.multiple_of(r * ROW_CHUNK, ROW_CHUNK), ROW_CHUNK)
            for j in range(N_CHIP):
                mine = pl.ds(pl.multiple_of((2 * j + c) * rh + r * ROW_CHUNK, 8), ROW_CHUNK)
                wire0[j, src, :] = (acc0[mine, :] + pland0[j, src, :]).astype(_WIRE)
            mine = pl.ds(pl.multiple_of((2 * me + c) * rh + r * ROW_CHUNK, 8), ROW_CHUNK)
            own[0][src, :] = acc0[mine, :] + pland0[me, src, :]
            return carry

        lax.fori_loop(0, rh // ROW_CHUNK, chip_sum, 0)
        exchange.start()

        for t in list(range(1, n)) + [0]:
            if t == 0:
                exchange.wait_recv()
            else:
                own_copy(t).wait()

            def chunk(r, carry, t=t):
                src = pl.ds(pl.multiple_of(r * ROW_CHUNK, ROW_CHUNK), ROW_CHUNK)
                dst = pl.ds(pl.multiple_of(c * halves[t] + r * ROW_CHUNK, 8), ROW_CHUNK)
                s = own[t][src, :]
                for d in range(3):
                    s = s + land[t][d, src, :].astype(F32)
                g[t][dst, :] = s
                return carry

            lax.fori_loop(0, halves[t] // ROW_CHUNK, chunk, 0)
            pair_copy(t, c).start()
        for t in range(n):
            pair_copy(t, 1 - c).wait_recv()
        for t in range(n):
            pair_copy(t, c).wait_send()
        exchange.wait_send()

    half0 = (halves[0], widths[0])
    return pl.pallas_call(
        body, name="grad_finish",
        in_specs=[_vmem()] * n + [_hbm()] * (n - 1), out_specs=[_vmem()] * n,
        out_shape=[jax.ShapeDtypeStruct((2 * h, w), F32) for h, w in zip(halves, widths)],
        scratch_shapes=[pltpu.VMEM((N_CHIP,) + half0, F32), pltpu.VMEM((N_CHIP,) + half0, _WIRE),
                        pltpu.VMEM((3,) + half0, _WIRE)]
        + [pltpu.VMEM((h, w), F32) for h, w in zip(halves, widths)]
        + [pltpu.SemaphoreType.DMA((N_CHIP,)), pltpu.SemaphoreType.DMA((N_CHIP,))]
        + _ChipExchange.sems()
        + [pltpu.SemaphoreType.DMA((n,)), pltpu.SemaphoreType.DMA((n,)), pltpu.SemaphoreType.DMA((n,))],
        compiler_params=pltpu.CompilerParams(vmem_limit_bytes=56 * MIB),
    )(last_acc, *lands, *accs)


_SMALL = ("ln_in_g", "ln_in_b", "b_in", "attn_sinks", "sgu_ln_g", "sgu_ln_b", "sgu_w", "sgu_b", "b_out",
          "ln_mix_g", "ln_mix_b", "ln_ffn_g", "ln_ffn_b")
_VEC_ROW = dict(ln_in_g=0, ln_in_b=1, b_in=2, attn_sinks=4, sgu_ln_g=5, sgu_ln_b=6, b_out=7, ln_mix_g=8, ln_mix_b=9,
                ln_ffn_g=10, ln_ffn_b=11)
_LOSS_ROW = 12
_VEC_ROWS = 16
_MAT_ROWS = N_GRP * BLK + BLK


def _small_allreduce(local):
    n_in = 16

    def body(*refs):
        (g_ln_in_g, g_ln_in_b, g_bq, g_bkv, g_bsuv, g_sink, g_sln_g, g_sln_b, g_sw, g_sbt, g_bout,
         g_lmg, g_lmb, g_lfg, g_lfb, g_loss) = refs[:n_in]
        out_a, out_b = refs[n_in:n_in + 2]
        (buf_a, buf_b, pair_a, pair_b, stage_a, stage_b, tot_a, tot_b,
         p1_send, p1_recv, x_send, x_recv, p2_send, p2_recv) = refs[n_in + 2:]
        x, y, c, chips = _place()
        me = 2 * x + y
        sibling = (x, y, 1 - c)
        half_a, half_b = _VEC_ROWS // 2, _MAT_ROWS // 2

        buf_a[...] = jnp.zeros_like(buf_a)
        for row, ref in ((0, g_ln_in_g), (1, g_ln_in_b), (7, g_bout), (8, g_lmg), (9, g_lmb), (10, g_lfg), (11, g_lfb),
                         (_LOSS_ROW, g_loss)):
            buf_a[row:row + 1, :] = ref[...]
        buf_a[2:3, 0:ATTN_W] = g_bq[...]
        buf_a[2:3, ATTN_W:ATTN_W + 2 * KV_W] = g_bkv[...]
        buf_a[2:3, ATTN_W + 2 * KV_W:D_MODEL] = g_bsuv[:, 0:2 * KV_W]
        buf_a[3:4, 0:2 * SGU_W - 2 * KV_W] = g_bsuv[:, 2 * KV_W:2 * SGU_W]
        buf_a[4:5, 0:128] = g_sink[...]
        buf_a[5:6, 0:SGU_W] = g_sln_g[...]
        buf_a[6:7, 0:SGU_W] = g_sln_b[...]
        for h in range(N_GRP):
            buf_b[h * BLK:(h + 1) * BLK, :] = g_sw[h]
        buf_b[N_GRP * BLK:_MAT_ROWS, :] = g_sbt[...]

        def remote(src, dst, send_sem, recv_sem, to):
            return pltpu.make_async_remote_copy(src_ref=src, dst_ref=dst, send_sem=send_sem, recv_sem=recv_sem,
                                                device_id=to, device_id_type=MESH)

        first = [remote(buf_a, pair_a, p1_send.at[0], p1_recv.at[0], sibling),
                 remote(buf_b, pair_b, p1_send.at[1], p1_recv.at[1], sibling)]
        for cp in first:
            cp.start()
        for cp in first:
            cp.wait()
        rows_a = pl.ds(pl.multiple_of(c * half_a, 8), half_a)
        rows_b = pl.ds(pl.multiple_of(c * half_b, 8), half_b)
        stage_a[me] = buf_a[rows_a, :] + pair_a[rows_a, :]
        stage_b[me] = buf_b[rows_b, :] + pair_b[rows_b, :]

        def chip_copies(d):
            to = (chips[d][0], chips[d][1], c)
            return [remote(stage_a.at[me], stage_a.at[me], x_send.at[2 * d], x_recv.at[2 * d], to),
                    remote(stage_b.at[me], stage_b.at[me], x_send.at[2 * d + 1], x_recv.at[2 * d + 1], to)]

        def chip_arrivals(d):
            slot = 2 * chips[d][0] + chips[d][1]
            to = (chips[d][0], chips[d][1], c)
            return [remote(stage_a.at[slot], stage_a.at[slot], x_send.at[2 * d], x_recv.at[2 * d], to),
                    remote(stage_b.at[slot], stage_b.at[slot], x_send.at[2 * d + 1], x_recv.at[2 * d + 1], to)]

        for d in range(3):
            for cp in chip_copies(d):
                cp.start()
        for d in range(3):
            for cp in chip_arrivals(d):
                cp.wait_recv()
        tot_a[rows_a, :] = ((stage_a[0] + stage_a[1]) + stage_a[2]) + stage_a[3]
        tot_b[rows_b, :] = ((stage_b[0] + stage_b[1]) + stage_b[2]) + stage_b[3]

        second = [remote(tot_a.at[rows_a, :], tot_a.at[rows_a, :], p2_send.at[0], p2_recv.at[0], sibling),
                  remote(tot_b.at[rows_b, :], tot_b.at[rows_b, :], p2_send.at[1], p2_recv.at[1], sibling)]
        for cp in second:
            cp.start()
        other_a = pl.ds(pl.multiple_of((1 - c) * half_a, 8), half_a)
        other_b = pl.ds(pl.multiple_of((1 - c) * half_b, 8), half_b)
        remote(tot_a.at[other_a, :], tot_a.at[other_a, :], p2_send.at[0], p2_recv.at[0], sibling).wait_recv()
        remote(tot_b.at[other_b, :], tot_b.at[other_b, :], p2_send.at[1], p2_recv.at[1], sibling).wait_recv()
        for cp in second:
            cp.wait_send()
        for d in range(3):
            for cp in chip_copies(d):
                cp.wait_send()
        out_a[...] = tot_a[...]
        out_b[...] = tot_b[...]

    ins = [local[k] for k in ("ln_in_g", "ln_in_b", "bq", "bkv", "bsuv", "sink", "sgu_ln_g", "sgu_ln_b", "sgu_w",
                              "sgu_bt", "b_out", "ln_mix_g", "ln_mix_b", "ln_ffn_g", "ln_ffn_b", "loss")]
    out_dims = [(_VEC_ROWS, D_MODEL), (_MAT_ROWS, 128)]
    vec = pltpu.VMEM((_VEC_ROWS, D_MODEL), F32)
    mat = pltpu.VMEM((_MAT_ROWS, 128), F32)
    return pl.pallas_call(
        body, name="small_allreduce", grid=(1,),
        in_specs=[_const2(a.shape) for a in ins], out_specs=[_const2(s) for s in out_dims],
        out_shape=[_hbm_shape(s, F32) for s in out_dims],
        scratch_shapes=[vec, mat, vec, mat, pltpu.VMEM((N_CHIP, _VEC_ROWS // 2, D_MODEL), F32),
                        pltpu.VMEM((N_CHIP, _MAT_ROWS // 2, 128), F32), vec, mat,
                        pltpu.SemaphoreType.DMA((2,)), pltpu.SemaphoreType.DMA((2,)), pltpu.SemaphoreType.DMA((6,)),
                        pltpu.SemaphoreType.DMA((6,)), pltpu.SemaphoreType.DMA((2,)), pltpu.SemaphoreType.DMA((2,))],
        compiler_params=pltpu.CompilerParams(vmem_limit_bytes=32 * MIB),
    )(*ins)


def _small_adamw(tot_a, tot_b, params):
    shapes = [params[nm][0].shape for nm in _SMALL]

    def body(*refs):
        ta, tb = refs[:2]
        prm = refs[2:2 + 3 * len(_SMALL)]
        outs = refs[2 + 3 * len(_SMALL):]

        def grad_of(k, name):
            if name == "sgu_w":
                return [tb[h * BLK:(h + 1) * BLK, :] for h in range(N_GRP)]
            if name == "sgu_b":
                return jnp.transpose(tb[N_GRP * BLK:_MAT_ROWS, :])[0:N_GRP, :]
            row = _VEC_ROW[name]
            if name == "b_in":
                return jnp.concatenate([ta[row:row + 1, :], ta[row + 1:row + 2, 0:IN_W - D_MODEL]], axis=1)
            return ta[row:row + 1, 0:shapes[k][-1]]

        for k, name in enumerate(_SMALL):
            w_ref, m_ref, v_ref = prm[3 * k:3 * k + 3]
            g_out, d_out, m_out, v_out = outs[4 * k:4 * k + 4]
            g = grad_of(k, name)
            if name == "sgu_w":
                for h in range(N_GRP):
                    d_, m_, v_ = _adamw_math(w_ref[h], g[h], m_ref[h], v_ref[h])
                    g_out[h], d_out[h], m_out[h], v_out[h] = g[h], d_, m_, v_
            else:
                d_, m_, v_ = _adamw_math(w_ref[...], g, m_ref[...], v_ref[...])
                g_out[...], d_out[...], m_out[...], v_out[...] = g, d_, m_, v_
        outs[-1][...] = ta[_LOSS_ROW:_LOSS_ROW + 1, :]

    ins = [tot_a, tot_b] + [_in_hbm(a) for nm in _SMALL for a in params[nm]]
    out_dims = [s for s in shapes for _ in range(4)] + [(1, D_MODEL)]
    res = pl.pallas_call(
        body, name="small_adamw", grid=(1,),
        in_specs=[_const2(a.shape) for a in ins], out_specs=[_const2(s) for s in out_dims],
        out_shape=[_hbm_shape(s, F32) for s in out_dims],
        compiler_params=_params(32),
    )(*ins)
    return {nm: tuple(res[4 * k:4 * k + 4]) for k, nm in enumerate(_SMALL)}, res[-1]


def _elementwise(name, fn, ins, out_dtypes, tile_rows=256):
    shape = ins[0].shape
    lead = shape[:-2]
    rows, cols = shape[-2:]
    tr = _tile(rows, tile_rows)
    n_lead = math.prod(lead)
    nr = rows // tr
    flat = [_in_hbm(a.reshape((n_lead, rows, cols))) for a in ins]

    def body(*refs):
        outs = fn(*[r[0] for r in refs[:len(ins)]])
        for o_ref, o in zip(refs[len(ins):], outs):
            o_ref[0] = o.astype(o_ref.dtype)

    spec = pl.BlockSpec((1, tr, cols), lambda i: (i // nr, i % nr, 0))
    res = pl.pallas_call(
        body, name=name, grid=(n_lead * nr,),
        in_specs=[spec] * len(ins), out_specs=[spec] * len(out_dtypes),
        out_shape=[_hbm_shape((n_lead, rows, cols), dt) for dt in out_dtypes],
        compiler_params=_params(32),
    )(*flat)
    return [r.reshape(shape) for r in res]


def _adamw_math(w, g, m, v):
    m = ADAM_B1 * m + (1.0 - ADAM_B1) * g
    v = ADAM_B2 * v + (1.0 - ADAM_B2) * (g * g)
    m_hat = m / (1.0 - ADAM_B1 ** ADAM_STEP)
    v_hat = v / (1.0 - ADAM_B2 ** ADAM_STEP)
    delta = -ADAM_LR * (m_hat / (jnp.sqrt(v_hat) + ADAM_EPS) + ADAM_WD * w)
    return delta, m, v


def _adamw(name, w, g, m, v, tile_rows=256):
    return _elementwise(name, lambda w_, g_, m_, v_: (g_,) + _adamw_math(w_, g_, m_, v_), [w, g, m, v],
                        [F32, F32, F32, F32], tile_rows)


def kernel(x, positions, ln_in_g, ln_in_b, w_in, b_in, attn_sinks, sgu_ln_g, sgu_ln_b, sgu_w, sgu_b, w_out, b_out, ln_mix_g, ln_mix_b, w_gate, w_up, w_down, ln_ffn_g, ln_ffn_b, loss_target, m_ln_in_g, m_ln_in_b, m_w_in, m_b_in, m_attn_sinks, m_sgu_ln_g, m_sgu_ln_b, m_sgu_w, m_sgu_b, m_w_out, m_b_out, m_ln_mix_g, m_ln_mix_b, m_w_gate, m_w_up, m_w_down, m_ln_ffn_g, m_ln_ffn_b, v_ln_in_g, v_ln_in_b, v_w_in, v_b_in, v_attn_sinks, v_sgu_ln_g, v_sgu_ln_b, v_sgu_w, v_sgu_b, v_w_out, v_b_out, v_ln_mix_g, v_ln_mix_b, v_w_gate, v_w_up, v_w_down, v_ln_ffn_g, v_ln_ffn_b):
    weights = dict(ln_in_g=ln_in_g, ln_in_b=ln_in_b, w_in=w_in, b_in=b_in, attn_sinks=attn_sinks, sgu_ln_g=sgu_ln_g,
                   sgu_ln_b=sgu_ln_b, sgu_w=sgu_w, sgu_b=sgu_b, w_out=w_out, b_out=b_out, ln_mix_g=ln_mix_g,
                   ln_mix_b=ln_mix_b, w_gate=w_gate, w_up=w_up, w_down=w_down, ln_ffn_g=ln_ffn_g, ln_ffn_b=ln_ffn_b)
    mom_m = dict(ln_in_g=m_ln_in_g, ln_in_b=m_ln_in_b, w_in=m_w_in, b_in=m_b_in, attn_sinks=m_attn_sinks,
                 sgu_ln_g=m_sgu_ln_g, sgu_ln_b=m_sgu_ln_b, sgu_w=m_sgu_w, sgu_b=m_sgu_b, w_out=m_w_out, b_out=m_b_out,
                 ln_mix_g=m_ln_mix_g, ln_mix_b=m_ln_mix_b, w_gate=m_w_gate, w_up=m_w_up, w_down=m_w_down,
                 ln_ffn_g=m_ln_ffn_g, ln_ffn_b=m_ln_ffn_b)
    mom_v = dict(ln_in_g=v_ln_in_g, ln_in_b=v_ln_in_b, w_in=v_w_in, b_in=v_b_in, attn_sinks=v_attn_sinks,
                 sgu_ln_g=v_sgu_ln_g, sgu_ln_b=v_sgu_ln_b, sgu_w=v_sgu_w, sgu_b=v_sgu_b, w_out=v_w_out, b_out=v_b_out,
                 ln_mix_g=v_ln_mix_g, ln_mix_b=v_ln_mix_b, w_gate=v_w_gate, w_up=v_w_up, w_down=v_w_down,
                 ln_ffn_g=v_ln_ffn_g, ln_ffn_b=v_ln_ffn_b)
    order = list(weights)
    big = ("w_in", "w_out", "w_gate", "w_up", "w_down")

    s_len = x.shape[1]
    xs = _in_hbm(x.reshape(s_len, D_MODEL))
    tgt = _in_hbm(loss_target.reshape(s_len, D_MODEL))
    pos_col = _in_hbm(positions.reshape(s_len, 1))
    g0, b0 = _in_hbm(ln_in_g.reshape(1, D_MODEL)), _in_hbm(ln_in_b.reshape(1, D_MODEL))
    sinks = attn_sinks.reshape(N_Q)
    sgu_w3 = _in_hbm(sgu_w.reshape(N_GRP, BLK, BLK))
    sgu_bt = _in_hbm(sgu_b.reshape(N_GRP, BLK).T)
    b_in, b_out, sgu_ln_g, sgu_ln_b, ln_mix_g, ln_mix_b, ln_ffn_g, ln_ffn_b = (
        _in_hbm(a) for a in (b_in, b_out, sgu_ln_g, sgu_ln_b, ln_mix_g, ln_mix_b, ln_ffn_g, ln_ffn_b))

    col_sharded = ("w_in", "w_gate", "w_up")

    def rowmajor(name, a):
        return jnp.swapaxes(a[0], 0, 1) if name in col_sharded else a[0]

    def as_given(name, a):
        return (jnp.swapaxes(a, 0, 1) if name in col_sharded else a)[None]

    shards = [rowmajor(n, weights[n]) for n in big]
    (gw_in,) = _gather_weights(shards[0:1])
    w_in_full = gw_in.reshape(IN_W, D_MODEL)

    sh_out, sh_gate, sh_up, sh_down = shards[1:]
    cut = GATHER_CUT
    *acts, gw_out, gw_gate0 = _ln_inproj(xs, pos_col, g0, b0, w_in_full, b_in, _GatherPlan(
        [(sh_out, (0, OUT_SH), None), (sh_gate, (0, cut), None)]))
    q, k, v, su, sv, tc, t1, t2 = (_in_hbm(a) for a in acts)
    mc, gw_gate, gw_up0 = _mixer_fwd(q, k, v, su, sv, sinks, sgu_ln_g, sgu_ln_b, sgu_w3, sgu_bt, _GatherPlan(
        [(sh_gate, (cut, FF_SH), gw_gate0), (sh_up, (0, cut), None)]))
    mc = _in_hbm(mc)
    w_out_full = gw_out.reshape(D_MODEL, D_MODEL)
    r1, gw_up = _outproj(mc, w_out_full, b_out, xs, g0, b0, _GatherPlan([(sh_up, (cut, FF_SH), gw_up0)]))
    r1 = _in_hbm(r1)
    act, p_act, q_act, gw_down = _ffn_up(r1, ln_mix_g, ln_mix_b, gw_gate, gw_up,
                                         _GatherPlan([(sh_down, (0, FF_SH), None)]))
    act, p_act, q_act = _in_hbm(act), _in_hbm(p_act), _in_hbm(q_act)
    dr2, loss_cols, d_ln_ffn_g, d_ln_ffn_b = _ffn_down_loss(act, gw_down, r1, ln_mix_g, ln_mix_b, ln_ffn_g, ln_ffn_b, tgt)
    dr2 = _in_hbm(dr2)

    dg, du, acc_down, wire_down = _ffn_bwd_a(dr2, act, p_act, q_act, gw_down)
    dh1a, acc_gate, wire_gate, land_down = _ffn_bwd_g(dr2, _in_hbm(dg), r1, ln_mix_g, ln_mix_b, gw_gate, wire_down)
    dr1, acc_up, wire_up, d_ln_mix_g, d_ln_mix_b, land_gate = _ffn_bwd_u(_in_hbm(dh1a), _in_hbm(du), r1, ln_mix_g,
                                                                         ln_mix_b, gw_up, wire_gate)
    dr1 = _in_hbm(dr1)
    dmc, acc_out, wire_out, d_b_out, land_up = _outproj_bwd(dr1, mc, w_out_full, wire_up)
    (dq, dkv, dsuv, dbq, dbkv, dbsuv, d_sink, d_sgu_ln_g, d_sgu_ln_b, d_sgu_w, d_sgu_bt, land_out) = _mixer_bwd(
        q, k, v, su, sv, _in_hbm(dmc), tc, t1, t2, sinks, sgu_ln_g, sgu_ln_b, sgu_w3, sgu_bt, wire_out)
    grad_x, acc_in, d_ln_in_g, d_ln_in_b = _inproj_bwd(_in_hbm(dq), _in_hbm(dkv), _in_hbm(dsuv), dr1, xs, g0, b0,
                                                       w_in_full)

    reduced = _grad_finish(acc_in, [land_out, land_gate, land_up, land_down], [acc_out, acc_gate, acc_up, acc_down])
    small_shape = dict(ln_in_g=(1, D_MODEL), ln_in_b=(1, D_MODEL), sgu_w=(N_GRP, BLK, BLK), sgu_b=(N_GRP, BLK))
    small_local = dict(
        ln_in_g=d_ln_in_g, ln_in_b=d_ln_in_b, bq=dbq, bkv=dbkv, bsuv=dbsuv, sink=d_sink, sgu_ln_g=d_sgu_ln_g,
        sgu_ln_b=d_sgu_ln_b, sgu_w=d_sgu_w, sgu_bt=d_sgu_bt, b_out=d_b_out, ln_mix_g=d_ln_mix_g, ln_mix_b=d_ln_mix_b,
        ln_ffn_g=d_ln_ffn_g, ln_ffn_b=d_ln_ffn_b, loss=loss_cols)
    small_params = {nm: tuple(src[nm].reshape(small_shape.get(nm, src[nm].shape)) for src in (weights, mom_m, mom_v))
                    for nm in _SMALL}
    tot_a, tot_b = _small_allreduce({nm: _in_hbm(a) for nm, a in small_local.items()})
    small_out, loss_sum = _small_adamw(_in_hbm(tot_a), _in_hbm(tot_b), small_params)
    loss = jnp.sum(loss_sum) * (0.5 / D_MODEL)
    grads, delta, new_m, new_v = {}, {}, {}, {}
    for nm in _SMALL:
        grads[nm], delta[nm], new_m[nm], new_v[nm] = (a.reshape(weights[nm].shape) for a in small_out[nm])

    for t, name in enumerate(big):
        g_, d_, m_, v_ = _adamw("adamw_" + name, shards[t], reduced[t], rowmajor(name, mom_m[name]),
                                rowmajor(name, mom_v[name]))
        grads[name], delta[name], new_m[name], new_v[name] = (as_given(name, a) for a in (g_, d_, m_, v_))

    return (loss, grad_x.reshape(x.shape), *[grads[n] for n in order], *[delta[n] for n in order],
            *[new_m[n] for n in order], *[new_v[n] for n in order])
```

```python
import functools
import math

import jax
import jax.numpy as jnp
from jax import lax
from jax.experimental import pallas as pl
from jax.experimental.pallas import tpu as pltpu

F32 = jnp.float32
_MXU = jnp.bfloat16
_WIRE = jnp.bfloat16
_ACT = jnp.bfloat16

D_MODEL = 1024
ATTN_W = 512
SGU_W = 512
HEAD_DIM = 64
N_Q = 8
N_KV = 2
Q_PER_KV = 4
KV_W = 128
BLK = 128
ROT_DIM = 16
ROPE_THETA = 500000.0
N_GRP = 4
GRP_DIM = 128
D_FF = 2816
IN_W = 1792
LN_EPS = 1e-5
ALPHA = 2.0 ** 0.25
N_CHIP = 4
FF_SH = D_FF // N_CHIP
IN_SH = IN_W // N_CHIP
OUT_SH = D_MODEL // N_CHIP
ROW_CHUNK = 32
GATHER_CUT = 224

ADAM_LR = 0.001
ADAM_B1 = 0.9
ADAM_B2 = 0.999
ADAM_EPS = 1e-08
ADAM_WD = 0.01
ADAM_STEP = 10

SQRT_HALF = 0.7071067811865476
INV_SQRT_2PI = 0.3989422804014327
MESH_AXES = ("x", "y", "c")
MESH = pl.DeviceIdType.MESH
MIB = 2 ** 20


def _vmem():
    return pl.BlockSpec(memory_space=pltpu.VMEM)


def _smem():
    return pl.BlockSpec(memory_space=pltpu.SMEM)


def _hbm():
    return pl.BlockSpec(memory_space=pl.ANY)


def _hbm_shape(shape, dtype):
    return pltpu.HBM(shape, dtype)


def _in_hbm(a):
    return pltpu.with_memory_space_constraint(a, pltpu.HBM)


def _params(vmem_mib=48):
    return pltpu.CompilerParams(dimension_semantics=("arbitrary",), vmem_limit_bytes=vmem_mib * MIB)


def _tile(n, cap):
    if n <= cap:
        return n
    for t in range(cap - cap % 16, 0, -16):
        if n % t == 0:
            return t
    raise ValueError((n, cap))


def _rows(tm, width):
    return pl.BlockSpec((tm, width), lambda i: (i, 0))


def _const2(shape):
    return pl.BlockSpec(shape, lambda i: (0,) * len(shape))


def _ln(x, g, b):
    mu = jnp.mean(x, axis=-1, keepdims=True)
    xc = x - mu
    var = jnp.mean(xc * xc, axis=-1, keepdims=True)
    rstd = lax.rsqrt(var + LN_EPS)
    xhat = xc * rstd
    return xhat * g + b, xhat, rstd


def _ln_bwd(dy, xhat, rstd, g):
    gdy = dy * g
    m1 = jnp.mean(gdy, axis=-1, keepdims=True)
    m2 = jnp.mean(gdy * xhat, axis=-1, keepdims=True)
    return rstd * (gdy - m1 - xhat * m2)


def _colsum(a):
    return jnp.sum(a, axis=0, keepdims=True)


def _gelu_and_grad(x):
    cdf = 0.5 * (1.0 + lax.erf(x * SQRT_HALF))
    return x * cdf, cdf + x * jnp.exp(-0.5 * x * x) * INV_SQRT_2PI


def _dot(a, b):
    return jnp.dot(a, b, preferred_element_type=F32)


def _dot_nt(a, b):
    return lax.dot_general(a, b, (((1,), (1,)), ((), ())), preferred_element_type=F32)


def _dot_tn(a, b):
    return lax.dot_general(a, b, (((0,), (0,)), ((), ())), preferred_element_type=F32)


def _rope(t, tc, t1, t2):
    n = t.shape[1]
    rep = n // 128
    if rep > 1:
        tc, t1, t2 = (jnp.tile(a, (1, rep)) for a in (tc, t1, t2))
    return t * tc + pltpu.roll(t, n - 8, 1) * t1 + pltpu.roll(t, 8, 1) * t2


def _rope_bwd(d, tc, t1, t2):
    n = d.shape[1]
    rep = n // 128
    if rep > 1:
        tc, t1, t2 = (jnp.tile(a, (1, rep)) for a in (tc, t1, t2))
    return d * tc + pltpu.roll(d * t1, 8, 1) + pltpu.roll(d * t2, n - 8, 1)


def _causal_w(w_ref, h):
    t = lax.broadcasted_iota(jnp.int32, (BLK, BLK), 0)
    s = lax.broadcasted_iota(jnp.int32, (BLK, BLK), 1)
    return jnp.where(s <= t, w_ref[h], 0.0)


def _lane_put(vals, width):
    rows = vals[0].shape[0]
    lane = lax.broadcasted_iota(jnp.int32, (rows, width), 1)
    out = jnp.zeros((rows, width), F32)
    for k, v in enumerate(vals):
        out = out + jnp.where(lane == k, v, 0.0)
    return out


def _rope_consts():
    lane = jnp.arange(128) % HEAD_DIM
    rot = lane < ROT_DIM
    pair = (2 * (lane % (ROT_DIM // 2))).astype(F32)
    freq = jnp.where(rot, ROPE_THETA ** (-pair / ROT_DIM), 0.0)
    rows = [freq, rot.astype(F32), 1.0 - rot.astype(F32), (lane < ROT_DIM // 2).astype(F32),
            jnp.logical_and(lane >= ROT_DIM // 2, rot).astype(F32)]
    rows += [jnp.zeros((128,), F32)] * 3
    return jnp.stack(rows).astype(F32)


def _ln_inproj(x, pos_col, g0, b0, w_in, b_in, plan):
    s_len = x.shape[0]
    tm = _tile(s_len, 512)
    m, n = len(plan.operands()), plan.n

    def body(x_ref, pos_ref, g_ref, b_ref, w_ref, bi_ref, rc_ref, *rest):
        q_ref, k_ref, v_ref, su_ref, sv_ref, tc_ref, t1_ref, t2_ref = rest[m:m + 8]
        gather = plan.bind(rest[:m], rest[m + 8:m + 8 + n], rest[m + 8 + n:])
        i = pl.program_id(0)

        @pl.when(i == 0)
        def _():
            gather.start()

        h0, _, _ = _ln(x_ref[...], g_ref[...], b_ref[...])
        proj = _dot_nt(h0.astype(_MXU), w_ref[...]) + bi_ref[...]
        ang = pos_ref[...].astype(F32) * rc_ref[0:1, :]
        cs = jnp.cos(ang)
        sn = jnp.sin(ang)
        tc = cs * rc_ref[1:2, :] + rc_ref[2:3, :]
        t1 = -sn * rc_ref[3:4, :]
        t2 = sn * rc_ref[4:5, :]
        tc_ref[...] = tc
        t1_ref[...] = t1
        t2_ref[...] = t2
        q = _rope(proj[:, 0:ATTN_W], tc, t1, t2) * (HEAD_DIM ** -0.5)
        q_ref[...] = q.astype(_MXU)
        k_ref[...] = _rope(proj[:, ATTN_W:ATTN_W + KV_W], tc, t1, t2).astype(_MXU)
        v_ref[...] = proj[:, ATTN_W + KV_W:ATTN_W + 2 * KV_W].astype(_MXU)
        su_ref[...] = proj[:, ATTN_W + 2 * KV_W:ATTN_W + 2 * KV_W + SGU_W]
        sv_ref[...] = proj[:, ATTN_W + 2 * KV_W + SGU_W:IN_W]

        @pl.when(i == pl.num_programs(0) - 1)
        def _():
            gather.finish()

    sd = _hbm_shape
    return pl.pallas_call(
        body, name="ln_inproj", grid=(s_len // tm,),
        in_specs=[_rows(tm, D_MODEL), _rows(tm, 1), _const2((1, D_MODEL)), _const2((1, D_MODEL)), _vmem(),
                  _const2((1, IN_W)), _const2((8, 128))] + plan.in_specs(),
        out_specs=[_rows(tm, ATTN_W), _rows(tm, KV_W), _rows(tm, KV_W), _rows(tm, SGU_W), _rows(tm, SGU_W),
                   _rows(tm, 128), _rows(tm, 128), _rows(tm, 128)] + plan.out_specs(),
        out_shape=[sd((s_len, ATTN_W), _MXU), sd((s_len, KV_W), _MXU), sd((s_len, KV_W), _MXU),
                   sd((s_len, SGU_W), F32), sd((s_len, SGU_W), F32),
                   sd((s_len, 128), F32), sd((s_len, 128), F32), sd((s_len, 128), F32)] + plan.out_shapes(),
        scratch_shapes=plan.scratch(),
        compiler_params=_params(56),
    )(x, pos_col, g0, b0, w_in, b_in, _rope_consts(), *plan.operands())


def _band_mask_t(first_block):
    kj = lax.broadcasted_iota(jnp.int32, (2 * BLK, BLK), 0)
    qi = lax.broadcasted_iota(jnp.int32, (2 * BLK, BLK), 1)
    shut = jnp.where(first_block, 2 * BLK, 0)
    prev_ok = jnp.logical_and(kj < BLK, kj > qi + shut)
    cur_ok = jnp.logical_and(kj >= BLK, (kj - BLK) <= qi)
    return jnp.logical_or(prev_ok, cur_ok)


def _attn_probs_t(kh, qh, sink, allowed_t):
    s = jnp.where(allowed_t, _dot_nt(kh, qh), -1e30)
    m = jnp.maximum(jnp.max(s, axis=0, keepdims=True), sink)
    p = jnp.exp(s - m)
    ps = jnp.exp(sink - m)
    inv = 1.0 / (jnp.sum(p, axis=0, keepdims=True) + ps)
    return p * inv, ps * inv


def _sgu_fwd(su, sv, lg, lb, w_ref, bt_ref):
    u, du_dsu = _gelu_and_grad(su)
    gv, dgv_dsv = _gelu_and_grad(sv)
    vv, vhat, rstd = _ln(gv, lg, lb)
    vvb = vv.astype(_MXU)
    wcs, mixed = [], []
    for h in range(N_GRP):
        wc = _causal_w(w_ref, h).astype(_MXU)
        wcs.append(wc)
        mixed.append(_dot(wc, vvb[:, h * GRP_DIM:(h + 1) * GRP_DIM]) + bt_ref[:, h:h + 1])
    return u, jnp.concatenate(mixed, axis=1), du_dsu, dgv_dsv, vhat, rstd, vvb, wcs


def _prev_map(i):
    return (jnp.maximum(i - 1, 0), 0)


def _mixer_fwd(q, k, v, su, sv, sinks, sg, sb, sgu_w, sgu_bt, plan):
    s_len = q.shape[0]
    nb = s_len // BLK
    m, n = len(plan.operands()), plan.n

    def body(q_ref, kc_ref, kp_ref, vc_ref, vp_ref, su_ref, sv_ref, sink_ref, lg_ref, lb_ref, w_ref, bt_ref, *rest):
        mc_ref = rest[m]
        gather = plan.bind(rest[:m], rest[m + 1:m + 1 + n], rest[m + 1 + n:])
        i = pl.program_id(0)

        @pl.when(i == 0)
        def _():
            gather.start()

        @pl.when(i == nb - 1)
        def _():
            gather.finish()

        allowed_t = _band_mask_t(i == 0)
        kb = jnp.concatenate([kp_ref[...], kc_ref[...]], axis=0)
        vb = jnp.concatenate([vp_ref[...], vc_ref[...]], axis=0)
        qv = q_ref[...]
        outs = []
        allowed_g = jnp.tile(allowed_t, (1, Q_PER_KV))
        for g in range(N_KV):
            heads = range(g * Q_PER_KV, (g + 1) * Q_PER_KV)
            kh = kb[:, g * HEAD_DIM:(g + 1) * HEAD_DIM]
            vh = vb[:, g * HEAD_DIM:(g + 1) * HEAD_DIM]
            q_g = jnp.concatenate([qv[:, h * HEAD_DIM:(h + 1) * HEAD_DIM] for h in heads], axis=0)
            sink_g = jnp.concatenate([jnp.full((1, BLK), sink_ref[h], F32) for h in heads], axis=1)
            probs_t, _ = _attn_probs_t(kh, q_g, sink_g, allowed_g)
            o_g = _dot_tn(probs_t.astype(_MXU), vh)
            outs += [o_g[hh * BLK:(hh + 1) * BLK, :] for hh in range(Q_PER_KV)]
        u, mixed = _sgu_fwd(su_ref[...], sv_ref[...], lg_ref[...], lb_ref[...], w_ref, bt_ref)[:2]
        mc_ref[...] = jnp.concatenate(outs + [u * mixed], axis=1).astype(_MXU)

    cur = lambda w: pl.BlockSpec((BLK, w), lambda i: (i, 0))
    prev = lambda w: pl.BlockSpec((BLK, w), _prev_map)
    return pl.pallas_call(
        body, name="mixer_fwd", grid=(nb,),
        in_specs=[cur(ATTN_W), cur(KV_W), prev(KV_W), cur(KV_W), prev(KV_W), cur(SGU_W), cur(SGU_W), _smem(),
                  _const2((1, SGU_W)), _const2((1, SGU_W)), _const2((N_GRP, BLK, BLK)), _const2((BLK, N_GRP))]
        + plan.in_specs(),
        out_specs=[cur(D_MODEL)] + plan.out_specs(),
        out_shape=[_hbm_shape((s_len, D_MODEL), _MXU)] + plan.out_shapes(),
        scratch_shapes=plan.scratch(),
        compiler_params=_params(56),
    )(q, k, k, v, v, su, sv, sinks, sg, sb, sgu_w, sgu_bt, *plan.operands())


def _outproj(mc, w_out, b_out, x, g0, b0, plan):
    s_len = x.shape[0]
    tm = _tile(s_len, 512)
    m, n = len(plan.operands()), plan.n

    def body(mc_ref, w_ref, bo_ref, x_ref, g_ref, b_ref, *rest):
        r1_ref = rest[m]
        gather = plan.bind(rest[:m], rest[m + 1:m + 1 + n], rest[m + 1 + n:])
        i = pl.program_id(0)

        @pl.when(i == 0)
        def _():
            gather.start()

        h0, _, _ = _ln(x_ref[...], g_ref[...], b_ref[...])
        r1_ref[...] = ALPHA * h0 + (_dot(mc_ref[...], w_ref[...]) + bo_ref[...])

        @pl.when(i == pl.num_programs(0) - 1)
        def _():
            gather.finish()

    return pl.pallas_call(
        body, name="outproj", grid=(s_len // tm,),
        in_specs=[_rows(tm, D_MODEL), _vmem(), _const2((1, D_MODEL)), _rows(tm, D_MODEL),
                  _const2((1, D_MODEL)), _const2((1, D_MODEL))] + plan.in_specs(),
        out_specs=[_rows(tm, D_MODEL)] + plan.out_specs(),
        out_shape=[_hbm_shape((s_len, D_MODEL), F32)] + plan.out_shapes(),
        scratch_shapes=plan.scratch(),
        compiler_params=_params(40),
    )(mc, w_out, b_out, x, g0, b0, *plan.operands())


def _ffn_spec(tm):
    return pl.BlockSpec((N_CHIP, tm, FF_SH), lambda i: (0, i, 0))


def _ffn_up(r1, g1, b1, wg, wu, plan):
    s_len = r1.shape[0]
    tm = _tile(s_len, 512)
    m, n = len(plan.operands()), plan.n

    def body(r1_ref, g_ref, b_ref, wg_ref, wu_ref, *rest):
        a_ref, p_ref, q_ref = rest[m:m + 3]
        gather = plan.bind(rest[:m], rest[m + 3:m + 3 + n], rest[m + 3 + n:])
        i = pl.program_id(0)

        @pl.when(i == 0)
        def _():
            gather.start()

        h1, _, _ = _ln(r1_ref[...], g_ref[...], b_ref[...])
        h1b = h1.astype(_MXU)
        for j in range(N_CHIP):
            g = _dot_nt(h1b, wg_ref[j])
            u = _dot_nt(h1b, wu_ref[j])
            silu, sg = _silu_parts(g)
            a_ref[j] = (silu * u).astype(_MXU)
            p_ref[j] = silu.astype(_ACT)
            q_ref[j] = (u * (sg * (1.0 + g * (1.0 - sg)))).astype(_ACT)

        @pl.when(i == pl.num_programs(0) - 1)
        def _():
            gather.finish()

    sd = _hbm_shape((N_CHIP, s_len, FF_SH), _ACT)
    return pl.pallas_call(
        body, name="ffn_up", grid=(s_len // tm,),
        in_specs=[_rows(tm, D_MODEL), _const2((1, D_MODEL)), _const2((1, D_MODEL)), _vmem(), _vmem()] + plan.in_specs(),
        out_specs=[_ffn_spec(tm)] * 3 + plan.out_specs(),
        out_shape=[_hbm_shape((N_CHIP, s_len, FF_SH), _MXU), sd, sd] + plan.out_shapes(),
        scratch_shapes=plan.scratch(),
        compiler_params=_params(56),
    )(r1, g1, b1, wg, wu, *plan.operands())


def _silu_parts(g):
    sg = 1.0 / (1.0 + jnp.exp(-g))
    return g * sg, sg


def _ffn_down_loss(act, wd, r1, g1, b1, g2, b2, target):
    s_len = r1.shape[0]
    tm = _tile(s_len, 512)

    parts = 2 if tm % 32 == 0 else 1
    sub = tm // parts

    def body(a_ref, wd_ref, r1_ref, g1_ref, b1_ref, g2_ref, b2_ref, t_ref, dr2_ref, loss_ref, dg2_ref, db2_ref):
        i = pl.program_id(0)

        @pl.when(i == 0)
        def _():
            loss_ref[...] = jnp.zeros_like(loss_ref)
            dg2_ref[...] = jnp.zeros_like(dg2_ref)
            db2_ref[...] = jnp.zeros_like(db2_ref)

        for part in range(parts):
            rows = slice(part * sub, (part + 1) * sub)
            f = jnp.zeros((sub, D_MODEL), F32)
            for j in range(N_CHIP):
                f = f + _dot(a_ref[j, rows, :], wd_ref[j])
            h1, _, _ = _ln(r1_ref[rows, :], g1_ref[...], b1_ref[...])
            h2, r2hat, rstd2 = _ln(ALPHA * h1 + f, g2_ref[...], b2_ref[...])
            diff = h2 - t_ref[rows, :]
            dh2 = diff * (1.0 / D_MODEL)
            loss_ref[...] += _colsum(diff * diff)
            dg2_ref[...] += _colsum(dh2 * r2hat)
            db2_ref[...] += _colsum(dh2)
            dr2_ref[rows, :] = _ln_bwd(dh2, r2hat, rstd2, g2_ref[...])

    vec = _hbm_shape((1, D_MODEL), F32)
    c = _const2((1, D_MODEL))
    return pl.pallas_call(
        body, name="ffn_down_loss", grid=(s_len // tm,),
        in_specs=[_ffn_spec(tm), _vmem(), _rows(tm, D_MODEL), c, c, c, c, _rows(tm, D_MODEL)],
        out_specs=[_rows(tm, D_MODEL), c, c, c],
        out_shape=[_hbm_shape((s_len, D_MODEL), F32), vec, vec, vec],
        compiler_params=_params(48),
    )(act, wd, r1, g1, b1, g2, b2, target)


def _ffn_bwd_a(dr2, act, p_act, q_act, wd):
    s_len = dr2.shape[0]
    tm = _tile(s_len, 512)

    def body(dr2_ref, a_ref, p_ref, q_ref, wd_ref, dg_ref, du_ref, dwd_ref, wire_ref, land_ref, send_sem, recv_sem):
        i = pl.program_id(0)

        @pl.when(i == 0)
        def _():
            dwd_ref[...] = jnp.zeros_like(dwd_ref)

        dfb = dr2_ref[...].astype(_MXU)
        for j in range(N_CHIP):
            da = _dot_nt(dfb, wd_ref[j])
            dg_ref[j] = (da * q_ref[j].astype(F32)).astype(_MXU)
            du_ref[j] = (da * p_ref[j].astype(F32)).astype(_MXU)
            dwd_ref[j * FF_SH:(j + 1) * FF_SH, :] += _dot_tn(a_ref[j], dfb)

        @pl.when(i == pl.num_programs(0) - 1)
        def _():
            _pair_reduce(dwd_ref, land_ref, wire_ref, send_sem, recv_sem)

    sd = _hbm_shape((N_CHIP, s_len, FF_SH), _MXU)
    return pl.pallas_call(
        body, name="ffn_bwd_a", grid=(s_len // tm,),
        in_specs=[_rows(tm, D_MODEL), _ffn_spec(tm), _ffn_spec(tm), _ffn_spec(tm), _vmem()],
        out_specs=[_ffn_spec(tm), _ffn_spec(tm), _vmem(), _vmem()],
        out_shape=[sd, sd, jax.ShapeDtypeStruct((D_FF, D_MODEL), F32),
                   jax.ShapeDtypeStruct((N_CHIP, FF_SH // 2, D_MODEL), _WIRE)],
        scratch_shapes=_pair_scratch((N_CHIP, FF_SH // 2, D_MODEL)),
        compiler_params=_params(61),
    )(dr2, act, p_act, q_act, wd)


def _ffn_bwd_g(dr2, dg, r1, g1, b1, wg, prev_wire):
    s_len = dr2.shape[0]
    tm = _tile(s_len, 512)

    def body(dr2_ref, dg_ref, r1_ref, g1_ref, b1_ref, wg_ref, pw_ref, dh1_ref, dwg_ref, wire_ref, pl_ref,
             land_ref, send_sem, recv_sem, xl_ref, x_send, x_recv, x_flush):
        i = pl.program_id(0)
        exchange = _ChipExchange(pw_ref, xl_ref, x_send, x_recv)

        @pl.when(i == 0)
        def _():
            exchange.start()
            dwg_ref[...] = jnp.zeros_like(dwg_ref)

        h1, _, _ = _ln(r1_ref[...], g1_ref[...], b1_ref[...])
        h1b = h1.astype(_MXU)
        dh1 = ALPHA * dr2_ref[...]
        for j in range(N_CHIP):
            dgj = dg_ref[j]
            dh1 = dh1 + _dot(dgj, wg_ref[j])
            dwg_ref[j * FF_SH:(j + 1) * FF_SH, :] += _dot_tn(dgj, h1b)
        dh1_ref[...] = dh1

        @pl.when(i == pl.num_programs(0) - 1)
        def _():
            _pair_reduce(dwg_ref, land_ref, wire_ref, send_sem, recv_sem)
            exchange.finish_to(pl_ref, x_flush)

    c = _const2((1, D_MODEL))
    return pl.pallas_call(
        body, name="ffn_bwd_g", grid=(s_len // tm,),
        in_specs=[_rows(tm, D_MODEL), _ffn_spec(tm), _rows(tm, D_MODEL), c, c, _vmem(), _vmem()],
        out_specs=[_rows(tm, D_MODEL), _vmem(), _vmem(), _hbm()],
        out_shape=[_hbm_shape((s_len, D_MODEL), F32), jax.ShapeDtypeStruct((D_FF, D_MODEL), F32),
                   jax.ShapeDtypeStruct((N_CHIP, FF_SH // 2, D_MODEL), _WIRE), _ChipExchange.land_shape(prev_wire)],
        scratch_shapes=_pair_scratch((N_CHIP, FF_SH // 2, D_MODEL)) + _ChipExchange.scratch(prev_wire),
        compiler_params=_params(58),
    )(dr2, dg, r1, g1, b1, wg, prev_wire)


def _ffn_bwd_u(dh1a, du, r1, g1, b1, wu, prev_wire):
    s_len = dh1a.shape[0]
    tm = _tile(s_len, 512)

    def body(dh1_ref, du_ref, r1_ref, g1_ref, b1_ref, wu_ref, pw_ref,
             dr1_ref, dwu_ref, wire_ref, dg1_ref, db1_ref, pl_ref,
             land_ref, send_sem, recv_sem, xl_ref, x_send, x_recv, x_flush):
        i = pl.program_id(0)
        exchange = _ChipExchange(pw_ref, xl_ref, x_send, x_recv)

        @pl.when(i == 0)
        def _():
            exchange.start()
            dwu_ref[...] = jnp.zeros_like(dwu_ref)
            dg1_ref[...] = jnp.zeros_like(dg1_ref)
            db1_ref[...] = jnp.zeros_like(db1_ref)

        h1, r1hat, rstd1 = _ln(r1_ref[...], g1_ref[...], b1_ref[...])
        h1b = h1.astype(_MXU)
        dh1 = dh1_ref[...]
        for j in range(N_CHIP):
            duj = du_ref[j]
            dh1 = dh1 + _dot(duj, wu_ref[j])
            dwu_ref[j * FF_SH:(j + 1) * FF_SH, :] += _dot_tn(duj, h1b)
        dg1_ref[...] += _colsum(dh1 * r1hat)
        db1_ref[...] += _colsum(dh1)
        dr1_ref[...] = _ln_bwd(dh1, r1hat, rstd1, g1_ref[...])

        @pl.when(i == pl.num_programs(0) - 1)
        def _():
            _pair_reduce(dwu_ref, land_ref, wire_ref, send_sem, recv_sem)
            exchange.finish_to(pl_ref, x_flush)

    vec = _hbm_shape((1, D_MODEL), F32)
    c = _const2((1, D_MODEL))
    return pl.pallas_call(
        body, name="ffn_bwd_u", grid=(s_len // tm,),
        in_specs=[_rows(tm, D_MODEL), _ffn_spec(tm), _rows(tm, D_MODEL), c, c, _vmem(), _vmem()],
        out_specs=[_rows(tm, D_MODEL), _vmem(), _vmem(), c, c, _hbm()],
        out_shape=[_hbm_shape((s_len, D_MODEL), F32), jax.ShapeDtypeStruct((D_FF, D_MODEL), F32),
                   jax.ShapeDtypeStruct((N_CHIP, FF_SH // 2, D_MODEL), _WIRE), vec, vec,
                   _ChipExchange.land_shape(prev_wire)],
        scratch_shapes=_pair_scratch((N_CHIP, FF_SH // 2, D_MODEL)) + _ChipExchange.scratch(prev_wire),
        compiler_params=_params(58),
    )(dh1a, du, r1, g1, b1, wu, prev_wire)


def _outproj_bwd(dr1, mc, w_out, prev_wire):
    s_len = dr1.shape[0]
    tm = _tile(s_len, 512)

    def body(dr1_ref, mc_ref, w_ref, pw_ref, dmc_ref, dw_ref, wire_ref, db_ref, pl_ref,
             land_ref, send_sem, recv_sem, xl_ref, x_send, x_recv, x_flush):
        i = pl.program_id(0)
        exchange = _ChipExchange(pw_ref, xl_ref, x_send, x_recv)

        @pl.when(i == 0)
        def _():
            exchange.start()
            dw_ref[...] = jnp.zeros_like(dw_ref)
            db_ref[...] = jnp.zeros_like(db_ref)

        d = dr1_ref[...]
        db_ref[...] += _colsum(d)
        db16 = d.astype(_MXU)
        dmc_ref[...] = _dot_nt(db16, w_ref[...])
        dw_ref[...] += _dot_tn(mc_ref[...], db16)

        @pl.when(i == pl.num_programs(0) - 1)
        def _():
            _pair_reduce(dw_ref, land_ref, wire_ref, send_sem, recv_sem)
            exchange.finish_to(pl_ref, x_flush)

    return pl.pallas_call(
        body, name="outproj_bwd", grid=(s_len // tm,),
        in_specs=[_rows(tm, D_MODEL), _rows(tm, D_MODEL), _vmem(), _vmem()],
        out_specs=[_rows(tm, D_MODEL), _vmem(), _vmem(), _const2((1, D_MODEL)), _hbm()],
        out_shape=[_hbm_shape((s_len, D_MODEL), F32), jax.ShapeDtypeStruct((D_MODEL, D_MODEL), F32),
                   jax.ShapeDtypeStruct((N_CHIP, OUT_SH // 2, D_MODEL), _WIRE), _hbm_shape((1, D_MODEL), F32),
                   _ChipExchange.land_shape(prev_wire)],
        scratch_shapes=_pair_scratch((N_CHIP, OUT_SH // 2, D_MODEL)) + _ChipExchange.scratch(prev_wire),
        compiler_params=_params(48),
    )(dr1, mc, w_out, prev_wire)


def _mixer_bwd(q, k, v, su, sv, dmc, tc, t1, t2, sinks, sg, sb, sgu_w, sgu_bt, prev_wire):
    s_len = q.shape[0]
    nb = s_len // BLK

    def body(q_ref, kc_ref, kp_ref, vc_ref, vp_ref, su_ref, sv_ref, dmc_ref,
             tc_ref, t1_ref, t2_ref, tcp_ref, t1p_ref, t2p_ref,
             sink_ref, lg_ref, lb_ref, w_ref, bt_ref, pw_ref,
             dq_ref, dkv_ref, dsuv_ref, dbq_ref, dbkv_ref, dbsuv_ref,
             dsink_ref, dlg_ref, dlb_ref, dw_ref, dbt_ref, pl_ref, carry_ref, xl_ref, x_send, x_recv, x_flush):
        i = pl.program_id(0)
        exchange = _ChipExchange(pw_ref, xl_ref, x_send, x_recv)

        @pl.when(i == 0)
        def _():
            exchange.start()

        @pl.when(i == 0)
        def _():
            for r in (dbq_ref, dbkv_ref, dbsuv_ref, dsink_ref, dlg_ref, dlb_ref, dw_ref, dbt_ref):
                r[...] = jnp.zeros_like(r)

        def emit_kv(fin):
            dk = _rope_bwd(fin[:, 0:KV_W], tcp_ref[...], t1p_ref[...], t2p_ref[...])
            out = jnp.concatenate([dk, fin[:, KV_W:2 * KV_W]], axis=1)
            dkv_ref[...] = out.astype(_MXU)
            dbkv_ref[...] += _colsum(out)

        @pl.when(i < nb)
        def _():
            allowed_t = _band_mask_t(i == 0)
            kb = jnp.concatenate([kp_ref[...], kc_ref[...]], axis=0)
            vb = jnp.concatenate([vp_ref[...], vc_ref[...]], axis=0)
            qv = q_ref[...]
            dmc = dmc_ref[...]
            dqs, dks, dvs, dsinks = [], [], [], []
            allowed_g = jnp.tile(allowed_t, (1, Q_PER_KV))
            for g in range(N_KV):
                heads = range(g * Q_PER_KV, (g + 1) * Q_PER_KV)
                kh = kb[:, g * HEAD_DIM:(g + 1) * HEAD_DIM]
                vh = vb[:, g * HEAD_DIM:(g + 1) * HEAD_DIM]
                q_g = jnp.concatenate([qv[:, h * HEAD_DIM:(h + 1) * HEAD_DIM] for h in heads], axis=0)
                do_g = jnp.concatenate([dmc[:, h * HEAD_DIM:(h + 1) * HEAD_DIM] for h in heads], axis=0).astype(_MXU)
                sink_g = jnp.concatenate([jnp.full((1, BLK), sink_ref[h], F32) for h in heads], axis=1)
                probs_t, psink = _attn_probs_t(kh, q_g, sink_g, allowed_g)
                dvs.append(_dot(probs_t.astype(_MXU), do_g))
                dp_t = _dot_nt(vh, do_g)
                rd = jnp.sum(probs_t * dp_t, axis=0, keepdims=True)
                ds_t = (probs_t * (dp_t - rd)).astype(_MXU)
                ps_rd = psink * rd
                for hh in range(Q_PER_KV):
                    dsinks.append(-jnp.sum(ps_rd[:, hh * BLK:(hh + 1) * BLK], axis=1, keepdims=True))
                dq_g = _dot_tn(ds_t, kh)
                dqs += [dq_g[hh * BLK:(hh + 1) * BLK, :] for hh in range(Q_PER_KV)]
                dks.append(_dot(ds_t, q_g))
            dq = _rope_bwd(jnp.concatenate(dqs, axis=1) * (HEAD_DIM ** -0.5), tc_ref[...], t1_ref[...], t2_ref[...])
            dq_ref[...] = dq.astype(_MXU)
            dbq_ref[...] += _colsum(dq)
            dsink_ref[...] += _lane_put(dsinks, 128)
            contrib = jnp.concatenate(dks + dvs, axis=1)

            @pl.when(i > 0)
            def _():
                emit_kv(carry_ref[...] + contrib[0:BLK, :])

            carry_ref[...] = contrib[BLK:2 * BLK, :]

            su = su_ref[...]
            sv = sv_ref[...]
            lg = lg_ref[...]
            u, mixed, du_dsu, dgv_dsv, vhat, rstd, vvb, wcs = _sgu_fwd(su, sv, lg, lb_ref[...], w_ref, bt_ref)
            dsgu = dmc[:, ATTN_W:D_MODEL]
            dsu = dsgu * mixed * du_dsu
            dmixed = dsgu * u
            tri_t = lax.broadcasted_iota(jnp.int32, (BLK, BLK), 0)
            tri_s = lax.broadcasted_iota(jnp.int32, (BLK, BLK), 1)
            dvv, dbs = [], []
            for h in range(N_GRP):
                dm = dmixed[:, h * GRP_DIM:(h + 1) * GRP_DIM]
                dmb = dm.astype(_MXU)
                dbs.append(jnp.sum(dm, axis=1, keepdims=True))
                dw_ref[h] += jnp.where(tri_s <= tri_t, _dot_nt(dmb, vvb[:, h * GRP_DIM:(h + 1) * GRP_DIM]), 0.0)
                dvv.append(_dot_tn(wcs[h], dmb))
            dvv = jnp.concatenate(dvv, axis=1)
            dbt_ref[...] += _lane_put(dbs, 128)
            dlg_ref[...] += _colsum(dvv * vhat)
            dlb_ref[...] += _colsum(dvv)
            dsv = _ln_bwd(dvv, vhat, rstd, lg) * dgv_dsv
            dsuv = jnp.concatenate([dsu, dsv], axis=1)
            dsuv_ref[...] = dsuv.astype(_MXU)
            dbsuv_ref[...] += _colsum(dsuv)

        @pl.when(i == nb)
        def _():
            emit_kv(carry_ref[...])
            exchange.finish_to(pl_ref, x_flush)

    last = nb - 1
    cur = lambda w: pl.BlockSpec((BLK, w), lambda i: (jnp.minimum(i, last), 0))
    prev = lambda w: pl.BlockSpec((BLK, w), lambda i: (jnp.clip(i - 1, 0, last), 0))
    sd = _hbm_shape
    return pl.pallas_call(
        body, name="mixer_bwd", grid=(nb + 1,),
        in_specs=[cur(ATTN_W), cur(KV_W), prev(KV_W), cur(KV_W), prev(KV_W), cur(SGU_W), cur(SGU_W), cur(D_MODEL),
                  cur(128), cur(128), cur(128), prev(128), prev(128), prev(128),
                  _smem(), _const2((1, SGU_W)), _const2((1, SGU_W)), _const2((N_GRP, BLK, BLK)), _const2((BLK, N_GRP)),
                  _vmem()],
        out_specs=[cur(ATTN_W), prev(2 * KV_W), cur(2 * SGU_W),
                   _const2((1, ATTN_W)), _const2((1, 2 * KV_W)), _const2((1, 2 * SGU_W)),
                   _const2((1, 128)), _const2((1, SGU_W)), _const2((1, SGU_W)),
                   _const2((N_GRP, BLK, BLK)), _const2((BLK, 128)), _hbm()],
        out_shape=[sd((s_len, ATTN_W), _MXU), sd((s_len, 2 * KV_W), _MXU), sd((s_len, 2 * SGU_W), _MXU),
                   sd((1, ATTN_W), F32), sd((1, 2 * KV_W), F32), sd((1, 2 * SGU_W), F32),
                   sd((1, 128), F32), sd((1, SGU_W), F32), sd((1, SGU_W), F32),
                   sd((N_GRP, BLK, BLK), F32), sd((BLK, 128), F32), _ChipExchange.land_shape(prev_wire)],
        scratch_shapes=[pltpu.VMEM((BLK, 2 * KV_W), F32)] + _ChipExchange.scratch(prev_wire),
        compiler_params=_params(32),
    )(q, k, k, v, v, su, sv, dmc, tc, t1, t2, tc, t1, t2, sinks, sg, sb, sgu_w, sgu_bt, prev_wire)


def _inproj_bwd(dq, dkv, dsuv, dr1, x, g0, b0, w_in):
    s_len = x.shape[0]
    tm = _tile(s_len, 512)
    cuts = ((0, ATTN_W), (ATTN_W, ATTN_W + 2 * KV_W), (ATTN_W + 2 * KV_W, IN_W))

    def body(dq_ref, dkv_ref, dsuv_ref, dr1_ref, x_ref, g_ref, b_ref, w_ref, dx_ref, dw_ref, dg_ref, db_ref):
        i = pl.program_id(0)

        @pl.when(i == 0)
        def _():
            dw_ref[...] = jnp.zeros_like(dw_ref)
            dg_ref[...] = jnp.zeros_like(dg_ref)
            db_ref[...] = jnp.zeros_like(db_ref)

        h0, xhat, rstd = _ln(x_ref[...], g_ref[...], b_ref[...])
        h0b = h0.astype(_MXU)
        dh0 = ALPHA * dr1_ref[...]
        for (lo, hi), d_ref in zip(cuts, (dq_ref, dkv_ref, dsuv_ref)):
            d = d_ref[...]
            dh0 = dh0 + _dot(d, w_ref[lo:hi, :])
            dw_ref[lo:hi, :] += _dot_tn(d, h0b)
        dg_ref[...] += _colsum(dh0 * xhat)
        db_ref[...] += _colsum(dh0)
        dx_ref[...] = _ln_bwd(dh0, xhat, rstd, g_ref[...])

    vec = _hbm_shape((1, D_MODEL), F32)
    c = _const2((1, D_MODEL))
    return pl.pallas_call(
        body, name="inproj_bwd", grid=(s_len // tm,),
        in_specs=[_rows(tm, ATTN_W), _rows(tm, 2 * KV_W), _rows(tm, 2 * SGU_W), _rows(tm, D_MODEL), _rows(tm, D_MODEL),
                  c, c, _vmem()],
        out_specs=[_rows(tm, D_MODEL), _vmem(), c, c],
        out_shape=[_hbm_shape((s_len, D_MODEL), F32), jax.ShapeDtypeStruct((IN_W, D_MODEL), F32), vec, vec],
        compiler_params=_params(48),
    )(dq, dkv, dsuv, dr1, x, g0, b0, w_in)


def _place():
    x, y, c = (lax.axis_index(a) for a in MESH_AXES)
    chips = [(1 - x, y), (x, 1 - y), (1 - x, 1 - y)]
    return x, y, c, chips


class _Gather:
    def __init__(self, ins, outs, send_sems, recv_sems, spans=None):
        self.ins, self.outs, self.send_sems, self.recv_sems = ins, outs, send_sems, recv_sems
        self.n = len(ins)
        self.spans = spans or [(0, r.shape[0]) for r in ins]
        self.halves = [(hi - lo) // 2 for lo, hi in self.spans]

    def _copy(self, k, t, slot, half, to):
        rows = pl.ds(pl.multiple_of(self.spans[t][0] + half * self.halves[t], 16), self.halves[t])
        piece = self.outs[t].at[slot, rows, :]
        return pltpu.make_async_remote_copy(src_ref=piece, dst_ref=piece, send_sem=self.send_sems.at[k],
                                            recv_sem=self.recv_sems.at[k], device_id=to, device_id_type=MESH)

    def _chip_copy(self, t, d, slot):
        x, y, c, chips = _place()
        return self._copy(3 * t + d, t, slot, c, (chips[d][0], chips[d][1], c))

    def _pass_copy(self, t, d, half):
        x, y, c, chips = _place()
        return self._copy(3 * self.n + 3 * t + d, t, 2 * chips[d][0] + chips[d][1], half, (x, y, 1 - c))

    def start(self):
        x, y, c, chips = _place()
        me = 2 * x + y
        for t in range(self.n):
            lo, hi = self.spans[t]
            self.outs[t][me, lo:hi, :] = self.ins[t][lo:hi, :].astype(_WIRE)
        for t in range(self.n):
            for d in range(3):
                self._chip_copy(t, d, me).start()

    def finish(self):
        x, y, c, chips = _place()
        me = 2 * x + y
        for t in range(self.n):
            for d in range(3):
                self._chip_copy(t, d, 2 * chips[d][0] + chips[d][1]).wait_recv()
                self._pass_copy(t, d, c).start()
        for t in range(self.n):
            for d in range(3):
                self._pass_copy(t, d, 1 - c).wait_recv()
        for t in range(self.n):
            for d in range(3):
                self._chip_copy(t, d, me).wait_send()
                self._pass_copy(t, d, c).wait_send()

    @staticmethod
    def out_shapes(shards, make=jax.ShapeDtypeStruct):
        return [make((N_CHIP,) + s.shape, _WIRE) for s in shards]

    @staticmethod
    def sems(n):
        return [pltpu.SemaphoreType.DMA((6 * n,)), pltpu.SemaphoreType.DMA((6 * n,))]


class _GatherPlan:
    def __init__(self, pieces):
        self.shards = [p[0] for p in pieces]
        self.spans = [p[1] for p in pieces]
        self.earlier = [p[2] for p in pieces]
        self.n = len(pieces)
        self.carried = [t for t in range(self.n) if self.earlier[t] is not None]

    def operands(self):
        return self.shards + [self.earlier[t] for t in self.carried]

    def in_specs(self):
        return [_vmem()] * self.n + [_hbm()] * len(self.carried)

    def out_specs(self):
        return [_hbm()] * self.n

    def out_shapes(self):
        return _Gather.out_shapes(self.shards, _hbm_shape)

    def scratch(self):
        return ([pltpu.VMEM((N_CHIP,) + s.shape, _WIRE) for s in self.shards] + _Gather.sems(self.n)
                + [pltpu.SemaphoreType.DMA((self.n,)), pltpu.SemaphoreType.DMA((max(len(self.carried), 1),))])

    def bind(self, in_refs, out_refs, scratch_refs):
        plan = self
        shard_refs, earlier_refs = in_refs[:self.n], in_refs[self.n:]
        bufs = scratch_refs[:self.n]
        send_sems, recv_sems, flush_sems, carry_sems = scratch_refs[self.n:self.n + 4]
        gather = _Gather(shard_refs, bufs, send_sems, recv_sems, self.spans)

        def carry_copy(k):
            t = plan.carried[k]
            lo = plan.spans[t][0]
            return pltpu.make_async_copy(earlier_refs[k].at[:, 0:lo, :], bufs[t].at[:, 0:lo, :], carry_sems.at[k])

        class Bound:
            @staticmethod
            def start():
                for k in range(len(plan.carried)):
                    carry_copy(k).start()
                gather.start()

            @staticmethod
            def finish():
                gather.finish()
                for k in range(len(plan.carried)):
                    carry_copy(k).wait()
                _flush([bufs[t].at[:, 0:plan.spans[t][1], :] for t in range(plan.n)],
                       [out_refs[t].at[:, 0:plan.spans[t][1], :] for t in range(plan.n)], flush_sems)

        return Bound


def _flush(bufs, hbm_outs, sems):
    copies = [pltpu.make_async_copy(b, o, sems.at[k]) for k, (b, o) in enumerate(zip(bufs, hbm_outs))]
    for cp in copies:
        cp.start()
    for cp in copies:
        cp.wait()


def _gather_weights(shards):
    n = len(shards)

    def body(*refs):
        gather = _Gather(refs[:n], refs[n:2 * n], refs[2 * n], refs[2 * n + 1])
        gather.start()
        gather.finish()

    return pl.pallas_call(
        body, name="gather_weights",
        in_specs=[_vmem()] * n, out_specs=[_vmem()] * n,
        out_shape=_Gather.out_shapes(shards), scratch_shapes=_Gather.sems(n),
        compiler_params=pltpu.CompilerParams(vmem_limit_bytes=32 * MIB),
    )(*shards)


class _ChipExchange:
    def __init__(self, wire_ref, land_ref, send_sems, recv_sems):
        self.wire, self.land, self.send_sems, self.recv_sems = wire_ref, land_ref, send_sems, recv_sems

    def _copy(self, d):
        x, y, c, chips = _place()
        return pltpu.make_async_remote_copy(
            src_ref=self.wire.at[2 * chips[d][0] + chips[d][1]], dst_ref=self.land.at[d],
            send_sem=self.send_sems.at[d], recv_sem=self.recv_sems.at[d],
            device_id=(chips[d][0], chips[d][1], c), device_id_type=MESH)

    def start(self):
        for d in range(3):
            self._copy(d).start()

    def wait_recv(self):
        for d in range(3):
            self._copy(d).wait_recv()

    def wait_send(self):
        for d in range(3):
            self._copy(d).wait_send()

    def finish_to(self, hbm_out, flush_sem):
        self.wait_recv()
        _flush([self.land], [hbm_out], flush_sem)
        self.wait_send()

    @staticmethod
    def land_shape(wire):
        return _hbm_shape((3,) + wire.shape[1:], wire.dtype)

    @staticmethod
    def sems():
        return [pltpu.SemaphoreType.DMA((3,)), pltpu.SemaphoreType.DMA((3,))]

    @staticmethod
    def scratch(wire):
        return ([pltpu.VMEM((3,) + wire.shape[1:], wire.dtype)] + _ChipExchange.sems() + [pltpu.SemaphoreType.DMA((1,))])


def _pair_scratch(half_shape):
    return [pltpu.VMEM(half_shape, F32), pltpu.SemaphoreType.DMA((N_CHIP,)), pltpu.SemaphoreType.DMA((N_CHIP,))]


def _pair_reduce(acc_ref, land_ref, wire_ref, send_sems, recv_sems):
    rh = land_ref.shape[1]
    x, y, c, _ = _place()
    copies = []
    for j in range(N_CHIP):
        give = acc_ref.at[pl.ds(pl.multiple_of(j * 2 * rh + (1 - c) * rh, 8), rh), :]
        cp = pltpu.make_async_remote_copy(src_ref=give, dst_ref=land_ref.at[j], send_sem=send_sems.at[j],
                                          recv_sem=recv_sems.at[j], device_id=(x, y, 1 - c), device_id_type=MESH)
        cp.start()
        copies.append(cp)
    for j in range(N_CHIP):
        copies[j].wait_recv()

        def chunk(r, carry, j=j):
            theirs = pl.ds(pl.multiple_of(r * ROW_CHUNK, ROW_CHUNK), ROW_CHUNK)
            mine = pl.ds(pl.multiple_of(j * 2 * rh + c * rh + r * ROW_CHUNK, 8), ROW_CHUNK)
            s = acc_ref[mine, :] + land_ref[j, theirs, :]
            acc_ref[mine, :] = s
            wire_ref[j, theirs, :] = s.astype(_WIRE)
            return carry

        lax.fori_loop(0, rh // ROW_CHUNK, chunk, 0)
    for cp in copies:
        cp.wait_send()


def _grad_finish(last_acc, lands, accs):
    n = len(accs) + 1
    halves = [last_acc.shape[0] // (2 * N_CHIP)] + [w.shape[1] for w in lands]
    widths = [last_acc.shape[1]] + [a.shape[1] for a in accs]

    def body(*refs):
        acc0, land, acc, g = refs[0], (None,) + refs[1:n], (None,) + refs[n:2 * n - 1], refs[2 * n - 1:3 * n - 1]
        pland0, wire0, land0 = refs[3 * n - 1:3 * n + 2]
        own = refs[3 * n + 2:4 * n + 2]
        p_send, p_recv, x_send, x_recv, pair_send, pair_recv, local_sems = refs[4 * n + 2:4 * n + 9]
        land = (land0,) + land[1:]
        x, y, c, chips = _place()
        me = 2 * x + y
        exchange = _ChipExchange(wire0, land0, x_send, x_recv)

        def half_rows(t, half):
            return pl.ds(pl.multiple_of(half * halves[t], 8), halves[t])

        def own_copy(t):
            rows = pl.ds(pl.multiple_of((2 * me + c) * halves[t], 8), halves[t])
            return pltpu.make_async_copy(acc[t].at[rows, :], own[t], local_sems.at[t])

        def pair_copy(t, half):
            rows = g[t].at[half_rows(t, half), :]
            return pltpu.make_async_remote_copy(src_ref=rows, dst_ref=rows, send_sem=pair_send.at[t],
                                                recv_sem=pair_recv.at[t], device_id=(x, y, 1 - c), device_id_type=MESH)

        for t in range(1, n):
            own_copy(t).start()

        rh = halves[0]
        gives = []
        for j in range(N_CHIP):
            rows = acc0.at[pl.ds(pl.multiple_of((2 * j + 1 - c) * rh, 8), rh), :]
            cp = pltpu.make_async_remote_copy(src_ref=rows, dst_ref=pland0.at[j], send_sem=p_send.at[j],
                                              recv_sem=p_recv.at[j], device_id=(x, y, 1 - c), device_id_type=MESH)
            cp.start()
            gives.append(cp)
        for cp in gives:
            cp.wait()

        def chip_sum(r, carry):
            src = pl.ds(pl.multiple_of(r * ROW_CHUNK, ROW_CHUNK), ROW_CHUNK)
            for j in range(N_CHIP):
                mine = pl.ds(pl.multiple_of((2 * j + c) * rh + r * ROW_CHUNK, 8), ROW_CHUNK)
                wire0[j, src, :] = (acc0[mine, :] + pland0[j, src, :]).astype(_WIRE)
            mine = pl.ds(pl.multiple_of((2 * me + c) * rh + r * ROW_CHUNK, 8), ROW_CHUNK)
            own[0][src, :] = acc0[mine, :] + pland0[me, src, :]
            return carry

        lax.fori_loop(0, rh // ROW_CHUNK, chip_sum, 0)
        exchange.start()

        for t in list(range(1, n)) + [0]:
            if t == 0:
                exchange.wait_recv()
            else:
                own_copy(t).wait()

            def chunk(r, carry, t=t):
                src = pl.ds(pl.multiple_of(r * ROW_CHUNK, ROW_CHUNK), ROW_CHUNK)
                dst = pl.ds(pl.multiple_of(c * halves[t] + r * ROW_CHUNK, 8), ROW_CHUNK)
                s = own[t][src, :]
                for d in range(3):
                    s = s + land[t][d, src, :].astype(F32)
                g[t][dst, :] = s
                return carry

            lax.fori_loop(0, halves[t] // ROW_CHUNK, chunk, 0)
            pair_copy(t, c).start()
        for t in range(n):
            pair_copy(t, 1 - c).wait_recv()
        for t in range(n):
            pair_copy(t, c).wait_send()
        exchange.wait_send()

    half0 = (halves[0], widths[0])
    return pl.pallas_call(
        body, name="grad_finish",
        in_specs=[_vmem()] * n + [_hbm()] * (n - 1), out_specs=[_vmem()] * n,
        out_shape=[jax.ShapeDtypeStruct((2 * h, w), F32) for h, w in zip(halves, widths)],
        scratch_shapes=[pltpu.VMEM((N_CHIP,) + half0, F32), pltpu.VMEM((N_CHIP,) + half0, _WIRE),
                        pltpu.VMEM((3,) + half0, _WIRE)]
        + [pltpu.VMEM((h, w), F32) for h, w in zip(halves, widths)]
        + [pltpu.SemaphoreType.DMA((N_CHIP,)), pltpu.SemaphoreType.DMA((N_CHIP,))]
        + _ChipExchange.sems()
        + [pltpu.SemaphoreType.DMA((n,)), pltpu.SemaphoreType.DMA((n,)), pltpu.SemaphoreType.DMA((n,))],
        compiler_params=pltpu.CompilerParams(vmem_limit_bytes=56 * MIB),
    )(last_acc, *lands, *accs)


_SMALL = ("ln_in_g", "ln_in_b", "b_in", "attn_sinks", "sgu_ln_g", "sgu_ln_b", "sgu_w", "sgu_b", "b_out",
          "ln_mix_g", "ln_mix_b", "ln_ffn_g", "ln_ffn_b")
_VEC_ROW = dict(ln_in_g=0, ln_in_b=1, b_in=2, attn_sinks=4, sgu_ln_g=5, sgu_ln_b=6, b_out=7, ln_mix_g=8, ln_mix_b=9,
                ln_ffn_g=10, ln_ffn_b=11)
_LOSS_ROW = 12
_VEC_ROWS = 16
_MAT_ROWS = N_GRP * BLK + BLK


def _small_allreduce(local):
    n_in = 16

    def body(*refs):
        (g_ln_in_g, g_ln_in_b, g_bq, g_bkv, g_bsuv, g_sink, g_sln_g, g_sln_b, g_sw, g_sbt, g_bout,
         g_lmg, g_lmb, g_lfg, g_lfb, g_loss) = refs[:n_in]
        out_a, out_b = refs[n_in:n_in + 2]
        (buf_a, buf_b, pair_a, pair_b, stage_a, stage_b, tot_a, tot_b,
         p1_send, p1_recv, x_send, x_recv, p2_send, p2_recv) = refs[n_in + 2:]
        x, y, c, chips = _place()
        me = 2 * x + y
        sibling = (x, y, 1 - c)
        half_a, half_b = _VEC_ROWS // 2, _MAT_ROWS // 2

        buf_a[...] = jnp.zeros_like(buf_a)
        for row, ref in ((0, g_ln_in_g), (1, g_ln_in_b), (7, g_bout), (8, g_lmg), (9, g_lmb), (10, g_lfg), (11, g_lfb),
                         (_LOSS_ROW, g_loss)):
            buf_a[row:row + 1, :] = ref[...]
        buf_a[2:3, 0:ATTN_W] = g_bq[...]
        buf_a[2:3, ATTN_W:ATTN_W + 2 * KV_W] = g_bkv[...]
        buf_a[2:3, ATTN_W + 2 * KV_W:D_MODEL] = g_bsuv[:, 0:2 * KV_W]
        buf_a[3:4, 0:2 * SGU_W - 2 * KV_W] = g_bsuv[:, 2 * KV_W:2 * SGU_W]
        buf_a[4:5, 0:128] = g_sink[...]
        buf_a[5:6, 0:SGU_W] = g_sln_g[...]
        buf_a[6:7, 0:SGU_W] = g_sln_b[...]
        for h in range(N_GRP):
            buf_b[h * BLK:(h + 1) * BLK, :] = g_sw[h]
        buf_b[N_GRP * BLK:_MAT_ROWS, :] = g_sbt[...]

        def remote(src, dst, send_sem, recv_sem, to):
            return pltpu.make_async_remote_copy(src_ref=src, dst_ref=dst, send_sem=send_sem, recv_sem=recv_sem,
                                                device_id=to, device_id_type=MESH)

        first = [remote(buf_a, pair_a, p1_send.at[0], p1_recv.at[0], sibling),
                 remote(buf_b, pair_b, p1_send.at[1], p1_recv.at[1], sibling)]
        for cp in first:
            cp.start()
        for cp in first:
            cp.wait()
        rows_a = pl.ds(pl.multiple_of(c * half_a, 8), half_a)
        rows_b = pl.ds(pl.multiple_of(c * half_b, 8), half_b)
        stage_a[me] = buf_a[rows_a, :] + pair_a[rows_a, :]
        stage_b[me] = buf_b[rows_b, :] + pair_b[rows_b, :]

        def chip_copies(d):
            to = (chips[d][0], chips[d][1], c)
            return [remote(stage_a.at[me], stage_a.at[me], x_send.at[2 * d], x_recv.at[2 * d], to),
                    remote(stage_b.at[me], stage_b.at[me], x_send.at[2 * d + 1], x_recv.at[2 * d + 1], to)]

        def chip_arrivals(d):
            slot = 2 * chips[d][0] + chips[d][1]
            to = (chips[d][0], chips[d][1], c)
            return [remote(stage_a.at[slot], stage_a.at[slot], x_send.at[2 * d], x_recv.at[2 * d], to),
                    remote(stage_b.at[slot], stage_b.at[slot], x_send.at[2 * d + 1], x_recv.at[2 * d + 1], to)]

        for d in range(3):
            for cp in chip_copies(d):
                cp.start()
        for d in range(3):
            for cp in chip_arrivals(d):
                cp.wait_recv()
        tot_a[rows_a, :] = ((stage_a[0] + stage_a[1]) + stage_a[2]) + stage_a[3]
        tot_b[rows_b, :] = ((stage_b[0] + stage_b[1]) + stage_b[2]) + stage_b[3]

        second = [remote(tot_a.at[rows_a, :], tot_a.at[rows_a, :], p2_send.at[0], p2_recv.at[0], sibling),
                  remote(tot_b.at[rows_b, :], tot_b.at[rows_b, :], p2_send.at[1], p2_recv.at[1], sibling)]
        for cp in second:
            cp.start()
        other_a = pl.ds(pl.multiple_of((1 - c) * half_a, 8), half_a)
        other_b = pl.ds(pl.multiple_of((1 - c) * half_b, 8), half_b)
        remote(tot_a.at[other_a, :], tot_a.at[other_a, :], p2_send.at[0], p2_recv.at[0], sibling).wait_recv()
        remote(tot_b.at[other_b, :], tot_b.at[other_b, :], p2_send.at[1], p2_recv.at[1], sibling).wait_recv()
        for cp in second:
            cp.wait_send()
        for d in range(3):
            for cp in chip_copies(d):
                cp.wait_send()
        out_a[...] = tot_a[...]
        out_b[...] = tot_b[...]

    ins = [local[k] for k in ("ln_in_g", "ln_in_b", "bq", "bkv", "bsuv", "sink", "sgu_ln_g", "sgu_ln_b", "sgu_w",
                              "sgu_bt", "b_out", "ln_mix_g", "ln_mix_b", "ln_ffn_g", "ln_ffn_b", "loss")]
    out_dims = [(_VEC_ROWS, D_MODEL), (_MAT_ROWS, 128)]
    vec = pltpu.VMEM((_VEC_ROWS, D_MODEL), F32)
    mat = pltpu.VMEM((_MAT_ROWS, 128), F32)
    return pl.pallas_call(
        body, name="small_allreduce", grid=(1,),
        in_specs=[_const2(a.shape) for a in ins], out_specs=[_const2(s) for s in out_dims],
        out_shape=[_hbm_shape(s, F32) for s in out_dims],
        scratch_shapes=[vec, mat, vec, mat, pltpu.VMEM((N_CHIP, _VEC_ROWS // 2, D_MODEL), F32),
                        pltpu.VMEM((N_CHIP, _MAT_ROWS // 2, 128), F32), vec, mat,
                        pltpu.SemaphoreType.DMA((2,)), pltpu.SemaphoreType.DMA((2,)), pltpu.SemaphoreType.DMA((6,)),
                        pltpu.SemaphoreType.DMA((6,)), pltpu.SemaphoreType.DMA((2,)), pltpu.SemaphoreType.DMA((2,))],
        compiler_params=pltpu.CompilerParams(vmem_limit_bytes=32 * MIB),
    )(*ins)


def _small_adamw(tot_a, tot_b, params):
    shapes = [params[nm][0].shape for nm in _SMALL]

    def body(*refs):
        ta, tb = refs[:2]
        prm = refs[2:2 + 3 * len(_SMALL)]
        outs = refs[2 + 3 * len(_SMALL):]

        def grad_of(k, name):
            if name == "sgu_w":
                return [tb[h * BLK:(h + 1) * BLK, :] for h in range(N_GRP)]
            if name == "sgu_b":
                return jnp.transpose(tb[N_GRP * BLK:_MAT_ROWS, :])[0:N_GRP, :]
            row = _VEC_ROW[name]
            if name == "b_in":
                return jnp.concatenate([ta[row:row + 1, :], ta[row + 1:row + 2, 0:IN_W - D_MODEL]], axis=1)
            return ta[row:row + 1, 0:shapes[k][-1]]

        for k, name in enumerate(_SMALL):
            w_ref, m_ref, v_ref = prm[3 * k:3 * k + 3]
            g_out, d_out, m_out, v_out = outs[4 * k:4 * k + 4]
            g = grad_of(k, name)
            if name == "sgu_w":
                for h in range(N_GRP):
                    d_, m_, v_ = _adamw_math(w_ref[h], g[h], m_ref[h], v_ref[h])
                    g_out[h], d_out[h], m_out[h], v_out[h] = g[h], d_, m_, v_
            else:
                d_, m_, v_ = _adamw_math(w_ref[...], g, m_ref[...], v_ref[...])
                g_out[...], d_out[...], m_out[...], v_out[...] = g, d_, m_, v_
        outs[-1][...] = ta[_LOSS_ROW:_LOSS_ROW + 1, :]

    ins = [tot_a, tot_b] + [_in_hbm(a) for nm in _SMALL for a in params[nm]]
    out_dims = [s for s in shapes for _ in range(4)] + [(1, D_MODEL)]
    res = pl.pallas_call(
        body, name="small_adamw", grid=(1,),
        in_specs=[_const2(a.shape) for a in ins], out_specs=[_const2(s) for s in out_dims],
        out_shape=[_hbm_shape(s, F32) for s in out_dims],
        compiler_params=_params(32),
    )(*ins)
    return {nm: tuple(res[4 * k:4 * k + 4]) for k, nm in enumerate(_SMALL)}, res[-1]


def _elementwise(name, fn, ins, out_dtypes, tile_rows=256):
    shape = ins[0].shape
    lead = shape[:-2]
    rows, cols = shape[-2:]
    tr = _tile(rows, tile_rows)
    n_lead = math.prod(lead)
    nr = rows // tr
    flat = [_in_hbm(a.reshape((n_lead, rows, cols))) for a in ins]

    def body(*refs):
        outs = fn(*[r[0] for r in refs[:len(ins)]])
        for o_ref, o in zip(refs[len(ins):], outs):
            o_ref[0] = o.astype(o_ref.dtype)

    spec = pl.BlockSpec((1, tr, cols), lambda i: (i // nr, i % nr, 0))
    res = pl.pallas_call(
        body, name=name, grid=(n_lead * nr,),
        in_specs=[spec] * len(ins), out_specs=[spec] * len(out_dtypes),
        out_shape=[_hbm_shape((n_lead, rows, cols), dt) for dt in out_dtypes],
        compiler_params=_params(32),
    )(*flat)
    return [r.reshape(shape) for r in res]


def _adamw_math(w, g, m, v):
    m = ADAM_B1 * m + (1.0 - ADAM_B1) * g
    v = ADAM_B2 * v + (1.0 - ADAM_B2) * (g * g)
    m_hat = m / (1.0 - ADAM_B1 ** ADAM_STEP)
    v_hat = v / (1.0 - ADAM_B2 ** ADAM_STEP)
    delta = -ADAM_LR * (m_hat / (jnp.sqrt(v_hat) + ADAM_EPS) + ADAM_WD * w)
    return delta, m, v


def _adamw(name, w, g, m, v, tile_rows=256):
    return _elementwise(name, lambda w_, g_, m_, v_: (g_,) + _adamw_math(w_, g_, m_, v_), [w, g, m, v],
                        [F32, F32, F32, F32], tile_rows)


def kernel(x, positions, ln_in_g, ln_in_b, w_in, b_in, attn_sinks, sgu_ln_g, sgu_ln_b, sgu_w, sgu_b, w_out, b_out, ln_mix_g, ln_mix_b, w_gate, w_up, w_down, ln_ffn_g, ln_ffn_b, loss_target, m_ln_in_g, m_ln_in_b, m_w_in, m_b_in, m_attn_sinks, m_sgu_ln_g, m_sgu_ln_b, m_sgu_w, m_sgu_b, m_w_out, m_b_out, m_ln_mix_g, m_ln_mix_b, m_w_gate, m_w_up, m_w_down, m_ln_ffn_g, m_ln_ffn_b, v_ln_in_g, v_ln_in_b, v_w_in, v_b_in, v_attn_sinks, v_sgu_ln_g, v_sgu_ln_b, v_sgu_w, v_sgu_b, v_w_out, v_b_out, v_ln_mix_g, v_ln_mix_b, v_w_gate, v_w_up, v_w_down, v_ln_ffn_g, v_ln_ffn_b):
    weights = dict(ln_in_g=ln_in_g, ln_in_b=ln_in_b, w_in=w_in, b_in=b_in, attn_sinks=attn_sinks, sgu_ln_g=sgu_ln_g,
                   sgu_ln_b=sgu_ln_b, sgu_w=sgu_w, sgu_b=sgu_b, w_out=w_out, b_out=b_out, ln_mix_g=ln_mix_g,
                   ln_mix_b=ln_mix_b, w_gate=w_gate, w_up=w_up, w_down=w_down, ln_ffn_g=ln_ffn_g, ln_ffn_b=ln_ffn_b)
    mom_m = dict(ln_in_g=m_ln_in_g, ln_in_b=m_ln_in_b, w_in=m_w_in, b_in=m_b_in, attn_sinks=m_attn_sinks,
                 sgu_ln_g=m_sgu_ln_g, sgu_ln_b=m_sgu_ln_b, sgu_w=m_sgu_w, sgu_b=m_sgu_b, w_out=m_w_out, b_out=m_b_out,
                 ln_mix_g=m_ln_mix_g, ln_mix_b=m_ln_mix_b, w_gate=m_w_gate, w_up=m_w_up, w_down=m_w_down,
                 ln_ffn_g=m_ln_ffn_g, ln_ffn_b=m_ln_ffn_b)
    mom_v = dict(ln_in_g=v_ln_in_g, ln_in_b=v_ln_in_b, w_in=v_w_in, b_in=v_b_in, attn_sinks=v_attn_sinks,
                 sgu_ln_g=v_sgu_ln_g, sgu_ln_b=v_sgu_ln_b, sgu_w=v_sgu_w, sgu_b=v_sgu_b, w_out=v_w_out, b_out=v_b_out,
                 ln_mix_g=v_ln_mix_g, ln_mix_b=v_ln_mix_b, w_gate=v_w_gate, w_up=v_w_up, w_down=v_w_down,
                 ln_ffn_g=v_ln_ffn_g, ln_ffn_b=v_ln_ffn_b)
    order = list(weights)
    big = ("w_in", "w_out", "w_gate", "w_up", "w_down")

    s_len = x.shape[1]
    xs = _in_hbm(x.reshape(s_len, D_MODEL))
    tgt = _in_hbm(loss_target.reshape(s_len, D_MODEL))
    pos_col = _in_hbm(positions.reshape(s_len, 1))
    g0, b0 = _in_hbm(ln_in_g.reshape(1, D_MODEL)), _in_hbm(ln_in_b.reshape(1, D_MODEL))
    sinks = attn_sinks.reshape(N_Q)
    sgu_w3 = _in_hbm(sgu_w.reshape(N_GRP, BLK, BLK))
    sgu_bt = _in_hbm(sgu_b.reshape(N_GRP, BLK).T)
    b_in, b_out, sgu_ln_g, sgu_ln_b, ln_mix_g, ln_mix_b, ln_ffn_g, ln_ffn_b = (
        _in_hbm(a) for a in (b_in, b_out, sgu_ln_g, sgu_ln_b, ln_mix_g, ln_mix_b, ln_ffn_g, ln_ffn_b))

    col_sharded = ("w_in", "w_gate", "w_up")

    def rowmajor(name, a):
        return jnp.swapaxes(a[0], 0, 1) if name in col_sharded else a[0]

    def as_given(name, a):
        return (jnp.swapaxes(a, 0, 1) if name in col_sharded else a)[None]

    shards = [rowmajor(n, weights[n]) for n in big]
    (gw_in,) = _gather_weights(shards[0:1])
    w_in_full = gw_in.reshape(IN_W, D_MODEL)

    sh_out, sh_gate, sh_up, sh_down = shards[1:]
    cut = GATHER_CUT
    *acts, gw_out, gw_gate0 = _ln_inproj(xs, pos_col, g0, b0, w_in_full, b_in, _GatherPlan(
        [(sh_out, (0, OUT_SH), None), (sh_gate, (0, cut), None)]))
    q, k, v, su, sv, tc, t1, t2 = (_in_hbm(a) for a in acts)
    mc, gw_gate, gw_up0 = _mixer_fwd(q, k, v, su, sv, sinks, sgu_ln_g, sgu_ln_b, sgu_w3, sgu_bt, _GatherPlan(
        [(sh_gate, (cut, FF_SH), gw_gate0), (sh_up, (0, cut), None)]))
    mc = _in_hbm(mc)
    w_out_full = gw_out.reshape(D_MODEL, D_MODEL)
    r1, gw_up = _outproj(mc, w_out_full, b_out, xs, g0, b0, _GatherPlan([(sh_up, (cut, FF_SH), gw_up0)]))
    r1 = _in_hbm(r1)
    act, p_act, q_act, gw_down = _ffn_up(r1, ln_mix_g, ln_mix_b, gw_gate, gw_up,
                                         _GatherPlan([(sh_down, (0, FF_SH), None)]))
    act, p_act, q_act = _in_hbm(act), _in_hbm(p_act), _in_hbm(q_act)
    dr2, loss_cols, d_ln_ffn_g, d_ln_ffn_b = _ffn_down_loss(act, gw_down, r1, ln_mix_g, ln_mix_b, ln_ffn_g, ln_ffn_b, tgt)
    dr2 = _in_hbm(dr2)

    dg, du, acc_down, wire_down = _ffn_bwd_a(dr2, act, p_act, q_act, gw_down)
    dh1a, acc_gate, wire_gate, land_down = _ffn_bwd_g(dr2, _in_hbm(dg), r1, ln_mix_g, ln_mix_b, gw_gate, wire_down)
    dr1, acc_up, wire_up, d_ln_mix_g, d_ln_mix_b, land_gate = _ffn_bwd_u(_in_hbm(dh1a), _in_hbm(du), r1, ln_mix_g,
                                                                         ln_mix_b, gw_up, wire_gate)
    dr1 = _in_hbm(dr1)
    dmc, acc_out, wire_out, d_b_out, land_up = _outproj_bwd(dr1, mc, w_out_full, wire_up)
    (dq, dkv, dsuv, dbq, dbkv, dbsuv, d_sink, d_sgu_ln_g, d_sgu_ln_b, d_sgu_w, d_sgu_bt, land_out) = _mixer_bwd(
        q, k, v, su, sv, _in_hbm(dmc), tc, t1, t2, sinks, sgu_ln_g, sgu_ln_b, sgu_w3, sgu_bt, wire_out)
    grad_x, acc_in, d_ln_in_g, d_ln_in_b = _inproj_bwd(_in_hbm(dq), _in_hbm(dkv), _in_hbm(dsuv), dr1, xs, g0, b0,
                                                       w_in_full)

    reduced = _grad_finish(acc_in, [land_out, land_gate, land_up, land_down], [acc_out, acc_gate, acc_up, acc_down])
    small_shape = dict(ln_in_g=(1, D_MODEL), ln_in_b=(1, D_MODEL), sgu_w=(N_GRP, BLK, BLK), sgu_b=(N_GRP, BLK))
    small_local = dict(
        ln_in_g=d_ln_in_g, ln_in_b=d_ln_in_b, bq=dbq, bkv=dbkv, bsuv=dbsuv, sink=d_sink, sgu_ln_g=d_sgu_ln_g,
        sgu_ln_b=d_sgu_ln_b, sgu_w=d_sgu_w, sgu_bt=d_sgu_bt, b_out=d_b_out, ln_mix_g=d_ln_mix_g, ln_mix_b=d_ln_mix_b,
        ln_ffn_g=d_ln_ffn_g, ln_ffn_b=d_ln_ffn_b, loss=loss_cols)
    small_params = {nm: tuple(src[nm].reshape(small_shape.get(nm, src[nm].shape)) for src in (weights, mom_m, mom_v))
                    for nm in _SMALL}
    tot_a, tot_b = _small_allreduce({nm: _in_hbm(a) for nm, a in small_local.items()})
    small_out, loss_sum = _small_adamw(_in_hbm(tot_a), _in_hbm(tot_b), small_params)
    loss = jnp.sum(loss_sum) * (0.5 / D_MODEL)
    grads, delta, new_m, new_v = {}, {}, {}, {}
    for nm in _SMALL:
        grads[nm], delta[nm], new_m[nm], new_v[nm] = (a.reshape(weights[nm].shape) for a in small_out[nm])

    for t, name in enumerate(big):
        g_, d_, m_, v_ = _adamw("adamw_" + name, shards[t], reduced[t], rowmajor(name, mom_m[name]),
                                rowmajor(name, mom_v[name]))
        grads[name], delta[name], new_m[name], new_v[name] = (as_given(name, a) for a in (g_, d_, m_, v_))

    return (loss, grad_x.reshape(x.shape), *[grads[n] for n in order], *[delta[n] for n in order],
            *[new_m[n] for n in order], *[new_v[n] for n in order])
```

```python
import functools
import math

import jax
import jax.numpy as jnp
from jax import lax
from jax.experimental import pallas as pl
from jax.experimental.pallas import tpu as pltpu

F32 = jnp.float32
_MXU = jnp.bfloat16
_WIRE = jnp.bfloat16
_ACT = jnp.bfloat16

D_MODEL = 1024
ATTN_W = 512
SGU_W = 512
HEAD_DIM = 64
N_Q = 8
N_KV = 2
Q_PER_KV = 4
KV_W = 128
BLK = 128
ROT_DIM = 16
ROPE_THETA = 500000.0
N_GRP = 4
GRP_DIM = 128
D_FF = 2816
IN_W = 1792
LN_EPS = 1e-5
ALPHA = 2.0 ** 0.25
N_CHIP = 4
FF_SH = D_FF // N_CHIP
IN_SH = IN_W // N_CHIP
OUT_SH = D_MODEL // N_CHIP
ROW_CHUNK = 32
GATHER_CUT = 224

ADAM_LR = 0.001
ADAM_B1 = 0.9
ADAM_B2 = 0.999
ADAM_EPS = 1e-08
ADAM_WD = 0.01
ADAM_STEP = 10

SQRT_HALF = 0.7071067811865476
INV_SQRT_2PI = 0.3989422804014327
MESH_AXES = ("x", "y", "c")
MESH = pl.DeviceIdType.MESH
MIB = 2 ** 20


def _vmem():
    return pl.BlockSpec(memory_space=pltpu.VMEM)


def _smem():
    return pl.BlockSpec(memory_space=pltpu.SMEM)


def _hbm():
    return pl.BlockSpec(memory_space=pl.ANY)


def _hbm_shape(shape, dtype):
    return pltpu.HBM(shape, dtype)


def _in_hbm(a):
    return pltpu.with_memory_space_constraint(a, pltpu.HBM)


def _params(vmem_mib=48):
    return pltpu.CompilerParams(dimension_semantics=("arbitrary",), vmem_limit_bytes=vmem_mib * MIB)


def _tile(n, cap):
    if n <= cap:
        return n
    for t in range(cap - cap % 16, 0, -16):
        if n % t == 0:
            return t
    raise ValueError((n, cap))


def _rows(tm, width):
    return pl.BlockSpec((tm, width), lambda i: (i, 0))


def _const2(shape):
    return pl.BlockSpec(shape, lambda i: (0,) * len(shape))


def _ln(x, g, b):
    mu = jnp.mean(x, axis=-1, keepdims=True)
    xc = x - mu
    var = jnp.mean(xc * xc, axis=-1, keepdims=True)
    rstd = lax.rsqrt(var + LN_EPS)
    xhat = xc * rstd
    return xhat * g + b, xhat, rstd


def _ln_bwd(dy, xhat, rstd, g):
    gdy = dy * g
    m1 = jnp.mean(gdy, axis=-1, keepdims=True)
    m2 = jnp.mean(gdy * xhat, axis=-1, keepdims=True)
    return rstd * (gdy - m1 - xhat * m2)


def _colsum(a):
    return jnp.sum(a, axis=0, keepdims=True)


def _gelu_and_grad(x):
    cdf = 0.5 * (1.0 + lax.erf(x * SQRT_HALF))
    return x * cdf, cdf + x * jnp.exp(-0.5 * x * x) * INV_SQRT_2PI


def _dot(a, b):
    return jnp.dot(a, b, preferred_element_type=F32)


def _dot_nt(a, b):
    return lax.dot_general(a, b, (((1,), (1,)), ((), ())), preferred_element_type=F32)


def _dot_tn(a, b):
    return lax.dot_general(a, b, (((0,), (0,)), ((), ())), preferred_element_type=F32)


def _rope(t, tc, t1, t2):
    n = t.shape[1]
    rep = n // 128
    if rep > 1:
        tc, t1, t2 = (jnp.tile(a, (1, rep)) for a in (tc, t1, t2))
    return t * tc + pltpu.roll(t, n - 8, 1) * t1 + pltpu.roll(t, 8, 1) * t2


def _rope_bwd(d, tc, t1, t2):
    n = d.shape[1]
    rep = n // 128
    if rep > 1:
        tc, t1, t2 = (jnp.tile(a, (1, rep)) for a in (tc, t1, t2))
    return d * tc + pltpu.roll(d * t1, 8, 1) + pltpu.roll(d * t2, n - 8, 1)


def _causal_w(w_ref, h):
    t = lax.broadcasted_iota(jnp.int32, (BLK, BLK), 0)
    s = lax.broadcasted_iota(jnp.int32, (BLK, BLK), 1)
    return jnp.where(s <= t, w_ref[h], 0.0)


def _lane_put(vals, width):
    rows = vals[0].shape[0]
    lane = lax.broadcasted_iota(jnp.int32, (rows, width), 1)
    out = jnp.zeros((rows, width), F32)
    for k, v in enumerate(vals):
        out = out + jnp.where(lane == k, v, 0.0)
    return out


def _rope_consts():
    lane = jnp.arange(128) % HEAD_DIM
    rot = lane < ROT_DIM
    pair = (2 * (lane % (ROT_DIM // 2))).astype(F32)
    freq = jnp.where(rot, ROPE_THETA ** (-pair / ROT_DIM), 0.0)
    rows = [freq, rot.astype(F32), 1.0 - rot.astype(F32), (lane < ROT_DIM // 2).astype(F32),
            jnp.logical_and(lane >= ROT_DIM // 2, rot).astype(F32)]
    rows += [jnp.zeros((128,), F32)] * 3
    return jnp.stack(rows).astype(F32)


def _ln_inproj(x, pos_col, g0, b0, w_in, b_in, plan):
    s_len = x.shape[0]
    tm = _tile(s_len, 512)
    m, n = len(plan.operands()), plan.n

    def body(x_ref, pos_ref, g_ref, b_ref, w_ref, bi_ref, rc_ref, *rest):
        q_ref, k_ref, v_ref, su_ref, sv_ref, tc_ref, t1_ref, t2_ref = rest[m:m + 8]
        gather = plan.bind(rest[:m], rest[m + 8:m + 8 + n], rest[m + 8 + n:])
        i = pl.program_id(0)

        @pl.when(i == 0)
        def _():
            gather.start()

        h0, _, _ = _ln(x_ref[...], g_ref[...], b_ref[...])
        proj = _dot_nt(h0.astype(_MXU), w_ref[...]) + bi_ref[...]
        ang = pos_ref[...].astype(F32) * rc_ref[0:1, :]
        cs = jnp.cos(ang)
        sn = jnp.sin(ang)
        tc = cs * rc_ref[1:2, :] + rc_ref[2:3, :]
        t1 = -sn * rc_ref[3:4, :]
        t2 = sn * rc_ref[4:5, :]
        tc_ref[...] = tc
        t1_ref[...] = t1
        t2_ref[...] = t2
        q = _rope(proj[:, 0:ATTN_W], tc, t1, t2) * (HEAD_DIM ** -0.5)
        q_ref[...] = q.astype(_MXU)
        k_ref[...] = _rope(proj[:, ATTN_W:ATTN_W + KV_W], tc, t1, t2).astype(_MXU)
        v_ref[...] = proj[:, ATTN_W + KV_W:ATTN_W + 2 * KV_W].astype(_MXU)
        su_ref[...] = proj[:, ATTN_W + 2 * KV_W:ATTN_W + 2 * KV_W + SGU_W]
        sv_ref[...] = proj[:, ATTN_W + 2 * KV_W + SGU_W:IN_W]

        @pl.when(i == pl.num_programs(0) - 1)
        def _():
            gather.finish()

    sd = _hbm_shape
    return pl.pallas_call(
        body, name="ln_inproj", grid=(s_len // tm,),
        in_specs=[_rows(tm, D_MODEL), _rows(tm, 1), _const2((1, D_MODEL)), _const2((1, D_MODEL)), _vmem(),
                  _const2((1, IN_W)), _const2((8, 128))] + plan.in_specs(),
        out_specs=[_rows(tm, ATTN_W), _rows(tm, KV_W), _rows(tm, KV_W), _rows(tm, SGU_W), _rows(tm, SGU_W),
                   _rows(tm, 128), _rows(tm, 128), _rows(tm, 128)] + plan.out_specs(),
        out_shape=[sd((s_len, ATTN_W), _MXU), sd((s_len, KV_W), _MXU), sd((s_len, KV_W), _MXU),
                   sd((s_len, SGU_W), F32), sd((s_len, SGU_W), F32),
                   sd((s_len, 128), F32), sd((s_len, 128), F32), sd((s_len, 128), F32)] + plan.out_shapes(),
        scratch_shapes=plan.scratch(),
        compiler_params=_params(56),
    )(x, pos_col, g0, b0, w_in, b_in, _rope_consts(), *plan.operands())


def _band_mask_t(first_block):
    kj = lax.broadcasted_iota(jnp.int32, (2 * BLK, BLK), 0)
    qi = lax.broadcasted_iota(jnp.int32, (2 * BLK, BLK), 1)
    shut = jnp.where(first_block, 2 * BLK, 0)
    prev_ok = jnp.logical_and(kj < BLK, kj > qi + shut)
    cur_ok = jnp.logical_and(kj >= BLK, (kj - BLK) <= qi)
    return jnp.logical_or(prev_ok, cur_ok)


def _attn_probs_t(kh, qh, sink, allowed_t):
    s = jnp.where(allowed_t, _dot_nt(kh, qh), -1e30)
    m = jnp.maximum(jnp.max(s, axis=0, keepdims=True), sink)
    p = jnp.exp(s - m)
    ps = jnp.exp(sink - m)
    inv = 1.0 / (jnp.sum(p, axis=0, keepdims=True) + ps)
    return p * inv, ps * inv


def _sgu_fwd(su, sv, lg, lb, w_ref, bt_ref):
    u, du_dsu = _gelu_and_grad(su)
    gv, dgv_dsv = _gelu_and_grad(sv)
    vv, vhat, rstd = _ln(gv, lg, lb)
    vvb = vv.astype(_MXU)
    wcs, mixed = [], []
    for h in range(N_GRP):
        wc = _causal_w(w_ref, h).astype(_MXU)
        wcs.append(wc)
        mixed.append(_dot(wc, vvb[:, h * GRP_DIM:(h + 1) * GRP_DIM]) + bt_ref[:, h:h + 1])
    return u, jnp.concatenate(mixed, axis=1), du_dsu, dgv_dsv, vhat, rstd, vvb, wcs


def _prev_map(i):
    return (jnp.maximum(i - 1, 0), 0)


def _mixer_fwd(q, k, v, su, sv, sinks, sg, sb, sgu_w, sgu_bt, plan):
    s_len = q.shape[0]
    nb = s_len // BLK
    m, n = len(plan.operands()), plan.n

    def body(q_ref, kc_ref, kp_ref, vc_ref, vp_ref, su_ref, sv_ref, sink_ref, lg_ref, lb_ref, w_ref, bt_ref, *rest):
        mc_ref = rest[m]
        gather = plan.bind(rest[:m], rest[m + 1:m + 1 + n], rest[m + 1 + n:])
        i = pl.program_id(0)

        @pl.when(i == 0)
        def _():
            gather.start()

        @pl.when(i == nb - 1)
        def _():
            gather.finish()

        allowed_t = _band_mask_t(i == 0)
        kb = jnp.concatenate([kp_ref[...], kc_ref[...]], axis=0)
        vb = jnp.concatenate([vp_ref[...], vc_ref[...]], axis=0)
        qv = q_ref[...]
        outs = []
        allowed_g = jnp.tile(allowed_t, (1, Q_PER_KV))
        for g in range(N_KV):
            heads = range(g * Q_PER_KV, (g + 1) * Q_PER_KV)
            kh = kb[:, g * HEAD_DIM:(g + 1) * HEAD_DIM]
            vh = vb[:, g * HEAD_DIM:(g + 1) * HEAD_DIM]
            q_g = jnp.concatenate([qv[:, h * HEAD_DIM:(h + 1) * HEAD_DIM] for h in heads], axis=0)
            sink_g = jnp.concatenate([jnp.full((1, BLK), sink_ref[h], F32) for h in heads], axis=1)
            probs_t, _ = _attn_probs_t(kh, q_g, sink_g, allowed_g)
            o_g = _dot_tn(probs_t.astype(_MXU), vh)
            outs += [o_g[hh * BLK:(hh + 1) * BLK, :] for hh in range(Q_PER_KV)]
        u, mixed = _sgu_fwd(su_ref[...], sv_ref[...], lg_ref[...], lb_ref[...], w_ref, bt_ref)[:2]
        mc_ref[...] = jnp.concatenate(outs + [u * mixed], axis=1).astype(_MXU)

    cur = lambda w: pl.BlockSpec((BLK, w), lambda i: (i, 0))
    prev = lambda w: pl.BlockSpec((BLK, w), _prev_map)
    return pl.pallas_call(
        body, name="mixer_fwd", grid=(nb,),
        in_specs=[cur(ATTN_W), cur(KV_W), prev(KV_W), cur(KV_W), prev(KV_W), cur(SGU_W), cur(SGU_W), _smem(),
                  _const2((1, SGU_W)), _const2((1, SGU_W)), _const2((N_GRP, BLK, BLK)), _const2((BLK, N_GRP))]
        + plan.in_specs(),
        out_specs=[cur(D_MODEL)] + plan.out_specs(),
        out_shape=[_hbm_shape((s_len, D_MODEL), _MXU)] + plan.out_shapes(),
        scratch_shapes=plan.scratch(),
        compiler_params=_params(56),
    )(q, k, k, v, v, su, sv, sinks, sg, sb, sgu_w, sgu_bt, *plan.operands())


def _outproj(mc, w_out, b_out, x, g0, b0, plan):
    s_len = x.shape[0]
    tm = _tile(s_len, 512)
    m, n = len(plan.operands()), plan.n

    def body(mc_ref, w_ref, bo_ref, x_ref, g_ref, b_ref, *rest):
        r1_ref = rest[m]
        gather = plan.bind(rest[:m], rest[m + 1:m + 1 + n], rest[m + 1 + n:])
        i = pl.program_id(0)

        @pl.when(i == 0)
        def _():
            gather.start()

        h0, _, _ = _ln(x_ref[...], g_ref[...], b_ref[...])
        r1_ref[...] = ALPHA * h0 + (_dot(mc_ref[...], w_ref[...]) + bo_ref[...])

        @pl.when(i == pl.num_programs(0) - 1)
        def _():
            gather.finish()

    return pl.pallas_call(
        body, name="outproj", grid=(s_len // tm,),
        in_specs=[_rows(tm, D_MODEL), _vmem(), _const2((1, D_MODEL)), _rows(tm, D_MODEL),
                  _const2((1, D_MODEL)), _const2((1, D_MODEL))] + plan.in_specs(),
        out_specs=[_rows(tm, D_MODEL)] + plan.out_specs(),
        out_shape=[_hbm_shape((s_len, D_MODEL), F32)] + plan.out_shapes(),
        scratch_shapes=plan.scratch(),
        compiler_params=_params(40),
    )(mc, w_out, b_out, x, g0, b0, *plan.operands())


def _ffn_spec(tm):
    return pl.BlockSpec((N_CHIP, tm, FF_SH), lambda i: (0, i, 0))


def _ffn_up(r1, g1, b1, wg, wu, plan):
    s_len = r1.shape[0]
    tm = _tile(s_len, 512)
    m, n = len(plan.operands()), plan.n

    def body(r1_ref, g_ref, b_ref, wg_ref, wu_ref, *rest):
        a_ref, p_ref, q_ref = rest[m:m + 3]
        gather = plan.bind(rest[:m], rest[m + 3:m + 3 + n], rest[m + 3 + n:])
        i = pl.program_id(0)

        @pl.when(i == 0)
        def _():
            gather.start()

        h1, _, _ = _ln(r1_ref[...], g_ref[...], b_ref[...])
        h1b = h1.astype(_MXU)
        for j in range(N_CHIP):
            g = _dot_nt(h1b, wg_ref[j])
            u = _dot_nt(h1b, wu_ref[j])
            silu, sg = _silu_parts(g)
            a_ref[j] = (silu * u).astype(_MXU)
            p_ref[j] = silu.astype(_ACT)
            q_ref[j] = (u * (sg * (1.0 + g * (1.0 - sg)))).astype(_ACT)

        @pl.when(i == pl.num_programs(0) - 1)
        def _():
            gather.finish()

    sd = _hbm_shape((N_CHIP, s_len, FF_SH), _ACT)
    return pl.pallas_call(
        body, name="ffn_up", grid=(s_len // tm,),
        in_specs=[_rows(tm, D_MODEL), _const2((1, D_MODEL)), _const2((1, D_MODEL)), _vmem(), _vmem()] + plan.in_specs(),
        out_specs=[_ffn_spec(tm)] * 3 + plan.out_specs(),
        out_shape=[_hbm_shape((N_CHIP, s_len, FF_SH), _MXU), sd, sd] + plan.out_shapes(),
        scratch_shapes=plan.scratch(),
        compiler_params=_params(56),
    )(r1, g1, b1, wg, wu, *plan.operands())


def _silu_parts(g):
    sg = 1.0 / (1.0 + jnp.exp(-g))
    return g * sg, sg


def _ffn_down_loss(act, wd, r1, g1, b1, g2, b2, target):
    s_len = r1.shape[0]
    tm = _tile(s_len, 512)

    parts = 2 if tm % 32 == 0 else 1
    sub = tm // parts

    def body(a_ref, wd_ref, r1_ref, g1_ref, b1_ref, g2_ref, b2_ref, t_ref, dr2_ref, loss_ref, dg2_ref, db2_ref):
        i = pl.program_id(0)

        @pl.when(i == 0)
        def _():
            loss_ref[...] = jnp.zeros_like(loss_ref)
            dg2_ref[...] = jnp.zeros_like(dg2_ref)
            db2_ref[...] = jnp.zeros_like(db2_ref)

        for part in range(parts):
            rows = slice(part * sub, (part + 1) * sub)
            f = jnp.zeros((sub, D_MODEL), F32)
            for j in range(N_CHIP):
                f = f + _dot(a_ref[j, rows, :], wd_ref[j])
            h1, _, _ = _ln(r1_ref[rows, :], g1_ref[...], b1_ref[...])
            h2, r2hat, rstd2 = _ln(ALPHA * h1 + f, g2_ref[...], b2_ref[...])
            diff = h2 - t_ref[rows, :]
            dh2 = diff * (1.0 / D_MODEL)
            loss_ref[...] += _colsum(diff * diff)
            dg2_ref[...] += _colsum(dh2 * r2hat)
            db2_ref[...] += _colsum(dh2)
            dr2_ref[rows, :] = _ln_bwd(dh2, r2hat, rstd2, g2_ref[...])

    vec = _hbm_shape((1, D_MODEL), F32)
    c = _const2((1, D_MODEL))
    return pl.pallas_call(
        body, name="ffn_down_loss", grid=(s_len // tm,),
        in_specs=[_ffn_spec(tm), _vmem(), _rows(tm, D_MODEL), c, c, c, c, _rows(tm, D_MODEL)],
        out_specs=[_rows(tm, D_MODEL), c, c, c],
        out_shape=[_hbm_shape((s_len, D_MODEL), F32), vec, vec, vec],
        compiler_params=_params(48),
    )(act, wd, r1, g1, b1, g2, b2, target)


def _ffn_bwd_a(dr2, act, p_act, q_act, wd):
    s_len = dr2.shape[0]
    tm = _tile(s_len, 512)

    def body(dr2_ref, a_ref, p_ref, q_ref, wd_ref, dg_ref, du_ref, wire_ref, own_ref,
             dwd_ref, land_ref, send_sem, recv_sem):
        i = pl.program_id(0)

        @pl.when(i == 0)
        def _():
            dwd_ref[...] = jnp.zeros_like(dwd_ref)

        dfb = dr2_ref[...].astype(_MXU)
        for j in range(N_CHIP):
            da = _dot_nt(dfb, wd_ref[j])
            dg_ref[j] = (da * q_ref[j].astype(F32)).astype(_MXU)
            du_ref[j] = (da * p_ref[j].astype(F32)).astype(_MXU)
            dwd_ref[j * FF_SH:(j + 1) * FF_SH, :] += _dot_tn(a_ref[j], dfb)

        @pl.when(i == pl.num_programs(0) - 1)
        def _():
            _pair_reduce(dwd_ref, wire_ref, own_ref, land_ref, send_sem, recv_sem)

    sd = _hbm_shape((N_CHIP, s_len, FF_SH), _MXU)
    half = (N_CHIP, FF_SH // 2, D_MODEL)
    return pl.pallas_call(
        body, name="ffn_bwd_a", grid=(s_len // tm,),
        in_specs=[_rows(tm, D_MODEL), _ffn_spec(tm), _ffn_spec(tm), _ffn_spec(tm), _vmem()],
        out_specs=[_ffn_spec(tm), _ffn_spec(tm), _vmem(), _vmem()],
        out_shape=[sd, sd] + _pair_out_shapes(half),
        scratch_shapes=_pair_scratch((D_FF, D_MODEL), half),
        compiler_params=_params(61),
    )(dr2, act, p_act, q_act, wd)


def _ffn_bwd_g(dr2, dg, r1, g1, b1, wg, prev_wire):
    s_len = dr2.shape[0]
    tm = _tile(s_len, 512)

    def body(dr2_ref, dg_ref, r1_ref, g1_ref, b1_ref, wg_ref, pw_ref, dh1_ref, wire_ref, own_ref, pl_ref,
             dwg_ref, land_ref, send_sem, recv_sem, xl_ref, x_send, x_recv, x_flush):
        i = pl.program_id(0)
        exchange = _ChipExchange(pw_ref, xl_ref, x_send, x_recv)

        @pl.when(i == 0)
        def _():
            exchange.start()
            dwg_ref[...] = jnp.zeros_like(dwg_ref)

        h1, _, _ = _ln(r1_ref[...], g1_ref[...], b1_ref[...])
        h1b = h1.astype(_MXU)
        dh1 = ALPHA * dr2_ref[...]
        for j in range(N_CHIP):
            dgj = dg_ref[j]
            dh1 = dh1 + _dot(dgj, wg_ref[j])
            dwg_ref[j * FF_SH:(j + 1) * FF_SH, :] += _dot_tn(dgj, h1b)
        dh1_ref[...] = dh1

        @pl.when(i == pl.num_programs(0) - 1)
        def _():
            _pair_reduce(dwg_ref, wire_ref, own_ref, land_ref, send_sem, recv_sem)
            exchange.finish_to(pl_ref, x_flush)

    c = _const2((1, D_MODEL))
    half = (N_CHIP, FF_SH // 2, D_MODEL)
    return pl.pallas_call(
        body, name="ffn_bwd_g", grid=(s_len // tm,),
        in_specs=[_rows(tm, D_MODEL), _ffn_spec(tm), _rows(tm, D_MODEL), c, c, _vmem(), _vmem()],
        out_specs=[_rows(tm, D_MODEL), _vmem(), _vmem(), _hbm()],
        out_shape=[_hbm_shape((s_len, D_MODEL), F32)] + _pair_out_shapes(half) + [_ChipExchange.land_shape(prev_wire)],
        scratch_shapes=_pair_scratch((D_FF, D_MODEL), half) + _ChipExchange.scratch(prev_wire),
        compiler_params=_params(58),
    )(dr2, dg, r1, g1, b1, wg, prev_wire)


def _ffn_bwd_u(dh1a, du, r1, g1, b1, wu, prev_wire):
    s_len = dh1a.shape[0]
    tm = _tile(s_len, 512)

    def body(dh1_ref, du_ref, r1_ref, g1_ref, b1_ref, wu_ref, pw_ref,
             dr1_ref, wire_ref, own_ref, dg1_ref, db1_ref, pl_ref,
             dwu_ref, land_ref, send_sem, recv_sem, xl_ref, x_send, x_recv, x_flush):
        i = pl.program_id(0)
        exchange = _ChipExchange(pw_ref, xl_ref, x_send, x_recv)

        @pl.when(i == 0)
        def _():
            exchange.start()
            dwu_ref[...] = jnp.zeros_like(dwu_ref)
            dg1_ref[...] = jnp.zeros_like(dg1_ref)
            db1_ref[...] = jnp.zeros_like(db1_ref)

        h1, r1hat, rstd1 = _ln(r1_ref[...], g1_ref[...], b1_ref[...])
        h1b = h1.astype(_MXU)
        dh1 = dh1_ref[...]
        for j in range(N_CHIP):
            duj = du_ref[j]
            dh1 = dh1 + _dot(duj, wu_ref[j])
            dwu_ref[j * FF_SH:(j + 1) * FF_SH, :] += _dot_tn(duj, h1b)
        dg1_ref[...] += _colsum(dh1 * r1hat)
        db1_ref[...] += _colsum(dh1)
        dr1_ref[...] = _ln_bwd(dh1, r1hat, rstd1, g1_ref[...])

        @pl.when(i == pl.num_programs(0) - 1)
        def _():
            _pair_reduce(dwu_ref, wire_ref, own_ref, land_ref, send_sem, recv_sem)
            exchange.finish_to(pl_ref, x_flush)

    vec = _hbm_shape((1, D_MODEL), F32)
    c = _const2((1, D_MODEL))
    half = (N_CHIP, FF_SH // 2, D_MODEL)
    return pl.pallas_call(
        body, name="ffn_bwd_u", grid=(s_len // tm,),
        in_specs=[_rows(tm, D_MODEL), _ffn_spec(tm), _rows(tm, D_MODEL), c, c, _vmem(), _vmem()],
        out_specs=[_rows(tm, D_MODEL), _vmem(), _vmem(), c, c, _hbm()],
        out_shape=[_hbm_shape((s_len, D_MODEL), F32)] + _pair_out_shapes(half)
        + [vec, vec, _ChipExchange.land_shape(prev_wire)],
        scratch_shapes=_pair_scratch((D_FF, D_MODEL), half) + _ChipExchange.scratch(prev_wire),
        compiler_params=_params(58),
    )(dh1a, du, r1, g1, b1, wu, prev_wire)


def _outproj_bwd(dr1, mc, w_out, prev_wire):
    s_len = dr1.shape[0]
    tm = _tile(s_len, 512)

    def body(dr1_ref, mc_ref, w_ref, pw_ref, dmc_ref, wire_ref, own_ref, db_ref, pl_ref,
             dw_ref, land_ref, send_sem, recv_sem, xl_ref, x_send, x_recv, x_flush):
        i = pl.program_id(0)
        exchange = _ChipExchange(pw_ref, xl_ref, x_send, x_recv)

        @pl.when(i == 0)
        def _():
            exchange.start()
            dw_ref[...] = jnp.zeros_like(dw_ref)
            db_ref[...] = jnp.zeros_like(db_ref)

        d = dr1_ref[...]
        db_ref[...] += _colsum(d)
        db16 = d.astype(_MXU)
        dmc_ref[...] = _dot_nt(db16, w_ref[...])
        dw_ref[...] += _dot_tn(mc_ref[...], db16)

        @pl.when(i == pl.num_programs(0) - 1)
        def _():
            _pair_reduce(dw_ref, wire_ref, own_ref, land_ref, send_sem, recv_sem)
            exchange.finish_to(pl_ref, x_flush)

    half = (N_CHIP, OUT_SH // 2, D_MODEL)
    return pl.pallas_call(
        body, name="outproj_bwd", grid=(s_len // tm,),
        in_specs=[_rows(tm, D_MODEL), _rows(tm, D_MODEL), _vmem(), _vmem()],
        out_specs=[_rows(tm, D_MODEL), _vmem(), _vmem(), _const2((1, D_MODEL)), _hbm()],
        out_shape=[_hbm_shape((s_len, D_MODEL), F32)] + _pair_out_shapes(half)
        + [_hbm_shape((1, D_MODEL), F32), _ChipExchange.land_shape(prev_wire)],
        scratch_shapes=_pair_scratch((D_MODEL, D_MODEL), half) + _ChipExchange.scratch(prev_wire),
        compiler_params=_params(48),
    )(dr1, mc, w_out, prev_wire)


def _mixer_bwd(q, k, v, su, sv, dmc, tc, t1, t2, sinks, sg, sb, sgu_w, sgu_bt, prev_wire):
    s_len = q.shape[0]
    nb = s_len // BLK

    def body(q_ref, kc_ref, kp_ref, vc_ref, vp_ref, su_ref, sv_ref, dmc_ref,
             tc_ref, t1_ref, t2_ref, tcp_ref, t1p_ref, t2p_ref,
             sink_ref, lg_ref, lb_ref, w_ref, bt_ref, pw_ref,
             dq_ref, dkv_ref, dsuv_ref, dbq_ref, dbkv_ref, dbsuv_ref,
             dsink_ref, dlg_ref, dlb_ref, dw_ref, dbt_ref, pl_ref, carry_ref, xl_ref, x_send, x_recv, x_flush):
        i = pl.program_id(0)
        exchange = _ChipExchange(pw_ref, xl_ref, x_send, x_recv)

        @pl.when(i == 0)
        def _():
            exchange.start()

        @pl.when(i == 0)
        def _():
            for r in (dbq_ref, dbkv_ref, dbsuv_ref, dsink_ref, dlg_ref, dlb_ref, dw_ref, dbt_ref):
                r[...] = jnp.zeros_like(r)

        def emit_kv(fin):
            dk = _rope_bwd(fin[:, 0:KV_W], tcp_ref[...], t1p_ref[...], t2p_ref[...])
            out = jnp.concatenate([dk, fin[:, KV_W:2 * KV_W]], axis=1)
            dkv_ref[...] = out.astype(_MXU)
            dbkv_ref[...] += _colsum(out)

        @pl.when(i < nb)
        def _():
            allowed_t = _band_mask_t(i == 0)
            kb = jnp.concatenate([kp_ref[...], kc_ref[...]], axis=0)
            vb = jnp.concatenate([vp_ref[...], vc_ref[...]], axis=0)
            qv = q_ref[...]
            dmc = dmc_ref[...]
            dqs, dks, dvs, dsinks = [], [], [], []
            allowed_g = jnp.tile(allowed_t, (1, Q_PER_KV))
            for g in range(N_KV):
                heads = range(g * Q_PER_KV, (g + 1) * Q_PER_KV)
                kh = kb[:, g * HEAD_DIM:(g + 1) * HEAD_DIM]
                vh = vb[:, g * HEAD_DIM:(g + 1) * HEAD_DIM]
                q_g = jnp.concatenate([qv[:, h * HEAD_DIM:(h + 1) * HEAD_DIM] for h in heads], axis=0)
                do_g = jnp.concatenate([dmc[:, h * HEAD_DIM:(h + 1) * HEAD_DIM] for h in heads], axis=0).astype(_MXU)
                sink_g = jnp.concatenate([jnp.full((1, BLK), sink_ref[h], F32) for h in heads], axis=1)
                probs_t, psink = _attn_probs_t(kh, q_g, sink_g, allowed_g)
                dvs.append(_dot(probs_t.astype(_MXU), do_g))
                dp_t = _dot_nt(vh, do_g)
                rd = jnp.sum(probs_t * dp_t, axis=0, keepdims=True)
                ds_t = (probs_t * (dp_t - rd)).astype(_MXU)
                ps_rd = psink * rd
                for hh in range(Q_PER_KV):
                    dsinks.append(-jnp.sum(ps_rd[:, hh * BLK:(hh + 1) * BLK], axis=1, keepdims=True))
                dq_g = _dot_tn(ds_t, kh)
                dqs += [dq_g[hh * BLK:(hh + 1) * BLK, :] for hh in range(Q_PER_KV)]
                dks.append(_dot(ds_t, q_g))
            dq = _rope_bwd(jnp.concatenate(dqs, axis=1) * (HEAD_DIM ** -0.5), tc_ref[...], t1_ref[...], t2_ref[...])
            dq_ref[...] = dq.astype(_MXU)
            dbq_ref[...] += _colsum(dq)
            dsink_ref[...] += _lane_put(dsinks, 128)
            contrib = jnp.concatenate(dks + dvs, axis=1)

            @pl.when(i > 0)
            def _():
                emit_kv(carry_ref[...] + contrib[0:BLK, :])

            carry_ref[...] = contrib[BLK:2 * BLK, :]

            su = su_ref[...]
            sv = sv_ref[...]
            lg = lg_ref[...]
            u, mixed, du_dsu, dgv_dsv, vhat, rstd, vvb, wcs = _sgu_fwd(su, sv, lg, lb_ref[...], w_ref, bt_ref)
            dsgu = dmc[:, ATTN_W:D_MODEL]
            dsu = dsgu * mixed * du_dsu
            dmixed = dsgu * u
            tri_t = lax.broadcasted_iota(jnp.int32, (BLK, BLK), 0)
            tri_s = lax.broadcasted_iota(jnp.int32, (BLK, BLK), 1)
            dvv, dbs = [], []
            for h in range(N_GRP):
                dm = dmixed[:, h * GRP_DIM:(h + 1) * GRP_DIM]
                dmb = dm.astype(_MXU)
                dbs.append(jnp.sum(dm, axis=1, keepdims=True))
                dw_ref[h] += jnp.where(tri_s <= tri_t, _dot_nt(dmb, vvb[:, h * GRP_DIM:(h + 1) * GRP_DIM]), 0.0)
                dvv.append(_dot_tn(wcs[h], dmb))
            dvv = jnp.concatenate(dvv, axis=1)
            dbt_ref[...] += _lane_put(dbs, 128)
            dlg_ref[...] += _colsum(dvv * vhat)
            dlb_ref[...] += _colsum(dvv)
            dsv = _ln_bwd(dvv, vhat, rstd, lg) * dgv_dsv
            dsuv = jnp.concatenate([dsu, dsv], axis=1)
            dsuv_ref[...] = dsuv.astype(_MXU)
            dbsuv_ref[...] += _colsum(dsuv)

        @pl.when(i == nb)
        def _():
            emit_kv(carry_ref[...])
            exchange.finish_to(pl_ref, x_flush)

    last = nb - 1
    cur = lambda w: pl.BlockSpec((BLK, w), lambda i: (jnp.minimum(i, last), 0))
    prev = lambda w: pl.BlockSpec((BLK, w), lambda i: (jnp.clip(i - 1, 0, last), 0))
    sd = _hbm_shape
    return pl.pallas_call(
        body, name="mixer_bwd", grid=(nb + 1,),
        in_specs=[cur(ATTN_W), cur(KV_W), prev(KV_W), cur(KV_W), prev(KV_W), cur(SGU_W), cur(SGU_W), cur(D_MODEL),
                  cur(128), cur(128), cur(128), prev(128), prev(128), prev(128),
                  _smem(), _const2((1, SGU_W)), _const2((1, SGU_W)), _const2((N_GRP, BLK, BLK)), _const2((BLK, N_GRP)),
                  _vmem()],
        out_specs=[cur(ATTN_W), prev(2 * KV_W), cur(2 * SGU_W),
                   _const2((1, ATTN_W)), _const2((1, 2 * KV_W)), _const2((1, 2 * SGU_W)),
                   _const2((1, 128)), _const2((1, SGU_W)), _const2((1, SGU_W)),
                   _const2((N_GRP, BLK, BLK)), _const2((BLK, 128)), _hbm()],
        out_shape=[sd((s_len, ATTN_W), _MXU), sd((s_len, 2 * KV_W), _MXU), sd((s_len, 2 * SGU_W), _MXU),
                   sd((1, ATTN_W), F32), sd((1, 2 * KV_W), F32), sd((1, 2 * SGU_W), F32),
                   sd((1, 128), F32), sd((1, SGU_W), F32), sd((1, SGU_W), F32),
                   sd((N_GRP, BLK, BLK), F32), sd((BLK, 128), F32), _ChipExchange.land_shape(prev_wire)],
        scratch_shapes=[pltpu.VMEM((BLK, 2 * KV_W), F32)] + _ChipExchange.scratch(prev_wire),
        compiler_params=_params(32),
    )(q, k, k, v, v, su, sv, dmc, tc, t1, t2, tc, t1, t2, sinks, sg, sb, sgu_w, sgu_bt, prev_wire)


def _inproj_bwd(dq, dkv, dsuv, dr1, x, g0, b0, w_in):
    s_len = x.shape[0]
    tm = _tile(s_len, 512)
    cuts = ((0, ATTN_W), (ATTN_W, ATTN_W + 2 * KV_W), (ATTN_W + 2 * KV_W, IN_W))

    def body(dq_ref, dkv_ref, dsuv_ref, dr1_ref, x_ref, g_ref, b_ref, w_ref, dx_ref, dw_ref, dg_ref, db_ref):
        i = pl.program_id(0)

        @pl.when(i == 0)
        def _():
            dw_ref[...] = jnp.zeros_like(dw_ref)
            dg_ref[...] = jnp.zeros_like(dg_ref)
            db_ref[...] = jnp.zeros_like(db_ref)

        h0, xhat, rstd = _ln(x_ref[...], g_ref[...], b_ref[...])
        h0b = h0.astype(_MXU)
        dh0 = ALPHA * dr1_ref[...]
        for (lo, hi), d_ref in zip(cuts, (dq_ref, dkv_ref, dsuv_ref)):
            d = d_ref[...]
            dh0 = dh0 + _dot(d, w_ref[lo:hi, :])
            dw_ref[lo:hi, :] += _dot_tn(d, h0b)
        dg_ref[...] += _colsum(dh0 * xhat)
        db_ref[...] += _colsum(dh0)
        dx_ref[...] = _ln_bwd(dh0, xhat, rstd, g_ref[...])

    vec = _hbm_shape((1, D_MODEL), F32)
    c = _const2((1, D_MODEL))
    return pl.pallas_call(
        body, name="inproj_bwd", grid=(s_len // tm,),
        in_specs=[_rows(tm, ATTN_W), _rows(tm, 2 * KV_W), _rows(tm, 2 * SGU_W), _rows(tm, D_MODEL), _rows(tm, D_MODEL),
                  c, c, _vmem()],
        out_specs=[_rows(tm, D_MODEL), _vmem(), c, c],
        out_shape=[_hbm_shape((s_len, D_MODEL), F32), jax.ShapeDtypeStruct((IN_W, D_MODEL), F32), vec, vec],
        compiler_params=_params(48),
    )(dq, dkv, dsuv, dr1, x, g0, b0, w_in)


def _place():
    x, y, c = (lax.axis_index(a) for a in MESH_AXES)
    chips = [(1 - x, y), (x, 1 - y), (1 - x, 1 - y)]
    return x, y, c, chips


class _Gather:
    def __init__(self, ins, outs, send_sems, recv_sems, spans=None):
        self.ins, self.outs, self.send_sems, self.recv_sems = ins, outs, send_sems, recv_sems
        self.n = len(ins)
        self.spans = spans or [(0, r.shape[0]) for r in ins]
        self.halves = [(hi - lo) // 2 for lo, hi in self.spans]

    def _copy(self, k, t, slot, half, to):
        rows = pl.ds(pl.multiple_of(self.spans[t][0] + half * self.halves[t], 16), self.halves[t])
        piece = self.outs[t].at[slot, rows, :]
        return pltpu.make_async_remote_copy(src_ref=piece, dst_ref=piece, send_sem=self.send_sems.at[k],
                                            recv_sem=self.recv_sems.at[k], device_id=to, device_id_type=MESH)

    def _chip_copy(self, t, d, slot):
        x, y, c, chips = _place()
        return self._copy(3 * t + d, t, slot, c, (chips[d][0], chips[d][1], c))

    def _pass_copy(self, t, d, half):
        x, y, c, chips = _place()
        return self._copy(3 * self.n + 3 * t + d, t, 2 * chips[d][0] + chips[d][1], half, (x, y, 1 - c))

    def start(self):
        x, y, c, chips = _place()
        me = 2 * x + y
        for t in range(self.n):
            lo, hi = self.spans[t]
            self.outs[t][me, lo:hi, :] = self.ins[t][lo:hi, :].astype(_WIRE)
        for t in range(self.n):
            for d in range(3):
                self._chip_copy(t, d, me).start()

    def finish(self):
        x, y, c, chips = _place()
        me = 2 * x + y
        for t in range(self.n):
            for d in range(3):
                self._chip_copy(t, d, 2 * chips[d][0] + chips[d][1]).wait_recv()
                self._pass_copy(t, d, c).start()
        for t in range(self.n):
            for d in range(3):
                self._pass_copy(t, d, 1 - c).wait_recv()
        for t in range(self.n):
            for d in range(3):
                self._chip_copy(t, d, me).wait_send()
                self._pass_copy(t, d, c).wait_send()

    @staticmethod
    def out_shapes(shards, make=jax.ShapeDtypeStruct):
        return [make((N_CHIP,) + s.shape, _WIRE) for s in shards]

    @staticmethod
    def sems(n):
        return [pltpu.SemaphoreType.DMA((6 * n,)), pltpu.SemaphoreType.DMA((6 * n,))]


class _GatherPlan:
    def __init__(self, pieces):
        self.shards = [p[0] for p in pieces]
        self.spans = [p[1] for p in pieces]
        self.earlier = [p[2] for p in pieces]
        self.n = len(pieces)
        self.carried = [t for t in range(self.n) if self.earlier[t] is not None]

    def operands(self):
        return self.shards + [self.earlier[t] for t in self.carried]

    def in_specs(self):
        return [_vmem()] * self.n + [_hbm()] * len(self.carried)

    def out_specs(self):
        return [_hbm()] * self.n

    def out_shapes(self):
        return _Gather.out_shapes(self.shards, _hbm_shape)

    def scratch(self):
        return ([pltpu.VMEM((N_CHIP,) + s.shape, _WIRE) for s in self.shards] + _Gather.sems(self.n)
                + [pltpu.SemaphoreType.DMA((self.n,)), pltpu.SemaphoreType.DMA((max(len(self.carried), 1),))])

    def bind(self, in_refs, out_refs, scratch_refs):
        plan = self
        shard_refs, earlier_refs = in_refs[:self.n], in_refs[self.n:]
        bufs = scratch_refs[:self.n]
        send_sems, recv_sems, flush_sems, carry_sems = scratch_refs[self.n:self.n + 4]
        gather = _Gather(shard_refs, bufs, send_sems, recv_sems, self.spans)

        def carry_copy(k):
            t = plan.carried[k]
            lo = plan.spans[t][0]
            return pltpu.make_async_copy(earlier_refs[k].at[:, 0:lo, :], bufs[t].at[:, 0:lo, :], carry_sems.at[k])

        class Bound:
            @staticmethod
            def start():
                for k in range(len(plan.carried)):
                    carry_copy(k).start()
                gather.start()

            @staticmethod
            def finish():
                gather.finish()
                for k in range(len(plan.carried)):
                    carry_copy(k).wait()
                _flush([bufs[t].at[:, 0:plan.spans[t][1], :] for t in range(plan.n)],
                       [out_refs[t].at[:, 0:plan.spans[t][1], :] for t in range(plan.n)], flush_sems)

        return Bound


def _flush(bufs, hbm_outs, sems):
    copies = [pltpu.make_async_copy(b, o, sems.at[k]) for k, (b, o) in enumerate(zip(bufs, hbm_outs))]
    for cp in copies:
        cp.start()
    for cp in copies:
        cp.wait()


def _gather_weights(shards):
    n = len(shards)

    def body(*refs):
        gather = _Gather(refs[:n], refs[n:2 * n], refs[2 * n], refs[2 * n + 1])
        gather.start()
        gather.finish()

    return pl.pallas_call(
        body, name="gather_weights",
        in_specs=[_vmem()] * n, out_specs=[_vmem()] * n,
        out_shape=_Gather.out_shapes(shards), scratch_shapes=_Gather.sems(n),
        compiler_params=pltpu.CompilerParams(vmem_limit_bytes=32 * MIB),
    )(*shards)


class _ChipExchange:
    def __init__(self, wire_ref, land_ref, send_sems, recv_sems):
        self.wire, self.land, self.send_sems, self.recv_sems = wire_ref, land_ref, send_sems, recv_sems

    def _copy(self, d):
        x, y, c, chips = _place()
        return pltpu.make_async_remote_copy(
            src_ref=self.wire.at[2 * chips[d][0] + chips[d][1]], dst_ref=self.land.at[d],
            send_sem=self.send_sems.at[d], recv_sem=self.recv_sems.at[d],
            device_id=(chips[d][0], chips[d][1], c), device_id_type=MESH)

    def start(self):
        for d in range(3):
            self._copy(d).start()

    def wait_recv(self):
        for d in range(3):
            self._copy(d).wait_recv()

    def wait_send(self):
        for d in range(3):
            self._copy(d).wait_send()

    def finish_to(self, hbm_out, flush_sem):
        self.wait_recv()
        _flush([self.land], [hbm_out], flush_sem)
        self.wait_send()

    @staticmethod
    def land_shape(wire):
        return _hbm_shape((3,) + wire.shape[1:], wire.dtype)

    @staticmethod
    def sems():
        return [pltpu.SemaphoreType.DMA((3,)), pltpu.SemaphoreType.DMA((3,))]

    @staticmethod
    def scratch(wire):
        return ([pltpu.VMEM((3,) + wire.shape[1:], wire.dtype)] + _ChipExchange.sems() + [pltpu.SemaphoreType.DMA((1,))])


def _pair_out_shapes(half_shape):
    return [jax.ShapeDtypeStruct(half_shape, _WIRE), jax.ShapeDtypeStruct(half_shape[1:], F32)]


def _pair_scratch(acc_shape, half_shape):
    return [pltpu.VMEM(acc_shape, F32), pltpu.VMEM(half_shape, _WIRE),
            pltpu.SemaphoreType.DMA((N_CHIP,)), pltpu.SemaphoreType.DMA((N_CHIP,))]


def _pair_reduce(acc_ref, wire_ref, own_ref, land_ref, send_sems, recv_sems):
    rh = land_ref.shape[1]
    x, y, c, _ = _place()
    me = 2 * x + y
    copies = []
    for j in range(N_CHIP):
        def cast(r, carry, j=j):
            dst = pl.ds(pl.multiple_of(r * ROW_CHUNK, ROW_CHUNK), ROW_CHUNK)
            src = pl.ds(pl.multiple_of((2 * j + 1 - c) * rh + r * ROW_CHUNK, 8), ROW_CHUNK)
            wire_ref[j, dst, :] = acc_ref[src, :].astype(_WIRE)
            return carry

        lax.fori_loop(0, rh // ROW_CHUNK, cast, 0)
        cp = pltpu.make_async_remote_copy(src_ref=wire_ref.at[j], dst_ref=land_ref.at[j], send_sem=send_sems.at[j],
                                          recv_sem=recv_sems.at[j], device_id=(x, y, 1 - c), device_id_type=MESH)
        cp.start()
        copies.append(cp)
    for j in range(N_CHIP):
        copies[j].wait()

        def chunk(r, carry, j=j):
            theirs = pl.ds(pl.multiple_of(r * ROW_CHUNK, ROW_CHUNK), ROW_CHUNK)
            mine = pl.ds(pl.multiple_of((2 * j + c) * rh + r * ROW_CHUNK, 8), ROW_CHUNK)
            wire_ref[j, theirs, :] = (acc_ref[mine, :] + land_ref[j, theirs, :].astype(F32)).astype(_WIRE)
            return carry

        lax.fori_loop(0, rh // ROW_CHUNK, chunk, 0)

    def own_chunk(r, carry):
        theirs = pl.ds(pl.multiple_of(r * ROW_CHUNK, ROW_CHUNK), ROW_CHUNK)
        mine = pl.ds(pl.multiple_of((2 * me + c) * rh + r * ROW_CHUNK, 8), ROW_CHUNK)
        own_ref[theirs, :] = acc_ref[mine, :] + land_ref[me, theirs, :].astype(F32)
        return carry

    lax.fori_loop(0, rh // ROW_CHUNK, own_chunk, 0)


def _grad_finish(last_acc, lands, owns):
    n = len(owns) + 1
    halves = [last_acc.shape[0] // (2 * N_CHIP)] + [w.shape[1] for w in lands]
    widths = [last_acc.shape[1]] + [a.shape[1] for a in owns]

    def body(*refs):
        acc0, land, own, g = refs[0], (None,) + refs[1:n], (None,) + refs[n:2 * n - 1], refs[2 * n - 1:3 * n - 1]
        pland0, wire0, land0, own0 = refs[3 * n - 1:3 * n + 3]
        p_send, p_recv, x_send, x_recv, pair_send, pair_recv = refs[3 * n + 3:3 * n + 9]
        land = (land0,) + land[1:]
        own = (own0,) + own[1:]
        x, y, c, chips = _place()
        me = 2 * x + y
        exchange = _ChipExchange(wire0, land0, x_send, x_recv)

        def half_rows(t, half):
            return pl.ds(pl.multiple_of(half * halves[t], 8), halves[t])

        def pair_copy(t, half):
            rows = g[t].at[half_rows(t, half), :]
            return pltpu.make_async_remote_copy(src_ref=rows, dst_ref=rows, send_sem=pair_send.at[t],
                                                recv_sem=pair_recv.at[t], device_id=(x, y, 1 - c), device_id_type=MESH)

        rh = halves[0]
        gives = []
        for j in range(N_CHIP):
            rows = acc0.at[pl.ds(pl.multiple_of((2 * j + 1 - c) * rh, 8), rh), :]
            cp = pltpu.make_async_remote_copy(src_ref=rows, dst_ref=pland0.at[j], send_sem=p_send.at[j],
                                              recv_sem=p_recv.at[j], device_id=(x, y, 1 - c), device_id_type=MESH)
            cp.start()
            gives.append(cp)
        for cp in gives:
            cp.wait()

        def chip_sum(r, carry):
            src = pl.ds(pl.multiple_of(r * ROW_CHUNK, ROW_CHUNK), ROW_CHUNK)
            for j in range(N_CHIP):
                mine = pl.ds(pl.multiple_of((2 * j + c) * rh + r * ROW_CHUNK, 8), ROW_CHUNK)
                wire0[j, src, :] = (acc0[mine, :] + pland0[j, src, :]).astype(_WIRE)
            mine = pl.ds(pl.multiple_of((2 * me + c) * rh + r * ROW_CHUNK, 8), ROW_CHUNK)
            own[0][src, :] = acc0[mine, :] + pland0[me, src, :]
            return carry

        lax.fori_loop(0, rh // ROW_CHUNK, chip_sum, 0)
        exchange.start()

        for t in list(range(1, n)) + [0]:
            if t == 0:
                exchange.wait_recv()

            def chunk(r, carry, t=t):
                src = pl.ds(pl.multiple_of(r * ROW_CHUNK, ROW_CHUNK), ROW_CHUNK)
                dst = pl.ds(pl.multiple_of(c * halves[t] + r * ROW_CHUNK, 8), ROW_CHUNK)
                s = own[t][src, :]
                for d in range(3):
                    s = s + land[t][d, src, :].astype(F32)
                g[t][dst, :] = s
                return carry

            lax.fori_loop(0, halves[t] // ROW_CHUNK, chunk, 0)
            pair_copy(t, c).start()
        for t in range(n):
            pair_copy(t, 1 - c).wait_recv()
        for t in range(n):
            pair_copy(t, c).wait_send()
        exchange.wait_send()

    half0 = (halves[0], widths[0])
    return pl.pallas_call(
        body, name="grad_finish",
        in_specs=[_vmem()] * (2 * n - 1), out_specs=[_vmem()] * n,
        out_shape=[jax.ShapeDtypeStruct((2 * h, w), F32) for h, w in zip(halves, widths)],
        scratch_shapes=[pltpu.VMEM((N_CHIP,) + half0, F32), pltpu.VMEM((N_CHIP,) + half0, _WIRE),
                        pltpu.VMEM((3,) + half0, _WIRE), pltpu.VMEM(half0, F32)]
        + [pltpu.SemaphoreType.DMA((N_CHIP,)), pltpu.SemaphoreType.DMA((N_CHIP,))]
        + _ChipExchange.sems()
        + [pltpu.SemaphoreType.DMA((n,)), pltpu.SemaphoreType.DMA((n,))],
        compiler_params=pltpu.CompilerParams(vmem_limit_bytes=56 * MIB),
    )(last_acc, *lands, *owns)


_SMALL = ("ln_in_g", "ln_in_b", "b_in", "attn_sinks", "sgu_ln_g", "sgu_ln_b", "sgu_w", "sgu_b", "b_out",
          "ln_mix_g", "ln_mix_b", "ln_ffn_g", "ln_ffn_b")
_VEC_ROW = dict(ln_in_g=0, ln_in_b=1, b_in=2, attn_sinks=4, sgu_ln_g=5, sgu_ln_b=6, b_out=7, ln_mix_g=8, ln_mix_b=9,
                ln_ffn_g=10, ln_ffn_b=11)
_LOSS_ROW = 12
_VEC_ROWS = 16
_MAT_ROWS = N_GRP * BLK + BLK


def _small_allreduce(local):
    n_in = 16

    def body(*refs):
        (g_ln_in_g, g_ln_in_b, g_bq, g_bkv, g_bsuv, g_sink, g_sln_g, g_sln_b, g_sw, g_sbt, g_bout,
         g_lmg, g_lmb, g_lfg, g_lfb, g_loss) = refs[:n_in]
        out_a, out_b = refs[n_in:n_in + 2]
        (buf_a, buf_b, pair_a, pair_b, stage_a, stage_b, tot_a, tot_b,
         p1_send, p1_recv, x_send, x_recv, p2_send, p2_recv) = refs[n_in + 2:]
        x, y, c, chips = _place()
        me = 2 * x + y
        sibling = (x, y, 1 - c)
        half_a, half_b = _VEC_ROWS // 2, _MAT_ROWS // 2

        buf_a[...] = jnp.zeros_like(buf_a)
        for row, ref in ((0, g_ln_in_g), (1, g_ln_in_b), (7, g_bout), (8, g_lmg), (9, g_lmb), (10, g_lfg), (11, g_lfb),
                         (_LOSS_ROW, g_loss)):
            buf_a[row:row + 1, :] = ref[...]
        buf_a[2:3, 0:ATTN_W] = g_bq[...]
        buf_a[2:3, ATTN_W:ATTN_W + 2 * KV_W] = g_bkv[...]
        buf_a[2:3, ATTN_W + 2 * KV_W:D_MODEL] = g_bsuv[:, 0:2 * KV_W]
        buf_a[3:4, 0:2 * SGU_W - 2 * KV_W] = g_bsuv[:, 2 * KV_W:2 * SGU_W]
        buf_a[4:5, 0:128] = g_sink[...]
        buf_a[5:6, 0:SGU_W] = g_sln_g[...]
        buf_a[6:7, 0:SGU_W] = g_sln_b[...]
        for h in range(N_GRP):
            buf_b[h * BLK:(h + 1) * BLK, :] = g_sw[h]
        buf_b[N_GRP * BLK:_MAT_ROWS, :] = g_sbt[...]

        def remote(src, dst, send_sem, recv_sem, to):
            return pltpu.make_async_remote_copy(src_ref=src, dst_ref=dst, send_sem=send_sem, recv_sem=recv_sem,
                                                device_id=to, device_id_type=MESH)

        first = [remote(buf_a, pair_a, p1_send.at[0], p1_recv.at[0], sibling),
                 remote(buf_b, pair_b, p1_send.at[1], p1_recv.at[1], sibling)]
        for cp in first:
            cp.start()
        for cp in first:
            cp.wait()
        rows_a = pl.ds(pl.multiple_of(c * half_a, 8), half_a)
        rows_b = pl.ds(pl.multiple_of(c * half_b, 8), half_b)
        stage_a[me] = buf_a[rows_a, :] + pair_a[rows_a, :]
        stage_b[me] = buf_b[rows_b, :] + pair_b[rows_b, :]

        def chip_copies(d):
            to = (chips[d][0], chips[d][1], c)
            return [remote(stage_a.at[me], stage_a.at[me], x_send.at[2 * d], x_recv.at[2 * d], to),
                    remote(stage_b.at[me], stage_b.at[me], x_send.at[2 * d + 1], x_recv.at[2 * d + 1], to)]

        def chip_arrivals(d):
            slot = 2 * chips[d][0] + chips[d][1]
            to = (chips[d][0], chips[d][1], c)
            return [remote(stage_a.at[slot], stage_a.at[slot], x_send.at[2 * d], x_recv.at[2 * d], to),
                    remote(stage_b.at[slot], stage_b.at[slot], x_send.at[2 * d + 1], x_recv.at[2 * d + 1], to)]

        for d in range(3):
            for cp in chip_copies(d):
                cp.start()
        for d in range(3):
            for cp in chip_arrivals(d):
                cp.wait_recv()
        tot_a[rows_a, :] = ((stage_a[0] + stage_a[1]) + stage_a[2]) + stage_a[3]
        tot_b[rows_b, :] = ((stage_b[0] + stage_b[1]) + stage_b[2]) + stage_b[3]

        second = [remote(tot_a.at[rows_a, :], tot_a.at[rows_a, :], p2_send.at[0], p2_recv.at[0], sibling),
                  remote(tot_b.at[rows_b, :], tot_b.at[rows_b, :], p2_send.at[1], p2_recv.at[1], sibling)]
        for cp in second:
            cp.start()
        other_a = pl.ds(pl.multiple_of((1 - c) * half_a, 8), half_a)
        other_b = pl.ds(pl.multiple_of((1 - c) * half_b, 8), half_b)
        remote(tot_a.at[other_a, :], tot_a.at[other_a, :], p2_send.at[0], p2_recv.at[0], sibling).wait_recv()
        remote(tot_b.at[other_b, :], tot_b.at[other_b, :], p2_send.at[1], p2_recv.at[1], sibling).wait_recv()
        for cp in second:
            cp.wait_send()
        for d in range(3):
            for cp in chip_copies(d):
                cp.wait_send()
        out_a[...] = tot_a[...]
        out_b[...] = tot_b[...]

    ins = [local[k] for k in ("ln_in_g", "ln_in_b", "bq", "bkv", "bsuv", "sink", "sgu_ln_g", "sgu_ln_b", "sgu_w",
                              "sgu_bt", "b_out", "ln_mix_g", "ln_mix_b", "ln_ffn_g", "ln_ffn_b", "loss")]
    out_dims = [(_VEC_ROWS, D_MODEL), (_MAT_ROWS, 128)]
    vec = pltpu.VMEM((_VEC_ROWS, D_MODEL), F32)
    mat = pltpu.VMEM((_MAT_ROWS, 128), F32)
    return pl.pallas_call(
        body, name="small_allreduce", grid=(1,),
        in_specs=[_const2(a.shape) for a in ins], out_specs=[_const2(s) for s in out_dims],
        out_shape=[_hbm_shape(s, F32) for s in out_dims],
        scratch_shapes=[vec, mat, vec, mat, pltpu.VMEM((N_CHIP, _VEC_ROWS // 2, D_MODEL), F32),
                        pltpu.VMEM((N_CHIP, _MAT_ROWS // 2, 128), F32), vec, mat,
                        pltpu.SemaphoreType.DMA((2,)), pltpu.SemaphoreType.DMA((2,)), pltpu.SemaphoreType.DMA((6,)),
                        pltpu.SemaphoreType.DMA((6,)), pltpu.SemaphoreType.DMA((2,)), pltpu.SemaphoreType.DMA((2,))],
        compiler_params=pltpu.CompilerParams(vmem_limit_bytes=32 * MIB),
    )(*ins)


def _small_adamw(tot_a, tot_b, params):
    shapes = [params[nm][0].shape for nm in _SMALL]

    def body(*refs):
        ta, tb = refs[:2]
        prm = refs[2:2 + 3 * len(_SMALL)]
        outs = refs[2 + 3 * len(_SMALL):]

        def grad_of(k, name):
            if name == "sgu_w":
                return [tb[h * BLK:(h + 1) * BLK, :] for h in range(N_GRP)]
            if name == "sgu_b":
                return jnp.transpose(tb[N_GRP * BLK:_MAT_ROWS, :])[0:N_GRP, :]
            row = _VEC_ROW[name]
            if name == "b_in":
                return jnp.concatenate([ta[row:row + 1, :], ta[row + 1:row + 2, 0:IN_W - D_MODEL]], axis=1)
            return ta[row:row + 1, 0:shapes[k][-1]]

        for k, name in enumerate(_SMALL):
            w_ref, m_ref, v_ref = prm[3 * k:3 * k + 3]
            g_out, d_out, m_out, v_out = outs[4 * k:4 * k + 4]
            g = grad_of(k, name)
            if name == "sgu_w":
                for h in range(N_GRP):
                    d_, m_, v_ = _adamw_math(w_ref[h], g[h], m_ref[h], v_ref[h])
                    g_out[h], d_out[h], m_out[h], v_out[h] = g[h], d_, m_, v_
            else:
                d_, m_, v_ = _adamw_math(w_ref[...], g, m_ref[...], v_ref[...])
                g_out[...], d_out[...], m_out[...], v_out[...] = g, d_, m_, v_
        outs[-1][...] = ta[_LOSS_ROW:_LOSS_ROW + 1, :]

    ins = [tot_a, tot_b] + [_in_hbm(a) for nm in _SMALL for a in params[nm]]
    out_dims = [s for s in shapes for _ in range(4)] + [(1, D_MODEL)]
    res = pl.pallas_call(
        body, name="small_adamw", grid=(1,),
        in_specs=[_const2(a.shape) for a in ins], out_specs=[_const2(s) for s in out_dims],
        out_shape=[_hbm_shape(s, F32) for s in out_dims],
        compiler_params=_params(32),
    )(*ins)
    return {nm: tuple(res[4 * k:4 * k + 4]) for k, nm in enumerate(_SMALL)}, res[-1]


def _elementwise(name, fn, ins, out_dtypes, tile_rows=256):
    shape = ins[0].shape
    lead = shape[:-2]
    rows, cols = shape[-2:]
    tr = _tile(rows, tile_rows)
    n_lead = math.prod(lead)
    nr = rows // tr
    flat = [_in_hbm(a.reshape((n_lead, rows, cols))) for a in ins]

    def body(*refs):
        outs = fn(*[r[0] for r in refs[:len(ins)]])
        for o_ref, o in zip(refs[len(ins):], outs):
            o_ref[0] = o.astype(o_ref.dtype)

    spec = pl.BlockSpec((1, tr, cols), lambda i: (i // nr, i % nr, 0))
    res = pl.pallas_call(
        body, name=name, grid=(n_lead * nr,),
        in_specs=[spec] * len(ins), out_specs=[spec] * len(out_dtypes),
        out_shape=[_hbm_shape((n_lead, rows, cols), dt) for dt in out_dtypes],
        compiler_params=_params(32),
    )(*flat)
    return [r.reshape(shape) for r in res]


def _adamw_math(w, g, m, v):
    m = ADAM_B1 * m + (1.0 - ADAM_B1) * g
    v = ADAM_B2 * v + (1.0 - ADAM_B2) * (g * g)
    m_hat = m / (1.0 - ADAM_B1 ** ADAM_STEP)
    v_hat = v / (1.0 - ADAM_B2 ** ADAM_STEP)
    delta = -ADAM_LR * (m_hat / (jnp.sqrt(v_hat) + ADAM_EPS) + ADAM_WD * w)
    return delta, m, v


def _adamw(name, w, g, m, v, tile_rows=256):
    return _elementwise(name, lambda w_, g_, m_, v_: (g_,) + _adamw_math(w_, g_, m_, v_), [w, g, m, v],
                        [F32, F32, F32, F32], tile_rows)


def kernel(x, positions, ln_in_g, ln_in_b, w_in, b_in, attn_sinks, sgu_ln_g, sgu_ln_b, sgu_w, sgu_b, w_out, b_out, ln_mix_g, ln_mix_b, w_gate, w_up, w_down, ln_ffn_g, ln_ffn_b, loss_target, m_ln_in_g, m_ln_in_b, m_w_in, m_b_in, m_attn_sinks, m_sgu_ln_g, m_sgu_ln_b, m_sgu_w, m_sgu_b, m_w_out, m_b_out, m_ln_mix_g, m_ln_mix_b, m_w_gate, m_w_up, m_w_down, m_ln_ffn_g, m_ln_ffn_b, v_ln_in_g, v_ln_in_b, v_w_in, v_b_in, v_attn_sinks, v_sgu_ln_g, v_sgu_ln_b, v_sgu_w, v_sgu_b, v_w_out, v_b_out, v_ln_mix_g, v_ln_mix_b, v_w_gate, v_w_up, v_w_down, v_ln_ffn_g, v_ln_ffn_b):
    weights = dict(ln_in_g=ln_in_g, ln_in_b=ln_in_b, w_in=w_in, b_in=b_in, attn_sinks=attn_sinks, sgu_ln_g=sgu_ln_g,
                   sgu_ln_b=sgu_ln_b, sgu_w=sgu_w, sgu_b=sgu_b, w_out=w_out, b_out=b_out, ln_mix_g=ln_mix_g,
                   ln_mix_b=ln_mix_b, w_gate=w_gate, w_up=w_up, w_down=w_down, ln_ffn_g=ln_ffn_g, ln_ffn_b=ln_ffn_b)
    mom_m = dict(ln_in_g=m_ln_in_g, ln_in_b=m_ln_in_b, w_in=m_w_in, b_in=m_b_in, attn_sinks=m_attn_sinks,
                 sgu_ln_g=m_sgu_ln_g, sgu_ln_b=m_sgu_ln_b, sgu_w=m_sgu_w, sgu_b=m_sgu_b, w_out=m_w_out, b_out=m_b_out,
                 ln_mix_g=m_ln_mix_g, ln_mix_b=m_ln_mix_b, w_gate=m_w_gate, w_up=m_w_up, w_down=m_w_down,
                 ln_ffn_g=m_ln_ffn_g, ln_ffn_b=m_ln_ffn_b)
    mom_v = dict(ln_in_g=v_ln_in_g, ln_in_b=v_ln_in_b, w_in=v_w_in, b_in=v_b_in, attn_sinks=v_attn_sinks,
                 sgu_ln_g=v_sgu_ln_g, sgu_ln_b=v_sgu_ln_b, sgu_w=v_sgu_w, sgu_b=v_sgu_b, w_out=v_w_out, b_out=v_b_out,
                 ln_mix_g=v_ln_mix_g, ln_mix_b=v_ln_mix_b, w_gate=v_w_gate, w_up=v_w_up, w_down=v_w_down,
                 ln_ffn_g=v_ln_ffn_g, ln_ffn_b=v_ln_ffn_b)
    order = list(weights)
    big = ("w_in", "w_out", "w_gate", "w_up", "w_down")

    s_len = x.shape[1]
    xs = _in_hbm(x.reshape(s_len, D_MODEL))
    tgt = _in_hbm(loss_target.reshape(s_len, D_MODEL))
    pos_col = _in_hbm(positions.reshape(s_len, 1))
    g0, b0 = _in_hbm(ln_in_g.reshape(1, D_MODEL)), _in_hbm(ln_in_b.reshape(1, D_MODEL))
    sinks = attn_sinks.reshape(N_Q)
    sgu_w3 = _in_hbm(sgu_w.reshape(N_GRP, BLK, BLK))
    sgu_bt = _in_hbm(sgu_b.reshape(N_GRP, BLK).T)
    b_in, b_out, sgu_ln_g, sgu_ln_b, ln_mix_g, ln_mix_b, ln_ffn_g, ln_ffn_b = (
        _in_hbm(a) for a in (b_in, b_out, sgu_ln_g, sgu_ln_b, ln_mix_g, ln_mix_b, ln_ffn_g, ln_ffn_b))

    col_sharded = ("w_in", "w_gate", "w_up")

    def rowmajor(name, a):
        return jnp.swapaxes(a[0], 0, 1) if name in col_sharded else a[0]

    def as_given(name, a):
        return (jnp.swapaxes(a, 0, 1) if name in col_sharded else a)[None]

    shards = [rowmajor(n, weights[n]) for n in big]
    (gw_in,) = _gather_weights(shards[0:1])
    w_in_full = gw_in.reshape(IN_W, D_MODEL)

    sh_out, sh_gate, sh_up, sh_down = shards[1:]
    cut = GATHER_CUT
    *acts, gw_out, gw_gate0 = _ln_inproj(xs, pos_col, g0, b0, w_in_full, b_in, _GatherPlan(
        [(sh_out, (0, OUT_SH), None), (sh_gate, (0, cut), None)]))
    q, k, v, su, sv, tc, t1, t2 = (_in_hbm(a) for a in acts)
    mc, gw_gate, gw_up0 = _mixer_fwd(q, k, v, su, sv, sinks, sgu_ln_g, sgu_ln_b, sgu_w3, sgu_bt, _GatherPlan(
        [(sh_gate, (cut, FF_SH), gw_gate0), (sh_up, (0, cut), None)]))
    mc = _in_hbm(mc)
    w_out_full = gw_out.reshape(D_MODEL, D_MODEL)
    r1, gw_up = _outproj(mc, w_out_full, b_out, xs, g0, b0, _GatherPlan([(sh_up, (cut, FF_SH), gw_up0)]))
    r1 = _in_hbm(r1)
    act, p_act, q_act, gw_down = _ffn_up(r1, ln_mix_g, ln_mix_b, gw_gate, gw_up,
                                         _GatherPlan([(sh_down, (0, FF_SH), None)]))
    act, p_act, q_act = _in_hbm(act), _in_hbm(p_act), _in_hbm(q_act)
    dr2, loss_cols, d_ln_ffn_g, d_ln_ffn_b = _ffn_down_loss(act, gw_down, r1, ln_mix_g, ln_mix_b, ln_ffn_g, ln_ffn_b, tgt)
    dr2 = _in_hbm(dr2)

    dg, du, wire_down, own_down = _ffn_bwd_a(dr2, act, p_act, q_act, gw_down)
    dh1a, wire_gate, own_gate, land_down = _ffn_bwd_g(dr2, _in_hbm(dg), r1, ln_mix_g, ln_mix_b, gw_gate, wire_down)
    dr1, wire_up, own_up, d_ln_mix_g, d_ln_mix_b, land_gate = _ffn_bwd_u(_in_hbm(dh1a), _in_hbm(du), r1, ln_mix_g,
                                                                         ln_mix_b, gw_up, wire_gate)
    dr1 = _in_hbm(dr1)
    dmc, wire_out, own_out, d_b_out, land_up = _outproj_bwd(dr1, mc, w_out_full, wire_up)
    (dq, dkv, dsuv, dbq, dbkv, dbsuv, d_sink, d_sgu_ln_g, d_sgu_ln_b, d_sgu_w, d_sgu_bt, land_out) = _mixer_bwd(
        q, k, v, su, sv, _in_hbm(dmc), tc, t1, t2, sinks, sgu_ln_g, sgu_ln_b, sgu_w3, sgu_bt, wire_out)
    grad_x, acc_in, d_ln_in_g, d_ln_in_b = _inproj_bwd(_in_hbm(dq), _in_hbm(dkv), _in_hbm(dsuv), dr1, xs, g0, b0,
                                                       w_in_full)

    reduced = _grad_finish(acc_in, [land_out, land_gate, land_up, land_down], [own_out, own_gate, own_up, own_down])
    small_shape = dict(ln_in_g=(1, D_MODEL), ln_in_b=(1, D_MODEL), sgu_w=(N_GRP, BLK, BLK), sgu_b=(N_GRP, BLK))
    small_local = dict(
        ln_in_g=d_ln_in_g, ln_in_b=d_ln_in_b, bq=dbq, bkv=dbkv, bsuv=dbsuv, sink=d_sink, sgu_ln_g=d_sgu_ln_g,
        sgu_ln_b=d_sgu_ln_b, sgu_w=d_sgu_w, sgu_bt=d_sgu_bt, b_out=d_b_out, ln_mix_g=d_ln_mix_g, ln_mix_b=d_ln_mix_b,
        ln_ffn_g=d_ln_ffn_g, ln_ffn_b=d_ln_ffn_b, loss=loss_cols)
    small_params = {nm: tuple(src[nm].reshape(small_shape.get(nm, src[nm].shape)) for src in (weights, mom_m, mom_v))
                    for nm in _SMALL}
    tot_a, tot_b = _small_allreduce({nm: _in_hbm(a) for nm, a in small_local.items()})
    small_out, loss_sum = _small_adamw(_in_hbm(tot_a), _in_hbm(tot_b), small_params)
    loss = jnp.sum(loss_sum) * (0.5 / D_MODEL)
    grads, delta, new_m, new_v = {}, {}, {}, {}
    for nm in _SMALL:
        grads[nm], delta[nm], new_m[nm], new_v[nm] = (a.reshape(weights[nm].shape) for a in small_out[nm])

    for t, name in enumerate(big):
        g_, d_, m_, v_ = _adamw("adamw_" + name, shards[t], reduced[t], rowmajor(name, mom_m[name]),
                                rowmajor(name, mom_v[name]))
        grads[name], delta[name], new_m[name], new_v[name] = (as_given(name, a) for a in (g_, d_, m_, v_))

    return (loss, grad_x.reshape(x.shape), *[grads[n] for n in order], *[delta[n] for n in order],
            *[new_m[n] for n in order], *[new_v[n] for n in order])
```

```python
import functools
import math

import jax
import jax.numpy as jnp
from jax import lax
from jax.experimental import pallas as pl
from jax.experimental.pallas import tpu as pltpu

F32 = jnp.float32
_MXU = jnp.bfloat16
_WIRE = jnp.bfloat16
_ACT = jnp.bfloat16

D_MODEL = 1024
ATTN_W = 512
SGU_W = 512
HEAD_DIM = 64
N_Q = 8
N_KV = 2
Q_PER_KV = 4
KV_W = 128
BLK = 128
ROT_DIM = 16
ROPE_THETA = 500000.0
N_GRP = 4
GRP_DIM = 128
D_FF = 2816
IN_W = 1792
LN_EPS = 1e-5
ALPHA = 2.0 ** 0.25
N_CHIP = 4
FF_SH = D_FF // N_CHIP
IN_SH = IN_W // N_CHIP
OUT_SH = D_MODEL // N_CHIP
ROW_CHUNK = 32
GATHER_CUT = 224

ADAM_LR = 0.001
ADAM_B1 = 0.9
ADAM_B2 = 0.999
ADAM_EPS = 1e-08
ADAM_WD = 0.01
ADAM_STEP = 10

SQRT_HALF = 0.7071067811865476
INV_SQRT_2PI = 0.3989422804014327
MESH_AXES = ("x", "y", "c")
MESH = pl.DeviceIdType.MESH
MIB = 2 ** 20


def _vmem():
    return pl.BlockSpec(memory_space=pltpu.VMEM)


def _smem():
    return pl.BlockSpec(memory_space=pltpu.SMEM)


def _hbm():
    return pl.BlockSpec(memory_space=pl.ANY)


def _hbm_shape(shape, dtype):
    return pltpu.HBM(shape, dtype)


def _in_hbm(a):
    return pltpu.with_memory_space_constraint(a, pltpu.HBM)


def _params(vmem_mib=48):
    return pltpu.CompilerParams(dimension_semantics=("arbitrary",), vmem_limit_bytes=vmem_mib * MIB)


def _tile(n, cap):
    if n <= cap:
        return n
    for t in range(cap - cap % 16, 0, -16):
        if n % t == 0:
            return t
    raise ValueError((n, cap))


def _rows(tm, width):
    return pl.BlockSpec((tm, width), lambda i: (i, 0))


def _const2(shape):
    return pl.BlockSpec(shape, lambda i: (0,) * len(shape))


def _ln(x, g, b):
    mu = jnp.mean(x, axis=-1, keepdims=True)
    xc = x - mu
    var = jnp.mean(xc * xc, axis=-1, keepdims=True)
    rstd = lax.rsqrt(var + LN_EPS)
    xhat = xc * rstd
    return xhat * g + b, xhat, rstd


def _ln_bwd(dy, xhat, rstd, g):
    gdy = dy * g
    m1 = jnp.mean(gdy, axis=-1, keepdims=True)
    m2 = jnp.mean(gdy * xhat, axis=-1, keepdims=True)
    return rstd * (gdy - m1 - xhat * m2)


def _colsum(a):
    return jnp.sum(a, axis=0, keepdims=True)


def _gelu_and_grad(x):
    cdf = 0.5 * (1.0 + lax.erf(x * SQRT_HALF))
    return x * cdf, cdf + x * jnp.exp(-0.5 * x * x) * INV_SQRT_2PI


def _dot(a, b):
    return jnp.dot(a, b, preferred_element_type=F32)


def _dot_nt(a, b):
    return lax.dot_general(a, b, (((1,), (1,)), ((), ())), preferred_element_type=F32)


def _dot_tn(a, b):
    return lax.dot_general(a, b, (((0,), (0,)), ((), ())), preferred_element_type=F32)


def _rope(t, tc, t1, t2):
    n = t.shape[1]
    rep = n // 128
    if rep > 1:
        tc, t1, t2 = (jnp.tile(a, (1, rep)) for a in (tc, t1, t2))
    return t * tc + pltpu.roll(t, n - 8, 1) * t1 + pltpu.roll(t, 8, 1) * t2


def _rope_bwd(d, tc, t1, t2):
    n = d.shape[1]
    rep = n // 128
    if rep > 1:
        tc, t1, t2 = (jnp.tile(a, (1, rep)) for a in (tc, t1, t2))
    return d * tc + pltpu.roll(d * t1, 8, 1) + pltpu.roll(d * t2, n - 8, 1)


def _causal_w(w_ref, h):
    t = lax.broadcasted_iota(jnp.int32, (BLK, BLK), 0)
    s = lax.broadcasted_iota(jnp.int32, (BLK, BLK), 1)
    return jnp.where(s <= t, w_ref[h], 0.0)


def _lane_put(vals, width):
    rows = vals[0].shape[0]
    lane = lax.broadcasted_iota(jnp.int32, (rows, width), 1)
    out = jnp.zeros((rows, width), F32)
    for k, v in enumerate(vals):
        out = out + jnp.where(lane == k, v, 0.0)
    return out


def _rope_consts():
    lane = jnp.arange(128) % HEAD_DIM
    rot = lane < ROT_DIM
    pair = (2 * (lane % (ROT_DIM // 2))).astype(F32)
    freq = jnp.where(rot, ROPE_THETA ** (-pair / ROT_DIM), 0.0)
    rows = [freq, rot.astype(F32), 1.0 - rot.astype(F32), (lane < ROT_DIM // 2).astype(F32),
            jnp.logical_and(lane >= ROT_DIM // 2, rot).astype(F32)]
    rows += [jnp.zeros((128,), F32)] * 3
    return jnp.stack(rows).astype(F32)


def _ln_inproj(x, pos_col, g0, b0, w_in, b_in, plan):
    s_len = x.shape[0]
    tm = _tile(s_len, 512)
    m, n = len(plan.operands()), plan.n

    def body(x_ref, pos_ref, g_ref, b_ref, w_ref, bi_ref, rc_ref, *rest):
        q_ref, k_ref, v_ref, su_ref, sv_ref, tc_ref, t1_ref, t2_ref = rest[m:m + 8]
        gather = plan.bind(rest[:m], rest[m + 8:m + 8 + n], rest[m + 8 + n:])
        i = pl.program_id(0)

        @pl.when(i == 0)
        def _():
            gather.start()

        h0, _, _ = _ln(x_ref[...], g_ref[...], b_ref[...])
        proj = _dot_nt(h0.astype(_MXU), w_ref[...]) + bi_ref[...]
        ang = pos_ref[...].astype(F32) * rc_ref[0:1, :]
        cs = jnp.cos(ang)
        sn = jnp.sin(ang)
        tc = cs * rc_ref[1:2, :] + rc_ref[2:3, :]
        t1 = -sn * rc_ref[3:4, :]
        t2 = sn * rc_ref[4:5, :]
        tc_ref[...] = tc
        t1_ref[...] = t1
        t2_ref[...] = t2
        q = _rope(proj[:, 0:ATTN_W], tc, t1, t2) * (HEAD_DIM ** -0.5)
        q_ref[...] = q.astype(_MXU)
        k_ref[...] = _rope(proj[:, ATTN_W:ATTN_W + KV_W], tc, t1, t2).astype(_MXU)
        v_ref[...] = proj[:, ATTN_W + KV_W:ATTN_W + 2 * KV_W].astype(_MXU)
        su_ref[...] = proj[:, ATTN_W + 2 * KV_W:ATTN_W + 2 * KV_W + SGU_W]
        sv_ref[...] = proj[:, ATTN_W + 2 * KV_W + SGU_W:IN_W]

        last = pl.num_programs(0) - 1

        @pl.when(i == jnp.maximum(last - 1, 0))
        def _():
            gather.pass_on()

        @pl.when(i == last)
        def _():
            gather.finish()

    sd = _hbm_shape
    return pl.pallas_call(
        body, name="ln_inproj", grid=(s_len // tm,),
        in_specs=[_rows(tm, D_MODEL), _rows(tm, 1), _const2((1, D_MODEL)), _const2((1, D_MODEL)), _vmem(),
                  _const2((1, IN_W)), _const2((8, 128))] + plan.in_specs(),
        out_specs=[_rows(tm, ATTN_W), _rows(tm, KV_W), _rows(tm, KV_W), _rows(tm, SGU_W), _rows(tm, SGU_W),
                   _rows(tm, 128), _rows(tm, 128), _rows(tm, 128)] + plan.out_specs(),
        out_shape=[sd((s_len, ATTN_W), _MXU), sd((s_len, KV_W), _MXU), sd((s_len, KV_W), _MXU),
                   sd((s_len, SGU_W), F32), sd((s_len, SGU_W), F32),
                   sd((s_len, 128), F32), sd((s_len, 128), F32), sd((s_len, 128), F32)] + plan.out_shapes(),
        scratch_shapes=plan.scratch(),
        compiler_params=_params(56),
    )(x, pos_col, g0, b0, w_in, b_in, _rope_consts(), *plan.operands())


def _band_mask_t(first_block):
    kj = lax.broadcasted_iota(jnp.int32, (2 * BLK, BLK), 0)
    qi = lax.broadcasted_iota(jnp.int32, (2 * BLK, BLK), 1)
    shut = jnp.where(first_block, 2 * BLK, 0)
    prev_ok = jnp.logical_and(kj < BLK, kj > qi + shut)
    cur_ok = jnp.logical_and(kj >= BLK, (kj - BLK) <= qi)
    return jnp.logical_or(prev_ok, cur_ok)


def _attn_probs_t(kh, qh, sink, allowed_t):
    s = jnp.where(allowed_t, _dot_nt(kh, qh), -1e30)
    m = jnp.maximum(jnp.max(s, axis=0, keepdims=True), sink)
    p = jnp.exp(s - m)
    ps = jnp.exp(sink - m)
    inv = 1.0 / (jnp.sum(p, axis=0, keepdims=True) + ps)
    return p * inv, ps * inv


def _sgu_mix(gv, lg, lb, w_ref, bt_ref):
    vv, vhat, rstd = _ln(gv, lg, lb)
    vvb = vv.astype(_MXU)
    wcs, mixed = [], []
    for h in range(N_GRP):
        wc = _causal_w(w_ref, h).astype(_MXU)
        wcs.append(wc)
        mixed.append(_dot(wc, vvb[:, h * GRP_DIM:(h + 1) * GRP_DIM]) + bt_ref[:, h:h + 1])
    return jnp.concatenate(mixed, axis=1), vhat, rstd, vvb, wcs


def _prev_map(i):
    return (jnp.maximum(i - 1, 0), 0)


def _mixer_fwd(q, k, v, su, sv, sinks, sg, sb, sgu_w, sgu_bt, plan):
    s_len = q.shape[0]
    nb = s_len // BLK
    m, n = len(plan.operands()), plan.n

    def body(q_ref, kc_ref, kp_ref, vc_ref, vp_ref, su_ref, sv_ref, sink_ref, lg_ref, lb_ref, w_ref, bt_ref, *rest):
        mc_ref = rest[m]
        gather = plan.bind(rest[:m], rest[m + 1:m + 1 + n], rest[m + 1 + n:])
        i = pl.program_id(0)

        @pl.when(i == 0)
        def _():
            gather.start()

        @pl.when(i == max(nb - 4, 0))
        def _():
            gather.pass_on()

        @pl.when(i == nb - 1)
        def _():
            gather.finish()

        allowed_t = _band_mask_t(i == 0)
        kb = jnp.concatenate([kp_ref[...], kc_ref[...]], axis=0)
        vb = jnp.concatenate([vp_ref[...], vc_ref[...]], axis=0)
        qv = q_ref[...]
        outs = []
        allowed_g = jnp.tile(allowed_t, (1, Q_PER_KV))
        for g in range(N_KV):
            heads = range(g * Q_PER_KV, (g + 1) * Q_PER_KV)
            kh = kb[:, g * HEAD_DIM:(g + 1) * HEAD_DIM]
            vh = vb[:, g * HEAD_DIM:(g + 1) * HEAD_DIM]
            q_g = jnp.concatenate([qv[:, h * HEAD_DIM:(h + 1) * HEAD_DIM] for h in heads], axis=0)
            sink_g = jnp.concatenate([jnp.full((1, BLK), sink_ref[h], F32) for h in heads], axis=1)
            probs_t, _ = _attn_probs_t(kh, q_g, sink_g, allowed_g)
            o_g = _dot_tn(probs_t.astype(_MXU), vh)
            outs += [o_g[hh * BLK:(hh + 1) * BLK, :] for hh in range(Q_PER_KV)]
        u = _gelu_and_grad(su_ref[...])[0]
        gv = _gelu_and_grad(sv_ref[...])[0]
        mixed = _sgu_mix(gv, lg_ref[...], lb_ref[...], w_ref, bt_ref)[0]
        mc_ref[...] = jnp.concatenate(outs + [u * mixed], axis=1).astype(_MXU)

    cur = lambda w: pl.BlockSpec((BLK, w), lambda i: (i, 0))
    prev = lambda w: pl.BlockSpec((BLK, w), _prev_map)
    return pl.pallas_call(
        body, name="mixer_fwd", grid=(nb,),
        in_specs=[cur(ATTN_W), cur(KV_W), prev(KV_W), cur(KV_W), prev(KV_W), cur(SGU_W), cur(SGU_W), _smem(),
                  _const2((1, SGU_W)), _const2((1, SGU_W)), _const2((N_GRP, BLK, BLK)), _const2((BLK, N_GRP))]
        + plan.in_specs(),
        out_specs=[cur(D_MODEL)] + plan.out_specs(),
        out_shape=[_hbm_shape((s_len, D_MODEL), _MXU)] + plan.out_shapes(),
        scratch_shapes=plan.scratch(),
        compiler_params=_params(56),
    )(q, k, k, v, v, su, sv, sinks, sg, sb, sgu_w, sgu_bt, *plan.operands())


def _outproj(mc, w_out, b_out, x, g0, b0, plan):
    s_len = x.shape[0]
    tm = _tile(s_len, 512)
    m, n = len(plan.operands()), plan.n

    def body(mc_ref, w_ref, bo_ref, x_ref, g_ref, b_ref, *rest):
        r1_ref = rest[m]
        gather = plan.bind(rest[:m], rest[m + 1:m + 1 + n], rest[m + 1 + n:])
        i = pl.program_id(0)

        @pl.when(i == 0)
        def _():
            gather.start()

        h0, _, _ = _ln(x_ref[...], g_ref[...], b_ref[...])
        r1_ref[...] = ALPHA * h0 + (_dot(mc_ref[...], w_ref[...]) + bo_ref[...])

        last = pl.num_programs(0) - 1

        @pl.when(i == jnp.maximum(last - 1, 0))
        def _():
            gather.pass_on()

        @pl.when(i == last)
        def _():
            gather.finish()

    return pl.pallas_call(
        body, name="outproj", grid=(s_len // tm,),
        in_specs=[_rows(tm, D_MODEL), _vmem(), _const2((1, D_MODEL)), _rows(tm, D_MODEL),
                  _const2((1, D_MODEL)), _const2((1, D_MODEL))] + plan.in_specs(),
        out_specs=[_rows(tm, D_MODEL)] + plan.out_specs(),
        out_shape=[_hbm_shape((s_len, D_MODEL), F32)] + plan.out_shapes(),
        scratch_shapes=plan.scratch(),
        compiler_params=_params(40),
    )(mc, w_out, b_out, x, g0, b0, *plan.operands())


def _ffn_spec(tm):
    return pl.BlockSpec((N_CHIP, tm, FF_SH), lambda i: (0, i, 0))


def _ffn_up(r1, g1, b1, wg, wu, plan):
    s_len = r1.shape[0]
    tm = _tile(s_len, 512)
    m, n = len(plan.operands()), plan.n

    def body(r1_ref, g_ref, b_ref, wg_ref, wu_ref, *rest):
        a_ref, p_ref, q_ref = rest[m:m + 3]
        gather = plan.bind(rest[:m], rest[m + 3:m + 3 + n], rest[m + 3 + n:])
        i = pl.program_id(0)

        @pl.when(i == 0)
        def _():
            gather.start()

        h1, _, _ = _ln(r1_ref[...], g_ref[...], b_ref[...])
        h1b = h1.astype(_MXU)
        for j in range(N_CHIP):
            g = _dot_nt(h1b, wg_ref[j])
            u = _dot_nt(h1b, wu_ref[j])
            silu, sg = _silu_parts(g)
            a_ref[j] = (silu * u).astype(_MXU)
            p_ref[j] = silu.astype(_ACT)
            q_ref[j] = (u * (sg * (1.0 + g * (1.0 - sg)))).astype(_ACT)

        last = pl.num_programs(0) - 1

        @pl.when(i == jnp.maximum(last - 1, 0))
        def _():
            gather.pass_on()

        @pl.when(i == last)
        def _():
            gather.finish()

    sd = _hbm_shape((N_CHIP, s_len, FF_SH), _ACT)
    return pl.pallas_call(
        body, name="ffn_up", grid=(s_len // tm,),
        in_specs=[_rows(tm, D_MODEL), _const2((1, D_MODEL)), _const2((1, D_MODEL)), _vmem(), _vmem()] + plan.in_specs(),
        out_specs=[_ffn_spec(tm)] * 3 + plan.out_specs(),
        out_shape=[_hbm_shape((N_CHIP, s_len, FF_SH), _MXU), sd, sd] + plan.out_shapes(),
        scratch_shapes=plan.scratch(),
        compiler_params=_params(56),
    )(r1, g1, b1, wg, wu, *plan.operands())


def _silu_parts(g):
    sg = 1.0 / (1.0 + jnp.exp(-g))
    return g * sg, sg


def _ffn_down_loss(act, wd, r1, g1, b1, g2, b2, target):
    s_len = r1.shape[0]
    tm = _tile(s_len, 512)

    parts = 2 if tm % 32 == 0 else 1
    sub = tm // parts

    def body(a_ref, wd_ref, r1_ref, g1_ref, b1_ref, g2_ref, b2_ref, t_ref, dr2_ref, loss_ref, dg2_ref, db2_ref):
        i = pl.program_id(0)

        @pl.when(i == 0)
        def _():
            loss_ref[...] = jnp.zeros_like(loss_ref)
            dg2_ref[...] = jnp.zeros_like(dg2_ref)
            db2_ref[...] = jnp.zeros_like(db2_ref)

        for part in range(parts):
            rows = slice(part * sub, (part + 1) * sub)
            f = jnp.zeros((sub, D_MODEL), F32)
            for j in range(N_CHIP):
                f = f + _dot(a_ref[j, rows, :], wd_ref[j])
            h1, _, _ = _ln(r1_ref[rows, :], g1_ref[...], b1_ref[...])
            h2, r2hat, rstd2 = _ln(ALPHA * h1 + f, g2_ref[...], b2_ref[...])
            diff = h2 - t_ref[rows, :]
            dh2 = diff * (1.0 / D_MODEL)
            loss_ref[...] += _colsum(diff * diff)
            dg2_ref[...] += _colsum(dh2 * r2hat)
            db2_ref[...] += _colsum(dh2)
            dr2_ref[rows, :] = _ln_bwd(dh2, r2hat, rstd2, g2_ref[...])

    vec = _hbm_shape((1, D_MODEL), F32)
    c = _const2((1, D_MODEL))
    return pl.pallas_call(
        body, name="ffn_down_loss", grid=(s_len // tm,),
        in_specs=[_ffn_spec(tm), _vmem(), _rows(tm, D_MODEL), c, c, c, c, _rows(tm, D_MODEL)],
        out_specs=[_rows(tm, D_MODEL), c, c, c],
        out_shape=[_hbm_shape((s_len, D_MODEL), F32), vec, vec, vec],
        compiler_params=_params(48),
    )(act, wd, r1, g1, b1, g2, b2, target)


def _ffn_bwd_a(dr2, act, p_act, q_act, wd):
    s_len = dr2.shape[0]
    tm = _tile(s_len, 512)

    def body(dr2_ref, a_ref, p_ref, q_ref, wd_ref, dg_ref, du_ref, wire_ref, own_ref,
             dwd_ref, land_ref, send_sem, recv_sem):
        i = pl.program_id(0)

        @pl.when(i == 0)
        def _():
            dwd_ref[...] = jnp.zeros_like(dwd_ref)

        dfb = dr2_ref[...].astype(_MXU)
        for j in range(N_CHIP):
            da = _dot_nt(dfb, wd_ref[j])
            dg_ref[j] = (da * q_ref[j].astype(F32)).astype(_MXU)
            du_ref[j] = (da * p_ref[j].astype(F32)).astype(_MXU)
            dwd_ref[j * FF_SH:(j + 1) * FF_SH, :] += _dot_tn(a_ref[j], dfb)

        @pl.when(i == pl.num_programs(0) - 1)
        def _():
            _pair_reduce(dwd_ref, wire_ref, own_ref, land_ref, send_sem, recv_sem)

    sd = _hbm_shape((N_CHIP, s_len, FF_SH), _MXU)
    half = (N_CHIP, FF_SH // 2, D_MODEL)
    return pl.pallas_call(
        body, name="ffn_bwd_a", grid=(s_len // tm,),
        in_specs=[_rows(tm, D_MODEL), _ffn_spec(tm), _ffn_spec(tm), _ffn_spec(tm), _vmem()],
        out_specs=[_ffn_spec(tm), _ffn_spec(tm), _vmem(), _vmem()],
        out_shape=[sd, sd] + _pair_out_shapes(half),
        scratch_shapes=_pair_scratch((D_FF, D_MODEL), half),
        compiler_params=_params(61),
    )(dr2, act, p_act, q_act, wd)


def _ffn_bwd_g(dr2, dg, r1, g1, b1, wg, prev_wire):
    s_len = dr2.shape[0]
    tm = _tile(s_len, 512)

    def body(dr2_ref, dg_ref, r1_ref, g1_ref, b1_ref, wg_ref, pw_ref, dh1_ref, wire_ref, own_ref, pl_ref,
             dwg_ref, land_ref, send_sem, recv_sem, xl_ref, x_send, x_recv, x_flush):
        i = pl.program_id(0)
        exchange = _ChipExchange(pw_ref, xl_ref, x_send, x_recv)

        @pl.when(i == 0)
        def _():
            exchange.start()
            dwg_ref[...] = jnp.zeros_like(dwg_ref)

        h1, _, _ = _ln(r1_ref[...], g1_ref[...], b1_ref[...])
        h1b = h1.astype(_MXU)
        dh1 = ALPHA * dr2_ref[...]
        for j in range(N_CHIP):
            dgj = dg_ref[j]
            dh1 = dh1 + _dot(dgj, wg_ref[j])
            dwg_ref[j * FF_SH:(j + 1) * FF_SH, :] += _dot_tn(dgj, h1b)
        dh1_ref[...] = dh1

        @pl.when(i == pl.num_programs(0) - 1)
        def _():
            _pair_reduce(dwg_ref, wire_ref, own_ref, land_ref, send_sem, recv_sem)
            exchange.finish_to(pl_ref, x_flush)

    c = _const2((1, D_MODEL))
    half = (N_CHIP, FF_SH // 2, D_MODEL)
    return pl.pallas_call(
        body, name="ffn_bwd_g", grid=(s_len // tm,),
        in_specs=[_rows(tm, D_MODEL), _ffn_spec(tm), _rows(tm, D_MODEL), c, c, _vmem(), _vmem()],
        out_specs=[_rows(tm, D_MODEL), _vmem(), _vmem(), _hbm()],
        out_shape=[_hbm_shape((s_len, D_MODEL), F32)] + _pair_out_shapes(half) + [_ChipExchange.land_shape(prev_wire)],
        scratch_shapes=_pair_scratch((D_FF, D_MODEL), half) + _ChipExchange.scratch(prev_wire),
        compiler_params=_params(58),
    )(dr2, dg, r1, g1, b1, wg, prev_wire)


def _ffn_bwd_u(dh1a, du, r1, g1, b1, wu, prev_wire):
    s_len = dh1a.shape[0]
    tm = _tile(s_len, 512)

    def body(dh1_ref, du_ref, r1_ref, g1_ref, b1_ref, wu_ref, pw_ref,
             dr1_ref, wire_ref, own_ref, dg1_ref, db1_ref, pl_ref,
             dwu_ref, land_ref, send_sem, recv_sem, xl_ref, x_send, x_recv, x_flush):
        i = pl.program_id(0)
        exchange = _ChipExchange(pw_ref, xl_ref, x_send, x_recv)

        @pl.when(i == 0)
        def _():
            exchange.start()
            dwu_ref[...] = jnp.zeros_like(dwu_ref)
            dg1_ref[...] = jnp.zeros_like(dg1_ref)
            db1_ref[...] = jnp.zeros_like(db1_ref)

        h1, r1hat, rstd1 = _ln(r1_ref[...], g1_ref[...], b1_ref[...])
        h1b = h1.astype(_MXU)
        dh1 = dh1_ref[...]
        for j in range(N_CHIP):
            duj = du_ref[j]
            dh1 = dh1 + _dot(duj, wu_ref[j])
            dwu_ref[j * FF_SH:(j + 1) * FF_SH, :] += _dot_tn(duj, h1b)
        dg1_ref[...] += _colsum(dh1 * r1hat)
        db1_ref[...] += _colsum(dh1)
        dr1_ref[...] = _ln_bwd(dh1, r1hat, rstd1, g1_ref[...])

        @pl.when(i == pl.num_programs(0) - 1)
        def _():
            _pair_reduce(dwu_ref, wire_ref, own_ref, land_ref, send_sem, recv_sem)
            exchange.finish_to(pl_ref, x_flush)

    vec = _hbm_shape((1, D_MODEL), F32)
    c = _const2((1, D_MODEL))
    half = (N_CHIP, FF_SH // 2, D_MODEL)
    return pl.pallas_call(
        body, name="ffn_bwd_u", grid=(s_len // tm,),
        in_specs=[_rows(tm, D_MODEL), _ffn_spec(tm), _rows(tm, D_MODEL), c, c, _vmem(), _vmem()],
        out_specs=[_rows(tm, D_MODEL), _vmem(), _vmem(), c, c, _hbm()],
        out_shape=[_hbm_shape((s_len, D_MODEL), F32)] + _pair_out_shapes(half)
        + [vec, vec, _ChipExchange.land_shape(prev_wire)],
        scratch_shapes=_pair_scratch((D_FF, D_MODEL), half) + _ChipExchange.scratch(prev_wire),
        compiler_params=_params(58),
    )(dh1a, du, r1, g1, b1, wu, prev_wire)


def _outproj_bwd(dr1, mc, w_out, prev_wire):
    s_len = dr1.shape[0]
    tm = _tile(s_len, 512)

    def body(dr1_ref, mc_ref, w_ref, pw_ref, dmc_ref, wire_ref, own_ref, db_ref, pl_ref,
             dw_ref, land_ref, send_sem, recv_sem, xl_ref, x_send, x_recv, x_flush):
        i = pl.program_id(0)
        exchange = _ChipExchange(pw_ref, xl_ref, x_send, x_recv)

        @pl.when(i == 0)
        def _():
            exchange.start()
            dw_ref[...] = jnp.zeros_like(dw_ref)
            db_ref[...] = jnp.zeros_like(db_ref)

        d = dr1_ref[...]
        db_ref[...] += _colsum(d)
        db16 = d.astype(_MXU)
        dmc_ref[...] = _dot_nt(db16, w_ref[...])
        dw_ref[...] += _dot_tn(mc_ref[...], db16)

        @pl.when(i == pl.num_programs(0) - 1)
        def _():
            _pair_reduce(dw_ref, wire_ref, own_ref, land_ref, send_sem, recv_sem)
            exchange.finish_to(pl_ref, x_flush)

    half = (N_CHIP, OUT_SH // 2, D_MODEL)
    return pl.pallas_call(
        body, name="outproj_bwd", grid=(s_len // tm,),
        in_specs=[_rows(tm, D_MODEL), _rows(tm, D_MODEL), _vmem(), _vmem()],
        out_specs=[_rows(tm, D_MODEL), _vmem(), _vmem(), _const2((1, D_MODEL)), _hbm()],
        out_shape=[_hbm_shape((s_len, D_MODEL), F32)] + _pair_out_shapes(half)
        + [_hbm_shape((1, D_MODEL), F32), _ChipExchange.land_shape(prev_wire)],
        scratch_shapes=_pair_scratch((D_MODEL, D_MODEL), half) + _ChipExchange.scratch(prev_wire),
        compiler_params=_params(48),
    )(dr1, mc, w_out, prev_wire)


def _mixer_bwd(q, k, v, su, sv, dmc, tc, t1, t2, sinks, sg, sb, sgu_w, sgu_bt, prev_wire):
    s_len = q.shape[0]
    nb = s_len // BLK

    def body(q_ref, kc_ref, kp_ref, vc_ref, vp_ref, su_ref, sv_ref, dmc_ref,
             tc_ref, t1_ref, t2_ref, tcp_ref, t1p_ref, t2p_ref,
             sink_ref, lg_ref, lb_ref, w_ref, bt_ref, pw_ref,
             dq_ref, dkv_ref, dsuv_ref, dbq_ref, dbkv_ref, dbsuv_ref,
             dsink_ref, dlg_ref, dlb_ref, dw_ref, dbt_ref, pl_ref, carry_ref, xl_ref, x_send, x_recv, x_flush):
        i = pl.program_id(0)
        exchange = _ChipExchange(pw_ref, xl_ref, x_send, x_recv)

        @pl.when(i == 0)
        def _():
            exchange.start()

        @pl.when(i == 0)
        def _():
            for r in (dbq_ref, dbkv_ref, dbsuv_ref, dsink_ref, dlg_ref, dlb_ref, dw_ref, dbt_ref):
                r[...] = jnp.zeros_like(r)

        def emit_kv(fin):
            dk = _rope_bwd(fin[:, 0:KV_W], tcp_ref[...], t1p_ref[...], t2p_ref[...])
            out = jnp.concatenate([dk, fin[:, KV_W:2 * KV_W]], axis=1)
            dkv_ref[...] = out.astype(_MXU)
            dbkv_ref[...] += _colsum(out)

        @pl.when(i < nb)
        def _():
            allowed_t = _band_mask_t(i == 0)
            kb = jnp.concatenate([kp_ref[...], kc_ref[...]], axis=0)
            vb = jnp.concatenate([vp_ref[...], vc_ref[...]], axis=0)
            qv = q_ref[...]
            dmc = dmc_ref[...]
            dqs, dks, dvs, dsinks = [], [], [], []
            allowed_g = jnp.tile(allowed_t, (1, Q_PER_KV))
            for g in range(N_KV):
                heads = range(g * Q_PER_KV, (g + 1) * Q_PER_KV)
                kh = kb[:, g * HEAD_DIM:(g + 1) * HEAD_DIM]
                vh = vb[:, g * HEAD_DIM:(g + 1) * HEAD_DIM]
                q_g = jnp.concatenate([qv[:, h * HEAD_DIM:(h + 1) * HEAD_DIM] for h in heads], axis=0)
                do_g = jnp.concatenate([dmc[:, h * HEAD_DIM:(h + 1) * HEAD_DIM] for h in heads], axis=0).astype(_MXU)
                sink_g = jnp.concatenate([jnp.full((1, BLK), sink_ref[h], F32) for h in heads], axis=1)
                probs_t, psink = _attn_probs_t(kh, q_g, sink_g, allowed_g)
                dvs.append(_dot(probs_t.astype(_MXU), do_g))
                dp_t = _dot_nt(vh, do_g)
                rd = jnp.sum(probs_t * dp_t, axis=0, keepdims=True)
                ds_t = (probs_t * (dp_t - rd)).astype(_MXU)
                ps_rd = psink * rd
                for hh in range(Q_PER_KV):
                    dsinks.append(-jnp.sum(ps_rd[:, hh * BLK:(hh + 1) * BLK], axis=1, keepdims=True))
                dq_g = _dot_tn(ds_t, kh)
                dqs += [dq_g[hh * BLK:(hh + 1) * BLK, :] for hh in range(Q_PER_KV)]
                dks.append(_dot(ds_t, q_g))
            dq = _rope_bwd(jnp.concatenate(dqs, axis=1) * (HEAD_DIM ** -0.5), tc_ref[...], t1_ref[...], t2_ref[...])
            dq_ref[...] = dq.astype(_MXU)
            dbq_ref[...] += _colsum(dq)
            dsink_ref[...] += _lane_put(dsinks, 128)
            contrib = jnp.concatenate(dks + dvs, axis=1)

            @pl.when(i > 0)
            def _():
                emit_kv(carry_ref[...] + contrib[0:BLK, :])

            carry_ref[...] = contrib[BLK:2 * BLK, :]

            lg = lg_ref[...]
            u, du_dsu = _gelu_and_grad(su_ref[...])
            gv, dgv_dsv = _gelu_and_grad(sv_ref[...])
            mixed, vhat, rstd, vvb, wcs = _sgu_mix(gv, lg, lb_ref[...], w_ref, bt_ref)
            dsgu = dmc[:, ATTN_W:D_MODEL]
            dsu = dsgu * mixed * du_dsu
            dmixed = dsgu * u
            tri_t = lax.broadcasted_iota(jnp.int32, (BLK, BLK), 0)
            tri_s = lax.broadcasted_iota(jnp.int32, (BLK, BLK), 1)
            dvv, dbs = [], []
            for h in range(N_GRP):
                dm = dmixed[:, h * GRP_DIM:(h + 1) * GRP_DIM]
                dmb = dm.astype(_MXU)
                dbs.append(jnp.sum(dm, axis=1, keepdims=True))
                dw_ref[h] += jnp.where(tri_s <= tri_t, _dot_nt(dmb, vvb[:, h * GRP_DIM:(h + 1) * GRP_DIM]), 0.0)
                dvv.append(_dot_tn(wcs[h], dmb))
            dvv = jnp.concatenate(dvv, axis=1)
            dbt_ref[...] += _lane_put(dbs, 128)
            dlg_ref[...] += _colsum(dvv * vhat)
            dlb_ref[...] += _colsum(dvv)
            dsv = _ln_bwd(dvv, vhat, rstd, lg) * dgv_dsv
            dsuv = jnp.concatenate([dsu, dsv], axis=1)
            dsuv_ref[...] = dsuv.astype(_MXU)
            dbsuv_ref[...] += _colsum(dsuv)

        @pl.when(i == nb)
        def _():
            emit_kv(carry_ref[...])
            exchange.finish_to(pl_ref, x_flush)

    last = nb - 1
    cur = lambda w: pl.BlockSpec((BLK, w), lambda i: (jnp.minimum(i, last), 0))
    prev = lambda w: pl.BlockSpec((BLK, w), lambda i: (jnp.clip(i - 1, 0, last), 0))
    sd = _hbm_shape
    return pl.pallas_call(
        body, name="mixer_bwd", grid=(nb + 1,),
        in_specs=[cur(ATTN_W), cur(KV_W), prev(KV_W), cur(KV_W), prev(KV_W), cur(SGU_W), cur(SGU_W), cur(D_MODEL),
                  cur(128), cur(128), cur(128), prev(128), prev(128), prev(128),
                  _smem(), _const2((1, SGU_W)), _const2((1, SGU_W)), _const2((N_GRP, BLK, BLK)), _const2((BLK, N_GRP)),
                  _vmem()],
        out_specs=[cur(ATTN_W), prev(2 * KV_W), cur(2 * SGU_W),
                   _const2((1, ATTN_W)), _const2((1, 2 * KV_W)), _const2((1, 2 * SGU_W)),
                   _const2((1, 128)), _const2((1, SGU_W)), _const2((1, SGU_W)),
                   _const2((N_GRP, BLK, BLK)), _const2((BLK, 128)), _hbm()],
        out_shape=[sd((s_len, ATTN_W), _MXU), sd((s_len, 2 * KV_W), _MXU), sd((s_len, 2 * SGU_W), _MXU),
                   sd((1, ATTN_W), F32), sd((1, 2 * KV_W), F32), sd((1, 2 * SGU_W), F32),
                   sd((1, 128), F32), sd((1, SGU_W), F32), sd((1, SGU_W), F32),
                   sd((N_GRP, BLK, BLK), F32), sd((BLK, 128), F32), _ChipExchange.land_shape(prev_wire)],
        scratch_shapes=[pltpu.VMEM((BLK, 2 * KV_W), F32)] + _ChipExchange.scratch(prev_wire),
        compiler_params=_params(32),
    )(q, k, k, v, v, su, sv, dmc, tc, t1, t2, tc, t1, t2, sinks, sg, sb, sgu_w, sgu_bt, prev_wire)


def _inproj_bwd(dq, dkv, dsuv, dr1, x, g0, b0, w_in):
    s_len = x.shape[0]
    tm = _tile(s_len, 512)
    cuts = ((0, ATTN_W), (ATTN_W, ATTN_W + 2 * KV_W), (ATTN_W + 2 * KV_W, IN_W))

    def body(dq_ref, dkv_ref, dsuv_ref, dr1_ref, x_ref, g_ref, b_ref, w_ref, dx_ref, dw_ref, dg_ref, db_ref):
        i = pl.program_id(0)

        @pl.when(i == 0)
        def _():
            dw_ref[...] = jnp.zeros_like(dw_ref)
            dg_ref[...] = jnp.zeros_like(dg_ref)
            db_ref[...] = jnp.zeros_like(db_ref)

        h0, xhat, rstd = _ln(x_ref[...], g_ref[...], b_ref[...])
        h0b = h0.astype(_MXU)
        dh0 = ALPHA * dr1_ref[...]
        for (lo, hi), d_ref in zip(cuts, (dq_ref, dkv_ref, dsuv_ref)):
            d = d_ref[...]
            dh0 = dh0 + _dot(d, w_ref[lo:hi, :])
            dw_ref[lo:hi, :] += _dot_tn(d, h0b)
        dg_ref[...] += _colsum(dh0 * xhat)
        db_ref[...] += _colsum(dh0)
        dx_ref[...] = _ln_bwd(dh0, xhat, rstd, g_ref[...])

    vec = _hbm_shape((1, D_MODEL), F32)
    c = _const2((1, D_MODEL))
    return pl.pallas_call(
        body, name="inproj_bwd", grid=(s_len // tm,),
        in_specs=[_rows(tm, ATTN_W), _rows(tm, 2 * KV_W), _rows(tm, 2 * SGU_W), _rows(tm, D_MODEL), _rows(tm, D_MODEL),
                  c, c, _vmem()],
        out_specs=[_rows(tm, D_MODEL), _vmem(), c, c],
        out_shape=[_hbm_shape((s_len, D_MODEL), F32), jax.ShapeDtypeStruct((IN_W, D_MODEL), F32), vec, vec],
        compiler_params=_params(48),
    )(dq, dkv, dsuv, dr1, x, g0, b0, w_in)


def _place():
    x, y, c = (lax.axis_index(a) for a in MESH_AXES)
    chips = [(1 - x, y), (x, 1 - y), (1 - x, 1 - y)]
    return x, y, c, chips


class _Gather:
    def __init__(self, ins, outs, send_sems, recv_sems, spans=None):
        self.ins, self.outs, self.send_sems, self.recv_sems = ins, outs, send_sems, recv_sems
        self.n = len(ins)
        self.spans = spans or [(0, r.shape[0]) for r in ins]
        self.halves = [(hi - lo) // 2 for lo, hi in self.spans]

    def _copy(self, k, t, slot, half, to):
        rows = pl.ds(pl.multiple_of(self.spans[t][0] + half * self.halves[t], 16), self.halves[t])
        piece = self.outs[t].at[slot, rows, :]
        return pltpu.make_async_remote_copy(src_ref=piece, dst_ref=piece, send_sem=self.send_sems.at[k],
                                            recv_sem=self.recv_sems.at[k], device_id=to, device_id_type=MESH)

    def _chip_copy(self, t, d, slot):
        x, y, c, chips = _place()
        return self._copy(3 * t + d, t, slot, c, (chips[d][0], chips[d][1], c))

    def _pass_copy(self, t, d, half):
        x, y, c, chips = _place()
        return self._copy(3 * self.n + 3 * t + d, t, 2 * chips[d][0] + chips[d][1], half, (x, y, 1 - c))

    def start(self):
        x, y, c, chips = _place()
        me = 2 * x + y
        for t in range(self.n):
            lo, hi = self.spans[t]
            self.outs[t][me, lo:hi, :] = self.ins[t][lo:hi, :].astype(_WIRE)
        for t in range(self.n):
            for d in range(3):
                self._chip_copy(t, d, me).start()

    def pass_on(self):
        x, y, c, chips = _place()
        for t in range(self.n):
            for d in range(3):
                self._chip_copy(t, d, 2 * chips[d][0] + chips[d][1]).wait_recv()
                self._pass_copy(t, d, c).start()

    def finish(self):
        x, y, c, chips = _place()
        me = 2 * x + y
        for t in range(self.n):
            for d in range(3):
                self._pass_copy(t, d, 1 - c).wait_recv()
        for t in range(self.n):
            for d in range(3):
                self._chip_copy(t, d, me).wait_send()
                self._pass_copy(t, d, c).wait_send()

    @staticmethod
    def out_shapes(shards, make=jax.ShapeDtypeStruct):
        return [make((N_CHIP,) + s.shape, _WIRE) for s in shards]

    @staticmethod
    def sems(n):
        return [pltpu.SemaphoreType.DMA((6 * n,)), pltpu.SemaphoreType.DMA((6 * n,))]


class _GatherPlan:
    def __init__(self, pieces):
        self.shards = [p[0] for p in pieces]
        self.spans = [p[1] for p in pieces]
        self.earlier = [p[2] for p in pieces]
        self.n = len(pieces)
        self.carried = [t for t in range(self.n) if self.earlier[t] is not None]

    def operands(self):
        return self.shards + [self.earlier[t] for t in self.carried]

    def in_specs(self):
        return [_vmem()] * self.n + [_hbm()] * len(self.carried)

    def out_specs(self):
        return [_hbm()] * self.n

    def out_shapes(self):
        return _Gather.out_shapes(self.shards, _hbm_shape)

    def scratch(self):
        return ([pltpu.VMEM((N_CHIP,) + s.shape, _WIRE) for s in self.shards] + _Gather.sems(self.n)
                + [pltpu.SemaphoreType.DMA((self.n,)), pltpu.SemaphoreType.DMA((max(len(self.carried), 1),))])

    def bind(self, in_refs, out_refs, scratch_refs):
        plan = self
        shard_refs, earlier_refs = in_refs[:self.n], in_refs[self.n:]
        bufs = scratch_refs[:self.n]
        send_sems, recv_sems, flush_sems, carry_sems = scratch_refs[self.n:self.n + 4]
        gather = _Gather(shard_refs, bufs, send_sems, recv_sems, self.spans)

        def carry_copy(k):
            t = plan.carried[k]
            lo = plan.spans[t][0]
            return pltpu.make_async_copy(earlier_refs[k].at[:, 0:lo, :], bufs[t].at[:, 0:lo, :], carry_sems.at[k])

        class Bound:
            @staticmethod
            def start():
                for k in range(len(plan.carried)):
                    carry_copy(k).start()
                gather.start()

            @staticmethod
            def pass_on():
                gather.pass_on()

            @staticmethod
            def finish():
                gather.finish()
                for k in range(len(plan.carried)):
                    carry_copy(k).wait()
                _flush([bufs[t].at[:, 0:plan.spans[t][1], :] for t in range(plan.n)],
                       [out_refs[t].at[:, 0:plan.spans[t][1], :] for t in range(plan.n)], flush_sems)

        return Bound


def _flush(bufs, hbm_outs, sems):
    copies = [pltpu.make_async_copy(b, o, sems.at[k]) for k, (b, o) in enumerate(zip(bufs, hbm_outs))]
    for cp in copies:
        cp.start()
    for cp in copies:
        cp.wait()


def _gather_weights(shards):
    n = len(shards)

    def body(*refs):
        gather = _Gather(refs[:n], refs[n:2 * n], refs[2 * n], refs[2 * n + 1])
        gather.start()
        gather.pass_on()
        gather.finish()

    return pl.pallas_call(
        body, name="gather_weights",
        in_specs=[_vmem()] * n, out_specs=[_vmem()] * n,
        out_shape=_Gather.out_shapes(shards), scratch_shapes=_Gather.sems(n),
        compiler_params=pltpu.CompilerParams(vmem_limit_bytes=32 * MIB),
    )(*shards)


class _ChipExchange:
    def __init__(self, wire_ref, land_ref, send_sems, recv_sems):
        self.wire, self.land, self.send_sems, self.recv_sems = wire_ref, land_ref, send_sems, recv_sems

    def _copy(self, d):
        x, y, c, chips = _place()
        return pltpu.make_async_remote_copy(
            src_ref=self.wire.at[2 * chips[d][0] + chips[d][1]], dst_ref=self.land.at[d],
            send_sem=self.send_sems.at[d], recv_sem=self.recv_sems.at[d],
            device_id=(chips[d][0], chips[d][1], c), device_id_type=MESH)

    def start(self):
        for d in range(3):
            self._copy(d).start()

    def wait_recv(self):
        for d in range(3):
            self._copy(d).wait_recv()

    def wait_send(self):
        for d in range(3):
            self._copy(d).wait_send()

    def finish_to(self, hbm_out, flush_sem):
        self.wait_recv()
        _flush([self.land], [hbm_out], flush_sem)
        self.wait_send()

    @staticmethod
    def land_shape(wire):
        return _hbm_shape((3,) + wire.shape[1:], wire.dtype)

    @staticmethod
    def sems():
        return [pltpu.SemaphoreType.DMA((3,)), pltpu.SemaphoreType.DMA((3,))]

    @staticmethod
    def scratch(wire):
        return ([pltpu.VMEM((3,) + wire.shape[1:], wire.dtype)] + _ChipExchange.sems() + [pltpu.SemaphoreType.DMA((1,))])


def _pair_out_shapes(half_shape):
    return [jax.ShapeDtypeStruct(half_shape, _WIRE), jax.ShapeDtypeStruct(half_shape[1:], F32)]


def _pair_scratch(acc_shape, half_shape):
    return [pltpu.VMEM(acc_shape, F32), pltpu.VMEM(half_shape, _WIRE),
            pltpu.SemaphoreType.DMA((N_CHIP,)), pltpu.SemaphoreType.DMA((N_CHIP,))]


def _pair_reduce(acc_ref, wire_ref, own_ref, land_ref, send_sems, recv_sems):
    rh = land_ref.shape[1]
    x, y, c, _ = _place()
    me = 2 * x + y
    copies = []
    for j in range(N_CHIP):
        def cast(r, carry, j=j):
            dst = pl.ds(pl.multiple_of(r * ROW_CHUNK, ROW_CHUNK), ROW_CHUNK)
            src = pl.ds(pl.multiple_of((2 * j + 1 - c) * rh + r * ROW_CHUNK, 8), ROW_CHUNK)
            wire_ref[j, dst, :] = acc_ref[src, :].astype(_WIRE)
            return carry

        lax.fori_loop(0, rh // ROW_CHUNK, cast, 0)
        cp = pltpu.make_async_remote_copy(src_ref=wire_ref.at[j], dst_ref=land_ref.at[j], send_sem=send_sems.at[j],
                                          recv_sem=recv_sems.at[j], device_id=(x, y, 1 - c), device_id_type=MESH)
        cp.start()
        copies.append(cp)
    for j in range(N_CHIP):
        copies[j].wait()

        def chunk(r, carry, j=j):
            theirs = pl.ds(pl.multiple_of(r * ROW_CHUNK, ROW_CHUNK), ROW_CHUNK)
            mine = pl.ds(pl.multiple_of((2 * j + c) * rh + r * ROW_CHUNK, 8), ROW_CHUNK)
            wire_ref[j, theirs, :] = (acc_ref[mine, :] + land_ref[j, theirs, :].astype(F32)).astype(_WIRE)
            return carry

        lax.fori_loop(0, rh // ROW_CHUNK, chunk, 0)

    def own_chunk(r, carry):
        theirs = pl.ds(pl.multiple_of(r * ROW_CHUNK, ROW_CHUNK), ROW_CHUNK)
        mine = pl.ds(pl.multiple_of((2 * me + c) * rh + r * ROW_CHUNK, 8), ROW_CHUNK)
        own_ref[theirs, :] = acc_ref[mine, :] + land_ref[me, theirs, :].astype(F32)
        return carry

    lax.fori_loop(0, rh // ROW_CHUNK, own_chunk, 0)


def _grad_finish(last_acc, lands, owns):
    n = len(owns) + 1
    halves = [last_acc.shape[0] // (2 * N_CHIP)] + [w.shape[1] for w in lands]
    widths = [last_acc.shape[1]] + [a.shape[1] for a in owns]

    def body(*refs):
        acc0, land, own, g = refs[0], (None,) + refs[1:n], (None,) + refs[n:2 * n - 1], refs[2 * n - 1:3 * n - 1]
        pland0, wire0, land0, own0 = refs[3 * n - 1:3 * n + 3]
        p_send, p_recv, x_send, x_recv, pair_send, pair_recv = refs[3 * n + 3:3 * n + 9]
        land = (land0,) + land[1:]
        own = (own0,) + own[1:]
        x, y, c, chips = _place()
        me = 2 * x + y
        exchange = _ChipExchange(wire0, land0, x_send, x_recv)

        def half_rows(t, half):
            return pl.ds(pl.multiple_of(half * halves[t], 8), halves[t])

        def pair_copy(t, half):
            rows = g[t].at[half_rows(t, half), :]
            return pltpu.make_async_remote_copy(src_ref=rows, dst_ref=rows, send_sem=pair_send.at[t],
                                                recv_sem=pair_recv.at[t], device_id=(x, y, 1 - c), device_id_type=MESH)

        _pair_reduce(acc0, wire0, own0, pland0, p_send, p_recv)
        exchange.start()

        for t in list(range(1, n)) + [0]:
            if t == 0:
                exchange.wait_recv()

            def chunk(r, carry, t=t):
                src = pl.ds(pl.multiple_of(r * ROW_CHUNK, ROW_CHUNK), ROW_CHUNK)
                dst = pl.ds(pl.multiple_of(c * halves[t] + r * ROW_CHUNK, 8), ROW_CHUNK)
                s = own[t][src, :]
                for d in range(3):
                    s = s + land[t][d, src, :].astype(F32)
                g[t][dst, :] = s
                return carry

            lax.fori_loop(0, halves[t] // ROW_CHUNK, chunk, 0)
            pair_copy(t, c).start()
        for t in range(n):
            pair_copy(t, 1 - c).wait_recv()
        for t in range(n):
            pair_copy(t, c).wait_send()
        exchange.wait_send()

    half0 = (halves[0], widths[0])
    return pl.pallas_call(
        body, name="grad_finish",
        in_specs=[_vmem()] * (2 * n - 1), out_specs=[_vmem()] * n,
        out_shape=[jax.ShapeDtypeStruct((2 * h, w), F32) for h, w in zip(halves, widths)],
        scratch_shapes=[pltpu.VMEM((N_CHIP,) + half0, _WIRE), pltpu.VMEM((N_CHIP,) + half0, _WIRE),
                        pltpu.VMEM((3,) + half0, _WIRE), pltpu.VMEM(half0, F32)]
        + [pltpu.SemaphoreType.DMA((N_CHIP,)), pltpu.SemaphoreType.DMA((N_CHIP,))]
        + _ChipExchange.sems()
        + [pltpu.SemaphoreType.DMA((n,)), pltpu.SemaphoreType.DMA((n,))],
        compiler_params=pltpu.CompilerParams(vmem_limit_bytes=56 * MIB),
    )(last_acc, *lands, *owns)


_SMALL = ("ln_in_g", "ln_in_b", "b_in", "attn_sinks", "sgu_ln_g", "sgu_ln_b", "sgu_w", "sgu_b", "b_out",
          "ln_mix_g", "ln_mix_b", "ln_ffn_g", "ln_ffn_b")
_VEC_ROW = dict(ln_in_g=0, ln_in_b=1, b_in=2, attn_sinks=4, sgu_ln_g=5, sgu_ln_b=6, b_out=7, ln_mix_g=8, ln_mix_b=9,
                ln_ffn_g=10, ln_ffn_b=11)
_LOSS_ROW = 12
_VEC_ROWS = 16
_MAT_ROWS = N_GRP * BLK + BLK


def _small_allreduce(local):
    n_in = 16

    def body(*refs):
        (g_ln_in_g, g_ln_in_b, g_bq, g_bkv, g_bsuv, g_sink, g_sln_g, g_sln_b, g_sw, g_sbt, g_bout,
         g_lmg, g_lmb, g_lfg, g_lfb, g_loss) = refs[:n_in]
        out_a, out_b = refs[n_in:n_in + 2]
        (buf_a, buf_b, pair_a, pair_b, stage_a, stage_b, tot_a, tot_b,
         p1_send, p1_recv, x_send, x_recv, p2_send, p2_recv) = refs[n_in + 2:]
        x, y, c, chips = _place()
        me = 2 * x + y
        sibling = (x, y, 1 - c)
        half_a, half_b = _VEC_ROWS // 2, _MAT_ROWS // 2

        buf_a[...] = jnp.zeros_like(buf_a)
        for row, ref in ((0, g_ln_in_g), (1, g_ln_in_b), (7, g_bout), (8, g_lmg), (9, g_lmb), (10, g_lfg), (11, g_lfb),
                         (_LOSS_ROW, g_loss)):
            buf_a[row:row + 1, :] = ref[...]
        buf_a[2:3, 0:ATTN_W] = g_bq[...]
        buf_a[2:3, ATTN_W:ATTN_W + 2 * KV_W] = g_bkv[...]
        buf_a[2:3, ATTN_W + 2 * KV_W:D_MODEL] = g_bsuv[:, 0:2 * KV_W]
        buf_a[3:4, 0:2 * SGU_W - 2 * KV_W] = g_bsuv[:, 2 * KV_W:2 * SGU_W]
        buf_a[4:5, 0:128] = g_sink[...]
        buf_a[5:6, 0:SGU_W] = g_sln_g[...]
        buf_a[6:7, 0:SGU_W] = g_sln_b[...]
        for h in range(N_GRP):
            buf_b[h * BLK:(h + 1) * BLK, :] = g_sw[h]
        buf_b[N_GRP * BLK:_MAT_ROWS, :] = g_sbt[...]

        def remote(src, dst, send_sem, recv_sem, to):
            return pltpu.make_async_remote_copy(src_ref=src, dst_ref=dst, send_sem=send_sem, recv_sem=recv_sem,
                                                device_id=to, device_id_type=MESH)

        first = [remote(buf_a, pair_a, p1_send.at[0], p1_recv.at[0], sibling),
                 remote(buf_b, pair_b, p1_send.at[1], p1_recv.at[1], sibling)]
        for cp in first:
            cp.start()
        for cp in first:
            cp.wait()
        rows_a = pl.ds(pl.multiple_of(c * half_a, 8), half_a)
        rows_b = pl.ds(pl.multiple_of(c * half_b, 8), half_b)
        stage_a[me] = buf_a[rows_a, :] + pair_a[rows_a, :]
        stage_b[me] = buf_b[rows_b, :] + pair_b[rows_b, :]

        def chip_copies(d):
            to = (chips[d][0], chips[d][1], c)
            return [remote(stage_a.at[me], stage_a.at[me], x_send.at[2 * d], x_recv.at[2 * d], to),
                    remote(stage_b.at[me], stage_b.at[me], x_send.at[2 * d + 1], x_recv.at[2 * d + 1], to)]

        def chip_arrivals(d):
            slot = 2 * chips[d][0] + chips[d][1]
            to = (chips[d][0], chips[d][1], c)
            return [remote(stage_a.at[slot], stage_a.at[slot], x_send.at[2 * d], x_recv.at[2 * d], to),
                    remote(stage_b.at[slot], stage_b.at[slot], x_send.at[2 * d + 1], x_recv.at[2 * d + 1], to)]

        for d in range(3):
            for cp in chip_copies(d):
                cp.start()
        for d in range(3):
            for cp in chip_arrivals(d):
                cp.wait_recv()
        tot_a[rows_a, :] = ((stage_a[0] + stage_a[1]) + stage_a[2]) + stage_a[3]
        tot_b[rows_b, :] = ((stage_b[0] + stage_b[1]) + stage_b[2]) + stage_b[3]

        second = [remote(tot_a.at[rows_a, :], tot_a.at[rows_a, :], p2_send.at[0], p2_recv.at[0], sibling),
                  remote(tot_b.at[rows_b, :], tot_b.at[rows_b, :], p2_send.at[1], p2_recv.at[1], sibling)]
        for cp in second:
            cp.start()
        other_a = pl.ds(pl.multiple_of((1 - c) * half_a, 8), half_a)
        other_b = pl.ds(pl.multiple_of((1 - c) * half_b, 8), half_b)
        remote(tot_a.at[other_a, :], tot_a.at[other_a, :], p2_send.at[0], p2_recv.at[0], sibling).wait_recv()
        remote(tot_b.at[other_b, :], tot_b.at[other_b, :], p2_send.at[1], p2_recv.at[1], sibling).wait_recv()
        for cp in second:
            cp.wait_send()
        for d in range(3):
            for cp in chip_copies(d):
                cp.wait_send()
        out_a[...] = tot_a[...]
        out_b[...] = tot_b[...]

    ins = [local[k] for k in ("ln_in_g", "ln_in_b", "bq", "bkv", "bsuv", "sink", "sgu_ln_g", "sgu_ln_b", "sgu_w",
                              "sgu_bt", "b_out", "ln_mix_g", "ln_mix_b", "ln_ffn_g", "ln_ffn_b", "loss")]
    out_dims = [(_VEC_ROWS, D_MODEL), (_MAT_ROWS, 128)]
    vec = pltpu.VMEM((_VEC_ROWS, D_MODEL), F32)
    mat = pltpu.VMEM((_MAT_ROWS, 128), F32)
    return pl.pallas_call(
        body, name="small_allreduce", grid=(1,),
        in_specs=[_const2(a.shape) for a in ins], out_specs=[_const2(s) for s in out_dims],
        out_shape=[_hbm_shape(s, F32) for s in out_dims],
        scratch_shapes=[vec, mat, vec, mat, pltpu.VMEM((N_CHIP, _VEC_ROWS // 2, D_MODEL), F32),
                        pltpu.VMEM((N_CHIP, _MAT_ROWS // 2, 128), F32), vec, mat,
                        pltpu.SemaphoreType.DMA((2,)), pltpu.SemaphoreType.DMA((2,)), pltpu.SemaphoreType.DMA((6,)),
                        pltpu.SemaphoreType.DMA((6,)), pltpu.SemaphoreType.DMA((2,)), pltpu.SemaphoreType.DMA((2,))],
        compiler_params=pltpu.CompilerParams(vmem_limit_bytes=32 * MIB),
    )(*ins)


def _small_adamw(tot_a, tot_b, params):
    shapes = [params[nm][0].shape for nm in _SMALL]

    def body(*refs):
        ta, tb = refs[:2]
        prm = refs[2:2 + 3 * len(_SMALL)]
        outs = refs[2 + 3 * len(_SMALL):]

        def grad_of(k, name):
            if name == "sgu_w":
                return [tb[h * BLK:(h + 1) * BLK, :] for h in range(N_GRP)]
            if name == "sgu_b":
                return jnp.transpose(tb[N_GRP * BLK:_MAT_ROWS, :])[0:N_GRP, :]
            row = _VEC_ROW[name]
            if name == "b_in":
                return jnp.concatenate([ta[row:row + 1, :], ta[row + 1:row + 2, 0:IN_W - D_MODEL]], axis=1)
            return ta[row:row + 1, 0:shapes[k][-1]]

        for k, name in enumerate(_SMALL):
            w_ref, m_ref, v_ref = prm[3 * k:3 * k + 3]
            g_out, d_out, m_out, v_out = outs[4 * k:4 * k + 4]
            g = grad_of(k, name)
            if name == "sgu_w":
                for h in range(N_GRP):
                    d_, m_, v_ = _adamw_math(w_ref[h], g[h], m_ref[h], v_ref[h])
                    g_out[h], d_out[h], m_out[h], v_out[h] = g[h], d_, m_, v_
            else:
                d_, m_, v_ = _adamw_math(w_ref[...], g, m_ref[...], v_ref[...])
                g_out[...], d_out[...], m_out[...], v_out[...] = g, d_, m_, v_
        outs[-1][...] = ta[_LOSS_ROW:_LOSS_ROW + 1, :]

    ins = [tot_a, tot_b] + [_in_hbm(a) for nm in _SMALL for a in params[nm]]
    out_dims = [s for s in shapes for _ in range(4)] + [(1, D_MODEL)]
    res = pl.pallas_call(
        body, name="small_adamw", grid=(1,),
        in_specs=[_const2(a.shape) for a in ins], out_specs=[_const2(s) for s in out_dims],
        out_shape=[_hbm_shape(s, F32) for s in out_dims],
        compiler_params=_params(32),
    )(*ins)
    return {nm: tuple(res[4 * k:4 * k + 4]) for k, nm in enumerate(_SMALL)}, res[-1]


def _elementwise(name, fn, ins, out_dtypes, tile_rows=256):
    shape = ins[0].shape
    lead = shape[:-2]
    rows, cols = shape[-2:]
    tr = _tile(rows, tile_rows)
    n_lead = math.prod(lead)
    nr = rows // tr
    flat = [_in_hbm(a.reshape((n_lead, rows, cols))) for a in ins]

    def body(*refs):
        outs = fn(*[r[0] for r in refs[:len(ins)]])
        for o_ref, o in zip(refs[len(ins):], outs):
            o_ref[0] = o.astype(o_ref.dtype)

    spec = pl.BlockSpec((1, tr, cols), lambda i: (i // nr, i % nr, 0))
    res = pl.pallas_call(
        body, name=name, grid=(n_lead * nr,),
        in_specs=[spec] * len(ins), out_specs=[spec] * len(out_dtypes),
        out_shape=[_hbm_shape((n_lead, rows, cols), dt) for dt in out_dtypes],
        compiler_params=_params(32),
    )(*flat)
    return [r.reshape(shape) for r in res]


def _adamw_math(w, g, m, v):
    m = ADAM_B1 * m + (1.0 - ADAM_B1) * g
    v = ADAM_B2 * v + (1.0 - ADAM_B2) * (g * g)
    m_hat = m / (1.0 - ADAM_B1 ** ADAM_STEP)
    v_hat = v / (1.0 - ADAM_B2 ** ADAM_STEP)
    delta = -ADAM_LR * (m_hat / (jnp.sqrt(v_hat) + ADAM_EPS) + ADAM_WD * w)
    return delta, m, v


def _adamw(name, w, g, m, v, tile_rows=256):
    return _elementwise(name, lambda w_, g_, m_, v_: (g_,) + _adamw_math(w_, g_, m_, v_), [w, g, m, v],
                        [F32, F32, F32, F32], tile_rows)


def kernel(x, positions, ln_in_g, ln_in_b, w_in, b_in, attn_sinks, sgu_ln_g, sgu_ln_b, sgu_w, sgu_b, w_out, b_out, ln_mix_g, ln_mix_b, w_gate, w_up, w_down, ln_ffn_g, ln_ffn_b, loss_target, m_ln_in_g, m_ln_in_b, m_w_in, m_b_in, m_attn_sinks, m_sgu_ln_g, m_sgu_ln_b, m_sgu_w, m_sgu_b, m_w_out, m_b_out, m_ln_mix_g, m_ln_mix_b, m_w_gate, m_w_up, m_w_down, m_ln_ffn_g, m_ln_ffn_b, v_ln_in_g, v_ln_in_b, v_w_in, v_b_in, v_attn_sinks, v_sgu_ln_g, v_sgu_ln_b, v_sgu_w, v_sgu_b, v_w_out, v_b_out, v_ln_mix_g, v_ln_mix_b, v_w_gate, v_w_up, v_w_down, v_ln_ffn_g, v_ln_ffn_b):
    weights = dict(ln_in_g=ln_in_g, ln_in_b=ln_in_b, w_in=w_in, b_in=b_in, attn_sinks=attn_sinks, sgu_ln_g=sgu_ln_g,
                   sgu_ln_b=sgu_ln_b, sgu_w=sgu_w, sgu_b=sgu_b, w_out=w_out, b_out=b_out, ln_mix_g=ln_mix_g,
                   ln_mix_b=ln_mix_b, w_gate=w_gate, w_up=w_up, w_down=w_down, ln_ffn_g=ln_ffn_g, ln_ffn_b=ln_ffn_b)
    mom_m = dict(ln_in_g=m_ln_in_g, ln_in_b=m_ln_in_b, w_in=m_w_in, b_in=m_b_in, attn_sinks=m_attn_sinks,
                 sgu_ln_g=m_sgu_ln_g, sgu_ln_b=m_sgu_ln_b, sgu_w=m_sgu_w, sgu_b=m_sgu_b, w_out=m_w_out, b_out=m_b_out,
                 ln_mix_g=m_ln_mix_g, ln_mix_b=m_ln_mix_b, w_gate=m_w_gate, w_up=m_w_up, w_down=m_w_down,
                 ln_ffn_g=m_ln_ffn_g, ln_ffn_b=m_ln_ffn_b)
    mom_v = dict(ln_in_g=v_ln_in_g, ln_in_b=v_ln_in_b, w_in=v_w_in, b_in=v_b_in, attn_sinks=v_attn_sinks,
                 sgu_ln_g=v_sgu_ln_g, sgu_ln_b=v_sgu_ln_b, sgu_w=v_sgu_w, sgu_b=v_sgu_b, w_out=v_w_out, b_out=v_b_out,
                 ln_mix_g=v_ln_mix_g, ln_mix_b=v_ln_mix_b, w_gate=v_w_gate, w_up=v_w_up, w_down=v_w_down,
                 ln_ffn_g=v_ln_ffn_g, ln_ffn_b=v_ln_ffn_b)
    order = list(weights)
    big = ("w_in", "w_out", "w_gate", "w_up", "w_down")

    s_len = x.shape[1]
    xs = _in_hbm(x.reshape(s_len, D_MODEL))
    tgt = _in_hbm(loss_target.reshape(s_len, D_MODEL))
    pos_col = _in_hbm(positions.reshape(s_len, 1))
    g0, b0 = _in_hbm(ln_in_g.reshape(1, D_MODEL)), _in_hbm(ln_in_b.reshape(1, D_MODEL))
    sinks = attn_sinks.reshape(N_Q)
    sgu_w3 = _in_hbm(sgu_w.reshape(N_GRP, BLK, BLK))
    sgu_bt = _in_hbm(sgu_b.reshape(N_GRP, BLK).T)
    b_in, b_out, sgu_ln_g, sgu_ln_b, ln_mix_g, ln_mix_b, ln_ffn_g, ln_ffn_b = (
        _in_hbm(a) for a in (b_in, b_out, sgu_ln_g, sgu_ln_b, ln_mix_g, ln_mix_b, ln_ffn_g, ln_ffn_b))

    col_sharded = ("w_in", "w_gate", "w_up")

    def rowmajor(name, a):
        return jnp.swapaxes(a[0], 0, 1) if name in col_sharded else a[0]

    def as_given(name, a):
        return (jnp.swapaxes(a, 0, 1) if name in col_sharded else a)[None]

    shards = [rowmajor(n, weights[n]) for n in big]
    (gw_in,) = _gather_weights(shards[0:1])
    w_in_full = gw_in.reshape(IN_W, D_MODEL)

    sh_out, sh_gate, sh_up, sh_down = shards[1:]
    cut = GATHER_CUT
    *acts, gw_out, gw_gate0 = _ln_inproj(xs, pos_col, g0, b0, w_in_full, b_in, _GatherPlan(
        [(sh_out, (0, OUT_SH), None), (sh_gate, (0, cut), None)]))
    q, k, v, su, sv, tc, t1, t2 = (_in_hbm(a) for a in acts)
    mc, gw_gate, gw_up0 = _mixer_fwd(q, k, v, su, sv, sinks, sgu_ln_g, sgu_ln_b, sgu_w3, sgu_bt, _GatherPlan(
        [(sh_gate, (cut, FF_SH), gw_gate0), (sh_up, (0, cut), None)]))
    mc = _in_hbm(mc)
    w_out_full = gw_out.reshape(D_MODEL, D_MODEL)
    r1, gw_up = _outproj(mc, w_out_full, b_out, xs, g0, b0, _GatherPlan([(sh_up, (cut, FF_SH), gw_up0)]))
    r1 = _in_hbm(r1)
    act, p_act, q_act, gw_down = _ffn_up(r1, ln_mix_g, ln_mix_b, gw_gate, gw_up,
                                         _GatherPlan([(sh_down, (0, FF_SH), None)]))
    act, p_act, q_act = _in_hbm(act), _in_hbm(p_act), _in_hbm(q_act)
    dr2, loss_cols, d_ln_ffn_g, d_ln_ffn_b = _ffn_down_loss(act, gw_down, r1, ln_mix_g, ln_mix_b, ln_ffn_g, ln_ffn_b, tgt)
    dr2 = _in_hbm(dr2)

    dg, du, wire_down, own_down = _ffn_bwd_a(dr2, act, p_act, q_act, gw_down)
    dh1a, wire_gate, own_gate, land_down = _ffn_bwd_g(dr2, _in_hbm(dg), r1, ln_mix_g, ln_mix_b, gw_gate, wire_down)
    dr1, wire_up, own_up, d_ln_mix_g, d_ln_mix_b, land_gate = _ffn_bwd_u(_in_hbm(dh1a), _in_hbm(du), r1, ln_mix_g,
                                                                         ln_mix_b, gw_up, wire_gate)
    dr1 = _in_hbm(dr1)
    dmc, wire_out, own_out, d_b_out, land_up = _outproj_bwd(dr1, mc, w_out_full, wire_up)
    (dq, dkv, dsuv, dbq, dbkv, dbsuv, d_sink, d_sgu_ln_g, d_sgu_ln_b, d_sgu_w, d_sgu_bt, land_out) = _mixer_bwd(
        q, k, v, su, sv, _in_hbm(dmc), tc, t1, t2, sinks, sgu_ln_g, sgu_ln_b, sgu_w3, sgu_bt, wire_out)
    grad_x, acc_in, d_ln_in_g, d_ln_in_b = _inproj_bwd(_in_hbm(dq), _in_hbm(dkv), _in_hbm(dsuv), dr1, xs, g0, b0,
                                                       w_in_full)

    reduced = _grad_finish(acc_in, [land_out, land_gate, land_up, land_down], [own_out, own_gate, own_up, own_down])
    small_shape = dict(ln_in_g=(1, D_MODEL), ln_in_b=(1, D_MODEL), sgu_w=(N_GRP, BLK, BLK), sgu_b=(N_GRP, BLK))
    small_local = dict(
        ln_in_g=d_ln_in_g, ln_in_b=d_ln_in_b, bq=dbq, bkv=dbkv, bsuv=dbsuv, sink=d_sink, sgu_ln_g=d_sgu_ln_g,
        sgu_ln_b=d_sgu_ln_b, sgu_w=d_sgu_w, sgu_bt=d_sgu_bt, b_out=d_b_out, ln_mix_g=d_ln_mix_g, ln_mix_b=d_ln_mix_b,
        ln_ffn_g=d_ln_ffn_g, ln_ffn_b=d_ln_ffn_b, loss=loss_cols)
    small_params = {nm: tuple(src[nm].reshape(small_shape.get(nm, src[nm].shape)) for src in (weights, mom_m, mom_v))
                    for nm in _SMALL}
    tot_a, tot_b = _small_allreduce({nm: _in_hbm(a) for nm, a in small_local.items()})
    small_out, loss_sum = _small_adamw(_in_hbm(tot_a), _in_hbm(tot_b), small_params)
    loss = jnp.sum(loss_sum) * (0.5 / D_MODEL)
    grads, delta, new_m, new_v = {}, {}, {}, {}
    for nm in _SMALL:
        grads[nm], delta[nm], new_m[nm], new_v[nm] = (a.reshape(weights[nm].shape) for a in small_out[nm])

    for t, name in enumerate(big):
        g_, d_, m_, v_ = _adamw("adamw_" + name, shards[t], reduced[t], rowmajor(name, mom_m[name]),
                                rowmajor(name, mom_v[name]))
        grads[name], delta[name], new_m[name], new_v[name] = (as_given(name, a) for a in (g_, d_, m_, v_))

    return (loss, grad_x.reshape(x.shape), *[grads[n] for n in order], *[delta[n] for n in order],
            *[new_m[n] for n in order], *[new_v[n] for n in order])
```

```python
import functools
import math

import jax
import jax.numpy as jnp
from jax import lax
from jax.experimental import pallas as pl
from jax.experimental.pallas import tpu as pltpu

F32 = jnp.float32
_MXU = jnp.bfloat16
_WIRE = jnp.bfloat16
_ACT = jnp.bfloat16

D_MODEL = 1024
ATTN_W = 512
SGU_W = 512
HEAD_DIM = 64
N_Q = 8
N_KV = 2
Q_PER_KV = 4
KV_W = 128
BLK = 128
ROT_DIM = 16
ROPE_THETA = 500000.0
N_GRP = 4
GRP_DIM = 128
D_FF = 2816
IN_W = 1792
LN_EPS = 1e-5
ALPHA = 2.0 ** 0.25
N_CHIP = 4
FF_SH = D_FF // N_CHIP
IN_SH = IN_W // N_CHIP
OUT_SH = D_MODEL // N_CHIP
ROW_CHUNK = 32
GATHER_CUT = 224

ADAM_LR = 0.001
ADAM_B1 = 0.9
ADAM_B2 = 0.999
ADAM_EPS = 1e-08
ADAM_WD = 0.01
ADAM_STEP = 10

SQRT_HALF = 0.7071067811865476
INV_SQRT_2PI = 0.3989422804014327
MESH_AXES = ("x", "y", "c")
MESH = pl.DeviceIdType.MESH
MIB = 2 ** 20


def _vmem():
    return pl.BlockSpec(memory_space=pltpu.VMEM)


def _smem():
    return pl.BlockSpec(memory_space=pltpu.SMEM)


def _hbm():
    return pl.BlockSpec(memory_space=pl.ANY)


def _hbm_shape(shape, dtype):
    return pltpu.HBM(shape, dtype)


def _in_hbm(a):
    return pltpu.with_memory_space_constraint(a, pltpu.HBM)


def _params(vmem_mib=48):
    return pltpu.CompilerParams(dimension_semantics=("arbitrary",), vmem_limit_bytes=vmem_mib * MIB)


def _tile(n, cap):
    if n <= cap:
        return n
    for t in range(cap - cap % 16, 0, -16):
        if n % t == 0:
            return t
    raise ValueError((n, cap))


def _rows(tm, width):
    return pl.BlockSpec((tm, width), lambda i: (i, 0))


def _const2(shape):
    return pl.BlockSpec(shape, lambda i: (0,) * len(shape))


def _ln(x, g, b):
    mu = jnp.mean(x, axis=-1, keepdims=True)
    xc = x - mu
    var = jnp.mean(xc * xc, axis=-1, keepdims=True)
    rstd = lax.rsqrt(var + LN_EPS)
    xhat = xc * rstd
    return xhat * g + b, xhat, rstd


def _ln_bwd(dy, xhat, rstd, g):
    gdy = dy * g
    m1 = jnp.mean(gdy, axis=-1, keepdims=True)
    m2 = jnp.mean(gdy * xhat, axis=-1, keepdims=True)
    return rstd * (gdy - m1 - xhat * m2)


def _colsum(a):
    return jnp.sum(a, axis=0, keepdims=True)


def _gelu_and_grad(x):
    cdf = 0.5 * (1.0 + lax.erf(x * SQRT_HALF))
    return x * cdf, cdf + x * jnp.exp(-0.5 * x * x) * INV_SQRT_2PI


def _dot(a, b):
    return jnp.dot(a, b, preferred_element_type=F32)


def _dot_nt(a, b):
    return lax.dot_general(a, b, (((1,), (1,)), ((), ())), preferred_element_type=F32)


def _dot_tn(a, b):
    return lax.dot_general(a, b, (((0,), (0,)), ((), ())), preferred_element_type=F32)


def _rope(t, tc, t1, t2):
    n = t.shape[1]
    rep = n // 128
    if rep > 1:
        tc, t1, t2 = (jnp.tile(a, (1, rep)) for a in (tc, t1, t2))
    return t * tc + pltpu.roll(t, n - 8, 1) * t1 + pltpu.roll(t, 8, 1) * t2


def _rope_bwd(d, tc, t1, t2):
    n = d.shape[1]
    rep = n // 128
    if rep > 1:
        tc, t1, t2 = (jnp.tile(a, (1, rep)) for a in (tc, t1, t2))
    return d * tc + pltpu.roll(d * t1, 8, 1) + pltpu.roll(d * t2, n - 8, 1)


def _causal_w(w_ref, h):
    t = lax.broadcasted_iota(jnp.int32, (BLK, BLK), 0)
    s = lax.broadcasted_iota(jnp.int32, (BLK, BLK), 1)
    return jnp.where(s <= t, w_ref[h], 0.0)


def _lane_put(vals, width):
    rows = vals[0].shape[0]
    lane = lax.broadcasted_iota(jnp.int32, (rows, width), 1)
    out = jnp.zeros((rows, width), F32)
    for k, v in enumerate(vals):
        out = out + jnp.where(lane == k, v, 0.0)
    return out


def _rope_consts():
    lane = jnp.arange(128) % HEAD_DIM
    rot = lane < ROT_DIM
    pair = (2 * (lane % (ROT_DIM // 2))).astype(F32)
    freq = jnp.where(rot, ROPE_THETA ** (-pair / ROT_DIM), 0.0)
    rows = [freq, rot.astype(F32), 1.0 - rot.astype(F32), (lane < ROT_DIM // 2).astype(F32),
            jnp.logical_and(lane >= ROT_DIM // 2, rot).astype(F32)]
    rows += [jnp.zeros((128,), F32)] * 3
    return jnp.stack(rows).astype(F32)


def _ln_inproj(x, pos_col, g0, b0, w_in, b_in, plan):
    s_len = x.shape[0]
    tm = _tile(s_len, 512)
    m, n = len(plan.operands()), plan.n

    def body(x_ref, pos_ref, g_ref, b_ref, w_ref, bi_ref, rc_ref, *rest):
        q_ref, k_ref, v_ref, su_ref, sv_ref, tc_ref, t1_ref, t2_ref = rest[m:m + 8]
        gather = plan.bind(rest[:m], rest[m + 8:m + 8 + n], rest[m + 8 + n:])
        i = pl.program_id(0)

        @pl.when(i == 0)
        def _():
            gather.start()

        h0, _, _ = _ln(x_ref[...], g_ref[...], b_ref[...])
        proj = _dot_nt(h0.astype(_MXU), w_ref[...]) + bi_ref[...]
        ang = pos_ref[...].astype(F32) * rc_ref[0:1, :]
        cs = jnp.cos(ang)
        sn = jnp.sin(ang)
        tc = cs * rc_ref[1:2, :] + rc_ref[2:3, :]
        t1 = -sn * rc_ref[3:4, :]
        t2 = sn * rc_ref[4:5, :]
        tc_ref[...] = tc
        t1_ref[...] = t1
        t2_ref[...] = t2
        q = _rope(proj[:, 0:ATTN_W], tc, t1, t2) * (HEAD_DIM ** -0.5)
        q_ref[...] = q.astype(_MXU)
        k_ref[...] = _rope(proj[:, ATTN_W:ATTN_W + KV_W], tc, t1, t2).astype(_MXU)
        v_ref[...] = proj[:, ATTN_W + KV_W:ATTN_W + 2 * KV_W].astype(_MXU)
        su_ref[...] = proj[:, ATTN_W + 2 * KV_W:ATTN_W + 2 * KV_W + SGU_W]
        sv_ref[...] = proj[:, ATTN_W + 2 * KV_W + SGU_W:IN_W]

        last = pl.num_programs(0) - 1

        @pl.when(i == jnp.maximum(last - 1, 0))
        def _():
            gather.pass_on()

        @pl.when(i == last)
        def _():
            gather.finish()

    sd = _hbm_shape
    return pl.pallas_call(
        body, name="ln_inproj", grid=(s_len // tm,),
        in_specs=[_rows(tm, D_MODEL), _rows(tm, 1), _const2((1, D_MODEL)), _const2((1, D_MODEL)), _vmem(),
                  _const2((1, IN_W)), _const2((8, 128))] + plan.in_specs(),
        out_specs=[_rows(tm, ATTN_W), _rows(tm, KV_W), _rows(tm, KV_W), _rows(tm, SGU_W), _rows(tm, SGU_W),
                   _rows(tm, 128), _rows(tm, 128), _rows(tm, 128)] + plan.out_specs(),
        out_shape=[sd((s_len, ATTN_W), _MXU), sd((s_len, KV_W), _MXU), sd((s_len, KV_W), _MXU),
                   sd((s_len, SGU_W), F32), sd((s_len, SGU_W), F32),
                   sd((s_len, 128), F32), sd((s_len, 128), F32), sd((s_len, 128), F32)] + plan.out_shapes(),
        scratch_shapes=plan.scratch(),
        compiler_params=_params(56),
    )(x, pos_col, g0, b0, w_in, b_in, _rope_consts(), *plan.operands())


def _band_mask_t(first_block):
    kj = lax.broadcasted_iota(jnp.int32, (2 * BLK, BLK), 0)
    qi = lax.broadcasted_iota(jnp.int32, (2 * BLK, BLK), 1)
    shut = jnp.where(first_block, 2 * BLK, 0)
    prev_ok = jnp.logical_and(kj < BLK, kj > qi + shut)
    cur_ok = jnp.logical_and(kj >= BLK, (kj - BLK) <= qi)
    return jnp.logical_or(prev_ok, cur_ok)


def _attn_probs_t(kh, qh, sink, allowed_t):
    s = jnp.where(allowed_t, _dot_nt(kh, qh), -1e30)
    m = jnp.maximum(jnp.max(s, axis=0, keepdims=True), sink)
    p = jnp.exp(s - m)
    ps = jnp.exp(sink - m)
    inv = 1.0 / (jnp.sum(p, axis=0, keepdims=True) + ps)
    return p * inv, ps * inv


def _sgu_mix(gv, lg, lb, w_ref, bt_ref):
    vv, vhat, rstd = _ln(gv, lg, lb)
    vvb = vv.astype(_MXU)
    wcs, mixed = [], []
    for h in range(N_GRP):
        wc = _causal_w(w_ref, h).astype(_MXU)
        wcs.append(wc)
        mixed.append(_dot(wc, vvb[:, h * GRP_DIM:(h + 1) * GRP_DIM]) + bt_ref[:, h:h + 1])
    return jnp.concatenate(mixed, axis=1), vhat, rstd, vvb, wcs


def _mixer_fwd(q, k, v, su, sv, sinks, sg, sb, sgu_w, sgu_bt, plan):
    s_len = q.shape[0]
    nb = s_len // BLK
    per = 2 if nb % 2 == 0 else 1
    steps = nb // per
    m, n = len(plan.operands()), plan.n

    def body(q_ref, kc_ref, kp_ref, vc_ref, vp_ref, su_ref, sv_ref, sink_ref, lg_ref, lb_ref, w_ref, bt_ref, *rest):
        mc_ref = rest[m]
        gather = plan.bind(rest[:m], rest[m + 1:m + 1 + n], rest[m + 1 + n:])
        i = pl.program_id(0)

        @pl.when(i == 0)
        def _():
            gather.start()

        @pl.when(i == max(steps - 2, 0))
        def _():
            gather.pass_on()

        @pl.when(i == steps - 1)
        def _():
            gather.finish()

        for s in range(per):
            rows = slice(s * BLK, (s + 1) * BLK)
            before = slice((s - 1) * BLK, s * BLK)
            k_prev = kp_ref[...] if s == 0 else kc_ref[before, :]
            v_prev = vp_ref[...] if s == 0 else vc_ref[before, :]
            allowed_t = _band_mask_t(i == 0 if s == 0 else False)
            kb = jnp.concatenate([k_prev, kc_ref[rows, :]], axis=0)
            vb = jnp.concatenate([v_prev, vc_ref[rows, :]], axis=0)
            qv = q_ref[rows, :]
            outs = []
            allowed_g = jnp.tile(allowed_t, (1, Q_PER_KV))
            for g in range(N_KV):
                heads = range(g * Q_PER_KV, (g + 1) * Q_PER_KV)
                kh = kb[:, g * HEAD_DIM:(g + 1) * HEAD_DIM]
                vh = vb[:, g * HEAD_DIM:(g + 1) * HEAD_DIM]
                q_g = jnp.concatenate([qv[:, h * HEAD_DIM:(h + 1) * HEAD_DIM] for h in heads], axis=0)
                sink_g = jnp.concatenate([jnp.full((1, BLK), sink_ref[h], F32) for h in heads], axis=1)
                probs_t, _ = _attn_probs_t(kh, q_g, sink_g, allowed_g)
                o_g = _dot_tn(probs_t.astype(_MXU), vh)
                outs += [o_g[hh * BLK:(hh + 1) * BLK, :] for hh in range(Q_PER_KV)]
            u = _gelu_and_grad(su_ref[rows, :])[0]
            gv = _gelu_and_grad(sv_ref[rows, :])[0]
            mixed = _sgu_mix(gv, lg_ref[...], lb_ref[...], w_ref, bt_ref)[0]
            mc_ref[rows, :] = jnp.concatenate(outs + [u * mixed], axis=1).astype(_MXU)

    cur = lambda w: pl.BlockSpec((per * BLK, w), lambda i: (i, 0))
    prev = lambda w: pl.BlockSpec((BLK, w), lambda i: (jnp.maximum(per * i - 1, 0), 0))
    return pl.pallas_call(
        body, name="mixer_fwd", grid=(steps,),
        in_specs=[cur(ATTN_W), cur(KV_W), prev(KV_W), cur(KV_W), prev(KV_W), cur(SGU_W), cur(SGU_W), _smem(),
                  _const2((1, SGU_W)), _const2((1, SGU_W)), _const2((N_GRP, BLK, BLK)), _const2((BLK, N_GRP))]
        + plan.in_specs(),
        out_specs=[cur(D_MODEL)] + plan.out_specs(),
        out_shape=[_hbm_shape((s_len, D_MODEL), _MXU)] + plan.out_shapes(),
        scratch_shapes=plan.scratch(),
        compiler_params=_params(56),
    )(q, k, k, v, v, su, sv, sinks, sg, sb, sgu_w, sgu_bt, *plan.operands())


def _outproj(mc, w_out, b_out, x, g0, b0, plan):
    s_len = x.shape[0]
    tm = _tile(s_len, 512)
    m, n = len(plan.operands()), plan.n

    def body(mc_ref, w_ref, bo_ref, x_ref, g_ref, b_ref, *rest):
        r1_ref = rest[m]
        gather = plan.bind(rest[:m], rest[m + 1:m + 1 + n], rest[m + 1 + n:])
        i = pl.program_id(0)

        @pl.when(i == 0)
        def _():
            gather.start()

        h0, _, _ = _ln(x_ref[...], g_ref[...], b_ref[...])
        r1_ref[...] = ALPHA * h0 + (_dot(mc_ref[...], w_ref[...]) + bo_ref[...])

        last = pl.num_programs(0) - 1

        @pl.when(i == jnp.maximum(last - 1, 0))
        def _():
            gather.pass_on()

        @pl.when(i == last)
        def _():
            gather.finish()

    return pl.pallas_call(
        body, name="outproj", grid=(s_len // tm,),
        in_specs=[_rows(tm, D_MODEL), _vmem(), _const2((1, D_MODEL)), _rows(tm, D_MODEL),
                  _const2((1, D_MODEL)), _const2((1, D_MODEL))] + plan.in_specs(),
        out_specs=[_rows(tm, D_MODEL)] + plan.out_specs(),
        out_shape=[_hbm_shape((s_len, D_MODEL), F32)] + plan.out_shapes(),
        scratch_shapes=plan.scratch(),
        compiler_params=_params(40),
    )(mc, w_out, b_out, x, g0, b0, *plan.operands())


def _ffn_spec(tm):
    return pl.BlockSpec((N_CHIP, tm, FF_SH), lambda i: (0, i, 0))


def _ffn_up(r1, g1, b1, wg, wu, plan):
    s_len = r1.shape[0]
    tm = _tile(s_len, 512)
    m, n = len(plan.operands()), plan.n

    def body(r1_ref, g_ref, b_ref, wg_ref, wu_ref, *rest):
        a_ref, p_ref, q_ref = rest[m:m + 3]
        gather = plan.bind(rest[:m], rest[m + 3:m + 3 + n], rest[m + 3 + n:])
        i = pl.program_id(0)

        @pl.when(i == 0)
        def _():
            gather.start()

        h1, _, _ = _ln(r1_ref[...], g_ref[...], b_ref[...])
        h1b = h1.astype(_MXU)
        for j in range(N_CHIP):
            g = _dot_nt(h1b, wg_ref[j])
            u = _dot_nt(h1b, wu_ref[j])
            silu, sg = _silu_parts(g)
            a_ref[j] = (silu * u).astype(_MXU)
            p_ref[j] = silu.astype(_ACT)
            q_ref[j] = (u * (sg * (1.0 + g * (1.0 - sg)))).astype(_ACT)

        last = pl.num_programs(0) - 1

        @pl.when(i == jnp.maximum(last - 1, 0))
        def _():
            gather.pass_on()

        @pl.when(i == last)
        def _():
            gather.finish()

    sd = _hbm_shape((N_CHIP, s_len, FF_SH), _ACT)
    return pl.pallas_call(
        body, name="ffn_up", grid=(s_len // tm,),
        in_specs=[_rows(tm, D_MODEL), _const2((1, D_MODEL)), _const2((1, D_MODEL)), _vmem(), _vmem()] + plan.in_specs(),
        out_specs=[_ffn_spec(tm)] * 3 + plan.out_specs(),
        out_shape=[_hbm_shape((N_CHIP, s_len, FF_SH), _MXU), sd, sd] + plan.out_shapes(),
        scratch_shapes=plan.scratch(),
        compiler_params=_params(56),
    )(r1, g1, b1, wg, wu, *plan.operands())


def _silu_parts(g):
    sg = 1.0 / (1.0 + jnp.exp(-g))
    return g * sg, sg


def _ffn_down_loss(act, wd, r1, g1, b1, g2, b2, target):
    s_len = r1.shape[0]
    tm = _tile(s_len, 512)

    parts = 2 if tm % 32 == 0 else 1
    sub = tm // parts

    def body(a_ref, wd_ref, r1_ref, g1_ref, b1_ref, g2_ref, b2_ref, t_ref, dr2_ref, loss_ref, dg2_ref, db2_ref):
        i = pl.program_id(0)

        @pl.when(i == 0)
        def _():
            loss_ref[...] = jnp.zeros_like(loss_ref)
            dg2_ref[...] = jnp.zeros_like(dg2_ref)
            db2_ref[...] = jnp.zeros_like(db2_ref)

        for part in range(parts):
            rows = slice(part * sub, (part + 1) * sub)
            f = jnp.zeros((sub, D_MODEL), F32)
            for j in range(N_CHIP):
                f = f + _dot(a_ref[j, rows, :], wd_ref[j])
            h1, _, _ = _ln(r1_ref[rows, :], g1_ref[...], b1_ref[...])
            h2, r2hat, rstd2 = _ln(ALPHA * h1 + f, g2_ref[...], b2_ref[...])
            diff = h2 - t_ref[rows, :]
            dh2 = diff * (1.0 / D_MODEL)
            loss_ref[...] += _colsum(diff * diff)
            dg2_ref[...] += _colsum(dh2 * r2hat)
            db2_ref[...] += _colsum(dh2)
            dr2_ref[rows, :] = _ln_bwd(dh2, r2hat, rstd2, g2_ref[...])

    vec = _hbm_shape((1, D_MODEL), F32)
    c = _const2((1, D_MODEL))
    return pl.pallas_call(
        body, name="ffn_down_loss", grid=(s_len // tm,),
        in_specs=[_ffn_spec(tm), _vmem(), _rows(tm, D_MODEL), c, c, c, c, _rows(tm, D_MODEL)],
        out_specs=[_rows(tm, D_MODEL), c, c, c],
        out_shape=[_hbm_shape((s_len, D_MODEL), F32), vec, vec, vec],
        compiler_params=_params(48),
    )(act, wd, r1, g1, b1, g2, b2, target)


def _ffn_bwd_a(dr2, act, p_act, q_act, wd):
    s_len = dr2.shape[0]
    tm = _tile(s_len, 512)

    def body(dr2_ref, a_ref, p_ref, q_ref, wd_ref, dg_ref, du_ref, wire_ref, own_ref,
             dwd_ref, land_ref, send_sem, recv_sem):
        i = pl.program_id(0)

        @pl.when(i == 0)
        def _():
            dwd_ref[...] = jnp.zeros_like(dwd_ref)

        dfb = dr2_ref[...].astype(_MXU)
        for j in range(N_CHIP):
            da = _dot_nt(dfb, wd_ref[j])
            dg_ref[j] = (da * q_ref[j].astype(F32)).astype(_MXU)
            du_ref[j] = (da * p_ref[j].astype(F32)).astype(_MXU)
            dwd_ref[j * FF_SH:(j + 1) * FF_SH, :] += _dot_tn(a_ref[j], dfb)

        @pl.when(i == pl.num_programs(0) - 1)
        def _():
            _pair_reduce(dwd_ref, wire_ref, own_ref, land_ref, send_sem, recv_sem)

    sd = _hbm_shape((N_CHIP, s_len, FF_SH), _MXU)
    half = (N_CHIP, FF_SH // 2, D_MODEL)
    return pl.pallas_call(
        body, name="ffn_bwd_a", grid=(s_len // tm,),
        in_specs=[_rows(tm, D_MODEL), _ffn_spec(tm), _ffn_spec(tm), _ffn_spec(tm), _vmem()],
        out_specs=[_ffn_spec(tm), _ffn_spec(tm), _vmem(), _vmem()],
        out_shape=[sd, sd] + _pair_out_shapes(half),
        scratch_shapes=_pair_scratch((D_FF, D_MODEL), half),
        compiler_params=_params(61),
    )(dr2, act, p_act, q_act, wd)


def _ffn_bwd_g(dr2, dg, r1, g1, b1, wg, prev_wire):
    s_len = dr2.shape[0]
    tm = _tile(s_len, 512)

    def body(dr2_ref, dg_ref, r1_ref, g1_ref, b1_ref, wg_ref, pw_ref, dh1_ref, wire_ref, own_ref, pl_ref,
             dwg_ref, land_ref, send_sem, recv_sem, xl_ref, x_send, x_recv, x_flush):
        i = pl.program_id(0)
        exchange = _ChipExchange(pw_ref, xl_ref, x_send, x_recv)

        @pl.when(i == 0)
        def _():
            exchange.start()
            dwg_ref[...] = jnp.zeros_like(dwg_ref)

        h1, _, _ = _ln(r1_ref[...], g1_ref[...], b1_ref[...])
        h1b = h1.astype(_MXU)
        dh1 = ALPHA * dr2_ref[...]
        for j in range(N_CHIP):
            dgj = dg_ref[j]
            dh1 = dh1 + _dot(dgj, wg_ref[j])
            dwg_ref[j * FF_SH:(j + 1) * FF_SH, :] += _dot_tn(dgj, h1b)
        dh1_ref[...] = dh1

        @pl.when(i == pl.num_programs(0) - 1)
        def _():
            _pair_reduce(dwg_ref, wire_ref, own_ref, land_ref, send_sem, recv_sem)
            exchange.finish_to(pl_ref, x_flush)

    c = _const2((1, D_MODEL))
    half = (N_CHIP, FF_SH // 2, D_MODEL)
    return pl.pallas_call(
        body, name="ffn_bwd_g", grid=(s_len // tm,),
        in_specs=[_rows(tm, D_MODEL), _ffn_spec(tm), _rows(tm, D_MODEL), c, c, _vmem(), _vmem()],
        out_specs=[_rows(tm, D_MODEL), _vmem(), _vmem(), _hbm()],
        out_shape=[_hbm_shape((s_len, D_MODEL), F32)] + _pair_out_shapes(half) + [_ChipExchange.land_shape(prev_wire)],
        scratch_shapes=_pair_scratch((D_FF, D_MODEL), half) + _ChipExchange.scratch(prev_wire),
        compiler_params=_params(58),
    )(dr2, dg, r1, g1, b1, wg, prev_wire)


def _ffn_bwd_u(dh1a, du, r1, g1, b1, wu, prev_wire):
    s_len = dh1a.shape[0]
    tm = _tile(s_len, 512)

    def body(dh1_ref, du_ref, r1_ref, g1_ref, b1_ref, wu_ref, pw_ref,
             dr1_ref, wire_ref, own_ref, dg1_ref, db1_ref, pl_ref,
             dwu_ref, land_ref, send_sem, recv_sem, xl_ref, x_send, x_recv, x_flush):
        i = pl.program_id(0)
        exchange = _ChipExchange(pw_ref, xl_ref, x_send, x_recv)

        @pl.when(i == 0)
        def _():
            exchange.start()
            dwu_ref[...] = jnp.zeros_like(dwu_ref)
            dg1_ref[...] = jnp.zeros_like(dg1_ref)
            db1_ref[...] = jnp.zeros_like(db1_ref)

        h1, r1hat, rstd1 = _ln(r1_ref[...], g1_ref[...], b1_ref[...])
        h1b = h1.astype(_MXU)
        dh1 = dh1_ref[...]
        for j in range(N_CHIP):
            duj = du_ref[j]
            dh1 = dh1 + _dot(duj, wu_ref[j])
            dwu_ref[j * FF_SH:(j + 1) * FF_SH, :] += _dot_tn(duj, h1b)
        dg1_ref[...] += _colsum(dh1 * r1hat)
        db1_ref[...] += _colsum(dh1)
        dr1_ref[...] = _ln_bwd(dh1, r1hat, rstd1, g1_ref[...])

        @pl.when(i == pl.num_programs(0) - 1)
        def _():
            _pair_reduce(dwu_ref, wire_ref, own_ref, land_ref, send_sem, recv_sem)
            exchange.finish_to(pl_ref, x_flush)

    vec = _hbm_shape((1, D_MODEL), F32)
    c = _const2((1, D_MODEL))
    half = (N_CHIP, FF_SH // 2, D_MODEL)
    return pl.pallas_call(
        body, name="ffn_bwd_u", grid=(s_len // tm,),
        in_specs=[_rows(tm, D_MODEL), _ffn_spec(tm), _rows(tm, D_MODEL), c, c, _vmem(), _vmem()],
        out_specs=[_rows(tm, D_MODEL), _vmem(), _vmem(), c, c, _hbm()],
        out_shape=[_hbm_shape((s_len, D_MODEL), F32)] + _pair_out_shapes(half)
        + [vec, vec, _ChipExchange.land_shape(prev_wire)],
        scratch_shapes=_pair_scratch((D_FF, D_MODEL), half) + _ChipExchange.scratch(prev_wire),
        compiler_params=_params(58),
    )(dh1a, du, r1, g1, b1, wu, prev_wire)


def _outproj_bwd(dr1, mc, w_out, prev_wire):
    s_len = dr1.shape[0]
    tm = _tile(s_len, 512)

    def body(dr1_ref, mc_ref, w_ref, pw_ref, dmc_ref, wire_ref, own_ref, db_ref, pl_ref,
             dw_ref, land_ref, send_sem, recv_sem, xl_ref, x_send, x_recv, x_flush):
        i = pl.program_id(0)
        exchange = _ChipExchange(pw_ref, xl_ref, x_send, x_recv)

        @pl.when(i == 0)
        def _():
            exchange.start()
            dw_ref[...] = jnp.zeros_like(dw_ref)
            db_ref[...] = jnp.zeros_like(db_ref)

        d = dr1_ref[...]
        db_ref[...] += _colsum(d)
        db16 = d.astype(_MXU)
        dmc_ref[...] = _dot_nt(db16, w_ref[...])
        dw_ref[...] += _dot_tn(mc_ref[...], db16)

        @pl.when(i == pl.num_programs(0) - 1)
        def _():
            _pair_reduce(dw_ref, wire_ref, own_ref, land_ref, send_sem, recv_sem)
            exchange.finish_to(pl_ref, x_flush)

    half = (N_CHIP, OUT_SH // 2, D_MODEL)
    return pl.pallas_call(
        body, name="outproj_bwd", grid=(s_len // tm,),
        in_specs=[_rows(tm, D_MODEL), _rows(tm, D_MODEL), _vmem(), _vmem()],
        out_specs=[_rows(tm, D_MODEL), _vmem(), _vmem(), _const2((1, D_MODEL)), _hbm()],
        out_shape=[_hbm_shape((s_len, D_MODEL), F32)] + _pair_out_shapes(half)
        + [_hbm_shape((1, D_MODEL), F32), _ChipExchange.land_shape(prev_wire)],
        scratch_shapes=_pair_scratch((D_MODEL, D_MODEL), half) + _ChipExchange.scratch(prev_wire),
        compiler_params=_params(48),
    )(dr1, mc, w_out, prev_wire)


def _mixer_bwd(q, k, v, su, sv, dmc, tc, t1, t2, sinks, sg, sb, sgu_w, sgu_bt, prev_wire):
    s_len = q.shape[0]
    nb = s_len // BLK
    per = 2 if nb % 2 == 0 else 1
    steps = nb // per

    def body(q_ref, kc_ref, kp_ref, vc_ref, vp_ref, su_ref, sv_ref, dmc_ref,
             tc_ref, t1_ref, t2_ref, tcp_ref, t1p_ref, t2p_ref,
             sink_ref, lg_ref, lb_ref, w_ref, bt_ref, pw_ref,
             dq_ref, dkv_ref, dsuv_ref, dbq_ref, dbkv_ref, dbsuv_ref,
             dsink_ref, dlg_ref, dlb_ref, dw_ref, dbt_ref, pl_ref, carry_ref, xl_ref, x_send, x_recv, x_flush):
        i = pl.program_id(0)
        exchange = _ChipExchange(pw_ref, xl_ref, x_send, x_recv)

        @pl.when(i == 0)
        def _():
            exchange.start()

        @pl.when(i == 0)
        def _():
            for r in (dbq_ref, dbkv_ref, dbsuv_ref, dsink_ref, dlg_ref, dlb_ref, dw_ref, dbt_ref, carry_ref):
                r[...] = jnp.zeros_like(r)

        def emit_kv(fin, t):
            if t == 0:
                tables = (tcp_ref[...], t1p_ref[...], t2p_ref[...])
            else:
                before = slice((t - 1) * BLK, t * BLK)
                tables = (tc_ref[before, :], t1_ref[before, :], t2_ref[before, :])
            dk = _rope_bwd(fin[:, 0:KV_W], *tables)
            out = jnp.concatenate([dk, fin[:, KV_W:2 * KV_W]], axis=1)
            dkv_ref[t * BLK:(t + 1) * BLK, :] = out.astype(_MXU)
            dbkv_ref[...] += _colsum(out)

        def one_block(s):
            rows = slice(s * BLK, (s + 1) * BLK)
            before = slice((s - 1) * BLK, s * BLK)
            k_prev = kp_ref[...] if s == 0 else kc_ref[before, :]
            v_prev = vp_ref[...] if s == 0 else vc_ref[before, :]
            allowed_t = _band_mask_t(i == 0 if s == 0 else False)
            kb = jnp.concatenate([k_prev, kc_ref[rows, :]], axis=0)
            vb = jnp.concatenate([v_prev, vc_ref[rows, :]], axis=0)
            qv = q_ref[rows, :]
            dmc = dmc_ref[rows, :]
            dqs, dks, dvs, dsinks = [], [], [], []
            allowed_g = jnp.tile(allowed_t, (1, Q_PER_KV))
            for g in range(N_KV):
                heads = range(g * Q_PER_KV, (g + 1) * Q_PER_KV)
                kh = kb[:, g * HEAD_DIM:(g + 1) * HEAD_DIM]
                vh = vb[:, g * HEAD_DIM:(g + 1) * HEAD_DIM]
                q_g = jnp.concatenate([qv[:, h * HEAD_DIM:(h + 1) * HEAD_DIM] for h in heads], axis=0)
                do_g = jnp.concatenate([dmc[:, h * HEAD_DIM:(h + 1) * HEAD_DIM] for h in heads], axis=0).astype(_MXU)
                sink_g = jnp.concatenate([jnp.full((1, BLK), sink_ref[h], F32) for h in heads], axis=1)
                probs_t, psink = _attn_probs_t(kh, q_g, sink_g, allowed_g)
                dvs.append(_dot(probs_t.astype(_MXU), do_g))
                dp_t = _dot_nt(vh, do_g)
                rd = jnp.sum(probs_t * dp_t, axis=0, keepdims=True)
                ds_t = (probs_t * (dp_t - rd)).astype(_MXU)
                ps_rd = psink * rd
                for hh in range(Q_PER_KV):
                    dsinks.append(-jnp.sum(ps_rd[:, hh * BLK:(hh + 1) * BLK], axis=1, keepdims=True))
                dq_g = _dot_tn(ds_t, kh)
                dqs += [dq_g[hh * BLK:(hh + 1) * BLK, :] for hh in range(Q_PER_KV)]
                dks.append(_dot(ds_t, q_g))
            dq = _rope_bwd(jnp.concatenate(dqs, axis=1) * (HEAD_DIM ** -0.5),
                           tc_ref[rows, :], t1_ref[rows, :], t2_ref[rows, :])
            dq_ref[rows, :] = dq.astype(_MXU)
            dbq_ref[...] += _colsum(dq)
            dsink_ref[...] += _lane_put(dsinks, 128)
            contrib = jnp.concatenate(dks + dvs, axis=1)

            lg = lg_ref[...]
            u, du_dsu = _gelu_and_grad(su_ref[rows, :])
            gv, dgv_dsv = _gelu_and_grad(sv_ref[rows, :])
            mixed, vhat, rstd, vvb, wcs = _sgu_mix(gv, lg, lb_ref[...], w_ref, bt_ref)
            dsgu = dmc[:, ATTN_W:D_MODEL]
            dsu = dsgu * mixed * du_dsu
            dmixed = dsgu * u
            tri_t = lax.broadcasted_iota(jnp.int32, (BLK, BLK), 0)
            tri_s = lax.broadcasted_iota(jnp.int32, (BLK, BLK), 1)
            dvv, dbs = [], []
            for h in range(N_GRP):
                dm = dmixed[:, h * GRP_DIM:(h + 1) * GRP_DIM]
                dmb = dm.astype(_MXU)
                dbs.append(jnp.sum(dm, axis=1, keepdims=True))
                dw_ref[h] += jnp.where(tri_s <= tri_t, _dot_nt(dmb, vvb[:, h * GRP_DIM:(h + 1) * GRP_DIM]), 0.0)
                dvv.append(_dot_tn(wcs[h], dmb))
            dvv = jnp.concatenate(dvv, axis=1)
            dbt_ref[...] += _lane_put(dbs, 128)
            dlg_ref[...] += _colsum(dvv * vhat)
            dlb_ref[...] += _colsum(dvv)
            dsv = _ln_bwd(dvv, vhat, rstd, lg) * dgv_dsv
            dsuv = jnp.concatenate([dsu, dsv], axis=1)
            dsuv_ref[rows, :] = dsuv.astype(_MXU)
            dbsuv_ref[...] += _colsum(dsuv)
            return contrib

        @pl.when(i < steps)
        def _():
            contribs = [one_block(s) for s in range(per)]
            for t in range(per):
                top = carry_ref[...] if t == 0 else contribs[t - 1][BLK:2 * BLK, :]
                emit_kv(top + contribs[t][0:BLK, :], t)
            carry_ref[...] = contribs[per - 1][BLK:2 * BLK, :]

        @pl.when(i == steps)
        def _():
            emit_kv(carry_ref[...], 0)
            if per > 1:
                dkv_ref[BLK:per * BLK, :] = jnp.zeros(((per - 1) * BLK, 2 * KV_W), _MXU)
            exchange.finish_to(pl_ref, x_flush)

    last = steps - 1
    cur = lambda w: pl.BlockSpec((per * BLK, w), lambda i: (jnp.minimum(i, last), 0))
    prev = lambda w: pl.BlockSpec((BLK, w), lambda i: (jnp.clip(per * i - 1, 0, nb - 1), 0))
    shifted = pl.BlockSpec((per * BLK, 2 * KV_W), lambda i: (i, 0))
    sd = _hbm_shape
    return pl.pallas_call(
        body, name="mixer_bwd", grid=(steps + 1,),
        in_specs=[cur(ATTN_W), cur(KV_W), prev(KV_W), cur(KV_W), prev(KV_W), cur(SGU_W), cur(SGU_W), cur(D_MODEL),
                  cur(128), cur(128), cur(128), prev(128), prev(128), prev(128),
                  _smem(), _const2((1, SGU_W)), _const2((1, SGU_W)), _const2((N_GRP, BLK, BLK)), _const2((BLK, N_GRP)),
                  _vmem()],
        out_specs=[cur(ATTN_W), shifted, cur(2 * SGU_W),
                   _const2((1, ATTN_W)), _const2((1, 2 * KV_W)), _const2((1, 2 * SGU_W)),
                   _const2((1, 128)), _const2((1, SGU_W)), _const2((1, SGU_W)),
                   _const2((N_GRP, BLK, BLK)), _const2((BLK, 128)), _hbm()],
        out_shape=[sd((s_len, ATTN_W), _MXU), sd((s_len + per * BLK, 2 * KV_W), _MXU), sd((s_len, 2 * SGU_W), _MXU),
                   sd((1, ATTN_W), F32), sd((1, 2 * KV_W), F32), sd((1, 2 * SGU_W), F32),
                   sd((1, 128), F32), sd((1, SGU_W), F32), sd((1, SGU_W), F32),
                   sd((N_GRP, BLK, BLK), F32), sd((BLK, 128), F32), _ChipExchange.land_shape(prev_wire)],
        scratch_shapes=[pltpu.VMEM((BLK, 2 * KV_W), F32)] + _ChipExchange.scratch(prev_wire),
        compiler_params=_params(32),
    )(q, k, k, v, v, su, sv, dmc, tc, t1, t2, tc, t1, t2, sinks, sg, sb, sgu_w, sgu_bt, prev_wire)


def _inproj_bwd(dq, dkv, dsuv, dr1, x, g0, b0, w_in):
    s_len = x.shape[0]
    tm = _tile(s_len, 512)
    cuts = ((0, ATTN_W), (ATTN_W, ATTN_W + 2 * KV_W), (ATTN_W + 2 * KV_W, IN_W))

    def body(dq_ref, dkv_ref, dsuv_ref, dr1_ref, x_ref, g_ref, b_ref, w_ref, dx_ref, dw_ref, dg_ref, db_ref):
        i = pl.program_id(0)

        @pl.when(i == 0)
        def _():
            dw_ref[...] = jnp.zeros_like(dw_ref)
            dg_ref[...] = jnp.zeros_like(dg_ref)
            db_ref[...] = jnp.zeros_like(db_ref)

        h0, xhat, rstd = _ln(x_ref[...], g_ref[...], b_ref[...])
        h0b = h0.astype(_MXU)
        dh0 = ALPHA * dr1_ref[...]
        for (lo, hi), d_ref in zip(cuts, (dq_ref, dkv_ref, dsuv_ref)):
            d = d_ref[...]
            dh0 = dh0 + _dot(d, w_ref[lo:hi, :])
            dw_ref[lo:hi, :] += _dot_tn(d, h0b)
        dg_ref[...] += _colsum(dh0 * xhat)
        db_ref[...] += _colsum(dh0)
        dx_ref[...] = _ln_bwd(dh0, xhat, rstd, g_ref[...])

    vec = _hbm_shape((1, D_MODEL), F32)
    c = _const2((1, D_MODEL))
    return pl.pallas_call(
        body, name="inproj_bwd", grid=(s_len // tm,),
        in_specs=[_rows(tm, ATTN_W), _rows(tm, 2 * KV_W), _rows(tm, 2 * SGU_W), _rows(tm, D_MODEL), _rows(tm, D_MODEL),
                  c, c, _vmem()],
        out_specs=[_rows(tm, D_MODEL), _vmem(), c, c],
        out_shape=[_hbm_shape((s_len, D_MODEL), F32), jax.ShapeDtypeStruct((IN_W, D_MODEL), F32), vec, vec],
        compiler_params=_params(48),
    )(dq, dkv, dsuv, dr1, x, g0, b0, w_in)


def _place():
    x, y, c = (lax.axis_index(a) for a in MESH_AXES)
    chips = [(1 - x, y), (x, 1 - y), (1 - x, 1 - y)]
    return x, y, c, chips


class _Gather:
    def __init__(self, ins, outs, send_sems, recv_sems, spans=None):
        self.ins, self.outs, self.send_sems, self.recv_sems = ins, outs, send_sems, recv_sems
        self.n = len(ins)
        self.spans = spans or [(0, r.shape[0]) for r in ins]
        self.halves = [(hi - lo) // 2 for lo, hi in self.spans]

    def _copy(self, k, t, slot, half, to):
        rows = pl.ds(pl.multiple_of(self.spans[t][0] + half * self.halves[t], 16), self.halves[t])
        piece = self.outs[t].at[slot, rows, :]
        return pltpu.make_async_remote_copy(src_ref=piece, dst_ref=piece, send_sem=self.send_sems.at[k],
                                            recv_sem=self.recv_sems.at[k], device_id=to, device_id_type=MESH)

    def _chip_copy(self, t, d, slot):
        x, y, c, chips = _place()
        return self._copy(3 * t + d, t, slot, c, (chips[d][0], chips[d][1], c))

    def _pass_copy(self, t, d, half):
        x, y, c, chips = _place()
        return self._copy(3 * self.n + 3 * t + d, t, 2 * chips[d][0] + chips[d][1], half, (x, y, 1 - c))

    def start(self):
        x, y, c, chips = _place()
        me = 2 * x + y
        for t in range(self.n):
            lo, hi = self.spans[t]
            self.outs[t][me, lo:hi, :] = self.ins[t][lo:hi, :].astype(_WIRE)
        for t in range(self.n):
            for d in range(3):
                self._chip_copy(t, d, me).start()

    def pass_on(self):
        x, y, c, chips = _place()
        for t in range(self.n):
            for d in range(3):
                self._chip_copy(t, d, 2 * chips[d][0] + chips[d][1]).wait_recv()
                self._pass_copy(t, d, c).start()

    def finish(self):
        x, y, c, chips = _place()
        me = 2 * x + y
        for t in range(self.n):
            for d in range(3):
                self._pass_copy(t, d, 1 - c).wait_recv()
        for t in range(self.n):
            for d in range(3):
                self._chip_copy(t, d, me).wait_send()
                self._pass_copy(t, d, c).wait_send()

    @staticmethod
    def out_shapes(shards, make=jax.ShapeDtypeStruct):
        return [make((N_CHIP,) + s.shape, _WIRE) for s in shards]

    @staticmethod
    def sems(n):
        return [pltpu.SemaphoreType.DMA((6 * n,)), pltpu.SemaphoreType.DMA((6 * n,))]


class _GatherPlan:
    def __init__(self, pieces):
        self.shards = [p[0] for p in pieces]
        self.spans = [p[1] for p in pieces]
        self.earlier = [p[2] for p in pieces]
        self.n = len(pieces)
        self.carried = [t for t in range(self.n) if self.earlier[t] is not None]

    def operands(self):
        return self.shards + [self.earlier[t] for t in self.carried]

    def in_specs(self):
        return [_vmem()] * self.n + [_hbm()] * len(self.carried)

    def out_specs(self):
        return [_hbm()] * self.n

    def out_shapes(self):
        return _Gather.out_shapes(self.shards, _hbm_shape)

    def scratch(self):
        return ([pltpu.VMEM((N_CHIP,) + s.shape, _WIRE) for s in self.shards] + _Gather.sems(self.n)
                + [pltpu.SemaphoreType.DMA((self.n,)), pltpu.SemaphoreType.DMA((max(len(self.carried), 1),))])

    def bind(self, in_refs, out_refs, scratch_refs):
        plan = self
        shard_refs, earlier_refs = in_refs[:self.n], in_refs[self.n:]
        bufs = scratch_refs[:self.n]
        send_sems, recv_sems, flush_sems, carry_sems = scratch_refs[self.n:self.n + 4]
        gather = _Gather(shard_refs, bufs, send_sems, recv_sems, self.spans)

        def carry_copy(k):
            t = plan.carried[k]
            lo = plan.spans[t][0]
            return pltpu.make_async_copy(earlier_refs[k].at[:, 0:lo, :], bufs[t].at[:, 0:lo, :], carry_sems.at[k])

        class Bound:
            @staticmethod
            def start():
                for k in range(len(plan.carried)):
                    carry_copy(k).start()
                gather.start()

            @staticmethod
            def pass_on():
                gather.pass_on()

            @staticmethod
            def finish():
                gather.finish()
                for k in range(len(plan.carried)):
                    carry_copy(k).wait()
                _flush([bufs[t].at[:, 0:plan.spans[t][1], :] for t in range(plan.n)],
                       [out_refs[t].at[:, 0:plan.spans[t][1], :] for t in range(plan.n)], flush_sems)

        return Bound


def _flush(bufs, hbm_outs, sems):
    copies = [pltpu.make_async_copy(b, o, sems.at[k]) for k, (b, o) in enumerate(zip(bufs, hbm_outs))]
    for cp in copies:
        cp.start()
    for cp in copies:
        cp.wait()


def _gather_weights(shards):
    n = len(shards)

    def body(*refs):
        gather = _Gather(refs[:n], refs[n:2 * n], refs[2 * n], refs[2 * n + 1])
        gather.start()
        gather.pass_on()
        gather.finish()

    return pl.pallas_call(
        body, name="gather_weights",
        in_specs=[_vmem()] * n, out_specs=[_vmem()] * n,
        out_shape=_Gather.out_shapes(shards), scratch_shapes=_Gather.sems(n),
        compiler_params=pltpu.CompilerParams(vmem_limit_bytes=32 * MIB),
    )(*shards)


class _ChipExchange:
    def __init__(self, wire_ref, land_ref, send_sems, recv_sems):
        self.wire, self.land, self.send_sems, self.recv_sems = wire_ref, land_ref, send_sems, recv_sems

    def _copy(self, d):
        x, y, c, chips = _place()
        return pltpu.make_async_remote_copy(
            src_ref=self.wire.at[2 * chips[d][0] + chips[d][1]], dst_ref=self.land.at[d],
            send_sem=self.send_sems.at[d], recv_sem=self.recv_sems.at[d],
            device_id=(chips[d][0], chips[d][1], c), device_id_type=MESH)

    def start(self):
        for d in range(3):
            self._copy(d).start()

    def wait_recv(self):
        for d in range(3):
            self._copy(d).wait_recv()

    def wait_send(self):
        for d in range(3):
            self._copy(d).wait_send()

    def finish_to(self, hbm_out, flush_sem):
        self.wait_recv()
        _flush([self.land], [hbm_out], flush_sem)
        self.wait_send()

    @staticmethod
    def land_shape(wire):
        return _hbm_shape((3,) + wire.shape[1:], wire.dtype)

    @staticmethod
    def sems():
        return [pltpu.SemaphoreType.DMA((3,)), pltpu.SemaphoreType.DMA((3,))]

    @staticmethod
    def scratch(wire):
        return ([pltpu.VMEM((3,) + wire.shape[1:], wire.dtype)] + _ChipExchange.sems() + [pltpu.SemaphoreType.DMA((1,))])


def _pair_out_shapes(half_shape):
    return [jax.ShapeDtypeStruct(half_shape, _WIRE), jax.ShapeDtypeStruct(half_shape[1:], F32)]


def _pair_scratch(acc_shape, half_shape):
    return [pltpu.VMEM(acc_shape, F32), pltpu.VMEM(half_shape, _WIRE),
            pltpu.SemaphoreType.DMA((N_CHIP,)), pltpu.SemaphoreType.DMA((N_CHIP,))]


def _pair_reduce(acc_ref, wire_ref, own_ref, land_ref, send_sems, recv_sems):
    rh = land_ref.shape[1]
    x, y, c, _ = _place()
    me = 2 * x + y
    copies = []
    for j in range(N_CHIP):
        def cast(r, carry, j=j):
            dst = pl.ds(pl.multiple_of(r * ROW_CHUNK, ROW_CHUNK), ROW_CHUNK)
            src = pl.ds(pl.multiple_of((2 * j + 1 - c) * rh + r * ROW_CHUNK, 8), ROW_CHUNK)
            wire_ref[j, dst, :] = acc_ref[src, :].astype(_WIRE)
            return carry

        lax.fori_loop(0, rh // ROW_CHUNK, cast, 0)
        cp = pltpu.make_async_remote_copy(src_ref=wire_ref.at[j], dst_ref=land_ref.at[j], send_sem=send_sems.at[j],
                                          recv_sem=recv_sems.at[j], device_id=(x, y, 1 - c), device_id_type=MESH)
        cp.start()
        copies.append(cp)
    for j in range(N_CHIP):
        copies[j].wait()

        def chunk(r, carry, j=j):
            theirs = pl.ds(pl.multiple_of(r * ROW_CHUNK, ROW_CHUNK), ROW_CHUNK)
            mine = pl.ds(pl.multiple_of((2 * j + c) * rh + r * ROW_CHUNK, 8), ROW_CHUNK)
            wire_ref[j, theirs, :] = (acc_ref[mine, :] + land_ref[j, theirs, :].astype(F32)).astype(_WIRE)
            return carry

        lax.fori_loop(0, rh // ROW_CHUNK, chunk, 0)

    def own_chunk(r, carry):
        theirs = pl.ds(pl.multiple_of(r * ROW_CHUNK, ROW_CHUNK), ROW_CHUNK)
        mine = pl.ds(pl.multiple_of((2 * me + c) * rh + r * ROW_CHUNK, 8), ROW_CHUNK)
        own_ref[theirs, :] = acc_ref[mine, :] + land_ref[me, theirs, :].astype(F32)
        return carry

    lax.fori_loop(0, rh // ROW_CHUNK, own_chunk, 0)


def _grad_finish(last_acc, lands, owns):
    n = len(owns) + 1
    halves = [last_acc.shape[0] // (2 * N_CHIP)] + [w.shape[1] for w in lands]
    widths = [last_acc.shape[1]] + [a.shape[1] for a in owns]

    def body(*refs):
        acc0, land, own, g = refs[0], (None,) + refs[1:n], (None,) + refs[n:2 * n - 1], refs[2 * n - 1:3 * n - 1]
        pland0, wire0, land0, own0 = refs[3 * n - 1:3 * n + 3]
        p_send, p_recv, x_send, x_recv, pair_send, pair_recv = refs[3 * n + 3:3 * n + 9]
        land = (land0,) + land[1:]
        own = (own0,) + own[1:]
        x, y, c, chips = _place()
        me = 2 * x + y
        exchange = _ChipExchange(wire0, land0, x_send, x_recv)

        def half_rows(t, half):
            return pl.ds(pl.multiple_of(half * halves[t], 8), halves[t])

        def pair_copy(t, half):
            rows = g[t].at[half_rows(t, half), :]
            return pltpu.make_async_remote_copy(src_ref=rows, dst_ref=rows, send_sem=pair_send.at[t],
                                                recv_sem=pair_recv.at[t], device_id=(x, y, 1 - c), device_id_type=MESH)

        _pair_reduce(acc0, wire0, own0, pland0, p_send, p_recv)
        exchange.start()

        for t in list(range(1, n)) + [0]:
            if t == 0:
                exchange.wait_recv()

            def chunk(r, carry, t=t):
                src = pl.ds(pl.multiple_of(r * ROW_CHUNK, ROW_CHUNK), ROW_CHUNK)
                dst = pl.ds(pl.multiple_of(c * halves[t] + r * ROW_CHUNK, 8), ROW_CHUNK)
                s = own[t][src, :]
                for d in range(3):
                    s = s + land[t][d, src, :].astype(F32)
                g[t][dst, :] = s
                return carry

            lax.fori_loop(0, halves[t] // ROW_CHUNK, chunk, 0)
            pair_copy(t, c).start()
        for t in range(n):
            pair_copy(t, 1 - c).wait_recv()
        for t in range(n):
            pair_copy(t, c).wait_send()
        exchange.wait_send()

    half0 = (halves[0], widths[0])
    return pl.pallas_call(
        body, name="grad_finish",
        in_specs=[_vmem()] * (2 * n - 1), out_specs=[_vmem()] * n,
        out_shape=[jax.ShapeDtypeStruct((2 * h, w), F32) for h, w in zip(halves, widths)],
        scratch_shapes=[pltpu.VMEM((N_CHIP,) + half0, _WIRE), pltpu.VMEM((N_CHIP,) + half0, _WIRE),
                        pltpu.VMEM((3,) + half0, _WIRE), pltpu.VMEM(half0, F32)]
        + [pltpu.SemaphoreType.DMA((N_CHIP,)), pltpu.SemaphoreType.DMA((N_CHIP,))]
        + _ChipExchange.sems()
        + [pltpu.SemaphoreType.DMA((n,)), pltpu.SemaphoreType.DMA((n,))],
        compiler_params=pltpu.CompilerParams(vmem_limit_bytes=56 * MIB),
    )(last_acc, *lands, *owns)


_SMALL = ("ln_in_g", "ln_in_b", "b_in", "attn_sinks", "sgu_ln_g", "sgu_ln_b", "sgu_w", "sgu_b", "b_out",
          "ln_mix_g", "ln_mix_b", "ln_ffn_g", "ln_ffn_b")
_VEC_ROW = dict(ln_in_g=0, ln_in_b=1, b_in=2, attn_sinks=4, sgu_ln_g=5, sgu_ln_b=6, b_out=7, ln_mix_g=8, ln_mix_b=9,
                ln_ffn_g=10, ln_ffn_b=11)
_LOSS_ROW = 12
_VEC_ROWS = 16
_MAT_ROWS = N_GRP * BLK + BLK


def _small_allreduce(local):
    n_in = 16

    def body(*refs):
        (g_ln_in_g, g_ln_in_b, g_bq, g_bkv, g_bsuv, g_sink, g_sln_g, g_sln_b, g_sw, g_sbt, g_bout,
         g_lmg, g_lmb, g_lfg, g_lfb, g_loss) = refs[:n_in]
        out_a, out_b = refs[n_in:n_in + 2]
        (buf_a, buf_b, pair_a, pair_b, stage_a, stage_b, tot_a, tot_b,
         p1_send, p1_recv, x_send, x_recv, p2_send, p2_recv) = refs[n_in + 2:]
        x, y, c, chips = _place()
        me = 2 * x + y
        sibling = (x, y, 1 - c)
        half_a, half_b = _VEC_ROWS // 2, _MAT_ROWS // 2

        buf_a[...] = jnp.zeros_like(buf_a)
        for row, ref in ((0, g_ln_in_g), (1, g_ln_in_b), (7, g_bout), (8, g_lmg), (9, g_lmb), (10, g_lfg), (11, g_lfb),
                         (_LOSS_ROW, g_loss)):
            buf_a[row:row + 1, :] = ref[...]
        buf_a[2:3, 0:ATTN_W] = g_bq[...]
        buf_a[2:3, ATTN_W:ATTN_W + 2 * KV_W] = g_bkv[...]
        buf_a[2:3, ATTN_W + 2 * KV_W:D_MODEL] = g_bsuv[:, 0:2 * KV_W]
        buf_a[3:4, 0:2 * SGU_W - 2 * KV_W] = g_bsuv[:, 2 * KV_W:2 * SGU_W]
        buf_a[4:5, 0:128] = g_sink[...]
        buf_a[5:6, 0:SGU_W] = g_sln_g[...]
        buf_a[6:7, 0:SGU_W] = g_sln_b[...]
        for h in range(N_GRP):
            buf_b[h * BLK:(h + 1) * BLK, :] = g_sw[h]
        buf_b[N_GRP * BLK:_MAT_ROWS, :] = g_sbt[...]

        def remote(src, dst, send_sem, recv_sem, to):
            return pltpu.make_async_remote_copy(src_ref=src, dst_ref=dst, send_sem=send_sem, recv_sem=recv_sem,
                                                device_id=to, device_id_type=MESH)

        first = [remote(buf_a, pair_a, p1_send.at[0], p1_recv.at[0], sibling),
                 remote(buf_b, pair_b, p1_send.at[1], p1_recv.at[1], sibling)]
        for cp in first:
            cp.start()
        for cp in first:
            cp.wait()
        rows_a = pl.ds(pl.multiple_of(c * half_a, 8), half_a)
        rows_b = pl.ds(pl.multiple_of(c * half_b, 8), half_b)
        stage_a[me] = buf_a[rows_a, :] + pair_a[rows_a, :]
        stage_b[me] = buf_b[rows_b, :] + pair_b[rows_b, :]

        def chip_copies(d):
            to = (chips[d][0], chips[d][1], c)
            return [remote(stage_a.at[me], stage_a.at[me], x_send.at[2 * d], x_recv.at[2 * d], to),
                    remote(stage_b.at[me], stage_b.at[me], x_send.at[2 * d + 1], x_recv.at[2 * d + 1], to)]

        def chip_arrivals(d):
            slot = 2 * chips[d][0] + chips[d][1]
            to = (chips[d][0], chips[d][1], c)
            return [remote(stage_a.at[slot], stage_a.at[slot], x_send.at[2 * d], x_recv.at[2 * d], to),
                    remote(stage_b.at[slot], stage_b.at[slot], x_send.at[2 * d + 1], x_recv.at[2 * d + 1], to)]

        for d in range(3):
            for cp in chip_copies(d):
                cp.start()
        for d in range(3):
            for cp in chip_arrivals(d):
                cp.wait_recv()
        tot_a[rows_a, :] = ((stage_a[0] + stage_a[1]) + stage_a[2]) + stage_a[3]
        tot_b[rows_b, :] = ((stage_b[0] + stage_b[1]) + stage_b[2]) + stage_b[3]

        second = [remote(tot_a.at[rows_a, :], tot_a.at[rows_a, :], p2_send.at[0], p2_recv.at[0], sibling),
                  remote(tot_b.at[rows_b, :], tot_b.at[rows_b, :], p2_send.at[1], p2_recv.at[1], sibling)]
        for cp in second:
            cp.start()
        other_a = pl.ds(pl.multiple_of((1 - c) * half_a, 8), half_a)
        other_b = pl.ds(pl.multiple_of((1 - c) * half_b, 8), half_b)
        remote(tot_a.at[other_a, :], tot_a.at[other_a, :], p2_send.at[0], p2_recv.at[0], sibling).wait_recv()
        remote(tot_b.at[other_b, :], tot_b.at[other_b, :], p2_send.at[1], p2_recv.at[1], sibling).wait_recv()
        for cp in second:
            cp.wait_send()
        for d in range(3):
            for cp in chip_copies(d):
                cp.wait_send()
        out_a[...] = tot_a[...]
        out_b[...] = tot_b[...]

    ins = [local[k] for k in ("ln_in_g", "ln_in_b", "bq", "bkv", "bsuv", "sink", "sgu_ln_g", "sgu_ln_b", "sgu_w",
                              "sgu_bt", "b_out", "ln_mix_g", "ln_mix_b", "ln_ffn_g", "ln_ffn_b", "loss")]
    out_dims = [(_VEC_ROWS, D_MODEL), (_MAT_ROWS, 128)]
    vec = pltpu.VMEM((_VEC_ROWS, D_MODEL), F32)
    mat = pltpu.VMEM((_MAT_ROWS, 128), F32)
    return pl.pallas_call(
        body, name="small_allreduce", grid=(1,),
        in_specs=[_const2(a.shape) for a in ins], out_specs=[_const2(s) for s in out_dims],
        out_shape=[_hbm_shape(s, F32) for s in out_dims],
        scratch_shapes=[vec, mat, vec, mat, pltpu.VMEM((N_CHIP, _VEC_ROWS // 2, D_MODEL), F32),
                        pltpu.VMEM((N_CHIP, _MAT_ROWS // 2, 128), F32), vec, mat,
                        pltpu.SemaphoreType.DMA((2,)), pltpu.SemaphoreType.DMA((2,)), pltpu.SemaphoreType.DMA((6,)),
                        pltpu.SemaphoreType.DMA((6,)), pltpu.SemaphoreType.DMA((2,)), pltpu.SemaphoreType.DMA((2,))],
        compiler_params=pltpu.CompilerParams(vmem_limit_bytes=32 * MIB),
    )(*ins)


def _small_adamw(tot_a, tot_b, params):
    shapes = [params[nm][0].shape for nm in _SMALL]

    def body(*refs):
        ta, tb = refs[:2]
        prm = refs[2:2 + 3 * len(_SMALL)]
        outs = refs[2 + 3 * len(_SMALL):]

        def grad_of(k, name):
            if name == "sgu_w":
                return [tb[h * BLK:(h + 1) * BLK, :] for h in range(N_GRP)]
            if name == "sgu_b":
                return jnp.transpose(tb[N_GRP * BLK:_MAT_ROWS, :])[0:N_GRP, :]
            row = _VEC_ROW[name]
            if name == "b_in":
                return jnp.concatenate([ta[row:row + 1, :], ta[row + 1:row + 2, 0:IN_W - D_MODEL]], axis=1)
            return ta[row:row + 1, 0:shapes[k][-1]]

        for k, name in enumerate(_SMALL):
            w_ref, m_ref, v_ref = prm[3 * k:3 * k + 3]
            g_out, d_out, m_out, v_out = outs[4 * k:4 * k + 4]
            g = grad_of(k, name)
            if name == "sgu_w":
                for h in range(N_GRP):
                    d_, m_, v_ = _adamw_math(w_ref[h], g[h], m_ref[h], v_ref[h])
                    g_out[h], d_out[h], m_out[h], v_out[h] = g[h], d_, m_, v_
            else:
                d_, m_, v_ = _adamw_math(w_ref[...], g, m_ref[...], v_ref[...])
                g_out[...], d_out[...], m_out[...], v_out[...] = g, d_, m_, v_
        outs[-1][...] = ta[_LOSS_ROW:_LOSS_ROW + 1, :]

    ins = [tot_a, tot_b] + [_in_hbm(a) for nm in _SMALL for a in params[nm]]
    out_dims = [s for s in shapes for _ in range(4)] + [(1, D_MODEL)]
    res = pl.pallas_call(
        body, name="small_adamw", grid=(1,),
        in_specs=[_const2(a.shape) for a in ins], out_specs=[_const2(s) for s in out_dims],
        out_shape=[_hbm_shape(s, F32) for s in out_dims],
        compiler_params=_params(32),
    )(*ins)
    return {nm: tuple(res[4 * k:4 * k + 4]) for k, nm in enumerate(_SMALL)}, res[-1]


def _elementwise(name, fn, ins, out_dtypes, tile_rows=256):
    shape = ins[0].shape
    lead = shape[:-2]
    rows, cols = shape[-2:]
    tr = _tile(rows, tile_rows)
    n_lead = math.prod(lead)
    nr = rows // tr
    flat = [_in_hbm(a.reshape((n_lead, rows, cols))) for a in ins]

    def body(*refs):
        outs = fn(*[r[0] for r in refs[:len(ins)]])
        for o_ref, o in zip(refs[len(ins):], outs):
            o_ref[0] = o.astype(o_ref.dtype)

    spec = pl.BlockSpec((1, tr, cols), lambda i: (i // nr, i % nr, 0))
    res = pl.pallas_call(
        body, name=name, grid=(n_lead * nr,),
        in_specs=[spec] * len(ins), out_specs=[spec] * len(out_dtypes),
        out_shape=[_hbm_shape((n_lead, rows, cols), dt) for dt in out_dtypes],
        compiler_params=_params(48),
    )(*flat)
    return [r.reshape(shape) for r in res]


def _adamw_math(w, g, m, v):
    m = ADAM_B1 * m + (1.0 - ADAM_B1) * g
    v = ADAM_B2 * v + (1.0 - ADAM_B2) * (g * g)
    m_hat = m / (1.0 - ADAM_B1 ** ADAM_STEP)
    v_hat = v / (1.0 - ADAM_B2 ** ADAM_STEP)
    delta = -ADAM_LR * (m_hat / (jnp.sqrt(v_hat) + ADAM_EPS) + ADAM_WD * w)
    return delta, m, v


def _adamw(name, groups, tile_rows=256):
    k = len(groups)

    def fn(*blocks):
        outs = []
        for i in range(k):
            w_, g_, m_, v_ = blocks[4 * i:4 * i + 4]
            outs += [g_, *_adamw_math(w_, g_, m_, v_)]
        return outs

    res = _elementwise(name, fn, [a for grp in groups for a in grp], [F32] * (4 * k), tile_rows)
    return [res[4 * i:4 * i + 4] for i in range(k)]


def kernel(x, positions, ln_in_g, ln_in_b, w_in, b_in, attn_sinks, sgu_ln_g, sgu_ln_b, sgu_w, sgu_b, w_out, b_out, ln_mix_g, ln_mix_b, w_gate, w_up, w_down, ln_ffn_g, ln_ffn_b, loss_target, m_ln_in_g, m_ln_in_b, m_w_in, m_b_in, m_attn_sinks, m_sgu_ln_g, m_sgu_ln_b, m_sgu_w, m_sgu_b, m_w_out, m_b_out, m_ln_mix_g, m_ln_mix_b, m_w_gate, m_w_up, m_w_down, m_ln_ffn_g, m_ln_ffn_b, v_ln_in_g, v_ln_in_b, v_w_in, v_b_in, v_attn_sinks, v_sgu_ln_g, v_sgu_ln_b, v_sgu_w, v_sgu_b, v_w_out, v_b_out, v_ln_mix_g, v_ln_mix_b, v_w_gate, v_w_up, v_w_down, v_ln_ffn_g, v_ln_ffn_b):
    weights = dict(ln_in_g=ln_in_g, ln_in_b=ln_in_b, w_in=w_in, b_in=b_in, attn_sinks=attn_sinks, sgu_ln_g=sgu_ln_g,
                   sgu_ln_b=sgu_ln_b, sgu_w=sgu_w, sgu_b=sgu_b, w_out=w_out, b_out=b_out, ln_mix_g=ln_mix_g,
                   ln_mix_b=ln_mix_b, w_gate=w_gate, w_up=w_up, w_down=w_down, ln_ffn_g=ln_ffn_g, ln_ffn_b=ln_ffn_b)
    mom_m = dict(ln_in_g=m_ln_in_g, ln_in_b=m_ln_in_b, w_in=m_w_in, b_in=m_b_in, attn_sinks=m_attn_sinks,
                 sgu_ln_g=m_sgu_ln_g, sgu_ln_b=m_sgu_ln_b, sgu_w=m_sgu_w, sgu_b=m_sgu_b, w_out=m_w_out, b_out=m_b_out,
                 ln_mix_g=m_ln_mix_g, ln_mix_b=m_ln_mix_b, w_gate=m_w_gate, w_up=m_w_up, w_down=m_w_down,
                 ln_ffn_g=m_ln_ffn_g, ln_ffn_b=m_ln_ffn_b)
    mom_v = dict(ln_in_g=v_ln_in_g, ln_in_b=v_ln_in_b, w_in=v_w_in, b_in=v_b_in, attn_sinks=v_attn_sinks,
                 sgu_ln_g=v_sgu_ln_g, sgu_ln_b=v_sgu_ln_b, sgu_w=v_sgu_w, sgu_b=v_sgu_b, w_out=v_w_out, b_out=v_b_out,
                 ln_mix_g=v_ln_mix_g, ln_mix_b=v_ln_mix_b, w_gate=v_w_gate, w_up=v_w_up, w_down=v_w_down,
                 ln_ffn_g=v_ln_ffn_g, ln_ffn_b=v_ln_ffn_b)
    order = list(weights)
    big = ("w_in", "w_out", "w_gate", "w_up", "w_down")

    s_len = x.shape[1]
    xs = _in_hbm(x.reshape(s_len, D_MODEL))
    tgt = _in_hbm(loss_target.reshape(s_len, D_MODEL))
    pos_col = _in_hbm(positions.reshape(s_len, 1))
    g0, b0 = _in_hbm(ln_in_g.reshape(1, D_MODEL)), _in_hbm(ln_in_b.reshape(1, D_MODEL))
    sinks = attn_sinks.reshape(N_Q)
    sgu_w3 = _in_hbm(sgu_w.reshape(N_GRP, BLK, BLK))
    sgu_bt = _in_hbm(sgu_b.reshape(N_GRP, BLK).T)
    b_in, b_out, sgu_ln_g, sgu_ln_b, ln_mix_g, ln_mix_b, ln_ffn_g, ln_ffn_b = (
        _in_hbm(a) for a in (b_in, b_out, sgu_ln_g, sgu_ln_b, ln_mix_g, ln_mix_b, ln_ffn_g, ln_ffn_b))

    col_sharded = ("w_in", "w_gate", "w_up")

    def rowmajor(name, a):
        return jnp.swapaxes(a[0], 0, 1) if name in col_sharded else a[0]

    def as_given(name, a):
        return (jnp.swapaxes(a, 0, 1) if name in col_sharded else a)[None]

    shards = [rowmajor(n, weights[n]) for n in big]
    (gw_in,) = _gather_weights(shards[0:1])
    w_in_full = gw_in.reshape(IN_W, D_MODEL)

    sh_out, sh_gate, sh_up, sh_down = shards[1:]
    cut = GATHER_CUT
    *acts, gw_out, gw_gate0 = _ln_inproj(xs, pos_col, g0, b0, w_in_full, b_in, _GatherPlan(
        [(sh_out, (0, OUT_SH), None), (sh_gate, (0, cut), None)]))
    q, k, v, su, sv, tc, t1, t2 = (_in_hbm(a) for a in acts)
    mc, gw_gate, gw_up0 = _mixer_fwd(q, k, v, su, sv, sinks, sgu_ln_g, sgu_ln_b, sgu_w3, sgu_bt, _GatherPlan(
        [(sh_gate, (cut, FF_SH), gw_gate0), (sh_up, (0, cut), None)]))
    mc = _in_hbm(mc)
    w_out_full = gw_out.reshape(D_MODEL, D_MODEL)
    r1, gw_up = _outproj(mc, w_out_full, b_out, xs, g0, b0, _GatherPlan([(sh_up, (cut, FF_SH), gw_up0)]))
    r1 = _in_hbm(r1)
    act, p_act, q_act, gw_down = _ffn_up(r1, ln_mix_g, ln_mix_b, gw_gate, gw_up,
                                         _GatherPlan([(sh_down, (0, FF_SH), None)]))
    act, p_act, q_act = _in_hbm(act), _in_hbm(p_act), _in_hbm(q_act)
    dr2, loss_cols, d_ln_ffn_g, d_ln_ffn_b = _ffn_down_loss(act, gw_down, r1, ln_mix_g, ln_mix_b, ln_ffn_g, ln_ffn_b, tgt)
    dr2 = _in_hbm(dr2)

    dg, du, wire_down, own_down = _ffn_bwd_a(dr2, act, p_act, q_act, gw_down)
    dh1a, wire_gate, own_gate, land_down = _ffn_bwd_g(dr2, _in_hbm(dg), r1, ln_mix_g, ln_mix_b, gw_gate, wire_down)
    dr1, wire_up, own_up, d_ln_mix_g, d_ln_mix_b, land_gate = _ffn_bwd_u(_in_hbm(dh1a), _in_hbm(du), r1, ln_mix_g,
                                                                         ln_mix_b, gw_up, wire_gate)
    dr1 = _in_hbm(dr1)
    dmc, wire_out, own_out, d_b_out, land_up = _outproj_bwd(dr1, mc, w_out_full, wire_up)
    (dq, dkv, dsuv, dbq, dbkv, dbsuv, d_sink, d_sgu_ln_g, d_sgu_ln_b, d_sgu_w, d_sgu_bt, land_out) = _mixer_bwd(
        q, k, v, su, sv, _in_hbm(dmc), tc, t1, t2, sinks, sgu_ln_g, sgu_ln_b, sgu_w3, sgu_bt, wire_out)
    dkv = dkv[BLK:BLK + s_len]
    grad_x, acc_in, d_ln_in_g, d_ln_in_b = _inproj_bwd(_in_hbm(dq), _in_hbm(dkv), _in_hbm(dsuv), dr1, xs, g0, b0,
                                                       w_in_full)

    reduced = _grad_finish(acc_in, [land_out, land_gate, land_up, land_down], [own_out, own_gate, own_up, own_down])
    small_shape = dict(ln_in_g=(1, D_MODEL), ln_in_b=(1, D_MODEL), sgu_w=(N_GRP, BLK, BLK), sgu_b=(N_GRP, BLK))
    small_local = dict(
        ln_in_g=d_ln_in_g, ln_in_b=d_ln_in_b, bq=dbq, bkv=dbkv, bsuv=dbsuv, sink=d_sink, sgu_ln_g=d_sgu_ln_g,
        sgu_ln_b=d_sgu_ln_b, sgu_w=d_sgu_w, sgu_bt=d_sgu_bt, b_out=d_b_out, ln_mix_g=d_ln_mix_g, ln_mix_b=d_ln_mix_b,
        ln_ffn_g=d_ln_ffn_g, ln_ffn_b=d_ln_ffn_b, loss=loss_cols)
    small_params = {nm: tuple(src[nm].reshape(small_shape.get(nm, src[nm].shape)) for src in (weights, mom_m, mom_v))
                    for nm in _SMALL}
    tot_a, tot_b = _small_allreduce({nm: _in_hbm(a) for nm, a in small_local.items()})
    small_out, loss_sum = _small_adamw(_in_hbm(tot_a), _in_hbm(tot_b), small_params)
    loss = jnp.sum(loss_sum) * (0.5 / D_MODEL)
    grads, delta, new_m, new_v = {}, {}, {}, {}
    for nm in _SMALL:
        grads[nm], delta[nm], new_m[nm], new_v[nm] = (a.reshape(weights[nm].shape) for a in small_out[nm])

    def update(call_name, names):
        groups = [(shards[big.index(nm)], reduced[big.index(nm)], rowmajor(nm, mom_m[nm]), rowmajor(nm, mom_v[nm]))
                  for nm in names]
        for nm, res in zip(names, _adamw(call_name, groups)):
            grads[nm], delta[nm], new_m[nm], new_v[nm] = (as_given(nm, a) for a in res)

    update("adamw_w_in", ["w_in"])
    update("adamw_w_out", ["w_out"])
    update("adamw_ffn", ["w_gate", "w_up", "w_down"])

    return (loss, grad_x.reshape(x.shape), *[grads[n] for n in order], *[delta[n] for n in order],
            *[new_m[n] for n in order], *[new_v[n] for n in order])
```

```python
import functools
import math

import jax
import jax.numpy as jnp
from jax import lax
from jax.experimental import pallas as pl
from jax.experimental.pallas import tpu as pltpu

F32 = jnp.float32
_MXU = jnp.bfloat16
_WIRE = jnp.bfloat16
_ACT = jnp.bfloat16

D_MODEL = 1024
ATTN_W = 512
SGU_W = 512
HEAD_DIM = 64
N_Q = 8
N_KV = 2
Q_PER_KV = 4
KV_W = 128
BLK = 128
ROT_DIM = 16
ROPE_THETA = 500000.0
N_GRP = 4
GRP_DIM = 128
D_FF = 2816
IN_W = 1792
LN_EPS = 1e-5
ALPHA = 2.0 ** 0.25
N_CHIP = 4
FF_SH = D_FF // N_CHIP
IN_SH = IN_W // N_CHIP
OUT_SH = D_MODEL // N_CHIP
ROW_CHUNK = 32
GATE_CUT, UP_CUT = 352, 320

ADAM_LR = 0.001
ADAM_B1 = 0.9
ADAM_B2 = 0.999
ADAM_EPS = 1e-08
ADAM_WD = 0.01
ADAM_STEP = 10

SQRT_HALF = 0.7071067811865476
INV_SQRT_2PI = 0.3989422804014327
MESH_AXES = ("x", "y", "c")
MESH = pl.DeviceIdType.MESH
MIB = 2 ** 20


def _vmem():
    return pl.BlockSpec(memory_space=pltpu.VMEM)


def _smem():
    return pl.BlockSpec(memory_space=pltpu.SMEM)


def _hbm():
    return pl.BlockSpec(memory_space=pl.ANY)


def _hbm_shape(shape, dtype):
    return pltpu.HBM(shape, dtype)


def _in_hbm(a):
    return pltpu.with_memory_space_constraint(a, pltpu.HBM)


def _params(vmem_mib=48):
    return pltpu.CompilerParams(dimension_semantics=("arbitrary",), vmem_limit_bytes=vmem_mib * MIB)


def _tile(n, cap):
    if n <= cap:
        return n
    for t in range(cap - cap % 16, 0, -16):
        if n % t == 0:
            return t
    raise ValueError((n, cap))


def _rows(tm, width):
    return pl.BlockSpec((tm, width), lambda i: (i, 0))


def _const2(shape):
    return pl.BlockSpec(shape, lambda i: (0,) * len(shape))


def _ln(x, g, b):
    mu = jnp.mean(x, axis=-1, keepdims=True)
    xc = x - mu
    var = jnp.mean(xc * xc, axis=-1, keepdims=True)
    rstd = lax.rsqrt(var + LN_EPS)
    xhat = xc * rstd
    return xhat * g + b, xhat, rstd


def _ln_bwd(dy, xhat, rstd, g):
    gdy = dy * g
    m1 = jnp.mean(gdy, axis=-1, keepdims=True)
    m2 = jnp.mean(gdy * xhat, axis=-1, keepdims=True)
    return rstd * (gdy - m1 - xhat * m2)


def _colsum(a):
    return jnp.sum(a, axis=0, keepdims=True)


def _gelu_and_grad(x):
    cdf = 0.5 * (1.0 + lax.erf(x * SQRT_HALF))
    return x * cdf, cdf + x * jnp.exp(-0.5 * x * x) * INV_SQRT_2PI


def _dot(a, b):
    return jnp.dot(a, b, preferred_element_type=F32)


def _dot_nt(a, b):
    return lax.dot_general(a, b, (((1,), (1,)), ((), ())), preferred_element_type=F32)


def _dot_tn(a, b):
    return lax.dot_general(a, b, (((0,), (0,)), ((), ())), preferred_element_type=F32)


def _rope(t, tc, t1, t2):
    n = t.shape[1]
    rep = n // 128
    if rep > 1:
        tc, t1, t2 = (jnp.tile(a, (1, rep)) for a in (tc, t1, t2))
    return t * tc + pltpu.roll(t, n - 8, 1) * t1 + pltpu.roll(t, 8, 1) * t2


def _rope_bwd(d, tc, t1, t2):
    n = d.shape[1]
    rep = n // 128
    if rep > 1:
        tc, t1, t2 = (jnp.tile(a, (1, rep)) for a in (tc, t1, t2))
    return d * tc + pltpu.roll(d * t1, 8, 1) + pltpu.roll(d * t2, n - 8, 1)


def _causal_w(w_ref, h):
    t = lax.broadcasted_iota(jnp.int32, (BLK, BLK), 0)
    s = lax.broadcasted_iota(jnp.int32, (BLK, BLK), 1)
    return jnp.where(s <= t, w_ref[h], 0.0)


def _lane_put(vals, width):
    rows = vals[0].shape[0]
    lane = lax.broadcasted_iota(jnp.int32, (rows, width), 1)
    out = jnp.zeros((rows, width), F32)
    for k, v in enumerate(vals):
        out = out + jnp.where(lane == k, v, 0.0)
    return out


def _rope_consts():
    lane = jnp.arange(128) % HEAD_DIM
    rot = lane < ROT_DIM
    pair = (2 * (lane % (ROT_DIM // 2))).astype(F32)
    freq = jnp.where(rot, ROPE_THETA ** (-pair / ROT_DIM), 0.0)
    rows = [freq, rot.astype(F32), 1.0 - rot.astype(F32), (lane < ROT_DIM // 2).astype(F32),
            jnp.logical_and(lane >= ROT_DIM // 2, rot).astype(F32)]
    rows += [jnp.zeros((128,), F32)] * 3
    return jnp.stack(rows).astype(F32)


def _ln_inproj(x, pos_col, g0, b0, w_in, b_in, plan):
    s_len = x.shape[0]
    tm = _tile(s_len, 512)
    m, n = len(plan.operands()), plan.n

    def body(x_ref, pos_ref, g_ref, b_ref, w_ref, bi_ref, rc_ref, *rest):
        q_ref, k_ref, v_ref, su_ref, sv_ref, tc_ref, t1_ref, t2_ref = rest[m:m + 8]
        gather = plan.bind(rest[:m], rest[m + 8:m + 8 + n], rest[m + 8 + n:])
        i = pl.program_id(0)

        @pl.when(i == 0)
        def _():
            gather.start()

        h0, _, _ = _ln(x_ref[...], g_ref[...], b_ref[...])
        proj = _dot_nt(h0.astype(_MXU), w_ref[...]) + bi_ref[...]
        ang = pos_ref[...].astype(F32) * rc_ref[0:1, :]
        cs = jnp.cos(ang)
        sn = jnp.sin(ang)
        tc = cs * rc_ref[1:2, :] + rc_ref[2:3, :]
        t1 = -sn * rc_ref[3:4, :]
        t2 = sn * rc_ref[4:5, :]
        tc_ref[...] = tc
        t1_ref[...] = t1
        t2_ref[...] = t2
        q = _rope(proj[:, 0:ATTN_W], tc, t1, t2) * (HEAD_DIM ** -0.5)
        q_ref[...] = q.astype(_MXU)
        k_ref[...] = _rope(proj[:, ATTN_W:ATTN_W + KV_W], tc, t1, t2).astype(_MXU)
        v_ref[...] = proj[:, ATTN_W + KV_W:ATTN_W + 2 * KV_W].astype(_MXU)
        su_ref[...] = proj[:, ATTN_W + 2 * KV_W:ATTN_W + 2 * KV_W + SGU_W]
        sv_ref[...] = proj[:, ATTN_W + 2 * KV_W + SGU_W:IN_W]

        last = pl.num_programs(0) - 1

        @pl.when(i == jnp.maximum(last - 1, 0))
        def _():
            gather.pass_on()

        @pl.when(i == last)
        def _():
            gather.finish()

    sd = _hbm_shape
    return pl.pallas_call(
        body, name="ln_inproj", grid=(s_len // tm,),
        in_specs=[_rows(tm, D_MODEL), _rows(tm, 1), _const2((1, D_MODEL)), _const2((1, D_MODEL)), _vmem(),
                  _const2((1, IN_W)), _const2((8, 128))] + plan.in_specs(),
        out_specs=[_rows(tm, ATTN_W), _rows(tm, KV_W), _rows(tm, KV_W), _rows(tm, SGU_W), _rows(tm, SGU_W),
                   _rows(tm, 128), _rows(tm, 128), _rows(tm, 128)] + plan.out_specs(),
        out_shape=[sd((s_len, ATTN_W), _MXU), sd((s_len, KV_W), _MXU), sd((s_len, KV_W), _MXU),
                   sd((s_len, SGU_W), F32), sd((s_len, SGU_W), F32),
                   sd((s_len, 128), F32), sd((s_len, 128), F32), sd((s_len, 128), F32)] + plan.out_shapes(),
        scratch_shapes=plan.scratch(),
        compiler_params=_params(56),
    )(x, pos_col, g0, b0, w_in, b_in, _rope_consts(), *plan.operands())


def _band_mask_t(first_block):
    kj = lax.broadcasted_iota(jnp.int32, (2 * BLK, BLK), 0)
    qi = lax.broadcasted_iota(jnp.int32, (2 * BLK, BLK), 1)
    shut = jnp.where(first_block, 2 * BLK, 0)
    prev_ok = jnp.logical_and(kj < BLK, kj > qi + shut)
    cur_ok = jnp.logical_and(kj >= BLK, (kj - BLK) <= qi)
    return jnp.logical_or(prev_ok, cur_ok)


def _attn_probs_t(kh, qh, sink, allowed_t):
    s = jnp.where(allowed_t, _dot_nt(kh, qh), -1e30)
    m = jnp.maximum(jnp.max(s, axis=0, keepdims=True), sink)
    p = jnp.exp(s - m)
    ps = jnp.exp(sink - m)
    inv = 1.0 / (jnp.sum(p, axis=0, keepdims=True) + ps)
    return p * inv, ps * inv


def _sgu_mix(gv, lg, lb, w_ref, bt_ref):
    vv, vhat, rstd = _ln(gv, lg, lb)
    vvb = vv.astype(_MXU)
    wcs, mixed = [], []
    for h in range(N_GRP):
        wc = _causal_w(w_ref, h).astype(_MXU)
        wcs.append(wc)
        mixed.append(_dot(wc, vvb[:, h * GRP_DIM:(h + 1) * GRP_DIM]) + bt_ref[:, h:h + 1])
    return jnp.concatenate(mixed, axis=1), vhat, rstd, vvb, wcs


def _mixer_fwd(q, k, v, su, sv, sinks, sg, sb, sgu_w, sgu_bt, plan):
    s_len = q.shape[0]
    nb = s_len // BLK
    per = 2 if nb % 2 == 0 else 1
    steps = nb // per
    m, n = len(plan.operands()), plan.n

    def body(q_ref, kc_ref, kp_ref, vc_ref, vp_ref, su_ref, sv_ref, sink_ref, lg_ref, lb_ref, w_ref, bt_ref, *rest):
        mc_ref = rest[m]
        gather = plan.bind(rest[:m], rest[m + 1:m + 1 + n], rest[m + 1 + n:])
        i = pl.program_id(0)

        @pl.when(i == 0)
        def _():
            gather.start()

        @pl.when(i == max(steps - 2, 0))
        def _():
            gather.pass_on()

        @pl.when(i == steps - 1)
        def _():
            gather.finish()

        for s in range(per):
            rows = slice(s * BLK, (s + 1) * BLK)
            before = slice((s - 1) * BLK, s * BLK)
            k_prev = kp_ref[...] if s == 0 else kc_ref[before, :]
            v_prev = vp_ref[...] if s == 0 else vc_ref[before, :]
            allowed_t = _band_mask_t(i == 0 if s == 0 else False)
            kb = jnp.concatenate([k_prev, kc_ref[rows, :]], axis=0)
            vb = jnp.concatenate([v_prev, vc_ref[rows, :]], axis=0)
            qv = q_ref[rows, :]
            outs = []
            allowed_g = jnp.tile(allowed_t, (1, Q_PER_KV))
            for g in range(N_KV):
                heads = range(g * Q_PER_KV, (g + 1) * Q_PER_KV)
                kh = kb[:, g * HEAD_DIM:(g + 1) * HEAD_DIM]
                vh = vb[:, g * HEAD_DIM:(g + 1) * HEAD_DIM]
                q_g = jnp.concatenate([qv[:, h * HEAD_DIM:(h + 1) * HEAD_DIM] for h in heads], axis=0)
                sink_g = jnp.concatenate([jnp.full((1, BLK), sink_ref[h], F32) for h in heads], axis=1)
                probs_t, _ = _attn_probs_t(kh, q_g, sink_g, allowed_g)
                o_g = _dot_tn(probs_t.astype(_MXU), vh)
                outs += [o_g[hh * BLK:(hh + 1) * BLK, :] for hh in range(Q_PER_KV)]
            u = _gelu_and_grad(su_ref[rows, :])[0]
            gv = _gelu_and_grad(sv_ref[rows, :])[0]
            mixed = _sgu_mix(gv, lg_ref[...], lb_ref[...], w_ref, bt_ref)[0]
            mc_ref[rows, :] = jnp.concatenate(outs + [u * mixed], axis=1).astype(_MXU)

    cur = lambda w: pl.BlockSpec((per * BLK, w), lambda i: (i, 0))
    prev = lambda w: pl.BlockSpec((BLK, w), lambda i: (jnp.maximum(per * i - 1, 0), 0))
    return pl.pallas_call(
        body, name="mixer_fwd", grid=(steps,),
        in_specs=[cur(ATTN_W), cur(KV_W), prev(KV_W), cur(KV_W), prev(KV_W), cur(SGU_W), cur(SGU_W), _smem(),
                  _const2((1, SGU_W)), _const2((1, SGU_W)), _const2((N_GRP, BLK, BLK)), _const2((BLK, N_GRP))]
        + plan.in_specs(),
        out_specs=[cur(D_MODEL)] + plan.out_specs(),
        out_shape=[_hbm_shape((s_len, D_MODEL), _MXU)] + plan.out_shapes(),
        scratch_shapes=plan.scratch(),
        compiler_params=_params(56),
    )(q, k, k, v, v, su, sv, sinks, sg, sb, sgu_w, sgu_bt, *plan.operands())


def _outproj(mc, w_out, b_out, x, g0, b0, plan):
    s_len = x.shape[0]
    tm = _tile(s_len, 512)
    m, n = len(plan.operands()), plan.n

    def body(mc_ref, w_ref, bo_ref, x_ref, g_ref, b_ref, *rest):
        r1_ref = rest[m]
        gather = plan.bind(rest[:m], rest[m + 1:m + 1 + n], rest[m + 1 + n:])
        i = pl.program_id(0)

        @pl.when(i == 0)
        def _():
            gather.start()

        h0, _, _ = _ln(x_ref[...], g_ref[...], b_ref[...])
        r1_ref[...] = ALPHA * h0 + (_dot(mc_ref[...], w_ref[...]) + bo_ref[...])

        last = pl.num_programs(0) - 1

        @pl.when(i == jnp.maximum(last - 1, 0))
        def _():
            gather.pass_on()

        @pl.when(i == last)
        def _():
            gather.finish()

    return pl.pallas_call(
        body, name="outproj", grid=(s_len // tm,),
        in_specs=[_rows(tm, D_MODEL), _vmem(), _const2((1, D_MODEL)), _rows(tm, D_MODEL),
                  _const2((1, D_MODEL)), _const2((1, D_MODEL))] + plan.in_specs(),
        out_specs=[_rows(tm, D_MODEL)] + plan.out_specs(),
        out_shape=[_hbm_shape((s_len, D_MODEL), F32)] + plan.out_shapes(),
        scratch_shapes=plan.scratch(),
        compiler_params=_params(40),
    )(mc, w_out, b_out, x, g0, b0, *plan.operands())


def _ffn_spec(tm):
    return pl.BlockSpec((N_CHIP, tm, FF_SH), lambda i: (0, i, 0))


def _ffn_up(r1, g1, b1, wg, wu, plan):
    s_len = r1.shape[0]
    tm = _tile(s_len, 512)
    m, n = len(plan.operands()), plan.n

    def body(r1_ref, g_ref, b_ref, wg_ref, wu_ref, *rest):
        a_ref, p_ref, q_ref = rest[m:m + 3]
        gather = plan.bind(rest[:m], rest[m + 3:m + 3 + n], rest[m + 3 + n:])
        i = pl.program_id(0)

        @pl.when(i == 0)
        def _():
            gather.start()

        h1, _, _ = _ln(r1_ref[...], g_ref[...], b_ref[...])
        h1b = h1.astype(_MXU)
        for j in range(N_CHIP):
            g = _dot_nt(h1b, wg_ref[j])
            u = _dot_nt(h1b, wu_ref[j])
            silu, sg = _silu_parts(g)
            a_ref[j] = (silu * u).astype(_MXU)
            p_ref[j] = silu.astype(_ACT)
            q_ref[j] = (u * (sg * (1.0 + g * (1.0 - sg)))).astype(_ACT)

        last = pl.num_programs(0) - 1

        @pl.when(i == jnp.maximum(last - 1, 0))
        def _():
            gather.pass_on()

        @pl.when(i == last)
        def _():
            gather.finish()

    sd = _hbm_shape((N_CHIP, s_len, FF_SH), _ACT)
    return pl.pallas_call(
        body, name="ffn_up", grid=(s_len // tm,),
        in_specs=[_rows(tm, D_MODEL), _const2((1, D_MODEL)), _const2((1, D_MODEL)), _vmem(), _vmem()] + plan.in_specs(),
        out_specs=[_ffn_spec(tm)] * 3 + plan.out_specs(),
        out_shape=[_hbm_shape((N_CHIP, s_len, FF_SH), _MXU), sd, sd] + plan.out_shapes(),
        scratch_shapes=plan.scratch(),
        compiler_params=_params(56),
    )(r1, g1, b1, wg, wu, *plan.operands())


def _silu_parts(g):
    sg = 1.0 / (1.0 + jnp.exp(-g))
    return g * sg, sg


def _ffn_down_loss(act, wd, r1, g1, b1, g2, b2, target):
    s_len = r1.shape[0]
    tm = _tile(s_len, 512)

    parts = 2 if tm % 32 == 0 else 1
    sub = tm // parts

    def body(a_ref, wd_ref, r1_ref, g1_ref, b1_ref, g2_ref, b2_ref, t_ref, dr2_ref, loss_ref, dg2_ref, db2_ref):
        i = pl.program_id(0)

        @pl.when(i == 0)
        def _():
            loss_ref[...] = jnp.zeros_like(loss_ref)
            dg2_ref[...] = jnp.zeros_like(dg2_ref)
            db2_ref[...] = jnp.zeros_like(db2_ref)

        for part in range(parts):
            rows = slice(part * sub, (part + 1) * sub)
            f = jnp.zeros((sub, D_MODEL), F32)
            for j in range(N_CHIP):
                f = f + _dot(a_ref[j, rows, :], wd_ref[j])
            h1, _, _ = _ln(r1_ref[rows, :], g1_ref[...], b1_ref[...])
            h2, r2hat, rstd2 = _ln(ALPHA * h1 + f, g2_ref[...], b2_ref[...])
            diff = h2 - t_ref[rows, :]
            dh2 = diff * (1.0 / D_MODEL)
            loss_ref[...] += _colsum(diff * diff)
            dg2_ref[...] += _colsum(dh2 * r2hat)
            db2_ref[...] += _colsum(dh2)
            dr2_ref[rows, :] = _ln_bwd(dh2, r2hat, rstd2, g2_ref[...])

    vec = _hbm_shape((1, D_MODEL), F32)
    c = _const2((1, D_MODEL))
    return pl.pallas_call(
        body, name="ffn_down_loss", grid=(s_len // tm,),
        in_specs=[_ffn_spec(tm), _vmem(), _rows(tm, D_MODEL), c, c, c, c, _rows(tm, D_MODEL)],
        out_specs=[_rows(tm, D_MODEL), c, c, c],
        out_shape=[_hbm_shape((s_len, D_MODEL), F32), vec, vec, vec],
        compiler_params=_params(48),
    )(act, wd, r1, g1, b1, g2, b2, target)


def _ffn_bwd_a(dr2, act, p_act, q_act, wd):
    s_len = dr2.shape[0]
    tm = _tile(s_len, 512)

    def body(dr2_ref, a_ref, p_ref, q_ref, wd_ref, dg_ref, du_ref, wire_ref, own_ref,
             dwd_ref, land_ref, send_sem, recv_sem):
        i = pl.program_id(0)

        @pl.when(i == 0)
        def _():
            dwd_ref[...] = jnp.zeros_like(dwd_ref)

        dfb = dr2_ref[...].astype(_MXU)
        for j in range(N_CHIP):
            da = _dot_nt(dfb, wd_ref[j])
            dg_ref[j] = (da * q_ref[j].astype(F32)).astype(_MXU)
            du_ref[j] = (da * p_ref[j].astype(F32)).astype(_MXU)
            dwd_ref[j * FF_SH:(j + 1) * FF_SH, :] += _dot_tn(a_ref[j], dfb)

        @pl.when(i == pl.num_programs(0) - 1)
        def _():
            _pair_reduce(dwd_ref, wire_ref, own_ref, land_ref, send_sem, recv_sem)

    sd = _hbm_shape((N_CHIP, s_len, FF_SH), _MXU)
    half = (N_CHIP, FF_SH // 2, D_MODEL)
    return pl.pallas_call(
        body, name="ffn_bwd_a", grid=(s_len // tm,),
        in_specs=[_rows(tm, D_MODEL), _ffn_spec(tm), _ffn_spec(tm), _ffn_spec(tm), _vmem()],
        out_specs=[_ffn_spec(tm), _ffn_spec(tm), _vmem(), _vmem()],
        out_shape=[sd, sd] + _pair_out_shapes(half),
        scratch_shapes=_pair_scratch((D_FF, D_MODEL), half),
        compiler_params=_params(61),
    )(dr2, act, p_act, q_act, wd)


def _ffn_bwd_g(dr2, dg, r1, g1, b1, wg, prev_wire):
    s_len = dr2.shape[0]
    tm = _tile(s_len, 512)

    def body(dr2_ref, dg_ref, r1_ref, g1_ref, b1_ref, wg_ref, pw_ref, dh1_ref, wire_ref, own_ref, pl_ref,
             dwg_ref, land_ref, send_sem, recv_sem, xl_ref, x_send, x_recv, x_flush):
        i = pl.program_id(0)
        exchange = _ChipExchange(pw_ref, xl_ref, x_send, x_recv)

        @pl.when(i == 0)
        def _():
            exchange.start()
            dwg_ref[...] = jnp.zeros_like(dwg_ref)

        h1, _, _ = _ln(r1_ref[...], g1_ref[...], b1_ref[...])
        h1b = h1.astype(_MXU)
        dh1 = ALPHA * dr2_ref[...]
        for j in range(N_CHIP):
            dgj = dg_ref[j]
            dh1 = dh1 + _dot(dgj, wg_ref[j])
            dwg_ref[j * FF_SH:(j + 1) * FF_SH, :] += _dot_tn(dgj, h1b)
        dh1_ref[...] = dh1

        @pl.when(i == pl.num_programs(0) - 1)
        def _():
            _pair_reduce(dwg_ref, wire_ref, own_ref, land_ref, send_sem, recv_sem)
            exchange.finish_to(pl_ref, x_flush)

    c = _const2((1, D_MODEL))
    half = (N_CHIP, FF_SH // 2, D_MODEL)
    return pl.pallas_call(
        body, name="ffn_bwd_g", grid=(s_len // tm,),
        in_specs=[_rows(tm, D_MODEL), _ffn_spec(tm), _rows(tm, D_MODEL), c, c, _vmem(), _vmem()],
        out_specs=[_rows(tm, D_MODEL), _vmem(), _vmem(), _hbm()],
        out_shape=[_hbm_shape((s_len, D_MODEL), F32)] + _pair_out_shapes(half) + [_ChipExchange.land_shape(prev_wire)],
        scratch_shapes=_pair_scratch((D_FF, D_MODEL), half) + _ChipExchange.scratch(prev_wire),
        compiler_params=_params(58),
    )(dr2, dg, r1, g1, b1, wg, prev_wire)


def _ffn_bwd_u(dh1a, du, r1, g1, b1, wu, prev_wire):
    s_len = dh1a.shape[0]
    tm = _tile(s_len, 512)

    def body(dh1_ref, du_ref, r1_ref, g1_ref, b1_ref, wu_ref, pw_ref,
             dr1_ref, wire_ref, own_ref, dg1_ref, db1_ref, pl_ref,
             dwu_ref, land_ref, send_sem, recv_sem, xl_ref, x_send, x_recv, x_flush):
        i = pl.program_id(0)
        exchange = _ChipExchange(pw_ref, xl_ref, x_send, x_recv)

        @pl.when(i == 0)
        def _():
            exchange.start()
            dwu_ref[...] = jnp.zeros_like(dwu_ref)
            dg1_ref[...] = jnp.zeros_like(dg1_ref)
            db1_ref[...] = jnp.zeros_like(db1_ref)

        h1, r1hat, rstd1 = _ln(r1_ref[...], g1_ref[...], b1_ref[...])
        h1b = h1.astype(_MXU)
        dh1 = dh1_ref[...]
        for j in range(N_CHIP):
            duj = du_ref[j]
            dh1 = dh1 + _dot(duj, wu_ref[j])
            dwu_ref[j * FF_SH:(j + 1) * FF_SH, :] += _dot_tn(duj, h1b)
        dg1_ref[...] += _colsum(dh1 * r1hat)
        db1_ref[...] += _colsum(dh1)
        dr1_ref[...] = _ln_bwd(dh1, r1hat, rstd1, g1_ref[...])

        @pl.when(i == pl.num_programs(0) - 1)
        def _():
            _pair_reduce(dwu_ref, wire_ref, own_ref, land_ref, send_sem, recv_sem)
            exchange.finish_to(pl_ref, x_flush)

    vec = _hbm_shape((1, D_MODEL), F32)
    c = _const2((1, D_MODEL))
    half = (N_CHIP, FF_SH // 2, D_MODEL)
    return pl.pallas_call(
        body, name="ffn_bwd_u", grid=(s_len // tm,),
        in_specs=[_rows(tm, D_MODEL), _ffn_spec(tm), _rows(tm, D_MODEL), c, c, _vmem(), _vmem()],
        out_specs=[_rows(tm, D_MODEL), _vmem(), _vmem(), c, c, _hbm()],
        out_shape=[_hbm_shape((s_len, D_MODEL), F32)] + _pair_out_shapes(half)
        + [vec, vec, _ChipExchange.land_shape(prev_wire)],
        scratch_shapes=_pair_scratch((D_FF, D_MODEL), half) + _ChipExchange.scratch(prev_wire),
        compiler_params=_params(58),
    )(dh1a, du, r1, g1, b1, wu, prev_wire)


def _outproj_bwd(dr1, mc, w_out):
    s_len = dr1.shape[0]
    tm = _tile(s_len, 512)

    def body(dr1_ref, mc_ref, w_ref, dmc_ref, wire_ref, own_ref, db_ref, dw_ref, land_ref, send_sem, recv_sem):
        i = pl.program_id(0)

        @pl.when(i == 0)
        def _():
            dw_ref[...] = jnp.zeros_like(dw_ref)
            db_ref[...] = jnp.zeros_like(db_ref)

        d = dr1_ref[...]
        db_ref[...] += _colsum(d)
        db16 = d.astype(_MXU)
        dmc_ref[...] = _dot_nt(db16, w_ref[...])
        dw_ref[...] += _dot_tn(mc_ref[...], db16)

        @pl.when(i == pl.num_programs(0) - 1)
        def _():
            _pair_reduce(dw_ref, wire_ref, own_ref, land_ref, send_sem, recv_sem)

    half = (N_CHIP, OUT_SH // 2, D_MODEL)
    return pl.pallas_call(
        body, name="outproj_bwd", grid=(s_len // tm,),
        in_specs=[_rows(tm, D_MODEL), _rows(tm, D_MODEL), _vmem()],
        out_specs=[_rows(tm, D_MODEL), _vmem(), _vmem(), _const2((1, D_MODEL))],
        out_shape=[_hbm_shape((s_len, D_MODEL), F32)] + _pair_out_shapes(half) + [_hbm_shape((1, D_MODEL), F32)],
        scratch_shapes=_pair_scratch((D_MODEL, D_MODEL), half),
        compiler_params=_params(48),
    )(dr1, mc, w_out)


def _mixer_bwd(q, k, v, su, sv, dmc, tc, t1, t2, sinks, sg, sb, sgu_w, sgu_bt, prev_wires):
    s_len = q.shape[0]
    nb = s_len // BLK
    per = next(p for p in (4, 2, 1) if nb % p == 0)
    steps = nb // per

    def body(q_ref, kc_ref, kp_ref, vc_ref, vp_ref, su_ref, sv_ref, dmc_ref,
             tc_ref, t1_ref, t2_ref, tcp_ref, t1p_ref, t2p_ref,
             sink_ref, lg_ref, lb_ref, w_ref, bt_ref, pw0_ref, pw1_ref,
             dq_ref, dkv_ref, dsuv_ref, dbq_ref, dbkv_ref, dbsuv_ref,
             dsink_ref, dlg_ref, dlb_ref, dw_ref, dbt_ref, pl0_ref, pl1_ref, carry_ref,
             xl0_ref, x0_send, x0_recv, x0_flush, xl1_ref, x1_send, x1_recv, x1_flush):
        i = pl.program_id(0)
        exchanges = [(_ChipExchange(pw0_ref, xl0_ref, x0_send, x0_recv), pl0_ref, x0_flush),
                     (_ChipExchange(pw1_ref, xl1_ref, x1_send, x1_recv), pl1_ref, x1_flush)]

        @pl.when(i == 0)
        def _():
            for exchange, _, _ in exchanges:
                exchange.start()

        @pl.when(i == 0)
        def _():
            for r in (dbq_ref, dbkv_ref, dbsuv_ref, dsink_ref, dlg_ref, dlb_ref, dw_ref, dbt_ref, carry_ref):
                r[...] = jnp.zeros_like(r)

        def emit_kv(fin, t):
            if t == 0:
                tables = (tcp_ref[...], t1p_ref[...], t2p_ref[...])
            else:
                before = slice((t - 1) * BLK, t * BLK)
                tables = (tc_ref[before, :], t1_ref[before, :], t2_ref[before, :])
            dk = _rope_bwd(fin[:, 0:KV_W], *tables)
            out = jnp.concatenate([dk, fin[:, KV_W:2 * KV_W]], axis=1)
            dkv_ref[t * BLK:(t + 1) * BLK, :] = out.astype(_MXU)
            dbkv_ref[...] += _colsum(out)

        def one_block(s):
            rows = slice(s * BLK, (s + 1) * BLK)
            before = slice((s - 1) * BLK, s * BLK)
            k_prev = kp_ref[...] if s == 0 else kc_ref[before, :]
            v_prev = vp_ref[...] if s == 0 else vc_ref[before, :]
            allowed_t = _band_mask_t(i == 0 if s == 0 else False)
            kb = jnp.concatenate([k_prev, kc_ref[rows, :]], axis=0)
            vb = jnp.concatenate([v_prev, vc_ref[rows, :]], axis=0)
            qv = q_ref[rows, :]
            dmc = dmc_ref[rows, :]
            dqs, dks, dvs, dsinks = [], [], [], []
            allowed_g = jnp.tile(allowed_t, (1, Q_PER_KV))
            for g in range(N_KV):
                heads = range(g * Q_PER_KV, (g + 1) * Q_PER_KV)
                kh = kb[:, g * HEAD_DIM:(g + 1) * HEAD_DIM]
                vh = vb[:, g * HEAD_DIM:(g + 1) * HEAD_DIM]
                q_g = jnp.concatenate([qv[:, h * HEAD_DIM:(h + 1) * HEAD_DIM] for h in heads], axis=0)
                do_g = jnp.concatenate([dmc[:, h * HEAD_DIM:(h + 1) * HEAD_DIM] for h in heads], axis=0).astype(_MXU)
                sink_g = jnp.concatenate([jnp.full((1, BLK), sink_ref[h], F32) for h in heads], axis=1)
                probs_t, psink = _attn_probs_t(kh, q_g, sink_g, allowed_g)
                dvs.append(_dot(probs_t.astype(_MXU), do_g))
                dp_t = _dot_nt(vh, do_g)
                rd = jnp.sum(probs_t * dp_t, axis=0, keepdims=True)
                ds_t = (probs_t * (dp_t - rd)).astype(_MXU)
                ps_rd = psink * rd
                for hh in range(Q_PER_KV):
                    dsinks.append(-jnp.sum(ps_rd[:, hh * BLK:(hh + 1) * BLK], axis=1, keepdims=True))
                dq_g = _dot_tn(ds_t, kh)
                dqs += [dq_g[hh * BLK:(hh + 1) * BLK, :] for hh in range(Q_PER_KV)]
                dks.append(_dot(ds_t, q_g))
            dq = _rope_bwd(jnp.concatenate(dqs, axis=1) * (HEAD_DIM ** -0.5),
                           tc_ref[rows, :], t1_ref[rows, :], t2_ref[rows, :])
            dq_ref[rows, :] = dq.astype(_MXU)
            dbq_ref[...] += _colsum(dq)
            dsink_ref[...] += _lane_put(dsinks, 128)
            contrib = jnp.concatenate(dks + dvs, axis=1)

            lg = lg_ref[...]
            u, du_dsu = _gelu_and_grad(su_ref[rows, :])
            gv, dgv_dsv = _gelu_and_grad(sv_ref[rows, :])
            mixed, vhat, rstd, vvb, wcs = _sgu_mix(gv, lg, lb_ref[...], w_ref, bt_ref)
            dsgu = dmc[:, ATTN_W:D_MODEL]
            dsu = dsgu * mixed * du_dsu
            dmixed = dsgu * u
            tri_t = lax.broadcasted_iota(jnp.int32, (BLK, BLK), 0)
            tri_s = lax.broadcasted_iota(jnp.int32, (BLK, BLK), 1)
            dvv, dbs = [], []
            for h in range(N_GRP):
                dm = dmixed[:, h * GRP_DIM:(h + 1) * GRP_DIM]
                dmb = dm.astype(_MXU)
                dbs.append(jnp.sum(dm, axis=1, keepdims=True))
                dw_ref[h] += jnp.where(tri_s <= tri_t, _dot_nt(dmb, vvb[:, h * GRP_DIM:(h + 1) * GRP_DIM]), 0.0)
                dvv.append(_dot_tn(wcs[h], dmb))
            dvv = jnp.concatenate(dvv, axis=1)
            dbt_ref[...] += _lane_put(dbs, 128)
            dlg_ref[...] += _colsum(dvv * vhat)
            dlb_ref[...] += _colsum(dvv)
            dsv = _ln_bwd(dvv, vhat, rstd, lg) * dgv_dsv
            dsuv = jnp.concatenate([dsu, dsv], axis=1)
            dsuv_ref[rows, :] = dsuv.astype(_MXU)
            dbsuv_ref[...] += _colsum(dsuv)
            return contrib

        @pl.when(i < steps)
        def _():
            contribs = [one_block(s) for s in range(per)]
            for t in range(per):
                top = carry_ref[...] if t == 0 else contribs[t - 1][BLK:2 * BLK, :]
                emit_kv(top + contribs[t][0:BLK, :], t)
            carry_ref[...] = contribs[per - 1][BLK:2 * BLK, :]

        @pl.when(i == steps)
        def _():
            emit_kv(carry_ref[...], 0)
            if per > 1:
                dkv_ref[BLK:per * BLK, :] = jnp.zeros(((per - 1) * BLK, 2 * KV_W), _MXU)
            for exchange, landed, flush_sem in exchanges:
                exchange.finish_to(landed, flush_sem)

    last = steps - 1
    cur = lambda w: pl.BlockSpec((per * BLK, w), lambda i: (jnp.minimum(i, last), 0))
    prev = lambda w: pl.BlockSpec((BLK, w), lambda i: (jnp.clip(per * i - 1, 0, nb - 1), 0))
    shifted = pl.BlockSpec((per * BLK, 2 * KV_W), lambda i: (i, 0))
    sd = _hbm_shape
    return pl.pallas_call(
        body, name="mixer_bwd", grid=(steps + 1,),
        in_specs=[cur(ATTN_W), cur(KV_W), prev(KV_W), cur(KV_W), prev(KV_W), cur(SGU_W), cur(SGU_W), cur(D_MODEL),
                  cur(128), cur(128), cur(128), prev(128), prev(128), prev(128),
                  _smem(), _const2((1, SGU_W)), _const2((1, SGU_W)), _const2((N_GRP, BLK, BLK)), _const2((BLK, N_GRP)),
                  _vmem(), _vmem()],
        out_specs=[cur(ATTN_W), shifted, cur(2 * SGU_W),
                   _const2((1, ATTN_W)), _const2((1, 2 * KV_W)), _const2((1, 2 * SGU_W)),
                   _const2((1, 128)), _const2((1, SGU_W)), _const2((1, SGU_W)),
                   _const2((N_GRP, BLK, BLK)), _const2((BLK, 128)), _hbm(), _hbm()],
        out_shape=[sd((s_len, ATTN_W), _MXU), sd((s_len + per * BLK, 2 * KV_W), _MXU), sd((s_len, 2 * SGU_W), _MXU),
                   sd((1, ATTN_W), F32), sd((1, 2 * KV_W), F32), sd((1, 2 * SGU_W), F32),
                   sd((1, 128), F32), sd((1, SGU_W), F32), sd((1, SGU_W), F32),
                   sd((N_GRP, BLK, BLK), F32), sd((BLK, 128), F32)]
        + [_ChipExchange.land_shape(w) for w in prev_wires],
        scratch_shapes=[pltpu.VMEM((BLK, 2 * KV_W), F32)] + _ChipExchange.scratch(prev_wires[0])
        + _ChipExchange.scratch(prev_wires[1]),
        compiler_params=_params(40),
    )(q, k, k, v, v, su, sv, dmc, tc, t1, t2, tc, t1, t2, sinks, sg, sb, sgu_w, sgu_bt, *prev_wires)


def _inproj_bwd(dq, dkv, dsuv, dr1, x, g0, b0, w_in):
    s_len = x.shape[0]
    tm = _tile(s_len, 512)
    cuts = ((0, ATTN_W), (ATTN_W, ATTN_W + 2 * KV_W), (ATTN_W + 2 * KV_W, IN_W))

    def body(dq_ref, dkv_ref, dsuv_ref, dr1_ref, x_ref, g_ref, b_ref, w_ref, dx_ref, dw_ref, dg_ref, db_ref):
        i = pl.program_id(0)

        @pl.when(i == 0)
        def _():
            dw_ref[...] = jnp.zeros_like(dw_ref)
            dg_ref[...] = jnp.zeros_like(dg_ref)
            db_ref[...] = jnp.zeros_like(db_ref)

        h0, xhat, rstd = _ln(x_ref[...], g_ref[...], b_ref[...])
        h0b = h0.astype(_MXU)
        dh0 = ALPHA * dr1_ref[...]
        for (lo, hi), d_ref in zip(cuts, (dq_ref, dkv_ref, dsuv_ref)):
            d = d_ref[...]
            dh0 = dh0 + _dot(d, w_ref[lo:hi, :])
            dw_ref[lo:hi, :] += _dot_tn(d, h0b)
        dg_ref[...] += _colsum(dh0 * xhat)
        db_ref[...] += _colsum(dh0)
        dx_ref[...] = _ln_bwd(dh0, xhat, rstd, g_ref[...])

    vec = _hbm_shape((1, D_MODEL), F32)
    c = _const2((1, D_MODEL))
    return pl.pallas_call(
        body, name="inproj_bwd", grid=(s_len // tm,),
        in_specs=[_rows(tm, ATTN_W), _rows(tm, 2 * KV_W), _rows(tm, 2 * SGU_W), _rows(tm, D_MODEL), _rows(tm, D_MODEL),
                  c, c, _vmem()],
        out_specs=[_rows(tm, D_MODEL), _vmem(), c, c],
        out_shape=[_hbm_shape((s_len, D_MODEL), F32), jax.ShapeDtypeStruct((IN_W, D_MODEL), F32), vec, vec],
        compiler_params=_params(48),
    )(dq, dkv, dsuv, dr1, x, g0, b0, w_in)


def _place():
    x, y, c = (lax.axis_index(a) for a in MESH_AXES)
    chips = [(1 - x, y), (x, 1 - y), (1 - x, 1 - y)]
    return x, y, c, chips


class _Gather:
    def __init__(self, ins, outs, send_sems, recv_sems, spans=None):
        self.ins, self.outs, self.send_sems, self.recv_sems = ins, outs, send_sems, recv_sems
        self.n = len(ins)
        self.spans = spans or [(0, r.shape[0]) for r in ins]
        self.halves = [(hi - lo) // 2 for lo, hi in self.spans]

    def _copy(self, k, t, slot, half, to):
        rows = pl.ds(pl.multiple_of(self.spans[t][0] + half * self.halves[t], 16), self.halves[t])
        piece = self.outs[t].at[slot, rows, :]
        return pltpu.make_async_remote_copy(src_ref=piece, dst_ref=piece, send_sem=self.send_sems.at[k],
                                            recv_sem=self.recv_sems.at[k], device_id=to, device_id_type=MESH)

    def _chip_copy(self, t, d, slot):
        x, y, c, chips = _place()
        return self._copy(3 * t + d, t, slot, c, (chips[d][0], chips[d][1], c))

    def _pass_copy(self, t, d, half):
        x, y, c, chips = _place()
        return self._copy(3 * self.n + 3 * t + d, t, 2 * chips[d][0] + chips[d][1], half, (x, y, 1 - c))

    def start(self):
        x, y, c, chips = _place()
        me = 2 * x + y
        for t in range(self.n):
            lo, hi = self.spans[t]
            self.outs[t][me, lo:hi, :] = self.ins[t][lo:hi, :].astype(_WIRE)
        for t in range(self.n):
            for d in range(3):
                self._chip_copy(t, d, me).start()

    def pass_on(self):
        x, y, c, chips = _place()
        for t in range(self.n):
            for d in range(3):
                self._chip_copy(t, d, 2 * chips[d][0] + chips[d][1]).wait_recv()
                self._pass_copy(t, d, c).start()

    def finish(self):
        x, y, c, chips = _place()
        me = 2 * x + y
        for t in range(self.n):
            for d in range(3):
                self._pass_copy(t, d, 1 - c).wait_recv()
        for t in range(self.n):
            for d in range(3):
                self._chip_copy(t, d, me).wait_send()
                self._pass_copy(t, d, c).wait_send()

    @staticmethod
    def out_shapes(shards, make=jax.ShapeDtypeStruct):
        return [make((N_CHIP,) + s.shape, _WIRE) for s in shards]

    @staticmethod
    def sems(n):
        return [pltpu.SemaphoreType.DMA((6 * n,)), pltpu.SemaphoreType.DMA((6 * n,))]


class _GatherPlan:
    def __init__(self, pieces):
        self.shards = [p[0] for p in pieces]
        self.spans = [p[1] for p in pieces]
        self.earlier = [p[2] for p in pieces]
        self.n = len(pieces)
        self.carried = [t for t in range(self.n) if self.earlier[t] is not None]

    def operands(self):
        return self.shards + [self.earlier[t] for t in self.carried]

    def in_specs(self):
        return [_vmem()] * self.n + [_hbm()] * len(self.carried)

    def out_specs(self):
        return [_hbm()] * self.n

    def out_shapes(self):
        return _Gather.out_shapes(self.shards, _hbm_shape)

    def scratch(self):
        return ([pltpu.VMEM((N_CHIP,) + s.shape, _WIRE) for s in self.shards] + _Gather.sems(self.n)
                + [pltpu.SemaphoreType.DMA((self.n,)), pltpu.SemaphoreType.DMA((max(len(self.carried), 1),))])

    def bind(self, in_refs, out_refs, scratch_refs):
        plan = self
        shard_refs, earlier_refs = in_refs[:self.n], in_refs[self.n:]
        bufs = scratch_refs[:self.n]
        send_sems, recv_sems, flush_sems, carry_sems = scratch_refs[self.n:self.n + 4]
        gather = _Gather(shard_refs, bufs, send_sems, recv_sems, self.spans)

        def carry_copy(k):
            t = plan.carried[k]
            lo = plan.spans[t][0]
            return pltpu.make_async_copy(earlier_refs[k].at[:, 0:lo, :], bufs[t].at[:, 0:lo, :], carry_sems.at[k])

        class Bound:
            @staticmethod
            def start():
                for k in range(len(plan.carried)):
                    carry_copy(k).start()
                gather.start()

            @staticmethod
            def pass_on():
                gather.pass_on()

            @staticmethod
            def finish():
                gather.finish()
                for k in range(len(plan.carried)):
                    carry_copy(k).wait()
                _flush([bufs[t].at[:, 0:plan.spans[t][1], :] for t in range(plan.n)],
                       [out_refs[t].at[:, 0:plan.spans[t][1], :] for t in range(plan.n)], flush_sems)

        return Bound


def _flush(bufs, hbm_outs, sems):
    copies = [pltpu.make_async_copy(b, o, sems.at[k]) for k, (b, o) in enumerate(zip(bufs, hbm_outs))]
    for cp in copies:
        cp.start()
    for cp in copies:
        cp.wait()


def _gather_weights(shards):
    n = len(shards)

    def body(*refs):
        gather = _Gather(refs[:n], refs[n:2 * n], refs[2 * n], refs[2 * n + 1])
        gather.start()
        gather.pass_on()
        gather.finish()

    return pl.pallas_call(
        body, name="gather_weights",
        in_specs=[_vmem()] * n, out_specs=[_vmem()] * n,
        out_shape=_Gather.out_shapes(shards), scratch_shapes=_Gather.sems(n),
        compiler_params=pltpu.CompilerParams(vmem_limit_bytes=32 * MIB),
    )(*shards)


class _ChipExchange:
    def __init__(self, wire_ref, land_ref, send_sems, recv_sems):
        self.wire, self.land, self.send_sems, self.recv_sems = wire_ref, land_ref, send_sems, recv_sems

    def _copy(self, d):
        x, y, c, chips = _place()
        return pltpu.make_async_remote_copy(
            src_ref=self.wire.at[2 * chips[d][0] + chips[d][1]], dst_ref=self.land.at[d],
            send_sem=self.send_sems.at[d], recv_sem=self.recv_sems.at[d],
            device_id=(chips[d][0], chips[d][1], c), device_id_type=MESH)

    def start(self):
        for d in range(3):
            self._copy(d).start()

    def wait_recv(self):
        for d in range(3):
            self._copy(d).wait_recv()

    def wait_send(self):
        for d in range(3):
            self._copy(d).wait_send()

    def finish_to(self, hbm_out, flush_sem):
        self.wait_recv()
        _flush([self.land], [hbm_out], flush_sem)
        self.wait_send()

    @staticmethod
    def land_shape(wire):
        return _hbm_shape((3,) + wire.shape[1:], wire.dtype)

    @staticmethod
    def sems():
        return [pltpu.SemaphoreType.DMA((3,)), pltpu.SemaphoreType.DMA((3,))]

    @staticmethod
    def scratch(wire):
        return ([pltpu.VMEM((3,) + wire.shape[1:], wire.dtype)] + _ChipExchange.sems() + [pltpu.SemaphoreType.DMA((1,))])


def _pair_out_shapes(half_shape):
    return [jax.ShapeDtypeStruct(half_shape, _WIRE), jax.ShapeDtypeStruct(half_shape[1:], F32)]


def _pair_scratch(acc_shape, half_shape):
    return [pltpu.VMEM(acc_shape, F32), pltpu.VMEM(half_shape, _WIRE),
            pltpu.SemaphoreType.DMA((N_CHIP,)), pltpu.SemaphoreType.DMA((N_CHIP,))]


def _pair_reduce(acc_ref, wire_ref, own_ref, land_ref, send_sems, recv_sems):
    rh = land_ref.shape[1]
    x, y, c, _ = _place()
    me = 2 * x + y
    copies = []
    for j in range(N_CHIP):
        def cast(r, carry, j=j):
            dst = pl.ds(pl.multiple_of(r * ROW_CHUNK, ROW_CHUNK), ROW_CHUNK)
            src = pl.ds(pl.multiple_of((2 * j + 1 - c) * rh + r * ROW_CHUNK, 8), ROW_CHUNK)
            wire_ref[j, dst, :] = acc_ref[src, :].astype(_WIRE)
            return carry

        lax.fori_loop(0, rh // ROW_CHUNK, cast, 0)
        cp = pltpu.make_async_remote_copy(src_ref=wire_ref.at[j], dst_ref=land_ref.at[j], send_sem=send_sems.at[j],
                                          recv_sem=recv_sems.at[j], device_id=(x, y, 1 - c), device_id_type=MESH)
        cp.start()
        copies.append(cp)
    for j in range(N_CHIP):
        copies[j].wait()

        def chunk(r, carry, j=j):
            theirs = pl.ds(pl.multiple_of(r * ROW_CHUNK, ROW_CHUNK), ROW_CHUNK)
            mine = pl.ds(pl.multiple_of((2 * j + c) * rh + r * ROW_CHUNK, 8), ROW_CHUNK)
            wire_ref[j, theirs, :] = (acc_ref[mine, :] + land_ref[j, theirs, :].astype(F32)).astype(_WIRE)
            return carry

        lax.fori_loop(0, rh // ROW_CHUNK, chunk, 0)

    def own_chunk(r, carry):
        theirs = pl.ds(pl.multiple_of(r * ROW_CHUNK, ROW_CHUNK), ROW_CHUNK)
        mine = pl.ds(pl.multiple_of((2 * me + c) * rh + r * ROW_CHUNK, 8), ROW_CHUNK)
        own_ref[theirs, :] = acc_ref[mine, :] + land_ref[me, theirs, :].astype(F32)
        return carry

    lax.fori_loop(0, rh // ROW_CHUNK, own_chunk, 0)


def _grad_finish(last_acc, lands, owns):
    n = len(owns) + 1
    halves = [last_acc.shape[0] // (2 * N_CHIP)] + [w.shape[1] for w in lands]
    widths = [last_acc.shape[1]] + [a.shape[1] for a in owns]

    def body(*refs):
        acc0, land, own, g = refs[0], (None,) + refs[1:n], (None,) + refs[n:2 * n - 1], refs[2 * n - 1:3 * n - 1]
        pland0, wire0, land0, own0 = refs[3 * n - 1:3 * n + 3]
        p_send, p_recv, x_send, x_recv, pair_send, pair_recv = refs[3 * n + 3:3 * n + 9]
        land = (land0,) + land[1:]
        own = (own0,) + own[1:]
        x, y, c, chips = _place()
        me = 2 * x + y
        exchange = _ChipExchange(wire0, land0, x_send, x_recv)

        def half_rows(t, half):
            return pl.ds(pl.multiple_of(half * halves[t], 8), halves[t])

        def pair_copy(t, half):
            rows = g[t].at[half_rows(t, half), :]
            return pltpu.make_async_remote_copy(src_ref=rows, dst_ref=rows, send_sem=pair_send.at[t],
                                                recv_sem=pair_recv.at[t], device_id=(x, y, 1 - c), device_id_type=MESH)

        _pair_reduce(acc0, wire0, own0, pland0, p_send, p_recv)
        exchange.start()

        for t in list(range(1, n)) + [0]:
            if t == 0:
                exchange.wait_recv()

            def chunk(r, carry, t=t):
                src = pl.ds(pl.multiple_of(r * ROW_CHUNK, ROW_CHUNK), ROW_CHUNK)
                dst = pl.ds(pl.multiple_of(c * halves[t] + r * ROW_CHUNK, 8), ROW_CHUNK)
                s = own[t][src, :]
                for d in range(3):
                    s = s + land[t][d, src, :].astype(F32)
                g[t][dst, :] = s
                return carry

            lax.fori_loop(0, halves[t] // ROW_CHUNK, chunk, 0)
            pair_copy(t, c).start()
        for t in range(n):
            pair_copy(t, 1 - c).wait_recv()
        for t in range(n):
            pair_copy(t, c).wait_send()
        exchange.wait_send()

    half0 = (halves[0], widths[0])
    return pl.pallas_call(
        body, name="grad_finish",
        in_specs=[_vmem()] * (2 * n - 1), out_specs=[_vmem()] * n,
        out_shape=[jax.ShapeDtypeStruct((2 * h, w), F32) for h, w in zip(halves, widths)],
        scratch_shapes=[pltpu.VMEM((N_CHIP,) + half0, _WIRE), pltpu.VMEM((N_CHIP,) + half0, _WIRE),
                        pltpu.VMEM((3,) + half0, _WIRE), pltpu.VMEM(half0, F32)]
        + [pltpu.SemaphoreType.DMA((N_CHIP,)), pltpu.SemaphoreType.DMA((N_CHIP,))]
        + _ChipExchange.sems()
        + [pltpu.SemaphoreType.DMA((n,)), pltpu.SemaphoreType.DMA((n,))],
        compiler_params=pltpu.CompilerParams(vmem_limit_bytes=56 * MIB),
    )(last_acc, *lands, *owns)


_SMALL = ("ln_in_g", "ln_in_b", "b_in", "attn_sinks", "sgu_ln_g", "sgu_ln_b", "sgu_w", "sgu_b", "b_out",
          "ln_mix_g", "ln_mix_b", "ln_ffn_g", "ln_ffn_b")
_VEC_ROW = dict(ln_in_g=0, ln_in_b=1, b_in=2, attn_sinks=4, sgu_ln_g=5, sgu_ln_b=6, b_out=7, ln_mix_g=8, ln_mix_b=9,
                ln_ffn_g=10, ln_ffn_b=11)
_LOSS_ROW = 12
_VEC_ROWS = 16
_MAT_ROWS = N_GRP * BLK + BLK


def _small_allreduce(local):
    n_in = 16

    def body(*refs):
        (g_ln_in_g, g_ln_in_b, g_bq, g_bkv, g_bsuv, g_sink, g_sln_g, g_sln_b, g_sw, g_sbt, g_bout,
         g_lmg, g_lmb, g_lfg, g_lfb, g_loss) = refs[:n_in]
        out_a, out_b = refs[n_in:n_in + 2]
        (buf_a, buf_b, pair_a, pair_b, stage_a, stage_b, tot_a, tot_b,
         p1_send, p1_recv, x_send, x_recv, p2_send, p2_recv) = refs[n_in + 2:]
        x, y, c, chips = _place()
        me = 2 * x + y
        sibling = (x, y, 1 - c)
        half_a, half_b = _VEC_ROWS // 2, _MAT_ROWS // 2

        buf_a[...] = jnp.zeros_like(buf_a)
        for row, ref in ((0, g_ln_in_g), (1, g_ln_in_b), (7, g_bout), (8, g_lmg), (9, g_lmb), (10, g_lfg), (11, g_lfb),
                         (_LOSS_ROW, g_loss)):
            buf_a[row:row + 1, :] = ref[...]
        buf_a[2:3, 0:ATTN_W] = g_bq[...]
        buf_a[2:3, ATTN_W:ATTN_W + 2 * KV_W] = g_bkv[...]
        buf_a[2:3, ATTN_W + 2 * KV_W:D_MODEL] = g_bsuv[:, 0:2 * KV_W]
        buf_a[3:4, 0:2 * SGU_W - 2 * KV_W] = g_bsuv[:, 2 * KV_W:2 * SGU_W]
        buf_a[4:5, 0:128] = g_sink[...]
        buf_a[5:6, 0:SGU_W] = g_sln_g[...]
        buf_a[6:7, 0:SGU_W] = g_sln_b[...]
        for h in range(N_GRP):
            buf_b[h * BLK:(h + 1) * BLK, :] = g_sw[h]
        buf_b[N_GRP * BLK:_MAT_ROWS, :] = g_sbt[...]

        def remote(src, dst, send_sem, recv_sem, to):
            return pltpu.make_async_remote_copy(src_ref=src, dst_ref=dst, send_sem=send_sem, recv_sem=recv_sem,
                                                device_id=to, device_id_type=MESH)

        first = [remote(buf_a, pair_a, p1_send.at[0], p1_recv.at[0], sibling),
                 remote(buf_b, pair_b, p1_send.at[1], p1_recv.at[1], sibling)]
        for cp in first:
            cp.start()
        for cp in first:
            cp.wait()
        rows_a = pl.ds(pl.multiple_of(c * half_a, 8), half_a)
        rows_b = pl.ds(pl.multiple_of(c * half_b, 8), half_b)
        stage_a[me] = buf_a[rows_a, :] + pair_a[rows_a, :]
        stage_b[me] = buf_b[rows_b, :] + pair_b[rows_b, :]

        def chip_copies(d):
            to = (chips[d][0], chips[d][1], c)
            return [remote(stage_a.at[me], stage_a.at[me], x_send.at[2 * d], x_recv.at[2 * d], to),
                    remote(stage_b.at[me], stage_b.at[me], x_send.at[2 * d + 1], x_recv.at[2 * d + 1], to)]

        def chip_arrivals(d):
            slot = 2 * chips[d][0] + chips[d][1]
            to = (chips[d][0], chips[d][1], c)
            return [remote(stage_a.at[slot], stage_a.at[slot], x_send.at[2 * d], x_recv.at[2 * d], to),
                    remote(stage_b.at[slot], stage_b.at[slot], x_send.at[2 * d + 1], x_recv.at[2 * d + 1], to)]

        for d in range(3):
            for cp in chip_copies(d):
                cp.start()
        for d in range(3):
            for cp in chip_arrivals(d):
                cp.wait_recv()
        tot_a[rows_a, :] = ((stage_a[0] + stage_a[1]) + stage_a[2]) + stage_a[3]
        tot_b[rows_b, :] = ((stage_b[0] + stage_b[1]) + stage_b[2]) + stage_b[3]

        second = [remote(tot_a.at[rows_a, :], tot_a.at[rows_a, :], p2_send.at[0], p2_recv.at[0], sibling),
                  remote(tot_b.at[rows_b, :], tot_b.at[rows_b, :], p2_send.at[1], p2_recv.at[1], sibling)]
        for cp in second:
            cp.start()
        other_a = pl.ds(pl.multiple_of((1 - c) * half_a, 8), half_a)
        other_b = pl.ds(pl.multiple_of((1 - c) * half_b, 8), half_b)
        remote(tot_a.at[other_a, :], tot_a.at[other_a, :], p2_send.at[0], p2_recv.at[0], sibling).wait_recv()
        remote(tot_b.at[other_b, :], tot_b.at[other_b, :], p2_send.at[1], p2_recv.at[1], sibling).wait_recv()
        for cp in second:
            cp.wait_send()
        for d in range(3):
            for cp in chip_copies(d):
                cp.wait_send()
        out_a[...] = tot_a[...]
        out_b[...] = tot_b[...]

    ins = [local[k] for k in ("ln_in_g", "ln_in_b", "bq", "bkv", "bsuv", "sink", "sgu_ln_g", "sgu_ln_b", "sgu_w",
                              "sgu_bt", "b_out", "ln_mix_g", "ln_mix_b", "ln_ffn_g", "ln_ffn_b", "loss")]
    out_dims = [(_VEC_ROWS, D_MODEL), (_MAT_ROWS, 128)]
    vec = pltpu.VMEM((_VEC_ROWS, D_MODEL), F32)
    mat = pltpu.VMEM((_MAT_ROWS, 128), F32)
    return pl.pallas_call(
        body, name="small_allreduce", grid=(1,),
        in_specs=[_const2(a.shape) for a in ins], out_specs=[_const2(s) for s in out_dims],
        out_shape=[_hbm_shape(s, F32) for s in out_dims],
        scratch_shapes=[vec, mat, vec, mat, pltpu.VMEM((N_CHIP, _VEC_ROWS // 2, D_MODEL), F32),
                        pltpu.VMEM((N_CHIP, _MAT_ROWS // 2, 128), F32), vec, mat,
                        pltpu.SemaphoreType.DMA((2,)), pltpu.SemaphoreType.DMA((2,)), pltpu.SemaphoreType.DMA((6,)),
                        pltpu.SemaphoreType.DMA((6,)), pltpu.SemaphoreType.DMA((2,)), pltpu.SemaphoreType.DMA((2,))],
        compiler_params=pltpu.CompilerParams(vmem_limit_bytes=32 * MIB),
    )(*ins)


def _small_adamw(tot_a, tot_b, params):
    shapes = [params[nm][0].shape for nm in _SMALL]

    def body(*refs):
        ta, tb = refs[:2]
        prm = refs[2:2 + 3 * len(_SMALL)]
        outs = refs[2 + 3 * len(_SMALL):]

        def grad_of(k, name):
            if name == "sgu_w":
                return [tb[h * BLK:(h + 1) * BLK, :] for h in range(N_GRP)]
            if name == "sgu_b":
                return jnp.transpose(tb[N_GRP * BLK:_MAT_ROWS, :])[0:N_GRP, :]
            row = _VEC_ROW[name]
            if name == "b_in":
                return jnp.concatenate([ta[row:row + 1, :], ta[row + 1:row + 2, 0:IN_W - D_MODEL]], axis=1)
            return ta[row:row + 1, 0:shapes[k][-1]]

        for k, name in enumerate(_SMALL):
            w_ref, m_ref, v_ref = prm[3 * k:3 * k + 3]
            g_out, d_out, m_out, v_out = outs[4 * k:4 * k + 4]
            g = grad_of(k, name)
            if name == "sgu_w":
                for h in range(N_GRP):
                    d_, m_, v_ = _adamw_math(w_ref[h], g[h], m_ref[h], v_ref[h])
                    g_out[h], d_out[h], m_out[h], v_out[h] = g[h], d_, m_, v_
            else:
                d_, m_, v_ = _adamw_math(w_ref[...], g, m_ref[...], v_ref[...])
                g_out[...], d_out[...], m_out[...], v_out[...] = g, d_, m_, v_
        outs[-1][...] = ta[_LOSS_ROW:_LOSS_ROW + 1, :]

    ins = [tot_a, tot_b] + [_in_hbm(a) for nm in _SMALL for a in params[nm]]
    out_dims = [s for s in shapes for _ in range(4)] + [(1, D_MODEL)]
    res = pl.pallas_call(
        body, name="small_adamw", grid=(1,),
        in_specs=[_const2(a.shape) for a in ins], out_specs=[_const2(s) for s in out_dims],
        out_shape=[_hbm_shape(s, F32) for s in out_dims],
        compiler_params=_params(32),
    )(*ins)
    return {nm: tuple(res[4 * k:4 * k + 4]) for k, nm in enumerate(_SMALL)}, res[-1]


def _elementwise(name, fn, ins, out_dtypes, tile_rows=256):
    shape = ins[0].shape
    lead = shape[:-2]
    rows, cols = shape[-2:]
    tr = _tile(rows, tile_rows)
    n_lead = math.prod(lead)
    nr = rows // tr
    flat = [_in_hbm(a.reshape((n_lead, rows, cols))) for a in ins]

    def body(*refs):
        outs = fn(*[r[0] for r in refs[:len(ins)]])
        for o_ref, o in zip(refs[len(ins):], outs):
            o_ref[0] = o.astype(o_ref.dtype)

    spec = pl.BlockSpec((1, tr, cols), lambda i: (i // nr, i % nr, 0))
    res = pl.pallas_call(
        body, name=name, grid=(n_lead * nr,),
        in_specs=[spec] * len(ins), out_specs=[spec] * len(out_dtypes),
        out_shape=[_hbm_shape((n_lead, rows, cols), dt) for dt in out_dtypes],
        compiler_params=_params(48),
    )(*flat)
    return [r.reshape(shape) for r in res]


def _adamw_math(w, g, m, v):
    m = ADAM_B1 * m + (1.0 - ADAM_B1) * g
    v = ADAM_B2 * v + (1.0 - ADAM_B2) * (g * g)
    m_hat = m / (1.0 - ADAM_B1 ** ADAM_STEP)
    v_hat = v / (1.0 - ADAM_B2 ** ADAM_STEP)
    delta = -ADAM_LR * (m_hat / (jnp.sqrt(v_hat) + ADAM_EPS) + ADAM_WD * w)
    return delta, m, v


def _adamw(name, groups, tile_rows=256):
    k = len(groups)

    def fn(*blocks):
        outs = []
        for i in range(k):
            w_, g_, m_, v_ = blocks[4 * i:4 * i + 4]
            outs += [g_, *_adamw_math(w_, g_, m_, v_)]
        return outs

    res = _elementwise(name, fn, [a for grp in groups for a in grp], [F32] * (4 * k), tile_rows)
    return [res[4 * i:4 * i + 4] for i in range(k)]


def kernel(x, positions, ln_in_g, ln_in_b, w_in, b_in, attn_sinks, sgu_ln_g, sgu_ln_b, sgu_w, sgu_b, w_out, b_out, ln_mix_g, ln_mix_b, w_gate, w_up, w_down, ln_ffn_g, ln_ffn_b, loss_target, m_ln_in_g, m_ln_in_b, m_w_in, m_b_in, m_attn_sinks, m_sgu_ln_g, m_sgu_ln_b, m_sgu_w, m_sgu_b, m_w_out, m_b_out, m_ln_mix_g, m_ln_mix_b, m_w_gate, m_w_up, m_w_down, m_ln_ffn_g, m_ln_ffn_b, v_ln_in_g, v_ln_in_b, v_w_in, v_b_in, v_attn_sinks, v_sgu_ln_g, v_sgu_ln_b, v_sgu_w, v_sgu_b, v_w_out, v_b_out, v_ln_mix_g, v_ln_mix_b, v_w_gate, v_w_up, v_w_down, v_ln_ffn_g, v_ln_ffn_b):
    weights = dict(ln_in_g=ln_in_g, ln_in_b=ln_in_b, w_in=w_in, b_in=b_in, attn_sinks=attn_sinks, sgu_ln_g=sgu_ln_g,
                   sgu_ln_b=sgu_ln_b, sgu_w=sgu_w, sgu_b=sgu_b, w_out=w_out, b_out=b_out, ln_mix_g=ln_mix_g,
                   ln_mix_b=ln_mix_b, w_gate=w_gate, w_up=w_up, w_down=w_down, ln_ffn_g=ln_ffn_g, ln_ffn_b=ln_ffn_b)
    mom_m = dict(ln_in_g=m_ln_in_g, ln_in_b=m_ln_in_b, w_in=m_w_in, b_in=m_b_in, attn_sinks=m_attn_sinks,
                 sgu_ln_g=m_sgu_ln_g, sgu_ln_b=m_sgu_ln_b, sgu_w=m_sgu_w, sgu_b=m_sgu_b, w_out=m_w_out, b_out=m_b_out,
                 ln_mix_g=m_ln_mix_g, ln_mix_b=m_ln_mix_b, w_gate=m_w_gate, w_up=m_w_up, w_down=m_w_down,
                 ln_ffn_g=m_ln_ffn_g, ln_ffn_b=m_ln_ffn_b)
    mom_v = dict(ln_in_g=v_ln_in_g, ln_in_b=v_ln_in_b, w_in=v_w_in, b_in=v_b_in, attn_sinks=v_attn_sinks,
                 sgu_ln_g=v_sgu_ln_g, sgu_ln_b=v_sgu_ln_b, sgu_w=v_sgu_w, sgu_b=v_sgu_b, w_out=v_w_out, b_out=v_b_out,
                 ln_mix_g=v_ln_mix_g, ln_mix_b=v_ln_mix_b, w_gate=v_w_gate, w_up=v_w_up, w_down=v_w_down,
                 ln_ffn_g=v_ln_ffn_g, ln_ffn_b=v_ln_ffn_b)
    order = list(weights)
    big = ("w_in", "w_out", "w_gate", "w_up", "w_down")

    s_len = x.shape[1]
    xs = _in_hbm(x.reshape(s_len, D_MODEL))
    tgt = _in_hbm(loss_target.reshape(s_len, D_MODEL))
    pos_col = _in_hbm(positions.reshape(s_len, 1))
    g0, b0 = _in_hbm(ln_in_g.reshape(1, D_MODEL)), _in_hbm(ln_in_b.reshape(1, D_MODEL))
    sinks = attn_sinks.reshape(N_Q)
    sgu_w3 = _in_hbm(sgu_w.reshape(N_GRP, BLK, BLK))
    sgu_bt = _in_hbm(sgu_b.reshape(N_GRP, BLK).T)
    b_in, b_out, sgu_ln_g, sgu_ln_b, ln_mix_g, ln_mix_b, ln_ffn_g, ln_ffn_b = (
        _in_hbm(a) for a in (b_in, b_out, sgu_ln_g, sgu_ln_b, ln_mix_g, ln_mix_b, ln_ffn_g, ln_ffn_b))

    col_sharded = ("w_in", "w_gate", "w_up")

    def rowmajor(name, a):
        return jnp.swapaxes(a[0], 0, 1) if name in col_sharded else a[0]

    def as_given(name, a):
        return (jnp.swapaxes(a, 0, 1) if name in col_sharded else a)[None]

    shards = [rowmajor(n, weights[n]) for n in big]
    (gw_in,) = _gather_weights(shards[0:1])
    w_in_full = gw_in.reshape(IN_W, D_MODEL)

    sh_out, sh_gate, sh_up, sh_down = shards[1:]
    *acts, gw_out, gw_gate0 = _ln_inproj(xs, pos_col, g0, b0, w_in_full, b_in, _GatherPlan(
        [(sh_out, (0, OUT_SH), None), (sh_gate, (0, GATE_CUT), None)]))
    q, k, v, su, sv, tc, t1, t2 = (_in_hbm(a) for a in acts)
    mc, gw_gate, gw_up0 = _mixer_fwd(q, k, v, su, sv, sinks, sgu_ln_g, sgu_ln_b, sgu_w3, sgu_bt, _GatherPlan(
        [(sh_gate, (GATE_CUT, FF_SH), gw_gate0), (sh_up, (0, UP_CUT), None)]))
    mc = _in_hbm(mc)
    w_out_full = gw_out.reshape(D_MODEL, D_MODEL)
    r1, gw_up = _outproj(mc, w_out_full, b_out, xs, g0, b0, _GatherPlan([(sh_up, (UP_CUT, FF_SH), gw_up0)]))
    r1 = _in_hbm(r1)
    act, p_act, q_act, gw_down = _ffn_up(r1, ln_mix_g, ln_mix_b, gw_gate, gw_up,
                                         _GatherPlan([(sh_down, (0, FF_SH), None)]))
    act, p_act, q_act = _in_hbm(act), _in_hbm(p_act), _in_hbm(q_act)
    dr2, loss_cols, d_ln_ffn_g, d_ln_ffn_b = _ffn_down_loss(act, gw_down, r1, ln_mix_g, ln_mix_b, ln_ffn_g, ln_ffn_b, tgt)
    dr2 = _in_hbm(dr2)

    dg, du, wire_down, own_down = _ffn_bwd_a(dr2, act, p_act, q_act, gw_down)
    dh1a, wire_gate, own_gate, land_down = _ffn_bwd_g(dr2, _in_hbm(dg), r1, ln_mix_g, ln_mix_b, gw_gate, wire_down)
    dr1, wire_up, own_up, d_ln_mix_g, d_ln_mix_b, land_gate = _ffn_bwd_u(_in_hbm(dh1a), _in_hbm(du), r1, ln_mix_g,
                                                                         ln_mix_b, gw_up, wire_gate)
    dr1 = _in_hbm(dr1)
    dmc, wire_out, own_out, d_b_out = _outproj_bwd(dr1, mc, w_out_full)
    (dq, dkv, dsuv, dbq, dbkv, dbsuv, d_sink, d_sgu_ln_g, d_sgu_ln_b, d_sgu_w, d_sgu_bt, land_up, land_out) = _mixer_bwd(
        q, k, v, su, sv, _in_hbm(dmc), tc, t1, t2, sinks, sgu_ln_g, sgu_ln_b, sgu_w3, sgu_bt, [wire_up, wire_out])
    dkv = dkv[BLK:BLK + s_len]
    grad_x, acc_in, d_ln_in_g, d_ln_in_b = _inproj_bwd(_in_hbm(dq), _in_hbm(dkv), _in_hbm(dsuv), dr1, xs, g0, b0,
                                                       w_in_full)

    reduced = _grad_finish(acc_in, [land_out, land_gate, land_up, land_down], [own_out, own_gate, own_up, own_down])
    small_shape = dict(ln_in_g=(1, D_MODEL), ln_in_b=(1, D_MODEL), sgu_w=(N_GRP, BLK, BLK), sgu_b=(N_GRP, BLK))
    small_local = dict(
        ln_in_g=d_ln_in_g, ln_in_b=d_ln_in_b, bq=dbq, bkv=dbkv, bsuv=dbsuv, sink=d_sink, sgu_ln_g=d_sgu_ln_g,
        sgu_ln_b=d_sgu_ln_b, sgu_w=d_sgu_w, sgu_bt=d_sgu_bt, b_out=d_b_out, ln_mix_g=d_ln_mix_g, ln_mix_b=d_ln_mix_b,
        ln_ffn_g=d_ln_ffn_g, ln_ffn_b=d_ln_ffn_b, loss=loss_cols)
    small_params = {nm: tuple(src[nm].reshape(small_shape.get(nm, src[nm].shape)) for src in (weights, mom_m, mom_v))
                    for nm in _SMALL}
    tot_a, tot_b = _small_allreduce({nm: _in_hbm(a) for nm, a in small_local.items()})
    small_out, loss_sum = _small_adamw(_in_hbm(tot_a), _in_hbm(tot_b), small_params)
    loss = jnp.sum(loss_sum) * (0.5 / D_MODEL)
    grads, delta, new_m, new_v = {}, {}, {}, {}
    for nm in _SMALL:
        grads[nm], delta[nm], new_m[nm], new_v[nm] = (a.reshape(weights[nm].shape) for a in small_out[nm])

    def update(call_name, names):
        groups = [(shards[big.index(nm)], reduced[big.index(nm)], rowmajor(nm, mom_m[nm]), rowmajor(nm, mom_v[nm]))
                  for nm in names]
        for nm, res in zip(names, _adamw(call_name, groups)):
            grads[nm], delta[nm], new_m[nm], new_v[nm] = (as_given(nm, a) for a in res)

    update("adamw_w_in", ["w_in"])
    update("adamw_w_out", ["w_out"])
    update("adamw_ffn", ["w_gate", "w_up", "w_down"])

    return (loss, grad_x.reshape(x.shape), *[grads[n] for n in order], *[delta[n] for n in order],
            *[new_m[n] for n in order], *[new_v[n] for n in order])
```

```python
import functools
import math

import jax
import jax.numpy as jnp
from jax import lax
from jax.experimental import pallas as pl
from jax.experimental.pallas import tpu as pltpu

F32 = jnp.float32
_MXU = jnp.bfloat16
_WIRE = jnp.bfloat16
_ACT = jnp.bfloat16

D_MODEL = 1024
ATTN_W = 512
SGU_W = 512
HEAD_DIM = 64
N_Q = 8
N_KV = 2
Q_PER_KV = 4
KV_W = 128
BLK = 128
ROT_DIM = 16
ROPE_THETA = 500000.0
N_GRP = 4
GRP_DIM = 128
D_FF = 2816
IN_W = 1792
LN_EPS = 1e-5
ALPHA = 2.0 ** 0.25
N_CHIP = 4
FF_SH = D_FF // N_CHIP
IN_SH = IN_W // N_CHIP
OUT_SH = D_MODEL // N_CHIP
ROW_CHUNK = 32
GATE_CUT, UP_CUT = 352, 320

ADAM_LR = 0.001
ADAM_B1 = 0.9
ADAM_B2 = 0.999
ADAM_EPS = 1e-08
ADAM_WD = 0.01
ADAM_STEP = 10

SQRT_HALF = 0.7071067811865476
INV_SQRT_2PI = 0.3989422804014327
MESH_AXES = ("x", "y", "c")
MESH = pl.DeviceIdType.MESH
MIB = 2 ** 20


def _vmem():
    return pl.BlockSpec(memory_space=pltpu.VMEM)


def _smem():
    return pl.BlockSpec(memory_space=pltpu.SMEM)


def _hbm():
    return pl.BlockSpec(memory_space=pl.ANY)


def _hbm_shape(shape, dtype):
    return pltpu.HBM(shape, dtype)


def _in_hbm(a):
    return pltpu.with_memory_space_constraint(a, pltpu.HBM)


def _params(vmem_mib=48):
    return pltpu.CompilerParams(dimension_semantics=("arbitrary",), vmem_limit_bytes=vmem_mib * MIB)


def _tile(n, cap):
    if n <= cap:
        return n
    for t in range(cap - cap % 16, 0, -16):
        if n % t == 0:
            return t
    raise ValueError((n, cap))


def _rows(tm, width):
    return pl.BlockSpec((tm, width), lambda i: (i, 0))


def _const2(shape):
    return pl.BlockSpec(shape, lambda i: (0,) * len(shape))


def _ln(x, g, b):
    mu = jnp.mean(x, axis=-1, keepdims=True)
    xc = x - mu
    var = jnp.mean(xc * xc, axis=-1, keepdims=True)
    rstd = lax.rsqrt(var + LN_EPS)
    xhat = xc * rstd
    return xhat * g + b, xhat, rstd


def _ln_bwd(dy, xhat, rstd, g):
    gdy = dy * g
    m1 = jnp.mean(gdy, axis=-1, keepdims=True)
    m2 = jnp.mean(gdy * xhat, axis=-1, keepdims=True)
    return rstd * (gdy - m1 - xhat * m2)


def _colsum(a):
    return jnp.sum(a, axis=0, keepdims=True)


def _gelu_and_grad(x):
    cdf = 0.5 * (1.0 + lax.erf(x * SQRT_HALF))
    return x * cdf, cdf + x * jnp.exp(-0.5 * x * x) * INV_SQRT_2PI


def _dot(a, b):
    return jnp.dot(a, b, preferred_element_type=F32)


def _dot_nt(a, b):
    return lax.dot_general(a, b, (((1,), (1,)), ((), ())), preferred_element_type=F32)


def _dot_tn(a, b):
    return lax.dot_general(a, b, (((0,), (0,)), ((), ())), preferred_element_type=F32)


def _rope(t, tc, t1, t2):
    n = t.shape[1]
    rep = n // 128
    if rep > 1:
        tc, t1, t2 = (jnp.tile(a, (1, rep)) for a in (tc, t1, t2))
    return t * tc + pltpu.roll(t, n - 8, 1) * t1 + pltpu.roll(t, 8, 1) * t2


def _rope_bwd(d, tc, t1, t2):
    n = d.shape[1]
    rep = n // 128
    if rep > 1:
        tc, t1, t2 = (jnp.tile(a, (1, rep)) for a in (tc, t1, t2))
    return d * tc + pltpu.roll(d * t1, 8, 1) + pltpu.roll(d * t2, n - 8, 1)


def _causal_w(w_ref, h):
    t = lax.broadcasted_iota(jnp.int32, (BLK, BLK), 0)
    s = lax.broadcasted_iota(jnp.int32, (BLK, BLK), 1)
    return jnp.where(s <= t, w_ref[h], 0.0)


def _lane_put(vals, width):
    rows = vals[0].shape[0]
    lane = lax.broadcasted_iota(jnp.int32, (rows, width), 1)
    out = jnp.zeros((rows, width), F32)
    for k, v in enumerate(vals):
        out = out + jnp.where(lane == k, v, 0.0)
    return out


def _rope_consts():
    lane = jnp.arange(128) % HEAD_DIM
    rot = lane < ROT_DIM
    pair = (2 * (lane % (ROT_DIM // 2))).astype(F32)
    freq = jnp.where(rot, ROPE_THETA ** (-pair / ROT_DIM), 0.0)
    rows = [freq, rot.astype(F32), 1.0 - rot.astype(F32), (lane < ROT_DIM // 2).astype(F32),
            jnp.logical_and(lane >= ROT_DIM // 2, rot).astype(F32)]
    rows += [jnp.zeros((128,), F32)] * 3
    return jnp.stack(rows).astype(F32)


def _ln_inproj(x, pos_col, g0, b0, w_in, b_in, plan):
    s_len = x.shape[0]
    tm = _tile(s_len, 512)
    m, n = len(plan.operands()), plan.n

    def body(x_ref, pos_ref, g_ref, b_ref, w_ref, bi_ref, rc_ref, *rest):
        q_ref, k_ref, v_ref, su_ref, sv_ref, tc_ref, t1_ref, t2_ref = rest[m:m + 8]
        gather = plan.bind(rest[:m], rest[m + 8:m + 8 + n], rest[m + 8 + n:])
        i = pl.program_id(0)

        @pl.when(i == 0)
        def _():
            gather.start()

        h0, _, _ = _ln(x_ref[...], g_ref[...], b_ref[...])
        proj = _dot_nt(h0.astype(_MXU), w_ref[...]) + bi_ref[...]
        ang = pos_ref[...].astype(F32) * rc_ref[0:1, :]
        cs = jnp.cos(ang)
        sn = jnp.sin(ang)
        tc = cs * rc_ref[1:2, :] + rc_ref[2:3, :]
        t1 = -sn * rc_ref[3:4, :]
        t2 = sn * rc_ref[4:5, :]
        tc_ref[...] = tc
        t1_ref[...] = t1
        t2_ref[...] = t2
        q = _rope(proj[:, 0:ATTN_W], tc, t1, t2) * (HEAD_DIM ** -0.5)
        q_ref[...] = q.astype(_MXU)
        k_ref[...] = _rope(proj[:, ATTN_W:ATTN_W + KV_W], tc, t1, t2).astype(_MXU)
        v_ref[...] = proj[:, ATTN_W + KV_W:ATTN_W + 2 * KV_W].astype(_MXU)
        su_ref[...] = proj[:, ATTN_W + 2 * KV_W:ATTN_W + 2 * KV_W + SGU_W]
        sv_ref[...] = proj[:, ATTN_W + 2 * KV_W + SGU_W:IN_W]

        last = pl.num_programs(0) - 1

        @pl.when(i == jnp.maximum(last - 1, 0))
        def _():
            gather.pass_on()

        @pl.when(i == last)
        def _():
            gather.finish()

    sd = _hbm_shape
    return pl.pallas_call(
        body, name="ln_inproj", grid=(s_len // tm,),
        in_specs=[_rows(tm, D_MODEL), _rows(tm, 1), _const2((1, D_MODEL)), _const2((1, D_MODEL)), _vmem(),
                  _const2((1, IN_W)), _const2((8, 128))] + plan.in_specs(),
        out_specs=[_rows(tm, ATTN_W), _rows(tm, KV_W), _rows(tm, KV_W), _rows(tm, SGU_W), _rows(tm, SGU_W),
                   _rows(tm, 128), _rows(tm, 128), _rows(tm, 128)] + plan.out_specs(),
        out_shape=[sd((s_len, ATTN_W), _MXU), sd((s_len, KV_W), _MXU), sd((s_len, KV_W), _MXU),
                   sd((s_len, SGU_W), F32), sd((s_len, SGU_W), F32),
                   sd((s_len, 128), F32), sd((s_len, 128), F32), sd((s_len, 128), F32)] + plan.out_shapes(),
        scratch_shapes=plan.scratch(),
        compiler_params=_params(56),
    )(x, pos_col, g0, b0, w_in, b_in, _rope_consts(), *plan.operands())


def _band_mask_t(first_block):
    kj = lax.broadcasted_iota(jnp.int32, (2 * BLK, BLK), 0)
    qi = lax.broadcasted_iota(jnp.int32, (2 * BLK, BLK), 1)
    shut = jnp.where(first_block, 2 * BLK, 0)
    prev_ok = jnp.logical_and(kj < BLK, kj > qi + shut)
    cur_ok = jnp.logical_and(kj >= BLK, (kj - BLK) <= qi)
    return jnp.logical_or(prev_ok, cur_ok)


def _attn_probs_t(kh, qh, sink, allowed_t):
    s = jnp.where(allowed_t, _dot_nt(kh, qh), -1e30)
    m = jnp.maximum(jnp.max(s, axis=0, keepdims=True), sink)
    p = jnp.exp(s - m)
    ps = jnp.exp(sink - m)
    inv = 1.0 / (jnp.sum(p, axis=0, keepdims=True) + ps)
    return p * inv, ps * inv


def _sgu_mix(gv, lg, lb, w_ref, bt_ref):
    vv, vhat, rstd = _ln(gv, lg, lb)
    vvb = vv.astype(_MXU)
    wcs, mixed = [], []
    for h in range(N_GRP):
        wc = _causal_w(w_ref, h).astype(_MXU)
        wcs.append(wc)
        mixed.append(_dot(wc, vvb[:, h * GRP_DIM:(h + 1) * GRP_DIM]) + bt_ref[:, h:h + 1])
    return jnp.concatenate(mixed, axis=1), vhat, rstd, vvb, wcs


def _mixer_fwd(q, k, v, su, sv, sinks, sg, sb, sgu_w, sgu_bt, plan):
    s_len = q.shape[0]
    nb = s_len // BLK
    per = 2 if nb % 2 == 0 else 1
    steps = nb // per
    m, n = len(plan.operands()), plan.n

    def body(q_ref, kc_ref, kp_ref, vc_ref, vp_ref, su_ref, sv_ref, sink_ref, lg_ref, lb_ref, w_ref, bt_ref, *rest):
        mc_ref = rest[m]
        gather = plan.bind(rest[:m], rest[m + 1:m + 1 + n], rest[m + 1 + n:])
        i = pl.program_id(0)

        @pl.when(i == 0)
        def _():
            gather.start()

        @pl.when(i == max(steps - 2, 0))
        def _():
            gather.pass_on()

        @pl.when(i == steps - 1)
        def _():
            gather.finish()

        for s in range(per):
            rows = slice(s * BLK, (s + 1) * BLK)
            before = slice((s - 1) * BLK, s * BLK)
            k_prev = kp_ref[...] if s == 0 else kc_ref[before, :]
            v_prev = vp_ref[...] if s == 0 else vc_ref[before, :]
            allowed_t = _band_mask_t(i == 0 if s == 0 else False)
            kb = jnp.concatenate([k_prev, kc_ref[rows, :]], axis=0)
            vb = jnp.concatenate([v_prev, vc_ref[rows, :]], axis=0)
            qv = q_ref[rows, :]
            outs = []
            allowed_g = jnp.tile(allowed_t, (1, Q_PER_KV))
            for g in range(N_KV):
                heads = range(g * Q_PER_KV, (g + 1) * Q_PER_KV)
                kh = kb[:, g * HEAD_DIM:(g + 1) * HEAD_DIM]
                vh = vb[:, g * HEAD_DIM:(g + 1) * HEAD_DIM]
                q_g = jnp.concatenate([qv[:, h * HEAD_DIM:(h + 1) * HEAD_DIM] for h in heads], axis=0)
                sink_g = jnp.concatenate([jnp.full((1, BLK), sink_ref[h], F32) for h in heads], axis=1)
                probs_t, _ = _attn_probs_t(kh, q_g, sink_g, allowed_g)
                o_g = _dot_tn(probs_t.astype(_MXU), vh)
                outs += [o_g[hh * BLK:(hh + 1) * BLK, :] for hh in range(Q_PER_KV)]
            u = _gelu_and_grad(su_ref[rows, :])[0]
            gv = _gelu_and_grad(sv_ref[rows, :])[0]
            mixed = _sgu_mix(gv, lg_ref[...], lb_ref[...], w_ref, bt_ref)[0]
            mc_ref[rows, :] = jnp.concatenate(outs + [u * mixed], axis=1).astype(_MXU)

    cur = lambda w: pl.BlockSpec((per * BLK, w), lambda i: (i, 0))
    prev = lambda w: pl.BlockSpec((BLK, w), lambda i: (jnp.maximum(per * i - 1, 0), 0))
    return pl.pallas_call(
        body, name="mixer_fwd", grid=(steps,),
        in_specs=[cur(ATTN_W), cur(KV_W), prev(KV_W), cur(KV_W), prev(KV_W), cur(SGU_W), cur(SGU_W), _smem(),
                  _const2((1, SGU_W)), _const2((1, SGU_W)), _const2((N_GRP, BLK, BLK)), _const2((BLK, N_GRP))]
        + plan.in_specs(),
        out_specs=[cur(D_MODEL)] + plan.out_specs(),
        out_shape=[_hbm_shape((s_len, D_MODEL), _MXU)] + plan.out_shapes(),
        scratch_shapes=plan.scratch(),
        compiler_params=_params(56),
    )(q, k, k, v, v, su, sv, sinks, sg, sb, sgu_w, sgu_bt, *plan.operands())


def _outproj(mc, w_out, b_out, x, g0, b0, plan):
    s_len = x.shape[0]
    tm = _tile(s_len, 512)
    m, n = len(plan.operands()), plan.n

    def body(mc_ref, w_ref, bo_ref, x_ref, g_ref, b_ref, *rest):
        r1_ref = rest[m]
        gather = plan.bind(rest[:m], rest[m + 1:m + 1 + n], rest[m + 1 + n:])
        i = pl.program_id(0)

        @pl.when(i == 0)
        def _():
            gather.start()

        h0, _, _ = _ln(x_ref[...], g_ref[...], b_ref[...])
        r1_ref[...] = ALPHA * h0 + (_dot(mc_ref[...], w_ref[...]) + bo_ref[...])

        last = pl.num_programs(0) - 1

        @pl.when(i == jnp.maximum(last - 1, 0))
        def _():
            gather.pass_on()

        @pl.when(i == last)
        def _():
            gather.finish()

    return pl.pallas_call(
        body, name="outproj", grid=(s_len // tm,),
        in_specs=[_rows(tm, D_MODEL), _vmem(), _const2((1, D_MODEL)), _rows(tm, D_MODEL),
                  _const2((1, D_MODEL)), _const2((1, D_MODEL))] + plan.in_specs(),
        out_specs=[_rows(tm, D_MODEL)] + plan.out_specs(),
        out_shape=[_hbm_shape((s_len, D_MODEL), F32)] + plan.out_shapes(),
        scratch_shapes=plan.scratch(),
        compiler_params=_params(40),
    )(mc, w_out, b_out, x, g0, b0, *plan.operands())


def _ffn_spec(tm):
    return pl.BlockSpec((N_CHIP, tm, FF_SH), lambda i: (0, i, 0))


def _ffn_up(r1, g1, b1, wg, wu, plan):
    s_len = r1.shape[0]
    tm = _tile(s_len, 512)
    m, n = len(plan.operands()), plan.n

    def body(r1_ref, g_ref, b_ref, wg_ref, wu_ref, *rest):
        a_ref, p_ref, q_ref = rest[m:m + 3]
        gather = plan.bind(rest[:m], rest[m + 3:m + 3 + n], rest[m + 3 + n:])
        i = pl.program_id(0)

        @pl.when(i == 0)
        def _():
            gather.start()

        h1, _, _ = _ln(r1_ref[...], g_ref[...], b_ref[...])
        h1b = h1.astype(_MXU)
        for j in range(N_CHIP):
            g = _dot_nt(h1b, wg_ref[j])
            u = _dot_nt(h1b, wu_ref[j])
            silu, sg = _silu_parts(g)
            a_ref[j] = (silu * u).astype(_MXU)
            p_ref[j] = silu.astype(_ACT)
            q_ref[j] = (u * (sg * (1.0 + g * (1.0 - sg)))).astype(_ACT)

        last = pl.num_programs(0) - 1

        @pl.when(i == jnp.maximum(last - 1, 0))
        def _():
            gather.pass_on()

        @pl.when(i == last)
        def _():
            gather.finish()

    sd = _hbm_shape((N_CHIP, s_len, FF_SH), _ACT)
    return pl.pallas_call(
        body, name="ffn_up", grid=(s_len // tm,),
        in_specs=[_rows(tm, D_MODEL), _const2((1, D_MODEL)), _const2((1, D_MODEL)), _vmem(), _vmem()] + plan.in_specs(),
        out_specs=[_ffn_spec(tm)] * 3 + plan.out_specs(),
        out_shape=[_hbm_shape((N_CHIP, s_len, FF_SH), _MXU), sd, sd] + plan.out_shapes(),
        scratch_shapes=plan.scratch(),
        compiler_params=_params(56),
    )(r1, g1, b1, wg, wu, *plan.operands())


def _silu_parts(g):
    sg = 1.0 / (1.0 + jnp.exp(-g))
    return g * sg, sg


def _ffn_down_loss(act, wd, r1, g1, b1, g2, b2, target):
    s_len = r1.shape[0]
    tm = _tile(s_len, 512)

    parts = 2 if tm % 32 == 0 else 1
    sub = tm // parts

    def body(a_ref, wd_ref, r1_ref, g1_ref, b1_ref, g2_ref, b2_ref, t_ref, dr2_ref, loss_ref, dg2_ref, db2_ref):
        i = pl.program_id(0)

        @pl.when(i == 0)
        def _():
            loss_ref[...] = jnp.zeros_like(loss_ref)
            dg2_ref[...] = jnp.zeros_like(dg2_ref)
            db2_ref[...] = jnp.zeros_like(db2_ref)

        for part in range(parts):
            rows = slice(part * sub, (part + 1) * sub)
            f = jnp.zeros((sub, D_MODEL), F32)
            for j in range(N_CHIP):
                f = f + _dot(a_ref[j, rows, :], wd_ref[j])
            h1, _, _ = _ln(r1_ref[rows, :], g1_ref[...], b1_ref[...])
            h2, r2hat, rstd2 = _ln(ALPHA * h1 + f, g2_ref[...], b2_ref[...])
            diff = h2 - t_ref[rows, :]
            dh2 = diff * (1.0 / D_MODEL)
            loss_ref[...] += _colsum(diff * diff)
            dg2_ref[...] += _colsum(dh2 * r2hat)
            db2_ref[...] += _colsum(dh2)
            dr2_ref[rows, :] = _ln_bwd(dh2, r2hat, rstd2, g2_ref[...])

    vec = _hbm_shape((1, D_MODEL), F32)
    c = _const2((1, D_MODEL))
    return pl.pallas_call(
        body, name="ffn_down_loss", grid=(s_len // tm,),
        in_specs=[_ffn_spec(tm), _vmem(), _rows(tm, D_MODEL), c, c, c, c, _rows(tm, D_MODEL)],
        out_specs=[_rows(tm, D_MODEL), c, c, c],
        out_shape=[_hbm_shape((s_len, D_MODEL), F32), vec, vec, vec],
        compiler_params=_params(48),
    )(act, wd, r1, g1, b1, g2, b2, target)


def _ffn_bwd_a(dr2, act, p_act, q_act, wd):
    s_len = dr2.shape[0]
    tm = _tile(s_len, 512)

    def body(dr2_ref, a_ref, p_ref, q_ref, wd_ref, dg_ref, du_ref, wire_ref, own_ref,
             dwd_ref, land_ref, send_sem, recv_sem):
        i = pl.program_id(0)

        @pl.when(i == 0)
        def _():
            dwd_ref[...] = jnp.zeros_like(dwd_ref)

        dfb = dr2_ref[...].astype(_MXU)
        for j in range(N_CHIP):
            da = _dot_nt(dfb, wd_ref[j])
            dg_ref[j] = (da * q_ref[j].astype(F32)).astype(_MXU)
            du_ref[j] = (da * p_ref[j].astype(F32)).astype(_MXU)
            dwd_ref[j * FF_SH:(j + 1) * FF_SH, :] += _dot_tn(a_ref[j], dfb)

        @pl.when(i == pl.num_programs(0) - 1)
        def _():
            _pair_reduce(dwd_ref, wire_ref, own_ref, land_ref, send_sem, recv_sem)

    sd = _hbm_shape((N_CHIP, s_len, FF_SH), _MXU)
    half = (N_CHIP, FF_SH // 2, D_MODEL)
    return pl.pallas_call(
        body, name="ffn_bwd_a", grid=(s_len // tm,),
        in_specs=[_rows(tm, D_MODEL), _ffn_spec(tm), _ffn_spec(tm), _ffn_spec(tm), _vmem()],
        out_specs=[_ffn_spec(tm), _ffn_spec(tm), _vmem(), _vmem()],
        out_shape=[sd, sd] + _pair_out_shapes(half),
        scratch_shapes=_pair_scratch((D_FF, D_MODEL), half),
        compiler_params=_params(61),
    )(dr2, act, p_act, q_act, wd)


def _ffn_bwd_g(dr2, dg, r1, g1, b1, wg, prev_wire):
    s_len = dr2.shape[0]
    tm = _tile(s_len, 512)

    def body(dr2_ref, dg_ref, r1_ref, g1_ref, b1_ref, wg_ref, pw_ref, dh1_ref, wire_ref, own_ref, pl_ref,
             dwg_ref, land_ref, send_sem, recv_sem, xl_ref, x_send, x_recv, x_flush):
        i = pl.program_id(0)
        exchange = _ChipExchange(pw_ref, xl_ref, x_send, x_recv)

        @pl.when(i == 0)
        def _():
            exchange.start()
            dwg_ref[...] = jnp.zeros_like(dwg_ref)

        h1, _, _ = _ln(r1_ref[...], g1_ref[...], b1_ref[...])
        h1b = h1.astype(_MXU)
        dh1 = ALPHA * dr2_ref[...]
        for j in range(N_CHIP):
            dgj = dg_ref[j]
            dh1 = dh1 + _dot(dgj, wg_ref[j])
            dwg_ref[j * FF_SH:(j + 1) * FF_SH, :] += _dot_tn(dgj, h1b)
        dh1_ref[...] = dh1

        @pl.when(i == pl.num_programs(0) - 1)
        def _():
            _pair_reduce(dwg_ref, wire_ref, own_ref, land_ref, send_sem, recv_sem)
            exchange.finish_to(pl_ref, x_flush)

    c = _const2((1, D_MODEL))
    half = (N_CHIP, FF_SH // 2, D_MODEL)
    return pl.pallas_call(
        body, name="ffn_bwd_g", grid=(s_len // tm,),
        in_specs=[_rows(tm, D_MODEL), _ffn_spec(tm), _rows(tm, D_MODEL), c, c, _vmem(), _vmem()],
        out_specs=[_rows(tm, D_MODEL), _vmem(), _vmem(), _hbm()],
        out_shape=[_hbm_shape((s_len, D_MODEL), F32)] + _pair_out_shapes(half) + [_ChipExchange.land_shape(prev_wire)],
        scratch_shapes=_pair_scratch((D_FF, D_MODEL), half) + _ChipExchange.scratch(prev_wire),
        compiler_params=_params(58),
    )(dr2, dg, r1, g1, b1, wg, prev_wire)


def _ffn_bwd_u(dh1a, du, r1, g1, b1, wu, prev_wire):
    s_len = dh1a.shape[0]
    tm = _tile(s_len, 512)

    def body(dh1_ref, du_ref, r1_ref, g1_ref, b1_ref, wu_ref, pw_ref,
             dr1_ref, wire_ref, own_ref, dg1_ref, db1_ref, pl_ref,
             dwu_ref, land_ref, send_sem, recv_sem, xl_ref, x_send, x_recv, x_flush):
        i = pl.program_id(0)
        exchange = _ChipExchange(pw_ref, xl_ref, x_send, x_recv)

        @pl.when(i == 0)
        def _():
            exchange.start()
            dwu_ref[...] = jnp.zeros_like(dwu_ref)
            dg1_ref[...] = jnp.zeros_like(dg1_ref)
            db1_ref[...] = jnp.zeros_like(db1_ref)

        h1, r1hat, rstd1 = _ln(r1_ref[...], g1_ref[...], b1_ref[...])
        h1b = h1.astype(_MXU)
        dh1 = dh1_ref[...]
        for j in range(N_CHIP):
            duj = du_ref[j]
            dh1 = dh1 + _dot(duj, wu_ref[j])
            dwu_ref[j * FF_SH:(j + 1) * FF_SH, :] += _dot_tn(duj, h1b)
        dg1_ref[...] += _colsum(dh1 * r1hat)
        db1_ref[...] += _colsum(dh1)
        dr1_ref[...] = _ln_bwd(dh1, r1hat, rstd1, g1_ref[...])

        @pl.when(i == pl.num_programs(0) - 1)
        def _():
            _pair_reduce(dwu_ref, wire_ref, own_ref, land_ref, send_sem, recv_sem)
            exchange.finish_to(pl_ref, x_flush)

    vec = _hbm_shape((1, D_MODEL), F32)
    c = _const2((1, D_MODEL))
    half = (N_CHIP, FF_SH // 2, D_MODEL)
    return pl.pallas_call(
        body, name="ffn_bwd_u", grid=(s_len // tm,),
        in_specs=[_rows(tm, D_MODEL), _ffn_spec(tm), _rows(tm, D_MODEL), c, c, _vmem(), _vmem()],
        out_specs=[_rows(tm, D_MODEL), _vmem(), _vmem(), c, c, _hbm()],
        out_shape=[_hbm_shape((s_len, D_MODEL), F32)] + _pair_out_shapes(half)
        + [vec, vec, _ChipExchange.land_shape(prev_wire)],
        scratch_shapes=_pair_scratch((D_FF, D_MODEL), half) + _ChipExchange.scratch(prev_wire),
        compiler_params=_params(58),
    )(dh1a, du, r1, g1, b1, wu, prev_wire)


def _outproj_bwd(dr1, mc, w_out):
    s_len = dr1.shape[0]
    tm = _tile(s_len, 512)

    def body(dr1_ref, mc_ref, w_ref, dmc_ref, wire_ref, own_ref, db_ref, dw_ref, land_ref, send_sem, recv_sem):
        i = pl.program_id(0)

        @pl.when(i == 0)
        def _():
            dw_ref[...] = jnp.zeros_like(dw_ref)
            db_ref[...] = jnp.zeros_like(db_ref)

        d = dr1_ref[...]
        db_ref[...] += _colsum(d)
        db16 = d.astype(_MXU)
        dmc_ref[...] = _dot_nt(db16, w_ref[...])
        dw_ref[...] += _dot_tn(mc_ref[...], db16)

        @pl.when(i == pl.num_programs(0) - 1)
        def _():
            _pair_reduce(dw_ref, wire_ref, own_ref, land_ref, send_sem, recv_sem)

    half = (N_CHIP, OUT_SH // 2, D_MODEL)
    return pl.pallas_call(
        body, name="outproj_bwd", grid=(s_len // tm,),
        in_specs=[_rows(tm, D_MODEL), _rows(tm, D_MODEL), _vmem()],
        out_specs=[_rows(tm, D_MODEL), _vmem(), _vmem(), _const2((1, D_MODEL))],
        out_shape=[_hbm_shape((s_len, D_MODEL), F32)] + _pair_out_shapes(half) + [_hbm_shape((1, D_MODEL), F32)],
        scratch_shapes=_pair_scratch((D_MODEL, D_MODEL), half),
        compiler_params=_params(48),
    )(dr1, mc, w_out)


def _mixer_bwd(q, k, v, su, sv, dmc, tc, t1, t2, sinks, sg, sb, sgu_w, sgu_bt, prev_wires):
    s_len = q.shape[0]
    nb = s_len // BLK
    per = next(p for p in (4, 2, 1) if nb % p == 0)
    steps = nb // per

    def body(q_ref, kc_ref, kp_ref, vc_ref, vp_ref, su_ref, sv_ref, dmc_ref,
             tc_ref, t1_ref, t2_ref, tcp_ref, t1p_ref, t2p_ref,
             sink_ref, lg_ref, lb_ref, w_ref, bt_ref, pw0_ref, pw1_ref,
             dq_ref, dkv_ref, dsuv_ref, dbq_ref, dbkv_ref, dbsuv_ref,
             dsink_ref, dlg_ref, dlb_ref, dw_ref, dbt_ref, pl0_ref, pl1_ref, carry_ref,
             xl0_ref, x0_send, x0_recv, x0_flush, xl1_ref, x1_send, x1_recv, x1_flush):
        i = pl.program_id(0)
        exchanges = [(_ChipExchange(pw0_ref, xl0_ref, x0_send, x0_recv), pl0_ref, x0_flush),
                     (_ChipExchange(pw1_ref, xl1_ref, x1_send, x1_recv), pl1_ref, x1_flush)]

        @pl.when(i == 0)
        def _():
            for exchange, _, _ in exchanges:
                exchange.start()

        @pl.when(i == 0)
        def _():
            for r in (dbq_ref, dbkv_ref, dbsuv_ref, dsink_ref, dlg_ref, dlb_ref, dw_ref, dbt_ref, carry_ref):
                r[...] = jnp.zeros_like(r)

        def emit_kv(fin, t):
            if t == 0:
                tables = (tcp_ref[...], t1p_ref[...], t2p_ref[...])
            else:
                before = slice((t - 1) * BLK, t * BLK)
                tables = (tc_ref[before, :], t1_ref[before, :], t2_ref[before, :])
            dk = _rope_bwd(fin[:, 0:KV_W], *tables)
            out = jnp.concatenate([dk, fin[:, KV_W:2 * KV_W]], axis=1)
            dkv_ref[t * BLK:(t + 1) * BLK, :] = out.astype(_MXU)
            dbkv_ref[...] += _colsum(out)

        def one_block(s):
            rows = slice(s * BLK, (s + 1) * BLK)
            before = slice((s - 1) * BLK, s * BLK)
            k_prev = kp_ref[...] if s == 0 else kc_ref[before, :]
            v_prev = vp_ref[...] if s == 0 else vc_ref[before, :]
            allowed_t = _band_mask_t(i == 0 if s == 0 else False)
            kb = jnp.concatenate([k_prev, kc_ref[rows, :]], axis=0)
            vb = jnp.concatenate([v_prev, vc_ref[rows, :]], axis=0)
            qv = q_ref[rows, :]
            dmc = dmc_ref[rows, :]
            dqs, dks, dvs, dsinks = [], [], [], []
            allowed_g = jnp.tile(allowed_t, (1, Q_PER_KV))
            for g in range(N_KV):
                heads = range(g * Q_PER_KV, (g + 1) * Q_PER_KV)
                kh = kb[:, g * HEAD_DIM:(g + 1) * HEAD_DIM]
                vh = vb[:, g * HEAD_DIM:(g + 1) * HEAD_DIM]
                q_g = jnp.concatenate([qv[:, h * HEAD_DIM:(h + 1) * HEAD_DIM] for h in heads], axis=0)
                do_g = jnp.concatenate([dmc[:, h * HEAD_DIM:(h + 1) * HEAD_DIM] for h in heads], axis=0).astype(_MXU)
                sink_g = jnp.concatenate([jnp.full((1, BLK), sink_ref[h], F32) for h in heads], axis=1)
                probs_t, psink = _attn_probs_t(kh, q_g, sink_g, allowed_g)
                dvs.append(_dot(probs_t.astype(_MXU), do_g))
                dp_t = _dot_nt(vh, do_g)
                rd = jnp.sum(probs_t * dp_t, axis=0, keepdims=True)
                ds_t = (probs_t * (dp_t - rd)).astype(_MXU)
                ps_rd = psink * rd
                for hh in range(Q_PER_KV):
                    dsinks.append(-jnp.sum(ps_rd[:, hh * BLK:(hh + 1) * BLK], axis=1, keepdims=True))
                dq_g = _dot_tn(ds_t, kh)
                dqs += [dq_g[hh * BLK:(hh + 1) * BLK, :] for hh in range(Q_PER_KV)]
                dks.append(_dot(ds_t, q_g))
            dq = _rope_bwd(jnp.concatenate(dqs, axis=1) * (HEAD_DIM ** -0.5),
                           tc_ref[rows, :], t1_ref[rows, :], t2_ref[rows, :])
            dq_ref[rows, :] = dq.astype(_MXU)
            dbq_ref[...] += _colsum(dq)
            dsink_ref[...] += _lane_put(dsinks, 128)
            contrib = jnp.concatenate(dks + dvs, axis=1)

            lg = lg_ref[...]
            u, du_dsu = _gelu_and_grad(su_ref[rows, :])
            gv, dgv_dsv = _gelu_and_grad(sv_ref[rows, :])
            mixed, vhat, rstd, vvb, wcs = _sgu_mix(gv, lg, lb_ref[...], w_ref, bt_ref)
            dsgu = dmc[:, ATTN_W:D_MODEL]
            dsu = dsgu * mixed * du_dsu
            dmixed = dsgu * u
            tri_t = lax.broadcasted_iota(jnp.int32, (BLK, BLK), 0)
            tri_s = lax.broadcasted_iota(jnp.int32, (BLK, BLK), 1)
            dvv, dbs = [], []
            for h in range(N_GRP):
                dm = dmixed[:, h * GRP_DIM:(h + 1) * GRP_DIM]
                dmb = dm.astype(_MXU)
                dbs.append(jnp.sum(dm, axis=1, keepdims=True))
                dw_ref[h] += jnp.where(tri_s <= tri_t, _dot_nt(dmb, vvb[:, h * GRP_DIM:(h + 1) * GRP_DIM]), 0.0)
                dvv.append(_dot_tn(wcs[h], dmb))
            dvv = jnp.concatenate(dvv, axis=1)
            dbt_ref[...] += _lane_put(dbs, 128)
            dlg_ref[...] += _colsum(dvv * vhat)
            dlb_ref[...] += _colsum(dvv)
            dsv = _ln_bwd(dvv, vhat, rstd, lg) * dgv_dsv
            dsuv = jnp.concatenate([dsu, dsv], axis=1)
            dsuv_ref[rows, :] = dsuv.astype(_MXU)
            dbsuv_ref[...] += _colsum(dsuv)
            return contrib

        @pl.when(i < steps)
        def _():
            contribs = [one_block(s) for s in range(per)]
            for t in range(per):
                top = carry_ref[...] if t == 0 else contribs[t - 1][BLK:2 * BLK, :]
                emit_kv(top + contribs[t][0:BLK, :], t)
            carry_ref[...] = contribs[per - 1][BLK:2 * BLK, :]

        @pl.when(i == steps)
        def _():
            emit_kv(carry_ref[...], 0)
            if per > 1:
                dkv_ref[BLK:per * BLK, :] = jnp.zeros(((per - 1) * BLK, 2 * KV_W), _MXU)
            for exchange, landed, flush_sem in exchanges:
                exchange.finish_to(landed, flush_sem)

    last = steps - 1
    cur = lambda w: pl.BlockSpec((per * BLK, w), lambda i: (jnp.minimum(i, last), 0))
    prev = lambda w: pl.BlockSpec((BLK, w), lambda i: (jnp.clip(per * i - 1, 0, nb - 1), 0))
    shifted = pl.BlockSpec((per * BLK, 2 * KV_W), lambda i: (i, 0))
    sd = _hbm_shape
    return pl.pallas_call(
        body, name="mixer_bwd", grid=(steps + 1,),
        in_specs=[cur(ATTN_W), cur(KV_W), prev(KV_W), cur(KV_W), prev(KV_W), cur(SGU_W), cur(SGU_W), cur(D_MODEL),
                  cur(128), cur(128), cur(128), prev(128), prev(128), prev(128),
                  _smem(), _const2((1, SGU_W)), _const2((1, SGU_W)), _const2((N_GRP, BLK, BLK)), _const2((BLK, N_GRP)),
                  _vmem(), _vmem()],
        out_specs=[cur(ATTN_W), shifted, cur(2 * SGU_W),
                   _const2((1, ATTN_W)), _const2((1, 2 * KV_W)), _const2((1, 2 * SGU_W)),
                   _const2((1, 128)), _const2((1, SGU_W)), _const2((1, SGU_W)),
                   _const2((N_GRP, BLK, BLK)), _const2((BLK, 128)), _hbm(), _hbm()],
        out_shape=[sd((s_len, ATTN_W), _MXU), sd((s_len + per * BLK, 2 * KV_W), _MXU), sd((s_len, 2 * SGU_W), _MXU),
                   sd((1, ATTN_W), F32), sd((1, 2 * KV_W), F32), sd((1, 2 * SGU_W), F32),
                   sd((1, 128), F32), sd((1, SGU_W), F32), sd((1, SGU_W), F32),
                   sd((N_GRP, BLK, BLK), F32), sd((BLK, 128), F32)]
        + [_ChipExchange.land_shape(w) for w in prev_wires],
        scratch_shapes=[pltpu.VMEM((BLK, 2 * KV_W), F32)] + _ChipExchange.scratch(prev_wires[0])
        + _ChipExchange.scratch(prev_wires[1]),
        compiler_params=_params(40),
    )(q, k, k, v, v, su, sv, dmc, tc, t1, t2, tc, t1, t2, sinks, sg, sb, sgu_w, sgu_bt, *prev_wires)


def _inproj_bwd(dq, dkv, dsuv, dr1, x, g0, b0, w_in):
    s_len = x.shape[0]
    tm = _tile(s_len, 512)
    cuts = ((0, ATTN_W), (ATTN_W, ATTN_W + 2 * KV_W), (ATTN_W + 2 * KV_W, IN_W))

    def body(dq_ref, dkv_ref, dsuv_ref, dr1_ref, x_ref, g_ref, b_ref, w_ref, dx_ref, dw_ref, dg_ref, db_ref):
        i = pl.program_id(0)

        @pl.when(i == 0)
        def _():
            dw_ref[...] = jnp.zeros_like(dw_ref)
            dg_ref[...] = jnp.zeros_like(dg_ref)
            db_ref[...] = jnp.zeros_like(db_ref)

        h0, xhat, rstd = _ln(x_ref[...], g_ref[...], b_ref[...])
        h0b = h0.astype(_MXU)
        dh0 = ALPHA * dr1_ref[...]
        for (lo, hi), d_ref in zip(cuts, (dq_ref, dkv_ref, dsuv_ref)):
            d = d_ref[...]
            dh0 = dh0 + _dot(d, w_ref[lo:hi, :])
            dw_ref[lo:hi, :] += _dot_tn(d, h0b)
        dg_ref[...] += _colsum(dh0 * xhat)
        db_ref[...] += _colsum(dh0)
        dx_ref[...] = _ln_bwd(dh0, xhat, rstd, g_ref[...])

    vec = _hbm_shape((1, D_MODEL), F32)
    c = _const2((1, D_MODEL))
    return pl.pallas_call(
        body, name="inproj_bwd", grid=(s_len // tm,),
        in_specs=[_rows(tm, ATTN_W), _rows(tm, 2 * KV_W), _rows(tm, 2 * SGU_W), _rows(tm, D_MODEL), _rows(tm, D_MODEL),
                  c, c, _vmem()],
        out_specs=[_rows(tm, D_MODEL), _vmem(), c, c],
        out_shape=[_hbm_shape((s_len, D_MODEL), F32), jax.ShapeDtypeStruct((IN_W, D_MODEL), F32), vec, vec],
        compiler_params=_params(48),
    )(dq, dkv, dsuv, dr1, x, g0, b0, w_in)


def _place():
    x, y, c = (lax.axis_index(a) for a in MESH_AXES)
    chips = [(1 - x, y), (x, 1 - y), (1 - x, 1 - y)]
    return x, y, c, chips


class _Gather:
    def __init__(self, ins, outs, send_sems, recv_sems, spans=None):
        self.ins, self.outs, self.send_sems, self.recv_sems = ins, outs, send_sems, recv_sems
        self.n = len(ins)
        self.spans = spans or [(0, r.shape[0]) for r in ins]
        self.halves = [(hi - lo) // 2 for lo, hi in self.spans]

    def _copy(self, k, t, slot, half, to):
        rows = pl.ds(pl.multiple_of(self.spans[t][0] + half * self.halves[t], 16), self.halves[t])
        piece = self.outs[t].at[slot, rows, :]
        return pltpu.make_async_remote_copy(src_ref=piece, dst_ref=piece, send_sem=self.send_sems.at[k],
                                            recv_sem=self.recv_sems.at[k], device_id=to, device_id_type=MESH)

    def _chip_copy(self, t, d, slot):
        x, y, c, chips = _place()
        return self._copy(3 * t + d, t, slot, c, (chips[d][0], chips[d][1], c))

    def _pass_copy(self, t, d, half):
        x, y, c, chips = _place()
        return self._copy(3 * self.n + 3 * t + d, t, 2 * chips[d][0] + chips[d][1], half, (x, y, 1 - c))

    def start(self):
        x, y, c, chips = _place()
        me = 2 * x + y
        for t in range(self.n):
            lo, hi = self.spans[t]
            self.outs[t][me, lo:hi, :] = self.ins[t][lo:hi, :].astype(_WIRE)
        for t in range(self.n):
            for d in range(3):
                self._chip_copy(t, d, me).start()

    def pass_on(self):
        x, y, c, chips = _place()
        for t in range(self.n):
            for d in range(3):
                self._chip_copy(t, d, 2 * chips[d][0] + chips[d][1]).wait_recv()
                self._pass_copy(t, d, c).start()

    def finish(self):
        x, y, c, chips = _place()
        me = 2 * x + y
        for t in range(self.n):
            for d in range(3):
                self._pass_copy(t, d, 1 - c).wait_recv()
        for t in range(self.n):
            for d in range(3):
                self._chip_copy(t, d, me).wait_send()
                self._pass_copy(t, d, c).wait_send()

    @staticmethod
    def out_shapes(shards, make=jax.ShapeDtypeStruct):
        return [make((N_CHIP,) + s.shape, _WIRE) for s in shards]

    @staticmethod
    def sems(n):
        return [pltpu.SemaphoreType.DMA((6 * n,)), pltpu.SemaphoreType.DMA((6 * n,))]


class _GatherPlan:
    def __init__(self, pieces):
        self.shards = [p[0] for p in pieces]
        self.spans = [p[1] for p in pieces]
        self.earlier = [p[2] for p in pieces]
        self.n = len(pieces)
        self.carried = [t for t in range(self.n) if self.earlier[t] is not None]

    def operands(self):
        return self.shards + [self.earlier[t] for t in self.carried]

    def in_specs(self):
        return [_vmem()] * self.n + [_hbm()] * len(self.carried)

    def out_specs(self):
        return [_hbm()] * self.n

    def out_shapes(self):
        return _Gather.out_shapes(self.shards, _hbm_shape)

    def scratch(self):
        return ([pltpu.VMEM((N_CHIP,) + s.shape, _WIRE) for s in self.shards] + _Gather.sems(self.n)
                + [pltpu.SemaphoreType.DMA((self.n,)), pltpu.SemaphoreType.DMA((max(len(self.carried), 1),))])

    def bind(self, in_refs, out_refs, scratch_refs):
        plan = self
        shard_refs, earlier_refs = in_refs[:self.n], in_refs[self.n:]
        bufs = scratch_refs[:self.n]
        send_sems, recv_sems, flush_sems, carry_sems = scratch_refs[self.n:self.n + 4]
        gather = _Gather(shard_refs, bufs, send_sems, recv_sems, self.spans)

        def carry_copy(k):
            t = plan.carried[k]
            lo = plan.spans[t][0]
            return pltpu.make_async_copy(earlier_refs[k].at[:, 0:lo, :], bufs[t].at[:, 0:lo, :], carry_sems.at[k])

        class Bound:
            @staticmethod
            def start():
                for k in range(len(plan.carried)):
                    carry_copy(k).start()
                gather.start()

            @staticmethod
            def pass_on():
                gather.pass_on()

            @staticmethod
            def finish():
                gather.finish()
                for k in range(len(plan.carried)):
                    carry_copy(k).wait()
                _flush([bufs[t].at[:, 0:plan.spans[t][1], :] for t in range(plan.n)],
                       [out_refs[t].at[:, 0:plan.spans[t][1], :] for t in range(plan.n)], flush_sems)

        return Bound


def _flush(bufs, hbm_outs, sems):
    copies = [pltpu.make_async_copy(b, o, sems.at[k]) for k, (b, o) in enumerate(zip(bufs, hbm_outs))]
    for cp in copies:
        cp.start()
    for cp in copies:
        cp.wait()


def _gather_weights(shards):
    n = len(shards)

    def body(*refs):
        gather = _Gather(refs[:n], refs[n:2 * n], refs[2 * n], refs[2 * n + 1])
        gather.start()
        gather.pass_on()
        gather.finish()

    return pl.pallas_call(
        body, name="gather_weights",
        in_specs=[_vmem()] * n, out_specs=[_vmem()] * n,
        out_shape=_Gather.out_shapes(shards), scratch_shapes=_Gather.sems(n),
        compiler_params=pltpu.CompilerParams(vmem_limit_bytes=32 * MIB),
    )(*shards)


class _ChipExchange:
    def __init__(self, wire_ref, land_ref, send_sems, recv_sems):
        self.wire, self.land, self.send_sems, self.recv_sems = wire_ref, land_ref, send_sems, recv_sems

    def _copy(self, d):
        x, y, c, chips = _place()
        return pltpu.make_async_remote_copy(
            src_ref=self.wire.at[2 * chips[d][0] + chips[d][1]], dst_ref=self.land.at[d],
            send_sem=self.send_sems.at[d], recv_sem=self.recv_sems.at[d],
            device_id=(chips[d][0], chips[d][1], c), device_id_type=MESH)

    def start(self):
        for d in range(3):
            self._copy(d).start()

    def wait_recv(self):
        for d in range(3):
            self._copy(d).wait_recv()

    def wait_send(self):
        for d in range(3):
            self._copy(d).wait_send()

    def finish_to(self, hbm_out, flush_sem):
        self.wait_recv()
        _flush([self.land], [hbm_out], flush_sem)
        self.wait_send()

    @staticmethod
    def land_shape(wire):
        return _hbm_shape((3,) + wire.shape[1:], wire.dtype)

    @staticmethod
    def sems():
        return [pltpu.SemaphoreType.DMA((3,)), pltpu.SemaphoreType.DMA((3,))]

    @staticmethod
    def scratch(wire):
        return ([pltpu.VMEM((3,) + wire.shape[1:], wire.dtype)] + _ChipExchange.sems() + [pltpu.SemaphoreType.DMA((1,))])


def _pair_out_shapes(half_shape):
    return [jax.ShapeDtypeStruct(half_shape, _WIRE), jax.ShapeDtypeStruct(half_shape[1:], F32)]


def _pair_scratch(acc_shape, half_shape):
    return [pltpu.VMEM(acc_shape, F32), pltpu.VMEM(half_shape, _WIRE),
            pltpu.SemaphoreType.DMA((N_CHIP,)), pltpu.SemaphoreType.DMA((N_CHIP,))]


def _pair_reduce(acc_ref, wire_ref, own_ref, land_ref, send_sems, recv_sems):
    rh = land_ref.shape[1]
    x, y, c, _ = _place()
    me = 2 * x + y
    copies = []
    for j in range(N_CHIP):
        def cast(r, carry, j=j):
            dst = pl.ds(pl.multiple_of(r * ROW_CHUNK, ROW_CHUNK), ROW_CHUNK)
            src = pl.ds(pl.multiple_of((2 * j + 1 - c) * rh + r * ROW_CHUNK, 8), ROW_CHUNK)
            wire_ref[j, dst, :] = acc_ref[src, :].astype(_WIRE)
            return carry

        lax.fori_loop(0, rh // ROW_CHUNK, cast, 0)
        cp = pltpu.make_async_remote_copy(src_ref=wire_ref.at[j], dst_ref=land_ref.at[j], send_sem=send_sems.at[j],
                                          recv_sem=recv_sems.at[j], device_id=(x, y, 1 - c), device_id_type=MESH)
        cp.start()
        copies.append(cp)
    for j in range(N_CHIP):
        copies[j].wait()

        def chunk(r, carry, j=j):
            theirs = pl.ds(pl.multiple_of(r * ROW_CHUNK, ROW_CHUNK), ROW_CHUNK)
            mine = pl.ds(pl.multiple_of((2 * j + c) * rh + r * ROW_CHUNK, 8), ROW_CHUNK)
            wire_ref[j, theirs, :] = (acc_ref[mine, :] + land_ref[j, theirs, :].astype(F32)).astype(_WIRE)
            return carry

        lax.fori_loop(0, rh // ROW_CHUNK, chunk, 0)

    def own_chunk(r, carry):
        theirs = pl.ds(pl.multiple_of(r * ROW_CHUNK, ROW_CHUNK), ROW_CHUNK)
        mine = pl.ds(pl.multiple_of((2 * me + c) * rh + r * ROW_CHUNK, 8), ROW_CHUNK)
        own_ref[theirs, :] = acc_ref[mine, :] + land_ref[me, theirs, :].astype(F32)
        return carry

    lax.fori_loop(0, rh // ROW_CHUNK, own_chunk, 0)


def _grad_finish(last_acc, lands, owns, small):
    n = len(owns) + 1
    halves = [last_acc.shape[0] // (2 * N_CHIP)] + [w.shape[1] for w in lands]
    widths = [last_acc.shape[1]] + [a.shape[1] for a in owns]
    small_body, small_scratch = _small_allreduce_parts()
    ns = len(small)

    def body(*refs):
        acc0, land, own = refs[0], (None,) + refs[1:n], (None,) + refs[n:2 * n - 1]
        refs = refs[2 * n - 1:]
        small_in, g, small_out = refs[:ns], refs[ns:ns + n], refs[ns + n:ns + n + 2]
        refs = refs[ns + n + 2:]
        pland0, wire0, land0, own0 = refs[0:4]
        p_send, p_recv, x_send, x_recv, pair_send, pair_recv = refs[4:10]
        small_refs = refs[10:]
        land = (land0,) + land[1:]
        own = (own0,) + own[1:]
        x, y, c, chips = _place()
        me = 2 * x + y
        exchange = _ChipExchange(wire0, land0, x_send, x_recv)

        def half_rows(t, half):
            return pl.ds(pl.multiple_of(half * halves[t], 8), halves[t])

        def pair_copy(t, half):
            rows = g[t].at[half_rows(t, half), :]
            return pltpu.make_async_remote_copy(src_ref=rows, dst_ref=rows, send_sem=pair_send.at[t],
                                                recv_sem=pair_recv.at[t], device_id=(x, y, 1 - c), device_id_type=MESH)

        _pair_reduce(acc0, wire0, own0, pland0, p_send, p_recv)
        exchange.start()
        small_body(*small_in, *small_out, *small_refs)

        for t in list(range(1, n)) + [0]:
            if t == 0:
                exchange.wait_recv()

            def chunk(r, carry, t=t):
                src = pl.ds(pl.multiple_of(r * ROW_CHUNK, ROW_CHUNK), ROW_CHUNK)
                dst = pl.ds(pl.multiple_of(c * halves[t] + r * ROW_CHUNK, 8), ROW_CHUNK)
                s = own[t][src, :]
                for d in range(3):
                    s = s + land[t][d, src, :].astype(F32)
                g[t][dst, :] = s
                return carry

            lax.fori_loop(0, halves[t] // ROW_CHUNK, chunk, 0)
            pair_copy(t, c).start()
        for t in range(n):
            pair_copy(t, 1 - c).wait_recv()
        for t in range(n):
            pair_copy(t, c).wait_send()
        exchange.wait_send()

    half0 = (halves[0], widths[0])
    return pl.pallas_call(
        body, name="grad_finish",
        in_specs=[_vmem()] * (2 * n - 1 + ns), out_specs=[_vmem()] * (n + 2),
        out_shape=[jax.ShapeDtypeStruct((2 * h, w), F32) for h, w in zip(halves, widths)]
        + [jax.ShapeDtypeStruct(s, F32) for s in _SMALL_OUT_DIMS],
        scratch_shapes=[pltpu.VMEM((N_CHIP,) + half0, _WIRE), pltpu.VMEM((N_CHIP,) + half0, _WIRE),
                        pltpu.VMEM((3,) + half0, _WIRE), pltpu.VMEM(half0, F32)]
        + [pltpu.SemaphoreType.DMA((N_CHIP,)), pltpu.SemaphoreType.DMA((N_CHIP,))]
        + _ChipExchange.sems()
        + [pltpu.SemaphoreType.DMA((n,)), pltpu.SemaphoreType.DMA((n,))]
        + small_scratch,
        compiler_params=pltpu.CompilerParams(vmem_limit_bytes=56 * MIB),
    )(last_acc, *lands, *owns, *small)


_SMALL = ("ln_in_g", "ln_in_b", "b_in", "attn_sinks", "sgu_ln_g", "sgu_ln_b", "sgu_w", "sgu_b", "b_out",
          "ln_mix_g", "ln_mix_b", "ln_ffn_g", "ln_ffn_b")
_VEC_ROW = dict(ln_in_g=0, ln_in_b=1, b_in=2, attn_sinks=4, sgu_ln_g=5, sgu_ln_b=6, b_out=7, ln_mix_g=8, ln_mix_b=9,
                ln_ffn_g=10, ln_ffn_b=11)
_LOSS_ROW = 12
_VEC_ROWS = 16
_MAT_ROWS = N_GRP * BLK + BLK


_SMALL_IN = ("ln_in_g", "ln_in_b", "bq", "bkv", "bsuv", "sink", "sgu_ln_g", "sgu_ln_b", "sgu_w", "sgu_bt", "b_out",
             "ln_mix_g", "ln_mix_b", "ln_ffn_g", "ln_ffn_b", "loss")
_SMALL_OUT_DIMS = ((_VEC_ROWS, D_MODEL), (_MAT_ROWS, 128))


def _small_allreduce_parts():
    n_in = len(_SMALL_IN)

    def body(*refs):
        (g_ln_in_g, g_ln_in_b, g_bq, g_bkv, g_bsuv, g_sink, g_sln_g, g_sln_b, g_sw, g_sbt, g_bout,
         g_lmg, g_lmb, g_lfg, g_lfb, g_loss) = refs[:n_in]
        out_a, out_b = refs[n_in:n_in + 2]
        (buf_a, buf_b, pair_a, pair_b, stage_a, stage_b, tot_a, tot_b,
         p1_send, p1_recv, x_send, x_recv, p2_send, p2_recv) = refs[n_in + 2:]
        x, y, c, chips = _place()
        me = 2 * x + y
        sibling = (x, y, 1 - c)
        half_a, half_b = _VEC_ROWS // 2, _MAT_ROWS // 2

        buf_a[...] = jnp.zeros_like(buf_a)
        for row, ref in ((0, g_ln_in_g), (1, g_ln_in_b), (7, g_bout), (8, g_lmg), (9, g_lmb), (10, g_lfg), (11, g_lfb),
                         (_LOSS_ROW, g_loss)):
            buf_a[row:row + 1, :] = ref[...]
        buf_a[2:3, 0:ATTN_W] = g_bq[...]
        buf_a[2:3, ATTN_W:ATTN_W + 2 * KV_W] = g_bkv[...]
        buf_a[2:3, ATTN_W + 2 * KV_W:D_MODEL] = g_bsuv[:, 0:2 * KV_W]
        buf_a[3:4, 0:2 * SGU_W - 2 * KV_W] = g_bsuv[:, 2 * KV_W:2 * SGU_W]
        buf_a[4:5, 0:128] = g_sink[...]
        buf_a[5:6, 0:SGU_W] = g_sln_g[...]
        buf_a[6:7, 0:SGU_W] = g_sln_b[...]
        for h in range(N_GRP):
            buf_b[h * BLK:(h + 1) * BLK, :] = g_sw[h]
        buf_b[N_GRP * BLK:_MAT_ROWS, :] = g_sbt[...]

        def remote(src, dst, send_sem, recv_sem, to):
            return pltpu.make_async_remote_copy(src_ref=src, dst_ref=dst, send_sem=send_sem, recv_sem=recv_sem,
                                                device_id=to, device_id_type=MESH)

        first = [remote(buf_a, pair_a, p1_send.at[0], p1_recv.at[0], sibling),
                 remote(buf_b, pair_b, p1_send.at[1], p1_recv.at[1], sibling)]
        for cp in first:
            cp.start()
        for cp in first:
            cp.wait()
        rows_a = pl.ds(pl.multiple_of(c * half_a, 8), half_a)
        rows_b = pl.ds(pl.multiple_of(c * half_b, 8), half_b)
        stage_a[me] = buf_a[rows_a, :] + pair_a[rows_a, :]
        stage_b[me] = buf_b[rows_b, :] + pair_b[rows_b, :]

        def chip_copies(d):
            to = (chips[d][0], chips[d][1], c)
            return [remote(stage_a.at[me], stage_a.at[me], x_send.at[2 * d], x_recv.at[2 * d], to),
                    remote(stage_b.at[me], stage_b.at[me], x_send.at[2 * d + 1], x_recv.at[2 * d + 1], to)]

        def chip_arrivals(d):
            slot = 2 * chips[d][0] + chips[d][1]
            to = (chips[d][0], chips[d][1], c)
            return [remote(stage_a.at[slot], stage_a.at[slot], x_send.at[2 * d], x_recv.at[2 * d], to),
                    remote(stage_b.at[slot], stage_b.at[slot], x_send.at[2 * d + 1], x_recv.at[2 * d + 1], to)]

        for d in range(3):
            for cp in chip_copies(d):
                cp.start()
        for d in range(3):
            for cp in chip_arrivals(d):
                cp.wait_recv()
        tot_a[rows_a, :] = ((stage_a[0] + stage_a[1]) + stage_a[2]) + stage_a[3]
        tot_b[rows_b, :] = ((stage_b[0] + stage_b[1]) + stage_b[2]) + stage_b[3]

        second = [remote(tot_a.at[rows_a, :], tot_a.at[rows_a, :], p2_send.at[0], p2_recv.at[0], sibling),
                  remote(tot_b.at[rows_b, :], tot_b.at[rows_b, :], p2_send.at[1], p2_recv.at[1], sibling)]
        for cp in second:
            cp.start()
        other_a = pl.ds(pl.multiple_of((1 - c) * half_a, 8), half_a)
        other_b = pl.ds(pl.multiple_of((1 - c) * half_b, 8), half_b)
        remote(tot_a.at[other_a, :], tot_a.at[other_a, :], p2_send.at[0], p2_recv.at[0], sibling).wait_recv()
        remote(tot_b.at[other_b, :], tot_b.at[other_b, :], p2_send.at[1], p2_recv.at[1], sibling).wait_recv()
        for cp in second:
            cp.wait_send()
        for d in range(3):
            for cp in chip_copies(d):
                cp.wait_send()
        out_a[...] = tot_a[...]
        out_b[...] = tot_b[...]

    vec = pltpu.VMEM((_VEC_ROWS, D_MODEL), F32)
    mat = pltpu.VMEM((_MAT_ROWS, 128), F32)
    scratch = [vec, mat, vec, mat, pltpu.VMEM((N_CHIP, _VEC_ROWS // 2, D_MODEL), F32),
               pltpu.VMEM((N_CHIP, _MAT_ROWS // 2, 128), F32), vec, mat,
               pltpu.SemaphoreType.DMA((2,)), pltpu.SemaphoreType.DMA((2,)), pltpu.SemaphoreType.DMA((6,)),
               pltpu.SemaphoreType.DMA((6,)), pltpu.SemaphoreType.DMA((2,)), pltpu.SemaphoreType.DMA((2,))]
    return body, scratch


def _small_adamw(tot_a, tot_b, params):
    shapes = [params[nm][0].shape for nm in _SMALL]

    def body(*refs):
        ta, tb = refs[:2]
        prm = refs[2:2 + 3 * len(_SMALL)]
        outs = refs[2 + 3 * len(_SMALL):]

        def grad_of(k, name):
            if name == "sgu_w":
                return [tb[h * BLK:(h + 1) * BLK, :] for h in range(N_GRP)]
            if name == "sgu_b":
                return jnp.transpose(tb[N_GRP * BLK:_MAT_ROWS, :])[0:N_GRP, :]
            row = _VEC_ROW[name]
            if name == "b_in":
                return jnp.concatenate([ta[row:row + 1, :], ta[row + 1:row + 2, 0:IN_W - D_MODEL]], axis=1)
            return ta[row:row + 1, 0:shapes[k][-1]]

        for k, name in enumerate(_SMALL):
            w_ref, m_ref, v_ref = prm[3 * k:3 * k + 3]
            g_out, d_out, m_out, v_out = outs[4 * k:4 * k + 4]
            g = grad_of(k, name)
            if name == "sgu_w":
                for h in range(N_GRP):
                    d_, m_, v_ = _adamw_math(w_ref[h], g[h], m_ref[h], v_ref[h])
                    g_out[h], d_out[h], m_out[h], v_out[h] = g[h], d_, m_, v_
            else:
                d_, m_, v_ = _adamw_math(w_ref[...], g, m_ref[...], v_ref[...])
                g_out[...], d_out[...], m_out[...], v_out[...] = g, d_, m_, v_
        outs[-1][...] = ta[_LOSS_ROW:_LOSS_ROW + 1, :]

    ins = [tot_a, tot_b] + [_in_hbm(a) for nm in _SMALL for a in params[nm]]
    out_dims = [s for s in shapes for _ in range(4)] + [(1, D_MODEL)]
    res = pl.pallas_call(
        body, name="small_adamw", grid=(1,),
        in_specs=[_const2(a.shape) for a in ins], out_specs=[_const2(s) for s in out_dims],
        out_shape=[_hbm_shape(s, F32) for s in out_dims],
        compiler_params=_params(32),
    )(*ins)
    return {nm: tuple(res[4 * k:4 * k + 4]) for k, nm in enumerate(_SMALL)}, res[-1]


def _elementwise(name, fn, ins, out_dtypes, tile_rows=256):
    shape = ins[0].shape
    lead = shape[:-2]
    rows, cols = shape[-2:]
    tr = _tile(rows, tile_rows)
    n_lead = math.prod(lead)
    nr = rows // tr
    flat = [_in_hbm(a.reshape((n_lead, rows, cols))) for a in ins]

    def body(*refs):
        outs = fn(*[r[0] for r in refs[:len(ins)]])
        for o_ref, o in zip(refs[len(ins):], outs):
            o_ref[0] = o.astype(o_ref.dtype)

    spec = pl.BlockSpec((1, tr, cols), lambda i: (i // nr, i % nr, 0))
    res = pl.pallas_call(
        body, name=name, grid=(n_lead * nr,),
        in_specs=[spec] * len(ins), out_specs=[spec] * len(out_dtypes),
        out_shape=[_hbm_shape((n_lead, rows, cols), dt) for dt in out_dtypes],
        compiler_params=_params(48),
    )(*flat)
    return [r.reshape(shape) for r in res]


def _adamw_math(w, g, m, v):
    m = ADAM_B1 * m + (1.0 - ADAM_B1) * g
    v = ADAM_B2 * v + (1.0 - ADAM_B2) * (g * g)
    m_hat = m / (1.0 - ADAM_B1 ** ADAM_STEP)
    v_hat = v / (1.0 - ADAM_B2 ** ADAM_STEP)
    delta = -ADAM_LR * (m_hat / (jnp.sqrt(v_hat) + ADAM_EPS) + ADAM_WD * w)
    return delta, m, v


def _adamw(name, groups, tile_rows=256):
    k = len(groups)

    def fn(*blocks):
        outs = []
        for i in range(k):
            w_, g_, m_, v_ = blocks[4 * i:4 * i + 4]
            outs += [g_, *_adamw_math(w_, g_, m_, v_)]
        return outs

    res = _elementwise(name, fn, [a for grp in groups for a in grp], [F32] * (4 * k), tile_rows)
    return [res[4 * i:4 * i + 4] for i in range(k)]


def kernel(x, positions, ln_in_g, ln_in_b, w_in, b_in, attn_sinks, sgu_ln_g, sgu_ln_b, sgu_w, sgu_b, w_out, b_out, ln_mix_g, ln_mix_b, w_gate, w_up, w_down, ln_ffn_g, ln_ffn_b, loss_target, m_ln_in_g, m_ln_in_b, m_w_in, m_b_in, m_attn_sinks, m_sgu_ln_g, m_sgu_ln_b, m_sgu_w, m_sgu_b, m_w_out, m_b_out, m_ln_mix_g, m_ln_mix_b, m_w_gate, m_w_up, m_w_down, m_ln_ffn_g, m_ln_ffn_b, v_ln_in_g, v_ln_in_b, v_w_in, v_b_in, v_attn_sinks, v_sgu_ln_g, v_sgu_ln_b, v_sgu_w, v_sgu_b, v_w_out, v_b_out, v_ln_mix_g, v_ln_mix_b, v_w_gate, v_w_up, v_w_down, v_ln_ffn_g, v_ln_ffn_b):
    weights = dict(ln_in_g=ln_in_g, ln_in_b=ln_in_b, w_in=w_in, b_in=b_in, attn_sinks=attn_sinks, sgu_ln_g=sgu_ln_g,
                   sgu_ln_b=sgu_ln_b, sgu_w=sgu_w, sgu_b=sgu_b, w_out=w_out, b_out=b_out, ln_mix_g=ln_mix_g,
                   ln_mix_b=ln_mix_b, w_gate=w_gate, w_up=w_up, w_down=w_down, ln_ffn_g=ln_ffn_g, ln_ffn_b=ln_ffn_b)
    mom_m = dict(ln_in_g=m_ln_in_g, ln_in_b=m_ln_in_b, w_in=m_w_in, b_in=m_b_in, attn_sinks=m_attn_sinks,
                 sgu_ln_g=m_sgu_ln_g, sgu_ln_b=m_sgu_ln_b, sgu_w=m_sgu_w, sgu_b=m_sgu_b, w_out=m_w_out, b_out=m_b_out,
                 ln_mix_g=m_ln_mix_g, ln_mix_b=m_ln_mix_b, w_gate=m_w_gate, w_up=m_w_up, w_down=m_w_down,
                 ln_ffn_g=m_ln_ffn_g, ln_ffn_b=m_ln_ffn_b)
    mom_v = dict(ln_in_g=v_ln_in_g, ln_in_b=v_ln_in_b, w_in=v_w_in, b_in=v_b_in, attn_sinks=v_attn_sinks,
                 sgu_ln_g=v_sgu_ln_g, sgu_ln_b=v_sgu_ln_b, sgu_w=v_sgu_w, sgu_b=v_sgu_b, w_out=v_w_out, b_out=v_b_out,
                 ln_mix_g=v_ln_mix_g, ln_mix_b=v_ln_mix_b, w_gate=v_w_gate, w_up=v_w_up, w_down=v_w_down,
                 ln_ffn_g=v_ln_ffn_g, ln_ffn_b=v_ln_ffn_b)
    order = list(weights)
    big = ("w_in", "w_out", "w_gate", "w_up", "w_down")

    s_len = x.shape[1]
    xs = _in_hbm(x.reshape(s_len, D_MODEL))
    tgt = _in_hbm(loss_target.reshape(s_len, D_MODEL))
    pos_col = _in_hbm(positions.reshape(s_len, 1))
    g0, b0 = _in_hbm(ln_in_g.reshape(1, D_MODEL)), _in_hbm(ln_in_b.reshape(1, D_MODEL))
    sinks = attn_sinks.reshape(N_Q)
    sgu_w3 = _in_hbm(sgu_w.reshape(N_GRP, BLK, BLK))
    sgu_bt = _in_hbm(sgu_b.reshape(N_GRP, BLK).T)
    b_in, b_out, sgu_ln_g, sgu_ln_b, ln_mix_g, ln_mix_b, ln_ffn_g, ln_ffn_b = (
        _in_hbm(a) for a in (b_in, b_out, sgu_ln_g, sgu_ln_b, ln_mix_g, ln_mix_b, ln_ffn_g, ln_ffn_b))

    col_sharded = ("w_in", "w_gate", "w_up")

    def rowmajor(name, a):
        return jnp.swapaxes(a[0], 0, 1) if name in col_sharded else a[0]

    def as_given(name, a):
        return (jnp.swapaxes(a, 0, 1) if name in col_sharded else a)[None]

    shards = [rowmajor(n, weights[n]) for n in big]
    (gw_in,) = _gather_weights(shards[0:1])
    w_in_full = gw_in.reshape(IN_W, D_MODEL)

    sh_out, sh_gate, sh_up, sh_down = shards[1:]
    *acts, gw_out, gw_gate0 = _ln_inproj(xs, pos_col, g0, b0, w_in_full, b_in, _GatherPlan(
        [(sh_out, (0, OUT_SH), None), (sh_gate, (0, GATE_CUT), None)]))
    q, k, v, su, sv, tc, t1, t2 = (_in_hbm(a) for a in acts)
    mc, gw_gate, gw_up0 = _mixer_fwd(q, k, v, su, sv, sinks, sgu_ln_g, sgu_ln_b, sgu_w3, sgu_bt, _GatherPlan(
        [(sh_gate, (GATE_CUT, FF_SH), gw_gate0), (sh_up, (0, UP_CUT), None)]))
    mc = _in_hbm(mc)
    w_out_full = gw_out.reshape(D_MODEL, D_MODEL)
    r1, gw_up = _outproj(mc, w_out_full, b_out, xs, g0, b0, _GatherPlan([(sh_up, (UP_CUT, FF_SH), gw_up0)]))
    r1 = _in_hbm(r1)
    act, p_act, q_act, gw_down = _ffn_up(r1, ln_mix_g, ln_mix_b, gw_gate, gw_up,
                                         _GatherPlan([(sh_down, (0, FF_SH), None)]))
    act, p_act, q_act = _in_hbm(act), _in_hbm(p_act), _in_hbm(q_act)
    dr2, loss_cols, d_ln_ffn_g, d_ln_ffn_b = _ffn_down_loss(act, gw_down, r1, ln_mix_g, ln_mix_b, ln_ffn_g, ln_ffn_b, tgt)
    dr2 = _in_hbm(dr2)

    dg, du, wire_down, own_down = _ffn_bwd_a(dr2, act, p_act, q_act, gw_down)
    dh1a, wire_gate, own_gate, land_down = _ffn_bwd_g(dr2, _in_hbm(dg), r1, ln_mix_g, ln_mix_b, gw_gate, wire_down)
    dr1, wire_up, own_up, d_ln_mix_g, d_ln_mix_b, land_gate = _ffn_bwd_u(_in_hbm(dh1a), _in_hbm(du), r1, ln_mix_g,
                                                                         ln_mix_b, gw_up, wire_gate)
    dr1 = _in_hbm(dr1)
    dmc, wire_out, own_out, d_b_out = _outproj_bwd(dr1, mc, w_out_full)
    (dq, dkv, dsuv, dbq, dbkv, dbsuv, d_sink, d_sgu_ln_g, d_sgu_ln_b, d_sgu_w, d_sgu_bt, land_up, land_out) = _mixer_bwd(
        q, k, v, su, sv, _in_hbm(dmc), tc, t1, t2, sinks, sgu_ln_g, sgu_ln_b, sgu_w3, sgu_bt, [wire_up, wire_out])
    dkv = dkv[BLK:BLK + s_len]
    grad_x, acc_in, d_ln_in_g, d_ln_in_b = _inproj_bwd(_in_hbm(dq), _in_hbm(dkv), _in_hbm(dsuv), dr1, xs, g0, b0,
                                                       w_in_full)

    small_local = dict(
        ln_in_g=d_ln_in_g, ln_in_b=d_ln_in_b, bq=dbq, bkv=dbkv, bsuv=dbsuv, sink=d_sink, sgu_ln_g=d_sgu_ln_g,
        sgu_ln_b=d_sgu_ln_b, sgu_w=d_sgu_w, sgu_bt=d_sgu_bt, b_out=d_b_out, ln_mix_g=d_ln_mix_g, ln_mix_b=d_ln_mix_b,
        ln_ffn_g=d_ln_ffn_g, ln_ffn_b=d_ln_ffn_b, loss=loss_cols)
    *reduced, tot_a, tot_b = _grad_finish(acc_in, [land_out, land_gate, land_up, land_down],
                                          [own_out, own_gate, own_up, own_down], [small_local[nm] for nm in _SMALL_IN])
    small_shape = dict(ln_in_g=(1, D_MODEL), ln_in_b=(1, D_MODEL), sgu_w=(N_GRP, BLK, BLK), sgu_b=(N_GRP, BLK))
    small_params = {nm: tuple(src[nm].reshape(small_shape.get(nm, src[nm].shape)) for src in (weights, mom_m, mom_v))
                    for nm in _SMALL}
    small_out, loss_sum = _small_adamw(_in_hbm(tot_a), _in_hbm(tot_b), small_params)
    loss = jnp.sum(loss_sum) * (0.5 / D_MODEL)
    grads, delta, new_m, new_v = {}, {}, {}, {}
    for nm in _SMALL:
        grads[nm], delta[nm], new_m[nm], new_v[nm] = (a.reshape(weights[nm].shape) for a in small_out[nm])

    def update(call_name, names):
        groups = [(shards[big.index(nm)], reduced[big.index(nm)], rowmajor(nm, mom_m[nm]), rowmajor(nm, mom_v[nm]))
                  for nm in names]
        for nm, res in zip(names, _adamw(call_name, groups)):
            grads[nm], delta[nm], new_m[nm], new_v[nm] = (as_given(nm, a) for a in res)

    update("adamw_w_in", ["w_in"])
    update("adamw_w_out", ["w_out"])
    update("adamw_ffn", ["w_gate", "w_up", "w_down"])

    return (loss, grad_x.reshape(x.shape), *[grads[n] for n in order], *[delta[n] for n in order],
            *[new_m[n] for n in order], *[new_v[n] for n in order])
```

```python
import functools
import math

import jax
import jax.numpy as jnp
from jax import lax
from jax.experimental import pallas as pl
from jax.experimental.pallas import tpu as pltpu

F32 = jnp.float32
_MXU = jnp.bfloat16
_WIRE = jnp.bfloat16
_ACT = jnp.bfloat16

D_MODEL = 1024
ATTN_W = 512
SGU_W = 512
HEAD_DIM = 64
N_Q = 8
N_KV = 2
Q_PER_KV = 4
KV_W = 128
BLK = 128
ROT_DIM = 16
ROPE_THETA = 500000.0
N_GRP = 4
GRP_DIM = 128
D_FF = 2816
IN_W = 1792
LN_EPS = 1e-5
ALPHA = 2.0 ** 0.25
N_CHIP = 4
FF_SH = D_FF // N_CHIP
IN_SH = IN_W // N_CHIP
OUT_SH = D_MODEL // N_CHIP
ROW_CHUNK = 32
GATE_CUT, UP_CUT = 352, 320

ADAM_LR = 0.001
ADAM_B1 = 0.9
ADAM_B2 = 0.999
ADAM_EPS = 1e-08
ADAM_WD = 0.01
ADAM_STEP = 10

SQRT_HALF = 0.7071067811865476
INV_SQRT_2PI = 0.3989422804014327
MESH_AXES = ("x", "y", "c")
MESH = pl.DeviceIdType.MESH
MIB = 2 ** 20


def _vmem():
    return pl.BlockSpec(memory_space=pltpu.VMEM)


def _smem():
    return pl.BlockSpec(memory_space=pltpu.SMEM)


def _hbm():
    return pl.BlockSpec(memory_space=pl.ANY)


def _hbm_shape(shape, dtype):
    return pltpu.HBM(shape, dtype)


def _in_hbm(a):
    return pltpu.with_memory_space_constraint(a, pltpu.HBM)


def _params(vmem_mib=48):
    return pltpu.CompilerParams(dimension_semantics=("arbitrary",), vmem_limit_bytes=vmem_mib * MIB)


def _tile(n, cap):
    if n <= cap:
        return n
    for t in range(cap - cap % 16, 0, -16):
        if n % t == 0:
            return t
    raise ValueError((n, cap))


def _rows(tm, width):
    return pl.BlockSpec((tm, width), lambda i: (i, 0))


def _const2(shape):
    return pl.BlockSpec(shape, lambda i: (0,) * len(shape))


def _ln(x, g, b):
    mu = jnp.mean(x, axis=-1, keepdims=True)
    xc = x - mu
    var = jnp.mean(xc * xc, axis=-1, keepdims=True)
    rstd = lax.rsqrt(var + LN_EPS)
    xhat = xc * rstd
    return xhat * g + b, xhat, rstd


def _ln_bwd(dy, xhat, rstd, g):
    gdy = dy * g
    m1 = jnp.mean(gdy, axis=-1, keepdims=True)
    m2 = jnp.mean(gdy * xhat, axis=-1, keepdims=True)
    return rstd * (gdy - m1 - xhat * m2)


def _colsum(a):
    return jnp.sum(a, axis=0, keepdims=True)


def _gelu_and_grad(x):
    cdf = 0.5 * (1.0 + lax.erf(x * SQRT_HALF))
    return x * cdf, cdf + x * jnp.exp(-0.5 * x * x) * INV_SQRT_2PI


def _dot(a, b):
    return jnp.dot(a, b, preferred_element_type=F32)


def _dot_nt(a, b):
    return lax.dot_general(a, b, (((1,), (1,)), ((), ())), preferred_element_type=F32)


def _dot_tn(a, b):
    return lax.dot_general(a, b, (((0,), (0,)), ((), ())), preferred_element_type=F32)


def _rope(t, tc, t1, t2):
    n = t.shape[1]
    rep = n // 128
    if rep > 1:
        tc, t1, t2 = (jnp.tile(a, (1, rep)) for a in (tc, t1, t2))
    return t * tc + pltpu.roll(t, n - 8, 1) * t1 + pltpu.roll(t, 8, 1) * t2


def _rope_bwd(d, tc, t1, t2):
    n = d.shape[1]
    rep = n // 128
    if rep > 1:
        tc, t1, t2 = (jnp.tile(a, (1, rep)) for a in (tc, t1, t2))
    return d * tc + pltpu.roll(d * t1, 8, 1) + pltpu.roll(d * t2, n - 8, 1)


def _causal_w(w_ref, h):
    t = lax.broadcasted_iota(jnp.int32, (BLK, BLK), 0)
    s = lax.broadcasted_iota(jnp.int32, (BLK, BLK), 1)
    return jnp.where(s <= t, w_ref[h], 0.0)


def _lane_put(vals, width):
    rows = vals[0].shape[0]
    lane = lax.broadcasted_iota(jnp.int32, (rows, width), 1)
    out = jnp.zeros((rows, width), F32)
    for k, v in enumerate(vals):
        out = out + jnp.where(lane == k, v, 0.0)
    return out


def _rope_consts():
    lane = jnp.arange(128) % HEAD_DIM
    rot = lane < ROT_DIM
    pair = (2 * (lane % (ROT_DIM // 2))).astype(F32)
    freq = jnp.where(rot, ROPE_THETA ** (-pair / ROT_DIM), 0.0)
    rows = [freq, rot.astype(F32), 1.0 - rot.astype(F32), (lane < ROT_DIM // 2).astype(F32),
            jnp.logical_and(lane >= ROT_DIM // 2, rot).astype(F32)]
    rows += [jnp.zeros((128,), F32)] * 3
    return jnp.stack(rows).astype(F32)


def _ln_inproj(x, pos_col, g0, b0, w_in, b_in, plan):
    s_len = x.shape[0]
    tm = _tile(s_len, 512)
    m, n = len(plan.operands()), plan.n

    def body(x_ref, pos_ref, g_ref, b_ref, w_ref, bi_ref, rc_ref, *rest):
        q_ref, k_ref, v_ref, su_ref, sv_ref, tc_ref, t1_ref, t2_ref = rest[m:m + 8]
        gather = plan.bind(rest[:m], rest[m + 8:m + 8 + n], rest[m + 8 + n:])
        i = pl.program_id(0)

        @pl.when(i == 0)
        def _():
            gather.start()

        h0, _, _ = _ln(x_ref[...], g_ref[...], b_ref[...])
        proj = _dot_nt(h0.astype(_MXU), w_ref[...]) + bi_ref[...]
        ang = pos_ref[...].astype(F32) * rc_ref[0:1, :]
        cs = jnp.cos(ang)
        sn = jnp.sin(ang)
        tc = cs * rc_ref[1:2, :] + rc_ref[2:3, :]
        t1 = -sn * rc_ref[3:4, :]
        t2 = sn * rc_ref[4:5, :]
        tc_ref[...] = tc
        t1_ref[...] = t1
        t2_ref[...] = t2
        q = _rope(proj[:, 0:ATTN_W], tc, t1, t2) * (HEAD_DIM ** -0.5)
        q_ref[...] = q.astype(_MXU)
        k_ref[...] = _rope(proj[:, ATTN_W:ATTN_W + KV_W], tc, t1, t2).astype(_MXU)
        v_ref[...] = proj[:, ATTN_W + KV_W:ATTN_W + 2 * KV_W].astype(_MXU)
        su_ref[...] = proj[:, ATTN_W + 2 * KV_W:ATTN_W + 2 * KV_W + SGU_W]
        sv_ref[...] = proj[:, ATTN_W + 2 * KV_W + SGU_W:IN_W]

        last = pl.num_programs(0) - 1

        @pl.when(i == jnp.maximum(last - 1, 0))
        def _():
            gather.pass_on()

        @pl.when(i == last)
        def _():
            gather.finish()

    sd = _hbm_shape
    return pl.pallas_call(
        body, name="ln_inproj", grid=(s_len // tm,),
        in_specs=[_rows(tm, D_MODEL), _rows(tm, 1), _const2((1, D_MODEL)), _const2((1, D_MODEL)), _vmem(),
                  _const2((1, IN_W)), _const2((8, 128))] + plan.in_specs(),
        out_specs=[_rows(tm, ATTN_W), _rows(tm, KV_W), _rows(tm, KV_W), _rows(tm, SGU_W), _rows(tm, SGU_W),
                   _rows(tm, 128), _rows(tm, 128), _rows(tm, 128)] + plan.out_specs(),
        out_shape=[sd((s_len, ATTN_W), _MXU), sd((s_len, KV_W), _MXU), sd((s_len, KV_W), _MXU),
                   sd((s_len, SGU_W), F32), sd((s_len, SGU_W), F32),
                   sd((s_len, 128), F32), sd((s_len, 128), F32), sd((s_len, 128), F32)] + plan.out_shapes(),
        scratch_shapes=plan.scratch(),
        compiler_params=_params(56),
    )(x, pos_col, g0, b0, w_in, b_in, _rope_consts(), *plan.operands())


def _band_mask_t(first_block):
    kj = lax.broadcasted_iota(jnp.int32, (2 * BLK, BLK), 0)
    qi = lax.broadcasted_iota(jnp.int32, (2 * BLK, BLK), 1)
    shut = jnp.where(first_block, 2 * BLK, 0)
    prev_ok = jnp.logical_and(kj < BLK, kj > qi + shut)
    cur_ok = jnp.logical_and(kj >= BLK, (kj - BLK) <= qi)
    return jnp.logical_or(prev_ok, cur_ok)


def _attn_probs_t(kh, qh, sink, allowed_t):
    s = jnp.where(allowed_t, _dot_nt(kh, qh), -1e30)
    m = jnp.maximum(jnp.max(s, axis=0, keepdims=True), sink)
    p = jnp.exp(s - m)
    ps = jnp.exp(sink - m)
    inv = 1.0 / (jnp.sum(p, axis=0, keepdims=True) + ps)
    return p * inv, ps * inv


def _sgu_mix(gv, lg, lb, w_ref, bt_ref):
    vv, vhat, rstd = _ln(gv, lg, lb)
    vvb = vv.astype(_MXU)
    wcs, mixed = [], []
    for h in range(N_GRP):
        wc = _causal_w(w_ref, h).astype(_MXU)
        wcs.append(wc)
        mixed.append(_dot(wc, vvb[:, h * GRP_DIM:(h + 1) * GRP_DIM]) + bt_ref[:, h:h + 1])
    return jnp.concatenate(mixed, axis=1), vhat, rstd, vvb, wcs


def _mixer_fwd(q, k, v, su, sv, sinks, sg, sb, sgu_w, sgu_bt, plan):
    s_len = q.shape[0]
    nb = s_len // BLK
    per = 2 if nb % 2 == 0 else 1
    steps = nb // per
    m, n = len(plan.operands()), plan.n

    def body(q_ref, kc_ref, kp_ref, vc_ref, vp_ref, su_ref, sv_ref, sink_ref, lg_ref, lb_ref, w_ref, bt_ref, *rest):
        mc_ref = rest[m]
        gather = plan.bind(rest[:m], rest[m + 1:m + 1 + n], rest[m + 1 + n:])
        i = pl.program_id(0)

        @pl.when(i == 0)
        def _():
            gather.start()

        @pl.when(i == max(steps - 2, 0))
        def _():
            gather.pass_on()

        @pl.when(i == steps - 1)
        def _():
            gather.finish()

        for s in range(per):
            rows = slice(s * BLK, (s + 1) * BLK)
            before = slice((s - 1) * BLK, s * BLK)
            k_prev = kp_ref[...] if s == 0 else kc_ref[before, :]
            v_prev = vp_ref[...] if s == 0 else vc_ref[before, :]
            allowed_t = _band_mask_t(i == 0 if s == 0 else False)
            kb = jnp.concatenate([k_prev, kc_ref[rows, :]], axis=0)
            vb = jnp.concatenate([v_prev, vc_ref[rows, :]], axis=0)
            qv = q_ref[rows, :]
            outs = []
            allowed_g = jnp.tile(allowed_t, (1, Q_PER_KV))
            for g in range(N_KV):
                heads = range(g * Q_PER_KV, (g + 1) * Q_PER_KV)
                kh = kb[:, g * HEAD_DIM:(g + 1) * HEAD_DIM]
                vh = vb[:, g * HEAD_DIM:(g + 1) * HEAD_DIM]
                q_g = jnp.concatenate([qv[:, h * HEAD_DIM:(h + 1) * HEAD_DIM] for h in heads], axis=0)
                sink_g = jnp.concatenate([jnp.full((1, BLK), sink_ref[h], F32) for h in heads], axis=1)
                probs_t, _ = _attn_probs_t(kh, q_g, sink_g, allowed_g)
                o_g = _dot_tn(probs_t.astype(_MXU), vh)
                outs += [o_g[hh * BLK:(hh + 1) * BLK, :] for hh in range(Q_PER_KV)]
            u = _gelu_and_grad(su_ref[rows, :])[0]
            gv = _gelu_and_grad(sv_ref[rows, :])[0]
            mixed = _sgu_mix(gv, lg_ref[...], lb_ref[...], w_ref, bt_ref)[0]
            mc_ref[rows, :] = jnp.concatenate(outs + [u * mixed], axis=1).astype(_MXU)

    cur = lambda w: pl.BlockSpec((per * BLK, w), lambda i: (i, 0))
    prev = lambda w: pl.BlockSpec((BLK, w), lambda i: (jnp.maximum(per * i - 1, 0), 0))
    return pl.pallas_call(
        body, name="mixer_fwd", grid=(steps,),
        in_specs=[cur(ATTN_W), cur(KV_W), prev(KV_W), cur(KV_W), prev(KV_W), cur(SGU_W), cur(SGU_W), _smem(),
                  _const2((1, SGU_W)), _const2((1, SGU_W)), _const2((N_GRP, BLK, BLK)), _const2((BLK, N_GRP))]
        + plan.in_specs(),
        out_specs=[cur(D_MODEL)] + plan.out_specs(),
        out_shape=[_hbm_shape((s_len, D_MODEL), _MXU)] + plan.out_shapes(),
        scratch_shapes=plan.scratch(),
        compiler_params=_params(56),
    )(q, k, k, v, v, su, sv, sinks, sg, sb, sgu_w, sgu_bt, *plan.operands())


def _outproj(mc, w_out, b_out, x, g0, b0, plan):
    s_len = x.shape[0]
    tm = _tile(s_len, 512)
    m, n = len(plan.operands()), plan.n

    def body(mc_ref, w_ref, bo_ref, x_ref, g_ref, b_ref, *rest):
        r1_ref = rest[m]
        gather = plan.bind(rest[:m], rest[m + 1:m + 1 + n], rest[m + 1 + n:])
        i = pl.program_id(0)

        @pl.when(i == 0)
        def _():
            gather.start()

        h0, _, _ = _ln(x_ref[...], g_ref[...], b_ref[...])
        r1_ref[...] = ALPHA * h0 + (_dot(mc_ref[...], w_ref[...]) + bo_ref[...])

        last = pl.num_programs(0) - 1

        @pl.when(i == jnp.maximum(last - 1, 0))
        def _():
            gather.pass_on()

        @pl.when(i == last)
        def _():
            gather.finish()

    return pl.pallas_call(
        body, name="outproj", grid=(s_len // tm,),
        in_specs=[_rows(tm, D_MODEL), _vmem(), _const2((1, D_MODEL)), _rows(tm, D_MODEL),
                  _const2((1, D_MODEL)), _const2((1, D_MODEL))] + plan.in_specs(),
        out_specs=[_rows(tm, D_MODEL)] + plan.out_specs(),
        out_shape=[_hbm_shape((s_len, D_MODEL), F32)] + plan.out_shapes(),
        scratch_shapes=plan.scratch(),
        compiler_params=_params(40),
    )(mc, w_out, b_out, x, g0, b0, *plan.operands())


def _ffn_spec(tm):
    return pl.BlockSpec((N_CHIP, tm, FF_SH), lambda i: (0, i, 0))


def _ffn_up(r1, g1, b1, wg, wu, plan):
    s_len = r1.shape[0]
    tm = _tile(s_len, 512)
    m, n = len(plan.operands()), plan.n

    def body(r1_ref, g_ref, b_ref, wg_ref, wu_ref, *rest):
        a_ref, p_ref, q_ref = rest[m:m + 3]
        gather = plan.bind(rest[:m], rest[m + 3:m + 3 + n], rest[m + 3 + n:])
        i = pl.program_id(0)

        @pl.when(i == 0)
        def _():
            gather.start()

        h1, _, _ = _ln(r1_ref[...], g_ref[...], b_ref[...])
        h1b = h1.astype(_MXU)
        for j in range(N_CHIP):
            g = _dot_nt(h1b, wg_ref[j])
            u = _dot_nt(h1b, wu_ref[j])
            silu, sg = _silu_parts(g)
            a_ref[j] = (silu * u).astype(_MXU)
            p_ref[j] = silu.astype(_ACT)
            q_ref[j] = (u * (sg * (1.0 + g * (1.0 - sg)))).astype(_ACT)

        last = pl.num_programs(0) - 1

        @pl.when(i == jnp.maximum(last - 1, 0))
        def _():
            gather.pass_on()

        @pl.when(i == last)
        def _():
            gather.finish()

    sd = _hbm_shape((N_CHIP, s_len, FF_SH), _ACT)
    return pl.pallas_call(
        body, name="ffn_up", grid=(s_len // tm,),
        in_specs=[_rows(tm, D_MODEL), _const2((1, D_MODEL)), _const2((1, D_MODEL)), _vmem(), _vmem()] + plan.in_specs(),
        out_specs=[_ffn_spec(tm)] * 3 + plan.out_specs(),
        out_shape=[_hbm_shape((N_CHIP, s_len, FF_SH), _MXU), sd, sd] + plan.out_shapes(),
        scratch_shapes=plan.scratch(),
        compiler_params=_params(56),
    )(r1, g1, b1, wg, wu, *plan.operands())


def _silu_parts(g):
    sg = 1.0 / (1.0 + jnp.exp(-g))
    return g * sg, sg


def _ffn_down_loss(act, wd, r1, g1, b1, g2, b2, target):
    s_len = r1.shape[0]
    tm = _tile(s_len, 512)

    parts = 2 if tm % 32 == 0 else 1
    sub = tm // parts

    def body(a_ref, wd_ref, r1_ref, g1_ref, b1_ref, g2_ref, b2_ref, t_ref, dr2_ref, loss_ref, dg2_ref, db2_ref):
        i = pl.program_id(0)

        @pl.when(i == 0)
        def _():
            loss_ref[...] = jnp.zeros_like(loss_ref)
            dg2_ref[...] = jnp.zeros_like(dg2_ref)
            db2_ref[...] = jnp.zeros_like(db2_ref)

        for part in range(parts):
            rows = slice(part * sub, (part + 1) * sub)
            f = jnp.zeros((sub, D_MODEL), F32)
            for j in range(N_CHIP):
                f = f + _dot(a_ref[j, rows, :], wd_ref[j])
            h1, _, _ = _ln(r1_ref[rows, :], g1_ref[...], b1_ref[...])
            h2, r2hat, rstd2 = _ln(ALPHA * h1 + f, g2_ref[...], b2_ref[...])
            diff = h2 - t_ref[rows, :]
            dh2 = diff * (1.0 / D_MODEL)
            loss_ref[...] += _colsum(diff * diff)
            dg2_ref[...] += _colsum(dh2 * r2hat)
            db2_ref[...] += _colsum(dh2)
            dr2_ref[rows, :] = _ln_bwd(dh2, r2hat, rstd2, g2_ref[...])

    vec = _hbm_shape((1, D_MODEL), F32)
    c = _const2((1, D_MODEL))
    return pl.pallas_call(
        body, name="ffn_down_loss", grid=(s_len // tm,),
        in_specs=[_ffn_spec(tm), _vmem(), _rows(tm, D_MODEL), c, c, c, c, _rows(tm, D_MODEL)],
        out_specs=[_rows(tm, D_MODEL), c, c, c],
        out_shape=[_hbm_shape((s_len, D_MODEL), F32), vec, vec, vec],
        compiler_params=_params(48),
    )(act, wd, r1, g1, b1, g2, b2, target)


def _ffn_bwd_a(dr2, act, p_act, q_act, wd):
    s_len = dr2.shape[0]
    tm = _tile(s_len, 512)

    def body(dr2_ref, a_ref, p_ref, q_ref, wd_ref, dg_ref, du_ref, wire_ref, own_ref,
             dwd_ref, land_ref, send_sem, recv_sem):
        i = pl.program_id(0)

        @pl.when(i == 0)
        def _():
            dwd_ref[...] = jnp.zeros_like(dwd_ref)

        dfb = dr2_ref[...].astype(_MXU)
        for j in range(N_CHIP):
            da = _dot_nt(dfb, wd_ref[j])
            dg_ref[j] = (da * q_ref[j].astype(F32)).astype(_MXU)
            du_ref[j] = (da * p_ref[j].astype(F32)).astype(_MXU)
            dwd_ref[j * FF_SH:(j + 1) * FF_SH, :] += _dot_tn(a_ref[j], dfb)

        @pl.when(i == pl.num_programs(0) - 1)
        def _():
            _pair_reduce(dwd_ref, wire_ref, own_ref, land_ref, send_sem, recv_sem)

    sd = _hbm_shape((N_CHIP, s_len, FF_SH), _MXU)
    half = (N_CHIP, FF_SH // 2, D_MODEL)
    return pl.pallas_call(
        body, name="ffn_bwd_a", grid=(s_len // tm,),
        in_specs=[_rows(tm, D_MODEL), _ffn_spec(tm), _ffn_spec(tm), _ffn_spec(tm), _vmem()],
        out_specs=[_ffn_spec(tm), _ffn_spec(tm), _vmem(), _vmem()],
        out_shape=[sd, sd] + _pair_out_shapes(half),
        scratch_shapes=_pair_scratch((D_FF, D_MODEL), half),
        compiler_params=_params(61),
    )(dr2, act, p_act, q_act, wd)


def _ffn_bwd_g(dr2, dg, r1, g1, b1, wg, prev_wire):
    s_len = dr2.shape[0]
    tm = _tile(s_len, 512)

    def body(dr2_ref, dg_ref, r1_ref, g1_ref, b1_ref, wg_ref, pw_ref, dh1_ref, wire_ref, own_ref, pl_ref,
             dwg_ref, land_ref, send_sem, recv_sem, xl_ref, x_send, x_recv, x_flush):
        i = pl.program_id(0)
        exchange = _ChipExchange(pw_ref, xl_ref, x_send, x_recv)

        @pl.when(i == 0)
        def _():
            exchange.start()
            dwg_ref[...] = jnp.zeros_like(dwg_ref)

        h1, _, _ = _ln(r1_ref[...], g1_ref[...], b1_ref[...])
        h1b = h1.astype(_MXU)
        dh1 = ALPHA * dr2_ref[...]
        for j in range(N_CHIP):
            dgj = dg_ref[j]
            dh1 = dh1 + _dot(dgj, wg_ref[j])
            dwg_ref[j * FF_SH:(j + 1) * FF_SH, :] += _dot_tn(dgj, h1b)
        dh1_ref[...] = dh1

        @pl.when(i == pl.num_programs(0) - 1)
        def _():
            _pair_reduce(dwg_ref, wire_ref, own_ref, land_ref, send_sem, recv_sem)
            exchange.finish_to(pl_ref, x_flush)

    c = _const2((1, D_MODEL))
    half = (N_CHIP, FF_SH // 2, D_MODEL)
    return pl.pallas_call(
        body, name="ffn_bwd_g", grid=(s_len // tm,),
        in_specs=[_rows(tm, D_MODEL), _ffn_spec(tm), _rows(tm, D_MODEL), c, c, _vmem(), _vmem()],
        out_specs=[_rows(tm, D_MODEL), _vmem(), _vmem(), _hbm()],
        out_shape=[_hbm_shape((s_len, D_MODEL), F32)] + _pair_out_shapes(half) + [_ChipExchange.land_shape(prev_wire)],
        scratch_shapes=_pair_scratch((D_FF, D_MODEL), half) + _ChipExchange.scratch(prev_wire),
        compiler_params=_params(58),
    )(dr2, dg, r1, g1, b1, wg, prev_wire)


def _ffn_bwd_u(dh1a, du, r1, g1, b1, wu, prev_wire):
    s_len = dh1a.shape[0]
    tm = _tile(s_len, 512)

    def body(dh1_ref, du_ref, r1_ref, g1_ref, b1_ref, wu_ref, pw_ref,
             dr1_ref, wire_ref, own_ref, dg1_ref, db1_ref, pl_ref,
             dwu_ref, land_ref, send_sem, recv_sem, xl_ref, x_send, x_recv, x_flush):
        i = pl.program_id(0)
        exchange = _ChipExchange(pw_ref, xl_ref, x_send, x_recv)

        @pl.when(i == 0)
        def _():
            exchange.start()
            dwu_ref[...] = jnp.zeros_like(dwu_ref)
            dg1_ref[...] = jnp.zeros_like(dg1_ref)
            db1_ref[...] = jnp.zeros_like(db1_ref)

        h1, r1hat, rstd1 = _ln(r1_ref[...], g1_ref[...], b1_ref[...])
        h1b = h1.astype(_MXU)
        dh1 = dh1_ref[...]
        for j in range(N_CHIP):
            duj = du_ref[j]
            dh1 = dh1 + _dot(duj, wu_ref[j])
            dwu_ref[j * FF_SH:(j + 1) * FF_SH, :] += _dot_tn(duj, h1b)
        dg1_ref[...] += _colsum(dh1 * r1hat)
        db1_ref[...] += _colsum(dh1)
        dr1_ref[...] = _ln_bwd(dh1, r1hat, rstd1, g1_ref[...])

        @pl.when(i == pl.num_programs(0) - 1)
        def _():
            _pair_reduce(dwu_ref, wire_ref, own_ref, land_ref, send_sem, recv_sem)
            exchange.finish_to(pl_ref, x_flush)

    vec = _hbm_shape((1, D_MODEL), F32)
    c = _const2((1, D_MODEL))
    half = (N_CHIP, FF_SH // 2, D_MODEL)
    return pl.pallas_call(
        body, name="ffn_bwd_u", grid=(s_len // tm,),
        in_specs=[_rows(tm, D_MODEL), _ffn_spec(tm), _rows(tm, D_MODEL), c, c, _vmem(), _vmem()],
        out_specs=[_rows(tm, D_MODEL), _vmem(), _vmem(), c, c, _hbm()],
        out_shape=[_hbm_shape((s_len, D_MODEL), F32)] + _pair_out_shapes(half)
        + [vec, vec, _ChipExchange.land_shape(prev_wire)],
        scratch_shapes=_pair_scratch((D_FF, D_MODEL), half) + _ChipExchange.scratch(prev_wire),
        compiler_params=_params(58),
    )(dh1a, du, r1, g1, b1, wu, prev_wire)


def _outproj_bwd(dr1, mc, w_out):
    s_len = dr1.shape[0]
    tm = _tile(s_len, 512)

    def body(dr1_ref, mc_ref, w_ref, dmc_ref, wire_ref, own_ref, db_ref, dw_ref, land_ref, send_sem, recv_sem):
        i = pl.program_id(0)

        @pl.when(i == 0)
        def _():
            dw_ref[...] = jnp.zeros_like(dw_ref)
            db_ref[...] = jnp.zeros_like(db_ref)

        d = dr1_ref[...]
        db_ref[...] += _colsum(d)
        db16 = d.astype(_MXU)
        dmc_ref[...] = _dot_nt(db16, w_ref[...])
        dw_ref[...] += _dot_tn(mc_ref[...], db16)

        @pl.when(i == pl.num_programs(0) - 1)
        def _():
            _pair_reduce(dw_ref, wire_ref, own_ref, land_ref, send_sem, recv_sem)

    half = (N_CHIP, OUT_SH // 2, D_MODEL)
    return pl.pallas_call(
        body, name="outproj_bwd", grid=(s_len // tm,),
        in_specs=[_rows(tm, D_MODEL), _rows(tm, D_MODEL), _vmem()],
        out_specs=[_rows(tm, D_MODEL), _vmem(), _vmem(), _const2((1, D_MODEL))],
        out_shape=[_hbm_shape((s_len, D_MODEL), F32)] + _pair_out_shapes(half) + [_hbm_shape((1, D_MODEL), F32)],
        scratch_shapes=_pair_scratch((D_MODEL, D_MODEL), half),
        compiler_params=_params(48),
    )(dr1, mc, w_out)


def _mixer_bwd(q, k, v, su, sv, dmc, tc, t1, t2, sinks, sg, sb, sgu_w, sgu_bt, prev_wires):
    s_len = q.shape[0]
    nb = s_len // BLK
    per = next(p for p in (4, 2, 1) if nb % p == 0)
    steps = nb // per

    def body(q_ref, kc_ref, kp_ref, vc_ref, vp_ref, su_ref, sv_ref, dmc_ref,
             tc_ref, t1_ref, t2_ref, tcp_ref, t1p_ref, t2p_ref,
             sink_ref, lg_ref, lb_ref, w_ref, bt_ref, pw0_ref, pw1_ref,
             dq_ref, dkv_ref, dsuv_ref, dbq_ref, dbkv_ref, dbsuv_ref,
             dsink_ref, dlg_ref, dlb_ref, dw_ref, dbt_ref, pl0_ref, pl1_ref, carry_ref,
             xl0_ref, x0_send, x0_recv, x0_flush, xl1_ref, x1_send, x1_recv, x1_flush):
        i = pl.program_id(0)
        exchanges = [(_ChipExchange(pw0_ref, xl0_ref, x0_send, x0_recv), pl0_ref, x0_flush),
                     (_ChipExchange(pw1_ref, xl1_ref, x1_send, x1_recv), pl1_ref, x1_flush)]

        @pl.when(i == 0)
        def _():
            for exchange, _, _ in exchanges:
                exchange.start()

        @pl.when(i == 0)
        def _():
            for r in (dbq_ref, dbkv_ref, dbsuv_ref, dsink_ref, dlg_ref, dlb_ref, dw_ref, dbt_ref, carry_ref):
                r[...] = jnp.zeros_like(r)

        def emit_kv(fin, t):
            if t == 0:
                tables = (tcp_ref[...], t1p_ref[...], t2p_ref[...])
            else:
                before = slice((t - 1) * BLK, t * BLK)
                tables = (tc_ref[before, :], t1_ref[before, :], t2_ref[before, :])
            dk = _rope_bwd(fin[:, 0:KV_W], *tables)
            out = jnp.concatenate([dk, fin[:, KV_W:2 * KV_W]], axis=1)
            dkv_ref[t * BLK:(t + 1) * BLK, :] = out.astype(_MXU)
            dbkv_ref[...] += _colsum(out)

        def one_block(s):
            rows = slice(s * BLK, (s + 1) * BLK)
            before = slice((s - 1) * BLK, s * BLK)
            k_prev = kp_ref[...] if s == 0 else kc_ref[before, :]
            v_prev = vp_ref[...] if s == 0 else vc_ref[before, :]
            allowed_t = _band_mask_t(i == 0 if s == 0 else False)
            kb = jnp.concatenate([k_prev, kc_ref[rows, :]], axis=0)
            vb = jnp.concatenate([v_prev, vc_ref[rows, :]], axis=0)
            qv = q_ref[rows, :]
            dmc = dmc_ref[rows, :]
            dqs, dks, dvs, dsinks = [], [], [], []
            allowed_g = jnp.tile(allowed_t, (1, Q_PER_KV))
            for g in range(N_KV):
                heads = range(g * Q_PER_KV, (g + 1) * Q_PER_KV)
                kh = kb[:, g * HEAD_DIM:(g + 1) * HEAD_DIM]
                vh = vb[:, g * HEAD_DIM:(g + 1) * HEAD_DIM]
                q_g = jnp.concatenate([qv[:, h * HEAD_DIM:(h + 1) * HEAD_DIM] for h in heads], axis=0)
                do_g = jnp.concatenate([dmc[:, h * HEAD_DIM:(h + 1) * HEAD_DIM] for h in heads], axis=0).astype(_MXU)
                sink_g = jnp.concatenate([jnp.full((1, BLK), sink_ref[h], F32) for h in heads], axis=1)
                probs_t, psink = _attn_probs_t(kh, q_g, sink_g, allowed_g)
                dvs.append(_dot(probs_t.astype(_MXU), do_g))
                dp_t = _dot_nt(vh, do_g)
                rd = jnp.sum(probs_t * dp_t, axis=0, keepdims=True)
                ds_t = (probs_t * (dp_t - rd)).astype(_MXU)
                ps_rd = psink * rd
                for hh in range(Q_PER_KV):
                    dsinks.append(-jnp.sum(ps_rd[:, hh * BLK:(hh + 1) * BLK], axis=1, keepdims=True))
                dq_g = _dot_tn(ds_t, kh)
                dqs += [dq_g[hh * BLK:(hh + 1) * BLK, :] for hh in range(Q_PER_KV)]
                dks.append(_dot(ds_t, q_g))
            dq = _rope_bwd(jnp.concatenate(dqs, axis=1) * (HEAD_DIM ** -0.5),
                           tc_ref[rows, :], t1_ref[rows, :], t2_ref[rows, :])
            dq_ref[rows, :] = dq.astype(_MXU)
            dbq_ref[...] += _colsum(dq)
            dsink_ref[...] += _lane_put(dsinks, 128)
            contrib = jnp.concatenate(dks + dvs, axis=1)

            lg = lg_ref[...]
            u, du_dsu = _gelu_and_grad(su_ref[rows, :])
            gv, dgv_dsv = _gelu_and_grad(sv_ref[rows, :])
            mixed, vhat, rstd, vvb, wcs = _sgu_mix(gv, lg, lb_ref[...], w_ref, bt_ref)
            dsgu = dmc[:, ATTN_W:D_MODEL]
            dsu = dsgu * mixed * du_dsu
            dmixed = dsgu * u
            tri_t = lax.broadcasted_iota(jnp.int32, (BLK, BLK), 0)
            tri_s = lax.broadcasted_iota(jnp.int32, (BLK, BLK), 1)
            dvv, dbs = [], []
            for h in range(N_GRP):
                dm = dmixed[:, h * GRP_DIM:(h + 1) * GRP_DIM]
                dmb = dm.astype(_MXU)
                dbs.append(jnp.sum(dm, axis=1, keepdims=True))
                dw_ref[h] += jnp.where(tri_s <= tri_t, _dot_nt(dmb, vvb[:, h * GRP_DIM:(h + 1) * GRP_DIM]), 0.0)
                dvv.append(_dot_tn(wcs[h], dmb))
            dvv = jnp.concatenate(dvv, axis=1)
            dbt_ref[...] += _lane_put(dbs, 128)
            dlg_ref[...] += _colsum(dvv * vhat)
            dlb_ref[...] += _colsum(dvv)
            dsv = _ln_bwd(dvv, vhat, rstd, lg) * dgv_dsv
            dsuv = jnp.concatenate([dsu, dsv], axis=1)
            dsuv_ref[rows, :] = dsuv.astype(_MXU)
            dbsuv_ref[...] += _colsum(dsuv)
            return contrib

        @pl.when(i < steps)
        def _():
            contribs = [one_block(s) for s in range(per)]
            for t in range(per):
                top = carry_ref[...] if t == 0 else contribs[t - 1][BLK:2 * BLK, :]
                emit_kv(top + contribs[t][0:BLK, :], t)
            carry_ref[...] = contribs[per - 1][BLK:2 * BLK, :]

        @pl.when(i == steps)
        def _():
            emit_kv(carry_ref[...], 0)
            if per > 1:
                dkv_ref[BLK:per * BLK, :] = jnp.zeros(((per - 1) * BLK, 2 * KV_W), _MXU)
            for exchange, landed, flush_sem in exchanges:
                exchange.finish_to(landed, flush_sem)

    last = steps - 1
    cur = lambda w: pl.BlockSpec((per * BLK, w), lambda i: (jnp.minimum(i, last), 0))
    prev = lambda w: pl.BlockSpec((BLK, w), lambda i: (jnp.clip(per * i - 1, 0, nb - 1), 0))
    shifted = pl.BlockSpec((per * BLK, 2 * KV_W), lambda i: (i, 0))
    sd = _hbm_shape
    return pl.pallas_call(
        body, name="mixer_bwd", grid=(steps + 1,),
        in_specs=[cur(ATTN_W), cur(KV_W), prev(KV_W), cur(KV_W), prev(KV_W), cur(SGU_W), cur(SGU_W), cur(D_MODEL),
                  cur(128), cur(128), cur(128), prev(128), prev(128), prev(128),
                  _smem(), _const2((1, SGU_W)), _const2((1, SGU_W)), _const2((N_GRP, BLK, BLK)), _const2((BLK, N_GRP)),
                  _vmem(), _vmem()],
        out_specs=[cur(ATTN_W), shifted, cur(2 * SGU_W),
                   _const2((1, ATTN_W)), _const2((1, 2 * KV_W)), _const2((1, 2 * SGU_W)),
                   _const2((1, 128)), _const2((1, SGU_W)), _const2((1, SGU_W)),
                   _const2((N_GRP, BLK, BLK)), _const2((BLK, 128)), _hbm(), _hbm()],
        out_shape=[sd((s_len, ATTN_W), _MXU), sd((s_len + per * BLK, 2 * KV_W), _MXU), sd((s_len, 2 * SGU_W), _MXU),
                   sd((1, ATTN_W), F32), sd((1, 2 * KV_W), F32), sd((1, 2 * SGU_W), F32),
                   sd((1, 128), F32), sd((1, SGU_W), F32), sd((1, SGU_W), F32),
                   sd((N_GRP, BLK, BLK), F32), sd((BLK, 128), F32)]
        + [_ChipExchange.land_shape(w) for w in prev_wires],
        scratch_shapes=[pltpu.VMEM((BLK, 2 * KV_W), F32)] + _ChipExchange.scratch(prev_wires[0])
        + _ChipExchange.scratch(prev_wires[1]),
        compiler_params=_params(40),
    )(q, k, k, v, v, su, sv, dmc, tc, t1, t2, tc, t1, t2, sinks, sg, sb, sgu_w, sgu_bt, *prev_wires)


def _inproj_bwd(dq, dkv, dsuv, dr1, x, g0, b0, w_in):
    s_len = x.shape[0]
    tm = _tile(s_len, 512)
    cuts = ((0, ATTN_W), (ATTN_W, ATTN_W + 2 * KV_W), (ATTN_W + 2 * KV_W, IN_W))

    def body(dq_ref, dkv_ref, dsuv_ref, dr1_ref, x_ref, g_ref, b_ref, w_ref, dx_ref, dw_ref, dg_ref, db_ref):
        i = pl.program_id(0)

        @pl.when(i == 0)
        def _():
            dw_ref[...] = jnp.zeros_like(dw_ref)
            dg_ref[...] = jnp.zeros_like(dg_ref)
            db_ref[...] = jnp.zeros_like(db_ref)

        h0, xhat, rstd = _ln(x_ref[...], g_ref[...], b_ref[...])
        h0b = h0.astype(_MXU)
        dh0 = ALPHA * dr1_ref[...]
        for (lo, hi), d_ref in zip(cuts, (dq_ref, dkv_ref, dsuv_ref)):
            d = d_ref[...]
            dh0 = dh0 + _dot(d, w_ref[lo:hi, :])
            dw_ref[lo:hi, :] += _dot_tn(d, h0b)
        dg_ref[...] += _colsum(dh0 * xhat)
        db_ref[...] += _colsum(dh0)
        dx_ref[...] = _ln_bwd(dh0, xhat, rstd, g_ref[...])

    vec = _hbm_shape((1, D_MODEL), F32)
    c = _const2((1, D_MODEL))
    return pl.pallas_call(
        body, name="inproj_bwd", grid=(s_len // tm,),
        in_specs=[_rows(tm, ATTN_W), _rows(tm, 2 * KV_W), _rows(tm, 2 * SGU_W), _rows(tm, D_MODEL), _rows(tm, D_MODEL),
                  c, c, _vmem()],
        out_specs=[_rows(tm, D_MODEL), _vmem(), c, c],
        out_shape=[_hbm_shape((s_len, D_MODEL), F32), jax.ShapeDtypeStruct((IN_W, D_MODEL), F32), vec, vec],
        compiler_params=_params(48),
    )(dq, dkv, dsuv, dr1, x, g0, b0, w_in)


def _place():
    x, y, c = (lax.axis_index(a) for a in MESH_AXES)
    chips = [(1 - x, y), (x, 1 - y), (1 - x, 1 - y)]
    return x, y, c, chips


class _Gather:
    def __init__(self, ins, outs, send_sems, recv_sems, spans=None):
        self.ins, self.outs, self.send_sems, self.recv_sems = ins, outs, send_sems, recv_sems
        self.n = len(ins)
        self.spans = spans or [(0, r.shape[0]) for r in ins]
        self.halves = [(hi - lo) // 2 for lo, hi in self.spans]

    def _copy(self, k, t, slot, half, to):
        rows = pl.ds(pl.multiple_of(self.spans[t][0] + half * self.halves[t], 16), self.halves[t])
        piece = self.outs[t].at[slot, rows, :]
        return pltpu.make_async_remote_copy(src_ref=piece, dst_ref=piece, send_sem=self.send_sems.at[k],
                                            recv_sem=self.recv_sems.at[k], device_id=to, device_id_type=MESH)

    def _chip_copy(self, t, d, slot):
        x, y, c, chips = _place()
        return self._copy(3 * t + d, t, slot, c, (chips[d][0], chips[d][1], c))

    def _pass_copy(self, t, d, half):
        x, y, c, chips = _place()
        return self._copy(3 * self.n + 3 * t + d, t, 2 * chips[d][0] + chips[d][1], half, (x, y, 1 - c))

    def start(self):
        x, y, c, chips = _place()
        me = 2 * x + y
        for t in range(self.n):
            lo, hi = self.spans[t]
            self.outs[t][me, lo:hi, :] = self.ins[t][lo:hi, :].astype(_WIRE)
        for t in range(self.n):
            for d in range(3):
                self._chip_copy(t, d, me).start()

    def pass_on(self):
        x, y, c, chips = _place()
        for t in range(self.n):
            for d in range(3):
                self._chip_copy(t, d, 2 * chips[d][0] + chips[d][1]).wait_recv()
                self._pass_copy(t, d, c).start()

    def finish(self):
        x, y, c, chips = _place()
        me = 2 * x + y
        for t in range(self.n):
            for d in range(3):
                self._pass_copy(t, d, 1 - c).wait_recv()
        for t in range(self.n):
            for d in range(3):
                self._chip_copy(t, d, me).wait_send()
                self._pass_copy(t, d, c).wait_send()

    @staticmethod
    def out_shapes(shards, make=jax.ShapeDtypeStruct):
        return [make((N_CHIP,) + s.shape, _WIRE) for s in shards]

    @staticmethod
    def sems(n):
        return [pltpu.SemaphoreType.DMA((6 * n,)), pltpu.SemaphoreType.DMA((6 * n,))]


class _GatherPlan:
    def __init__(self, pieces):
        self.shards = [p[0] for p in pieces]
        self.spans = [p[1] for p in pieces]
        self.earlier = [p[2] for p in pieces]
        self.n = len(pieces)
        self.carried = [t for t in range(self.n) if self.earlier[t] is not None]

    def operands(self):
        return self.shards + [self.earlier[t] for t in self.carried]

    def in_specs(self):
        return [_vmem()] * self.n + [_hbm()] * len(self.carried)

    def out_specs(self):
        return [_hbm()] * self.n

    def out_shapes(self):
        return _Gather.out_shapes(self.shards, _hbm_shape)

    def scratch(self):
        return ([pltpu.VMEM((N_CHIP,) + s.shape, _WIRE) for s in self.shards] + _Gather.sems(self.n)
                + [pltpu.SemaphoreType.DMA((self.n,)), pltpu.SemaphoreType.DMA((max(len(self.carried), 1),))])

    def bind(self, in_refs, out_refs, scratch_refs):
        plan = self
        shard_refs, earlier_refs = in_refs[:self.n], in_refs[self.n:]
        bufs = scratch_refs[:self.n]
        send_sems, recv_sems, flush_sems, carry_sems = scratch_refs[self.n:self.n + 4]
        gather = _Gather(shard_refs, bufs, send_sems, recv_sems, self.spans)

        def carry_copy(k):
            t = plan.carried[k]
            lo = plan.spans[t][0]
            return pltpu.make_async_copy(earlier_refs[k].at[:, 0:lo, :], bufs[t].at[:, 0:lo, :], carry_sems.at[k])

        class Bound:
            @staticmethod
            def start():
                for k in range(len(plan.carried)):
                    carry_copy(k).start()
                gather.start()

            @staticmethod
            def pass_on():
                gather.pass_on()

            @staticmethod
            def finish():
                gather.finish()
                for k in range(len(plan.carried)):
                    carry_copy(k).wait()
                _flush([bufs[t].at[:, 0:plan.spans[t][1], :] for t in range(plan.n)],
                       [out_refs[t].at[:, 0:plan.spans[t][1], :] for t in range(plan.n)], flush_sems)

        return Bound


def _flush(bufs, hbm_outs, sems):
    copies = [pltpu.make_async_copy(b, o, sems.at[k]) for k, (b, o) in enumerate(zip(bufs, hbm_outs))]
    for cp in copies:
        cp.start()
    for cp in copies:
        cp.wait()


def _gather_weights(shards):
    n = len(shards)

    def body(*refs):
        gather = _Gather(refs[:n], refs[n:2 * n], refs[2 * n], refs[2 * n + 1])
        gather.start()
        gather.pass_on()
        gather.finish()

    return pl.pallas_call(
        body, name="gather_weights",
        in_specs=[_vmem()] * n, out_specs=[_vmem()] * n,
        out_shape=_Gather.out_shapes(shards), scratch_shapes=_Gather.sems(n),
        compiler_params=pltpu.CompilerParams(vmem_limit_bytes=32 * MIB),
    )(*shards)


class _ChipExchange:
    def __init__(self, wire_ref, land_ref, send_sems, recv_sems):
        self.wire, self.land, self.send_sems, self.recv_sems = wire_ref, land_ref, send_sems, recv_sems

    def _copy(self, d):
        x, y, c, chips = _place()
        return pltpu.make_async_remote_copy(
            src_ref=self.wire.at[2 * chips[d][0] + chips[d][1]], dst_ref=self.land.at[d],
            send_sem=self.send_sems.at[d], recv_sem=self.recv_sems.at[d],
            device_id=(chips[d][0], chips[d][1], c), device_id_type=MESH)

    def start(self):
        for d in range(3):
            self._copy(d).start()

    def wait_recv(self):
        for d in range(3):
            self._copy(d).wait_recv()

    def wait_send(self):
        for d in range(3):
            self._copy(d).wait_send()

    def finish_to(self, hbm_out, flush_sem):
        self.wait_recv()
        _flush([self.land], [hbm_out], flush_sem)
        self.wait_send()

    @staticmethod
    def land_shape(wire):
        return _hbm_shape((3,) + wire.shape[1:], wire.dtype)

    @staticmethod
    def sems():
        return [pltpu.SemaphoreType.DMA((3,)), pltpu.SemaphoreType.DMA((3,))]

    @staticmethod
    def scratch(wire):
        return ([pltpu.VMEM((3,) + wire.shape[1:], wire.dtype)] + _ChipExchange.sems() + [pltpu.SemaphoreType.DMA((1,))])


def _pair_out_shapes(half_shape):
    return [jax.ShapeDtypeStruct(half_shape, _WIRE), jax.ShapeDtypeStruct(half_shape[1:], F32)]


def _pair_scratch(acc_shape, half_shape):
    return [pltpu.VMEM(acc_shape, F32), pltpu.VMEM(half_shape, _WIRE),
            pltpu.SemaphoreType.DMA((N_CHIP,)), pltpu.SemaphoreType.DMA((N_CHIP,))]


def _pair_reduce(acc_ref, wire_ref, own_ref, land_ref, send_sems, recv_sems):
    rh = land_ref.shape[1]
    x, y, c, _ = _place()
    me = 2 * x + y
    copies = []
    for j in range(N_CHIP):
        def cast(r, carry, j=j):
            dst = pl.ds(pl.multiple_of(r * ROW_CHUNK, ROW_CHUNK), ROW_CHUNK)
            src = pl.ds(pl.multiple_of((2 * j + 1 - c) * rh + r * ROW_CHUNK, 8), ROW_CHUNK)
            wire_ref[j, dst, :] = acc_ref[src, :].astype(_WIRE)
            return carry

        lax.fori_loop(0, rh // ROW_CHUNK, cast, 0)
        cp = pltpu.make_async_remote_copy(src_ref=wire_ref.at[j], dst_ref=land_ref.at[j], send_sem=send_sems.at[j],
                                          recv_sem=recv_sems.at[j], device_id=(x, y, 1 - c), device_id_type=MESH)
        cp.start()
        copies.append(cp)
    for j in range(N_CHIP):
        copies[j].wait()

        def chunk(r, carry, j=j):
            theirs = pl.ds(pl.multiple_of(r * ROW_CHUNK, ROW_CHUNK), ROW_CHUNK)
            mine = pl.ds(pl.multiple_of((2 * j + c) * rh + r * ROW_CHUNK, 8), ROW_CHUNK)
            wire_ref[j, theirs, :] = (acc_ref[mine, :] + land_ref[j, theirs, :].astype(F32)).astype(_WIRE)
            return carry

        lax.fori_loop(0, rh // ROW_CHUNK, chunk, 0)

    def own_chunk(r, carry):
        theirs = pl.ds(pl.multiple_of(r * ROW_CHUNK, ROW_CHUNK), ROW_CHUNK)
        mine = pl.ds(pl.multiple_of((2 * me + c) * rh + r * ROW_CHUNK, 8), ROW_CHUNK)
        own_ref[theirs, :] = acc_ref[mine, :] + land_ref[me, theirs, :].astype(F32)
        return carry

    lax.fori_loop(0, rh // ROW_CHUNK, own_chunk, 0)


def _grad_finish(last_acc, lands, owns, small):
    n = len(owns) + 1
    halves = [last_acc.shape[0] // (2 * N_CHIP)] + [w.shape[1] for w in lands]
    widths = [last_acc.shape[1]] + [a.shape[1] for a in owns]
    small_body, small_scratch = _small_allreduce_parts()
    ns = len(small)

    def body(*refs):
        acc0, land, own = refs[0], (None,) + refs[1:n], (None,) + refs[n:2 * n - 1]
        refs = refs[2 * n - 1:]
        small_in, g, small_out = refs[:ns], refs[ns:ns + n], refs[ns + n:ns + n + 2]
        refs = refs[ns + n + 2:]
        pland0, wire0, land0, own0 = refs[0:4]
        p_send, p_recv, x_send, x_recv, pair_send, pair_recv = refs[4:10]
        small_refs = refs[10:]
        land = (land0,) + land[1:]
        own = (own0,) + own[1:]
        x, y, c, chips = _place()
        me = 2 * x + y
        exchange = _ChipExchange(wire0, land0, x_send, x_recv)

        def half_rows(t, half):
            return pl.ds(pl.multiple_of(half * halves[t], 8), halves[t])

        def pair_copy(t, half):
            rows = g[t].at[half_rows(t, half), :]
            return pltpu.make_async_remote_copy(src_ref=rows, dst_ref=rows, send_sem=pair_send.at[t],
                                                recv_sem=pair_recv.at[t], device_id=(x, y, 1 - c), device_id_type=MESH)

        small_rounds = small_body(*small_in, *small_out, *small_refs)
        next(small_rounds)
        _pair_reduce(acc0, wire0, own0, pland0, p_send, p_recv)
        next(small_rounds)
        exchange.start()

        for t in list(range(1, n)) + [0]:
            if t == 0:
                exchange.wait_recv()
            if t == min(2, n - 1):
                next(small_rounds)
            if t == min(4, n - 1):
                next(small_rounds, None)

            def chunk(r, carry, t=t):
                src = pl.ds(pl.multiple_of(r * ROW_CHUNK, ROW_CHUNK), ROW_CHUNK)
                dst = pl.ds(pl.multiple_of(c * halves[t] + r * ROW_CHUNK, 8), ROW_CHUNK)
                s = own[t][src, :]
                for d in range(3):
                    s = s + land[t][d, src, :].astype(F32)
                g[t][dst, :] = s
                return carry

            lax.fori_loop(0, halves[t] // ROW_CHUNK, chunk, 0)
            pair_copy(t, c).start()
        for t in range(n):
            pair_copy(t, 1 - c).wait_recv()
        for t in range(n):
            pair_copy(t, c).wait_send()
        exchange.wait_send()

    half0 = (halves[0], widths[0])
    return pl.pallas_call(
        body, name="grad_finish",
        in_specs=[_vmem()] * (2 * n - 1 + ns), out_specs=[_vmem()] * (n + 2),
        out_shape=[jax.ShapeDtypeStruct((2 * h, w), F32) for h, w in zip(halves, widths)]
        + [jax.ShapeDtypeStruct(s, F32) for s in _SMALL_OUT_DIMS],
        scratch_shapes=[pltpu.VMEM((N_CHIP,) + half0, _WIRE), pltpu.VMEM((N_CHIP,) + half0, _WIRE),
                        pltpu.VMEM((3,) + half0, _WIRE), pltpu.VMEM(half0, F32)]
        + [pltpu.SemaphoreType.DMA((N_CHIP,)), pltpu.SemaphoreType.DMA((N_CHIP,))]
        + _ChipExchange.sems()
        + [pltpu.SemaphoreType.DMA((n,)), pltpu.SemaphoreType.DMA((n,))]
        + small_scratch,
        compiler_params=pltpu.CompilerParams(vmem_limit_bytes=56 * MIB),
    )(last_acc, *lands, *owns, *small)


_SMALL = ("ln_in_g", "ln_in_b", "b_in", "attn_sinks", "sgu_ln_g", "sgu_ln_b", "sgu_w", "sgu_b", "b_out",
          "ln_mix_g", "ln_mix_b", "ln_ffn_g", "ln_ffn_b")
_VEC_ROW = dict(ln_in_g=0, ln_in_b=1, b_in=2, attn_sinks=4, sgu_ln_g=5, sgu_ln_b=6, b_out=7, ln_mix_g=8, ln_mix_b=9,
                ln_ffn_g=10, ln_ffn_b=11)
_LOSS_ROW = 12
_VEC_ROWS = 16
_MAT_ROWS = N_GRP * BLK + BLK


_SMALL_IN = ("ln_in_g", "ln_in_b", "bq", "bkv", "bsuv", "sink", "sgu_ln_g", "sgu_ln_b", "sgu_w", "sgu_bt", "b_out",
             "ln_mix_g", "ln_mix_b", "ln_ffn_g", "ln_ffn_b", "loss")
_SMALL_OUT_DIMS = ((_VEC_ROWS, D_MODEL), (_MAT_ROWS, 128))


def _small_allreduce_parts():
    n_in = len(_SMALL_IN)

    def body(*refs):
        (g_ln_in_g, g_ln_in_b, g_bq, g_bkv, g_bsuv, g_sink, g_sln_g, g_sln_b, g_sw, g_sbt, g_bout,
         g_lmg, g_lmb, g_lfg, g_lfb, g_loss) = refs[:n_in]
        out_a, out_b = refs[n_in:n_in + 2]
        (buf_a, buf_b, pair_a, pair_b, stage_a, stage_b, tot_a, tot_b,
         p1_send, p1_recv, x_send, x_recv, p2_send, p2_recv) = refs[n_in + 2:]
        x, y, c, chips = _place()
        me = 2 * x + y
        sibling = (x, y, 1 - c)
        half_a, half_b = _VEC_ROWS // 2, _MAT_ROWS // 2

        buf_a[...] = jnp.zeros_like(buf_a)
        for row, ref in ((0, g_ln_in_g), (1, g_ln_in_b), (7, g_bout), (8, g_lmg), (9, g_lmb), (10, g_lfg), (11, g_lfb),
                         (_LOSS_ROW, g_loss)):
            buf_a[row:row + 1, :] = ref[...]
        buf_a[2:3, 0:ATTN_W] = g_bq[...]
        buf_a[2:3, ATTN_W:ATTN_W + 2 * KV_W] = g_bkv[...]
        buf_a[2:3, ATTN_W + 2 * KV_W:D_MODEL] = g_bsuv[:, 0:2 * KV_W]
        buf_a[3:4, 0:2 * SGU_W - 2 * KV_W] = g_bsuv[:, 2 * KV_W:2 * SGU_W]
        buf_a[4:5, 0:128] = g_sink[...]
        buf_a[5:6, 0:SGU_W] = g_sln_g[...]
        buf_a[6:7, 0:SGU_W] = g_sln_b[...]
        for h in range(N_GRP):
            buf_b[h * BLK:(h + 1) * BLK, :] = g_sw[h]
        buf_b[N_GRP * BLK:_MAT_ROWS, :] = g_sbt[...]

        def remote(src, dst, send_sem, recv_sem, to):
            return pltpu.make_async_remote_copy(src_ref=src, dst_ref=dst, send_sem=send_sem, recv_sem=recv_sem,
                                                device_id=to, device_id_type=MESH)

        first = [remote(buf_a, pair_a, p1_send.at[0], p1_recv.at[0], sibling),
                 remote(buf_b, pair_b, p1_send.at[1], p1_recv.at[1], sibling)]
        for cp in first:
            cp.start()
        yield
        for cp in first:
            cp.wait()
        rows_a = pl.ds(pl.multiple_of(c * half_a, 8), half_a)
        rows_b = pl.ds(pl.multiple_of(c * half_b, 8), half_b)
        stage_a[me] = buf_a[rows_a, :] + pair_a[rows_a, :]
        stage_b[me] = buf_b[rows_b, :] + pair_b[rows_b, :]

        def chip_copies(d):
            to = (chips[d][0], chips[d][1], c)
            return [remote(stage_a.at[me], stage_a.at[me], x_send.at[2 * d], x_recv.at[2 * d], to),
                    remote(stage_b.at[me], stage_b.at[me], x_send.at[2 * d + 1], x_recv.at[2 * d + 1], to)]

        def chip_arrivals(d):
            slot = 2 * chips[d][0] + chips[d][1]
            to = (chips[d][0], chips[d][1], c)
            return [remote(stage_a.at[slot], stage_a.at[slot], x_send.at[2 * d], x_recv.at[2 * d], to),
                    remote(stage_b.at[slot], stage_b.at[slot], x_send.at[2 * d + 1], x_recv.at[2 * d + 1], to)]

        for d in range(3):
            for cp in chip_copies(d):
                cp.start()
        yield
        for d in range(3):
            for cp in chip_arrivals(d):
                cp.wait_recv()
        tot_a[rows_a, :] = ((stage_a[0] + stage_a[1]) + stage_a[2]) + stage_a[3]
        tot_b[rows_b, :] = ((stage_b[0] + stage_b[1]) + stage_b[2]) + stage_b[3]

        second = [remote(tot_a.at[rows_a, :], tot_a.at[rows_a, :], p2_send.at[0], p2_recv.at[0], sibling),
                  remote(tot_b.at[rows_b, :], tot_b.at[rows_b, :], p2_send.at[1], p2_recv.at[1], sibling)]
        for cp in second:
            cp.start()
        yield
        other_a = pl.ds(pl.multiple_of((1 - c) * half_a, 8), half_a)
        other_b = pl.ds(pl.multiple_of((1 - c) * half_b, 8), half_b)
        remote(tot_a.at[other_a, :], tot_a.at[other_a, :], p2_send.at[0], p2_recv.at[0], sibling).wait_recv()
        remote(tot_b.at[other_b, :], tot_b.at[other_b, :], p2_send.at[1], p2_recv.at[1], sibling).wait_recv()
        for cp in second:
            cp.wait_send()
        for d in range(3):
            for cp in chip_copies(d):
                cp.wait_send()
        out_a[...] = tot_a[...]
        out_b[...] = tot_b[...]

    vec = pltpu.VMEM((_VEC_ROWS, D_MODEL), F32)
    mat = pltpu.VMEM((_MAT_ROWS, 128), F32)
    scratch = [vec, mat, vec, mat, pltpu.VMEM((N_CHIP, _VEC_ROWS // 2, D_MODEL), F32),
               pltpu.VMEM((N_CHIP, _MAT_ROWS // 2, 128), F32), vec, mat,
               pltpu.SemaphoreType.DMA((2,)), pltpu.SemaphoreType.DMA((2,)), pltpu.SemaphoreType.DMA((6,)),
               pltpu.SemaphoreType.DMA((6,)), pltpu.SemaphoreType.DMA((2,)), pltpu.SemaphoreType.DMA((2,))]
    return body, scratch


def _small_adamw(tot_a, tot_b, params):
    shapes = [params[nm][0].shape for nm in _SMALL]

    def body(*refs):
        ta, tb = refs[:2]
        prm = refs[2:2 + 3 * len(_SMALL)]
        outs = refs[2 + 3 * len(_SMALL):]

        def grad_of(k, name):
            if name == "sgu_w":
                return [tb[h * BLK:(h + 1) * BLK, :] for h in range(N_GRP)]
            if name == "sgu_b":
                return jnp.transpose(tb[N_GRP * BLK:_MAT_ROWS, :])[0:N_GRP, :]
            row = _VEC_ROW[name]
            if name == "b_in":
                return jnp.concatenate([ta[row:row + 1, :], ta[row + 1:row + 2, 0:IN_W - D_MODEL]], axis=1)
            return ta[row:row + 1, 0:shapes[k][-1]]

        for k, name in enumerate(_SMALL):
            w_ref, m_ref, v_ref = prm[3 * k:3 * k + 3]
            g_out, d_out, m_out, v_out = outs[4 * k:4 * k + 4]
            g = grad_of(k, name)
            if name == "sgu_w":
                for h in range(N_GRP):
                    d_, m_, v_ = _adamw_math(w_ref[h], g[h], m_ref[h], v_ref[h])
                    g_out[h], d_out[h], m_out[h], v_out[h] = g[h], d_, m_, v_
            else:
                d_, m_, v_ = _adamw_math(w_ref[...], g, m_ref[...], v_ref[...])
                g_out[...], d_out[...], m_out[...], v_out[...] = g, d_, m_, v_
        outs[-1][...] = ta[_LOSS_ROW:_LOSS_ROW + 1, :]

    ins = [tot_a, tot_b] + [_in_hbm(a) for nm in _SMALL for a in params[nm]]
    out_dims = [s for s in shapes for _ in range(4)] + [(1, D_MODEL)]
    res = pl.pallas_call(
        body, name="small_adamw", grid=(1,),
        in_specs=[_const2(a.shape) for a in ins], out_specs=[_const2(s) for s in out_dims],
        out_shape=[_hbm_shape(s, F32) for s in out_dims],
        compiler_params=_params(32),
    )(*ins)
    return {nm: tuple(res[4 * k:4 * k + 4]) for k, nm in enumerate(_SMALL)}, res[-1]


def _elementwise(name, fn, ins, out_dtypes, tile_rows=256):
    shape = ins[0].shape
    lead = shape[:-2]
    rows, cols = shape[-2:]
    tr = _tile(rows, tile_rows)
    n_lead = math.prod(lead)
    nr = rows // tr
    flat = [_in_hbm(a.reshape((n_lead, rows, cols))) for a in ins]

    def body(*refs):
        outs = fn(*[r[0] for r in refs[:len(ins)]])
        for o_ref, o in zip(refs[len(ins):], outs):
            o_ref[0] = o.astype(o_ref.dtype)

    spec = pl.BlockSpec((1, tr, cols), lambda i: (i // nr, i % nr, 0))
    res = pl.pallas_call(
        body, name=name, grid=(n_lead * nr,),
        in_specs=[spec] * len(ins), out_specs=[spec] * len(out_dtypes),
        out_shape=[_hbm_shape((n_lead, rows, cols), dt) for dt in out_dtypes],
        compiler_params=_params(48),
    )(*flat)
    return [r.reshape(shape) for r in res]


def _adamw_math(w, g, m, v):
    m = ADAM_B1 * m + (1.0 - ADAM_B1) * g
    v = ADAM_B2 * v + (1.0 - ADAM_B2) * (g * g)
    m_hat = m / (1.0 - ADAM_B1 ** ADAM_STEP)
    v_hat = v / (1.0 - ADAM_B2 ** ADAM_STEP)
    delta = -ADAM_LR * (m_hat / (jnp.sqrt(v_hat) + ADAM_EPS) + ADAM_WD * w)
    return delta, m, v


def _adamw(name, groups, tile_rows=256):
    k = len(groups)

    def fn(*blocks):
        outs = []
        for i in range(k):
            w_, g_, m_, v_ = blocks[4 * i:4 * i + 4]
            outs += [g_, *_adamw_math(w_, g_, m_, v_)]
        return outs

    res = _elementwise(name, fn, [a for grp in groups for a in grp], [F32] * (4 * k), tile_rows)
    return [res[4 * i:4 * i + 4] for i in range(k)]


def kernel(x, positions, ln_in_g, ln_in_b, w_in, b_in, attn_sinks, sgu_ln_g, sgu_ln_b, sgu_w, sgu_b, w_out, b_out, ln_mix_g, ln_mix_b, w_gate, w_up, w_down, ln_ffn_g, ln_ffn_b, loss_target, m_ln_in_g, m_ln_in_b, m_w_in, m_b_in, m_attn_sinks, m_sgu_ln_g, m_sgu_ln_b, m_sgu_w, m_sgu_b, m_w_out, m_b_out, m_ln_mix_g, m_ln_mix_b, m_w_gate, m_w_up, m_w_down, m_ln_ffn_g, m_ln_ffn_b, v_ln_in_g, v_ln_in_b, v_w_in, v_b_in, v_attn_sinks, v_sgu_ln_g, v_sgu_ln_b, v_sgu_w, v_sgu_b, v_w_out, v_b_out, v_ln_mix_g, v_ln_mix_b, v_w_gate, v_w_up, v_w_down, v_ln_ffn_g, v_ln_ffn_b):
    weights = dict(ln_in_g=ln_in_g, ln_in_b=ln_in_b, w_in=w_in, b_in=b_in, attn_sinks=attn_sinks, sgu_ln_g=sgu_ln_g,
                   sgu_ln_b=sgu_ln_b, sgu_w=sgu_w, sgu_b=sgu_b, w_out=w_out, b_out=b_out, ln_mix_g=ln_mix_g,
                   ln_mix_b=ln_mix_b, w_gate=w_gate, w_up=w_up, w_down=w_down, ln_ffn_g=ln_ffn_g, ln_ffn_b=ln_ffn_b)
    mom_m = dict(ln_in_g=m_ln_in_g, ln_in_b=m_ln_in_b, w_in=m_w_in, b_in=m_b_in, attn_sinks=m_attn_sinks,
                 sgu_ln_g=m_sgu_ln_g, sgu_ln_b=m_sgu_ln_b, sgu_w=m_sgu_w, sgu_b=m_sgu_b, w_out=m_w_out, b_out=m_b_out,
                 ln_mix_g=m_ln_mix_g, ln_mix_b=m_ln_mix_b, w_gate=m_w_gate, w_up=m_w_up, w_down=m_w_down,
                 ln_ffn_g=m_ln_ffn_g, ln_ffn_b=m_ln_ffn_b)
    mom_v = dict(ln_in_g=v_ln_in_g, ln_in_b=v_ln_in_b, w_in=v_w_in, b_in=v_b_in, attn_sinks=v_attn_sinks,
                 sgu_ln_g=v_sgu_ln_g, sgu_ln_b=v_sgu_ln_b, sgu_w=v_sgu_w, sgu_b=v_sgu_b, w_out=v_w_out, b_out=v_b_out,
                 ln_mix_g=v_ln_mix_g, ln_mix_b=v_ln_mix_b, w_gate=v_w_gate, w_up=v_w_up, w_down=v_w_down,
                 ln_ffn_g=v_ln_ffn_g, ln_ffn_b=v_ln_ffn_b)
    order = list(weights)
    big = ("w_in", "w_out", "w_gate", "w_up", "w_down")

    s_len = x.shape[1]
    xs = _in_hbm(x.reshape(s_len, D_MODEL))
    tgt = _in_hbm(loss_target.reshape(s_len, D_MODEL))
    pos_col = _in_hbm(positions.reshape(s_len, 1))
    g0, b0 = _in_hbm(ln_in_g.reshape(1, D_MODEL)), _in_hbm(ln_in_b.reshape(1, D_MODEL))
    sinks = attn_sinks.reshape(N_Q)
    sgu_w3 = _in_hbm(sgu_w.reshape(N_GRP, BLK, BLK))
    sgu_bt = _in_hbm(sgu_b.reshape(N_GRP, BLK).T)
    b_in, b_out, sgu_ln_g, sgu_ln_b, ln_mix_g, ln_mix_b, ln_ffn_g, ln_ffn_b = (
        _in_hbm(a) for a in (b_in, b_out, sgu_ln_g, sgu_ln_b, ln_mix_g, ln_mix_b, ln_ffn_g, ln_ffn_b))

    col_sharded = ("w_in", "w_gate", "w_up")

    def rowmajor(name, a):
        return jnp.swapaxes(a[0], 0, 1) if name in col_sharded else a[0]

    def as_given(name, a):
        return (jnp.swapaxes(a, 0, 1) if name in col_sharded else a)[None]

    shards = [rowmajor(n, weights[n]) for n in big]
    (gw_in,) = _gather_weights(shards[0:1])
    w_in_full = gw_in.reshape(IN_W, D_MODEL)

    sh_out, sh_gate, sh_up, sh_down = shards[1:]
    *acts, gw_out, gw_gate0 = _ln_inproj(xs, pos_col, g0, b0, w_in_full, b_in, _GatherPlan(
        [(sh_out, (0, OUT_SH), None), (sh_gate, (0, GATE_CUT), None)]))
    q, k, v, su, sv, tc, t1, t2 = (_in_hbm(a) for a in acts)
    mc, gw_gate, gw_up0 = _mixer_fwd(q, k, v, su, sv, sinks, sgu_ln_g, sgu_ln_b, sgu_w3, sgu_bt, _GatherPlan(
        [(sh_gate, (GATE_CUT, FF_SH), gw_gate0), (sh_up, (0, UP_CUT), None)]))
    mc = _in_hbm(mc)
    w_out_full = gw_out.reshape(D_MODEL, D_MODEL)
    r1, gw_up = _outproj(mc, w_out_full, b_out, xs, g0, b0, _GatherPlan([(sh_up, (UP_CUT, FF_SH), gw_up0)]))
    r1 = _in_hbm(r1)
    act, p_act, q_act, gw_down = _ffn_up(r1, ln_mix_g, ln_mix_b, gw_gate, gw_up,
                                         _GatherPlan([(sh_down, (0, FF_SH), None)]))
    act, p_act, q_act = _in_hbm(act), _in_hbm(p_act), _in_hbm(q_act)
    dr2, loss_cols, d_ln_ffn_g, d_ln_ffn_b = _ffn_down_loss(act, gw_down, r1, ln_mix_g, ln_mix_b, ln_ffn_g, ln_ffn_b, tgt)
    dr2 = _in_hbm(dr2)

    dg, du, wire_down, own_down = _ffn_bwd_a(dr2, act, p_act, q_act, gw_down)
    dh1a, wire_gate, own_gate, land_down = _ffn_bwd_g(dr2, _in_hbm(dg), r1, ln_mix_g, ln_mix_b, gw_gate, wire_down)
    dr1, wire_up, own_up, d_ln_mix_g, d_ln_mix_b, land_gate = _ffn_bwd_u(_in_hbm(dh1a), _in_hbm(du), r1, ln_mix_g,
                                                                         ln_mix_b, gw_up, wire_gate)
    dr1 = _in_hbm(dr1)
    dmc, wire_out, own_out, d_b_out = _outproj_bwd(dr1, mc, w_out_full)
    (dq, dkv, dsuv, dbq, dbkv, dbsuv, d_sink, d_sgu_ln_g, d_sgu_ln_b, d_sgu_w, d_sgu_bt, land_up, land_out) = _mixer_bwd(
        q, k, v, su, sv, _in_hbm(dmc), tc, t1, t2, sinks, sgu_ln_g, sgu_ln_b, sgu_w3, sgu_bt, [wire_up, wire_out])
    dkv = dkv[BLK:BLK + s_len]
    grad_x, acc_in, d_ln_in_g, d_ln_in_b = _inproj_bwd(_in_hbm(dq), _in_hbm(dkv), _in_hbm(dsuv), dr1, xs, g0, b0,
                                                       w_in_full)

    small_local = dict(
        ln_in_g=d_ln_in_g, ln_in_b=d_ln_in_b, bq=dbq, bkv=dbkv, bsuv=dbsuv, sink=d_sink, sgu_ln_g=d_sgu_ln_g,
        sgu_ln_b=d_sgu_ln_b, sgu_w=d_sgu_w, sgu_bt=d_sgu_bt, b_out=d_b_out, ln_mix_g=d_ln_mix_g, ln_mix_b=d_ln_mix_b,
        ln_ffn_g=d_ln_ffn_g, ln_ffn_b=d_ln_ffn_b, loss=loss_cols)
    *reduced, tot_a, tot_b = _grad_finish(acc_in, [land_out, land_gate, land_up, land_down],
                                          [own_out, own_gate, own_up, own_down], [small_local[nm] for nm in _SMALL_IN])
    small_shape = dict(ln_in_g=(1, D_MODEL), ln_in_b=(1, D_MODEL), sgu_w=(N_GRP, BLK, BLK), sgu_b=(N_GRP, BLK))
    small_params = {nm: tuple(src[nm].reshape(small_shape.get(nm, src[nm].shape)) for src in (weights, mom_m, mom_v))
                    for nm in _SMALL}
    small_out, loss_sum = _small_adamw(_in_hbm(tot_a), _in_hbm(tot_b), small_params)
    loss = jnp.sum(loss_sum) * (0.5 / D_MODEL)
    grads, delta, new_m, new_v = {}, {}, {}, {}
    for nm in _SMALL:
        grads[nm], delta[nm], new_m[nm], new_v[nm] = (a.reshape(weights[nm].shape) for a in small_out[nm])

    def update(call_name, names):
        groups = [(shards[big.index(nm)], reduced[big.index(nm)], rowmajor(nm, mom_m[nm]), rowmajor(nm, mom_v[nm]))
                  for nm in names]
        for nm, res in zip(names, _adamw(call_name, groups)):
            grads[nm], delta[nm], new_m[nm], new_v[nm] = (as_given(nm, a) for a in res)

    update("adamw_w_in", ["w_in"])
    update("adamw_w_out", ["w_out"])
    update("adamw_ffn", ["w_gate", "w_up", "w_down"])

    return (loss, grad_x.reshape(x.shape), *[grads[n] for n in order], *[delta[n] for n in order],
            *[new_m[n] for n in order], *[new_v[n] for n in order])
```

```python
import jax
import jax.numpy as jnp
from jax import lax
from jax.experimental import pallas as pl
from jax.experimental.pallas import tpu as pltpu

F32 = jnp.float32
_MXU = jnp.bfloat16
_WIRE = jnp.bfloat16
_ACT = jnp.bfloat16

D_MODEL = 1024
ATTN_W = 512
SGU_W = 512
HEAD_DIM = 64
N_Q = 8
N_KV = 2
Q_PER_KV = 4
KV_W = 128
BLK = 128
ROT_DIM = 16
ROPE_THETA = 500000.0
N_GRP = 4
GRP_DIM = 128
D_FF = 2816
IN_W = 1792
LN_EPS = 1e-5
ALPHA = 2.0 ** 0.25
N_CHIP = 4
FF_SH = D_FF // N_CHIP
IN_SH = IN_W // N_CHIP
OUT_SH = D_MODEL // N_CHIP
ROW_CHUNK = 32
GATE_CUT, UP_CUT = 352, 320

ADAM_LR = 0.001
ADAM_B1 = 0.9
ADAM_B2 = 0.999
ADAM_EPS = 1e-08
ADAM_WD = 0.01
ADAM_STEP = 10

SQRT_HALF = 0.7071067811865476
INV_SQRT_2PI = 0.3989422804014327
MESH_AXES = ("x", "y", "c")
MESH = pl.DeviceIdType.MESH
MIB = 2 ** 20


def _vmem():
    return pl.BlockSpec(memory_space=pltpu.VMEM)


def _smem():
    return pl.BlockSpec(memory_space=pltpu.SMEM)


def _hbm():
    return pl.BlockSpec(memory_space=pl.ANY)


def _hbm_shape(shape, dtype):
    return pltpu.HBM(shape, dtype)


def _in_hbm(a):
    return pltpu.with_memory_space_constraint(a, pltpu.HBM)


def _params(vmem_mib=48):
    return pltpu.CompilerParams(dimension_semantics=("arbitrary",), vmem_limit_bytes=vmem_mib * MIB)


def _tile(n, cap):
    if n <= cap:
        return n
    for t in range(cap - cap % 16, 0, -16):
        if n % t == 0:
            return t
    raise ValueError((n, cap))


def _rows(tm, width):
    return pl.BlockSpec((tm, width), lambda i: (i, 0))


def _const2(shape):
    return pl.BlockSpec(shape, lambda i: (0,) * len(shape))


def _ln(x, g, b):
    mu = jnp.mean(x, axis=-1, keepdims=True)
    xc = x - mu
    var = jnp.mean(xc * xc, axis=-1, keepdims=True)
    rstd = lax.rsqrt(var + LN_EPS)
    xhat = xc * rstd
    return xhat * g + b, xhat, rstd


def _ln_bwd(dy, xhat, rstd, g):
    gdy = dy * g
    m1 = jnp.mean(gdy, axis=-1, keepdims=True)
    m2 = jnp.mean(gdy * xhat, axis=-1, keepdims=True)
    return rstd * (gdy - m1 - xhat * m2)


def _colsum(a):
    return jnp.sum(a, axis=0, keepdims=True)


def _gelu_and_grad(x):
    cdf = 0.5 * (1.0 + lax.erf(x * SQRT_HALF))
    return x * cdf, cdf + x * jnp.exp(-0.5 * x * x) * INV_SQRT_2PI


def _dot(a, b):
    return jnp.dot(a, b, preferred_element_type=F32)


def _dot_nt(a, b):
    return lax.dot_general(a, b, (((1,), (1,)), ((), ())), preferred_element_type=F32)


def _dot_tn(a, b):
    return lax.dot_general(a, b, (((0,), (0,)), ((), ())), preferred_element_type=F32)


def _rope(t, tc, t1, t2):
    n = t.shape[1]
    rep = n // 128
    if rep > 1:
        tc, t1, t2 = (jnp.tile(a, (1, rep)) for a in (tc, t1, t2))
    return t * tc + pltpu.roll(t, n - 8, 1) * t1 + pltpu.roll(t, 8, 1) * t2


def _rope_bwd(d, tc, t1, t2):
    n = d.shape[1]
    rep = n // 128
    if rep > 1:
        tc, t1, t2 = (jnp.tile(a, (1, rep)) for a in (tc, t1, t2))
    return d * tc + pltpu.roll(d * t1, 8, 1) + pltpu.roll(d * t2, n - 8, 1)


def _causal_w(w_ref, h):
    t = lax.broadcasted_iota(jnp.int32, (BLK, BLK), 0)
    s = lax.broadcasted_iota(jnp.int32, (BLK, BLK), 1)
    return jnp.where(s <= t, w_ref[h], 0.0)


def _lane_put(vals, width):
    rows = vals[0].shape[0]
    lane = lax.broadcasted_iota(jnp.int32, (rows, width), 1)
    out = jnp.zeros((rows, width), F32)
    for k, v in enumerate(vals):
        out = out + jnp.where(lane == k, v, 0.0)
    return out


def _rope_consts():
    lane = jnp.arange(128) % HEAD_DIM
    rot = lane < ROT_DIM
    pair = (2 * (lane % (ROT_DIM // 2))).astype(F32)
    freq = jnp.where(rot, ROPE_THETA ** (-pair / ROT_DIM), 0.0)
    rows = [freq, rot.astype(F32), 1.0 - rot.astype(F32), (lane < ROT_DIM // 2).astype(F32),
            jnp.logical_and(lane >= ROT_DIM // 2, rot).astype(F32)]
    rows += [jnp.zeros((128,), F32)] * 3
    return jnp.stack(rows).astype(F32)


def _ln_inproj(x, pos_col, g0, b0, w_in, b_in, plan):
    s_len = x.shape[0]
    tm = _tile(s_len, 512)
    m, n = len(plan.operands()), plan.n

    def body(x_ref, pos_ref, g_ref, b_ref, w_ref, bi_ref, rc_ref, *rest):
        q_ref, k_ref, v_ref, su_ref, sv_ref, tc_ref, t1_ref, t2_ref = rest[m:m + 8]
        gather = plan.bind(rest[:m], rest[m + 8:m + 8 + n], rest[m + 8 + n:])
        i = pl.program_id(0)

        @pl.when(i == 0)
        def _():
            gather.start()

        h0, _, _ = _ln(x_ref[...], g_ref[...], b_ref[...])
        proj = _dot_nt(h0.astype(_MXU), w_ref[...]) + bi_ref[...]
        ang = pos_ref[...].astype(F32) * rc_ref[0:1, :]
        cs = jnp.cos(ang)
        sn = jnp.sin(ang)
        tc = cs * rc_ref[1:2, :] + rc_ref[2:3, :]
        t1 = -sn * rc_ref[3:4, :]
        t2 = sn * rc_ref[4:5, :]
        tc_ref[...] = tc
        t1_ref[...] = t1
        t2_ref[...] = t2
        q = _rope(proj[:, 0:ATTN_W], tc, t1, t2) * (HEAD_DIM ** -0.5)
        q_ref[...] = q.astype(_MXU)
        k_ref[...] = _rope(proj[:, ATTN_W:ATTN_W + KV_W], tc, t1, t2).astype(_MXU)
        v_ref[...] = proj[:, ATTN_W + KV_W:ATTN_W + 2 * KV_W].astype(_MXU)
        su_ref[...] = proj[:, ATTN_W + 2 * KV_W:ATTN_W + 2 * KV_W + SGU_W]
        sv_ref[...] = proj[:, ATTN_W + 2 * KV_W + SGU_W:IN_W]

        last = pl.num_programs(0) - 1

        @pl.when(i == jnp.maximum(last - 1, 0))
        def _():
            gather.pass_on()

        @pl.when(i == last)
        def _():
            gather.finish()

    sd = _hbm_shape
    return pl.pallas_call(
        body, name="ln_inproj", grid=(s_len // tm,),
        in_specs=[_rows(tm, D_MODEL), _rows(tm, 1), _const2((1, D_MODEL)), _const2((1, D_MODEL)), _vmem(),
                  _const2((1, IN_W)), _const2((8, 128))] + plan.in_specs(),
        out_specs=[_rows(tm, ATTN_W), _rows(tm, KV_W), _rows(tm, KV_W), _rows(tm, SGU_W), _rows(tm, SGU_W),
                   _rows(tm, 128), _rows(tm, 128), _rows(tm, 128)] + plan.out_specs(),
        out_shape=[sd((s_len, ATTN_W), _MXU), sd((s_len, KV_W), _MXU), sd((s_len, KV_W), _MXU),
                   sd((s_len, SGU_W), F32), sd((s_len, SGU_W), F32),
                   sd((s_len, 128), F32), sd((s_len, 128), F32), sd((s_len, 128), F32)] + plan.out_shapes(),
        scratch_shapes=plan.scratch(),
        compiler_params=_params(56),
    )(x, pos_col, g0, b0, w_in, b_in, _rope_consts(), *plan.operands())


def _band_mask_t(first_block):
    kj = lax.broadcasted_iota(jnp.int32, (2 * BLK, BLK), 0)
    qi = lax.broadcasted_iota(jnp.int32, (2 * BLK, BLK), 1)
    shut = jnp.where(first_block, 2 * BLK, 0)
    prev_ok = jnp.logical_and(kj < BLK, kj > qi + shut)
    cur_ok = jnp.logical_and(kj >= BLK, (kj - BLK) <= qi)
    return jnp.logical_or(prev_ok, cur_ok)


def _attn_probs_t(kh, qh, sink, allowed_t):
    s = jnp.where(allowed_t, _dot_nt(kh, qh), -1e30)
    m = jnp.maximum(jnp.max(s, axis=0, keepdims=True), sink)
    p = jnp.exp(s - m)
    ps = jnp.exp(sink - m)
    inv = 1.0 / (jnp.sum(p, axis=0, keepdims=True) + ps)
    return p * inv, ps * inv


def _sgu_mix(gv, lg, lb, w_ref, bt_ref):
    vv, vhat, rstd = _ln(gv, lg, lb)
    vvb = vv.astype(_MXU)
    wcs, mixed = [], []
    for h in range(N_GRP):
        wc = _causal_w(w_ref, h).astype(_MXU)
        wcs.append(wc)
        mixed.append(_dot(wc, vvb[:, h * GRP_DIM:(h + 1) * GRP_DIM]) + bt_ref[:, h:h + 1])
    return jnp.concatenate(mixed, axis=1), vhat, rstd, vvb, wcs


def _mixer_fwd(q, k, v, su, sv, sinks, sg, sb, sgu_w, sgu_bt, plan):
    s_len = q.shape[0]
    nb = s_len // BLK
    per = 2 if nb % 2 == 0 else 1
    steps = nb // per
    m, n = len(plan.operands()), plan.n

    def body(q_ref, kc_ref, kp_ref, vc_ref, vp_ref, su_ref, sv_ref, sink_ref, lg_ref, lb_ref, w_ref, bt_ref, *rest):
        mc_ref = rest[m]
        gather = plan.bind(rest[:m], rest[m + 1:m + 1 + n], rest[m + 1 + n:])
        i = pl.program_id(0)

        @pl.when(i == 0)
        def _():
            gather.start()

        @pl.when(i == max(steps - 2, 0))
        def _():
            gather.pass_on()

        @pl.when(i == steps - 1)
        def _():
            gather.finish()

        for s in range(per):
            rows = slice(s * BLK, (s + 1) * BLK)
            before = slice((s - 1) * BLK, s * BLK)
            k_prev = kp_ref[...] if s == 0 else kc_ref[before, :]
            v_prev = vp_ref[...] if s == 0 else vc_ref[before, :]
            allowed_t = _band_mask_t(i == 0 if s == 0 else False)
            kb = jnp.concatenate([k_prev, kc_ref[rows, :]], axis=0)
            vb = jnp.concatenate([v_prev, vc_ref[rows, :]], axis=0)
            qv = q_ref[rows, :]
            outs = []
            allowed_g = jnp.tile(allowed_t, (1, Q_PER_KV))
            for g in range(N_KV):
                heads = range(g * Q_PER_KV, (g + 1) * Q_PER_KV)
                kh = kb[:, g * HEAD_DIM:(g + 1) * HEAD_DIM]
                vh = vb[:, g * HEAD_DIM:(g + 1) * HEAD_DIM]
                q_g = jnp.concatenate([qv[:, h * HEAD_DIM:(h + 1) * HEAD_DIM] for h in heads], axis=0)
                sink_g = jnp.concatenate([jnp.full((1, BLK), sink_ref[h], F32) for h in heads], axis=1)
                probs_t, _ = _attn_probs_t(kh, q_g, sink_g, allowed_g)
                o_g = _dot_tn(probs_t.astype(_MXU), vh)
                outs += [o_g[hh * BLK:(hh + 1) * BLK, :] for hh in range(Q_PER_KV)]
            u = _gelu_and_grad(su_ref[rows, :])[0]
            gv = _gelu_and_grad(sv_ref[rows, :])[0]
            mixed = _sgu_mix(gv, lg_ref[...], lb_ref[...], w_ref, bt_ref)[0]
            mc_ref[rows, :] = jnp.concatenate(outs + [u * mixed], axis=1).astype(_MXU)

    cur = lambda w: pl.BlockSpec((per * BLK, w), lambda i: (i, 0))
    prev = lambda w: pl.BlockSpec((BLK, w), lambda i: (jnp.maximum(per * i - 1, 0), 0))
    return pl.pallas_call(
        body, name="mixer_fwd", grid=(steps,),
        in_specs=[cur(ATTN_W), cur(KV_W), prev(KV_W), cur(KV_W), prev(KV_W), cur(SGU_W), cur(SGU_W), _smem(),
                  _const2((1, SGU_W)), _const2((1, SGU_W)), _const2((N_GRP, BLK, BLK)), _const2((BLK, N_GRP))]
        + plan.in_specs(),
        out_specs=[cur(D_MODEL)] + plan.out_specs(),
        out_shape=[_hbm_shape((s_len, D_MODEL), _MXU)] + plan.out_shapes(),
        scratch_shapes=plan.scratch(),
        compiler_params=_params(56),
    )(q, k, k, v, v, su, sv, sinks, sg, sb, sgu_w, sgu_bt, *plan.operands())


def _outproj(mc, w_out, b_out, x, g0, b0, plan):
    s_len = x.shape[0]
    tm = _tile(s_len, 512)
    m, n = len(plan.operands()), plan.n

    def body(mc_ref, w_ref, bo_ref, x_ref, g_ref, b_ref, *rest):
        r1_ref = rest[m]
        gather = plan.bind(rest[:m], rest[m + 1:m + 1 + n], rest[m + 1 + n:])
        i = pl.program_id(0)

        @pl.when(i == 0)
        def _():
            gather.start()

        h0, _, _ = _ln(x_ref[...], g_ref[...], b_ref[...])
        r1_ref[...] = ALPHA * h0 + (_dot(mc_ref[...], w_ref[...]) + bo_ref[...])

        last = pl.num_programs(0) - 1

        @pl.when(i == jnp.maximum(last - 1, 0))
        def _():
            gather.pass_on()

        @pl.when(i == last)
        def _():
            gather.finish()

    return pl.pallas_call(
        body, name="outproj", grid=(s_len // tm,),
        in_specs=[_rows(tm, D_MODEL), _vmem(), _const2((1, D_MODEL)), _rows(tm, D_MODEL),
                  _const2((1, D_MODEL)), _const2((1, D_MODEL))] + plan.in_specs(),
        out_specs=[_rows(tm, D_MODEL)] + plan.out_specs(),
        out_shape=[_hbm_shape((s_len, D_MODEL), F32)] + plan.out_shapes(),
        scratch_shapes=plan.scratch(),
        compiler_params=_params(40),
    )(mc, w_out, b_out, x, g0, b0, *plan.operands())


def _ffn_spec(tm):
    return pl.BlockSpec((N_CHIP, tm, FF_SH), lambda i: (0, i, 0))


def _ffn_up(r1, g1, b1, wg, wu, plan):
    s_len = r1.shape[0]
    tm = _tile(s_len, 512)
    m, n = len(plan.operands()), plan.n

    def body(r1_ref, g_ref, b_ref, wg_ref, wu_ref, *rest):
        a_ref, p_ref, q_ref = rest[m:m + 3]
        gather = plan.bind(rest[:m], rest[m + 3:m + 3 + n], rest[m + 3 + n:])
        i = pl.program_id(0)

        @pl.when(i == 0)
        def _():
            gather.start()

        h1, _, _ = _ln(r1_ref[...], g_ref[...], b_ref[...])
        h1b = h1.astype(_MXU)
        for j in range(N_CHIP):
            g = _dot_nt(h1b, wg_ref[j])
            u = _dot_nt(h1b, wu_ref[j])
            silu, sg = _silu_parts(g)
            a_ref[j] = (silu * u).astype(_MXU)
            p_ref[j] = silu.astype(_ACT)
            q_ref[j] = (u * (sg * (1.0 + g * (1.0 - sg)))).astype(_ACT)

        last = pl.num_programs(0) - 1

        @pl.when(i == jnp.maximum(last - 1, 0))
        def _():
            gather.pass_on()

        @pl.when(i == last)
        def _():
            gather.finish()

    sd = _hbm_shape((N_CHIP, s_len, FF_SH), _ACT)
    return pl.pallas_call(
        body, name="ffn_up", grid=(s_len // tm,),
        in_specs=[_rows(tm, D_MODEL), _const2((1, D_MODEL)), _const2((1, D_MODEL)), _vmem(), _vmem()] + plan.in_specs(),
        out_specs=[_ffn_spec(tm)] * 3 + plan.out_specs(),
        out_shape=[_hbm_shape((N_CHIP, s_len, FF_SH), _MXU), sd, sd] + plan.out_shapes(),
        scratch_shapes=plan.scratch(),
        compiler_params=_params(56),
    )(r1, g1, b1, wg, wu, *plan.operands())


def _silu_parts(g):
    sg = 1.0 / (1.0 + jnp.exp(-g))
    return g * sg, sg


def _ffn_down_loss(act, wd, r1, g1, b1, g2, b2, target):
    s_len = r1.shape[0]
    tm = _tile(s_len, 512)

    parts = 2 if tm % 32 == 0 else 1
    sub = tm // parts

    def body(a_ref, wd_ref, r1_ref, g1_ref, b1_ref, g2_ref, b2_ref, t_ref, dr2_ref, loss_ref, dg2_ref, db2_ref):
        i = pl.program_id(0)

        @pl.when(i == 0)
        def _():
            loss_ref[...] = jnp.zeros_like(loss_ref)
            dg2_ref[...] = jnp.zeros_like(dg2_ref)
            db2_ref[...] = jnp.zeros_like(db2_ref)

        for part in range(parts):
            rows = slice(part * sub, (part + 1) * sub)
            f = jnp.zeros((sub, D_MODEL), F32)
            for j in range(N_CHIP):
                f = f + _dot(a_ref[j, rows, :], wd_ref[j])
            h1, _, _ = _ln(r1_ref[rows, :], g1_ref[...], b1_ref[...])
            h2, r2hat, rstd2 = _ln(ALPHA * h1 + f, g2_ref[...], b2_ref[...])
            diff = h2 - t_ref[rows, :]
            dh2 = diff * (1.0 / D_MODEL)
            loss_ref[...] += _colsum(diff * diff)
            dg2_ref[...] += _colsum(dh2 * r2hat)
            db2_ref[...] += _colsum(dh2)
            dr2_ref[rows, :] = _ln_bwd(dh2, r2hat, rstd2, g2_ref[...])

    vec = _hbm_shape((1, D_MODEL), F32)
    c = _const2((1, D_MODEL))
    return pl.pallas_call(
        body, name="ffn_down_loss", grid=(s_len // tm,),
        in_specs=[_ffn_spec(tm), _vmem(), _rows(tm, D_MODEL), c, c, c, c, _rows(tm, D_MODEL)],
        out_specs=[_rows(tm, D_MODEL), c, c, c],
        out_shape=[_hbm_shape((s_len, D_MODEL), F32), vec, vec, vec],
        compiler_params=_params(48),
    )(act, wd, r1, g1, b1, g2, b2, target)


def _ffn_bwd_a(dr2, act, p_act, q_act, wd):
    s_len = dr2.shape[0]
    tm = _tile(s_len, 512)

    def body(dr2_ref, a_ref, p_ref, q_ref, wd_ref, dg_ref, du_ref, wire_ref, own_ref,
             dwd_ref, land_ref, send_sem, recv_sem):
        i = pl.program_id(0)

        @pl.when(i == 0)
        def _():
            dwd_ref[...] = jnp.zeros_like(dwd_ref)

        dfb = dr2_ref[...].astype(_MXU)
        for j in range(N_CHIP):
            da = _dot_nt(dfb, wd_ref[j])
            dg_ref[j] = (da * q_ref[j].astype(F32)).astype(_MXU)
            du_ref[j] = (da * p_ref[j].astype(F32)).astype(_MXU)
            dwd_ref[j * FF_SH:(j + 1) * FF_SH, :] += _dot_tn(a_ref[j], dfb)

        @pl.when(i == pl.num_programs(0) - 1)
        def _():
            _pair_reduce(dwd_ref, wire_ref, own_ref, land_ref, send_sem, recv_sem)

    sd = _hbm_shape((N_CHIP, s_len, FF_SH), _MXU)
    half = (N_CHIP, FF_SH // 2, D_MODEL)
    return pl.pallas_call(
        body, name="ffn_bwd_a", grid=(s_len // tm,),
        in_specs=[_rows(tm, D_MODEL), _ffn_spec(tm), _ffn_spec(tm), _ffn_spec(tm), _vmem()],
        out_specs=[_ffn_spec(tm), _ffn_spec(tm), _vmem(), _vmem()],
        out_shape=[sd, sd] + _pair_out_shapes(half),
        scratch_shapes=_pair_scratch((D_FF, D_MODEL), half),
        compiler_params=_params(61),
    )(dr2, act, p_act, q_act, wd)


def _ffn_bwd_g(dr2, dg, r1, g1, b1, wg, prev_wire):
    s_len = dr2.shape[0]
    tm = _tile(s_len, 512)

    def body(dr2_ref, dg_ref, r1_ref, g1_ref, b1_ref, wg_ref, pw_ref, dh1_ref, wire_ref, own_ref, pl_ref,
             dwg_ref, land_ref, send_sem, recv_sem, xl_ref, x_send, x_recv, x_flush):
        i = pl.program_id(0)
        exchange = _ChipExchange(pw_ref, xl_ref, x_send, x_recv)

        @pl.when(i == 0)
        def _():
            exchange.start()
            dwg_ref[...] = jnp.zeros_like(dwg_ref)

        h1, _, _ = _ln(r1_ref[...], g1_ref[...], b1_ref[...])
        h1b = h1.astype(_MXU)
        dh1 = ALPHA * dr2_ref[...]
        for j in range(N_CHIP):
            dgj = dg_ref[j]
            dh1 = dh1 + _dot(dgj, wg_ref[j])
            dwg_ref[j * FF_SH:(j + 1) * FF_SH, :] += _dot_tn(dgj, h1b)
        dh1_ref[...] = dh1

        @pl.when(i == pl.num_programs(0) - 1)
        def _():
            _pair_reduce(dwg_ref, wire_ref, own_ref, land_ref, send_sem, recv_sem)
            exchange.finish_to(pl_ref, x_flush)

    c = _const2((1, D_MODEL))
    half = (N_CHIP, FF_SH // 2, D_MODEL)
    return pl.pallas_call(
        body, name="ffn_bwd_g", grid=(s_len // tm,),
        in_specs=[_rows(tm, D_MODEL), _ffn_spec(tm), _rows(tm, D_MODEL), c, c, _vmem(), _vmem()],
        out_specs=[_rows(tm, D_MODEL), _vmem(), _vmem(), _hbm()],
        out_shape=[_hbm_shape((s_len, D_MODEL), F32)] + _pair_out_shapes(half) + [_ChipExchange.land_shape(prev_wire)],
        scratch_shapes=_pair_scratch((D_FF, D_MODEL), half) + _ChipExchange.scratch(prev_wire),
        compiler_params=_params(58),
    )(dr2, dg, r1, g1, b1, wg, prev_wire)


def _ffn_bwd_u(dh1a, du, r1, g1, b1, wu, prev_wire):
    s_len = dh1a.shape[0]
    tm = _tile(s_len, 512)

    def body(dh1_ref, du_ref, r1_ref, g1_ref, b1_ref, wu_ref, pw_ref,
             dr1_ref, wire_ref, own_ref, dg1_ref, db1_ref, pl_ref,
             dwu_ref, land_ref, send_sem, recv_sem, xl_ref, x_send, x_recv, x_flush):
        i = pl.program_id(0)
        exchange = _ChipExchange(pw_ref, xl_ref, x_send, x_recv)

        @pl.when(i == 0)
        def _():
            exchange.start()
            dwu_ref[...] = jnp.zeros_like(dwu_ref)
            dg1_ref[...] = jnp.zeros_like(dg1_ref)
            db1_ref[...] = jnp.zeros_like(db1_ref)

        h1, r1hat, rstd1 = _ln(r1_ref[...], g1_ref[...], b1_ref[...])
        h1b = h1.astype(_MXU)
        dh1 = dh1_ref[...]
        for j in range(N_CHIP):
            duj = du_ref[j]
            dh1 = dh1 + _dot(duj, wu_ref[j])
            dwu_ref[j * FF_SH:(j + 1) * FF_SH, :] += _dot_tn(duj, h1b)
        dg1_ref[...] += _colsum(dh1 * r1hat)
        db1_ref[...] += _colsum(dh1)
        dr1_ref[...] = _ln_bwd(dh1, r1hat, rstd1, g1_ref[...])

        @pl.when(i == pl.num_programs(0) - 1)
        def _():
            _pair_reduce(dwu_ref, wire_ref, own_ref, land_ref, send_sem, recv_sem)
            exchange.finish_to(pl_ref, x_flush)

    vec = _hbm_shape((1, D_MODEL), F32)
    c = _const2((1, D_MODEL))
    half = (N_CHIP, FF_SH // 2, D_MODEL)
    return pl.pallas_call(
        body, name="ffn_bwd_u", grid=(s_len // tm,),
        in_specs=[_rows(tm, D_MODEL), _ffn_spec(tm), _rows(tm, D_MODEL), c, c, _vmem(), _vmem()],
        out_specs=[_rows(tm, D_MODEL), _vmem(), _vmem(), c, c, _hbm()],
        out_shape=[_hbm_shape((s_len, D_MODEL), F32)] + _pair_out_shapes(half)
        + [vec, vec, _ChipExchange.land_shape(prev_wire)],
        scratch_shapes=_pair_scratch((D_FF, D_MODEL), half) + _ChipExchange.scratch(prev_wire),
        compiler_params=_params(58),
    )(dh1a, du, r1, g1, b1, wu, prev_wire)


def _outproj_bwd(dr1, mc, w_out):
    s_len = dr1.shape[0]
    tm = _tile(s_len, 512)

    def body(dr1_ref, mc_ref, w_ref, dmc_ref, wire_ref, own_ref, db_ref, dw_ref, land_ref, send_sem, recv_sem):
        i = pl.program_id(0)

        @pl.when(i == 0)
        def _():
            dw_ref[...] = jnp.zeros_like(dw_ref)
            db_ref[...] = jnp.zeros_like(db_ref)

        d = dr1_ref[...]
        db_ref[...] += _colsum(d)
        db16 = d.astype(_MXU)
        dmc_ref[...] = _dot_nt(db16, w_ref[...])
        dw_ref[...] += _dot_tn(mc_ref[...], db16)

        @pl.when(i == pl.num_programs(0) - 1)
        def _():
            _pair_reduce(dw_ref, wire_ref, own_ref, land_ref, send_sem, recv_sem)

    half = (N_CHIP, OUT_SH // 2, D_MODEL)
    return pl.pallas_call(
        body, name="outproj_bwd", grid=(s_len // tm,),
        in_specs=[_rows(tm, D_MODEL), _rows(tm, D_MODEL), _vmem()],
        out_specs=[_rows(tm, D_MODEL), _vmem(), _vmem(), _const2((1, D_MODEL))],
        out_shape=[_hbm_shape((s_len, D_MODEL), F32)] + _pair_out_shapes(half) + [_hbm_shape((1, D_MODEL), F32)],
        scratch_shapes=_pair_scratch((D_MODEL, D_MODEL), half),
        compiler_params=_params(48),
    )(dr1, mc, w_out)


def _mixer_bwd(q, k, v, su, sv, dmc, tc, t1, t2, sinks, sg, sb, sgu_w, sgu_bt, prev_wires):
    s_len = q.shape[0]
    nb = s_len // BLK
    per = next(p for p in (4, 2, 1) if nb % p == 0)
    steps = nb // per

    def body(q_ref, kc_ref, kp_ref, vc_ref, vp_ref, su_ref, sv_ref, dmc_ref,
             tc_ref, t1_ref, t2_ref, tcp_ref, t1p_ref, t2p_ref,
             sink_ref, lg_ref, lb_ref, w_ref, bt_ref, pw0_ref, pw1_ref,
             dq_ref, dkv_ref, dsuv_ref, dbq_ref, dbkv_ref, dbsuv_ref,
             dsink_ref, dlg_ref, dlb_ref, dw_ref, dbt_ref, pl0_ref, pl1_ref, carry_ref,
             xl0_ref, x0_send, x0_recv, x0_flush, xl1_ref, x1_send, x1_recv, x1_flush):
        i = pl.program_id(0)
        exchanges = [(_ChipExchange(pw0_ref, xl0_ref, x0_send, x0_recv), pl0_ref, x0_flush),
                     (_ChipExchange(pw1_ref, xl1_ref, x1_send, x1_recv), pl1_ref, x1_flush)]

        @pl.when(i == 0)
        def _():
            for exchange, _, _ in exchanges:
                exchange.start()

        @pl.when(i == 0)
        def _():
            for r in (dbq_ref, dbkv_ref, dbsuv_ref, dsink_ref, dlg_ref, dlb_ref, dw_ref, dbt_ref, carry_ref):
                r[...] = jnp.zeros_like(r)

        def emit_kv(fin, t):
            if t == 0:
                tables = (tcp_ref[...], t1p_ref[...], t2p_ref[...])
            else:
                before = slice((t - 1) * BLK, t * BLK)
                tables = (tc_ref[before, :], t1_ref[before, :], t2_ref[before, :])
            dk = _rope_bwd(fin[:, 0:KV_W], *tables)
            out = jnp.concatenate([dk, fin[:, KV_W:2 * KV_W]], axis=1)
            dkv_ref[t * BLK:(t + 1) * BLK, :] = out.astype(_MXU)
            dbkv_ref[...] += _colsum(out)

        def one_block(s):
            rows = slice(s * BLK, (s + 1) * BLK)
            before = slice((s - 1) * BLK, s * BLK)
            k_prev = kp_ref[...] if s == 0 else kc_ref[before, :]
            v_prev = vp_ref[...] if s == 0 else vc_ref[before, :]
            allowed_t = _band_mask_t(i == 0 if s == 0 else False)
            kb = jnp.concatenate([k_prev, kc_ref[rows, :]], axis=0)
            vb = jnp.concatenate([v_prev, vc_ref[rows, :]], axis=0)
            qv = q_ref[rows, :]
            dmc = dmc_ref[rows, :]
            dqs, dks, dvs, dsinks = [], [], [], []
            allowed_g = jnp.tile(allowed_t, (1, Q_PER_KV))
            for g in range(N_KV):
                heads = range(g * Q_PER_KV, (g + 1) * Q_PER_KV)
                kh = kb[:, g * HEAD_DIM:(g + 1) * HEAD_DIM]
                vh = vb[:, g * HEAD_DIM:(g + 1) * HEAD_DIM]
                q_g = jnp.concatenate([qv[:, h * HEAD_DIM:(h + 1) * HEAD_DIM] for h in heads], axis=0)
                do_g = jnp.concatenate([dmc[:, h * HEAD_DIM:(h + 1) * HEAD_DIM] for h in heads], axis=0).astype(_MXU)
                sink_g = jnp.concatenate([jnp.full((1, BLK), sink_ref[h], F32) for h in heads], axis=1)
                probs_t, psink = _attn_probs_t(kh, q_g, sink_g, allowed_g)
                dvs.append(_dot(probs_t.astype(_MXU), do_g))
                dp_t = _dot_nt(vh, do_g)
                rd = jnp.sum(probs_t * dp_t, axis=0, keepdims=True)
                ds_t = (probs_t * (dp_t - rd)).astype(_MXU)
                ps_rd = psink * rd
                for hh in range(Q_PER_KV):
                    dsinks.append(-jnp.sum(ps_rd[:, hh * BLK:(hh + 1) * BLK], axis=1, keepdims=True))
                dq_g = _dot_tn(ds_t, kh)
                dqs += [dq_g[hh * BLK:(hh + 1) * BLK, :] for hh in range(Q_PER_KV)]
                dks.append(_dot(ds_t, q_g))
            dq = _rope_bwd(jnp.concatenate(dqs, axis=1) * (HEAD_DIM ** -0.5),
                           tc_ref[rows, :], t1_ref[rows, :], t2_ref[rows, :])
            dq_ref[rows, :] = dq.astype(_MXU)
            dbq_ref[...] += _colsum(dq)
            dsink_ref[...] += _lane_put(dsinks, 128)
            contrib = jnp.concatenate(dks + dvs, axis=1)

            lg = lg_ref[...]
            u, du_dsu = _gelu_and_grad(su_ref[rows, :])
            gv, dgv_dsv = _gelu_and_grad(sv_ref[rows, :])
            mixed, vhat, rstd, vvb, wcs = _sgu_mix(gv, lg, lb_ref[...], w_ref, bt_ref)
            dsgu = dmc[:, ATTN_W:D_MODEL]
            dsu = dsgu * mixed * du_dsu
            dmixed = dsgu * u
            tri_t = lax.broadcasted_iota(jnp.int32, (BLK, BLK), 0)
            tri_s = lax.broadcasted_iota(jnp.int32, (BLK, BLK), 1)
            dvv, dbs = [], []
            for h in range(N_GRP):
                dm = dmixed[:, h * GRP_DIM:(h + 1) * GRP_DIM]
                dmb = dm.astype(_MXU)
                dbs.append(jnp.sum(dm, axis=1, keepdims=True))
                dw_ref[h] += jnp.where(tri_s <= tri_t, _dot_nt(dmb, vvb[:, h * GRP_DIM:(h + 1) * GRP_DIM]), 0.0)
                dvv.append(_dot_tn(wcs[h], dmb))
            dvv = jnp.concatenate(dvv, axis=1)
            dbt_ref[...] += _lane_put(dbs, 128)
            dlg_ref[...] += _colsum(dvv * vhat)
            dlb_ref[...] += _colsum(dvv)
            dsv = _ln_bwd(dvv, vhat, rstd, lg) * dgv_dsv
            dsuv = jnp.concatenate([dsu, dsv], axis=1)
            dsuv_ref[rows, :] = dsuv.astype(_MXU)
            dbsuv_ref[...] += _colsum(dsuv)
            return contrib

        @pl.when(i < steps)
        def _():
            contribs = [one_block(s) for s in range(per)]
            for t in range(per):
                top = carry_ref[...] if t == 0 else contribs[t - 1][BLK:2 * BLK, :]
                emit_kv(top + contribs[t][0:BLK, :], t)
            carry_ref[...] = contribs[per - 1][BLK:2 * BLK, :]

        @pl.when(i == steps)
        def _():
            emit_kv(carry_ref[...], 0)
            if per > 1:
                dkv_ref[BLK:per * BLK, :] = jnp.zeros(((per - 1) * BLK, 2 * KV_W), _MXU)
            for exchange, landed, flush_sem in exchanges:
                exchange.finish_to(landed, flush_sem)

    last = steps - 1
    cur = lambda w: pl.BlockSpec((per * BLK, w), lambda i: (jnp.minimum(i, last), 0))
    prev = lambda w: pl.BlockSpec((BLK, w), lambda i: (jnp.clip(per * i - 1, 0, nb - 1), 0))
    shifted = pl.BlockSpec((per * BLK, 2 * KV_W), lambda i: (i, 0))
    sd = _hbm_shape
    return pl.pallas_call(
        body, name="mixer_bwd", grid=(steps + 1,),
        in_specs=[cur(ATTN_W), cur(KV_W), prev(KV_W), cur(KV_W), prev(KV_W), cur(SGU_W), cur(SGU_W), cur(D_MODEL),
                  cur(128), cur(128), cur(128), prev(128), prev(128), prev(128),
                  _smem(), _const2((1, SGU_W)), _const2((1, SGU_W)), _const2((N_GRP, BLK, BLK)), _const2((BLK, N_GRP)),
                  _vmem(), _vmem()],
        out_specs=[cur(ATTN_W), shifted, cur(2 * SGU_W),
                   _const2((1, ATTN_W)), _const2((1, 2 * KV_W)), _const2((1, 2 * SGU_W)),
                   _const2((1, 128)), _const2((1, SGU_W)), _const2((1, SGU_W)),
                   _const2((N_GRP, BLK, BLK)), _const2((BLK, 128)), _hbm(), _hbm()],
        out_shape=[sd((s_len, ATTN_W), _MXU), sd((s_len + per * BLK, 2 * KV_W), _MXU), sd((s_len, 2 * SGU_W), _MXU),
                   sd((1, ATTN_W), F32), sd((1, 2 * KV_W), F32), sd((1, 2 * SGU_W), F32),
                   sd((1, 128), F32), sd((1, SGU_W), F32), sd((1, SGU_W), F32),
                   sd((N_GRP, BLK, BLK), F32), sd((BLK, 128), F32)]
        + [_ChipExchange.land_shape(w) for w in prev_wires],
        scratch_shapes=[pltpu.VMEM((BLK, 2 * KV_W), F32)] + _ChipExchange.scratch(prev_wires[0])
        + _ChipExchange.scratch(prev_wires[1]),
        compiler_params=_params(40),
    )(q, k, k, v, v, su, sv, dmc, tc, t1, t2, tc, t1, t2, sinks, sg, sb, sgu_w, sgu_bt, *prev_wires)


def _inproj_bwd(dq, dkv, dsuv, dr1, x, g0, b0, w_in):
    s_len = x.shape[0]
    tm = _tile(s_len, 512)
    cuts = ((0, ATTN_W), (ATTN_W, ATTN_W + 2 * KV_W), (ATTN_W + 2 * KV_W, IN_W))

    def body(dq_ref, dkv_ref, dsuv_ref, dr1_ref, x_ref, g_ref, b_ref, w_ref, dx_ref, dw_ref, dg_ref, db_ref):
        i = pl.program_id(0)

        @pl.when(i == 0)
        def _():
            dw_ref[...] = jnp.zeros_like(dw_ref)
            dg_ref[...] = jnp.zeros_like(dg_ref)
            db_ref[...] = jnp.zeros_like(db_ref)

        h0, xhat, rstd = _ln(x_ref[...], g_ref[...], b_ref[...])
        h0b = h0.astype(_MXU)
        dh0 = ALPHA * dr1_ref[...]
        for (lo, hi), d_ref in zip(cuts, (dq_ref, dkv_ref, dsuv_ref)):
            d = d_ref[...]
            dh0 = dh0 + _dot(d, w_ref[lo:hi, :])
            dw_ref[lo:hi, :] += _dot_tn(d, h0b)
        dg_ref[...] += _colsum(dh0 * xhat)
        db_ref[...] += _colsum(dh0)
        dx_ref[...] = _ln_bwd(dh0, xhat, rstd, g_ref[...])

    vec = _hbm_shape((1, D_MODEL), F32)
    c = _const2((1, D_MODEL))
    return pl.pallas_call(
        body, name="inproj_bwd", grid=(s_len // tm,),
        in_specs=[_rows(tm, ATTN_W), _rows(tm, 2 * KV_W), _rows(tm, 2 * SGU_W), _rows(tm, D_MODEL), _rows(tm, D_MODEL),
                  c, c, _vmem()],
        out_specs=[_rows(tm, D_MODEL), _vmem(), c, c],
        out_shape=[_hbm_shape((s_len, D_MODEL), F32), jax.ShapeDtypeStruct((IN_W, D_MODEL), F32), vec, vec],
        compiler_params=_params(48),
    )(dq, dkv, dsuv, dr1, x, g0, b0, w_in)


def _place():
    x, y, c = (lax.axis_index(a) for a in MESH_AXES)
    chips = [(1 - x, y), (x, 1 - y), (1 - x, 1 - y)]
    return x, y, c, chips


class _Gather:
    def __init__(self, ins, outs, send_sems, recv_sems, spans=None):
        self.ins, self.outs, self.send_sems, self.recv_sems = ins, outs, send_sems, recv_sems
        self.n = len(ins)
        self.spans = spans or [(0, r.shape[0]) for r in ins]
        self.halves = [(hi - lo) // 2 for lo, hi in self.spans]

    def _copy(self, k, t, slot, half, to):
        rows = pl.ds(pl.multiple_of(self.spans[t][0] + half * self.halves[t], 16), self.halves[t])
        piece = self.outs[t].at[slot, rows, :]
        return pltpu.make_async_remote_copy(src_ref=piece, dst_ref=piece, send_sem=self.send_sems.at[k],
                                            recv_sem=self.recv_sems.at[k], device_id=to, device_id_type=MESH)

    def _chip_copy(self, t, d, slot):
        x, y, c, chips = _place()
        return self._copy(3 * t + d, t, slot, c, (chips[d][0], chips[d][1], c))

    def _pass_copy(self, t, d, half):
        x, y, c, chips = _place()
        return self._copy(3 * self.n + 3 * t + d, t, 2 * chips[d][0] + chips[d][1], half, (x, y, 1 - c))

    def start(self):
        x, y, c, chips = _place()
        me = 2 * x + y
        for t in range(self.n):
            lo, hi = self.spans[t]
            self.outs[t][me, lo:hi, :] = self.ins[t][lo:hi, :].astype(_WIRE)
        for t in range(self.n):
            for d in range(3):
                self._chip_copy(t, d, me).start()

    def pass_on(self):
        x, y, c, chips = _place()
        for t in range(self.n):
            for d in range(3):
                self._chip_copy(t, d, 2 * chips[d][0] + chips[d][1]).wait_recv()
                self._pass_copy(t, d, c).start()

    def finish(self):
        x, y, c, chips = _place()
        me = 2 * x + y
        for t in range(self.n):
            for d in range(3):
                self._pass_copy(t, d, 1 - c).wait_recv()
        for t in range(self.n):
            for d in range(3):
                self._chip_copy(t, d, me).wait_send()
                self._pass_copy(t, d, c).wait_send()

    @staticmethod
    def out_shapes(shards, make=jax.ShapeDtypeStruct):
        return [make((N_CHIP,) + s.shape, _WIRE) for s in shards]

    @staticmethod
    def sems(n):
        return [pltpu.SemaphoreType.DMA((6 * n,)), pltpu.SemaphoreType.DMA((6 * n,))]


class _GatherPlan:
    def __init__(self, pieces):
        self.shards = [p[0] for p in pieces]
        self.spans = [p[1] for p in pieces]
        self.earlier = [p[2] for p in pieces]
        self.n = len(pieces)
        self.carried = [t for t in range(self.n) if self.earlier[t] is not None]

    def operands(self):
        return self.shards + [self.earlier[t] for t in self.carried]

    def in_specs(self):
        return [_vmem()] * self.n + [_hbm()] * len(self.carried)

    def out_specs(self):
        return [_hbm()] * self.n

    def out_shapes(self):
        return _Gather.out_shapes(self.shards, _hbm_shape)

    def scratch(self):
        return ([pltpu.VMEM((N_CHIP,) + s.shape, _WIRE) for s in self.shards] + _Gather.sems(self.n)
                + [pltpu.SemaphoreType.DMA((self.n,)), pltpu.SemaphoreType.DMA((max(len(self.carried), 1),))])

    def bind(self, in_refs, out_refs, scratch_refs):
        plan = self
        shard_refs, earlier_refs = in_refs[:self.n], in_refs[self.n:]
        bufs = scratch_refs[:self.n]
        send_sems, recv_sems, flush_sems, carry_sems = scratch_refs[self.n:self.n + 4]
        gather = _Gather(shard_refs, bufs, send_sems, recv_sems, self.spans)

        def carry_copy(k):
            t = plan.carried[k]
            lo = plan.spans[t][0]
            return pltpu.make_async_copy(earlier_refs[k].at[:, 0:lo, :], bufs[t].at[:, 0:lo, :], carry_sems.at[k])

        class Bound:
            @staticmethod
            def start():
                for k in range(len(plan.carried)):
                    carry_copy(k).start()
                gather.start()

            @staticmethod
            def pass_on():
                gather.pass_on()

            @staticmethod
            def finish():
                gather.finish()
                for k in range(len(plan.carried)):
                    carry_copy(k).wait()
                _flush([bufs[t].at[:, 0:plan.spans[t][1], :] for t in range(plan.n)],
                       [out_refs[t].at[:, 0:plan.spans[t][1], :] for t in range(plan.n)], flush_sems)

        return Bound


def _flush(bufs, hbm_outs, sems):
    copies = [pltpu.make_async_copy(b, o, sems.at[k]) for k, (b, o) in enumerate(zip(bufs, hbm_outs))]
    for cp in copies:
        cp.start()
    for cp in copies:
        cp.wait()


def _gather_weights(shards):
    n = len(shards)

    def body(*refs):
        gather = _Gather(refs[:n], refs[n:2 * n], refs[2 * n], refs[2 * n + 1])
        gather.start()
        gather.pass_on()
        gather.finish()

    return pl.pallas_call(
        body, name="gather_weights",
        in_specs=[_vmem()] * n, out_specs=[_vmem()] * n,
        out_shape=_Gather.out_shapes(shards), scratch_shapes=_Gather.sems(n),
        compiler_params=pltpu.CompilerParams(vmem_limit_bytes=32 * MIB),
    )(*shards)


class _ChipExchange:
    def __init__(self, wire_ref, land_ref, send_sems, recv_sems):
        self.wire, self.land, self.send_sems, self.recv_sems = wire_ref, land_ref, send_sems, recv_sems

    def _copy(self, d):
        x, y, c, chips = _place()
        return pltpu.make_async_remote_copy(
            src_ref=self.wire.at[2 * chips[d][0] + chips[d][1]], dst_ref=self.land.at[d],
            send_sem=self.send_sems.at[d], recv_sem=self.recv_sems.at[d],
            device_id=(chips[d][0], chips[d][1], c), device_id_type=MESH)

    def start(self):
        for d in range(3):
            self._copy(d).start()

    def wait_recv(self):
        for d in range(3):
            self._copy(d).wait_recv()

    def wait_send(self):
        for d in range(3):
            self._copy(d).wait_send()

    def finish_to(self, hbm_out, flush_sem):
        self.wait_recv()
        _flush([self.land], [hbm_out], flush_sem)
        self.wait_send()

    @staticmethod
    def land_shape(wire):
        return _hbm_shape((3,) + wire.shape[1:], wire.dtype)

    @staticmethod
    def sems():
        return [pltpu.SemaphoreType.DMA((3,)), pltpu.SemaphoreType.DMA((3,))]

    @staticmethod
    def scratch(wire):
        return ([pltpu.VMEM((3,) + wire.shape[1:], wire.dtype)] + _ChipExchange.sems() + [pltpu.SemaphoreType.DMA((1,))])


def _pair_out_shapes(half_shape):
    return [jax.ShapeDtypeStruct(half_shape, _WIRE), jax.ShapeDtypeStruct(half_shape[1:], F32)]


def _pair_scratch(acc_shape, half_shape):
    return [pltpu.VMEM(acc_shape, F32), pltpu.VMEM(half_shape, _WIRE),
            pltpu.SemaphoreType.DMA((N_CHIP,)), pltpu.SemaphoreType.DMA((N_CHIP,))]


def _pair_reduce(acc_ref, wire_ref, own_ref, land_ref, send_sems, recv_sems):
    rh = land_ref.shape[1]
    x, y, c, _ = _place()
    me = 2 * x + y
    copies = []
    for j in range(N_CHIP):
        def cast(r, carry, j=j):
            dst = pl.ds(pl.multiple_of(r * ROW_CHUNK, ROW_CHUNK), ROW_CHUNK)
            src = pl.ds(pl.multiple_of((2 * j + 1 - c) * rh + r * ROW_CHUNK, 8), ROW_CHUNK)
            wire_ref[j, dst, :] = acc_ref[src, :].astype(_WIRE)
            return carry

        lax.fori_loop(0, rh // ROW_CHUNK, cast, 0)
        cp = pltpu.make_async_remote_copy(src_ref=wire_ref.at[j], dst_ref=land_ref.at[j], send_sem=send_sems.at[j],
                                          recv_sem=recv_sems.at[j], device_id=(x, y, 1 - c), device_id_type=MESH)
        cp.start()
        copies.append(cp)
    for j in range(N_CHIP):
        copies[j].wait()

        def chunk(r, carry, j=j):
            theirs = pl.ds(pl.multiple_of(r * ROW_CHUNK, ROW_CHUNK), ROW_CHUNK)
            mine = pl.ds(pl.multiple_of((2 * j + c) * rh + r * ROW_CHUNK, 8), ROW_CHUNK)
            wire_ref[j, theirs, :] = (acc_ref[mine, :] + land_ref[j, theirs, :].astype(F32)).astype(_WIRE)
            return carry

        lax.fori_loop(0, rh // ROW_CHUNK, chunk, 0)

    def own_chunk(r, carry):
        theirs = pl.ds(pl.multiple_of(r * ROW_CHUNK, ROW_CHUNK), ROW_CHUNK)
        mine = pl.ds(pl.multiple_of((2 * me + c) * rh + r * ROW_CHUNK, 8), ROW_CHUNK)
        own_ref[theirs, :] = acc_ref[mine, :] + land_ref[me, theirs, :].astype(F32)
        return carry

    lax.fori_loop(0, rh // ROW_CHUNK, own_chunk, 0)


def _grad_finish(last_acc, lands, owns, small):
    n = len(owns) + 1
    halves = [last_acc.shape[0] // (2 * N_CHIP)] + [w.shape[1] for w in lands]
    widths = [last_acc.shape[1]] + [a.shape[1] for a in owns]
    small_body, small_scratch = _small_allreduce_parts()
    ns = len(small)

    def body(*refs):
        acc0, land, own = refs[0], (None,) + refs[1:n], (None,) + refs[n:2 * n - 1]
        refs = refs[2 * n - 1:]
        small_in, g, small_out = refs[:ns], refs[ns:ns + n], refs[ns + n:ns + n + 2]
        refs = refs[ns + n + 2:]
        pland0, wire0, land0, own0 = refs[0:4]
        p_send, p_recv, x_send, x_recv, pair_send, pair_recv = refs[4:10]
        small_refs = refs[10:]
        land = (land0,) + land[1:]
        own = (own0,) + own[1:]
        x, y, c, chips = _place()
        me = 2 * x + y
        exchange = _ChipExchange(wire0, land0, x_send, x_recv)

        def half_rows(t, half):
            return pl.ds(pl.multiple_of(half * halves[t], 8), halves[t])

        def pair_copy(t, half):
            rows = g[t].at[half_rows(t, half), :]
            return pltpu.make_async_remote_copy(src_ref=rows, dst_ref=rows, send_sem=pair_send.at[t],
                                                recv_sem=pair_recv.at[t], device_id=(x, y, 1 - c), device_id_type=MESH)

        small_rounds = small_body(*small_in, *small_out, *small_refs)
        next(small_rounds)
        _pair_reduce(acc0, wire0, own0, pland0, p_send, p_recv)
        next(small_rounds)
        exchange.start()

        for t in list(range(1, n)) + [0]:
            if t == 0:
                exchange.wait_recv()
            if t == min(2, n - 1):
                next(small_rounds)
            if t == min(4, n - 1):
                next(small_rounds, None)

            def chunk(r, carry, t=t):
                src = pl.ds(pl.multiple_of(r * ROW_CHUNK, ROW_CHUNK), ROW_CHUNK)
                dst = pl.ds(pl.multiple_of(c * halves[t] + r * ROW_CHUNK, 8), ROW_CHUNK)
                s = own[t][src, :]
                for d in range(3):
                    s = s + land[t][d, src, :].astype(F32)
                g[t][dst, :] = s
                return carry

            lax.fori_loop(0, halves[t] // ROW_CHUNK, chunk, 0)
            pair_copy(t, c).start()
        for t in range(n):
            pair_copy(t, 1 - c).wait_recv()
        for t in range(n):
            pair_copy(t, c).wait_send()
        exchange.wait_send()

    half0 = (halves[0], widths[0])
    return pl.pallas_call(
        body, name="grad_finish",
        in_specs=[_vmem()] * (2 * n - 1 + ns), out_specs=[_vmem()] * (n + 2),
        out_shape=[jax.ShapeDtypeStruct((2 * h, w), F32) for h, w in zip(halves, widths)]
        + [jax.ShapeDtypeStruct(s, F32) for s in _SMALL_OUT_DIMS],
        scratch_shapes=[pltpu.VMEM((N_CHIP,) + half0, _WIRE), pltpu.VMEM((N_CHIP,) + half0, _WIRE),
                        pltpu.VMEM((3,) + half0, _WIRE), pltpu.VMEM(half0, F32)]
        + [pltpu.SemaphoreType.DMA((N_CHIP,)), pltpu.SemaphoreType.DMA((N_CHIP,))]
        + _ChipExchange.sems()
        + [pltpu.SemaphoreType.DMA((n,)), pltpu.SemaphoreType.DMA((n,))]
        + small_scratch,
        compiler_params=pltpu.CompilerParams(vmem_limit_bytes=56 * MIB),
    )(last_acc, *lands, *owns, *small)


_SMALL = ("ln_in_g", "ln_in_b", "b_in", "attn_sinks", "sgu_ln_g", "sgu_ln_b", "sgu_w", "sgu_b", "b_out",
          "ln_mix_g", "ln_mix_b", "ln_ffn_g", "ln_ffn_b")
_VEC_ROW = dict(ln_in_g=0, ln_in_b=1, b_in=2, attn_sinks=4, sgu_ln_g=5, sgu_ln_b=6, b_out=7, ln_mix_g=8, ln_mix_b=9,
                ln_ffn_g=10, ln_ffn_b=11)
_LOSS_ROW = 12
_VEC_ROWS = 16
_MAT_ROWS = N_GRP * BLK + BLK


_SMALL_IN = ("ln_in_g", "ln_in_b", "bq", "bkv", "bsuv", "sink", "sgu_ln_g", "sgu_ln_b", "sgu_w", "sgu_bt", "b_out",
             "ln_mix_g", "ln_mix_b", "ln_ffn_g", "ln_ffn_b", "loss")
_SMALL_OUT_DIMS = ((_VEC_ROWS, D_MODEL), (_MAT_ROWS, 128))


def _small_allreduce_parts():
    n_in = len(_SMALL_IN)

    def body(*refs):
        (g_ln_in_g, g_ln_in_b, g_bq, g_bkv, g_bsuv, g_sink, g_sln_g, g_sln_b, g_sw, g_sbt, g_bout,
         g_lmg, g_lmb, g_lfg, g_lfb, g_loss) = refs[:n_in]
        out_a, out_b = refs[n_in:n_in + 2]
        (buf_a, buf_b, pair_a, pair_b, stage_a, stage_b, tot_a, tot_b,
         p1_send, p1_recv, x_send, x_recv, p2_send, p2_recv) = refs[n_in + 2:]
        x, y, c, chips = _place()
        me = 2 * x + y
        sibling = (x, y, 1 - c)
        half_a, half_b = _VEC_ROWS // 2, _MAT_ROWS // 2

        buf_a[...] = jnp.zeros_like(buf_a)
        for row, ref in ((0, g_ln_in_g), (1, g_ln_in_b), (7, g_bout), (8, g_lmg), (9, g_lmb), (10, g_lfg), (11, g_lfb),
                         (_LOSS_ROW, g_loss)):
            buf_a[row:row + 1, :] = ref[...]
        buf_a[2:3, 0:ATTN_W] = g_bq[...]
        buf_a[2:3, ATTN_W:ATTN_W + 2 * KV_W] = g_bkv[...]
        buf_a[2:3, ATTN_W + 2 * KV_W:D_MODEL] = g_bsuv[:, 0:2 * KV_W]
        buf_a[3:4, 0:2 * SGU_W - 2 * KV_W] = g_bsuv[:, 2 * KV_W:2 * SGU_W]
        buf_a[4:5, 0:128] = g_sink[...]
        buf_a[5:6, 0:SGU_W] = g_sln_g[...]
        buf_a[6:7, 0:SGU_W] = g_sln_b[...]
        for h in range(N_GRP):
            buf_b[h * BLK:(h + 1) * BLK, :] = g_sw[h]
        buf_b[N_GRP * BLK:_MAT_ROWS, :] = g_sbt[...]

        def remote(src, dst, send_sem, recv_sem, to):
            return pltpu.make_async_remote_copy(src_ref=src, dst_ref=dst, send_sem=send_sem, recv_sem=recv_sem,
                                                device_id=to, device_id_type=MESH)

        first = [remote(buf_a, pair_a, p1_send.at[0], p1_recv.at[0], sibling),
                 remote(buf_b, pair_b, p1_send.at[1], p1_recv.at[1], sibling)]
        for cp in first:
            cp.start()
        yield
        for cp in first:
            cp.wait()
        rows_a = pl.ds(pl.multiple_of(c * half_a, 8), half_a)
        rows_b = pl.ds(pl.multiple_of(c * half_b, 8), half_b)
        stage_a[me] = buf_a[rows_a, :] + pair_a[rows_a, :]
        stage_b[me] = buf_b[rows_b, :] + pair_b[rows_b, :]

        def chip_copies(d):
            to = (chips[d][0], chips[d][1], c)
            return [remote(stage_a.at[me], stage_a.at[me], x_send.at[2 * d], x_recv.at[2 * d], to),
                    remote(stage_b.at[me], stage_b.at[me], x_send.at[2 * d + 1], x_recv.at[2 * d + 1], to)]

        def chip_arrivals(d):
            slot = 2 * chips[d][0] + chips[d][1]
            to = (chips[d][0], chips[d][1], c)
            return [remote(stage_a.at[slot], stage_a.at[slot], x_send.at[2 * d], x_recv.at[2 * d], to),
                    remote(stage_b.at[slot], stage_b.at[slot], x_send.at[2 * d + 1], x_recv.at[2 * d + 1], to)]

        for d in range(3):
            for cp in chip_copies(d):
                cp.start()
        yield
        for d in range(3):
            for cp in chip_arrivals(d):
                cp.wait_recv()
        tot_a[rows_a, :] = ((stage_a[0] + stage_a[1]) + stage_a[2]) + stage_a[3]
        tot_b[rows_b, :] = ((stage_b[0] + stage_b[1]) + stage_b[2]) + stage_b[3]

        second = [remote(tot_a.at[rows_a, :], tot_a.at[rows_a, :], p2_send.at[0], p2_recv.at[0], sibling),
                  remote(tot_b.at[rows_b, :], tot_b.at[rows_b, :], p2_send.at[1], p2_recv.at[1], sibling)]
        for cp in second:
            cp.start()
        yield
        other_a = pl.ds(pl.multiple_of((1 - c) * half_a, 8), half_a)
        other_b = pl.ds(pl.multiple_of((1 - c) * half_b, 8), half_b)
        remote(tot_a.at[other_a, :], tot_a.at[other_a, :], p2_send.at[0], p2_recv.at[0], sibling).wait_recv()
        remote(tot_b.at[other_b, :], tot_b.at[other_b, :], p2_send.at[1], p2_recv.at[1], sibling).wait_recv()
        for cp in second:
            cp.wait_send()
        for d in range(3):
            for cp in chip_copies(d):
                cp.wait_send()
        out_a[...] = tot_a[...]
        out_b[...] = tot_b[...]

    vec = pltpu.VMEM((_VEC_ROWS, D_MODEL), F32)
    mat = pltpu.VMEM((_MAT_ROWS, 128), F32)
    scratch = [vec, mat, vec, mat, pltpu.VMEM((N_CHIP, _VEC_ROWS // 2, D_MODEL), F32),
               pltpu.VMEM((N_CHIP, _MAT_ROWS // 2, 128), F32), vec, mat,
               pltpu.SemaphoreType.DMA((2,)), pltpu.SemaphoreType.DMA((2,)), pltpu.SemaphoreType.DMA((6,)),
               pltpu.SemaphoreType.DMA((6,)), pltpu.SemaphoreType.DMA((2,)), pltpu.SemaphoreType.DMA((2,))]
    return body, scratch


def _small_adamw(tot_a, tot_b, params):
    shapes = [params[nm][0].shape for nm in _SMALL]

    def body(*refs):
        ta, tb = refs[:2]
        prm = refs[2:2 + 3 * len(_SMALL)]
        outs = refs[2 + 3 * len(_SMALL):]

        def grad_of(k, name):
            if name == "sgu_w":
                return [tb[h * BLK:(h + 1) * BLK, :] for h in range(N_GRP)]
            if name == "sgu_b":
                return jnp.transpose(tb[N_GRP * BLK:_MAT_ROWS, :])[0:N_GRP, :]
            row = _VEC_ROW[name]
            if name == "b_in":
                return jnp.concatenate([ta[row:row + 1, :], ta[row + 1:row + 2, 0:IN_W - D_MODEL]], axis=1)
            return ta[row:row + 1, 0:shapes[k][-1]]

        for k, name in enumerate(_SMALL):
            w_ref, m_ref, v_ref = prm[3 * k:3 * k + 3]
            g_out, d_out, m_out, v_out = outs[4 * k:4 * k + 4]
            g = grad_of(k, name)
            if name == "sgu_w":
                for h in range(N_GRP):
                    d_, m_, v_ = _adamw_math(w_ref[h], g[h], m_ref[h], v_ref[h])
                    g_out[h], d_out[h], m_out[h], v_out[h] = g[h], d_, m_, v_
            else:
                d_, m_, v_ = _adamw_math(w_ref[...], g, m_ref[...], v_ref[...])
                g_out[...], d_out[...], m_out[...], v_out[...] = g, d_, m_, v_
        outs[-1][...] = ta[_LOSS_ROW:_LOSS_ROW + 1, :]

    ins = [tot_a, tot_b] + [_in_hbm(a) for nm in _SMALL for a in params[nm]]
    out_dims = [s for s in shapes for _ in range(4)] + [(1, D_MODEL)]
    res = pl.pallas_call(
        body, name="small_adamw", grid=(1,),
        in_specs=[_const2(a.shape) for a in ins], out_specs=[_const2(s) for s in out_dims],
        out_shape=[_hbm_shape(s, F32) for s in out_dims],
        compiler_params=_params(32),
    )(*ins)
    return {nm: tuple(res[4 * k:4 * k + 4]) for k, nm in enumerate(_SMALL)}, res[-1]


def _adamw_math(w, g, m, v):
    m = ADAM_B1 * m + (1.0 - ADAM_B1) * g
    v = ADAM_B2 * v + (1.0 - ADAM_B2) * (g * g)
    m_hat = m / (1.0 - ADAM_B1 ** ADAM_STEP)
    v_hat = v / (1.0 - ADAM_B2 ** ADAM_STEP)
    delta = -ADAM_LR * (m_hat / (jnp.sqrt(v_hat) + ADAM_EPS) + ADAM_WD * w)
    return delta, m, v


ADAMW_STEPS = 4


def _adamw(name, groups):
    k = len(groups)

    def body(*refs):
        for i in range(k):
            w_ref, g_ref, m_ref, v_ref = refs[4 * i:4 * i + 4]
            g = g_ref[...]
            for o_ref, o in zip(refs[4 * k + 4 * i:4 * k + 4 * i + 4], (g,) + _adamw_math(w_ref[...], g, m_ref[...], v_ref[...])):
                o_ref[...] = o

    specs = []
    for grp in groups:
        rows, cols = grp[0].shape
        assert rows % (8 * ADAMW_STEPS) == 0, rows
        specs += [pl.BlockSpec((rows // ADAMW_STEPS, cols), lambda i: (i, 0))] * 4
    res = pl.pallas_call(
        body, name=name, grid=(ADAMW_STEPS,), in_specs=specs, out_specs=specs,
        out_shape=[_hbm_shape(grp[0].shape, F32) for grp in groups for _ in range(4)],
        compiler_params=_params(56),
    )(*[_in_hbm(a) for grp in groups for a in grp])
    return [res[4 * i:4 * i + 4] for i in range(k)]


def kernel(x, positions, ln_in_g, ln_in_b, w_in, b_in, attn_sinks, sgu_ln_g, sgu_ln_b, sgu_w, sgu_b, w_out, b_out, ln_mix_g, ln_mix_b, w_gate, w_up, w_down, ln_ffn_g, ln_ffn_b, loss_target, m_ln_in_g, m_ln_in_b, m_w_in, m_b_in, m_attn_sinks, m_sgu_ln_g, m_sgu_ln_b, m_sgu_w, m_sgu_b, m_w_out, m_b_out, m_ln_mix_g, m_ln_mix_b, m_w_gate, m_w_up, m_w_down, m_ln_ffn_g, m_ln_ffn_b, v_ln_in_g, v_ln_in_b, v_w_in, v_b_in, v_attn_sinks, v_sgu_ln_g, v_sgu_ln_b, v_sgu_w, v_sgu_b, v_w_out, v_b_out, v_ln_mix_g, v_ln_mix_b, v_w_gate, v_w_up, v_w_down, v_ln_ffn_g, v_ln_ffn_b):
    weights = dict(ln_in_g=ln_in_g, ln_in_b=ln_in_b, w_in=w_in, b_in=b_in, attn_sinks=attn_sinks, sgu_ln_g=sgu_ln_g,
                   sgu_ln_b=sgu_ln_b, sgu_w=sgu_w, sgu_b=sgu_b, w_out=w_out, b_out=b_out, ln_mix_g=ln_mix_g,
                   ln_mix_b=ln_mix_b, w_gate=w_gate, w_up=w_up, w_down=w_down, ln_ffn_g=ln_ffn_g, ln_ffn_b=ln_ffn_b)
    mom_m = dict(ln_in_g=m_ln_in_g, ln_in_b=m_ln_in_b, w_in=m_w_in, b_in=m_b_in, attn_sinks=m_attn_sinks,
                 sgu_ln_g=m_sgu_ln_g, sgu_ln_b=m_sgu_ln_b, sgu_w=m_sgu_w, sgu_b=m_sgu_b, w_out=m_w_out, b_out=m_b_out,
                 ln_mix_g=m_ln_mix_g, ln_mix_b=m_ln_mix_b, w_gate=m_w_gate, w_up=m_w_up, w_down=m_w_down,
                 ln_ffn_g=m_ln_ffn_g, ln_ffn_b=m_ln_ffn_b)
    mom_v = dict(ln_in_g=v_ln_in_g, ln_in_b=v_ln_in_b, w_in=v_w_in, b_in=v_b_in, attn_sinks=v_attn_sinks,
                 sgu_ln_g=v_sgu_ln_g, sgu_ln_b=v_sgu_ln_b, sgu_w=v_sgu_w, sgu_b=v_sgu_b, w_out=v_w_out, b_out=v_b_out,
                 ln_mix_g=v_ln_mix_g, ln_mix_b=v_ln_mix_b, w_gate=v_w_gate, w_up=v_w_up, w_down=v_w_down,
                 ln_ffn_g=v_ln_ffn_g, ln_ffn_b=v_ln_ffn_b)
    order = list(weights)
    big = ("w_in", "w_out", "w_gate", "w_up", "w_down")

    s_len = x.shape[1]
    xs = _in_hbm(x.reshape(s_len, D_MODEL))
    tgt = _in_hbm(loss_target.reshape(s_len, D_MODEL))
    pos_col = _in_hbm(positions.reshape(s_len, 1))
    g0, b0 = _in_hbm(ln_in_g.reshape(1, D_MODEL)), _in_hbm(ln_in_b.reshape(1, D_MODEL))
    sinks = attn_sinks.reshape(N_Q)
    sgu_w3 = _in_hbm(sgu_w.reshape(N_GRP, BLK, BLK))
    sgu_bt = _in_hbm(sgu_b.reshape(N_GRP, BLK).T)
    b_in, b_out, sgu_ln_g, sgu_ln_b, ln_mix_g, ln_mix_b, ln_ffn_g, ln_ffn_b = (
        _in_hbm(a) for a in (b_in, b_out, sgu_ln_g, sgu_ln_b, ln_mix_g, ln_mix_b, ln_ffn_g, ln_ffn_b))

    col_sharded = ("w_in", "w_gate", "w_up")

    def rowmajor(name, a):
        return jnp.swapaxes(a[0], 0, 1) if name in col_sharded else a[0]

    def as_given(name, a):
        return (jnp.swapaxes(a, 0, 1) if name in col_sharded else a)[None]

    shards = [rowmajor(n, weights[n]) for n in big]
    (gw_in,) = _gather_weights(shards[0:1])
    w_in_full = gw_in.reshape(IN_W, D_MODEL)

    sh_out, sh_gate, sh_up, sh_down = shards[1:]
    *acts, gw_out, gw_gate0 = _ln_inproj(xs, pos_col, g0, b0, w_in_full, b_in, _GatherPlan(
        [(sh_out, (0, OUT_SH), None), (sh_gate, (0, GATE_CUT), None)]))
    q, k, v, su, sv, tc, t1, t2 = (_in_hbm(a) for a in acts)
    mc, gw_gate, gw_up0 = _mixer_fwd(q, k, v, su, sv, sinks, sgu_ln_g, sgu_ln_b, sgu_w3, sgu_bt, _GatherPlan(
        [(sh_gate, (GATE_CUT, FF_SH), gw_gate0), (sh_up, (0, UP_CUT), None)]))
    mc = _in_hbm(mc)
    w_out_full = gw_out.reshape(D_MODEL, D_MODEL)
    r1, gw_up = _outproj(mc, w_out_full, b_out, xs, g0, b0, _GatherPlan([(sh_up, (UP_CUT, FF_SH), gw_up0)]))
    r1 = _in_hbm(r1)
    act, p_act, q_act, gw_down = _ffn_up(r1, ln_mix_g, ln_mix_b, gw_gate, gw_up,
                                         _GatherPlan([(sh_down, (0, FF_SH), None)]))
    act, p_act, q_act = _in_hbm(act), _in_hbm(p_act), _in_hbm(q_act)
    dr2, loss_cols, d_ln_ffn_g, d_ln_ffn_b = _ffn_down_loss(act, gw_down, r1, ln_mix_g, ln_mix_b, ln_ffn_g, ln_ffn_b, tgt)
    dr2 = _in_hbm(dr2)

    dg, du, wire_down, own_down = _ffn_bwd_a(dr2, act, p_act, q_act, gw_down)
    dh1a, wire_gate, own_gate, land_down = _ffn_bwd_g(dr2, _in_hbm(dg), r1, ln_mix_g, ln_mix_b, gw_gate, wire_down)
    dr1, wire_up, own_up, d_ln_mix_g, d_ln_mix_b, land_gate = _ffn_bwd_u(_in_hbm(dh1a), _in_hbm(du), r1, ln_mix_g,
                                                                         ln_mix_b, gw_up, wire_gate)
    dr1 = _in_hbm(dr1)
    dmc, wire_out, own_out, d_b_out = _outproj_bwd(dr1, mc, w_out_full)
    (dq, dkv, dsuv, dbq, dbkv, dbsuv, d_sink, d_sgu_ln_g, d_sgu_ln_b, d_sgu_w, d_sgu_bt, land_up, land_out) = _mixer_bwd(
        q, k, v, su, sv, _in_hbm(dmc), tc, t1, t2, sinks, sgu_ln_g, sgu_ln_b, sgu_w3, sgu_bt, [wire_up, wire_out])
    dkv = dkv[BLK:BLK + s_len]
    grad_x, acc_in, d_ln_in_g, d_ln_in_b = _inproj_bwd(_in_hbm(dq), _in_hbm(dkv), _in_hbm(dsuv), dr1, xs, g0, b0,
                                                       w_in_full)

    small_local = dict(
        ln_in_g=d_ln_in_g, ln_in_b=d_ln_in_b, bq=dbq, bkv=dbkv, bsuv=dbsuv, sink=d_sink, sgu_ln_g=d_sgu_ln_g,
        sgu_ln_b=d_sgu_ln_b, sgu_w=d_sgu_w, sgu_bt=d_sgu_bt, b_out=d_b_out, ln_mix_g=d_ln_mix_g, ln_mix_b=d_ln_mix_b,
        ln_ffn_g=d_ln_ffn_g, ln_ffn_b=d_ln_ffn_b, loss=loss_cols)
    *reduced, tot_a, tot_b = _grad_finish(acc_in, [land_out, land_gate, land_up, land_down],
                                          [own_out, own_gate, own_up, own_down], [small_local[nm] for nm in _SMALL_IN])
    small_shape = dict(ln_in_g=(1, D_MODEL), ln_in_b=(1, D_MODEL), sgu_w=(N_GRP, BLK, BLK), sgu_b=(N_GRP, BLK))
    small_params = {nm: tuple(src[nm].reshape(small_shape.get(nm, src[nm].shape)) for src in (weights, mom_m, mom_v))
                    for nm in _SMALL}
    small_out, loss_sum = _small_adamw(_in_hbm(tot_a), _in_hbm(tot_b), small_params)
    loss = jnp.sum(loss_sum) * (0.5 / D_MODEL)
    grads, delta, new_m, new_v = {}, {}, {}, {}
    for nm in _SMALL:
        grads[nm], delta[nm], new_m[nm], new_v[nm] = (a.reshape(weights[nm].shape) for a in small_out[nm])

    groups = [(shards[t], reduced[t], rowmajor(nm, mom_m[nm]), rowmajor(nm, mom_v[nm])) for t, nm in enumerate(big)]
    for nm, res in zip(big, _adamw("adamw", groups)):
        grads[nm], delta[nm], new_m[nm], new_v[nm] = (as_given(nm, a) for a in res)

    return (loss, grad_x.reshape(x.shape), *[grads[n] for n in order], *[delta[n] for n in order],
            *[new_m[n] for n in order], *[new_v[n] for n in order])
```

```python
import jax
import jax.numpy as jnp
from jax import lax
from jax.experimental import pallas as pl
from jax.experimental.pallas import tpu as pltpu

F32 = jnp.float32
_MXU = jnp.bfloat16
_WIRE = jnp.bfloat16
_ACT = jnp.bfloat16

D_MODEL = 1024
ATTN_W = 512
SGU_W = 512
HEAD_DIM = 64
N_Q = 8
N_KV = 2
Q_PER_KV = 4
KV_W = 128
BLK = 128
ROT_DIM = 16
ROPE_THETA = 500000.0
N_GRP = 4
GRP_DIM = 128
D_FF = 2816
IN_W = 1792
LN_EPS = 1e-5
ALPHA = 2.0 ** 0.25
N_CHIP = 4
FF_SH = D_FF // N_CHIP
IN_SH = IN_W // N_CHIP
OUT_SH = D_MODEL // N_CHIP
ROW_CHUNK = 32
GATE_CUT, UP_CUT = 352, 320

ADAM_LR = 0.001
ADAM_B1 = 0.9
ADAM_B2 = 0.999
ADAM_EPS = 1e-08
ADAM_WD = 0.01
ADAM_STEP = 10

SQRT_HALF = 0.7071067811865476
INV_SQRT_2PI = 0.3989422804014327
MESH_AXES = ("x", "y", "c")
MESH = pl.DeviceIdType.MESH
MIB = 2 ** 20


def _vmem():
    return pl.BlockSpec(memory_space=pltpu.VMEM)


def _smem():
    return pl.BlockSpec(memory_space=pltpu.SMEM)


def _hbm():
    return pl.BlockSpec(memory_space=pl.ANY)


def _hbm_shape(shape, dtype):
    return pltpu.HBM(shape, dtype)


def _in_hbm(a):
    return pltpu.with_memory_space_constraint(a, pltpu.HBM)


def _params(vmem_mib=48):
    return pltpu.CompilerParams(dimension_semantics=("arbitrary",), vmem_limit_bytes=vmem_mib * MIB)


def _tile(n, cap):
    if n <= cap:
        return n
    for t in range(cap - cap % 16, 0, -16):
        if n % t == 0:
            return t
    raise ValueError((n, cap))


def _rows(tm, width):
    return pl.BlockSpec((tm, width), lambda i: (i, 0))


def _const2(shape):
    return pl.BlockSpec(shape, lambda i: (0,) * len(shape))


def _ln(x, g, b):
    mu = jnp.mean(x, axis=-1, keepdims=True)
    xc = x - mu
    var = jnp.mean(xc * xc, axis=-1, keepdims=True)
    rstd = lax.rsqrt(var + LN_EPS)
    xhat = xc * rstd
    return xhat * g + b, xhat, rstd


def _ln_bwd(dy, xhat, rstd, g):
    gdy = dy * g
    m1 = jnp.mean(gdy, axis=-1, keepdims=True)
    m2 = jnp.mean(gdy * xhat, axis=-1, keepdims=True)
    return rstd * (gdy - m1 - xhat * m2)


def _colsum(a):
    return jnp.sum(a, axis=0, keepdims=True)


def _gelu_and_grad(x):
    cdf = 0.5 * (1.0 + lax.erf(x * SQRT_HALF))
    return x * cdf, cdf + x * jnp.exp(-0.5 * x * x) * INV_SQRT_2PI


def _dot(a, b):
    return jnp.dot(a, b, preferred_element_type=F32)


def _dot_nt(a, b):
    return lax.dot_general(a, b, (((1,), (1,)), ((), ())), preferred_element_type=F32)


def _dot_tn(a, b):
    return lax.dot_general(a, b, (((0,), (0,)), ((), ())), preferred_element_type=F32)


def _rope(t, tc, t1, t2):
    n = t.shape[1]
    rep = n // 128
    if rep > 1:
        tc, t1, t2 = (jnp.tile(a, (1, rep)) for a in (tc, t1, t2))
    return t * tc + pltpu.roll(t, n - 8, 1) * t1 + pltpu.roll(t, 8, 1) * t2


def _rope_bwd(d, tc, t1, t2):
    n = d.shape[1]
    rep = n // 128
    if rep > 1:
        tc, t1, t2 = (jnp.tile(a, (1, rep)) for a in (tc, t1, t2))
    return d * tc + pltpu.roll(d * t1, 8, 1) + pltpu.roll(d * t2, n - 8, 1)


def _causal_w(w_ref, h):
    t = lax.broadcasted_iota(jnp.int32, (BLK, BLK), 0)
    s = lax.broadcasted_iota(jnp.int32, (BLK, BLK), 1)
    return jnp.where(s <= t, w_ref[h], 0.0)


def _lane_put(vals, width):
    rows = vals[0].shape[0]
    lane = lax.broadcasted_iota(jnp.int32, (rows, width), 1)
    out = jnp.zeros((rows, width), F32)
    for k, v in enumerate(vals):
        out = out + jnp.where(lane == k, v, 0.0)
    return out


def _rope_consts():
    lane = jnp.arange(128) % HEAD_DIM
    rot = lane < ROT_DIM
    pair = (2 * (lane % (ROT_DIM // 2))).astype(F32)
    freq = jnp.where(rot, ROPE_THETA ** (-pair / ROT_DIM), 0.0)
    rows = [freq, rot.astype(F32), 1.0 - rot.astype(F32), (lane < ROT_DIM // 2).astype(F32),
            jnp.logical_and(lane >= ROT_DIM // 2, rot).astype(F32)]
    rows += [jnp.zeros((128,), F32)] * 3
    return jnp.stack(rows).astype(F32)


def _ln_inproj(x, pos_row, g0, b0, w_in, b_in, plan):
    s_len = x.shape[0]
    tm = _tile(s_len, 512)
    m, n = len(plan.operands()), plan.n

    def body(x_ref, pos_ref, g_ref, b_ref, w_ref, bi_ref, rc_ref, *rest):
        q_ref, k_ref, v_ref, su_ref, sv_ref, tc_ref, t1_ref, t2_ref = rest[m:m + 8]
        gather = plan.bind(rest[:m], rest[m + 8:m + 8 + n], rest[m + 8 + n:])
        i = pl.program_id(0)

        @pl.when(i == 0)
        def _():
            gather.start()

        h0, _, _ = _ln(x_ref[...], g_ref[...], b_ref[...])
        proj = _dot_nt(h0.astype(_MXU), w_ref[...]) + bi_ref[...]
        pos = jnp.broadcast_to(pos_ref[...].astype(F32), (128, tm))
        ang = jnp.transpose(pos) * rc_ref[0:1, :]
        cs = jnp.cos(ang)
        sn = jnp.sin(ang)
        tc = cs * rc_ref[1:2, :] + rc_ref[2:3, :]
        t1 = -sn * rc_ref[3:4, :]
        t2 = sn * rc_ref[4:5, :]
        tc_ref[...] = tc
        t1_ref[...] = t1
        t2_ref[...] = t2
        q = _rope(proj[:, 0:ATTN_W], tc, t1, t2) * (HEAD_DIM ** -0.5)
        q_ref[...] = q.astype(_MXU)
        k_ref[...] = _rope(proj[:, ATTN_W:ATTN_W + KV_W], tc, t1, t2).astype(_MXU)
        v_ref[...] = proj[:, ATTN_W + KV_W:ATTN_W + 2 * KV_W].astype(_MXU)
        su_ref[...] = proj[:, ATTN_W + 2 * KV_W:ATTN_W + 2 * KV_W + SGU_W]
        sv_ref[...] = proj[:, ATTN_W + 2 * KV_W + SGU_W:IN_W]

        last = pl.num_programs(0) - 1

        @pl.when(i == jnp.maximum(last - 1, 0))
        def _():
            gather.pass_on()

        @pl.when(i == last)
        def _():
            gather.finish()

    sd = _hbm_shape
    return pl.pallas_call(
        body, name="ln_inproj", grid=(s_len // tm,),
        in_specs=[_rows(tm, D_MODEL), pl.BlockSpec((1, tm), lambda i: (0, i)), _const2((1, D_MODEL)),
                  _const2((1, D_MODEL)), _vmem(),
                  _const2((1, IN_W)), _const2((8, 128))] + plan.in_specs(),
        out_specs=[_rows(tm, ATTN_W), _rows(tm, KV_W), _rows(tm, KV_W), _rows(tm, SGU_W), _rows(tm, SGU_W),
                   _rows(tm, 128), _rows(tm, 128), _rows(tm, 128)] + plan.out_specs(),
        out_shape=[sd((s_len, ATTN_W), _MXU), sd((s_len, KV_W), _MXU), sd((s_len, KV_W), _MXU),
                   sd((s_len, SGU_W), F32), sd((s_len, SGU_W), F32),
                   sd((s_len, 128), F32), sd((s_len, 128), F32), sd((s_len, 128), F32)] + plan.out_shapes(),
        scratch_shapes=plan.scratch(),
        compiler_params=_params(56),
    )(x, pos_row, g0, b0, w_in, b_in, _rope_consts(), *plan.operands())


def _band_mask_t(first_block):
    kj = lax.broadcasted_iota(jnp.int32, (2 * BLK, BLK), 0)
    qi = lax.broadcasted_iota(jnp.int32, (2 * BLK, BLK), 1)
    shut = jnp.where(first_block, 2 * BLK, 0)
    prev_ok = jnp.logical_and(kj < BLK, kj > qi + shut)
    cur_ok = jnp.logical_and(kj >= BLK, (kj - BLK) <= qi)
    return jnp.logical_or(prev_ok, cur_ok)


def _attn_probs_t(kh, qh, sink, allowed_t):
    s = jnp.where(allowed_t, _dot_nt(kh, qh), -1e30)
    m = jnp.maximum(jnp.max(s, axis=0, keepdims=True), sink)
    p = jnp.exp(s - m)
    ps = jnp.exp(sink - m)
    inv = 1.0 / (jnp.sum(p, axis=0, keepdims=True) + ps)
    return p * inv, ps * inv


def _sgu_mix(gv, lg, lb, w_ref, bt_ref):
    vv, vhat, rstd = _ln(gv, lg, lb)
    vvb = vv.astype(_MXU)
    wcs, mixed = [], []
    for h in range(N_GRP):
        wc = _causal_w(w_ref, h).astype(_MXU)
        wcs.append(wc)
        mixed.append(_dot(wc, vvb[:, h * GRP_DIM:(h + 1) * GRP_DIM]) + bt_ref[:, h:h + 1])
    return jnp.concatenate(mixed, axis=1), vhat, rstd, vvb, wcs


def _mixer_fwd(q, k, v, su, sv, sinks, sg, sb, sgu_w, sgu_bt, plan):
    s_len = q.shape[0]
    nb = s_len // BLK
    per = 2 if nb % 2 == 0 else 1
    steps = nb // per
    m, n = len(plan.operands()), plan.n

    def body(q_ref, kc_ref, kp_ref, vc_ref, vp_ref, su_ref, sv_ref, sink_ref, lg_ref, lb_ref, w_ref, bt_ref, *rest):
        mc_ref = rest[m]
        gather = plan.bind(rest[:m], rest[m + 1:m + 1 + n], rest[m + 1 + n:])
        i = pl.program_id(0)

        @pl.when(i == 0)
        def _():
            gather.start()

        @pl.when(i == max(steps - 2, 0))
        def _():
            gather.pass_on()

        @pl.when(i == steps - 1)
        def _():
            gather.finish()

        for s in range(per):
            rows = slice(s * BLK, (s + 1) * BLK)
            before = slice((s - 1) * BLK, s * BLK)
            k_prev = kp_ref[...] if s == 0 else kc_ref[before, :]
            v_prev = vp_ref[...] if s == 0 else vc_ref[before, :]
            allowed_t = _band_mask_t(i == 0 if s == 0 else False)
            kb = jnp.concatenate([k_prev, kc_ref[rows, :]], axis=0)
            vb = jnp.concatenate([v_prev, vc_ref[rows, :]], axis=0)
            qv = q_ref[rows, :]
            outs = []
            allowed_g = jnp.tile(allowed_t, (1, Q_PER_KV))
            for g in range(N_KV):
                heads = range(g * Q_PER_KV, (g + 1) * Q_PER_KV)
                kh = kb[:, g * HEAD_DIM:(g + 1) * HEAD_DIM]
                vh = vb[:, g * HEAD_DIM:(g + 1) * HEAD_DIM]
                q_g = jnp.concatenate([qv[:, h * HEAD_DIM:(h + 1) * HEAD_DIM] for h in heads], axis=0)
                sink_g = jnp.concatenate([jnp.full((1, BLK), sink_ref[h], F32) for h in heads], axis=1)
                probs_t, _ = _attn_probs_t(kh, q_g, sink_g, allowed_g)
                o_g = _dot_tn(probs_t.astype(_MXU), vh)
                outs += [o_g[hh * BLK:(hh + 1) * BLK, :] for hh in range(Q_PER_KV)]
            u = _gelu_and_grad(su_ref[rows, :])[0]
            gv = _gelu_and_grad(sv_ref[rows, :])[0]
            mixed = _sgu_mix(gv, lg_ref[...], lb_ref[...], w_ref, bt_ref)[0]
            mc_ref[rows, :] = jnp.concatenate(outs + [u * mixed], axis=1).astype(_MXU)

    cur = lambda w: pl.BlockSpec((per * BLK, w), lambda i: (i, 0))
    prev = lambda w: pl.BlockSpec((BLK, w), lambda i: (jnp.maximum(per * i - 1, 0), 0))
    return pl.pallas_call(
        body, name="mixer_fwd", grid=(steps,),
        in_specs=[cur(ATTN_W), cur(KV_W), prev(KV_W), cur(KV_W), prev(KV_W), cur(SGU_W), cur(SGU_W), _smem(),
                  _const2((1, SGU_W)), _const2((1, SGU_W)), _const2((N_GRP, BLK, BLK)), _const2((BLK, N_GRP))]
        + plan.in_specs(),
        out_specs=[cur(D_MODEL)] + plan.out_specs(),
        out_shape=[_hbm_shape((s_len, D_MODEL), _MXU)] + plan.out_shapes(),
        scratch_shapes=plan.scratch(),
        compiler_params=_params(56),
    )(q, k, k, v, v, su, sv, sinks, sg, sb, sgu_w, sgu_bt, *plan.operands())


def _outproj(mc, w_out, b_out, x, g0, b0, plan):
    s_len = x.shape[0]
    tm = _tile(s_len, 512)
    m, n = len(plan.operands()), plan.n

    def body(mc_ref, w_ref, bo_ref, x_ref, g_ref, b_ref, *rest):
        r1_ref = rest[m]
        gather = plan.bind(rest[:m], rest[m + 1:m + 1 + n], rest[m + 1 + n:])
        i = pl.program_id(0)

        @pl.when(i == 0)
        def _():
            gather.start()

        h0, _, _ = _ln(x_ref[...], g_ref[...], b_ref[...])
        r1_ref[...] = ALPHA * h0 + (_dot(mc_ref[...], w_ref[...]) + bo_ref[...])

        last = pl.num_programs(0) - 1

        @pl.when(i == jnp.maximum(last - 1, 0))
        def _():
            gather.pass_on()

        @pl.when(i == last)
        def _():
            gather.finish()

    return pl.pallas_call(
        body, name="outproj", grid=(s_len // tm,),
        in_specs=[_rows(tm, D_MODEL), _vmem(), _const2((1, D_MODEL)), _rows(tm, D_MODEL),
                  _const2((1, D_MODEL)), _const2((1, D_MODEL))] + plan.in_specs(),
        out_specs=[_rows(tm, D_MODEL)] + plan.out_specs(),
        out_shape=[_hbm_shape((s_len, D_MODEL), F32)] + plan.out_shapes(),
        scratch_shapes=plan.scratch(),
        compiler_params=_params(40),
    )(mc, w_out, b_out, x, g0, b0, *plan.operands())


def _ffn_spec(tm):
    return pl.BlockSpec((N_CHIP, tm, FF_SH), lambda i: (0, i, 0))


def _ffn_up(r1, g1, b1, wg, wu, plan):
    s_len = r1.shape[0]
    tm = _tile(s_len, 512)
    m, n = len(plan.operands()), plan.n

    def body(r1_ref, g_ref, b_ref, wg_ref, wu_ref, *rest):
        a_ref, p_ref, q_ref, h1_ref = rest[m:m + 4]
        gather = plan.bind(rest[:m], rest[m + 4:m + 4 + n], rest[m + 4 + n:])
        i = pl.program_id(0)

        @pl.when(i == 0)
        def _():
            gather.start()

        h1, _, _ = _ln(r1_ref[...], g_ref[...], b_ref[...])
        h1_ref[...] = h1
        h1b = h1.astype(_MXU)
        for j in range(N_CHIP):
            g = _dot_nt(h1b, wg_ref[j])
            u = _dot_nt(h1b, wu_ref[j])
            silu, sg = _silu_parts(g)
            a_ref[j] = (silu * u).astype(_MXU)
            p_ref[j] = silu.astype(_ACT)
            q_ref[j] = (u * (sg * (1.0 + g * (1.0 - sg)))).astype(_ACT)

        last = pl.num_programs(0) - 1

        @pl.when(i == jnp.maximum(last - 1, 0))
        def _():
            gather.pass_on()

        @pl.when(i == last)
        def _():
            gather.finish()

    sd = _hbm_shape((N_CHIP, s_len, FF_SH), _ACT)
    return pl.pallas_call(
        body, name="ffn_up", grid=(s_len // tm,),
        in_specs=[_rows(tm, D_MODEL), _const2((1, D_MODEL)), _const2((1, D_MODEL)), _vmem(), _vmem()] + plan.in_specs(),
        out_specs=[_ffn_spec(tm)] * 3 + [_rows(tm, D_MODEL)] + plan.out_specs(),
        out_shape=[_hbm_shape((N_CHIP, s_len, FF_SH), _MXU), sd, sd, _hbm_shape((s_len, D_MODEL), F32)]
        + plan.out_shapes(),
        scratch_shapes=plan.scratch(),
        compiler_params=_params(56),
    )(r1, g1, b1, wg, wu, *plan.operands())


def _silu_parts(g):
    sg = 1.0 / (1.0 + jnp.exp(-g))
    return g * sg, sg


def _ffn_down_loss(act, wd, h1, g2, b2, target):
    s_len = h1.shape[0]
    tm = _tile(s_len, 512)

    parts = 2 if tm % 32 == 0 else 1
    sub = tm // parts

    def body(a_ref, wd_ref, h1_ref, g2_ref, b2_ref, t_ref, dr2_ref, loss_ref, dg2_ref, db2_ref):
        i = pl.program_id(0)

        @pl.when(i == 0)
        def _():
            loss_ref[...] = jnp.zeros_like(loss_ref)
            dg2_ref[...] = jnp.zeros_like(dg2_ref)
            db2_ref[...] = jnp.zeros_like(db2_ref)

        for part in range(parts):
            rows = slice(part * sub, (part + 1) * sub)
            f = jnp.zeros((sub, D_MODEL), F32)
            for j in range(N_CHIP):
                f = f + _dot(a_ref[j, rows, :], wd_ref[j])
            h2, r2hat, rstd2 = _ln(ALPHA * h1_ref[rows, :] + f, g2_ref[...], b2_ref[...])
            diff = h2 - t_ref[rows, :]
            dh2 = diff * (1.0 / D_MODEL)
            loss_ref[...] += _colsum(diff * diff)
            dg2_ref[...] += _colsum(dh2 * r2hat)
            db2_ref[...] += _colsum(dh2)
            dr2_ref[rows, :] = _ln_bwd(dh2, r2hat, rstd2, g2_ref[...])

    vec = _hbm_shape((1, D_MODEL), F32)
    c = _const2((1, D_MODEL))
    return pl.pallas_call(
        body, name="ffn_down_loss", grid=(s_len // tm,),
        in_specs=[_ffn_spec(tm), _vmem(), _rows(tm, D_MODEL), c, c, _rows(tm, D_MODEL)],
        out_specs=[_rows(tm, D_MODEL), c, c, c],
        out_shape=[_hbm_shape((s_len, D_MODEL), F32), vec, vec, vec],
        compiler_params=_params(48),
    )(act, wd, h1, g2, b2, target)


def _ffn_bwd_a(dr2, act, p_act, q_act, wd):
    s_len = dr2.shape[0]
    tm = _tile(s_len, 512)

    def body(dr2_ref, a_ref, p_ref, q_ref, wd_ref, dg_ref, du_ref, wire_ref, own_ref,
             dwd_ref, land_ref, send_sem, recv_sem):
        i = pl.program_id(0)

        @pl.when(i == 0)
        def _():
            dwd_ref[...] = jnp.zeros_like(dwd_ref)

        dfb = dr2_ref[...].astype(_MXU)
        for j in range(N_CHIP):
            da = _dot_nt(dfb, wd_ref[j])
            dg_ref[j] = (da * q_ref[j].astype(F32)).astype(_MXU)
            du_ref[j] = (da * p_ref[j].astype(F32)).astype(_MXU)
            dwd_ref[j * FF_SH:(j + 1) * FF_SH, :] += _dot_tn(a_ref[j], dfb)

        @pl.when(i == pl.num_programs(0) - 1)
        def _():
            _pair_reduce(dwd_ref, wire_ref, own_ref, land_ref, send_sem, recv_sem)

    sd = _hbm_shape((N_CHIP, s_len, FF_SH), _MXU)
    half = (N_CHIP, FF_SH // 2, D_MODEL)
    return pl.pallas_call(
        body, name="ffn_bwd_a", grid=(s_len // tm,),
        in_specs=[_rows(tm, D_MODEL), _ffn_spec(tm), _ffn_spec(tm), _ffn_spec(tm), _vmem()],
        out_specs=[_ffn_spec(tm), _ffn_spec(tm), _vmem(), _vmem()],
        out_shape=[sd, sd] + _pair_out_shapes(half),
        scratch_shapes=_pair_scratch((D_FF, D_MODEL), half),
        compiler_params=_params(61),
    )(dr2, act, p_act, q_act, wd)


def _ffn_bwd_g(dr2, dg, r1, g1, b1, wg, prev_wire):
    s_len = dr2.shape[0]
    tm = _tile(s_len, 512)

    def body(dr2_ref, dg_ref, r1_ref, g1_ref, b1_ref, wg_ref, pw_ref, dh1_ref, wire_ref, own_ref, pl_ref,
             dwg_ref, land_ref, send_sem, recv_sem, xl_ref, x_send, x_recv, x_flush):
        i = pl.program_id(0)
        exchange = _ChipExchange(pw_ref, xl_ref, x_send, x_recv)

        @pl.when(i == 0)
        def _():
            exchange.start()
            dwg_ref[...] = jnp.zeros_like(dwg_ref)

        h1, _, _ = _ln(r1_ref[...], g1_ref[...], b1_ref[...])
        h1b = h1.astype(_MXU)
        dh1 = ALPHA * dr2_ref[...]
        for j in range(N_CHIP):
            dgj = dg_ref[j]
            dh1 = dh1 + _dot(dgj, wg_ref[j])
            dwg_ref[j * FF_SH:(j + 1) * FF_SH, :] += _dot_tn(dgj, h1b)
        dh1_ref[...] = dh1

        @pl.when(i == pl.num_programs(0) - 1)
        def _():
            _pair_reduce(dwg_ref, wire_ref, own_ref, land_ref, send_sem, recv_sem)
            exchange.finish_to(pl_ref, x_flush)

    c = _const2((1, D_MODEL))
    half = (N_CHIP, FF_SH // 2, D_MODEL)
    return pl.pallas_call(
        body, name="ffn_bwd_g", grid=(s_len // tm,),
        in_specs=[_rows(tm, D_MODEL), _ffn_spec(tm), _rows(tm, D_MODEL), c, c, _vmem(), _vmem()],
        out_specs=[_rows(tm, D_MODEL), _vmem(), _vmem(), _hbm()],
        out_shape=[_hbm_shape((s_len, D_MODEL), F32)] + _pair_out_shapes(half) + [_ChipExchange.land_shape(prev_wire)],
        scratch_shapes=_pair_scratch((D_FF, D_MODEL), half) + _ChipExchange.scratch(prev_wire),
        compiler_params=_params(58),
    )(dr2, dg, r1, g1, b1, wg, prev_wire)


def _ffn_bwd_u(dh1a, du, r1, g1, b1, wu, prev_wire):
    s_len = dh1a.shape[0]
    tm = _tile(s_len, 512)

    def body(dh1_ref, du_ref, r1_ref, g1_ref, b1_ref, wu_ref, pw_ref,
             dr1_ref, wire_ref, own_ref, dg1_ref, db1_ref, pl_ref,
             dwu_ref, land_ref, send_sem, recv_sem, xl_ref, x_send, x_recv, x_flush):
        i = pl.program_id(0)
        exchange = _ChipExchange(pw_ref, xl_ref, x_send, x_recv)

        @pl.when(i == 0)
        def _():
            exchange.start()
            dwu_ref[...] = jnp.zeros_like(dwu_ref)
            dg1_ref[...] = jnp.zeros_like(dg1_ref)
            db1_ref[...] = jnp.zeros_like(db1_ref)

        h1, r1hat, rstd1 = _ln(r1_ref[...], g1_ref[...], b1_ref[...])
        h1b = h1.astype(_MXU)
        dh1 = dh1_ref[...]
        for j in range(N_CHIP):
            duj = du_ref[j]
            dh1 = dh1 + _dot(duj, wu_ref[j])
            dwu_ref[j * FF_SH:(j + 1) * FF_SH, :] += _dot_tn(duj, h1b)
        dg1_ref[...] += _colsum(dh1 * r1hat)
        db1_ref[...] += _colsum(dh1)
        dr1_ref[...] = _ln_bwd(dh1, r1hat, rstd1, g1_ref[...])

        @pl.when(i == pl.num_programs(0) - 1)
        def _():
            _pair_reduce(dwu_ref, wire_ref, own_ref, land_ref, send_sem, recv_sem)
            exchange.finish_to(pl_ref, x_flush)

    vec = _hbm_shape((1, D_MODEL), F32)
    c = _const2((1, D_MODEL))
    half = (N_CHIP, FF_SH // 2, D_MODEL)
    return pl.pallas_call(
        body, name="ffn_bwd_u", grid=(s_len // tm,),
        in_specs=[_rows(tm, D_MODEL), _ffn_spec(tm), _rows(tm, D_MODEL), c, c, _vmem(), _vmem()],
        out_specs=[_rows(tm, D_MODEL), _vmem(), _vmem(), c, c, _hbm()],
        out_shape=[_hbm_shape((s_len, D_MODEL), F32)] + _pair_out_shapes(half)
        + [vec, vec, _ChipExchange.land_shape(prev_wire)],
        scratch_shapes=_pair_scratch((D_FF, D_MODEL), half) + _ChipExchange.scratch(prev_wire),
        compiler_params=_params(58),
    )(dh1a, du, r1, g1, b1, wu, prev_wire)


def _outproj_bwd(dr1, mc, w_out):
    s_len = dr1.shape[0]
    tm = _tile(s_len, 512)

    def body(dr1_ref, mc_ref, w_ref, dmc_ref, wire_ref, own_ref, db_ref, dw_ref, land_ref, send_sem, recv_sem):
        i = pl.program_id(0)

        @pl.when(i == 0)
        def _():
            dw_ref[...] = jnp.zeros_like(dw_ref)
            db_ref[...] = jnp.zeros_like(db_ref)

        d = dr1_ref[...]
        db_ref[...] += _colsum(d)
        db16 = d.astype(_MXU)
        dmc_ref[...] = _dot_nt(db16, w_ref[...])
        dw_ref[...] += _dot_tn(mc_ref[...], db16)

        @pl.when(i == pl.num_programs(0) - 1)
        def _():
            _pair_reduce(dw_ref, wire_ref, own_ref, land_ref, send_sem, recv_sem)

    half = (N_CHIP, OUT_SH // 2, D_MODEL)
    return pl.pallas_call(
        body, name="outproj_bwd", grid=(s_len // tm,),
        in_specs=[_rows(tm, D_MODEL), _rows(tm, D_MODEL), _vmem()],
        out_specs=[_rows(tm, D_MODEL), _vmem(), _vmem(), _const2((1, D_MODEL))],
        out_shape=[_hbm_shape((s_len, D_MODEL), F32)] + _pair_out_shapes(half) + [_hbm_shape((1, D_MODEL), F32)],
        scratch_shapes=_pair_scratch((D_MODEL, D_MODEL), half),
        compiler_params=_params(48),
    )(dr1, mc, w_out)


def _mixer_bwd(q, k, v, su, sv, dmc, tc, t1, t2, sinks, sg, sb, sgu_w, sgu_bt, prev_wires):
    s_len = q.shape[0]
    nb = s_len // BLK
    per = next(p for p in (4, 2, 1) if nb % p == 0)
    steps = nb // per

    def body(q_ref, kc_ref, kp_ref, vc_ref, vp_ref, su_ref, sv_ref, dmc_ref,
             tc_ref, t1_ref, t2_ref, tcp_ref, t1p_ref, t2p_ref,
             sink_ref, lg_ref, lb_ref, w_ref, bt_ref, pw0_ref, pw1_ref,
             dq_ref, dkv_ref, dsuv_ref, dbq_ref, dbkv_ref, dbsuv_ref,
             dsink_ref, dlg_ref, dlb_ref, dw_ref, dbt_ref, pl0_ref, pl1_ref, carry_ref,
             xl0_ref, x0_send, x0_recv, x0_flush, xl1_ref, x1_send, x1_recv, x1_flush):
        i = pl.program_id(0)
        exchanges = [(_ChipExchange(pw0_ref, xl0_ref, x0_send, x0_recv), pl0_ref, x0_flush),
                     (_ChipExchange(pw1_ref, xl1_ref, x1_send, x1_recv), pl1_ref, x1_flush)]

        @pl.when(i == 0)
        def _():
            for exchange, _, _ in exchanges:
                exchange.start()

        @pl.when(i == 0)
        def _():
            for r in (dbq_ref, dbkv_ref, dbsuv_ref, dsink_ref, dlg_ref, dlb_ref, dw_ref, dbt_ref, carry_ref):
                r[...] = jnp.zeros_like(r)

        def emit_kv(fin, t):
            if t == 0:
                tables = (tcp_ref[...], t1p_ref[...], t2p_ref[...])
            else:
                before = slice((t - 1) * BLK, t * BLK)
                tables = (tc_ref[before, :], t1_ref[before, :], t2_ref[before, :])
            dk = _rope_bwd(fin[:, 0:KV_W], *tables)
            out = jnp.concatenate([dk, fin[:, KV_W:2 * KV_W]], axis=1)
            dkv_ref[t * BLK:(t + 1) * BLK, :] = out.astype(_MXU)
            dbkv_ref[...] += _colsum(out)

        def one_block(s):
            rows = slice(s * BLK, (s + 1) * BLK)
            before = slice((s - 1) * BLK, s * BLK)
            k_prev = kp_ref[...] if s == 0 else kc_ref[before, :]
            v_prev = vp_ref[...] if s == 0 else vc_ref[before, :]
            allowed_t = _band_mask_t(i == 0 if s == 0 else False)
            kb = jnp.concatenate([k_prev, kc_ref[rows, :]], axis=0)
            vb = jnp.concatenate([v_prev, vc_ref[rows, :]], axis=0)
            qv = q_ref[rows, :]
            dmc = dmc_ref[rows, :]
            dqs, dks, dvs, dsinks = [], [], [], []
            allowed_g = jnp.tile(allowed_t, (1, Q_PER_KV))
            for g in range(N_KV):
                heads = range(g * Q_PER_KV, (g + 1) * Q_PER_KV)
                kh = kb[:, g * HEAD_DIM:(g + 1) * HEAD_DIM]
                vh = vb[:, g * HEAD_DIM:(g + 1) * HEAD_DIM]
                q_g = jnp.concatenate([qv[:, h * HEAD_DIM:(h + 1) * HEAD_DIM] for h in heads], axis=0)
                do_g = jnp.concatenate([dmc[:, h * HEAD_DIM:(h + 1) * HEAD_DIM] for h in heads], axis=0).astype(_MXU)
                sink_g = jnp.concatenate([jnp.full((1, BLK), sink_ref[h], F32) for h in heads], axis=1)
                probs_t, psink = _attn_probs_t(kh, q_g, sink_g, allowed_g)
                dvs.append(_dot(probs_t.astype(_MXU), do_g))
                dp_t = _dot_nt(vh, do_g)
                rd = jnp.sum(probs_t * dp_t, axis=0, keepdims=True)
                ds_t = (probs_t * (dp_t - rd)).astype(_MXU)
                ps_rd = psink * rd
                for hh in range(Q_PER_KV):
                    dsinks.append(-jnp.sum(ps_rd[:, hh * BLK:(hh + 1) * BLK], axis=1, keepdims=True))
                dq_g = _dot_tn(ds_t, kh)
                dqs += [dq_g[hh * BLK:(hh + 1) * BLK, :] for hh in range(Q_PER_KV)]
                dks.append(_dot(ds_t, q_g))
            dq = _rope_bwd(jnp.concatenate(dqs, axis=1) * (HEAD_DIM ** -0.5),
                           tc_ref[rows, :], t1_ref[rows, :], t2_ref[rows, :])
            dq_ref[rows, :] = dq.astype(_MXU)
            dbq_ref[...] += _colsum(dq)
            dsink_ref[...] += _lane_put(dsinks, 128)
            contrib = jnp.concatenate(dks + dvs, axis=1)

            lg = lg_ref[...]
            u, du_dsu = _gelu_and_grad(su_ref[rows, :])
            gv, dgv_dsv = _gelu_and_grad(sv_ref[rows, :])
            mixed, vhat, rstd, vvb, wcs = _sgu_mix(gv, lg, lb_ref[...], w_ref, bt_ref)
            dsgu = dmc[:, ATTN_W:D_MODEL]
            dsu = dsgu * mixed * du_dsu
            dmixed = dsgu * u
            tri_t = lax.broadcasted_iota(jnp.int32, (BLK, BLK), 0)
            tri_s = lax.broadcasted_iota(jnp.int32, (BLK, BLK), 1)
            dvv, dbs = [], []
            for h in range(N_GRP):
                dm = dmixed[:, h * GRP_DIM:(h + 1) * GRP_DIM]
                dmb = dm.astype(_MXU)
                dbs.append(jnp.sum(dm, axis=1, keepdims=True))
                dw_ref[h] += jnp.where(tri_s <= tri_t, _dot_nt(dmb, vvb[:, h * GRP_DIM:(h + 1) * GRP_DIM]), 0.0)
                dvv.append(_dot_tn(wcs[h], dmb))
            dvv = jnp.concatenate(dvv, axis=1)
            dbt_ref[...] += _lane_put(dbs, 128)
            dlg_ref[...] += _colsum(dvv * vhat)
            dlb_ref[...] += _colsum(dvv)
            dsv = _ln_bwd(dvv, vhat, rstd, lg) * dgv_dsv
            dsuv = jnp.concatenate([dsu, dsv], axis=1)
            dsuv_ref[rows, :] = dsuv.astype(_MXU)
            dbsuv_ref[...] += _colsum(dsuv)
            return contrib

        @pl.when(i < steps)
        def _():
            contribs = [one_block(s) for s in range(per)]
            for t in range(per):
                top = carry_ref[...] if t == 0 else contribs[t - 1][BLK:2 * BLK, :]
                emit_kv(top + contribs[t][0:BLK, :], t)
            carry_ref[...] = contribs[per - 1][BLK:2 * BLK, :]

        @pl.when(i == steps)
        def _():
            emit_kv(carry_ref[...], 0)
            if per > 1:
                dkv_ref[BLK:per * BLK, :] = jnp.zeros(((per - 1) * BLK, 2 * KV_W), _MXU)
            for exchange, landed, flush_sem in exchanges:
                exchange.finish_to(landed, flush_sem)

    last = steps - 1
    cur = lambda w: pl.BlockSpec((per * BLK, w), lambda i: (jnp.minimum(i, last), 0))
    prev = lambda w: pl.BlockSpec((BLK, w), lambda i: (jnp.clip(per * i - 1, 0, nb - 1), 0))
    shifted = pl.BlockSpec((per * BLK, 2 * KV_W), lambda i: (i, 0))
    sd = _hbm_shape
    return pl.pallas_call(
        body, name="mixer_bwd", grid=(steps + 1,),
        in_specs=[cur(ATTN_W), cur(KV_W), prev(KV_W), cur(KV_W), prev(KV_W), cur(SGU_W), cur(SGU_W), cur(D_MODEL),
                  cur(128), cur(128), cur(128), prev(128), prev(128), prev(128),
                  _smem(), _const2((1, SGU_W)), _const2((1, SGU_W)), _const2((N_GRP, BLK, BLK)), _const2((BLK, N_GRP)),
                  _vmem(), _vmem()],
        out_specs=[cur(ATTN_W), shifted, cur(2 * SGU_W),
                   _const2((1, ATTN_W)), _const2((1, 2 * KV_W)), _const2((1, 2 * SGU_W)),
                   _const2((1, 128)), _const2((1, SGU_W)), _const2((1, SGU_W)),
                   _const2((N_GRP, BLK, BLK)), _const2((BLK, 128)), _hbm(), _hbm()],
        out_shape=[sd((s_len, ATTN_W), _MXU), sd((s_len + per * BLK, 2 * KV_W), _MXU), sd((s_len, 2 * SGU_W), _MXU),
                   sd((1, ATTN_W), F32), sd((1, 2 * KV_W), F32), sd((1, 2 * SGU_W), F32),
                   sd((1, 128), F32), sd((1, SGU_W), F32), sd((1, SGU_W), F32),
                   sd((N_GRP, BLK, BLK), F32), sd((BLK, 128), F32)]
        + [_ChipExchange.land_shape(w) for w in prev_wires],
        scratch_shapes=[pltpu.VMEM((BLK, 2 * KV_W), F32)] + _ChipExchange.scratch(prev_wires[0])
        + _ChipExchange.scratch(prev_wires[1]),
        compiler_params=_params(40),
    )(q, k, k, v, v, su, sv, dmc, tc, t1, t2, tc, t1, t2, sinks, sg, sb, sgu_w, sgu_bt, *prev_wires)


def _inproj_bwd(dq, dkv, dsuv, dr1, x, g0, b0, w_in):
    s_len = x.shape[0]
    tm = _tile(s_len, 512)
    cuts = ((0, ATTN_W), (ATTN_W, ATTN_W + 2 * KV_W), (ATTN_W + 2 * KV_W, IN_W))

    def body(dq_ref, dkv_ref, dsuv_ref, dr1_ref, x_ref, g_ref, b_ref, w_ref, dx_ref, dw_ref, dg_ref, db_ref):
        i = pl.program_id(0)

        @pl.when(i == 0)
        def _():
            dw_ref[...] = jnp.zeros_like(dw_ref)
            dg_ref[...] = jnp.zeros_like(dg_ref)
            db_ref[...] = jnp.zeros_like(db_ref)

        h0, xhat, rstd = _ln(x_ref[...], g_ref[...], b_ref[...])
        h0b = h0.astype(_MXU)
        dh0 = ALPHA * dr1_ref[...]
        for (lo, hi), d_ref in zip(cuts, (dq_ref, dkv_ref, dsuv_ref)):
            d = d_ref[...]
            dh0 = dh0 + _dot(d, w_ref[lo:hi, :])
            dw_ref[lo:hi, :] += _dot_tn(d, h0b)
        dg_ref[...] += _colsum(dh0 * xhat)
        db_ref[...] += _colsum(dh0)
        dx_ref[...] = _ln_bwd(dh0, xhat, rstd, g_ref[...])

    vec = _hbm_shape((1, D_MODEL), F32)
    c = _const2((1, D_MODEL))
    return pl.pallas_call(
        body, name="inproj_bwd", grid=(s_len // tm,),
        in_specs=[_rows(tm, ATTN_W), _rows(tm, 2 * KV_W), _rows(tm, 2 * SGU_W), _rows(tm, D_MODEL), _rows(tm, D_MODEL),
                  c, c, _vmem()],
        out_specs=[_rows(tm, D_MODEL), _vmem(), c, c],
        out_shape=[_hbm_shape((s_len, D_MODEL), F32), jax.ShapeDtypeStruct((IN_W, D_MODEL), F32), vec, vec],
        compiler_params=_params(48),
    )(dq, dkv, dsuv, dr1, x, g0, b0, w_in)


def _place():
    x, y, c = (lax.axis_index(a) for a in MESH_AXES)
    chips = [(1 - x, y), (x, 1 - y), (1 - x, 1 - y)]
    return x, y, c, chips


class _Gather:
    def __init__(self, ins, outs, send_sems, recv_sems, spans=None):
        self.ins, self.outs, self.send_sems, self.recv_sems = ins, outs, send_sems, recv_sems
        self.n = len(ins)
        self.spans = spans or [(0, r.shape[0]) for r in ins]
        self.halves = [(hi - lo) // 2 for lo, hi in self.spans]

    def _copy(self, k, t, slot, half, to):
        rows = pl.ds(pl.multiple_of(self.spans[t][0] + half * self.halves[t], 16), self.halves[t])
        piece = self.outs[t].at[slot, rows, :]
        return pltpu.make_async_remote_copy(src_ref=piece, dst_ref=piece, send_sem=self.send_sems.at[k],
                                            recv_sem=self.recv_sems.at[k], device_id=to, device_id_type=MESH)

    def _chip_copy(self, t, d, slot):
        x, y, c, chips = _place()
        return self._copy(3 * t + d, t, slot, c, (chips[d][0], chips[d][1], c))

    def _pass_copy(self, t, d, half):
        x, y, c, chips = _place()
        return self._copy(3 * self.n + 3 * t + d, t, 2 * chips[d][0] + chips[d][1], half, (x, y, 1 - c))

    def start(self):
        x, y, c, chips = _place()
        me = 2 * x + y
        for t in range(self.n):
            lo, hi = self.spans[t]
            self.outs[t][me, lo:hi, :] = self.ins[t][lo:hi, :].astype(_WIRE)
        for t in range(self.n):
            for d in range(3):
                self._chip_copy(t, d, me).start()

    def pass_on(self):
        x, y, c, chips = _place()
        for t in range(self.n):
            for d in range(3):
                self._chip_copy(t, d, 2 * chips[d][0] + chips[d][1]).wait_recv()
                self._pass_copy(t, d, c).start()

    def finish(self):
        x, y, c, chips = _place()
        me = 2 * x + y
        for t in range(self.n):
            for d in range(3):
                self._pass_copy(t, d, 1 - c).wait_recv()
        for t in range(self.n):
            for d in range(3):
                self._chip_copy(t, d, me).wait_send()
                self._pass_copy(t, d, c).wait_send()

    @staticmethod
    def out_shapes(shards, make=jax.ShapeDtypeStruct):
        return [make((N_CHIP,) + s.shape, _WIRE) for s in shards]

    @staticmethod
    def sems(n):
        return [pltpu.SemaphoreType.DMA((6 * n,)), pltpu.SemaphoreType.DMA((6 * n,))]


class _GatherPlan:
    def __init__(self, pieces):
        self.shards = [p[0] for p in pieces]
        self.spans = [p[1] for p in pieces]
        self.earlier = [p[2] for p in pieces]
        self.n = len(pieces)
        self.carried = [t for t in range(self.n) if self.earlier[t] is not None]

    def operands(self):
        return self.shards + [self.earlier[t] for t in self.carried]

    def in_specs(self):
        return [_vmem()] * self.n + [_hbm()] * len(self.carried)

    def out_specs(self):
        return [_hbm()] * self.n

    def out_shapes(self):
        return _Gather.out_shapes(self.shards, _hbm_shape)

    def scratch(self):
        return ([pltpu.VMEM((N_CHIP,) + s.shape, _WIRE) for s in self.shards] + _Gather.sems(self.n)
                + [pltpu.SemaphoreType.DMA((self.n,)), pltpu.SemaphoreType.DMA((max(len(self.carried), 1),))])

    def bind(self, in_refs, out_refs, scratch_refs):
        plan = self
        shard_refs, earlier_refs = in_refs[:self.n], in_refs[self.n:]
        bufs = scratch_refs[:self.n]
        send_sems, recv_sems, flush_sems, carry_sems = scratch_refs[self.n:self.n + 4]
        gather = _Gather(shard_refs, bufs, send_sems, recv_sems, self.spans)

        def carry_copy(k):
            t = plan.carried[k]
            lo = plan.spans[t][0]
            return pltpu.make_async_copy(earlier_refs[k].at[:, 0:lo, :], bufs[t].at[:, 0:lo, :], carry_sems.at[k])

        class Bound:
            @staticmethod
            def start():
                for k in range(len(plan.carried)):
                    carry_copy(k).start()
                gather.start()

            @staticmethod
            def pass_on():
                gather.pass_on()

            @staticmethod
            def finish():
                gather.finish()
                for k in range(len(plan.carried)):
                    carry_copy(k).wait()
                _flush([bufs[t].at[:, 0:plan.spans[t][1], :] for t in range(plan.n)],
                       [out_refs[t].at[:, 0:plan.spans[t][1], :] for t in range(plan.n)], flush_sems)

        return Bound


def _flush(bufs, hbm_outs, sems):
    copies = [pltpu.make_async_copy(b, o, sems.at[k]) for k, (b, o) in enumerate(zip(bufs, hbm_outs))]
    for cp in copies:
        cp.start()
    for cp in copies:
        cp.wait()


def _gather_weights(shards):
    n = len(shards)

    def body(*refs):
        gather = _Gather(refs[:n], refs[n:2 * n], refs[2 * n], refs[2 * n + 1])
        gather.start()
        gather.pass_on()
        gather.finish()

    return pl.pallas_call(
        body, name="gather_weights",
        in_specs=[_vmem()] * n, out_specs=[_vmem()] * n,
        out_shape=_Gather.out_shapes(shards), scratch_shapes=_Gather.sems(n),
        compiler_params=pltpu.CompilerParams(vmem_limit_bytes=32 * MIB),
    )(*shards)


class _ChipExchange:
    def __init__(self, wire_ref, land_ref, send_sems, recv_sems):
        self.wire, self.land, self.send_sems, self.recv_sems = wire_ref, land_ref, send_sems, recv_sems

    def _copy(self, d):
        x, y, c, chips = _place()
        return pltpu.make_async_remote_copy(
            src_ref=self.wire.at[2 * chips[d][0] + chips[d][1]], dst_ref=self.land.at[d],
            send_sem=self.send_sems.at[d], recv_sem=self.recv_sems.at[d],
            device_id=(chips[d][0], chips[d][1], c), device_id_type=MESH)

    def start(self):
        for d in range(3):
            self._copy(d).start()

    def wait_recv(self):
        for d in range(3):
            self._copy(d).wait_recv()

    def wait_send(self):
        for d in range(3):
            self._copy(d).wait_send()

    def finish_to(self, hbm_out, flush_sem):
        self.wait_recv()
        _flush([self.land], [hbm_out], flush_sem)
        self.wait_send()

    @staticmethod
    def land_shape(wire):
        return _hbm_shape((3,) + wire.shape[1:], wire.dtype)

    @staticmethod
    def sems():
        return [pltpu.SemaphoreType.DMA((3,)), pltpu.SemaphoreType.DMA((3,))]

    @staticmethod
    def scratch(wire):
        return ([pltpu.VMEM((3,) + wire.shape[1:], wire.dtype)] + _ChipExchange.sems() + [pltpu.SemaphoreType.DMA((1,))])


def _pair_out_shapes(half_shape):
    return [jax.ShapeDtypeStruct(half_shape, _WIRE), jax.ShapeDtypeStruct(half_shape[1:], F32)]


def _pair_scratch(acc_shape, half_shape):
    return [pltpu.VMEM(acc_shape, F32), pltpu.VMEM(half_shape, _WIRE),
            pltpu.SemaphoreType.DMA((N_CHIP,)), pltpu.SemaphoreType.DMA((N_CHIP,))]


def _pair_reduce(acc_ref, wire_ref, own_ref, land_ref, send_sems, recv_sems):
    rh = land_ref.shape[1]
    x, y, c, _ = _place()
    me = 2 * x + y
    copies = []
    for j in range(N_CHIP):
        def cast(r, carry, j=j):
            dst = pl.ds(pl.multiple_of(r * ROW_CHUNK, ROW_CHUNK), ROW_CHUNK)
            src = pl.ds(pl.multiple_of((2 * j + 1 - c) * rh + r * ROW_CHUNK, 8), ROW_CHUNK)
            wire_ref[j, dst, :] = acc_ref[src, :].astype(_WIRE)
            return carry

        lax.fori_loop(0, rh // ROW_CHUNK, cast, 0)
        cp = pltpu.make_async_remote_copy(src_ref=wire_ref.at[j], dst_ref=land_ref.at[j], send_sem=send_sems.at[j],
                                          recv_sem=recv_sems.at[j], device_id=(x, y, 1 - c), device_id_type=MESH)
        cp.start()
        copies.append(cp)
    for j in range(N_CHIP):
        copies[j].wait()

        def chunk(r, carry, j=j):
            theirs = pl.ds(pl.multiple_of(r * ROW_CHUNK, ROW_CHUNK), ROW_CHUNK)
            mine = pl.ds(pl.multiple_of((2 * j + c) * rh + r * ROW_CHUNK, 8), ROW_CHUNK)
            wire_ref[j, theirs, :] = (acc_ref[mine, :] + land_ref[j, theirs, :].astype(F32)).astype(_WIRE)
            return carry

        lax.fori_loop(0, rh // ROW_CHUNK, chunk, 0)

    def own_chunk(r, carry):
        theirs = pl.ds(pl.multiple_of(r * ROW_CHUNK, ROW_CHUNK), ROW_CHUNK)
        mine = pl.ds(pl.multiple_of((2 * me + c) * rh + r * ROW_CHUNK, 8), ROW_CHUNK)
        own_ref[theirs, :] = acc_ref[mine, :] + land_ref[me, theirs, :].astype(F32)
        return carry

    lax.fori_loop(0, rh // ROW_CHUNK, own_chunk, 0)


def _grad_finish(last_acc, lands, owns, small):
    n = len(owns) + 1
    halves = [last_acc.shape[0] // (2 * N_CHIP)] + [w.shape[1] for w in lands]
    widths = [last_acc.shape[1]] + [a.shape[1] for a in owns]
    small_body, small_scratch = _small_allreduce_parts()
    ns = len(small)

    def body(*refs):
        acc0, land, own = refs[0], (None,) + refs[1:n], (None,) + refs[n:2 * n - 1]
        refs = refs[2 * n - 1:]
        small_in, g, small_out = refs[:ns], refs[ns:ns + n], refs[ns + n:ns + n + 2]
        refs = refs[ns + n + 2:]
        pland0, wire0, land0, own0 = refs[0:4]
        p_send, p_recv, x_send, x_recv, pair_send, pair_recv = refs[4:10]
        small_refs = refs[10:]
        land = (land0,) + land[1:]
        own = (own0,) + own[1:]
        x, y, c, chips = _place()
        me = 2 * x + y
        exchange = _ChipExchange(wire0, land0, x_send, x_recv)

        def half_rows(t, half):
            return pl.ds(pl.multiple_of(half * halves[t], 8), halves[t])

        def pair_copy(t, half):
            rows = g[t].at[half_rows(t, half), :]
            return pltpu.make_async_remote_copy(src_ref=rows, dst_ref=rows, send_sem=pair_send.at[t],
                                                recv_sem=pair_recv.at[t], device_id=(x, y, 1 - c), device_id_type=MESH)

        small_rounds = small_body(*small_in, *small_out, *small_refs)
        next(small_rounds)
        _pair_reduce(acc0, wire0, own0, pland0, p_send, p_recv)
        next(small_rounds)
        exchange.start()

        for t in list(range(1, n)) + [0]:
            if t == 0:
                exchange.wait_recv()
            if t == min(2, n - 1):
                next(small_rounds)
            if t == min(4, n - 1):
                next(small_rounds, None)

            def chunk(r, carry, t=t):
                src = pl.ds(pl.multiple_of(r * ROW_CHUNK, ROW_CHUNK), ROW_CHUNK)
                dst = pl.ds(pl.multiple_of(c * halves[t] + r * ROW_CHUNK, 8), ROW_CHUNK)
                s = own[t][src, :]
                for d in range(3):
                    s = s + land[t][d, src, :].astype(F32)
                g[t][dst, :] = s
                return carry

            lax.fori_loop(0, halves[t] // ROW_CHUNK, chunk, 0)
            pair_copy(t, c).start()
        for t in range(n):
            pair_copy(t, 1 - c).wait_recv()
        for t in range(n):
            pair_copy(t, c).wait_send()
        exchange.wait_send()

    half0 = (halves[0], widths[0])
    return pl.pallas_call(
        body, name="grad_finish",
        in_specs=[_vmem()] * (2 * n - 1 + ns), out_specs=[_vmem()] * (n + 2),
        out_shape=[jax.ShapeDtypeStruct((2 * h, w), F32) for h, w in zip(halves, widths)]
        + [jax.ShapeDtypeStruct(s, F32) for s in _SMALL_OUT_DIMS],
        scratch_shapes=[pltpu.VMEM((N_CHIP,) + half0, _WIRE), pltpu.VMEM((N_CHIP,) + half0, _WIRE),
                        pltpu.VMEM((3,) + half0, _WIRE), pltpu.VMEM(half0, F32)]
        + [pltpu.SemaphoreType.DMA((N_CHIP,)), pltpu.SemaphoreType.DMA((N_CHIP,))]
        + _ChipExchange.sems()
        + [pltpu.SemaphoreType.DMA((n,)), pltpu.SemaphoreType.DMA((n,))]
        + small_scratch,
        compiler_params=pltpu.CompilerParams(vmem_limit_bytes=56 * MIB),
    )(last_acc, *lands, *owns, *small)


_SMALL = ("ln_in_g", "ln_in_b", "b_in", "attn_sinks", "sgu_ln_g", "sgu_ln_b", "sgu_w", "sgu_b", "b_out",
          "ln_mix_g", "ln_mix_b", "ln_ffn_g", "ln_ffn_b")
_VEC_ROW = dict(ln_in_g=0, ln_in_b=1, b_in=2, attn_sinks=4, sgu_ln_g=5, sgu_ln_b=6, b_out=7, ln_mix_g=8, ln_mix_b=9,
                ln_ffn_g=10, ln_ffn_b=11)
_LOSS_ROW = 12
_VEC_ROWS = 16
_MAT_ROWS = N_GRP * BLK + BLK


_SMALL_IN = ("ln_in_g", "ln_in_b", "bq", "bkv", "bsuv", "sink", "sgu_ln_g", "sgu_ln_b", "sgu_w", "sgu_bt", "b_out",
             "ln_mix_g", "ln_mix_b", "ln_ffn_g", "ln_ffn_b", "loss")
_SMALL_OUT_DIMS = ((_VEC_ROWS, D_MODEL), (_MAT_ROWS, 128))


def _small_allreduce_parts():
    n_in = len(_SMALL_IN)

    def body(*refs):
        (g_ln_in_g, g_ln_in_b, g_bq, g_bkv, g_bsuv, g_sink, g_sln_g, g_sln_b, g_sw, g_sbt, g_bout,
         g_lmg, g_lmb, g_lfg, g_lfb, g_loss) = refs[:n_in]
        out_a, out_b = refs[n_in:n_in + 2]
        (buf_a, buf_b, pair_a, pair_b, stage_a, stage_b, tot_a, tot_b,
         p1_send, p1_recv, x_send, x_recv, p2_send, p2_recv) = refs[n_in + 2:]
        x, y, c, chips = _place()
        me = 2 * x + y
        sibling = (x, y, 1 - c)
        half_a, half_b = _VEC_ROWS // 2, _MAT_ROWS // 2

        buf_a[...] = jnp.zeros_like(buf_a)
        for row, ref in ((0, g_ln_in_g), (1, g_ln_in_b), (7, g_bout), (8, g_lmg), (9, g_lmb), (10, g_lfg), (11, g_lfb),
                         (_LOSS_ROW, g_loss)):
            buf_a[row:row + 1, :] = ref[...]
        buf_a[2:3, 0:ATTN_W] = g_bq[...]
        buf_a[2:3, ATTN_W:ATTN_W + 2 * KV_W] = g_bkv[...]
        buf_a[2:3, ATTN_W + 2 * KV_W:D_MODEL] = g_bsuv[:, 0:2 * KV_W]
        buf_a[3:4, 0:2 * SGU_W - 2 * KV_W] = g_bsuv[:, 2 * KV_W:2 * SGU_W]
        buf_a[4:5, 0:128] = g_sink[...]
        buf_a[5:6, 0:SGU_W] = g_sln_g[...]
        buf_a[6:7, 0:SGU_W] = g_sln_b[...]
        for h in range(N_GRP):
            buf_b[h * BLK:(h + 1) * BLK, :] = g_sw[h]
        buf_b[N_GRP * BLK:_MAT_ROWS, :] = g_sbt[...]

        def remote(src, dst, send_sem, recv_sem, to):
            return pltpu.make_async_remote_copy(src_ref=src, dst_ref=dst, send_sem=send_sem, recv_sem=recv_sem,
                                                device_id=to, device_id_type=MESH)

        first = [remote(buf_a, pair_a, p1_send.at[0], p1_recv.at[0], sibling),
                 remote(buf_b, pair_b, p1_send.at[1], p1_recv.at[1], sibling)]
        for cp in first:
            cp.start()
        yield
        for cp in first:
            cp.wait()
        rows_a = pl.ds(pl.multiple_of(c * half_a, 8), half_a)
        rows_b = pl.ds(pl.multiple_of(c * half_b, 8), half_b)
        stage_a[me] = buf_a[rows_a, :] + pair_a[rows_a, :]
        stage_b[me] = buf_b[rows_b, :] + pair_b[rows_b, :]

        def chip_copies(d):
            to = (chips[d][0], chips[d][1], c)
            return [remote(stage_a.at[me], stage_a.at[me], x_send.at[2 * d], x_recv.at[2 * d], to),
                    remote(stage_b.at[me], stage_b.at[me], x_send.at[2 * d + 1], x_recv.at[2 * d + 1], to)]

        def chip_arrivals(d):
            slot = 2 * chips[d][0] + chips[d][1]
            to = (chips[d][0], chips[d][1], c)
            return [remote(stage_a.at[slot], stage_a.at[slot], x_send.at[2 * d], x_recv.at[2 * d], to),
                    remote(stage_b.at[slot], stage_b.at[slot], x_send.at[2 * d + 1], x_recv.at[2 * d + 1], to)]

        for d in range(3):
            for cp in chip_copies(d):
                cp.start()
        yield
        for d in range(3):
            for cp in chip_arrivals(d):
                cp.wait_recv()
        tot_a[rows_a, :] = ((stage_a[0] + stage_a[1]) + stage_a[2]) + stage_a[3]
        tot_b[rows_b, :] = ((stage_b[0] + stage_b[1]) + stage_b[2]) + stage_b[3]

        second = [remote(tot_a.at[rows_a, :], tot_a.at[rows_a, :], p2_send.at[0], p2_recv.at[0], sibling),
                  remote(tot_b.at[rows_b, :], tot_b.at[rows_b, :], p2_send.at[1], p2_recv.at[1], sibling)]
        for cp in second:
            cp.start()
        yield
        other_a = pl.ds(pl.multiple_of((1 - c) * half_a, 8), half_a)
        other_b = pl.ds(pl.multiple_of((1 - c) * half_b, 8), half_b)
        remote(tot_a.at[other_a, :], tot_a.at[other_a, :], p2_send.at[0], p2_recv.at[0], sibling).wait_recv()
        remote(tot_b.at[other_b, :], tot_b.at[other_b, :], p2_send.at[1], p2_recv.at[1], sibling).wait_recv()
        for cp in second:
            cp.wait_send()
        for d in range(3):
            for cp in chip_copies(d):
                cp.wait_send()
        out_a[...] = tot_a[...]
        out_b[...] = tot_b[...]

    vec = pltpu.VMEM((_VEC_ROWS, D_MODEL), F32)
    mat = pltpu.VMEM((_MAT_ROWS, 128), F32)
    scratch = [vec, mat, vec, mat, pltpu.VMEM((N_CHIP, _VEC_ROWS // 2, D_MODEL), F32),
               pltpu.VMEM((N_CHIP, _MAT_ROWS // 2, 128), F32), vec, mat,
               pltpu.SemaphoreType.DMA((2,)), pltpu.SemaphoreType.DMA((2,)), pltpu.SemaphoreType.DMA((6,)),
               pltpu.SemaphoreType.DMA((6,)), pltpu.SemaphoreType.DMA((2,)), pltpu.SemaphoreType.DMA((2,))]
    return body, scratch


def _small_adamw(tot_a, tot_b, params):
    shapes = [params[nm][0].shape for nm in _SMALL]

    def body(*refs):
        ta, tb = refs[:2]
        prm = refs[2:2 + 3 * len(_SMALL)]
        outs = refs[2 + 3 * len(_SMALL):]

        def grad_of(k, name):
            if name == "sgu_w":
                return [tb[h * BLK:(h + 1) * BLK, :] for h in range(N_GRP)]
            if name == "sgu_b":
                return jnp.transpose(tb[N_GRP * BLK:_MAT_ROWS, :])[0:N_GRP, :]
            row = _VEC_ROW[name]
            if name == "b_in":
                return jnp.concatenate([ta[row:row + 1, :], ta[row + 1:row + 2, 0:IN_W - D_MODEL]], axis=1)
            return ta[row:row + 1, 0:shapes[k][-1]]

        for k, name in enumerate(_SMALL):
            w_ref, m_ref, v_ref = prm[3 * k:3 * k + 3]
            g_out, d_out, m_out, v_out = outs[4 * k:4 * k + 4]
            g = grad_of(k, name)
            if name == "sgu_w":
                for h in range(N_GRP):
                    d_, m_, v_ = _adamw_math(w_ref[h], g[h], m_ref[h], v_ref[h])
                    g_out[h], d_out[h], m_out[h], v_out[h] = g[h], d_, m_, v_
            else:
                d_, m_, v_ = _adamw_math(w_ref[...], g, m_ref[...], v_ref[...])
                g_out[...], d_out[...], m_out[...], v_out[...] = g, d_, m_, v_
        outs[-1][...] = ta[_LOSS_ROW:_LOSS_ROW + 1, :]

    ins = [tot_a, tot_b] + [_in_hbm(a) for nm in _SMALL for a in params[nm]]
    out_dims = [s for s in shapes for _ in range(4)] + [(1, D_MODEL)]
    res = pl.pallas_call(
        body, name="small_adamw", grid=(1,),
        in_specs=[_const2(a.shape) for a in ins], out_specs=[_const2(s) for s in out_dims],
        out_shape=[_hbm_shape(s, F32) for s in out_dims],
        compiler_params=_params(32),
    )(*ins)
    return {nm: tuple(res[4 * k:4 * k + 4]) for k, nm in enumerate(_SMALL)}, res[-1]


def _adamw_math(w, g, m, v):
    m = ADAM_B1 * m + (1.0 - ADAM_B1) * g
    v = ADAM_B2 * v + (1.0 - ADAM_B2) * (g * g)
    m_hat = m / (1.0 - ADAM_B1 ** ADAM_STEP)
    v_hat = v / (1.0 - ADAM_B2 ** ADAM_STEP)
    delta = -ADAM_LR * (m_hat / (jnp.sqrt(v_hat) + ADAM_EPS) + ADAM_WD * w)
    return delta, m, v


ADAMW_STEPS = 4


def _adamw(name, groups):
    k = len(groups)

    def body(*refs):
        for i in range(k):
            w_ref, g_ref, m_ref, v_ref = refs[4 * i:4 * i + 4]
            g = g_ref[...]
            for o_ref, o in zip(refs[4 * k + 4 * i:4 * k + 4 * i + 4], (g,) + _adamw_math(w_ref[...], g, m_ref[...], v_ref[...])):
                o_ref[...] = o

    specs = []
    for grp in groups:
        rows, cols = grp[0].shape
        assert rows % (8 * ADAMW_STEPS) == 0, rows
        specs += [pl.BlockSpec((rows // ADAMW_STEPS, cols), lambda i: (i, 0))] * 4
    res = pl.pallas_call(
        body, name=name, grid=(ADAMW_STEPS,), in_specs=specs, out_specs=specs,
        out_shape=[_hbm_shape(grp[0].shape, F32) for grp in groups for _ in range(4)],
        compiler_params=_params(56),
    )(*[_in_hbm(a) for grp in groups for a in grp])
    return [res[4 * i:4 * i + 4] for i in range(k)]


def kernel(x, positions, ln_in_g, ln_in_b, w_in, b_in, attn_sinks, sgu_ln_g, sgu_ln_b, sgu_w, sgu_b, w_out, b_out, ln_mix_g, ln_mix_b, w_gate, w_up, w_down, ln_ffn_g, ln_ffn_b, loss_target, m_ln_in_g, m_ln_in_b, m_w_in, m_b_in, m_attn_sinks, m_sgu_ln_g, m_sgu_ln_b, m_sgu_w, m_sgu_b, m_w_out, m_b_out, m_ln_mix_g, m_ln_mix_b, m_w_gate, m_w_up, m_w_down, m_ln_ffn_g, m_ln_ffn_b, v_ln_in_g, v_ln_in_b, v_w_in, v_b_in, v_attn_sinks, v_sgu_ln_g, v_sgu_ln_b, v_sgu_w, v_sgu_b, v_w_out, v_b_out, v_ln_mix_g, v_ln_mix_b, v_w_gate, v_w_up, v_w_down, v_ln_ffn_g, v_ln_ffn_b):
    weights = dict(ln_in_g=ln_in_g, ln_in_b=ln_in_b, w_in=w_in, b_in=b_in, attn_sinks=attn_sinks, sgu_ln_g=sgu_ln_g,
                   sgu_ln_b=sgu_ln_b, sgu_w=sgu_w, sgu_b=sgu_b, w_out=w_out, b_out=b_out, ln_mix_g=ln_mix_g,
                   ln_mix_b=ln_mix_b, w_gate=w_gate, w_up=w_up, w_down=w_down, ln_ffn_g=ln_ffn_g, ln_ffn_b=ln_ffn_b)
    mom_m = dict(ln_in_g=m_ln_in_g, ln_in_b=m_ln_in_b, w_in=m_w_in, b_in=m_b_in, attn_sinks=m_attn_sinks,
                 sgu_ln_g=m_sgu_ln_g, sgu_ln_b=m_sgu_ln_b, sgu_w=m_sgu_w, sgu_b=m_sgu_b, w_out=m_w_out, b_out=m_b_out,
                 ln_mix_g=m_ln_mix_g, ln_mix_b=m_ln_mix_b, w_gate=m_w_gate, w_up=m_w_up, w_down=m_w_down,
                 ln_ffn_g=m_ln_ffn_g, ln_ffn_b=m_ln_ffn_b)
    mom_v = dict(ln_in_g=v_ln_in_g, ln_in_b=v_ln_in_b, w_in=v_w_in, b_in=v_b_in, attn_sinks=v_attn_sinks,
                 sgu_ln_g=v_sgu_ln_g, sgu_ln_b=v_sgu_ln_b, sgu_w=v_sgu_w, sgu_b=v_sgu_b, w_out=v_w_out, b_out=v_b_out,
                 ln_mix_g=v_ln_mix_g, ln_mix_b=v_ln_mix_b, w_gate=v_w_gate, w_up=v_w_up, w_down=v_w_down,
                 ln_ffn_g=v_ln_ffn_g, ln_ffn_b=v_ln_ffn_b)
    order = list(weights)
    big = ("w_in", "w_out", "w_gate", "w_up", "w_down")

    s_len = x.shape[1]
    xs = _in_hbm(x.reshape(s_len, D_MODEL))
    tgt = _in_hbm(loss_target.reshape(s_len, D_MODEL))
    pos_row = _in_hbm(positions.reshape(1, s_len))
    g0, b0 = _in_hbm(ln_in_g.reshape(1, D_MODEL)), _in_hbm(ln_in_b.reshape(1, D_MODEL))
    sinks = attn_sinks.reshape(N_Q)
    sgu_w3 = _in_hbm(sgu_w.reshape(N_GRP, BLK, BLK))
    sgu_bt = _in_hbm(sgu_b.reshape(N_GRP, BLK).T)
    b_in, b_out, sgu_ln_g, sgu_ln_b, ln_mix_g, ln_mix_b, ln_ffn_g, ln_ffn_b = (
        _in_hbm(a) for a in (b_in, b_out, sgu_ln_g, sgu_ln_b, ln_mix_g, ln_mix_b, ln_ffn_g, ln_ffn_b))

    col_sharded = ("w_in", "w_gate", "w_up")

    def rowmajor(name, a):
        return jnp.swapaxes(a[0], 0, 1) if name in col_sharded else a[0]

    def as_given(name, a):
        return (jnp.swapaxes(a, 0, 1) if name in col_sharded else a)[None]

    shards = [rowmajor(n, weights[n]) for n in big]
    (gw_in,) = _gather_weights(shards[0:1])
    w_in_full = gw_in.reshape(IN_W, D_MODEL)

    sh_out, sh_gate, sh_up, sh_down = shards[1:]
    *acts, gw_out, gw_gate0 = _ln_inproj(xs, pos_row, g0, b0, w_in_full, b_in, _GatherPlan(
        [(sh_out, (0, OUT_SH), None), (sh_gate, (0, GATE_CUT), None)]))
    q, k, v, su, sv, tc, t1, t2 = (_in_hbm(a) for a in acts)
    mc, gw_gate, gw_up0 = _mixer_fwd(q, k, v, su, sv, sinks, sgu_ln_g, sgu_ln_b, sgu_w3, sgu_bt, _GatherPlan(
        [(sh_gate, (GATE_CUT, FF_SH), gw_gate0), (sh_up, (0, UP_CUT), None)]))
    mc = _in_hbm(mc)
    w_out_full = gw_out.reshape(D_MODEL, D_MODEL)
    r1, gw_up = _outproj(mc, w_out_full, b_out, xs, g0, b0, _GatherPlan([(sh_up, (UP_CUT, FF_SH), gw_up0)]))
    r1 = _in_hbm(r1)
    act, p_act, q_act, h1, gw_down = _ffn_up(r1, ln_mix_g, ln_mix_b, gw_gate, gw_up,
                                             _GatherPlan([(sh_down, (0, FF_SH), None)]))
    act, p_act, q_act = _in_hbm(act), _in_hbm(p_act), _in_hbm(q_act)
    dr2, loss_cols, d_ln_ffn_g, d_ln_ffn_b = _ffn_down_loss(act, gw_down, _in_hbm(h1), ln_ffn_g, ln_ffn_b, tgt)
    dr2 = _in_hbm(dr2)

    dg, du, wire_down, own_down = _ffn_bwd_a(dr2, act, p_act, q_act, gw_down)
    dh1a, wire_gate, own_gate, land_down = _ffn_bwd_g(dr2, _in_hbm(dg), r1, ln_mix_g, ln_mix_b, gw_gate, wire_down)
    dr1, wire_up, own_up, d_ln_mix_g, d_ln_mix_b, land_gate = _ffn_bwd_u(_in_hbm(dh1a), _in_hbm(du), r1, ln_mix_g,
                                                                         ln_mix_b, gw_up, wire_gate)
    dr1 = _in_hbm(dr1)
    dmc, wire_out, own_out, d_b_out = _outproj_bwd(dr1, mc, w_out_full)
    (dq, dkv, dsuv, dbq, dbkv, dbsuv, d_sink, d_sgu_ln_g, d_sgu_ln_b, d_sgu_w, d_sgu_bt, land_up, land_out) = _mixer_bwd(
        q, k, v, su, sv, _in_hbm(dmc), tc, t1, t2, sinks, sgu_ln_g, sgu_ln_b, sgu_w3, sgu_bt, [wire_up, wire_out])
    dkv = dkv[BLK:BLK + s_len]
    grad_x, acc_in, d_ln_in_g, d_ln_in_b = _inproj_bwd(_in_hbm(dq), _in_hbm(dkv), _in_hbm(dsuv), dr1, xs, g0, b0,
                                                       w_in_full)

    small_local = dict(
        ln_in_g=d_ln_in_g, ln_in_b=d_ln_in_b, bq=dbq, bkv=dbkv, bsuv=dbsuv, sink=d_sink, sgu_ln_g=d_sgu_ln_g,
        sgu_ln_b=d_sgu_ln_b, sgu_w=d_sgu_w, sgu_bt=d_sgu_bt, b_out=d_b_out, ln_mix_g=d_ln_mix_g, ln_mix_b=d_ln_mix_b,
        ln_ffn_g=d_ln_ffn_g, ln_ffn_b=d_ln_ffn_b, loss=loss_cols)
    *reduced, tot_a, tot_b = _grad_finish(acc_in, [land_out, land_gate, land_up, land_down],
                                          [own_out, own_gate, own_up, own_down], [small_local[nm] for nm in _SMALL_IN])
    small_shape = dict(ln_in_g=(1, D_MODEL), ln_in_b=(1, D_MODEL), sgu_w=(N_GRP, BLK, BLK), sgu_b=(N_GRP, BLK))
    small_params = {nm: tuple(src[nm].reshape(small_shape.get(nm, src[nm].shape)) for src in (weights, mom_m, mom_v))
                    for nm in _SMALL}
    small_out, loss_sum = _small_adamw(_in_hbm(tot_a), _in_hbm(tot_b), small_params)
    loss = jnp.sum(loss_sum) * (0.5 / D_MODEL)
    grads, delta, new_m, new_v = {}, {}, {}, {}
    for nm in _SMALL:
        grads[nm], delta[nm], new_m[nm], new_v[nm] = (a.reshape(weights[nm].shape) for a in small_out[nm])

    groups = [(shards[t], reduced[t], rowmajor(nm, mom_m[nm]), rowmajor(nm, mom_v[nm])) for t, nm in enumerate(big)]
    for nm, res in zip(big, _adamw("adamw", groups)):
        grads[nm], delta[nm], new_m[nm], new_v[nm] = (as_given(nm, a) for a in res)

    return (loss, grad_x.reshape(x.shape), *[grads[n] for n in order], *[delta[n] for n in order],
            *[new_m[n] for n in order], *[new_v[n] for n in order])
```

```python
import jax
import jax.numpy as jnp
from jax import lax
from jax.experimental import pallas as pl
from jax.experimental.pallas import tpu as pltpu

F32 = jnp.float32
_MXU = jnp.bfloat16
_WIRE = jnp.bfloat16
_ACT = jnp.bfloat16

D_MODEL = 1024
ATTN_W = 512
SGU_W = 512
HEAD_DIM = 64
N_Q = 8
N_KV = 2
Q_PER_KV = 4
KV_W = 128
BLK = 128
ROT_DIM = 16
ROPE_THETA = 500000.0
N_GRP = 4
GRP_DIM = 128
D_FF = 2816
IN_W = 1792
LN_EPS = 1e-5
ALPHA = 2.0 ** 0.25
N_CHIP = 4
FF_SH = D_FF // N_CHIP
IN_SH = IN_W // N_CHIP
OUT_SH = D_MODEL // N_CHIP
ROW_CHUNK = 32
GATE_CUT, UP_CUT = 352, 320

ADAM_LR = 0.001
ADAM_B1 = 0.9
ADAM_B2 = 0.999
ADAM_EPS = 1e-08
ADAM_WD = 0.01
ADAM_STEP = 10

SQRT_HALF = 0.7071067811865476
INV_SQRT_2PI = 0.3989422804014327
MESH_AXES = ("x", "y", "c")
MESH = pl.DeviceIdType.MESH
MIB = 2 ** 20


def _vmem():
    return pl.BlockSpec(memory_space=pltpu.VMEM)


def _smem():
    return pl.BlockSpec(memory_space=pltpu.SMEM)


def _hbm():
    return pl.BlockSpec(memory_space=pl.ANY)


def _hbm_shape(shape, dtype):
    return pltpu.HBM(shape, dtype)


def _in_hbm(a):
    return pltpu.with_memory_space_constraint(a, pltpu.HBM)


def _params(vmem_mib=48):
    return pltpu.CompilerParams(dimension_semantics=("arbitrary",), vmem_limit_bytes=vmem_mib * MIB)


def _tile(n, cap):
    if n <= cap:
        return n
    for t in range(cap - cap % 16, 0, -16):
        if n % t == 0:
            return t
    raise ValueError((n, cap))


def _rows(tm, width):
    return pl.BlockSpec((tm, width), lambda i: (i, 0))


def _const2(shape):
    return pl.BlockSpec(shape, lambda i: (0,) * len(shape))


def _ln(x, g, b):
    mu = jnp.mean(x, axis=-1, keepdims=True)
    xc = x - mu
    var = jnp.mean(xc * xc, axis=-1, keepdims=True)
    rstd = lax.rsqrt(var + LN_EPS)
    xhat = xc * rstd
    return xhat * g + b, xhat, rstd


def _ln_bwd(dy, xhat, rstd, g):
    gdy = dy * g
    m1 = jnp.mean(gdy, axis=-1, keepdims=True)
    m2 = jnp.mean(gdy * xhat, axis=-1, keepdims=True)
    return rstd * (gdy - m1 - xhat * m2)


def _colsum(a):
    return jnp.sum(a, axis=0, keepdims=True)


def _gelu_and_grad(x):
    cdf = 0.5 * (1.0 + lax.erf(x * SQRT_HALF))
    return x * cdf, cdf + x * jnp.exp(-0.5 * x * x) * INV_SQRT_2PI


def _dot(a, b):
    return jnp.dot(a, b, preferred_element_type=F32)


def _dot_nt(a, b):
    return lax.dot_general(a, b, (((1,), (1,)), ((), ())), preferred_element_type=F32)


def _dot_tn(a, b):
    return lax.dot_general(a, b, (((0,), (0,)), ((), ())), preferred_element_type=F32)


def _rope(t, tc, t1, t2):
    n = t.shape[1]
    rep = n // 128
    if rep > 1:
        tc, t1, t2 = (jnp.tile(a, (1, rep)) for a in (tc, t1, t2))
    return t * tc + pltpu.roll(t, n - 8, 1) * t1 + pltpu.roll(t, 8, 1) * t2


def _rope_bwd(d, tc, t1, t2):
    n = d.shape[1]
    rep = n // 128
    if rep > 1:
        tc, t1, t2 = (jnp.tile(a, (1, rep)) for a in (tc, t1, t2))
    return d * tc + pltpu.roll(d * t1, 8, 1) + pltpu.roll(d * t2, n - 8, 1)


def _causal_w(w_ref, h):
    t = lax.broadcasted_iota(jnp.int32, (BLK, BLK), 0)
    s = lax.broadcasted_iota(jnp.int32, (BLK, BLK), 1)
    return jnp.where(s <= t, w_ref[h], 0.0)


def _lane_put(vals, width):
    rows = vals[0].shape[0]
    lane = lax.broadcasted_iota(jnp.int32, (rows, width), 1)
    out = jnp.zeros((rows, width), F32)
    for k, v in enumerate(vals):
        out = out + jnp.where(lane == k, v, 0.0)
    return out


def _rope_consts():
    lane = jnp.arange(128) % HEAD_DIM
    rot = lane < ROT_DIM
    pair = (2 * (lane % (ROT_DIM // 2))).astype(F32)
    freq = jnp.where(rot, ROPE_THETA ** (-pair / ROT_DIM), 0.0)
    rows = [freq, rot.astype(F32), 1.0 - rot.astype(F32), (lane < ROT_DIM // 2).astype(F32),
            jnp.logical_and(lane >= ROT_DIM // 2, rot).astype(F32)]
    rows += [jnp.zeros((128,), F32)] * 3
    return jnp.stack(rows).astype(F32)


def _ln_inproj(x, pos_row, g0, b0, w_in, b_in, plan):
    s_len = x.shape[0]
    tm = _tile(s_len, 512)
    m, n = len(plan.operands()), plan.n

    def body(x_ref, pos_ref, g_ref, b_ref, w_ref, bi_ref, rc_ref, *rest):
        q_ref, k_ref, v_ref, su_ref, sv_ref, tc_ref, t1_ref, t2_ref = rest[m:m + 8]
        gather = plan.bind(rest[:m], rest[m + 8:m + 8 + n], rest[m + 8 + n:])
        i = pl.program_id(0)

        @pl.when(i == 0)
        def _():
            gather.start()

        h0, _, _ = _ln(x_ref[...], g_ref[...], b_ref[...])
        proj = _dot_nt(h0.astype(_MXU), w_ref[...]) + bi_ref[...]
        pos = jnp.broadcast_to(pos_ref[...].astype(F32), (128, tm))
        ang = jnp.transpose(pos) * rc_ref[0:1, :]
        cs = jnp.cos(ang)
        sn = jnp.sin(ang)
        tc = cs * rc_ref[1:2, :] + rc_ref[2:3, :]
        t1 = -sn * rc_ref[3:4, :]
        t2 = sn * rc_ref[4:5, :]
        tc_ref[...] = tc
        t1_ref[...] = t1
        t2_ref[...] = t2
        q = _rope(proj[:, 0:ATTN_W], tc, t1, t2) * (HEAD_DIM ** -0.5)
        q_ref[...] = q.astype(_MXU)
        k_ref[...] = _rope(proj[:, ATTN_W:ATTN_W + KV_W], tc, t1, t2).astype(_MXU)
        v_ref[...] = proj[:, ATTN_W + KV_W:ATTN_W + 2 * KV_W].astype(_MXU)
        su_ref[...] = proj[:, ATTN_W + 2 * KV_W:ATTN_W + 2 * KV_W + SGU_W]
        sv_ref[...] = proj[:, ATTN_W + 2 * KV_W + SGU_W:IN_W]

        last = pl.num_programs(0) - 1

        @pl.when(i == jnp.maximum(last - 1, 0))
        def _():
            gather.pass_on()

        @pl.when(i == last)
        def _():
            gather.finish()

    sd = _hbm_shape
    return pl.pallas_call(
        body, name="ln_inproj", grid=(s_len // tm,),
        in_specs=[_rows(tm, D_MODEL), pl.BlockSpec((1, tm), lambda i: (0, i)), _const2((1, D_MODEL)),
                  _const2((1, D_MODEL)), _vmem(),
                  _const2((1, IN_W)), _const2((8, 128))] + plan.in_specs(),
        out_specs=[_rows(tm, ATTN_W), _rows(tm, KV_W), _rows(tm, KV_W), _rows(tm, SGU_W), _rows(tm, SGU_W),
                   _rows(tm, 128), _rows(tm, 128), _rows(tm, 128)] + plan.out_specs(),
        out_shape=[sd((s_len, ATTN_W), _MXU), sd((s_len, KV_W), _MXU), sd((s_len, KV_W), _MXU),
                   sd((s_len, SGU_W), F32), sd((s_len, SGU_W), F32),
                   sd((s_len, 128), F32), sd((s_len, 128), F32), sd((s_len, 128), F32)] + plan.out_shapes(),
        scratch_shapes=plan.scratch(),
        compiler_params=_params(56),
    )(x, pos_row, g0, b0, w_in, b_in, _rope_consts(), *plan.operands())


def _band_mask_t(first_block):
    kj = lax.broadcasted_iota(jnp.int32, (2 * BLK, BLK), 0)
    qi = lax.broadcasted_iota(jnp.int32, (2 * BLK, BLK), 1)
    shut = jnp.where(first_block, 2 * BLK, 0)
    prev_ok = jnp.logical_and(kj < BLK, kj > qi + shut)
    cur_ok = jnp.logical_and(kj >= BLK, (kj - BLK) <= qi)
    return jnp.logical_or(prev_ok, cur_ok)


def _attn_probs_t(kh, qh, sink, allowed_t):
    s = jnp.where(allowed_t, _dot_nt(kh, qh), -1e30)
    m = jnp.maximum(jnp.max(s, axis=0, keepdims=True), sink)
    p = jnp.exp(s - m)
    ps = jnp.exp(sink - m)
    inv = 1.0 / (jnp.sum(p, axis=0, keepdims=True) + ps)
    return p * inv, ps * inv


def _sgu_mix(gv, lg, lb, w_ref, bt_ref):
    vv, vhat, rstd = _ln(gv, lg, lb)
    vvb = vv.astype(_MXU)
    wcs, mixed = [], []
    for h in range(N_GRP):
        wc = _causal_w(w_ref, h).astype(_MXU)
        wcs.append(wc)
        mixed.append(_dot(wc, vvb[:, h * GRP_DIM:(h + 1) * GRP_DIM]) + bt_ref[:, h:h + 1])
    return jnp.concatenate(mixed, axis=1), vhat, rstd, vvb, wcs


def _mixer_fwd(q, k, v, su, sv, sinks, sg, sb, sgu_w, sgu_bt, plan):
    s_len = q.shape[0]
    nb = s_len // BLK
    per = 2 if nb % 2 == 0 else 1
    steps = nb // per
    m, n = len(plan.operands()), plan.n

    def body(q_ref, kc_ref, kp_ref, vc_ref, vp_ref, su_ref, sv_ref, sink_ref, lg_ref, lb_ref, w_ref, bt_ref, *rest):
        mc_ref = rest[m]
        gather = plan.bind(rest[:m], rest[m + 1:m + 1 + n], rest[m + 1 + n:])
        i = pl.program_id(0)

        @pl.when(i == 0)
        def _():
            gather.start()

        @pl.when(i == max(steps - 2, 0))
        def _():
            gather.pass_on()

        @pl.when(i == steps - 1)
        def _():
            gather.finish()

        for s in range(per):
            rows = slice(s * BLK, (s + 1) * BLK)
            before = slice((s - 1) * BLK, s * BLK)
            k_prev = kp_ref[...] if s == 0 else kc_ref[before, :]
            v_prev = vp_ref[...] if s == 0 else vc_ref[before, :]
            allowed_t = _band_mask_t(i == 0 if s == 0 else False)
            kb = jnp.concatenate([k_prev, kc_ref[rows, :]], axis=0)
            vb = jnp.concatenate([v_prev, vc_ref[rows, :]], axis=0)
            qv = q_ref[rows, :]
            outs = []
            allowed_g = jnp.tile(allowed_t, (1, Q_PER_KV))
            for g in range(N_KV):
                heads = range(g * Q_PER_KV, (g + 1) * Q_PER_KV)
                kh = kb[:, g * HEAD_DIM:(g + 1) * HEAD_DIM]
                vh = vb[:, g * HEAD_DIM:(g + 1) * HEAD_DIM]
                q_g = jnp.concatenate([qv[:, h * HEAD_DIM:(h + 1) * HEAD_DIM] for h in heads], axis=0)
                sink_g = jnp.concatenate([jnp.full((1, BLK), sink_ref[h], F32) for h in heads], axis=1)
                probs_t, _ = _attn_probs_t(kh, q_g, sink_g, allowed_g)
                o_g = _dot_tn(probs_t.astype(_MXU), vh)
                outs += [o_g[hh * BLK:(hh + 1) * BLK, :] for hh in range(Q_PER_KV)]
            u = _gelu_and_grad(su_ref[rows, :])[0]
            gv = _gelu_and_grad(sv_ref[rows, :])[0]
            mixed = _sgu_mix(gv, lg_ref[...], lb_ref[...], w_ref, bt_ref)[0]
            mc_ref[rows, :] = jnp.concatenate(outs + [u * mixed], axis=1).astype(_MXU)

    cur = lambda w: pl.BlockSpec((per * BLK, w), lambda i: (i, 0))
    prev = lambda w: pl.BlockSpec((BLK, w), lambda i: (jnp.maximum(per * i - 1, 0), 0))
    return pl.pallas_call(
        body, name="mixer_fwd", grid=(steps,),
        in_specs=[cur(ATTN_W), cur(KV_W), prev(KV_W), cur(KV_W), prev(KV_W), cur(SGU_W), cur(SGU_W), _smem(),
                  _const2((1, SGU_W)), _const2((1, SGU_W)), _const2((N_GRP, BLK, BLK)), _const2((BLK, N_GRP))]
        + plan.in_specs(),
        out_specs=[cur(D_MODEL)] + plan.out_specs(),
        out_shape=[_hbm_shape((s_len, D_MODEL), _MXU)] + plan.out_shapes(),
        scratch_shapes=plan.scratch(),
        compiler_params=_params(56),
    )(q, k, k, v, v, su, sv, sinks, sg, sb, sgu_w, sgu_bt, *plan.operands())


def _outproj(mc, w_out, b_out, x, g0, b0, plan):
    s_len = x.shape[0]
    tm = _tile(s_len, 512)
    m, n = len(plan.operands()), plan.n

    def body(mc_ref, w_ref, bo_ref, x_ref, g_ref, b_ref, *rest):
        r1_ref = rest[m]
        gather = plan.bind(rest[:m], rest[m + 1:m + 1 + n], rest[m + 1 + n:])
        i = pl.program_id(0)

        @pl.when(i == 0)
        def _():
            gather.start()

        h0, _, _ = _ln(x_ref[...], g_ref[...], b_ref[...])
        r1_ref[...] = ALPHA * h0 + (_dot(mc_ref[...], w_ref[...]) + bo_ref[...])

        last = pl.num_programs(0) - 1

        @pl.when(i == jnp.maximum(last - 1, 0))
        def _():
            gather.pass_on()

        @pl.when(i == last)
        def _():
            gather.finish()

    return pl.pallas_call(
        body, name="outproj", grid=(s_len // tm,),
        in_specs=[_rows(tm, D_MODEL), _vmem(), _const2((1, D_MODEL)), _rows(tm, D_MODEL),
                  _const2((1, D_MODEL)), _const2((1, D_MODEL))] + plan.in_specs(),
        out_specs=[_rows(tm, D_MODEL)] + plan.out_specs(),
        out_shape=[_hbm_shape((s_len, D_MODEL), F32)] + plan.out_shapes(),
        scratch_shapes=plan.scratch(),
        compiler_params=_params(40),
    )(mc, w_out, b_out, x, g0, b0, *plan.operands())


def _ffn_spec(tm):
    return pl.BlockSpec((N_CHIP, tm, FF_SH), lambda i: (0, i, 0))


def _ffn_up(r1, g1, b1, wg, wu, plan):
    s_len = r1.shape[0]
    tm = _tile(s_len, 512)
    m, n = len(plan.operands()), plan.n

    def body(r1_ref, g_ref, b_ref, wg_ref, wu_ref, *rest):
        a_ref, p_ref, q_ref, h1_ref = rest[m:m + 4]
        gather = plan.bind(rest[:m], rest[m + 4:m + 4 + n], rest[m + 4 + n:])
        i = pl.program_id(0)

        @pl.when(i == 0)
        def _():
            gather.start()

        h1, _, _ = _ln(r1_ref[...], g_ref[...], b_ref[...])
        h1_ref[...] = h1
        h1b = h1.astype(_MXU)
        for j in range(N_CHIP):
            g = _dot_nt(h1b, wg_ref[j])
            u = _dot_nt(h1b, wu_ref[j])
            silu, sg = _silu_parts(g)
            a_ref[j] = (silu * u).astype(_MXU)
            p_ref[j] = silu.astype(_ACT)
            q_ref[j] = (u * (sg * (1.0 + g * (1.0 - sg)))).astype(_ACT)

        last = pl.num_programs(0) - 1

        @pl.when(i == jnp.maximum(last - 1, 0))
        def _():
            gather.pass_on()

        @pl.when(i == last)
        def _():
            gather.finish()

    sd = _hbm_shape((N_CHIP, s_len, FF_SH), _ACT)
    return pl.pallas_call(
        body, name="ffn_up", grid=(s_len // tm,),
        in_specs=[_rows(tm, D_MODEL), _const2((1, D_MODEL)), _const2((1, D_MODEL)), _vmem(), _vmem()] + plan.in_specs(),
        out_specs=[_ffn_spec(tm)] * 3 + [_rows(tm, D_MODEL)] + plan.out_specs(),
        out_shape=[_hbm_shape((N_CHIP, s_len, FF_SH), _MXU), sd, sd, _hbm_shape((s_len, D_MODEL), F32)]
        + plan.out_shapes(),
        scratch_shapes=plan.scratch(),
        compiler_params=_params(56),
    )(r1, g1, b1, wg, wu, *plan.operands())


def _silu_parts(g):
    sg = 1.0 / (1.0 + jnp.exp(-g))
    return g * sg, sg


def _ffn_down_loss(act, wd, h1, g2, b2, target):
    s_len = h1.shape[0]
    tm = _tile(s_len, 512)

    parts = 2 if tm % 32 == 0 else 1
    sub = tm // parts

    def body(a_ref, wd_ref, h1_ref, g2_ref, b2_ref, t_ref, dr2_ref, loss_ref, dg2_ref, db2_ref):
        i = pl.program_id(0)

        @pl.when(i == 0)
        def _():
            loss_ref[...] = jnp.zeros_like(loss_ref)
            dg2_ref[...] = jnp.zeros_like(dg2_ref)
            db2_ref[...] = jnp.zeros_like(db2_ref)

        for part in range(parts):
            rows = slice(part * sub, (part + 1) * sub)
            f = jnp.zeros((sub, D_MODEL), F32)
            for j in range(N_CHIP):
                f = f + _dot(a_ref[j, rows, :], wd_ref[j])
            h2, r2hat, rstd2 = _ln(ALPHA * h1_ref[rows, :] + f, g2_ref[...], b2_ref[...])
            diff = h2 - t_ref[rows, :]
            dh2 = diff * (1.0 / D_MODEL)
            loss_ref[...] += _colsum(diff * diff)
            dg2_ref[...] += _colsum(dh2 * r2hat)
            db2_ref[...] += _colsum(dh2)
            dr2_ref[rows, :] = _ln_bwd(dh2, r2hat, rstd2, g2_ref[...])

    vec = _hbm_shape((1, D_MODEL), F32)
    c = _const2((1, D_MODEL))
    return pl.pallas_call(
        body, name="ffn_down_loss", grid=(s_len // tm,),
        in_specs=[_ffn_spec(tm), _vmem(), _rows(tm, D_MODEL), c, c, _rows(tm, D_MODEL)],
        out_specs=[_rows(tm, D_MODEL), c, c, c],
        out_shape=[_hbm_shape((s_len, D_MODEL), F32), vec, vec, vec],
        compiler_params=_params(48),
    )(act, wd, h1, g2, b2, target)


def _ffn_bwd_a(dr2, act, p_act, q_act, wd):
    s_len = dr2.shape[0]
    tm = _tile(s_len, 512)

    def body(dr2_ref, a_ref, p_ref, q_ref, wd_ref, dg_ref, du_ref, wire_ref, own_ref,
             dwd_ref, land_ref, send_sem, recv_sem):
        i = pl.program_id(0)

        @pl.when(i == 0)
        def _():
            dwd_ref[...] = jnp.zeros_like(dwd_ref)

        dfb = dr2_ref[...].astype(_MXU)
        for j in range(N_CHIP):
            da = _dot_nt(dfb, wd_ref[j])
            dg_ref[j] = (da * q_ref[j].astype(F32)).astype(_MXU)
            du_ref[j] = (da * p_ref[j].astype(F32)).astype(_MXU)
            dwd_ref[j * FF_SH:(j + 1) * FF_SH, :] += _dot_tn(a_ref[j], dfb)

        @pl.when(i == pl.num_programs(0) - 1)
        def _():
            _pair_reduce(dwd_ref, wire_ref, own_ref, land_ref, send_sem, recv_sem)

    sd = _hbm_shape((N_CHIP, s_len, FF_SH), _MXU)
    half = (N_CHIP, FF_SH // 2, D_MODEL)
    return pl.pallas_call(
        body, name="ffn_bwd_a", grid=(s_len // tm,),
        in_specs=[_rows(tm, D_MODEL), _ffn_spec(tm), _ffn_spec(tm), _ffn_spec(tm), _vmem()],
        out_specs=[_ffn_spec(tm), _ffn_spec(tm), _vmem(), _vmem()],
        out_shape=[sd, sd] + _pair_out_shapes(half),
        scratch_shapes=_pair_scratch((D_FF, D_MODEL), half),
        compiler_params=_params(61),
    )(dr2, act, p_act, q_act, wd)


def _ffn_bwd_g(dr2, dg, r1, g1, b1, wg, prev_wire):
    s_len = dr2.shape[0]
    tm = _tile(s_len, 512)

    def body(dr2_ref, dg_ref, r1_ref, g1_ref, b1_ref, wg_ref, pw_ref, dh1_ref, wire_ref, own_ref, pl_ref,
             dwg_ref, land_ref, send_sem, recv_sem, xl_ref, x_send, x_recv, x_flush):
        i = pl.program_id(0)
        exchange = _ChipExchange(pw_ref, xl_ref, x_send, x_recv)

        @pl.when(i == 0)
        def _():
            exchange.start()
            dwg_ref[...] = jnp.zeros_like(dwg_ref)

        h1, _, _ = _ln(r1_ref[...], g1_ref[...], b1_ref[...])
        h1b = h1.astype(_MXU)
        dh1 = ALPHA * dr2_ref[...]
        for j in range(N_CHIP):
            dgj = dg_ref[j]
            dh1 = dh1 + _dot(dgj, wg_ref[j])
            dwg_ref[j * FF_SH:(j + 1) * FF_SH, :] += _dot_tn(dgj, h1b)
        dh1_ref[...] = dh1

        @pl.when(i == pl.num_programs(0) - 1)
        def _():
            _pair_reduce(dwg_ref, wire_ref, own_ref, land_ref, send_sem, recv_sem)
            exchange.finish_to(pl_ref, x_flush)

    c = _const2((1, D_MODEL))
    half = (N_CHIP, FF_SH // 2, D_MODEL)
    return pl.pallas_call(
        body, name="ffn_bwd_g", grid=(s_len // tm,),
        in_specs=[_rows(tm, D_MODEL), _ffn_spec(tm), _rows(tm, D_MODEL), c, c, _vmem(), _vmem()],
        out_specs=[_rows(tm, D_MODEL), _vmem(), _vmem(), _hbm()],
        out_shape=[_hbm_shape((s_len, D_MODEL), F32)] + _pair_out_shapes(half) + [_ChipExchange.land_shape(prev_wire)],
        scratch_shapes=_pair_scratch((D_FF, D_MODEL), half) + _ChipExchange.scratch(prev_wire),
        compiler_params=_params(58),
    )(dr2, dg, r1, g1, b1, wg, prev_wire)


def _ffn_bwd_u(dh1a, du, r1, g1, b1, wu, prev_wire):
    s_len = dh1a.shape[0]
    tm = _tile(s_len, 512)

    def body(dh1_ref, du_ref, r1_ref, g1_ref, b1_ref, wu_ref, pw_ref,
             dr1_ref, wire_ref, own_ref, dg1_ref, db1_ref, pl_ref,
             dwu_ref, land_ref, send_sem, recv_sem, xl_ref, x_send, x_recv, x_flush):
        i = pl.program_id(0)
        exchange = _ChipExchange(pw_ref, xl_ref, x_send, x_recv)

        @pl.when(i == 0)
        def _():
            exchange.start()
            dwu_ref[...] = jnp.zeros_like(dwu_ref)
            dg1_ref[...] = jnp.zeros_like(dg1_ref)
            db1_ref[...] = jnp.zeros_like(db1_ref)

        h1, r1hat, rstd1 = _ln(r1_ref[...], g1_ref[...], b1_ref[...])
        h1b = h1.astype(_MXU)
        dh1 = dh1_ref[...]
        for j in range(N_CHIP):
            duj = du_ref[j]
            dh1 = dh1 + _dot(duj, wu_ref[j])
            dwu_ref[j * FF_SH:(j + 1) * FF_SH, :] += _dot_tn(duj, h1b)
        dg1_ref[...] += _colsum(dh1 * r1hat)
        db1_ref[...] += _colsum(dh1)
        dr1_ref[...] = _ln_bwd(dh1, r1hat, rstd1, g1_ref[...])

        @pl.when(i == pl.num_programs(0) - 1)
        def _():
            _pair_reduce(dwu_ref, wire_ref, own_ref, land_ref, send_sem, recv_sem)
            exchange.finish_to(pl_ref, x_flush)

    vec = _hbm_shape((1, D_MODEL), F32)
    c = _const2((1, D_MODEL))
    half = (N_CHIP, FF_SH // 2, D_MODEL)
    return pl.pallas_call(
        body, name="ffn_bwd_u", grid=(s_len // tm,),
        in_specs=[_rows(tm, D_MODEL), _ffn_spec(tm), _rows(tm, D_MODEL), c, c, _vmem(), _vmem()],
        out_specs=[_rows(tm, D_MODEL), _vmem(), _vmem(), c, c, _hbm()],
        out_shape=[_hbm_shape((s_len, D_MODEL), F32)] + _pair_out_shapes(half)
        + [vec, vec, _ChipExchange.land_shape(prev_wire)],
        scratch_shapes=_pair_scratch((D_FF, D_MODEL), half) + _ChipExchange.scratch(prev_wire),
        compiler_params=_params(58),
    )(dh1a, du, r1, g1, b1, wu, prev_wire)


def _outproj_bwd(dr1, mc, w_out):
    s_len = dr1.shape[0]
    tm = _tile(s_len, 512)

    def body(dr1_ref, mc_ref, w_ref, dmc_ref, wire_ref, own_ref, db_ref, dw_ref, land_ref, send_sem, recv_sem):
        i = pl.program_id(0)

        @pl.when(i == 0)
        def _():
            dw_ref[...] = jnp.zeros_like(dw_ref)
            db_ref[...] = jnp.zeros_like(db_ref)

        d = dr1_ref[...]
        db_ref[...] += _colsum(d)
        db16 = d.astype(_MXU)
        dmc_ref[...] = _dot_nt(db16, w_ref[...])
        dw_ref[...] += _dot_tn(mc_ref[...], db16)

        @pl.when(i == pl.num_programs(0) - 1)
        def _():
            _pair_reduce(dw_ref, wire_ref, own_ref, land_ref, send_sem, recv_sem)

    half = (N_CHIP, OUT_SH // 2, D_MODEL)
    return pl.pallas_call(
        body, name="outproj_bwd", grid=(s_len // tm,),
        in_specs=[_rows(tm, D_MODEL), _rows(tm, D_MODEL), _vmem()],
        out_specs=[_rows(tm, D_MODEL), _vmem(), _vmem(), _const2((1, D_MODEL))],
        out_shape=[_hbm_shape((s_len, D_MODEL), F32)] + _pair_out_shapes(half) + [_hbm_shape((1, D_MODEL), F32)],
        scratch_shapes=_pair_scratch((D_MODEL, D_MODEL), half),
        compiler_params=_params(48),
    )(dr1, mc, w_out)


def _mixer_bwd(q, k, v, su, sv, dmc, tc, t1, t2, sinks, sg, sb, sgu_w, sgu_bt, prev_wires):
    s_len = q.shape[0]
    nb = s_len // BLK
    per = next(p for p in (4, 2, 1) if nb % p == 0)
    steps = nb // per

    def body(q_ref, kc_ref, kp_ref, vc_ref, vp_ref, su_ref, sv_ref, dmc_ref,
             tc_ref, t1_ref, t2_ref, tcp_ref, t1p_ref, t2p_ref,
             sink_ref, lg_ref, lb_ref, w_ref, bt_ref, pw0_ref, pw1_ref,
             dq_ref, dkv_ref, dsuv_ref, dbq_ref, dbkv_ref, dbsuv_ref,
             dsink_ref, dlg_ref, dlb_ref, dw_ref, dbt_ref, pl0_ref, pl1_ref, carry_ref,
             xl0_ref, x0_send, x0_recv, x0_flush, xl1_ref, x1_send, x1_recv, x1_flush):
        i = pl.program_id(0)
        exchanges = [(_ChipExchange(pw0_ref, xl0_ref, x0_send, x0_recv), pl0_ref, x0_flush),
                     (_ChipExchange(pw1_ref, xl1_ref, x1_send, x1_recv), pl1_ref, x1_flush)]

        @pl.when(i == 0)
        def _():
            for exchange, _, _ in exchanges:
                exchange.start()

        @pl.when(i == 0)
        def _():
            for r in (dbq_ref, dbkv_ref, dbsuv_ref, dsink_ref, dlg_ref, dlb_ref, dw_ref, dbt_ref, carry_ref):
                r[...] = jnp.zeros_like(r)

        def emit_kv(fin, t):
            if t == 0:
                tables = (tcp_ref[...], t1p_ref[...], t2p_ref[...])
            else:
                before = slice((t - 1) * BLK, t * BLK)
                tables = (tc_ref[before, :], t1_ref[before, :], t2_ref[before, :])
            dk = _rope_bwd(fin[:, 0:KV_W], *tables)
            out = jnp.concatenate([dk, fin[:, KV_W:2 * KV_W]], axis=1)
            dkv_ref[t * BLK:(t + 1) * BLK, :] = out.astype(_MXU)
            dbkv_ref[...] += _colsum(out)

        def one_block(s):
            rows = slice(s * BLK, (s + 1) * BLK)
            before = slice((s - 1) * BLK, s * BLK)
            k_prev = kp_ref[...] if s == 0 else kc_ref[before, :]
            v_prev = vp_ref[...] if s == 0 else vc_ref[before, :]
            allowed_t = _band_mask_t(i == 0 if s == 0 else False)
            kb = jnp.concatenate([k_prev, kc_ref[rows, :]], axis=0)
            vb = jnp.concatenate([v_prev, vc_ref[rows, :]], axis=0)
            qv = q_ref[rows, :]
            dmc = dmc_ref[rows, :]
            dqs, dks, dvs, dsinks = [], [], [], []
            allowed_g = jnp.tile(allowed_t, (1, Q_PER_KV))
            for g in range(N_KV):
                heads = range(g * Q_PER_KV, (g + 1) * Q_PER_KV)
                kh = kb[:, g * HEAD_DIM:(g + 1) * HEAD_DIM]
                vh = vb[:, g * HEAD_DIM:(g + 1) * HEAD_DIM]
                q_g = jnp.concatenate([qv[:, h * HEAD_DIM:(h + 1) * HEAD_DIM] for h in heads], axis=0)
                do_g = jnp.concatenate([dmc[:, h * HEAD_DIM:(h + 1) * HEAD_DIM] for h in heads], axis=0).astype(_MXU)
                sink_g = jnp.concatenate([jnp.full((1, BLK), sink_ref[h], F32) for h in heads], axis=1)
                probs_t, psink = _attn_probs_t(kh, q_g, sink_g, allowed_g)
                dvs.append(_dot(probs_t.astype(_MXU), do_g))
                dp_t = _dot_nt(vh, do_g)
                rd = jnp.sum(probs_t * dp_t, axis=0, keepdims=True)
                ds_t = (probs_t * (dp_t - rd)).astype(_MXU)
                ps_rd = psink * rd
                for hh in range(Q_PER_KV):
                    dsinks.append(-jnp.sum(ps_rd[:, hh * BLK:(hh + 1) * BLK], axis=1, keepdims=True))
                dq_g = _dot_tn(ds_t, kh)
                dqs += [dq_g[hh * BLK:(hh + 1) * BLK, :] for hh in range(Q_PER_KV)]
                dks.append(_dot(ds_t, q_g))
            dq = _rope_bwd(jnp.concatenate(dqs, axis=1) * (HEAD_DIM ** -0.5),
                           tc_ref[rows, :], t1_ref[rows, :], t2_ref[rows, :])
            dq_ref[rows, :] = dq.astype(_MXU)
            dbq_ref[...] += _colsum(dq)
            dsink_ref[...] += _lane_put(dsinks, 128)
            contrib = jnp.concatenate(dks + dvs, axis=1)

            lg = lg_ref[...]
            u, du_dsu = _gelu_and_grad(su_ref[rows, :])
            gv, dgv_dsv = _gelu_and_grad(sv_ref[rows, :])
            mixed, vhat, rstd, vvb, wcs = _sgu_mix(gv, lg, lb_ref[...], w_ref, bt_ref)
            dsgu = dmc[:, ATTN_W:D_MODEL]
            dsu = dsgu * mixed * du_dsu
            dmixed = dsgu * u
            tri_t = lax.broadcasted_iota(jnp.int32, (BLK, BLK), 0)
            tri_s = lax.broadcasted_iota(jnp.int32, (BLK, BLK), 1)
            dvv, dbs = [], []
            for h in range(N_GRP):
                dm = dmixed[:, h * GRP_DIM:(h + 1) * GRP_DIM]
                dmb = dm.astype(_MXU)
                dbs.append(jnp.sum(dm, axis=1, keepdims=True))
                dw_ref[h] += jnp.where(tri_s <= tri_t, _dot_nt(dmb, vvb[:, h * GRP_DIM:(h + 1) * GRP_DIM]), 0.0)
                dvv.append(_dot_tn(wcs[h], dmb))
            dvv = jnp.concatenate(dvv, axis=1)
            dbt_ref[...] += _lane_put(dbs, 128)
            dlg_ref[...] += _colsum(dvv * vhat)
            dlb_ref[...] += _colsum(dvv)
            dsv = _ln_bwd(dvv, vhat, rstd, lg) * dgv_dsv
            dsuv = jnp.concatenate([dsu, dsv], axis=1)
            dsuv_ref[rows, :] = dsuv.astype(_MXU)
            dbsuv_ref[...] += _colsum(dsuv)
            return contrib

        @pl.when(i < steps)
        def _():
            contribs = [one_block(s) for s in range(per)]
            for t in range(per):
                top = carry_ref[...] if t == 0 else contribs[t - 1][BLK:2 * BLK, :]
                emit_kv(top + contribs[t][0:BLK, :], t)
            carry_ref[...] = contribs[per - 1][BLK:2 * BLK, :]

        @pl.when(i == steps)
        def _():
            emit_kv(carry_ref[...], 0)
            if per > 1:
                dkv_ref[BLK:per * BLK, :] = jnp.zeros(((per - 1) * BLK, 2 * KV_W), _MXU)
            for exchange, landed, flush_sem in exchanges:
                exchange.finish_to(landed, flush_sem)

    last = steps - 1
    cur = lambda w: pl.BlockSpec((per * BLK, w), lambda i: (jnp.minimum(i, last), 0))
    prev = lambda w: pl.BlockSpec((BLK, w), lambda i: (jnp.clip(per * i - 1, 0, nb - 1), 0))
    shifted = pl.BlockSpec((per * BLK, 2 * KV_W), lambda i: (i, 0))
    sd = _hbm_shape
    return pl.pallas_call(
        body, name="mixer_bwd", grid=(steps + 1,),
        in_specs=[cur(ATTN_W), cur(KV_W), prev(KV_W), cur(KV_W), prev(KV_W), cur(SGU_W), cur(SGU_W), cur(D_MODEL),
                  cur(128), cur(128), cur(128), prev(128), prev(128), prev(128),
                  _smem(), _const2((1, SGU_W)), _const2((1, SGU_W)), _const2((N_GRP, BLK, BLK)), _const2((BLK, N_GRP)),
                  _vmem(), _vmem()],
        out_specs=[cur(ATTN_W), shifted, cur(2 * SGU_W),
                   _const2((1, ATTN_W)), _const2((1, 2 * KV_W)), _const2((1, 2 * SGU_W)),
                   _const2((1, 128)), _const2((1, SGU_W)), _const2((1, SGU_W)),
                   _const2((N_GRP, BLK, BLK)), _const2((BLK, 128)), _hbm(), _hbm()],
        out_shape=[sd((s_len, ATTN_W), _MXU), sd((s_len + per * BLK, 2 * KV_W), _MXU), sd((s_len, 2 * SGU_W), _MXU),
                   sd((1, ATTN_W), F32), sd((1, 2 * KV_W), F32), sd((1, 2 * SGU_W), F32),
                   sd((1, 128), F32), sd((1, SGU_W), F32), sd((1, SGU_W), F32),
                   sd((N_GRP, BLK, BLK), F32), sd((BLK, 128), F32)]
        + [_ChipExchange.land_shape(w) for w in prev_wires],
        scratch_shapes=[pltpu.VMEM((BLK, 2 * KV_W), F32)] + _ChipExchange.scratch(prev_wires[0])
        + _ChipExchange.scratch(prev_wires[1]),
        compiler_params=_params(40),
    )(q, k, k, v, v, su, sv, dmc, tc, t1, t2, tc, t1, t2, sinks, sg, sb, sgu_w, sgu_bt, *prev_wires)


def _inproj_bwd(dq, dkv_late, dsuv, dr1, x, g0, b0, w_in):
    s_len = x.shape[0]
    tm = _tile(s_len, 512)
    assert tm % BLK == 0
    per = tm // BLK
    cuts = ((0, ATTN_W), (ATTN_W, ATTN_W + 2 * KV_W), (ATTN_W + 2 * KV_W, IN_W))

    def body(dq_ref, *rest):
        dkv_refs = rest[:per]
        dsuv_ref, dr1_ref, x_ref, g_ref, b_ref, w_ref, dx_ref, dw_ref, dg_ref, db_ref = rest[per:]
        i = pl.program_id(0)

        @pl.when(i == 0)
        def _():
            dw_ref[...] = jnp.zeros_like(dw_ref)
            dg_ref[...] = jnp.zeros_like(dg_ref)
            db_ref[...] = jnp.zeros_like(db_ref)

        h0, xhat, rstd = _ln(x_ref[...], g_ref[...], b_ref[...])
        h0b = h0.astype(_MXU)
        dh0 = ALPHA * dr1_ref[...]
        dkv = jnp.concatenate([r[...] for r in dkv_refs], axis=0)
        for (lo, hi), d in zip(cuts, (dq_ref[...], dkv, dsuv_ref[...])):
            dh0 = dh0 + _dot(d, w_ref[lo:hi, :])
            dw_ref[lo:hi, :] += _dot_tn(d, h0b)
        dg_ref[...] += _colsum(dh0 * xhat)
        db_ref[...] += _colsum(dh0)
        dx_ref[...] = _ln_bwd(dh0, xhat, rstd, g_ref[...])

    vec = _hbm_shape((1, D_MODEL), F32)
    c = _const2((1, D_MODEL))
    return pl.pallas_call(
        body, name="inproj_bwd", grid=(s_len // tm,),
        in_specs=[_rows(tm, ATTN_W)]
        + [pl.BlockSpec((BLK, 2 * KV_W), lambda i, b=b: (i * per + b + 1, 0)) for b in range(per)]
        + [_rows(tm, 2 * SGU_W), _rows(tm, D_MODEL), _rows(tm, D_MODEL), c, c, _vmem()],
        out_specs=[_rows(tm, D_MODEL), _vmem(), c, c],
        out_shape=[_hbm_shape((s_len, D_MODEL), F32), jax.ShapeDtypeStruct((IN_W, D_MODEL), F32), vec, vec],
        compiler_params=_params(48),
    )(dq, *[dkv_late] * per, dsuv, dr1, x, g0, b0, w_in)


def _place():
    x, y, c = (lax.axis_index(a) for a in MESH_AXES)
    chips = [(1 - x, y), (x, 1 - y), (1 - x, 1 - y)]
    return x, y, c, chips


class _Gather:
    def __init__(self, ins, outs, send_sems, recv_sems, spans=None):
        self.ins, self.outs, self.send_sems, self.recv_sems = ins, outs, send_sems, recv_sems
        self.n = len(ins)
        self.spans = spans or [(0, r.shape[0]) for r in ins]
        self.halves = [(hi - lo) // 2 for lo, hi in self.spans]

    def _copy(self, k, t, slot, half, to):
        rows = pl.ds(pl.multiple_of(self.spans[t][0] + half * self.halves[t], 16), self.halves[t])
        piece = self.outs[t].at[slot, rows, :]
        return pltpu.make_async_remote_copy(src_ref=piece, dst_ref=piece, send_sem=self.send_sems.at[k],
                                            recv_sem=self.recv_sems.at[k], device_id=to, device_id_type=MESH)

    def _chip_copy(self, t, d, slot):
        x, y, c, chips = _place()
        return self._copy(3 * t + d, t, slot, c, (chips[d][0], chips[d][1], c))

    def _pass_copy(self, t, d, half):
        x, y, c, chips = _place()
        return self._copy(3 * self.n + 3 * t + d, t, 2 * chips[d][0] + chips[d][1], half, (x, y, 1 - c))

    def start(self):
        x, y, c, chips = _place()
        me = 2 * x + y
        for t in range(self.n):
            lo, hi = self.spans[t]
            self.outs[t][me, lo:hi, :] = self.ins[t][lo:hi, :].astype(_WIRE)
        for t in range(self.n):
            for d in range(3):
                self._chip_copy(t, d, me).start()

    def pass_on(self):
        x, y, c, chips = _place()
        for t in range(self.n):
            for d in range(3):
                self._chip_copy(t, d, 2 * chips[d][0] + chips[d][1]).wait_recv()
                self._pass_copy(t, d, c).start()

    def finish(self):
        x, y, c, chips = _place()
        me = 2 * x + y
        for t in range(self.n):
            for d in range(3):
                self._pass_copy(t, d, 1 - c).wait_recv()
        for t in range(self.n):
            for d in range(3):
                self._chip_copy(t, d, me).wait_send()
                self._pass_copy(t, d, c).wait_send()

    @staticmethod
    def out_shapes(shards, make=jax.ShapeDtypeStruct):
        return [make((N_CHIP,) + s.shape, _WIRE) for s in shards]

    @staticmethod
    def sems(n):
        return [pltpu.SemaphoreType.DMA((6 * n,)), pltpu.SemaphoreType.DMA((6 * n,))]


class _GatherPlan:
    def __init__(self, pieces):
        self.shards = [p[0] for p in pieces]
        self.spans = [p[1] for p in pieces]
        self.earlier = [p[2] for p in pieces]
        self.n = len(pieces)
        self.carried = [t for t in range(self.n) if self.earlier[t] is not None]

    def operands(self):
        return self.shards + [self.earlier[t] for t in self.carried]

    def in_specs(self):
        return [_vmem()] * self.n + [_hbm()] * len(self.carried)

    def out_specs(self):
        return [_hbm()] * self.n

    def out_shapes(self):
        return _Gather.out_shapes(self.shards, _hbm_shape)

    def scratch(self):
        return ([pltpu.VMEM((N_CHIP,) + s.shape, _WIRE) for s in self.shards] + _Gather.sems(self.n)
                + [pltpu.SemaphoreType.DMA((self.n,)), pltpu.SemaphoreType.DMA((max(len(self.carried), 1),))])

    def bind(self, in_refs, out_refs, scratch_refs):
        plan = self
        shard_refs, earlier_refs = in_refs[:self.n], in_refs[self.n:]
        bufs = scratch_refs[:self.n]
        send_sems, recv_sems, flush_sems, carry_sems = scratch_refs[self.n:self.n + 4]
        gather = _Gather(shard_refs, bufs, send_sems, recv_sems, self.spans)

        def carry_copy(k):
            t = plan.carried[k]
            lo = plan.spans[t][0]
            return pltpu.make_async_copy(earlier_refs[k].at[:, 0:lo, :], bufs[t].at[:, 0:lo, :], carry_sems.at[k])

        class Bound:
            @staticmethod
            def start():
                for k in range(len(plan.carried)):
                    carry_copy(k).start()
                gather.start()

            @staticmethod
            def pass_on():
                gather.pass_on()

            @staticmethod
            def finish():
                gather.finish()
                for k in range(len(plan.carried)):
                    carry_copy(k).wait()
                _flush([bufs[t].at[:, 0:plan.spans[t][1], :] for t in range(plan.n)],
                       [out_refs[t].at[:, 0:plan.spans[t][1], :] for t in range(plan.n)], flush_sems)

        return Bound


def _flush(bufs, hbm_outs, sems):
    copies = [pltpu.make_async_copy(b, o, sems.at[k]) for k, (b, o) in enumerate(zip(bufs, hbm_outs))]
    for cp in copies:
        cp.start()
    for cp in copies:
        cp.wait()


def _gather_weights(shards):
    n = len(shards)

    def body(*refs):
        gather = _Gather(refs[:n], refs[n:2 * n], refs[2 * n], refs[2 * n + 1])
        gather.start()
        gather.pass_on()
        gather.finish()

    return pl.pallas_call(
        body, name="gather_weights",
        in_specs=[_vmem()] * n, out_specs=[_vmem()] * n,
        out_shape=_Gather.out_shapes(shards), scratch_shapes=_Gather.sems(n),
        compiler_params=pltpu.CompilerParams(vmem_limit_bytes=32 * MIB),
    )(*shards)


class _ChipExchange:
    def __init__(self, wire_ref, land_ref, send_sems, recv_sems):
        self.wire, self.land, self.send_sems, self.recv_sems = wire_ref, land_ref, send_sems, recv_sems

    def _copy(self, d):
        x, y, c, chips = _place()
        return pltpu.make_async_remote_copy(
            src_ref=self.wire.at[2 * chips[d][0] + chips[d][1]], dst_ref=self.land.at[d],
            send_sem=self.send_sems.at[d], recv_sem=self.recv_sems.at[d],
            device_id=(chips[d][0], chips[d][1], c), device_id_type=MESH)

    def start(self):
        for d in range(3):
            self._copy(d).start()

    def wait_recv(self):
        for d in range(3):
            self._copy(d).wait_recv()

    def wait_send(self):
        for d in range(3):
            self._copy(d).wait_send()

    def finish_to(self, hbm_out, flush_sem):
        self.wait_recv()
        _flush([self.land], [hbm_out], flush_sem)
        self.wait_send()

    @staticmethod
    def land_shape(wire):
        return _hbm_shape((3,) + wire.shape[1:], wire.dtype)

    @staticmethod
    def sems():
        return [pltpu.SemaphoreType.DMA((3,)), pltpu.SemaphoreType.DMA((3,))]

    @staticmethod
    def scratch(wire):
        return ([pltpu.VMEM((3,) + wire.shape[1:], wire.dtype)] + _ChipExchange.sems() + [pltpu.SemaphoreType.DMA((1,))])


def _pair_out_shapes(half_shape):
    return [jax.ShapeDtypeStruct(half_shape, _WIRE), jax.ShapeDtypeStruct(half_shape[1:], F32)]


def _pair_scratch(acc_shape, half_shape):
    return [pltpu.VMEM(acc_shape, F32), pltpu.VMEM(half_shape, _WIRE),
            pltpu.SemaphoreType.DMA((N_CHIP,)), pltpu.SemaphoreType.DMA((N_CHIP,))]


def _pair_reduce(acc_ref, wire_ref, own_ref, land_ref, send_sems, recv_sems):
    rh = land_ref.shape[1]
    x, y, c, _ = _place()
    me = 2 * x + y
    copies = []
    for j in range(N_CHIP):
        def cast(r, carry, j=j):
            dst = pl.ds(pl.multiple_of(r * ROW_CHUNK, ROW_CHUNK), ROW_CHUNK)
            src = pl.ds(pl.multiple_of((2 * j + 1 - c) * rh + r * ROW_CHUNK, 8), ROW_CHUNK)
            wire_ref[j, dst, :] = acc_ref[src, :].astype(_WIRE)
            return carry

        lax.fori_loop(0, rh // ROW_CHUNK, cast, 0)
        cp = pltpu.make_async_remote_copy(src_ref=wire_ref.at[j], dst_ref=land_ref.at[j], send_sem=send_sems.at[j],
                                          recv_sem=recv_sems.at[j], device_id=(x, y, 1 - c), device_id_type=MESH)
        cp.start()
        copies.append(cp)
    for j in range(N_CHIP):
        copies[j].wait()

        def chunk(r, carry, j=j):
            theirs = pl.ds(pl.multiple_of(r * ROW_CHUNK, ROW_CHUNK), ROW_CHUNK)
            mine = pl.ds(pl.multiple_of((2 * j + c) * rh + r * ROW_CHUNK, 8), ROW_CHUNK)
            wire_ref[j, theirs, :] = (acc_ref[mine, :] + land_ref[j, theirs, :].astype(F32)).astype(_WIRE)
            return carry

        lax.fori_loop(0, rh // ROW_CHUNK, chunk, 0)

    def own_chunk(r, carry):
        theirs = pl.ds(pl.multiple_of(r * ROW_CHUNK, ROW_CHUNK), ROW_CHUNK)
        mine = pl.ds(pl.multiple_of((2 * me + c) * rh + r * ROW_CHUNK, 8), ROW_CHUNK)
        own_ref[theirs, :] = acc_ref[mine, :] + land_ref[me, theirs, :].astype(F32)
        return carry

    lax.fori_loop(0, rh // ROW_CHUNK, own_chunk, 0)


def _grad_finish(last_acc, lands, owns, small):
    n = len(owns) + 1
    halves = [last_acc.shape[0] // (2 * N_CHIP)] + [w.shape[1] for w in lands]
    widths = [last_acc.shape[1]] + [a.shape[1] for a in owns]
    small_body, small_scratch = _small_allreduce_parts()
    ns = len(small)

    def body(*refs):
        acc0, land, own = refs[0], (None,) + refs[1:n], (None,) + refs[n:2 * n - 1]
        refs = refs[2 * n - 1:]
        small_in, g, small_out = refs[:ns], refs[ns:ns + n], refs[ns + n:ns + n + 2]
        refs = refs[ns + n + 2:]
        pland0, wire0, land0, own0 = refs[0:4]
        p_send, p_recv, x_send, x_recv, pair_send, pair_recv = refs[4:10]
        small_refs = refs[10:]
        land = (land0,) + land[1:]
        own = (own0,) + own[1:]
        x, y, c, chips = _place()
        me = 2 * x + y
        exchange = _ChipExchange(wire0, land0, x_send, x_recv)

        def half_rows(t, half):
            return pl.ds(pl.multiple_of(half * halves[t], 8), halves[t])

        def pair_copy(t, half):
            rows = g[t].at[half_rows(t, half), :]
            return pltpu.make_async_remote_copy(src_ref=rows, dst_ref=rows, send_sem=pair_send.at[t],
                                                recv_sem=pair_recv.at[t], device_id=(x, y, 1 - c), device_id_type=MESH)

        small_rounds = small_body(*small_in, *small_out, *small_refs)
        next(small_rounds)
        _pair_reduce(acc0, wire0, own0, pland0, p_send, p_recv)
        next(small_rounds)
        exchange.start()

        for t in list(range(1, n)) + [0]:
            if t == 0:
                exchange.wait_recv()
            if t == min(2, n - 1):
                next(small_rounds)
            if t == min(4, n - 1):
                next(small_rounds, None)

            def chunk(r, carry, t=t):
                src = pl.ds(pl.multiple_of(r * ROW_CHUNK, ROW_CHUNK), ROW_CHUNK)
                dst = pl.ds(pl.multiple_of(c * halves[t] + r * ROW_CHUNK, 8), ROW_CHUNK)
                s = own[t][src, :]
                for d in range(3):
                    s = s + land[t][d, src, :].astype(F32)
                g[t][dst, :] = s
                return carry

            lax.fori_loop(0, halves[t] // ROW_CHUNK, chunk, 0)
            pair_copy(t, c).start()
        for t in range(n):
            pair_copy(t, 1 - c).wait_recv()
        for t in range(n):
            pair_copy(t, c).wait_send()
        exchange.wait_send()

    half0 = (halves[0], widths[0])
    return pl.pallas_call(
        body, name="grad_finish",
        in_specs=[_vmem()] * (2 * n - 1 + ns), out_specs=[_vmem()] * (n + 2),
        out_shape=[jax.ShapeDtypeStruct((2 * h, w), F32) for h, w in zip(halves, widths)]
        + [jax.ShapeDtypeStruct(s, F32) for s in _SMALL_OUT_DIMS],
        scratch_shapes=[pltpu.VMEM((N_CHIP,) + half0, _WIRE), pltpu.VMEM((N_CHIP,) + half0, _WIRE),
                        pltpu.VMEM((3,) + half0, _WIRE), pltpu.VMEM(half0, F32)]
        + [pltpu.SemaphoreType.DMA((N_CHIP,)), pltpu.SemaphoreType.DMA((N_CHIP,))]
        + _ChipExchange.sems()
        + [pltpu.SemaphoreType.DMA((n,)), pltpu.SemaphoreType.DMA((n,))]
        + small_scratch,
        compiler_params=pltpu.CompilerParams(vmem_limit_bytes=56 * MIB),
    )(last_acc, *lands, *owns, *small)


_SMALL = ("ln_in_g", "ln_in_b", "b_in", "attn_sinks", "sgu_ln_g", "sgu_ln_b", "sgu_w", "sgu_b", "b_out",
          "ln_mix_g", "ln_mix_b", "ln_ffn_g", "ln_ffn_b")
_VEC_ROW = dict(ln_in_g=0, ln_in_b=1, b_in=2, attn_sinks=4, sgu_ln_g=5, sgu_ln_b=6, b_out=7, ln_mix_g=8, ln_mix_b=9,
                ln_ffn_g=10, ln_ffn_b=11)
_LOSS_ROW = 12
_VEC_ROWS = 16
_MAT_ROWS = N_GRP * BLK + BLK


_SMALL_IN = ("ln_in_g", "ln_in_b", "bq", "bkv", "bsuv", "sink", "sgu_ln_g", "sgu_ln_b", "sgu_w", "sgu_bt", "b_out",
             "ln_mix_g", "ln_mix_b", "ln_ffn_g", "ln_ffn_b", "loss")
_SMALL_OUT_DIMS = ((_VEC_ROWS, D_MODEL), (_MAT_ROWS, 128))


def _small_allreduce_parts():
    n_in = len(_SMALL_IN)

    def body(*refs):
        (g_ln_in_g, g_ln_in_b, g_bq, g_bkv, g_bsuv, g_sink, g_sln_g, g_sln_b, g_sw, g_sbt, g_bout,
         g_lmg, g_lmb, g_lfg, g_lfb, g_loss) = refs[:n_in]
        out_a, out_b = refs[n_in:n_in + 2]
        (buf_a, buf_b, pair_a, pair_b, stage_a, stage_b, tot_a, tot_b,
         p1_send, p1_recv, x_send, x_recv, p2_send, p2_recv) = refs[n_in + 2:]
        x, y, c, chips = _place()
        me = 2 * x + y
        sibling = (x, y, 1 - c)
        half_a, half_b = _VEC_ROWS // 2, _MAT_ROWS // 2

        buf_a[...] = jnp.zeros_like(buf_a)
        for row, ref in ((0, g_ln_in_g), (1, g_ln_in_b), (7, g_bout), (8, g_lmg), (9, g_lmb), (10, g_lfg), (11, g_lfb),
                         (_LOSS_ROW, g_loss)):
            buf_a[row:row + 1, :] = ref[...]
        buf_a[2:3, 0:ATTN_W] = g_bq[...]
        buf_a[2:3, ATTN_W:ATTN_W + 2 * KV_W] = g_bkv[...]
        buf_a[2:3, ATTN_W + 2 * KV_W:D_MODEL] = g_bsuv[:, 0:2 * KV_W]
        buf_a[3:4, 0:2 * SGU_W - 2 * KV_W] = g_bsuv[:, 2 * KV_W:2 * SGU_W]
        buf_a[4:5, 0:128] = g_sink[...]
        buf_a[5:6, 0:SGU_W] = g_sln_g[...]
        buf_a[6:7, 0:SGU_W] = g_sln_b[...]
        for h in range(N_GRP):
            buf_b[h * BLK:(h + 1) * BLK, :] = g_sw[h]
        buf_b[N_GRP * BLK:_MAT_ROWS, :] = g_sbt[...]

        def remote(src, dst, send_sem, recv_sem, to):
            return pltpu.make_async_remote_copy(src_ref=src, dst_ref=dst, send_sem=send_sem, recv_sem=recv_sem,
                                                device_id=to, device_id_type=MESH)

        first = [remote(buf_a, pair_a, p1_send.at[0], p1_recv.at[0], sibling),
                 remote(buf_b, pair_b, p1_send.at[1], p1_recv.at[1], sibling)]
        for cp in first:
            cp.start()
        yield
        for cp in first:
            cp.wait()
        rows_a = pl.ds(pl.multiple_of(c * half_a, 8), half_a)
        rows_b = pl.ds(pl.multiple_of(c * half_b, 8), half_b)
        stage_a[me] = buf_a[rows_a, :] + pair_a[rows_a, :]
        stage_b[me] = buf_b[rows_b, :] + pair_b[rows_b, :]

        def chip_copies(d):
            to = (chips[d][0], chips[d][1], c)
            return [remote(stage_a.at[me], stage_a.at[me], x_send.at[2 * d], x_recv.at[2 * d], to),
                    remote(stage_b.at[me], stage_b.at[me], x_send.at[2 * d + 1], x_recv.at[2 * d + 1], to)]

        def chip_arrivals(d):
            slot = 2 * chips[d][0] + chips[d][1]
            to = (chips[d][0], chips[d][1], c)
            return [remote(stage_a.at[slot], stage_a.at[slot], x_send.at[2 * d], x_recv.at[2 * d], to),
                    remote(stage_b.at[slot], stage_b.at[slot], x_send.at[2 * d + 1], x_recv.at[2 * d + 1], to)]

        for d in range(3):
            for cp in chip_copies(d):
                cp.start()
        yield
        for d in range(3):
            for cp in chip_arrivals(d):
                cp.wait_recv()
        tot_a[rows_a, :] = ((stage_a[0] + stage_a[1]) + stage_a[2]) + stage_a[3]
        tot_b[rows_b, :] = ((stage_b[0] + stage_b[1]) + stage_b[2]) + stage_b[3]

        second = [remote(tot_a.at[rows_a, :], tot_a.at[rows_a, :], p2_send.at[0], p2_recv.at[0], sibling),
                  remote(tot_b.at[rows_b, :], tot_b.at[rows_b, :], p2_send.at[1], p2_recv.at[1], sibling)]
        for cp in second:
            cp.start()
        yield
        other_a = pl.ds(pl.multiple_of((1 - c) * half_a, 8), half_a)
        other_b = pl.ds(pl.multiple_of((1 - c) * half_b, 8), half_b)
        remote(tot_a.at[other_a, :], tot_a.at[other_a, :], p2_send.at[0], p2_recv.at[0], sibling).wait_recv()
        remote(tot_b.at[other_b, :], tot_b.at[other_b, :], p2_send.at[1], p2_recv.at[1], sibling).wait_recv()
        for cp in second:
            cp.wait_send()
        for d in range(3):
            for cp in chip_copies(d):
                cp.wait_send()
        out_a[...] = tot_a[...]
        out_b[...] = tot_b[...]

    vec = pltpu.VMEM((_VEC_ROWS, D_MODEL), F32)
    mat = pltpu.VMEM((_MAT_ROWS, 128), F32)
    scratch = [vec, mat, vec, mat, pltpu.VMEM((N_CHIP, _VEC_ROWS // 2, D_MODEL), F32),
               pltpu.VMEM((N_CHIP, _MAT_ROWS // 2, 128), F32), vec, mat,
               pltpu.SemaphoreType.DMA((2,)), pltpu.SemaphoreType.DMA((2,)), pltpu.SemaphoreType.DMA((6,)),
               pltpu.SemaphoreType.DMA((6,)), pltpu.SemaphoreType.DMA((2,)), pltpu.SemaphoreType.DMA((2,))]
    return body, scratch


def _small_adamw(tot_a, tot_b, params):
    shapes = [params[nm][0].shape for nm in _SMALL]

    def body(*refs):
        ta, tb = refs[:2]
        prm = refs[2:2 + 3 * len(_SMALL)]
        outs = refs[2 + 3 * len(_SMALL):]

        def grad_of(k, name):
            if name == "sgu_w":
                return [tb[h * BLK:(h + 1) * BLK, :] for h in range(N_GRP)]
            if name == "sgu_b":
                return jnp.transpose(tb[N_GRP * BLK:_MAT_ROWS, :])[0:N_GRP, :]
            row = _VEC_ROW[name]
            if name == "b_in":
                return jnp.concatenate([ta[row:row + 1, :], ta[row + 1:row + 2, 0:IN_W - D_MODEL]], axis=1)
            return ta[row:row + 1, 0:shapes[k][-1]]

        for k, name in enumerate(_SMALL):
            w_ref, m_ref, v_ref = prm[3 * k:3 * k + 3]
            g_out, d_out, m_out, v_out = outs[4 * k:4 * k + 4]
            g = grad_of(k, name)
            if name == "sgu_w":
                for h in range(N_GRP):
                    d_, m_, v_ = _adamw_math(w_ref[h], g[h], m_ref[h], v_ref[h])
                    g_out[h], d_out[h], m_out[h], v_out[h] = g[h], d_, m_, v_
            else:
                d_, m_, v_ = _adamw_math(w_ref[...], g, m_ref[...], v_ref[...])
                g_out[...], d_out[...], m_out[...], v_out[...] = g, d_, m_, v_
        outs[-1][...] = jnp.sum(ta[_LOSS_ROW:_LOSS_ROW + 1, :], axis=1, keepdims=True) * (0.5 / D_MODEL)

    ins = [tot_a, tot_b] + [_in_hbm(a) for nm in _SMALL for a in params[nm]]
    out_dims = [s for s in shapes for _ in range(4)] + [(1, 1)]
    res = pl.pallas_call(
        body, name="small_adamw", grid=(1,),
        in_specs=[_const2(a.shape) for a in ins], out_specs=[_const2(s) for s in out_dims],
        out_shape=[_hbm_shape(s, F32) for s in out_dims],
        compiler_params=_params(32),
    )(*ins)
    return {nm: tuple(res[4 * k:4 * k + 4]) for k, nm in enumerate(_SMALL)}, res[-1]


def _adamw_math(w, g, m, v):
    m = ADAM_B1 * m + (1.0 - ADAM_B1) * g
    v = ADAM_B2 * v + (1.0 - ADAM_B2) * (g * g)
    m_hat = m / (1.0 - ADAM_B1 ** ADAM_STEP)
    v_hat = v / (1.0 - ADAM_B2 ** ADAM_STEP)
    delta = -ADAM_LR * (m_hat / (jnp.sqrt(v_hat) + ADAM_EPS) + ADAM_WD * w)
    return delta, m, v


ADAMW_STEPS = 4


def _adamw(name, groups):
    k = len(groups)

    def body(*refs):
        for i in range(k):
            w_ref, g_ref, m_ref, v_ref = refs[4 * i:4 * i + 4]
            g = g_ref[...]
            for o_ref, o in zip(refs[4 * k + 4 * i:4 * k + 4 * i + 4], (g,) + _adamw_math(w_ref[...], g, m_ref[...], v_ref[...])):
                o_ref[...] = o

    specs = []
    for grp in groups:
        rows, cols = grp[0].shape
        assert rows % (8 * ADAMW_STEPS) == 0, rows
        specs += [pl.BlockSpec((rows // ADAMW_STEPS, cols), lambda i: (i, 0))] * 4
    res = pl.pallas_call(
        body, name=name, grid=(ADAMW_STEPS,), in_specs=specs, out_specs=specs,
        out_shape=[_hbm_shape(grp[0].shape, F32) for grp in groups for _ in range(4)],
        compiler_params=_params(56),
    )(*[_in_hbm(a) for grp in groups for a in grp])
    return [res[4 * i:4 * i + 4] for i in range(k)]


def kernel(x, positions, ln_in_g, ln_in_b, w_in, b_in, attn_sinks, sgu_ln_g, sgu_ln_b, sgu_w, sgu_b, w_out, b_out, ln_mix_g, ln_mix_b, w_gate, w_up, w_down, ln_ffn_g, ln_ffn_b, loss_target, m_ln_in_g, m_ln_in_b, m_w_in, m_b_in, m_attn_sinks, m_sgu_ln_g, m_sgu_ln_b, m_sgu_w, m_sgu_b, m_w_out, m_b_out, m_ln_mix_g, m_ln_mix_b, m_w_gate, m_w_up, m_w_down, m_ln_ffn_g, m_ln_ffn_b, v_ln_in_g, v_ln_in_b, v_w_in, v_b_in, v_attn_sinks, v_sgu_ln_g, v_sgu_ln_b, v_sgu_w, v_sgu_b, v_w_out, v_b_out, v_ln_mix_g, v_ln_mix_b, v_w_gate, v_w_up, v_w_down, v_ln_ffn_g, v_ln_ffn_b):
    weights = dict(ln_in_g=ln_in_g, ln_in_b=ln_in_b, w_in=w_in, b_in=b_in, attn_sinks=attn_sinks, sgu_ln_g=sgu_ln_g,
                   sgu_ln_b=sgu_ln_b, sgu_w=sgu_w, sgu_b=sgu_b, w_out=w_out, b_out=b_out, ln_mix_g=ln_mix_g,
                   ln_mix_b=ln_mix_b, w_gate=w_gate, w_up=w_up, w_down=w_down, ln_ffn_g=ln_ffn_g, ln_ffn_b=ln_ffn_b)
    mom_m = dict(ln_in_g=m_ln_in_g, ln_in_b=m_ln_in_b, w_in=m_w_in, b_in=m_b_in, attn_sinks=m_attn_sinks,
                 sgu_ln_g=m_sgu_ln_g, sgu_ln_b=m_sgu_ln_b, sgu_w=m_sgu_w, sgu_b=m_sgu_b, w_out=m_w_out, b_out=m_b_out,
                 ln_mix_g=m_ln_mix_g, ln_mix_b=m_ln_mix_b, w_gate=m_w_gate, w_up=m_w_up, w_down=m_w_down,
                 ln_ffn_g=m_ln_ffn_g, ln_ffn_b=m_ln_ffn_b)
    mom_v = dict(ln_in_g=v_ln_in_g, ln_in_b=v_ln_in_b, w_in=v_w_in, b_in=v_b_in, attn_sinks=v_attn_sinks,
                 sgu_ln_g=v_sgu_ln_g, sgu_ln_b=v_sgu_ln_b, sgu_w=v_sgu_w, sgu_b=v_sgu_b, w_out=v_w_out, b_out=v_b_out,
                 ln_mix_g=v_ln_mix_g, ln_mix_b=v_ln_mix_b, w_gate=v_w_gate, w_up=v_w_up, w_down=v_w_down,
                 ln_ffn_g=v_ln_ffn_g, ln_ffn_b=v_ln_ffn_b)
    order = list(weights)
    big = ("w_in", "w_out", "w_gate", "w_up", "w_down")

    s_len = x.shape[1]
    xs = _in_hbm(x.reshape(s_len, D_MODEL))
    tgt = _in_hbm(loss_target.reshape(s_len, D_MODEL))
    pos_row = _in_hbm(positions.reshape(1, s_len))
    g0, b0 = _in_hbm(ln_in_g.reshape(1, D_MODEL)), _in_hbm(ln_in_b.reshape(1, D_MODEL))
    sinks = attn_sinks.reshape(N_Q)
    sgu_w3 = _in_hbm(sgu_w.reshape(N_GRP, BLK, BLK))
    sgu_bt = _in_hbm(sgu_b.reshape(N_GRP, BLK).T)
    b_in, b_out, sgu_ln_g, sgu_ln_b, ln_mix_g, ln_mix_b, ln_ffn_g, ln_ffn_b = (
        _in_hbm(a) for a in (b_in, b_out, sgu_ln_g, sgu_ln_b, ln_mix_g, ln_mix_b, ln_ffn_g, ln_ffn_b))

    col_sharded = ("w_in", "w_gate", "w_up")

    def rowmajor(name, a):
        return jnp.swapaxes(a[0], 0, 1) if name in col_sharded else a[0]

    def as_given(name, a):
        return (jnp.swapaxes(a, 0, 1) if name in col_sharded else a)[None]

    shards = [rowmajor(n, weights[n]) for n in big]
    (gw_in,) = _gather_weights(shards[0:1])
    w_in_full = gw_in.reshape(IN_W, D_MODEL)

    sh_out, sh_gate, sh_up, sh_down = shards[1:]
    *acts, gw_out, gw_gate0 = _ln_inproj(xs, pos_row, g0, b0, w_in_full, b_in, _GatherPlan(
        [(sh_out, (0, OUT_SH), None), (sh_gate, (0, GATE_CUT), None)]))
    q, k, v, su, sv, tc, t1, t2 = (_in_hbm(a) for a in acts)
    mc, gw_gate, gw_up0 = _mixer_fwd(q, k, v, su, sv, sinks, sgu_ln_g, sgu_ln_b, sgu_w3, sgu_bt, _GatherPlan(
        [(sh_gate, (GATE_CUT, FF_SH), gw_gate0), (sh_up, (0, UP_CUT), None)]))
    mc = _in_hbm(mc)
    w_out_full = gw_out.reshape(D_MODEL, D_MODEL)
    r1, gw_up = _outproj(mc, w_out_full, b_out, xs, g0, b0, _GatherPlan([(sh_up, (UP_CUT, FF_SH), gw_up0)]))
    r1 = _in_hbm(r1)
    act, p_act, q_act, h1, gw_down = _ffn_up(r1, ln_mix_g, ln_mix_b, gw_gate, gw_up,
                                             _GatherPlan([(sh_down, (0, FF_SH), None)]))
    act, p_act, q_act = _in_hbm(act), _in_hbm(p_act), _in_hbm(q_act)
    dr2, loss_cols, d_ln_ffn_g, d_ln_ffn_b = _ffn_down_loss(act, gw_down, _in_hbm(h1), ln_ffn_g, ln_ffn_b, tgt)
    dr2 = _in_hbm(dr2)

    dg, du, wire_down, own_down = _ffn_bwd_a(dr2, act, p_act, q_act, gw_down)
    dh1a, wire_gate, own_gate, land_down = _ffn_bwd_g(dr2, _in_hbm(dg), r1, ln_mix_g, ln_mix_b, gw_gate, wire_down)
    dr1, wire_up, own_up, d_ln_mix_g, d_ln_mix_b, land_gate = _ffn_bwd_u(_in_hbm(dh1a), _in_hbm(du), r1, ln_mix_g,
                                                                         ln_mix_b, gw_up, wire_gate)
    dr1 = _in_hbm(dr1)
    dmc, wire_out, own_out, d_b_out = _outproj_bwd(dr1, mc, w_out_full)
    (dq, dkv, dsuv, dbq, dbkv, dbsuv, d_sink, d_sgu_ln_g, d_sgu_ln_b, d_sgu_w, d_sgu_bt, land_up, land_out) = _mixer_bwd(
        q, k, v, su, sv, _in_hbm(dmc), tc, t1, t2, sinks, sgu_ln_g, sgu_ln_b, sgu_w3, sgu_bt, [wire_up, wire_out])
    grad_x, acc_in, d_ln_in_g, d_ln_in_b = _inproj_bwd(_in_hbm(dq), _in_hbm(dkv), _in_hbm(dsuv), dr1, xs, g0, b0,
                                                       w_in_full)

    small_local = dict(
        ln_in_g=d_ln_in_g, ln_in_b=d_ln_in_b, bq=dbq, bkv=dbkv, bsuv=dbsuv, sink=d_sink, sgu_ln_g=d_sgu_ln_g,
        sgu_ln_b=d_sgu_ln_b, sgu_w=d_sgu_w, sgu_bt=d_sgu_bt, b_out=d_b_out, ln_mix_g=d_ln_mix_g, ln_mix_b=d_ln_mix_b,
        ln_ffn_g=d_ln_ffn_g, ln_ffn_b=d_ln_ffn_b, loss=loss_cols)
    *reduced, tot_a, tot_b = _grad_finish(acc_in, [land_out, land_gate, land_up, land_down],
                                          [own_out, own_gate, own_up, own_down], [small_local[nm] for nm in _SMALL_IN])
    small_shape = dict(ln_in_g=(1, D_MODEL), ln_in_b=(1, D_MODEL), sgu_w=(N_GRP, BLK, BLK), sgu_b=(N_GRP, BLK))
    small_params = {nm: tuple(src[nm].reshape(small_shape.get(nm, src[nm].shape)) for src in (weights, mom_m, mom_v))
                    for nm in _SMALL}
    small_out, loss = _small_adamw(_in_hbm(tot_a), _in_hbm(tot_b), small_params)
    loss = loss.reshape(())
    grads, delta, new_m, new_v = {}, {}, {}, {}
    for nm in _SMALL:
        grads[nm], delta[nm], new_m[nm], new_v[nm] = (a.reshape(weights[nm].shape) for a in small_out[nm])

    groups = [(shards[t], reduced[t], rowmajor(nm, mom_m[nm]), rowmajor(nm, mom_v[nm])) for t, nm in enumerate(big)]
    for nm, res in zip(big, _adamw("adamw", groups)):
        grads[nm], delta[nm], new_m[nm], new_v[nm] = (as_given(nm, a) for a in res)

    return (loss, grad_x.reshape(x.shape), *[grads[n] for n in order], *[delta[n] for n in order],
            *[new_m[n] for n in order], *[new_v[n] for n in order])
```

```python
import jax
import jax.numpy as jnp
from jax import lax
from jax.experimental import pallas as pl
from jax.experimental.pallas import tpu as pltpu

F32 = jnp.float32
_MXU = jnp.bfloat16
_WIRE = jnp.bfloat16
_ACT = jnp.bfloat16

D_MODEL = 1024
ATTN_W = 512
SGU_W = 512
HEAD_DIM = 64
N_Q = 8
N_KV = 2
Q_PER_KV = 4
KV_W = 128
BLK = 128
ROT_DIM = 16
ROPE_THETA = 500000.0
N_GRP = 4
GRP_DIM = 128
D_FF = 2816
IN_W = 1792
LN_EPS = 1e-5
ALPHA = 2.0 ** 0.25
N_CHIP = 4
FF_SH = D_FF // N_CHIP
IN_SH = IN_W // N_CHIP
OUT_SH = D_MODEL // N_CHIP
ROW_CHUNK = 32
LANES = 128
FF_PAD = -(-FF_SH // LANES) * LANES
FF_CUT = 512
GATE_CUT, UP_CUT, DOWN_CUT = 192, 224, 608

ADAM_LR = 0.001
ADAM_B1 = 0.9
ADAM_B2 = 0.999
ADAM_EPS = 1e-08
ADAM_WD = 0.01
ADAM_STEP = 10

SQRT_HALF = 0.7071067811865476
INV_SQRT_2PI = 0.3989422804014327
MESH_AXES = ("x", "y", "c")
MESH = pl.DeviceIdType.MESH
MIB = 2 ** 20


def _vmem():
    return pl.BlockSpec(memory_space=pltpu.VMEM)


def _smem():
    return pl.BlockSpec(memory_space=pltpu.SMEM)


def _hbm():
    return pl.BlockSpec(memory_space=pl.ANY)


def _hbm_shape(shape, dtype):
    return pltpu.HBM(shape, dtype)


def _in_hbm(a):
    return pltpu.with_memory_space_constraint(a, pltpu.HBM)


def _params(vmem_mib=48):
    return pltpu.CompilerParams(dimension_semantics=("arbitrary",), vmem_limit_bytes=vmem_mib * MIB)


def _tile(n, cap):
    if n <= cap:
        return n
    for t in range(cap - cap % 16, 0, -16):
        if n % t == 0:
            return t
    raise ValueError((n, cap))


def _rows(tm, width):
    return pl.BlockSpec((tm, width), lambda i: (i, 0))


def _const2(shape):
    return pl.BlockSpec(shape, lambda i: (0,) * len(shape))


def _ln(x, g, b):
    mu = jnp.mean(x, axis=-1, keepdims=True)
    xc = x - mu
    var = jnp.mean(xc * xc, axis=-1, keepdims=True)
    rstd = lax.rsqrt(var + LN_EPS)
    xhat = xc * rstd
    return xhat * g + b, xhat, rstd


def _ln_bwd(dy, xhat, rstd, g):
    gdy = dy * g
    m1 = jnp.mean(gdy, axis=-1, keepdims=True)
    m2 = jnp.mean(gdy * xhat, axis=-1, keepdims=True)
    return rstd * (gdy - m1 - xhat * m2)


def _colsum(a):
    return jnp.sum(a, axis=0, keepdims=True)


def _gelu_and_grad(x):
    cdf = 0.5 * (1.0 + lax.erf(x * SQRT_HALF))
    return x * cdf, cdf + x * jnp.exp(-0.5 * x * x) * INV_SQRT_2PI


def _dot(a, b):
    return jnp.dot(a, b, preferred_element_type=F32)


def _dot_nt(a, b):
    return lax.dot_general(a, b, (((1,), (1,)), ((), ())), preferred_element_type=F32)


def _dot_tn(a, b):
    return lax.dot_general(a, b, (((0,), (0,)), ((), ())), preferred_element_type=F32)


def _rope(t, tc, t1, t2):
    n = t.shape[1]
    rep = n // 128
    if rep > 1:
        tc, t1, t2 = (jnp.tile(a, (1, rep)) for a in (tc, t1, t2))
    return t * tc + pltpu.roll(t, n - 8, 1) * t1 + pltpu.roll(t, 8, 1) * t2


def _rope_bwd(d, tc, t1, t2):
    n = d.shape[1]
    rep = n // 128
    if rep > 1:
        tc, t1, t2 = (jnp.tile(a, (1, rep)) for a in (tc, t1, t2))
    return d * tc + pltpu.roll(d * t1, 8, 1) + pltpu.roll(d * t2, n - 8, 1)


def _causal_w(w_ref, h):
    t = lax.broadcasted_iota(jnp.int32, (BLK, BLK), 0)
    s = lax.broadcasted_iota(jnp.int32, (BLK, BLK), 1)
    return jnp.where(s <= t, w_ref[h], 0.0)


def _lane_put(vals, width):
    rows = vals[0].shape[0]
    lane = lax.broadcasted_iota(jnp.int32, (rows, width), 1)
    out = jnp.zeros((rows, width), F32)
    for k, v in enumerate(vals):
        out = out + jnp.where(lane == k, v, 0.0)
    return out


def _rope_consts():
    lane = jnp.arange(128) % HEAD_DIM
    rot = lane < ROT_DIM
    pair = (2 * (lane % (ROT_DIM // 2))).astype(F32)
    freq = jnp.where(rot, ROPE_THETA ** (-pair / ROT_DIM), 0.0)
    rows = [freq, rot.astype(F32), 1.0 - rot.astype(F32), (lane < ROT_DIM // 2).astype(F32),
            jnp.logical_and(lane >= ROT_DIM // 2, rot).astype(F32)]
    rows += [jnp.zeros((128,), F32)] * 3
    return jnp.stack(rows).astype(F32)


def _ln_inproj(x, pos_row, g0, b0, w_in, b_in, plan):
    s_len = x.shape[0]
    tm = _tile(s_len, 512)
    m, n = len(plan.operands()), plan.n

    def body(x_ref, pos_ref, g_ref, b_ref, w_ref, bi_ref, rc_ref, *rest):
        q_ref, k_ref, v_ref, su_ref, sv_ref, tc_ref, t1_ref, t2_ref = rest[m:m + 8]
        gather = plan.bind(rest[:m], rest[m + 8:m + 8 + n], rest[m + 8 + n:])
        i = pl.program_id(0)

        @pl.when(i == 0)
        def _():
            gather.start()

        h0, _, _ = _ln(x_ref[...], g_ref[...], b_ref[...])
        proj = _dot_nt(h0.astype(_MXU), w_ref[...]) + bi_ref[...]
        pos = jnp.broadcast_to(pos_ref[...].astype(F32), (128, tm))
        ang = jnp.transpose(pos) * rc_ref[0:1, :]
        cs = jnp.cos(ang)
        sn = jnp.sin(ang)
        tc = cs * rc_ref[1:2, :] + rc_ref[2:3, :]
        t1 = -sn * rc_ref[3:4, :]
        t2 = sn * rc_ref[4:5, :]
        tc_ref[...] = tc
        t1_ref[...] = t1
        t2_ref[...] = t2
        q = _rope(proj[:, 0:ATTN_W], tc, t1, t2) * (HEAD_DIM ** -0.5)
        q_ref[...] = q.astype(_MXU)
        k_ref[...] = _rope(proj[:, ATTN_W:ATTN_W + KV_W], tc, t1, t2).astype(_MXU)
        v_ref[...] = proj[:, ATTN_W + KV_W:ATTN_W + 2 * KV_W].astype(_MXU)
        su_ref[...] = proj[:, ATTN_W + 2 * KV_W:ATTN_W + 2 * KV_W + SGU_W]
        sv_ref[...] = proj[:, ATTN_W + 2 * KV_W + SGU_W:IN_W]

        last = pl.num_programs(0) - 1

        @pl.when(i == jnp.maximum(last - 1, 0))
        def _():
            gather.pass_on()

        @pl.when(i == last)
        def _():
            gather.finish()

    sd = _hbm_shape
    return pl.pallas_call(
        body, name="ln_inproj", grid=(s_len // tm,),
        in_specs=[_rows(tm, D_MODEL), pl.BlockSpec((1, tm), lambda i: (0, i)), _const2((1, D_MODEL)),
                  _const2((1, D_MODEL)), _vmem(),
                  _const2((1, IN_W)), _const2((8, 128))] + plan.in_specs(),
        out_specs=[_rows(tm, ATTN_W), _rows(tm, KV_W), _rows(tm, KV_W), _rows(tm, SGU_W), _rows(tm, SGU_W),
                   _rows(tm, 128), _rows(tm, 128), _rows(tm, 128)] + plan.out_specs(),
        out_shape=[sd((s_len, ATTN_W), _MXU), sd((s_len, KV_W), _MXU), sd((s_len, KV_W), _MXU),
                   sd((s_len, SGU_W), F32), sd((s_len, SGU_W), F32),
                   sd((s_len, 128), F32), sd((s_len, 128), F32), sd((s_len, 128), F32)] + plan.out_shapes(),
        scratch_shapes=plan.scratch(),
        compiler_params=_params(56),
    )(x, pos_row, g0, b0, w_in, b_in, _rope_consts(), *plan.operands())


def _band_mask_t(first_block):
    kj = lax.broadcasted_iota(jnp.int32, (2 * BLK, BLK), 0)
    qi = lax.broadcasted_iota(jnp.int32, (2 * BLK, BLK), 1)
    shut = jnp.where(first_block, 2 * BLK, 0)
    prev_ok = jnp.logical_and(kj < BLK, kj > qi + shut)
    cur_ok = jnp.logical_and(kj >= BLK, (kj - BLK) <= qi)
    return jnp.logical_or(prev_ok, cur_ok)


def _attn_probs_t(kh, qh, sink, allowed_t):
    s = jnp.where(allowed_t, _dot_nt(kh, qh), -1e30)
    m = jnp.maximum(jnp.max(s, axis=0, keepdims=True), sink)
    p = jnp.exp(s - m)
    ps = jnp.exp(sink - m)
    inv = 1.0 / (jnp.sum(p, axis=0, keepdims=True) + ps)
    return p * inv, ps * inv


def _sgu_mix(gv, lg, lb, w_ref, bt_ref):
    vv, vhat, rstd = _ln(gv, lg, lb)
    vvb = vv.astype(_MXU)
    wcs, mixed = [], []
    for h in range(N_GRP):
        wc = _causal_w(w_ref, h).astype(_MXU)
        wcs.append(wc)
        mixed.append(_dot(wc, vvb[:, h * GRP_DIM:(h + 1) * GRP_DIM]) + bt_ref[:, h:h + 1])
    return jnp.concatenate(mixed, axis=1), vhat, rstd, vvb, wcs


def _mixer_fwd(q, k, v, su, sv, sinks, sg, sb, sgu_w, sgu_bt, plan):
    s_len = q.shape[0]
    nb = s_len // BLK
    per = 2 if nb % 2 == 0 else 1
    steps = nb // per
    m, n = len(plan.operands()), plan.n

    def body(q_ref, kc_ref, kp_ref, vc_ref, vp_ref, su_ref, sv_ref, sink_ref, lg_ref, lb_ref, w_ref, bt_ref, *rest):
        mc_ref = rest[m]
        gather = plan.bind(rest[:m], rest[m + 1:m + 1 + n], rest[m + 1 + n:])
        i = pl.program_id(0)

        @pl.when(i == 0)
        def _():
            gather.start()

        @pl.when(i == max(steps - 2, 0))
        def _():
            gather.pass_on()

        @pl.when(i == steps - 1)
        def _():
            gather.finish()

        for s in range(per):
            rows = slice(s * BLK, (s + 1) * BLK)
            before = slice((s - 1) * BLK, s * BLK)
            k_prev = kp_ref[...] if s == 0 else kc_ref[before, :]
            v_prev = vp_ref[...] if s == 0 else vc_ref[before, :]
            allowed_t = _band_mask_t(i == 0 if s == 0 else False)
            kb = jnp.concatenate([k_prev, kc_ref[rows, :]], axis=0)
            vb = jnp.concatenate([v_prev, vc_ref[rows, :]], axis=0)
            qv = q_ref[rows, :]
            outs = []
            allowed_g = jnp.tile(allowed_t, (1, Q_PER_KV))
            for g in range(N_KV):
                heads = range(g * Q_PER_KV, (g + 1) * Q_PER_KV)
                kh = kb[:, g * HEAD_DIM:(g + 1) * HEAD_DIM]
                vh = vb[:, g * HEAD_DIM:(g + 1) * HEAD_DIM]
                q_g = jnp.concatenate([qv[:, h * HEAD_DIM:(h + 1) * HEAD_DIM] for h in heads], axis=0)
                sink_g = jnp.concatenate([jnp.full((1, BLK), sink_ref[h], F32) for h in heads], axis=1)
                probs_t, _ = _attn_probs_t(kh, q_g, sink_g, allowed_g)
                o_g = _dot_tn(probs_t.astype(_MXU), vh)
                outs += [o_g[hh * BLK:(hh + 1) * BLK, :] for hh in range(Q_PER_KV)]
            u = _gelu_and_grad(su_ref[rows, :])[0]
            gv = _gelu_and_grad(sv_ref[rows, :])[0]
            mixed = _sgu_mix(gv, lg_ref[...], lb_ref[...], w_ref, bt_ref)[0]
            mc_ref[rows, :] = jnp.concatenate(outs + [u * mixed], axis=1).astype(_MXU)

    cur = lambda w: pl.BlockSpec((per * BLK, w), lambda i: (i, 0))
    prev = lambda w: pl.BlockSpec((BLK, w), lambda i: (jnp.maximum(per * i - 1, 0), 0))
    return pl.pallas_call(
        body, name="mixer_fwd", grid=(steps,),
        in_specs=[cur(ATTN_W), cur(KV_W), prev(KV_W), cur(KV_W), prev(KV_W), cur(SGU_W), cur(SGU_W), _smem(),
                  _const2((1, SGU_W)), _const2((1, SGU_W)), _const2((N_GRP, BLK, BLK)), _const2((BLK, N_GRP))]
        + plan.in_specs(),
        out_specs=[cur(D_MODEL)] + plan.out_specs(),
        out_shape=[_hbm_shape((s_len, D_MODEL), _MXU)] + plan.out_shapes(),
        scratch_shapes=plan.scratch(),
        compiler_params=_params(56),
    )(q, k, k, v, v, su, sv, sinks, sg, sb, sgu_w, sgu_bt, *plan.operands())


def _outproj(mc, w_out, b_out, x, g0, b0, plan):
    s_len = x.shape[0]
    tm = _tile(s_len, 512)
    m, n = len(plan.operands()), plan.n

    def body(mc_ref, w_ref, bo_ref, x_ref, g_ref, b_ref, *rest):
        r1_ref = rest[m]
        gather = plan.bind(rest[:m], rest[m + 1:m + 1 + n], rest[m + 1 + n:])
        i = pl.program_id(0)

        @pl.when(i == 0)
        def _():
            gather.start()

        h0, _, _ = _ln(x_ref[...], g_ref[...], b_ref[...])
        r1_ref[...] = ALPHA * h0 + (_dot(mc_ref[...], w_ref[...]) + bo_ref[...])

        last = pl.num_programs(0) - 1

        @pl.when(i == jnp.maximum(last - 1, 0))
        def _():
            gather.pass_on()

        @pl.when(i == last)
        def _():
            gather.finish()

    return pl.pallas_call(
        body, name="outproj", grid=(s_len // tm,),
        in_specs=[_rows(tm, D_MODEL), _vmem(), _const2((1, D_MODEL)), _rows(tm, D_MODEL),
                  _const2((1, D_MODEL)), _const2((1, D_MODEL))] + plan.in_specs(),
        out_specs=[_rows(tm, D_MODEL)] + plan.out_specs(),
        out_shape=[_hbm_shape((s_len, D_MODEL), F32)] + plan.out_shapes(),
        scratch_shapes=plan.scratch(),
        compiler_params=_params(40),
    )(mc, w_out, b_out, x, g0, b0, *plan.operands())


def _ffn_spec(tm):
    return pl.BlockSpec((N_CHIP, tm, FF_SH), lambda i: (0, i, 0))


def _act_spec(tm):
    return pl.BlockSpec((N_CHIP, tm, FF_PAD), lambda i: (0, i, 0))


def _ffn_up(name, r1, g1, b1, wg, wu, plan, cols, prior=()):
    s_len = r1.shape[0]
    lo, hi = cols
    width = hi - lo
    block_w = -(-width // LANES) * LANES
    assert lo % block_w == 0 and lo + block_w <= FF_PAD
    first = not prior
    tm = _tile(s_len, 512)
    m, n = len(plan.operands()), plan.n
    n_out = 4 if first else 3

    def body(r1_ref, g_ref, b_ref, wg_hbm, wu_hbm, *rest):
        rest = rest[len(prior):]
        a_ref, p_ref, q_ref = rest[m:m + 3]
        wg_buf, wu_buf, w_sems = rest[-3:]
        gather = plan.bind(rest[:m], rest[m + n_out:m + n_out + n], rest[m + n_out + n:-3])
        i = pl.program_id(0)

        def fetch(j):
            return [pltpu.make_async_copy(src.at[j, lo:hi, :], dst.at[j], w_sems.at[k, j])
                    for k, (src, dst) in enumerate(((wg_hbm, wg_buf), (wu_hbm, wu_buf)))]

        @pl.when(i == 0)
        def _():
            gather.start()
            for j in range(N_CHIP):
                for cp in fetch(j):
                    cp.start()

        h1, _, _ = _ln(r1_ref[...], g_ref[...], b_ref[...])
        if first:
            rest[m + 3][...] = h1
        h1b = h1.astype(_MXU)
        for j in range(N_CHIP):
            @pl.when(i == 0)
            def _():
                for cp in fetch(j):
                    cp.wait()

            g = _dot_nt(h1b, wg_buf[j])
            u = _dot_nt(h1b, wu_buf[j])
            silu, sg = _silu_parts(g)
            a_ref[j, :, 0:width] = (silu * u).astype(_MXU)
            p_ref[j, :, 0:width] = silu.astype(_ACT)
            q_ref[j, :, 0:width] = (u * (sg * (1.0 + g * (1.0 - sg)))).astype(_ACT)

        last = pl.num_programs(0) - 1

        @pl.when(i == jnp.maximum(last - 1, 0))
        def _():
            gather.pass_on()

        @pl.when(i == last)
        def _():
            gather.finish()

    sd = _hbm_shape((N_CHIP, s_len, FF_PAD), _ACT)
    c = _const2((1, D_MODEL))
    out_block = pl.BlockSpec((N_CHIP, tm, block_w), lambda i: (0, i, lo // block_w))
    return pl.pallas_call(
        body, name=name, grid=(s_len // tm,),
        in_specs=[_rows(tm, D_MODEL), c, c, _hbm(), _hbm()] + [_hbm()] * len(prior) + plan.in_specs(),
        out_specs=[out_block] * 3 + [_rows(tm, D_MODEL)] * first + plan.out_specs(),
        out_shape=[_hbm_shape((N_CHIP, s_len, FF_PAD), _MXU), sd, sd] + [_hbm_shape((s_len, D_MODEL), F32)] * first
        + plan.out_shapes(),
        input_output_aliases={5 + k: k for k in range(len(prior))},
        scratch_shapes=plan.scratch() + [pltpu.VMEM((N_CHIP, width, D_MODEL), _MXU)] * 2
        + [pltpu.SemaphoreType.DMA((2, N_CHIP))],
        compiler_params=_params(60),
    )(r1, g1, b1, wg, wu, *prior, *plan.operands())


def _silu_parts(g):
    sg = 1.0 / (1.0 + jnp.exp(-g))
    return g * sg, sg


def _ffn_down_loss(act, wd, h1, g2, b2, target):
    s_len = h1.shape[0]
    tm = _tile(s_len, 512)

    parts = 2 if tm % 32 == 0 else 1
    sub = tm // parts

    def body(a_ref, wd_ref, h1_ref, g2_ref, b2_ref, t_ref, dr2_ref, loss_ref, dg2_ref, db2_ref):
        i = pl.program_id(0)

        @pl.when(i == 0)
        def _():
            loss_ref[...] = jnp.zeros_like(loss_ref)
            dg2_ref[...] = jnp.zeros_like(dg2_ref)
            db2_ref[...] = jnp.zeros_like(db2_ref)

        for part in range(parts):
            rows = slice(part * sub, (part + 1) * sub)
            f = jnp.zeros((sub, D_MODEL), F32)
            for j in range(N_CHIP):
                f = f + _dot(a_ref[j, rows, 0:FF_SH], wd_ref[j])
            h2, r2hat, rstd2 = _ln(ALPHA * h1_ref[rows, :] + f, g2_ref[...], b2_ref[...])
            diff = h2 - t_ref[rows, :]
            dh2 = diff * (1.0 / D_MODEL)
            loss_ref[...] += _colsum(diff * diff)
            dg2_ref[...] += _colsum(dh2 * r2hat)
            db2_ref[...] += _colsum(dh2)
            dr2_ref[rows, :] = _ln_bwd(dh2, r2hat, rstd2, g2_ref[...])

    vec = _hbm_shape((1, D_MODEL), F32)
    c = _const2((1, D_MODEL))
    return pl.pallas_call(
        body, name="ffn_down_loss", grid=(s_len // tm,),
        in_specs=[_act_spec(tm), _vmem(), _rows(tm, D_MODEL), c, c, _rows(tm, D_MODEL)],
        out_specs=[_rows(tm, D_MODEL), c, c, c],
        out_shape=[_hbm_shape((s_len, D_MODEL), F32), vec, vec, vec],
        compiler_params=_params(48),
    )(act, wd, h1, g2, b2, target)


def _ffn_bwd_a(dr2, act, p_act, q_act, wd):
    s_len = dr2.shape[0]
    tm = _tile(s_len, 512)

    def body(dr2_ref, a_ref, p_ref, q_ref, wd_ref, dg_ref, du_ref, wire_ref, own_ref,
             dwd_ref, land_ref, send_sem, recv_sem):
        i = pl.program_id(0)

        @pl.when(i == 0)
        def _():
            dwd_ref[...] = jnp.zeros_like(dwd_ref)

        dfb = dr2_ref[...].astype(_MXU)
        for j in range(N_CHIP):
            da = _dot_nt(dfb, wd_ref[j])
            dg_ref[j] = (da * q_ref[j, :, 0:FF_SH].astype(F32)).astype(_MXU)
            du_ref[j] = (da * p_ref[j, :, 0:FF_SH].astype(F32)).astype(_MXU)
            dwd_ref[j * FF_SH:(j + 1) * FF_SH, :] += _dot_tn(a_ref[j, :, 0:FF_SH], dfb)

        @pl.when(i == pl.num_programs(0) - 1)
        def _():
            _pair_reduce(dwd_ref, wire_ref, own_ref, land_ref, send_sem, recv_sem)

    sd = _hbm_shape((N_CHIP, s_len, FF_SH), _MXU)
    half = (N_CHIP, FF_SH // 2, D_MODEL)
    return pl.pallas_call(
        body, name="ffn_bwd_a", grid=(s_len // tm,),
        in_specs=[_rows(tm, D_MODEL), _act_spec(tm), _act_spec(tm), _act_spec(tm), _vmem()],
        out_specs=[_ffn_spec(tm), _ffn_spec(tm), _vmem(), _vmem()],
        out_shape=[sd, sd] + _pair_out_shapes(half),
        scratch_shapes=_pair_scratch((D_FF, D_MODEL), half),
        compiler_params=_params(61),
    )(dr2, act, p_act, q_act, wd)


def _ffn_bwd_g(dr2, dg, r1, g1, b1, wg, prev_wire):
    s_len = dr2.shape[0]
    tm = _tile(s_len, 512)

    def body(dr2_ref, dg_ref, r1_ref, g1_ref, b1_ref, wg_ref, pw_ref, dh1_ref, wire_ref, own_ref, pl_ref,
             dwg_ref, land_ref, send_sem, recv_sem, xl_ref, x_send, x_recv, x_flush):
        i = pl.program_id(0)
        exchange = _ChipExchange(pw_ref, xl_ref, x_send, x_recv)

        @pl.when(i == 0)
        def _():
            exchange.start()
            dwg_ref[...] = jnp.zeros_like(dwg_ref)

        h1, _, _ = _ln(r1_ref[...], g1_ref[...], b1_ref[...])
        h1b = h1.astype(_MXU)
        dh1 = ALPHA * dr2_ref[...]
        for j in range(N_CHIP):
            dgj = dg_ref[j]
            dh1 = dh1 + _dot(dgj, wg_ref[j])
            dwg_ref[j * FF_SH:(j + 1) * FF_SH, :] += _dot_tn(dgj, h1b)
        dh1_ref[...] = dh1

        @pl.when(i == pl.num_programs(0) - 1)
        def _():
            _pair_reduce(dwg_ref, wire_ref, own_ref, land_ref, send_sem, recv_sem)
            exchange.finish_to(pl_ref, x_flush)

    c = _const2((1, D_MODEL))
    half = (N_CHIP, FF_SH // 2, D_MODEL)
    return pl.pallas_call(
        body, name="ffn_bwd_g", grid=(s_len // tm,),
        in_specs=[_rows(tm, D_MODEL), _ffn_spec(tm), _rows(tm, D_MODEL), c, c, _vmem(), _vmem()],
        out_specs=[_rows(tm, D_MODEL), _vmem(), _vmem(), _hbm()],
        out_shape=[_hbm_shape((s_len, D_MODEL), F32)] + _pair_out_shapes(half) + [_ChipExchange.land_shape(prev_wire)],
        scratch_shapes=_pair_scratch((D_FF, D_MODEL), half) + _ChipExchange.scratch(prev_wire),
        compiler_params=_params(58),
    )(dr2, dg, r1, g1, b1, wg, prev_wire)


def _ffn_bwd_u(dh1a, du, r1, g1, b1, wu, prev_wire):
    s_len = dh1a.shape[0]
    tm = _tile(s_len, 512)

    def body(dh1_ref, du_ref, r1_ref, g1_ref, b1_ref, wu_ref, pw_ref,
             dr1_ref, wire_ref, own_ref, dg1_ref, db1_ref, pl_ref,
             dwu_ref, land_ref, send_sem, recv_sem, xl_ref, x_send, x_recv, x_flush):
        i = pl.program_id(0)
        exchange = _ChipExchange(pw_ref, xl_ref, x_send, x_recv)

        @pl.when(i == 0)
        def _():
            exchange.start()
            dwu_ref[...] = jnp.zeros_like(dwu_ref)
            dg1_ref[...] = jnp.zeros_like(dg1_ref)
            db1_ref[...] = jnp.zeros_like(db1_ref)

        h1, r1hat, rstd1 = _ln(r1_ref[...], g1_ref[...], b1_ref[...])
        h1b = h1.astype(_MXU)
        dh1 = dh1_ref[...]
        for j in range(N_CHIP):
            duj = du_ref[j]
            dh1 = dh1 + _dot(duj, wu_ref[j])
            dwu_ref[j * FF_SH:(j + 1) * FF_SH, :] += _dot_tn(duj, h1b)
        dg1_ref[...] += _colsum(dh1 * r1hat)
        db1_ref[...] += _colsum(dh1)
        dr1_ref[...] = _ln_bwd(dh1, r1hat, rstd1, g1_ref[...])

        @pl.when(i == pl.num_programs(0) - 1)
        def _():
            _pair_reduce(dwu_ref, wire_ref, own_ref, land_ref, send_sem, recv_sem)
            exchange.finish_to(pl_ref, x_flush)

    vec = _hbm_shape((1, D_MODEL), F32)
    c = _const2((1, D_MODEL))
    half = (N_CHIP, FF_SH // 2, D_MODEL)
    return pl.pallas_call(
        body, name="ffn_bwd_u", grid=(s_len // tm,),
        in_specs=[_rows(tm, D_MODEL), _ffn_spec(tm), _rows(tm, D_MODEL), c, c, _vmem(), _vmem()],
        out_specs=[_rows(tm, D_MODEL), _vmem(), _vmem(), c, c, _hbm()],
        out_shape=[_hbm_shape((s_len, D_MODEL), F32)] + _pair_out_shapes(half)
        + [vec, vec, _ChipExchange.land_shape(prev_wire)],
        scratch_shapes=_pair_scratch((D_FF, D_MODEL), half) + _ChipExchange.scratch(prev_wire),
        compiler_params=_params(58),
    )(dh1a, du, r1, g1, b1, wu, prev_wire)


def _outproj_bwd(dr1, mc, w_out):
    s_len = dr1.shape[0]
    tm = _tile(s_len, 512)

    def body(dr1_ref, mc_ref, w_ref, dmc_ref, wire_ref, own_ref, db_ref, dw_ref, land_ref, send_sem, recv_sem):
        i = pl.program_id(0)

        @pl.when(i == 0)
        def _():
            dw_ref[...] = jnp.zeros_like(dw_ref)
            db_ref[...] = jnp.zeros_like(db_ref)

        d = dr1_ref[...]
        db_ref[...] += _colsum(d)
        db16 = d.astype(_MXU)
        dmc_ref[...] = _dot_nt(db16, w_ref[...])
        dw_ref[...] += _dot_tn(mc_ref[...], db16)

        @pl.when(i == pl.num_programs(0) - 1)
        def _():
            _pair_reduce(dw_ref, wire_ref, own_ref, land_ref, send_sem, recv_sem)

    half = (N_CHIP, OUT_SH // 2, D_MODEL)
    return pl.pallas_call(
        body, name="outproj_bwd", grid=(s_len // tm,),
        in_specs=[_rows(tm, D_MODEL), _rows(tm, D_MODEL), _vmem()],
        out_specs=[_rows(tm, D_MODEL), _vmem(), _vmem(), _const2((1, D_MODEL))],
        out_shape=[_hbm_shape((s_len, D_MODEL), F32)] + _pair_out_shapes(half) + [_hbm_shape((1, D_MODEL), F32)],
        scratch_shapes=_pair_scratch((D_MODEL, D_MODEL), half),
        compiler_params=_params(48),
    )(dr1, mc, w_out)


def _mixer_bwd(q, k, v, su, sv, dmc, tc, t1, t2, sinks, sg, sb, sgu_w, sgu_bt, prev_wires):
    s_len = q.shape[0]
    nb = s_len // BLK
    per = next(p for p in (4, 2, 1) if nb % p == 0)
    steps = nb // per

    def body(q_ref, kc_ref, kp_ref, vc_ref, vp_ref, su_ref, sv_ref, dmc_ref,
             tc_ref, t1_ref, t2_ref, tcp_ref, t1p_ref, t2p_ref,
             sink_ref, lg_ref, lb_ref, w_ref, bt_ref, pw0_ref, pw1_ref,
             dq_ref, dkv_ref, dsuv_ref, dbq_ref, dbkv_ref, dbsuv_ref,
             dsink_ref, dlg_ref, dlb_ref, dw_ref, dbt_ref, pl0_ref, pl1_ref, carry_ref,
             xl0_ref, x0_send, x0_recv, x0_flush, xl1_ref, x1_send, x1_recv, x1_flush):
        i = pl.program_id(0)
        exchanges = [(_ChipExchange(pw0_ref, xl0_ref, x0_send, x0_recv), pl0_ref, x0_flush),
                     (_ChipExchange(pw1_ref, xl1_ref, x1_send, x1_recv), pl1_ref, x1_flush)]

        @pl.when(i == 0)
        def _():
            for exchange, _, _ in exchanges:
                exchange.start()

        @pl.when(i == 0)
        def _():
            for r in (dbq_ref, dbkv_ref, dbsuv_ref, dsink_ref, dlg_ref, dlb_ref, dw_ref, dbt_ref, carry_ref):
                r[...] = jnp.zeros_like(r)

        def emit_kv(fin, t):
            if t == 0:
                tables = (tcp_ref[...], t1p_ref[...], t2p_ref[...])
            else:
                before = slice((t - 1) * BLK, t * BLK)
                tables = (tc_ref[before, :], t1_ref[before, :], t2_ref[before, :])
            dk = _rope_bwd(fin[:, 0:KV_W], *tables)
            out = jnp.concatenate([dk, fin[:, KV_W:2 * KV_W]], axis=1)
            dkv_ref[t * BLK:(t + 1) * BLK, :] = out.astype(_MXU)
            dbkv_ref[...] += _colsum(out)

        def one_block(s):
            rows = slice(s * BLK, (s + 1) * BLK)
            before = slice((s - 1) * BLK, s * BLK)
            k_prev = kp_ref[...] if s == 0 else kc_ref[before, :]
            v_prev = vp_ref[...] if s == 0 else vc_ref[before, :]
            allowed_t = _band_mask_t(i == 0 if s == 0 else False)
            kb = jnp.concatenate([k_prev, kc_ref[rows, :]], axis=0)
            vb = jnp.concatenate([v_prev, vc_ref[rows, :]], axis=0)
            qv = q_ref[rows, :]
            dmc = dmc_ref[rows, :]
            dqs, dks, dvs, dsinks = [], [], [], []
            allowed_g = jnp.tile(allowed_t, (1, Q_PER_KV))
            for g in range(N_KV):
                heads = range(g * Q_PER_KV, (g + 1) * Q_PER_KV)
                kh = kb[:, g * HEAD_DIM:(g + 1) * HEAD_DIM]
                vh = vb[:, g * HEAD_DIM:(g + 1) * HEAD_DIM]
                q_g = jnp.concatenate([qv[:, h * HEAD_DIM:(h + 1) * HEAD_DIM] for h in heads], axis=0)
                do_g = jnp.concatenate([dmc[:, h * HEAD_DIM:(h + 1) * HEAD_DIM] for h in heads], axis=0).astype(_MXU)
                sink_g = jnp.concatenate([jnp.full((1, BLK), sink_ref[h], F32) for h in heads], axis=1)
                probs_t, psink = _attn_probs_t(kh, q_g, sink_g, allowed_g)
                dvs.append(_dot(probs_t.astype(_MXU), do_g))
                dp_t = _dot_nt(vh, do_g)
                rd = jnp.sum(probs_t * dp_t, axis=0, keepdims=True)
                ds_t = (probs_t * (dp_t - rd)).astype(_MXU)
                ps_rd = psink * rd
                for hh in range(Q_PER_KV):
                    dsinks.append(-jnp.sum(ps_rd[:, hh * BLK:(hh + 1) * BLK], axis=1, keepdims=True))
                dq_g = _dot_tn(ds_t, kh)
                dqs += [dq_g[hh * BLK:(hh + 1) * BLK, :] for hh in range(Q_PER_KV)]
                dks.append(_dot(ds_t, q_g))
            dq = _rope_bwd(jnp.concatenate(dqs, axis=1) * (HEAD_DIM ** -0.5),
                           tc_ref[rows, :], t1_ref[rows, :], t2_ref[rows, :])
            dq_ref[rows, :] = dq.astype(_MXU)
            dbq_ref[...] += _colsum(dq)
            dsink_ref[...] += _lane_put(dsinks, 128)
            contrib = jnp.concatenate(dks + dvs, axis=1)

            lg = lg_ref[...]
            u, du_dsu = _gelu_and_grad(su_ref[rows, :])
            gv, dgv_dsv = _gelu_and_grad(sv_ref[rows, :])
            mixed, vhat, rstd, vvb, wcs = _sgu_mix(gv, lg, lb_ref[...], w_ref, bt_ref)
            dsgu = dmc[:, ATTN_W:D_MODEL]
            dsu = dsgu * mixed * du_dsu
            dmixed = dsgu * u
            tri_t = lax.broadcasted_iota(jnp.int32, (BLK, BLK), 0)
            tri_s = lax.broadcasted_iota(jnp.int32, (BLK, BLK), 1)
            dvv, dbs = [], []
            for h in range(N_GRP):
                dm = dmixed[:, h * GRP_DIM:(h + 1) * GRP_DIM]
                dmb = dm.astype(_MXU)
                dbs.append(jnp.sum(dm, axis=1, keepdims=True))
                dw_ref[h] += jnp.where(tri_s <= tri_t, _dot_nt(dmb, vvb[:, h * GRP_DIM:(h + 1) * GRP_DIM]), 0.0)
                dvv.append(_dot_tn(wcs[h], dmb))
            dvv = jnp.concatenate(dvv, axis=1)
            dbt_ref[...] += _lane_put(dbs, 128)
            dlg_ref[...] += _colsum(dvv * vhat)
            dlb_ref[...] += _colsum(dvv)
            dsv = _ln_bwd(dvv, vhat, rstd, lg) * dgv_dsv
            dsuv = jnp.concatenate([dsu, dsv], axis=1)
            dsuv_ref[rows, :] = dsuv.astype(_MXU)
            dbsuv_ref[...] += _colsum(dsuv)
            return contrib

        @pl.when(i < steps)
        def _():
            contribs = [one_block(s) for s in range(per)]
            for t in range(per):
                top = carry_ref[...] if t == 0 else contribs[t - 1][BLK:2 * BLK, :]
                emit_kv(top + contribs[t][0:BLK, :], t)
            carry_ref[...] = contribs[per - 1][BLK:2 * BLK, :]

        @pl.when(i == steps)
        def _():
            emit_kv(carry_ref[...], 0)
            if per > 1:
                dkv_ref[BLK:per * BLK, :] = jnp.zeros(((per - 1) * BLK, 2 * KV_W), _MXU)
            for exchange, landed, flush_sem in exchanges:
                exchange.finish_to(landed, flush_sem)

    last = steps - 1
    cur = lambda w: pl.BlockSpec((per * BLK, w), lambda i: (jnp.minimum(i, last), 0))
    prev = lambda w: pl.BlockSpec((BLK, w), lambda i: (jnp.clip(per * i - 1, 0, nb - 1), 0))
    shifted = pl.BlockSpec((per * BLK, 2 * KV_W), lambda i: (i, 0))
    sd = _hbm_shape
    return pl.pallas_call(
        body, name="mixer_bwd", grid=(steps + 1,),
        in_specs=[cur(ATTN_W), cur(KV_W), prev(KV_W), cur(KV_W), prev(KV_W), cur(SGU_W), cur(SGU_W), cur(D_MODEL),
                  cur(128), cur(128), cur(128), prev(128), prev(128), prev(128),
                  _smem(), _const2((1, SGU_W)), _const2((1, SGU_W)), _const2((N_GRP, BLK, BLK)), _const2((BLK, N_GRP)),
                  _vmem(), _vmem()],
        out_specs=[cur(ATTN_W), shifted, cur(2 * SGU_W),
                   _const2((1, ATTN_W)), _const2((1, 2 * KV_W)), _const2((1, 2 * SGU_W)),
                   _const2((1, 128)), _const2((1, SGU_W)), _const2((1, SGU_W)),
                   _const2((N_GRP, BLK, BLK)), _const2((BLK, 128)), _hbm(), _hbm()],
        out_shape=[sd((s_len, ATTN_W), _MXU), sd((s_len + per * BLK, 2 * KV_W), _MXU), sd((s_len, 2 * SGU_W), _MXU),
                   sd((1, ATTN_W), F32), sd((1, 2 * KV_W), F32), sd((1, 2 * SGU_W), F32),
                   sd((1, 128), F32), sd((1, SGU_W), F32), sd((1, SGU_W), F32),
                   sd((N_GRP, BLK, BLK), F32), sd((BLK, 128), F32)]
        + [_ChipExchange.land_shape(w) for w in prev_wires],
        scratch_shapes=[pltpu.VMEM((BLK, 2 * KV_W), F32)] + _ChipExchange.scratch(prev_wires[0])
        + _ChipExchange.scratch(prev_wires[1]),
        compiler_params=_params(40),
    )(q, k, k, v, v, su, sv, dmc, tc, t1, t2, tc, t1, t2, sinks, sg, sb, sgu_w, sgu_bt, *prev_wires)


def _inproj_bwd(dq, dkv_late, dsuv, dr1, x, g0, b0, w_in):
    s_len = x.shape[0]
    tm = _tile(s_len, 512)
    assert tm % BLK == 0
    per = tm // BLK
    cuts = ((0, ATTN_W), (ATTN_W, ATTN_W + 2 * KV_W), (ATTN_W + 2 * KV_W, IN_W))

    def body(dq_ref, *rest):
        dkv_refs = rest[:per]
        dsuv_ref, dr1_ref, x_ref, g_ref, b_ref, w_ref, dx_ref, dw_ref, dg_ref, db_ref = rest[per:]
        i = pl.program_id(0)

        @pl.when(i == 0)
        def _():
            dw_ref[...] = jnp.zeros_like(dw_ref)
            dg_ref[...] = jnp.zeros_like(dg_ref)
            db_ref[...] = jnp.zeros_like(db_ref)

        h0, xhat, rstd = _ln(x_ref[...], g_ref[...], b_ref[...])
        h0b = h0.astype(_MXU)
        dh0 = ALPHA * dr1_ref[...]
        dkv = jnp.concatenate([r[...] for r in dkv_refs], axis=0)
        for (lo, hi), d in zip(cuts, (dq_ref[...], dkv, dsuv_ref[...])):
            dh0 = dh0 + _dot(d, w_ref[lo:hi, :])
            dw_ref[lo:hi, :] += _dot_tn(d, h0b)
        dg_ref[...] += _colsum(dh0 * xhat)
        db_ref[...] += _colsum(dh0)
        dx_ref[...] = _ln_bwd(dh0, xhat, rstd, g_ref[...])

    vec = _hbm_shape((1, D_MODEL), F32)
    c = _const2((1, D_MODEL))
    return pl.pallas_call(
        body, name="inproj_bwd", grid=(s_len // tm,),
        in_specs=[_rows(tm, ATTN_W)]
        + [pl.BlockSpec((BLK, 2 * KV_W), lambda i, b=b: (i * per + b + 1, 0)) for b in range(per)]
        + [_rows(tm, 2 * SGU_W), _rows(tm, D_MODEL), _rows(tm, D_MODEL), c, c, _vmem()],
        out_specs=[_rows(tm, D_MODEL), _vmem(), c, c],
        out_shape=[_hbm_shape((s_len, D_MODEL), F32), jax.ShapeDtypeStruct((IN_W, D_MODEL), F32), vec, vec],
        compiler_params=_params(48),
    )(dq, *[dkv_late] * per, dsuv, dr1, x, g0, b0, w_in)


def _place():
    x, y, c = (lax.axis_index(a) for a in MESH_AXES)
    chips = [(1 - x, y), (x, 1 - y), (1 - x, 1 - y)]
    return x, y, c, chips


class _Gather:
    def __init__(self, ins, outs, send_sems, recv_sems, spans=None):
        self.ins, self.outs, self.send_sems, self.recv_sems = ins, outs, send_sems, recv_sems
        self.n = len(ins)
        self.spans = spans or [(0, r.shape[0]) for r in ins]
        self.halves = [(hi - lo) // 2 for lo, hi in self.spans]

    def _copy(self, k, t, slot, half, to):
        rows = pl.ds(pl.multiple_of(self.spans[t][0] + half * self.halves[t], 16), self.halves[t])
        piece = self.outs[t].at[slot, rows, :]
        return pltpu.make_async_remote_copy(src_ref=piece, dst_ref=piece, send_sem=self.send_sems.at[k],
                                            recv_sem=self.recv_sems.at[k], device_id=to, device_id_type=MESH)

    def _chip_copy(self, t, d, slot):
        x, y, c, chips = _place()
        return self._copy(3 * t + d, t, slot, c, (chips[d][0], chips[d][1], c))

    def _pass_copy(self, t, d, half):
        x, y, c, chips = _place()
        return self._copy(3 * self.n + 3 * t + d, t, 2 * chips[d][0] + chips[d][1], half, (x, y, 1 - c))

    def start(self):
        x, y, c, chips = _place()
        me = 2 * x + y
        for t in range(self.n):
            lo, hi = self.spans[t]
            self.outs[t][me, lo:hi, :] = self.ins[t][lo:hi, :].astype(_WIRE)
        for t in range(self.n):
            for d in range(3):
                self._chip_copy(t, d, me).start()

    def pass_on(self):
        x, y, c, chips = _place()
        for t in range(self.n):
            for d in range(3):
                self._chip_copy(t, d, 2 * chips[d][0] + chips[d][1]).wait_recv()
                self._pass_copy(t, d, c).start()

    def finish(self):
        x, y, c, chips = _place()
        me = 2 * x + y
        for t in range(self.n):
            for d in range(3):
                self._pass_copy(t, d, 1 - c).wait_recv()
        for t in range(self.n):
            for d in range(3):
                self._chip_copy(t, d, me).wait_send()
                self._pass_copy(t, d, c).wait_send()

    @staticmethod
    def out_shapes(shards, make=jax.ShapeDtypeStruct):
        return [make((N_CHIP,) + s.shape, _WIRE) for s in shards]

    @staticmethod
    def sems(n):
        return [pltpu.SemaphoreType.DMA((6 * n,)), pltpu.SemaphoreType.DMA((6 * n,))]


class _GatherPlan:
    def __init__(self, pieces):
        self.shards = [p[0] for p in pieces]
        self.spans = [p[1] for p in pieces]
        self.earlier = [p[2] for p in pieces]
        self.n = len(pieces)
        self.carried = [t for t in range(self.n) if self.earlier[t] is not None]

    def operands(self):
        return self.shards + [self.earlier[t] for t in self.carried]

    def in_specs(self):
        return [_vmem()] * self.n + [_hbm()] * len(self.carried)

    def out_specs(self):
        return [_hbm()] * self.n

    def out_shapes(self):
        return _Gather.out_shapes(self.shards, _hbm_shape)

    def scratch(self):
        return ([pltpu.VMEM((N_CHIP,) + s.shape, _WIRE) for s in self.shards] + _Gather.sems(self.n)
                + [pltpu.SemaphoreType.DMA((self.n,)), pltpu.SemaphoreType.DMA((max(len(self.carried), 1),))])

    def bind(self, in_refs, out_refs, scratch_refs):
        plan = self
        shard_refs, earlier_refs = in_refs[:self.n], in_refs[self.n:]
        bufs = scratch_refs[:self.n]
        send_sems, recv_sems, flush_sems, carry_sems = scratch_refs[self.n:self.n + 4]
        gather = _Gather(shard_refs, bufs, send_sems, recv_sems, self.spans)

        def carry_copy(k):
            t = plan.carried[k]
            lo = plan.spans[t][0]
            return pltpu.make_async_copy(earlier_refs[k].at[:, 0:lo, :], bufs[t].at[:, 0:lo, :], carry_sems.at[k])

        class Bound:
            @staticmethod
            def start():
                for k in range(len(plan.carried)):
                    carry_copy(k).start()
                gather.start()

            @staticmethod
            def pass_on():
                gather.pass_on()

            @staticmethod
            def finish():
                gather.finish()
                for k in range(len(plan.carried)):
                    carry_copy(k).wait()
                _flush([bufs[t].at[:, 0:plan.spans[t][1], :] for t in range(plan.n)],
                       [out_refs[t].at[:, 0:plan.spans[t][1], :] for t in range(plan.n)], flush_sems)

        return Bound


def _flush(bufs, hbm_outs, sems):
    copies = [pltpu.make_async_copy(b, o, sems.at[k]) for k, (b, o) in enumerate(zip(bufs, hbm_outs))]
    for cp in copies:
        cp.start()
    for cp in copies:
        cp.wait()


def _gather_weights(shards):
    n = len(shards)

    def body(*refs):
        gather = _Gather(refs[:n], refs[n:2 * n], refs[2 * n], refs[2 * n + 1])
        gather.start()
        gather.pass_on()
        gather.finish()

    return pl.pallas_call(
        body, name="gather_weights",
        in_specs=[_vmem()] * n, out_specs=[_vmem()] * n,
        out_shape=_Gather.out_shapes(shards), scratch_shapes=_Gather.sems(n),
        compiler_params=pltpu.CompilerParams(vmem_limit_bytes=32 * MIB),
    )(*shards)


class _ChipExchange:
    def __init__(self, wire_ref, land_ref, send_sems, recv_sems):
        self.wire, self.land, self.send_sems, self.recv_sems = wire_ref, land_ref, send_sems, recv_sems

    def _copy(self, d):
        x, y, c, chips = _place()
        return pltpu.make_async_remote_copy(
            src_ref=self.wire.at[2 * chips[d][0] + chips[d][1]], dst_ref=self.land.at[d],
            send_sem=self.send_sems.at[d], recv_sem=self.recv_sems.at[d],
            device_id=(chips[d][0], chips[d][1], c), device_id_type=MESH)

    def start(self):
        for d in range(3):
            self._copy(d).start()

    def wait_recv(self):
        for d in range(3):
            self._copy(d).wait_recv()

    def wait_send(self):
        for d in range(3):
            self._copy(d).wait_send()

    def finish_to(self, hbm_out, flush_sem):
        self.wait_recv()
        _flush([self.land], [hbm_out], flush_sem)
        self.wait_send()

    @staticmethod
    def land_shape(wire):
        return _hbm_shape((3,) + wire.shape[1:], wire.dtype)

    @staticmethod
    def sems():
        return [pltpu.SemaphoreType.DMA((3,)), pltpu.SemaphoreType.DMA((3,))]

    @staticmethod
    def scratch(wire):
        return ([pltpu.VMEM((3,) + wire.shape[1:], wire.dtype)] + _ChipExchange.sems() + [pltpu.SemaphoreType.DMA((1,))])


def _pair_out_shapes(half_shape):
    return [jax.ShapeDtypeStruct(half_shape, _WIRE), jax.ShapeDtypeStruct(half_shape[1:], F32)]


def _pair_scratch(acc_shape, half_shape):
    return [pltpu.VMEM(acc_shape, F32), pltpu.VMEM(half_shape, _WIRE),
            pltpu.SemaphoreType.DMA((N_CHIP,)), pltpu.SemaphoreType.DMA((N_CHIP,))]


def _pair_reduce(acc_ref, wire_ref, own_ref, land_ref, send_sems, recv_sems):
    rh = land_ref.shape[1]
    x, y, c, _ = _place()
    me = 2 * x + y
    copies = []
    for j in range(N_CHIP):
        def cast(r, carry, j=j):
            dst = pl.ds(pl.multiple_of(r * ROW_CHUNK, ROW_CHUNK), ROW_CHUNK)
            src = pl.ds(pl.multiple_of((2 * j + 1 - c) * rh + r * ROW_CHUNK, 8), ROW_CHUNK)
            wire_ref[j, dst, :] = acc_ref[src, :].astype(_WIRE)
            return carry

        lax.fori_loop(0, rh // ROW_CHUNK, cast, 0)
        cp = pltpu.make_async_remote_copy(src_ref=wire_ref.at[j], dst_ref=land_ref.at[j], send_sem=send_sems.at[j],
                                          recv_sem=recv_sems.at[j], device_id=(x, y, 1 - c), device_id_type=MESH)
        cp.start()
        copies.append(cp)
    for j in range(N_CHIP):
        copies[j].wait()

        def chunk(r, carry, j=j):
            theirs = pl.ds(pl.multiple_of(r * ROW_CHUNK, ROW_CHUNK), ROW_CHUNK)
            mine = pl.ds(pl.multiple_of((2 * j + c) * rh + r * ROW_CHUNK, 8), ROW_CHUNK)
            wire_ref[j, theirs, :] = (acc_ref[mine, :] + land_ref[j, theirs, :].astype(F32)).astype(_WIRE)
            return carry

        lax.fori_loop(0, rh // ROW_CHUNK, chunk, 0)

    def own_chunk(r, carry):
        theirs = pl.ds(pl.multiple_of(r * ROW_CHUNK, ROW_CHUNK), ROW_CHUNK)
        mine = pl.ds(pl.multiple_of((2 * me + c) * rh + r * ROW_CHUNK, 8), ROW_CHUNK)
        own_ref[theirs, :] = acc_ref[mine, :] + land_ref[me, theirs, :].astype(F32)
        return carry

    lax.fori_loop(0, rh // ROW_CHUNK, own_chunk, 0)


def _grad_finish(last_acc, lands, owns, small):
    n = len(owns) + 1
    halves = [last_acc.shape[0] // (2 * N_CHIP)] + [w.shape[1] for w in lands]
    widths = [last_acc.shape[1]] + [a.shape[1] for a in owns]
    small_body, small_scratch = _small_allreduce_parts()
    ns = len(small)

    def body(*refs):
        acc0, land, own = refs[0], (None,) + refs[1:n], (None,) + refs[n:2 * n - 1]
        refs = refs[2 * n - 1:]
        small_in, g, small_out = refs[:ns], refs[ns:ns + n], refs[ns + n:ns + n + 2]
        refs = refs[ns + n + 2:]
        pland0, wire0, land0, own0 = refs[0:4]
        p_send, p_recv, x_send, x_recv, pair_send, pair_recv = refs[4:10]
        small_refs = refs[10:]
        land = (land0,) + land[1:]
        own = (own0,) + own[1:]
        x, y, c, chips = _place()
        me = 2 * x + y
        exchange = _ChipExchange(wire0, land0, x_send, x_recv)

        def half_rows(t, half):
            return pl.ds(pl.multiple_of(half * halves[t], 8), halves[t])

        def pair_copy(t, half):
            rows = g[t].at[half_rows(t, half), :]
            return pltpu.make_async_remote_copy(src_ref=rows, dst_ref=rows, send_sem=pair_send.at[t],
                                                recv_sem=pair_recv.at[t], device_id=(x, y, 1 - c), device_id_type=MESH)

        small_rounds = small_body(*small_in, *small_out, *small_refs)
        next(small_rounds)
        _pair_reduce(acc0, wire0, own0, pland0, p_send, p_recv)
        next(small_rounds)
        exchange.start()

        for t in list(range(1, n)) + [0]:
            if t == 0:
                exchange.wait_recv()
            if t == min(2, n - 1):
                next(small_rounds)
            if t == min(4, n - 1):
                next(small_rounds, None)

            def chunk(r, carry, t=t):
                src = pl.ds(pl.multiple_of(r * ROW_CHUNK, ROW_CHUNK), ROW_CHUNK)
                dst = pl.ds(pl.multiple_of(c * halves[t] + r * ROW_CHUNK, 8), ROW_CHUNK)
                s = own[t][src, :]
                for d in range(3):
                    s = s + land[t][d, src, :].astype(F32)
                g[t][dst, :] = s
                return carry

            lax.fori_loop(0, halves[t] // ROW_CHUNK, chunk, 0)
            pair_copy(t, c).start()
        for t in range(n):
            pair_copy(t, 1 - c).wait_recv()
        for t in range(n):
            pair_copy(t, c).wait_send()
        exchange.wait_send()

    half0 = (halves[0], widths[0])
    return pl.pallas_call(
        body, name="grad_finish",
        in_specs=[_vmem()] * (2 * n - 1 + ns), out_specs=[_vmem()] * (n + 2),
        out_shape=[jax.ShapeDtypeStruct((2 * h, w), F32) for h, w in zip(halves, widths)]
        + [jax.ShapeDtypeStruct(s, F32) for s in _SMALL_OUT_DIMS],
        scratch_shapes=[pltpu.VMEM((N_CHIP,) + half0, _WIRE), pltpu.VMEM((N_CHIP,) + half0, _WIRE),
                        pltpu.VMEM((3,) + half0, _WIRE), pltpu.VMEM(half0, F32)]
        + [pltpu.SemaphoreType.DMA((N_CHIP,)), pltpu.SemaphoreType.DMA((N_CHIP,))]
        + _ChipExchange.sems()
        + [pltpu.SemaphoreType.DMA((n,)), pltpu.SemaphoreType.DMA((n,))]
        + small_scratch,
        compiler_params=pltpu.CompilerParams(vmem_limit_bytes=56 * MIB),
    )(last_acc, *lands, *owns, *small)


_SMALL = ("ln_in_g", "ln_in_b", "b_in", "attn_sinks", "sgu_ln_g", "sgu_ln_b", "sgu_w", "sgu_b", "b_out",
          "ln_mix_g", "ln_mix_b", "ln_ffn_g", "ln_ffn_b")
_VEC_ROW = dict(ln_in_g=0, ln_in_b=1, b_in=2, attn_sinks=4, sgu_ln_g=5, sgu_ln_b=6, b_out=7, ln_mix_g=8, ln_mix_b=9,
                ln_ffn_g=10, ln_ffn_b=11)
_LOSS_ROW = 12
_VEC_ROWS = 16
_MAT_ROWS = N_GRP * BLK + BLK


_SMALL_IN = ("ln_in_g", "ln_in_b", "bq", "bkv", "bsuv", "sink", "sgu_ln_g", "sgu_ln_b", "sgu_w", "sgu_bt", "b_out",
             "ln_mix_g", "ln_mix_b", "ln_ffn_g", "ln_ffn_b", "loss")
_SMALL_OUT_DIMS = ((_VEC_ROWS, D_MODEL), (_MAT_ROWS, 128))


def _small_allreduce_parts():
    n_in = len(_SMALL_IN)

    def body(*refs):
        (g_ln_in_g, g_ln_in_b, g_bq, g_bkv, g_bsuv, g_sink, g_sln_g, g_sln_b, g_sw, g_sbt, g_bout,
         g_lmg, g_lmb, g_lfg, g_lfb, g_loss) = refs[:n_in]
        out_a, out_b = refs[n_in:n_in + 2]
        (buf_a, buf_b, pair_a, pair_b, stage_a, stage_b, tot_a, tot_b,
         p1_send, p1_recv, x_send, x_recv, p2_send, p2_recv) = refs[n_in + 2:]
        x, y, c, chips = _place()
        me = 2 * x + y
        sibling = (x, y, 1 - c)
        half_a, half_b = _VEC_ROWS // 2, _MAT_ROWS // 2

        buf_a[...] = jnp.zeros_like(buf_a)
        for row, ref in ((0, g_ln_in_g), (1, g_ln_in_b), (7, g_bout), (8, g_lmg), (9, g_lmb), (10, g_lfg), (11, g_lfb),
                         (_LOSS_ROW, g_loss)):
            buf_a[row:row + 1, :] = ref[...]
        buf_a[2:3, 0:ATTN_W] = g_bq[...]
        buf_a[2:3, ATTN_W:ATTN_W + 2 * KV_W] = g_bkv[...]
        buf_a[2:3, ATTN_W + 2 * KV_W:D_MODEL] = g_bsuv[:, 0:2 * KV_W]
        buf_a[3:4, 0:2 * SGU_W - 2 * KV_W] = g_bsuv[:, 2 * KV_W:2 * SGU_W]
        buf_a[4:5, 0:128] = g_sink[...]
        buf_a[5:6, 0:SGU_W] = g_sln_g[...]
        buf_a[6:7, 0:SGU_W] = g_sln_b[...]
        for h in range(N_GRP):
            buf_b[h * BLK:(h + 1) * BLK, :] = g_sw[h]
        buf_b[N_GRP * BLK:_MAT_ROWS, :] = g_sbt[...]

        def remote(src, dst, send_sem, recv_sem, to):
            return pltpu.make_async_remote_copy(src_ref=src, dst_ref=dst, send_sem=send_sem, recv_sem=recv_sem,
                                                device_id=to, device_id_type=MESH)

        first = [remote(buf_a, pair_a, p1_send.at[0], p1_recv.at[0], sibling),
                 remote(buf_b, pair_b, p1_send.at[1], p1_recv.at[1], sibling)]
        for cp in first:
            cp.start()
        yield
        for cp in first:
            cp.wait()
        rows_a = pl.ds(pl.multiple_of(c * half_a, 8), half_a)
        rows_b = pl.ds(pl.multiple_of(c * half_b, 8), half_b)
        stage_a[me] = buf_a[rows_a, :] + pair_a[rows_a, :]
        stage_b[me] = buf_b[rows_b, :] + pair_b[rows_b, :]

        def chip_copies(d):
            to = (chips[d][0], chips[d][1], c)
            return [remote(stage_a.at[me], stage_a.at[me], x_send.at[2 * d], x_recv.at[2 * d], to),
                    remote(stage_b.at[me], stage_b.at[me], x_send.at[2 * d + 1], x_recv.at[2 * d + 1], to)]

        def chip_arrivals(d):
            slot = 2 * chips[d][0] + chips[d][1]
            to = (chips[d][0], chips[d][1], c)
            return [remote(stage_a.at[slot], stage_a.at[slot], x_send.at[2 * d], x_recv.at[2 * d], to),
                    remote(stage_b.at[slot], stage_b.at[slot], x_send.at[2 * d + 1], x_recv.at[2 * d + 1], to)]

        for d in range(3):
            for cp in chip_copies(d):
                cp.start()
        yield
        for d in range(3):
            for cp in chip_arrivals(d):
                cp.wait_recv()
        tot_a[rows_a, :] = ((stage_a[0] + stage_a[1]) + stage_a[2]) + stage_a[3]
        tot_b[rows_b, :] = ((stage_b[0] + stage_b[1]) + stage_b[2]) + stage_b[3]

        second = [remote(tot_a.at[rows_a, :], tot_a.at[rows_a, :], p2_send.at[0], p2_recv.at[0], sibling),
                  remote(tot_b.at[rows_b, :], tot_b.at[rows_b, :], p2_send.at[1], p2_recv.at[1], sibling)]
        for cp in second:
            cp.start()
        yield
        other_a = pl.ds(pl.multiple_of((1 - c) * half_a, 8), half_a)
        other_b = pl.ds(pl.multiple_of((1 - c) * half_b, 8), half_b)
        remote(tot_a.at[other_a, :], tot_a.at[other_a, :], p2_send.at[0], p2_recv.at[0], sibling).wait_recv()
        remote(tot_b.at[other_b, :], tot_b.at[other_b, :], p2_send.at[1], p2_recv.at[1], sibling).wait_recv()
        for cp in second:
            cp.wait_send()
        for d in range(3):
            for cp in chip_copies(d):
                cp.wait_send()
        out_a[...] = tot_a[...]
        out_b[...] = tot_b[...]

    vec = pltpu.VMEM((_VEC_ROWS, D_MODEL), F32)
    mat = pltpu.VMEM((_MAT_ROWS, 128), F32)
    scratch = [vec, mat, vec, mat, pltpu.VMEM((N_CHIP, _VEC_ROWS // 2, D_MODEL), F32),
               pltpu.VMEM((N_CHIP, _MAT_ROWS // 2, 128), F32), vec, mat,
               pltpu.SemaphoreType.DMA((2,)), pltpu.SemaphoreType.DMA((2,)), pltpu.SemaphoreType.DMA((6,)),
               pltpu.SemaphoreType.DMA((6,)), pltpu.SemaphoreType.DMA((2,)), pltpu.SemaphoreType.DMA((2,))]
    return body, scratch


def _small_adamw(tot_a, tot_b, params):
    shapes = [params[nm][0].shape for nm in _SMALL]

    def body(*refs):
        ta, tb = refs[:2]
        prm = refs[2:2 + 3 * len(_SMALL)]
        outs = refs[2 + 3 * len(_SMALL):]

        def grad_of(k, name):
            if name == "sgu_w":
                return [tb[h * BLK:(h + 1) * BLK, :] for h in range(N_GRP)]
            if name == "sgu_b":
                return jnp.transpose(tb[N_GRP * BLK:_MAT_ROWS, :])[0:N_GRP, :]
            row = _VEC_ROW[name]
            if name == "b_in":
                return jnp.concatenate([ta[row:row + 1, :], ta[row + 1:row + 2, 0:IN_W - D_MODEL]], axis=1)
            return ta[row:row + 1, 0:shapes[k][-1]]

        for k, name in enumerate(_SMALL):
            w_ref, m_ref, v_ref = prm[3 * k:3 * k + 3]
            g_out, d_out, m_out, v_out = outs[4 * k:4 * k + 4]
            g = grad_of(k, name)
            if name == "sgu_w":
                for h in range(N_GRP):
                    d_, m_, v_ = _adamw_math(w_ref[h], g[h], m_ref[h], v_ref[h])
                    g_out[h], d_out[h], m_out[h], v_out[h] = g[h], d_, m_, v_
            else:
                d_, m_, v_ = _adamw_math(w_ref[...], g, m_ref[...], v_ref[...])
                g_out[...], d_out[...], m_out[...], v_out[...] = g, d_, m_, v_
        outs[-1][...] = jnp.sum(ta[_LOSS_ROW:_LOSS_ROW + 1, :], axis=1, keepdims=True) * (0.5 / D_MODEL)

    ins = [tot_a, tot_b] + [_in_hbm(a) for nm in _SMALL for a in params[nm]]
    out_dims = [s for s in shapes for _ in range(4)] + [(1, 1)]
    res = pl.pallas_call(
        body, name="small_adamw", grid=(1,),
        in_specs=[_const2(a.shape) for a in ins], out_specs=[_const2(s) for s in out_dims],
        out_shape=[_hbm_shape(s, F32) for s in out_dims],
        compiler_params=_params(32),
    )(*ins)
    return {nm: tuple(res[4 * k:4 * k + 4]) for k, nm in enumerate(_SMALL)}, res[-1]


def _adamw_math(w, g, m, v):
    m = ADAM_B1 * m + (1.0 - ADAM_B1) * g
    v = ADAM_B2 * v + (1.0 - ADAM_B2) * (g * g)
    m_hat = m / (1.0 - ADAM_B1 ** ADAM_STEP)
    v_hat = v / (1.0 - ADAM_B2 ** ADAM_STEP)
    delta = -ADAM_LR * (m_hat / (jnp.sqrt(v_hat) + ADAM_EPS) + ADAM_WD * w)
    return delta, m, v


ADAMW_STEPS = 4


def _adamw(name, groups):
    k = len(groups)

    def body(*refs):
        for i in range(k):
            w_ref, g_ref, m_ref, v_ref = refs[4 * i:4 * i + 4]
            g = g_ref[...]
            for o_ref, o in zip(refs[4 * k + 4 * i:4 * k + 4 * i + 4], (g,) + _adamw_math(w_ref[...], g, m_ref[...], v_ref[...])):
                o_ref[...] = o

    specs = []
    for grp in groups:
        rows, cols = grp[0].shape
        assert rows % (8 * ADAMW_STEPS) == 0, rows
        specs += [pl.BlockSpec((rows // ADAMW_STEPS, cols), lambda i: (i, 0))] * 4
    res = pl.pallas_call(
        body, name=name, grid=(ADAMW_STEPS,), in_specs=specs, out_specs=specs,
        out_shape=[_hbm_shape(grp[0].shape, F32) for grp in groups for _ in range(4)],
        compiler_params=_params(56),
    )(*[_in_hbm(a) for grp in groups for a in grp])
    return [res[4 * i:4 * i + 4] for i in range(k)]


def kernel(x, positions, ln_in_g, ln_in_b, w_in, b_in, attn_sinks, sgu_ln_g, sgu_ln_b, sgu_w, sgu_b, w_out, b_out, ln_mix_g, ln_mix_b, w_gate, w_up, w_down, ln_ffn_g, ln_ffn_b, loss_target, m_ln_in_g, m_ln_in_b, m_w_in, m_b_in, m_attn_sinks, m_sgu_ln_g, m_sgu_ln_b, m_sgu_w, m_sgu_b, m_w_out, m_b_out, m_ln_mix_g, m_ln_mix_b, m_w_gate, m_w_up, m_w_down, m_ln_ffn_g, m_ln_ffn_b, v_ln_in_g, v_ln_in_b, v_w_in, v_b_in, v_attn_sinks, v_sgu_ln_g, v_sgu_ln_b, v_sgu_w, v_sgu_b, v_w_out, v_b_out, v_ln_mix_g, v_ln_mix_b, v_w_gate, v_w_up, v_w_down, v_ln_ffn_g, v_ln_ffn_b):
    weights = dict(ln_in_g=ln_in_g, ln_in_b=ln_in_b, w_in=w_in, b_in=b_in, attn_sinks=attn_sinks, sgu_ln_g=sgu_ln_g,
                   sgu_ln_b=sgu_ln_b, sgu_w=sgu_w, sgu_b=sgu_b, w_out=w_out, b_out=b_out, ln_mix_g=ln_mix_g,
                   ln_mix_b=ln_mix_b, w_gate=w_gate, w_up=w_up, w_down=w_down, ln_ffn_g=ln_ffn_g, ln_ffn_b=ln_ffn_b)
    mom_m = dict(ln_in_g=m_ln_in_g, ln_in_b=m_ln_in_b, w_in=m_w_in, b_in=m_b_in, attn_sinks=m_attn_sinks,
                 sgu_ln_g=m_sgu_ln_g, sgu_ln_b=m_sgu_ln_b, sgu_w=m_sgu_w, sgu_b=m_sgu_b, w_out=m_w_out, b_out=m_b_out,
                 ln_mix_g=m_ln_mix_g, ln_mix_b=m_ln_mix_b, w_gate=m_w_gate, w_up=m_w_up, w_down=m_w_down,
                 ln_ffn_g=m_ln_ffn_g, ln_ffn_b=m_ln_ffn_b)
    mom_v = dict(ln_in_g=v_ln_in_g, ln_in_b=v_ln_in_b, w_in=v_w_in, b_in=v_b_in, attn_sinks=v_attn_sinks,
                 sgu_ln_g=v_sgu_ln_g, sgu_ln_b=v_sgu_ln_b, sgu_w=v_sgu_w, sgu_b=v_sgu_b, w_out=v_w_out, b_out=v_b_out,
                 ln_mix_g=v_ln_mix_g, ln_mix_b=v_ln_mix_b, w_gate=v_w_gate, w_up=v_w_up, w_down=v_w_down,
                 ln_ffn_g=v_ln_ffn_g, ln_ffn_b=v_ln_ffn_b)
    order = list(weights)
    big = ("w_in", "w_out", "w_gate", "w_up", "w_down")

    s_len = x.shape[1]
    xs = _in_hbm(x.reshape(s_len, D_MODEL))
    tgt = _in_hbm(loss_target.reshape(s_len, D_MODEL))
    pos_row = _in_hbm(positions.reshape(1, s_len))
    g0, b0 = _in_hbm(ln_in_g.reshape(1, D_MODEL)), _in_hbm(ln_in_b.reshape(1, D_MODEL))
    sinks = attn_sinks.reshape(N_Q)
    sgu_w3 = _in_hbm(sgu_w.reshape(N_GRP, BLK, BLK))
    sgu_bt = _in_hbm(sgu_b.reshape(N_GRP, BLK).T)
    b_in, b_out, sgu_ln_g, sgu_ln_b, ln_mix_g, ln_mix_b, ln_ffn_g, ln_ffn_b = (
        _in_hbm(a) for a in (b_in, b_out, sgu_ln_g, sgu_ln_b, ln_mix_g, ln_mix_b, ln_ffn_g, ln_ffn_b))

    col_sharded = ("w_in", "w_gate", "w_up")

    def rowmajor(name, a):
        return jnp.swapaxes(a[0], 0, 1) if name in col_sharded else a[0]

    def as_given(name, a):
        return (jnp.swapaxes(a, 0, 1) if name in col_sharded else a)[None]

    shards = [rowmajor(n, weights[n]) for n in big]
    (gw_in,) = _gather_weights(shards[0:1])
    w_in_full = gw_in.reshape(IN_W, D_MODEL)

    sh_out, sh_gate, sh_up, sh_down = shards[1:]
    *acts, gw_out, gw_gate0 = _ln_inproj(xs, pos_row, g0, b0, w_in_full, b_in, _GatherPlan(
        [(sh_out, (0, OUT_SH), None), (sh_gate, (0, GATE_CUT), None)]))
    q, k, v, su, sv, tc, t1, t2 = (_in_hbm(a) for a in acts)
    mc, gw_gate1, gw_up0 = _mixer_fwd(q, k, v, su, sv, sinks, sgu_ln_g, sgu_ln_b, sgu_w3, sgu_bt, _GatherPlan(
        [(sh_gate, (GATE_CUT, FF_CUT), gw_gate0), (sh_up, (0, UP_CUT), None)]))
    mc = _in_hbm(mc)
    w_out_full = gw_out.reshape(D_MODEL, D_MODEL)
    r1, gw_up1 = _outproj(mc, w_out_full, b_out, xs, g0, b0, _GatherPlan([(sh_up, (UP_CUT, FF_CUT), gw_up0)]))
    r1 = _in_hbm(r1)
    *prior, h1, gw_gate, gw_up, gw_down0 = _ffn_up(
        "ffn_up_first", r1, ln_mix_g, ln_mix_b, gw_gate1, gw_up1, _GatherPlan(
            [(sh_gate, (FF_CUT, FF_SH), gw_gate1), (sh_up, (FF_CUT, FF_SH), gw_up1), (sh_down, (0, DOWN_CUT), None)]),
        (0, FF_CUT))
    act, p_act, q_act, gw_down = _ffn_up(
        "ffn_up_rest", r1, ln_mix_g, ln_mix_b, gw_gate, gw_up, _GatherPlan([(sh_down, (DOWN_CUT, FF_SH), gw_down0)]),
        (FF_CUT, FF_SH), tuple(prior))
    act, p_act, q_act = _in_hbm(act), _in_hbm(p_act), _in_hbm(q_act)
    dr2, loss_cols, d_ln_ffn_g, d_ln_ffn_b = _ffn_down_loss(act, gw_down, _in_hbm(h1), ln_ffn_g, ln_ffn_b, tgt)
    dr2 = _in_hbm(dr2)

    dg, du, wire_down, own_down = _ffn_bwd_a(dr2, act, p_act, q_act, gw_down)
    dh1a, wire_gate, own_gate, land_down = _ffn_bwd_g(dr2, _in_hbm(dg), r1, ln_mix_g, ln_mix_b, gw_gate, wire_down)
    dr1, wire_up, own_up, d_ln_mix_g, d_ln_mix_b, land_gate = _ffn_bwd_u(_in_hbm(dh1a), _in_hbm(du), r1, ln_mix_g,
                                                                         ln_mix_b, gw_up, wire_gate)
    dr1 = _in_hbm(dr1)
    dmc, wire_out, own_out, d_b_out = _outproj_bwd(dr1, mc, w_out_full)
    (dq, dkv, dsuv, dbq, dbkv, dbsuv, d_sink, d_sgu_ln_g, d_sgu_ln_b, d_sgu_w, d_sgu_bt, land_up, land_out) = _mixer_bwd(
        q, k, v, su, sv, _in_hbm(dmc), tc, t1, t2, sinks, sgu_ln_g, sgu_ln_b, sgu_w3, sgu_bt, [wire_up, wire_out])
    grad_x, acc_in, d_ln_in_g, d_ln_in_b = _inproj_bwd(_in_hbm(dq), _in_hbm(dkv), _in_hbm(dsuv), dr1, xs, g0, b0,
                                                       w_in_full)

    small_local = dict(
        ln_in_g=d_ln_in_g, ln_in_b=d_ln_in_b, bq=dbq, bkv=dbkv, bsuv=dbsuv, sink=d_sink, sgu_ln_g=d_sgu_ln_g,
        sgu_ln_b=d_sgu_ln_b, sgu_w=d_sgu_w, sgu_bt=d_sgu_bt, b_out=d_b_out, ln_mix_g=d_ln_mix_g, ln_mix_b=d_ln_mix_b,
        ln_ffn_g=d_ln_ffn_g, ln_ffn_b=d_ln_ffn_b, loss=loss_cols)
    *reduced, tot_a, tot_b = _grad_finish(acc_in, [land_out, land_gate, land_up, land_down],
                                          [own_out, own_gate, own_up, own_down], [small_local[nm] for nm in _SMALL_IN])
    small_shape = dict(ln_in_g=(1, D_MODEL), ln_in_b=(1, D_MODEL), sgu_w=(N_GRP, BLK, BLK), sgu_b=(N_GRP, BLK))
    small_params = {nm: tuple(src[nm].reshape(small_shape.get(nm, src[nm].shape)) for src in (weights, mom_m, mom_v))
                    for nm in _SMALL}
    small_out, loss = _small_adamw(_in_hbm(tot_a), _in_hbm(tot_b), small_params)
    loss = loss.reshape(())
    grads, delta, new_m, new_v = {}, {}, {}, {}
    for nm in _SMALL:
        grads[nm], delta[nm], new_m[nm], new_v[nm] = (a.reshape(weights[nm].shape) for a in small_out[nm])

    groups = [(shards[t], reduced[t], rowmajor(nm, mom_m[nm]), rowmajor(nm, mom_v[nm])) for t, nm in enumerate(big)]
    for nm, res in zip(big, _adamw("adamw", groups)):
        grads[nm], delta[nm], new_m[nm], new_v[nm] = (as_given(nm, a) for a in res)

    return (loss, grad_x.reshape(x.shape), *[grads[n] for n in order], *[delta[n] for n in order],
            *[new_m[n] for n in order], *[new_v[n] for n in order])
```

```python
import jax
import jax.numpy as jnp
from jax import lax
from jax.experimental import pallas as pl
from jax.experimental.pallas import tpu as pltpu

F32 = jnp.float32
_MXU = jnp.bfloat16
_WIRE = jnp.bfloat16
_ACT = jnp.bfloat16

D_MODEL = 1024
ATTN_W = 512
SGU_W = 512
HEAD_DIM = 64
N_Q = 8
N_KV = 2
Q_PER_KV = 4
KV_W = 128
BLK = 128
ROT_DIM = 16
ROPE_THETA = 500000.0
N_GRP = 4
GRP_DIM = 128
D_FF = 2816
IN_W = 1792
LN_EPS = 1e-5
ALPHA = 2.0 ** 0.25
N_CHIP = 4
FF_SH = D_FF // N_CHIP
IN_SH = IN_W // N_CHIP
OUT_SH = D_MODEL // N_CHIP
ROW_CHUNK = 32
LANES = 128
FF_PAD = -(-FF_SH // LANES) * LANES
FF_CUT = 512
GATE_CUT, UP_CUT, DOWN_CUT = 192, 224, 448

ADAM_LR = 0.001
ADAM_B1 = 0.9
ADAM_B2 = 0.999
ADAM_EPS = 1e-08
ADAM_WD = 0.01
ADAM_STEP = 10

SQRT_HALF = 0.7071067811865476
INV_SQRT_2PI = 0.3989422804014327
MESH_AXES = ("x", "y", "c")
MESH = pl.DeviceIdType.MESH
MIB = 2 ** 20


def _vmem():
    return pl.BlockSpec(memory_space=pltpu.VMEM)


def _smem():
    return pl.BlockSpec(memory_space=pltpu.SMEM)


def _hbm():
    return pl.BlockSpec(memory_space=pl.ANY)


def _hbm_shape(shape, dtype):
    return pltpu.HBM(shape, dtype)


def _in_hbm(a):
    return pltpu.with_memory_space_constraint(a, pltpu.HBM)


def _params(vmem_mib=48):
    return pltpu.CompilerParams(dimension_semantics=("arbitrary",), vmem_limit_bytes=vmem_mib * MIB)


def _tile(n, cap):
    if n <= cap:
        return n
    for t in range(cap - cap % 16, 0, -16):
        if n % t == 0:
            return t
    raise ValueError((n, cap))


def _rows(tm, width):
    return pl.BlockSpec((tm, width), lambda i: (i, 0))


def _const2(shape):
    return pl.BlockSpec(shape, lambda i: (0,) * len(shape))


def _ln(x, g, b):
    mu = jnp.mean(x, axis=-1, keepdims=True)
    xc = x - mu
    var = jnp.mean(xc * xc, axis=-1, keepdims=True)
    rstd = lax.rsqrt(var + LN_EPS)
    xhat = xc * rstd
    return xhat * g + b, xhat, rstd


def _ln_bwd(dy, xhat, rstd, g):
    gdy = dy * g
    m1 = jnp.mean(gdy, axis=-1, keepdims=True)
    m2 = jnp.mean(gdy * xhat, axis=-1, keepdims=True)
    return rstd * (gdy - m1 - xhat * m2)


def _colsum(a):
    return jnp.sum(a, axis=0, keepdims=True)


def _gelu_and_grad(x):
    cdf = 0.5 * (1.0 + lax.erf(x * SQRT_HALF))
    return x * cdf, cdf + x * jnp.exp(-0.5 * x * x) * INV_SQRT_2PI


def _dot(a, b):
    return jnp.dot(a, b, preferred_element_type=F32)


def _dot_nt(a, b):
    return lax.dot_general(a, b, (((1,), (1,)), ((), ())), preferred_element_type=F32)


def _dot_tn(a, b):
    return lax.dot_general(a, b, (((0,), (0,)), ((), ())), preferred_element_type=F32)


def _rope(t, tc, t1, t2):
    n = t.shape[1]
    rep = n // 128
    if rep > 1:
        tc, t1, t2 = (jnp.tile(a, (1, rep)) for a in (tc, t1, t2))
    return t * tc + pltpu.roll(t, n - 8, 1) * t1 + pltpu.roll(t, 8, 1) * t2


def _rope_bwd(d, tc, t1, t2):
    n = d.shape[1]
    rep = n // 128
    if rep > 1:
        tc, t1, t2 = (jnp.tile(a, (1, rep)) for a in (tc, t1, t2))
    return d * tc + pltpu.roll(d * t1, 8, 1) + pltpu.roll(d * t2, n - 8, 1)


def _causal_w(w_ref, h):
    t = lax.broadcasted_iota(jnp.int32, (BLK, BLK), 0)
    s = lax.broadcasted_iota(jnp.int32, (BLK, BLK), 1)
    return jnp.where(s <= t, w_ref[h], 0.0)


def _lane_put(vals, width):
    rows = vals[0].shape[0]
    lane = lax.broadcasted_iota(jnp.int32, (rows, width), 1)
    out = jnp.zeros((rows, width), F32)
    for k, v in enumerate(vals):
        out = out + jnp.where(lane == k, v, 0.0)
    return out


def _rope_consts():
    lane = jnp.arange(128) % HEAD_DIM
    rot = lane < ROT_DIM
    pair = (2 * (lane % (ROT_DIM // 2))).astype(F32)
    freq = jnp.where(rot, ROPE_THETA ** (-pair / ROT_DIM), 0.0)
    rows = [freq, rot.astype(F32), 1.0 - rot.astype(F32), (lane < ROT_DIM // 2).astype(F32),
            jnp.logical_and(lane >= ROT_DIM // 2, rot).astype(F32)]
    rows += [jnp.zeros((128,), F32)] * 3
    return jnp.stack(rows).astype(F32)


def _ln_inproj(x, pos_row, g0, b0, w_in, b_in, plan):
    s_len = x.shape[0]
    tm = _tile(s_len, 512)
    m, n = len(plan.operands()), plan.n

    def body(x_ref, pos_ref, g_ref, b_ref, w_ref, bi_ref, rc_ref, *rest):
        q_ref, k_ref, v_ref, su_ref, sv_ref, tc_ref, t1_ref, t2_ref = rest[m:m + 8]
        gather = plan.bind(rest[:m], rest[m + 8:m + 8 + n], rest[m + 8 + n:])
        i = pl.program_id(0)

        @pl.when(i == 0)
        def _():
            gather.start()

        h0, _, _ = _ln(x_ref[...], g_ref[...], b_ref[...])
        proj = _dot_nt(h0.astype(_MXU), w_ref[...]) + bi_ref[...]
        pos = jnp.broadcast_to(pos_ref[...].astype(F32), (128, tm))
        ang = jnp.transpose(pos) * rc_ref[0:1, :]
        cs = jnp.cos(ang)
        sn = jnp.sin(ang)
        tc = cs * rc_ref[1:2, :] + rc_ref[2:3, :]
        t1 = -sn * rc_ref[3:4, :]
        t2 = sn * rc_ref[4:5, :]
        tc_ref[...] = tc
        t1_ref[...] = t1
        t2_ref[...] = t2
        q = _rope(proj[:, 0:ATTN_W], tc, t1, t2) * (HEAD_DIM ** -0.5)
        q_ref[...] = q.astype(_MXU)
        k_ref[...] = _rope(proj[:, ATTN_W:ATTN_W + KV_W], tc, t1, t2).astype(_MXU)
        v_ref[...] = proj[:, ATTN_W + KV_W:ATTN_W + 2 * KV_W].astype(_MXU)
        su_ref[...] = proj[:, ATTN_W + 2 * KV_W:ATTN_W + 2 * KV_W + SGU_W]
        sv_ref[...] = proj[:, ATTN_W + 2 * KV_W + SGU_W:IN_W]

        last = pl.num_programs(0) - 1

        @pl.when(i == jnp.maximum(last - 1, 0))
        def _():
            gather.pass_on()

        @pl.when(i == last)
        def _():
            gather.finish()

    sd = _hbm_shape
    return pl.pallas_call(
        body, name="ln_inproj", grid=(s_len // tm,),
        in_specs=[_rows(tm, D_MODEL), pl.BlockSpec((1, tm), lambda i: (0, i)), _const2((1, D_MODEL)),
                  _const2((1, D_MODEL)), _vmem(),
                  _const2((1, IN_W)), _const2((8, 128))] + plan.in_specs(),
        out_specs=[_rows(tm, ATTN_W), _rows(tm, KV_W), _rows(tm, KV_W), _rows(tm, SGU_W), _rows(tm, SGU_W),
                   _rows(tm, 128), _rows(tm, 128), _rows(tm, 128)] + plan.out_specs(),
        out_shape=[sd((s_len, ATTN_W), _MXU), sd((s_len, KV_W), _MXU), sd((s_len, KV_W), _MXU),
                   sd((s_len, SGU_W), F32), sd((s_len, SGU_W), F32),
                   sd((s_len, 128), F32), sd((s_len, 128), F32), sd((s_len, 128), F32)] + plan.out_shapes(),
        scratch_shapes=plan.scratch(),
        compiler_params=_params(56),
    )(x, pos_row, g0, b0, w_in, b_in, _rope_consts(), *plan.operands())


def _band_mask_t(first_block):
    kj = lax.broadcasted_iota(jnp.int32, (2 * BLK, BLK), 0)
    qi = lax.broadcasted_iota(jnp.int32, (2 * BLK, BLK), 1)
    shut = jnp.where(first_block, 2 * BLK, 0)
    prev_ok = jnp.logical_and(kj < BLK, kj > qi + shut)
    cur_ok = jnp.logical_and(kj >= BLK, (kj - BLK) <= qi)
    return jnp.logical_or(prev_ok, cur_ok)


def _attn_probs_t(kh, qh, sink, allowed_t):
    s = jnp.where(allowed_t, _dot_nt(kh, qh), -1e30)
    m = jnp.maximum(jnp.max(s, axis=0, keepdims=True), sink)
    p = jnp.exp(s - m)
    ps = jnp.exp(sink - m)
    inv = 1.0 / (jnp.sum(p, axis=0, keepdims=True) + ps)
    return p * inv, ps * inv


def _sgu_mix(gv, lg, lb, w_ref, bt_ref):
    vv, vhat, rstd = _ln(gv, lg, lb)
    vvb = vv.astype(_MXU)
    wcs, mixed = [], []
    for h in range(N_GRP):
        wc = _causal_w(w_ref, h).astype(_MXU)
        wcs.append(wc)
        mixed.append(_dot(wc, vvb[:, h * GRP_DIM:(h + 1) * GRP_DIM]) + bt_ref[:, h:h + 1])
    return jnp.concatenate(mixed, axis=1), vhat, rstd, vvb, wcs


def _mixer_fwd(q, k, v, su, sv, sinks, sg, sb, sgu_w, sgu_bt, plan):
    s_len = q.shape[0]
    nb = s_len // BLK
    per = 2 if nb % 2 == 0 else 1
    steps = nb // per
    m, n = len(plan.operands()), plan.n

    def body(q_ref, kc_ref, kp_ref, vc_ref, vp_ref, su_ref, sv_ref, sink_ref, lg_ref, lb_ref, w_ref, bt_ref, *rest):
        mc_ref = rest[m]
        gather = plan.bind(rest[:m], rest[m + 1:m + 1 + n], rest[m + 1 + n:])
        i = pl.program_id(0)

        @pl.when(i == 0)
        def _():
            gather.start()

        @pl.when(i == max(steps - 2, 0))
        def _():
            gather.pass_on()

        @pl.when(i == steps - 1)
        def _():
            gather.finish()

        for s in range(per):
            rows = slice(s * BLK, (s + 1) * BLK)
            before = slice((s - 1) * BLK, s * BLK)
            k_prev = kp_ref[...] if s == 0 else kc_ref[before, :]
            v_prev = vp_ref[...] if s == 0 else vc_ref[before, :]
            allowed_t = _band_mask_t(i == 0 if s == 0 else False)
            kb = jnp.concatenate([k_prev, kc_ref[rows, :]], axis=0)
            vb = jnp.concatenate([v_prev, vc_ref[rows, :]], axis=0)
            qv = q_ref[rows, :]
            outs = []
            allowed_g = jnp.tile(allowed_t, (1, Q_PER_KV))
            for g in range(N_KV):
                heads = range(g * Q_PER_KV, (g + 1) * Q_PER_KV)
                kh = kb[:, g * HEAD_DIM:(g + 1) * HEAD_DIM]
                vh = vb[:, g * HEAD_DIM:(g + 1) * HEAD_DIM]
                q_g = jnp.concatenate([qv[:, h * HEAD_DIM:(h + 1) * HEAD_DIM] for h in heads], axis=0)
                sink_g = jnp.concatenate([jnp.full((1, BLK), sink_ref[h], F32) for h in heads], axis=1)
                probs_t, _ = _attn_probs_t(kh, q_g, sink_g, allowed_g)
                o_g = _dot_tn(probs_t.astype(_MXU), vh)
                outs += [o_g[hh * BLK:(hh + 1) * BLK, :] for hh in range(Q_PER_KV)]
            u = _gelu_and_grad(su_ref[rows, :])[0]
            gv = _gelu_and_grad(sv_ref[rows, :])[0]
            mixed = _sgu_mix(gv, lg_ref[...], lb_ref[...], w_ref, bt_ref)[0]
            mc_ref[rows, :] = jnp.concatenate(outs + [u * mixed], axis=1).astype(_MXU)

    cur = lambda w: pl.BlockSpec((per * BLK, w), lambda i: (i, 0))
    prev = lambda w: pl.BlockSpec((BLK, w), lambda i: (jnp.maximum(per * i - 1, 0), 0))
    return pl.pallas_call(
        body, name="mixer_fwd", grid=(steps,),
        in_specs=[cur(ATTN_W), cur(KV_W), prev(KV_W), cur(KV_W), prev(KV_W), cur(SGU_W), cur(SGU_W), _smem(),
                  _const2((1, SGU_W)), _const2((1, SGU_W)), _const2((N_GRP, BLK, BLK)), _const2((BLK, N_GRP))]
        + plan.in_specs(),
        out_specs=[cur(D_MODEL)] + plan.out_specs(),
        out_shape=[_hbm_shape((s_len, D_MODEL), _MXU)] + plan.out_shapes(),
        scratch_shapes=plan.scratch(),
        compiler_params=_params(56),
    )(q, k, k, v, v, su, sv, sinks, sg, sb, sgu_w, sgu_bt, *plan.operands())


def _outproj(mc, w_out, b_out, x, g0, b0, plan):
    s_len = x.shape[0]
    tm = _tile(s_len, 512)
    m, n = len(plan.operands()), plan.n

    def body(mc_ref, w_ref, bo_ref, x_ref, g_ref, b_ref, *rest):
        r1_ref = rest[m]
        gather = plan.bind(rest[:m], rest[m + 1:m + 1 + n], rest[m + 1 + n:])
        i = pl.program_id(0)

        @pl.when(i == 0)
        def _():
            gather.start()

        h0, _, _ = _ln(x_ref[...], g_ref[...], b_ref[...])
        r1_ref[...] = ALPHA * h0 + (_dot(mc_ref[...], w_ref[...]) + bo_ref[...])

        last = pl.num_programs(0) - 1

        @pl.when(i == jnp.maximum(last - 1, 0))
        def _():
            gather.pass_on()

        @pl.when(i == last)
        def _():
            gather.finish()

    return pl.pallas_call(
        body, name="outproj", grid=(s_len // tm,),
        in_specs=[_rows(tm, D_MODEL), _vmem(), _const2((1, D_MODEL)), _rows(tm, D_MODEL),
                  _const2((1, D_MODEL)), _const2((1, D_MODEL))] + plan.in_specs(),
        out_specs=[_rows(tm, D_MODEL)] + plan.out_specs(),
        out_shape=[_hbm_shape((s_len, D_MODEL), F32)] + plan.out_shapes(),
        scratch_shapes=plan.scratch(),
        compiler_params=_params(40),
    )(mc, w_out, b_out, x, g0, b0, *plan.operands())


def _ffn_spec(tm):
    return pl.BlockSpec((N_CHIP, tm, FF_SH), lambda i: (0, i, 0))


def _act_spec(tm):
    return pl.BlockSpec((N_CHIP, tm, FF_PAD), lambda i: (0, i, 0))


def _ffn_up(name, r1, g1, b1, wg, wu, plan, cols, prior=()):
    s_len = r1.shape[0]
    lo, hi = cols
    width = hi - lo
    block_w = -(-width // LANES) * LANES
    assert lo % block_w == 0 and lo + block_w <= FF_PAD
    first = not prior
    tm = _tile(s_len, 512)
    m, n = len(plan.operands()), plan.n
    n_out = 4 if first else 3

    def body(r1_ref, g_ref, b_ref, wg_hbm, wu_hbm, *rest):
        rest = rest[len(prior):]
        a_ref, p_ref, q_ref = rest[m:m + 3]
        wg_buf, wu_buf, w_sems = rest[-3:]
        gather = plan.bind(rest[:m], rest[m + n_out:m + n_out + n], rest[m + n_out + n:-3])
        i = pl.program_id(0)

        def fetch(j):
            return [pltpu.make_async_copy(src.at[j, lo:hi, :], dst.at[j], w_sems.at[k, j])
                    for k, (src, dst) in enumerate(((wg_hbm, wg_buf), (wu_hbm, wu_buf)))]

        @pl.when(i == 0)
        def _():
            gather.start()
            for j in range(N_CHIP):
                for cp in fetch(j):
                    cp.start()

        h1, _, _ = _ln(r1_ref[...], g_ref[...], b_ref[...])
        if first:
            rest[m + 3][...] = h1
        h1b = h1.astype(_MXU)

        @pl.when(i == 0)
        def _():
            for j in range(N_CHIP):
                for cp in fetch(j):
                    cp.wait()

        for j in range(N_CHIP):
            g = _dot_nt(h1b, wg_buf[j])
            u = _dot_nt(h1b, wu_buf[j])
            silu, sg = _silu_parts(g)
            a_ref[j, :, 0:width] = (silu * u).astype(_MXU)
            p_ref[j, :, 0:width] = silu.astype(_ACT)
            q_ref[j, :, 0:width] = (u * (sg * (1.0 + g * (1.0 - sg)))).astype(_ACT)

        last = pl.num_programs(0) - 1

        @pl.when(i == jnp.maximum(last - 1, 0))
        def _():
            gather.pass_on()

        @pl.when(i == last)
        def _():
            gather.finish()

    sd = _hbm_shape((N_CHIP, s_len, FF_PAD), _ACT)
    c = _const2((1, D_MODEL))
    out_block = pl.BlockSpec((N_CHIP, tm, block_w), lambda i: (0, i, lo // block_w))
    return pl.pallas_call(
        body, name=name, grid=(s_len // tm,),
        in_specs=[_rows(tm, D_MODEL), c, c, _hbm(), _hbm()] + [_hbm()] * len(prior) + plan.in_specs(),
        out_specs=[out_block] * 3 + [_rows(tm, D_MODEL)] * first + plan.out_specs(),
        out_shape=[_hbm_shape((N_CHIP, s_len, FF_PAD), _MXU), sd, sd] + [_hbm_shape((s_len, D_MODEL), F32)] * first
        + plan.out_shapes(),
        input_output_aliases={5 + k: k for k in range(len(prior))},
        scratch_shapes=plan.scratch() + [pltpu.VMEM((N_CHIP, width, D_MODEL), _MXU)] * 2
        + [pltpu.SemaphoreType.DMA((2, N_CHIP))],
        compiler_params=_params(60),
    )(r1, g1, b1, wg, wu, *prior, *plan.operands())


def _silu_parts(g):
    sg = 1.0 / (1.0 + jnp.exp(-g))
    return g * sg, sg


def _ffn_down_loss(act, wd, h1, g2, b2, target):
    s_len = h1.shape[0]
    tm = _tile(s_len, 512)

    parts = 2 if tm % 32 == 0 else 1
    sub = tm // parts

    def body(a_ref, wd_ref, h1_ref, g2_ref, b2_ref, t_ref, dr2_ref, loss_ref, dg2_ref, db2_ref):
        i = pl.program_id(0)

        @pl.when(i == 0)
        def _():
            loss_ref[...] = jnp.zeros_like(loss_ref)
            dg2_ref[...] = jnp.zeros_like(dg2_ref)
            db2_ref[...] = jnp.zeros_like(db2_ref)

        for part in range(parts):
            rows = slice(part * sub, (part + 1) * sub)
            f = jnp.zeros((sub, D_MODEL), F32)
            for j in range(N_CHIP):
                f = f + _dot(a_ref[j, rows, 0:FF_SH], wd_ref[j])
            h2, r2hat, rstd2 = _ln(ALPHA * h1_ref[rows, :] + f, g2_ref[...], b2_ref[...])
            diff = h2 - t_ref[rows, :]
            dh2 = diff * (1.0 / D_MODEL)
            loss_ref[...] += _colsum(diff * diff)
            dg2_ref[...] += _colsum(dh2 * r2hat)
            db2_ref[...] += _colsum(dh2)
            dr2_ref[rows, :] = _ln_bwd(dh2, r2hat, rstd2, g2_ref[...])

    vec = _hbm_shape((1, D_MODEL), F32)
    c = _const2((1, D_MODEL))
    return pl.pallas_call(
        body, name="ffn_down_loss", grid=(s_len // tm,),
        in_specs=[_act_spec(tm), _vmem(), _rows(tm, D_MODEL), c, c, _rows(tm, D_MODEL)],
        out_specs=[_rows(tm, D_MODEL), c, c, c],
        out_shape=[_hbm_shape((s_len, D_MODEL), F32), vec, vec, vec],
        compiler_params=_params(48),
    )(act, wd, h1, g2, b2, target)


def _ffn_bwd_a(dr2, act, p_act, q_act, wd):
    s_len = dr2.shape[0]
    tm = _tile(s_len, 512)

    def body(dr2_ref, a_ref, p_ref, q_ref, wd_ref, dg_ref, du_ref, wire_ref, own_ref,
             dwd_ref, land_ref, send_sem, recv_sem):
        i = pl.program_id(0)

        @pl.when(i == 0)
        def _():
            dwd_ref[...] = jnp.zeros_like(dwd_ref)

        dfb = dr2_ref[...].astype(_MXU)
        for j in range(N_CHIP):
            da = _dot_nt(dfb, wd_ref[j])
            dg_ref[j] = (da * q_ref[j, :, 0:FF_SH].astype(F32)).astype(_MXU)
            du_ref[j] = (da * p_ref[j, :, 0:FF_SH].astype(F32)).astype(_MXU)
            dwd_ref[j * FF_SH:(j + 1) * FF_SH, :] += _dot_tn(a_ref[j, :, 0:FF_SH], dfb)

        @pl.when(i == pl.num_programs(0) - 1)
        def _():
            _pair_reduce(dwd_ref, wire_ref, own_ref, land_ref, send_sem, recv_sem)

    sd = _hbm_shape((N_CHIP, s_len, FF_SH), _MXU)
    half = (N_CHIP, FF_SH // 2, D_MODEL)
    return pl.pallas_call(
        body, name="ffn_bwd_a", grid=(s_len // tm,),
        in_specs=[_rows(tm, D_MODEL), _act_spec(tm), _act_spec(tm), _act_spec(tm), _vmem()],
        out_specs=[_ffn_spec(tm), _ffn_spec(tm), _vmem(), _vmem()],
        out_shape=[sd, sd] + _pair_out_shapes(half),
        scratch_shapes=_pair_scratch((D_FF, D_MODEL), half),
        compiler_params=_params(61),
    )(dr2, act, p_act, q_act, wd)


def _ffn_bwd_g(dr2, dg, r1, g1, b1, wg, prev_wire):
    s_len = dr2.shape[0]
    tm = _tile(s_len, 512)

    def body(dr2_ref, dg_ref, r1_ref, g1_ref, b1_ref, wg_ref, pw_ref, dh1_ref, wire_ref, own_ref, pl_ref,
             dwg_ref, land_ref, send_sem, recv_sem, xl_ref, x_send, x_recv, x_flush):
        i = pl.program_id(0)
        exchange = _ChipExchange(pw_ref, xl_ref, x_send, x_recv)

        @pl.when(i == 0)
        def _():
            exchange.start()
            dwg_ref[...] = jnp.zeros_like(dwg_ref)

        h1, _, _ = _ln(r1_ref[...], g1_ref[...], b1_ref[...])
        h1b = h1.astype(_MXU)
        dh1 = ALPHA * dr2_ref[...]
        for j in range(N_CHIP):
            dgj = dg_ref[j]
            dh1 = dh1 + _dot(dgj, wg_ref[j])
            dwg_ref[j * FF_SH:(j + 1) * FF_SH, :] += _dot_tn(dgj, h1b)
        dh1_ref[...] = dh1

        @pl.when(i == pl.num_programs(0) - 1)
        def _():
            _pair_reduce(dwg_ref, wire_ref, own_ref, land_ref, send_sem, recv_sem)
            exchange.finish_to(pl_ref, x_flush)

    c = _const2((1, D_MODEL))
    half = (N_CHIP, FF_SH // 2, D_MODEL)
    return pl.pallas_call(
        body, name="ffn_bwd_g", grid=(s_len // tm,),
        in_specs=[_rows(tm, D_MODEL), _ffn_spec(tm), _rows(tm, D_MODEL), c, c, _vmem(), _vmem()],
        out_specs=[_rows(tm, D_MODEL), _vmem(), _vmem(), _hbm()],
        out_shape=[_hbm_shape((s_len, D_MODEL), F32)] + _pair_out_shapes(half) + [_ChipExchange.land_shape(prev_wire)],
        scratch_shapes=_pair_scratch((D_FF, D_MODEL), half) + _ChipExchange.scratch(prev_wire),
        compiler_params=_params(58),
    )(dr2, dg, r1, g1, b1, wg, prev_wire)


def _ffn_bwd_u(dh1a, du, r1, g1, b1, wu, prev_wire):
    s_len = dh1a.shape[0]
    tm = _tile(s_len, 512)

    def body(dh1_ref, du_ref, r1_ref, g1_ref, b1_ref, wu_ref, pw_ref,
             dr1_ref, wire_ref, own_ref, dg1_ref, db1_ref, pl_ref,
             dwu_ref, land_ref, send_sem, recv_sem, xl_ref, x_send, x_recv, x_flush):
        i = pl.program_id(0)
        exchange = _ChipExchange(pw_ref, xl_ref, x_send, x_recv)

        @pl.when(i == 0)
        def _():
            exchange.start()
            dwu_ref[...] = jnp.zeros_like(dwu_ref)
            dg1_ref[...] = jnp.zeros_like(dg1_ref)
            db1_ref[...] = jnp.zeros_like(db1_ref)

        h1, r1hat, rstd1 = _ln(r1_ref[...], g1_ref[...], b1_ref[...])
        h1b = h1.astype(_MXU)
        dh1 = dh1_ref[...]
        for j in range(N_CHIP):
            duj = du_ref[j]
            dh1 = dh1 + _dot(duj, wu_ref[j])
            dwu_ref[j * FF_SH:(j + 1) * FF_SH, :] += _dot_tn(duj, h1b)
        dg1_ref[...] += _colsum(dh1 * r1hat)
        db1_ref[...] += _colsum(dh1)
        dr1_ref[...] = _ln_bwd(dh1, r1hat, rstd1, g1_ref[...])

        @pl.when(i == pl.num_programs(0) - 1)
        def _():
            _pair_reduce(dwu_ref, wire_ref, own_ref, land_ref, send_sem, recv_sem)
            exchange.finish_to(pl_ref, x_flush)

    vec = _hbm_shape((1, D_MODEL), F32)
    c = _const2((1, D_MODEL))
    half = (N_CHIP, FF_SH // 2, D_MODEL)
    return pl.pallas_call(
        body, name="ffn_bwd_u", grid=(s_len // tm,),
        in_specs=[_rows(tm, D_MODEL), _ffn_spec(tm), _rows(tm, D_MODEL), c, c, _vmem(), _vmem()],
        out_specs=[_rows(tm, D_MODEL), _vmem(), _vmem(), c, c, _hbm()],
        out_shape=[_hbm_shape((s_len, D_MODEL), F32)] + _pair_out_shapes(half)
        + [vec, vec, _ChipExchange.land_shape(prev_wire)],
        scratch_shapes=_pair_scratch((D_FF, D_MODEL), half) + _ChipExchange.scratch(prev_wire),
        compiler_params=_params(58),
    )(dh1a, du, r1, g1, b1, wu, prev_wire)


def _outproj_bwd(dr1, mc, w_out):
    s_len = dr1.shape[0]
    tm = _tile(s_len, 512)

    def body(dr1_ref, mc_ref, w_ref, dmc_ref, wire_ref, own_ref, db_ref, dw_ref, land_ref, send_sem, recv_sem):
        i = pl.program_id(0)

        @pl.when(i == 0)
        def _():
            dw_ref[...] = jnp.zeros_like(dw_ref)
            db_ref[...] = jnp.zeros_like(db_ref)

        d = dr1_ref[...]
        db_ref[...] += _colsum(d)
        db16 = d.astype(_MXU)
        dmc_ref[...] = _dot_nt(db16, w_ref[...])
        dw_ref[...] += _dot_tn(mc_ref[...], db16)

        @pl.when(i == pl.num_programs(0) - 1)
        def _():
            _pair_reduce(dw_ref, wire_ref, own_ref, land_ref, send_sem, recv_sem)

    half = (N_CHIP, OUT_SH // 2, D_MODEL)
    return pl.pallas_call(
        body, name="outproj_bwd", grid=(s_len // tm,),
        in_specs=[_rows(tm, D_MODEL), _rows(tm, D_MODEL), _vmem()],
        out_specs=[_rows(tm, D_MODEL), _vmem(), _vmem(), _const2((1, D_MODEL))],
        out_shape=[_hbm_shape((s_len, D_MODEL), F32)] + _pair_out_shapes(half) + [_hbm_shape((1, D_MODEL), F32)],
        scratch_shapes=_pair_scratch((D_MODEL, D_MODEL), half),
        compiler_params=_params(48),
    )(dr1, mc, w_out)


def _mixer_bwd(q, k, v, su, sv, dmc, tc, t1, t2, sinks, sg, sb, sgu_w, sgu_bt, prev_wires):
    s_len = q.shape[0]
    nb = s_len // BLK
    per = next(p for p in (4, 2, 1) if nb % p == 0)
    steps = nb // per

    def body(q_ref, kc_ref, kp_ref, vc_ref, vp_ref, su_ref, sv_ref, dmc_ref,
             tc_ref, t1_ref, t2_ref, tcp_ref, t1p_ref, t2p_ref,
             sink_ref, lg_ref, lb_ref, w_ref, bt_ref, pw0_ref, pw1_ref,
             dq_ref, dkv_ref, dsuv_ref, dbq_ref, dbkv_ref, dbsuv_ref,
             dsink_ref, dlg_ref, dlb_ref, dw_ref, dbt_ref, pl0_ref, pl1_ref, carry_ref,
             xl0_ref, x0_send, x0_recv, x0_flush, xl1_ref, x1_send, x1_recv, x1_flush):
        i = pl.program_id(0)
        exchanges = [(_ChipExchange(pw0_ref, xl0_ref, x0_send, x0_recv), pl0_ref, x0_flush),
                     (_ChipExchange(pw1_ref, xl1_ref, x1_send, x1_recv), pl1_ref, x1_flush)]

        @pl.when(i == 0)
        def _():
            for exchange, _, _ in exchanges:
                exchange.start()

        @pl.when(i == 0)
        def _():
            for r in (dbq_ref, dbkv_ref, dbsuv_ref, dsink_ref, dlg_ref, dlb_ref, dw_ref, dbt_ref, carry_ref):
                r[...] = jnp.zeros_like(r)

        def emit_kv(fin, t):
            if t == 0:
                tables = (tcp_ref[...], t1p_ref[...], t2p_ref[...])
            else:
                before = slice((t - 1) * BLK, t * BLK)
                tables = (tc_ref[before, :], t1_ref[before, :], t2_ref[before, :])
            dk = _rope_bwd(fin[:, 0:KV_W], *tables)
            out = jnp.concatenate([dk, fin[:, KV_W:2 * KV_W]], axis=1)
            dkv_ref[t * BLK:(t + 1) * BLK, :] = out.astype(_MXU)
            dbkv_ref[...] += _colsum(out)

        def one_block(s):
            rows = slice(s * BLK, (s + 1) * BLK)
            before = slice((s - 1) * BLK, s * BLK)
            k_prev = kp_ref[...] if s == 0 else kc_ref[before, :]
            v_prev = vp_ref[...] if s == 0 else vc_ref[before, :]
            allowed_t = _band_mask_t(i == 0 if s == 0 else False)
            kb = jnp.concatenate([k_prev, kc_ref[rows, :]], axis=0)
            vb = jnp.concatenate([v_prev, vc_ref[rows, :]], axis=0)
            qv = q_ref[rows, :]
            dmc = dmc_ref[rows, :]
            dqs, dks, dvs, dsinks = [], [], [], []
            allowed_g = jnp.tile(allowed_t, (1, Q_PER_KV))
            for g in range(N_KV):
                heads = range(g * Q_PER_KV, (g + 1) * Q_PER_KV)
                kh = kb[:, g * HEAD_DIM:(g + 1) * HEAD_DIM]
                vh = vb[:, g * HEAD_DIM:(g + 1) * HEAD_DIM]
                q_g = jnp.concatenate([qv[:, h * HEAD_DIM:(h + 1) * HEAD_DIM] for h in heads], axis=0)
                do_g = jnp.concatenate([dmc[:, h * HEAD_DIM:(h + 1) * HEAD_DIM] for h in heads], axis=0).astype(_MXU)
                sink_g = jnp.concatenate([jnp.full((1, BLK), sink_ref[h], F32) for h in heads], axis=1)
                probs_t, psink = _attn_probs_t(kh, q_g, sink_g, allowed_g)
                dvs.append(_dot(probs_t.astype(_MXU), do_g))
                dp_t = _dot_nt(vh, do_g)
                rd = jnp.sum(probs_t * dp_t, axis=0, keepdims=True)
                ds_t = (probs_t * (dp_t - rd)).astype(_MXU)
                ps_rd = psink * rd
                for hh in range(Q_PER_KV):
                    dsinks.append(-jnp.sum(ps_rd[:, hh * BLK:(hh + 1) * BLK], axis=1, keepdims=True))
                dq_g = _dot_tn(ds_t, kh)
                dqs += [dq_g[hh * BLK:(hh + 1) * BLK, :] for hh in range(Q_PER_KV)]
                dks.append(_dot(ds_t, q_g))
            dq = _rope_bwd(jnp.concatenate(dqs, axis=1) * (HEAD_DIM ** -0.5),
                           tc_ref[rows, :], t1_ref[rows, :], t2_ref[rows, :])
            dq_ref[rows, :] = dq.astype(_MXU)
            dbq_ref[...] += _colsum(dq)
            dsink_ref[...] += _lane_put(dsinks, 128)
            contrib = jnp.concatenate(dks + dvs, axis=1)

            lg = lg_ref[...]
            u, du_dsu = _gelu_and_grad(su_ref[rows, :])
            gv, dgv_dsv = _gelu_and_grad(sv_ref[rows, :])
            mixed, vhat, rstd, vvb, wcs = _sgu_mix(gv, lg, lb_ref[...], w_ref, bt_ref)
            dsgu = dmc[:, ATTN_W:D_MODEL]
            dsu = dsgu * mixed * du_dsu
            dmixed = dsgu * u
            tri_t = lax.broadcasted_iota(jnp.int32, (BLK, BLK), 0)
            tri_s = lax.broadcasted_iota(jnp.int32, (BLK, BLK), 1)
            dvv, dbs = [], []
            for h in range(N_GRP):
                dm = dmixed[:, h * GRP_DIM:(h + 1) * GRP_DIM]
                dmb = dm.astype(_MXU)
                dbs.append(jnp.sum(dm, axis=1, keepdims=True))
                dw_ref[h] += jnp.where(tri_s <= tri_t, _dot_nt(dmb, vvb[:, h * GRP_DIM:(h + 1) * GRP_DIM]), 0.0)
                dvv.append(_dot_tn(wcs[h], dmb))
            dvv = jnp.concatenate(dvv, axis=1)
            dbt_ref[...] += _lane_put(dbs, 128)
            dlg_ref[...] += _colsum(dvv * vhat)
            dlb_ref[...] += _colsum(dvv)
            dsv = _ln_bwd(dvv, vhat, rstd, lg) * dgv_dsv
            dsuv = jnp.concatenate([dsu, dsv], axis=1)
            dsuv_ref[rows, :] = dsuv.astype(_MXU)
            dbsuv_ref[...] += _colsum(dsuv)
            return contrib

        @pl.when(i < steps)
        def _():
            contribs = [one_block(s) for s in range(per)]
            for t in range(per):
                top = carry_ref[...] if t == 0 else contribs[t - 1][BLK:2 * BLK, :]
                emit_kv(top + contribs[t][0:BLK, :], t)
            carry_ref[...] = contribs[per - 1][BLK:2 * BLK, :]

        @pl.when(i == steps)
        def _():
            emit_kv(carry_ref[...], 0)
            if per > 1:
                dkv_ref[BLK:per * BLK, :] = jnp.zeros(((per - 1) * BLK, 2 * KV_W), _MXU)
            for exchange, landed, flush_sem in exchanges:
                exchange.finish_to(landed, flush_sem)

    last = steps - 1
    cur = lambda w: pl.BlockSpec((per * BLK, w), lambda i: (jnp.minimum(i, last), 0))
    prev = lambda w: pl.BlockSpec((BLK, w), lambda i: (jnp.clip(per * i - 1, 0, nb - 1), 0))
    shifted = pl.BlockSpec((per * BLK, 2 * KV_W), lambda i: (i, 0))
    sd = _hbm_shape
    return pl.pallas_call(
        body, name="mixer_bwd", grid=(steps + 1,),
        in_specs=[cur(ATTN_W), cur(KV_W), prev(KV_W), cur(KV_W), prev(KV_W), cur(SGU_W), cur(SGU_W), cur(D_MODEL),
                  cur(128), cur(128), cur(128), prev(128), prev(128), prev(128),
                  _smem(), _const2((1, SGU_W)), _const2((1, SGU_W)), _const2((N_GRP, BLK, BLK)), _const2((BLK, N_GRP)),
                  _vmem(), _vmem()],
        out_specs=[cur(ATTN_W), shifted, cur(2 * SGU_W),
                   _const2((1, ATTN_W)), _const2((1, 2 * KV_W)), _const2((1, 2 * SGU_W)),
                   _const2((1, 128)), _const2((1, SGU_W)), _const2((1, SGU_W)),
                   _const2((N_GRP, BLK, BLK)), _const2((BLK, 128)), _hbm(), _hbm()],
        out_shape=[sd((s_len, ATTN_W), _MXU), sd((s_len + per * BLK, 2 * KV_W), _MXU), sd((s_len, 2 * SGU_W), _MXU),
                   sd((1, ATTN_W), F32), sd((1, 2 * KV_W), F32), sd((1, 2 * SGU_W), F32),
                   sd((1, 128), F32), sd((1, SGU_W), F32), sd((1, SGU_W), F32),
                   sd((N_GRP, BLK, BLK), F32), sd((BLK, 128), F32)]
        + [_ChipExchange.land_shape(w) for w in prev_wires],
        scratch_shapes=[pltpu.VMEM((BLK, 2 * KV_W), F32)] + _ChipExchange.scratch(prev_wires[0])
        + _ChipExchange.scratch(prev_wires[1]),
        compiler_params=_params(40),
    )(q, k, k, v, v, su, sv, dmc, tc, t1, t2, tc, t1, t2, sinks, sg, sb, sgu_w, sgu_bt, *prev_wires)


def _inproj_bwd(dq, dkv_late, dsuv, dr1, x, g0, b0, w_in):
    s_len = x.shape[0]
    tm = _tile(s_len, 512)
    assert tm % BLK == 0
    per = tm // BLK
    cuts = ((0, ATTN_W), (ATTN_W, ATTN_W + 2 * KV_W), (ATTN_W + 2 * KV_W, IN_W))

    def body(dq_ref, *rest):
        dkv_refs = rest[:per]
        dsuv_ref, dr1_ref, x_ref, g_ref, b_ref, w_ref, dx_ref, dw_ref, dg_ref, db_ref = rest[per:]
        i = pl.program_id(0)

        @pl.when(i == 0)
        def _():
            dw_ref[...] = jnp.zeros_like(dw_ref)
            dg_ref[...] = jnp.zeros_like(dg_ref)
            db_ref[...] = jnp.zeros_like(db_ref)

        h0, xhat, rstd = _ln(x_ref[...], g_ref[...], b_ref[...])
        h0b = h0.astype(_MXU)
        dh0 = ALPHA * dr1_ref[...]
        dkv = jnp.concatenate([r[...] for r in dkv_refs], axis=0)
        for (lo, hi), d in zip(cuts, (dq_ref[...], dkv, dsuv_ref[...])):
            dh0 = dh0 + _dot(d, w_ref[lo:hi, :])
            dw_ref[lo:hi, :] += _dot_tn(d, h0b)
        dg_ref[...] += _colsum(dh0 * xhat)
        db_ref[...] += _colsum(dh0)
        dx_ref[...] = _ln_bwd(dh0, xhat, rstd, g_ref[...])

    vec = _hbm_shape((1, D_MODEL), F32)
    c = _const2((1, D_MODEL))
    return pl.pallas_call(
        body, name="inproj_bwd", grid=(s_len // tm,),
        in_specs=[_rows(tm, ATTN_W)]
        + [pl.BlockSpec((BLK, 2 * KV_W), lambda i, b=b: (i * per + b + 1, 0)) for b in range(per)]
        + [_rows(tm, 2 * SGU_W), _rows(tm, D_MODEL), _rows(tm, D_MODEL), c, c, _vmem()],
        out_specs=[_rows(tm, D_MODEL), _vmem(), c, c],
        out_shape=[_hbm_shape((s_len, D_MODEL), F32), jax.ShapeDtypeStruct((IN_W, D_MODEL), F32), vec, vec],
        compiler_params=_params(48),
    )(dq, *[dkv_late] * per, dsuv, dr1, x, g0, b0, w_in)


def _place():
    x, y, c = (lax.axis_index(a) for a in MESH_AXES)
    chips = [(1 - x, y), (x, 1 - y), (1 - x, 1 - y)]
    return x, y, c, chips


class _Gather:
    def __init__(self, ins, outs, send_sems, recv_sems, spans=None):
        self.ins, self.outs, self.send_sems, self.recv_sems = ins, outs, send_sems, recv_sems
        self.n = len(ins)
        self.spans = spans or [(0, r.shape[0]) for r in ins]
        self.halves = [(hi - lo) // 2 for lo, hi in self.spans]

    def _copy(self, k, t, slot, half, to):
        rows = pl.ds(pl.multiple_of(self.spans[t][0] + half * self.halves[t], 16), self.halves[t])
        piece = self.outs[t].at[slot, rows, :]
        return pltpu.make_async_remote_copy(src_ref=piece, dst_ref=piece, send_sem=self.send_sems.at[k],
                                            recv_sem=self.recv_sems.at[k], device_id=to, device_id_type=MESH)

    def _chip_copy(self, t, d, slot):
        x, y, c, chips = _place()
        return self._copy(3 * t + d, t, slot, c, (chips[d][0], chips[d][1], c))

    def _pass_copy(self, t, d, half):
        x, y, c, chips = _place()
        return self._copy(3 * self.n + 3 * t + d, t, 2 * chips[d][0] + chips[d][1], half, (x, y, 1 - c))

    def start(self):
        x, y, c, chips = _place()
        me = 2 * x + y
        for t in range(self.n):
            lo, hi = self.spans[t]
            self.outs[t][me, lo:hi, :] = self.ins[t][lo:hi, :].astype(_WIRE)
        for t in range(self.n):
            for d in range(3):
                self._chip_copy(t, d, me).start()

    def pass_on(self):
        x, y, c, chips = _place()
        for t in range(self.n):
            for d in range(3):
                self._chip_copy(t, d, 2 * chips[d][0] + chips[d][1]).wait_recv()
                self._pass_copy(t, d, c).start()

    def finish(self):
        x, y, c, chips = _place()
        me = 2 * x + y
        for t in range(self.n):
            for d in range(3):
                self._pass_copy(t, d, 1 - c).wait_recv()
        for t in range(self.n):
            for d in range(3):
                self._chip_copy(t, d, me).wait_send()
                self._pass_copy(t, d, c).wait_send()

    @staticmethod
    def out_shapes(shards, make=jax.ShapeDtypeStruct):
        return [make((N_CHIP,) + s.shape, _WIRE) for s in shards]

    @staticmethod
    def sems(n):
        return [pltpu.SemaphoreType.DMA((6 * n,)), pltpu.SemaphoreType.DMA((6 * n,))]


class _GatherPlan:
    def __init__(self, pieces):
        self.shards = [p[0] for p in pieces]
        self.spans = [p[1] for p in pieces]
        self.earlier = [p[2] for p in pieces]
        self.n = len(pieces)
        self.carried = [t for t in range(self.n) if self.earlier[t] is not None]

    def operands(self):
        return self.shards + [self.earlier[t] for t in self.carried]

    def in_specs(self):
        return [_vmem()] * self.n + [_hbm()] * len(self.carried)

    def out_specs(self):
        return [_hbm()] * self.n

    def out_shapes(self):
        return _Gather.out_shapes(self.shards, _hbm_shape)

    def scratch(self):
        return ([pltpu.VMEM((N_CHIP,) + s.shape, _WIRE) for s in self.shards] + _Gather.sems(self.n)
                + [pltpu.SemaphoreType.DMA((self.n,)), pltpu.SemaphoreType.DMA((max(len(self.carried), 1),))])

    def bind(self, in_refs, out_refs, scratch_refs):
        plan = self
        shard_refs, earlier_refs = in_refs[:self.n], in_refs[self.n:]
        bufs = scratch_refs[:self.n]
        send_sems, recv_sems, flush_sems, carry_sems = scratch_refs[self.n:self.n + 4]
        gather = _Gather(shard_refs, bufs, send_sems, recv_sems, self.spans)

        def carry_copy(k):
            t = plan.carried[k]
            lo = plan.spans[t][0]
            return pltpu.make_async_copy(earlier_refs[k].at[:, 0:lo, :], bufs[t].at[:, 0:lo, :], carry_sems.at[k])

        class Bound:
            @staticmethod
            def start():
                for k in range(len(plan.carried)):
                    carry_copy(k).start()
                gather.start()

            @staticmethod
            def pass_on():
                gather.pass_on()

            @staticmethod
            def finish():
                gather.finish()
                for k in range(len(plan.carried)):
                    carry_copy(k).wait()
                _flush([bufs[t].at[:, 0:plan.spans[t][1], :] for t in range(plan.n)],
                       [out_refs[t].at[:, 0:plan.spans[t][1], :] for t in range(plan.n)], flush_sems)

        return Bound


def _flush(bufs, hbm_outs, sems):
    copies = [pltpu.make_async_copy(b, o, sems.at[k]) for k, (b, o) in enumerate(zip(bufs, hbm_outs))]
    for cp in copies:
        cp.start()
    for cp in copies:
        cp.wait()


def _gather_weights(shards):
    n = len(shards)

    def body(*refs):
        gather = _Gather(refs[:n], refs[n:2 * n], refs[2 * n], refs[2 * n + 1])
        gather.start()
        gather.pass_on()
        gather.finish()

    return pl.pallas_call(
        body, name="gather_weights",
        in_specs=[_vmem()] * n, out_specs=[_vmem()] * n,
        out_shape=_Gather.out_shapes(shards), scratch_shapes=_Gather.sems(n),
        compiler_params=pltpu.CompilerParams(vmem_limit_bytes=32 * MIB),
    )(*shards)


class _ChipExchange:
    def __init__(self, wire_ref, land_ref, send_sems, recv_sems):
        self.wire, self.land, self.send_sems, self.recv_sems = wire_ref, land_ref, send_sems, recv_sems

    def _copy(self, d):
        x, y, c, chips = _place()
        return pltpu.make_async_remote_copy(
            src_ref=self.wire.at[2 * chips[d][0] + chips[d][1]], dst_ref=self.land.at[d],
            send_sem=self.send_sems.at[d], recv_sem=self.recv_sems.at[d],
            device_id=(chips[d][0], chips[d][1], c), device_id_type=MESH)

    def start(self):
        for d in range(3):
            self._copy(d).start()

    def wait_recv(self):
        for d in range(3):
            self._copy(d).wait_recv()

    def wait_send(self):
        for d in range(3):
            self._copy(d).wait_send()

    def finish_to(self, hbm_out, flush_sem):
        self.wait_recv()
        _flush([self.land], [hbm_out], flush_sem)
        self.wait_send()

    @staticmethod
    def land_shape(wire):
        return _hbm_shape((3,) + wire.shape[1:], wire.dtype)

    @staticmethod
    def sems():
        return [pltpu.SemaphoreType.DMA((3,)), pltpu.SemaphoreType.DMA((3,))]

    @staticmethod
    def scratch(wire):
        return ([pltpu.VMEM((3,) + wire.shape[1:], wire.dtype)] + _ChipExchange.sems() + [pltpu.SemaphoreType.DMA((1,))])


def _pair_out_shapes(half_shape):
    return [jax.ShapeDtypeStruct(half_shape, _WIRE), jax.ShapeDtypeStruct(half_shape[1:], F32)]


def _pair_scratch(acc_shape, half_shape):
    return [pltpu.VMEM(acc_shape, F32), pltpu.VMEM(half_shape, _WIRE),
            pltpu.SemaphoreType.DMA((N_CHIP,)), pltpu.SemaphoreType.DMA((N_CHIP,))]


def _pair_reduce(acc_ref, wire_ref, own_ref, land_ref, send_sems, recv_sems):
    rh = land_ref.shape[1]
    x, y, c, _ = _place()
    me = 2 * x + y
    copies = []
    for j in range(N_CHIP):
        def cast(r, carry, j=j):
            dst = pl.ds(pl.multiple_of(r * ROW_CHUNK, ROW_CHUNK), ROW_CHUNK)
            src = pl.ds(pl.multiple_of((2 * j + 1 - c) * rh + r * ROW_CHUNK, 8), ROW_CHUNK)
            wire_ref[j, dst, :] = acc_ref[src, :].astype(_WIRE)
            return carry

        lax.fori_loop(0, rh // ROW_CHUNK, cast, 0)
        cp = pltpu.make_async_remote_copy(src_ref=wire_ref.at[j], dst_ref=land_ref.at[j], send_sem=send_sems.at[j],
                                          recv_sem=recv_sems.at[j], device_id=(x, y, 1 - c), device_id_type=MESH)
        cp.start()
        copies.append(cp)
    for j in range(N_CHIP):
        copies[j].wait()

        def chunk(r, carry, j=j):
            theirs = pl.ds(pl.multiple_of(r * ROW_CHUNK, ROW_CHUNK), ROW_CHUNK)
            mine = pl.ds(pl.multiple_of((2 * j + c) * rh + r * ROW_CHUNK, 8), ROW_CHUNK)
            wire_ref[j, theirs, :] = (acc_ref[mine, :] + land_ref[j, theirs, :].astype(F32)).astype(_WIRE)
            return carry

        lax.fori_loop(0, rh // ROW_CHUNK, chunk, 0)

    def own_chunk(r, carry):
        theirs = pl.ds(pl.multiple_of(r * ROW_CHUNK, ROW_CHUNK), ROW_CHUNK)
        mine = pl.ds(pl.multiple_of((2 * me + c) * rh + r * ROW_CHUNK, 8), ROW_CHUNK)
        own_ref[theirs, :] = acc_ref[mine, :] + land_ref[me, theirs, :].astype(F32)
        return carry

    lax.fori_loop(0, rh // ROW_CHUNK, own_chunk, 0)


def _grad_finish(last_acc, lands, owns, small):
    n = len(owns) + 1
    halves = [last_acc.shape[0] // (2 * N_CHIP)] + [w.shape[1] for w in lands]
    widths = [last_acc.shape[1]] + [a.shape[1] for a in owns]
    small_body, small_scratch = _small_allreduce_parts()
    ns = len(small)

    def body(*refs):
        acc0, land, own = refs[0], (None,) + refs[1:n], (None,) + refs[n:2 * n - 1]
        refs = refs[2 * n - 1:]
        small_in, g, small_out = refs[:ns], refs[ns:ns + n], refs[ns + n:ns + n + 2]
        refs = refs[ns + n + 2:]
        pland0, wire0, land0, own0 = refs[0:4]
        p_send, p_recv, x_send, x_recv, pair_send, pair_recv = refs[4:10]
        small_refs = refs[10:]
        land = (land0,) + land[1:]
        own = (own0,) + own[1:]
        x, y, c, chips = _place()
        me = 2 * x + y
        exchange = _ChipExchange(wire0, land0, x_send, x_recv)

        def half_rows(t, half):
            return pl.ds(pl.multiple_of(half * halves[t], 8), halves[t])

        def pair_copy(t, half):
            rows = g[t].at[half_rows(t, half), :]
            return pltpu.make_async_remote_copy(src_ref=rows, dst_ref=rows, send_sem=pair_send.at[t],
                                                recv_sem=pair_recv.at[t], device_id=(x, y, 1 - c), device_id_type=MESH)

        small_rounds = small_body(*small_in, *small_out, *small_refs)
        next(small_rounds)
        _pair_reduce(acc0, wire0, own0, pland0, p_send, p_recv)
        next(small_rounds)
        exchange.start()

        for t in list(range(1, n)) + [0]:
            if t == 0:
                exchange.wait_recv()
            if t == min(2, n - 1):
                next(small_rounds)
            if t == min(4, n - 1):
                next(small_rounds, None)

            def chunk(r, carry, t=t):
                src = pl.ds(pl.multiple_of(r * ROW_CHUNK, ROW_CHUNK), ROW_CHUNK)
                dst = pl.ds(pl.multiple_of(c * halves[t] + r * ROW_CHUNK, 8), ROW_CHUNK)
                s = own[t][src, :]
                for d in range(3):
                    s = s + land[t][d, src, :].astype(F32)
                g[t][dst, :] = s
                return carry

            lax.fori_loop(0, halves[t] // ROW_CHUNK, chunk, 0)
            pair_copy(t, c).start()
        for t in range(n):
            pair_copy(t, 1 - c).wait_recv()
        for t in range(n):
            pair_copy(t, c).wait_send()
        exchange.wait_send()

    half0 = (halves[0], widths[0])
    return pl.pallas_call(
        body, name="grad_finish",
        in_specs=[_vmem()] * (2 * n - 1 + ns), out_specs=[_vmem()] * (n + 2),
        out_shape=[jax.ShapeDtypeStruct((2 * h, w), F32) for h, w in zip(halves, widths)]
        + [jax.ShapeDtypeStruct(s, F32) for s in _SMALL_OUT_DIMS],
        scratch_shapes=[pltpu.VMEM((N_CHIP,) + half0, _WIRE), pltpu.VMEM((N_CHIP,) + half0, _WIRE),
                        pltpu.VMEM((3,) + half0, _WIRE), pltpu.VMEM(half0, F32)]
        + [pltpu.SemaphoreType.DMA((N_CHIP,)), pltpu.SemaphoreType.DMA((N_CHIP,))]
        + _ChipExchange.sems()
        + [pltpu.SemaphoreType.DMA((n,)), pltpu.SemaphoreType.DMA((n,))]
        + small_scratch,
        compiler_params=pltpu.CompilerParams(vmem_limit_bytes=56 * MIB),
    )(last_acc, *lands, *owns, *small)


_SMALL = ("ln_in_g", "ln_in_b", "b_in", "attn_sinks", "sgu_ln_g", "sgu_ln_b", "sgu_w", "sgu_b", "b_out",
          "ln_mix_g", "ln_mix_b", "ln_ffn_g", "ln_ffn_b")
_VEC_ROW = dict(ln_in_g=0, ln_in_b=1, b_in=2, attn_sinks=4, sgu_ln_g=5, sgu_ln_b=6, b_out=7, ln_mix_g=8, ln_mix_b=9,
                ln_ffn_g=10, ln_ffn_b=11)
_LOSS_ROW = 12
_VEC_ROWS = 16
_MAT_ROWS = N_GRP * BLK + BLK


_SMALL_IN = ("ln_in_g", "ln_in_b", "bq", "bkv", "bsuv", "sink", "sgu_ln_g", "sgu_ln_b", "sgu_w", "sgu_bt", "b_out",
             "ln_mix_g", "ln_mix_b", "ln_ffn_g", "ln_ffn_b", "loss")
_SMALL_OUT_DIMS = ((_VEC_ROWS, D_MODEL), (_MAT_ROWS, 128))


def _small_allreduce_parts():
    n_in = len(_SMALL_IN)

    def body(*refs):
        (g_ln_in_g, g_ln_in_b, g_bq, g_bkv, g_bsuv, g_sink, g_sln_g, g_sln_b, g_sw, g_sbt, g_bout,
         g_lmg, g_lmb, g_lfg, g_lfb, g_loss) = refs[:n_in]
        out_a, out_b = refs[n_in:n_in + 2]
        (buf_a, buf_b, pair_a, pair_b, stage_a, stage_b, tot_a, tot_b,
         p1_send, p1_recv, x_send, x_recv, p2_send, p2_recv) = refs[n_in + 2:]
        x, y, c, chips = _place()
        me = 2 * x + y
        sibling = (x, y, 1 - c)
        half_a, half_b = _VEC_ROWS // 2, _MAT_ROWS // 2

        buf_a[...] = jnp.zeros_like(buf_a)
        for row, ref in ((0, g_ln_in_g), (1, g_ln_in_b), (7, g_bout), (8, g_lmg), (9, g_lmb), (10, g_lfg), (11, g_lfb),
                         (_LOSS_ROW, g_loss)):
            buf_a[row:row + 1, :] = ref[...]
        buf_a[2:3, 0:ATTN_W] = g_bq[...]
        buf_a[2:3, ATTN_W:ATTN_W + 2 * KV_W] = g_bkv[...]
        buf_a[2:3, ATTN_W + 2 * KV_W:D_MODEL] = g_bsuv[:, 0:2 * KV_W]
        buf_a[3:4, 0:2 * SGU_W - 2 * KV_W] = g_bsuv[:, 2 * KV_W:2 * SGU_W]
        buf_a[4:5, 0:128] = g_sink[...]
        buf_a[5:6, 0:SGU_W] = g_sln_g[...]
        buf_a[6:7, 0:SGU_W] = g_sln_b[...]
        for h in range(N_GRP):
            buf_b[h * BLK:(h + 1) * BLK, :] = g_sw[h]
        buf_b[N_GRP * BLK:_MAT_ROWS, :] = g_sbt[...]

        def remote(src, dst, send_sem, recv_sem, to):
            return pltpu.make_async_remote_copy(src_ref=src, dst_ref=dst, send_sem=send_sem, recv_sem=recv_sem,
                                                device_id=to, device_id_type=MESH)

        first = [remote(buf_a, pair_a, p1_send.at[0], p1_recv.at[0], sibling),
                 remote(buf_b, pair_b, p1_send.at[1], p1_recv.at[1], sibling)]
        for cp in first:
            cp.start()
        yield
        for cp in first:
            cp.wait()
        rows_a = pl.ds(pl.multiple_of(c * half_a, 8), half_a)
        rows_b = pl.ds(pl.multiple_of(c * half_b, 8), half_b)
        stage_a[me] = buf_a[rows_a, :] + pair_a[rows_a, :]
        stage_b[me] = buf_b[rows_b, :] + pair_b[rows_b, :]

        def chip_copies(d):
            to = (chips[d][0], chips[d][1], c)
            return [remote(stage_a.at[me], stage_a.at[me], x_send.at[2 * d], x_recv.at[2 * d], to),
                    remote(stage_b.at[me], stage_b.at[me], x_send.at[2 * d + 1], x_recv.at[2 * d + 1], to)]

        def chip_arrivals(d):
            slot = 2 * chips[d][0] + chips[d][1]
            to = (chips[d][0], chips[d][1], c)
            return [remote(stage_a.at[slot], stage_a.at[slot], x_send.at[2 * d], x_recv.at[2 * d], to),
                    remote(stage_b.at[slot], stage_b.at[slot], x_send.at[2 * d + 1], x_recv.at[2 * d + 1], to)]

        for d in range(3):
            for cp in chip_copies(d):
                cp.start()
        yield
        for d in range(3):
            for cp in chip_arrivals(d):
                cp.wait_recv()
        tot_a[rows_a, :] = ((stage_a[0] + stage_a[1]) + stage_a[2]) + stage_a[3]
        tot_b[rows_b, :] = ((stage_b[0] + stage_b[1]) + stage_b[2]) + stage_b[3]

        second = [remote(tot_a.at[rows_a, :], tot_a.at[rows_a, :], p2_send.at[0], p2_recv.at[0], sibling),
                  remote(tot_b.at[rows_b, :], tot_b.at[rows_b, :], p2_send.at[1], p2_recv.at[1], sibling)]
        for cp in second:
            cp.start()
        yield
        other_a = pl.ds(pl.multiple_of((1 - c) * half_a, 8), half_a)
        other_b = pl.ds(pl.multiple_of((1 - c) * half_b, 8), half_b)
        remote(tot_a.at[other_a, :], tot_a.at[other_a, :], p2_send.at[0], p2_recv.at[0], sibling).wait_recv()
        remote(tot_b.at[other_b, :], tot_b.at[other_b, :], p2_send.at[1], p2_recv.at[1], sibling).wait_recv()
        for cp in second:
            cp.wait_send()
        for d in range(3):
            for cp in chip_copies(d):
                cp.wait_send()
        out_a[...] = tot_a[...]
        out_b[...] = tot_b[...]

    vec = pltpu.VMEM((_VEC_ROWS, D_MODEL), F32)
    mat = pltpu.VMEM((_MAT_ROWS, 128), F32)
    scratch = [vec, mat, vec, mat, pltpu.VMEM((N_CHIP, _VEC_ROWS // 2, D_MODEL), F32),
               pltpu.VMEM((N_CHIP, _MAT_ROWS // 2, 128), F32), vec, mat,
               pltpu.SemaphoreType.DMA((2,)), pltpu.SemaphoreType.DMA((2,)), pltpu.SemaphoreType.DMA((6,)),
               pltpu.SemaphoreType.DMA((6,)), pltpu.SemaphoreType.DMA((2,)), pltpu.SemaphoreType.DMA((2,))]
    return body, scratch


def _small_adamw(tot_a, tot_b, params):
    shapes = [params[nm][0].shape for nm in _SMALL]

    def body(*refs):
        ta, tb = refs[:2]
        prm = refs[2:2 + 3 * len(_SMALL)]
        outs = refs[2 + 3 * len(_SMALL):]

        def grad_of(k, name):
            if name == "sgu_w":
                return [tb[h * BLK:(h + 1) * BLK, :] for h in range(N_GRP)]
            if name == "sgu_b":
                return jnp.transpose(tb[N_GRP * BLK:_MAT_ROWS, :])[0:N_GRP, :]
            row = _VEC_ROW[name]
            if name == "b_in":
                return jnp.concatenate([ta[row:row + 1, :], ta[row + 1:row + 2, 0:IN_W - D_MODEL]], axis=1)
            return ta[row:row + 1, 0:shapes[k][-1]]

        for k, name in enumerate(_SMALL):
            w_ref, m_ref, v_ref = prm[3 * k:3 * k + 3]
            g_out, d_out, m_out, v_out = outs[4 * k:4 * k + 4]
            g = grad_of(k, name)
            if name == "sgu_w":
                for h in range(N_GRP):
                    d_, m_, v_ = _adamw_math(w_ref[h], g[h], m_ref[h], v_ref[h])
                    g_out[h], d_out[h], m_out[h], v_out[h] = g[h], d_, m_, v_
            else:
                d_, m_, v_ = _adamw_math(w_ref[...], g, m_ref[...], v_ref[...])
                g_out[...], d_out[...], m_out[...], v_out[...] = g, d_, m_, v_
        outs[-1][...] = jnp.sum(ta[_LOSS_ROW:_LOSS_ROW + 1, :], axis=1, keepdims=True) * (0.5 / D_MODEL)

    ins = [tot_a, tot_b] + [_in_hbm(a) for nm in _SMALL for a in params[nm]]
    out_dims = [s for s in shapes for _ in range(4)] + [(1, 1)]
    res = pl.pallas_call(
        body, name="small_adamw", grid=(1,),
        in_specs=[_const2(a.shape) for a in ins], out_specs=[_const2(s) for s in out_dims],
        out_shape=[_hbm_shape(s, F32) for s in out_dims],
        compiler_params=_params(32),
    )(*ins)
    return {nm: tuple(res[4 * k:4 * k + 4]) for k, nm in enumerate(_SMALL)}, res[-1]


def _adamw_math(w, g, m, v):
    m = ADAM_B1 * m + (1.0 - ADAM_B1) * g
    v = ADAM_B2 * v + (1.0 - ADAM_B2) * (g * g)
    m_hat = m / (1.0 - ADAM_B1 ** ADAM_STEP)
    v_hat = v / (1.0 - ADAM_B2 ** ADAM_STEP)
    delta = -ADAM_LR * (m_hat / (jnp.sqrt(v_hat) + ADAM_EPS) + ADAM_WD * w)
    return delta, m, v


ADAMW_STEPS = 4


def _adamw(name, groups):
    k = len(groups)

    def body(*refs):
        for i in range(k):
            w_ref, g_ref, m_ref, v_ref = refs[4 * i:4 * i + 4]
            g = g_ref[...]
            for o_ref, o in zip(refs[4 * k + 4 * i:4 * k + 4 * i + 4], (g,) + _adamw_math(w_ref[...], g, m_ref[...], v_ref[...])):
                o_ref[...] = o

    specs = []
    for grp in groups:
        rows, cols = grp[0].shape
        assert rows % (8 * ADAMW_STEPS) == 0, rows
        specs += [pl.BlockSpec((rows // ADAMW_STEPS, cols), lambda i: (i, 0))] * 4
    res = pl.pallas_call(
        body, name=name, grid=(ADAMW_STEPS,), in_specs=specs, out_specs=specs,
        out_shape=[_hbm_shape(grp[0].shape, F32) for grp in groups for _ in range(4)],
        compiler_params=_params(56),
    )(*[_in_hbm(a) for grp in groups for a in grp])
    return [res[4 * i:4 * i + 4] for i in range(k)]


def kernel(x, positions, ln_in_g, ln_in_b, w_in, b_in, attn_sinks, sgu_ln_g, sgu_ln_b, sgu_w, sgu_b, w_out, b_out, ln_mix_g, ln_mix_b, w_gate, w_up, w_down, ln_ffn_g, ln_ffn_b, loss_target, m_ln_in_g, m_ln_in_b, m_w_in, m_b_in, m_attn_sinks, m_sgu_ln_g, m_sgu_ln_b, m_sgu_w, m_sgu_b, m_w_out, m_b_out, m_ln_mix_g, m_ln_mix_b, m_w_gate, m_w_up, m_w_down, m_ln_ffn_g, m_ln_ffn_b, v_ln_in_g, v_ln_in_b, v_w_in, v_b_in, v_attn_sinks, v_sgu_ln_g, v_sgu_ln_b, v_sgu_w, v_sgu_b, v_w_out, v_b_out, v_ln_mix_g, v_ln_mix_b, v_w_gate, v_w_up, v_w_down, v_ln_ffn_g, v_ln_ffn_b):
    weights = dict(ln_in_g=ln_in_g, ln_in_b=ln_in_b, w_in=w_in, b_in=b_in, attn_sinks=attn_sinks, sgu_ln_g=sgu_ln_g,
                   sgu_ln_b=sgu_ln_b, sgu_w=sgu_w, sgu_b=sgu_b, w_out=w_out, b_out=b_out, ln_mix_g=ln_mix_g,
                   ln_mix_b=ln_mix_b, w_gate=w_gate, w_up=w_up, w_down=w_down, ln_ffn_g=ln_ffn_g, ln_ffn_b=ln_ffn_b)
    mom_m = dict(ln_in_g=m_ln_in_g, ln_in_b=m_ln_in_b, w_in=m_w_in, b_in=m_b_in, attn_sinks=m_attn_sinks,
                 sgu_ln_g=m_sgu_ln_g, sgu_ln_b=m_sgu_ln_b, sgu_w=m_sgu_w, sgu_b=m_sgu_b, w_out=m_w_out, b_out=m_b_out,
                 ln_mix_g=m_ln_mix_g, ln_mix_b=m_ln_mix_b, w_gate=m_w_gate, w_up=m_w_up, w_down=m_w_down,
                 ln_ffn_g=m_ln_ffn_g, ln_ffn_b=m_ln_ffn_b)
    mom_v = dict(ln_in_g=v_ln_in_g, ln_in_b=v_ln_in_b, w_in=v_w_in, b_in=v_b_in, attn_sinks=v_attn_sinks,
                 sgu_ln_g=v_sgu_ln_g, sgu_ln_b=v_sgu_ln_b, sgu_w=v_sgu_w, sgu_b=v_sgu_b, w_out=v_w_out, b_out=v_b_out,
                 ln_mix_g=v_ln_mix_g, ln_mix_b=v_ln_mix_b, w_gate=v_w_gate, w_up=v_w_up, w_down=v_w_down,
                 ln_ffn_g=v_ln_ffn_g, ln_ffn_b=v_ln_ffn_b)
    order = list(weights)
    big = ("w_in", "w_out", "w_gate", "w_up", "w_down")

    s_len = x.shape[1]
    xs = _in_hbm(x.reshape(s_len, D_MODEL))
    tgt = _in_hbm(loss_target.reshape(s_len, D_MODEL))
    pos_row = _in_hbm(positions.reshape(1, s_len))
    g0, b0 = _in_hbm(ln_in_g.reshape(1, D_MODEL)), _in_hbm(ln_in_b.reshape(1, D_MODEL))
    sinks = attn_sinks.reshape(N_Q)
    sgu_w3 = _in_hbm(sgu_w.reshape(N_GRP, BLK, BLK))
    sgu_bt = _in_hbm(sgu_b.reshape(N_GRP, BLK).T)
    b_in, b_out, sgu_ln_g, sgu_ln_b, ln_mix_g, ln_mix_b, ln_ffn_g, ln_ffn_b = (
        _in_hbm(a) for a in (b_in, b_out, sgu_ln_g, sgu_ln_b, ln_mix_g, ln_mix_b, ln_ffn_g, ln_ffn_b))

    col_sharded = ("w_in", "w_gate", "w_up")

    def rowmajor(name, a):
        return jnp.swapaxes(a[0], 0, 1) if name in col_sharded else a[0]

    def as_given(name, a):
        return (jnp.swapaxes(a, 0, 1) if name in col_sharded else a)[None]

    shards = [rowmajor(n, weights[n]) for n in big]
    (gw_in,) = _gather_weights(shards[0:1])
    w_in_full = gw_in.reshape(IN_W, D_MODEL)

    sh_out, sh_gate, sh_up, sh_down = shards[1:]
    *acts, gw_out, gw_gate0 = _ln_inproj(xs, pos_row, g0, b0, w_in_full, b_in, _GatherPlan(
        [(sh_out, (0, OUT_SH), None), (sh_gate, (0, GATE_CUT), None)]))
    q, k, v, su, sv, tc, t1, t2 = (_in_hbm(a) for a in acts)
    mc, gw_gate1, gw_up0 = _mixer_fwd(q, k, v, su, sv, sinks, sgu_ln_g, sgu_ln_b, sgu_w3, sgu_bt, _GatherPlan(
        [(sh_gate, (GATE_CUT, FF_CUT), gw_gate0), (sh_up, (0, UP_CUT), None)]))
    mc = _in_hbm(mc)
    w_out_full = gw_out.reshape(D_MODEL, D_MODEL)
    r1, gw_up1 = _outproj(mc, w_out_full, b_out, xs, g0, b0, _GatherPlan([(sh_up, (UP_CUT, FF_CUT), gw_up0)]))
    r1 = _in_hbm(r1)
    *prior, h1, gw_gate, gw_up, gw_down0 = _ffn_up(
        "ffn_up_first", r1, ln_mix_g, ln_mix_b, gw_gate1, gw_up1, _GatherPlan(
            [(sh_gate, (FF_CUT, FF_SH), gw_gate1), (sh_up, (FF_CUT, FF_SH), gw_up1), (sh_down, (0, DOWN_CUT), None)]),
        (0, FF_CUT))
    act, p_act, q_act, gw_down = _ffn_up(
        "ffn_up_rest", r1, ln_mix_g, ln_mix_b, gw_gate, gw_up, _GatherPlan([(sh_down, (DOWN_CUT, FF_SH), gw_down0)]),
        (FF_CUT, FF_SH), tuple(prior))
    act, p_act, q_act = _in_hbm(act), _in_hbm(p_act), _in_hbm(q_act)
    dr2, loss_cols, d_ln_ffn_g, d_ln_ffn_b = _ffn_down_loss(act, gw_down, _in_hbm(h1), ln_ffn_g, ln_ffn_b, tgt)
    dr2 = _in_hbm(dr2)

    dg, du, wire_down, own_down = _ffn_bwd_a(dr2, act, p_act, q_act, gw_down)
    dh1a, wire_gate, own_gate, land_down = _ffn_bwd_g(dr2, _in_hbm(dg), r1, ln_mix_g, ln_mix_b, gw_gate, wire_down)
    dr1, wire_up, own_up, d_ln_mix_g, d_ln_mix_b, land_gate = _ffn_bwd_u(_in_hbm(dh1a), _in_hbm(du), r1, ln_mix_g,
                                                                         ln_mix_b, gw_up, wire_gate)
    dr1 = _in_hbm(dr1)
    dmc, wire_out, own_out, d_b_out = _outproj_bwd(dr1, mc, w_out_full)
    (dq, dkv, dsuv, dbq, dbkv, dbsuv, d_sink, d_sgu_ln_g, d_sgu_ln_b, d_sgu_w, d_sgu_bt, land_up, land_out) = _mixer_bwd(
        q, k, v, su, sv, _in_hbm(dmc), tc, t1, t2, sinks, sgu_ln_g, sgu_ln_b, sgu_w3, sgu_bt, [wire_up, wire_out])
    grad_x, acc_in, d_ln_in_g, d_ln_in_b = _inproj_bwd(_in_hbm(dq), _in_hbm(dkv), _in_hbm(dsuv), dr1, xs, g0, b0,
                                                       w_in_full)

    small_local = dict(
        ln_in_g=d_ln_in_g, ln_in_b=d_ln_in_b, bq=dbq, bkv=dbkv, bsuv=dbsuv, sink=d_sink, sgu_ln_g=d_sgu_ln_g,
        sgu_ln_b=d_sgu_ln_b, sgu_w=d_sgu_w, sgu_bt=d_sgu_bt, b_out=d_b_out, ln_mix_g=d_ln_mix_g, ln_mix_b=d_ln_mix_b,
        ln_ffn_g=d_ln_ffn_g, ln_ffn_b=d_ln_ffn_b, loss=loss_cols)
    *reduced, tot_a, tot_b = _grad_finish(acc_in, [land_out, land_gate, land_up, land_down],
                                          [own_out, own_gate, own_up, own_down], [small_local[nm] for nm in _SMALL_IN])
    small_shape = dict(ln_in_g=(1, D_MODEL), ln_in_b=(1, D_MODEL), sgu_w=(N_GRP, BLK, BLK), sgu_b=(N_GRP, BLK))
    small_params = {nm: tuple(src[nm].reshape(small_shape.get(nm, src[nm].shape)) for src in (weights, mom_m, mom_v))
                    for nm in _SMALL}
    small_out, loss = _small_adamw(_in_hbm(tot_a), _in_hbm(tot_b), small_params)
    loss = loss.reshape(())
    grads, delta, new_m, new_v = {}, {}, {}, {}
    for nm in _SMALL:
        grads[nm], delta[nm], new_m[nm], new_v[nm] = (a.reshape(weights[nm].shape) for a in small_out[nm])

    groups = [(shards[t], reduced[t], rowmajor(nm, mom_m[nm]), rowmajor(nm, mom_v[nm])) for t, nm in enumerate(big)]
    for nm, res in zip(big, _adamw("adamw", groups)):
        grads[nm], delta[nm], new_m[nm], new_v[nm] = (as_given(nm, a) for a in res)

    return (loss, grad_x.reshape(x.shape), *[grads[n] for n in order], *[delta[n] for n in order],
            *[new_m[n] for n in order], *[new_v[n] for n in order])
```

```python
import jax
import jax.numpy as jnp
from jax import lax
from jax.experimental import pallas as pl
from jax.experimental.pallas import tpu as pltpu

F32 = jnp.float32
_MXU = jnp.bfloat16
_WIRE = jnp.bfloat16
_ACT = jnp.bfloat16

D_MODEL = 1024
ATTN_W = 512
SGU_W = 512
HEAD_DIM = 64
N_Q = 8
N_KV = 2
Q_PER_KV = 4
KV_W = 128
BLK = 128
ROT_DIM = 16
ROPE_THETA = 500000.0
N_GRP = 4
GRP_DIM = 128
D_FF = 2816
IN_W = 1792
LN_EPS = 1e-5
ALPHA = 2.0 ** 0.25
N_CHIP = 4
FF_SH = D_FF // N_CHIP
IN_SH = IN_W // N_CHIP
OUT_SH = D_MODEL // N_CHIP
ROW_CHUNK = 32
GATE_CUT, UP_CUT = 352, 320

ADAM_LR = 0.001
ADAM_B1 = 0.9
ADAM_B2 = 0.999
ADAM_EPS = 1e-08
ADAM_WD = 0.01
ADAM_STEP = 10

SQRT_HALF = 0.7071067811865476
INV_SQRT_2PI = 0.3989422804014327
MESH_AXES = ("x", "y", "c")
MESH = pl.DeviceIdType.MESH
MIB = 2 ** 20


def _vmem():
    return pl.BlockSpec(memory_space=pltpu.VMEM)


def _smem():
    return pl.BlockSpec(memory_space=pltpu.SMEM)


def _hbm():
    return pl.BlockSpec(memory_space=pl.ANY)


def _hbm_shape(shape, dtype):
    return pltpu.HBM(shape, dtype)


def _in_hbm(a):
    return pltpu.with_memory_space_constraint(a, pltpu.HBM)


def _params(vmem_mib=48):
    return pltpu.CompilerParams(dimension_semantics=("arbitrary",), vmem_limit_bytes=vmem_mib * MIB)


def _tile(n, cap):
    if n <= cap:
        return n
    for t in range(cap - cap % 16, 0, -16):
        if n % t == 0:
            return t
    raise ValueError((n, cap))


def _rows(tm, width):
    return pl.BlockSpec((tm, width), lambda i: (i, 0))


def _const2(shape):
    return pl.BlockSpec(shape, lambda i: (0,) * len(shape))


def _ln(x, g, b):
    mu = jnp.mean(x, axis=-1, keepdims=True)
    xc = x - mu
    var = jnp.mean(xc * xc, axis=-1, keepdims=True)
    rstd = lax.rsqrt(var + LN_EPS)
    xhat = xc * rstd
    return xhat * g + b, xhat, rstd


def _ln_bwd(dy, xhat, rstd, g):
    gdy = dy * g
    m1 = jnp.mean(gdy, axis=-1, keepdims=True)
    m2 = jnp.mean(gdy * xhat, axis=-1, keepdims=True)
    return rstd * (gdy - m1 - xhat * m2)


def _colsum(a):
    return jnp.sum(a, axis=0, keepdims=True)


def _gelu_and_grad(x):
    cdf = 0.5 * (1.0 + lax.erf(x * SQRT_HALF))
    return x * cdf, cdf + x * jnp.exp(-0.5 * x * x) * INV_SQRT_2PI


def _dot(a, b):
    return jnp.dot(a, b, preferred_element_type=F32)


def _dot_nt(a, b):
    return lax.dot_general(a, b, (((1,), (1,)), ((), ())), preferred_element_type=F32)


def _dot_tn(a, b):
    return lax.dot_general(a, b, (((0,), (0,)), ((), ())), preferred_element_type=F32)


def _rope(t, tc, t1, t2):
    n = t.shape[1]
    rep = n // 128
    if rep > 1:
        tc, t1, t2 = (jnp.tile(a, (1, rep)) for a in (tc, t1, t2))
    return t * tc + pltpu.roll(t, n - 8, 1) * t1 + pltpu.roll(t, 8, 1) * t2


def _rope_bwd(d, tc, t1, t2):
    n = d.shape[1]
    rep = n // 128
    if rep > 1:
        tc, t1, t2 = (jnp.tile(a, (1, rep)) for a in (tc, t1, t2))
    return d * tc + pltpu.roll(d * t1, 8, 1) + pltpu.roll(d * t2, n - 8, 1)


def _causal_w(w_ref, h):
    t = lax.broadcasted_iota(jnp.int32, (BLK, BLK), 0)
    s = lax.broadcasted_iota(jnp.int32, (BLK, BLK), 1)
    return jnp.where(s <= t, w_ref[h], 0.0)


def _lane_put(vals, width):
    rows = vals[0].shape[0]
    lane = lax.broadcasted_iota(jnp.int32, (rows, width), 1)
    out = jnp.zeros((rows, width), F32)
    for k, v in enumerate(vals):
        out = out + jnp.where(lane == k, v, 0.0)
    return out


def _rope_consts():
    lane = jnp.arange(128) % HEAD_DIM
    rot = lane < ROT_DIM
    pair = (2 * (lane % (ROT_DIM // 2))).astype(F32)
    freq = jnp.where(rot, ROPE_THETA ** (-pair / ROT_DIM), 0.0)
    rows = [freq, rot.astype(F32), 1.0 - rot.astype(F32), (lane < ROT_DIM // 2).astype(F32),
            jnp.logical_and(lane >= ROT_DIM // 2, rot).astype(F32)]
    rows += [jnp.zeros((128,), F32)] * 3
    return jnp.stack(rows).astype(F32)


def _ln_inproj(x, pos_row, g0, b0, w_in, b_in, plan):
    s_len = x.shape[0]
    tm = _tile(s_len, 512)
    m, n = len(plan.operands()), plan.n

    def body(x_ref, pos_ref, g_ref, b_ref, w_ref, bi_ref, rc_ref, *rest):
        q_ref, k_ref, v_ref, su_ref, sv_ref, tc_ref, t1_ref, t2_ref = rest[m:m + 8]
        gather = plan.bind(rest[:m], rest[m + 8:m + 8 + n], rest[m + 8 + n:])
        i = pl.program_id(0)

        @pl.when(i == 0)
        def _():
            gather.start()

        h0, _, _ = _ln(x_ref[...], g_ref[...], b_ref[...])
        proj = _dot_nt(h0.astype(_MXU), w_ref[...]) + bi_ref[...]
        pos = jnp.broadcast_to(pos_ref[...].astype(F32), (128, tm))
        ang = jnp.transpose(pos) * rc_ref[0:1, :]
        cs = jnp.cos(ang)
        sn = jnp.sin(ang)
        tc = cs * rc_ref[1:2, :] + rc_ref[2:3, :]
        t1 = -sn * rc_ref[3:4, :]
        t2 = sn * rc_ref[4:5, :]
        tc_ref[...] = tc
        t1_ref[...] = t1
        t2_ref[...] = t2
        q = _rope(proj[:, 0:ATTN_W], tc, t1, t2) * (HEAD_DIM ** -0.5)
        q_ref[...] = q.astype(_MXU)
        k_ref[...] = _rope(proj[:, ATTN_W:ATTN_W + KV_W], tc, t1, t2).astype(_MXU)
        v_ref[...] = proj[:, ATTN_W + KV_W:ATTN_W + 2 * KV_W].astype(_MXU)
        su_ref[...] = proj[:, ATTN_W + 2 * KV_W:ATTN_W + 2 * KV_W + SGU_W]
        sv_ref[...] = proj[:, ATTN_W + 2 * KV_W + SGU_W:IN_W]

        last = pl.num_programs(0) - 1

        @pl.when(i == jnp.maximum(last - 1, 0))
        def _():
            gather.pass_on()

        @pl.when(i == last)
        def _():
            gather.finish()

    sd = _hbm_shape
    return pl.pallas_call(
        body, name="ln_inproj", grid=(s_len // tm,),
        in_specs=[_rows(tm, D_MODEL), pl.BlockSpec((1, tm), lambda i: (0, i)), _const2((1, D_MODEL)),
                  _const2((1, D_MODEL)), _vmem(),
                  _const2((1, IN_W)), _const2((8, 128))] + plan.in_specs(),
        out_specs=[_rows(tm, ATTN_W), _rows(tm, KV_W), _rows(tm, KV_W), _rows(tm, SGU_W), _rows(tm, SGU_W),
                   _rows(tm, 128), _rows(tm, 128), _rows(tm, 128)] + plan.out_specs(),
        out_shape=[sd((s_len, ATTN_W), _MXU), sd((s_len, KV_W), _MXU), sd((s_len, KV_W), _MXU),
                   sd((s_len, SGU_W), F32), sd((s_len, SGU_W), F32),
                   sd((s_len, 128), F32), sd((s_len, 128), F32), sd((s_len, 128), F32)] + plan.out_shapes(),
        scratch_shapes=plan.scratch(),
        compiler_params=_params(56),
    )(x, pos_row, g0, b0, w_in, b_in, _rope_consts(), *plan.operands())


def _band_mask_t(first_block):
    kj = lax.broadcasted_iota(jnp.int32, (2 * BLK, BLK), 0)
    qi = lax.broadcasted_iota(jnp.int32, (2 * BLK, BLK), 1)
    shut = jnp.where(first_block, 2 * BLK, 0)
    prev_ok = jnp.logical_and(kj < BLK, kj > qi + shut)
    cur_ok = jnp.logical_and(kj >= BLK, (kj - BLK) <= qi)
    return jnp.logical_or(prev_ok, cur_ok)


def _attn_probs_t(kh, qh, sink, allowed_t):
    s = jnp.where(allowed_t, _dot_nt(kh, qh), -1e30)
    m = jnp.maximum(jnp.max(s, axis=0, keepdims=True), sink)
    p = jnp.exp(s - m)
    ps = jnp.exp(sink - m)
    inv = 1.0 / (jnp.sum(p, axis=0, keepdims=True) + ps)
    return p * inv, ps * inv


def _sgu_mix(gv, lg, lb, w_ref, bt_ref):
    vv, vhat, rstd = _ln(gv, lg, lb)
    vvb = vv.astype(_MXU)
    wcs, mixed = [], []
    for h in range(N_GRP):
        wc = _causal_w(w_ref, h).astype(_MXU)
        wcs.append(wc)
        mixed.append(_dot(wc, vvb[:, h * GRP_DIM:(h + 1) * GRP_DIM]) + bt_ref[:, h:h + 1])
    return jnp.concatenate(mixed, axis=1), vhat, rstd, vvb, wcs


def _mixer_fwd(q, k, v, su, sv, sinks, sg, sb, sgu_w, sgu_bt, plan):
    s_len = q.shape[0]
    nb = s_len // BLK
    per = 2 if nb % 2 == 0 else 1
    steps = nb // per
    m, n = len(plan.operands()), plan.n

    def body(q_ref, kc_ref, kp_ref, vc_ref, vp_ref, su_ref, sv_ref, sink_ref, lg_ref, lb_ref, w_ref, bt_ref, *rest):
        mc_ref = rest[m]
        gather = plan.bind(rest[:m], rest[m + 1:m + 1 + n], rest[m + 1 + n:])
        i = pl.program_id(0)

        @pl.when(i == 0)
        def _():
            gather.start()

        @pl.when(i == max(steps - 2, 0))
        def _():
            gather.pass_on()

        @pl.when(i == steps - 1)
        def _():
            gather.finish()

        for s in range(per):
            rows = slice(s * BLK, (s + 1) * BLK)
            before = slice((s - 1) * BLK, s * BLK)
            k_prev = kp_ref[...] if s == 0 else kc_ref[before, :]
            v_prev = vp_ref[...] if s == 0 else vc_ref[before, :]
            allowed_t = _band_mask_t(i == 0 if s == 0 else False)
            kb = jnp.concatenate([k_prev, kc_ref[rows, :]], axis=0)
            vb = jnp.concatenate([v_prev, vc_ref[rows, :]], axis=0)
            qv = q_ref[rows, :]
            outs = []
            allowed_g = jnp.tile(allowed_t, (1, Q_PER_KV))
            for g in range(N_KV):
                heads = range(g * Q_PER_KV, (g + 1) * Q_PER_KV)
                kh = kb[:, g * HEAD_DIM:(g + 1) * HEAD_DIM]
                vh = vb[:, g * HEAD_DIM:(g + 1) * HEAD_DIM]
                q_g = jnp.concatenate([qv[:, h * HEAD_DIM:(h + 1) * HEAD_DIM] for h in heads], axis=0)
                sink_g = jnp.concatenate([jnp.full((1, BLK), sink_ref[h], F32) for h in heads], axis=1)
                probs_t, _ = _attn_probs_t(kh, q_g, sink_g, allowed_g)
                o_g = _dot_tn(probs_t.astype(_MXU), vh)
                outs += [o_g[hh * BLK:(hh + 1) * BLK, :] for hh in range(Q_PER_KV)]
            u = _gelu_and_grad(su_ref[rows, :])[0]
            gv = _gelu_and_grad(sv_ref[rows, :])[0]
            mixed = _sgu_mix(gv, lg_ref[...], lb_ref[...], w_ref, bt_ref)[0]
            mc_ref[rows, :] = jnp.concatenate(outs + [u * mixed], axis=1).astype(_MXU)

    cur = lambda w: pl.BlockSpec((per * BLK, w), lambda i: (i, 0))
    prev = lambda w: pl.BlockSpec((BLK, w), lambda i: (jnp.maximum(per * i - 1, 0), 0))
    return pl.pallas_call(
        body, name="mixer_fwd", grid=(steps,),
        in_specs=[cur(ATTN_W), cur(KV_W), prev(KV_W), cur(KV_W), prev(KV_W), cur(SGU_W), cur(SGU_W), _smem(),
                  _const2((1, SGU_W)), _const2((1, SGU_W)), _const2((N_GRP, BLK, BLK)), _const2((BLK, N_GRP))]
        + plan.in_specs(),
        out_specs=[cur(D_MODEL)] + plan.out_specs(),
        out_shape=[_hbm_shape((s_len, D_MODEL), _MXU)] + plan.out_shapes(),
        scratch_shapes=plan.scratch(),
        compiler_params=_params(56),
    )(q, k, k, v, v, su, sv, sinks, sg, sb, sgu_w, sgu_bt, *plan.operands())


def _outproj(mc, w_out, b_out, x, g0, b0, plan):
    s_len = x.shape[0]
    tm = _tile(s_len, 512)
    m, n = len(plan.operands()), plan.n

    def body(mc_ref, w_ref, bo_ref, x_ref, g_ref, b_ref, *rest):
        r1_ref = rest[m]
        gather = plan.bind(rest[:m], rest[m + 1:m + 1 + n], rest[m + 1 + n:])
        i = pl.program_id(0)

        @pl.when(i == 0)
        def _():
            gather.start()

        h0, _, _ = _ln(x_ref[...], g_ref[...], b_ref[...])
        r1_ref[...] = ALPHA * h0 + (_dot(mc_ref[...], w_ref[...]) + bo_ref[...])

        last = pl.num_programs(0) - 1

        @pl.when(i == jnp.maximum(last - 1, 0))
        def _():
            gather.pass_on()

        @pl.when(i == last)
        def _():
            gather.finish()

    return pl.pallas_call(
        body, name="outproj", grid=(s_len // tm,),
        in_specs=[_rows(tm, D_MODEL), _vmem(), _const2((1, D_MODEL)), _rows(tm, D_MODEL),
                  _const2((1, D_MODEL)), _const2((1, D_MODEL))] + plan.in_specs(),
        out_specs=[_rows(tm, D_MODEL)] + plan.out_specs(),
        out_shape=[_hbm_shape((s_len, D_MODEL), F32)] + plan.out_shapes(),
        scratch_shapes=plan.scratch(),
        compiler_params=_params(40),
    )(mc, w_out, b_out, x, g0, b0, *plan.operands())


def _ffn_spec(tm):
    return pl.BlockSpec((N_CHIP, tm, FF_SH), lambda i: (0, i, 0))


def _ffn_up(r1, g1, b1, wg, wu, plan):
    s_len = r1.shape[0]
    tm = _tile(s_len, 512)
    m, n = len(plan.operands()), plan.n

    def body(r1_ref, g_ref, b_ref, wg_ref, wu_ref, *rest):
        a_ref, p_ref, q_ref, h1_ref = rest[m:m + 4]
        gather = plan.bind(rest[:m], rest[m + 4:m + 4 + n], rest[m + 4 + n:])
        i = pl.program_id(0)

        @pl.when(i == 0)
        def _():
            gather.start()

        h1, _, _ = _ln(r1_ref[...], g_ref[...], b_ref[...])
        h1_ref[...] = h1
        h1b = h1.astype(_MXU)
        for j in range(N_CHIP):
            g = _dot_nt(h1b, wg_ref[j])
            u = _dot_nt(h1b, wu_ref[j])
            silu, sg = _silu_parts(g)
            a_ref[j] = (silu * u).astype(_MXU)
            p_ref[j] = silu.astype(_ACT)
            q_ref[j] = (u * (sg * (1.0 + g * (1.0 - sg)))).astype(_ACT)

        last = pl.num_programs(0) - 1

        @pl.when(i == jnp.maximum(last - 1, 0))
        def _():
            gather.pass_on()

        @pl.when(i == last)
        def _():
            gather.finish()

    sd = _hbm_shape((N_CHIP, s_len, FF_SH), _ACT)
    return pl.pallas_call(
        body, name="ffn_up", grid=(s_len // tm,),
        in_specs=[_rows(tm, D_MODEL), _const2((1, D_MODEL)), _const2((1, D_MODEL)), _vmem(), _vmem()] + plan.in_specs(),
        out_specs=[_ffn_spec(tm)] * 3 + [_rows(tm, D_MODEL)] + plan.out_specs(),
        out_shape=[_hbm_shape((N_CHIP, s_len, FF_SH), _MXU), sd, sd, _hbm_shape((s_len, D_MODEL), F32)]
        + plan.out_shapes(),
        scratch_shapes=plan.scratch(),
        compiler_params=_params(56),
    )(r1, g1, b1, wg, wu, *plan.operands())


def _silu_parts(g):
    sg = 1.0 / (1.0 + jnp.exp(-g))
    return g * sg, sg


def _ffn_down_loss(act, wd, h1, g2, b2, target):
    s_len = h1.shape[0]
    tm = _tile(s_len, 512)

    parts = 2 if tm % 32 == 0 else 1
    sub = tm // parts

    def body(a_ref, wd_ref, h1_ref, g2_ref, b2_ref, t_ref, dr2_ref, loss_ref, dg2_ref, db2_ref):
        i = pl.program_id(0)

        @pl.when(i == 0)
        def _():
            loss_ref[...] = jnp.zeros_like(loss_ref)
            dg2_ref[...] = jnp.zeros_like(dg2_ref)
            db2_ref[...] = jnp.zeros_like(db2_ref)

        for part in range(parts):
            rows = slice(part * sub, (part + 1) * sub)
            f = jnp.zeros((sub, D_MODEL), F32)
            for j in range(N_CHIP):
                f = f + _dot(a_ref[j, rows, :], wd_ref[j])
            h2, r2hat, rstd2 = _ln(ALPHA * h1_ref[rows, :] + f, g2_ref[...], b2_ref[...])
            diff = h2 - t_ref[rows, :]
            dh2 = diff * (1.0 / D_MODEL)
            loss_ref[...] += _colsum(diff * diff)
            dg2_ref[...] += _colsum(dh2 * r2hat)
            db2_ref[...] += _colsum(dh2)
            dr2_ref[rows, :] = _ln_bwd(dh2, r2hat, rstd2, g2_ref[...])

    vec = _hbm_shape((1, D_MODEL), F32)
    c = _const2((1, D_MODEL))
    return pl.pallas_call(
        body, name="ffn_down_loss", grid=(s_len // tm,),
        in_specs=[_ffn_spec(tm), _vmem(), _rows(tm, D_MODEL), c, c, _rows(tm, D_MODEL)],
        out_specs=[_rows(tm, D_MODEL), c, c, c],
        out_shape=[_hbm_shape((s_len, D_MODEL), F32), vec, vec, vec],
        compiler_params=_params(48),
    )(act, wd, h1, g2, b2, target)


def _ffn_bwd_a(dr2, act, p_act, q_act, wd):
    s_len = dr2.shape[0]
    tm = _tile(s_len, 512)

    def body(dr2_ref, a_ref, p_ref, q_ref, wd_ref, dg_ref, du_ref, wire_ref, own_ref,
             dwd_ref, land_ref, send_sem, recv_sem):
        i = pl.program_id(0)

        @pl.when(i == 0)
        def _():
            dwd_ref[...] = jnp.zeros_like(dwd_ref)

        dfb = dr2_ref[...].astype(_MXU)
        for j in range(N_CHIP):
            da = _dot_nt(dfb, wd_ref[j])
            dg_ref[j] = (da * q_ref[j].astype(F32)).astype(_MXU)
            du_ref[j] = (da * p_ref[j].astype(F32)).astype(_MXU)
            dwd_ref[j * FF_SH:(j + 1) * FF_SH, :] += _dot_tn(a_ref[j], dfb)

        @pl.when(i == pl.num_programs(0) - 1)
        def _():
            _pair_reduce(dwd_ref, wire_ref, own_ref, land_ref, send_sem, recv_sem)

    sd = _hbm_shape((N_CHIP, s_len, FF_SH), _MXU)
    half = (N_CHIP, FF_SH // 2, D_MODEL)
    return pl.pallas_call(
        body, name="ffn_bwd_a", grid=(s_len // tm,),
        in_specs=[_rows(tm, D_MODEL), _ffn_spec(tm), _ffn_spec(tm), _ffn_spec(tm), _vmem()],
        out_specs=[_ffn_spec(tm), _ffn_spec(tm), _vmem(), _vmem()],
        out_shape=[sd, sd] + _pair_out_shapes(half),
        scratch_shapes=_pair_scratch((D_FF, D_MODEL), half),
        compiler_params=_params(61),
    )(dr2, act, p_act, q_act, wd)


def _ffn_bwd_g(dr2, dg, r1, g1, b1, wg, prev_wire):
    s_len = dr2.shape[0]
    tm = _tile(s_len, 512)

    def body(dr2_ref, dg_ref, r1_ref, g1_ref, b1_ref, wg_ref, pw_ref, dh1_ref, wire_ref, own_ref, pl_ref,
             dwg_ref, land_ref, send_sem, recv_sem, xl_ref, x_send, x_recv, x_flush):
        i = pl.program_id(0)
        exchange = _ChipExchange(pw_ref, xl_ref, x_send, x_recv)

        @pl.when(i == 0)
        def _():
            exchange.start()
            dwg_ref[...] = jnp.zeros_like(dwg_ref)

        h1, _, _ = _ln(r1_ref[...], g1_ref[...], b1_ref[...])
        h1b = h1.astype(_MXU)
        dh1 = ALPHA * dr2_ref[...]
        for j in range(N_CHIP):
            dgj = dg_ref[j]
            dh1 = dh1 + _dot(dgj, wg_ref[j])
            dwg_ref[j * FF_SH:(j + 1) * FF_SH, :] += _dot_tn(dgj, h1b)
        dh1_ref[...] = dh1

        @pl.when(i == pl.num_programs(0) - 1)
        def _():
            _pair_reduce(dwg_ref, wire_ref, own_ref, land_ref, send_sem, recv_sem)
            exchange.finish_to(pl_ref, x_flush)

    c = _const2((1, D_MODEL))
    half = (N_CHIP, FF_SH // 2, D_MODEL)
    return pl.pallas_call(
        body, name="ffn_bwd_g", grid=(s_len // tm,),
        in_specs=[_rows(tm, D_MODEL), _ffn_spec(tm), _rows(tm, D_MODEL), c, c, _vmem(), _vmem()],
        out_specs=[_rows(tm, D_MODEL), _vmem(), _vmem(), _hbm()],
        out_shape=[_hbm_shape((s_len, D_MODEL), F32)] + _pair_out_shapes(half) + [_ChipExchange.land_shape(prev_wire)],
        scratch_shapes=_pair_scratch((D_FF, D_MODEL), half) + _ChipExchange.scratch(prev_wire),
        compiler_params=_params(58),
    )(dr2, dg, r1, g1, b1, wg, prev_wire)


def _ffn_bwd_u(dh1a, du, r1, g1, b1, wu, prev_wire):
    s_len = dh1a.shape[0]
    tm = _tile(s_len, 512)

    def body(dh1_ref, du_ref, r1_ref, g1_ref, b1_ref, wu_ref, pw_ref,
             dr1_ref, wire_ref, own_ref, dg1_ref, db1_ref, pl_ref,
             dwu_ref, land_ref, send_sem, recv_sem, xl_ref, x_send, x_recv, x_flush):
        i = pl.program_id(0)
        exchange = _ChipExchange(pw_ref, xl_ref, x_send, x_recv)

        @pl.when(i == 0)
        def _():
            exchange.start()
            dwu_ref[...] = jnp.zeros_like(dwu_ref)
            dg1_ref[...] = jnp.zeros_like(dg1_ref)
            db1_ref[...] = jnp.zeros_like(db1_ref)

        h1, r1hat, rstd1 = _ln(r1_ref[...], g1_ref[...], b1_ref[...])
        h1b = h1.astype(_MXU)
        dh1 = dh1_ref[...]
        for j in range(N_CHIP):
            duj = du_ref[j]
            dh1 = dh1 + _dot(duj, wu_ref[j])
            dwu_ref[j * FF_SH:(j + 1) * FF_SH, :] += _dot_tn(duj, h1b)
        dg1_ref[...] += _colsum(dh1 * r1hat)
        db1_ref[...] += _colsum(dh1)
        dr1_ref[...] = _ln_bwd(dh1, r1hat, rstd1, g1_ref[...])

        @pl.when(i == pl.num_programs(0) - 1)
        def _():
            _pair_reduce(dwu_ref, wire_ref, own_ref, land_ref, send_sem, recv_sem)
            exchange.finish_to(pl_ref, x_flush)

    vec = _hbm_shape((1, D_MODEL), F32)
    c = _const2((1, D_MODEL))
    half = (N_CHIP, FF_SH // 2, D_MODEL)
    return pl.pallas_call(
        body, name="ffn_bwd_u", grid=(s_len // tm,),
        in_specs=[_rows(tm, D_MODEL), _ffn_spec(tm), _rows(tm, D_MODEL), c, c, _vmem(), _vmem()],
        out_specs=[_rows(tm, D_MODEL), _vmem(), _vmem(), c, c, _hbm()],
        out_shape=[_hbm_shape((s_len, D_MODEL), F32)] + _pair_out_shapes(half)
        + [vec, vec, _ChipExchange.land_shape(prev_wire)],
        scratch_shapes=_pair_scratch((D_FF, D_MODEL), half) + _ChipExchange.scratch(prev_wire),
        compiler_params=_params(58),
    )(dh1a, du, r1, g1, b1, wu, prev_wire)


def _outproj_bwd(dr1, mc, w_out):
    s_len = dr1.shape[0]
    tm = _tile(s_len, 512)

    def body(dr1_ref, mc_ref, w_ref, dmc_ref, wire_ref, own_ref, db_ref, dw_ref, land_ref, send_sem, recv_sem):
        i = pl.program_id(0)

        @pl.when(i == 0)
        def _():
            dw_ref[...] = jnp.zeros_like(dw_ref)
            db_ref[...] = jnp.zeros_like(db_ref)

        d = dr1_ref[...]
        db_ref[...] += _colsum(d)
        db16 = d.astype(_MXU)
        dmc_ref[...] = _dot_nt(db16, w_ref[...])
        dw_ref[...] += _dot_tn(mc_ref[...], db16)

        @pl.when(i == pl.num_programs(0) - 1)
        def _():
            _pair_reduce(dw_ref, wire_ref, own_ref, land_ref, send_sem, recv_sem)

    half = (N_CHIP, OUT_SH // 2, D_MODEL)
    return pl.pallas_call(
        body, name="outproj_bwd", grid=(s_len // tm,),
        in_specs=[_rows(tm, D_MODEL), _rows(tm, D_MODEL), _vmem()],
        out_specs=[_rows(tm, D_MODEL), _vmem(), _vmem(), _const2((1, D_MODEL))],
        out_shape=[_hbm_shape((s_len, D_MODEL), F32)] + _pair_out_shapes(half) + [_hbm_shape((1, D_MODEL), F32)],
        scratch_shapes=_pair_scratch((D_MODEL, D_MODEL), half),
        compiler_params=_params(48),
    )(dr1, mc, w_out)


def _mixer_bwd(q, k, v, su, sv, dmc, tc, t1, t2, sinks, sg, sb, sgu_w, sgu_bt, prev_wires):
    s_len = q.shape[0]
    nb = s_len // BLK
    per = next(p for p in (4, 2, 1) if nb % p == 0)
    steps = nb // per

    def body(q_ref, kc_ref, kp_ref, vc_ref, vp_ref, su_ref, sv_ref, dmc_ref,
             tc_ref, t1_ref, t2_ref, tcp_ref, t1p_ref, t2p_ref,
             sink_ref, lg_ref, lb_ref, w_ref, bt_ref, pw0_ref, pw1_ref,
             dq_ref, dkv_ref, dsuv_ref, dbq_ref, dbkv_ref, dbsuv_ref,
             dsink_ref, dlg_ref, dlb_ref, dw_ref, dbt_ref, pl0_ref, pl1_ref, carry_ref,
             xl0_ref, x0_send, x0_recv, x0_flush, xl1_ref, x1_send, x1_recv, x1_flush):
        i = pl.program_id(0)
        exchanges = [(_ChipExchange(pw0_ref, xl0_ref, x0_send, x0_recv), pl0_ref, x0_flush),
                     (_ChipExchange(pw1_ref, xl1_ref, x1_send, x1_recv), pl1_ref, x1_flush)]

        @pl.when(i == 0)
        def _():
            for exchange, _, _ in exchanges:
                exchange.start()

        @pl.when(i == 0)
        def _():
            for r in (dbq_ref, dbkv_ref, dbsuv_ref, dsink_ref, dlg_ref, dlb_ref, dw_ref, dbt_ref, carry_ref):
                r[...] = jnp.zeros_like(r)

        def emit_kv(fin, t):
            if t == 0:
                tables = (tcp_ref[...], t1p_ref[...], t2p_ref[...])
            else:
                before = slice((t - 1) * BLK, t * BLK)
                tables = (tc_ref[before, :], t1_ref[before, :], t2_ref[before, :])
            dk = _rope_bwd(fin[:, 0:KV_W], *tables)
            out = jnp.concatenate([dk, fin[:, KV_W:2 * KV_W]], axis=1)
            dkv_ref[t * BLK:(t + 1) * BLK, :] = out.astype(_MXU)
            dbkv_ref[...] += _colsum(out)

        def one_block(s):
            rows = slice(s * BLK, (s + 1) * BLK)
            before = slice((s - 1) * BLK, s * BLK)
            k_prev = kp_ref[...] if s == 0 else kc_ref[before, :]
            v_prev = vp_ref[...] if s == 0 else vc_ref[before, :]
            allowed_t = _band_mask_t(i == 0 if s == 0 else False)
            kb = jnp.concatenate([k_prev, kc_ref[rows, :]], axis=0)
            vb = jnp.concatenate([v_prev, vc_ref[rows, :]], axis=0)
            qv = q_ref[rows, :]
            dmc = dmc_ref[rows, :]
            dqs, dks, dvs, dsinks = [], [], [], []
            allowed_g = jnp.tile(allowed_t, (1, Q_PER_KV))
            for g in range(N_KV):
                heads = range(g * Q_PER_KV, (g + 1) * Q_PER_KV)
                kh = kb[:, g * HEAD_DIM:(g + 1) * HEAD_DIM]
                vh = vb[:, g * HEAD_DIM:(g + 1) * HEAD_DIM]
                q_g = jnp.concatenate([qv[:, h * HEAD_DIM:(h + 1) * HEAD_DIM] for h in heads], axis=0)
                do_g = jnp.concatenate([dmc[:, h * HEAD_DIM:(h + 1) * HEAD_DIM] for h in heads], axis=0).astype(_MXU)
                sink_g = jnp.concatenate([jnp.full((1, BLK), sink_ref[h], F32) for h in heads], axis=1)
                probs_t, psink = _attn_probs_t(kh, q_g, sink_g, allowed_g)
                dvs.append(_dot(probs_t.astype(_MXU), do_g))
                dp_t = _dot_nt(vh, do_g)
                rd = jnp.sum(probs_t * dp_t, axis=0, keepdims=True)
                ds_t = (probs_t * (dp_t - rd)).astype(_MXU)
                ps_rd = psink * rd
                for hh in range(Q_PER_KV):
                    dsinks.append(-jnp.sum(ps_rd[:, hh * BLK:(hh + 1) * BLK], axis=1, keepdims=True))
                dq_g = _dot_tn(ds_t, kh)
                dqs += [dq_g[hh * BLK:(hh + 1) * BLK, :] for hh in range(Q_PER_KV)]
                dks.append(_dot(ds_t, q_g))
            dq = _rope_bwd(jnp.concatenate(dqs, axis=1) * (HEAD_DIM ** -0.5),
                           tc_ref[rows, :], t1_ref[rows, :], t2_ref[rows, :])
            dq_ref[rows, :] = dq.astype(_MXU)
            dbq_ref[...] += _colsum(dq)
            dsink_ref[...] += _lane_put(dsinks, 128)
            contrib = jnp.concatenate(dks + dvs, axis=1)

            lg = lg_ref[...]
            u, du_dsu = _gelu_and_grad(su_ref[rows, :])
            gv, dgv_dsv = _gelu_and_grad(sv_ref[rows, :])
            mixed, vhat, rstd, vvb, wcs = _sgu_mix(gv, lg, lb_ref[...], w_ref, bt_ref)
            dsgu = dmc[:, ATTN_W:D_MODEL]
            dsu = dsgu * mixed * du_dsu
            dmixed = dsgu * u
            tri_t = lax.broadcasted_iota(jnp.int32, (BLK, BLK), 0)
            tri_s = lax.broadcasted_iota(jnp.int32, (BLK, BLK), 1)
            dvv, dbs = [], []
            for h in range(N_GRP):
                dm = dmixed[:, h * GRP_DIM:(h + 1) * GRP_DIM]
                dmb = dm.astype(_MXU)
                dbs.append(jnp.sum(dm, axis=1, keepdims=True))
                dw_ref[h] += jnp.where(tri_s <= tri_t, _dot_nt(dmb, vvb[:, h * GRP_DIM:(h + 1) * GRP_DIM]), 0.0)
                dvv.append(_dot_tn(wcs[h], dmb))
            dvv = jnp.concatenate(dvv, axis=1)
            dbt_ref[...] += _lane_put(dbs, 128)
            dlg_ref[...] += _colsum(dvv * vhat)
            dlb_ref[...] += _colsum(dvv)
            dsv = _ln_bwd(dvv, vhat, rstd, lg) * dgv_dsv
            dsuv = jnp.concatenate([dsu, dsv], axis=1)
            dsuv_ref[rows, :] = dsuv.astype(_MXU)
            dbsuv_ref[...] += _colsum(dsuv)
            return contrib

        @pl.when(i < steps)
        def _():
            contribs = [one_block(s) for s in range(per)]
            for t in range(per):
                top = carry_ref[...] if t == 0 else contribs[t - 1][BLK:2 * BLK, :]
                emit_kv(top + contribs[t][0:BLK, :], t)
            carry_ref[...] = contribs[per - 1][BLK:2 * BLK, :]

        @pl.when(i == steps)
        def _():
            emit_kv(carry_ref[...], 0)
            if per > 1:
                dkv_ref[BLK:per * BLK, :] = jnp.zeros(((per - 1) * BLK, 2 * KV_W), _MXU)
            for exchange, landed, flush_sem in exchanges:
                exchange.finish_to(landed, flush_sem)

    last = steps - 1
    cur = lambda w: pl.BlockSpec((per * BLK, w), lambda i: (jnp.minimum(i, last), 0))
    prev = lambda w: pl.BlockSpec((BLK, w), lambda i: (jnp.clip(per * i - 1, 0, nb - 1), 0))
    shifted = pl.BlockSpec((per * BLK, 2 * KV_W), lambda i: (i, 0))
    sd = _hbm_shape
    return pl.pallas_call(
        body, name="mixer_bwd", grid=(steps + 1,),
        in_specs=[cur(ATTN_W), cur(KV_W), prev(KV_W), cur(KV_W), prev(KV_W), cur(SGU_W), cur(SGU_W), cur(D_MODEL),
                  cur(128), cur(128), cur(128), prev(128), prev(128), prev(128),
                  _smem(), _const2((1, SGU_W)), _const2((1, SGU_W)), _const2((N_GRP, BLK, BLK)), _const2((BLK, N_GRP)),
                  _vmem(), _vmem()],
        out_specs=[cur(ATTN_W), shifted, cur(2 * SGU_W),
                   _const2((1, ATTN_W)), _const2((1, 2 * KV_W)), _const2((1, 2 * SGU_W)),
                   _const2((1, 128)), _const2((1, SGU_W)), _const2((1, SGU_W)),
                   _const2((N_GRP, BLK, BLK)), _const2((BLK, 128)), _hbm(), _hbm()],
        out_shape=[sd((s_len, ATTN_W), _MXU), sd((s_len + per * BLK, 2 * KV_W), _MXU), sd((s_len, 2 * SGU_W), _MXU),
                   sd((1, ATTN_W), F32), sd((1, 2 * KV_W), F32), sd((1, 2 * SGU_W), F32),
                   sd((1, 128), F32), sd((1, SGU_W), F32), sd((1, SGU_W), F32),
                   sd((N_GRP, BLK, BLK), F32), sd((BLK, 128), F32)]
        + [_ChipExchange.land_shape(w) for w in prev_wires],
        scratch_shapes=[pltpu.VMEM((BLK, 2 * KV_W), F32)] + _ChipExchange.scratch(prev_wires[0])
        + _ChipExchange.scratch(prev_wires[1]),
        compiler_params=_params(40),
    )(q, k, k, v, v, su, sv, dmc, tc, t1, t2, tc, t1, t2, sinks, sg, sb, sgu_w, sgu_bt, *prev_wires)


def _inproj_bwd(dq, dkv_late, dsuv, dr1, x, g0, b0, w_in):
    s_len = x.shape[0]
    tm = _tile(s_len, 512)
    assert tm % BLK == 0
    per = tm // BLK
    cuts = ((0, ATTN_W), (ATTN_W, ATTN_W + 2 * KV_W), (ATTN_W + 2 * KV_W, IN_W))

    def body(dq_ref, *rest):
        dkv_refs = rest[:per]
        dsuv_ref, dr1_ref, x_ref, g_ref, b_ref, w_ref, dx_ref, dw_ref, dg_ref, db_ref = rest[per:]
        i = pl.program_id(0)

        @pl.when(i == 0)
        def _():
            dw_ref[...] = jnp.zeros_like(dw_ref)
            dg_ref[...] = jnp.zeros_like(dg_ref)
            db_ref[...] = jnp.zeros_like(db_ref)

        h0, xhat, rstd = _ln(x_ref[...], g_ref[...], b_ref[...])
        h0b = h0.astype(_MXU)
        dh0 = ALPHA * dr1_ref[...]
        dkv = jnp.concatenate([r[...] for r in dkv_refs], axis=0)
        for (lo, hi), d in zip(cuts, (dq_ref[...], dkv, dsuv_ref[...])):
            dh0 = dh0 + _dot(d, w_ref[lo:hi, :])
            dw_ref[lo:hi, :] += _dot_tn(d, h0b)
        dg_ref[...] += _colsum(dh0 * xhat)
        db_ref[...] += _colsum(dh0)
        dx_ref[...] = _ln_bwd(dh0, xhat, rstd, g_ref[...])

    vec = _hbm_shape((1, D_MODEL), F32)
    c = _const2((1, D_MODEL))
    return pl.pallas_call(
        body, name="inproj_bwd", grid=(s_len // tm,),
        in_specs=[_rows(tm, ATTN_W)]
        + [pl.BlockSpec((BLK, 2 * KV_W), lambda i, b=b: (i * per + b + 1, 0)) for b in range(per)]
        + [_rows(tm, 2 * SGU_W), _rows(tm, D_MODEL), _rows(tm, D_MODEL), c, c, _vmem()],
        out_specs=[_rows(tm, D_MODEL), _vmem(), c, c],
        out_shape=[_hbm_shape((s_len, D_MODEL), F32), jax.ShapeDtypeStruct((IN_W, D_MODEL), F32), vec, vec],
        compiler_params=_params(48),
    )(dq, *[dkv_late] * per, dsuv, dr1, x, g0, b0, w_in)


def _place():
    x, y, c = (lax.axis_index(a) for a in MESH_AXES)
    chips = [(1 - x, y), (x, 1 - y), (1 - x, 1 - y)]
    return x, y, c, chips


class _Gather:
    def __init__(self, ins, outs, send_sems, recv_sems, spans=None):
        self.ins, self.outs, self.send_sems, self.recv_sems = ins, outs, send_sems, recv_sems
        self.n = len(ins)
        self.spans = spans or [(0, r.shape[0]) for r in ins]
        self.halves = [(hi - lo) // 2 for lo, hi in self.spans]

    def _copy(self, k, t, slot, half, to):
        rows = pl.ds(pl.multiple_of(self.spans[t][0] + half * self.halves[t], 16), self.halves[t])
        piece = self.outs[t].at[slot, rows, :]
        return pltpu.make_async_remote_copy(src_ref=piece, dst_ref=piece, send_sem=self.send_sems.at[k],
                                            recv_sem=self.recv_sems.at[k], device_id=to, device_id_type=MESH)

    def _chip_copy(self, t, d, slot):
        x, y, c, chips = _place()
        return self._copy(3 * t + d, t, slot, c, (chips[d][0], chips[d][1], c))

    def _pass_copy(self, t, d, half):
        x, y, c, chips = _place()
        return self._copy(3 * self.n + 3 * t + d, t, 2 * chips[d][0] + chips[d][1], half, (x, y, 1 - c))

    def start(self):
        x, y, c, chips = _place()
        me = 2 * x + y
        for t in range(self.n):
            lo, hi = self.spans[t]
            self.outs[t][me, lo:hi, :] = self.ins[t][lo:hi, :].astype(_WIRE)
        for t in range(self.n):
            for d in range(3):
                self._chip_copy(t, d, me).start()

    def pass_on(self):
        x, y, c, chips = _place()
        for t in range(self.n):
            for d in range(3):
                self._chip_copy(t, d, 2 * chips[d][0] + chips[d][1]).wait_recv()
                self._pass_copy(t, d, c).start()

    def finish(self):
        x, y, c, chips = _place()
        me = 2 * x + y
        for t in range(self.n):
            for d in range(3):
                self._pass_copy(t, d, 1 - c).wait_recv()
        for t in range(self.n):
            for d in range(3):
                self._chip_copy(t, d, me).wait_send()
                self._pass_copy(t, d, c).wait_send()

    @staticmethod
    def out_shapes(shards, make=jax.ShapeDtypeStruct):
        return [make((N_CHIP,) + s.shape, _WIRE) for s in shards]

    @staticmethod
    def sems(n):
        return [pltpu.SemaphoreType.DMA((6 * n,)), pltpu.SemaphoreType.DMA((6 * n,))]


class _GatherPlan:
    def __init__(self, pieces):
        self.shards = [p[0] for p in pieces]
        self.spans = [p[1] for p in pieces]
        self.earlier = [p[2] for p in pieces]
        self.n = len(pieces)
        self.carried = [t for t in range(self.n) if self.earlier[t] is not None]

    def operands(self):
        return self.shards + [self.earlier[t] for t in self.carried]

    def in_specs(self):
        return [_vmem()] * self.n + [_hbm()] * len(self.carried)

    def out_specs(self):
        return [_hbm()] * self.n

    def out_shapes(self):
        return _Gather.out_shapes(self.shards, _hbm_shape)

    def scratch(self):
        return ([pltpu.VMEM((N_CHIP,) + s.shape, _WIRE) for s in self.shards] + _Gather.sems(self.n)
                + [pltpu.SemaphoreType.DMA((self.n,)), pltpu.SemaphoreType.DMA((max(len(self.carried), 1),))])

    def bind(self, in_refs, out_refs, scratch_refs):
        plan = self
        shard_refs, earlier_refs = in_refs[:self.n], in_refs[self.n:]
        bufs = scratch_refs[:self.n]
        send_sems, recv_sems, flush_sems, carry_sems = scratch_refs[self.n:self.n + 4]
        gather = _Gather(shard_refs, bufs, send_sems, recv_sems, self.spans)

        def carry_copy(k):
            t = plan.carried[k]
            lo = plan.spans[t][0]
            return pltpu.make_async_copy(earlier_refs[k].at[:, 0:lo, :], bufs[t].at[:, 0:lo, :], carry_sems.at[k])

        class Bound:
            @staticmethod
            def start():
                for k in range(len(plan.carried)):
                    carry_copy(k).start()
                gather.start()

            @staticmethod
            def pass_on():
                gather.pass_on()

            @staticmethod
            def finish():
                gather.finish()
                for k in range(len(plan.carried)):
                    carry_copy(k).wait()
                _flush([bufs[t].at[:, 0:plan.spans[t][1], :] for t in range(plan.n)],
                       [out_refs[t].at[:, 0:plan.spans[t][1], :] for t in range(plan.n)], flush_sems)

        return Bound


def _flush(bufs, hbm_outs, sems):
    copies = [pltpu.make_async_copy(b, o, sems.at[k]) for k, (b, o) in enumerate(zip(bufs, hbm_outs))]
    for cp in copies:
        cp.start()
    for cp in copies:
        cp.wait()


def _gather_weights(shards):
    n = len(shards)

    def body(*refs):
        gather = _Gather(refs[:n], refs[n:2 * n], refs[2 * n], refs[2 * n + 1])
        gather.start()
        gather.pass_on()
        gather.finish()

    return pl.pallas_call(
        body, name="gather_weights",
        in_specs=[_vmem()] * n, out_specs=[_vmem()] * n,
        out_shape=_Gather.out_shapes(shards), scratch_shapes=_Gather.sems(n),
        compiler_params=pltpu.CompilerParams(vmem_limit_bytes=32 * MIB),
    )(*shards)


class _ChipExchange:
    def __init__(self, wire_ref, land_ref, send_sems, recv_sems):
        self.wire, self.land, self.send_sems, self.recv_sems = wire_ref, land_ref, send_sems, recv_sems

    def _copy(self, d):
        x, y, c, chips = _place()
        return pltpu.make_async_remote_copy(
            src_ref=self.wire.at[2 * chips[d][0] + chips[d][1]], dst_ref=self.land.at[d],
            send_sem=self.send_sems.at[d], recv_sem=self.recv_sems.at[d],
            device_id=(chips[d][0], chips[d][1], c), device_id_type=MESH)

    def start(self):
        for d in range(3):
            self._copy(d).start()

    def wait_recv(self):
        for d in range(3):
            self._copy(d).wait_recv()

    def wait_send(self):
        for d in range(3):
            self._copy(d).wait_send()

    def finish_to(self, hbm_out, flush_sem):
        self.wait_recv()
        _flush([self.land], [hbm_out], flush_sem)
        self.wait_send()

    @staticmethod
    def land_shape(wire):
        return _hbm_shape((3,) + wire.shape[1:], wire.dtype)

    @staticmethod
    def sems():
        return [pltpu.SemaphoreType.DMA((3,)), pltpu.SemaphoreType.DMA((3,))]

    @staticmethod
    def scratch(wire):
        return ([pltpu.VMEM((3,) + wire.shape[1:], wire.dtype)] + _ChipExchange.sems() + [pltpu.SemaphoreType.DMA((1,))])


def _pair_out_shapes(half_shape):
    return [jax.ShapeDtypeStruct(half_shape, _WIRE), jax.ShapeDtypeStruct(half_shape[1:], F32)]


def _pair_scratch(acc_shape, half_shape):
    return [pltpu.VMEM(acc_shape, F32), pltpu.VMEM(half_shape, _WIRE),
            pltpu.SemaphoreType.DMA((N_CHIP,)), pltpu.SemaphoreType.DMA((N_CHIP,))]


def _pair_reduce(acc_ref, wire_ref, own_ref, land_ref, send_sems, recv_sems):
    rh = land_ref.shape[1]
    x, y, c, _ = _place()
    me = 2 * x + y
    copies = []
    for j in range(N_CHIP):
        def cast(r, carry, j=j):
            dst = pl.ds(pl.multiple_of(r * ROW_CHUNK, ROW_CHUNK), ROW_CHUNK)
            src = pl.ds(pl.multiple_of((2 * j + 1 - c) * rh + r * ROW_CHUNK, 8), ROW_CHUNK)
            wire_ref[j, dst, :] = acc_ref[src, :].astype(_WIRE)
            return carry

        lax.fori_loop(0, rh // ROW_CHUNK, cast, 0)
        cp = pltpu.make_async_remote_copy(src_ref=wire_ref.at[j], dst_ref=land_ref.at[j], send_sem=send_sems.at[j],
                                          recv_sem=recv_sems.at[j], device_id=(x, y, 1 - c), device_id_type=MESH)
        cp.start()
        copies.append(cp)
    for j in range(N_CHIP):
        copies[j].wait()

        def chunk(r, carry, j=j):
            theirs = pl.ds(pl.multiple_of(r * ROW_CHUNK, ROW_CHUNK), ROW_CHUNK)
            mine = pl.ds(pl.multiple_of((2 * j + c) * rh + r * ROW_CHUNK, 8), ROW_CHUNK)
            wire_ref[j, theirs, :] = (acc_ref[mine, :] + land_ref[j, theirs, :].astype(F32)).astype(_WIRE)
            return carry

        lax.fori_loop(0, rh // ROW_CHUNK, chunk, 0)

    def own_chunk(r, carry):
        theirs = pl.ds(pl.multiple_of(r * ROW_CHUNK, ROW_CHUNK), ROW_CHUNK)
        mine = pl.ds(pl.multiple_of((2 * me + c) * rh + r * ROW_CHUNK, 8), ROW_CHUNK)
        own_ref[theirs, :] = acc_ref[mine, :] + land_ref[me, theirs, :].astype(F32)
        return carry

    lax.fori_loop(0, rh // ROW_CHUNK, own_chunk, 0)


def _grad_finish(last_acc, lands, owns, small):
    n = len(owns) + 1
    halves = [last_acc.shape[0] // (2 * N_CHIP)] + [w.shape[1] for w in lands]
    widths = [last_acc.shape[1]] + [a.shape[1] for a in owns]
    small_body, small_scratch = _small_allreduce_parts()
    ns = len(small)

    def body(*refs):
        acc0, land, own = refs[0], (None,) + refs[1:n], (None,) + refs[n:2 * n - 1]
        refs = refs[2 * n - 1:]
        small_in, g_out, small_out = refs[:ns], refs[ns:ns + n], refs[ns + n:ns + n + 2]
        refs = refs[ns + n + 2:]
        pland0, wire0, land0, own0 = refs[0:4]
        p_send, p_recv, x_send, x_recv, pair_send, pair_recv = refs[4:10]
        g, flush_sems = refs[10:10 + n], refs[10 + n]
        small_refs = refs[11 + n:]
        land = (land0,) + land[1:]
        own = (own0,) + own[1:]
        x, y, c, chips = _place()
        me = 2 * x + y
        exchange = _ChipExchange(wire0, land0, x_send, x_recv)

        def half_rows(t, half):
            return pl.ds(pl.multiple_of(half * halves[t], 8), halves[t])

        def pair_copy(t, half):
            rows = g[t].at[half_rows(t, half), :]
            return pltpu.make_async_remote_copy(src_ref=rows, dst_ref=rows, send_sem=pair_send.at[t],
                                                recv_sem=pair_recv.at[t], device_id=(x, y, 1 - c), device_id_type=MESH)

        def flush(t):
            return pltpu.make_async_copy(g[t], g_out[t], flush_sems.at[t])

        small_rounds = small_body(*small_in, *small_out, *small_refs)
        next(small_rounds)
        _pair_reduce(acc0, wire0, own0, pland0, p_send, p_recv)
        next(small_rounds)
        exchange.start()

        for t in list(range(1, n)) + [0]:
            if t == 0:
                for done in range(1, n):
                    pair_copy(done, 1 - c).wait_recv()
                    flush(done).start()
                exchange.wait_recv()
            if t == min(2, n - 1):
                next(small_rounds)
            if t == min(4, n - 1):
                next(small_rounds, None)

            def chunk(r, carry, t=t):
                src = pl.ds(pl.multiple_of(r * ROW_CHUNK, ROW_CHUNK), ROW_CHUNK)
                dst = pl.ds(pl.multiple_of(c * halves[t] + r * ROW_CHUNK, 8), ROW_CHUNK)
                s = own[t][src, :]
                for d in range(3):
                    s = s + land[t][d, src, :].astype(F32)
                g[t][dst, :] = s
                return carry

            lax.fori_loop(0, halves[t] // ROW_CHUNK, chunk, 0)
            pair_copy(t, c).start()
        pair_copy(0, 1 - c).wait_recv()
        flush(0).start()
        for t in range(n):
            pair_copy(t, c).wait_send()
        exchange.wait_send()
        for t in range(n):
            flush(t).wait()

    half0 = (halves[0], widths[0])
    shapes = [(2 * h, w) for h, w in zip(halves, widths)]
    return pl.pallas_call(
        body, name="grad_finish",
        in_specs=[_vmem()] * (2 * n - 1 + ns), out_specs=[_hbm()] * n + [_vmem()] * 2,
        out_shape=[_hbm_shape(s, F32) for s in shapes] + [jax.ShapeDtypeStruct(s, F32) for s in _SMALL_OUT_DIMS],
        scratch_shapes=[pltpu.VMEM((N_CHIP,) + half0, _WIRE), pltpu.VMEM((N_CHIP,) + half0, _WIRE),
                        pltpu.VMEM((3,) + half0, _WIRE), pltpu.VMEM(half0, F32)]
        + [pltpu.SemaphoreType.DMA((N_CHIP,)), pltpu.SemaphoreType.DMA((N_CHIP,))]
        + _ChipExchange.sems()
        + [pltpu.SemaphoreType.DMA((n,)), pltpu.SemaphoreType.DMA((n,))]
        + [pltpu.VMEM(s, F32) for s in shapes] + [pltpu.SemaphoreType.DMA((n,))]
        + small_scratch,
        compiler_params=pltpu.CompilerParams(vmem_limit_bytes=56 * MIB),
    )(last_acc, *lands, *owns, *small)


_SMALL = ("ln_in_g", "ln_in_b", "b_in", "attn_sinks", "sgu_ln_g", "sgu_ln_b", "sgu_w", "sgu_b", "b_out",
          "ln_mix_g", "ln_mix_b", "ln_ffn_g", "ln_ffn_b")
_VEC_ROW = dict(ln_in_g=0, ln_in_b=1, b_in=2, attn_sinks=4, sgu_ln_g=5, sgu_ln_b=6, b_out=7, ln_mix_g=8, ln_mix_b=9,
                ln_ffn_g=10, ln_ffn_b=11)
_LOSS_ROW = 12
_VEC_ROWS = 16
_MAT_ROWS = N_GRP * BLK + BLK


_SMALL_IN = ("ln_in_g", "ln_in_b", "bq", "bkv", "bsuv", "sink", "sgu_ln_g", "sgu_ln_b", "sgu_w", "sgu_bt", "b_out",
             "ln_mix_g", "ln_mix_b", "ln_ffn_g", "ln_ffn_b", "loss")
_SMALL_OUT_DIMS = ((_VEC_ROWS, D_MODEL), (_MAT_ROWS, 128))


def _small_allreduce_parts():
    n_in = len(_SMALL_IN)

    def body(*refs):
        (g_ln_in_g, g_ln_in_b, g_bq, g_bkv, g_bsuv, g_sink, g_sln_g, g_sln_b, g_sw, g_sbt, g_bout,
         g_lmg, g_lmb, g_lfg, g_lfb, g_loss) = refs[:n_in]
        out_a, out_b = refs[n_in:n_in + 2]
        (buf_a, buf_b, pair_a, pair_b, stage_a, stage_b, tot_a, tot_b,
         p1_send, p1_recv, x_send, x_recv, p2_send, p2_recv) = refs[n_in + 2:]
        x, y, c, chips = _place()
        me = 2 * x + y
        sibling = (x, y, 1 - c)
        half_a, half_b = _VEC_ROWS // 2, _MAT_ROWS // 2

        buf_a[...] = jnp.zeros_like(buf_a)
        for row, ref in ((0, g_ln_in_g), (1, g_ln_in_b), (7, g_bout), (8, g_lmg), (9, g_lmb), (10, g_lfg), (11, g_lfb),
                         (_LOSS_ROW, g_loss)):
            buf_a[row:row + 1, :] = ref[...]
        buf_a[2:3, 0:ATTN_W] = g_bq[...]
        buf_a[2:3, ATTN_W:ATTN_W + 2 * KV_W] = g_bkv[...]
        buf_a[2:3, ATTN_W + 2 * KV_W:D_MODEL] = g_bsuv[:, 0:2 * KV_W]
        buf_a[3:4, 0:2 * SGU_W - 2 * KV_W] = g_bsuv[:, 2 * KV_W:2 * SGU_W]
        buf_a[4:5, 0:128] = g_sink[...]
        buf_a[5:6, 0:SGU_W] = g_sln_g[...]
        buf_a[6:7, 0:SGU_W] = g_sln_b[...]
        for h in range(N_GRP):
            buf_b[h * BLK:(h + 1) * BLK, :] = g_sw[h]
        buf_b[N_GRP * BLK:_MAT_ROWS, :] = g_sbt[...]

        def remote(src, dst, send_sem, recv_sem, to):
            return pltpu.make_async_remote_copy(src_ref=src, dst_ref=dst, send_sem=send_sem, recv_sem=recv_sem,
                                                device_id=to, device_id_type=MESH)

        first = [remote(buf_a, pair_a, p1_send.at[0], p1_recv.at[0], sibling),
                 remote(buf_b, pair_b, p1_send.at[1], p1_recv.at[1], sibling)]
        for cp in first:
            cp.start()
        yield
        for cp in first:
            cp.wait()
        rows_a = pl.ds(pl.multiple_of(c * half_a, 8), half_a)
        rows_b = pl.ds(pl.multiple_of(c * half_b, 8), half_b)
        stage_a[me] = buf_a[rows_a, :] + pair_a[rows_a, :]
        stage_b[me] = buf_b[rows_b, :] + pair_b[rows_b, :]

        def chip_copies(d):
            to = (chips[d][0], chips[d][1], c)
            return [remote(stage_a.at[me], stage_a.at[me], x_send.at[2 * d], x_recv.at[2 * d], to),
                    remote(stage_b.at[me], stage_b.at[me], x_send.at[2 * d + 1], x_recv.at[2 * d + 1], to)]

        def chip_arrivals(d):
            slot = 2 * chips[d][0] + chips[d][1]
            to = (chips[d][0], chips[d][1], c)
            return [remote(stage_a.at[slot], stage_a.at[slot], x_send.at[2 * d], x_recv.at[2 * d], to),
                    remote(stage_b.at[slot], stage_b.at[slot], x_send.at[2 * d + 1], x_recv.at[2 * d + 1], to)]

        for d in range(3):
            for cp in chip_copies(d):
                cp.start()
        yield
        for d in range(3):
            for cp in chip_arrivals(d):
                cp.wait_recv()
        tot_a[rows_a, :] = ((stage_a[0] + stage_a[1]) + stage_a[2]) + stage_a[3]
        tot_b[rows_b, :] = ((stage_b[0] + stage_b[1]) + stage_b[2]) + stage_b[3]

        second = [remote(tot_a.at[rows_a, :], tot_a.at[rows_a, :], p2_send.at[0], p2_recv.at[0], sibling),
                  remote(tot_b.at[rows_b, :], tot_b.at[rows_b, :], p2_send.at[1], p2_recv.at[1], sibling)]
        for cp in second:
            cp.start()
        yield
        other_a = pl.ds(pl.multiple_of((1 - c) * half_a, 8), half_a)
        other_b = pl.ds(pl.multiple_of((1 - c) * half_b, 8), half_b)
        remote(tot_a.at[other_a, :], tot_a.at[other_a, :], p2_send.at[0], p2_recv.at[0], sibling).wait_recv()
        remote(tot_b.at[other_b, :], tot_b.at[other_b, :], p2_send.at[1], p2_recv.at[1], sibling).wait_recv()
        for cp in second:
            cp.wait_send()
        for d in range(3):
            for cp in chip_copies(d):
                cp.wait_send()
        out_a[...] = tot_a[...]
        out_b[...] = tot_b[...]

    vec = pltpu.VMEM((_VEC_ROWS, D_MODEL), F32)
    mat = pltpu.VMEM((_MAT_ROWS, 128), F32)
    scratch = [vec, mat, vec, mat, pltpu.VMEM((N_CHIP, _VEC_ROWS // 2, D_MODEL), F32),
               pltpu.VMEM((N_CHIP, _MAT_ROWS // 2, 128), F32), vec, mat,
               pltpu.SemaphoreType.DMA((2,)), pltpu.SemaphoreType.DMA((2,)), pltpu.SemaphoreType.DMA((6,)),
               pltpu.SemaphoreType.DMA((6,)), pltpu.SemaphoreType.DMA((2,)), pltpu.SemaphoreType.DMA((2,))]
    return body, scratch


def _small_adamw(tot_a, tot_b, params):
    shapes = [params[nm][0].shape for nm in _SMALL]

    def body(*refs):
        ta, tb = refs[:2]
        prm = refs[2:2 + 3 * len(_SMALL)]
        outs = refs[2 + 3 * len(_SMALL):]

        def grad_of(k, name):
            if name == "sgu_w":
                return [tb[h * BLK:(h + 1) * BLK, :] for h in range(N_GRP)]
            if name == "sgu_b":
                return jnp.transpose(tb[N_GRP * BLK:_MAT_ROWS, :])[0:N_GRP, :]
            row = _VEC_ROW[name]
            if name == "b_in":
                return jnp.concatenate([ta[row:row + 1, :], ta[row + 1:row + 2, 0:IN_W - D_MODEL]], axis=1)
            return ta[row:row + 1, 0:shapes[k][-1]]

        for k, name in enumerate(_SMALL):
            w_ref, m_ref, v_ref = prm[3 * k:3 * k + 3]
            g_out, d_out, m_out, v_out = outs[4 * k:4 * k + 4]
            g = grad_of(k, name)
            if name == "sgu_w":
                for h in range(N_GRP):
                    d_, m_, v_ = _adamw_math(w_ref[h], g[h], m_ref[h], v_ref[h])
                    g_out[h], d_out[h], m_out[h], v_out[h] = g[h], d_, m_, v_
            else:
                d_, m_, v_ = _adamw_math(w_ref[...], g, m_ref[...], v_ref[...])
                g_out[...], d_out[...], m_out[...], v_out[...] = g, d_, m_, v_
        outs[-1][...] = jnp.sum(ta[_LOSS_ROW:_LOSS_ROW + 1, :], axis=1, keepdims=True) * (0.5 / D_MODEL)

    ins = [tot_a, tot_b] + [_in_hbm(a) for nm in _SMALL for a in params[nm]]
    out_dims = [s for s in shapes for _ in range(4)] + [(1, 1)]
    res = pl.pallas_call(
        body, name="small_adamw", grid=(1,),
        in_specs=[_const2(a.shape) for a in ins], out_specs=[_const2(s) for s in out_dims],
        out_shape=[_hbm_shape(s, F32) for s in out_dims],
        compiler_params=_params(32),
    )(*ins)
    return {nm: tuple(res[4 * k:4 * k + 4]) for k, nm in enumerate(_SMALL)}, res[-1]


def _adamw_math(w, g, m, v):
    m = ADAM_B1 * m + (1.0 - ADAM_B1) * g
    v = ADAM_B2 * v + (1.0 - ADAM_B2) * (g * g)
    m_hat = m / (1.0 - ADAM_B1 ** ADAM_STEP)
    v_hat = v / (1.0 - ADAM_B2 ** ADAM_STEP)
    delta = -ADAM_LR * (m_hat / (jnp.sqrt(v_hat) + ADAM_EPS) + ADAM_WD * w)
    return delta, m, v


ADAMW_STEPS = 4


def _adamw(name, groups):
    k = len(groups)

    def body(*refs):
        for i in range(k):
            w_ref, g_ref, m_ref, v_ref = refs[4 * i:4 * i + 4]
            g = g_ref[...]
            for o_ref, o in zip(refs[4 * k + 4 * i:4 * k + 4 * i + 4], (g,) + _adamw_math(w_ref[...], g, m_ref[...], v_ref[...])):
                o_ref[...] = o

    specs = []
    for grp in groups:
        rows, cols = grp[0].shape
        assert rows % (8 * ADAMW_STEPS) == 0, rows
        specs += [pl.BlockSpec((rows // ADAMW_STEPS, cols), lambda i: (i, 0))] * 4
    res = pl.pallas_call(
        body, name=name, grid=(ADAMW_STEPS,), in_specs=specs, out_specs=specs,
        out_shape=[_hbm_shape(grp[0].shape, F32) for grp in groups for _ in range(4)],
        compiler_params=_params(56),
    )(*[_in_hbm(a) for grp in groups for a in grp])
    return [res[4 * i:4 * i + 4] for i in range(k)]


def kernel(x, positions, ln_in_g, ln_in_b, w_in, b_in, attn_sinks, sgu_ln_g, sgu_ln_b, sgu_w, sgu_b, w_out, b_out, ln_mix_g, ln_mix_b, w_gate, w_up, w_down, ln_ffn_g, ln_ffn_b, loss_target, m_ln_in_g, m_ln_in_b, m_w_in, m_b_in, m_attn_sinks, m_sgu_ln_g, m_sgu_ln_b, m_sgu_w, m_sgu_b, m_w_out, m_b_out, m_ln_mix_g, m_ln_mix_b, m_w_gate, m_w_up, m_w_down, m_ln_ffn_g, m_ln_ffn_b, v_ln_in_g, v_ln_in_b, v_w_in, v_b_in, v_attn_sinks, v_sgu_ln_g, v_sgu_ln_b, v_sgu_w, v_sgu_b, v_w_out, v_b_out, v_ln_mix_g, v_ln_mix_b, v_w_gate, v_w_up, v_w_down, v_ln_ffn_g, v_ln_ffn_b):
    weights = dict(ln_in_g=ln_in_g, ln_in_b=ln_in_b, w_in=w_in, b_in=b_in, attn_sinks=attn_sinks, sgu_ln_g=sgu_ln_g,
                   sgu_ln_b=sgu_ln_b, sgu_w=sgu_w, sgu_b=sgu_b, w_out=w_out, b_out=b_out, ln_mix_g=ln_mix_g,
                   ln_mix_b=ln_mix_b, w_gate=w_gate, w_up=w_up, w_down=w_down, ln_ffn_g=ln_ffn_g, ln_ffn_b=ln_ffn_b)
    mom_m = dict(ln_in_g=m_ln_in_g, ln_in_b=m_ln_in_b, w_in=m_w_in, b_in=m_b_in, attn_sinks=m_attn_sinks,
                 sgu_ln_g=m_sgu_ln_g, sgu_ln_b=m_sgu_ln_b, sgu_w=m_sgu_w, sgu_b=m_sgu_b, w_out=m_w_out, b_out=m_b_out,
                 ln_mix_g=m_ln_mix_g, ln_mix_b=m_ln_mix_b, w_gate=m_w_gate, w_up=m_w_up, w_down=m_w_down,
                 ln_ffn_g=m_ln_ffn_g, ln_ffn_b=m_ln_ffn_b)
    mom_v = dict(ln_in_g=v_ln_in_g, ln_in_b=v_ln_in_b, w_in=v_w_in, b_in=v_b_in, attn_sinks=v_attn_sinks,
                 sgu_ln_g=v_sgu_ln_g, sgu_ln_b=v_sgu_ln_b, sgu_w=v_sgu_w, sgu_b=v_sgu_b, w_out=v_w_out, b_out=v_b_out,
                 ln_mix_g=v_ln_mix_g, ln_mix_b=v_ln_mix_b, w_gate=v_w_gate, w_up=v_w_up, w_down=v_w_down,
                 ln_ffn_g=v_ln_ffn_g, ln_ffn_b=v_ln_ffn_b)
    order = list(weights)
    big = ("w_in", "w_out", "w_gate", "w_up", "w_down")

    s_len = x.shape[1]
    xs = _in_hbm(x.reshape(s_len, D_MODEL))
    tgt = _in_hbm(loss_target.reshape(s_len, D_MODEL))
    pos_row = _in_hbm(positions.reshape(1, s_len))
    g0, b0 = _in_hbm(ln_in_g.reshape(1, D_MODEL)), _in_hbm(ln_in_b.reshape(1, D_MODEL))
    sinks = attn_sinks.reshape(N_Q)
    sgu_w3 = _in_hbm(sgu_w.reshape(N_GRP, BLK, BLK))
    sgu_bt = _in_hbm(sgu_b.reshape(N_GRP, BLK).T)
    b_in, b_out, sgu_ln_g, sgu_ln_b, ln_mix_g, ln_mix_b, ln_ffn_g, ln_ffn_b = (
        _in_hbm(a) for a in (b_in, b_out, sgu_ln_g, sgu_ln_b, ln_mix_g, ln_mix_b, ln_ffn_g, ln_ffn_b))

    col_sharded = ("w_in", "w_gate", "w_up")

    def rowmajor(name, a):
        return jnp.swapaxes(a[0], 0, 1) if name in col_sharded else a[0]

    def as_given(name, a):
        return (jnp.swapaxes(a, 0, 1) if name in col_sharded else a)[None]

    shards = [rowmajor(n, weights[n]) for n in big]
    (gw_in,) = _gather_weights(shards[0:1])
    w_in_full = gw_in.reshape(IN_W, D_MODEL)

    sh_out, sh_gate, sh_up, sh_down = shards[1:]
    *acts, gw_out, gw_gate0 = _ln_inproj(xs, pos_row, g0, b0, w_in_full, b_in, _GatherPlan(
        [(sh_out, (0, OUT_SH), None), (sh_gate, (0, GATE_CUT), None)]))
    q, k, v, su, sv, tc, t1, t2 = (_in_hbm(a) for a in acts)
    mc, gw_gate, gw_up0 = _mixer_fwd(q, k, v, su, sv, sinks, sgu_ln_g, sgu_ln_b, sgu_w3, sgu_bt, _GatherPlan(
        [(sh_gate, (GATE_CUT, FF_SH), gw_gate0), (sh_up, (0, UP_CUT), None)]))
    mc = _in_hbm(mc)
    w_out_full = gw_out.reshape(D_MODEL, D_MODEL)
    r1, gw_up = _outproj(mc, w_out_full, b_out, xs, g0, b0, _GatherPlan([(sh_up, (UP_CUT, FF_SH), gw_up0)]))
    r1 = _in_hbm(r1)
    act, p_act, q_act, h1, gw_down = _ffn_up(r1, ln_mix_g, ln_mix_b, gw_gate, gw_up,
                                             _GatherPlan([(sh_down, (0, FF_SH), None)]))
    act, p_act, q_act = _in_hbm(act), _in_hbm(p_act), _in_hbm(q_act)
    dr2, loss_cols, d_ln_ffn_g, d_ln_ffn_b = _ffn_down_loss(act, gw_down, _in_hbm(h1), ln_ffn_g, ln_ffn_b, tgt)
    dr2 = _in_hbm(dr2)

    dg, du, wire_down, own_down = _ffn_bwd_a(dr2, act, p_act, q_act, gw_down)
    dh1a, wire_gate, own_gate, land_down = _ffn_bwd_g(dr2, _in_hbm(dg), r1, ln_mix_g, ln_mix_b, gw_gate, wire_down)
    dr1, wire_up, own_up, d_ln_mix_g, d_ln_mix_b, land_gate = _ffn_bwd_u(_in_hbm(dh1a), _in_hbm(du), r1, ln_mix_g,
                                                                         ln_mix_b, gw_up, wire_gate)
    dr1 = _in_hbm(dr1)
    dmc, wire_out, own_out, d_b_out = _outproj_bwd(dr1, mc, w_out_full)
    (dq, dkv, dsuv, dbq, dbkv, dbsuv, d_sink, d_sgu_ln_g, d_sgu_ln_b, d_sgu_w, d_sgu_bt, land_up, land_out) = _mixer_bwd(
        q, k, v, su, sv, _in_hbm(dmc), tc, t1, t2, sinks, sgu_ln_g, sgu_ln_b, sgu_w3, sgu_bt, [wire_up, wire_out])
    grad_x, acc_in, d_ln_in_g, d_ln_in_b = _inproj_bwd(_in_hbm(dq), _in_hbm(dkv), _in_hbm(dsuv), dr1, xs, g0, b0,
                                                       w_in_full)

    small_local = dict(
        ln_in_g=d_ln_in_g, ln_in_b=d_ln_in_b, bq=dbq, bkv=dbkv, bsuv=dbsuv, sink=d_sink, sgu_ln_g=d_sgu_ln_g,
        sgu_ln_b=d_sgu_ln_b, sgu_w=d_sgu_w, sgu_bt=d_sgu_bt, b_out=d_b_out, ln_mix_g=d_ln_mix_g, ln_mix_b=d_ln_mix_b,
        ln_ffn_g=d_ln_ffn_g, ln_ffn_b=d_ln_ffn_b, loss=loss_cols)
    *reduced, tot_a, tot_b = _grad_finish(acc_in, [land_out, land_gate, land_up, land_down],
                                          [own_out, own_gate, own_up, own_down], [small_local[nm] for nm in _SMALL_IN])
    small_shape = dict(ln_in_g=(1, D_MODEL), ln_in_b=(1, D_MODEL), sgu_w=(N_GRP, BLK, BLK), sgu_b=(N_GRP, BLK))
    small_params = {nm: tuple(src[nm].reshape(small_shape.get(nm, src[nm].shape)) for src in (weights, mom_m, mom_v))
                    for nm in _SMALL}
    small_out, loss = _small_adamw(_in_hbm(tot_a), _in_hbm(tot_b), small_params)
    loss = loss.reshape(())
    grads, delta, new_m, new_v = {}, {}, {}, {}
    for nm in _SMALL:
        grads[nm], delta[nm], new_m[nm], new_v[nm] = (a.reshape(weights[nm].shape) for a in small_out[nm])

    groups = [(shards[t], reduced[t], rowmajor(nm, mom_m[nm]), rowmajor(nm, mom_v[nm])) for t, nm in enumerate(big)]
    for nm, res in zip(big, _adamw("adamw", groups)):
        grads[nm], delta[nm], new_m[nm], new_v[nm] = (as_given(nm, a) for a in res)

    return (loss, grad_x.reshape(x.shape), *[grads[n] for n in order], *[delta[n] for n in order],
            *[new_m[n] for n in order], *[new_v[n] for n in order])
```

```python
import jax
import jax.numpy as jnp
from jax import lax
from jax.experimental import pallas as pl
from jax.experimental.pallas import tpu as pltpu

F32 = jnp.float32
_MXU = jnp.bfloat16
_WIRE = jnp.bfloat16
_ACT = jnp.bfloat16

D_MODEL = 1024
ATTN_W = 512
SGU_W = 512
HEAD_DIM = 64
N_Q = 8
N_KV = 2
Q_PER_KV = 4
KV_W = 128
BLK = 128
ROT_DIM = 16
ROPE_THETA = 500000.0
N_GRP = 4
GRP_DIM = 128
D_FF = 2816
IN_W = 1792
LN_EPS = 1e-5
ALPHA = 2.0 ** 0.25
N_CHIP = 4
FF_SH = D_FF // N_CHIP
IN_SH = IN_W // N_CHIP
OUT_SH = D_MODEL // N_CHIP
ROW_CHUNK = 32
GATE_CUT, UP_CUT = 352, 320

ADAM_LR = 0.001
ADAM_B1 = 0.9
ADAM_B2 = 0.999
ADAM_EPS = 1e-08
ADAM_WD = 0.01
ADAM_STEP = 10

SQRT_HALF = 0.7071067811865476
INV_SQRT_2PI = 0.3989422804014327
MESH_AXES = ("x", "y", "c")
MESH = pl.DeviceIdType.MESH
MIB = 2 ** 20


def _vmem():
    return pl.BlockSpec(memory_space=pltpu.VMEM)


def _smem():
    return pl.BlockSpec(memory_space=pltpu.SMEM)


def _hbm():
    return pl.BlockSpec(memory_space=pl.ANY)


def _hbm_shape(shape, dtype):
    return pltpu.HBM(shape, dtype)


def _in_hbm(a):
    return pltpu.with_memory_space_constraint(a, pltpu.HBM)


def _params(vmem_mib=48):
    return pltpu.CompilerParams(dimension_semantics=("arbitrary",), vmem_limit_bytes=vmem_mib * MIB)


def _tile(n, cap):
    if n <= cap:
        return n
    for t in range(cap - cap % 16, 0, -16):
        if n % t == 0:
            return t
    raise ValueError((n, cap))


def _rows(tm, width):
    return pl.BlockSpec((tm, width), lambda i: (i, 0))


def _const2(shape):
    return pl.BlockSpec(shape, lambda i: (0,) * len(shape))


def _ln(x, g, b):
    mu = jnp.mean(x, axis=-1, keepdims=True)
    xc = x - mu
    var = jnp.mean(xc * xc, axis=-1, keepdims=True)
    rstd = lax.rsqrt(var + LN_EPS)
    xhat = xc * rstd
    return xhat * g + b, xhat, rstd


def _ln_bwd(dy, xhat, rstd, g):
    gdy = dy * g
    m1 = jnp.mean(gdy, axis=-1, keepdims=True)
    m2 = jnp.mean(gdy * xhat, axis=-1, keepdims=True)
    return rstd * (gdy - m1 - xhat * m2)


def _colsum(a):
    return jnp.sum(a, axis=0, keepdims=True)


def _gelu_and_grad(x):
    cdf = 0.5 * (1.0 + lax.erf(x * SQRT_HALF))
    return x * cdf, cdf + x * jnp.exp(-0.5 * x * x) * INV_SQRT_2PI


def _dot(a, b):
    return jnp.dot(a, b, preferred_element_type=F32)


def _dot_nt(a, b):
    return lax.dot_general(a, b, (((1,), (1,)), ((), ())), preferred_element_type=F32)


def _dot_tn(a, b):
    return lax.dot_general(a, b, (((0,), (0,)), ((), ())), preferred_element_type=F32)


def _rope(t, tc, t1, t2):
    n = t.shape[1]
    rep = n // 128
    if rep > 1:
        tc, t1, t2 = (jnp.tile(a, (1, rep)) for a in (tc, t1, t2))
    return t * tc + pltpu.roll(t, n - 8, 1) * t1 + pltpu.roll(t, 8, 1) * t2


def _rope_bwd(d, tc, t1, t2):
    n = d.shape[1]
    rep = n // 128
    if rep > 1:
        tc, t1, t2 = (jnp.tile(a, (1, rep)) for a in (tc, t1, t2))
    return d * tc + pltpu.roll(d * t1, 8, 1) + pltpu.roll(d * t2, n - 8, 1)


def _causal_w(w_ref, h):
    t = lax.broadcasted_iota(jnp.int32, (BLK, BLK), 0)
    s = lax.broadcasted_iota(jnp.int32, (BLK, BLK), 1)
    return jnp.where(s <= t, w_ref[h], 0.0)


def _lane_put(vals, width):
    rows = vals[0].shape[0]
    lane = lax.broadcasted_iota(jnp.int32, (rows, width), 1)
    out = jnp.zeros((rows, width), F32)
    for k, v in enumerate(vals):
        out = out + jnp.where(lane == k, v, 0.0)
    return out


def _rope_consts():
    lane = jnp.arange(128) % HEAD_DIM
    rot = lane < ROT_DIM
    pair = (2 * (lane % (ROT_DIM // 2))).astype(F32)
    freq = jnp.where(rot, ROPE_THETA ** (-pair / ROT_DIM), 0.0)
    rows = [freq, rot.astype(F32), 1.0 - rot.astype(F32), (lane < ROT_DIM // 2).astype(F32),
            jnp.logical_and(lane >= ROT_DIM // 2, rot).astype(F32)]
    rows += [jnp.zeros((128,), F32)] * 3
    return jnp.stack(rows).astype(F32)


def _ln_inproj(x, pos_row, g0, b0, w_in, b_in, plan):
    s_len = x.shape[0]
    tm = _tile(s_len, 512)
    m, n = len(plan.operands()), plan.n

    def body(x_ref, pos_ref, g_ref, b_ref, w_ref, bi_ref, rc_ref, *rest):
        q_ref, k_ref, v_ref, su_ref, sv_ref, tc_ref, t1_ref, t2_ref = rest[m:m + 8]
        gather = plan.bind(rest[:m], rest[m + 8:m + 8 + n], rest[m + 8 + n:])
        i = pl.program_id(0)

        @pl.when(i == 0)
        def _():
            gather.start()

        h0, _, _ = _ln(x_ref[...], g_ref[...], b_ref[...])
        proj = _dot_nt(h0.astype(_MXU), w_ref[...]) + bi_ref[...]
        pos = jnp.broadcast_to(pos_ref[...].astype(F32), (128, tm))
        ang = jnp.transpose(pos) * rc_ref[0:1, :]
        cs = jnp.cos(ang)
        sn = jnp.sin(ang)
        tc = cs * rc_ref[1:2, :] + rc_ref[2:3, :]
        t1 = -sn * rc_ref[3:4, :]
        t2 = sn * rc_ref[4:5, :]
        tc_ref[...] = tc
        t1_ref[...] = t1
        t2_ref[...] = t2
        q = _rope(proj[:, 0:ATTN_W], tc, t1, t2) * (HEAD_DIM ** -0.5)
        q_ref[...] = q.astype(_MXU)
        k_ref[...] = _rope(proj[:, ATTN_W:ATTN_W + KV_W], tc, t1, t2).astype(_MXU)
        v_ref[...] = proj[:, ATTN_W + KV_W:ATTN_W + 2 * KV_W].astype(_MXU)
        su_ref[...] = proj[:, ATTN_W + 2 * KV_W:ATTN_W + 2 * KV_W + SGU_W]
        sv_ref[...] = proj[:, ATTN_W + 2 * KV_W + SGU_W:IN_W]

        last = pl.num_programs(0) - 1

        @pl.when(i == jnp.maximum(last - 1, 0))
        def _():
            gather.pass_on()

        @pl.when(i == last)
        def _():
            gather.finish()

    sd = _hbm_shape
    return pl.pallas_call(
        body, name="ln_inproj", grid=(s_len // tm,),
        in_specs=[_rows(tm, D_MODEL), pl.BlockSpec((1, tm), lambda i: (0, i)), _const2((1, D_MODEL)),
                  _const2((1, D_MODEL)), _vmem(),
                  _const2((1, IN_W)), _const2((8, 128))] + plan.in_specs(),
        out_specs=[_rows(tm, ATTN_W), _rows(tm, KV_W), _rows(tm, KV_W), _rows(tm, SGU_W), _rows(tm, SGU_W),
                   _rows(tm, 128), _rows(tm, 128), _rows(tm, 128)] + plan.out_specs(),
        out_shape=[sd((s_len, ATTN_W), _MXU), sd((s_len, KV_W), _MXU), sd((s_len, KV_W), _MXU),
                   sd((s_len, SGU_W), F32), sd((s_len, SGU_W), F32),
                   sd((s_len, 128), F32), sd((s_len, 128), F32), sd((s_len, 128), F32)] + plan.out_shapes(),
        scratch_shapes=plan.scratch(),
        compiler_params=_params(56),
    )(x, pos_row, g0, b0, w_in, b_in, _rope_consts(), *plan.operands())


def _band_mask_t(first_block):
    kj = lax.broadcasted_iota(jnp.int32, (2 * BLK, BLK), 0)
    qi = lax.broadcasted_iota(jnp.int32, (2 * BLK, BLK), 1)
    shut = jnp.where(first_block, 2 * BLK, 0)
    prev_ok = jnp.logical_and(kj < BLK, kj > qi + shut)
    cur_ok = jnp.logical_and(kj >= BLK, (kj - BLK) <= qi)
    return jnp.logical_or(prev_ok, cur_ok)


def _attn_probs_t(kh, qh, sink, allowed_t):
    s = jnp.where(allowed_t, _dot_nt(kh, qh), -1e30)
    m = jnp.maximum(jnp.max(s, axis=0, keepdims=True), sink)
    p = jnp.exp(s - m)
    ps = jnp.exp(sink - m)
    inv = 1.0 / (jnp.sum(p, axis=0, keepdims=True) + ps)
    return p * inv, ps * inv


def _sgu_mix(gv, lg, lb, w_ref, bt_ref):
    vv, vhat, rstd = _ln(gv, lg, lb)
    vvb = vv.astype(_MXU)
    wcs, mixed = [], []
    for h in range(N_GRP):
        wc = _causal_w(w_ref, h).astype(_MXU)
        wcs.append(wc)
        mixed.append(_dot(wc, vvb[:, h * GRP_DIM:(h + 1) * GRP_DIM]) + bt_ref[:, h:h + 1])
    return jnp.concatenate(mixed, axis=1), vhat, rstd, vvb, wcs


def _mixer_fwd(q, k, v, su, sv, sinks, sg, sb, sgu_w, sgu_bt, plan):
    s_len = q.shape[0]
    nb = s_len // BLK
    per = 2 if nb % 2 == 0 else 1
    steps = nb // per
    m, n = len(plan.operands()), plan.n

    def body(q_ref, kc_ref, kp_ref, vc_ref, vp_ref, su_ref, sv_ref, sink_ref, lg_ref, lb_ref, w_ref, bt_ref, *rest):
        mc_ref = rest[m]
        gather = plan.bind(rest[:m], rest[m + 1:m + 1 + n], rest[m + 1 + n:])
        i = pl.program_id(0)

        @pl.when(i == 0)
        def _():
            gather.start()

        @pl.when(i == max(steps - 2, 0))
        def _():
            gather.pass_on()

        @pl.when(i == steps - 1)
        def _():
            gather.finish()

        for s in range(per):
            rows = slice(s * BLK, (s + 1) * BLK)
            before = slice((s - 1) * BLK, s * BLK)
            k_prev = kp_ref[...] if s == 0 else kc_ref[before, :]
            v_prev = vp_ref[...] if s == 0 else vc_ref[before, :]
            allowed_t = _band_mask_t(i == 0 if s == 0 else False)
            kb = jnp.concatenate([k_prev, kc_ref[rows, :]], axis=0)
            vb = jnp.concatenate([v_prev, vc_ref[rows, :]], axis=0)
            qv = q_ref[rows, :]
            outs = []
            allowed_g = jnp.tile(allowed_t, (1, Q_PER_KV))
            for g in range(N_KV):
                heads = range(g * Q_PER_KV, (g + 1) * Q_PER_KV)
                kh = kb[:, g * HEAD_DIM:(g + 1) * HEAD_DIM]
                vh = vb[:, g * HEAD_DIM:(g + 1) * HEAD_DIM]
                q_g = jnp.concatenate([qv[:, h * HEAD_DIM:(h + 1) * HEAD_DIM] for h in heads], axis=0)
                sink_g = jnp.concatenate([jnp.full((1, BLK), sink_ref[h], F32) for h in heads], axis=1)
                probs_t, _ = _attn_probs_t(kh, q_g, sink_g, allowed_g)
                o_g = _dot_tn(probs_t.astype(_MXU), vh)
                outs += [o_g[hh * BLK:(hh + 1) * BLK, :] for hh in range(Q_PER_KV)]
            u = _gelu_and_grad(su_ref[rows, :])[0]
            gv = _gelu_and_grad(sv_ref[rows, :])[0]
            mixed = _sgu_mix(gv, lg_ref[...], lb_ref[...], w_ref, bt_ref)[0]
            mc_ref[rows, :] = jnp.concatenate(outs + [u * mixed], axis=1).astype(_MXU)

    cur = lambda w: pl.BlockSpec((per * BLK, w), lambda i: (i, 0))
    prev = lambda w: pl.BlockSpec((BLK, w), lambda i: (jnp.maximum(per * i - 1, 0), 0))
    return pl.pallas_call(
        body, name="mixer_fwd", grid=(steps,),
        in_specs=[cur(ATTN_W), cur(KV_W), prev(KV_W), cur(KV_W), prev(KV_W), cur(SGU_W), cur(SGU_W), _smem(),
                  _const2((1, SGU_W)), _const2((1, SGU_W)), _const2((N_GRP, BLK, BLK)), _const2((BLK, N_GRP))]
        + plan.in_specs(),
        out_specs=[cur(D_MODEL)] + plan.out_specs(),
        out_shape=[_hbm_shape((s_len, D_MODEL), _MXU)] + plan.out_shapes(),
        scratch_shapes=plan.scratch(),
        compiler_params=_params(56),
    )(q, k, k, v, v, su, sv, sinks, sg, sb, sgu_w, sgu_bt, *plan.operands())


def _outproj(mc, w_out, b_out, x, g0, b0, plan):
    s_len = x.shape[0]
    tm = _tile(s_len, 512)
    m, n = len(plan.operands()), plan.n

    def body(mc_ref, w_ref, bo_ref, x_ref, g_ref, b_ref, *rest):
        r1_ref = rest[m]
        gather = plan.bind(rest[:m], rest[m + 1:m + 1 + n], rest[m + 1 + n:])
        i = pl.program_id(0)

        @pl.when(i == 0)
        def _():
            gather.start()

        h0, _, _ = _ln(x_ref[...], g_ref[...], b_ref[...])
        r1_ref[...] = ALPHA * h0 + (_dot(mc_ref[...], w_ref[...]) + bo_ref[...])

        last = pl.num_programs(0) - 1

        @pl.when(i == jnp.maximum(last - 1, 0))
        def _():
            gather.pass_on()

        @pl.when(i == last)
        def _():
            gather.finish()

    return pl.pallas_call(
        body, name="outproj", grid=(s_len // tm,),
        in_specs=[_rows(tm, D_MODEL), _vmem(), _const2((1, D_MODEL)), _rows(tm, D_MODEL),
                  _const2((1, D_MODEL)), _const2((1, D_MODEL))] + plan.in_specs(),
        out_specs=[_rows(tm, D_MODEL)] + plan.out_specs(),
        out_shape=[_hbm_shape((s_len, D_MODEL), F32)] + plan.out_shapes(),
        scratch_shapes=plan.scratch(),
        compiler_params=_params(40),
    )(mc, w_out, b_out, x, g0, b0, *plan.operands())


def _ffn_spec(tm):
    return pl.BlockSpec((N_CHIP, tm, FF_SH), lambda i: (0, i, 0))


def _ffn_up(r1, g1, b1, wg, wu, plan):
    s_len = r1.shape[0]
    tm = _tile(s_len, 512)
    m, n = len(plan.operands()), plan.n

    def body(r1_ref, g_ref, b_ref, wg_ref, wu_ref, *rest):
        a_ref, p_ref, q_ref, h1_ref = rest[m:m + 4]
        gather = plan.bind(rest[:m], rest[m + 4:m + 4 + n], rest[m + 4 + n:])
        i = pl.program_id(0)

        @pl.when(i == 0)
        def _():
            gather.start()

        h1, _, _ = _ln(r1_ref[...], g_ref[...], b_ref[...])
        h1_ref[...] = h1
        h1b = h1.astype(_MXU)
        for j in range(N_CHIP):
            g = _dot_nt(h1b, wg_ref[j])
            u = _dot_nt(h1b, wu_ref[j])
            silu, sg = _silu_parts(g)
            a_ref[j] = (silu * u).astype(_MXU)
            p_ref[j] = silu.astype(_ACT)
            q_ref[j] = (u * (sg * (1.0 + g * (1.0 - sg)))).astype(_ACT)

        last = pl.num_programs(0) - 1

        @pl.when(i == jnp.maximum(last - 1, 0))
        def _():
            gather.pass_on()

        @pl.when(i == last)
        def _():
            gather.finish()

    sd = _hbm_shape((N_CHIP, s_len, FF_SH), _ACT)
    return pl.pallas_call(
        body, name="ffn_up", grid=(s_len // tm,),
        in_specs=[_rows(tm, D_MODEL), _const2((1, D_MODEL)), _const2((1, D_MODEL)), _vmem(), _vmem()] + plan.in_specs(),
        out_specs=[_ffn_spec(tm)] * 3 + [_rows(tm, D_MODEL)] + plan.out_specs(),
        out_shape=[_hbm_shape((N_CHIP, s_len, FF_SH), _MXU), sd, sd, _hbm_shape((s_len, D_MODEL), F32)]
        + plan.out_shapes(),
        scratch_shapes=plan.scratch(),
        compiler_params=_params(56),
    )(r1, g1, b1, wg, wu, *plan.operands())


def _silu_parts(g):
    sg = 1.0 / (1.0 + jnp.exp(-g))
    return g * sg, sg


def _ffn_down_loss(act, wd, h1, g2, b2, target):
    s_len = h1.shape[0]
    tm = _tile(s_len, 512)

    parts = 2 if tm % 32 == 0 else 1
    sub = tm // parts

    def body(a_ref, wd_ref, h1_ref, g2_ref, b2_ref, t_ref, dr2_ref, loss_ref, dg2_ref, db2_ref):
        i = pl.program_id(0)

        @pl.when(i == 0)
        def _():
            loss_ref[...] = jnp.zeros_like(loss_ref)
            dg2_ref[...] = jnp.zeros_like(dg2_ref)
            db2_ref[...] = jnp.zeros_like(db2_ref)

        for part in range(parts):
            rows = slice(part * sub, (part + 1) * sub)
            f = jnp.zeros((sub, D_MODEL), F32)
            for j in range(N_CHIP):
                f = f + _dot(a_ref[j, rows, :], wd_ref[j])
            h2, r2hat, rstd2 = _ln(ALPHA * h1_ref[rows, :] + f, g2_ref[...], b2_ref[...])
            diff = h2 - t_ref[rows, :]
            dh2 = diff * (1.0 / D_MODEL)
            loss_ref[...] += _colsum(diff * diff)
            dg2_ref[...] += _colsum(dh2 * r2hat)
            db2_ref[...] += _colsum(dh2)
            dr2_ref[rows, :] = _ln_bwd(dh2, r2hat, rstd2, g2_ref[...])

    vec = _hbm_shape((1, D_MODEL), F32)
    c = _const2((1, D_MODEL))
    return pl.pallas_call(
        body, name="ffn_down_loss", grid=(s_len // tm,),
        in_specs=[_ffn_spec(tm), _vmem(), _rows(tm, D_MODEL), c, c, _rows(tm, D_MODEL)],
        out_specs=[_rows(tm, D_MODEL), c, c, c],
        out_shape=[_hbm_shape((s_len, D_MODEL), F32), vec, vec, vec],
        compiler_params=_params(48),
    )(act, wd, h1, g2, b2, target)


def _ffn_bwd_a(dr2, act, p_act, q_act, wd):
    s_len = dr2.shape[0]
    tm = _tile(s_len, 512)

    def body(dr2_ref, a_ref, p_ref, q_ref, wd_ref, dg_ref, du_ref, wire_ref, own_ref,
             dwd_ref, land_ref, send_sem, recv_sem):
        i = pl.program_id(0)

        @pl.when(i == 0)
        def _():
            dwd_ref[...] = jnp.zeros_like(dwd_ref)

        dfb = dr2_ref[...].astype(_MXU)
        for j in range(N_CHIP):
            da = _dot_nt(dfb, wd_ref[j])
            dg_ref[j] = (da * q_ref[j].astype(F32)).astype(_MXU)
            du_ref[j] = (da * p_ref[j].astype(F32)).astype(_MXU)
            dwd_ref[j * FF_SH:(j + 1) * FF_SH, :] += _dot_tn(a_ref[j], dfb)

        @pl.when(i == pl.num_programs(0) - 1)
        def _():
            _pair_reduce(dwd_ref, wire_ref, own_ref, land_ref, send_sem, recv_sem)

    sd = _hbm_shape((N_CHIP, s_len, FF_SH), _MXU)
    half = (N_CHIP, FF_SH // 2, D_MODEL)
    return pl.pallas_call(
        body, name="ffn_bwd_a", grid=(s_len // tm,),
        in_specs=[_rows(tm, D_MODEL), _ffn_spec(tm), _ffn_spec(tm), _ffn_spec(tm), _vmem()],
        out_specs=[_ffn_spec(tm), _ffn_spec(tm), _vmem(), _vmem()],
        out_shape=[sd, sd] + _pair_out_shapes(half),
        scratch_shapes=_pair_scratch((D_FF, D_MODEL), half),
        compiler_params=_params(61),
    )(dr2, act, p_act, q_act, wd)


def _ffn_bwd_g(dr2, dg, r1, g1, b1, wg, prev_wire):
    s_len = dr2.shape[0]
    tm = _tile(s_len, 512)

    def body(dr2_ref, dg_ref, r1_ref, g1_ref, b1_ref, wg_ref, pw_ref, dh1_ref, wire_ref, own_ref, pl_ref,
             dwg_ref, land_ref, send_sem, recv_sem, xl_ref, x_send, x_recv, x_flush):
        i = pl.program_id(0)
        exchange = _ChipExchange(pw_ref, xl_ref, x_send, x_recv)

        @pl.when(i == 0)
        def _():
            exchange.start()
            dwg_ref[...] = jnp.zeros_like(dwg_ref)

        h1, _, _ = _ln(r1_ref[...], g1_ref[...], b1_ref[...])
        h1b = h1.astype(_MXU)
        dh1 = ALPHA * dr2_ref[...]
        for j in range(N_CHIP):
            dgj = dg_ref[j]
            dh1 = dh1 + _dot(dgj, wg_ref[j])
            dwg_ref[j * FF_SH:(j + 1) * FF_SH, :] += _dot_tn(dgj, h1b)
        dh1_ref[...] = dh1

        @pl.when(i == pl.num_programs(0) - 1)
        def _():
            _pair_reduce(dwg_ref, wire_ref, own_ref, land_ref, send_sem, recv_sem)
            exchange.finish_to(pl_ref, x_flush)

    c = _const2((1, D_MODEL))
    half = (N_CHIP, FF_SH // 2, D_MODEL)
    return pl.pallas_call(
        body, name="ffn_bwd_g", grid=(s_len // tm,),
        in_specs=[_rows(tm, D_MODEL), _ffn_spec(tm), _rows(tm, D_MODEL), c, c, _vmem(), _vmem()],
        out_specs=[_rows(tm, D_MODEL), _vmem(), _vmem(), _hbm()],
        out_shape=[_hbm_shape((s_len, D_MODEL), F32)] + _pair_out_shapes(half) + [_ChipExchange.land_shape(prev_wire)],
        scratch_shapes=_pair_scratch((D_FF, D_MODEL), half) + _ChipExchange.scratch(prev_wire),
        compiler_params=_params(58),
    )(dr2, dg, r1, g1, b1, wg, prev_wire)


def _ffn_bwd_u(dh1a, du, r1, g1, b1, wu, prev_wire):
    s_len = dh1a.shape[0]
    tm = _tile(s_len, 512)

    def body(dh1_ref, du_ref, r1_ref, g1_ref, b1_ref, wu_ref, pw_ref,
             dr1_ref, wire_ref, own_ref, dg1_ref, db1_ref, pl_ref,
             dwu_ref, land_ref, send_sem, recv_sem, xl_ref, x_send, x_recv, x_flush):
        i = pl.program_id(0)
        exchange = _ChipExchange(pw_ref, xl_ref, x_send, x_recv)

        @pl.when(i == 0)
        def _():
            exchange.start()
            dwu_ref[...] = jnp.zeros_like(dwu_ref)
            dg1_ref[...] = jnp.zeros_like(dg1_ref)
            db1_ref[...] = jnp.zeros_like(db1_ref)

        h1, r1hat, rstd1 = _ln(r1_ref[...], g1_ref[...], b1_ref[...])
        h1b = h1.astype(_MXU)
        dh1 = dh1_ref[...]
        for j in range(N_CHIP):
            duj = du_ref[j]
            dh1 = dh1 + _dot(duj, wu_ref[j])
            dwu_ref[j * FF_SH:(j + 1) * FF_SH, :] += _dot_tn(duj, h1b)
        dg1_ref[...] += _colsum(dh1 * r1hat)
        db1_ref[...] += _colsum(dh1)
        dr1_ref[...] = _ln_bwd(dh1, r1hat, rstd1, g1_ref[...])

        @pl.when(i == pl.num_programs(0) - 1)
        def _():
            _pair_reduce(dwu_ref, wire_ref, own_ref, land_ref, send_sem, recv_sem)
            exchange.finish_to(pl_ref, x_flush)

    vec = _hbm_shape((1, D_MODEL), F32)
    c = _const2((1, D_MODEL))
    half = (N_CHIP, FF_SH // 2, D_MODEL)
    return pl.pallas_call(
        body, name="ffn_bwd_u", grid=(s_len // tm,),
        in_specs=[_rows(tm, D_MODEL), _ffn_spec(tm), _rows(tm, D_MODEL), c, c, _vmem(), _vmem()],
        out_specs=[_rows(tm, D_MODEL), _vmem(), _vmem(), c, c, _hbm()],
        out_shape=[_hbm_shape((s_len, D_MODEL), F32)] + _pair_out_shapes(half)
        + [vec, vec, _ChipExchange.land_shape(prev_wire)],
        scratch_shapes=_pair_scratch((D_FF, D_MODEL), half) + _ChipExchange.scratch(prev_wire),
        compiler_params=_params(58),
    )(dh1a, du, r1, g1, b1, wu, prev_wire)


def _outproj_bwd(dr1, mc, w_out):
    s_len = dr1.shape[0]
    tm = _tile(s_len, 512)

    def body(dr1_ref, mc_ref, w_ref, dmc_ref, wire_ref, own_ref, db_ref, dw_ref, land_ref, send_sem, recv_sem):
        i = pl.program_id(0)

        @pl.when(i == 0)
        def _():
            dw_ref[...] = jnp.zeros_like(dw_ref)
            db_ref[...] = jnp.zeros_like(db_ref)

        d = dr1_ref[...]
        db_ref[...] += _colsum(d)
        db16 = d.astype(_MXU)
        dmc_ref[...] = _dot_nt(db16, w_ref[...])
        dw_ref[...] += _dot_tn(mc_ref[...], db16)

        @pl.when(i == pl.num_programs(0) - 1)
        def _():
            _pair_reduce(dw_ref, wire_ref, own_ref, land_ref, send_sem, recv_sem)

    half = (N_CHIP, OUT_SH // 2, D_MODEL)
    return pl.pallas_call(
        body, name="outproj_bwd", grid=(s_len // tm,),
        in_specs=[_rows(tm, D_MODEL), _rows(tm, D_MODEL), _vmem()],
        out_specs=[_rows(tm, D_MODEL), _vmem(), _vmem(), _const2((1, D_MODEL))],
        out_shape=[_hbm_shape((s_len, D_MODEL), F32)] + _pair_out_shapes(half) + [_hbm_shape((1, D_MODEL), F32)],
        scratch_shapes=_pair_scratch((D_MODEL, D_MODEL), half),
        compiler_params=_params(48),
    )(dr1, mc, w_out)


def _mixer_bwd(q, k, v, su, sv, dmc, tc, t1, t2, sinks, sg, sb, sgu_w, sgu_bt, prev_wires):
    s_len = q.shape[0]
    nb = s_len // BLK
    per = next(p for p in (4, 2, 1) if nb % p == 0)
    steps = nb // per

    def body(q_ref, kc_ref, kp_ref, vc_ref, vp_ref, su_ref, sv_ref, dmc_ref,
             tc_ref, t1_ref, t2_ref, tcp_ref, t1p_ref, t2p_ref,
             sink_ref, lg_ref, lb_ref, w_ref, bt_ref, pw0_ref, pw1_ref,
             dq_ref, dkv_ref, dsuv_ref, dbq_ref, dbkv_ref, dbsuv_ref,
             dsink_ref, dlg_ref, dlb_ref, dw_ref, dbt_ref, pl0_ref, pl1_ref, carry_ref,
             xl0_ref, x0_send, x0_recv, x0_flush, xl1_ref, x1_send, x1_recv, x1_flush):
        i = pl.program_id(0)
        exchanges = [(_ChipExchange(pw0_ref, xl0_ref, x0_send, x0_recv), pl0_ref, x0_flush),
                     (_ChipExchange(pw1_ref, xl1_ref, x1_send, x1_recv), pl1_ref, x1_flush)]

        @pl.when(i == 0)
        def _():
            for exchange, _, _ in exchanges:
                exchange.start()

        @pl.when(i == 0)
        def _():
            for r in (dbq_ref, dbkv_ref, dbsuv_ref, dsink_ref, dlg_ref, dlb_ref, dw_ref, dbt_ref, carry_ref):
                r[...] = jnp.zeros_like(r)

        def emit_kv(fin, t):
            if t == 0:
                tables = (tcp_ref[...], t1p_ref[...], t2p_ref[...])
            else:
                before = slice((t - 1) * BLK, t * BLK)
                tables = (tc_ref[before, :], t1_ref[before, :], t2_ref[before, :])
            dk = _rope_bwd(fin[:, 0:KV_W], *tables)
            out = jnp.concatenate([dk, fin[:, KV_W:2 * KV_W]], axis=1)
            dkv_ref[t * BLK:(t + 1) * BLK, :] = out.astype(_MXU)
            dbkv_ref[...] += _colsum(out)

        def one_block(s):
            rows = slice(s * BLK, (s + 1) * BLK)
            before = slice((s - 1) * BLK, s * BLK)
            k_prev = kp_ref[...] if s == 0 else kc_ref[before, :]
            v_prev = vp_ref[...] if s == 0 else vc_ref[before, :]
            allowed_t = _band_mask_t(i == 0 if s == 0 else False)
            kb = jnp.concatenate([k_prev, kc_ref[rows, :]], axis=0)
            vb = jnp.concatenate([v_prev, vc_ref[rows, :]], axis=0)
            qv = q_ref[rows, :]
            dmc = dmc_ref[rows, :]
            dqs, dks, dvs, dsinks = [], [], [], []
            allowed_g = jnp.tile(allowed_t, (1, Q_PER_KV))
            for g in range(N_KV):
                heads = range(g * Q_PER_KV, (g + 1) * Q_PER_KV)
                kh = kb[:, g * HEAD_DIM:(g + 1) * HEAD_DIM]
                vh = vb[:, g * HEAD_DIM:(g + 1) * HEAD_DIM]
                q_g = jnp.concatenate([qv[:, h * HEAD_DIM:(h + 1) * HEAD_DIM] for h in heads], axis=0)
                do_g = jnp.concatenate([dmc[:, h * HEAD_DIM:(h + 1) * HEAD_DIM] for h in heads], axis=0).astype(_MXU)
                sink_g = jnp.concatenate([jnp.full((1, BLK), sink_ref[h], F32) for h in heads], axis=1)
                probs_t, psink = _attn_probs_t(kh, q_g, sink_g, allowed_g)
                dvs.append(_dot(probs_t.astype(_MXU), do_g))
                dp_t = _dot_nt(vh, do_g)
                rd = jnp.sum(probs_t * dp_t, axis=0, keepdims=True)
                ds_t = (probs_t * (dp_t - rd)).astype(_MXU)
                ps_rd = psink * rd
                for hh in range(Q_PER_KV):
                    dsinks.append(-jnp.sum(ps_rd[:, hh * BLK:(hh + 1) * BLK], axis=1, keepdims=True))
                dq_g = _dot_tn(ds_t, kh)
                dqs += [dq_g[hh * BLK:(hh + 1) * BLK, :] for hh in range(Q_PER_KV)]
                dks.append(_dot(ds_t, q_g))
            dq = _rope_bwd(jnp.concatenate(dqs, axis=1) * (HEAD_DIM ** -0.5),
                           tc_ref[rows, :], t1_ref[rows, :], t2_ref[rows, :])
            dq_ref[rows, :] = dq.astype(_MXU)
            dbq_ref[...] += _colsum(dq)
            dsink_ref[...] += _lane_put(dsinks, 128)
            contrib = jnp.concatenate(dks + dvs, axis=1)

            lg = lg_ref[...]
            u, du_dsu = _gelu_and_grad(su_ref[rows, :])
            gv, dgv_dsv = _gelu_and_grad(sv_ref[rows, :])
            mixed, vhat, rstd, vvb, wcs = _sgu_mix(gv, lg, lb_ref[...], w_ref, bt_ref)
            dsgu = dmc[:, ATTN_W:D_MODEL]
            dsu = dsgu * mixed * du_dsu
            dmixed = dsgu * u
            tri_t = lax.broadcasted_iota(jnp.int32, (BLK, BLK), 0)
            tri_s = lax.broadcasted_iota(jnp.int32, (BLK, BLK), 1)
            dvv, dbs = [], []
            for h in range(N_GRP):
                dm = dmixed[:, h * GRP_DIM:(h + 1) * GRP_DIM]
                dmb = dm.astype(_MXU)
                dbs.append(jnp.sum(dm, axis=1, keepdims=True))
                dw_ref[h] += jnp.where(tri_s <= tri_t, _dot_nt(dmb, vvb[:, h * GRP_DIM:(h + 1) * GRP_DIM]), 0.0)
                dvv.append(_dot_tn(wcs[h], dmb))
            dvv = jnp.concatenate(dvv, axis=1)
            dbt_ref[...] += _lane_put(dbs, 128)
            dlg_ref[...] += _colsum(dvv * vhat)
            dlb_ref[...] += _colsum(dvv)
            dsv = _ln_bwd(dvv, vhat, rstd, lg) * dgv_dsv
            dsuv = jnp.concatenate([dsu, dsv], axis=1)
            dsuv_ref[rows, :] = dsuv.astype(_MXU)
            dbsuv_ref[...] += _colsum(dsuv)
            return contrib

        @pl.when(i < steps)
        def _():
            contribs = [one_block(s) for s in range(per)]
            for t in range(per):
                top = carry_ref[...] if t == 0 else contribs[t - 1][BLK:2 * BLK, :]
                emit_kv(top + contribs[t][0:BLK, :], t)
            carry_ref[...] = contribs[per - 1][BLK:2 * BLK, :]

        @pl.when(i == steps)
        def _():
            emit_kv(carry_ref[...], 0)
            if per > 1:
                dkv_ref[BLK:per * BLK, :] = jnp.zeros(((per - 1) * BLK, 2 * KV_W), _MXU)
            for exchange, landed, flush_sem in exchanges:
                exchange.finish_to(landed, flush_sem)

    last = steps - 1
    cur = lambda w: pl.BlockSpec((per * BLK, w), lambda i: (jnp.minimum(i, last), 0))
    prev = lambda w: pl.BlockSpec((BLK, w), lambda i: (jnp.clip(per * i - 1, 0, nb - 1), 0))
    shifted = pl.BlockSpec((per * BLK, 2 * KV_W), lambda i: (i, 0))
    sd = _hbm_shape
    return pl.pallas_call(
        body, name="mixer_bwd", grid=(steps + 1,),
        in_specs=[cur(ATTN_W), cur(KV_W), prev(KV_W), cur(KV_W), prev(KV_W), cur(SGU_W), cur(SGU_W), cur(D_MODEL),
                  cur(128), cur(128), cur(128), prev(128), prev(128), prev(128),
                  _smem(), _const2((1, SGU_W)), _const2((1, SGU_W)), _const2((N_GRP, BLK, BLK)), _const2((BLK, N_GRP)),
                  _vmem(), _vmem()],
        out_specs=[cur(ATTN_W), shifted, cur(2 * SGU_W),
                   _const2((1, ATTN_W)), _const2((1, 2 * KV_W)), _const2((1, 2 * SGU_W)),
                   _const2((1, 128)), _const2((1, SGU_W)), _const2((1, SGU_W)),
                   _const2((N_GRP, BLK, BLK)), _const2((BLK, 128)), _hbm(), _hbm()],
        out_shape=[sd((s_len, ATTN_W), _MXU), sd((s_len + per * BLK, 2 * KV_W), _MXU), sd((s_len, 2 * SGU_W), _MXU),
                   sd((1, ATTN_W), F32), sd((1, 2 * KV_W), F32), sd((1, 2 * SGU_W), F32),
                   sd((1, 128), F32), sd((1, SGU_W), F32), sd((1, SGU_W), F32),
                   sd((N_GRP, BLK, BLK), F32), sd((BLK, 128), F32)]
        + [_ChipExchange.land_shape(w) for w in prev_wires],
        scratch_shapes=[pltpu.VMEM((BLK, 2 * KV_W), F32)] + _ChipExchange.scratch(prev_wires[0])
        + _ChipExchange.scratch(prev_wires[1]),
        compiler_params=_params(40),
    )(q, k, k, v, v, su, sv, dmc, tc, t1, t2, tc, t1, t2, sinks, sg, sb, sgu_w, sgu_bt, *prev_wires)


def _inproj_bwd(dq, dkv_late, dsuv, dr1, x, g0, b0, w_in):
    s_len = x.shape[0]
    tm = _tile(s_len, 512)
    assert tm % BLK == 0
    per = tm // BLK
    cuts = ((0, ATTN_W), (ATTN_W, ATTN_W + 2 * KV_W), (ATTN_W + 2 * KV_W, IN_W))

    def body(dq_ref, *rest):
        dkv_refs = rest[:per]
        dsuv_ref, dr1_ref, x_ref, g_ref, b_ref, w_ref, dx_ref, dw_ref, dg_ref, db_ref = rest[per:]
        i = pl.program_id(0)

        @pl.when(i == 0)
        def _():
            dw_ref[...] = jnp.zeros_like(dw_ref)
            dg_ref[...] = jnp.zeros_like(dg_ref)
            db_ref[...] = jnp.zeros_like(db_ref)

        h0, xhat, rstd = _ln(x_ref[...], g_ref[...], b_ref[...])
        h0b = h0.astype(_MXU)
        dh0 = ALPHA * dr1_ref[...]
        dkv = jnp.concatenate([r[...] for r in dkv_refs], axis=0)
        for (lo, hi), d in zip(cuts, (dq_ref[...], dkv, dsuv_ref[...])):
            dh0 = dh0 + _dot(d, w_ref[lo:hi, :])
            dw_ref[lo:hi, :] += _dot_tn(d, h0b)
        dg_ref[...] += _colsum(dh0 * xhat)
        db_ref[...] += _colsum(dh0)
        dx_ref[...] = _ln_bwd(dh0, xhat, rstd, g_ref[...])

    vec = _hbm_shape((1, D_MODEL), F32)
    c = _const2((1, D_MODEL))
    return pl.pallas_call(
        body, name="inproj_bwd", grid=(s_len // tm,),
        in_specs=[_rows(tm, ATTN_W)]
        + [pl.BlockSpec((BLK, 2 * KV_W), lambda i, b=b: (i * per + b + 1, 0)) for b in range(per)]
        + [_rows(tm, 2 * SGU_W), _rows(tm, D_MODEL), _rows(tm, D_MODEL), c, c, _vmem()],
        out_specs=[_rows(tm, D_MODEL), _vmem(), c, c],
        out_shape=[_hbm_shape((s_len, D_MODEL), F32), jax.ShapeDtypeStruct((IN_W, D_MODEL), F32), vec, vec],
        compiler_params=_params(48),
    )(dq, *[dkv_late] * per, dsuv, dr1, x, g0, b0, w_in)


def _place():
    x, y, c = (lax.axis_index(a) for a in MESH_AXES)
    chips = [(1 - x, y), (x, 1 - y), (1 - x, 1 - y)]
    return x, y, c, chips


class _Gather:
    def __init__(self, ins, outs, send_sems, recv_sems, spans=None):
        self.ins, self.outs, self.send_sems, self.recv_sems = ins, outs, send_sems, recv_sems
        self.n = len(ins)
        self.spans = spans or [(0, r.shape[0]) for r in ins]
        self.halves = [(hi - lo) // 2 for lo, hi in self.spans]

    def _copy(self, k, t, slot, half, to):
        rows = pl.ds(pl.multiple_of(self.spans[t][0] + half * self.halves[t], 16), self.halves[t])
        piece = self.outs[t].at[slot, rows, :]
        return pltpu.make_async_remote_copy(src_ref=piece, dst_ref=piece, send_sem=self.send_sems.at[k],
                                            recv_sem=self.recv_sems.at[k], device_id=to, device_id_type=MESH)

    def _chip_copy(self, t, d, slot):
        x, y, c, chips = _place()
        return self._copy(3 * t + d, t, slot, c, (chips[d][0], chips[d][1], c))

    def _pass_copy(self, t, d, half):
        x, y, c, chips = _place()
        return self._copy(3 * self.n + 3 * t + d, t, 2 * chips[d][0] + chips[d][1], half, (x, y, 1 - c))

    def start(self):
        x, y, c, chips = _place()
        me = 2 * x + y
        for t in range(self.n):
            lo, hi = self.spans[t]
            self.outs[t][me, lo:hi, :] = self.ins[t][lo:hi, :].astype(_WIRE)
        for t in range(self.n):
            for d in range(3):
                self._chip_copy(t, d, me).start()

    def pass_on(self):
        x, y, c, chips = _place()
        for t in range(self.n):
            for d in range(3):
                self._chip_copy(t, d, 2 * chips[d][0] + chips[d][1]).wait_recv()
                self._pass_copy(t, d, c).start()

    def finish(self):
        x, y, c, chips = _place()
        me = 2 * x + y
        for t in range(self.n):
            for d in range(3):
                self._pass_copy(t, d, 1 - c).wait_recv()
        for t in range(self.n):
            for d in range(3):
                self._chip_copy(t, d, me).wait_send()
                self._pass_copy(t, d, c).wait_send()

    @staticmethod
    def out_shapes(shards, make=jax.ShapeDtypeStruct):
        return [make((N_CHIP,) + s.shape, _WIRE) for s in shards]

    @staticmethod
    def sems(n):
        return [pltpu.SemaphoreType.DMA((6 * n,)), pltpu.SemaphoreType.DMA((6 * n,))]


class _GatherPlan:
    def __init__(self, pieces):
        self.shards = [p[0] for p in pieces]
        self.spans = [p[1] for p in pieces]
        self.earlier = [p[2] for p in pieces]
        self.n = len(pieces)
        self.carried = [t for t in range(self.n) if self.earlier[t] is not None]

    def operands(self):
        return self.shards + [self.earlier[t] for t in self.carried]

    def in_specs(self):
        return [_vmem()] * self.n + [_hbm()] * len(self.carried)

    def out_specs(self):
        return [_hbm()] * self.n

    def out_shapes(self):
        return _Gather.out_shapes(self.shards, _hbm_shape)

    def scratch(self):
        return ([pltpu.VMEM((N_CHIP,) + s.shape, _WIRE) for s in self.shards] + _Gather.sems(self.n)
                + [pltpu.SemaphoreType.DMA((self.n,)), pltpu.SemaphoreType.DMA((max(len(self.carried), 1),))])

    def bind(self, in_refs, out_refs, scratch_refs):
        plan = self
        shard_refs, earlier_refs = in_refs[:self.n], in_refs[self.n:]
        bufs = scratch_refs[:self.n]
        send_sems, recv_sems, flush_sems, carry_sems = scratch_refs[self.n:self.n + 4]
        gather = _Gather(shard_refs, bufs, send_sems, recv_sems, self.spans)

        def carry_copy(k):
            t = plan.carried[k]
            lo = plan.spans[t][0]
            return pltpu.make_async_copy(earlier_refs[k].at[:, 0:lo, :], bufs[t].at[:, 0:lo, :], carry_sems.at[k])

        class Bound:
            @staticmethod
            def start():
                for k in range(len(plan.carried)):
                    carry_copy(k).start()
                gather.start()

            @staticmethod
            def pass_on():
                gather.pass_on()

            @staticmethod
            def finish():
                gather.finish()
                for k in range(len(plan.carried)):
                    carry_copy(k).wait()
                _flush([bufs[t].at[:, 0:plan.spans[t][1], :] for t in range(plan.n)],
                       [out_refs[t].at[:, 0:plan.spans[t][1], :] for t in range(plan.n)], flush_sems)

        return Bound


def _flush(bufs, hbm_outs, sems):
    copies = [pltpu.make_async_copy(b, o, sems.at[k]) for k, (b, o) in enumerate(zip(bufs, hbm_outs))]
    for cp in copies:
        cp.start()
    for cp in copies:
        cp.wait()


def _gather_weights(shards):
    n = len(shards)

    def body(*refs):
        gather = _Gather(refs[:n], refs[n:2 * n], refs[2 * n], refs[2 * n + 1])
        gather.start()
        gather.pass_on()
        gather.finish()

    return pl.pallas_call(
        body, name="gather_weights",
        in_specs=[_vmem()] * n, out_specs=[_vmem()] * n,
        out_shape=_Gather.out_shapes(shards), scratch_shapes=_Gather.sems(n),
        compiler_params=pltpu.CompilerParams(vmem_limit_bytes=32 * MIB),
    )(*shards)


class _ChipExchange:
    def __init__(self, wire_ref, land_ref, send_sems, recv_sems):
        self.wire, self.land, self.send_sems, self.recv_sems = wire_ref, land_ref, send_sems, recv_sems

    def _copy(self, d):
        x, y, c, chips = _place()
        return pltpu.make_async_remote_copy(
            src_ref=self.wire.at[2 * chips[d][0] + chips[d][1]], dst_ref=self.land.at[d],
            send_sem=self.send_sems.at[d], recv_sem=self.recv_sems.at[d],
            device_id=(chips[d][0], chips[d][1], c), device_id_type=MESH)

    def start(self):
        for d in range(3):
            self._copy(d).start()

    def wait_recv(self):
        for d in range(3):
            self._copy(d).wait_recv()

    def wait_send(self):
        for d in range(3):
            self._copy(d).wait_send()

    def finish_to(self, hbm_out, flush_sem):
        self.wait_recv()
        _flush([self.land], [hbm_out], flush_sem)
        self.wait_send()

    @staticmethod
    def land_shape(wire):
        return _hbm_shape((3,) + wire.shape[1:], wire.dtype)

    @staticmethod
    def sems():
        return [pltpu.SemaphoreType.DMA((3,)), pltpu.SemaphoreType.DMA((3,))]

    @staticmethod
    def scratch(wire):
        return ([pltpu.VMEM((3,) + wire.shape[1:], wire.dtype)] + _ChipExchange.sems() + [pltpu.SemaphoreType.DMA((1,))])


def _pair_out_shapes(half_shape):
    return [jax.ShapeDtypeStruct(half_shape, _WIRE), jax.ShapeDtypeStruct(half_shape[1:], F32)]


def _pair_scratch(acc_shape, half_shape):
    return [pltpu.VMEM(acc_shape, F32), pltpu.VMEM(half_shape, _WIRE),
            pltpu.SemaphoreType.DMA((N_CHIP,)), pltpu.SemaphoreType.DMA((N_CHIP,))]


def _pair_reduce(acc_ref, wire_ref, own_ref, land_ref, send_sems, recv_sems):
    rh = land_ref.shape[1]
    x, y, c, _ = _place()
    me = 2 * x + y
    copies = []
    for j in range(N_CHIP):
        def cast(r, carry, j=j):
            dst = pl.ds(pl.multiple_of(r * ROW_CHUNK, ROW_CHUNK), ROW_CHUNK)
            src = pl.ds(pl.multiple_of((2 * j + 1 - c) * rh + r * ROW_CHUNK, 8), ROW_CHUNK)
            wire_ref[j, dst, :] = acc_ref[src, :].astype(_WIRE)
            return carry

        lax.fori_loop(0, rh // ROW_CHUNK, cast, 0)
        cp = pltpu.make_async_remote_copy(src_ref=wire_ref.at[j], dst_ref=land_ref.at[j], send_sem=send_sems.at[j],
                                          recv_sem=recv_sems.at[j], device_id=(x, y, 1 - c), device_id_type=MESH)
        cp.start()
        copies.append(cp)
    for j in range(N_CHIP):
        copies[j].wait()

        def chunk(r, carry, j=j):
            theirs = pl.ds(pl.multiple_of(r * ROW_CHUNK, ROW_CHUNK), ROW_CHUNK)
            mine = pl.ds(pl.multiple_of((2 * j + c) * rh + r * ROW_CHUNK, 8), ROW_CHUNK)
            wire_ref[j, theirs, :] = (acc_ref[mine, :] + land_ref[j, theirs, :].astype(F32)).astype(_WIRE)
            return carry

        lax.fori_loop(0, rh // ROW_CHUNK, chunk, 0)

    def own_chunk(r, carry):
        theirs = pl.ds(pl.multiple_of(r * ROW_CHUNK, ROW_CHUNK), ROW_CHUNK)
        mine = pl.ds(pl.multiple_of((2 * me + c) * rh + r * ROW_CHUNK, 8), ROW_CHUNK)
        own_ref[theirs, :] = acc_ref[mine, :] + land_ref[me, theirs, :].astype(F32)
        return carry

    lax.fori_loop(0, rh // ROW_CHUNK, own_chunk, 0)


def _grad_finish(last_acc, lands, owns, small):
    n = len(owns) + 1
    halves = [last_acc.shape[0] // (2 * N_CHIP)] + [w.shape[1] for w in lands]
    widths = [last_acc.shape[1]] + [a.shape[1] for a in owns]
    small_body, small_scratch = _small_allreduce_parts()
    ns = len(small)

    def body(*refs):
        acc0, land, own = refs[0], (None,) + refs[1:n], (None,) + refs[n:2 * n - 1]
        refs = refs[2 * n - 1:]
        small_in, g_out, small_out = refs[:ns], refs[ns:ns + n], refs[ns + n:ns + n + 2]
        refs = refs[ns + n + 2:]
        pland0, wire0, land0, own0 = refs[0:4]
        p_send, p_recv, x_send, x_recv, pair_send, pair_recv = refs[4:10]
        g, flush_sems = refs[10:10 + n], refs[10 + n]
        small_refs = refs[11 + n:]
        land = (land0,) + land[1:]
        own = (own0,) + own[1:]
        x, y, c, chips = _place()
        me = 2 * x + y
        exchange = _ChipExchange(wire0, land0, x_send, x_recv)

        def half_rows(t, half):
            return pl.ds(pl.multiple_of(half * halves[t], 8), halves[t])

        def pair_copy(t, half):
            rows = g[t].at[half_rows(t, half), :]
            return pltpu.make_async_remote_copy(src_ref=rows, dst_ref=rows, send_sem=pair_send.at[t],
                                                recv_sem=pair_recv.at[t], device_id=(x, y, 1 - c), device_id_type=MESH)

        def flush(t):
            return pltpu.make_async_copy(g[t], g_out[t], flush_sems.at[t])

        small_rounds = small_body(*small_in, *small_out, *small_refs)
        next(small_rounds)
        _pair_reduce(acc0, wire0, own0, pland0, p_send, p_recv)
        next(small_rounds)
        exchange.start()

        for t in list(range(1, n)) + [0]:
            if t == 0:
                for done in range(1, n):
                    pair_copy(done, 1 - c).wait_recv()
                    flush(done).start()
                exchange.wait_recv()
            if t == min(2, n - 1):
                next(small_rounds)
            if t == min(4, n - 1):
                next(small_rounds, None)

            def chunk(r, carry, t=t):
                src = pl.ds(pl.multiple_of(r * ROW_CHUNK, ROW_CHUNK), ROW_CHUNK)
                dst = pl.ds(pl.multiple_of(c * halves[t] + r * ROW_CHUNK, 8), ROW_CHUNK)
                s = own[t][src, :]
                for d in range(3):
                    s = s + land[t][d, src, :].astype(F32)
                g[t][dst, :] = s
                return carry

            lax.fori_loop(0, halves[t] // ROW_CHUNK, chunk, 0)
            pair_copy(t, c).start()
        pair_copy(0, 1 - c).wait_recv()
        flush(0).start()
        for t in range(n):
            pair_copy(t, c).wait_send()
        exchange.wait_send()
        for t in range(n):
            flush(t).wait()

    half0 = (halves[0], widths[0])
    shapes = [(2 * h, w) for h, w in zip(halves, widths)]
    return pl.pallas_call(
        body, name="grad_finish",
        in_specs=[_vmem()] * (2 * n - 1 + ns), out_specs=[_hbm()] * n + [_vmem()] * 2,
        out_shape=[_hbm_shape(s, F32) for s in shapes] + [jax.ShapeDtypeStruct(s, F32) for s in _SMALL_OUT_DIMS],
        scratch_shapes=[pltpu.VMEM((N_CHIP,) + half0, _WIRE), pltpu.VMEM((N_CHIP,) + half0, _WIRE),
                        pltpu.VMEM((3,) + half0, _WIRE), pltpu.VMEM(half0, F32)]
        + [pltpu.SemaphoreType.DMA((N_CHIP,)), pltpu.SemaphoreType.DMA((N_CHIP,))]
        + _ChipExchange.sems()
        + [pltpu.SemaphoreType.DMA((n,)), pltpu.SemaphoreType.DMA((n,))]
        + [pltpu.VMEM(s, F32) for s in shapes] + [pltpu.SemaphoreType.DMA((n,))]
        + small_scratch,
        compiler_params=pltpu.CompilerParams(vmem_limit_bytes=56 * MIB),
    )(last_acc, *lands, *owns, *small)


_SMALL = ("ln_in_g", "ln_in_b", "b_in", "attn_sinks", "sgu_ln_g", "sgu_ln_b", "sgu_w", "sgu_b", "b_out",
          "ln_mix_g", "ln_mix_b", "ln_ffn_g", "ln_ffn_b")
_VEC_ROW = dict(ln_in_g=0, ln_in_b=1, b_in=2, attn_sinks=4, sgu_ln_g=5, sgu_ln_b=6, b_out=7, ln_mix_g=8, ln_mix_b=9,
                ln_ffn_g=10, ln_ffn_b=11)
_LOSS_ROW = 12
_VEC_ROWS = 16
_MAT_ROWS = N_GRP * BLK + BLK


_SMALL_IN = ("ln_in_g", "ln_in_b", "bq", "bkv", "bsuv", "sink", "sgu_ln_g", "sgu_ln_b", "sgu_w", "sgu_bt", "b_out",
             "ln_mix_g", "ln_mix_b", "ln_ffn_g", "ln_ffn_b", "loss")
_SMALL_OUT_DIMS = ((_VEC_ROWS, D_MODEL), (_MAT_ROWS, 128))


def _small_allreduce_parts():
    n_in = len(_SMALL_IN)

    def body(*refs):
        (g_ln_in_g, g_ln_in_b, g_bq, g_bkv, g_bsuv, g_sink, g_sln_g, g_sln_b, g_sw, g_sbt, g_bout,
         g_lmg, g_lmb, g_lfg, g_lfb, g_loss) = refs[:n_in]
        out_a, out_b = refs[n_in:n_in + 2]
        (buf_a, buf_b, pair_a, pair_b, stage_a, stage_b, tot_a, tot_b,
         p1_send, p1_recv, x_send, x_recv, p2_send, p2_recv) = refs[n_in + 2:]
        x, y, c, chips = _place()
        me = 2 * x + y
        sibling = (x, y, 1 - c)
        half_a, half_b = _VEC_ROWS // 2, _MAT_ROWS // 2

        buf_a[...] = jnp.zeros_like(buf_a)
        for row, ref in ((0, g_ln_in_g), (1, g_ln_in_b), (7, g_bout), (8, g_lmg), (9, g_lmb), (10, g_lfg), (11, g_lfb),
                         (_LOSS_ROW, g_loss)):
            buf_a[row:row + 1, :] = ref[...]
        buf_a[2:3, 0:ATTN_W] = g_bq[...]
        buf_a[2:3, ATTN_W:ATTN_W + 2 * KV_W] = g_bkv[...]
        buf_a[2:3, ATTN_W + 2 * KV_W:D_MODEL] = g_bsuv[:, 0:2 * KV_W]
        buf_a[3:4, 0:2 * SGU_W - 2 * KV_W] = g_bsuv[:, 2 * KV_W:2 * SGU_W]
        buf_a[4:5, 0:128] = g_sink[...]
        buf_a[5:6, 0:SGU_W] = g_sln_g[...]
        buf_a[6:7, 0:SGU_W] = g_sln_b[...]
        for h in range(N_GRP):
            buf_b[h * BLK:(h + 1) * BLK, :] = g_sw[h]
        buf_b[N_GRP * BLK:_MAT_ROWS, :] = g_sbt[...]

        def remote(src, dst, send_sem, recv_sem, to):
            return pltpu.make_async_remote_copy(src_ref=src, dst_ref=dst, send_sem=send_sem, recv_sem=recv_sem,
                                                device_id=to, device_id_type=MESH)

        first = [remote(buf_a, pair_a, p1_send.at[0], p1_recv.at[0], sibling),
                 remote(buf_b, pair_b, p1_send.at[1], p1_recv.at[1], sibling)]
        for cp in first:
            cp.start()
        yield
        for cp in first:
            cp.wait()
        rows_a = pl.ds(pl.multiple_of(c * half_a, 8), half_a)
        rows_b = pl.ds(pl.multiple_of(c * half_b, 8), half_b)
        stage_a[me] = buf_a[rows_a, :] + pair_a[rows_a, :]
        stage_b[me] = buf_b[rows_b, :] + pair_b[rows_b, :]

        def chip_copies(d):
            to = (chips[d][0], chips[d][1], c)
            return [remote(stage_a.at[me], stage_a.at[me], x_send.at[2 * d], x_recv.at[2 * d], to),
                    remote(stage_b.at[me], stage_b.at[me], x_send.at[2 * d + 1], x_recv.at[2 * d + 1], to)]

        def chip_arrivals(d):
            slot = 2 * chips[d][0] + chips[d][1]
            to = (chips[d][0], chips[d][1], c)
            return [remote(stage_a.at[slot], stage_a.at[slot], x_send.at[2 * d], x_recv.at[2 * d], to),
                    remote(stage_b.at[slot], stage_b.at[slot], x_send.at[2 * d + 1], x_recv.at[2 * d + 1], to)]

        for d in range(3):
            for cp in chip_copies(d):
                cp.start()
        yield
        for d in range(3):
            for cp in chip_arrivals(d):
                cp.wait_recv()
        tot_a[rows_a, :] = ((stage_a[0] + stage_a[1]) + stage_a[2]) + stage_a[3]
        tot_b[rows_b, :] = ((stage_b[0] + stage_b[1]) + stage_b[2]) + stage_b[3]

        second = [remote(tot_a.at[rows_a, :], tot_a.at[rows_a, :], p2_send.at[0], p2_recv.at[0], sibling),
                  remote(tot_b.at[rows_b, :], tot_b.at[rows_b, :], p2_send.at[1], p2_recv.at[1], sibling)]
        for cp in second:
            cp.start()
        yield
        other_a = pl.ds(pl.multiple_of((1 - c) * half_a, 8), half_a)
        other_b = pl.ds(pl.multiple_of((1 - c) * half_b, 8), half_b)
        remote(tot_a.at[other_a, :], tot_a.at[other_a, :], p2_send.at[0], p2_recv.at[0], sibling).wait_recv()
        remote(tot_b.at[other_b, :], tot_b.at[other_b, :], p2_send.at[1], p2_recv.at[1], sibling).wait_recv()
        for cp in second:
            cp.wait_send()
        for d in range(3):
            for cp in chip_copies(d):
                cp.wait_send()
        out_a[...] = tot_a[...]
        out_b[...] = tot_b[...]

    vec = pltpu.VMEM((_VEC_ROWS, D_MODEL), F32)
    mat = pltpu.VMEM((_MAT_ROWS, 128), F32)
    scratch = [vec, mat, vec, mat, pltpu.VMEM((N_CHIP, _VEC_ROWS // 2, D_MODEL), F32),
               pltpu.VMEM((N_CHIP, _MAT_ROWS // 2, 128), F32), vec, mat,
               pltpu.SemaphoreType.DMA((2,)), pltpu.SemaphoreType.DMA((2,)), pltpu.SemaphoreType.DMA((6,)),
               pltpu.SemaphoreType.DMA((6,)), pltpu.SemaphoreType.DMA((2,)), pltpu.SemaphoreType.DMA((2,))]
    return body, scratch


def _small_adamw(tot_a, tot_b, params):
    shapes = [params[nm][0].shape for nm in _SMALL]

    def body(*refs):
        ta, tb = refs[:2]
        prm = refs[2:2 + 3 * len(_SMALL)]
        outs = refs[2 + 3 * len(_SMALL):]

        def grad_of(k, name):
            if name == "sgu_w":
                return [tb[h * BLK:(h + 1) * BLK, :] for h in range(N_GRP)]
            if name == "sgu_b":
                return jnp.transpose(tb[N_GRP * BLK:_MAT_ROWS, :])[0:N_GRP, :]
            row = _VEC_ROW[name]
            if name == "b_in":
                return jnp.concatenate([ta[row:row + 1, :], ta[row + 1:row + 2, 0:IN_W - D_MODEL]], axis=1)
            return ta[row:row + 1, 0:shapes[k][-1]]

        for k, name in enumerate(_SMALL):
            w_ref, m_ref, v_ref = prm[3 * k:3 * k + 3]
            g_out, d_out, m_out, v_out = outs[4 * k:4 * k + 4]
            g = grad_of(k, name)
            if name == "sgu_w":
                for h in range(N_GRP):
                    d_, m_, v_ = _adamw_math(w_ref[h], g[h], m_ref[h], v_ref[h])
                    g_out[h], d_out[h], m_out[h], v_out[h] = g[h], d_, m_, v_
            else:
                d_, m_, v_ = _adamw_math(w_ref[...], g, m_ref[...], v_ref[...])
                g_out[...], d_out[...], m_out[...], v_out[...] = g, d_, m_, v_
        outs[-1][...] = jnp.sum(ta[_LOSS_ROW:_LOSS_ROW + 1, :], axis=1, keepdims=True) * (0.5 / D_MODEL)

    ins = [tot_a, tot_b] + [_in_hbm(a) for nm in _SMALL for a in params[nm]]
    out_dims = [s for s in shapes for _ in range(4)] + [(1, 1)]
    res = pl.pallas_call(
        body, name="small_adamw", grid=(1,),
        in_specs=[_const2(a.shape) for a in ins], out_specs=[_const2(s) for s in out_dims],
        out_shape=[_hbm_shape(s, F32) for s in out_dims],
        compiler_params=_params(32),
    )(*ins)
    return {nm: tuple(res[4 * k:4 * k + 4]) for k, nm in enumerate(_SMALL)}, res[-1]


def _adamw_math(w, g, m, v):
    m = ADAM_B1 * m + (1.0 - ADAM_B1) * g
    v = ADAM_B2 * v + (1.0 - ADAM_B2) * (g * g)
    m_hat = m / (1.0 - ADAM_B1 ** ADAM_STEP)
    v_hat = v / (1.0 - ADAM_B2 ** ADAM_STEP)
    delta = -ADAM_LR * (m_hat / (jnp.sqrt(v_hat) + ADAM_EPS) + ADAM_WD * w)
    return delta, m, v


ADAMW_STEPS = 4


def _adamw(name, groups):
    k = len(groups)

    def body(*refs):
        for i in range(k):
            w_ref, g_ref, m_ref, v_ref = refs[4 * i:4 * i + 4]
            outs = _adamw_math(w_ref[...], g_ref[...], m_ref[...], v_ref[...])
            for o_ref, o in zip(refs[4 * k + 3 * i:4 * k + 3 * i + 3], outs):
                o_ref[...] = o

    in_specs, out_specs = [], []
    for grp in groups:
        rows, cols = grp[0].shape
        assert rows % (8 * ADAMW_STEPS) == 0, rows
        tile = pl.BlockSpec((rows // ADAMW_STEPS, cols), lambda i: (i, 0))
        in_specs += [tile] * 4
        out_specs += [tile] * 3
    res = pl.pallas_call(
        body, name=name, grid=(ADAMW_STEPS,), in_specs=in_specs, out_specs=out_specs,
        out_shape=[_hbm_shape(grp[0].shape, F32) for grp in groups for _ in range(3)],
        compiler_params=_params(56),
    )(*[_in_hbm(a) for grp in groups for a in grp])
    return [res[3 * i:3 * i + 3] for i in range(k)]


def kernel(x, positions, ln_in_g, ln_in_b, w_in, b_in, attn_sinks, sgu_ln_g, sgu_ln_b, sgu_w, sgu_b, w_out, b_out, ln_mix_g, ln_mix_b, w_gate, w_up, w_down, ln_ffn_g, ln_ffn_b, loss_target, m_ln_in_g, m_ln_in_b, m_w_in, m_b_in, m_attn_sinks, m_sgu_ln_g, m_sgu_ln_b, m_sgu_w, m_sgu_b, m_w_out, m_b_out, m_ln_mix_g, m_ln_mix_b, m_w_gate, m_w_up, m_w_down, m_ln_ffn_g, m_ln_ffn_b, v_ln_in_g, v_ln_in_b, v_w_in, v_b_in, v_attn_sinks, v_sgu_ln_g, v_sgu_ln_b, v_sgu_w, v_sgu_b, v_w_out, v_b_out, v_ln_mix_g, v_ln_mix_b, v_w_gate, v_w_up, v_w_down, v_ln_ffn_g, v_ln_ffn_b):
    weights = dict(ln_in_g=ln_in_g, ln_in_b=ln_in_b, w_in=w_in, b_in=b_in, attn_sinks=attn_sinks, sgu_ln_g=sgu_ln_g,
                   sgu_ln_b=sgu_ln_b, sgu_w=sgu_w, sgu_b=sgu_b, w_out=w_out, b_out=b_out, ln_mix_g=ln_mix_g,
                   ln_mix_b=ln_mix_b, w_gate=w_gate, w_up=w_up, w_down=w_down, ln_ffn_g=ln_ffn_g, ln_ffn_b=ln_ffn_b)
    mom_m = dict(ln_in_g=m_ln_in_g, ln_in_b=m_ln_in_b, w_in=m_w_in, b_in=m_b_in, attn_sinks=m_attn_sinks,
                 sgu_ln_g=m_sgu_ln_g, sgu_ln_b=m_sgu_ln_b, sgu_w=m_sgu_w, sgu_b=m_sgu_b, w_out=m_w_out, b_out=m_b_out,
                 ln_mix_g=m_ln_mix_g, ln_mix_b=m_ln_mix_b, w_gate=m_w_gate, w_up=m_w_up, w_down=m_w_down,
                 ln_ffn_g=m_ln_ffn_g, ln_ffn_b=m_ln_ffn_b)
    mom_v = dict(ln_in_g=v_ln_in_g, ln_in_b=v_ln_in_b, w_in=v_w_in, b_in=v_b_in, attn_sinks=v_attn_sinks,
                 sgu_ln_g=v_sgu_ln_g, sgu_ln_b=v_sgu_ln_b, sgu_w=v_sgu_w, sgu_b=v_sgu_b, w_out=v_w_out, b_out=v_b_out,
                 ln_mix_g=v_ln_mix_g, ln_mix_b=v_ln_mix_b, w_gate=v_w_gate, w_up=v_w_up, w_down=v_w_down,
                 ln_ffn_g=v_ln_ffn_g, ln_ffn_b=v_ln_ffn_b)
    order = list(weights)
    big = ("w_in", "w_out", "w_gate", "w_up", "w_down")

    s_len = x.shape[1]
    xs = _in_hbm(x.reshape(s_len, D_MODEL))
    tgt = _in_hbm(loss_target.reshape(s_len, D_MODEL))
    pos_row = _in_hbm(positions.reshape(1, s_len))
    g0, b0 = _in_hbm(ln_in_g.reshape(1, D_MODEL)), _in_hbm(ln_in_b.reshape(1, D_MODEL))
    sinks = attn_sinks.reshape(N_Q)
    sgu_w3 = _in_hbm(sgu_w.reshape(N_GRP, BLK, BLK))
    sgu_bt = _in_hbm(sgu_b.reshape(N_GRP, BLK).T)
    b_in, b_out, sgu_ln_g, sgu_ln_b, ln_mix_g, ln_mix_b, ln_ffn_g, ln_ffn_b = (
        _in_hbm(a) for a in (b_in, b_out, sgu_ln_g, sgu_ln_b, ln_mix_g, ln_mix_b, ln_ffn_g, ln_ffn_b))

    col_sharded = ("w_in", "w_gate", "w_up")

    def rowmajor(name, a):
        return jnp.swapaxes(a[0], 0, 1) if name in col_sharded else a[0]

    def as_given(name, a):
        return (jnp.swapaxes(a, 0, 1) if name in col_sharded else a)[None]

    shards = [rowmajor(n, weights[n]) for n in big]
    (gw_in,) = _gather_weights(shards[0:1])
    w_in_full = gw_in.reshape(IN_W, D_MODEL)

    sh_out, sh_gate, sh_up, sh_down = shards[1:]
    *acts, gw_out, gw_gate0 = _ln_inproj(xs, pos_row, g0, b0, w_in_full, b_in, _GatherPlan(
        [(sh_out, (0, OUT_SH), None), (sh_gate, (0, GATE_CUT), None)]))
    q, k, v, su, sv, tc, t1, t2 = (_in_hbm(a) for a in acts)
    mc, gw_gate, gw_up0 = _mixer_fwd(q, k, v, su, sv, sinks, sgu_ln_g, sgu_ln_b, sgu_w3, sgu_bt, _GatherPlan(
        [(sh_gate, (GATE_CUT, FF_SH), gw_gate0), (sh_up, (0, UP_CUT), None)]))
    mc = _in_hbm(mc)
    w_out_full = gw_out.reshape(D_MODEL, D_MODEL)
    r1, gw_up = _outproj(mc, w_out_full, b_out, xs, g0, b0, _GatherPlan([(sh_up, (UP_CUT, FF_SH), gw_up0)]))
    r1 = _in_hbm(r1)
    act, p_act, q_act, h1, gw_down = _ffn_up(r1, ln_mix_g, ln_mix_b, gw_gate, gw_up,
                                             _GatherPlan([(sh_down, (0, FF_SH), None)]))
    act, p_act, q_act = _in_hbm(act), _in_hbm(p_act), _in_hbm(q_act)
    dr2, loss_cols, d_ln_ffn_g, d_ln_ffn_b = _ffn_down_loss(act, gw_down, _in_hbm(h1), ln_ffn_g, ln_ffn_b, tgt)
    dr2 = _in_hbm(dr2)

    dg, du, wire_down, own_down = _ffn_bwd_a(dr2, act, p_act, q_act, gw_down)
    dh1a, wire_gate, own_gate, land_down = _ffn_bwd_g(dr2, _in_hbm(dg), r1, ln_mix_g, ln_mix_b, gw_gate, wire_down)
    dr1, wire_up, own_up, d_ln_mix_g, d_ln_mix_b, land_gate = _ffn_bwd_u(_in_hbm(dh1a), _in_hbm(du), r1, ln_mix_g,
                                                                         ln_mix_b, gw_up, wire_gate)
    dr1 = _in_hbm(dr1)
    dmc, wire_out, own_out, d_b_out = _outproj_bwd(dr1, mc, w_out_full)
    (dq, dkv, dsuv, dbq, dbkv, dbsuv, d_sink, d_sgu_ln_g, d_sgu_ln_b, d_sgu_w, d_sgu_bt, land_up, land_out) = _mixer_bwd(
        q, k, v, su, sv, _in_hbm(dmc), tc, t1, t2, sinks, sgu_ln_g, sgu_ln_b, sgu_w3, sgu_bt, [wire_up, wire_out])
    grad_x, acc_in, d_ln_in_g, d_ln_in_b = _inproj_bwd(_in_hbm(dq), _in_hbm(dkv), _in_hbm(dsuv), dr1, xs, g0, b0,
                                                       w_in_full)

    small_local = dict(
        ln_in_g=d_ln_in_g, ln_in_b=d_ln_in_b, bq=dbq, bkv=dbkv, bsuv=dbsuv, sink=d_sink, sgu_ln_g=d_sgu_ln_g,
        sgu_ln_b=d_sgu_ln_b, sgu_w=d_sgu_w, sgu_bt=d_sgu_bt, b_out=d_b_out, ln_mix_g=d_ln_mix_g, ln_mix_b=d_ln_mix_b,
        ln_ffn_g=d_ln_ffn_g, ln_ffn_b=d_ln_ffn_b, loss=loss_cols)
    *reduced, tot_a, tot_b = _grad_finish(acc_in, [land_out, land_gate, land_up, land_down],
                                          [own_out, own_gate, own_up, own_down], [small_local[nm] for nm in _SMALL_IN])
    small_shape = dict(ln_in_g=(1, D_MODEL), ln_in_b=(1, D_MODEL), sgu_w=(N_GRP, BLK, BLK), sgu_b=(N_GRP, BLK))
    small_params = {nm: tuple(src[nm].reshape(small_shape.get(nm, src[nm].shape)) for src in (weights, mom_m, mom_v))
                    for nm in _SMALL}
    small_out, loss = _small_adamw(_in_hbm(tot_a), _in_hbm(tot_b), small_params)
    loss = loss.reshape(())
    grads, delta, new_m, new_v = {}, {}, {}, {}
    for nm in _SMALL:
        grads[nm], delta[nm], new_m[nm], new_v[nm] = (a.reshape(weights[nm].shape) for a in small_out[nm])

    groups = [(shards[t], reduced[t], rowmajor(nm, mom_m[nm]), rowmajor(nm, mom_v[nm])) for t, nm in enumerate(big)]
    for t, (nm, res) in enumerate(zip(big, _adamw("adamw", groups))):
        grads[nm] = as_given(nm, reduced[t])
        delta[nm], new_m[nm], new_v[nm] = (as_given(nm, a) for a in res)

    return (loss, grad_x.reshape(x.shape), *[grads[n] for n in order], *[delta[n] for n in order],
            *[new_m[n] for n in order], *[new_v[n] for n in order])
```

```python
import jax
import jax.numpy as jnp
from jax import lax
from jax.experimental import pallas as pl
from jax.experimental.pallas import tpu as pltpu

F32 = jnp.float32
_MXU = jnp.bfloat16
_WIRE = jnp.bfloat16
_ACT = jnp.bfloat16

D_MODEL = 1024
ATTN_W = 512
SGU_W = 512
HEAD_DIM = 64
N_Q = 8
N_KV = 2
Q_PER_KV = 4
KV_W = 128
BLK = 128
ROT_DIM = 16
ROPE_THETA = 500000.0
N_GRP = 4
GRP_DIM = 128
D_FF = 2816
IN_W = 1792
LN_EPS = 1e-5
ALPHA = 2.0 ** 0.25
N_CHIP = 4
FF_SH = D_FF // N_CHIP
IN_SH = IN_W // N_CHIP
OUT_SH = D_MODEL // N_CHIP
ROW_CHUNK = 32
GATE_CUT, UP_CUT = 352, 320

ADAM_LR = 0.001
ADAM_B1 = 0.9
ADAM_B2 = 0.999
ADAM_EPS = 1e-08
ADAM_WD = 0.01
ADAM_STEP = 10

SQRT_HALF = 0.7071067811865476
INV_SQRT_2PI = 0.3989422804014327
MESH_AXES = ("x", "y", "c")
MESH = pl.DeviceIdType.MESH
MIB = 2 ** 20


def _vmem():
    return pl.BlockSpec(memory_space=pltpu.VMEM)


def _smem():
    return pl.BlockSpec(memory_space=pltpu.SMEM)


def _hbm():
    return pl.BlockSpec(memory_space=pl.ANY)


def _hbm_shape(shape, dtype):
    return pltpu.HBM(shape, dtype)


def _in_hbm(a):
    return pltpu.with_memory_space_constraint(a, pltpu.HBM)


def _params(vmem_mib=48):
    return pltpu.CompilerParams(dimension_semantics=("arbitrary",), vmem_limit_bytes=vmem_mib * MIB)


def _tile(n, cap):
    if n <= cap:
        return n
    for t in range(cap - cap % 16, 0, -16):
        if n % t == 0:
            return t
    raise ValueError((n, cap))


def _rows(tm, width):
    return pl.BlockSpec((tm, width), lambda i: (i, 0))


def _const2(shape):
    return pl.BlockSpec(shape, lambda i: (0,) * len(shape))


def _ln(x, g, b):
    mu = jnp.mean(x, axis=-1, keepdims=True)
    xc = x - mu
    var = jnp.mean(xc * xc, axis=-1, keepdims=True)
    rstd = lax.rsqrt(var + LN_EPS)
    xhat = xc * rstd
    return xhat * g + b, xhat, rstd


def _ln_bwd(dy, xhat, rstd, g):
    gdy = dy * g
    m1 = jnp.mean(gdy, axis=-1, keepdims=True)
    m2 = jnp.mean(gdy * xhat, axis=-1, keepdims=True)
    return rstd * (gdy - m1 - xhat * m2)


def _colsum(a):
    return jnp.sum(a, axis=0, keepdims=True)


def _gelu_and_grad(x):
    cdf = 0.5 * (1.0 + lax.erf(x * SQRT_HALF))
    return x * cdf, cdf + x * jnp.exp(-0.5 * x * x) * INV_SQRT_2PI


def _dot(a, b):
    return jnp.dot(a, b, preferred_element_type=F32)


def _dot_nt(a, b):
    return lax.dot_general(a, b, (((1,), (1,)), ((), ())), preferred_element_type=F32)


def _dot_tn(a, b):
    return lax.dot_general(a, b, (((0,), (0,)), ((), ())), preferred_element_type=F32)


def _rope(t, tc, t1, t2):
    n = t.shape[1]
    rep = n // 128
    if rep > 1:
        tc, t1, t2 = (jnp.tile(a, (1, rep)) for a in (tc, t1, t2))
    return t * tc + pltpu.roll(t, n - 8, 1) * t1 + pltpu.roll(t, 8, 1) * t2


def _rope_bwd(d, tc, t1, t2):
    n = d.shape[1]
    rep = n // 128
    if rep > 1:
        tc, t1, t2 = (jnp.tile(a, (1, rep)) for a in (tc, t1, t2))
    return d * tc + pltpu.roll(d * t1, 8, 1) + pltpu.roll(d * t2, n - 8, 1)


def _causal_w(w_ref, h):
    t = lax.broadcasted_iota(jnp.int32, (BLK, BLK), 0)
    s = lax.broadcasted_iota(jnp.int32, (BLK, BLK), 1)
    return jnp.where(s <= t, w_ref[h], 0.0)


def _lane_put(vals, width):
    rows = vals[0].shape[0]
    lane = lax.broadcasted_iota(jnp.int32, (rows, width), 1)
    out = jnp.zeros((rows, width), F32)
    for k, v in enumerate(vals):
        out = out + jnp.where(lane == k, v, 0.0)
    return out


def _rope_consts():
    lane = jnp.arange(128) % HEAD_DIM
    rot = lane < ROT_DIM
    pair = (2 * (lane % (ROT_DIM // 2))).astype(F32)
    freq = jnp.where(rot, ROPE_THETA ** (-pair / ROT_DIM), 0.0)
    rows = [freq, rot.astype(F32), 1.0 - rot.astype(F32), (lane < ROT_DIM // 2).astype(F32),
            jnp.logical_and(lane >= ROT_DIM // 2, rot).astype(F32)]
    rows += [jnp.zeros((128,), F32)] * 3
    return jnp.stack(rows).astype(F32)


def _ln_inproj(x, pos_row, g0, b0, w_in, b_in, plan):
    s_len = x.shape[0]
    tm = _tile(s_len, 512)
    m, n = len(plan.operands()), plan.n

    def body(x_ref, pos_ref, g_ref, b_ref, w_ref, bi_ref, rc_ref, *rest):
        q_ref, k_ref, v_ref, su_ref, sv_ref, tc_ref, t1_ref, t2_ref = rest[m:m + 8]
        gather = plan.bind(rest[:m], rest[m + 8:m + 8 + n], rest[m + 8 + n:])
        i = pl.program_id(0)

        @pl.when(i == 0)
        def _():
            gather.start()

        h0, _, _ = _ln(x_ref[...], g_ref[...], b_ref[...])
        proj = _dot_nt(h0.astype(_MXU), w_ref[...]) + bi_ref[...]
        pos = jnp.broadcast_to(pos_ref[...].astype(F32), (128, tm))
        ang = jnp.transpose(pos) * rc_ref[0:1, :]
        cs = jnp.cos(ang)
        sn = jnp.sin(ang)
        tc = cs * rc_ref[1:2, :] + rc_ref[2:3, :]
        t1 = -sn * rc_ref[3:4, :]
        t2 = sn * rc_ref[4:5, :]
        tc_ref[...] = tc
        t1_ref[...] = t1
        t2_ref[...] = t2
        q = _rope(proj[:, 0:ATTN_W], tc, t1, t2) * (HEAD_DIM ** -0.5)
        q_ref[...] = q.astype(_MXU)
        k_ref[...] = _rope(proj[:, ATTN_W:ATTN_W + KV_W], tc, t1, t2).astype(_MXU)
        v_ref[...] = proj[:, ATTN_W + KV_W:ATTN_W + 2 * KV_W].astype(_MXU)
        su_ref[...] = proj[:, ATTN_W + 2 * KV_W:ATTN_W + 2 * KV_W + SGU_W]
        sv_ref[...] = proj[:, ATTN_W + 2 * KV_W + SGU_W:IN_W]

        last = pl.num_programs(0) - 1

        @pl.when(i == jnp.maximum(last - 1, 0))
        def _():
            gather.pass_on()

        @pl.when(i == last)
        def _():
            gather.finish()

    sd = _hbm_shape
    return pl.pallas_call(
        body, name="ln_inproj", grid=(s_len // tm,),
        in_specs=[_rows(tm, D_MODEL), pl.BlockSpec((1, tm), lambda i: (0, i)), _const2((1, D_MODEL)),
                  _const2((1, D_MODEL)), _vmem(),
                  _const2((1, IN_W)), _const2((8, 128))] + plan.in_specs(),
        out_specs=[_rows(tm, ATTN_W), _rows(tm, KV_W), _rows(tm, KV_W), _rows(tm, SGU_W), _rows(tm, SGU_W),
                   _rows(tm, 128), _rows(tm, 128), _rows(tm, 128)] + plan.out_specs(),
        out_shape=[sd((s_len, ATTN_W), _MXU), sd((s_len, KV_W), _MXU), sd((s_len, KV_W), _MXU),
                   sd((s_len, SGU_W), F32), sd((s_len, SGU_W), F32),
                   sd((s_len, 128), F32), sd((s_len, 128), F32), sd((s_len, 128), F32)] + plan.out_shapes(),
        scratch_shapes=plan.scratch(),
        compiler_params=_params(56),
    )(x, pos_row, g0, b0, w_in, b_in, _rope_consts(), *plan.operands())


def _band_mask_t(first_block):
    kj = lax.broadcasted_iota(jnp.int32, (2 * BLK, BLK), 0)
    qi = lax.broadcasted_iota(jnp.int32, (2 * BLK, BLK), 1)
    shut = jnp.where(first_block, 2 * BLK, 0)
    prev_ok = jnp.logical_and(kj < BLK, kj > qi + shut)
    cur_ok = jnp.logical_and(kj >= BLK, (kj - BLK) <= qi)
    return jnp.logical_or(prev_ok, cur_ok)


def _attn_probs_t(kh, qh, sink, allowed_t):
    s = jnp.where(allowed_t, _dot_nt(kh, qh), -1e30)
    m = jnp.maximum(jnp.max(s, axis=0, keepdims=True), sink)
    p = jnp.exp(s - m)
    ps = jnp.exp(sink - m)
    inv = 1.0 / (jnp.sum(p, axis=0, keepdims=True) + ps)
    return p * inv, ps * inv


def _sgu_mix(gv, lg, lb, w_ref, bt_ref):
    vv, vhat, rstd = _ln(gv, lg, lb)
    vvb = vv.astype(_MXU)
    wcs, mixed = [], []
    for h in range(N_GRP):
        wc = _causal_w(w_ref, h).astype(_MXU)
        wcs.append(wc)
        mixed.append(_dot(wc, vvb[:, h * GRP_DIM:(h + 1) * GRP_DIM]) + bt_ref[:, h:h + 1])
    return jnp.concatenate(mixed, axis=1), vhat, rstd, vvb, wcs


def _mixer_fwd(q, k, v, su, sv, sinks, sg, sb, sgu_w, sgu_bt, plan):
    s_len = q.shape[0]
    nb = s_len // BLK
    per = 2 if nb % 2 == 0 else 1
    steps = nb // per
    m, n = len(plan.operands()), plan.n

    def body(q_ref, kc_ref, kp_ref, vc_ref, vp_ref, su_ref, sv_ref, sink_ref, lg_ref, lb_ref, w_ref, bt_ref, *rest):
        mc_ref = rest[m]
        gather = plan.bind(rest[:m], rest[m + 1:m + 1 + n], rest[m + 1 + n:])
        i = pl.program_id(0)

        @pl.when(i == 0)
        def _():
            gather.start()

        @pl.when(i == max(steps - 2, 0))
        def _():
            gather.pass_on()

        @pl.when(i == steps - 1)
        def _():
            gather.finish()

        for s in range(per):
            rows = slice(s * BLK, (s + 1) * BLK)
            before = slice((s - 1) * BLK, s * BLK)
            k_prev = kp_ref[...] if s == 0 else kc_ref[before, :]
            v_prev = vp_ref[...] if s == 0 else vc_ref[before, :]
            allowed_t = _band_mask_t(i == 0 if s == 0 else False)
            kb = jnp.concatenate([k_prev, kc_ref[rows, :]], axis=0)
            vb = jnp.concatenate([v_prev, vc_ref[rows, :]], axis=0)
            qv = q_ref[rows, :]
            outs = []
            allowed_g = jnp.tile(allowed_t, (1, Q_PER_KV))
            for g in range(N_KV):
                heads = range(g * Q_PER_KV, (g + 1) * Q_PER_KV)
                kh = kb[:, g * HEAD_DIM:(g + 1) * HEAD_DIM]
                vh = vb[:, g * HEAD_DIM:(g + 1) * HEAD_DIM]
                q_g = jnp.concatenate([qv[:, h * HEAD_DIM:(h + 1) * HEAD_DIM] for h in heads], axis=0)
                sink_g = jnp.concatenate([jnp.full((1, BLK), sink_ref[h], F32) for h in heads], axis=1)
                probs_t, _ = _attn_probs_t(kh, q_g, sink_g, allowed_g)
                o_g = _dot_tn(probs_t.astype(_MXU), vh)
                outs += [o_g[hh * BLK:(hh + 1) * BLK, :] for hh in range(Q_PER_KV)]
            u = _gelu_and_grad(su_ref[rows, :])[0]
            gv = _gelu_and_grad(sv_ref[rows, :])[0]
            mixed = _sgu_mix(gv, lg_ref[...], lb_ref[...], w_ref, bt_ref)[0]
            mc_ref[rows, :] = jnp.concatenate(outs + [u * mixed], axis=1).astype(_MXU)

    cur = lambda w: pl.BlockSpec((per * BLK, w), lambda i: (i, 0))
    prev = lambda w: pl.BlockSpec((BLK, w), lambda i: (jnp.maximum(per * i - 1, 0), 0))
    return pl.pallas_call(
        body, name="mixer_fwd", grid=(steps,),
        in_specs=[cur(ATTN_W), cur(KV_W), prev(KV_W), cur(KV_W), prev(KV_W), cur(SGU_W), cur(SGU_W), _smem(),
                  _const2((1, SGU_W)), _const2((1, SGU_W)), _const2((N_GRP, BLK, BLK)), _const2((BLK, N_GRP))]
        + plan.in_specs(),
        out_specs=[cur(D_MODEL)] + plan.out_specs(),
        out_shape=[_hbm_shape((s_len, D_MODEL), _MXU)] + plan.out_shapes(),
        scratch_shapes=plan.scratch(),
        compiler_params=_params(56),
    )(q, k, k, v, v, su, sv, sinks, sg, sb, sgu_w, sgu_bt, *plan.operands())


def _outproj(mc, w_out, b_out, x, g0, b0, plan):
    s_len = x.shape[0]
    tm = _tile(s_len, 512)
    m, n = len(plan.operands()), plan.n

    def body(mc_ref, w_ref, bo_ref, x_ref, g_ref, b_ref, *rest):
        r1_ref = rest[m]
        gather = plan.bind(rest[:m], rest[m + 1:m + 1 + n], rest[m + 1 + n:])
        i = pl.program_id(0)

        @pl.when(i == 0)
        def _():
            gather.start()

        h0, _, _ = _ln(x_ref[...], g_ref[...], b_ref[...])
        r1_ref[...] = ALPHA * h0 + (_dot(mc_ref[...], w_ref[...]) + bo_ref[...])

        last = pl.num_programs(0) - 1

        @pl.when(i == jnp.maximum(last - 1, 0))
        def _():
            gather.pass_on()

        @pl.when(i == last)
        def _():
            gather.finish()

    return pl.pallas_call(
        body, name="outproj", grid=(s_len // tm,),
        in_specs=[_rows(tm, D_MODEL), _vmem(), _const2((1, D_MODEL)), _rows(tm, D_MODEL),
                  _const2((1, D_MODEL)), _const2((1, D_MODEL))] + plan.in_specs(),
        out_specs=[_rows(tm, D_MODEL)] + plan.out_specs(),
        out_shape=[_hbm_shape((s_len, D_MODEL), F32)] + plan.out_shapes(),
        scratch_shapes=plan.scratch(),
        compiler_params=_params(40),
    )(mc, w_out, b_out, x, g0, b0, *plan.operands())


def _ffn_spec(tm):
    return pl.BlockSpec((N_CHIP, tm, FF_SH), lambda i: (0, i, 0))


def _ffn_up(r1, g1, b1, wg, wu, plan):
    s_len = r1.shape[0]
    tm = _tile(s_len, 512)
    m, n = len(plan.operands()), plan.n

    def body(r1_ref, g_ref, b_ref, wg_ref, wu_ref, *rest):
        a_ref, p_ref, q_ref, h1_ref = rest[m:m + 4]
        gather = plan.bind(rest[:m], rest[m + 4:m + 4 + n], rest[m + 4 + n:])
        i = pl.program_id(0)

        @pl.when(i == 0)
        def _():
            gather.start()

        h1, _, _ = _ln(r1_ref[...], g_ref[...], b_ref[...])
        h1_ref[...] = h1
        h1b = h1.astype(_MXU)
        for j in range(N_CHIP):
            g = _dot_nt(h1b, wg_ref[j])
            u = _dot_nt(h1b, wu_ref[j])
            silu, sg = _silu_parts(g)
            a_ref[j] = (silu * u).astype(_MXU)
            p_ref[j] = silu.astype(_ACT)
            q_ref[j] = (u * (sg * (1.0 + g * (1.0 - sg)))).astype(_ACT)

        last = pl.num_programs(0) - 1

        @pl.when(i == jnp.maximum(last - 1, 0))
        def _():
            gather.pass_on()

        @pl.when(i == last)
        def _():
            gather.finish()

    sd = _hbm_shape((N_CHIP, s_len, FF_SH), _ACT)
    return pl.pallas_call(
        body, name="ffn_up", grid=(s_len // tm,),
        in_specs=[_rows(tm, D_MODEL), _const2((1, D_MODEL)), _const2((1, D_MODEL)), _vmem(), _vmem()] + plan.in_specs(),
        out_specs=[_ffn_spec(tm)] * 3 + [_rows(tm, D_MODEL)] + plan.out_specs(),
        out_shape=[_hbm_shape((N_CHIP, s_len, FF_SH), _MXU), sd, sd, _hbm_shape((s_len, D_MODEL), F32)]
        + plan.out_shapes(),
        scratch_shapes=plan.scratch(),
        compiler_params=_params(56),
    )(r1, g1, b1, wg, wu, *plan.operands())


def _silu_parts(g):
    sg = 1.0 / (1.0 + jnp.exp(-g))
    return g * sg, sg


def _ffn_down_loss(act, wd, h1, g2, b2, target):
    s_len = h1.shape[0]
    tm = _tile(s_len, 512)

    parts = 2 if tm % 32 == 0 else 1
    sub = tm // parts

    def body(a_ref, wd_ref, h1_ref, g2_ref, b2_ref, t_ref, dr2_ref, loss_ref, dg2_ref, db2_ref):
        i = pl.program_id(0)

        @pl.when(i == 0)
        def _():
            loss_ref[...] = jnp.zeros_like(loss_ref)
            dg2_ref[...] = jnp.zeros_like(dg2_ref)
            db2_ref[...] = jnp.zeros_like(db2_ref)

        for part in range(parts):
            rows = slice(part * sub, (part + 1) * sub)
            f = jnp.zeros((sub, D_MODEL), F32)
            for j in range(N_CHIP):
                f = f + _dot(a_ref[j, rows, :], wd_ref[j])
            h2, r2hat, rstd2 = _ln(ALPHA * h1_ref[rows, :] + f, g2_ref[...], b2_ref[...])
            diff = h2 - t_ref[rows, :]
            dh2 = diff * (1.0 / D_MODEL)
            loss_ref[...] += _colsum(diff * diff)
            dg2_ref[...] += _colsum(dh2 * r2hat)
            db2_ref[...] += _colsum(dh2)
            dr2_ref[rows, :] = _ln_bwd(dh2, r2hat, rstd2, g2_ref[...])

    vec = _hbm_shape((1, D_MODEL), F32)
    c = _const2((1, D_MODEL))
    return pl.pallas_call(
        body, name="ffn_down_loss", grid=(s_len // tm,),
        in_specs=[_ffn_spec(tm), _vmem(), _rows(tm, D_MODEL), c, c, _rows(tm, D_MODEL)],
        out_specs=[_rows(tm, D_MODEL), c, c, c],
        out_shape=[_hbm_shape((s_len, D_MODEL), F32), vec, vec, vec],
        compiler_params=_params(48),
    )(act, wd, h1, g2, b2, target)


def _ffn_bwd_a(dr2, act, p_act, q_act, wd):
    s_len = dr2.shape[0]
    tm = _tile(s_len, 512)

    def body(dr2_ref, a_ref, p_ref, q_ref, wd_ref, dg_ref, du_ref, wire_ref, own_ref,
             dwd_ref, land_ref, send_sem, recv_sem):
        i = pl.program_id(0)

        @pl.when(i == 0)
        def _():
            dwd_ref[...] = jnp.zeros_like(dwd_ref)

        dfb = dr2_ref[...].astype(_MXU)
        for j in range(N_CHIP):
            da = _dot_nt(dfb, wd_ref[j])
            dg_ref[j] = (da * q_ref[j].astype(F32)).astype(_MXU)
            du_ref[j] = (da * p_ref[j].astype(F32)).astype(_MXU)
            dwd_ref[j * FF_SH:(j + 1) * FF_SH, :] += _dot_tn(a_ref[j], dfb)

        @pl.when(i == pl.num_programs(0) - 1)
        def _():
            _pair_reduce(dwd_ref, wire_ref, own_ref, land_ref, send_sem, recv_sem)

    sd = _hbm_shape((N_CHIP, s_len, FF_SH), _MXU)
    half = (N_CHIP, FF_SH // 2, D_MODEL)
    return pl.pallas_call(
        body, name="ffn_bwd_a", grid=(s_len // tm,),
        in_specs=[_rows(tm, D_MODEL), _ffn_spec(tm), _ffn_spec(tm), _ffn_spec(tm), _vmem()],
        out_specs=[_ffn_spec(tm), _ffn_spec(tm), _vmem(), _vmem()],
        out_shape=[sd, sd] + _pair_out_shapes(half),
        scratch_shapes=_pair_scratch((D_FF, D_MODEL), half),
        compiler_params=_params(61),
    )(dr2, act, p_act, q_act, wd)


def _ffn_bwd_g(dr2, dg, r1, g1, b1, wg, prev_wire):
    s_len = dr2.shape[0]
    tm = _tile(s_len, 512)

    def body(dr2_ref, dg_ref, r1_ref, g1_ref, b1_ref, wg_ref, pw_ref, dh1_ref, wire_ref, own_ref, pl_ref,
             dwg_ref, land_ref, send_sem, recv_sem, xl_ref, x_send, x_recv, x_flush):
        i = pl.program_id(0)
        exchange = _ChipExchange(pw_ref, xl_ref, x_send, x_recv)

        @pl.when(i == 0)
        def _():
            exchange.start()
            dwg_ref[...] = jnp.zeros_like(dwg_ref)

        h1, _, _ = _ln(r1_ref[...], g1_ref[...], b1_ref[...])
        h1b = h1.astype(_MXU)
        dh1 = ALPHA * dr2_ref[...]
        for j in range(N_CHIP):
            dgj = dg_ref[j]
            dh1 = dh1 + _dot(dgj, wg_ref[j])
            dwg_ref[j * FF_SH:(j + 1) * FF_SH, :] += _dot_tn(dgj, h1b)
        dh1_ref[...] = dh1

        @pl.when(i == pl.num_programs(0) - 1)
        def _():
            _pair_reduce(dwg_ref, wire_ref, own_ref, land_ref, send_sem, recv_sem)
            exchange.finish_to(pl_ref, x_flush)

    c = _const2((1, D_MODEL))
    half = (N_CHIP, FF_SH // 2, D_MODEL)
    return pl.pallas_call(
        body, name="ffn_bwd_g", grid=(s_len // tm,),
        in_specs=[_rows(tm, D_MODEL), _ffn_spec(tm), _rows(tm, D_MODEL), c, c, _vmem(), _vmem()],
        out_specs=[_rows(tm, D_MODEL), _vmem(), _vmem(), _hbm()],
        out_shape=[_hbm_shape((s_len, D_MODEL), F32)] + _pair_out_shapes(half) + [_ChipExchange.land_shape(prev_wire)],
        scratch_shapes=_pair_scratch((D_FF, D_MODEL), half) + _ChipExchange.scratch(prev_wire),
        compiler_params=_params(58),
    )(dr2, dg, r1, g1, b1, wg, prev_wire)


def _ffn_bwd_u(dh1a, du, r1, g1, b1, wu, prev_wire):
    s_len = dh1a.shape[0]
    tm = _tile(s_len, 512)

    def body(dh1_ref, du_ref, r1_ref, g1_ref, b1_ref, wu_ref, pw_ref,
             dr1_ref, wire_ref, own_ref, dg1_ref, db1_ref, pl_ref,
             dwu_ref, land_ref, send_sem, recv_sem, xl_ref, x_send, x_recv, x_flush):
        i = pl.program_id(0)
        exchange = _ChipExchange(pw_ref, xl_ref, x_send, x_recv)

        @pl.when(i == 0)
        def _():
            exchange.start()
            dwu_ref[...] = jnp.zeros_like(dwu_ref)
            dg1_ref[...] = jnp.zeros_like(dg1_ref)
            db1_ref[...] = jnp.zeros_like(db1_ref)

        h1, r1hat, rstd1 = _ln(r1_ref[...], g1_ref[...], b1_ref[...])
        h1b = h1.astype(_MXU)
        dh1 = dh1_ref[...]
        for j in range(N_CHIP):
            duj = du_ref[j]
            dh1 = dh1 + _dot(duj, wu_ref[j])
            dwu_ref[j * FF_SH:(j + 1) * FF_SH, :] += _dot_tn(duj, h1b)
        dg1_ref[...] += _colsum(dh1 * r1hat)
        db1_ref[...] += _colsum(dh1)
        dr1_ref[...] = _ln_bwd(dh1, r1hat, rstd1, g1_ref[...])

        @pl.when(i == pl.num_programs(0) - 1)
        def _():
            _pair_reduce(dwu_ref, wire_ref, own_ref, land_ref, send_sem, recv_sem)
            exchange.finish_to(pl_ref, x_flush)

    vec = _hbm_shape((1, D_MODEL), F32)
    c = _const2((1, D_MODEL))
    half = (N_CHIP, FF_SH // 2, D_MODEL)
    return pl.pallas_call(
        body, name="ffn_bwd_u", grid=(s_len // tm,),
        in_specs=[_rows(tm, D_MODEL), _ffn_spec(tm), _rows(tm, D_MODEL), c, c, _vmem(), _vmem()],
        out_specs=[_rows(tm, D_MODEL), _vmem(), _vmem(), c, c, _hbm()],
        out_shape=[_hbm_shape((s_len, D_MODEL), F32)] + _pair_out_shapes(half)
        + [vec, vec, _ChipExchange.land_shape(prev_wire)],
        scratch_shapes=_pair_scratch((D_FF, D_MODEL), half) + _ChipExchange.scratch(prev_wire),
        compiler_params=_params(58),
    )(dh1a, du, r1, g1, b1, wu, prev_wire)


def _outproj_bwd(dr1, mc, w_out):
    s_len = dr1.shape[0]
    tm = _tile(s_len, 512)

    def body(dr1_ref, mc_ref, w_ref, dmc_ref, wire_ref, own_ref, db_ref, dw_ref, land_ref, send_sem, recv_sem):
        i = pl.program_id(0)

        @pl.when(i == 0)
        def _():
            dw_ref[...] = jnp.zeros_like(dw_ref)
            db_ref[...] = jnp.zeros_like(db_ref)

        d = dr1_ref[...]
        db_ref[...] += _colsum(d)
        db16 = d.astype(_MXU)
        dmc_ref[...] = _dot_nt(db16, w_ref[...])
        dw_ref[...] += _dot_tn(mc_ref[...], db16)

        @pl.when(i == pl.num_programs(0) - 1)
        def _():
            _pair_reduce(dw_ref, wire_ref, own_ref, land_ref, send_sem, recv_sem)

    half = (N_CHIP, OUT_SH // 2, D_MODEL)
    return pl.pallas_call(
        body, name="outproj_bwd", grid=(s_len // tm,),
        in_specs=[_rows(tm, D_MODEL), _rows(tm, D_MODEL), _vmem()],
        out_specs=[_rows(tm, D_MODEL), _vmem(), _vmem(), _const2((1, D_MODEL))],
        out_shape=[_hbm_shape((s_len, D_MODEL), F32)] + _pair_out_shapes(half) + [_hbm_shape((1, D_MODEL), F32)],
        scratch_shapes=_pair_scratch((D_MODEL, D_MODEL), half),
        compiler_params=_params(48),
    )(dr1, mc, w_out)


def _mixer_bwd(q, k, v, su, sv, dmc, tc, t1, t2, sinks, sg, sb, sgu_w, sgu_bt, prev_wires):
    s_len = q.shape[0]
    nb = s_len // BLK
    per = next(p for p in (4, 2, 1) if nb % p == 0)
    steps = nb // per

    def body(q_ref, kc_ref, kp_ref, vc_ref, vp_ref, su_ref, sv_ref, dmc_ref,
             tc_ref, t1_ref, t2_ref, tcp_ref, t1p_ref, t2p_ref,
             sink_ref, lg_ref, lb_ref, w_ref, bt_ref, pw0_ref, pw1_ref,
             dq_ref, dkv_ref, dsuv_ref, dbq_ref, dbkv_ref, dbsuv_ref,
             dsink_ref, dlg_ref, dlb_ref, dw_ref, dbt_ref, pl0_ref, pl1_ref, carry_ref,
             xl0_ref, x0_send, x0_recv, x0_flush, xl1_ref, x1_send, x1_recv, x1_flush):
        i = pl.program_id(0)
        exchanges = [(_ChipExchange(pw0_ref, xl0_ref, x0_send, x0_recv), pl0_ref, x0_flush),
                     (_ChipExchange(pw1_ref, xl1_ref, x1_send, x1_recv), pl1_ref, x1_flush)]

        @pl.when(i == 0)
        def _():
            for exchange, _, _ in exchanges:
                exchange.start()

        @pl.when(i == 0)
        def _():
            for r in (dbq_ref, dbkv_ref, dbsuv_ref, dsink_ref, dlg_ref, dlb_ref, dw_ref, dbt_ref, carry_ref):
                r[...] = jnp.zeros_like(r)

        def emit_kv(fin, t):
            if t == 0:
                tables = (tcp_ref[...], t1p_ref[...], t2p_ref[...])
            else:
                before = slice((t - 1) * BLK, t * BLK)
                tables = (tc_ref[before, :], t1_ref[before, :], t2_ref[before, :])
            dk = _rope_bwd(fin[:, 0:KV_W], *tables)
            out = jnp.concatenate([dk, fin[:, KV_W:2 * KV_W]], axis=1)
            dkv_ref[t * BLK:(t + 1) * BLK, :] = out.astype(_MXU)
            dbkv_ref[...] += _colsum(out)

        def one_block(s):
            rows = slice(s * BLK, (s + 1) * BLK)
            before = slice((s - 1) * BLK, s * BLK)
            k_prev = kp_ref[...] if s == 0 else kc_ref[before, :]
            v_prev = vp_ref[...] if s == 0 else vc_ref[before, :]
            allowed_t = _band_mask_t(i == 0 if s == 0 else False)
            kb = jnp.concatenate([k_prev, kc_ref[rows, :]], axis=0)
            vb = jnp.concatenate([v_prev, vc_ref[rows, :]], axis=0)
            qv = q_ref[rows, :]
            dmc = dmc_ref[rows, :]
            dqs, dks, dvs, dsinks = [], [], [], []
            allowed_g = jnp.tile(allowed_t, (1, Q_PER_KV))
            for g in range(N_KV):
                heads = range(g * Q_PER_KV, (g + 1) * Q_PER_KV)
                kh = kb[:, g * HEAD_DIM:(g + 1) * HEAD_DIM]
                vh = vb[:, g * HEAD_DIM:(g + 1) * HEAD_DIM]
                q_g = jnp.concatenate([qv[:, h * HEAD_DIM:(h + 1) * HEAD_DIM] for h in heads], axis=0)
                do_g = jnp.concatenate([dmc[:, h * HEAD_DIM:(h + 1) * HEAD_DIM] for h in heads], axis=0).astype(_MXU)
                sink_g = jnp.concatenate([jnp.full((1, BLK), sink_ref[h], F32) for h in heads], axis=1)
                probs_t, psink = _attn_probs_t(kh, q_g, sink_g, allowed_g)
                dvs.append(_dot(probs_t.astype(_MXU), do_g))
                dp_t = _dot_nt(vh, do_g)
                rd = jnp.sum(probs_t * dp_t, axis=0, keepdims=True)
                ds_t = (probs_t * (dp_t - rd)).astype(_MXU)
                ps_rd = psink * rd
                for hh in range(Q_PER_KV):
                    dsinks.append(-jnp.sum(ps_rd[:, hh * BLK:(hh + 1) * BLK], axis=1, keepdims=True))
                dq_g = _dot_tn(ds_t, kh)
                dqs += [dq_g[hh * BLK:(hh + 1) * BLK, :] for hh in range(Q_PER_KV)]
                dks.append(_dot(ds_t, q_g))
            dq = _rope_bwd(jnp.concatenate(dqs, axis=1) * (HEAD_DIM ** -0.5),
                           tc_ref[rows, :], t1_ref[rows, :], t2_ref[rows, :])
            dq_ref[rows, :] = dq.astype(_MXU)
            dbq_ref[...] += _colsum(dq)
            dsink_ref[...] += _lane_put(dsinks, 128)
            contrib = jnp.concatenate(dks + dvs, axis=1)

            lg = lg_ref[...]
            u, du_dsu = _gelu_and_grad(su_ref[rows, :])
            gv, dgv_dsv = _gelu_and_grad(sv_ref[rows, :])
            mixed, vhat, rstd, vvb, wcs = _sgu_mix(gv, lg, lb_ref[...], w_ref, bt_ref)
            dsgu = dmc[:, ATTN_W:D_MODEL]
            dsu = dsgu * mixed * du_dsu
            dmixed = dsgu * u
            tri_t = lax.broadcasted_iota(jnp.int32, (BLK, BLK), 0)
            tri_s = lax.broadcasted_iota(jnp.int32, (BLK, BLK), 1)
            dvv, dbs = [], []
            for h in range(N_GRP):
                dm = dmixed[:, h * GRP_DIM:(h + 1) * GRP_DIM]
                dmb = dm.astype(_MXU)
                dbs.append(jnp.sum(dm, axis=1, keepdims=True))
                dw_ref[h] += jnp.where(tri_s <= tri_t, _dot_nt(dmb, vvb[:, h * GRP_DIM:(h + 1) * GRP_DIM]), 0.0)
                dvv.append(_dot_tn(wcs[h], dmb))
            dvv = jnp.concatenate(dvv, axis=1)
            dbt_ref[...] += _lane_put(dbs, 128)
            dlg_ref[...] += _colsum(dvv * vhat)
            dlb_ref[...] += _colsum(dvv)
            dsv = _ln_bwd(dvv, vhat, rstd, lg) * dgv_dsv
            dsuv = jnp.concatenate([dsu, dsv], axis=1)
            dsuv_ref[rows, :] = dsuv.astype(_MXU)
            dbsuv_ref[...] += _colsum(dsuv)
            return contrib

        @pl.when(i < steps)
        def _():
            contribs = [one_block(s) for s in range(per)]
            for t in range(per):
                top = carry_ref[...] if t == 0 else contribs[t - 1][BLK:2 * BLK, :]
                emit_kv(top + contribs[t][0:BLK, :], t)
            carry_ref[...] = contribs[per - 1][BLK:2 * BLK, :]

        @pl.when(i == steps)
        def _():
            emit_kv(carry_ref[...], 0)
            if per > 1:
                dkv_ref[BLK:per * BLK, :] = jnp.zeros(((per - 1) * BLK, 2 * KV_W), _MXU)
            for exchange, landed, flush_sem in exchanges:
                exchange.finish_to(landed, flush_sem)

    last = steps - 1
    cur = lambda w: pl.BlockSpec((per * BLK, w), lambda i: (jnp.minimum(i, last), 0))
    prev = lambda w: pl.BlockSpec((BLK, w), lambda i: (jnp.clip(per * i - 1, 0, nb - 1), 0))
    shifted = pl.BlockSpec((per * BLK, 2 * KV_W), lambda i: (i, 0))
    sd = _hbm_shape
    return pl.pallas_call(
        body, name="mixer_bwd", grid=(steps + 1,),
        in_specs=[cur(ATTN_W), cur(KV_W), prev(KV_W), cur(KV_W), prev(KV_W), cur(SGU_W), cur(SGU_W), cur(D_MODEL),
                  cur(128), cur(128), cur(128), prev(128), prev(128), prev(128),
                  _smem(), _const2((1, SGU_W)), _const2((1, SGU_W)), _const2((N_GRP, BLK, BLK)), _const2((BLK, N_GRP)),
                  _vmem(), _vmem()],
        out_specs=[cur(ATTN_W), shifted, cur(2 * SGU_W),
                   _const2((1, ATTN_W)), _const2((1, 2 * KV_W)), _const2((1, 2 * SGU_W)),
                   _const2((1, 128)), _const2((1, SGU_W)), _const2((1, SGU_W)),
                   _const2((N_GRP, BLK, BLK)), _const2((BLK, 128)), _hbm(), _hbm()],
        out_shape=[sd((s_len, ATTN_W), _MXU), sd((s_len + per * BLK, 2 * KV_W), _MXU), sd((s_len, 2 * SGU_W), _MXU),
                   sd((1, ATTN_W), F32), sd((1, 2 * KV_W), F32), sd((1, 2 * SGU_W), F32),
                   sd((1, 128), F32), sd((1, SGU_W), F32), sd((1, SGU_W), F32),
                   sd((N_GRP, BLK, BLK), F32), sd((BLK, 128), F32)]
        + [_ChipExchange.land_shape(w) for w in prev_wires],
        scratch_shapes=[pltpu.VMEM((BLK, 2 * KV_W), F32)] + _ChipExchange.scratch(prev_wires[0])
        + _ChipExchange.scratch(prev_wires[1]),
        compiler_params=_params(40),
    )(q, k, k, v, v, su, sv, dmc, tc, t1, t2, tc, t1, t2, sinks, sg, sb, sgu_w, sgu_bt, *prev_wires)


def _inproj_bwd(dq, dkv_late, dsuv, dr1, x, g0, b0, w_in):
    s_len = x.shape[0]
    tm = _tile(s_len, 512)
    assert tm % BLK == 0
    per = tm // BLK
    cuts = ((0, ATTN_W), (ATTN_W, ATTN_W + 2 * KV_W), (ATTN_W + 2 * KV_W, IN_W))

    def body(dq_ref, *rest):
        dkv_refs = rest[:per]
        dsuv_ref, dr1_ref, x_ref, g_ref, b_ref, w_ref, dx_ref, dw_ref, dg_ref, db_ref = rest[per:]
        i = pl.program_id(0)

        @pl.when(i == 0)
        def _():
            dw_ref[...] = jnp.zeros_like(dw_ref)
            dg_ref[...] = jnp.zeros_like(dg_ref)
            db_ref[...] = jnp.zeros_like(db_ref)

        h0, xhat, rstd = _ln(x_ref[...], g_ref[...], b_ref[...])
        h0b = h0.astype(_MXU)
        dh0 = ALPHA * dr1_ref[...]
        dkv = jnp.concatenate([r[...] for r in dkv_refs], axis=0)
        for (lo, hi), d in zip(cuts, (dq_ref[...], dkv, dsuv_ref[...])):
            dh0 = dh0 + _dot(d, w_ref[lo:hi, :])
            dw_ref[lo:hi, :] += _dot_tn(d, h0b)
        dg_ref[...] += _colsum(dh0 * xhat)
        db_ref[...] += _colsum(dh0)
        dx_ref[...] = _ln_bwd(dh0, xhat, rstd, g_ref[...])

    vec = _hbm_shape((1, D_MODEL), F32)
    c = _const2((1, D_MODEL))
    return pl.pallas_call(
        body, name="inproj_bwd", grid=(s_len // tm,),
        in_specs=[_rows(tm, ATTN_W)]
        + [pl.BlockSpec((BLK, 2 * KV_W), lambda i, b=b: (i * per + b + 1, 0)) for b in range(per)]
        + [_rows(tm, 2 * SGU_W), _rows(tm, D_MODEL), _rows(tm, D_MODEL), c, c, _vmem()],
        out_specs=[_rows(tm, D_MODEL), _vmem(), c, c],
        out_shape=[_hbm_shape((s_len, D_MODEL), F32), jax.ShapeDtypeStruct((IN_W, D_MODEL), F32), vec, vec],
        compiler_params=_params(48),
    )(dq, *[dkv_late] * per, dsuv, dr1, x, g0, b0, w_in)


def _place():
    x, y, c = (lax.axis_index(a) for a in MESH_AXES)
    chips = [(1 - x, y), (x, 1 - y), (1 - x, 1 - y)]
    return x, y, c, chips


class _Gather:
    def __init__(self, ins, outs, send_sems, recv_sems, spans=None):
        self.ins, self.outs, self.send_sems, self.recv_sems = ins, outs, send_sems, recv_sems
        self.n = len(ins)
        self.spans = spans or [(0, r.shape[0]) for r in ins]
        self.halves = [(hi - lo) // 2 for lo, hi in self.spans]

    def _copy(self, k, t, slot, half, to):
        rows = pl.ds(pl.multiple_of(self.spans[t][0] + half * self.halves[t], 16), self.halves[t])
        piece = self.outs[t].at[slot, rows, :]
        return pltpu.make_async_remote_copy(src_ref=piece, dst_ref=piece, send_sem=self.send_sems.at[k],
                                            recv_sem=self.recv_sems.at[k], device_id=to, device_id_type=MESH)

    def _chip_copy(self, t, d, slot):
        x, y, c, chips = _place()
        return self._copy(3 * t + d, t, slot, c, (chips[d][0], chips[d][1], c))

    def _pass_copy(self, t, d, half):
        x, y, c, chips = _place()
        return self._copy(3 * self.n + 3 * t + d, t, 2 * chips[d][0] + chips[d][1], half, (x, y, 1 - c))

    def start(self):
        x, y, c, chips = _place()
        me = 2 * x + y
        for t in range(self.n):
            lo, hi = self.spans[t]
            self.outs[t][me, lo:hi, :] = self.ins[t][lo:hi, :].astype(_WIRE)
        for t in range(self.n):
            for d in range(3):
                self._chip_copy(t, d, me).start()

    def pass_on(self):
        x, y, c, chips = _place()
        for t in range(self.n):
            for d in range(3):
                self._chip_copy(t, d, 2 * chips[d][0] + chips[d][1]).wait_recv()
                self._pass_copy(t, d, c).start()

    def finish(self):
        x, y, c, chips = _place()
        me = 2 * x + y
        for t in range(self.n):
            for d in range(3):
                self._pass_copy(t, d, 1 - c).wait_recv()
        for t in range(self.n):
            for d in range(3):
                self._chip_copy(t, d, me).wait_send()
                self._pass_copy(t, d, c).wait_send()

    @staticmethod
    def out_shapes(shards, make=jax.ShapeDtypeStruct):
        return [make((N_CHIP,) + s.shape, _WIRE) for s in shards]

    @staticmethod
    def sems(n):
        return [pltpu.SemaphoreType.DMA((6 * n,)), pltpu.SemaphoreType.DMA((6 * n,))]


_FLUSHES_EARLY, _FLUSHES = 5, 8


class _GatherPlan:
    def __init__(self, pieces):
        self.shards = [p[0] for p in pieces]
        self.spans = [p[1] for p in pieces]
        self.earlier = [p[2] for p in pieces]
        self.n = len(pieces)
        self.carried = [t for t in range(self.n) if self.earlier[t] is not None]

    def operands(self):
        return self.shards + [self.earlier[t] for t in self.carried]

    def in_specs(self):
        return [_vmem()] * self.n + [_hbm()] * len(self.carried)

    def out_specs(self):
        return [_hbm()] * self.n

    def out_shapes(self):
        return _Gather.out_shapes(self.shards, _hbm_shape)

    def scratch(self):
        return ([pltpu.VMEM((N_CHIP,) + s.shape, _WIRE) for s in self.shards] + _Gather.sems(self.n)
                + [pltpu.SemaphoreType.DMA((_FLUSHES * self.n,)), pltpu.SemaphoreType.DMA((max(len(self.carried), 1),))])

    def bind(self, in_refs, out_refs, scratch_refs):
        plan = self
        shard_refs, earlier_refs = in_refs[:self.n], in_refs[self.n:]
        bufs = scratch_refs[:self.n]
        send_sems, recv_sems, flush_sems, carry_sems = scratch_refs[self.n:self.n + 4]
        gather = _Gather(shard_refs, bufs, send_sems, recv_sems, self.spans)

        def carry_copy(k):
            t = plan.carried[k]
            lo = plan.spans[t][0]
            return pltpu.make_async_copy(earlier_refs[k].at[:, 0:lo, :], bufs[t].at[:, 0:lo, :], carry_sems.at[k])

        def flushes(t, late):
            x, y, c, chips = _place()
            lo, hi = plan.spans[t]
            half = (hi - lo) // 2
            others = [2 * chips[d][0] + chips[d][1] for d in range(3)]

            def half_rows(h):
                return pl.ds(pl.multiple_of(lo + h * half, 16), half)

            if late:
                parts = [(slot, half_rows(1 - c)) for slot in others]
            else:
                parts = [(2 * x + y, pl.ds(lo, hi - lo))] + [(slot, half_rows(c)) for slot in others]
                if lo:
                    parts.append((slice(None), pl.ds(0, lo)))
            first = _FLUSHES_EARLY if late else 0
            return [pltpu.make_async_copy(bufs[t].at[slot, rows, :], out_refs[t].at[slot, rows, :],
                                          flush_sems.at[_FLUSHES * t + first + k]) for k, (slot, rows) in enumerate(parts)]

        class Bound:
            @staticmethod
            def start():
                for k in range(len(plan.carried)):
                    carry_copy(k).start()
                gather.start()

            @staticmethod
            def pass_on():
                gather.pass_on()
                for k in range(len(plan.carried)):
                    carry_copy(k).wait()
                for t in range(plan.n):
                    for cp in flushes(t, late=False):
                        cp.start()

            @staticmethod
            def finish():
                gather.finish()
                for t in range(plan.n):
                    for cp in flushes(t, late=True):
                        cp.start()
                for t in range(plan.n):
                    for cp in flushes(t, late=False) + flushes(t, late=True):
                        cp.wait()

        return Bound


def _flush(bufs, hbm_outs, sems):
    copies = [pltpu.make_async_copy(b, o, sems.at[k]) for k, (b, o) in enumerate(zip(bufs, hbm_outs))]
    for cp in copies:
        cp.start()
    for cp in copies:
        cp.wait()


def _gather_weights(shards):
    n = len(shards)

    def body(*refs):
        gather = _Gather(refs[:n], refs[n:2 * n], refs[2 * n], refs[2 * n + 1])
        gather.start()
        gather.pass_on()
        gather.finish()

    return pl.pallas_call(
        body, name="gather_weights",
        in_specs=[_vmem()] * n, out_specs=[_vmem()] * n,
        out_shape=_Gather.out_shapes(shards), scratch_shapes=_Gather.sems(n),
        compiler_params=pltpu.CompilerParams(vmem_limit_bytes=32 * MIB),
    )(*shards)


class _ChipExchange:
    def __init__(self, wire_ref, land_ref, send_sems, recv_sems):
        self.wire, self.land, self.send_sems, self.recv_sems = wire_ref, land_ref, send_sems, recv_sems

    def _copy(self, d):
        x, y, c, chips = _place()
        return pltpu.make_async_remote_copy(
            src_ref=self.wire.at[2 * chips[d][0] + chips[d][1]], dst_ref=self.land.at[d],
            send_sem=self.send_sems.at[d], recv_sem=self.recv_sems.at[d],
            device_id=(chips[d][0], chips[d][1], c), device_id_type=MESH)

    def start(self):
        for d in range(3):
            self._copy(d).start()

    def wait_recv(self):
        for d in range(3):
            self._copy(d).wait_recv()

    def wait_send(self):
        for d in range(3):
            self._copy(d).wait_send()

    def finish_to(self, hbm_out, flush_sem):
        self.wait_recv()
        _flush([self.land], [hbm_out], flush_sem)
        self.wait_send()

    @staticmethod
    def land_shape(wire):
        return _hbm_shape((3,) + wire.shape[1:], wire.dtype)

    @staticmethod
    def sems():
        return [pltpu.SemaphoreType.DMA((3,)), pltpu.SemaphoreType.DMA((3,))]

    @staticmethod
    def scratch(wire):
        return ([pltpu.VMEM((3,) + wire.shape[1:], wire.dtype)] + _ChipExchange.sems() + [pltpu.SemaphoreType.DMA((1,))])


def _pair_out_shapes(half_shape):
    return [jax.ShapeDtypeStruct(half_shape, _WIRE), jax.ShapeDtypeStruct(half_shape[1:], F32)]


def _pair_scratch(acc_shape, half_shape):
    return [pltpu.VMEM(acc_shape, F32), pltpu.VMEM(half_shape, _WIRE),
            pltpu.SemaphoreType.DMA((N_CHIP,)), pltpu.SemaphoreType.DMA((N_CHIP,))]


def _pair_reduce(acc_ref, wire_ref, own_ref, land_ref, send_sems, recv_sems):
    rh = land_ref.shape[1]
    x, y, c, _ = _place()
    me = 2 * x + y
    copies = []
    for j in range(N_CHIP):
        def cast(r, carry, j=j):
            dst = pl.ds(pl.multiple_of(r * ROW_CHUNK, ROW_CHUNK), ROW_CHUNK)
            src = pl.ds(pl.multiple_of((2 * j + 1 - c) * rh + r * ROW_CHUNK, 8), ROW_CHUNK)
            wire_ref[j, dst, :] = acc_ref[src, :].astype(_WIRE)
            return carry

        lax.fori_loop(0, rh // ROW_CHUNK, cast, 0)
        cp = pltpu.make_async_remote_copy(src_ref=wire_ref.at[j], dst_ref=land_ref.at[j], send_sem=send_sems.at[j],
                                          recv_sem=recv_sems.at[j], device_id=(x, y, 1 - c), device_id_type=MESH)
        cp.start()
        copies.append(cp)
    for j in range(N_CHIP):
        copies[j].wait()

        def chunk(r, carry, j=j):
            theirs = pl.ds(pl.multiple_of(r * ROW_CHUNK, ROW_CHUNK), ROW_CHUNK)
            mine = pl.ds(pl.multiple_of((2 * j + c) * rh + r * ROW_CHUNK, 8), ROW_CHUNK)
            wire_ref[j, theirs, :] = (acc_ref[mine, :] + land_ref[j, theirs, :].astype(F32)).astype(_WIRE)
            return carry

        lax.fori_loop(0, rh // ROW_CHUNK, chunk, 0)

    def own_chunk(r, carry):
        theirs = pl.ds(pl.multiple_of(r * ROW_CHUNK, ROW_CHUNK), ROW_CHUNK)
        mine = pl.ds(pl.multiple_of((2 * me + c) * rh + r * ROW_CHUNK, 8), ROW_CHUNK)
        own_ref[theirs, :] = acc_ref[mine, :] + land_ref[me, theirs, :].astype(F32)
        return carry

    lax.fori_loop(0, rh // ROW_CHUNK, own_chunk, 0)


def _grad_finish(last_acc, lands, owns, small):
    n = len(owns) + 1
    halves = [last_acc.shape[0] // (2 * N_CHIP)] + [w.shape[1] for w in lands]
    widths = [last_acc.shape[1]] + [a.shape[1] for a in owns]
    small_body, small_scratch = _small_allreduce_parts()
    ns = len(small)

    def body(*refs):
        acc0, land, own = refs[0], (None,) + refs[1:n], (None,) + refs[n:2 * n - 1]
        refs = refs[2 * n - 1:]
        small_in, g_out, small_out = refs[:ns], refs[ns:ns + n], refs[ns + n:ns + n + 2]
        refs = refs[ns + n + 2:]
        pland0, wire0, land0, own0 = refs[0:4]
        p_send, p_recv, x_send, x_recv, pair_send, pair_recv = refs[4:10]
        g, flush_sems = refs[10:10 + n], refs[10 + n]
        small_refs = refs[11 + n:]
        land = (land0,) + land[1:]
        own = (own0,) + own[1:]
        x, y, c, chips = _place()
        me = 2 * x + y
        exchange = _ChipExchange(wire0, land0, x_send, x_recv)

        def half_rows(t, half):
            return pl.ds(pl.multiple_of(half * halves[t], 8), halves[t])

        def pair_copy(t, half):
            rows = g[t].at[half_rows(t, half), :]
            return pltpu.make_async_remote_copy(src_ref=rows, dst_ref=rows, send_sem=pair_send.at[t],
                                                recv_sem=pair_recv.at[t], device_id=(x, y, 1 - c), device_id_type=MESH)

        def flush(t):
            return pltpu.make_async_copy(g[t], g_out[t], flush_sems.at[t])

        small_rounds = small_body(*small_in, *small_out, *small_refs)
        next(small_rounds)
        _pair_reduce(acc0, wire0, own0, pland0, p_send, p_recv)
        next(small_rounds)
        exchange.start()

        for t in list(range(1, n)) + [0]:
            if t == 0:
                for done in range(1, n):
                    pair_copy(done, 1 - c).wait_recv()
                    flush(done).start()
                exchange.wait_recv()
            if t == min(2, n - 1):
                next(small_rounds)
            if t == min(4, n - 1):
                next(small_rounds, None)

            def chunk(r, carry, t=t):
                src = pl.ds(pl.multiple_of(r * ROW_CHUNK, ROW_CHUNK), ROW_CHUNK)
                dst = pl.ds(pl.multiple_of(c * halves[t] + r * ROW_CHUNK, 8), ROW_CHUNK)
                s = own[t][src, :]
                for d in range(3):
                    s = s + land[t][d, src, :].astype(F32)
                g[t][dst, :] = s
                return carry

            lax.fori_loop(0, halves[t] // ROW_CHUNK, chunk, 0)
            pair_copy(t, c).start()
        pair_copy(0, 1 - c).wait_recv()
        flush(0).start()
        for t in range(n):
            pair_copy(t, c).wait_send()
        exchange.wait_send()
        for t in range(n):
            flush(t).wait()

    half0 = (halves[0], widths[0])
    shapes = [(2 * h, w) for h, w in zip(halves, widths)]
    return pl.pallas_call(
        body, name="grad_finish",
        in_specs=[_vmem()] * (2 * n - 1 + ns), out_specs=[_hbm()] * n + [_vmem()] * 2,
        out_shape=[_hbm_shape(s, F32) for s in shapes] + [jax.ShapeDtypeStruct(s, F32) for s in _SMALL_OUT_DIMS],
        scratch_shapes=[pltpu.VMEM((N_CHIP,) + half0, _WIRE), pltpu.VMEM((N_CHIP,) + half0, _WIRE),
                        pltpu.VMEM((3,) + half0, _WIRE), pltpu.VMEM(half0, F32)]
        + [pltpu.SemaphoreType.DMA((N_CHIP,)), pltpu.SemaphoreType.DMA((N_CHIP,))]
        + _ChipExchange.sems()
        + [pltpu.SemaphoreType.DMA((n,)), pltpu.SemaphoreType.DMA((n,))]
        + [pltpu.VMEM(s, F32) for s in shapes] + [pltpu.SemaphoreType.DMA((n,))]
        + small_scratch,
        compiler_params=pltpu.CompilerParams(vmem_limit_bytes=56 * MIB),
    )(last_acc, *lands, *owns, *small)


_SMALL = ("ln_in_g", "ln_in_b", "b_in", "attn_sinks", "sgu_ln_g", "sgu_ln_b", "sgu_w", "sgu_b", "b_out",
          "ln_mix_g", "ln_mix_b", "ln_ffn_g", "ln_ffn_b")
_VEC_ROW = dict(ln_in_g=0, ln_in_b=1, b_in=2, attn_sinks=4, sgu_ln_g=5, sgu_ln_b=6, b_out=7, ln_mix_g=8, ln_mix_b=9,
                ln_ffn_g=10, ln_ffn_b=11)
_LOSS_ROW = 12
_VEC_ROWS = 16
_MAT_ROWS = N_GRP * BLK + BLK


_SMALL_IN = ("ln_in_g", "ln_in_b", "bq", "bkv", "bsuv", "sink", "sgu_ln_g", "sgu_ln_b", "sgu_w", "sgu_bt", "b_out",
             "ln_mix_g", "ln_mix_b", "ln_ffn_g", "ln_ffn_b", "loss")
_SMALL_OUT_DIMS = ((_VEC_ROWS, D_MODEL), (_MAT_ROWS, 128))


def _small_allreduce_parts():
    n_in = len(_SMALL_IN)

    def body(*refs):
        (g_ln_in_g, g_ln_in_b, g_bq, g_bkv, g_bsuv, g_sink, g_sln_g, g_sln_b, g_sw, g_sbt, g_bout,
         g_lmg, g_lmb, g_lfg, g_lfb, g_loss) = refs[:n_in]
        out_a, out_b = refs[n_in:n_in + 2]
        (buf_a, buf_b, pair_a, pair_b, stage_a, stage_b, tot_a, tot_b,
         p1_send, p1_recv, x_send, x_recv, p2_send, p2_recv) = refs[n_in + 2:]
        x, y, c, chips = _place()
        me = 2 * x + y
        sibling = (x, y, 1 - c)
        half_a, half_b = _VEC_ROWS // 2, _MAT_ROWS // 2

        buf_a[...] = jnp.zeros_like(buf_a)
        for row, ref in ((0, g_ln_in_g), (1, g_ln_in_b), (7, g_bout), (8, g_lmg), (9, g_lmb), (10, g_lfg), (11, g_lfb),
                         (_LOSS_ROW, g_loss)):
            buf_a[row:row + 1, :] = ref[...]
        buf_a[2:3, 0:ATTN_W] = g_bq[...]
        buf_a[2:3, ATTN_W:ATTN_W + 2 * KV_W] = g_bkv[...]
        buf_a[2:3, ATTN_W + 2 * KV_W:D_MODEL] = g_bsuv[:, 0:2 * KV_W]
        buf_a[3:4, 0:2 * SGU_W - 2 * KV_W] = g_bsuv[:, 2 * KV_W:2 * SGU_W]
        buf_a[4:5, 0:128] = g_sink[...]
        buf_a[5:6, 0:SGU_W] = g_sln_g[...]
        buf_a[6:7, 0:SGU_W] = g_sln_b[...]
        for h in range(N_GRP):
            buf_b[h * BLK:(h + 1) * BLK, :] = g_sw[h]
        buf_b[N_GRP * BLK:_MAT_ROWS, :] = g_sbt[...]

        def remote(src, dst, send_sem, recv_sem, to):
            return pltpu.make_async_remote_copy(src_ref=src, dst_ref=dst, send_sem=send_sem, recv_sem=recv_sem,
                                                device_id=to, device_id_type=MESH)

        first = [remote(buf_a, pair_a, p1_send.at[0], p1_recv.at[0], sibling),
                 remote(buf_b, pair_b, p1_send.at[1], p1_recv.at[1], sibling)]
        for cp in first:
            cp.start()
        yield
        for cp in first:
            cp.wait()
        rows_a = pl.ds(pl.multiple_of(c * half_a, 8), half_a)
        rows_b = pl.ds(pl.multiple_of(c * half_b, 8), half_b)
        stage_a[me] = buf_a[rows_a, :] + pair_a[rows_a, :]
        stage_b[me] = buf_b[rows_b, :] + pair_b[rows_b, :]

        def chip_copies(d):
            to = (chips[d][0], chips[d][1], c)
            return [remote(stage_a.at[me], stage_a.at[me], x_send.at[2 * d], x_recv.at[2 * d], to),
                    remote(stage_b.at[me], stage_b.at[me], x_send.at[2 * d + 1], x_recv.at[2 * d + 1], to)]

        def chip_arrivals(d):
            slot = 2 * chips[d][0] + chips[d][1]
            to = (chips[d][0], chips[d][1], c)
            return [remote(stage_a.at[slot], stage_a.at[slot], x_send.at[2 * d], x_recv.at[2 * d], to),
                    remote(stage_b.at[slot], stage_b.at[slot], x_send.at[2 * d + 1], x_recv.at[2 * d + 1], to)]

        for d in range(3):
            for cp in chip_copies(d):
                cp.start()
        yield
        for d in range(3):
            for cp in chip_arrivals(d):
                cp.wait_recv()
        tot_a[rows_a, :] = ((stage_a[0] + stage_a[1]) + stage_a[2]) + stage_a[3]
        tot_b[rows_b, :] = ((stage_b[0] + stage_b[1]) + stage_b[2]) + stage_b[3]

        second = [remote(tot_a.at[rows_a, :], tot_a.at[rows_a, :], p2_send.at[0], p2_recv.at[0], sibling),
                  remote(tot_b.at[rows_b, :], tot_b.at[rows_b, :], p2_send.at[1], p2_recv.at[1], sibling)]
        for cp in second:
            cp.start()
        yield
        other_a = pl.ds(pl.multiple_of((1 - c) * half_a, 8), half_a)
        other_b = pl.ds(pl.multiple_of((1 - c) * half_b, 8), half_b)
        remote(tot_a.at[other_a, :], tot_a.at[other_a, :], p2_send.at[0], p2_recv.at[0], sibling).wait_recv()
        remote(tot_b.at[other_b, :], tot_b.at[other_b, :], p2_send.at[1], p2_recv.at[1], sibling).wait_recv()
        for cp in second:
            cp.wait_send()
        for d in range(3):
            for cp in chip_copies(d):
                cp.wait_send()
        out_a[...] = tot_a[...]
        out_b[...] = tot_b[...]

    vec = pltpu.VMEM((_VEC_ROWS, D_MODEL), F32)
    mat = pltpu.VMEM((_MAT_ROWS, 128), F32)
    scratch = [vec, mat, vec, mat, pltpu.VMEM((N_CHIP, _VEC_ROWS // 2, D_MODEL), F32),
               pltpu.VMEM((N_CHIP, _MAT_ROWS // 2, 128), F32), vec, mat,
               pltpu.SemaphoreType.DMA((2,)), pltpu.SemaphoreType.DMA((2,)), pltpu.SemaphoreType.DMA((6,)),
               pltpu.SemaphoreType.DMA((6,)), pltpu.SemaphoreType.DMA((2,)), pltpu.SemaphoreType.DMA((2,))]
    return body, scratch


def _small_adamw(tot_a, tot_b, params):
    shapes = [params[nm][0].shape for nm in _SMALL]

    def body(*refs):
        ta, tb = refs[:2]
        prm = refs[2:2 + 3 * len(_SMALL)]
        outs = refs[2 + 3 * len(_SMALL):]

        def grad_of(k, name):
            if name == "sgu_w":
                return [tb[h * BLK:(h + 1) * BLK, :] for h in range(N_GRP)]
            if name == "sgu_b":
                return jnp.transpose(tb[N_GRP * BLK:_MAT_ROWS, :])[0:N_GRP, :]
            row = _VEC_ROW[name]
            if name == "b_in":
                return jnp.concatenate([ta[row:row + 1, :], ta[row + 1:row + 2, 0:IN_W - D_MODEL]], axis=1)
            return ta[row:row + 1, 0:shapes[k][-1]]

        for k, name in enumerate(_SMALL):
            w_ref, m_ref, v_ref = prm[3 * k:3 * k + 3]
            g_out, d_out, m_out, v_out = outs[4 * k:4 * k + 4]
            g = grad_of(k, name)
            if name == "sgu_w":
                for h in range(N_GRP):
                    d_, m_, v_ = _adamw_math(w_ref[h], g[h], m_ref[h], v_ref[h])
                    g_out[h], d_out[h], m_out[h], v_out[h] = g[h], d_, m_, v_
            else:
                d_, m_, v_ = _adamw_math(w_ref[...], g, m_ref[...], v_ref[...])
                g_out[...], d_out[...], m_out[...], v_out[...] = g, d_, m_, v_
        outs[-1][...] = jnp.sum(ta[_LOSS_ROW:_LOSS_ROW + 1, :], axis=1, keepdims=True) * (0.5 / D_MODEL)

    ins = [tot_a, tot_b] + [_in_hbm(a) for nm in _SMALL for a in params[nm]]
    out_dims = [s for s in shapes for _ in range(4)] + [(1, 1)]
    res = pl.pallas_call(
        body, name="small_adamw", grid=(1,),
        in_specs=[_const2(a.shape) for a in ins], out_specs=[_const2(s) for s in out_dims],
        out_shape=[_hbm_shape(s, F32) for s in out_dims],
        compiler_params=_params(32),
    )(*ins)
    return {nm: tuple(res[4 * k:4 * k + 4]) for k, nm in enumerate(_SMALL)}, res[-1]


def _adamw_math(w, g, m, v):
    m = ADAM_B1 * m + (1.0 - ADAM_B1) * g
    v = ADAM_B2 * v + (1.0 - ADAM_B2) * (g * g)
    m_hat = m / (1.0 - ADAM_B1 ** ADAM_STEP)
    v_hat = v / (1.0 - ADAM_B2 ** ADAM_STEP)
    delta = -ADAM_LR * (m_hat / (jnp.sqrt(v_hat) + ADAM_EPS) + ADAM_WD * w)
    return delta, m, v


ADAMW_STEPS = 4


def _adamw(name, groups):
    k = len(groups)

    def body(*refs):
        for i in range(k):
            w_ref, g_ref, m_ref, v_ref = refs[4 * i:4 * i + 4]
            g = g_ref[...]
            for o_ref, o in zip(refs[4 * k + 4 * i:4 * k + 4 * i + 4], (g,) + _adamw_math(w_ref[...], g, m_ref[...], v_ref[...])):
                o_ref[...] = o

    specs = []
    for grp in groups:
        rows, cols = grp[0].shape
        assert rows % (8 * ADAMW_STEPS) == 0, rows
        specs += [pl.BlockSpec((rows // ADAMW_STEPS, cols), lambda i: (i, 0))] * 4
    res = pl.pallas_call(
        body, name=name, grid=(ADAMW_STEPS,), in_specs=specs, out_specs=specs,
        out_shape=[_hbm_shape(grp[0].shape, F32) for grp in groups for _ in range(4)],
        compiler_params=_params(56),
    )(*[_in_hbm(a) for grp in groups for a in grp])
    return [res[4 * i:4 * i + 4] for i in range(k)]


def kernel(x, positions, ln_in_g, ln_in_b, w_in, b_in, attn_sinks, sgu_ln_g, sgu_ln_b, sgu_w, sgu_b, w_out, b_out, ln_mix_g, ln_mix_b, w_gate, w_up, w_down, ln_ffn_g, ln_ffn_b, loss_target, m_ln_in_g, m_ln_in_b, m_w_in, m_b_in, m_attn_sinks, m_sgu_ln_g, m_sgu_ln_b, m_sgu_w, m_sgu_b, m_w_out, m_b_out, m_ln_mix_g, m_ln_mix_b, m_w_gate, m_w_up, m_w_down, m_ln_ffn_g, m_ln_ffn_b, v_ln_in_g, v_ln_in_b, v_w_in, v_b_in, v_attn_sinks, v_sgu_ln_g, v_sgu_ln_b, v_sgu_w, v_sgu_b, v_w_out, v_b_out, v_ln_mix_g, v_ln_mix_b, v_w_gate, v_w_up, v_w_down, v_ln_ffn_g, v_ln_ffn_b):
    weights = dict(ln_in_g=ln_in_g, ln_in_b=ln_in_b, w_in=w_in, b_in=b_in, attn_sinks=attn_sinks, sgu_ln_g=sgu_ln_g,
                   sgu_ln_b=sgu_ln_b, sgu_w=sgu_w, sgu_b=sgu_b, w_out=w_out, b_out=b_out, ln_mix_g=ln_mix_g,
                   ln_mix_b=ln_mix_b, w_gate=w_gate, w_up=w_up, w_down=w_down, ln_ffn_g=ln_ffn_g, ln_ffn_b=ln_ffn_b)
    mom_m = dict(ln_in_g=m_ln_in_g, ln_in_b=m_ln_in_b, w_in=m_w_in, b_in=m_b_in, attn_sinks=m_attn_sinks,
                 sgu_ln_g=m_sgu_ln_g, sgu_ln_b=m_sgu_ln_b, sgu_w=m_sgu_w, sgu_b=m_sgu_b, w_out=m_w_out, b_out=m_b_out,
                 ln_mix_g=m_ln_mix_g, ln_mix_b=m_ln_mix_b, w_gate=m_w_gate, w_up=m_w_up, w_down=m_w_down,
                 ln_ffn_g=m_ln_ffn_g, ln_ffn_b=m_ln_ffn_b)
    mom_v = dict(ln_in_g=v_ln_in_g, ln_in_b=v_ln_in_b, w_in=v_w_in, b_in=v_b_in, attn_sinks=v_attn_sinks,
                 sgu_ln_g=v_sgu_ln_g, sgu_ln_b=v_sgu_ln_b, sgu_w=v_sgu_w, sgu_b=v_sgu_b, w_out=v_w_out, b_out=v_b_out,
                 ln_mix_g=v_ln_mix_g, ln_mix_b=v_ln_mix_b, w_gate=v_w_gate, w_up=v_w_up, w_down=v_w_down,
                 ln_ffn_g=v_ln_ffn_g, ln_ffn_b=v_ln_ffn_b)
    order = list(weights)
    big = ("w_in", "w_out", "w_gate", "w_up", "w_down")

    s_len = x.shape[1]
    xs = _in_hbm(x.reshape(s_len, D_MODEL))
    tgt = _in_hbm(loss_target.reshape(s_len, D_MODEL))
    pos_row = _in_hbm(positions.reshape(1, s_len))
    g0, b0 = _in_hbm(ln_in_g.reshape(1, D_MODEL)), _in_hbm(ln_in_b.reshape(1, D_MODEL))
    sinks = attn_sinks.reshape(N_Q)
    sgu_w3 = _in_hbm(sgu_w.reshape(N_GRP, BLK, BLK))
    sgu_bt = _in_hbm(sgu_b.reshape(N_GRP, BLK).T)
    b_in, b_out, sgu_ln_g, sgu_ln_b, ln_mix_g, ln_mix_b, ln_ffn_g, ln_ffn_b = (
        _in_hbm(a) for a in (b_in, b_out, sgu_ln_g, sgu_ln_b, ln_mix_g, ln_mix_b, ln_ffn_g, ln_ffn_b))

    col_sharded = ("w_in", "w_gate", "w_up")

    def rowmajor(name, a):
        return jnp.swapaxes(a[0], 0, 1) if name in col_sharded else a[0]

    def as_given(name, a):
        return (jnp.swapaxes(a, 0, 1) if name in col_sharded else a)[None]

    shards = [rowmajor(n, weights[n]) for n in big]
    (gw_in,) = _gather_weights(shards[0:1])
    w_in_full = gw_in.reshape(IN_W, D_MODEL)

    sh_out, sh_gate, sh_up, sh_down = shards[1:]
    *acts, gw_out, gw_gate0 = _ln_inproj(xs, pos_row, g0, b0, w_in_full, b_in, _GatherPlan(
        [(sh_out, (0, OUT_SH), None), (sh_gate, (0, GATE_CUT), None)]))
    q, k, v, su, sv, tc, t1, t2 = (_in_hbm(a) for a in acts)
    mc, gw_gate, gw_up0 = _mixer_fwd(q, k, v, su, sv, sinks, sgu_ln_g, sgu_ln_b, sgu_w3, sgu_bt, _GatherPlan(
        [(sh_gate, (GATE_CUT, FF_SH), gw_gate0), (sh_up, (0, UP_CUT), None)]))
    mc = _in_hbm(mc)
    w_out_full = gw_out.reshape(D_MODEL, D_MODEL)
    r1, gw_up = _outproj(mc, w_out_full, b_out, xs, g0, b0, _GatherPlan([(sh_up, (UP_CUT, FF_SH), gw_up0)]))
    r1 = _in_hbm(r1)
    act, p_act, q_act, h1, gw_down = _ffn_up(r1, ln_mix_g, ln_mix_b, gw_gate, gw_up,
                                             _GatherPlan([(sh_down, (0, FF_SH), None)]))
    act, p_act, q_act = _in_hbm(act), _in_hbm(p_act), _in_hbm(q_act)
    dr2, loss_cols, d_ln_ffn_g, d_ln_ffn_b = _ffn_down_loss(act, gw_down, _in_hbm(h1), ln_ffn_g, ln_ffn_b, tgt)
    dr2 = _in_hbm(dr2)

    dg, du, wire_down, own_down = _ffn_bwd_a(dr2, act, p_act, q_act, gw_down)
    dh1a, wire_gate, own_gate, land_down = _ffn_bwd_g(dr2, _in_hbm(dg), r1, ln_mix_g, ln_mix_b, gw_gate, wire_down)
    dr1, wire_up, own_up, d_ln_mix_g, d_ln_mix_b, land_gate = _ffn_bwd_u(_in_hbm(dh1a), _in_hbm(du), r1, ln_mix_g,
                                                                         ln_mix_b, gw_up, wire_gate)
    dr1 = _in_hbm(dr1)
    dmc, wire_out, own_out, d_b_out = _outproj_bwd(dr1, mc, w_out_full)
    (dq, dkv, dsuv, dbq, dbkv, dbsuv, d_sink, d_sgu_ln_g, d_sgu_ln_b, d_sgu_w, d_sgu_bt, land_up, land_out) = _mixer_bwd(
        q, k, v, su, sv, _in_hbm(dmc), tc, t1, t2, sinks, sgu_ln_g, sgu_ln_b, sgu_w3, sgu_bt, [wire_up, wire_out])
    grad_x, acc_in, d_ln_in_g, d_ln_in_b = _inproj_bwd(_in_hbm(dq), _in_hbm(dkv), _in_hbm(dsuv), dr1, xs, g0, b0,
                                                       w_in_full)

    small_local = dict(
        ln_in_g=d_ln_in_g, ln_in_b=d_ln_in_b, bq=dbq, bkv=dbkv, bsuv=dbsuv, sink=d_sink, sgu_ln_g=d_sgu_ln_g,
        sgu_ln_b=d_sgu_ln_b, sgu_w=d_sgu_w, sgu_bt=d_sgu_bt, b_out=d_b_out, ln_mix_g=d_ln_mix_g, ln_mix_b=d_ln_mix_b,
        ln_ffn_g=d_ln_ffn_g, ln_ffn_b=d_ln_ffn_b, loss=loss_cols)
    *reduced, tot_a, tot_b = _grad_finish(acc_in, [land_out, land_gate, land_up, land_down],
                                          [own_out, own_gate, own_up, own_down], [small_local[nm] for nm in _SMALL_IN])
    small_shape = dict(ln_in_g=(1, D_MODEL), ln_in_b=(1, D_MODEL), sgu_w=(N_GRP, BLK, BLK), sgu_b=(N_GRP, BLK))
    small_params = {nm: tuple(src[nm].reshape(small_shape.get(nm, src[nm].shape)) for src in (weights, mom_m, mom_v))
                    for nm in _SMALL}
    small_out, loss = _small_adamw(_in_hbm(tot_a), _in_hbm(tot_b), small_params)
    loss = loss.reshape(())
    grads, delta, new_m, new_v = {}, {}, {}, {}
    for nm in _SMALL:
        grads[nm], delta[nm], new_m[nm], new_v[nm] = (a.reshape(weights[nm].shape) for a in small_out[nm])

    groups = [(shards[t], reduced[t], rowmajor(nm, mom_m[nm]), rowmajor(nm, mom_v[nm])) for t, nm in enumerate(big)]
    for nm, res in zip(big, _adamw("adamw", groups)):
        grads[nm], delta[nm], new_m[nm], new_v[nm] = (as_given(nm, a) for a in res)

    return (loss, grad_x.reshape(x.shape), *[grads[n] for n in order], *[delta[n] for n in order],
            *[new_m[n] for n in order], *[new_v[n] for n in order])
```

```python
import jax
import jax.numpy as jnp
from jax import lax
from jax.experimental import pallas as pl
from jax.experimental.pallas import tpu as pltpu

F32 = jnp.float32
_MXU = jnp.bfloat16
_WIRE = jnp.bfloat16
_ACT = jnp.bfloat16

D_MODEL = 1024
ATTN_W = 512
SGU_W = 512
HEAD_DIM = 64
N_Q = 8
N_KV = 2
Q_PER_KV = 4
KV_W = 128
BLK = 128
ROT_DIM = 16
ROPE_THETA = 500000.0
N_GRP = 4
GRP_DIM = 128
D_FF = 2816
IN_W = 1792
LN_EPS = 1e-5
ALPHA = 2.0 ** 0.25
N_CHIP = 4
FF_SH = D_FF // N_CHIP
IN_SH = IN_W // N_CHIP
OUT_SH = D_MODEL // N_CHIP
ROW_CHUNK = 32
GATE_CUT, UP_CUT = 352, 320

ADAM_LR = 0.001
ADAM_B1 = 0.9
ADAM_B2 = 0.999
ADAM_EPS = 1e-08
ADAM_WD = 0.01
ADAM_STEP = 10

SQRT_HALF = 0.7071067811865476
INV_SQRT_2PI = 0.3989422804014327
MESH_AXES = ("x", "y", "c")
MESH = pl.DeviceIdType.MESH
MIB = 2 ** 20


def _vmem():
    return pl.BlockSpec(memory_space=pltpu.VMEM)


def _smem():
    return pl.BlockSpec(memory_space=pltpu.SMEM)


def _hbm():
    return pl.BlockSpec(memory_space=pl.ANY)


def _hbm_shape(shape, dtype):
    return pltpu.HBM(shape, dtype)


def _in_hbm(a):
    return pltpu.with_memory_space_constraint(a, pltpu.HBM)


def _params(vmem_mib=48):
    return pltpu.CompilerParams(dimension_semantics=("arbitrary",), vmem_limit_bytes=vmem_mib * MIB)


def _tile(n, cap):
    if n <= cap:
        return n
    for t in range(cap - cap % 16, 0, -16):
        if n % t == 0:
            return t
    raise ValueError((n, cap))


def _rows(tm, width):
    return pl.BlockSpec((tm, width), lambda i: (i, 0))


def _const2(shape):
    return pl.BlockSpec(shape, lambda i: (0,) * len(shape))


def _ln(x, g, b):
    mu = jnp.mean(x, axis=-1, keepdims=True)
    xc = x - mu
    var = jnp.mean(xc * xc, axis=-1, keepdims=True)
    rstd = lax.rsqrt(var + LN_EPS)
    xhat = xc * rstd
    return xhat * g + b, xhat, rstd


def _ln_bwd(dy, xhat, rstd, g):
    gdy = dy * g
    m1 = jnp.mean(gdy, axis=-1, keepdims=True)
    m2 = jnp.mean(gdy * xhat, axis=-1, keepdims=True)
    return rstd * (gdy - m1 - xhat * m2)


def _colsum(a):
    return jnp.sum(a, axis=0, keepdims=True)


def _gelu_and_grad(x):
    cdf = 0.5 * (1.0 + lax.erf(x * SQRT_HALF))
    return x * cdf, cdf + x * jnp.exp(-0.5 * x * x) * INV_SQRT_2PI


def _dot(a, b):
    return jnp.dot(a, b, preferred_element_type=F32)


def _dot_nt(a, b):
    return lax.dot_general(a, b, (((1,), (1,)), ((), ())), preferred_element_type=F32)


def _dot_tn(a, b):
    return lax.dot_general(a, b, (((0,), (0,)), ((), ())), preferred_element_type=F32)


def _rope(t, tc, t1, t2):
    n = t.shape[1]
    rep = n // 128
    if rep > 1:
        tc, t1, t2 = (jnp.tile(a, (1, rep)) for a in (tc, t1, t2))
    return t * tc + pltpu.roll(t, n - 8, 1) * t1 + pltpu.roll(t, 8, 1) * t2


def _rope_bwd(d, tc, t1, t2):
    n = d.shape[1]
    rep = n // 128
    if rep > 1:
        tc, t1, t2 = (jnp.tile(a, (1, rep)) for a in (tc, t1, t2))
    return d * tc + pltpu.roll(d * t1, 8, 1) + pltpu.roll(d * t2, n - 8, 1)


def _causal_w(w_ref, h):
    t = lax.broadcasted_iota(jnp.int32, (BLK, BLK), 0)
    s = lax.broadcasted_iota(jnp.int32, (BLK, BLK), 1)
    return jnp.where(s <= t, w_ref[h], 0.0)


def _lane_put(vals, width):
    rows = vals[0].shape[0]
    lane = lax.broadcasted_iota(jnp.int32, (rows, width), 1)
    out = jnp.zeros((rows, width), F32)
    for k, v in enumerate(vals):
        out = out + jnp.where(lane == k, v, 0.0)
    return out


def _rope_consts():
    lane = jnp.arange(128) % HEAD_DIM
    rot = lane < ROT_DIM
    pair = (2 * (lane % (ROT_DIM // 2))).astype(F32)
    freq = jnp.where(rot, ROPE_THETA ** (-pair / ROT_DIM), 0.0)
    rows = [freq, rot.astype(F32), 1.0 - rot.astype(F32), (lane < ROT_DIM // 2).astype(F32),
            jnp.logical_and(lane >= ROT_DIM // 2, rot).astype(F32)]
    rows += [jnp.zeros((128,), F32)] * 3
    return jnp.stack(rows).astype(F32)


def _ln_inproj(x, pos_row, g0, b0, w_in, b_in, plan):
    s_len = x.shape[0]
    tm = _tile(s_len, 512)
    m, n = len(plan.operands()), plan.n

    def body(x_ref, pos_ref, g_ref, b_ref, w_ref, bi_ref, rc_ref, *rest):
        q_ref, k_ref, v_ref, su_ref, sv_ref, tc_ref, t1_ref, t2_ref = rest[m:m + 8]
        gather = plan.bind(rest[:m], rest[m + 8:m + 8 + n], rest[m + 8 + n:])
        i = pl.program_id(0)

        @pl.when(i == 0)
        def _():
            gather.start()

        h0, _, _ = _ln(x_ref[...], g_ref[...], b_ref[...])
        proj = _dot_nt(h0.astype(_MXU), w_ref[...]) + bi_ref[...]
        pos = jnp.broadcast_to(pos_ref[...].astype(F32), (128, tm))
        ang = jnp.transpose(pos) * rc_ref[0:1, :]
        cs = jnp.cos(ang)
        sn = jnp.sin(ang)
        tc = cs * rc_ref[1:2, :] + rc_ref[2:3, :]
        t1 = -sn * rc_ref[3:4, :]
        t2 = sn * rc_ref[4:5, :]
        tc_ref[...] = tc
        t1_ref[...] = t1
        t2_ref[...] = t2
        q = _rope(proj[:, 0:ATTN_W], tc, t1, t2) * (HEAD_DIM ** -0.5)
        q_ref[...] = q.astype(_MXU)
        k_ref[...] = _rope(proj[:, ATTN_W:ATTN_W + KV_W], tc, t1, t2).astype(_MXU)
        v_ref[...] = proj[:, ATTN_W + KV_W:ATTN_W + 2 * KV_W].astype(_MXU)
        su_ref[...] = proj[:, ATTN_W + 2 * KV_W:ATTN_W + 2 * KV_W + SGU_W]
        sv_ref[...] = proj[:, ATTN_W + 2 * KV_W + SGU_W:IN_W]

        last = pl.num_programs(0) - 1

        @pl.when(i == jnp.maximum(last - 1, 0))
        def _():
            gather.pass_on()

        @pl.when(i == last)
        def _():
            gather.finish()

    sd = _hbm_shape
    return pl.pallas_call(
        body, name="ln_inproj", grid=(s_len // tm,),
        in_specs=[_rows(tm, D_MODEL), pl.BlockSpec((1, tm), lambda i: (0, i)), _const2((1, D_MODEL)),
                  _const2((1, D_MODEL)), _vmem(),
                  _const2((1, IN_W)), _const2((8, 128))] + plan.in_specs(),
        out_specs=[_rows(tm, ATTN_W), _rows(tm, KV_W), _rows(tm, KV_W), _rows(tm, SGU_W), _rows(tm, SGU_W),
                   _rows(tm, 128), _rows(tm, 128), _rows(tm, 128)] + plan.out_specs(),
        out_shape=[sd((s_len, ATTN_W), _MXU), sd((s_len, KV_W), _MXU), sd((s_len, KV_W), _MXU),
                   sd((s_len, SGU_W), F32), sd((s_len, SGU_W), F32),
                   sd((s_len, 128), F32), sd((s_len, 128), F32), sd((s_len, 128), F32)] + plan.out_shapes(),
        scratch_shapes=plan.scratch(),
        compiler_params=_params(56),
    )(x, pos_row, g0, b0, w_in, b_in, _rope_consts(), *plan.operands())


def _band_mask_t(first_block):
    kj = lax.broadcasted_iota(jnp.int32, (2 * BLK, BLK), 0)
    qi = lax.broadcasted_iota(jnp.int32, (2 * BLK, BLK), 1)
    shut = jnp.where(first_block, 2 * BLK, 0)
    prev_ok = jnp.logical_and(kj < BLK, kj > qi + shut)
    cur_ok = jnp.logical_and(kj >= BLK, (kj - BLK) <= qi)
    return jnp.logical_or(prev_ok, cur_ok)


def _attn_probs_t(kh, qh, sink, allowed_t):
    s = jnp.where(allowed_t, _dot_nt(kh, qh), -1e30)
    m = jnp.maximum(jnp.max(s, axis=0, keepdims=True), sink)
    p = jnp.exp(s - m)
    ps = jnp.exp(sink - m)
    inv = 1.0 / (jnp.sum(p, axis=0, keepdims=True) + ps)
    return p * inv, ps * inv


def _sgu_mix(gv, lg, lb, w_ref, bt_ref):
    vv, vhat, rstd = _ln(gv, lg, lb)
    vvb = vv.astype(_MXU)
    wcs, mixed = [], []
    for h in range(N_GRP):
        wc = _causal_w(w_ref, h).astype(_MXU)
        wcs.append(wc)
        mixed.append(_dot(wc, vvb[:, h * GRP_DIM:(h + 1) * GRP_DIM]) + bt_ref[:, h:h + 1])
    return jnp.concatenate(mixed, axis=1), vhat, rstd, vvb, wcs


def _mixer_fwd(q, k, v, su, sv, sinks, sg, sb, sgu_w, sgu_bt, plan):
    s_len = q.shape[0]
    nb = s_len // BLK
    per = 2 if nb % 2 == 0 else 1
    steps = nb // per
    m, n = len(plan.operands()), plan.n

    def body(q_ref, kc_ref, kp_ref, vc_ref, vp_ref, su_ref, sv_ref, sink_ref, lg_ref, lb_ref, w_ref, bt_ref, *rest):
        mc_ref = rest[m]
        gather = plan.bind(rest[:m], rest[m + 1:m + 1 + n], rest[m + 1 + n:])
        i = pl.program_id(0)

        @pl.when(i == 0)
        def _():
            gather.start()

        @pl.when(i == max(steps - 2, 0))
        def _():
            gather.pass_on()

        @pl.when(i == steps - 1)
        def _():
            gather.finish()

        for s in range(per):
            rows = slice(s * BLK, (s + 1) * BLK)
            before = slice((s - 1) * BLK, s * BLK)
            k_prev = kp_ref[...] if s == 0 else kc_ref[before, :]
            v_prev = vp_ref[...] if s == 0 else vc_ref[before, :]
            allowed_t = _band_mask_t(i == 0 if s == 0 else False)
            kb = jnp.concatenate([k_prev, kc_ref[rows, :]], axis=0)
            vb = jnp.concatenate([v_prev, vc_ref[rows, :]], axis=0)
            qv = q_ref[rows, :]
            outs = []
            allowed_g = jnp.tile(allowed_t, (1, Q_PER_KV))
            for g in range(N_KV):
                heads = range(g * Q_PER_KV, (g + 1) * Q_PER_KV)
                kh = kb[:, g * HEAD_DIM:(g + 1) * HEAD_DIM]
                vh = vb[:, g * HEAD_DIM:(g + 1) * HEAD_DIM]
                q_g = jnp.concatenate([qv[:, h * HEAD_DIM:(h + 1) * HEAD_DIM] for h in heads], axis=0)
                sink_g = jnp.concatenate([jnp.full((1, BLK), sink_ref[h], F32) for h in heads], axis=1)
                probs_t, _ = _attn_probs_t(kh, q_g, sink_g, allowed_g)
                o_g = _dot_tn(probs_t.astype(_MXU), vh)
                outs += [o_g[hh * BLK:(hh + 1) * BLK, :] for hh in range(Q_PER_KV)]
            u = _gelu_and_grad(su_ref[rows, :])[0]
            gv = _gelu_and_grad(sv_ref[rows, :])[0]
            mixed = _sgu_mix(gv, lg_ref[...], lb_ref[...], w_ref, bt_ref)[0]
            mc_ref[rows, :] = jnp.concatenate(outs + [u * mixed], axis=1).astype(_MXU)

    cur = lambda w: pl.BlockSpec((per * BLK, w), lambda i: (i, 0))
    prev = lambda w: pl.BlockSpec((BLK, w), lambda i: (jnp.maximum(per * i - 1, 0), 0))
    return pl.pallas_call(
        body, name="mixer_fwd", grid=(steps,),
        in_specs=[cur(ATTN_W), cur(KV_W), prev(KV_W), cur(KV_W), prev(KV_W), cur(SGU_W), cur(SGU_W), _smem(),
                  _const2((1, SGU_W)), _const2((1, SGU_W)), _const2((N_GRP, BLK, BLK)), _const2((BLK, N_GRP))]
        + plan.in_specs(),
        out_specs=[cur(D_MODEL)] + plan.out_specs(),
        out_shape=[_hbm_shape((s_len, D_MODEL), _MXU)] + plan.out_shapes(),
        scratch_shapes=plan.scratch(),
        compiler_params=_params(56),
    )(q, k, k, v, v, su, sv, sinks, sg, sb, sgu_w, sgu_bt, *plan.operands())


def _outproj(mc, w_out, b_out, x, g0, b0, plan):
    s_len = x.shape[0]
    tm = _tile(s_len, 512)
    m, n = len(plan.operands()), plan.n

    def body(mc_ref, w_ref, bo_ref, x_ref, g_ref, b_ref, *rest):
        r1_ref = rest[m]
        gather = plan.bind(rest[:m], rest[m + 1:m + 1 + n], rest[m + 1 + n:])
        i = pl.program_id(0)

        @pl.when(i == 0)
        def _():
            gather.start()

        h0, _, _ = _ln(x_ref[...], g_ref[...], b_ref[...])
        r1_ref[...] = ALPHA * h0 + (_dot(mc_ref[...], w_ref[...]) + bo_ref[...])

        last = pl.num_programs(0) - 1

        @pl.when(i == jnp.maximum(last - 1, 0))
        def _():
            gather.pass_on()

        @pl.when(i == last)
        def _():
            gather.finish()

    return pl.pallas_call(
        body, name="outproj", grid=(s_len // tm,),
        in_specs=[_rows(tm, D_MODEL), _vmem(), _const2((1, D_MODEL)), _rows(tm, D_MODEL),
                  _const2((1, D_MODEL)), _const2((1, D_MODEL))] + plan.in_specs(),
        out_specs=[_rows(tm, D_MODEL)] + plan.out_specs(),
        out_shape=[_hbm_shape((s_len, D_MODEL), F32)] + plan.out_shapes(),
        scratch_shapes=plan.scratch(),
        compiler_params=_params(40),
    )(mc, w_out, b_out, x, g0, b0, *plan.operands())


def _ffn_spec(tm):
    return pl.BlockSpec((N_CHIP, tm, FF_SH), lambda i: (0, i, 0))


def _ffn_up(r1, g1, b1, wg, wu, plan):
    s_len = r1.shape[0]
    tm = _tile(s_len, 512)
    m, n = len(plan.operands()), plan.n

    def body(r1_ref, g_ref, b_ref, wg_ref, wu_ref, *rest):
        a_ref, p_ref, q_ref, h1_ref = rest[m:m + 4]
        gather = plan.bind(rest[:m], rest[m + 4:m + 4 + n], rest[m + 4 + n:])
        i = pl.program_id(0)

        @pl.when(i == 0)
        def _():
            gather.start()

        h1, _, _ = _ln(r1_ref[...], g_ref[...], b_ref[...])
        h1_ref[...] = h1
        h1b = h1.astype(_MXU)
        for j in range(N_CHIP):
            g = _dot_nt(h1b, wg_ref[j])
            u = _dot_nt(h1b, wu_ref[j])
            silu, sg = _silu_parts(g)
            a_ref[j] = (silu * u).astype(_MXU)
            p_ref[j] = silu.astype(_ACT)
            q_ref[j] = (u * (sg * (1.0 + g * (1.0 - sg)))).astype(_ACT)

        last = pl.num_programs(0) - 1

        @pl.when(i == jnp.maximum(last - 1, 0))
        def _():
            gather.pass_on()

        @pl.when(i == last)
        def _():
            gather.finish()

    sd = _hbm_shape((N_CHIP, s_len, FF_SH), _ACT)
    return pl.pallas_call(
        body, name="ffn_up", grid=(s_len // tm,),
        in_specs=[_rows(tm, D_MODEL), _const2((1, D_MODEL)), _const2((1, D_MODEL)), _vmem(), _vmem()] + plan.in_specs(),
        out_specs=[_ffn_spec(tm)] * 3 + [_rows(tm, D_MODEL)] + plan.out_specs(),
        out_shape=[_hbm_shape((N_CHIP, s_len, FF_SH), _MXU), sd, sd, _hbm_shape((s_len, D_MODEL), F32)]
        + plan.out_shapes(),
        scratch_shapes=plan.scratch(),
        compiler_params=_params(56),
    )(r1, g1, b1, wg, wu, *plan.operands())


def _silu_parts(g):
    sg = 1.0 / (1.0 + jnp.exp(-g))
    return g * sg, sg


def _ffn_down_loss(act, wd, h1, g2, b2, target):
    s_len = h1.shape[0]
    tm = _tile(s_len, 512)

    parts = 2 if tm % 32 == 0 else 1
    sub = tm // parts

    def body(a_ref, wd_ref, h1_ref, g2_ref, b2_ref, t_ref, dr2_ref, loss_ref, dg2_ref, db2_ref):
        i = pl.program_id(0)

        @pl.when(i == 0)
        def _():
            loss_ref[...] = jnp.zeros_like(loss_ref)
            dg2_ref[...] = jnp.zeros_like(dg2_ref)
            db2_ref[...] = jnp.zeros_like(db2_ref)

        for part in range(parts):
            rows = slice(part * sub, (part + 1) * sub)
            f = jnp.zeros((sub, D_MODEL), F32)
            for j in range(N_CHIP):
                f = f + _dot(a_ref[j, rows, :], wd_ref[j])
            h2, r2hat, rstd2 = _ln(ALPHA * h1_ref[rows, :] + f, g2_ref[...], b2_ref[...])
            diff = h2 - t_ref[rows, :]
            dh2 = diff * (1.0 / D_MODEL)
            loss_ref[...] += _colsum(diff * diff)
            dg2_ref[...] += _colsum(dh2 * r2hat)
            db2_ref[...] += _colsum(dh2)
            dr2_ref[rows, :] = _ln_bwd(dh2, r2hat, rstd2, g2_ref[...])

    vec = _hbm_shape((1, D_MODEL), F32)
    c = _const2((1, D_MODEL))
    return pl.pallas_call(
        body, name="ffn_down_loss", grid=(s_len // tm,),
        in_specs=[_ffn_spec(tm), _vmem(), _rows(tm, D_MODEL), c, c, _rows(tm, D_MODEL)],
        out_specs=[_rows(tm, D_MODEL), c, c, c],
        out_shape=[_hbm_shape((s_len, D_MODEL), F32), vec, vec, vec],
        compiler_params=_params(48),
    )(act, wd, h1, g2, b2, target)


def _ffn_bwd_a(dr2, act, p_act, q_act, wd):
    s_len = dr2.shape[0]
    tm = _tile(s_len, 512)

    def body(dr2_ref, a_ref, p_ref, q_ref, wd_ref, dg_ref, du_ref, wire_ref, own_ref,
             dwd_ref, land_ref, send_sem, recv_sem):
        i = pl.program_id(0)

        @pl.when(i == 0)
        def _():
            dwd_ref[...] = jnp.zeros_like(dwd_ref)

        dfb = dr2_ref[...].astype(_MXU)
        for j in range(N_CHIP):
            da = _dot_nt(dfb, wd_ref[j])
            dg_ref[j] = (da * q_ref[j].astype(F32)).astype(_MXU)
            du_ref[j] = (da * p_ref[j].astype(F32)).astype(_MXU)
            dwd_ref[j * FF_SH:(j + 1) * FF_SH, :] += _dot_tn(a_ref[j], dfb)

        @pl.when(i == pl.num_programs(0) - 1)
        def _():
            _pair_reduce(dwd_ref, wire_ref, own_ref, land_ref, send_sem, recv_sem)

    sd = _hbm_shape((N_CHIP, s_len, FF_SH), _MXU)
    half = (N_CHIP, FF_SH // 2, D_MODEL)
    return pl.pallas_call(
        body, name="ffn_bwd_a", grid=(s_len // tm,),
        in_specs=[_rows(tm, D_MODEL), _ffn_spec(tm), _ffn_spec(tm), _ffn_spec(tm), _vmem()],
        out_specs=[_ffn_spec(tm), _ffn_spec(tm), _vmem(), _vmem()],
        out_shape=[sd, sd] + _pair_out_shapes(half),
        scratch_shapes=_pair_scratch((D_FF, D_MODEL), half),
        compiler_params=_params(61),
    )(dr2, act, p_act, q_act, wd)


def _ffn_bwd_g(dr2, dg, r1, g1, b1, wg, prev_wire):
    s_len = dr2.shape[0]
    tm = _tile(s_len, 512)

    def body(dr2_ref, dg_ref, r1_ref, g1_ref, b1_ref, wg_ref, pw_ref, dh1_ref, wire_ref, own_ref, pl_ref,
             dwg_ref, land_ref, send_sem, recv_sem, xl_ref, x_send, x_recv, x_flush):
        i = pl.program_id(0)
        exchange = _ChipExchange(pw_ref, xl_ref, x_send, x_recv)

        @pl.when(i == 0)
        def _():
            exchange.start()
            dwg_ref[...] = jnp.zeros_like(dwg_ref)

        h1, _, _ = _ln(r1_ref[...], g1_ref[...], b1_ref[...])
        h1b = h1.astype(_MXU)
        dh1 = ALPHA * dr2_ref[...]
        for j in range(N_CHIP):
            dgj = dg_ref[j]
            dh1 = dh1 + _dot(dgj, wg_ref[j])
            dwg_ref[j * FF_SH:(j + 1) * FF_SH, :] += _dot_tn(dgj, h1b)
        dh1_ref[...] = dh1

        @pl.when(i == pl.num_programs(0) - 1)
        def _():
            exchange.flush_start(pl_ref, x_flush)
            _pair_reduce(dwg_ref, wire_ref, own_ref, land_ref, send_sem, recv_sem)
            exchange.flush_finish(pl_ref, x_flush)

    c = _const2((1, D_MODEL))
    half = (N_CHIP, FF_SH // 2, D_MODEL)
    return pl.pallas_call(
        body, name="ffn_bwd_g", grid=(s_len // tm,),
        in_specs=[_rows(tm, D_MODEL), _ffn_spec(tm), _rows(tm, D_MODEL), c, c, _vmem(), _vmem()],
        out_specs=[_rows(tm, D_MODEL), _vmem(), _vmem(), _hbm()],
        out_shape=[_hbm_shape((s_len, D_MODEL), F32)] + _pair_out_shapes(half) + [_ChipExchange.land_shape(prev_wire)],
        scratch_shapes=_pair_scratch((D_FF, D_MODEL), half) + _ChipExchange.scratch(prev_wire),
        compiler_params=_params(58),
    )(dr2, dg, r1, g1, b1, wg, prev_wire)


def _ffn_bwd_u(dh1a, du, r1, g1, b1, wu, prev_wire):
    s_len = dh1a.shape[0]
    tm = _tile(s_len, 512)

    def body(dh1_ref, du_ref, r1_ref, g1_ref, b1_ref, wu_ref, pw_ref,
             dr1_ref, wire_ref, own_ref, dg1_ref, db1_ref, pl_ref,
             dwu_ref, land_ref, send_sem, recv_sem, xl_ref, x_send, x_recv, x_flush):
        i = pl.program_id(0)
        exchange = _ChipExchange(pw_ref, xl_ref, x_send, x_recv)

        @pl.when(i == 0)
        def _():
            exchange.start()
            dwu_ref[...] = jnp.zeros_like(dwu_ref)
            dg1_ref[...] = jnp.zeros_like(dg1_ref)
            db1_ref[...] = jnp.zeros_like(db1_ref)

        h1, r1hat, rstd1 = _ln(r1_ref[...], g1_ref[...], b1_ref[...])
        h1b = h1.astype(_MXU)
        dh1 = dh1_ref[...]
        for j in range(N_CHIP):
            duj = du_ref[j]
            dh1 = dh1 + _dot(duj, wu_ref[j])
            dwu_ref[j * FF_SH:(j + 1) * FF_SH, :] += _dot_tn(duj, h1b)
        dg1_ref[...] += _colsum(dh1 * r1hat)
        db1_ref[...] += _colsum(dh1)
        dr1_ref[...] = _ln_bwd(dh1, r1hat, rstd1, g1_ref[...])

        @pl.when(i == pl.num_programs(0) - 1)
        def _():
            exchange.flush_start(pl_ref, x_flush)
            _pair_reduce(dwu_ref, wire_ref, own_ref, land_ref, send_sem, recv_sem)
            exchange.flush_finish(pl_ref, x_flush)

    vec = _hbm_shape((1, D_MODEL), F32)
    c = _const2((1, D_MODEL))
    half = (N_CHIP, FF_SH // 2, D_MODEL)
    return pl.pallas_call(
        body, name="ffn_bwd_u", grid=(s_len // tm,),
        in_specs=[_rows(tm, D_MODEL), _ffn_spec(tm), _rows(tm, D_MODEL), c, c, _vmem(), _vmem()],
        out_specs=[_rows(tm, D_MODEL), _vmem(), _vmem(), c, c, _hbm()],
        out_shape=[_hbm_shape((s_len, D_MODEL), F32)] + _pair_out_shapes(half)
        + [vec, vec, _ChipExchange.land_shape(prev_wire)],
        scratch_shapes=_pair_scratch((D_FF, D_MODEL), half) + _ChipExchange.scratch(prev_wire),
        compiler_params=_params(58),
    )(dh1a, du, r1, g1, b1, wu, prev_wire)


def _outproj_bwd(dr1, mc, w_out):
    s_len = dr1.shape[0]
    tm = _tile(s_len, 512)

    def body(dr1_ref, mc_ref, w_ref, dmc_ref, wire_ref, own_ref, db_ref, dw_ref, land_ref, send_sem, recv_sem):
        i = pl.program_id(0)

        @pl.when(i == 0)
        def _():
            dw_ref[...] = jnp.zeros_like(dw_ref)
            db_ref[...] = jnp.zeros_like(db_ref)

        d = dr1_ref[...]
        db_ref[...] += _colsum(d)
        db16 = d.astype(_MXU)
        dmc_ref[...] = _dot_nt(db16, w_ref[...])
        dw_ref[...] += _dot_tn(mc_ref[...], db16)

        @pl.when(i == pl.num_programs(0) - 1)
        def _():
            _pair_reduce(dw_ref, wire_ref, own_ref, land_ref, send_sem, recv_sem)

    half = (N_CHIP, OUT_SH // 2, D_MODEL)
    return pl.pallas_call(
        body, name="outproj_bwd", grid=(s_len // tm,),
        in_specs=[_rows(tm, D_MODEL), _rows(tm, D_MODEL), _vmem()],
        out_specs=[_rows(tm, D_MODEL), _vmem(), _vmem(), _const2((1, D_MODEL))],
        out_shape=[_hbm_shape((s_len, D_MODEL), F32)] + _pair_out_shapes(half) + [_hbm_shape((1, D_MODEL), F32)],
        scratch_shapes=_pair_scratch((D_MODEL, D_MODEL), half),
        compiler_params=_params(48),
    )(dr1, mc, w_out)


def _mixer_bwd(q, k, v, su, sv, dmc, tc, t1, t2, sinks, sg, sb, sgu_w, sgu_bt, prev_wires):
    s_len = q.shape[0]
    nb = s_len // BLK
    per = next(p for p in (4, 2, 1) if nb % p == 0)
    steps = nb // per

    def body(q_ref, kc_ref, kp_ref, vc_ref, vp_ref, su_ref, sv_ref, dmc_ref,
             tc_ref, t1_ref, t2_ref, tcp_ref, t1p_ref, t2p_ref,
             sink_ref, lg_ref, lb_ref, w_ref, bt_ref, pw0_ref, pw1_ref,
             dq_ref, dkv_ref, dsuv_ref, dbq_ref, dbkv_ref, dbsuv_ref,
             dsink_ref, dlg_ref, dlb_ref, dw_ref, dbt_ref, pl0_ref, pl1_ref, carry_ref,
             xl0_ref, x0_send, x0_recv, x0_flush, xl1_ref, x1_send, x1_recv, x1_flush):
        i = pl.program_id(0)
        exchanges = [(_ChipExchange(pw0_ref, xl0_ref, x0_send, x0_recv), pl0_ref, x0_flush),
                     (_ChipExchange(pw1_ref, xl1_ref, x1_send, x1_recv), pl1_ref, x1_flush)]

        @pl.when(i == 0)
        def _():
            for exchange, _, _ in exchanges:
                exchange.start()

        @pl.when(i == 0)
        def _():
            for r in (dbq_ref, dbkv_ref, dbsuv_ref, dsink_ref, dlg_ref, dlb_ref, dw_ref, dbt_ref, carry_ref):
                r[...] = jnp.zeros_like(r)

        def emit_kv(fin, t):
            if t == 0:
                tables = (tcp_ref[...], t1p_ref[...], t2p_ref[...])
            else:
                before = slice((t - 1) * BLK, t * BLK)
                tables = (tc_ref[before, :], t1_ref[before, :], t2_ref[before, :])
            dk = _rope_bwd(fin[:, 0:KV_W], *tables)
            out = jnp.concatenate([dk, fin[:, KV_W:2 * KV_W]], axis=1)
            dkv_ref[t * BLK:(t + 1) * BLK, :] = out.astype(_MXU)
            dbkv_ref[...] += _colsum(out)

        def one_block(s):
            rows = slice(s * BLK, (s + 1) * BLK)
            before = slice((s - 1) * BLK, s * BLK)
            k_prev = kp_ref[...] if s == 0 else kc_ref[before, :]
            v_prev = vp_ref[...] if s == 0 else vc_ref[before, :]
            allowed_t = _band_mask_t(i == 0 if s == 0 else False)
            kb = jnp.concatenate([k_prev, kc_ref[rows, :]], axis=0)
            vb = jnp.concatenate([v_prev, vc_ref[rows, :]], axis=0)
            qv = q_ref[rows, :]
            dmc = dmc_ref[rows, :]
            dqs, dks, dvs, dsinks = [], [], [], []
            allowed_g = jnp.tile(allowed_t, (1, Q_PER_KV))
            for g in range(N_KV):
                heads = range(g * Q_PER_KV, (g + 1) * Q_PER_KV)
                kh = kb[:, g * HEAD_DIM:(g + 1) * HEAD_DIM]
                vh = vb[:, g * HEAD_DIM:(g + 1) * HEAD_DIM]
                q_g = jnp.concatenate([qv[:, h * HEAD_DIM:(h + 1) * HEAD_DIM] for h in heads], axis=0)
                do_g = jnp.concatenate([dmc[:, h * HEAD_DIM:(h + 1) * HEAD_DIM] for h in heads], axis=0).astype(_MXU)
                sink_g = jnp.concatenate([jnp.full((1, BLK), sink_ref[h], F32) for h in heads], axis=1)
                probs_t, psink = _attn_probs_t(kh, q_g, sink_g, allowed_g)
                dvs.append(_dot(probs_t.astype(_MXU), do_g))
                dp_t = _dot_nt(vh, do_g)
                rd = jnp.sum(probs_t * dp_t, axis=0, keepdims=True)
                ds_t = (probs_t * (dp_t - rd)).astype(_MXU)
                ps_rd = psink * rd
                for hh in range(Q_PER_KV):
                    dsinks.append(-jnp.sum(ps_rd[:, hh * BLK:(hh + 1) * BLK], axis=1, keepdims=True))
                dq_g = _dot_tn(ds_t, kh)
                dqs += [dq_g[hh * BLK:(hh + 1) * BLK, :] for hh in range(Q_PER_KV)]
                dks.append(_dot(ds_t, q_g))
            dq = _rope_bwd(jnp.concatenate(dqs, axis=1) * (HEAD_DIM ** -0.5),
                           tc_ref[rows, :], t1_ref[rows, :], t2_ref[rows, :])
            dq_ref[rows, :] = dq.astype(_MXU)
            dbq_ref[...] += _colsum(dq)
            dsink_ref[...] += _lane_put(dsinks, 128)
            contrib = jnp.concatenate(dks + dvs, axis=1)

            lg = lg_ref[...]
            u, du_dsu = _gelu_and_grad(su_ref[rows, :])
            gv, dgv_dsv = _gelu_and_grad(sv_ref[rows, :])
            mixed, vhat, rstd, vvb, wcs = _sgu_mix(gv, lg, lb_ref[...], w_ref, bt_ref)
            dsgu = dmc[:, ATTN_W:D_MODEL]
            dsu = dsgu * mixed * du_dsu
            dmixed = dsgu * u
            tri_t = lax.broadcasted_iota(jnp.int32, (BLK, BLK), 0)
            tri_s = lax.broadcasted_iota(jnp.int32, (BLK, BLK), 1)
            dvv, dbs = [], []
            for h in range(N_GRP):
                dm = dmixed[:, h * GRP_DIM:(h + 1) * GRP_DIM]
                dmb = dm.astype(_MXU)
                dbs.append(jnp.sum(dm, axis=1, keepdims=True))
                dw_ref[h] += jnp.where(tri_s <= tri_t, _dot_nt(dmb, vvb[:, h * GRP_DIM:(h + 1) * GRP_DIM]), 0.0)
                dvv.append(_dot_tn(wcs[h], dmb))
            dvv = jnp.concatenate(dvv, axis=1)
            dbt_ref[...] += _lane_put(dbs, 128)
            dlg_ref[...] += _colsum(dvv * vhat)
            dlb_ref[...] += _colsum(dvv)
            dsv = _ln_bwd(dvv, vhat, rstd, lg) * dgv_dsv
            dsuv = jnp.concatenate([dsu, dsv], axis=1)
            dsuv_ref[rows, :] = dsuv.astype(_MXU)
            dbsuv_ref[...] += _colsum(dsuv)
            return contrib

        @pl.when(i < steps)
        def _():
            contribs = [one_block(s) for s in range(per)]
            for t in range(per):
                top = carry_ref[...] if t == 0 else contribs[t - 1][BLK:2 * BLK, :]
                emit_kv(top + contribs[t][0:BLK, :], t)
            carry_ref[...] = contribs[per - 1][BLK:2 * BLK, :]

        @pl.when(i == steps)
        def _():
            for exchange, landed, flush_sem in exchanges:
                exchange.flush_start(landed, flush_sem)
            emit_kv(carry_ref[...], 0)
            if per > 1:
                dkv_ref[BLK:per * BLK, :] = jnp.zeros(((per - 1) * BLK, 2 * KV_W), _MXU)
            for exchange, landed, flush_sem in exchanges:
                exchange.flush_finish(landed, flush_sem)

    last = steps - 1
    cur = lambda w: pl.BlockSpec((per * BLK, w), lambda i: (jnp.minimum(i, last), 0))
    prev = lambda w: pl.BlockSpec((BLK, w), lambda i: (jnp.clip(per * i - 1, 0, nb - 1), 0))
    shifted = pl.BlockSpec((per * BLK, 2 * KV_W), lambda i: (i, 0))
    sd = _hbm_shape
    return pl.pallas_call(
        body, name="mixer_bwd", grid=(steps + 1,),
        in_specs=[cur(ATTN_W), cur(KV_W), prev(KV_W), cur(KV_W), prev(KV_W), cur(SGU_W), cur(SGU_W), cur(D_MODEL),
                  cur(128), cur(128), cur(128), prev(128), prev(128), prev(128),
                  _smem(), _const2((1, SGU_W)), _const2((1, SGU_W)), _const2((N_GRP, BLK, BLK)), _const2((BLK, N_GRP)),
                  _vmem(), _vmem()],
        out_specs=[cur(ATTN_W), shifted, cur(2 * SGU_W),
                   _const2((1, ATTN_W)), _const2((1, 2 * KV_W)), _const2((1, 2 * SGU_W)),
                   _const2((1, 128)), _const2((1, SGU_W)), _const2((1, SGU_W)),
                   _const2((N_GRP, BLK, BLK)), _const2((BLK, 128)), _hbm(), _hbm()],
        out_shape=[sd((s_len, ATTN_W), _MXU), sd((s_len + per * BLK, 2 * KV_W), _MXU), sd((s_len, 2 * SGU_W), _MXU),
                   sd((1, ATTN_W), F32), sd((1, 2 * KV_W), F32), sd((1, 2 * SGU_W), F32),
                   sd((1, 128), F32), sd((1, SGU_W), F32), sd((1, SGU_W), F32),
                   sd((N_GRP, BLK, BLK), F32), sd((BLK, 128), F32)]
        + [_ChipExchange.land_shape(w) for w in prev_wires],
        scratch_shapes=[pltpu.VMEM((BLK, 2 * KV_W), F32)] + _ChipExchange.scratch(prev_wires[0])
        + _ChipExchange.scratch(prev_wires[1]),
        compiler_params=_params(40),
    )(q, k, k, v, v, su, sv, dmc, tc, t1, t2, tc, t1, t2, sinks, sg, sb, sgu_w, sgu_bt, *prev_wires)


def _inproj_bwd(dq, dkv_late, dsuv, dr1, x, g0, b0, w_in):
    s_len = x.shape[0]
    tm = _tile(s_len, 512)
    assert tm % BLK == 0
    per = tm // BLK
    cuts = ((0, ATTN_W), (ATTN_W, ATTN_W + 2 * KV_W), (ATTN_W + 2 * KV_W, IN_W))

    def body(dq_ref, *rest):
        dkv_refs = rest[:per]
        dsuv_ref, dr1_ref, x_ref, g_ref, b_ref, w_ref, dx_ref, dw_ref, dg_ref, db_ref = rest[per:]
        i = pl.program_id(0)

        @pl.when(i == 0)
        def _():
            dw_ref[...] = jnp.zeros_like(dw_ref)
            dg_ref[...] = jnp.zeros_like(dg_ref)
            db_ref[...] = jnp.zeros_like(db_ref)

        h0, xhat, rstd = _ln(x_ref[...], g_ref[...], b_ref[...])
        h0b = h0.astype(_MXU)
        dh0 = ALPHA * dr1_ref[...]
        dkv = jnp.concatenate([r[...] for r in dkv_refs], axis=0)
        for (lo, hi), d in zip(cuts, (dq_ref[...], dkv, dsuv_ref[...])):
            dh0 = dh0 + _dot(d, w_ref[lo:hi, :])
            dw_ref[lo:hi, :] += _dot_tn(d, h0b)
        dg_ref[...] += _colsum(dh0 * xhat)
        db_ref[...] += _colsum(dh0)
        dx_ref[...] = _ln_bwd(dh0, xhat, rstd, g_ref[...])

    vec = _hbm_shape((1, D_MODEL), F32)
    c = _const2((1, D_MODEL))
    return pl.pallas_call(
        body, name="inproj_bwd", grid=(s_len // tm,),
        in_specs=[_rows(tm, ATTN_W)]
        + [pl.BlockSpec((BLK, 2 * KV_W), lambda i, b=b: (i * per + b + 1, 0)) for b in range(per)]
        + [_rows(tm, 2 * SGU_W), _rows(tm, D_MODEL), _rows(tm, D_MODEL), c, c, _vmem()],
        out_specs=[_rows(tm, D_MODEL), _vmem(), c, c],
        out_shape=[_hbm_shape((s_len, D_MODEL), F32), jax.ShapeDtypeStruct((IN_W, D_MODEL), F32), vec, vec],
        compiler_params=_params(48),
    )(dq, *[dkv_late] * per, dsuv, dr1, x, g0, b0, w_in)


def _place():
    x, y, c = (lax.axis_index(a) for a in MESH_AXES)
    chips = [(1 - x, y), (x, 1 - y), (1 - x, 1 - y)]
    return x, y, c, chips


class _Gather:
    def __init__(self, ins, outs, send_sems, recv_sems, spans=None):
        self.ins, self.outs, self.send_sems, self.recv_sems = ins, outs, send_sems, recv_sems
        self.n = len(ins)
        self.spans = spans or [(0, r.shape[0]) for r in ins]
        self.halves = [(hi - lo) // 2 for lo, hi in self.spans]

    def _copy(self, k, t, slot, half, to):
        rows = pl.ds(pl.multiple_of(self.spans[t][0] + half * self.halves[t], 16), self.halves[t])
        piece = self.outs[t].at[slot, rows, :]
        return pltpu.make_async_remote_copy(src_ref=piece, dst_ref=piece, send_sem=self.send_sems.at[k],
                                            recv_sem=self.recv_sems.at[k], device_id=to, device_id_type=MESH)

    def _chip_copy(self, t, d, slot):
        x, y, c, chips = _place()
        return self._copy(3 * t + d, t, slot, c, (chips[d][0], chips[d][1], c))

    def _pass_copy(self, t, d, half):
        x, y, c, chips = _place()
        return self._copy(3 * self.n + 3 * t + d, t, 2 * chips[d][0] + chips[d][1], half, (x, y, 1 - c))

    def start(self):
        x, y, c, chips = _place()
        me = 2 * x + y
        for t in range(self.n):
            lo, hi = self.spans[t]
            self.outs[t][me, lo:hi, :] = self.ins[t][lo:hi, :].astype(_WIRE)
        for t in range(self.n):
            for d in range(3):
                self._chip_copy(t, d, me).start()

    def pass_on(self):
        x, y, c, chips = _place()
        for t in range(self.n):
            for d in range(3):
                self._chip_copy(t, d, 2 * chips[d][0] + chips[d][1]).wait_recv()
                self._pass_copy(t, d, c).start()

    def finish(self):
        x, y, c, chips = _place()
        me = 2 * x + y
        for t in range(self.n):
            for d in range(3):
                self._pass_copy(t, d, 1 - c).wait_recv()
        for t in range(self.n):
            for d in range(3):
                self._chip_copy(t, d, me).wait_send()
                self._pass_copy(t, d, c).wait_send()

    @staticmethod
    def out_shapes(shards, make=jax.ShapeDtypeStruct):
        return [make((N_CHIP,) + s.shape, _WIRE) for s in shards]

    @staticmethod
    def sems(n):
        return [pltpu.SemaphoreType.DMA((6 * n,)), pltpu.SemaphoreType.DMA((6 * n,))]


_FLUSHES_EARLY, _FLUSHES = 5, 8


class _GatherPlan:
    def __init__(self, pieces):
        self.shards = [p[0] for p in pieces]
        self.spans = [p[1] for p in pieces]
        self.earlier = [p[2] for p in pieces]
        self.n = len(pieces)
        self.carried = [t for t in range(self.n) if self.earlier[t] is not None]

    def operands(self):
        return self.shards + [self.earlier[t] for t in self.carried]

    def in_specs(self):
        return [_vmem()] * self.n + [_hbm()] * len(self.carried)

    def out_specs(self):
        return [_hbm()] * self.n

    def out_shapes(self):
        return _Gather.out_shapes(self.shards, _hbm_shape)

    def scratch(self):
        return ([pltpu.VMEM((N_CHIP,) + s.shape, _WIRE) for s in self.shards] + _Gather.sems(self.n)
                + [pltpu.SemaphoreType.DMA((_FLUSHES * self.n,)), pltpu.SemaphoreType.DMA((max(len(self.carried), 1),))])

    def bind(self, in_refs, out_refs, scratch_refs):
        plan = self
        shard_refs, earlier_refs = in_refs[:self.n], in_refs[self.n:]
        bufs = scratch_refs[:self.n]
        send_sems, recv_sems, flush_sems, carry_sems = scratch_refs[self.n:self.n + 4]
        gather = _Gather(shard_refs, bufs, send_sems, recv_sems, self.spans)

        def carry_copy(k):
            t = plan.carried[k]
            lo = plan.spans[t][0]
            return pltpu.make_async_copy(earlier_refs[k].at[:, 0:lo, :], bufs[t].at[:, 0:lo, :], carry_sems.at[k])

        def flushes(t, late):
            x, y, c, chips = _place()
            lo, hi = plan.spans[t]
            half = (hi - lo) // 2
            others = [2 * chips[d][0] + chips[d][1] for d in range(3)]

            def half_rows(h):
                return pl.ds(pl.multiple_of(lo + h * half, 16), half)

            if late:
                parts = [(slot, half_rows(1 - c)) for slot in others]
            else:
                parts = [(2 * x + y, pl.ds(lo, hi - lo))] + [(slot, half_rows(c)) for slot in others]
                if lo:
                    parts.append((slice(None), pl.ds(0, lo)))
            first = _FLUSHES_EARLY if late else 0
            return [pltpu.make_async_copy(bufs[t].at[slot, rows, :], out_refs[t].at[slot, rows, :],
                                          flush_sems.at[_FLUSHES * t + first + k]) for k, (slot, rows) in enumerate(parts)]

        class Bound:
            @staticmethod
            def start():
                for k in range(len(plan.carried)):
                    carry_copy(k).start()
                gather.start()

            @staticmethod
            def pass_on():
                gather.pass_on()
                for k in range(len(plan.carried)):
                    carry_copy(k).wait()
                for t in range(plan.n):
                    for cp in flushes(t, late=False):
                        cp.start()

            @staticmethod
            def finish():
                gather.finish()
                for t in range(plan.n):
                    for cp in flushes(t, late=True):
                        cp.start()
                for t in range(plan.n):
                    for cp in flushes(t, late=False) + flushes(t, late=True):
                        cp.wait()

        return Bound


def _gather_weights(shards):
    n = len(shards)

    def body(*refs):
        gather = _Gather(refs[:n], refs[n:2 * n], refs[2 * n], refs[2 * n + 1])
        gather.start()
        gather.pass_on()
        gather.finish()

    return pl.pallas_call(
        body, name="gather_weights",
        in_specs=[_vmem()] * n, out_specs=[_vmem()] * n,
        out_shape=_Gather.out_shapes(shards), scratch_shapes=_Gather.sems(n),
        compiler_params=pltpu.CompilerParams(vmem_limit_bytes=32 * MIB),
    )(*shards)


class _ChipExchange:
    def __init__(self, wire_ref, land_ref, send_sems, recv_sems):
        self.wire, self.land, self.send_sems, self.recv_sems = wire_ref, land_ref, send_sems, recv_sems

    def _copy(self, d):
        x, y, c, chips = _place()
        return pltpu.make_async_remote_copy(
            src_ref=self.wire.at[2 * chips[d][0] + chips[d][1]], dst_ref=self.land.at[d],
            send_sem=self.send_sems.at[d], recv_sem=self.recv_sems.at[d],
            device_id=(chips[d][0], chips[d][1], c), device_id_type=MESH)

    def start(self):
        for d in range(3):
            self._copy(d).start()

    def wait_recv(self):
        for d in range(3):
            self._copy(d).wait_recv()

    def wait_send(self):
        for d in range(3):
            self._copy(d).wait_send()

    def _flush_copy(self, hbm_out, flush_sem):
        return pltpu.make_async_copy(self.land, hbm_out, flush_sem.at[0])

    def flush_start(self, hbm_out, flush_sem):
        self.wait_recv()
        self._flush_copy(hbm_out, flush_sem).start()

    def flush_finish(self, hbm_out, flush_sem):
        self._flush_copy(hbm_out, flush_sem).wait()
        self.wait_send()

    @staticmethod
    def land_shape(wire):
        return _hbm_shape((3,) + wire.shape[1:], wire.dtype)

    @staticmethod
    def sems():
        return [pltpu.SemaphoreType.DMA((3,)), pltpu.SemaphoreType.DMA((3,))]

    @staticmethod
    def scratch(wire):
        return ([pltpu.VMEM((3,) + wire.shape[1:], wire.dtype)] + _ChipExchange.sems() + [pltpu.SemaphoreType.DMA((1,))])


def _pair_out_shapes(half_shape):
    return [jax.ShapeDtypeStruct(half_shape, _WIRE), jax.ShapeDtypeStruct(half_shape[1:], F32)]


def _pair_scratch(acc_shape, half_shape):
    return [pltpu.VMEM(acc_shape, F32), pltpu.VMEM(half_shape, _WIRE),
            pltpu.SemaphoreType.DMA((N_CHIP,)), pltpu.SemaphoreType.DMA((N_CHIP,))]


def _pair_reduce(acc_ref, wire_ref, own_ref, land_ref, send_sems, recv_sems):
    rh = land_ref.shape[1]
    x, y, c, _ = _place()
    me = 2 * x + y
    copies = []
    for j in range(N_CHIP):
        def cast(r, carry, j=j):
            dst = pl.ds(pl.multiple_of(r * ROW_CHUNK, ROW_CHUNK), ROW_CHUNK)
            src = pl.ds(pl.multiple_of((2 * j + 1 - c) * rh + r * ROW_CHUNK, 8), ROW_CHUNK)
            wire_ref[j, dst, :] = acc_ref[src, :].astype(_WIRE)
            return carry

        lax.fori_loop(0, rh // ROW_CHUNK, cast, 0)
        cp = pltpu.make_async_remote_copy(src_ref=wire_ref.at[j], dst_ref=land_ref.at[j], send_sem=send_sems.at[j],
                                          recv_sem=recv_sems.at[j], device_id=(x, y, 1 - c), device_id_type=MESH)
        cp.start()
        copies.append(cp)
    for j in range(N_CHIP):
        copies[j].wait()

        def chunk(r, carry, j=j):
            theirs = pl.ds(pl.multiple_of(r * ROW_CHUNK, ROW_CHUNK), ROW_CHUNK)
            mine = pl.ds(pl.multiple_of((2 * j + c) * rh + r * ROW_CHUNK, 8), ROW_CHUNK)
            wire_ref[j, theirs, :] = (acc_ref[mine, :] + land_ref[j, theirs, :].astype(F32)).astype(_WIRE)
            return carry

        lax.fori_loop(0, rh // ROW_CHUNK, chunk, 0)

    def own_chunk(r, carry):
        theirs = pl.ds(pl.multiple_of(r * ROW_CHUNK, ROW_CHUNK), ROW_CHUNK)
        mine = pl.ds(pl.multiple_of((2 * me + c) * rh + r * ROW_CHUNK, 8), ROW_CHUNK)
        own_ref[theirs, :] = acc_ref[mine, :] + land_ref[me, theirs, :].astype(F32)
        return carry

    lax.fori_loop(0, rh // ROW_CHUNK, own_chunk, 0)


def _grad_finish(last_acc, lands, owns, small):
    n = len(owns) + 1
    halves = [last_acc.shape[0] // (2 * N_CHIP)] + [w.shape[1] for w in lands]
    widths = [last_acc.shape[1]] + [a.shape[1] for a in owns]
    small_body, small_scratch = _small_allreduce_parts()
    ns = len(small)

    def body(*refs):
        acc0, land, own = refs[0], (None,) + refs[1:n], (None,) + refs[n:2 * n - 1]
        refs = refs[2 * n - 1:]
        small_in, g_out, small_out = refs[:ns], refs[ns:ns + n], refs[ns + n:ns + n + 2]
        refs = refs[ns + n + 2:]
        pland0, wire0, land0, own0 = refs[0:4]
        p_send, p_recv, x_send, x_recv, pair_send, pair_recv = refs[4:10]
        g, flush_sems = refs[10:10 + n], refs[10 + n]
        small_refs = refs[11 + n:]
        land = (land0,) + land[1:]
        own = (own0,) + own[1:]
        x, y, c, chips = _place()
        me = 2 * x + y
        exchange = _ChipExchange(wire0, land0, x_send, x_recv)

        def half_rows(t, half):
            return pl.ds(pl.multiple_of(half * halves[t], 8), halves[t])

        def pair_copy(t, half):
            rows = g[t].at[half_rows(t, half), :]
            return pltpu.make_async_remote_copy(src_ref=rows, dst_ref=rows, send_sem=pair_send.at[t],
                                                recv_sem=pair_recv.at[t], device_id=(x, y, 1 - c), device_id_type=MESH)

        def flush(t):
            return pltpu.make_async_copy(g[t], g_out[t], flush_sems.at[t])

        small_rounds = small_body(*small_in, *small_out, *small_refs)
        next(small_rounds)
        _pair_reduce(acc0, wire0, own0, pland0, p_send, p_recv)
        next(small_rounds)
        exchange.start()

        for t in list(range(1, n)) + [0]:
            if t == 0:
                for done in range(1, n):
                    pair_copy(done, 1 - c).wait_recv()
                    flush(done).start()
                exchange.wait_recv()
            if t == min(2, n - 1):
                next(small_rounds)
            if t == min(4, n - 1):
                next(small_rounds, None)

            def chunk(r, carry, t=t):
                src = pl.ds(pl.multiple_of(r * ROW_CHUNK, ROW_CHUNK), ROW_CHUNK)
                dst = pl.ds(pl.multiple_of(c * halves[t] + r * ROW_CHUNK, 8), ROW_CHUNK)
                s = own[t][src, :]
                for d in range(3):
                    s = s + land[t][d, src, :].astype(F32)
                g[t][dst, :] = s
                return carry

            lax.fori_loop(0, halves[t] // ROW_CHUNK, chunk, 0)
            pair_copy(t, c).start()
        pair_copy(0, 1 - c).wait_recv()
        flush(0).start()
        for t in range(n):
            pair_copy(t, c).wait_send()
        exchange.wait_send()
        for t in range(n):
            flush(t).wait()

    half0 = (halves[0], widths[0])
    shapes = [(2 * h, w) for h, w in zip(halves, widths)]
    return pl.pallas_call(
        body, name="grad_finish",
        in_specs=[_vmem()] * (2 * n - 1 + ns), out_specs=[_hbm()] * n + [_vmem()] * 2,
        out_shape=[_hbm_shape(s, F32) for s in shapes] + [jax.ShapeDtypeStruct(s, F32) for s in _SMALL_OUT_DIMS],
        scratch_shapes=[pltpu.VMEM((N_CHIP,) + half0, _WIRE), pltpu.VMEM((N_CHIP,) + half0, _WIRE),
                        pltpu.VMEM((3,) + half0, _WIRE), pltpu.VMEM(half0, F32)]
        + [pltpu.SemaphoreType.DMA((N_CHIP,)), pltpu.SemaphoreType.DMA((N_CHIP,))]
        + _ChipExchange.sems()
        + [pltpu.SemaphoreType.DMA((n,)), pltpu.SemaphoreType.DMA((n,))]
        + [pltpu.VMEM(s, F32) for s in shapes] + [pltpu.SemaphoreType.DMA((n,))]
        + small_scratch,
        compiler_params=pltpu.CompilerParams(vmem_limit_bytes=56 * MIB),
    )(last_acc, *lands, *owns, *small)


_SMALL = ("ln_in_g", "ln_in_b", "b_in", "attn_sinks", "sgu_ln_g", "sgu_ln_b", "sgu_w", "sgu_b", "b_out",
          "ln_mix_g", "ln_mix_b", "ln_ffn_g", "ln_ffn_b")
_VEC_ROW = dict(ln_in_g=0, ln_in_b=1, b_in=2, attn_sinks=4, sgu_ln_g=5, sgu_ln_b=6, b_out=7, ln_mix_g=8, ln_mix_b=9,
                ln_ffn_g=10, ln_ffn_b=11)
_LOSS_ROW = 12
_VEC_ROWS = 16
_MAT_ROWS = N_GRP * BLK + BLK


_SMALL_IN = ("ln_in_g", "ln_in_b", "bq", "bkv", "bsuv", "sink", "sgu_ln_g", "sgu_ln_b", "sgu_w", "sgu_bt", "b_out",
             "ln_mix_g", "ln_mix_b", "ln_ffn_g", "ln_ffn_b", "loss")
_SMALL_OUT_DIMS = ((_VEC_ROWS, D_MODEL), (_MAT_ROWS, 128))


def _small_allreduce_parts():
    n_in = len(_SMALL_IN)

    def body(*refs):
        (g_ln_in_g, g_ln_in_b, g_bq, g_bkv, g_bsuv, g_sink, g_sln_g, g_sln_b, g_sw, g_sbt, g_bout,
         g_lmg, g_lmb, g_lfg, g_lfb, g_loss) = refs[:n_in]
        out_a, out_b = refs[n_in:n_in + 2]
        (buf_a, buf_b, pair_a, pair_b, stage_a, stage_b, tot_a, tot_b,
         p1_send, p1_recv, x_send, x_recv, p2_send, p2_recv) = refs[n_in + 2:]
        x, y, c, chips = _place()
        me = 2 * x + y
        sibling = (x, y, 1 - c)
        half_a, half_b = _VEC_ROWS // 2, _MAT_ROWS // 2

        buf_a[...] = jnp.zeros_like(buf_a)
        for row, ref in ((0, g_ln_in_g), (1, g_ln_in_b), (7, g_bout), (8, g_lmg), (9, g_lmb), (10, g_lfg), (11, g_lfb),
                         (_LOSS_ROW, g_loss)):
            buf_a[row:row + 1, :] = ref[...]
        buf_a[2:3, 0:ATTN_W] = g_bq[...]
        buf_a[2:3, ATTN_W:ATTN_W + 2 * KV_W] = g_bkv[...]
        buf_a[2:3, ATTN_W + 2 * KV_W:D_MODEL] = g_bsuv[:, 0:2 * KV_W]
        buf_a[3:4, 0:2 * SGU_W - 2 * KV_W] = g_bsuv[:, 2 * KV_W:2 * SGU_W]
        buf_a[4:5, 0:128] = g_sink[...]
        buf_a[5:6, 0:SGU_W] = g_sln_g[...]
        buf_a[6:7, 0:SGU_W] = g_sln_b[...]
        for h in range(N_GRP):
            buf_b[h * BLK:(h + 1) * BLK, :] = g_sw[h]
        buf_b[N_GRP * BLK:_MAT_ROWS, :] = g_sbt[...]

        def remote(src, dst, send_sem, recv_sem, to):
            return pltpu.make_async_remote_copy(src_ref=src, dst_ref=dst, send_sem=send_sem, recv_sem=recv_sem,
                                                device_id=to, device_id_type=MESH)

        first = [remote(buf_a, pair_a, p1_send.at[0], p1_recv.at[0], sibling),
                 remote(buf_b, pair_b, p1_send.at[1], p1_recv.at[1], sibling)]
        for cp in first:
            cp.start()
        yield
        for cp in first:
            cp.wait()
        rows_a = pl.ds(pl.multiple_of(c * half_a, 8), half_a)
        rows_b = pl.ds(pl.multiple_of(c * half_b, 8), half_b)
        stage_a[me] = buf_a[rows_a, :] + pair_a[rows_a, :]
        stage_b[me] = buf_b[rows_b, :] + pair_b[rows_b, :]

        def chip_copies(d):
            to = (chips[d][0], chips[d][1], c)
            return [remote(stage_a.at[me], stage_a.at[me], x_send.at[2 * d], x_recv.at[2 * d], to),
                    remote(stage_b.at[me], stage_b.at[me], x_send.at[2 * d + 1], x_recv.at[2 * d + 1], to)]

        def chip_arrivals(d):
            slot = 2 * chips[d][0] + chips[d][1]
            to = (chips[d][0], chips[d][1], c)
            return [remote(stage_a.at[slot], stage_a.at[slot], x_send.at[2 * d], x_recv.at[2 * d], to),
                    remote(stage_b.at[slot], stage_b.at[slot], x_send.at[2 * d + 1], x_recv.at[2 * d + 1], to)]

        for d in range(3):
            for cp in chip_copies(d):
                cp.start()
        yield
        for d in range(3):
            for cp in chip_arrivals(d):
                cp.wait_recv()
        tot_a[rows_a, :] = ((stage_a[0] + stage_a[1]) + stage_a[2]) + stage_a[3]
        tot_b[rows_b, :] = ((stage_b[0] + stage_b[1]) + stage_b[2]) + stage_b[3]

        second = [remote(tot_a.at[rows_a, :], tot_a.at[rows_a, :], p2_send.at[0], p2_recv.at[0], sibling),
                  remote(tot_b.at[rows_b, :], tot_b.at[rows_b, :], p2_send.at[1], p2_recv.at[1], sibling)]
        for cp in second:
            cp.start()
        yield
        other_a = pl.ds(pl.multiple_of((1 - c) * half_a, 8), half_a)
        other_b = pl.ds(pl.multiple_of((1 - c) * half_b, 8), half_b)
        remote(tot_a.at[other_a, :], tot_a.at[other_a, :], p2_send.at[0], p2_recv.at[0], sibling).wait_recv()
        remote(tot_b.at[other_b, :], tot_b.at[other_b, :], p2_send.at[1], p2_recv.at[1], sibling).wait_recv()
        for cp in second:
            cp.wait_send()
        for d in range(3):
            for cp in chip_copies(d):
                cp.wait_send()
        out_a[...] = tot_a[...]
        out_b[...] = tot_b[...]

    vec = pltpu.VMEM((_VEC_ROWS, D_MODEL), F32)
    mat = pltpu.VMEM((_MAT_ROWS, 128), F32)
    scratch = [vec, mat, vec, mat, pltpu.VMEM((N_CHIP, _VEC_ROWS // 2, D_MODEL), F32),
               pltpu.VMEM((N_CHIP, _MAT_ROWS // 2, 128), F32), vec, mat,
               pltpu.SemaphoreType.DMA((2,)), pltpu.SemaphoreType.DMA((2,)), pltpu.SemaphoreType.DMA((6,)),
               pltpu.SemaphoreType.DMA((6,)), pltpu.SemaphoreType.DMA((2,)), pltpu.SemaphoreType.DMA((2,))]
    return body, scratch


def _small_adamw(tot_a, tot_b, params):
    shapes = [params[nm][0].shape for nm in _SMALL]

    def body(*refs):
        ta, tb = refs[:2]
        prm = refs[2:2 + 3 * len(_SMALL)]
        outs = refs[2 + 3 * len(_SMALL):]

        def grad_of(k, name):
            if name == "sgu_w":
                return [tb[h * BLK:(h + 1) * BLK, :] for h in range(N_GRP)]
            if name == "sgu_b":
                return jnp.transpose(tb[N_GRP * BLK:_MAT_ROWS, :])[0:N_GRP, :]
            row = _VEC_ROW[name]
            if name == "b_in":
                return jnp.concatenate([ta[row:row + 1, :], ta[row + 1:row + 2, 0:IN_W - D_MODEL]], axis=1)
            return ta[row:row + 1, 0:shapes[k][-1]]

        for k, name in enumerate(_SMALL):
            w_ref, m_ref, v_ref = prm[3 * k:3 * k + 3]
            g_out, d_out, m_out, v_out = outs[4 * k:4 * k + 4]
            g = grad_of(k, name)
            if name == "sgu_w":
                for h in range(N_GRP):
                    d_, m_, v_ = _adamw_math(w_ref[h], g[h], m_ref[h], v_ref[h])
                    g_out[h], d_out[h], m_out[h], v_out[h] = g[h], d_, m_, v_
            else:
                d_, m_, v_ = _adamw_math(w_ref[...], g, m_ref[...], v_ref[...])
                g_out[...], d_out[...], m_out[...], v_out[...] = g, d_, m_, v_
        outs[-1][...] = jnp.sum(ta[_LOSS_ROW:_LOSS_ROW + 1, :], axis=1, keepdims=True) * (0.5 / D_MODEL)

    ins = [tot_a, tot_b] + [_in_hbm(a) for nm in _SMALL for a in params[nm]]
    out_dims = [s for s in shapes for _ in range(4)] + [(1, 1)]
    res = pl.pallas_call(
        body, name="small_adamw", grid=(1,),
        in_specs=[_const2(a.shape) for a in ins], out_specs=[_const2(s) for s in out_dims],
        out_shape=[_hbm_shape(s, F32) for s in out_dims],
        compiler_params=_params(32),
    )(*ins)
    return {nm: tuple(res[4 * k:4 * k + 4]) for k, nm in enumerate(_SMALL)}, res[-1]


def _adamw_math(w, g, m, v):
    m = ADAM_B1 * m + (1.0 - ADAM_B1) * g
    v = ADAM_B2 * v + (1.0 - ADAM_B2) * (g * g)
    m_hat = m / (1.0 - ADAM_B1 ** ADAM_STEP)
    v_hat = v / (1.0 - ADAM_B2 ** ADAM_STEP)
    delta = -ADAM_LR * (m_hat / (jnp.sqrt(v_hat) + ADAM_EPS) + ADAM_WD * w)
    return delta, m, v


ADAMW_STEPS = 4


def _adamw(name, groups):
    k = len(groups)

    def body(*refs):
        for i in range(k):
            w_ref, g_ref, m_ref, v_ref = refs[4 * i:4 * i + 4]
            g = g_ref[...]
            for o_ref, o in zip(refs[4 * k + 4 * i:4 * k + 4 * i + 4], (g,) + _adamw_math(w_ref[...], g, m_ref[...], v_ref[...])):
                o_ref[...] = o

    specs = []
    for grp in groups:
        rows, cols = grp[0].shape
        assert rows % (8 * ADAMW_STEPS) == 0, rows
        specs += [pl.BlockSpec((rows // ADAMW_STEPS, cols), lambda i: (i, 0))] * 4
    res = pl.pallas_call(
        body, name=name, grid=(ADAMW_STEPS,), in_specs=specs, out_specs=specs,
        out_shape=[_hbm_shape(grp[0].shape, F32) for grp in groups for _ in range(4)],
        compiler_params=_params(56),
    )(*[_in_hbm(a) for grp in groups for a in grp])
    return [res[4 * i:4 * i + 4] for i in range(k)]


def kernel(x, positions, ln_in_g, ln_in_b, w_in, b_in, attn_sinks, sgu_ln_g, sgu_ln_b, sgu_w, sgu_b, w_out, b_out, ln_mix_g, ln_mix_b, w_gate, w_up, w_down, ln_ffn_g, ln_ffn_b, loss_target, m_ln_in_g, m_ln_in_b, m_w_in, m_b_in, m_attn_sinks, m_sgu_ln_g, m_sgu_ln_b, m_sgu_w, m_sgu_b, m_w_out, m_b_out, m_ln_mix_g, m_ln_mix_b, m_w_gate, m_w_up, m_w_down, m_ln_ffn_g, m_ln_ffn_b, v_ln_in_g, v_ln_in_b, v_w_in, v_b_in, v_attn_sinks, v_sgu_ln_g, v_sgu_ln_b, v_sgu_w, v_sgu_b, v_w_out, v_b_out, v_ln_mix_g, v_ln_mix_b, v_w_gate, v_w_up, v_w_down, v_ln_ffn_g, v_ln_ffn_b):
    weights = dict(ln_in_g=ln_in_g, ln_in_b=ln_in_b, w_in=w_in, b_in=b_in, attn_sinks=attn_sinks, sgu_ln_g=sgu_ln_g,
                   sgu_ln_b=sgu_ln_b, sgu_w=sgu_w, sgu_b=sgu_b, w_out=w_out, b_out=b_out, ln_mix_g=ln_mix_g,
                   ln_mix_b=ln_mix_b, w_gate=w_gate, w_up=w_up, w_down=w_down, ln_ffn_g=ln_ffn_g, ln_ffn_b=ln_ffn_b)
    mom_m = dict(ln_in_g=m_ln_in_g, ln_in_b=m_ln_in_b, w_in=m_w_in, b_in=m_b_in, attn_sinks=m_attn_sinks,
                 sgu_ln_g=m_sgu_ln_g, sgu_ln_b=m_sgu_ln_b, sgu_w=m_sgu_w, sgu_b=m_sgu_b, w_out=m_w_out, b_out=m_b_out,
                 ln_mix_g=m_ln_mix_g, ln_mix_b=m_ln_mix_b, w_gate=m_w_gate, w_up=m_w_up, w_down=m_w_down,
                 ln_ffn_g=m_ln_ffn_g, ln_ffn_b=m_ln_ffn_b)
    mom_v = dict(ln_in_g=v_ln_in_g, ln_in_b=v_ln_in_b, w_in=v_w_in, b_in=v_b_in, attn_sinks=v_attn_sinks,
                 sgu_ln_g=v_sgu_ln_g, sgu_ln_b=v_sgu_ln_b, sgu_w=v_sgu_w, sgu_b=v_sgu_b, w_out=v_w_out, b_out=v_b_out,
                 ln_mix_g=v_ln_mix_g, ln_mix_b=v_ln_mix_b, w_gate=v_w_gate, w_up=v_w_up, w_down=v_w_down,
                 ln_ffn_g=v_ln_ffn_g, ln_ffn_b=v_ln_ffn_b)
    order = list(weights)
    big = ("w_in", "w_out", "w_gate", "w_up", "w_down")

    s_len = x.shape[1]
    xs = _in_hbm(x.reshape(s_len, D_MODEL))
    tgt = _in_hbm(loss_target.reshape(s_len, D_MODEL))
    pos_row = _in_hbm(positions.reshape(1, s_len))
    g0, b0 = _in_hbm(ln_in_g.reshape(1, D_MODEL)), _in_hbm(ln_in_b.reshape(1, D_MODEL))
    sinks = attn_sinks.reshape(N_Q)
    sgu_w3 = _in_hbm(sgu_w.reshape(N_GRP, BLK, BLK))
    sgu_bt = _in_hbm(sgu_b.reshape(N_GRP, BLK).T)
    b_in, b_out, sgu_ln_g, sgu_ln_b, ln_mix_g, ln_mix_b, ln_ffn_g, ln_ffn_b = (
        _in_hbm(a) for a in (b_in, b_out, sgu_ln_g, sgu_ln_b, ln_mix_g, ln_mix_b, ln_ffn_g, ln_ffn_b))

    col_sharded = ("w_in", "w_gate", "w_up")

    def rowmajor(name, a):
        return jnp.swapaxes(a[0], 0, 1) if name in col_sharded else a[0]

    def as_given(name, a):
        return (jnp.swapaxes(a, 0, 1) if name in col_sharded else a)[None]

    shards = [rowmajor(n, weights[n]) for n in big]
    (gw_in,) = _gather_weights(shards[0:1])
    w_in_full = gw_in.reshape(IN_W, D_MODEL)

    sh_out, sh_gate, sh_up, sh_down = shards[1:]
    *acts, gw_out, gw_gate0 = _ln_inproj(xs, pos_row, g0, b0, w_in_full, b_in, _GatherPlan(
        [(sh_out, (0, OUT_SH), None), (sh_gate, (0, GATE_CUT), None)]))
    q, k, v, su, sv, tc, t1, t2 = (_in_hbm(a) for a in acts)
    mc, gw_gate, gw_up0 = _mixer_fwd(q, k, v, su, sv, sinks, sgu_ln_g, sgu_ln_b, sgu_w3, sgu_bt, _GatherPlan(
        [(sh_gate, (GATE_CUT, FF_SH), gw_gate0), (sh_up, (0, UP_CUT), None)]))
    mc = _in_hbm(mc)
    w_out_full = gw_out.reshape(D_MODEL, D_MODEL)
    r1, gw_up = _outproj(mc, w_out_full, b_out, xs, g0, b0, _GatherPlan([(sh_up, (UP_CUT, FF_SH), gw_up0)]))
    r1 = _in_hbm(r1)
    act, p_act, q_act, h1, gw_down = _ffn_up(r1, ln_mix_g, ln_mix_b, gw_gate, gw_up,
                                             _GatherPlan([(sh_down, (0, FF_SH), None)]))
    act, p_act, q_act = _in_hbm(act), _in_hbm(p_act), _in_hbm(q_act)
    dr2, loss_cols, d_ln_ffn_g, d_ln_ffn_b = _ffn_down_loss(act, gw_down, _in_hbm(h1), ln_ffn_g, ln_ffn_b, tgt)
    dr2 = _in_hbm(dr2)

    dg, du, wire_down, own_down = _ffn_bwd_a(dr2, act, p_act, q_act, gw_down)
    dh1a, wire_gate, own_gate, land_down = _ffn_bwd_g(dr2, _in_hbm(dg), r1, ln_mix_g, ln_mix_b, gw_gate, wire_down)
    dr1, wire_up, own_up, d_ln_mix_g, d_ln_mix_b, land_gate = _ffn_bwd_u(_in_hbm(dh1a), _in_hbm(du), r1, ln_mix_g,
                                                                         ln_mix_b, gw_up, wire_gate)
    dr1 = _in_hbm(dr1)
    dmc, wire_out, own_out, d_b_out = _outproj_bwd(dr1, mc, w_out_full)
    (dq, dkv, dsuv, dbq, dbkv, dbsuv, d_sink, d_sgu_ln_g, d_sgu_ln_b, d_sgu_w, d_sgu_bt, land_up, land_out) = _mixer_bwd(
        q, k, v, su, sv, _in_hbm(dmc), tc, t1, t2, sinks, sgu_ln_g, sgu_ln_b, sgu_w3, sgu_bt, [wire_up, wire_out])
    grad_x, acc_in, d_ln_in_g, d_ln_in_b = _inproj_bwd(_in_hbm(dq), _in_hbm(dkv), _in_hbm(dsuv), dr1, xs, g0, b0,
                                                       w_in_full)

    small_local = dict(
        ln_in_g=d_ln_in_g, ln_in_b=d_ln_in_b, bq=dbq, bkv=dbkv, bsuv=dbsuv, sink=d_sink, sgu_ln_g=d_sgu_ln_g,
        sgu_ln_b=d_sgu_ln_b, sgu_w=d_sgu_w, sgu_bt=d_sgu_bt, b_out=d_b_out, ln_mix_g=d_ln_mix_g, ln_mix_b=d_ln_mix_b,
        ln_ffn_g=d_ln_ffn_g, ln_ffn_b=d_ln_ffn_b, loss=loss_cols)
    *reduced, tot_a, tot_b = _grad_finish(acc_in, [land_out, land_gate, land_up, land_down],
                                          [own_out, own_gate, own_up, own_down], [small_local[nm] for nm in _SMALL_IN])
    small_shape = dict(ln_in_g=(1, D_MODEL), ln_in_b=(1, D_MODEL), sgu_w=(N_GRP, BLK, BLK), sgu_b=(N_GRP, BLK))
    small_params = {nm: tuple(src[nm].reshape(small_shape.get(nm, src[nm].shape)) for src in (weights, mom_m, mom_v))
                    for nm in _SMALL}
    small_out, loss = _small_adamw(_in_hbm(tot_a), _in_hbm(tot_b), small_params)
    loss = loss.reshape(())
    grads, delta, new_m, new_v = {}, {}, {}, {}
    for nm in _SMALL:
        grads[nm], delta[nm], new_m[nm], new_v[nm] = (a.reshape(weights[nm].shape) for a in small_out[nm])

    groups = [(shards[t], reduced[t], rowmajor(nm, mom_m[nm]), rowmajor(nm, mom_v[nm])) for t, nm in enumerate(big)]
    for nm, res in zip(big, _adamw("adamw", groups)):
        grads[nm], delta[nm], new_m[nm], new_v[nm] = (as_given(nm, a) for a in res)

    return (loss, grad_x.reshape(x.shape), *[grads[n] for n in order], *[delta[n] for n in order],
            *[new_m[n] for n in order], *[new_v[n] for n in order])
```

```python
import jax
import jax.numpy as jnp
from jax import lax
from jax.experimental import pallas as pl
from jax.experimental.pallas import tpu as pltpu

F32 = jnp.float32
_MXU = jnp.bfloat16
_WIRE = jnp.bfloat16
_ACT = jnp.bfloat16

D_MODEL = 1024
ATTN_W = 512
SGU_W = 512
HEAD_DIM = 64
N_Q = 8
N_KV = 2
Q_PER_KV = 4
KV_W = 128
BLK = 128
ROT_DIM = 16
ROPE_THETA = 500000.0
N_GRP = 4
GRP_DIM = 128
D_FF = 2816
IN_W = 1792
LN_EPS = 1e-5
ALPHA = 2.0 ** 0.25
N_CHIP = 4
FF_SH = D_FF // N_CHIP
IN_SH = IN_W // N_CHIP
OUT_SH = D_MODEL // N_CHIP
ROW_CHUNK = 32
GATE_CUT, UP_CUT = 352, 320

ADAM_LR = 0.001
ADAM_B1 = 0.9
ADAM_B2 = 0.999
ADAM_EPS = 1e-08
ADAM_WD = 0.01
ADAM_STEP = 10

SQRT_HALF = 0.7071067811865476
INV_SQRT_2PI = 0.3989422804014327
MESH_AXES = ("x", "y", "c")
MESH = pl.DeviceIdType.MESH
MIB = 2 ** 20


def _vmem():
    return pl.BlockSpec(memory_space=pltpu.VMEM)


def _smem():
    return pl.BlockSpec(memory_space=pltpu.SMEM)


def _hbm():
    return pl.BlockSpec(memory_space=pl.ANY)


def _hbm_shape(shape, dtype):
    return pltpu.HBM(shape, dtype)


def _in_hbm(a):
    return pltpu.with_memory_space_constraint(a, pltpu.HBM)


def _params(vmem_mib=48):
    return pltpu.CompilerParams(dimension_semantics=("arbitrary",), vmem_limit_bytes=vmem_mib * MIB)


def _tile(n, cap):
    if n <= cap:
        return n
    for t in range(cap - cap % 16, 0, -16):
        if n % t == 0:
            return t
    raise ValueError((n, cap))


def _rows(tm, width):
    return pl.BlockSpec((tm, width), lambda i: (i, 0))


def _const2(shape):
    return pl.BlockSpec(shape, lambda i: (0,) * len(shape))


def _ln(x, g, b):
    mu = jnp.mean(x, axis=-1, keepdims=True)
    xc = x - mu
    var = jnp.mean(xc * xc, axis=-1, keepdims=True)
    rstd = lax.rsqrt(var + LN_EPS)
    xhat = xc * rstd
    return xhat * g + b, xhat, rstd


def _ln_bwd(dy, xhat, rstd, g):
    gdy = dy * g
    m1 = jnp.mean(gdy, axis=-1, keepdims=True)
    m2 = jnp.mean(gdy * xhat, axis=-1, keepdims=True)
    return rstd * (gdy - m1 - xhat * m2)


def _colsum(a):
    return jnp.sum(a, axis=0, keepdims=True)


def _gelu_and_grad(x):
    cdf = 0.5 * (1.0 + lax.erf(x * SQRT_HALF))
    return x * cdf, cdf + x * jnp.exp(-0.5 * x * x) * INV_SQRT_2PI


def _dot(a, b):
    return jnp.dot(a, b, preferred_element_type=F32)


def _dot_nt(a, b):
    return lax.dot_general(a, b, (((1,), (1,)), ((), ())), preferred_element_type=F32)


def _dot_tn(a, b):
    return lax.dot_general(a, b, (((0,), (0,)), ((), ())), preferred_element_type=F32)


def _rope(t, tc, t1, t2):
    n = t.shape[1]
    rep = n // 128
    if rep > 1:
        tc, t1, t2 = (jnp.tile(a, (1, rep)) for a in (tc, t1, t2))
    return t * tc + pltpu.roll(t, n - 8, 1) * t1 + pltpu.roll(t, 8, 1) * t2


def _rope_bwd(d, tc, t1, t2):
    n = d.shape[1]
    rep = n // 128
    if rep > 1:
        tc, t1, t2 = (jnp.tile(a, (1, rep)) for a in (tc, t1, t2))
    return d * tc + pltpu.roll(d * t1, 8, 1) + pltpu.roll(d * t2, n - 8, 1)


def _causal_w(w_ref, h):
    t = lax.broadcasted_iota(jnp.int32, (BLK, BLK), 0)
    s = lax.broadcasted_iota(jnp.int32, (BLK, BLK), 1)
    return jnp.where(s <= t, w_ref[h], 0.0)


def _lane_put(vals, width):
    rows = vals[0].shape[0]
    lane = lax.broadcasted_iota(jnp.int32, (rows, width), 1)
    out = jnp.zeros((rows, width), F32)
    for k, v in enumerate(vals):
        out = out + jnp.where(lane == k, v, 0.0)
    return out


def _rope_consts():
    lane = jnp.arange(128) % HEAD_DIM
    rot = lane < ROT_DIM
    pair = (2 * (lane % (ROT_DIM // 2))).astype(F32)
    freq = jnp.where(rot, ROPE_THETA ** (-pair / ROT_DIM), 0.0)
    rows = [freq, rot.astype(F32), 1.0 - rot.astype(F32), (lane < ROT_DIM // 2).astype(F32),
            jnp.logical_and(lane >= ROT_DIM // 2, rot).astype(F32)]
    rows += [jnp.zeros((128,), F32)] * 3
    return jnp.stack(rows).astype(F32)


def _ln_inproj(x, pos_row, g0, b0, w_in_shard, b_in, plan):
    s_len = x.shape[0]
    tm = _tile(s_len, 512)
    m, n = len(plan.operands()), plan.n

    def body(x_ref, pos_ref, g_ref, b_ref, sh_ref, bi_ref, rc_ref, *rest):
        q_ref, k_ref, v_ref, su_ref, sv_ref, tc_ref, t1_ref, t2_ref, w_out_ref = rest[m:m + 9]
        w_buf, w_send, w_recv, w_flush = rest[-4:]
        gather = plan.bind(rest[:m], rest[m + 9:m + 9 + n], rest[m + 9 + n:-4])
        i = pl.program_id(0)
        keep = pltpu.make_async_copy(w_buf, w_out_ref, w_flush.at[0])

        @pl.when(i == 0)
        def _():
            w_gather = _Gather([sh_ref], [w_buf], w_send, w_recv)
            w_gather.start()
            gather.start()
            w_gather.pass_on()
            w_gather.finish()
            keep.start()

        h0, _, _ = _ln(x_ref[...], g_ref[...], b_ref[...])
        proj = _dot_nt(h0.astype(_MXU), w_buf[...].reshape(IN_W, D_MODEL)) + bi_ref[...]
        pos = jnp.broadcast_to(pos_ref[...].astype(F32), (128, tm))
        ang = jnp.transpose(pos) * rc_ref[0:1, :]
        cs = jnp.cos(ang)
        sn = jnp.sin(ang)
        tc = cs * rc_ref[1:2, :] + rc_ref[2:3, :]
        t1 = -sn * rc_ref[3:4, :]
        t2 = sn * rc_ref[4:5, :]
        tc_ref[...] = tc
        t1_ref[...] = t1
        t2_ref[...] = t2
        q = _rope(proj[:, 0:ATTN_W], tc, t1, t2) * (HEAD_DIM ** -0.5)
        q_ref[...] = q.astype(_MXU)
        k_ref[...] = _rope(proj[:, ATTN_W:ATTN_W + KV_W], tc, t1, t2).astype(_MXU)
        v_ref[...] = proj[:, ATTN_W + KV_W:ATTN_W + 2 * KV_W].astype(_MXU)
        su_ref[...] = proj[:, ATTN_W + 2 * KV_W:ATTN_W + 2 * KV_W + SGU_W]
        sv_ref[...] = proj[:, ATTN_W + 2 * KV_W + SGU_W:IN_W]

        last = pl.num_programs(0) - 1

        @pl.when(i == jnp.maximum(last - 1, 0))
        def _():
            gather.pass_on()

        @pl.when(i == last)
        def _():
            gather.finish()
            keep.wait()

    sd = _hbm_shape
    w_shape = (N_CHIP,) + w_in_shard.shape
    return pl.pallas_call(
        body, name="ln_inproj", grid=(s_len // tm,),
        in_specs=[_rows(tm, D_MODEL), pl.BlockSpec((1, tm), lambda i: (0, i)), _const2((1, D_MODEL)),
                  _const2((1, D_MODEL)), _vmem(),
                  _const2((1, IN_W)), _const2((8, 128))] + plan.in_specs(),
        out_specs=[_rows(tm, ATTN_W), _rows(tm, KV_W), _rows(tm, KV_W), _rows(tm, SGU_W), _rows(tm, SGU_W),
                   _rows(tm, 128), _rows(tm, 128), _rows(tm, 128), _hbm()] + plan.out_specs(),
        out_shape=[sd((s_len, ATTN_W), _MXU), sd((s_len, KV_W), _MXU), sd((s_len, KV_W), _MXU),
                   sd((s_len, SGU_W), F32), sd((s_len, SGU_W), F32),
                   sd((s_len, 128), F32), sd((s_len, 128), F32), sd((s_len, 128), F32), sd(w_shape, _WIRE)]
        + plan.out_shapes(),
        scratch_shapes=plan.scratch() + [pltpu.VMEM(w_shape, _WIRE)] + _Gather.sems(1) + [pltpu.SemaphoreType.DMA((1,))],
        compiler_params=_params(56),
    )(x, pos_row, g0, b0, w_in_shard, b_in, _rope_consts(), *plan.operands())


def _band_mask_t(first_block):
    kj = lax.broadcasted_iota(jnp.int32, (2 * BLK, BLK), 0)
    qi = lax.broadcasted_iota(jnp.int32, (2 * BLK, BLK), 1)
    shut = jnp.where(first_block, 2 * BLK, 0)
    prev_ok = jnp.logical_and(kj < BLK, kj > qi + shut)
    cur_ok = jnp.logical_and(kj >= BLK, (kj - BLK) <= qi)
    return jnp.logical_or(prev_ok, cur_ok)


def _attn_probs_t(kh, qh, sink, allowed_t):
    s = jnp.where(allowed_t, _dot_nt(kh, qh), -1e30)
    m = jnp.maximum(jnp.max(s, axis=0, keepdims=True), sink)
    p = jnp.exp(s - m)
    ps = jnp.exp(sink - m)
    inv = 1.0 / (jnp.sum(p, axis=0, keepdims=True) + ps)
    return p * inv, ps * inv


def _sgu_mix(gv, lg, lb, w_ref, bt_ref):
    vv, vhat, rstd = _ln(gv, lg, lb)
    vvb = vv.astype(_MXU)
    wcs, mixed = [], []
    for h in range(N_GRP):
        wc = _causal_w(w_ref, h).astype(_MXU)
        wcs.append(wc)
        mixed.append(_dot(wc, vvb[:, h * GRP_DIM:(h + 1) * GRP_DIM]) + bt_ref[:, h:h + 1])
    return jnp.concatenate(mixed, axis=1), vhat, rstd, vvb, wcs


def _mixer_fwd(q, k, v, su, sv, sinks, sg, sb, sgu_w, sgu_bt, plan):
    s_len = q.shape[0]
    nb = s_len // BLK
    per = 2 if nb % 2 == 0 else 1
    steps = nb // per
    m, n = len(plan.operands()), plan.n

    def body(q_ref, kc_ref, kp_ref, vc_ref, vp_ref, su_ref, sv_ref, sink_ref, lg_ref, lb_ref, w_ref, bt_ref, *rest):
        mc_ref = rest[m]
        gather = plan.bind(rest[:m], rest[m + 1:m + 1 + n], rest[m + 1 + n:])
        i = pl.program_id(0)

        @pl.when(i == 0)
        def _():
            gather.start()

        @pl.when(i == max(steps - 2, 0))
        def _():
            gather.pass_on()

        @pl.when(i == steps - 1)
        def _():
            gather.finish()

        for s in range(per):
            rows = slice(s * BLK, (s + 1) * BLK)
            before = slice((s - 1) * BLK, s * BLK)
            k_prev = kp_ref[...] if s == 0 else kc_ref[before, :]
            v_prev = vp_ref[...] if s == 0 else vc_ref[before, :]
            allowed_t = _band_mask_t(i == 0 if s == 0 else False)
            kb = jnp.concatenate([k_prev, kc_ref[rows, :]], axis=0)
            vb = jnp.concatenate([v_prev, vc_ref[rows, :]], axis=0)
            qv = q_ref[rows, :]
            outs = []
            allowed_g = jnp.tile(allowed_t, (1, Q_PER_KV))
            for g in range(N_KV):
                heads = range(g * Q_PER_KV, (g + 1) * Q_PER_KV)
                kh = kb[:, g * HEAD_DIM:(g + 1) * HEAD_DIM]
                vh = vb[:, g * HEAD_DIM:(g + 1) * HEAD_DIM]
                q_g = jnp.concatenate([qv[:, h * HEAD_DIM:(h + 1) * HEAD_DIM] for h in heads], axis=0)
                sink_g = jnp.concatenate([jnp.full((1, BLK), sink_ref[h], F32) for h in heads], axis=1)
                probs_t, _ = _attn_probs_t(kh, q_g, sink_g, allowed_g)
                o_g = _dot_tn(probs_t.astype(_MXU), vh)
                outs += [o_g[hh * BLK:(hh + 1) * BLK, :] for hh in range(Q_PER_KV)]
            u = _gelu_and_grad(su_ref[rows, :])[0]
            gv = _gelu_and_grad(sv_ref[rows, :])[0]
            mixed = _sgu_mix(gv, lg_ref[...], lb_ref[...], w_ref, bt_ref)[0]
            mc_ref[rows, :] = jnp.concatenate(outs + [u * mixed], axis=1).astype(_MXU)

    cur = lambda w: pl.BlockSpec((per * BLK, w), lambda i: (i, 0))
    prev = lambda w: pl.BlockSpec((BLK, w), lambda i: (jnp.maximum(per * i - 1, 0), 0))
    return pl.pallas_call(
        body, name="mixer_fwd", grid=(steps,),
        in_specs=[cur(ATTN_W), cur(KV_W), prev(KV_W), cur(KV_W), prev(KV_W), cur(SGU_W), cur(SGU_W), _smem(),
                  _const2((1, SGU_W)), _const2((1, SGU_W)), _const2((N_GRP, BLK, BLK)), _const2((BLK, N_GRP))]
        + plan.in_specs(),
        out_specs=[cur(D_MODEL)] + plan.out_specs(),
        out_shape=[_hbm_shape((s_len, D_MODEL), _MXU)] + plan.out_shapes(),
        scratch_shapes=plan.scratch(),
        compiler_params=_params(56),
    )(q, k, k, v, v, su, sv, sinks, sg, sb, sgu_w, sgu_bt, *plan.operands())


def _outproj(mc, w_out, b_out, x, g0, b0, plan):
    s_len = x.shape[0]
    tm = _tile(s_len, 512)
    m, n = len(plan.operands()), plan.n

    def body(mc_ref, w_ref, bo_ref, x_ref, g_ref, b_ref, *rest):
        r1_ref = rest[m]
        gather = plan.bind(rest[:m], rest[m + 1:m + 1 + n], rest[m + 1 + n:])
        i = pl.program_id(0)

        @pl.when(i == 0)
        def _():
            gather.start()

        h0, _, _ = _ln(x_ref[...], g_ref[...], b_ref[...])
        r1_ref[...] = ALPHA * h0 + (_dot(mc_ref[...], w_ref[...]) + bo_ref[...])

        last = pl.num_programs(0) - 1

        @pl.when(i == jnp.maximum(last - 1, 0))
        def _():
            gather.pass_on()

        @pl.when(i == last)
        def _():
            gather.finish()

    return pl.pallas_call(
        body, name="outproj", grid=(s_len // tm,),
        in_specs=[_rows(tm, D_MODEL), _vmem(), _const2((1, D_MODEL)), _rows(tm, D_MODEL),
                  _const2((1, D_MODEL)), _const2((1, D_MODEL))] + plan.in_specs(),
        out_specs=[_rows(tm, D_MODEL)] + plan.out_specs(),
        out_shape=[_hbm_shape((s_len, D_MODEL), F32)] + plan.out_shapes(),
        scratch_shapes=plan.scratch(),
        compiler_params=_params(40),
    )(mc, w_out, b_out, x, g0, b0, *plan.operands())


def _ffn_spec(tm):
    return pl.BlockSpec((N_CHIP, tm, FF_SH), lambda i: (0, i, 0))


def _ffn_up(r1, g1, b1, wg, wu, plan):
    s_len = r1.shape[0]
    tm = _tile(s_len, 512)
    m, n = len(plan.operands()), plan.n

    def body(r1_ref, g_ref, b_ref, wg_ref, wu_ref, *rest):
        a_ref, p_ref, q_ref, h1_ref = rest[m:m + 4]
        gather = plan.bind(rest[:m], rest[m + 4:m + 4 + n], rest[m + 4 + n:])
        i = pl.program_id(0)

        @pl.when(i == 0)
        def _():
            gather.start()

        h1, _, _ = _ln(r1_ref[...], g_ref[...], b_ref[...])
        h1_ref[...] = h1
        h1b = h1.astype(_MXU)
        for j in range(N_CHIP):
            g = _dot_nt(h1b, wg_ref[j])
            u = _dot_nt(h1b, wu_ref[j])
            silu, sg = _silu_parts(g)
            a_ref[j] = (silu * u).astype(_MXU)
            p_ref[j] = silu.astype(_ACT)
            q_ref[j] = (u * (sg * (1.0 + g * (1.0 - sg)))).astype(_ACT)

        last = pl.num_programs(0) - 1

        @pl.when(i == jnp.maximum(last - 1, 0))
        def _():
            gather.pass_on()

        @pl.when(i == last)
        def _():
            gather.finish()

    sd = _hbm_shape((N_CHIP, s_len, FF_SH), _ACT)
    return pl.pallas_call(
        body, name="ffn_up", grid=(s_len // tm,),
        in_specs=[_rows(tm, D_MODEL), _const2((1, D_MODEL)), _const2((1, D_MODEL)), _vmem(), _vmem()] + plan.in_specs(),
        out_specs=[_ffn_spec(tm)] * 3 + [_rows(tm, D_MODEL)] + plan.out_specs(),
        out_shape=[_hbm_shape((N_CHIP, s_len, FF_SH), _MXU), sd, sd, _hbm_shape((s_len, D_MODEL), F32)]
        + plan.out_shapes(),
        scratch_shapes=plan.scratch(),
        compiler_params=_params(56),
    )(r1, g1, b1, wg, wu, *plan.operands())


def _silu_parts(g):
    sg = 1.0 / (1.0 + jnp.exp(-g))
    return g * sg, sg


def _ffn_down_loss(act, wd, h1, g2, b2, target):
    s_len = h1.shape[0]
    tm = _tile(s_len, 512)

    parts = 2 if tm % 32 == 0 else 1
    sub = tm // parts

    def body(a_ref, wd_ref, h1_ref, g2_ref, b2_ref, t_ref, dr2_ref, loss_ref, dg2_ref, db2_ref):
        i = pl.program_id(0)

        @pl.when(i == 0)
        def _():
            loss_ref[...] = jnp.zeros_like(loss_ref)
            dg2_ref[...] = jnp.zeros_like(dg2_ref)
            db2_ref[...] = jnp.zeros_like(db2_ref)

        for part in range(parts):
            rows = slice(part * sub, (part + 1) * sub)
            f = jnp.zeros((sub, D_MODEL), F32)
            for j in range(N_CHIP):
                f = f + _dot(a_ref[j, rows, :], wd_ref[j])
            h2, r2hat, rstd2 = _ln(ALPHA * h1_ref[rows, :] + f, g2_ref[...], b2_ref[...])
            diff = h2 - t_ref[rows, :]
            dh2 = diff * (1.0 / D_MODEL)
            loss_ref[...] += _colsum(diff * diff)
            dg2_ref[...] += _colsum(dh2 * r2hat)
            db2_ref[...] += _colsum(dh2)
            dr2_ref[rows, :] = _ln_bwd(dh2, r2hat, rstd2, g2_ref[...])

    vec = _hbm_shape((1, D_MODEL), F32)
    c = _const2((1, D_MODEL))
    return pl.pallas_call(
        body, name="ffn_down_loss", grid=(s_len // tm,),
        in_specs=[_ffn_spec(tm), _vmem(), _rows(tm, D_MODEL), c, c, _rows(tm, D_MODEL)],
        out_specs=[_rows(tm, D_MODEL), c, c, c],
        out_shape=[_hbm_shape((s_len, D_MODEL), F32), vec, vec, vec],
        compiler_params=_params(48),
    )(act, wd, h1, g2, b2, target)


def _ffn_bwd_a(dr2, act, p_act, q_act, wd):
    s_len = dr2.shape[0]
    tm = _tile(s_len, 512)

    def body(dr2_ref, a_ref, p_ref, q_ref, wd_ref, dg_ref, du_ref, wire_ref, own_ref,
             dwd_ref, land_ref, send_sem, recv_sem):
        i = pl.program_id(0)

        @pl.when(i == 0)
        def _():
            dwd_ref[...] = jnp.zeros_like(dwd_ref)

        dfb = dr2_ref[...].astype(_MXU)
        for j in range(N_CHIP):
            da = _dot_nt(dfb, wd_ref[j])
            dg_ref[j] = (da * q_ref[j].astype(F32)).astype(_MXU)
            du_ref[j] = (da * p_ref[j].astype(F32)).astype(_MXU)
            dwd_ref[j * FF_SH:(j + 1) * FF_SH, :] += _dot_tn(a_ref[j], dfb)

        @pl.when(i == pl.num_programs(0) - 1)
        def _():
            _pair_reduce(dwd_ref, wire_ref, own_ref, land_ref, send_sem, recv_sem)

    sd = _hbm_shape((N_CHIP, s_len, FF_SH), _MXU)
    half = (N_CHIP, FF_SH // 2, D_MODEL)
    return pl.pallas_call(
        body, name="ffn_bwd_a", grid=(s_len // tm,),
        in_specs=[_rows(tm, D_MODEL), _ffn_spec(tm), _ffn_spec(tm), _ffn_spec(tm), _vmem()],
        out_specs=[_ffn_spec(tm), _ffn_spec(tm), _vmem(), _vmem()],
        out_shape=[sd, sd] + _pair_out_shapes(half),
        scratch_shapes=_pair_scratch((D_FF, D_MODEL), half),
        compiler_params=_params(61),
    )(dr2, act, p_act, q_act, wd)


def _ffn_bwd_g(dr2, dg, r1, g1, b1, wg, prev_wire):
    s_len = dr2.shape[0]
    tm = _tile(s_len, 512)

    def body(dr2_ref, dg_ref, r1_ref, g1_ref, b1_ref, wg_ref, pw_ref, dh1_ref, wire_ref, own_ref, pl_ref,
             dwg_ref, land_ref, send_sem, recv_sem, xl_ref, x_send, x_recv, x_flush):
        i = pl.program_id(0)
        exchange = _ChipExchange(pw_ref, xl_ref, x_send, x_recv)

        @pl.when(i == 0)
        def _():
            exchange.start()
            dwg_ref[...] = jnp.zeros_like(dwg_ref)

        h1, _, _ = _ln(r1_ref[...], g1_ref[...], b1_ref[...])
        h1b = h1.astype(_MXU)
        dh1 = ALPHA * dr2_ref[...]
        for j in range(N_CHIP):
            dgj = dg_ref[j]
            dh1 = dh1 + _dot(dgj, wg_ref[j])
            dwg_ref[j * FF_SH:(j + 1) * FF_SH, :] += _dot_tn(dgj, h1b)
        dh1_ref[...] = dh1

        @pl.when(i == pl.num_programs(0) - 1)
        def _():
            exchange.flush_start(pl_ref, x_flush)
            _pair_reduce(dwg_ref, wire_ref, own_ref, land_ref, send_sem, recv_sem)
            exchange.flush_finish(pl_ref, x_flush)

    c = _const2((1, D_MODEL))
    half = (N_CHIP, FF_SH // 2, D_MODEL)
    return pl.pallas_call(
        body, name="ffn_bwd_g", grid=(s_len // tm,),
        in_specs=[_rows(tm, D_MODEL), _ffn_spec(tm), _rows(tm, D_MODEL), c, c, _vmem(), _vmem()],
        out_specs=[_rows(tm, D_MODEL), _vmem(), _vmem(), _hbm()],
        out_shape=[_hbm_shape((s_len, D_MODEL), F32)] + _pair_out_shapes(half) + [_ChipExchange.land_shape(prev_wire)],
        scratch_shapes=_pair_scratch((D_FF, D_MODEL), half) + _ChipExchange.scratch(prev_wire),
        compiler_params=_params(58),
    )(dr2, dg, r1, g1, b1, wg, prev_wire)


def _ffn_bwd_u(dh1a, du, r1, g1, b1, wu, prev_wire):
    s_len = dh1a.shape[0]
    tm = _tile(s_len, 512)

    def body(dh1_ref, du_ref, r1_ref, g1_ref, b1_ref, wu_ref, pw_ref,
             dr1_ref, wire_ref, own_ref, dg1_ref, db1_ref, pl_ref,
             dwu_ref, land_ref, send_sem, recv_sem, xl_ref, x_send, x_recv, x_flush):
        i = pl.program_id(0)
        exchange = _ChipExchange(pw_ref, xl_ref, x_send, x_recv)

        @pl.when(i == 0)
        def _():
            exchange.start()
            dwu_ref[...] = jnp.zeros_like(dwu_ref)
            dg1_ref[...] = jnp.zeros_like(dg1_ref)
            db1_ref[...] = jnp.zeros_like(db1_ref)

        h1, r1hat, rstd1 = _ln(r1_ref[...], g1_ref[...], b1_ref[...])
        h1b = h1.astype(_MXU)
        dh1 = dh1_ref[...]
        for j in range(N_CHIP):
            duj = du_ref[j]
            dh1 = dh1 + _dot(duj, wu_ref[j])
            dwu_ref[j * FF_SH:(j + 1) * FF_SH, :] += _dot_tn(duj, h1b)
        dg1_ref[...] += _colsum(dh1 * r1hat)
        db1_ref[...] += _colsum(dh1)
        dr1_ref[...] = _ln_bwd(dh1, r1hat, rstd1, g1_ref[...])

        @pl.when(i == pl.num_programs(0) - 1)
        def _():
            exchange.flush_start(pl_ref, x_flush)
            _pair_reduce(dwu_ref, wire_ref, own_ref, land_ref, send_sem, recv_sem)
            exchange.flush_finish(pl_ref, x_flush)

    vec = _hbm_shape((1, D_MODEL), F32)
    c = _const2((1, D_MODEL))
    half = (N_CHIP, FF_SH // 2, D_MODEL)
    return pl.pallas_call(
        body, name="ffn_bwd_u", grid=(s_len // tm,),
        in_specs=[_rows(tm, D_MODEL), _ffn_spec(tm), _rows(tm, D_MODEL), c, c, _vmem(), _vmem()],
        out_specs=[_rows(tm, D_MODEL), _vmem(), _vmem(), c, c, _hbm()],
        out_shape=[_hbm_shape((s_len, D_MODEL), F32)] + _pair_out_shapes(half)
        + [vec, vec, _ChipExchange.land_shape(prev_wire)],
        scratch_shapes=_pair_scratch((D_FF, D_MODEL), half) + _ChipExchange.scratch(prev_wire),
        compiler_params=_params(58),
    )(dh1a, du, r1, g1, b1, wu, prev_wire)


def _outproj_bwd(dr1, mc, w_out):
    s_len = dr1.shape[0]
    tm = _tile(s_len, 512)

    def body(dr1_ref, mc_ref, w_ref, dmc_ref, wire_ref, own_ref, db_ref, dw_ref, land_ref, send_sem, recv_sem):
        i = pl.program_id(0)

        @pl.when(i == 0)
        def _():
            dw_ref[...] = jnp.zeros_like(dw_ref)
            db_ref[...] = jnp.zeros_like(db_ref)

        d = dr1_ref[...]
        db_ref[...] += _colsum(d)
        db16 = d.astype(_MXU)
        dmc_ref[...] = _dot_nt(db16, w_ref[...])
        dw_ref[...] += _dot_tn(mc_ref[...], db16)

        @pl.when(i == pl.num_programs(0) - 1)
        def _():
            _pair_reduce(dw_ref, wire_ref, own_ref, land_ref, send_sem, recv_sem)

    half = (N_CHIP, OUT_SH // 2, D_MODEL)
    return pl.pallas_call(
        body, name="outproj_bwd", grid=(s_len // tm,),
        in_specs=[_rows(tm, D_MODEL), _rows(tm, D_MODEL), _vmem()],
        out_specs=[_rows(tm, D_MODEL), _vmem(), _vmem(), _const2((1, D_MODEL))],
        out_shape=[_hbm_shape((s_len, D_MODEL), F32)] + _pair_out_shapes(half) + [_hbm_shape((1, D_MODEL), F32)],
        scratch_shapes=_pair_scratch((D_MODEL, D_MODEL), half),
        compiler_params=_params(48),
    )(dr1, mc, w_out)


def _mixer_bwd(q, k, v, su, sv, dmc, tc, t1, t2, sinks, sg, sb, sgu_w, sgu_bt, prev_wires):
    s_len = q.shape[0]
    nb = s_len // BLK
    per = next(p for p in (4, 2, 1) if nb % p == 0)
    steps = nb // per

    def body(q_ref, kc_ref, kp_ref, vc_ref, vp_ref, su_ref, sv_ref, dmc_ref,
             tc_ref, t1_ref, t2_ref, tcp_ref, t1p_ref, t2p_ref,
             sink_ref, lg_ref, lb_ref, w_ref, bt_ref, pw0_ref, pw1_ref,
             dq_ref, dkv_ref, dsuv_ref, dbq_ref, dbkv_ref, dbsuv_ref,
             dsink_ref, dlg_ref, dlb_ref, dw_ref, dbt_ref, pl0_ref, pl1_ref, carry_ref,
             xl0_ref, x0_send, x0_recv, x0_flush, xl1_ref, x1_send, x1_recv, x1_flush):
        i = pl.program_id(0)
        exchanges = [(_ChipExchange(pw0_ref, xl0_ref, x0_send, x0_recv), pl0_ref, x0_flush),
                     (_ChipExchange(pw1_ref, xl1_ref, x1_send, x1_recv), pl1_ref, x1_flush)]

        @pl.when(i == 0)
        def _():
            for exchange, _, _ in exchanges:
                exchange.start()

        @pl.when(i == 0)
        def _():
            for r in (dbq_ref, dbkv_ref, dbsuv_ref, dsink_ref, dlg_ref, dlb_ref, dw_ref, dbt_ref, carry_ref):
                r[...] = jnp.zeros_like(r)

        def emit_kv(fin, t):
            if t == 0:
                tables = (tcp_ref[...], t1p_ref[...], t2p_ref[...])
            else:
                before = slice((t - 1) * BLK, t * BLK)
                tables = (tc_ref[before, :], t1_ref[before, :], t2_ref[before, :])
            dk = _rope_bwd(fin[:, 0:KV_W], *tables)
            out = jnp.concatenate([dk, fin[:, KV_W:2 * KV_W]], axis=1)
            dkv_ref[t * BLK:(t + 1) * BLK, :] = out.astype(_MXU)
            dbkv_ref[...] += _colsum(out)

        def one_block(s):
            rows = slice(s * BLK, (s + 1) * BLK)
            before = slice((s - 1) * BLK, s * BLK)
            k_prev = kp_ref[...] if s == 0 else kc_ref[before, :]
            v_prev = vp_ref[...] if s == 0 else vc_ref[before, :]
            allowed_t = _band_mask_t(i == 0 if s == 0 else False)
            kb = jnp.concatenate([k_prev, kc_ref[rows, :]], axis=0)
            vb = jnp.concatenate([v_prev, vc_ref[rows, :]], axis=0)
            qv = q_ref[rows, :]
            dmc = dmc_ref[rows, :]
            dqs, dks, dvs, dsinks = [], [], [], []
            allowed_g = jnp.tile(allowed_t, (1, Q_PER_KV))
            for g in range(N_KV):
                heads = range(g * Q_PER_KV, (g + 1) * Q_PER_KV)
                kh = kb[:, g * HEAD_DIM:(g + 1) * HEAD_DIM]
                vh = vb[:, g * HEAD_DIM:(g + 1) * HEAD_DIM]
                q_g = jnp.concatenate([qv[:, h * HEAD_DIM:(h + 1) * HEAD_DIM] for h in heads], axis=0)
                do_g = jnp.concatenate([dmc[:, h * HEAD_DIM:(h + 1) * HEAD_DIM] for h in heads], axis=0).astype(_MXU)
                sink_g = jnp.concatenate([jnp.full((1, BLK), sink_ref[h], F32) for h in heads], axis=1)
                probs_t, psink = _attn_probs_t(kh, q_g, sink_g, allowed_g)
                dvs.append(_dot(probs_t.astype(_MXU), do_g))
                dp_t = _dot_nt(vh, do_g)
                rd = jnp.sum(probs_t * dp_t, axis=0, keepdims=True)
                ds_t = (probs_t * (dp_t - rd)).astype(_MXU)
                ps_rd = psink * rd
                for hh in range(Q_PER_KV):
                    dsinks.append(-jnp.sum(ps_rd[:, hh * BLK:(hh + 1) * BLK], axis=1, keepdims=True))
                dq_g = _dot_tn(ds_t, kh)
                dqs += [dq_g[hh * BLK:(hh + 1) * BLK, :] for hh in range(Q_PER_KV)]
                dks.append(_dot(ds_t, q_g))
            dq = _rope_bwd(jnp.concatenate(dqs, axis=1) * (HEAD_DIM ** -0.5),
                           tc_ref[rows, :], t1_ref[rows, :], t2_ref[rows, :])
            dq_ref[rows, :] = dq.astype(_MXU)
            dbq_ref[...] += _colsum(dq)
            dsink_ref[...] += _lane_put(dsinks, 128)
            contrib = jnp.concatenate(dks + dvs, axis=1)

            lg = lg_ref[...]
            u, du_dsu = _gelu_and_grad(su_ref[rows, :])
            gv, dgv_dsv = _gelu_and_grad(sv_ref[rows, :])
            mixed, vhat, rstd, vvb, wcs = _sgu_mix(gv, lg, lb_ref[...], w_ref, bt_ref)
            dsgu = dmc[:, ATTN_W:D_MODEL]
            dsu = dsgu * mixed * du_dsu
            dmixed = dsgu * u
            tri_t = lax.broadcasted_iota(jnp.int32, (BLK, BLK), 0)
            tri_s = lax.broadcasted_iota(jnp.int32, (BLK, BLK), 1)
            dvv, dbs = [], []
            for h in range(N_GRP):
                dm = dmixed[:, h * GRP_DIM:(h + 1) * GRP_DIM]
                dmb = dm.astype(_MXU)
                dbs.append(jnp.sum(dm, axis=1, keepdims=True))
                dw_ref[h] += jnp.where(tri_s <= tri_t, _dot_nt(dmb, vvb[:, h * GRP_DIM:(h + 1) * GRP_DIM]), 0.0)
                dvv.append(_dot_tn(wcs[h], dmb))
            dvv = jnp.concatenate(dvv, axis=1)
            dbt_ref[...] += _lane_put(dbs, 128)
            dlg_ref[...] += _colsum(dvv * vhat)
            dlb_ref[...] += _colsum(dvv)
            dsv = _ln_bwd(dvv, vhat, rstd, lg) * dgv_dsv
            dsuv = jnp.concatenate([dsu, dsv], axis=1)
            dsuv_ref[rows, :] = dsuv.astype(_MXU)
            dbsuv_ref[...] += _colsum(dsuv)
            return contrib

        @pl.when(i < steps)
        def _():
            contribs = [one_block(s) for s in range(per)]
            for t in range(per):
                top = carry_ref[...] if t == 0 else contribs[t - 1][BLK:2 * BLK, :]
                emit_kv(top + contribs[t][0:BLK, :], t)
            carry_ref[...] = contribs[per - 1][BLK:2 * BLK, :]

        @pl.when(i == steps)
        def _():
            for exchange, landed, flush_sem in exchanges:
                exchange.flush_start(landed, flush_sem)
            emit_kv(carry_ref[...], 0)
            if per > 1:
                dkv_ref[BLK:per * BLK, :] = jnp.zeros(((per - 1) * BLK, 2 * KV_W), _MXU)
            for exchange, landed, flush_sem in exchanges:
                exchange.flush_finish(landed, flush_sem)

    last = steps - 1
    cur = lambda w: pl.BlockSpec((per * BLK, w), lambda i: (jnp.minimum(i, last), 0))
    prev = lambda w: pl.BlockSpec((BLK, w), lambda i: (jnp.clip(per * i - 1, 0, nb - 1), 0))
    shifted = pl.BlockSpec((per * BLK, 2 * KV_W), lambda i: (i, 0))
    sd = _hbm_shape
    return pl.pallas_call(
        body, name="mixer_bwd", grid=(steps + 1,),
        in_specs=[cur(ATTN_W), cur(KV_W), prev(KV_W), cur(KV_W), prev(KV_W), cur(SGU_W), cur(SGU_W), cur(D_MODEL),
                  cur(128), cur(128), cur(128), prev(128), prev(128), prev(128),
                  _smem(), _const2((1, SGU_W)), _const2((1, SGU_W)), _const2((N_GRP, BLK, BLK)), _const2((BLK, N_GRP)),
                  _vmem(), _vmem()],
        out_specs=[cur(ATTN_W), shifted, cur(2 * SGU_W),
                   _const2((1, ATTN_W)), _const2((1, 2 * KV_W)), _const2((1, 2 * SGU_W)),
                   _const2((1, 128)), _const2((1, SGU_W)), _const2((1, SGU_W)),
                   _const2((N_GRP, BLK, BLK)), _const2((BLK, 128)), _hbm(), _hbm()],
        out_shape=[sd((s_len, ATTN_W), _MXU), sd((s_len + per * BLK, 2 * KV_W), _MXU), sd((s_len, 2 * SGU_W), _MXU),
                   sd((1, ATTN_W), F32), sd((1, 2 * KV_W), F32), sd((1, 2 * SGU_W), F32),
                   sd((1, 128), F32), sd((1, SGU_W), F32), sd((1, SGU_W), F32),
                   sd((N_GRP, BLK, BLK), F32), sd((BLK, 128), F32)]
        + [_ChipExchange.land_shape(w) for w in prev_wires],
        scratch_shapes=[pltpu.VMEM((BLK, 2 * KV_W), F32)] + _ChipExchange.scratch(prev_wires[0])
        + _ChipExchange.scratch(prev_wires[1]),
        compiler_params=_params(40),
    )(q, k, k, v, v, su, sv, dmc, tc, t1, t2, tc, t1, t2, sinks, sg, sb, sgu_w, sgu_bt, *prev_wires)


def _inproj_bwd(dq, dkv_late, dsuv, dr1, x, g0, b0, w_in):
    s_len = x.shape[0]
    tm = _tile(s_len, 512)
    assert tm % BLK == 0
    per = tm // BLK
    cuts = ((0, ATTN_W), (ATTN_W, ATTN_W + 2 * KV_W), (ATTN_W + 2 * KV_W, IN_W))

    def body(dq_ref, *rest):
        dkv_refs = rest[:per]
        dsuv_ref, dr1_ref, x_ref, g_ref, b_ref, w_ref, dx_ref, dw_ref, dg_ref, db_ref = rest[per:]
        i = pl.program_id(0)

        @pl.when(i == 0)
        def _():
            dw_ref[...] = jnp.zeros_like(dw_ref)
            dg_ref[...] = jnp.zeros_like(dg_ref)
            db_ref[...] = jnp.zeros_like(db_ref)

        h0, xhat, rstd = _ln(x_ref[...], g_ref[...], b_ref[...])
        h0b = h0.astype(_MXU)
        dh0 = ALPHA * dr1_ref[...]
        dkv = jnp.concatenate([r[...] for r in dkv_refs], axis=0)
        for (lo, hi), d in zip(cuts, (dq_ref[...], dkv, dsuv_ref[...])):
            dh0 = dh0 + _dot(d, w_ref[lo:hi, :])
            dw_ref[lo:hi, :] += _dot_tn(d, h0b)
        dg_ref[...] += _colsum(dh0 * xhat)
        db_ref[...] += _colsum(dh0)
        dx_ref[...] = _ln_bwd(dh0, xhat, rstd, g_ref[...])

    vec = _hbm_shape((1, D_MODEL), F32)
    c = _const2((1, D_MODEL))
    return pl.pallas_call(
        body, name="inproj_bwd", grid=(s_len // tm,),
        in_specs=[_rows(tm, ATTN_W)]
        + [pl.BlockSpec((BLK, 2 * KV_W), lambda i, b=b: (i * per + b + 1, 0)) for b in range(per)]
        + [_rows(tm, 2 * SGU_W), _rows(tm, D_MODEL), _rows(tm, D_MODEL), c, c, _vmem()],
        out_specs=[_rows(tm, D_MODEL), _vmem(), c, c],
        out_shape=[_hbm_shape((s_len, D_MODEL), F32), jax.ShapeDtypeStruct((IN_W, D_MODEL), F32), vec, vec],
        compiler_params=_params(48),
    )(dq, *[dkv_late] * per, dsuv, dr1, x, g0, b0, w_in)


def _place():
    x, y, c = (lax.axis_index(a) for a in MESH_AXES)
    chips = [(1 - x, y), (x, 1 - y), (1 - x, 1 - y)]
    return x, y, c, chips


class _Gather:
    def __init__(self, ins, outs, send_sems, recv_sems, spans=None):
        self.ins, self.outs, self.send_sems, self.recv_sems = ins, outs, send_sems, recv_sems
        self.n = len(ins)
        self.spans = spans or [(0, r.shape[0]) for r in ins]
        self.halves = [(hi - lo) // 2 for lo, hi in self.spans]

    def _copy(self, k, t, slot, half, to):
        rows = pl.ds(pl.multiple_of(self.spans[t][0] + half * self.halves[t], 16), self.halves[t])
        piece = self.outs[t].at[slot, rows, :]
        return pltpu.make_async_remote_copy(src_ref=piece, dst_ref=piece, send_sem=self.send_sems.at[k],
                                            recv_sem=self.recv_sems.at[k], device_id=to, device_id_type=MESH)

    def _chip_copy(self, t, d, slot):
        x, y, c, chips = _place()
        return self._copy(3 * t + d, t, slot, c, (chips[d][0], chips[d][1], c))

    def _pass_copy(self, t, d, half):
        x, y, c, chips = _place()
        return self._copy(3 * self.n + 3 * t + d, t, 2 * chips[d][0] + chips[d][1], half, (x, y, 1 - c))

    def start(self):
        x, y, c, chips = _place()
        me = 2 * x + y
        for t in range(self.n):
            lo, hi = self.spans[t]
            self.outs[t][me, lo:hi, :] = self.ins[t][lo:hi, :].astype(_WIRE)
        for t in range(self.n):
            for d in range(3):
                self._chip_copy(t, d, me).start()

    def pass_on(self):
        x, y, c, chips = _place()
        for t in range(self.n):
            for d in range(3):
                self._chip_copy(t, d, 2 * chips[d][0] + chips[d][1]).wait_recv()
                self._pass_copy(t, d, c).start()

    def finish(self):
        x, y, c, chips = _place()
        me = 2 * x + y
        for t in range(self.n):
            for d in range(3):
                self._pass_copy(t, d, 1 - c).wait_recv()
        for t in range(self.n):
            for d in range(3):
                self._chip_copy(t, d, me).wait_send()
                self._pass_copy(t, d, c).wait_send()

    @staticmethod
    def out_shapes(shards, make=jax.ShapeDtypeStruct):
        return [make((N_CHIP,) + s.shape, _WIRE) for s in shards]

    @staticmethod
    def sems(n):
        return [pltpu.SemaphoreType.DMA((6 * n,)), pltpu.SemaphoreType.DMA((6 * n,))]


_FLUSHES_EARLY, _FLUSHES = 5, 8


class _GatherPlan:
    def __init__(self, pieces):
        self.shards = [p[0] for p in pieces]
        self.spans = [p[1] for p in pieces]
        self.earlier = [p[2] for p in pieces]
        self.n = len(pieces)
        self.carried = [t for t in range(self.n) if self.earlier[t] is not None]

    def operands(self):
        return self.shards + [self.earlier[t] for t in self.carried]

    def in_specs(self):
        return [_vmem()] * self.n + [_hbm()] * len(self.carried)

    def out_specs(self):
        return [_hbm()] * self.n

    def out_shapes(self):
        return _Gather.out_shapes(self.shards, _hbm_shape)

    def scratch(self):
        return ([pltpu.VMEM((N_CHIP,) + s.shape, _WIRE) for s in self.shards] + _Gather.sems(self.n)
                + [pltpu.SemaphoreType.DMA((_FLUSHES * self.n,)), pltpu.SemaphoreType.DMA((max(len(self.carried), 1),))])

    def bind(self, in_refs, out_refs, scratch_refs):
        plan = self
        shard_refs, earlier_refs = in_refs[:self.n], in_refs[self.n:]
        bufs = scratch_refs[:self.n]
        send_sems, recv_sems, flush_sems, carry_sems = scratch_refs[self.n:self.n + 4]
        gather = _Gather(shard_refs, bufs, send_sems, recv_sems, self.spans)

        def carry_copy(k):
            t = plan.carried[k]
            lo = plan.spans[t][0]
            return pltpu.make_async_copy(earlier_refs[k].at[:, 0:lo, :], bufs[t].at[:, 0:lo, :], carry_sems.at[k])

        def flushes(t, late):
            x, y, c, chips = _place()
            lo, hi = plan.spans[t]
            half = (hi - lo) // 2
            others = [2 * chips[d][0] + chips[d][1] for d in range(3)]

            def half_rows(h):
                return pl.ds(pl.multiple_of(lo + h * half, 16), half)

            if late:
                parts = [(slot, half_rows(1 - c)) for slot in others]
            else:
                parts = [(2 * x + y, pl.ds(lo, hi - lo))] + [(slot, half_rows(c)) for slot in others]
                if lo:
                    parts.append((slice(None), pl.ds(0, lo)))
            first = _FLUSHES_EARLY if late else 0
            return [pltpu.make_async_copy(bufs[t].at[slot, rows, :], out_refs[t].at[slot, rows, :],
                                          flush_sems.at[_FLUSHES * t + first + k]) for k, (slot, rows) in enumerate(parts)]

        class Bound:
            @staticmethod
            def start():
                for k in range(len(plan.carried)):
                    carry_copy(k).start()
                gather.start()

            @staticmethod
            def pass_on():
                gather.pass_on()
                for k in range(len(plan.carried)):
                    carry_copy(k).wait()
                for t in range(plan.n):
                    for cp in flushes(t, late=False):
                        cp.start()

            @staticmethod
            def finish():
                gather.finish()
                for t in range(plan.n):
                    for cp in flushes(t, late=True):
                        cp.start()
                for t in range(plan.n):
                    for cp in flushes(t, late=False) + flushes(t, late=True):
                        cp.wait()

        return Bound


class _ChipExchange:
    def __init__(self, wire_ref, land_ref, send_sems, recv_sems):
        self.wire, self.land, self.send_sems, self.recv_sems = wire_ref, land_ref, send_sems, recv_sems

    def _copy(self, d):
        x, y, c, chips = _place()
        return pltpu.make_async_remote_copy(
            src_ref=self.wire.at[2 * chips[d][0] + chips[d][1]], dst_ref=self.land.at[d],
            send_sem=self.send_sems.at[d], recv_sem=self.recv_sems.at[d],
            device_id=(chips[d][0], chips[d][1], c), device_id_type=MESH)

    def start(self):
        for d in range(3):
            self._copy(d).start()

    def wait_recv(self):
        for d in range(3):
            self._copy(d).wait_recv()

    def wait_send(self):
        for d in range(3):
            self._copy(d).wait_send()

    def _flush_copy(self, hbm_out, flush_sem):
        return pltpu.make_async_copy(self.land, hbm_out, flush_sem.at[0])

    def flush_start(self, hbm_out, flush_sem):
        self.wait_recv()
        self._flush_copy(hbm_out, flush_sem).start()

    def flush_finish(self, hbm_out, flush_sem):
        self._flush_copy(hbm_out, flush_sem).wait()
        self.wait_send()

    @staticmethod
    def land_shape(wire):
        return _hbm_shape((3,) + wire.shape[1:], wire.dtype)

    @staticmethod
    def sems():
        return [pltpu.SemaphoreType.DMA((3,)), pltpu.SemaphoreType.DMA((3,))]

    @staticmethod
    def scratch(wire):
        return ([pltpu.VMEM((3,) + wire.shape[1:], wire.dtype)] + _ChipExchange.sems() + [pltpu.SemaphoreType.DMA((1,))])


def _pair_out_shapes(half_shape):
    return [jax.ShapeDtypeStruct(half_shape, _WIRE), jax.ShapeDtypeStruct(half_shape[1:], F32)]


def _pair_scratch(acc_shape, half_shape):
    return [pltpu.VMEM(acc_shape, F32), pltpu.VMEM(half_shape, _WIRE),
            pltpu.SemaphoreType.DMA((N_CHIP,)), pltpu.SemaphoreType.DMA((N_CHIP,))]


def _pair_reduce(acc_ref, wire_ref, own_ref, land_ref, send_sems, recv_sems):
    rh = land_ref.shape[1]
    x, y, c, _ = _place()
    me = 2 * x + y
    copies = []
    for j in range(N_CHIP):
        def cast(r, carry, j=j):
            dst = pl.ds(pl.multiple_of(r * ROW_CHUNK, ROW_CHUNK), ROW_CHUNK)
            src = pl.ds(pl.multiple_of((2 * j + 1 - c) * rh + r * ROW_CHUNK, 8), ROW_CHUNK)
            wire_ref[j, dst, :] = acc_ref[src, :].astype(_WIRE)
            return carry

        lax.fori_loop(0, rh // ROW_CHUNK, cast, 0)
        cp = pltpu.make_async_remote_copy(src_ref=wire_ref.at[j], dst_ref=land_ref.at[j], send_sem=send_sems.at[j],
                                          recv_sem=recv_sems.at[j], device_id=(x, y, 1 - c), device_id_type=MESH)
        cp.start()
        copies.append(cp)
    for j in range(N_CHIP):
        copies[j].wait()

        def chunk(r, carry, j=j):
            theirs = pl.ds(pl.multiple_of(r * ROW_CHUNK, ROW_CHUNK), ROW_CHUNK)
            mine = pl.ds(pl.multiple_of((2 * j + c) * rh + r * ROW_CHUNK, 8), ROW_CHUNK)
            wire_ref[j, theirs, :] = (acc_ref[mine, :] + land_ref[j, theirs, :].astype(F32)).astype(_WIRE)
            return carry

        lax.fori_loop(0, rh // ROW_CHUNK, chunk, 0)

    def own_chunk(r, carry):
        theirs = pl.ds(pl.multiple_of(r * ROW_CHUNK, ROW_CHUNK), ROW_CHUNK)
        mine = pl.ds(pl.multiple_of((2 * me + c) * rh + r * ROW_CHUNK, 8), ROW_CHUNK)
        own_ref[theirs, :] = acc_ref[mine, :] + land_ref[me, theirs, :].astype(F32)
        return carry

    lax.fori_loop(0, rh // ROW_CHUNK, own_chunk, 0)


def _grad_finish(last_acc, lands, owns, small):
    n = len(owns) + 1
    halves = [last_acc.shape[0] // (2 * N_CHIP)] + [w.shape[1] for w in lands]
    widths = [last_acc.shape[1]] + [a.shape[1] for a in owns]
    small_body, small_scratch = _small_allreduce_parts()
    ns = len(small)

    def body(*refs):
        acc0, land, own = refs[0], (None,) + refs[1:n], (None,) + refs[n:2 * n - 1]
        refs = refs[2 * n - 1:]
        small_in, g_out, small_out = refs[:ns], refs[ns:ns + n], refs[ns + n:ns + n + 2]
        refs = refs[ns + n + 2:]
        pland0, wire0, land0, own0 = refs[0:4]
        p_send, p_recv, x_send, x_recv, pair_send, pair_recv = refs[4:10]
        g, flush_sems = refs[10:10 + n], refs[10 + n]
        small_refs = refs[11 + n:]
        land = (land0,) + land[1:]
        own = (own0,) + own[1:]
        x, y, c, chips = _place()
        me = 2 * x + y
        exchange = _ChipExchange(wire0, land0, x_send, x_recv)

        def half_rows(t, half):
            return pl.ds(pl.multiple_of(half * halves[t], 8), halves[t])

        def pair_copy(t, half):
            rows = g[t].at[half_rows(t, half), :]
            return pltpu.make_async_remote_copy(src_ref=rows, dst_ref=rows, send_sem=pair_send.at[t],
                                                recv_sem=pair_recv.at[t], device_id=(x, y, 1 - c), device_id_type=MESH)

        def flush(t):
            return pltpu.make_async_copy(g[t], g_out[t], flush_sems.at[t])

        small_rounds = small_body(*small_in, *small_out, *small_refs)
        next(small_rounds)
        _pair_reduce(acc0, wire0, own0, pland0, p_send, p_recv)
        next(small_rounds)
        exchange.start()

        for t in list(range(1, n)) + [0]:
            if t == 0:
                for done in range(1, n):
                    pair_copy(done, 1 - c).wait_recv()
                    flush(done).start()
                exchange.wait_recv()
            if t == min(2, n - 1):
                next(small_rounds)
            if t == min(4, n - 1):
                next(small_rounds, None)

            def chunk(r, carry, t=t):
                src = pl.ds(pl.multiple_of(r * ROW_CHUNK, ROW_CHUNK), ROW_CHUNK)
                dst = pl.ds(pl.multiple_of(c * halves[t] + r * ROW_CHUNK, 8), ROW_CHUNK)
                s = own[t][src, :]
                for d in range(3):
                    s = s + land[t][d, src, :].astype(F32)
                g[t][dst, :] = s
                return carry

            lax.fori_loop(0, halves[t] // ROW_CHUNK, chunk, 0)
            pair_copy(t, c).start()
        pair_copy(0, 1 - c).wait_recv()
        flush(0).start()
        for t in range(n):
            pair_copy(t, c).wait_send()
        exchange.wait_send()
        for t in range(n):
            flush(t).wait()

    half0 = (halves[0], widths[0])
    shapes = [(2 * h, w) for h, w in zip(halves, widths)]
    return pl.pallas_call(
        body, name="grad_finish",
        in_specs=[_vmem()] * (2 * n - 1 + ns), out_specs=[_hbm()] * n + [_vmem()] * 2,
        out_shape=[_hbm_shape(s, F32) for s in shapes] + [jax.ShapeDtypeStruct(s, F32) for s in _SMALL_OUT_DIMS],
        scratch_shapes=[pltpu.VMEM((N_CHIP,) + half0, _WIRE), pltpu.VMEM((N_CHIP,) + half0, _WIRE),
                        pltpu.VMEM((3,) + half0, _WIRE), pltpu.VMEM(half0, F32)]
        + [pltpu.SemaphoreType.DMA((N_CHIP,)), pltpu.SemaphoreType.DMA((N_CHIP,))]
        + _ChipExchange.sems()
        + [pltpu.SemaphoreType.DMA((n,)), pltpu.SemaphoreType.DMA((n,))]
        + [pltpu.VMEM(s, F32) for s in shapes] + [pltpu.SemaphoreType.DMA((n,))]
        + small_scratch,
        compiler_params=pltpu.CompilerParams(vmem_limit_bytes=56 * MIB),
    )(last_acc, *lands, *owns, *small)


_SMALL = ("ln_in_g", "ln_in_b", "b_in", "attn_sinks", "sgu_ln_g", "sgu_ln_b", "sgu_w", "sgu_b", "b_out",
          "ln_mix_g", "ln_mix_b", "ln_ffn_g", "ln_ffn_b")
_VEC_ROW = dict(ln_in_g=0, ln_in_b=1, b_in=2, attn_sinks=4, sgu_ln_g=5, sgu_ln_b=6, b_out=7, ln_mix_g=8, ln_mix_b=9,
                ln_ffn_g=10, ln_ffn_b=11)
_LOSS_ROW = 12
_VEC_ROWS = 16
_MAT_ROWS = N_GRP * BLK + BLK


_SMALL_IN = ("ln_in_g", "ln_in_b", "bq", "bkv", "bsuv", "sink", "sgu_ln_g", "sgu_ln_b", "sgu_w", "sgu_bt", "b_out",
             "ln_mix_g", "ln_mix_b", "ln_ffn_g", "ln_ffn_b", "loss")
_SMALL_OUT_DIMS = ((_VEC_ROWS, D_MODEL), (_MAT_ROWS, 128))


def _small_allreduce_parts():
    n_in = len(_SMALL_IN)

    def body(*refs):
        (g_ln_in_g, g_ln_in_b, g_bq, g_bkv, g_bsuv, g_sink, g_sln_g, g_sln_b, g_sw, g_sbt, g_bout,
         g_lmg, g_lmb, g_lfg, g_lfb, g_loss) = refs[:n_in]
        out_a, out_b = refs[n_in:n_in + 2]
        (buf_a, buf_b, pair_a, pair_b, stage_a, stage_b, tot_a, tot_b,
         p1_send, p1_recv, x_send, x_recv, p2_send, p2_recv) = refs[n_in + 2:]
        x, y, c, chips = _place()
        me = 2 * x + y
        sibling = (x, y, 1 - c)
        half_a, half_b = _VEC_ROWS // 2, _MAT_ROWS // 2

        buf_a[...] = jnp.zeros_like(buf_a)
        for row, ref in ((0, g_ln_in_g), (1, g_ln_in_b), (7, g_bout), (8, g_lmg), (9, g_lmb), (10, g_lfg), (11, g_lfb),
                         (_LOSS_ROW, g_loss)):
            buf_a[row:row + 1, :] = ref[...]
        buf_a[2:3, 0:ATTN_W] = g_bq[...]
        buf_a[2:3, ATTN_W:ATTN_W + 2 * KV_W] = g_bkv[...]
        buf_a[2:3, ATTN_W + 2 * KV_W:D_MODEL] = g_bsuv[:, 0:2 * KV_W]
        buf_a[3:4, 0:2 * SGU_W - 2 * KV_W] = g_bsuv[:, 2 * KV_W:2 * SGU_W]
        buf_a[4:5, 0:128] = g_sink[...]
        buf_a[5:6, 0:SGU_W] = g_sln_g[...]
        buf_a[6:7, 0:SGU_W] = g_sln_b[...]
        for h in range(N_GRP):
            buf_b[h * BLK:(h + 1) * BLK, :] = g_sw[h]
        buf_b[N_GRP * BLK:_MAT_ROWS, :] = g_sbt[...]

        def remote(src, dst, send_sem, recv_sem, to):
            return pltpu.make_async_remote_copy(src_ref=src, dst_ref=dst, send_sem=send_sem, recv_sem=recv_sem,
                                                device_id=to, device_id_type=MESH)

        first = [remote(buf_a, pair_a, p1_send.at[0], p1_recv.at[0], sibling),
                 remote(buf_b, pair_b, p1_send.at[1], p1_recv.at[1], sibling)]
        for cp in first:
            cp.start()
        yield
        for cp in first:
            cp.wait()
        rows_a = pl.ds(pl.multiple_of(c * half_a, 8), half_a)
        rows_b = pl.ds(pl.multiple_of(c * half_b, 8), half_b)
        stage_a[me] = buf_a[rows_a, :] + pair_a[rows_a, :]
        stage_b[me] = buf_b[rows_b, :] + pair_b[rows_b, :]

        def chip_copies(d):
            to = (chips[d][0], chips[d][1], c)
            return [remote(stage_a.at[me], stage_a.at[me], x_send.at[2 * d], x_recv.at[2 * d], to),
                    remote(stage_b.at[me], stage_b.at[me], x_send.at[2 * d + 1], x_recv.at[2 * d + 1], to)]

        def chip_arrivals(d):
            slot = 2 * chips[d][0] + chips[d][1]
            to = (chips[d][0], chips[d][1], c)
            return [remote(stage_a.at[slot], stage_a.at[slot], x_send.at[2 * d], x_recv.at[2 * d], to),
                    remote(stage_b.at[slot], stage_b.at[slot], x_send.at[2 * d + 1], x_recv.at[2 * d + 1], to)]

        for d in range(3):
            for cp in chip_copies(d):
                cp.start()
        yield
        for d in range(3):
            for cp in chip_arrivals(d):
                cp.wait_recv()
        tot_a[rows_a, :] = ((stage_a[0] + stage_a[1]) + stage_a[2]) + stage_a[3]
        tot_b[rows_b, :] = ((stage_b[0] + stage_b[1]) + stage_b[2]) + stage_b[3]

        second = [remote(tot_a.at[rows_a, :], tot_a.at[rows_a, :], p2_send.at[0], p2_recv.at[0], sibling),
                  remote(tot_b.at[rows_b, :], tot_b.at[rows_b, :], p2_send.at[1], p2_recv.at[1], sibling)]
        for cp in second:
            cp.start()
        yield
        other_a = pl.ds(pl.multiple_of((1 - c) * half_a, 8), half_a)
        other_b = pl.ds(pl.multiple_of((1 - c) * half_b, 8), half_b)
        remote(tot_a.at[other_a, :], tot_a.at[other_a, :], p2_send.at[0], p2_recv.at[0], sibling).wait_recv()
        remote(tot_b.at[other_b, :], tot_b.at[other_b, :], p2_send.at[1], p2_recv.at[1], sibling).wait_recv()
        for cp in second:
            cp.wait_send()
        for d in range(3):
            for cp in chip_copies(d):
                cp.wait_send()
        out_a[...] = tot_a[...]
        out_b[...] = tot_b[...]

    vec = pltpu.VMEM((_VEC_ROWS, D_MODEL), F32)
    mat = pltpu.VMEM((_MAT_ROWS, 128), F32)
    scratch = [vec, mat, vec, mat, pltpu.VMEM((N_CHIP, _VEC_ROWS // 2, D_MODEL), F32),
               pltpu.VMEM((N_CHIP, _MAT_ROWS // 2, 128), F32), vec, mat,
               pltpu.SemaphoreType.DMA((2,)), pltpu.SemaphoreType.DMA((2,)), pltpu.SemaphoreType.DMA((6,)),
               pltpu.SemaphoreType.DMA((6,)), pltpu.SemaphoreType.DMA((2,)), pltpu.SemaphoreType.DMA((2,))]
    return body, scratch


def _small_adamw(tot_a, tot_b, params):
    shapes = [params[nm][0].shape for nm in _SMALL]

    def body(*refs):
        ta, tb = refs[:2]
        prm = refs[2:2 + 3 * len(_SMALL)]
        outs = refs[2 + 3 * len(_SMALL):]

        def grad_of(k, name):
            if name == "sgu_w":
                return [tb[h * BLK:(h + 1) * BLK, :] for h in range(N_GRP)]
            if name == "sgu_b":
                return jnp.transpose(tb[N_GRP * BLK:_MAT_ROWS, :])[0:N_GRP, :]
            row = _VEC_ROW[name]
            if name == "b_in":
                return jnp.concatenate([ta[row:row + 1, :], ta[row + 1:row + 2, 0:IN_W - D_MODEL]], axis=1)
            return ta[row:row + 1, 0:shapes[k][-1]]

        for k, name in enumerate(_SMALL):
            w_ref, m_ref, v_ref = prm[3 * k:3 * k + 3]
            g_out, d_out, m_out, v_out = outs[4 * k:4 * k + 4]
            g = grad_of(k, name)
            if name == "sgu_w":
                for h in range(N_GRP):
                    d_, m_, v_ = _adamw_math(w_ref[h], g[h], m_ref[h], v_ref[h])
                    g_out[h], d_out[h], m_out[h], v_out[h] = g[h], d_, m_, v_
            else:
                d_, m_, v_ = _adamw_math(w_ref[...], g, m_ref[...], v_ref[...])
                g_out[...], d_out[...], m_out[...], v_out[...] = g, d_, m_, v_
        outs[-1][...] = jnp.sum(ta[_LOSS_ROW:_LOSS_ROW + 1, :], axis=1, keepdims=True) * (0.5 / D_MODEL)

    ins = [tot_a, tot_b] + [_in_hbm(a) for nm in _SMALL for a in params[nm]]
    out_dims = [s for s in shapes for _ in range(4)] + [(1, 1)]
    res = pl.pallas_call(
        body, name="small_adamw", grid=(1,),
        in_specs=[_const2(a.shape) for a in ins], out_specs=[_const2(s) for s in out_dims],
        out_shape=[_hbm_shape(s, F32) for s in out_dims],
        compiler_params=_params(32),
    )(*ins)
    return {nm: tuple(res[4 * k:4 * k + 4]) for k, nm in enumerate(_SMALL)}, res[-1]


def _adamw_math(w, g, m, v):
    m = ADAM_B1 * m + (1.0 - ADAM_B1) * g
    v = ADAM_B2 * v + (1.0 - ADAM_B2) * (g * g)
    m_hat = m / (1.0 - ADAM_B1 ** ADAM_STEP)
    v_hat = v / (1.0 - ADAM_B2 ** ADAM_STEP)
    delta = -ADAM_LR * (m_hat / (jnp.sqrt(v_hat) + ADAM_EPS) + ADAM_WD * w)
    return delta, m, v


ADAMW_STEPS = 4


def _adamw(name, groups):
    k = len(groups)

    def body(*refs):
        for i in range(k):
            w_ref, g_ref, m_ref, v_ref = refs[4 * i:4 * i + 4]
            g = g_ref[...]
            for o_ref, o in zip(refs[4 * k + 4 * i:4 * k + 4 * i + 4], (g,) + _adamw_math(w_ref[...], g, m_ref[...], v_ref[...])):
                o_ref[...] = o

    specs = []
    for grp in groups:
        rows, cols = grp[0].shape
        assert rows % (8 * ADAMW_STEPS) == 0, rows
        specs += [pl.BlockSpec((rows // ADAMW_STEPS, cols), lambda i: (i, 0))] * 4
    res = pl.pallas_call(
        body, name=name, grid=(ADAMW_STEPS,), in_specs=specs, out_specs=specs,
        out_shape=[_hbm_shape(grp[0].shape, F32) for grp in groups for _ in range(4)],
        compiler_params=_params(56),
    )(*[_in_hbm(a) for grp in groups for a in grp])
    return [res[4 * i:4 * i + 4] for i in range(k)]


def kernel(x, positions, ln_in_g, ln_in_b, w_in, b_in, attn_sinks, sgu_ln_g, sgu_ln_b, sgu_w, sgu_b, w_out, b_out, ln_mix_g, ln_mix_b, w_gate, w_up, w_down, ln_ffn_g, ln_ffn_b, loss_target, m_ln_in_g, m_ln_in_b, m_w_in, m_b_in, m_attn_sinks, m_sgu_ln_g, m_sgu_ln_b, m_sgu_w, m_sgu_b, m_w_out, m_b_out, m_ln_mix_g, m_ln_mix_b, m_w_gate, m_w_up, m_w_down, m_ln_ffn_g, m_ln_ffn_b, v_ln_in_g, v_ln_in_b, v_w_in, v_b_in, v_attn_sinks, v_sgu_ln_g, v_sgu_ln_b, v_sgu_w, v_sgu_b, v_w_out, v_b_out, v_ln_mix_g, v_ln_mix_b, v_w_gate, v_w_up, v_w_down, v_ln_ffn_g, v_ln_ffn_b):
    weights = dict(ln_in_g=ln_in_g, ln_in_b=ln_in_b, w_in=w_in, b_in=b_in, attn_sinks=attn_sinks, sgu_ln_g=sgu_ln_g,
                   sgu_ln_b=sgu_ln_b, sgu_w=sgu_w, sgu_b=sgu_b, w_out=w_out, b_out=b_out, ln_mix_g=ln_mix_g,
                   ln_mix_b=ln_mix_b, w_gate=w_gate, w_up=w_up, w_down=w_down, ln_ffn_g=ln_ffn_g, ln_ffn_b=ln_ffn_b)
    mom_m = dict(ln_in_g=m_ln_in_g, ln_in_b=m_ln_in_b, w_in=m_w_in, b_in=m_b_in, attn_sinks=m_attn_sinks,
                 sgu_ln_g=m_sgu_ln_g, sgu_ln_b=m_sgu_ln_b, sgu_w=m_sgu_w, sgu_b=m_sgu_b, w_out=m_w_out, b_out=m_b_out,
                 ln_mix_g=m_ln_mix_g, ln_mix_b=m_ln_mix_b, w_gate=m_w_gate, w_up=m_w_up, w_down=m_w_down,
                 ln_ffn_g=m_ln_ffn_g, ln_ffn_b=m_ln_ffn_b)
    mom_v = dict(ln_in_g=v_ln_in_g, ln_in_b=v_ln_in_b, w_in=v_w_in, b_in=v_b_in, attn_sinks=v_attn_sinks,
                 sgu_ln_g=v_sgu_ln_g, sgu_ln_b=v_sgu_ln_b, sgu_w=v_sgu_w, sgu_b=v_sgu_b, w_out=v_w_out, b_out=v_b_out,
                 ln_mix_g=v_ln_mix_g, ln_mix_b=v_ln_mix_b, w_gate=v_w_gate, w_up=v_w_up, w_down=v_w_down,
                 ln_ffn_g=v_ln_ffn_g, ln_ffn_b=v_ln_ffn_b)
    order = list(weights)
    big = ("w_in", "w_out", "w_gate", "w_up", "w_down")

    s_len = x.shape[1]
    xs = _in_hbm(x.reshape(s_len, D_MODEL))
    tgt = _in_hbm(loss_target.reshape(s_len, D_MODEL))
    pos_row = _in_hbm(positions.reshape(1, s_len))
    g0, b0 = _in_hbm(ln_in_g.reshape(1, D_MODEL)), _in_hbm(ln_in_b.reshape(1, D_MODEL))
    sinks = attn_sinks.reshape(N_Q)
    sgu_w3 = _in_hbm(sgu_w.reshape(N_GRP, BLK, BLK))
    sgu_bt = _in_hbm(sgu_b.reshape(N_GRP, BLK).T)
    b_in, b_out, sgu_ln_g, sgu_ln_b, ln_mix_g, ln_mix_b, ln_ffn_g, ln_ffn_b = (
        _in_hbm(a) for a in (b_in, b_out, sgu_ln_g, sgu_ln_b, ln_mix_g, ln_mix_b, ln_ffn_g, ln_ffn_b))

    col_sharded = ("w_in", "w_gate", "w_up")

    def rowmajor(name, a):
        return jnp.swapaxes(a[0], 0, 1) if name in col_sharded else a[0]

    def as_given(name, a):
        return (jnp.swapaxes(a, 0, 1) if name in col_sharded else a)[None]

    shards = [rowmajor(n, weights[n]) for n in big]

    sh_out, sh_gate, sh_up, sh_down = shards[1:]
    *acts, gw_in, gw_out, gw_gate0 = _ln_inproj(xs, pos_row, g0, b0, shards[0], b_in, _GatherPlan(
        [(sh_out, (0, OUT_SH), None), (sh_gate, (0, GATE_CUT), None)]))
    w_in_full = gw_in.reshape(IN_W, D_MODEL)
    q, k, v, su, sv, tc, t1, t2 = (_in_hbm(a) for a in acts)
    mc, gw_gate, gw_up0 = _mixer_fwd(q, k, v, su, sv, sinks, sgu_ln_g, sgu_ln_b, sgu_w3, sgu_bt, _GatherPlan(
        [(sh_gate, (GATE_CUT, FF_SH), gw_gate0), (sh_up, (0, UP_CUT), None)]))
    mc = _in_hbm(mc)
    w_out_full = gw_out.reshape(D_MODEL, D_MODEL)
    r1, gw_up = _outproj(mc, w_out_full, b_out, xs, g0, b0, _GatherPlan([(sh_up, (UP_CUT, FF_SH), gw_up0)]))
    r1 = _in_hbm(r1)
    act, p_act, q_act, h1, gw_down = _ffn_up(r1, ln_mix_g, ln_mix_b, gw_gate, gw_up,
                                             _GatherPlan([(sh_down, (0, FF_SH), None)]))
    act, p_act, q_act = _in_hbm(act), _in_hbm(p_act), _in_hbm(q_act)
    dr2, loss_cols, d_ln_ffn_g, d_ln_ffn_b = _ffn_down_loss(act, gw_down, _in_hbm(h1), ln_ffn_g, ln_ffn_b, tgt)
    dr2 = _in_hbm(dr2)

    dg, du, wire_down, own_down = _ffn_bwd_a(dr2, act, p_act, q_act, gw_down)
    dh1a, wire_gate, own_gate, land_down = _ffn_bwd_g(dr2, _in_hbm(dg), r1, ln_mix_g, ln_mix_b, gw_gate, wire_down)
    dr1, wire_up, own_up, d_ln_mix_g, d_ln_mix_b, land_gate = _ffn_bwd_u(_in_hbm(dh1a), _in_hbm(du), r1, ln_mix_g,
                                                                         ln_mix_b, gw_up, wire_gate)
    dr1 = _in_hbm(dr1)
    dmc, wire_out, own_out, d_b_out = _outproj_bwd(dr1, mc, w_out_full)
    (dq, dkv, dsuv, dbq, dbkv, dbsuv, d_sink, d_sgu_ln_g, d_sgu_ln_b, d_sgu_w, d_sgu_bt, land_up, land_out) = _mixer_bwd(
        q, k, v, su, sv, _in_hbm(dmc), tc, t1, t2, sinks, sgu_ln_g, sgu_ln_b, sgu_w3, sgu_bt, [wire_up, wire_out])
    grad_x, acc_in, d_ln_in_g, d_ln_in_b = _inproj_bwd(_in_hbm(dq), _in_hbm(dkv), _in_hbm(dsuv), dr1, xs, g0, b0,
                                                       w_in_full)

    small_local = dict(
        ln_in_g=d_ln_in_g, ln_in_b=d_ln_in_b, bq=dbq, bkv=dbkv, bsuv=dbsuv, sink=d_sink, sgu_ln_g=d_sgu_ln_g,
        sgu_ln_b=d_sgu_ln_b, sgu_w=d_sgu_w, sgu_bt=d_sgu_bt, b_out=d_b_out, ln_mix_g=d_ln_mix_g, ln_mix_b=d_ln_mix_b,
        ln_ffn_g=d_ln_ffn_g, ln_ffn_b=d_ln_ffn_b, loss=loss_cols)
    *reduced, tot_a, tot_b = _grad_finish(acc_in, [land_out, land_gate, land_up, land_down],
                                          [own_out, own_gate, own_up, own_down], [small_local[nm] for nm in _SMALL_IN])
    small_shape = dict(ln_in_g=(1, D_MODEL), ln_in_b=(1, D_MODEL), sgu_w=(N_GRP, BLK, BLK), sgu_b=(N_GRP, BLK))
    small_params = {nm: tuple(src[nm].reshape(small_shape.get(nm, src[nm].shape)) for src in (weights, mom_m, mom_v))
                    for nm in _SMALL}
    small_out, loss = _small_adamw(_in_hbm(tot_a), _in_hbm(tot_b), small_params)
    loss = loss.reshape(())
    grads, delta, new_m, new_v = {}, {}, {}, {}
    for nm in _SMALL:
        grads[nm], delta[nm], new_m[nm], new_v[nm] = (a.reshape(weights[nm].shape) for a in small_out[nm])

    groups = [(shards[t], reduced[t], rowmajor(nm, mom_m[nm]), rowmajor(nm, mom_v[nm])) for t, nm in enumerate(big)]
    for nm, res in zip(big, _adamw("adamw", groups)):
        grads[nm], delta[nm], new_m[nm], new_v[nm] = (as_given(nm, a) for a in res)

    return (loss, grad_x.reshape(x.shape), *[grads[n] for n in order], *[delta[n] for n in order],
            *[new_m[n] for n in order], *[new_v[n] for n in order])
```

```python
import jax
import jax.numpy as jnp
from jax import lax
from jax.experimental import pallas as pl
from jax.experimental.pallas import tpu as pltpu

F32 = jnp.float32
_MXU = jnp.bfloat16
_WIRE = jnp.bfloat16
_ACT = jnp.bfloat16

D_MODEL = 1024
ATTN_W = 512
SGU_W = 512
HEAD_DIM = 64
N_Q = 8
N_KV = 2
Q_PER_KV = 4
KV_W = 128
BLK = 128
ROT_DIM = 16
ROPE_THETA = 500000.0
N_GRP = 4
GRP_DIM = 128
D_FF = 2816
IN_W = 1792
LN_EPS = 1e-5
ALPHA = 2.0 ** 0.25
N_CHIP = 4
FF_SH = D_FF // N_CHIP
IN_SH = IN_W // N_CHIP
OUT_SH = D_MODEL // N_CHIP
ROW_CHUNK = 32
GATE_CUT, UP_CUT = 352, 320

ADAM_LR = 0.001
ADAM_B1 = 0.9
ADAM_B2 = 0.999
ADAM_EPS = 1e-08
ADAM_WD = 0.01
ADAM_STEP = 10

SQRT_HALF = 0.7071067811865476
INV_SQRT_2PI = 0.3989422804014327
MESH_AXES = ("x", "y", "c")
MESH = pl.DeviceIdType.MESH
MIB = 2 ** 20


def _vmem():
    return pl.BlockSpec(memory_space=pltpu.VMEM)


def _smem():
    return pl.BlockSpec(memory_space=pltpu.SMEM)


def _hbm():
    return pl.BlockSpec(memory_space=pl.ANY)


def _hbm_shape(shape, dtype):
    return pltpu.HBM(shape, dtype)


def _in_hbm(a):
    return pltpu.with_memory_space_constraint(a, pltpu.HBM)


def _params(vmem_mib=48):
    return pltpu.CompilerParams(dimension_semantics=("arbitrary",), vmem_limit_bytes=vmem_mib * MIB)


def _tile(n, cap):
    if n <= cap:
        return n
    for t in range(cap - cap % 16, 0, -16):
        if n % t == 0:
            return t
    raise ValueError((n, cap))


def _rows(tm, width):
    return pl.BlockSpec((tm, width), lambda i: (i, 0))


def _const2(shape):
    return pl.BlockSpec(shape, lambda i: (0,) * len(shape))


def _ln(x, g, b):
    mu = jnp.mean(x, axis=-1, keepdims=True)
    xc = x - mu
    var = jnp.mean(xc * xc, axis=-1, keepdims=True)
    rstd = lax.rsqrt(var + LN_EPS)
    xhat = xc * rstd
    return xhat * g + b, xhat, rstd


def _ln_bwd(dy, xhat, rstd, g):
    gdy = dy * g
    m1 = jnp.mean(gdy, axis=-1, keepdims=True)
    m2 = jnp.mean(gdy * xhat, axis=-1, keepdims=True)
    return rstd * (gdy - m1 - xhat * m2)


def _colsum(a):
    return jnp.sum(a, axis=0, keepdims=True)


def _gelu_and_grad(x):
    cdf = 0.5 * (1.0 + lax.erf(x * SQRT_HALF))
    return x * cdf, cdf + x * jnp.exp(-0.5 * x * x) * INV_SQRT_2PI


def _dot(a, b):
    return jnp.dot(a, b, preferred_element_type=F32)


def _dot_nt(a, b):
    return lax.dot_general(a, b, (((1,), (1,)), ((), ())), preferred_element_type=F32)


def _dot_tn(a, b):
    return lax.dot_general(a, b, (((0,), (0,)), ((), ())), preferred_element_type=F32)


def _rope(t, tc, t1, t2):
    n = t.shape[1]
    rep = n // 128
    if rep > 1:
        tc, t1, t2 = (jnp.tile(a, (1, rep)) for a in (tc, t1, t2))
    return t * tc + pltpu.roll(t, n - 8, 1) * t1 + pltpu.roll(t, 8, 1) * t2


def _rope_bwd(d, tc, t1, t2):
    n = d.shape[1]
    rep = n // 128
    if rep > 1:
        tc, t1, t2 = (jnp.tile(a, (1, rep)) for a in (tc, t1, t2))
    return d * tc + pltpu.roll(d * t1, 8, 1) + pltpu.roll(d * t2, n - 8, 1)


def _causal_w(w_ref, h):
    t = lax.broadcasted_iota(jnp.int32, (BLK, BLK), 0)
    s = lax.broadcasted_iota(jnp.int32, (BLK, BLK), 1)
    return jnp.where(s <= t, w_ref[h], 0.0)


def _lane_put(vals, width):
    rows = vals[0].shape[0]
    lane = lax.broadcasted_iota(jnp.int32, (rows, width), 1)
    out = jnp.zeros((rows, width), F32)
    for k, v in enumerate(vals):
        out = out + jnp.where(lane == k, v, 0.0)
    return out


def _rope_consts():
    lane = jnp.arange(128) % HEAD_DIM
    rot = lane < ROT_DIM
    pair = (2 * (lane % (ROT_DIM // 2))).astype(F32)
    freq = jnp.where(rot, ROPE_THETA ** (-pair / ROT_DIM), 0.0)
    rows = [freq, rot.astype(F32), 1.0 - rot.astype(F32), (lane < ROT_DIM // 2).astype(F32),
            jnp.logical_and(lane >= ROT_DIM // 2, rot).astype(F32)]
    rows += [jnp.zeros((128,), F32)] * 3
    return jnp.stack(rows).astype(F32)


def _ln_inproj(x, pos_row, g0, b0, w_in_shard, b_in, plan):
    s_len = x.shape[0]
    tm = _tile(s_len, 512)
    m, n = len(plan.operands()), plan.n

    def body(x_ref, pos_ref, g_ref, b_ref, sh_ref, bi_ref, rc_ref, *rest):
        q_ref, k_ref, v_ref, su_ref, sv_ref, tc_ref, t1_ref, t2_ref, w_out_ref = rest[m:m + 9]
        w_buf, w_send, w_recv, w_flush = rest[-4:]
        gather = plan.bind(rest[:m], rest[m + 9:m + 9 + n], rest[m + 9 + n:-4])
        i = pl.program_id(0)
        keep = pltpu.make_async_copy(w_buf, w_out_ref, w_flush.at[0])

        @pl.when(i == 0)
        def _():
            w_gather = _Gather([sh_ref], [w_buf], w_send, w_recv)
            w_gather.start()
            gather.start()
            w_gather.pass_on()
            w_gather.finish()
            keep.start()

        h0, _, _ = _ln(x_ref[...], g_ref[...], b_ref[...])
        proj = _dot_nt(h0.astype(_MXU), w_buf[...].reshape(IN_W, D_MODEL)) + bi_ref[...]
        pos = jnp.broadcast_to(pos_ref[...].astype(F32), (128, tm))
        ang = jnp.transpose(pos) * rc_ref[0:1, :]
        cs = jnp.cos(ang)
        sn = jnp.sin(ang)
        tc = cs * rc_ref[1:2, :] + rc_ref[2:3, :]
        t1 = -sn * rc_ref[3:4, :]
        t2 = sn * rc_ref[4:5, :]
        tc_ref[...] = tc
        t1_ref[...] = t1
        t2_ref[...] = t2
        q = _rope(proj[:, 0:ATTN_W], tc, t1, t2) * (HEAD_DIM ** -0.5)
        q_ref[...] = q.astype(_MXU)
        k_ref[...] = _rope(proj[:, ATTN_W:ATTN_W + KV_W], tc, t1, t2).astype(_MXU)
        v_ref[...] = proj[:, ATTN_W + KV_W:ATTN_W + 2 * KV_W].astype(_MXU)
        su_ref[...] = proj[:, ATTN_W + 2 * KV_W:ATTN_W + 2 * KV_W + SGU_W]
        sv_ref[...] = proj[:, ATTN_W + 2 * KV_W + SGU_W:IN_W]

        last = pl.num_programs(0) - 1

        @pl.when(i == jnp.maximum(last - 1, 0))
        def _():
            gather.pass_on()

        @pl.when(i == last)
        def _():
            gather.finish()
            keep.wait()

    sd = _hbm_shape
    w_shape = (N_CHIP,) + w_in_shard.shape
    return pl.pallas_call(
        body, name="ln_inproj", grid=(s_len // tm,),
        in_specs=[_rows(tm, D_MODEL), pl.BlockSpec((1, tm), lambda i: (0, i)), _const2((1, D_MODEL)),
                  _const2((1, D_MODEL)), _vmem(),
                  _const2((1, IN_W)), _const2((8, 128))] + plan.in_specs(),
        out_specs=[_rows(tm, ATTN_W), _rows(tm, KV_W), _rows(tm, KV_W), _rows(tm, SGU_W), _rows(tm, SGU_W),
                   _rows(tm, 128), _rows(tm, 128), _rows(tm, 128), _hbm()] + plan.out_specs(),
        out_shape=[sd((s_len, ATTN_W), _MXU), sd((s_len, KV_W), _MXU), sd((s_len, KV_W), _MXU),
                   sd((s_len, SGU_W), F32), sd((s_len, SGU_W), F32),
                   sd((s_len, 128), F32), sd((s_len, 128), F32), sd((s_len, 128), F32), sd(w_shape, _WIRE)]
        + plan.out_shapes(),
        scratch_shapes=plan.scratch() + [pltpu.VMEM(w_shape, _WIRE)] + _Gather.sems(1) + [pltpu.SemaphoreType.DMA((1,))],
        compiler_params=_params(56),
    )(x, pos_row, g0, b0, w_in_shard, b_in, _rope_consts(), *plan.operands())


def _band_mask_t(first_block):
    kj = lax.broadcasted_iota(jnp.int32, (2 * BLK, BLK), 0)
    qi = lax.broadcasted_iota(jnp.int32, (2 * BLK, BLK), 1)
    shut = jnp.where(first_block, 2 * BLK, 0)
    prev_ok = jnp.logical_and(kj < BLK, kj > qi + shut)
    cur_ok = jnp.logical_and(kj >= BLK, (kj - BLK) <= qi)
    return jnp.logical_or(prev_ok, cur_ok)


def _attn_probs_t(kh, qh, sink, allowed_t):
    s = jnp.where(allowed_t, _dot_nt(kh, qh), -1e30)
    m = jnp.maximum(jnp.max(s, axis=0, keepdims=True), sink)
    p = jnp.exp(s - m)
    ps = jnp.exp(sink - m)
    inv = 1.0 / (jnp.sum(p, axis=0, keepdims=True) + ps)
    return p * inv, ps * inv


def _sgu_mix(gv, lg, lb, w_ref, bt_ref):
    vv, vhat, rstd = _ln(gv, lg, lb)
    vvb = vv.astype(_MXU)
    wcs, mixed = [], []
    for h in range(N_GRP):
        wc = _causal_w(w_ref, h).astype(_MXU)
        wcs.append(wc)
        mixed.append(_dot(wc, vvb[:, h * GRP_DIM:(h + 1) * GRP_DIM]) + bt_ref[:, h:h + 1])
    return jnp.concatenate(mixed, axis=1), vhat, rstd, vvb, wcs


def _mixer_fwd(q, k, v, su, sv, sinks, sg, sb, sgu_w, sgu_bt, plan):
    s_len = q.shape[0]
    nb = s_len // BLK
    per = 2 if nb % 2 == 0 else 1
    steps = nb // per
    m, n = len(plan.operands()), plan.n

    def body(q_ref, kc_ref, kp_ref, vc_ref, vp_ref, su_ref, sv_ref, sink_ref, lg_ref, lb_ref, w_ref, bt_ref, *rest):
        mc_ref = rest[m]
        gather = plan.bind(rest[:m], rest[m + 1:m + 1 + n], rest[m + 1 + n:])
        i = pl.program_id(0)

        @pl.when(i == 0)
        def _():
            gather.start()

        @pl.when(i == max(steps - 2, 0))
        def _():
            gather.pass_on()

        @pl.when(i == steps - 1)
        def _():
            gather.finish()

        for s in range(per):
            rows = slice(s * BLK, (s + 1) * BLK)
            before = slice((s - 1) * BLK, s * BLK)
            k_prev = kp_ref[...] if s == 0 else kc_ref[before, :]
            v_prev = vp_ref[...] if s == 0 else vc_ref[before, :]
            allowed_t = _band_mask_t(i == 0 if s == 0 else False)
            kb = jnp.concatenate([k_prev, kc_ref[rows, :]], axis=0)
            vb = jnp.concatenate([v_prev, vc_ref[rows, :]], axis=0)
            qv = q_ref[rows, :]
            outs = []
            allowed_g = jnp.tile(allowed_t, (1, Q_PER_KV))
            for g in range(N_KV):
                heads = range(g * Q_PER_KV, (g + 1) * Q_PER_KV)
                kh = kb[:, g * HEAD_DIM:(g + 1) * HEAD_DIM]
                vh = vb[:, g * HEAD_DIM:(g + 1) * HEAD_DIM]
                q_g = jnp.concatenate([qv[:, h * HEAD_DIM:(h + 1) * HEAD_DIM] for h in heads], axis=0)
                sink_g = jnp.concatenate([jnp.full((1, BLK), sink_ref[h], F32) for h in heads], axis=1)
                probs_t, _ = _attn_probs_t(kh, q_g, sink_g, allowed_g)
                o_g = _dot_tn(probs_t.astype(_MXU), vh)
                outs += [o_g[hh * BLK:(hh + 1) * BLK, :] for hh in range(Q_PER_KV)]
            u = _gelu_and_grad(su_ref[rows, :])[0]
            gv = _gelu_and_grad(sv_ref[rows, :])[0]
            mixed = _sgu_mix(gv, lg_ref[...], lb_ref[...], w_ref, bt_ref)[0]
            mc_ref[rows, :] = jnp.concatenate(outs + [u * mixed], axis=1).astype(_MXU)

    cur = lambda w: pl.BlockSpec((per * BLK, w), lambda i: (i, 0))
    prev = lambda w: pl.BlockSpec((BLK, w), lambda i: (jnp.maximum(per * i - 1, 0), 0))
    return pl.pallas_call(
        body, name="mixer_fwd", grid=(steps,),
        in_specs=[cur(ATTN_W), cur(KV_W), prev(KV_W), cur(KV_W), prev(KV_W), cur(SGU_W), cur(SGU_W), _smem(),
                  _const2((1, SGU_W)), _const2((1, SGU_W)), _const2((N_GRP, BLK, BLK)), _const2((BLK, N_GRP))]
        + plan.in_specs(),
        out_specs=[cur(D_MODEL)] + plan.out_specs(),
        out_shape=[_hbm_shape((s_len, D_MODEL), _MXU)] + plan.out_shapes(),
        scratch_shapes=plan.scratch(),
        compiler_params=_params(56),
    )(q, k, k, v, v, su, sv, sinks, sg, sb, sgu_w, sgu_bt, *plan.operands())


def _outproj(mc, w_out, b_out, x, g0, b0, plan):
    s_len = x.shape[0]
    tm = _tile(s_len, 512)
    m, n = len(plan.operands()), plan.n

    def body(mc_ref, w_ref, bo_ref, x_ref, g_ref, b_ref, *rest):
        r1_ref = rest[m]
        gather = plan.bind(rest[:m], rest[m + 1:m + 1 + n], rest[m + 1 + n:])
        i = pl.program_id(0)

        @pl.when(i == 0)
        def _():
            gather.start()

        h0, _, _ = _ln(x_ref[...], g_ref[...], b_ref[...])
        r1_ref[...] = ALPHA * h0 + (_dot(mc_ref[...], w_ref[...]) + bo_ref[...])

        last = pl.num_programs(0) - 1

        @pl.when(i == jnp.maximum(last - 1, 0))
        def _():
            gather.pass_on()

        @pl.when(i == last)
        def _():
            gather.finish()

    return pl.pallas_call(
        body, name="outproj", grid=(s_len // tm,),
        in_specs=[_rows(tm, D_MODEL), _vmem(), _const2((1, D_MODEL)), _rows(tm, D_MODEL),
                  _const2((1, D_MODEL)), _const2((1, D_MODEL))] + plan.in_specs(),
        out_specs=[_rows(tm, D_MODEL)] + plan.out_specs(),
        out_shape=[_hbm_shape((s_len, D_MODEL), F32)] + plan.out_shapes(),
        scratch_shapes=plan.scratch(),
        compiler_params=_params(40),
    )(mc, w_out, b_out, x, g0, b0, *plan.operands())


def _ffn_spec(tm):
    return pl.BlockSpec((N_CHIP, tm, FF_SH), lambda i: (0, i, 0))


def _ffn_up(r1, g1, b1, wg, wu, plan):
    s_len = r1.shape[0]
    tm = _tile(s_len, 512)
    m, n = len(plan.operands()), plan.n

    def body(r1_ref, g_ref, b_ref, wg_ref, wu_ref, *rest):
        a_ref, p_ref, q_ref, h1_ref = rest[m:m + 4]
        gather = plan.bind(rest[:m], rest[m + 4:m + 4 + n], rest[m + 4 + n:])
        i = pl.program_id(0)

        @pl.when(i == 0)
        def _():
            gather.start()

        h1, _, _ = _ln(r1_ref[...], g_ref[...], b_ref[...])
        h1_ref[...] = h1
        h1b = h1.astype(_MXU)
        for j in range(N_CHIP):
            g = _dot_nt(h1b, wg_ref[j])
            u = _dot_nt(h1b, wu_ref[j])
            silu, sg = _silu_parts(g)
            a_ref[j] = (silu * u).astype(_MXU)
            p_ref[j] = silu.astype(_ACT)
            q_ref[j] = (u * (sg * (1.0 + g * (1.0 - sg)))).astype(_ACT)

        last = pl.num_programs(0) - 1

        @pl.when(i == jnp.maximum(last - 1, 0))
        def _():
            gather.pass_on()

        @pl.when(i == last)
        def _():
            gather.finish()

    sd = _hbm_shape((N_CHIP, s_len, FF_SH), _ACT)
    return pl.pallas_call(
        body, name="ffn_up", grid=(s_len // tm,),
        in_specs=[_rows(tm, D_MODEL), _const2((1, D_MODEL)), _const2((1, D_MODEL)), _vmem(), _vmem()] + plan.in_specs(),
        out_specs=[_ffn_spec(tm)] * 3 + [_rows(tm, D_MODEL)] + plan.out_specs(),
        out_shape=[_hbm_shape((N_CHIP, s_len, FF_SH), _MXU), sd, sd, _hbm_shape((s_len, D_MODEL), F32)]
        + plan.out_shapes(),
        scratch_shapes=plan.scratch(),
        compiler_params=_params(56),
    )(r1, g1, b1, wg, wu, *plan.operands())


def _silu_parts(g):
    sg = 1.0 / (1.0 + jnp.exp(-g))
    return g * sg, sg


def _ffn_down_loss(act, wd, h1, g2, b2, target):
    s_len = h1.shape[0]
    tm = _tile(s_len, 512)

    parts = 2 if tm % 32 == 0 else 1
    sub = tm // parts

    def body(a_ref, wd_ref, h1_ref, g2_ref, b2_ref, t_ref, dr2_ref, loss_ref, dg2_ref, db2_ref):
        i = pl.program_id(0)

        @pl.when(i == 0)
        def _():
            loss_ref[...] = jnp.zeros_like(loss_ref)
            dg2_ref[...] = jnp.zeros_like(dg2_ref)
            db2_ref[...] = jnp.zeros_like(db2_ref)

        for part in range(parts):
            rows = slice(part * sub, (part + 1) * sub)
            f = jnp.zeros((sub, D_MODEL), F32)
            for j in range(N_CHIP):
                f = f + _dot(a_ref[j, rows, :], wd_ref[j])
            h2, r2hat, rstd2 = _ln(ALPHA * h1_ref[rows, :] + f, g2_ref[...], b2_ref[...])
            diff = h2 - t_ref[rows, :]
            dh2 = diff * (1.0 / D_MODEL)
            loss_ref[...] += _colsum(diff * diff)
            dg2_ref[...] += _colsum(dh2 * r2hat)
            db2_ref[...] += _colsum(dh2)
            dr2_ref[rows, :] = _ln_bwd(dh2, r2hat, rstd2, g2_ref[...])

    vec = _hbm_shape((1, D_MODEL), F32)
    c = _const2((1, D_MODEL))
    return pl.pallas_call(
        body, name="ffn_down_loss", grid=(s_len // tm,),
        in_specs=[_ffn_spec(tm), _vmem(), _rows(tm, D_MODEL), c, c, _rows(tm, D_MODEL)],
        out_specs=[_rows(tm, D_MODEL), c, c, c],
        out_shape=[_hbm_shape((s_len, D_MODEL), F32), vec, vec, vec],
        compiler_params=_params(48),
    )(act, wd, h1, g2, b2, target)


def _ffn_bwd_a(dr2, act, p_act, q_act, wd):
    s_len = dr2.shape[0]
    tm = _tile(s_len, 512)

    def body(dr2_ref, a_ref, p_ref, q_ref, wd_ref, dg_ref, du_ref, wire_ref, own_ref,
             dwd_ref, land_ref, send_sem, recv_sem):
        i = pl.program_id(0)

        @pl.when(i == 0)
        def _():
            dwd_ref[...] = jnp.zeros_like(dwd_ref)

        dfb = dr2_ref[...].astype(_MXU)
        for j in range(N_CHIP):
            da = _dot_nt(dfb, wd_ref[j])
            dg_ref[j] = (da * q_ref[j].astype(F32)).astype(_MXU)
            du_ref[j] = (da * p_ref[j].astype(F32)).astype(_MXU)
            dwd_ref[j * FF_SH:(j + 1) * FF_SH, :] += _dot_tn(a_ref[j], dfb)

        @pl.when(i == pl.num_programs(0) - 1)
        def _():
            _pair_reduce(dwd_ref, wire_ref, own_ref, land_ref, send_sem, recv_sem)

    sd = _hbm_shape((N_CHIP, s_len, FF_SH), _MXU)
    half = (N_CHIP, FF_SH // 2, D_MODEL)
    return pl.pallas_call(
        body, name="ffn_bwd_a", grid=(s_len // tm,),
        in_specs=[_rows(tm, D_MODEL), _ffn_spec(tm), _ffn_spec(tm), _ffn_spec(tm), _vmem()],
        out_specs=[_ffn_spec(tm), _ffn_spec(tm), _vmem(), _vmem()],
        out_shape=[sd, sd] + _pair_out_shapes(half),
        scratch_shapes=_pair_scratch((D_FF, D_MODEL), half),
        compiler_params=_params(61),
    )(dr2, act, p_act, q_act, wd)


def _ffn_bwd_g(dr2, dg, r1, g1, b1, wg, prev_wire):
    s_len = dr2.shape[0]
    tm = _tile(s_len, 512)

    def body(dr2_ref, dg_ref, r1_ref, g1_ref, b1_ref, wg_ref, pw_ref, dh1_ref, wire_ref, own_ref, pl_ref,
             dwg_ref, land_ref, send_sem, recv_sem, xl_ref, x_send, x_recv, x_flush):
        i = pl.program_id(0)
        exchange = _ChipExchange(pw_ref, xl_ref, x_send, x_recv)

        @pl.when(i == 0)
        def _():
            exchange.start()
            dwg_ref[...] = jnp.zeros_like(dwg_ref)

        h1, _, _ = _ln(r1_ref[...], g1_ref[...], b1_ref[...])
        h1b = h1.astype(_MXU)
        dh1 = ALPHA * dr2_ref[...]
        for j in range(N_CHIP):
            dgj = dg_ref[j]
            dh1 = dh1 + _dot(dgj, wg_ref[j])
            dwg_ref[j * FF_SH:(j + 1) * FF_SH, :] += _dot_tn(dgj, h1b)
        dh1_ref[...] = dh1

        @pl.when(i == pl.num_programs(0) - 1)
        def _():
            exchange.flush_start(pl_ref, x_flush)
            _pair_reduce(dwg_ref, wire_ref, own_ref, land_ref, send_sem, recv_sem)
            exchange.flush_finish(pl_ref, x_flush)

    c = _const2((1, D_MODEL))
    half = (N_CHIP, FF_SH // 2, D_MODEL)
    return pl.pallas_call(
        body, name="ffn_bwd_g", grid=(s_len // tm,),
        in_specs=[_rows(tm, D_MODEL), _ffn_spec(tm), _rows(tm, D_MODEL), c, c, _vmem(), _vmem()],
        out_specs=[_rows(tm, D_MODEL), _vmem(), _vmem(), _hbm()],
        out_shape=[_hbm_shape((s_len, D_MODEL), F32)] + _pair_out_shapes(half) + [_ChipExchange.land_shape(prev_wire)],
        scratch_shapes=_pair_scratch((D_FF, D_MODEL), half) + _ChipExchange.scratch(prev_wire),
        compiler_params=_params(58),
    )(dr2, dg, r1, g1, b1, wg, prev_wire)


def _ffn_bwd_u(dh1a, du, r1, g1, b1, wu, prev_wire):
    s_len = dh1a.shape[0]
    tm = _tile(s_len, 512)

    def body(dh1_ref, du_ref, r1_ref, g1_ref, b1_ref, wu_ref, pw_ref,
             dr1_ref, wire_ref, own_ref, dg1_ref, db1_ref, pl_ref,
             dwu_ref, land_ref, send_sem, recv_sem, xl_ref, x_send, x_recv, x_flush):
        i = pl.program_id(0)
        exchange = _ChipExchange(pw_ref, xl_ref, x_send, x_recv)

        @pl.when(i == 0)
        def _():
            exchange.start()
            dwu_ref[...] = jnp.zeros_like(dwu_ref)
            dg1_ref[...] = jnp.zeros_like(dg1_ref)
            db1_ref[...] = jnp.zeros_like(db1_ref)

        h1, r1hat, rstd1 = _ln(r1_ref[...], g1_ref[...], b1_ref[...])
        h1b = h1.astype(_MXU)
        dh1 = dh1_ref[...]
        for j in range(N_CHIP):
            duj = du_ref[j]
            dh1 = dh1 + _dot(duj, wu_ref[j])
            dwu_ref[j * FF_SH:(j + 1) * FF_SH, :] += _dot_tn(duj, h1b)
        dg1_ref[...] += _colsum(dh1 * r1hat)
        db1_ref[...] += _colsum(dh1)
        dr1_ref[...] = _ln_bwd(dh1, r1hat, rstd1, g1_ref[...])

        @pl.when(i == pl.num_programs(0) - 1)
        def _():
            exchange.flush_start(pl_ref, x_flush)
            _pair_reduce(dwu_ref, wire_ref, own_ref, land_ref, send_sem, recv_sem)
            exchange.flush_finish(pl_ref, x_flush)

    vec = _hbm_shape((1, D_MODEL), F32)
    c = _const2((1, D_MODEL))
    half = (N_CHIP, FF_SH // 2, D_MODEL)
    return pl.pallas_call(
        body, name="ffn_bwd_u", grid=(s_len // tm,),
        in_specs=[_rows(tm, D_MODEL), _ffn_spec(tm), _rows(tm, D_MODEL), c, c, _vmem(), _vmem()],
        out_specs=[_rows(tm, D_MODEL), _vmem(), _vmem(), c, c, _hbm()],
        out_shape=[_hbm_shape((s_len, D_MODEL), F32)] + _pair_out_shapes(half)
        + [vec, vec, _ChipExchange.land_shape(prev_wire)],
        scratch_shapes=_pair_scratch((D_FF, D_MODEL), half) + _ChipExchange.scratch(prev_wire),
        compiler_params=_params(58),
    )(dh1a, du, r1, g1, b1, wu, prev_wire)


def _outproj_bwd(dr1, mc, w_out):
    s_len = dr1.shape[0]
    tm = _tile(s_len, 512)

    def body(dr1_ref, mc_ref, w_ref, dmc_ref, wire_ref, own_ref, db_ref, dw_ref, land_ref, send_sem, recv_sem):
        i = pl.program_id(0)

        @pl.when(i == 0)
        def _():
            dw_ref[...] = jnp.zeros_like(dw_ref)
            db_ref[...] = jnp.zeros_like(db_ref)

        d = dr1_ref[...]
        db_ref[...] += _colsum(d)
        db16 = d.astype(_MXU)
        dmc_ref[...] = _dot_nt(db16, w_ref[...])
        dw_ref[...] += _dot_tn(mc_ref[...], db16)

        @pl.when(i == pl.num_programs(0) - 1)
        def _():
            _pair_reduce(dw_ref, wire_ref, own_ref, land_ref, send_sem, recv_sem)

    half = (N_CHIP, OUT_SH // 2, D_MODEL)
    return pl.pallas_call(
        body, name="outproj_bwd", grid=(s_len // tm,),
        in_specs=[_rows(tm, D_MODEL), _rows(tm, D_MODEL), _vmem()],
        out_specs=[_rows(tm, D_MODEL), _vmem(), _vmem(), _const2((1, D_MODEL))],
        out_shape=[_hbm_shape((s_len, D_MODEL), F32)] + _pair_out_shapes(half) + [_hbm_shape((1, D_MODEL), F32)],
        scratch_shapes=_pair_scratch((D_MODEL, D_MODEL), half),
        compiler_params=_params(48),
    )(dr1, mc, w_out)


def _mixer_bwd(q, k, v, su, sv, dmc, tc, t1, t2, sinks, sg, sb, sgu_w, sgu_bt, prev_wires):
    s_len = q.shape[0]
    nb = s_len // BLK
    per = next(p for p in (4, 2, 1) if nb % p == 0)
    steps = nb // per

    def body(q_ref, kc_ref, kp_ref, vc_ref, vp_ref, su_ref, sv_ref, dmc_ref,
             tc_ref, t1_ref, t2_ref, tcp_ref, t1p_ref, t2p_ref,
             sink_ref, lg_ref, lb_ref, w_ref, bt_ref, pw0_ref, pw1_ref,
             dq_ref, dkv_ref, dsuv_ref, dbq_ref, dbkv_ref, dbsuv_ref,
             dsink_ref, dlg_ref, dlb_ref, dw_ref, dbt_ref, pl0_ref, pl1_ref, carry_ref,
             xl0_ref, x0_send, x0_recv, x0_flush, xl1_ref, x1_send, x1_recv, x1_flush):
        i = pl.program_id(0)
        exchanges = [(_ChipExchange(pw0_ref, xl0_ref, x0_send, x0_recv), pl0_ref, x0_flush),
                     (_ChipExchange(pw1_ref, xl1_ref, x1_send, x1_recv), pl1_ref, x1_flush)]

        @pl.when(i == 0)
        def _():
            for exchange, _, _ in exchanges:
                exchange.start()

        @pl.when(i == 0)
        def _():
            for r in (dbq_ref, dbkv_ref, dbsuv_ref, dsink_ref, dlg_ref, dlb_ref, dw_ref, dbt_ref, carry_ref):
                r[...] = jnp.zeros_like(r)

        def emit_kv(fin, t):
            if t == 0:
                tables = (tcp_ref[...], t1p_ref[...], t2p_ref[...])
            else:
                before = slice((t - 1) * BLK, t * BLK)
                tables = (tc_ref[before, :], t1_ref[before, :], t2_ref[before, :])
            dk = _rope_bwd(fin[:, 0:KV_W], *tables)
            out = jnp.concatenate([dk, fin[:, KV_W:2 * KV_W]], axis=1)
            dkv_ref[t * BLK:(t + 1) * BLK, :] = out.astype(_MXU)
            dbkv_ref[...] += _colsum(out)

        def one_block(s):
            rows = slice(s * BLK, (s + 1) * BLK)
            before = slice((s - 1) * BLK, s * BLK)
            k_prev = kp_ref[...] if s == 0 else kc_ref[before, :]
            v_prev = vp_ref[...] if s == 0 else vc_ref[before, :]
            allowed_t = _band_mask_t(i == 0 if s == 0 else False)
            kb = jnp.concatenate([k_prev, kc_ref[rows, :]], axis=0)
            vb = jnp.concatenate([v_prev, vc_ref[rows, :]], axis=0)
            qv = q_ref[rows, :]
            dmc = dmc_ref[rows, :]
            dqs, dks, dvs, dsinks = [], [], [], []
            allowed_g = jnp.tile(allowed_t, (1, Q_PER_KV))
            for g in range(N_KV):
                heads = range(g * Q_PER_KV, (g + 1) * Q_PER_KV)
                kh = kb[:, g * HEAD_DIM:(g + 1) * HEAD_DIM]
                vh = vb[:, g * HEAD_DIM:(g + 1) * HEAD_DIM]
                q_g = jnp.concatenate([qv[:, h * HEAD_DIM:(h + 1) * HEAD_DIM] for h in heads], axis=0)
                do_g = jnp.concatenate([dmc[:, h * HEAD_DIM:(h + 1) * HEAD_DIM] for h in heads], axis=0).astype(_MXU)
                sink_g = jnp.concatenate([jnp.full((1, BLK), sink_ref[h], F32) for h in heads], axis=1)
                probs_t, psink = _attn_probs_t(kh, q_g, sink_g, allowed_g)
                dvs.append(_dot(probs_t.astype(_MXU), do_g))
                dp_t = _dot_nt(vh, do_g)
                rd = jnp.sum(probs_t * dp_t, axis=0, keepdims=True)
                ds_t = (probs_t * (dp_t - rd)).astype(_MXU)
                ps_rd = psink * rd
                for hh in range(Q_PER_KV):
                    dsinks.append(-jnp.sum(ps_rd[:, hh * BLK:(hh + 1) * BLK], axis=1, keepdims=True))
                dq_g = _dot_tn(ds_t, kh)
                dqs += [dq_g[hh * BLK:(hh + 1) * BLK, :] for hh in range(Q_PER_KV)]
                dks.append(_dot(ds_t, q_g))
            dq = _rope_bwd(jnp.concatenate(dqs, axis=1) * (HEAD_DIM ** -0.5),
                           tc_ref[rows, :], t1_ref[rows, :], t2_ref[rows, :])
            dq_ref[rows, :] = dq.astype(_MXU)
            dbq_ref[...] += _colsum(dq)
            dsink_ref[...] += _lane_put(dsinks, 128)
            contrib = jnp.concatenate(dks + dvs, axis=1)

            lg = lg_ref[...]
            u, du_dsu = _gelu_and_grad(su_ref[rows, :])
            gv, dgv_dsv = _gelu_and_grad(sv_ref[rows, :])
            mixed, vhat, rstd, vvb, wcs = _sgu_mix(gv, lg, lb_ref[...], w_ref, bt_ref)
            dsgu = dmc[:, ATTN_W:D_MODEL]
            dsu = dsgu * mixed * du_dsu
            dmixed = dsgu * u
            tri_t = lax.broadcasted_iota(jnp.int32, (BLK, BLK), 0)
            tri_s = lax.broadcasted_iota(jnp.int32, (BLK, BLK), 1)
            dvv, dbs = [], []
            for h in range(N_GRP):
                dm = dmixed[:, h * GRP_DIM:(h + 1) * GRP_DIM]
                dmb = dm.astype(_MXU)
                dbs.append(jnp.sum(dm, axis=1, keepdims=True))
                dw_ref[h] += jnp.where(tri_s <= tri_t, _dot_nt(dmb, vvb[:, h * GRP_DIM:(h + 1) * GRP_DIM]), 0.0)
                dvv.append(_dot_tn(wcs[h], dmb))
            dvv = jnp.concatenate(dvv, axis=1)
            dbt_ref[...] += _lane_put(dbs, 128)
            dlg_ref[...] += _colsum(dvv * vhat)
            dlb_ref[...] += _colsum(dvv)
            dsv = _ln_bwd(dvv, vhat, rstd, lg) * dgv_dsv
            dsuv = jnp.concatenate([dsu, dsv], axis=1)
            dsuv_ref[rows, :] = dsuv.astype(_MXU)
            dbsuv_ref[...] += _colsum(dsuv)
            return contrib

        @pl.when(i < steps)
        def _():
            contribs = [one_block(s) for s in range(per)]
            for t in range(per):
                top = carry_ref[...] if t == 0 else contribs[t - 1][BLK:2 * BLK, :]
                emit_kv(top + contribs[t][0:BLK, :], t)
            carry_ref[...] = contribs[per - 1][BLK:2 * BLK, :]

        @pl.when(i == steps)
        def _():
            for exchange, landed, flush_sem in exchanges:
                exchange.flush_start(landed, flush_sem)
            emit_kv(carry_ref[...], 0)
            if per > 1:
                dkv_ref[BLK:per * BLK, :] = jnp.zeros(((per - 1) * BLK, 2 * KV_W), _MXU)
            for exchange, landed, flush_sem in exchanges:
                exchange.flush_finish(landed, flush_sem)

    last = steps - 1
    cur = lambda w: pl.BlockSpec((per * BLK, w), lambda i: (jnp.minimum(i, last), 0))
    prev = lambda w: pl.BlockSpec((BLK, w), lambda i: (jnp.clip(per * i - 1, 0, nb - 1), 0))
    shifted = pl.BlockSpec((per * BLK, 2 * KV_W), lambda i: (i, 0))
    sd = _hbm_shape
    return pl.pallas_call(
        body, name="mixer_bwd", grid=(steps + 1,),
        in_specs=[cur(ATTN_W), cur(KV_W), prev(KV_W), cur(KV_W), prev(KV_W), cur(SGU_W), cur(SGU_W), cur(D_MODEL),
                  cur(128), cur(128), cur(128), prev(128), prev(128), prev(128),
                  _smem(), _const2((1, SGU_W)), _const2((1, SGU_W)), _const2((N_GRP, BLK, BLK)), _const2((BLK, N_GRP)),
                  _vmem(), _vmem()],
        out_specs=[cur(ATTN_W), shifted, cur(2 * SGU_W),
                   _const2((1, ATTN_W)), _const2((1, 2 * KV_W)), _const2((1, 2 * SGU_W)),
                   _const2((1, 128)), _const2((1, SGU_W)), _const2((1, SGU_W)),
                   _const2((N_GRP, BLK, BLK)), _const2((BLK, 128)), _hbm(), _hbm()],
        out_shape=[sd((s_len, ATTN_W), _MXU), sd((s_len + per * BLK, 2 * KV_W), _MXU), sd((s_len, 2 * SGU_W), _MXU),
                   sd((1, ATTN_W), F32), sd((1, 2 * KV_W), F32), sd((1, 2 * SGU_W), F32),
                   sd((1, 128), F32), sd((1, SGU_W), F32), sd((1, SGU_W), F32),
                   sd((N_GRP, BLK, BLK), F32), sd((BLK, 128), F32)]
        + [_ChipExchange.land_shape(w) for w in prev_wires],
        scratch_shapes=[pltpu.VMEM((BLK, 2 * KV_W), F32)] + _ChipExchange.scratch(prev_wires[0])
        + _ChipExchange.scratch(prev_wires[1]),
        compiler_params=_params(40),
    )(q, k, k, v, v, su, sv, dmc, tc, t1, t2, tc, t1, t2, sinks, sg, sb, sgu_w, sgu_bt, *prev_wires)


def _inproj_bwd(dq, dkv_late, dsuv, dr1, x, g0, b0, w_in):
    s_len = x.shape[0]
    tm = _tile(s_len, 512)
    assert tm % BLK == 0
    per = tm // BLK
    cuts = ((0, ATTN_W), (ATTN_W, ATTN_W + 2 * KV_W), (ATTN_W + 2 * KV_W, IN_W))

    def body(dq_ref, *rest):
        dkv_refs = rest[:per]
        dsuv_ref, dr1_ref, x_ref, g_ref, b_ref, w_ref, dx_ref, dw_ref, dg_ref, db_ref = rest[per:]
        i = pl.program_id(0)

        @pl.when(i == 0)
        def _():
            dw_ref[...] = jnp.zeros_like(dw_ref)
            dg_ref[...] = jnp.zeros_like(dg_ref)
            db_ref[...] = jnp.zeros_like(db_ref)

        h0, xhat, rstd = _ln(x_ref[...], g_ref[...], b_ref[...])
        h0b = h0.astype(_MXU)
        dh0 = ALPHA * dr1_ref[...]
        dkv = jnp.concatenate([r[...] for r in dkv_refs], axis=0)
        for (lo, hi), d in zip(cuts, (dq_ref[...], dkv, dsuv_ref[...])):
            dh0 = dh0 + _dot(d, w_ref[lo:hi, :])
            dw_ref[lo:hi, :] += _dot_tn(d, h0b)
        dg_ref[...] += _colsum(dh0 * xhat)
        db_ref[...] += _colsum(dh0)
        dx_ref[...] = _ln_bwd(dh0, xhat, rstd, g_ref[...])

    vec = _hbm_shape((1, D_MODEL), F32)
    c = _const2((1, D_MODEL))
    return pl.pallas_call(
        body, name="inproj_bwd", grid=(s_len // tm,),
        in_specs=[_rows(tm, ATTN_W)]
        + [pl.BlockSpec((BLK, 2 * KV_W), lambda i, b=b: (i * per + b + 1, 0)) for b in range(per)]
        + [_rows(tm, 2 * SGU_W), _rows(tm, D_MODEL), _rows(tm, D_MODEL), c, c, _vmem()],
        out_specs=[_rows(tm, D_MODEL), _vmem(), c, c],
        out_shape=[_hbm_shape((s_len, D_MODEL), F32), jax.ShapeDtypeStruct((IN_W, D_MODEL), F32), vec, vec],
        compiler_params=_params(48),
    )(dq, *[dkv_late] * per, dsuv, dr1, x, g0, b0, w_in)


def _place():
    x, y, c = (lax.axis_index(a) for a in MESH_AXES)
    chips = [(1 - x, y), (x, 1 - y), (1 - x, 1 - y)]
    return x, y, c, chips


class _Gather:
    def __init__(self, ins, outs, send_sems, recv_sems, spans=None):
        self.ins, self.outs, self.send_sems, self.recv_sems = ins, outs, send_sems, recv_sems
        self.n = len(ins)
        self.spans = spans or [(0, r.shape[0]) for r in ins]
        self.halves = [(hi - lo) // 2 for lo, hi in self.spans]

    def _copy(self, k, t, slot, half, to):
        rows = pl.ds(pl.multiple_of(self.spans[t][0] + half * self.halves[t], 16), self.halves[t])
        piece = self.outs[t].at[slot, rows, :]
        return pltpu.make_async_remote_copy(src_ref=piece, dst_ref=piece, send_sem=self.send_sems.at[k],
                                            recv_sem=self.recv_sems.at[k], device_id=to, device_id_type=MESH)

    def _chip_copy(self, t, d, slot):
        x, y, c, chips = _place()
        return self._copy(3 * t + d, t, slot, c, (chips[d][0], chips[d][1], c))

    def _pass_copy(self, t, d, half):
        x, y, c, chips = _place()
        return self._copy(3 * self.n + 3 * t + d, t, 2 * chips[d][0] + chips[d][1], half, (x, y, 1 - c))

    def start(self):
        x, y, c, chips = _place()
        me = 2 * x + y
        for t in range(self.n):
            lo, hi = self.spans[t]
            self.outs[t][me, lo:hi, :] = self.ins[t][lo:hi, :].astype(_WIRE)
        for t in range(self.n):
            for d in range(3):
                self._chip_copy(t, d, me).start()

    def pass_on(self):
        x, y, c, chips = _place()
        for t in range(self.n):
            for d in range(3):
                self._chip_copy(t, d, 2 * chips[d][0] + chips[d][1]).wait_recv()
                self._pass_copy(t, d, c).start()

    def finish(self):
        x, y, c, chips = _place()
        me = 2 * x + y
        for t in range(self.n):
            for d in range(3):
                self._pass_copy(t, d, 1 - c).wait_recv()
        for t in range(self.n):
            for d in range(3):
                self._chip_copy(t, d, me).wait_send()
                self._pass_copy(t, d, c).wait_send()

    @staticmethod
    def out_shapes(shards, make=jax.ShapeDtypeStruct):
        return [make((N_CHIP,) + s.shape, _WIRE) for s in shards]

    @staticmethod
    def sems(n):
        return [pltpu.SemaphoreType.DMA((6 * n,)), pltpu.SemaphoreType.DMA((6 * n,))]


_FLUSHES_EARLY, _FLUSHES = 5, 8


class _GatherPlan:
    def __init__(self, pieces):
        self.shards = [p[0] for p in pieces]
        self.spans = [p[1] for p in pieces]
        self.earlier = [p[2] for p in pieces]
        self.n = len(pieces)
        self.carried = [t for t in range(self.n) if self.earlier[t] is not None]

    def operands(self):
        return self.shards + [self.earlier[t] for t in self.carried]

    def in_specs(self):
        return [_vmem()] * self.n + [_hbm()] * len(self.carried)

    def out_specs(self):
        return [_hbm()] * self.n

    def out_shapes(self):
        return _Gather.out_shapes(self.shards, _hbm_shape)

    def scratch(self):
        return ([pltpu.VMEM((N_CHIP,) + s.shape, _WIRE) for s in self.shards] + _Gather.sems(self.n)
                + [pltpu.SemaphoreType.DMA((_FLUSHES * self.n,)), pltpu.SemaphoreType.DMA((max(len(self.carried), 1),))])

    def bind(self, in_refs, out_refs, scratch_refs):
        plan = self
        shard_refs, earlier_refs = in_refs[:self.n], in_refs[self.n:]
        bufs = scratch_refs[:self.n]
        send_sems, recv_sems, flush_sems, carry_sems = scratch_refs[self.n:self.n + 4]
        gather = _Gather(shard_refs, bufs, send_sems, recv_sems, self.spans)

        def carry_copy(k):
            t = plan.carried[k]
            lo = plan.spans[t][0]
            return pltpu.make_async_copy(earlier_refs[k].at[:, 0:lo, :], bufs[t].at[:, 0:lo, :], carry_sems.at[k])

        def flushes(t, late):
            x, y, c, chips = _place()
            lo, hi = plan.spans[t]
            half = (hi - lo) // 2
            others = [2 * chips[d][0] + chips[d][1] for d in range(3)]

            def half_rows(h):
                return pl.ds(pl.multiple_of(lo + h * half, 16), half)

            if late:
                parts = [(slot, half_rows(1 - c)) for slot in others]
            else:
                parts = [(2 * x + y, pl.ds(lo, hi - lo))] + [(slot, half_rows(c)) for slot in others]
                if lo:
                    parts.append((slice(None), pl.ds(0, lo)))
            first = _FLUSHES_EARLY if late else 0
            return [pltpu.make_async_copy(bufs[t].at[slot, rows, :], out_refs[t].at[slot, rows, :],
                                          flush_sems.at[_FLUSHES * t + first + k]) for k, (slot, rows) in enumerate(parts)]

        class Bound:
            @staticmethod
            def start():
                for k in range(len(plan.carried)):
                    carry_copy(k).start()
                gather.start()

            @staticmethod
            def pass_on():
                gather.pass_on()
                for k in range(len(plan.carried)):
                    carry_copy(k).wait()
                for t in range(plan.n):
                    for cp in flushes(t, late=False):
                        cp.start()

            @staticmethod
            def finish():
                gather.finish()
                for t in range(plan.n):
                    for cp in flushes(t, late=True):
                        cp.start()
                for t in range(plan.n):
                    for cp in flushes(t, late=False) + flushes(t, late=True):
                        cp.wait()

        return Bound


class _ChipExchange:
    def __init__(self, wire_ref, land_ref, send_sems, recv_sems):
        self.wire, self.land, self.send_sems, self.recv_sems = wire_ref, land_ref, send_sems, recv_sems

    def _copy(self, d):
        x, y, c, chips = _place()
        return pltpu.make_async_remote_copy(
            src_ref=self.wire.at[2 * chips[d][0] + chips[d][1]], dst_ref=self.land.at[d],
            send_sem=self.send_sems.at[d], recv_sem=self.recv_sems.at[d],
            device_id=(chips[d][0], chips[d][1], c), device_id_type=MESH)

    def start(self):
        for d in range(3):
            self._copy(d).start()

    def wait_recv(self):
        for d in range(3):
            self._copy(d).wait_recv()

    def wait_send(self):
        for d in range(3):
            self._copy(d).wait_send()

    def _flush_copy(self, hbm_out, flush_sem):
        return pltpu.make_async_copy(self.land, hbm_out, flush_sem.at[0])

    def flush_start(self, hbm_out, flush_sem):
        self.wait_recv()
        self._flush_copy(hbm_out, flush_sem).start()

    def flush_finish(self, hbm_out, flush_sem):
        self._flush_copy(hbm_out, flush_sem).wait()
        self.wait_send()

    @staticmethod
    def land_shape(wire):
        return _hbm_shape((3,) + wire.shape[1:], wire.dtype)

    @staticmethod
    def sems():
        return [pltpu.SemaphoreType.DMA((3,)), pltpu.SemaphoreType.DMA((3,))]

    @staticmethod
    def scratch(wire):
        return ([pltpu.VMEM((3,) + wire.shape[1:], wire.dtype)] + _ChipExchange.sems() + [pltpu.SemaphoreType.DMA((1,))])


def _pair_out_shapes(half_shape):
    return [jax.ShapeDtypeStruct(half_shape, _WIRE), jax.ShapeDtypeStruct(half_shape[1:], F32)]


def _pair_scratch(acc_shape, half_shape):
    return [pltpu.VMEM(acc_shape, F32), pltpu.VMEM(half_shape, _WIRE),
            pltpu.SemaphoreType.DMA((N_CHIP,)), pltpu.SemaphoreType.DMA((N_CHIP,))]


def _pair_reduce(acc_ref, wire_ref, own_ref, land_ref, send_sems, recv_sems):
    rh = land_ref.shape[1]
    x, y, c, _ = _place()
    me = 2 * x + y
    copies = []
    for j in range(N_CHIP):
        def cast(r, carry, j=j):
            dst = pl.ds(pl.multiple_of(r * ROW_CHUNK, ROW_CHUNK), ROW_CHUNK)
            src = pl.ds(pl.multiple_of((2 * j + 1 - c) * rh + r * ROW_CHUNK, 8), ROW_CHUNK)
            wire_ref[j, dst, :] = acc_ref[src, :].astype(_WIRE)
            return carry

        lax.fori_loop(0, rh // ROW_CHUNK, cast, 0)
        cp = pltpu.make_async_remote_copy(src_ref=wire_ref.at[j], dst_ref=land_ref.at[j], send_sem=send_sems.at[j],
                                          recv_sem=recv_sems.at[j], device_id=(x, y, 1 - c), device_id_type=MESH)
        cp.start()
        copies.append(cp)
    for j in range(N_CHIP):
        copies[j].wait()

        def chunk(r, carry, j=j):
            theirs = pl.ds(pl.multiple_of(r * ROW_CHUNK, ROW_CHUNK), ROW_CHUNK)
            mine = pl.ds(pl.multiple_of((2 * j + c) * rh + r * ROW_CHUNK, 8), ROW_CHUNK)
            wire_ref[j, theirs, :] = (acc_ref[mine, :] + land_ref[j, theirs, :].astype(F32)).astype(_WIRE)
            return carry

        lax.fori_loop(0, rh // ROW_CHUNK, chunk, 0)

    def own_chunk(r, carry):
        theirs = pl.ds(pl.multiple_of(r * ROW_CHUNK, ROW_CHUNK), ROW_CHUNK)
        mine = pl.ds(pl.multiple_of((2 * me + c) * rh + r * ROW_CHUNK, 8), ROW_CHUNK)
        own_ref[theirs, :] = acc_ref[mine, :] + land_ref[me, theirs, :].astype(F32)
        return carry

    lax.fori_loop(0, rh // ROW_CHUNK, own_chunk, 0)


def _grad_finish(last_acc, lands, owns, small):
    n = len(owns) + 1
    halves = [last_acc.shape[0] // (2 * N_CHIP)] + [w.shape[1] for w in lands]
    widths = [last_acc.shape[1]] + [a.shape[1] for a in owns]
    small_body, small_scratch = _small_allreduce_parts()
    ns = len(small)

    def body(*refs):
        acc0, land, own = refs[0], (None,) + refs[1:n], (None,) + refs[n:2 * n - 1]
        refs = refs[2 * n - 1:]
        small_in, g_out, small_out = refs[:ns], refs[ns:ns + n], refs[ns + n:ns + n + 2]
        refs = refs[ns + n + 2:]
        pland0, wire0, land0, own0 = refs[0:4]
        p_send, p_recv, x_send, x_recv, pair_send, pair_recv = refs[4:10]
        g, flush_sems = refs[10:10 + n], refs[10 + n]
        small_refs = refs[11 + n:]
        land = (land0,) + land[1:]
        own = (own0,) + own[1:]
        x, y, c, chips = _place()
        me = 2 * x + y
        exchange = _ChipExchange(wire0, land0, x_send, x_recv)

        def half_rows(t, half):
            return pl.ds(pl.multiple_of(half * halves[t], 8), halves[t])

        def pair_copy(t, half):
            rows = g[t].at[half_rows(t, half), :]
            return pltpu.make_async_remote_copy(src_ref=rows, dst_ref=rows, send_sem=pair_send.at[t],
                                                recv_sem=pair_recv.at[t], device_id=(x, y, 1 - c), device_id_type=MESH)

        def flush(t):
            return pltpu.make_async_copy(g[t], g_out[t], flush_sems.at[t])

        small_rounds = small_body(*small_in, *small_out, *small_refs)
        next(small_rounds)
        _pair_reduce(acc0, wire0, own0, pland0, p_send, p_recv)
        next(small_rounds)
        exchange.start()

        for t in list(range(1, n)) + [0]:
            if t == 0:
                for done in range(1, n):
                    pair_copy(done, 1 - c).wait_recv()
                    flush(done).start()
                exchange.wait_recv()
            if t == min(2, n - 1):
                next(small_rounds)
            if t == min(4, n - 1):
                next(small_rounds, None)

            def chunk(r, carry, t=t):
                src = pl.ds(pl.multiple_of(r * ROW_CHUNK, ROW_CHUNK), ROW_CHUNK)
                dst = pl.ds(pl.multiple_of(c * halves[t] + r * ROW_CHUNK, 8), ROW_CHUNK)
                s = own[t][src, :]
                for d in range(3):
                    s = s + land[t][d, src, :].astype(F32)
                g[t][dst, :] = s
                return carry

            lax.fori_loop(0, halves[t] // ROW_CHUNK, chunk, 0)
            pair_copy(t, c).start()
        pair_copy(0, 1 - c).wait_recv()
        flush(0).start()
        for t in range(n):
            pair_copy(t, c).wait_send()
        exchange.wait_send()
        for t in range(n):
            flush(t).wait()

    half0 = (halves[0], widths[0])
    shapes = [(2 * h, w) for h, w in zip(halves, widths)]
    return pl.pallas_call(
        body, name="grad_finish",
        in_specs=[_vmem()] * (2 * n - 1 + ns), out_specs=[_hbm()] * n + [_vmem()] * 2,
        out_shape=[_hbm_shape(s, F32) for s in shapes] + [jax.ShapeDtypeStruct(s, F32) for s in _SMALL_OUT_DIMS],
        scratch_shapes=[pltpu.VMEM((N_CHIP,) + half0, _WIRE), pltpu.VMEM((N_CHIP,) + half0, _WIRE),
                        pltpu.VMEM((3,) + half0, _WIRE), pltpu.VMEM(half0, F32)]
        + [pltpu.SemaphoreType.DMA((N_CHIP,)), pltpu.SemaphoreType.DMA((N_CHIP,))]
        + _ChipExchange.sems()
        + [pltpu.SemaphoreType.DMA((n,)), pltpu.SemaphoreType.DMA((n,))]
        + [pltpu.VMEM(s, F32) for s in shapes] + [pltpu.SemaphoreType.DMA((n,))]
        + small_scratch,
        compiler_params=pltpu.CompilerParams(vmem_limit_bytes=56 * MIB),
    )(last_acc, *lands, *owns, *small)


_SMALL = ("ln_in_g", "ln_in_b", "b_in", "attn_sinks", "sgu_ln_g", "sgu_ln_b", "sgu_w", "sgu_b", "b_out",
          "ln_mix_g", "ln_mix_b", "ln_ffn_g", "ln_ffn_b")
_VEC_ROW = dict(ln_in_g=0, ln_in_b=1, b_in=2, attn_sinks=4, sgu_ln_g=5, sgu_ln_b=6, b_out=7, ln_mix_g=8, ln_mix_b=9,
                ln_ffn_g=10, ln_ffn_b=11)
_LOSS_ROW = 12
_VEC_ROWS = 16
_MAT_ROWS = N_GRP * BLK + BLK


_SMALL_IN = ("ln_in_g", "ln_in_b", "bq", "bkv", "bsuv", "sink", "sgu_ln_g", "sgu_ln_b", "sgu_w", "sgu_bt", "b_out",
             "ln_mix_g", "ln_mix_b", "ln_ffn_g", "ln_ffn_b", "loss")
_SMALL_OUT_DIMS = ((_VEC_ROWS, D_MODEL), (_MAT_ROWS, 128))


def _small_allreduce_parts():
    n_in = len(_SMALL_IN)

    def body(*refs):
        (g_ln_in_g, g_ln_in_b, g_bq, g_bkv, g_bsuv, g_sink, g_sln_g, g_sln_b, g_sw, g_sbt, g_bout,
         g_lmg, g_lmb, g_lfg, g_lfb, g_loss) = refs[:n_in]
        out_a, out_b = refs[n_in:n_in + 2]
        (buf_a, buf_b, pair_a, pair_b, stage_a, stage_b, tot_a, tot_b,
         p1_send, p1_recv, x_send, x_recv, p2_send, p2_recv) = refs[n_in + 2:]
        x, y, c, chips = _place()
        me = 2 * x + y
        sibling = (x, y, 1 - c)
        half_a, half_b = _VEC_ROWS // 2, _MAT_ROWS // 2

        buf_a[...] = jnp.zeros_like(buf_a)
        for row, ref in ((0, g_ln_in_g), (1, g_ln_in_b), (7, g_bout), (8, g_lmg), (9, g_lmb), (10, g_lfg), (11, g_lfb),
                         (_LOSS_ROW, g_loss)):
            buf_a[row:row + 1, :] = ref[...]
        buf_a[2:3, 0:ATTN_W] = g_bq[...]
        buf_a[2:3, ATTN_W:ATTN_W + 2 * KV_W] = g_bkv[...]
        buf_a[2:3, ATTN_W + 2 * KV_W:D_MODEL] = g_bsuv[:, 0:2 * KV_W]
        buf_a[3:4, 0:2 * SGU_W - 2 * KV_W] = g_bsuv[:, 2 * KV_W:2 * SGU_W]
        buf_a[4:5, 0:128] = g_sink[...]
        buf_a[5:6, 0:SGU_W] = g_sln_g[...]
        buf_a[6:7, 0:SGU_W] = g_sln_b[...]
        for h in range(N_GRP):
            buf_b[h * BLK:(h + 1) * BLK, :] = g_sw[h]
        buf_b[N_GRP * BLK:_MAT_ROWS, :] = g_sbt[...]

        def remote(src, dst, send_sem, recv_sem, to):
            return pltpu.make_async_remote_copy(src_ref=src, dst_ref=dst, send_sem=send_sem, recv_sem=recv_sem,
                                                device_id=to, device_id_type=MESH)

        first = [remote(buf_a, pair_a, p1_send.at[0], p1_recv.at[0], sibling),
                 remote(buf_b, pair_b, p1_send.at[1], p1_recv.at[1], sibling)]
        for cp in first:
            cp.start()
        yield
        for cp in first:
            cp.wait()
        rows_a = pl.ds(pl.multiple_of(c * half_a, 8), half_a)
        rows_b = pl.ds(pl.multiple_of(c * half_b, 8), half_b)
        stage_a[me] = buf_a[rows_a, :] + pair_a[rows_a, :]
        stage_b[me] = buf_b[rows_b, :] + pair_b[rows_b, :]

        def chip_copies(d):
            to = (chips[d][0], chips[d][1], c)
            return [remote(stage_a.at[me], stage_a.at[me], x_send.at[2 * d], x_recv.at[2 * d], to),
                    remote(stage_b.at[me], stage_b.at[me], x_send.at[2 * d + 1], x_recv.at[2 * d + 1], to)]

        def chip_arrivals(d):
            slot = 2 * chips[d][0] + chips[d][1]
            to = (chips[d][0], chips[d][1], c)
            return [remote(stage_a.at[slot], stage_a.at[slot], x_send.at[2 * d], x_recv.at[2 * d], to),
                    remote(stage_b.at[slot], stage_b.at[slot], x_send.at[2 * d + 1], x_recv.at[2 * d + 1], to)]

        for d in range(3):
            for cp in chip_copies(d):
                cp.start()
        yield
        for d in range(3):
            for cp in chip_arrivals(d):
                cp.wait_recv()
        tot_a[rows_a, :] = ((stage_a[0] + stage_a[1]) + stage_a[2]) + stage_a[3]
        tot_b[rows_b, :] = ((stage_b[0] + stage_b[1]) + stage_b[2]) + stage_b[3]

        second = [remote(tot_a.at[rows_a, :], tot_a.at[rows_a, :], p2_send.at[0], p2_recv.at[0], sibling),
                  remote(tot_b.at[rows_b, :], tot_b.at[rows_b, :], p2_send.at[1], p2_recv.at[1], sibling)]
        for cp in second:
            cp.start()
        yield
        other_a = pl.ds(pl.multiple_of((1 - c) * half_a, 8), half_a)
        other_b = pl.ds(pl.multiple_of((1 - c) * half_b, 8), half_b)
        remote(tot_a.at[other_a, :], tot_a.at[other_a, :], p2_send.at[0], p2_recv.at[0], sibling).wait_recv()
        remote(tot_b.at[other_b, :], tot_b.at[other_b, :], p2_send.at[1], p2_recv.at[1], sibling).wait_recv()
        for cp in second:
            cp.wait_send()
        for d in range(3):
            for cp in chip_copies(d):
                cp.wait_send()
        out_a[...] = tot_a[...]
        out_b[...] = tot_b[...]

    vec = pltpu.VMEM((_VEC_ROWS, D_MODEL), F32)
    mat = pltpu.VMEM((_MAT_ROWS, 128), F32)
    scratch = [vec, mat, vec, mat, pltpu.VMEM((N_CHIP, _VEC_ROWS // 2, D_MODEL), F32),
               pltpu.VMEM((N_CHIP, _MAT_ROWS // 2, 128), F32), vec, mat,
               pltpu.SemaphoreType.DMA((2,)), pltpu.SemaphoreType.DMA((2,)), pltpu.SemaphoreType.DMA((6,)),
               pltpu.SemaphoreType.DMA((6,)), pltpu.SemaphoreType.DMA((2,)), pltpu.SemaphoreType.DMA((2,))]
    return body, scratch


def _small_adamw(tot_a, tot_b, params):
    shapes = [params[nm][0].shape for nm in _SMALL]

    def body(*refs):
        ta, tb = refs[:2]
        prm = refs[2:2 + 3 * len(_SMALL)]
        outs = refs[2 + 3 * len(_SMALL):]

        def grad_of(k, name):
            if name == "sgu_w":
                return [tb[h * BLK:(h + 1) * BLK, :] for h in range(N_GRP)]
            if name == "sgu_b":
                return jnp.transpose(tb[N_GRP * BLK:_MAT_ROWS, :])[0:N_GRP, :]
            row = _VEC_ROW[name]
            if name == "b_in":
                return jnp.concatenate([ta[row:row + 1, :], ta[row + 1:row + 2, 0:IN_W - D_MODEL]], axis=1)
            return ta[row:row + 1, 0:shapes[k][-1]]

        for k, name in enumerate(_SMALL):
            w_ref, m_ref, v_ref = prm[3 * k:3 * k + 3]
            g_out, d_out, m_out, v_out = outs[4 * k:4 * k + 4]
            g = grad_of(k, name)
            if name == "sgu_w":
                for h in range(N_GRP):
                    d_, m_, v_ = _adamw_math(w_ref[h], g[h], m_ref[h], v_ref[h])
                    g_out[h], d_out[h], m_out[h], v_out[h] = g[h], d_, m_, v_
            else:
                d_, m_, v_ = _adamw_math(w_ref[...], g, m_ref[...], v_ref[...])
                g_out[...], d_out[...], m_out[...], v_out[...] = g, d_, m_, v_
        outs[-1][...] = jnp.sum(ta[_LOSS_ROW:_LOSS_ROW + 1, :], axis=1, keepdims=True) * (0.5 / D_MODEL)

    ins = [tot_a, tot_b] + [_in_hbm(a) for nm in _SMALL for a in params[nm]]
    out_dims = [s for s in shapes for _ in range(4)] + [(1, 1)]

    def unpack(res):
        return {nm: tuple(res[4 * k:4 * k + 4]) for k, nm in enumerate(_SMALL)}, res[-1]

    return body, ins, out_dims, unpack


def _adamw_math(w, g, m, v):
    m = ADAM_B1 * m + (1.0 - ADAM_B1) * g
    v = ADAM_B2 * v + (1.0 - ADAM_B2) * (g * g)
    m_hat = m / (1.0 - ADAM_B1 ** ADAM_STEP)
    v_hat = v / (1.0 - ADAM_B2 ** ADAM_STEP)
    delta = -ADAM_LR * (m_hat / (jnp.sqrt(v_hat) + ADAM_EPS) + ADAM_WD * w)
    return delta, m, v


ADAMW_STEPS = 4


def _adamw(name, groups, small):
    k = len(groups)
    small_body, small_ins, small_out_dims, _ = small
    n_small = len(small_ins)

    def body(*refs):
        big_in, small_in = refs[:4 * k], refs[4 * k:4 * k + n_small]
        big_out, small_out = refs[4 * k + n_small:8 * k + n_small], refs[8 * k + n_small:]

        @pl.when(pl.program_id(0) == 0)
        def _():
            small_body(*small_in, *small_out)

        for i in range(k):
            w_ref, g_ref, m_ref, v_ref = big_in[4 * i:4 * i + 4]
            g = g_ref[...]
            for o_ref, o in zip(big_out[4 * i:4 * i + 4], (g,) + _adamw_math(w_ref[...], g, m_ref[...], v_ref[...])):
                o_ref[...] = o

    specs = []
    for grp in groups:
        rows, cols = grp[0].shape
        assert rows % (8 * ADAMW_STEPS) == 0, rows
        specs += [pl.BlockSpec((rows // ADAMW_STEPS, cols), lambda i: (i, 0))] * 4
    res = pl.pallas_call(
        body, name=name, grid=(ADAMW_STEPS,),
        in_specs=specs + [_const2(a.shape) for a in small_ins],
        out_specs=specs + [_const2(s) for s in small_out_dims],
        out_shape=[_hbm_shape(grp[0].shape, F32) for grp in groups for _ in range(4)]
        + [_hbm_shape(s, F32) for s in small_out_dims],
        compiler_params=_params(56),
    )(*[_in_hbm(a) for grp in groups for a in grp], *small_ins)
    return [res[4 * i:4 * i + 4] for i in range(k)], res[4 * k:]


def kernel(x, positions, ln_in_g, ln_in_b, w_in, b_in, attn_sinks, sgu_ln_g, sgu_ln_b, sgu_w, sgu_b, w_out, b_out, ln_mix_g, ln_mix_b, w_gate, w_up, w_down, ln_ffn_g, ln_ffn_b, loss_target, m_ln_in_g, m_ln_in_b, m_w_in, m_b_in, m_attn_sinks, m_sgu_ln_g, m_sgu_ln_b, m_sgu_w, m_sgu_b, m_w_out, m_b_out, m_ln_mix_g, m_ln_mix_b, m_w_gate, m_w_up, m_w_down, m_ln_ffn_g, m_ln_ffn_b, v_ln_in_g, v_ln_in_b, v_w_in, v_b_in, v_attn_sinks, v_sgu_ln_g, v_sgu_ln_b, v_sgu_w, v_sgu_b, v_w_out, v_b_out, v_ln_mix_g, v_ln_mix_b, v_w_gate, v_w_up, v_w_down, v_ln_ffn_g, v_ln_ffn_b):
    weights = dict(ln_in_g=ln_in_g, ln_in_b=ln_in_b, w_in=w_in, b_in=b_in, attn_sinks=attn_sinks, sgu_ln_g=sgu_ln_g,
                   sgu_ln_b=sgu_ln_b, sgu_w=sgu_w, sgu_b=sgu_b, w_out=w_out, b_out=b_out, ln_mix_g=ln_mix_g,
                   ln_mix_b=ln_mix_b, w_gate=w_gate, w_up=w_up, w_down=w_down, ln_ffn_g=ln_ffn_g, ln_ffn_b=ln_ffn_b)
    mom_m = dict(ln_in_g=m_ln_in_g, ln_in_b=m_ln_in_b, w_in=m_w_in, b_in=m_b_in, attn_sinks=m_attn_sinks,
                 sgu_ln_g=m_sgu_ln_g, sgu_ln_b=m_sgu_ln_b, sgu_w=m_sgu_w, sgu_b=m_sgu_b, w_out=m_w_out, b_out=m_b_out,
                 ln_mix_g=m_ln_mix_g, ln_mix_b=m_ln_mix_b, w_gate=m_w_gate, w_up=m_w_up, w_down=m_w_down,
                 ln_ffn_g=m_ln_ffn_g, ln_ffn_b=m_ln_ffn_b)
    mom_v = dict(ln_in_g=v_ln_in_g, ln_in_b=v_ln_in_b, w_in=v_w_in, b_in=v_b_in, attn_sinks=v_attn_sinks,
                 sgu_ln_g=v_sgu_ln_g, sgu_ln_b=v_sgu_ln_b, sgu_w=v_sgu_w, sgu_b=v_sgu_b, w_out=v_w_out, b_out=v_b_out,
                 ln_mix_g=v_ln_mix_g, ln_mix_b=v_ln_mix_b, w_gate=v_w_gate, w_up=v_w_up, w_down=v_w_down,
                 ln_ffn_g=v_ln_ffn_g, ln_ffn_b=v_ln_ffn_b)
    order = list(weights)
    big = ("w_in", "w_out", "w_gate", "w_up", "w_down")

    s_len = x.shape[1]
    xs = _in_hbm(x.reshape(s_len, D_MODEL))
    tgt = _in_hbm(loss_target.reshape(s_len, D_MODEL))
    pos_row = _in_hbm(positions.reshape(1, s_len))
    g0, b0 = _in_hbm(ln_in_g.reshape(1, D_MODEL)), _in_hbm(ln_in_b.reshape(1, D_MODEL))
    sinks = attn_sinks.reshape(N_Q)
    sgu_w3 = _in_hbm(sgu_w.reshape(N_GRP, BLK, BLK))
    sgu_bt = _in_hbm(sgu_b.reshape(N_GRP, BLK).T)
    b_in, b_out, sgu_ln_g, sgu_ln_b, ln_mix_g, ln_mix_b, ln_ffn_g, ln_ffn_b = (
        _in_hbm(a) for a in (b_in, b_out, sgu_ln_g, sgu_ln_b, ln_mix_g, ln_mix_b, ln_ffn_g, ln_ffn_b))

    col_sharded = ("w_in", "w_gate", "w_up")

    def rowmajor(name, a):
        return jnp.swapaxes(a[0], 0, 1) if name in col_sharded else a[0]

    def as_given(name, a):
        return (jnp.swapaxes(a, 0, 1) if name in col_sharded else a)[None]

    shards = [rowmajor(n, weights[n]) for n in big]

    sh_out, sh_gate, sh_up, sh_down = shards[1:]
    *acts, gw_in, gw_out, gw_gate0 = _ln_inproj(xs, pos_row, g0, b0, shards[0], b_in, _GatherPlan(
        [(sh_out, (0, OUT_SH), None), (sh_gate, (0, GATE_CUT), None)]))
    w_in_full = gw_in.reshape(IN_W, D_MODEL)
    q, k, v, su, sv, tc, t1, t2 = (_in_hbm(a) for a in acts)
    mc, gw_gate, gw_up0 = _mixer_fwd(q, k, v, su, sv, sinks, sgu_ln_g, sgu_ln_b, sgu_w3, sgu_bt, _GatherPlan(
        [(sh_gate, (GATE_CUT, FF_SH), gw_gate0), (sh_up, (0, UP_CUT), None)]))
    mc = _in_hbm(mc)
    w_out_full = gw_out.reshape(D_MODEL, D_MODEL)
    r1, gw_up = _outproj(mc, w_out_full, b_out, xs, g0, b0, _GatherPlan([(sh_up, (UP_CUT, FF_SH), gw_up0)]))
    r1 = _in_hbm(r1)
    act, p_act, q_act, h1, gw_down = _ffn_up(r1, ln_mix_g, ln_mix_b, gw_gate, gw_up,
                                             _GatherPlan([(sh_down, (0, FF_SH), None)]))
    act, p_act, q_act = _in_hbm(act), _in_hbm(p_act), _in_hbm(q_act)
    dr2, loss_cols, d_ln_ffn_g, d_ln_ffn_b = _ffn_down_loss(act, gw_down, _in_hbm(h1), ln_ffn_g, ln_ffn_b, tgt)
    dr2 = _in_hbm(dr2)

    dg, du, wire_down, own_down = _ffn_bwd_a(dr2, act, p_act, q_act, gw_down)
    dh1a, wire_gate, own_gate, land_down = _ffn_bwd_g(dr2, _in_hbm(dg), r1, ln_mix_g, ln_mix_b, gw_gate, wire_down)
    dr1, wire_up, own_up, d_ln_mix_g, d_ln_mix_b, land_gate = _ffn_bwd_u(_in_hbm(dh1a), _in_hbm(du), r1, ln_mix_g,
                                                                         ln_mix_b, gw_up, wire_gate)
    dr1 = _in_hbm(dr1)
    dmc, wire_out, own_out, d_b_out = _outproj_bwd(dr1, mc, w_out_full)
    (dq, dkv, dsuv, dbq, dbkv, dbsuv, d_sink, d_sgu_ln_g, d_sgu_ln_b, d_sgu_w, d_sgu_bt, land_up, land_out) = _mixer_bwd(
        q, k, v, su, sv, _in_hbm(dmc), tc, t1, t2, sinks, sgu_ln_g, sgu_ln_b, sgu_w3, sgu_bt, [wire_up, wire_out])
    grad_x, acc_in, d_ln_in_g, d_ln_in_b = _inproj_bwd(_in_hbm(dq), _in_hbm(dkv), _in_hbm(dsuv), dr1, xs, g0, b0,
                                                       w_in_full)

    small_local = dict(
        ln_in_g=d_ln_in_g, ln_in_b=d_ln_in_b, bq=dbq, bkv=dbkv, bsuv=dbsuv, sink=d_sink, sgu_ln_g=d_sgu_ln_g,
        sgu_ln_b=d_sgu_ln_b, sgu_w=d_sgu_w, sgu_bt=d_sgu_bt, b_out=d_b_out, ln_mix_g=d_ln_mix_g, ln_mix_b=d_ln_mix_b,
        ln_ffn_g=d_ln_ffn_g, ln_ffn_b=d_ln_ffn_b, loss=loss_cols)
    *reduced, tot_a, tot_b = _grad_finish(acc_in, [land_out, land_gate, land_up, land_down],
                                          [own_out, own_gate, own_up, own_down], [small_local[nm] for nm in _SMALL_IN])
    small_shape = dict(ln_in_g=(1, D_MODEL), ln_in_b=(1, D_MODEL), sgu_w=(N_GRP, BLK, BLK), sgu_b=(N_GRP, BLK))
    small_params = {nm: tuple(src[nm].reshape(small_shape.get(nm, src[nm].shape)) for src in (weights, mom_m, mom_v))
                    for nm in _SMALL}
    small = _small_adamw(_in_hbm(tot_a), _in_hbm(tot_b), small_params)

    groups = [(shards[t], reduced[t], rowmajor(nm, mom_m[nm]), rowmajor(nm, mom_v[nm])) for t, nm in enumerate(big)]
    big_res, small_res = _adamw("adamw", groups, small)
    small_out, loss = small[3](small_res)
    loss = loss.reshape(())
    grads, delta, new_m, new_v = {}, {}, {}, {}
    for nm in _SMALL:
        grads[nm], delta[nm], new_m[nm], new_v[nm] = (a.reshape(weights[nm].shape) for a in small_out[nm])
    for nm, res in zip(big, big_res):
        grads[nm], delta[nm], new_m[nm], new_v[nm] = (as_given(nm, a) for a in res)

    return (loss, grad_x.reshape(x.shape), *[grads[n] for n in order], *[delta[n] for n in order],
            *[new_m[n] for n in order], *[new_v[n] for n in order])
```
